```python
import jax, jax.numpy as jnp
from jax import lax
import numpy as np

D_MODEL = 1024
BATCH = 8
SEQ = 2048
DEPTH = 1

MIX_WIDTH = D_MODEL
RET_HEADS = 8
RET_WIDTH = MIX_WIDTH // 2
RET_V_DIM = RET_WIDTH // RET_HEADS
RET_QK_DIM = RET_V_DIM // 2
RET_CHUNK = 128
RET_ROT_BASE = 10000.0
ATT_HEADS = 8
ATT_WIDTH = MIX_WIDTH - RET_WIDTH
ATT_HEAD_DIM = ATT_WIDTH // ATT_HEADS
DILATED_PATTERNS = ((128, 1), (512, 4), (2048, 16))
ATT_BLOCK = 128
ROPE_THETA = 500000.0
ROPE_DIM = ATT_HEAD_DIM // 4
D_FF = -(-8 * D_MODEL // (3 * 256)) * 256
RMS_EPS = 1e-6
GN_EPS = 1e-5
PROJ_SPLITS = (RET_HEADS * RET_QK_DIM, RET_HEADS * RET_QK_DIM, RET_WIDTH, RET_WIDTH,
               ATT_WIDTH, ATT_WIDTH, ATT_WIDTH)
PROJ_WIDTH = sum(PROJ_SPLITS)

kernel_name = 'hybrid_retention_dilated_attn_block'


def rms_norm(x, g):
    xf = x.astype(jnp.float32)
    y = xf * lax.rsqrt(jnp.mean(xf * xf, axis=-1, keepdims=True) + RMS_EPS)
    return (y * g.astype(jnp.float32)).astype(x.dtype)


def apply_rotary(x, pos, inv_freq):
    rot = 2 * inv_freq.shape[0]
    ang = pos.astype(jnp.float32)[:, :, None] * inv_freq[None, None, :]
    cos = jnp.cos(ang)[:, :, None, :]
    sin = jnp.sin(ang)[:, :, None, :]
    xr = x[..., :rot].astype(jnp.float32)
    x1, x2 = xr[..., :rot // 2], xr[..., rot // 2:]
    out = jnp.concatenate([x1 * cos - x2 * sin, x1 * sin + x2 * cos], axis=-1).astype(x.dtype)
    return jnp.concatenate([out, x[..., rot:]], axis=-1)


def retention_branch(q, k, v, g, pos):
    B, S, H, dk = q.shape
    dv = v.shape[-1]
    inv_freq = 1.0 / (RET_ROT_BASE ** jnp.linspace(0.0, 1.0, dk // 2, dtype=jnp.float32))
    q = apply_rotary(q, pos, inv_freq)
    k = apply_rotary(k, pos, inv_freq) * (dk ** -0.5)
    C = RET_CHUNK
    N = S // C
    log_g = jnp.log1p(-jnp.exp2(-5.0 - jnp.arange(H, dtype=jnp.float32)))
    idx = jnp.arange(C, dtype=jnp.float32)
    diff = idx[:, None] - idx[None, :]
    inner_decay = jnp.where(diff >= 0, jnp.exp(log_g[:, None, None] * jnp.maximum(diff, 0.0)), 0.0)
    q_decay = jnp.exp(log_g[:, None] * (idx + 1.0)[None, :])
    k_decay = jnp.exp(log_g[:, None] * (C - 1.0 - idx)[None, :])
    chunk_decay = jnp.exp(log_g * C)
    qc = q.reshape(B, N, C, H, dk)
    kc = k.reshape(B, N, C, H, dk)
    vc = v.reshape(B, N, C, H, dv)
    scores = jnp.einsum('bnchd,bnmhd->bnhcm', qc, kc) * inner_decay
    inner = jnp.einsum('bnhcm,bnmhe->bnche', scores, vc)
    kv = jnp.einsum('bnchd,hc,bnche->nbhde', kc, k_decay, vc)

    def step(state, kv_i):
        return chunk_decay[None, :, None, None] * state + kv_i, state

    _, prev_states = lax.scan(step, jnp.zeros((B, H, dk, dv), dtype=kv.dtype), kv)
    cross = jnp.einsum('bnchd,nbhde,hc->bnche', qc, prev_states, q_decay)
    out = (inner + cross).reshape(B, S, H, dv).astype(jnp.float32)
    mu = jnp.mean(out, axis=-1, keepdims=True)
    var = jnp.mean(jnp.square(out - mu), axis=-1, keepdims=True)
    out = ((out - mu) * lax.rsqrt(var + GN_EPS)).astype(v.dtype)
    return (out * jax.nn.silu(g)).reshape(B, S, H * dv)


def dilated_pattern(q, k, v, window, dilation):
    B, S, H, D = q.shape
    Lc = S // dilation
    wc = window // dilation
    nb = -(-Lc // ATT_BLOCK)
    Lp = nb * ATT_BLOCK

    def to_classes(t):
        return t.reshape(B, Lc, dilation, H, D).transpose(0, 2, 3, 1, 4)

    qc = jnp.pad(to_classes(q), ((0, 0), (0, 0), (0, 0), (0, Lp - Lc), (0, 0)))
    qb = qc.reshape(B, dilation, H, nb, ATT_BLOCK, D)
    kv_pad = ((0, 0), (0, 0), (0, 0), (ATT_BLOCK, Lp - Lc), (0, 0))
    kb = jnp.pad(to_classes(k), kv_pad).reshape(B, dilation, H, nb + 1, ATT_BLOCK, D)
    vb = jnp.pad(to_classes(v), kv_pad).reshape(B, dilation, H, nb + 1, ATT_BLOCK, D)
    kcat = jnp.concatenate([kb[:, :, :, :-1], kb[:, :, :, 1:]], axis=-2)
    vcat = jnp.concatenate([vb[:, :, :, :-1], vb[:, :, :, 1:]], axis=-2)
    s = jnp.einsum('brhnqd,brhnkd->brhnqk', qb, kcat).astype(jnp.float32) * (D ** -0.5)
    a = jnp.arange(ATT_BLOCK)
    kk = jnp.arange(2 * ATT_BLOCK)
    blk = jnp.arange(nb)
    dist = ATT_BLOCK + a[:, None] - kk[None, :]
    key_idx = blk[:, None] * ATT_BLOCK - ATT_BLOCK + kk[None, :]
    mask = ((dist >= 0) & (dist <= wc))[None, :, :] & (key_idx >= 0)[:, None, :]
    s = jnp.where(mask, s, -jnp.inf)
    m = jnp.max(s, axis=-1, keepdims=True)
    p = jnp.exp(s - m)
    den = jnp.sum(p, axis=-1)
    o = jnp.einsum('brhnqk,brhnkd->brhnqd', p.astype(v.dtype), vcat).astype(jnp.float32) / den[..., None]
    lse = m[..., 0] + jnp.log(den)
    o = o.reshape(B, dilation, H, Lp, D)[:, :, :, :Lc].transpose(0, 3, 1, 2, 4).reshape(B, S, H, D)
    lse = lse.reshape(B, dilation, H, Lp)[:, :, :, :Lc].transpose(0, 3, 1, 2).reshape(B, S, H)
    return o, lse


def dilated_attention_branch(q, k, v, pos):
    B, S, H, D = q.shape
    inv_freq = ROPE_THETA ** (-jnp.arange(0, ROPE_DIM, 2, dtype=jnp.float32) / ROPE_DIM)
    q = apply_rotary(q, pos, inv_freq)
    k = apply_rotary(k, pos, inv_freq)
    outs, lses = [], []
    for window, dilation in DILATED_PATTERNS:
        o, lse = dilated_pattern(q, k, v, window, dilation)
        outs.append(o)
        lses.append(lse)
    w = jax.nn.softmax(jnp.stack(lses, axis=0), axis=0)
    out = jnp.sum(w[..., None] * jnp.stack(outs, axis=0), axis=0)
    return out.astype(v.dtype).reshape(B, S, H * D)


def _fwd_setup_inputs(seed: int = 0) -> dict:
    key = jax.random.key(seed)
    ks = jax.random.split(key, 12)

    def gain(k):
        return 1.0 + 0.05 * jax.random.normal(k, (DEPTH, D_MODEL), jnp.float32)

    x = jax.random.normal(ks[0], (BATCH, SEQ, D_MODEL), jnp.float32)
    offsets = jax.random.randint(ks[1], (BATCH, 1), 0, 4096, dtype=jnp.int32)
    positions = (jnp.arange(SEQ, dtype=jnp.int32)[None, :] + offsets).astype(jnp.int32)
    w_in = jax.random.normal(ks[2], (DEPTH, D_MODEL, PROJ_WIDTH), jnp.float32) * D_MODEL ** -0.5
    w_out = jax.random.normal(ks[3], (DEPTH, MIX_WIDTH, D_MODEL), jnp.float32) * MIX_WIDTH ** -0.5
    w_gate = jax.random.normal(ks[4], (DEPTH, D_MODEL, D_FF), jnp.float32) * D_MODEL ** -0.5
    w_up = jax.random.normal(ks[5], (DEPTH, D_MODEL, D_FF), jnp.float32) * D_MODEL ** -0.5
    w_down = jax.random.normal(ks[6], (DEPTH, D_FF, D_MODEL), jnp.float32) * D_FF ** -0.5
    return {'x': x, 'positions': positions, 'w_in': w_in, 'w_out': w_out,
            'g_pre_mix': gain(ks[7]), 'g_post_mix': gain(ks[8]),
            'g_pre_ffn': gain(ks[9]), 'g_post_ffn': gain(ks[10]),
            'w_gate': w_gate, 'w_up': w_up, 'w_down': w_down}


def _fwd_reference(x, positions, w_in, w_out, g_pre_mix, g_post_mix, g_pre_ffn, g_post_ffn, w_gate, w_up, w_down):
    B, S, _ = x.shape
    split_idx = list(np.cumsum(PROJ_SPLITS)[:-1])
    for l in range(DEPTH):
        h = rms_norm(x, g_pre_mix[l])
        proj = h @ w_in[l]
        rq, rk, rv, rg, aq, ak, av = jnp.split(proj, split_idx, axis=-1)
        ret = retention_branch(rq.reshape(B, S, RET_HEADS, RET_QK_DIM),
                               rk.reshape(B, S, RET_HEADS, RET_QK_DIM),
                               rv.reshape(B, S, RET_HEADS, RET_V_DIM),
                               rg.reshape(B, S, RET_HEADS, RET_V_DIM), positions)
        att = dilated_attention_branch(aq.reshape(B, S, ATT_HEADS, ATT_HEAD_DIM),
                                       ak.reshape(B, S, ATT_HEADS, ATT_HEAD_DIM),
                                       av.reshape(B, S, ATT_HEADS, ATT_HEAD_DIM), positions)
        mix = jnp.concatenate([ret, att], axis=-1) @ w_out[l]
        x = x + rms_norm(mix, g_post_mix[l])
        h = rms_norm(x, g_pre_ffn[l])
        f = (jax.nn.silu(h @ w_gate[l]) * (h @ w_up[l])) @ w_down[l]
        x = x + rms_norm(f, g_post_ffn[l])
    return x


import jax as _jax
import jax.numpy as _jnp

TWIN_FORMAT = 'train_step'
FWD_PARAMS = ['x', 'positions', 'w_in', 'w_out', 'g_pre_mix', 'g_post_mix', 'g_pre_ffn', 'g_post_ffn', 'w_gate', 'w_up', 'w_down']
TWIN_WEIGHTS = ['w_in', 'w_out', 'g_pre_mix', 'g_post_mix', 'g_pre_ffn', 'g_post_ffn', 'w_gate', 'w_up', 'w_down']
TWIN_DIFF_INPUT = 'x'
TWIN_INPUTS = ['x', 'positions', 'w_in', 'w_out', 'g_pre_mix', 'g_post_mix', 'g_pre_ffn', 'g_post_ffn', 'w_gate', 'w_up', 'w_down', 'loss_target', 'm_w_in', 'm_w_out', 'm_g_pre_mix', 'm_g_post_mix', 'm_g_pre_ffn', 'm_g_post_ffn', 'm_w_gate', 'm_w_up', 'm_w_down', 'v_w_in', 'v_w_out', 'v_g_pre_mix', 'v_g_post_mix', 'v_g_pre_ffn', 'v_g_post_ffn', 'v_w_gate', 'v_w_up', 'v_w_down']
TWIN_OUTPUTS = ['loss', 'grad_x', 'grad_w_in', 'grad_w_out', 'grad_g_pre_mix', 'grad_g_post_mix', 'grad_g_pre_ffn', 'grad_g_post_ffn', 'grad_w_gate', 'grad_w_up', 'grad_w_down', 'delta_w_in', 'delta_w_out', 'delta_g_pre_mix', 'delta_g_post_mix', 'delta_g_pre_ffn', 'delta_g_post_ffn', 'delta_w_gate', 'delta_w_up', 'delta_w_down', 'new_m_w_in', 'new_m_w_out', 'new_m_g_pre_mix', 'new_m_g_post_mix', 'new_m_g_pre_ffn', 'new_m_g_post_ffn', 'new_m_w_gate', 'new_m_w_up', 'new_m_w_down', 'new_v_w_in', 'new_v_w_out', 'new_v_g_pre_mix', 'new_v_g_post_mix', 'new_v_g_pre_ffn', 'new_v_g_post_ffn', 'new_v_w_gate', 'new_v_w_up', 'new_v_w_down']
TWIN_LEAF_KINDS = {'loss': 'loss', 'grad_x': 'grad_x', 'grad_w_in': 'grad_w', 'grad_w_out': 'grad_w', 'grad_g_pre_mix': 'grad_w', 'grad_g_post_mix': 'grad_w', 'grad_g_pre_ffn': 'grad_w', 'grad_g_post_ffn': 'grad_w', 'grad_w_gate': 'grad_w', 'grad_w_up': 'grad_w', 'grad_w_down': 'grad_w', 'delta_w_in': 'delta_w', 'delta_w_out': 'delta_w', 'delta_g_pre_mix': 'delta_w', 'delta_g_post_mix': 'delta_w', 'delta_g_pre_ffn': 'delta_w', 'delta_g_post_ffn': 'delta_w', 'delta_w_gate': 'delta_w', 'delta_w_up': 'delta_w', 'delta_w_down': 'delta_w', 'new_m_w_in': 'new_m', 'new_m_w_out': 'new_m', 'new_m_g_pre_mix': 'new_m', 'new_m_g_post_mix': 'new_m', 'new_m_g_pre_ffn': 'new_m', 'new_m_g_post_ffn': 'new_m', 'new_m_w_gate': 'new_m', 'new_m_w_up': 'new_m', 'new_m_w_down': 'new_m', 'new_v_w_in': 'new_v', 'new_v_w_out': 'new_v', 'new_v_g_pre_mix': 'new_v', 'new_v_g_post_mix': 'new_v', 'new_v_g_pre_ffn': 'new_v', 'new_v_g_post_ffn': 'new_v', 'new_v_w_gate': 'new_v', 'new_v_w_up': 'new_v', 'new_v_w_down': 'new_v'}


def _forward(args):
    return _fwd_reference(*[args[k] for k in FWD_PARAMS])


def _output_shape():
    out = _jax.eval_shape(lambda: _forward(_fwd_setup_inputs(0)))
    return out.shape, out.dtype

N_MICROBATCH = 1
ADAM_LR = 0.001
ADAM_B1 = 0.9
ADAM_B2 = 0.999
ADAM_EPS = 1e-08
ADAM_WD = 0.01
ADAM_STEP = 10
PER_EXAMPLE_BATCH_AXIS = {'x': 0, 'positions': 0, 'loss_target': 0}
SHARED_INPUTS = []
_WEIGHT_DTYPES = {'w_in': _jnp.float32, 'w_out': _jnp.float32, 'g_pre_mix': _jnp.float32, 'g_post_mix': _jnp.float32, 'g_pre_ffn': _jnp.float32, 'g_post_ffn': _jnp.float32, 'w_gate': _jnp.float32, 'w_up': _jnp.float32, 'w_down': _jnp.float32}
MOMENT_SCALE = {'w_in': 3.689447e-01, 'w_out': 3.579963e-01, 'g_pre_mix': 6.557875e-01, 'g_post_mix': 1.599557e+01, 'g_pre_ffn': 3.757032e-01, 'g_post_ffn': 1.606601e+01, 'w_gate': 1.374429e-01, 'w_up': 1.830202e-01, 'w_down': 3.048821e-01}


def _to_microbatches(a, axis):
    t = _jnp.moveaxis(a, axis, 0)
    t = t.reshape((N_MICROBATCH, t.shape[0] // N_MICROBATCH) + t.shape[1:])
    return _jnp.moveaxis(t, 1, axis + 1)


def setup_inputs(seed: int = 0) -> dict:
    inp = _fwd_setup_inputs(seed)
    key = _jax.random.fold_in(_jax.random.key(seed), 7919)
    shape, _ = _output_shape()
    out = dict(inp)
    out["loss_target"] = _jax.random.normal(_jax.random.fold_in(key, 0), shape, _jnp.float32)
    for i, name in enumerate(TWIN_WEIGHTS):
        w = inp[name].astype(_jnp.float32)
        if MOMENT_SCALE is None:
            s = _jnp.sqrt(_jnp.mean(_jnp.square(w)) + 1e-30)
        else:
            s = MOMENT_SCALE[name]
        km, kv = _jax.random.split(_jax.random.fold_in(key, i + 1))
        out[name] = w
        out["m_" + name] = s * _jax.random.normal(km, w.shape, _jnp.float32)
        out["v_" + name] = (s * s) * _jax.random.uniform(kv, w.shape, _jnp.float32, 0.5, 1.5)
    if N_MICROBATCH > 1:
        for name, axis in PER_EXAMPLE_BATCH_AXIS.items():
            out[name] = _to_microbatches(out[name], axis)
    return {'x': out['x'], 'positions': out['positions'], 'w_in': out['w_in'], 'w_out': out['w_out'], 'g_pre_mix': out['g_pre_mix'], 'g_post_mix': out['g_post_mix'], 'g_pre_ffn': out['g_pre_ffn'], 'g_post_ffn': out['g_post_ffn'], 'w_gate': out['w_gate'], 'w_up': out['w_up'], 'w_down': out['w_down'], 'loss_target': out['loss_target'], 'm_w_in': out['m_w_in'], 'm_w_out': out['m_w_out'], 'm_g_pre_mix': out['m_g_pre_mix'], 'm_g_post_mix': out['m_g_post_mix'], 'm_g_pre_ffn': out['m_g_pre_ffn'], 'm_g_post_ffn': out['m_g_post_ffn'], 'm_w_gate': out['m_w_gate'], 'm_w_up': out['m_w_up'], 'm_w_down': out['m_w_down'], 'v_w_in': out['v_w_in'], 'v_w_out': out['v_w_out'], 'v_g_pre_mix': out['v_g_pre_mix'], 'v_g_post_mix': out['v_g_post_mix'], 'v_g_pre_ffn': out['v_g_pre_ffn'], 'v_g_post_ffn': out['v_g_post_ffn'], 'v_w_gate': out['v_w_gate'], 'v_w_up': out['v_w_up'], 'v_w_down': out['v_w_down']}


def _loss(weights, diff, rest, loss_target):
    with _jax.named_scope("forward"):
        args = {**rest, TWIN_DIFF_INPUT: diff, **{k: w.astype(_WEIGHT_DTYPES[k]) for k, w in weights.items()}}
        y = _forward(args)
    with _jax.named_scope("loss_head"):
        err = _jnp.square(y.astype(_jnp.float32) - loss_target)
        return 0.5 * _jnp.sum(_jnp.mean(err, axis=-1)) if err.ndim else 0.5 * err


def _adamw(w, g, m, v):
    m = ADAM_B1 * m + (1.0 - ADAM_B1) * g
    v = ADAM_B2 * v + (1.0 - ADAM_B2) * _jnp.square(g)
    m_hat = m / (1.0 - ADAM_B1 ** ADAM_STEP)
    v_hat = v / (1.0 - ADAM_B2 ** ADAM_STEP)
    delta = -ADAM_LR * (m_hat / (_jnp.sqrt(v_hat) + ADAM_EPS) + ADAM_WD * w)
    return delta, m, v


def reference(x, positions, w_in, w_out, g_pre_mix, g_post_mix, g_pre_ffn, g_post_ffn, w_gate, w_up, w_down, loss_target, m_w_in, m_w_out, m_g_pre_mix, m_g_post_mix, m_g_pre_ffn, m_g_post_ffn, m_w_gate, m_w_up, m_w_down, v_w_in, v_w_out, v_g_pre_mix, v_g_post_mix, v_g_pre_ffn, v_g_post_ffn, v_w_gate, v_w_up, v_w_down):
    given = dict(x=x, positions=positions, w_in=w_in, w_out=w_out, g_pre_mix=g_pre_mix, g_post_mix=g_post_mix, g_pre_ffn=g_pre_ffn, g_post_ffn=g_post_ffn, w_gate=w_gate, w_up=w_up, w_down=w_down, loss_target=loss_target, m_w_in=m_w_in, m_w_out=m_w_out, m_g_pre_mix=m_g_pre_mix, m_g_post_mix=m_g_post_mix, m_g_pre_ffn=m_g_pre_ffn, m_g_post_ffn=m_g_post_ffn, m_w_gate=m_w_gate, m_w_up=m_w_up, m_w_down=m_w_down, v_w_in=v_w_in, v_w_out=v_w_out, v_g_pre_mix=v_g_pre_mix, v_g_post_mix=v_g_post_mix, v_g_pre_ffn=v_g_pre_ffn, v_g_post_ffn=v_g_post_ffn, v_w_gate=v_w_gate, v_w_up=v_w_up, v_w_down=v_w_down)
    weights = {n: given[n] for n in TWIN_WEIGHTS}
    shared = {n: given[n] for n in SHARED_INPUTS}
    per_example = {n: given[n] for n in ['x', 'positions']}
    grad_fn = _jax.value_and_grad(_loss, argnums=(0, 1))

    def one_microbatch(ex, loss_target):
        ex = dict(ex)
        diff = ex.pop(TWIN_DIFF_INPUT)
        return grad_fn(weights, diff, {**shared, **ex}, loss_target)

    if N_MICROBATCH == 1:
        loss, (grad_w, grad_x) = one_microbatch(per_example, given["loss_target"])
    else:
        def body(carry, xs):
            loss_sum, grad_sum = carry
            l_k, (gw_k, gx_k) = one_microbatch(xs[0], xs[1])
            with _jax.named_scope("update"):
                return (loss_sum + l_k, _jax.tree.map(_jnp.add, grad_sum, gw_k)), gx_k

        init = (_jnp.zeros((), _jnp.float32), _jax.tree.map(_jnp.zeros_like, weights))
        (loss, grad_w), grad_x = _jax.lax.scan(body, init, (per_example, given["loss_target"]))
    with _jax.named_scope("update"):
        delta_w, new_m, new_v = {}, {}, {}
        for n in TWIN_WEIGHTS:
            delta_w[n], new_m[n], new_v[n] = _adamw(weights[n], grad_w[n], given["m_" + n], given["v_" + n])
    return (loss, grad_x, *[grad_w[n] for n in TWIN_WEIGHTS], *[delta_w[n] for n in TWIN_WEIGHTS],
            *[new_m[n] for n in TWIN_WEIGHTS], *[new_v[n] for n in TWIN_WEIGHTS])
```

```python
import functools

import numpy as np
import jax
import jax.numpy as jnp
from jax import lax
from jax.experimental import pallas as pl
from jax.experimental.pallas import tpu as pltpu

F32, BF16 = jnp.float32, jnp.bfloat16
MESH = pl.DeviceIdType.MESH

S = 2048
D = 1024
PW = 3072
N_CHIP = 4
WIN_C = PW // N_CHIP
DFF = 2816
FF_C = DFF // N_CHIP
WOUT_R = D // N_CHIP
RMS_EPS = 1e-6
GN_EPS = 1e-5
RET_C = 128
RET_SCALE = 32 ** -0.5
ATT_BLK = 128
ATT_SCALE = 64 ** -0.5
PATTERN_DILATIONS = (1, 4, 16)
NEG = -1e30
VMEM_LIMIT = 56 * 1024 * 1024

ADAM_LR, ADAM_B1, ADAM_B2, ADAM_EPS, ADAM_WD, ADAM_STEP = 0.001, 0.9, 0.999, 1e-08, 0.01, 10


def _params(*sem):
    return pltpu.CompilerParams(dimension_semantics=sem, vmem_limit_bytes=VMEM_LIMIT)


def _nt(a, b):
    return lax.dot_general(a, b, (((1,), (1,)), ((), ())), preferred_element_type=F32)


def _tn(a, b):
    return lax.dot_general(a, b, (((0,), (0,)), ((), ())), preferred_element_type=F32)


def _nn(a, b):
    return jnp.dot(a, b, preferred_element_type=F32)


def _rstd(v):
    return lax.rsqrt(jnp.mean(v * v, axis=-1, keepdims=True) + RMS_EPS)


def _sigmoid(v):
    return 1.0 / (1.0 + jnp.exp(-v))


def _rows(i, t):
    return pl.ds(pl.multiple_of(i * t, t), t)


def _retention_tables():
    h = np.arange(8, dtype=np.float32)
    log_g = np.log1p(-np.exp2(-5.0 - h)).astype(np.float32)
    idx = np.arange(RET_C, dtype=np.float32)
    diff = idx[:, None] - idx[None, :]
    dtab = np.where(diff >= 0, np.exp(log_g[:, None, None] * np.maximum(diff, 0.0)), 0.0).astype(np.float32)
    lane_head = np.arange(256) // 32
    a_tab = np.exp(log_g[lane_head][None, :] * (idx + 1.0)[:, None]).astype(np.float32)
    b_tab = np.exp(log_g[lane_head][None, :] * (RET_C - 1.0 - idx)[:, None]).astype(np.float32)
    lam = np.exp(log_g[lane_head] * RET_C).astype(np.float32)[:, None]
    bd = (lane_head[:, None] == (np.arange(512) // 64)[None, :]).astype(np.float32)
    return dtab, a_tab, b_tab, lam, bd


def _rotary_tables():
    inv_r = (1.0 / (np.float32(10000.0) ** np.linspace(0.0, 1.0, 16, dtype=np.float32))).astype(np.float32)
    inv_a = (np.float32(500000.0) ** (-np.arange(0, 16, 2, dtype=np.float32) / np.float32(16))).astype(np.float32)
    dr = np.arange(256) % 32
    ifr = inv_r[dr % 16][None, :].astype(np.float32)
    da = np.arange(512) % 64
    ifa = np.where(da < 16, inv_a[da % 8], 0.0)[None, :].astype(np.float32)
    return ifr, ifa


def _proj_fwd(x, g1, win_g):
    tm = 256

    def body(x_ref, g_ref, w_ref, proj_ref, h_ref):
        xv = x_ref[...]
        h = (xv * _rstd(xv) * g_ref[...]).astype(BF16)
        h_ref[...] = h
        for k in range(N_CHIP):
            proj_ref[:, k * WIN_C:(k + 1) * WIN_C] = _nn(h, w_ref[k])

    return pl.pallas_call(
        body, grid=(S // tm,), name="proj_fwd",
        in_specs=[pl.BlockSpec((tm, D), lambda i: (i, 0)), pl.BlockSpec((1, D), lambda i: (0, 0)),
                  pl.BlockSpec((N_CHIP, D, WIN_C), lambda i: (0, 0, 0))],
        out_specs=[pl.BlockSpec((tm, PW), lambda i: (i, 0)), pl.BlockSpec((tm, D), lambda i: (i, 0))],
        out_shape=[jax.ShapeDtypeStruct((S, PW), F32), jax.ShapeDtypeStruct((S, D), BF16)],
        compiler_params=_params("parallel"),
    )(x, g1, win_g)


def _rot_coeffs(pos_ref, ifr_ref, ifa_ref, tm):
    pos = pos_ref[...].astype(F32)
    ang_r = pos * ifr_ref[...]
    ang_a = pos * ifa_ref[...]
    lo_r = (lax.broadcasted_iota(jnp.int32, (tm, 256), 1) % 32) < 16
    lo_a = (lax.broadcasted_iota(jnp.int32, (tm, 512), 1) % 64) < 8
    return jnp.cos(ang_r), jnp.sin(ang_r), lo_r, jnp.cos(ang_a), jnp.sin(ang_a), lo_a


def _rot_fwd(proj, pos, ifr, ifa):
    tm = 256

    def body(p_ref, pos_ref, ifr_ref, ifa_ref, qr_ref, kr_ref, rv_ref, aq_ref, ak_ref, av_ref):
        cr, sr, lo_r, ca, sa, lo_a = _rot_coeffs(pos_ref, ifr_ref, ifa_ref, tm)

        def rot_r(v):
            return v * cr + sr * jnp.where(lo_r, -pltpu.roll(v, 240, 1), pltpu.roll(v, 16, 1))

        def rot_a(v):
            return v * ca + sa * jnp.where(lo_a, -pltpu.roll(v, 504, 1), pltpu.roll(v, 8, 1))

        qr_ref[...] = rot_r(p_ref[:, 0:256]).astype(BF16)
        kr_ref[...] = (rot_r(p_ref[:, 256:512]) * RET_SCALE).astype(BF16)
        rv_ref[...] = p_ref[:, 512:1024].astype(BF16)
        aq_ref[...] = rot_a(p_ref[:, 1536:2048]).astype(BF16)
        ak_ref[...] = rot_a(p_ref[:, 2048:2560]).astype(BF16)
        av_ref[...] = p_ref[:, 2560:3072].astype(BF16)

    row = lambda w: pl.BlockSpec((tm, w), lambda i: (i, 0))
    const = lambda w: pl.BlockSpec((1, w), lambda i: (0, 0))
    return pl.pallas_call(
        body, grid=(S // tm,), name="rot_fwd",
        in_specs=[row(PW), row(1), const(256), const(512)],
        out_specs=[row(256), row(256), row(512), row(512), row(512), row(512)],
        out_shape=[jax.ShapeDtypeStruct((S, w), BF16) for w in (256, 256, 512, 512, 512, 512)],
        compiler_params=_params("parallel"),
    )(proj, pos, ifr, ifa)


def _seg_mean(v):
    lo = lax.broadcasted_iota(jnp.int32, v.shape, 1) < 64
    s_lo = jnp.sum(jnp.where(lo, v, 0.0), axis=-1, keepdims=True)
    s_hi = jnp.sum(jnp.where(lo, 0.0, v), axis=-1, keepdims=True)
    return jnp.where(lo, s_lo, s_hi) * (1.0 / 64.0)


def _ret_fwd(qr, kr, rv, proj, tabs):
    C = RET_C
    dtab, a_tab, b_tab, lam, bd = tabs

    def body(q_ref, k_ref, v_ref, g_ref, dt_ref, a_ref, b_ref, lam_ref, bd_ref, o_ref, cat_ref, st_ref, R):
        @pl.when(pl.program_id(0) == 0)
        def _():
            R[...] = jnp.zeros_like(R)

        q, k, v = q_ref[...], k_ref[...], v_ref[...]
        lane_head = lax.broadcasted_iota(jnp.int32, (C, 256), 1) // 32
        col_head = lax.broadcasted_iota(jnp.int32, (C, 256), 1) // 64
        rb = R[...].astype(BF16)
        st_ref[...] = rb
        qa = (q.astype(F32) * a_ref[...]).astype(BF16)
        cross = _nn(qa, rb)
        og = [cross[:, 0:256], cross[:, 256:512]]
        for h in range(8):
            g = h // 4
            qm = jnp.where(lane_head == h, q, jnp.zeros_like(q))
            p = (_nt(qm, k) * dt_ref[h]).astype(BF16)
            pv = _nn(p, v[:, 256 * g:256 * g + 256])
            og[g] = og[g] + jnp.where(col_head == (h % 4), pv, 0.0)
        kb = (k.astype(F32) * b_ref[...]).astype(BF16)
        R[...] = R[...] * lam_ref[...] + _tn(kb, v) * bd_ref[...]
        o_ref[:, 0:256] = og[0]
        o_ref[:, 256:512] = og[1]
        for j in range(4):
            oj = og[j // 2][:, 128 * (j % 2):128 * (j % 2) + 128]
            xc = oj - _seg_mean(oj)
            rn = xc * lax.rsqrt(_seg_mean(xc * xc) + GN_EPS)
            gj = g_ref[:, 128 * j:128 * j + 128]
            cat_ref[:, 128 * j:128 * j + 128] = (rn * (gj * _sigmoid(gj))).astype(BF16)

    row = lambda w: pl.BlockSpec((C, w), lambda n: (n, 0))
    full = lambda a: pl.BlockSpec(a.shape, lambda n: (0,) * a.ndim)
    return pl.pallas_call(
        body, grid=(S // C,), name="ret_fwd",
        in_specs=[row(256), row(256), row(512), pl.BlockSpec((C, 512), lambda n: (n, 2)),
                  full(dtab), full(a_tab), full(b_tab), full(lam), full(bd)],
        out_specs=[row(512), row(512), pl.BlockSpec((None, 256, 512), lambda n: (n, 0, 0))],
        out_shape=[jax.ShapeDtypeStruct((S, 512), F32), jax.ShapeDtypeStruct((S, 512), BF16),
                   jax.ShapeDtypeStruct((S // C, 256, 512), BF16)],
        scratch_shapes=[pltpu.VMEM((256, 512), F32)],
        compiler_params=_params("arbitrary"),
    )(qr, kr, rv, proj, dtab, a_tab, b_tab, lam, bd)


def _att_mask(ib, has_prev):
    nk = 2 * ATT_BLK if has_prev else ATT_BLK
    a = lax.broadcasted_iota(jnp.int32, (ATT_BLK, nk), 0)
    kk = lax.broadcasted_iota(jnp.int32, (ATT_BLK, nk), 1)
    if has_prev:
        dist = ATT_BLK + a - kk
        return (dist >= 0) & (dist <= ATT_BLK) & ((ib * ATT_BLK - ATT_BLK + kk) >= 0)
    return (a - kk) >= 0


def _att_specs(d):
    cur = pl.BlockSpec((ATT_BLK, 512), lambda r, ib: (ib, r))
    prev = pl.BlockSpec((ATT_BLK, 512), lambda r, ib: (jnp.maximum(ib - 1, 0), r))
    return cur, prev


def _att_fwd(aq, ak, av, d):
    lc = S // d
    nb = lc // ATT_BLK
    has_prev = nb > 1
    nk = 2 * ATT_BLK if has_prev else ATT_BLK
    view = lambda t: t.reshape(lc, d * 512)

    def body(*refs):
        if has_prev:
            q_ref, kp_ref, kc_ref, vp_ref, vc_ref, o_ref, l_ref, kcat, vcat = refs
            kcat[0:ATT_BLK, :] = kp_ref[...]
            kcat[ATT_BLK:nk, :] = kc_ref[...]
            vcat[0:ATT_BLK, :] = vp_ref[...]
            vcat[ATT_BLK:nk, :] = vc_ref[...]
        else:
            q_ref, kcat, vcat, o_ref, l_ref = refs
        valid = _att_mask(pl.program_id(1), has_prev)
        lane_head = lax.broadcasted_iota(jnp.int32, (ATT_BLK, 256), 1) // 64
        for g in range(2):
            qg = q_ref[:, 256 * g:256 * g + 256]
            kg = kcat[:, 256 * g:256 * g + 256]
            vg = vcat[:, 256 * g:256 * g + 256]
            og = jnp.zeros((ATT_BLK, 256), F32)
            lg = jnp.zeros((ATT_BLK, 256), F32)
            for hh in range(4):
                qm = jnp.where(lane_head == hh, qg, jnp.zeros_like(qg))
                s = jnp.where(valid, _nt(qm, kg) * ATT_SCALE, NEG)
                m = jnp.max(s, axis=-1, keepdims=True)
                p = jnp.exp(s - m)
                den = jnp.sum(p, axis=-1, keepdims=True)
                o = _nn(p.astype(BF16), vg) / den
                og = jnp.where(lane_head == hh, o, og)
                lg = jnp.where(lane_head == hh, m + jnp.log(den), lg)
            o_ref[:, 256 * g:256 * g + 256] = og
            l_ref[:, 256 * g:256 * g + 256] = lg

    cur, prev = _att_specs(d)
    if has_prev:
        in_specs, args = [cur, prev, cur, prev, cur], (view(aq), view(ak), view(ak), view(av), view(av))
        scratch = [pltpu.VMEM((nk, 512), BF16), pltpu.VMEM((nk, 512), BF16)]
    else:
        in_specs, args, scratch = [cur, cur, cur], (view(aq), view(ak), view(av)), []
    o, l = pl.pallas_call(
        body, grid=(d, nb), name=f"att_fwd_d{d}",
        in_specs=in_specs, out_specs=[cur, cur],
        out_shape=[jax.ShapeDtypeStruct((lc, d * 512), F32)] * 2,
        scratch_shapes=scratch,
        compiler_params=_params("parallel", "parallel"),
    )(*args)
    return o.reshape(S, 512), l.reshape(S, 512)


def _att_merge(os_, ls_):
    tm = 256

    def body(o1, o2, o3, l1, l2, l3, out_ref, lse_ref, cat_ref):
        a, b, c = l1[...], l2[...], l3[...]
        mx = jnp.maximum(jnp.maximum(a, b), c)
        ea, eb, ec = jnp.exp(a - mx), jnp.exp(b - mx), jnp.exp(c - mx)
        den = ea + eb + ec
        out = (ea * o1[...] + eb * o2[...] + ec * o3[...]) / den
        out_ref[...] = out
        lse_ref[...] = mx + jnp.log(den)
        cat_ref[...] = out.astype(BF16)

    row = pl.BlockSpec((tm, 512), lambda i: (i, 0))
    return pl.pallas_call(
        body, grid=(S // tm,), name="att_merge",
        in_specs=[row] * 6, out_specs=[row] * 3,
        out_shape=[jax.ShapeDtypeStruct((S, 512), F32), jax.ShapeDtypeStruct((S, 512), F32),
                   jax.ShapeDtypeStruct((S, 512), BF16)],
        compiler_params=_params("parallel"),
    )(*os_, *ls_)


def _mix_fwd(cat_r, cat_a, wout, x, g2, g3):
    tm = 256

    def body(cr_ref, ca_ref, w_ref, x_ref, g2_ref, g3_ref, mix_ref, x2_ref, h3_ref):
        mix = _nn(cr_ref[...], w_ref[0:512, :]) + _nn(ca_ref[...], w_ref[512:1024, :])
        mix_ref[...] = mix
        x2 = x_ref[...] + mix * _rstd(mix) * g2_ref[...]
        x2_ref[...] = x2
        h3_ref[...] = (x2 * _rstd(x2) * g3_ref[...]).astype(BF16)

    row = lambda w: pl.BlockSpec((tm, w), lambda i: (i, 0))
    vec = pl.BlockSpec((1, D), lambda i: (0, 0))
    return pl.pallas_call(
        body, grid=(S // tm,), name="mix_fwd",
        in_specs=[row(512), row(512), pl.BlockSpec((D, D), lambda i: (0, 0)), row(D), vec, vec],
        out_specs=[row(D), row(D), row(D)],
        out_shape=[jax.ShapeDtypeStruct((S, D), F32), jax.ShapeDtypeStruct((S, D), F32),
                   jax.ShapeDtypeStruct((S, D), BF16)],
        compiler_params=_params("parallel"),
    )(cat_r, cat_a, wout, x, g2, g3)


def _ffn_fwd(h3, wg, wu, wd):
    tm = 256

    def body(h_ref, wg_ref, wu_ref, wd_ref, gt_ref, up_ref, a_ref, f_ref):
        k, i = pl.program_id(0), pl.program_id(1)
        h = h_ref[...]
        gt = _nn(h, wg_ref[...])
        up = _nn(h, wu_ref[...])
        gt_ref[...] = gt
        up_ref[...] = up
        a = (gt * _sigmoid(gt) * up).astype(BF16)
        a_ref[...] = a
        part = _nn(a, wd_ref[...])
        rows = _rows(i, tm)

        @pl.when(k == 0)
        def _():
            f_ref[rows, :] = part

        @pl.when(k > 0)
        def _():
            f_ref[rows, :] = f_ref[rows, :] + part

    wcol = pl.BlockSpec((None, D, FF_C), lambda k, i: (k, 0, 0))
    act = pl.BlockSpec((None, tm, FF_C), lambda k, i: (k, i, 0))
    return pl.pallas_call(
        body, grid=(N_CHIP, S // tm), name="ffn_fwd",
        in_specs=[pl.BlockSpec((tm, D), lambda k, i: (i, 0)), wcol, wcol,
                  pl.BlockSpec((None, FF_C, D), lambda k, i: (k, 0, 0))],
        out_specs=[act, act, act, pl.BlockSpec((S, D), lambda k, i: (0, 0))],
        out_shape=[jax.ShapeDtypeStruct((N_CHIP, S, FF_C), F32), jax.ShapeDtypeStruct((N_CHIP, S, FF_C), F32),
                   jax.ShapeDtypeStruct((N_CHIP, S, FF_C), BF16), jax.ShapeDtypeStruct((S, D), F32)],
        compiler_params=_params("arbitrary", "arbitrary"),
    )(h3, wg, wu, wd)


def _head_bwd(f, x2, tgt, g4):
    tm = 256

    def body(f_ref, x2_ref, t_ref, g_ref, loss_ref, dy_ref, df_ref, dg_ref):
        @pl.when(pl.program_id(0) == 0)
        def _():
            loss_ref[...] = jnp.zeros_like(loss_ref)
            dg_ref[...] = jnp.zeros_like(dg_ref)

        fv = f_ref[...]
        r = _rstd(fv)
        fn = fv * r
        e = x2_ref[...] + fn * g_ref[...] - t_ref[...]
        sq = jnp.sum(jnp.sum(e * e, axis=-1, keepdims=True), axis=0, keepdims=True)
        loss_ref[...] = loss_ref[...] + sq
        dy = e * (1.0 / D)
        dy_ref[...] = dy
        dg_ref[...] = dg_ref[...] + jnp.sum(dy * fn, axis=0, keepdims=True)
        t = dy * g_ref[...]
        df_ref[...] = (r * (t - fn * jnp.mean(t * fn, axis=-1, keepdims=True))).astype(BF16)

    row = pl.BlockSpec((tm, D), lambda i: (i, 0))
    vec = pl.BlockSpec((1, D), lambda i: (0, 0))
    return pl.pallas_call(
        body, grid=(S // tm,), name="head_bwd",
        in_specs=[row, row, row, vec],
        out_specs=[pl.BlockSpec((8, 128), lambda i: (0, 0)), row, row, vec],
        out_shape=[jax.ShapeDtypeStruct((8, 128), F32), jax.ShapeDtypeStruct((S, D), F32),
                   jax.ShapeDtypeStruct((S, D), BF16), jax.ShapeDtypeStruct((1, D), F32)],
        compiler_params=_params("arbitrary"),
    )(f, x2, tgt, g4)


def _ffn_bwd_act(df, gt, up, wg, wu, wd):
    tm = 512

    def body(df_ref, gt_ref, up_ref, wg_ref, wu_ref, wd_ref, dgt_ref, dup_ref, dh_ref):
        k = pl.program_id(1)
        da = _nt(df_ref[...], wd_ref[...])
        gt, up = gt_ref[...], up_ref[...]
        sg = _sigmoid(gt)
        dup = (da * gt * sg).astype(BF16)
        dgt = (da * up * (sg * (1.0 + gt * (1.0 - sg)))).astype(BF16)
        dup_ref[...] = dup
        dgt_ref[...] = dgt
        part = _nt(dgt, wg_ref[...]) + _nt(dup, wu_ref[...])

        @pl.when(k == 0)
        def _():
            dh_ref[...] = part

        @pl.when(k > 0)
        def _():
            dh_ref[...] = dh_ref[...] + part

    wcol = pl.BlockSpec((None, D, FF_C), lambda i, k: (k, 0, 0))
    act = pl.BlockSpec((None, tm, FF_C), lambda i, k: (k, i, 0))
    row = pl.BlockSpec((tm, D), lambda i, k: (i, 0))
    return pl.pallas_call(
        body, grid=(S // tm, N_CHIP), name="ffn_bwd_act",
        in_specs=[row, act, act, wcol, wcol, pl.BlockSpec((None, FF_C, D), lambda i, k: (k, 0, 0))],
        out_specs=[act, act, row],
        out_shape=[jax.ShapeDtypeStruct((N_CHIP, S, FF_C), BF16), jax.ShapeDtypeStruct((N_CHIP, S, FF_C), BF16),
                   jax.ShapeDtypeStruct((S, D), F32)],
        compiler_params=_params("parallel", "arbitrary"),
    )(df, gt, up, wg, wu, wd)


def _ffn_bwd_w(a, df, h3, dgt, dup):
    tm = 512

    def body(a_ref, df_ref, h_ref, dgt_ref, dup_ref, dwd_ref, dwg_ref, dwu_ref, acc_d, acc_g, acc_u):
        i = pl.program_id(1)

        @pl.when(i == 0)
        def _():
            acc_d[...] = jnp.zeros_like(acc_d)
            acc_g[...] = jnp.zeros_like(acc_g)
            acc_u[...] = jnp.zeros_like(acc_u)

        h = h_ref[...]
        acc_d[...] += _tn(a_ref[...], df_ref[...])
        acc_g[...] += _tn(h, dgt_ref[...])
        acc_u[...] += _tn(h, dup_ref[...])

        @pl.when(i == S // tm - 1)
        def _():
            dwd_ref[...] = acc_d[...].astype(BF16)
            dwg_ref[...] = acc_g[...].astype(BF16)
            dwu_ref[...] = acc_u[...].astype(BF16)

    act = pl.BlockSpec((None, tm, FF_C), lambda k, i: (k, i, 0))
    row = pl.BlockSpec((tm, D), lambda k, i: (i, 0))
    wcol = pl.BlockSpec((None, D, FF_C), lambda k, i: (k, 0, 0))
    return pl.pallas_call(
        body, grid=(N_CHIP, S // tm), name="ffn_bwd_w",
        in_specs=[act, row, row, act, act],
        out_specs=[pl.BlockSpec((None, FF_C, D), lambda k, i: (k, 0, 0)), wcol, wcol],
        out_shape=[jax.ShapeDtypeStruct((N_CHIP, FF_C, D), BF16), jax.ShapeDtypeStruct((N_CHIP, D, FF_C), BF16),
                   jax.ShapeDtypeStruct((N_CHIP, D, FF_C), BF16)],
        scratch_shapes=[pltpu.VMEM((FF_C, D), F32), pltpu.VMEM((D, FF_C), F32), pltpu.VMEM((D, FF_C), F32)],
        compiler_params=_params("parallel", "arbitrary"),
    )(a, df, h3, dgt, dup)


def _norm_bwd(dh3, dy, x2, mix, g2, g3):
    tm = 256

    def body(dh_ref, dy_ref, x2_ref, mix_ref, g2_ref, g3_ref, dx2_ref, dmix_ref, dg3_ref, dg2_ref):
        @pl.when(pl.program_id(0) == 0)
        def _():
            dg3_ref[...] = jnp.zeros_like(dg3_ref)
            dg2_ref[...] = jnp.zeros_like(dg2_ref)

        x2 = x2_ref[...]
        r3 = _rstd(x2)
        xn = x2 * r3
        dh = dh_ref[...]
        dg3_ref[...] = dg3_ref[...] + jnp.sum(dh * xn, axis=0, keepdims=True)
        t = dh * g3_ref[...]
        dx2 = dy_ref[...] + r3 * (t - xn * jnp.mean(t * xn, axis=-1, keepdims=True))
        dx2_ref[...] = dx2
        mix = mix_ref[...]
        r2 = _rstd(mix)
        mn = mix * r2
        dg2_ref[...] = dg2_ref[...] + jnp.sum(dx2 * mn, axis=0, keepdims=True)
        u = dx2 * g2_ref[...]
        dmix_ref[...] = (r2 * (u - mn * jnp.mean(u * mn, axis=-1, keepdims=True))).astype(BF16)

    row = pl.BlockSpec((tm, D), lambda i: (i, 0))
    vec = pl.BlockSpec((1, D), lambda i: (0, 0))
    return pl.pallas_call(
        body, grid=(S // tm,), name="norm_bwd",
        in_specs=[row, row, row, row, vec, vec], out_specs=[row, row, vec, vec],
        out_shape=[jax.ShapeDtypeStruct((S, D), F32), jax.ShapeDtypeStruct((S, D), BF16),
                   jax.ShapeDtypeStruct((1, D), F32), jax.ShapeDtypeStruct((1, D), F32)],
        compiler_params=_params("arbitrary"),
    )(dh3, dy, x2, mix, g2, g3)


def _mix_bwd(dmix, cat_r, cat_a, wout):
    tm = 512

    def body(dm_ref, cr_ref, ca_ref, w_ref, dret_ref, datt_ref, dw_ref, acc):
        i = pl.program_id(0)

        @pl.when(i == 0)
        def _():
            acc[...] = jnp.zeros_like(acc)

        dm = dm_ref[...]
        dret_ref[...] = _nt(dm, w_ref[0:512, :])
        datt_ref[...] = _nt(dm, w_ref[512:1024, :])
        acc[0:512, :] += _tn(cr_ref[...], dm)
        acc[512:1024, :] += _tn(ca_ref[...], dm)

        @pl.when(i == S // tm - 1)
        def _():
            dw_ref[...] = acc[...].astype(BF16)

    row = lambda w: pl.BlockSpec((tm, w), lambda i: (i, 0))
    full = pl.BlockSpec((D, D), lambda i: (0, 0))
    return pl.pallas_call(
        body, grid=(S // tm,), name="mix_bwd",
        in_specs=[row(D), row(512), row(512), full], out_specs=[row(512), row(512), full],
        out_shape=[jax.ShapeDtypeStruct((S, 512), F32), jax.ShapeDtypeStruct((S, 512), F32),
                   jax.ShapeDtypeStruct((D, D), BF16)],
        scratch_shapes=[pltpu.VMEM((D, D), F32)],
        compiler_params=_params("arbitrary"),
    )(dmix, cat_r, cat_a, wout)


def _att_bwd(aq, ak, av, datt, att_out, lse, d):
    lc = S // d
    nb = lc // ATT_BLK
    has_prev = nb > 1
    nk = 2 * ATT_BLK if has_prev else ATT_BLK
    view = lambda t: t.reshape(lc, d * 512)

    def body(*refs):
        if has_prev:
            (q_ref, kp_ref, kc_ref, vp_ref, vc_ref, do_ref, out_ref, l_ref,
             dq_ref, dka_ref, dkb_ref, dva_ref, dvb_ref, kcat, vcat) = refs
            kcat[0:ATT_BLK, :] = kp_ref[...]
            kcat[ATT_BLK:nk, :] = kc_ref[...]
            vcat[0:ATT_BLK, :] = vp_ref[...]
            vcat[ATT_BLK:nk, :] = vc_ref[...]
        else:
            q_ref, kcat, vcat, do_ref, out_ref, l_ref, dq_ref, dka_ref, dva_ref = refs
        valid = _att_mask(pl.program_id(1), has_prev)
        lane_head = lax.broadcasted_iota(jnp.int32, (ATT_BLK, 256), 1) // 64
        for g in range(2):
            sl = slice(256 * g, 256 * g + 256)
            qg, kg, vg = q_ref[:, sl], kcat[:, sl], vcat[:, sl]
            dog, outg, lg = do_ref[:, sl], out_ref[:, sl], l_ref[:, sl]
            dq = jnp.zeros((ATT_BLK, 256), F32)
            dk = jnp.zeros((nk, 256), F32)
            dv = jnp.zeros((nk, 256), F32)
            for hh in range(4):
                mine = lane_head == hh
                qm = jnp.where(mine, qg, jnp.zeros_like(qg))
                dom = jnp.where(mine, dog, 0.0)
                delta = jnp.sum(dom * outg, axis=-1, keepdims=True)
                lh = jnp.max(jnp.where(mine, lg, NEG), axis=-1, keepdims=True)
                s = jnp.where(valid, _nt(qm, kg) * ATT_SCALE, NEG)
                p = jnp.exp(s - lh)
                domb = dom.astype(BF16)
                dp = _nt(domb, vg)
                ds = (p * (dp - delta) * ATT_SCALE).astype(BF16)
                dq = jnp.where(mine, _nn(ds, kg), dq)
                dk = dk + _tn(ds, qm)
                dv = dv + _tn(p.astype(BF16), domb)
            dq_ref[:, sl] = dq
            if has_prev:
                dkb_ref[:, sl] = dk[0:ATT_BLK]
                dka_ref[:, sl] = dk[ATT_BLK:nk]
                dvb_ref[:, sl] = dv[0:ATT_BLK]
                dva_ref[:, sl] = dv[ATT_BLK:nk]
            else:
                dka_ref[:, sl] = dk
                dva_ref[:, sl] = dv

    cur, prev = _att_specs(d)
    prev_out = pl.BlockSpec((ATT_BLK, 512), lambda r, ib: ((ib + nb - 1) % nb, r))
    shp = jax.ShapeDtypeStruct((lc, d * 512), F32)
    if has_prev:
        in_specs = [cur, prev, cur, prev, cur, cur, cur, cur]
        args = (view(aq), view(ak), view(ak), view(av), view(av), view(datt), view(att_out), view(lse))
        out_specs, out_shape = [cur, cur, prev_out, cur, prev_out], [shp] * 5
        scratch = [pltpu.VMEM((nk, 512), BF16), pltpu.VMEM((nk, 512), BF16)]
    else:
        in_specs = [cur] * 6
        args = (view(aq), view(ak), view(av), view(datt), view(att_out), view(lse))
        out_specs, out_shape, scratch = [cur] * 3, [shp] * 3, []
    outs = pl.pallas_call(
        body, grid=(d, nb), name=f"att_bwd_d{d}",
        in_specs=in_specs, out_specs=out_specs, out_shape=out_shape, scratch_shapes=scratch,
        compiler_params=_params("parallel", "arbitrary"),
    )(*args)
    return [o.reshape(S, 512) for o in outs]


def _ret_bwd(qr, kr, rv, proj, o_raw, states, dret, tabs):
    C = RET_C
    nc = S // C
    dtab, a_tab, b_tab, lam, bd = tabs

    def body(q_ref, k_ref, v_ref, g_ref, o_ref, st_ref, dr_ref, dt_ref, a_ref, b_ref, lam_ref, bd_ref,
             dq_ref, dk_ref, dv_ref, dg_ref, dR):
        @pl.when(pl.program_id(0) == 0)
        def _():
            dR[...] = jnp.zeros_like(dR)

        q, k, v = q_ref[...], k_ref[...], v_ref[...]
        lane_head = lax.broadcasted_iota(jnp.int32, (C, 256), 1) // 32
        col_head = lax.broadcasted_iota(jnp.int32, (C, 256), 1) // 64
        dos = []
        for j in range(4):
            sl = slice(128 * j, 128 * j + 128)
            oj = o_ref[:, sl]
            xc = oj - _seg_mean(oj)
            rs = lax.rsqrt(_seg_mean(xc * xc) + GN_EPS)
            rn = xc * rs
            gj = g_ref[:, sl]
            sg = _sigmoid(gj)
            dret = dr_ref[:, sl]
            dg_ref[:, sl] = dret * rn * (sg * (1.0 + gj * (1.0 - sg)))
            drn = dret * (gj * sg)
            dos.append(rs * (drn - _seg_mean(drn) - rn * _seg_mean(drn * rn)))
        do = [jnp.concatenate(dos[0:2], axis=1), jnp.concatenate(dos[2:4], axis=1)]
        do8 = jnp.concatenate(do, axis=1).astype(BF16)
        drb = dR[...].astype(BF16)
        rb = st_ref[...]
        dq = _nt(do8, rb) * a_ref[...]
        dk = _nt(v, drb) * b_ref[...]
        kb = (k.astype(F32) * b_ref[...]).astype(BF16)
        dvall = _nn(kb, drb)
        dv = [dvall[:, 0:256], dvall[:, 256:512]]
        for h in range(8):
            g = h // 4
            vg = v[:, 256 * g:256 * g + 256]
            mine = lane_head == h
            qm = jnp.where(mine, q, jnp.zeros_like(q))
            dom = jnp.where(col_head == (h % 4), do[g], 0.0).astype(BF16)
            dec = dt_ref[h]
            p = (_nt(qm, k) * dec).astype(BF16)
            ds = (_nt(dom, vg) * dec).astype(BF16)
            dq = jnp.where(mine, dq + _nn(ds, k), dq)
            dk = dk + _tn(ds, qm)
            dv[g] = dv[g] + _tn(p, dom)
        qa = (q.astype(F32) * a_ref[...]).astype(BF16)
        dR[...] = dR[...] * lam_ref[...] + _tn(qa, do8) * bd_ref[...]
        dq_ref[...] = dq
        dk_ref[...] = dk
        dv_ref[:, 0:256] = dv[0]
        dv_ref[:, 256:512] = dv[1]

    rev = lambda w: pl.BlockSpec((C, w), lambda n: (nc - 1 - n, 0))
    full = lambda a: pl.BlockSpec(a.shape, lambda n: (0,) * a.ndim)
    return pl.pallas_call(
        body, grid=(nc,), name="ret_bwd",
        in_specs=[rev(256), rev(256), rev(512), pl.BlockSpec((C, 512), lambda n: (nc - 1 - n, 2)), rev(512),
                  pl.BlockSpec((None, 256, 512), lambda n: (nc - 1 - n, 0, 0)), rev(512),
                  full(dtab), full(a_tab), full(b_tab), full(lam), full(bd)],
        out_specs=[rev(256), rev(256), rev(512), rev(512)],
        out_shape=[jax.ShapeDtypeStruct((S, 256), F32), jax.ShapeDtypeStruct((S, 256), F32),
                   jax.ShapeDtypeStruct((S, 512), F32), jax.ShapeDtypeStruct((S, 512), F32)],
        scratch_shapes=[pltpu.VMEM((256, 512), F32)],
        compiler_params=_params("arbitrary"),
    )(qr, kr, rv, proj, o_raw, states, dret, dtab, a_tab, b_tab, lam, bd)


def _rot_bwd(pos, ifr, ifa, dqr, dkr, drv, drg, dq_att, dk_att, dv_att):
    tm = 256
    n_q, n_k, n_v = len(dq_att), len(dk_att), len(dv_att)

    def body(*refs):
        pos_ref, ifr_ref, ifa_ref, dqr_ref, dkr_ref, drv_ref, drg_ref = refs[:7]
        rest = refs[7:]
        dq_refs, dk_refs, dv_refs = rest[:n_q], rest[n_q:n_q + n_k], rest[n_q + n_k:n_q + n_k + n_v]
        dp_ref = rest[-1]
        cr, sr, lo_r, ca, sa, lo_a = _rot_coeffs(pos_ref, ifr_ref, ifa_ref, tm)

        def unrot_r(g):
            gs = g * sr
            return g * cr + pltpu.roll(jnp.where(lo_r, -gs, 0.0), 16, 1) + pltpu.roll(jnp.where(lo_r, 0.0, gs), 240, 1)

        def unrot_a(g):
            gs = g * sa
            return g * ca + pltpu.roll(jnp.where(lo_a, -gs, 0.0), 8, 1) + pltpu.roll(jnp.where(lo_a, 0.0, gs), 504, 1)

        def total(rs):
            t = rs[0][...]
            for r in rs[1:]:
                t = t + r[...]
            return t

        dp_ref[:, 0:256] = unrot_r(dqr_ref[...]).astype(BF16)
        dp_ref[:, 256:512] = unrot_r(dkr_ref[...] * RET_SCALE).astype(BF16)
        dp_ref[:, 512:1024] = drv_ref[...].astype(BF16)
        dp_ref[:, 1024:1536] = drg_ref[...].astype(BF16)
        dp_ref[:, 1536:2048] = unrot_a(total(dq_refs)).astype(BF16)
        dp_ref[:, 2048:2560] = unrot_a(total(dk_refs)).astype(BF16)
        dp_ref[:, 2560:3072] = total(dv_refs).astype(BF16)

    row = lambda w: pl.BlockSpec((tm, w), lambda i: (i, 0))
    const = lambda w: pl.BlockSpec((1, w), lambda i: (0, 0))
    return pl.pallas_call(
        body, grid=(S // tm,), name="rot_bwd",
        in_specs=[row(1), const(256), const(512), row(256), row(256), row(512), row(512)]
                 + [row(512)] * (n_q + n_k + n_v),
        out_specs=row(PW), out_shape=jax.ShapeDtypeStruct((S, PW), BF16),
        compiler_params=_params("parallel"),
    )(pos, ifr, ifa, dqr, dkr, drv, drg, *dq_att, *dk_att, *dv_att)


def _win_bwd_w(h1, dproj):
    tm = 512

    def body(h_ref, dp_ref, dw_ref, acc):
        i = pl.program_id(1)

        @pl.when(i == 0)
        def _():
            acc[...] = jnp.zeros_like(acc)

        acc[...] += _tn(h_ref[...], dp_ref[...])

        @pl.when(i == S // tm - 1)
        def _():
            dw_ref[...] = acc[...].astype(BF16)

    return pl.pallas_call(
        body, grid=(N_CHIP, S // tm), name="win_bwd_w",
        in_specs=[pl.BlockSpec((tm, D), lambda k, i: (i, 0)), pl.BlockSpec((tm, WIN_C), lambda k, i: (i, k))],
        out_specs=pl.BlockSpec((None, D, WIN_C), lambda k, i: (k, 0, 0)),
        out_shape=jax.ShapeDtypeStruct((N_CHIP, D, WIN_C), BF16),
        scratch_shapes=[pltpu.VMEM((D, WIN_C), F32)],
        compiler_params=_params("parallel", "arbitrary"),
    )(h1, dproj)


def _in_bwd(dproj, win_g, x, dx2, g1):
    tm = 256

    def body(dp_ref, w_ref, x_ref, dx2_ref, g_ref, dx_ref, dg_ref):
        @pl.when(pl.program_id(0) == 0)
        def _():
            dg_ref[...] = jnp.zeros_like(dg_ref)

        dh = _nt(dp_ref[:, 0:WIN_C], w_ref[0])
        for k in range(1, N_CHIP):
            dh = dh + _nt(dp_ref[:, k * WIN_C:(k + 1) * WIN_C], w_ref[k])
        xv = x_ref[...]
        r = _rstd(xv)
        xn = xv * r
        dg_ref[...] = dg_ref[...] + jnp.sum(dh * xn, axis=0, keepdims=True)
        t = dh * g_ref[...]
        dx_ref[...] = dx2_ref[...] + r * (t - xn * jnp.mean(t * xn, axis=-1, keepdims=True))

    row = lambda w: pl.BlockSpec((tm, w), lambda i: (i, 0))
    vec = pl.BlockSpec((1, D), lambda i: (0, 0))
    return pl.pallas_call(
        body, grid=(S // tm,), name="in_bwd",
        in_specs=[row(PW), pl.BlockSpec((N_CHIP, D, WIN_C), lambda i: (0, 0, 0)), row(D), row(D), vec],
        out_specs=[row(D), vec],
        out_shape=[jax.ShapeDtypeStruct((S, D), F32), jax.ShapeDtypeStruct((1, D), F32)],
        compiler_params=_params("arbitrary"),
    )(dproj, win_g, x, dx2, g1)


def _local_step(x, pos, tgt, g1, g2, g3, g4, win_g, wout_g, wg_g, wu_g, wd_g):
    tabs = tuple(jnp.asarray(t) for t in _retention_tables())
    ifr, ifa = (jnp.asarray(t) for t in _rotary_tables())

    proj, h1 = _proj_fwd(x, g1, win_g)
    qr, kr, rv, aq, ak, av = _rot_fwd(proj, pos, ifr, ifa)
    o_raw, cat_r, states = _ret_fwd(qr, kr, rv, proj, tabs)
    pat = [_att_fwd(aq, ak, av, d) for d in PATTERN_DILATIONS]
    att_out, lse, cat_a = _att_merge([p[0] for p in pat], [p[1] for p in pat])
    mix, x2, h3 = _mix_fwd(cat_r, cat_a, wout_g, x, g2, g3)
    gt, up, a, f = _ffn_fwd(h3, wg_g, wu_g, wd_g)

    sq, dy, df, dg4 = _head_bwd(f, x2, tgt, g4)
    dgt, dup, dh3 = _ffn_bwd_act(df, gt, up, wg_g, wu_g, wd_g)
    dwd, dwg, dwu = _ffn_bwd_w(a, df, h3, dgt, dup)
    dx2, dmix, dg3, dg2 = _norm_bwd(dh3, dy, x2, mix, g2, g3)
    dret, datt, dwout = _mix_bwd(dmix, cat_r, cat_a, wout_g)
    dq_att, dk_att, dv_att = [], [], []
    for d in PATTERN_DILATIONS:
        outs = _att_bwd(aq, ak, av, datt, att_out, lse, d)
        dq_att.append(outs[0])
        if len(outs) == 5:
            dk_att += [outs[1], outs[2]]
            dv_att += [outs[3], outs[4]]
        else:
            dk_att.append(outs[1])
            dv_att.append(outs[2])
    dqr, dkr, drv, drg = _ret_bwd(qr, kr, rv, proj, o_raw, states, dret, tabs)
    dproj = _rot_bwd(pos, ifr, ifa, dqr, dkr, drv, drg, dq_att, dk_att, dv_att)
    dwin = _win_bwd_w(h1, dproj)
    dx, dg1 = _in_bwd(dproj, win_g, x, dx2, g1)
    return sq[0, 0], dx, (dwin, dwout, dwg, dwu, dwd), (dg1, dg2, dg3, dg4)


ANY = pl.BlockSpec(memory_space=pl.ANY)
FLIPS = ((1, 0), (0, 1), (1, 1))


def _place():
    x, y, c = lax.axis_index("x"), lax.axis_index("y"), lax.axis_index("c")
    chips = [((1 - x) if fx else x, (1 - y) if fy else y) for fx, fy in FLIPS]
    return x, y, c, 2 * x + y, chips


def _remote(src, dst, send_sem, recv_sem, device):
    return pltpu.make_async_remote_copy(src_ref=src, dst_ref=dst, send_sem=send_sem, recv_sem=recv_sem,
                                        device_id=device, device_id_type=MESH)


def _gather_weights(shards):
    n = len(shards)

    def body(*refs):
        ins, outs = refs[:n], refs[n:2 * n]
        send_sems, recv_sems, local_sems = refs[2 * n:]
        x, y, c, me, chips = _place()
        sibling = (x, y, 1 - c)
        local, sends = [], []
        for a in range(n):
            half = ins[a].shape[0] // 2
            own = pltpu.make_async_copy(ins[a], outs[a].at[me], local_sems.at[a])
            own.start()
            local.append(own)
            for j, (cx, cy) in enumerate(chips):
                cp = _remote(ins[a].at[pl.ds(c * half, half), :], outs[a].at[me, pl.ds(c * half, half), :],
                             send_sems.at[6 * a + j], recv_sems.at[6 * a + j], (cx, cy, c))
                cp.start()
                sends.append(cp)
        for a in range(n):
            half = ins[a].shape[0] // 2
            for j, (cx, cy) in enumerate(chips):
                blk = outs[a].at[2 * cx + cy, pl.ds(c * half, half), :]
                _remote(blk, blk, send_sems.at[6 * a + j], recv_sems.at[6 * a + j], sibling).wait_recv()
                fwd = _remote(blk, blk, send_sems.at[6 * a + 3 + j], recv_sems.at[6 * a + 3 + j], sibling)
                fwd.start()
                sends.append(fwd)
        for a in range(n):
            half = ins[a].shape[0] // 2
            for j, (cx, cy) in enumerate(chips):
                blk = outs[a].at[2 * cx + cy, pl.ds((1 - c) * half, half), :]
                _remote(blk, blk, send_sems.at[6 * a + 3 + j], recv_sems.at[6 * a + 3 + j], sibling).wait_recv()
        for cp in sends:
            cp.wait_send()
        for cp in local:
            cp.wait()

    return pl.pallas_call(
        body, name="gather_weights",
        in_specs=[ANY] * n, out_specs=[ANY] * n,
        out_shape=[jax.ShapeDtypeStruct((N_CHIP,) + s.shape, s.dtype) for s in shards],
        scratch_shapes=[pltpu.SemaphoreType.DMA((6 * n,)), pltpu.SemaphoreType.DMA((6 * n,)),
                        pltpu.SemaphoreType.DMA((n,))],
    )(*shards)


def _reduce_to_sibling(grads, gvec):
    n = len(grads)

    def body(*refs):
        ins, gv_ref = refs[:n], refs[n]
        outs, gall_ref = refs[n + 1:2 * n + 1], refs[2 * n + 1]
        send_sems, recv_sems, gsend, grecv, local_sem = refs[2 * n + 2:]
        x, y, c, me, chips = _place()
        sibling = (x, y, 1 - c)
        dev = 2 * me + c
        own = pltpu.make_async_copy(gv_ref, gall_ref.at[dev], local_sem)
        own.start()
        sends = []
        for a in range(n):
            half = ins[a].shape[1] // 2
            cp = _remote(ins[a].at[:, pl.ds((1 - c) * half, half), :], outs[a],
                         send_sems.at[a], recv_sems.at[a], sibling)
            cp.start()
            sends.append(cp)
        peers = []
        for f in range(1, 8):
            fx, fy, fc = (f >> 2) & 1, (f >> 1) & 1, f & 1
            px, py, pc = (1 - x) if fx else x, (1 - y) if fy else y, (1 - c) if fc else c
            peers.append((px, py, pc))
            cp = _remote(gv_ref, gall_ref.at[dev], gsend.at[f - 1], grecv.at[f - 1], (px, py, pc))
            cp.start()
            sends.append(cp)
        for a in range(n):
            _remote(outs[a], outs[a], send_sems.at[a], recv_sems.at[a], sibling).wait_recv()
        for f, (px, py, pc) in enumerate(peers):
            blk = gall_ref.at[4 * px + 2 * py + pc]
            _remote(blk, blk, gsend.at[f], grecv.at[f], sibling).wait_recv()
        for cp in sends:
            cp.wait_send()
        own.wait()

    return pl.pallas_call(
        body, name="reduce_to_sibling",
        in_specs=[ANY] * (n + 1), out_specs=[ANY] * (n + 1),
        out_shape=[jax.ShapeDtypeStruct((N_CHIP, g.shape[1] // 2, g.shape[2]), g.dtype) for g in grads]
                  + [jax.ShapeDtypeStruct((8,) + gvec.shape, gvec.dtype)],
        scratch_shapes=[pltpu.SemaphoreType.DMA((n,)), pltpu.SemaphoreType.DMA((n,)),
                        pltpu.SemaphoreType.DMA((7,)), pltpu.SemaphoreType.DMA((7,)), pltpu.SemaphoreType.DMA],
    )(*grads, gvec)


def _core_index():
    return lax.axis_index("c").astype(jnp.int32).reshape(1)


def _pair_sum(g, got):
    _, r, cc = g.shape
    half = r // 2
    tr = half // 2

    def body(c_ref, g_ref, got_ref, out_ref):
        out_ref[...] = (g_ref[...].astype(F32) + got_ref[...].astype(F32)).astype(BF16)

    return pl.pallas_call(
        body, name=f"pair_sum_{r}x{cc}",
        grid_spec=pltpu.PrefetchScalarGridSpec(
            num_scalar_prefetch=1, grid=(N_CHIP, 2),
            in_specs=[pl.BlockSpec((None, tr, cc), lambda k, i, c_ref: (k, 2 * c_ref[0] + i, 0)),
                      pl.BlockSpec((None, tr, cc), lambda k, i, c_ref: (k, i, 0))],
            out_specs=pl.BlockSpec((None, tr, cc), lambda k, i, c_ref: (k, i, 0))),
        out_shape=jax.ShapeDtypeStruct((N_CHIP, half, cc), BF16),
        compiler_params=_params("parallel", "parallel"),
    )(_core_index(), g, got)


def _reduce_over_chips(pre):
    n = len(pre)

    def body(*refs):
        ins, outs = refs[:n], refs[n:2 * n]
        send_sems, recv_sems, local_sems = refs[2 * n:]
        x, y, c, me, chips = _place()
        local, sends = [], []
        for a in range(n):
            own = pltpu.make_async_copy(ins[a].at[me], outs[a].at[me], local_sems.at[a])
            own.start()
            local.append(own)
            for j, (cx, cy) in enumerate(chips):
                cp = _remote(ins[a].at[2 * cx + cy], outs[a].at[me],
                             send_sems.at[3 * a + j], recv_sems.at[3 * a + j], (cx, cy, c))
                cp.start()
                sends.append(cp)
        for a in range(n):
            for j, (cx, cy) in enumerate(chips):
                blk = outs[a].at[2 * cx + cy]
                _remote(blk, blk, send_sems.at[3 * a + j], recv_sems.at[3 * a + j], (cx, cy, c)).wait_recv()
        for cp in sends:
            cp.wait_send()
        for cp in local:
            cp.wait()

    return pl.pallas_call(
        body, name="reduce_over_chips",
        in_specs=[ANY] * n, out_specs=[ANY] * n,
        out_shape=[jax.ShapeDtypeStruct(p.shape, p.dtype) for p in pre],
        scratch_shapes=[pltpu.SemaphoreType.DMA((3 * n,)), pltpu.SemaphoreType.DMA((3 * n,)),
                        pltpu.SemaphoreType.DMA((n,))],
    )(*pre)


def _chip_sum(parts):
    _, half, cc = parts.shape
    tr = half // 2

    def body(p_ref, out_ref):
        out_ref[...] = ((p_ref[0].astype(F32) + p_ref[1].astype(F32)) + p_ref[2].astype(F32)) + p_ref[3].astype(F32)

    return pl.pallas_call(
        body, grid=(2,), name=f"chip_sum_{half}x{cc}",
        in_specs=[pl.BlockSpec((N_CHIP, tr, cc), lambda i: (0, i, 0))],
        out_specs=pl.BlockSpec((tr, cc), lambda i: (i, 0)),
        out_shape=jax.ShapeDtypeStruct((half, cc), F32),
        compiler_params=_params("parallel"),
    )(parts)


def _share_with_sibling(halves):
    n = len(halves)

    def body(*refs):
        ins, outs = refs[:n], refs[n:2 * n]
        send_sems, recv_sems, local_sems = refs[2 * n:]
        x, y, c, me, chips = _place()
        sibling = (x, y, 1 - c)
        local, sends = [], []
        for a in range(n):
            half = ins[a].shape[0]
            mine = outs[a].at[pl.ds(c * half, half), :]
            own = pltpu.make_async_copy(ins[a], mine, local_sems.at[a])
            own.start()
            local.append(own)
            cp = _remote(ins[a], mine, send_sems.at[a], recv_sems.at[a], sibling)
            cp.start()
            sends.append(cp)
        for a in range(n):
            half = ins[a].shape[0]
            theirs = outs[a].at[pl.ds((1 - c) * half, half), :]
            _remote(theirs, theirs, send_sems.at[a], recv_sems.at[a], sibling).wait_recv()
        for cp in sends:
            cp.wait_send()
        for cp in local:
            cp.wait()

    return pl.pallas_call(
        body, name="share_with_sibling",
        in_specs=[ANY] * n, out_specs=[ANY] * n,
        out_shape=[jax.ShapeDtypeStruct((2 * h.shape[0], h.shape[1]), h.dtype) for h in halves],
        scratch_shapes=[pltpu.SemaphoreType.DMA((n,)), pltpu.SemaphoreType.DMA((n,)), pltpu.SemaphoreType.DMA((n,))],
    )(*halves)


def _adamw_math(w, g, m, v):
    m = ADAM_B1 * m + (1.0 - ADAM_B1) * g
    v = ADAM_B2 * v + (1.0 - ADAM_B2) * (g * g)
    m_hat = m / (1.0 - ADAM_B1 ** ADAM_STEP)
    v_hat = v / (1.0 - ADAM_B2 ** ADAM_STEP)
    delta = -ADAM_LR * (m_hat / (jnp.sqrt(v_hat) + ADAM_EPS) + ADAM_WD * w)
    return delta, m, v


def _adamw(w, g, m, v):
    r, cc = w.shape
    tr = r // 4

    def body(w_ref, g_ref, m_ref, v_ref, d_ref, nm_ref, nv_ref):
        d_ref[...], nm_ref[...], nv_ref[...] = _adamw_math(w_ref[...], g_ref[...], m_ref[...], v_ref[...])

    blk = pl.BlockSpec((tr, cc), lambda i: (i, 0))
    return pl.pallas_call(
        body, grid=(4,), name=f"adamw_{r}x{cc}",
        in_specs=[blk] * 4, out_specs=[blk] * 3,
        out_shape=[jax.ShapeDtypeStruct((r, cc), F32)] * 3,
        compiler_params=_params("parallel"),
    )(w, g, m, v)


def _pack8(rows):
    def body(*refs):
        out_ref = refs[-1]
        out_ref[...] = jnp.zeros_like(out_ref)
        for i, r in enumerate(refs[:-1]):
            out_ref[i:i + 1, :] = r[...]

    return pl.pallas_call(body, name="pack8", out_shape=jax.ShapeDtypeStruct((8, D), F32))(*rows)


def _adamw_gains(gall, w8, m8, v8):
    def body(ga_ref, w_ref, m_ref, v_ref, g_ref, d_ref, nm_ref, nv_ref):
        g = ga_ref[0]
        for dev in range(1, 8):
            g = g + ga_ref[dev]
        g_ref[...] = g
        d_ref[...], nm_ref[...], nv_ref[...] = _adamw_math(w_ref[...], g, m_ref[...], v_ref[...])

    return pl.pallas_call(
        body, name="adamw_gains",
        out_shape=[jax.ShapeDtypeStruct((8, D), F32)] * 4,
    )(gall, w8, m8, v8)


def kernel(x, positions, w_in, w_out, g_pre_mix, g_post_mix, g_pre_ffn, g_post_ffn, w_gate, w_up, w_down, loss_target, m_w_in, m_w_out, m_g_pre_mix, m_g_post_mix, m_g_pre_ffn, m_g_post_ffn, m_w_gate, m_w_up, m_w_down, v_w_in, v_w_out, v_g_pre_mix, v_g_post_mix, v_g_pre_ffn, v_g_post_ffn, v_w_gate, v_w_up, v_w_down):
    shards = [w_in[0], w_out[0], w_gate[0], w_up[0], w_down[0]]
    moms = [m_w_in[0], m_w_out[0], m_w_gate[0], m_w_up[0], m_w_down[0]]
    vels = [v_w_in[0], v_w_out[0], v_w_gate[0], v_w_up[0], v_w_down[0]]
    win_g, wout_g, wg_g, wu_g, wd_g = _gather_weights([s.astype(BF16) for s in shards])

    sq, dx, (dwin, dwout, dwg, dwu, dwd), gain_grads = _local_step(
        x[0], positions.reshape(S, 1), loss_target[0], g_pre_mix, g_post_mix, g_pre_ffn, g_post_ffn,
        win_g, wout_g.reshape(D, D), wg_g, wu_g, wd_g)
    loss = 0.5 * lax.psum(sq, ("x", "y", "c")) / D

    grads = [dwin, dwout.reshape(N_CHIP, WOUT_R, D), dwg, dwu, dwd]
    *got, gall = _reduce_to_sibling(grads, _pack8(gain_grads))
    pre = [_pair_sum(g, r) for g, r in zip(grads, got)]
    parts = _reduce_over_chips(pre)
    halves = [_chip_sum(p) for p in parts]
    full = _share_with_sibling(halves)

    upd = [_adamw(w, g, m, v) for w, g, m, v in zip(shards, full, moms, vels)]
    gg, gd, gm, gv = _adamw_gains(gall, _pack8([g_pre_mix, g_post_mix, g_pre_ffn, g_post_ffn]),
                                  _pack8([m_g_pre_mix, m_g_post_mix, m_g_pre_ffn, m_g_post_ffn]),
                                  _pack8([v_g_pre_mix, v_g_post_mix, v_g_pre_ffn, v_g_post_ffn]))

    def order(mats, vecs):
        return [mats[0][None], mats[1][None]] + [vecs[i:i + 1] for i in range(4)] + [t[None] for t in mats[2:]]

    return (loss, dx[None],
            *order(full, gg),
            *order([u[0] for u in upd], gd),
            *order([u[1] for u in upd], gm),
            *order([u[2] for u in upd], gv))
```

```python
import functools

import numpy as np
import jax
import jax.numpy as jnp
from jax import lax
from jax.experimental import pallas as pl
from jax.experimental.pallas import tpu as pltpu

F32, BF16 = jnp.float32, jnp.bfloat16
MESH = pl.DeviceIdType.MESH

S = 2048
D = 1024
PW = 3072
N_CHIP = 4
WIN_C = PW // N_CHIP
DFF = 2816
FF_C = DFF // N_CHIP
WOUT_R = D // N_CHIP
RMS_EPS = 1e-6
GN_EPS = 1e-5
RET_C = 128
RET_SCALE = 32 ** -0.5
ATT_BLK = 128
ATT_SCALE = 64 ** -0.5
PATTERN_DILATIONS = (1, 4, 16)
NEG = -1e30
VMEM_LIMIT = 56 * 1024 * 1024

ADAM_LR, ADAM_B1, ADAM_B2, ADAM_EPS, ADAM_WD, ADAM_STEP = 0.001, 0.9, 0.999, 1e-08, 0.01, 10


def _params(*sem):
    return pltpu.CompilerParams(dimension_semantics=sem, vmem_limit_bytes=VMEM_LIMIT)


def _nt(a, b):
    return lax.dot_general(a, b, (((1,), (1,)), ((), ())), preferred_element_type=F32)


def _tn(a, b):
    return lax.dot_general(a, b, (((0,), (0,)), ((), ())), preferred_element_type=F32)


def _nn(a, b):
    return jnp.dot(a, b, preferred_element_type=F32)


def _rstd(v):
    return lax.rsqrt(jnp.mean(v * v, axis=-1, keepdims=True) + RMS_EPS)


def _sigmoid(v):
    return 1.0 / (1.0 + jnp.exp(-v))


def _rows(i, t):
    return pl.ds(pl.multiple_of(i * t, t), t)


def _retention_tables():
    h = np.arange(8, dtype=np.float32)
    log_g = np.log1p(-np.exp2(-5.0 - h)).astype(np.float32)
    idx = np.arange(RET_C, dtype=np.float32)
    diff = idx[:, None] - idx[None, :]
    dtab = np.where(diff >= 0, np.exp(log_g[:, None, None] * np.maximum(diff, 0.0)), 0.0).astype(np.float32)
    lane_head = np.arange(256) // 32
    a_tab = np.exp(log_g[lane_head][None, :] * (idx + 1.0)[:, None]).astype(np.float32)
    b_tab = np.exp(log_g[lane_head][None, :] * (RET_C - 1.0 - idx)[:, None]).astype(np.float32)
    lam = np.exp(log_g[lane_head] * RET_C).astype(np.float32)[:, None]
    bd = (lane_head[:, None] == (np.arange(512) // 64)[None, :]).astype(np.float32)
    return dtab, a_tab, b_tab, lam, bd


def _rotary_tables():
    inv_r = (1.0 / (np.float32(10000.0) ** np.linspace(0.0, 1.0, 16, dtype=np.float32))).astype(np.float32)
    inv_a = (np.float32(500000.0) ** (-np.arange(0, 16, 2, dtype=np.float32) / np.float32(16))).astype(np.float32)
    dr = np.arange(256) % 32
    ifr = inv_r[dr % 16][None, :].astype(np.float32)
    da = np.arange(512) % 64
    ifa = np.where(da < 16, inv_a[da % 8], 0.0)[None, :].astype(np.float32)
    return ifr, ifa


def _proj_fwd(x, g1, win_g):
    tm = 256

    def body(x_ref, g_ref, w_ref, proj_ref, h_ref):
        xv = x_ref[...]
        h = (xv * _rstd(xv) * g_ref[...]).astype(BF16)
        h_ref[...] = h
        for k in range(N_CHIP):
            proj_ref[:, k * WIN_C:(k + 1) * WIN_C] = _nn(h, w_ref[k])

    return pl.pallas_call(
        body, grid=(S // tm,), name="proj_fwd",
        in_specs=[pl.BlockSpec((tm, D), lambda i: (i, 0)), pl.BlockSpec((1, D), lambda i: (0, 0)),
                  pl.BlockSpec((N_CHIP, D, WIN_C), lambda i: (0, 0, 0))],
        out_specs=[pl.BlockSpec((tm, PW), lambda i: (i, 0)), pl.BlockSpec((tm, D), lambda i: (i, 0))],
        out_shape=[jax.ShapeDtypeStruct((S, PW), F32), jax.ShapeDtypeStruct((S, D), BF16)],
        compiler_params=_params("parallel"),
    )(x, g1, win_g)


def _rot_coeffs(pos_ref, ifr_ref, ifa_ref, tm):
    pos = pos_ref[...].astype(F32)
    ang_r = pos * ifr_ref[...]
    ang_a = pos * ifa_ref[...]
    lo_r = (lax.broadcasted_iota(jnp.int32, (tm, 256), 1) % 32) < 16
    lo_a = (lax.broadcasted_iota(jnp.int32, (tm, 512), 1) % 64) < 8
    return jnp.cos(ang_r), jnp.sin(ang_r), lo_r, jnp.cos(ang_a), jnp.sin(ang_a), lo_a


def _rot_fwd(proj, pos, ifr, ifa):
    tm = 256

    def body(p_ref, pos_ref, ifr_ref, ifa_ref, qr_ref, kr_ref, rv_ref, aq_ref, ak_ref, av_ref):
        cr, sr, lo_r, ca, sa, lo_a = _rot_coeffs(pos_ref, ifr_ref, ifa_ref, tm)

        def rot_r(v):
            return v * cr + sr * jnp.where(lo_r, -pltpu.roll(v, 240, 1), pltpu.roll(v, 16, 1))

        def rot_a(v):
            return v * ca + sa * jnp.where(lo_a, -pltpu.roll(v, 504, 1), pltpu.roll(v, 8, 1))

        qr_ref[...] = rot_r(p_ref[:, 0:256]).astype(BF16)
        kr_ref[...] = (rot_r(p_ref[:, 256:512]) * RET_SCALE).astype(BF16)
        rv_ref[...] = p_ref[:, 512:1024].astype(BF16)
        aq, ak = rot_a(p_ref[:, 1536:2048]), rot_a(p_ref[:, 2048:2560])
        for j in range(4):
            aq_ref[j] = aq[:, 128 * j:128 * j + 128]
            ak_ref[j] = ak[:, 128 * j:128 * j + 128]
            av_ref[j] = p_ref[:, 2560 + 128 * j:2560 + 128 * j + 128]

    row = lambda w: pl.BlockSpec((tm, w), lambda i: (i, 0))
    const = lambda w: pl.BlockSpec((1, w), lambda i: (0, 0))
    slab = pl.BlockSpec((4, tm, 128), lambda i: (0, i, 0))
    return pl.pallas_call(
        body, grid=(S // tm,), name="rot_fwd",
        in_specs=[row(PW), row(1), const(256), const(512)],
        out_specs=[row(256), row(256), row(512), slab, slab, slab],
        out_shape=[jax.ShapeDtypeStruct((S, w), BF16) for w in (256, 256, 512)]
                  + [jax.ShapeDtypeStruct((4, S, 128), F32)] * 3,
        compiler_params=_params("parallel"),
    )(proj, pos, ifr, ifa)


def _seg_mean(v):
    lo = lax.broadcasted_iota(jnp.int32, v.shape, 1) < 64
    s_lo = jnp.sum(jnp.where(lo, v, 0.0), axis=-1, keepdims=True)
    s_hi = jnp.sum(jnp.where(lo, 0.0, v), axis=-1, keepdims=True)
    return jnp.where(lo, s_lo, s_hi) * (1.0 / 64.0)


def _ret_fwd(qr, kr, rv, proj, tabs):
    C = RET_C
    dtab, a_tab, b_tab, lam, bd = tabs

    def body(q_ref, k_ref, v_ref, g_ref, dt_ref, a_ref, b_ref, lam_ref, bd_ref, o_ref, cat_ref, st_ref, R):
        @pl.when(pl.program_id(0) == 0)
        def _():
            R[...] = jnp.zeros_like(R)

        q, k, v = q_ref[...], k_ref[...], v_ref[...]
        lane_head = lax.broadcasted_iota(jnp.int32, (C, 256), 1) // 32
        col_head = lax.broadcasted_iota(jnp.int32, (C, 256), 1) // 64
        rb = R[...].astype(BF16)
        st_ref[...] = rb
        qa = (q.astype(F32) * a_ref[...]).astype(BF16)
        cross = _nn(qa, rb)
        og = [cross[:, 0:256], cross[:, 256:512]]
        for h in range(8):
            g = h // 4
            qm = jnp.where(lane_head == h, q, jnp.zeros_like(q))
            p = (_nt(qm, k) * dt_ref[h]).astype(BF16)
            pv = _nn(p, v[:, 256 * g:256 * g + 256])
            og[g] = og[g] + jnp.where(col_head == (h % 4), pv, 0.0)
        kb = (k.astype(F32) * b_ref[...]).astype(BF16)
        R[...] = R[...] * lam_ref[...] + _tn(kb, v) * bd_ref[...]
        o_ref[:, 0:256] = og[0]
        o_ref[:, 256:512] = og[1]
        for j in range(4):
            oj = og[j // 2][:, 128 * (j % 2):128 * (j % 2) + 128]
            xc = oj - _seg_mean(oj)
            rn = xc * lax.rsqrt(_seg_mean(xc * xc) + GN_EPS)
            gj = g_ref[:, 128 * j:128 * j + 128]
            cat_ref[:, 128 * j:128 * j + 128] = (rn * (gj * _sigmoid(gj))).astype(BF16)

    row = lambda w: pl.BlockSpec((C, w), lambda n: (n, 0))
    full = lambda a: pl.BlockSpec(a.shape, lambda n: (0,) * a.ndim)
    return pl.pallas_call(
        body, grid=(S // C,), name="ret_fwd",
        in_specs=[row(256), row(256), row(512), pl.BlockSpec((C, 512), lambda n: (n, 2)),
                  full(dtab), full(a_tab), full(b_tab), full(lam), full(bd)],
        out_specs=[row(512), row(512), pl.BlockSpec((None, 256, 512), lambda n: (n, 0, 0))],
        out_shape=[jax.ShapeDtypeStruct((S, 512), F32), jax.ShapeDtypeStruct((S, 512), BF16),
                   jax.ShapeDtypeStruct((S // C, 256, 512), BF16)],
        scratch_shapes=[pltpu.VMEM((256, 512), F32)],
        compiler_params=_params("arbitrary"),
    )(qr, kr, rv, proj, dtab, a_tab, b_tab, lam, bd)


def _att_mask(ib, has_prev):
    nk = 2 * ATT_BLK if has_prev else ATT_BLK
    a = lax.broadcasted_iota(jnp.int32, (ATT_BLK, nk), 0)
    kk = lax.broadcasted_iota(jnp.int32, (ATT_BLK, nk), 1)
    if has_prev:
        dist = ATT_BLK + a - kk
        return (dist >= 0) & (dist <= ATT_BLK) & ((ib * ATT_BLK - ATT_BLK + kk) >= 0)
    return (a - kk) >= 0


def _class_rows(ib, r, d):
    if d == 1:
        return pl.ds(pl.multiple_of(ib * ATT_BLK, ATT_BLK), ATT_BLK)
    return pl.ds(ib * ATT_BLK * d + r, ATT_BLK, stride=d)


def _slab_pair(ref, g, rows):
    return jnp.concatenate([ref[2 * g, rows, :], ref[2 * g + 1, rows, :]], axis=1)


def _att_blocks(d):
    nb = S // d // ATT_BLK
    return nb, nb > 1


def _att_fwd(aq, ak, av):
    def body(q_ref, k_ref, v_ref, o_ref, l_ref, cat_ref):
        lane_head = lax.broadcasted_iota(jnp.int32, (ATT_BLK, 256), 1) // 64
        for pi, d in enumerate(PATTERN_DILATIONS):
            nb, has_prev = _att_blocks(d)

            def block(b, carry, pi=pi, d=d, nb=nb, has_prev=has_prev):
                r, ib = b // nb, b % nb
                rows = _class_rows(ib, r, d)
                prow = _class_rows(jnp.maximum(ib - 1, 0), r, d)
                valid = _att_mask(ib, has_prev)
                for g in range(2):
                    qg = _slab_pair(q_ref, g, rows).astype(BF16)
                    kg = _slab_pair(k_ref, g, rows)
                    vg = _slab_pair(v_ref, g, rows)
                    if has_prev:
                        kg = jnp.concatenate([_slab_pair(k_ref, g, prow), kg], axis=0)
                        vg = jnp.concatenate([_slab_pair(v_ref, g, prow), vg], axis=0)
                    kg, vg = kg.astype(BF16), vg.astype(BF16)
                    og = jnp.zeros((ATT_BLK, 256), F32)
                    lg = jnp.zeros((ATT_BLK, 256), F32)
                    for hh in range(4):
                        qm = jnp.where(lane_head == hh, qg, jnp.zeros_like(qg))
                        s = jnp.where(valid, _nt(qm, kg) * ATT_SCALE, NEG)
                        m = jnp.max(s, axis=-1, keepdims=True)
                        p = jnp.exp(s - m)
                        den = jnp.sum(p, axis=-1, keepdims=True)
                        o = _nn(p.astype(BF16), vg) / den
                        og = jnp.where(lane_head == hh, o, og)
                        lg = jnp.where(lane_head == hh, m + jnp.log(den), lg)
                    for jj in range(2):
                        j = 2 * g + jj
                        o_new, l_new = og[:, 128 * jj:128 * jj + 128], lg[:, 128 * jj:128 * jj + 128]
                        if pi > 0:
                            o_old, l_old = o_ref[j, rows, :], l_ref[j, rows, :]
                            mx = jnp.maximum(l_old, l_new)
                            ea, eb = jnp.exp(l_old - mx), jnp.exp(l_new - mx)
                            den = ea + eb
                            o_new = (ea * o_old + eb * o_new) / den
                            l_new = mx + jnp.log(den)
                        o_ref[j, rows, :] = o_new
                        l_ref[j, rows, :] = l_new
                return carry

            lax.fori_loop(0, S // ATT_BLK, block, 0)

        def to_cat(i, carry):
            rows = _rows(i, 256)
            for j in range(4):
                cat_ref[rows, 128 * j:128 * j + 128] = o_ref[j, rows, :].astype(BF16)
            return carry

        lax.fori_loop(0, S // 256, to_cat, 0)

    slab = jax.ShapeDtypeStruct((4, S, 128), F32)
    return pl.pallas_call(
        body, name="att_fwd", out_shape=[slab, slab, jax.ShapeDtypeStruct((S, 512), BF16)],
        compiler_params=pltpu.CompilerParams(vmem_limit_bytes=VMEM_LIMIT),
    )(aq, ak, av)


def _mix_fwd(cat_r, cat_a, wout, x, g2, g3):
    tm = 256

    def body(cr_ref, ca_ref, w_ref, x_ref, g2_ref, g3_ref, mix_ref, x2_ref, h3_ref):
        mix = _nn(cr_ref[...], w_ref[0:512, :]) + _nn(ca_ref[...], w_ref[512:1024, :])
        mix_ref[...] = mix
        x2 = x_ref[...] + mix * _rstd(mix) * g2_ref[...]
        x2_ref[...] = x2
        h3_ref[...] = (x2 * _rstd(x2) * g3_ref[...]).astype(BF16)

    row = lambda w: pl.BlockSpec((tm, w), lambda i: (i, 0))
    vec = pl.BlockSpec((1, D), lambda i: (0, 0))
    return pl.pallas_call(
        body, grid=(S // tm,), name="mix_fwd",
        in_specs=[row(512), row(512), pl.BlockSpec((D, D), lambda i: (0, 0)), row(D), vec, vec],
        out_specs=[row(D), row(D), row(D)],
        out_shape=[jax.ShapeDtypeStruct((S, D), F32), jax.ShapeDtypeStruct((S, D), F32),
                   jax.ShapeDtypeStruct((S, D), BF16)],
        compiler_params=_params("parallel"),
    )(cat_r, cat_a, wout, x, g2, g3)


def _ffn_fwd(h3, wg, wu, wd):
    tm = 256

    def body(h_ref, wg_ref, wu_ref, wd_ref, gt_ref, up_ref, a_ref, f_ref):
        k, i = pl.program_id(0), pl.program_id(1)
        h = h_ref[...]
        gt = _nn(h, wg_ref[...])
        up = _nn(h, wu_ref[...])
        gt_ref[...] = gt
        up_ref[...] = up
        a = (gt * _sigmoid(gt) * up).astype(BF16)
        a_ref[...] = a
        part = _nn(a, wd_ref[...])
        rows = _rows(i, tm)

        @pl.when(k == 0)
        def _():
            f_ref[rows, :] = part

        @pl.when(k > 0)
        def _():
            f_ref[rows, :] = f_ref[rows, :] + part

    wcol = pl.BlockSpec((None, D, FF_C), lambda k, i: (k, 0, 0))
    act = pl.BlockSpec((None, tm, FF_C), lambda k, i: (k, i, 0))
    return pl.pallas_call(
        body, grid=(N_CHIP, S // tm), name="ffn_fwd",
        in_specs=[pl.BlockSpec((tm, D), lambda k, i: (i, 0)), wcol, wcol,
                  pl.BlockSpec((None, FF_C, D), lambda k, i: (k, 0, 0))],
        out_specs=[act, act, act, pl.BlockSpec((S, D), lambda k, i: (0, 0))],
        out_shape=[jax.ShapeDtypeStruct((N_CHIP, S, FF_C), F32), jax.ShapeDtypeStruct((N_CHIP, S, FF_C), F32),
                   jax.ShapeDtypeStruct((N_CHIP, S, FF_C), BF16), jax.ShapeDtypeStruct((S, D), F32)],
        compiler_params=_params("arbitrary", "arbitrary"),
    )(h3, wg, wu, wd)


def _head_bwd(f, x2, tgt, g4):
    tm = 256

    def body(f_ref, x2_ref, t_ref, g_ref, loss_ref, dy_ref, df_ref, dg_ref):
        @pl.when(pl.program_id(0) == 0)
        def _():
            loss_ref[...] = jnp.zeros_like(loss_ref)
            dg_ref[...] = jnp.zeros_like(dg_ref)

        fv = f_ref[...]
        r = _rstd(fv)
        fn = fv * r
        e = x2_ref[...] + fn * g_ref[...] - t_ref[...]
        sq = jnp.sum(jnp.sum(e * e, axis=-1, keepdims=True), axis=0, keepdims=True)
        loss_ref[...] = loss_ref[...] + sq
        dy = e * (1.0 / D)
        dy_ref[...] = dy
        dg_ref[...] = dg_ref[...] + jnp.sum(dy * fn, axis=0, keepdims=True)
        t = dy * g_ref[...]
        df_ref[...] = (r * (t - fn * jnp.mean(t * fn, axis=-1, keepdims=True))).astype(BF16)

    row = pl.BlockSpec((tm, D), lambda i: (i, 0))
    vec = pl.BlockSpec((1, D), lambda i: (0, 0))
    return pl.pallas_call(
        body, grid=(S // tm,), name="head_bwd",
        in_specs=[row, row, row, vec],
        out_specs=[pl.BlockSpec((8, 128), lambda i: (0, 0)), row, row, vec],
        out_shape=[jax.ShapeDtypeStruct((8, 128), F32), jax.ShapeDtypeStruct((S, D), F32),
                   jax.ShapeDtypeStruct((S, D), BF16), jax.ShapeDtypeStruct((1, D), F32)],
        compiler_params=_params("arbitrary"),
    )(f, x2, tgt, g4)


def _ffn_bwd_act(df, gt, up, wg, wu, wd):
    tm = 512

    def body(df_ref, gt_ref, up_ref, wg_ref, wu_ref, wd_ref, dgt_ref, dup_ref, dh_ref):
        k = pl.program_id(1)
        da = _nt(df_ref[...], wd_ref[...])
        gt, up = gt_ref[...], up_ref[...]
        sg = _sigmoid(gt)
        dup = (da * gt * sg).astype(BF16)
        dgt = (da * up * (sg * (1.0 + gt * (1.0 - sg)))).astype(BF16)
        dup_ref[...] = dup
        dgt_ref[...] = dgt
        part = _nt(dgt, wg_ref[...]) + _nt(dup, wu_ref[...])

        @pl.when(k == 0)
        def _():
            dh_ref[...] = part

        @pl.when(k > 0)
        def _():
            dh_ref[...] = dh_ref[...] + part

    wcol = pl.BlockSpec((None, D, FF_C), lambda i, k: (k, 0, 0))
    act = pl.BlockSpec((None, tm, FF_C), lambda i, k: (k, i, 0))
    row = pl.BlockSpec((tm, D), lambda i, k: (i, 0))
    return pl.pallas_call(
        body, grid=(S // tm, N_CHIP), name="ffn_bwd_act",
        in_specs=[row, act, act, wcol, wcol, pl.BlockSpec((None, FF_C, D), lambda i, k: (k, 0, 0))],
        out_specs=[act, act, row],
        out_shape=[jax.ShapeDtypeStruct((N_CHIP, S, FF_C), BF16), jax.ShapeDtypeStruct((N_CHIP, S, FF_C), BF16),
                   jax.ShapeDtypeStruct((S, D), F32)],
        compiler_params=_params("parallel", "arbitrary"),
    )(df, gt, up, wg, wu, wd)


def _ffn_bwd_w(a, df, h3, dgt, dup):
    tm = 512

    def body(a_ref, df_ref, h_ref, dgt_ref, dup_ref, dwd_ref, dwg_ref, dwu_ref, acc_d, acc_g, acc_u):
        i = pl.program_id(1)

        @pl.when(i == 0)
        def _():
            acc_d[...] = jnp.zeros_like(acc_d)
            acc_g[...] = jnp.zeros_like(acc_g)
            acc_u[...] = jnp.zeros_like(acc_u)

        h = h_ref[...]
        acc_d[...] += _tn(a_ref[...], df_ref[...])
        acc_g[...] += _tn(h, dgt_ref[...])
        acc_u[...] += _tn(h, dup_ref[...])

        @pl.when(i == S // tm - 1)
        def _():
            dwd_ref[...] = acc_d[...].astype(BF16)
            dwg_ref[...] = acc_g[...].astype(BF16)
            dwu_ref[...] = acc_u[...].astype(BF16)

    act = pl.BlockSpec((None, tm, FF_C), lambda k, i: (k, i, 0))
    row = pl.BlockSpec((tm, D), lambda k, i: (i, 0))
    wcol = pl.BlockSpec((None, D, FF_C), lambda k, i: (k, 0, 0))
    return pl.pallas_call(
        body, grid=(N_CHIP, S // tm), name="ffn_bwd_w",
        in_specs=[act, row, row, act, act],
        out_specs=[pl.BlockSpec((None, FF_C, D), lambda k, i: (k, 0, 0)), wcol, wcol],
        out_shape=[jax.ShapeDtypeStruct((N_CHIP, FF_C, D), BF16), jax.ShapeDtypeStruct((N_CHIP, D, FF_C), BF16),
                   jax.ShapeDtypeStruct((N_CHIP, D, FF_C), BF16)],
        scratch_shapes=[pltpu.VMEM((FF_C, D), F32), pltpu.VMEM((D, FF_C), F32), pltpu.VMEM((D, FF_C), F32)],
        compiler_params=_params("parallel", "arbitrary"),
    )(a, df, h3, dgt, dup)


def _norm_bwd(dh3, dy, x2, mix, g2, g3):
    tm = 256

    def body(dh_ref, dy_ref, x2_ref, mix_ref, g2_ref, g3_ref, dx2_ref, dmix_ref, dg3_ref, dg2_ref):
        @pl.when(pl.program_id(0) == 0)
        def _():
            dg3_ref[...] = jnp.zeros_like(dg3_ref)
            dg2_ref[...] = jnp.zeros_like(dg2_ref)

        x2 = x2_ref[...]
        r3 = _rstd(x2)
        xn = x2 * r3
        dh = dh_ref[...]
        dg3_ref[...] = dg3_ref[...] + jnp.sum(dh * xn, axis=0, keepdims=True)
        t = dh * g3_ref[...]
        dx2 = dy_ref[...] + r3 * (t - xn * jnp.mean(t * xn, axis=-1, keepdims=True))
        dx2_ref[...] = dx2
        mix = mix_ref[...]
        r2 = _rstd(mix)
        mn = mix * r2
        dg2_ref[...] = dg2_ref[...] + jnp.sum(dx2 * mn, axis=0, keepdims=True)
        u = dx2 * g2_ref[...]
        dmix_ref[...] = (r2 * (u - mn * jnp.mean(u * mn, axis=-1, keepdims=True))).astype(BF16)

    row = pl.BlockSpec((tm, D), lambda i: (i, 0))
    vec = pl.BlockSpec((1, D), lambda i: (0, 0))
    return pl.pallas_call(
        body, grid=(S // tm,), name="norm_bwd",
        in_specs=[row, row, row, row, vec, vec], out_specs=[row, row, vec, vec],
        out_shape=[jax.ShapeDtypeStruct((S, D), F32), jax.ShapeDtypeStruct((S, D), BF16),
                   jax.ShapeDtypeStruct((1, D), F32), jax.ShapeDtypeStruct((1, D), F32)],
        compiler_params=_params("arbitrary"),
    )(dh3, dy, x2, mix, g2, g3)


def _mix_bwd(dmix, cat_r, cat_a, wout):
    tm = 512

    def body(dm_ref, cr_ref, ca_ref, w_ref, dret_ref, datt_ref, dw_ref, acc):
        i = pl.program_id(0)

        @pl.when(i == 0)
        def _():
            acc[...] = jnp.zeros_like(acc)

        dm = dm_ref[...]
        dret_ref[...] = _nt(dm, w_ref[0:512, :])
        datt = _nt(dm, w_ref[512:1024, :])
        for j in range(4):
            datt_ref[j] = datt[:, 128 * j:128 * j + 128]
        acc[0:512, :] += _tn(cr_ref[...], dm)
        acc[512:1024, :] += _tn(ca_ref[...], dm)

        @pl.when(i == S // tm - 1)
        def _():
            dw_ref[...] = acc[...].astype(BF16)

    row = lambda w: pl.BlockSpec((tm, w), lambda i: (i, 0))
    full = pl.BlockSpec((D, D), lambda i: (0, 0))
    return pl.pallas_call(
        body, grid=(S // tm,), name="mix_bwd",
        in_specs=[row(D), row(512), row(512), full],
        out_specs=[row(512), pl.BlockSpec((4, tm, 128), lambda i: (0, i, 0)), full],
        out_shape=[jax.ShapeDtypeStruct((S, 512), F32), jax.ShapeDtypeStruct((4, S, 128), F32),
                   jax.ShapeDtypeStruct((D, D), BF16)],
        scratch_shapes=[pltpu.VMEM((D, D), F32)],
        compiler_params=_params("arbitrary"),
    )(dmix, cat_r, cat_a, wout)


def _att_bwd(aq, ak, av, datt, att_out, lse):
    def body(q_ref, k_ref, v_ref, do_ref, out_ref, l_ref, dq_ref, dk_ref, dv_ref):
        def clear(i, carry):
            rows = _rows(i, 256)
            for ref in (dq_ref, dk_ref, dv_ref):
                for j in range(4):
                    ref[j, rows, :] = jnp.zeros((256, 128), F32)
            return carry

        lax.fori_loop(0, S // 256, clear, 0)
        lane_head = lax.broadcasted_iota(jnp.int32, (ATT_BLK, 256), 1) // 64
        for d in PATTERN_DILATIONS:
            nb, has_prev = _att_blocks(d)

            def block(b, carry, d=d, nb=nb, has_prev=has_prev):
                r, ib = b // nb, b % nb
                rows = _class_rows(ib, r, d)
                prow = _class_rows(jnp.maximum(ib - 1, 0), r, d)
                valid = _att_mask(ib, has_prev)
                for g in range(2):
                    qg = _slab_pair(q_ref, g, rows).astype(BF16)
                    kg = _slab_pair(k_ref, g, rows)
                    vg = _slab_pair(v_ref, g, rows)
                    if has_prev:
                        kg = jnp.concatenate([_slab_pair(k_ref, g, prow), kg], axis=0)
                        vg = jnp.concatenate([_slab_pair(v_ref, g, prow), vg], axis=0)
                    kg, vg = kg.astype(BF16), vg.astype(BF16)
                    dog = _slab_pair(do_ref, g, rows)
                    outg = _slab_pair(out_ref, g, rows)
                    lg = _slab_pair(l_ref, g, rows)
                    dq = jnp.zeros((ATT_BLK, 256), F32)
                    dk = jnp.zeros(kg.shape, F32)
                    dv = jnp.zeros(kg.shape, F32)
                    for hh in range(4):
                        mine = lane_head == hh
                        qm = jnp.where(mine, qg, jnp.zeros_like(qg))
                        dom = jnp.where(mine, dog, 0.0)
                        delta = jnp.sum(dom * outg, axis=-1, keepdims=True)
                        lh = jnp.max(jnp.where(mine, lg, NEG), axis=-1, keepdims=True)
                        s = jnp.where(valid, _nt(qm, kg) * ATT_SCALE, NEG)
                        p = jnp.exp(s - lh)
                        domb = dom.astype(BF16)
                        ds = (p * (_nt(domb, vg) - delta) * ATT_SCALE).astype(BF16)
                        dq = jnp.where(mine, _nn(ds, kg), dq)
                        dk = dk + _tn(ds, qm)
                        dv = dv + _tn(p.astype(BF16), domb)
                    for jj in range(2):
                        j, sl = 2 * g + jj, slice(128 * jj, 128 * jj + 128)
                        dq_ref[j, rows, :] += dq[:, sl]
                        if has_prev:
                            dk_ref[j, prow, :] += dk[0:ATT_BLK, sl]
                            dv_ref[j, prow, :] += dv[0:ATT_BLK, sl]
                            dk_ref[j, rows, :] += dk[ATT_BLK:2 * ATT_BLK, sl]
                            dv_ref[j, rows, :] += dv[ATT_BLK:2 * ATT_BLK, sl]
                        else:
                            dk_ref[j, rows, :] += dk[:, sl]
                            dv_ref[j, rows, :] += dv[:, sl]
                return carry

            lax.fori_loop(0, S // ATT_BLK, block, 0)

    slab = jax.ShapeDtypeStruct((4, S, 128), F32)
    return pl.pallas_call(
        body, name="att_bwd", out_shape=[slab, slab, slab],
        compiler_params=pltpu.CompilerParams(vmem_limit_bytes=VMEM_LIMIT),
    )(aq, ak, av, datt, att_out, lse)


def _ret_bwd(qr, kr, rv, proj, o_raw, states, dret, tabs):
    C = RET_C
    nc = S // C
    dtab, a_tab, b_tab, lam, bd = tabs

    def body(q_ref, k_ref, v_ref, g_ref, o_ref, st_ref, dr_ref, dt_ref, a_ref, b_ref, lam_ref, bd_ref,
             dq_ref, dk_ref, dv_ref, dg_ref, dR):
        @pl.when(pl.program_id(0) == 0)
        def _():
            dR[...] = jnp.zeros_like(dR)

        q, k, v = q_ref[...], k_ref[...], v_ref[...]
        lane_head = lax.broadcasted_iota(jnp.int32, (C, 256), 1) // 32
        col_head = lax.broadcasted_iota(jnp.int32, (C, 256), 1) // 64
        dos = []
        for j in range(4):
            sl = slice(128 * j, 128 * j + 128)
            oj = o_ref[:, sl]
            xc = oj - _seg_mean(oj)
            rs = lax.rsqrt(_seg_mean(xc * xc) + GN_EPS)
            rn = xc * rs
            gj = g_ref[:, sl]
            sg = _sigmoid(gj)
            dret = dr_ref[:, sl]
            dg_ref[:, sl] = dret * rn * (sg * (1.0 + gj * (1.0 - sg)))
            drn = dret * (gj * sg)
            dos.append(rs * (drn - _seg_mean(drn) - rn * _seg_mean(drn * rn)))
        do = [jnp.concatenate(dos[0:2], axis=1), jnp.concatenate(dos[2:4], axis=1)]
        do8 = jnp.concatenate(do, axis=1).astype(BF16)
        drb = dR[...].astype(BF16)
        rb = st_ref[...]
        dq = _nt(do8, rb) * a_ref[...]
        dk = _nt(v, drb) * b_ref[...]
        kb = (k.astype(F32) * b_ref[...]).astype(BF16)
        dvall = _nn(kb, drb)
        dv = [dvall[:, 0:256], dvall[:, 256:512]]
        for h in range(8):
            g = h // 4
            vg = v[:, 256 * g:256 * g + 256]
            mine = lane_head == h
            qm = jnp.where(mine, q, jnp.zeros_like(q))
            dom = jnp.where(col_head == (h % 4), do[g], 0.0).astype(BF16)
            dec = dt_ref[h]
            p = (_nt(qm, k) * dec).astype(BF16)
            ds = (_nt(dom, vg) * dec).astype(BF16)
            dq = jnp.where(mine, dq + _nn(ds, k), dq)
            dk = dk + _tn(ds, qm)
            dv[g] = dv[g] + _tn(p, dom)
        qa = (q.astype(F32) * a_ref[...]).astype(BF16)
        dR[...] = dR[...] * lam_ref[...] + _tn(qa, do8) * bd_ref[...]
        dq_ref[...] = dq
        dk_ref[...] = dk
        dv_ref[:, 0:256] = dv[0]
        dv_ref[:, 256:512] = dv[1]

    rev = lambda w: pl.BlockSpec((C, w), lambda n: (nc - 1 - n, 0))
    full = lambda a: pl.BlockSpec(a.shape, lambda n: (0,) * a.ndim)
    return pl.pallas_call(
        body, grid=(nc,), name="ret_bwd",
        in_specs=[rev(256), rev(256), rev(512), pl.BlockSpec((C, 512), lambda n: (nc - 1 - n, 2)), rev(512),
                  pl.BlockSpec((None, 256, 512), lambda n: (nc - 1 - n, 0, 0)), rev(512),
                  full(dtab), full(a_tab), full(b_tab), full(lam), full(bd)],
        out_specs=[rev(256), rev(256), rev(512), rev(512)],
        out_shape=[jax.ShapeDtypeStruct((S, 256), F32), jax.ShapeDtypeStruct((S, 256), F32),
                   jax.ShapeDtypeStruct((S, 512), F32), jax.ShapeDtypeStruct((S, 512), F32)],
        scratch_shapes=[pltpu.VMEM((256, 512), F32)],
        compiler_params=_params("arbitrary"),
    )(qr, kr, rv, proj, o_raw, states, dret, dtab, a_tab, b_tab, lam, bd)


def _rot_bwd(pos, ifr, ifa, dqr, dkr, drv, drg, dq_att, dk_att, dv_att):
    tm = 256

    def body(pos_ref, ifr_ref, ifa_ref, dqr_ref, dkr_ref, drv_ref, drg_ref, dqa_ref, dka_ref, dva_ref, dp_ref):
        cr, sr, lo_r, ca, sa, lo_a = _rot_coeffs(pos_ref, ifr_ref, ifa_ref, tm)

        def unrot_r(g):
            gs = g * sr
            return g * cr + pltpu.roll(jnp.where(lo_r, -gs, 0.0), 16, 1) + pltpu.roll(jnp.where(lo_r, 0.0, gs), 240, 1)

        def unrot_a(g):
            gs = g * sa
            return g * ca + pltpu.roll(jnp.where(lo_a, -gs, 0.0), 8, 1) + pltpu.roll(jnp.where(lo_a, 0.0, gs), 504, 1)

        def wide(ref):
            return jnp.concatenate([ref[j] for j in range(4)], axis=1)

        dp_ref[:, 0:256] = unrot_r(dqr_ref[...]).astype(BF16)
        dp_ref[:, 256:512] = unrot_r(dkr_ref[...] * RET_SCALE).astype(BF16)
        dp_ref[:, 512:1024] = drv_ref[...].astype(BF16)
        dp_ref[:, 1024:1536] = drg_ref[...].astype(BF16)
        dp_ref[:, 1536:2048] = unrot_a(wide(dqa_ref)).astype(BF16)
        dp_ref[:, 2048:2560] = unrot_a(wide(dka_ref)).astype(BF16)
        dp_ref[:, 2560:3072] = wide(dva_ref).astype(BF16)

    row = lambda w: pl.BlockSpec((tm, w), lambda i: (i, 0))
    const = lambda w: pl.BlockSpec((1, w), lambda i: (0, 0))
    slab = pl.BlockSpec((4, tm, 128), lambda i: (0, i, 0))
    return pl.pallas_call(
        body, grid=(S // tm,), name="rot_bwd",
        in_specs=[row(1), const(256), const(512), row(256), row(256), row(512), row(512), slab, slab, slab],
        out_specs=row(PW), out_shape=jax.ShapeDtypeStruct((S, PW), BF16),
        compiler_params=_params("parallel"),
    )(pos, ifr, ifa, dqr, dkr, drv, drg, dq_att, dk_att, dv_att)


def _win_bwd_w(h1, dproj):
    tm = 512

    def body(h_ref, dp_ref, dw_ref, acc):
        i = pl.program_id(1)

        @pl.when(i == 0)
        def _():
            acc[...] = jnp.zeros_like(acc)

        acc[...] += _tn(h_ref[...], dp_ref[...])

        @pl.when(i == S // tm - 1)
        def _():
            dw_ref[...] = acc[...].astype(BF16)

    return pl.pallas_call(
        body, grid=(N_CHIP, S // tm), name="win_bwd_w",
        in_specs=[pl.BlockSpec((tm, D), lambda k, i: (i, 0)), pl.BlockSpec((tm, WIN_C), lambda k, i: (i, k))],
        out_specs=pl.BlockSpec((None, D, WIN_C), lambda k, i: (k, 0, 0)),
        out_shape=jax.ShapeDtypeStruct((N_CHIP, D, WIN_C), BF16),
        scratch_shapes=[pltpu.VMEM((D, WIN_C), F32)],
        compiler_params=_params("parallel", "arbitrary"),
    )(h1, dproj)


def _in_bwd(dproj, win_g, x, dx2, g1):
    tm = 256

    def body(dp_ref, w_ref, x_ref, dx2_ref, g_ref, dx_ref, dg_ref):
        @pl.when(pl.program_id(0) == 0)
        def _():
            dg_ref[...] = jnp.zeros_like(dg_ref)

        dh = _nt(dp_ref[:, 0:WIN_C], w_ref[0])
        for k in range(1, N_CHIP):
            dh = dh + _nt(dp_ref[:, k * WIN_C:(k + 1) * WIN_C], w_ref[k])
        xv = x_ref[...]
        r = _rstd(xv)
        xn = xv * r
        dg_ref[...] = dg_ref[...] + jnp.sum(dh * xn, axis=0, keepdims=True)
        t = dh * g_ref[...]
        dx_ref[...] = dx2_ref[...] + r * (t - xn * jnp.mean(t * xn, axis=-1, keepdims=True))

    row = lambda w: pl.BlockSpec((tm, w), lambda i: (i, 0))
    vec = pl.BlockSpec((1, D), lambda i: (0, 0))
    return pl.pallas_call(
        body, grid=(S // tm,), name="in_bwd",
        in_specs=[row(PW), pl.BlockSpec((N_CHIP, D, WIN_C), lambda i: (0, 0, 0)), row(D), row(D), vec],
        out_specs=[row(D), vec],
        out_shape=[jax.ShapeDtypeStruct((S, D), F32), jax.ShapeDtypeStruct((1, D), F32)],
        compiler_params=_params("arbitrary"),
    )(dproj, win_g, x, dx2, g1)


def _local_step(x, pos, tgt, g1, g2, g3, g4, win_g, wout_g, wg_g, wu_g, wd_g):
    tabs = tuple(jnp.asarray(t) for t in _retention_tables())
    ifr, ifa = (jnp.asarray(t) for t in _rotary_tables())

    proj, h1 = _proj_fwd(x, g1, win_g)
    qr, kr, rv, aq, ak, av = _rot_fwd(proj, pos, ifr, ifa)
    o_raw, cat_r, states = _ret_fwd(qr, kr, rv, proj, tabs)
    att_out, lse, cat_a = _att_fwd(aq, ak, av)
    mix, x2, h3 = _mix_fwd(cat_r, cat_a, wout_g, x, g2, g3)
    gt, up, a, f = _ffn_fwd(h3, wg_g, wu_g, wd_g)

    sq, dy, df, dg4 = _head_bwd(f, x2, tgt, g4)
    dgt, dup, dh3 = _ffn_bwd_act(df, gt, up, wg_g, wu_g, wd_g)
    dwd, dwg, dwu = _ffn_bwd_w(a, df, h3, dgt, dup)
    dx2, dmix, dg3, dg2 = _norm_bwd(dh3, dy, x2, mix, g2, g3)
    dret, datt, dwout = _mix_bwd(dmix, cat_r, cat_a, wout_g)
    dq_att, dk_att, dv_att = _att_bwd(aq, ak, av, datt, att_out, lse)
    dqr, dkr, drv, drg = _ret_bwd(qr, kr, rv, proj, o_raw, states, dret, tabs)
    dproj = _rot_bwd(pos, ifr, ifa, dqr, dkr, drv, drg, dq_att, dk_att, dv_att)
    dwin = _win_bwd_w(h1, dproj)
    dx, dg1 = _in_bwd(dproj, win_g, x, dx2, g1)
    return sq[0, 0], dx, (dwin, dwout, dwg, dwu, dwd), (dg1, dg2, dg3, dg4)


ANY = pl.BlockSpec(memory_space=pl.ANY)
FLIPS = ((1, 0), (0, 1), (1, 1))


def _place():
    x, y, c = lax.axis_index("x"), lax.axis_index("y"), lax.axis_index("c")
    chips = [((1 - x) if fx else x, (1 - y) if fy else y) for fx, fy in FLIPS]
    return x, y, c, 2 * x + y, chips


def _remote(src, dst, send_sem, recv_sem, device):
    return pltpu.make_async_remote_copy(src_ref=src, dst_ref=dst, send_sem=send_sem, recv_sem=recv_sem,
                                        device_id=device, device_id_type=MESH)


def _gather_weights(shards):
    n = len(shards)

    def body(*refs):
        ins, outs = refs[:n], refs[n:2 * n]
        send_sems, recv_sems, local_sems = refs[2 * n:]
        x, y, c, me, chips = _place()
        sibling = (x, y, 1 - c)
        local, sends = [], []
        for a in range(n):
            half = ins[a].shape[0] // 2
            for j, (cx, cy) in enumerate(chips):
                cp = _remote(ins[a].at[pl.ds(c * half, half), :], outs[a].at[me, pl.ds(c * half, half), :],
                             send_sems.at[6 * a + j], recv_sems.at[6 * a + j], (cx, cy, c))
                cp.start()
                sends.append(cp)
        for a in range(n):
            own = pltpu.make_async_copy(ins[a], outs[a].at[me], local_sems.at[a])
            own.start()
            local.append(own)
        for a in range(n):
            half = ins[a].shape[0] // 2
            for j, (cx, cy) in enumerate(chips):
                blk = outs[a].at[2 * cx + cy, pl.ds(c * half, half), :]
                _remote(blk, blk, send_sems.at[6 * a + j], recv_sems.at[6 * a + j], sibling).wait_recv()
                fwd = _remote(blk, blk, send_sems.at[6 * a + 3 + j], recv_sems.at[6 * a + 3 + j], sibling)
                fwd.start()
                sends.append(fwd)
        for a in range(n):
            half = ins[a].shape[0] // 2
            for j, (cx, cy) in enumerate(chips):
                blk = outs[a].at[2 * cx + cy, pl.ds((1 - c) * half, half), :]
                _remote(blk, blk, send_sems.at[6 * a + 3 + j], recv_sems.at[6 * a + 3 + j], sibling).wait_recv()
        for cp in sends:
            cp.wait_send()
        for cp in local:
            cp.wait()

    return pl.pallas_call(
        body, name="gather_weights",
        in_specs=[ANY] * n, out_specs=[ANY] * n,
        out_shape=[jax.ShapeDtypeStruct((N_CHIP,) + s.shape, s.dtype) for s in shards],
        scratch_shapes=[pltpu.SemaphoreType.DMA((6 * n,)), pltpu.SemaphoreType.DMA((6 * n,)),
                        pltpu.SemaphoreType.DMA((n,))],
    )(*shards)


def _reduce_to_sibling(grads, gvec):
    n = len(grads)

    def body(*refs):
        ins, gv_ref = refs[:n], refs[n]
        outs, gall_ref = refs[n + 1:2 * n + 1], refs[2 * n + 1]
        send_sems, recv_sems, gsend, grecv, local_sem = refs[2 * n + 2:]
        x, y, c, me, chips = _place()
        sibling = (x, y, 1 - c)
        dev = 2 * me + c
        own = pltpu.make_async_copy(gv_ref, gall_ref.at[dev], local_sem)
        own.start()
        sends = []
        for a in range(n):
            half = ins[a].shape[1] // 2
            cp = _remote(ins[a].at[:, pl.ds((1 - c) * half, half), :], outs[a],
                         send_sems.at[a], recv_sems.at[a], sibling)
            cp.start()
            sends.append(cp)
        peers = []
        for f in range(1, 8):
            fx, fy, fc = (f >> 2) & 1, (f >> 1) & 1, f & 1
            px, py, pc = (1 - x) if fx else x, (1 - y) if fy else y, (1 - c) if fc else c
            peers.append((px, py, pc))
            cp = _remote(gv_ref, gall_ref.at[dev], gsend.at[f - 1], grecv.at[f - 1], (px, py, pc))
            cp.start()
            sends.append(cp)
        for a in range(n):
            _remote(outs[a], outs[a], send_sems.at[a], recv_sems.at[a], sibling).wait_recv()
        for f, (px, py, pc) in enumerate(peers):
            blk = gall_ref.at[4 * px + 2 * py + pc]
            _remote(blk, blk, gsend.at[f], grecv.at[f], sibling).wait_recv()
        for cp in sends:
            cp.wait_send()
        own.wait()

    return pl.pallas_call(
        body, name="reduce_to_sibling",
        in_specs=[ANY] * (n + 1), out_specs=[ANY] * (n + 1),
        out_shape=[jax.ShapeDtypeStruct((N_CHIP, g.shape[1] // 2, g.shape[2]), g.dtype) for g in grads]
                  + [jax.ShapeDtypeStruct((8,) + gvec.shape, gvec.dtype)],
        scratch_shapes=[pltpu.SemaphoreType.DMA((n,)), pltpu.SemaphoreType.DMA((n,)),
                        pltpu.SemaphoreType.DMA((7,)), pltpu.SemaphoreType.DMA((7,)), pltpu.SemaphoreType.DMA],
    )(*grads, gvec)


def _core_index():
    return lax.axis_index("c").astype(jnp.int32).reshape(1)


def _pair_sum(g, got):
    _, r, cc = g.shape
    half = r // 2
    tr = half // 2

    def body(c_ref, g_ref, got_ref, out_ref):
        out_ref[...] = (g_ref[...].astype(F32) + got_ref[...].astype(F32)).astype(BF16)

    return pl.pallas_call(
        body, name=f"pair_sum_{r}x{cc}",
        grid_spec=pltpu.PrefetchScalarGridSpec(
            num_scalar_prefetch=1, grid=(N_CHIP, 2),
            in_specs=[pl.BlockSpec((None, tr, cc), lambda k, i, c_ref: (k, 2 * c_ref[0] + i, 0)),
                      pl.BlockSpec((None, tr, cc), lambda k, i, c_ref: (k, i, 0))],
            out_specs=pl.BlockSpec((None, tr, cc), lambda k, i, c_ref: (k, i, 0))),
        out_shape=jax.ShapeDtypeStruct((N_CHIP, half, cc), BF16),
        compiler_params=_params("parallel", "parallel"),
    )(_core_index(), g, got)


def _reduce_over_chips(pre):
    n = len(pre)

    def body(*refs):
        ins, outs = refs[:n], refs[n:2 * n]
        send_sems, recv_sems, local_sems = refs[2 * n:]
        x, y, c, me, chips = _place()
        local, sends = [], []
        for a in range(n):
            for j, (cx, cy) in enumerate(chips):
                cp = _remote(ins[a].at[2 * cx + cy], outs[a].at[me],
                             send_sems.at[3 * a + j], recv_sems.at[3 * a + j], (cx, cy, c))
                cp.start()
                sends.append(cp)
        for a in range(n):
            own = pltpu.make_async_copy(ins[a].at[me], outs[a].at[me], local_sems.at[a])
            own.start()
            local.append(own)
        for a in range(n):
            for j, (cx, cy) in enumerate(chips):
                blk = outs[a].at[2 * cx + cy]
                _remote(blk, blk, send_sems.at[3 * a + j], recv_sems.at[3 * a + j], (cx, cy, c)).wait_recv()
        for cp in sends:
            cp.wait_send()
        for cp in local:
            cp.wait()

    return pl.pallas_call(
        body, name="reduce_over_chips",
        in_specs=[ANY] * n, out_specs=[ANY] * n,
        out_shape=[jax.ShapeDtypeStruct(p.shape, p.dtype) for p in pre],
        scratch_shapes=[pltpu.SemaphoreType.DMA((3 * n,)), pltpu.SemaphoreType.DMA((3 * n,)),
                        pltpu.SemaphoreType.DMA((n,))],
    )(*pre)


def _chip_sum(parts):
    _, half, cc = parts.shape
    tr = half // 2

    def body(c_ref, p_ref, out_ref):
        out_ref[...] = ((p_ref[0].astype(F32) + p_ref[1].astype(F32)) + p_ref[2].astype(F32)) + p_ref[3].astype(F32)

    return pl.pallas_call(
        body, name=f"chip_sum_{half}x{cc}",
        grid_spec=pltpu.PrefetchScalarGridSpec(
            num_scalar_prefetch=1, grid=(2,),
            in_specs=[pl.BlockSpec((N_CHIP, tr, cc), lambda i, c_ref: (0, i, 0))],
            out_specs=pl.BlockSpec((tr, cc), lambda i, c_ref: (2 * c_ref[0] + i, 0))),
        out_shape=jax.ShapeDtypeStruct((2 * half, cc), F32),
        compiler_params=_params("parallel"),
    )(_core_index(), parts)


def _share_with_sibling(fulls):
    n = len(fulls)

    def body(*refs):
        outs = refs[n:2 * n]
        send_sems, recv_sems = refs[2 * n:]
        x, y, c, me, chips = _place()
        sibling = (x, y, 1 - c)
        sends = []
        for a in range(n):
            half = outs[a].shape[0] // 2
            mine = outs[a].at[pl.ds(c * half, half), :]
            cp = _remote(mine, mine, send_sems.at[a], recv_sems.at[a], sibling)
            cp.start()
            sends.append(cp)
        for a in range(n):
            half = outs[a].shape[0] // 2
            theirs = outs[a].at[pl.ds((1 - c) * half, half), :]
            _remote(theirs, theirs, send_sems.at[a], recv_sems.at[a], sibling).wait_recv()
        for cp in sends:
            cp.wait_send()

    return pl.pallas_call(
        body, name="share_with_sibling",
        in_specs=[ANY] * n, out_specs=[ANY] * n,
        out_shape=[jax.ShapeDtypeStruct(f.shape, f.dtype) for f in fulls],
        input_output_aliases={a: a for a in range(n)},
        scratch_shapes=[pltpu.SemaphoreType.DMA((n,)), pltpu.SemaphoreType.DMA((n,))],
    )(*fulls)


def _adamw_math(w, g, m, v):
    m = ADAM_B1 * m + (1.0 - ADAM_B1) * g
    v = ADAM_B2 * v + (1.0 - ADAM_B2) * (g * g)
    m_hat = m / (1.0 - ADAM_B1 ** ADAM_STEP)
    v_hat = v / (1.0 - ADAM_B2 ** ADAM_STEP)
    delta = -ADAM_LR * (m_hat / (jnp.sqrt(v_hat) + ADAM_EPS) + ADAM_WD * w)
    return delta, m, v


def _adamw(w, g, m, v):
    r, cc = w.shape
    tr = r // 4

    def body(w_ref, g_ref, m_ref, v_ref, go_ref, d_ref, nm_ref, nv_ref):
        g = g_ref[...]
        go_ref[...] = g
        d_ref[...], nm_ref[...], nv_ref[...] = _adamw_math(w_ref[...], g, m_ref[...], v_ref[...])

    blk = pl.BlockSpec((tr, cc), lambda i: (i, 0))
    return pl.pallas_call(
        body, grid=(4,), name=f"adamw_{r}x{cc}",
        in_specs=[blk] * 4, out_specs=[blk] * 4,
        out_shape=[jax.ShapeDtypeStruct((r, cc), F32)] * 4,
        compiler_params=_params("parallel"),
    )(w, g, m, v)


def _pack8(rows):
    def body(*refs):
        out_ref = refs[-1]
        out_ref[...] = jnp.zeros_like(out_ref)
        for i, r in enumerate(refs[:-1]):
            out_ref[i:i + 1, :] = r[...]

    return pl.pallas_call(body, name="pack8", out_shape=jax.ShapeDtypeStruct((8, D), F32))(*rows)


def _adamw_gains(gall, w8, m8, v8):
    def body(ga_ref, w_ref, m_ref, v_ref, g_ref, d_ref, nm_ref, nv_ref):
        g = ga_ref[0]
        for dev in range(1, 8):
            g = g + ga_ref[dev]
        g_ref[...] = g
        d_ref[...], nm_ref[...], nv_ref[...] = _adamw_math(w_ref[...], g, m_ref[...], v_ref[...])

    return pl.pallas_call(
        body, name="adamw_gains",
        out_shape=[jax.ShapeDtypeStruct((8, D), F32)] * 4,
    )(gall, w8, m8, v8)


def kernel(x, positions, w_in, w_out, g_pre_mix, g_post_mix, g_pre_ffn, g_post_ffn, w_gate, w_up, w_down, loss_target, m_w_in, m_w_out, m_g_pre_mix, m_g_post_mix, m_g_pre_ffn, m_g_post_ffn, m_w_gate, m_w_up, m_w_down, v_w_in, v_w_out, v_g_pre_mix, v_g_post_mix, v_g_pre_ffn, v_g_post_ffn, v_w_gate, v_w_up, v_w_down):
    shards = [w_in[0], w_out[0], w_gate[0], w_up[0], w_down[0]]
    moms = [m_w_in[0], m_w_out[0], m_w_gate[0], m_w_up[0], m_w_down[0]]
    vels = [v_w_in[0], v_w_out[0], v_w_gate[0], v_w_up[0], v_w_down[0]]
    win_g, wout_g, wg_g, wu_g, wd_g = _gather_weights([s.astype(BF16) for s in shards])

    sq, dx, (dwin, dwout, dwg, dwu, dwd), gain_grads = _local_step(
        x[0], positions.reshape(S, 1), loss_target[0], g_pre_mix, g_post_mix, g_pre_ffn, g_post_ffn,
        win_g, wout_g.reshape(D, D), wg_g, wu_g, wd_g)
    loss = 0.5 * lax.psum(sq, ("x", "y", "c")) / D

    grads = [dwin, dwout.reshape(N_CHIP, WOUT_R, D), dwg, dwu, dwd]
    *got, gall = _reduce_to_sibling(grads, _pack8(gain_grads))
    pre = [_pair_sum(g, r) for g, r in zip(grads, got)]
    parts = _reduce_over_chips(pre)
    full = _share_with_sibling([_chip_sum(p) for p in parts])

    upd = [_adamw(w, g, m, v) for w, g, m, v in zip(shards, full, moms, vels)]
    gg, gd, gm, gv = _adamw_gains(gall, _pack8([g_pre_mix, g_post_mix, g_pre_ffn, g_post_ffn]),
                                  _pack8([m_g_pre_mix, m_g_post_mix, m_g_pre_ffn, m_g_post_ffn]),
                                  _pack8([v_g_pre_mix, v_g_post_mix, v_g_pre_ffn, v_g_post_ffn]))

    def order(mats, vecs):
        return [mats[0][None], mats[1][None]] + [vecs[i:i + 1] for i in range(4)] + [t[None] for t in mats[2:]]

    return (loss, dx[None],
            *order([u[0] for u in upd], gg),
            *order([u[1] for u in upd], gd),
            *order([u[2] for u in upd], gm),
            *order([u[3] for u in upd], gv))
```

```python
import functools

import numpy as np
import jax
import jax.numpy as jnp
from jax import lax
from jax.experimental import pallas as pl
from jax.experimental.pallas import tpu as pltpu

F32, BF16 = jnp.float32, jnp.bfloat16
MESH = pl.DeviceIdType.MESH

S = 2048
D = 1024
PW = 3072
N_CHIP = 4
WIN_C = PW // N_CHIP
DFF = 2816
FF_C = DFF // N_CHIP
WOUT_R = D // N_CHIP
RMS_EPS = 1e-6
GN_EPS = 1e-5
RET_C = 128
RET_SCALE = 32 ** -0.5
ATT_BLK = 128
ATT_SCALE = 64 ** -0.5
PATTERN_DILATIONS = (1, 4, 16)
NEG = -1e30
VMEM_LIMIT = 56 * 1024 * 1024

ADAM_LR, ADAM_B1, ADAM_B2, ADAM_EPS, ADAM_WD, ADAM_STEP = 0.001, 0.9, 0.999, 1e-08, 0.01, 10


def _params(*sem):
    return pltpu.CompilerParams(dimension_semantics=sem, vmem_limit_bytes=VMEM_LIMIT)


def _nt(a, b):
    return lax.dot_general(a, b, (((1,), (1,)), ((), ())), preferred_element_type=F32)


def _tn(a, b):
    return lax.dot_general(a, b, (((0,), (0,)), ((), ())), preferred_element_type=F32)


def _nn(a, b):
    return jnp.dot(a, b, preferred_element_type=F32)


def _rstd(v):
    return lax.rsqrt(jnp.mean(v * v, axis=-1, keepdims=True) + RMS_EPS)


def _sigmoid(v):
    return 1.0 / (1.0 + jnp.exp(-v))


def _rows(i, t):
    return pl.ds(pl.multiple_of(i * t, t), t)


def _retention_tables():
    h = np.arange(8, dtype=np.float32)
    log_g = np.log1p(-np.exp2(-5.0 - h)).astype(np.float32)
    idx = np.arange(RET_C, dtype=np.float32)
    diff = idx[:, None] - idx[None, :]
    dtab = np.where(diff >= 0, np.exp(log_g[:, None, None] * np.maximum(diff, 0.0)), 0.0).astype(np.float32)
    lane_head = np.arange(256) // 32
    a_tab = np.exp(log_g[lane_head][None, :] * (idx + 1.0)[:, None]).astype(np.float32)
    b_tab = np.exp(log_g[lane_head][None, :] * (RET_C - 1.0 - idx)[:, None]).astype(np.float32)
    lam = np.exp(log_g[lane_head] * RET_C).astype(np.float32)[:, None]
    bd = (lane_head[:, None] == (np.arange(512) // 64)[None, :]).astype(np.float32)
    return dtab, a_tab, b_tab, lam, bd


def _rotary_tables():
    inv_r = (1.0 / (np.float32(10000.0) ** np.linspace(0.0, 1.0, 16, dtype=np.float32))).astype(np.float32)
    inv_a = (np.float32(500000.0) ** (-np.arange(0, 16, 2, dtype=np.float32) / np.float32(16))).astype(np.float32)
    dr = np.arange(256) % 32
    ifr = inv_r[dr % 16][None, :].astype(np.float32)
    da = np.arange(512) % 64
    ifa = np.where(da < 16, inv_a[da % 8], 0.0)[None, :].astype(np.float32)
    return ifr, ifa


def _proj_fwd(x, g1, win_g):
    tm = 256

    def body(x_ref, g_ref, w_ref, proj_ref, h_ref):
        xv = x_ref[...]
        h = (xv * _rstd(xv) * g_ref[...]).astype(BF16)
        h_ref[...] = h
        for k in range(N_CHIP):
            proj_ref[:, k * WIN_C:(k + 1) * WIN_C] = _nn(h, w_ref[k])

    return pl.pallas_call(
        body, grid=(S // tm,), name="proj_fwd",
        in_specs=[pl.BlockSpec((tm, D), lambda i: (i, 0)), pl.BlockSpec((1, D), lambda i: (0, 0)),
                  pl.BlockSpec((N_CHIP, D, WIN_C), lambda i: (0, 0, 0))],
        out_specs=[pl.BlockSpec((tm, PW), lambda i: (i, 0)), pl.BlockSpec((tm, D), lambda i: (i, 0))],
        out_shape=[jax.ShapeDtypeStruct((S, PW), F32), jax.ShapeDtypeStruct((S, D), BF16)],
        compiler_params=_params("parallel"),
    )(x, g1, win_g)


def _rot_coeffs(pos_ref, ifr_ref, ifa_ref, tm):
    pos = pos_ref[...].astype(F32)
    ang_r = pos * ifr_ref[...]
    ang_a = pos * ifa_ref[...]
    lo_r = (lax.broadcasted_iota(jnp.int32, (tm, 256), 1) % 32) < 16
    lo_a = (lax.broadcasted_iota(jnp.int32, (tm, 512), 1) % 64) < 8
    return jnp.cos(ang_r), jnp.sin(ang_r), lo_r, jnp.cos(ang_a), jnp.sin(ang_a), lo_a


def _rot_fwd(proj, pos, ifr, ifa):
    tm = 256

    def body(p_ref, pos_ref, ifr_ref, ifa_ref, qr_ref, kr_ref, rv_ref, aq_ref, ak_ref, av_ref):
        cr, sr, lo_r, ca, sa, lo_a = _rot_coeffs(pos_ref, ifr_ref, ifa_ref, tm)

        def rot_r(v):
            return v * cr + sr * jnp.where(lo_r, -pltpu.roll(v, 240, 1), pltpu.roll(v, 16, 1))

        def rot_a(v):
            return v * ca + sa * jnp.where(lo_a, -pltpu.roll(v, 504, 1), pltpu.roll(v, 8, 1))

        qr_ref[...] = rot_r(p_ref[:, 0:256]).astype(BF16)
        kr_ref[...] = (rot_r(p_ref[:, 256:512]) * RET_SCALE).astype(BF16)
        rv_ref[...] = p_ref[:, 512:1024].astype(BF16)
        aq, ak = rot_a(p_ref[:, 1536:2048]), rot_a(p_ref[:, 2048:2560])
        for j in range(4):
            aq_ref[j] = aq[:, 128 * j:128 * j + 128]
            ak_ref[j] = ak[:, 128 * j:128 * j + 128]
            av_ref[j] = p_ref[:, 2560 + 128 * j:2560 + 128 * j + 128]

    row = lambda w: pl.BlockSpec((tm, w), lambda i: (i, 0))
    const = lambda w: pl.BlockSpec((1, w), lambda i: (0, 0))
    slab = pl.BlockSpec((4, tm, 128), lambda i: (0, i, 0))
    return pl.pallas_call(
        body, grid=(S // tm,), name="rot_fwd",
        in_specs=[row(PW), row(1), const(256), const(512)],
        out_specs=[row(256), row(256), row(512), slab, slab, slab],
        out_shape=[jax.ShapeDtypeStruct((S, w), BF16) for w in (256, 256, 512)]
                  + [jax.ShapeDtypeStruct((4, S, 128), F32)] * 3,
        compiler_params=_params("parallel"),
    )(proj, pos, ifr, ifa)


def _seg_mean(v):
    lo = lax.broadcasted_iota(jnp.int32, v.shape, 1) < 64
    s_lo = jnp.sum(jnp.where(lo, v, 0.0), axis=-1, keepdims=True)
    s_hi = jnp.sum(jnp.where(lo, 0.0, v), axis=-1, keepdims=True)
    return jnp.where(lo, s_lo, s_hi) * (1.0 / 64.0)


def _ret_fwd(qr, kr, rv, proj, tabs):
    C = RET_C
    dtab, a_tab, b_tab, lam, bd = tabs

    def body(q_ref, k_ref, v_ref, g_ref, dt_ref, a_ref, b_ref, lam_ref, bd_ref, o_ref, cat_ref, st_ref, R):
        @pl.when(pl.program_id(0) == 0)
        def _():
            R[...] = jnp.zeros_like(R)

        q, k, v = q_ref[...], k_ref[...], v_ref[...]
        lane_head = lax.broadcasted_iota(jnp.int32, (C, 256), 1) // 32
        col_head = lax.broadcasted_iota(jnp.int32, (C, 256), 1) // 64
        rb = R[...].astype(BF16)
        st_ref[...] = rb
        qa = (q.astype(F32) * a_ref[...]).astype(BF16)
        cross = _nn(qa, rb)
        og = [cross[:, 0:256], cross[:, 256:512]]
        for h in range(8):
            g = h // 4
            qm = jnp.where(lane_head == h, q, jnp.zeros_like(q))
            p = (_nt(qm, k) * dt_ref[h]).astype(BF16)
            pv = _nn(p, v[:, 256 * g:256 * g + 256])
            og[g] = og[g] + jnp.where(col_head == (h % 4), pv, 0.0)
        kb = (k.astype(F32) * b_ref[...]).astype(BF16)
        R[...] = R[...] * lam_ref[...] + _tn(kb, v) * bd_ref[...]
        o_ref[:, 0:256] = og[0]
        o_ref[:, 256:512] = og[1]
        for j in range(4):
            oj = og[j // 2][:, 128 * (j % 2):128 * (j % 2) + 128]
            xc = oj - _seg_mean(oj)
            rn = xc * lax.rsqrt(_seg_mean(xc * xc) + GN_EPS)
            gj = g_ref[:, 128 * j:128 * j + 128]
            cat_ref[:, 128 * j:128 * j + 128] = (rn * (gj * _sigmoid(gj))).astype(BF16)

    row = lambda w: pl.BlockSpec((C, w), lambda n: (n, 0))
    full = lambda a: pl.BlockSpec(a.shape, lambda n: (0,) * a.ndim)
    return pl.pallas_call(
        body, grid=(S // C,), name="ret_fwd",
        in_specs=[row(256), row(256), row(512), pl.BlockSpec((C, 512), lambda n: (n, 2)),
                  full(dtab), full(a_tab), full(b_tab), full(lam), full(bd)],
        out_specs=[row(512), row(512), pl.BlockSpec((None, 256, 512), lambda n: (n, 0, 0))],
        out_shape=[jax.ShapeDtypeStruct((S, 512), F32), jax.ShapeDtypeStruct((S, 512), BF16),
                   jax.ShapeDtypeStruct((S // C, 256, 512), BF16)],
        scratch_shapes=[pltpu.VMEM((256, 512), F32)],
        compiler_params=_params("arbitrary"),
    )(qr, kr, rv, proj, dtab, a_tab, b_tab, lam, bd)


def _att_mask(ib, has_prev):
    nk = 2 * ATT_BLK if has_prev else ATT_BLK
    a = lax.broadcasted_iota(jnp.int32, (ATT_BLK, nk), 0)
    kk = lax.broadcasted_iota(jnp.int32, (ATT_BLK, nk), 1)
    if has_prev:
        dist = ATT_BLK + a - kk
        return (dist >= 0) & (dist <= ATT_BLK) & ((ib * ATT_BLK - ATT_BLK + kk) >= 0)
    return (a - kk) >= 0


def _class_rows(ib, r, d):
    if d == 1:
        return pl.ds(pl.multiple_of(ib * ATT_BLK, ATT_BLK), ATT_BLK)
    return pl.ds(ib * ATT_BLK * d + r, ATT_BLK, stride=d)


def _slab_pair(ref, g, rows):
    return jnp.concatenate([ref[2 * g, rows, :], ref[2 * g + 1, rows, :]], axis=1)


def _att_blocks(d):
    nb = S // d // ATT_BLK
    return nb, nb > 1


def _att_fwd(aq, ak, av, exchange, exchange_args):
    def body(q_ref, k_ref, v_ref, o_ref, l_ref, cat_ref, xc):
        xc.start()
        lane_head = lax.broadcasted_iota(jnp.int32, (ATT_BLK, 256), 1) // 64
        for pi, d in enumerate(PATTERN_DILATIONS):
            if pi == len(PATTERN_DILATIONS) - 1:
                xc.middle()
            nb, has_prev = _att_blocks(d)

            def block(b, carry, pi=pi, d=d, nb=nb, has_prev=has_prev):
                r, ib = b // nb, b % nb
                rows = _class_rows(ib, r, d)
                prow = _class_rows(jnp.maximum(ib - 1, 0), r, d)
                valid = _att_mask(ib, has_prev)
                for g in range(2):
                    qg = _slab_pair(q_ref, g, rows).astype(BF16)
                    kg = _slab_pair(k_ref, g, rows)
                    vg = _slab_pair(v_ref, g, rows)
                    if has_prev:
                        kg = jnp.concatenate([_slab_pair(k_ref, g, prow), kg], axis=0)
                        vg = jnp.concatenate([_slab_pair(v_ref, g, prow), vg], axis=0)
                    kg, vg = kg.astype(BF16), vg.astype(BF16)
                    og = jnp.zeros((ATT_BLK, 256), F32)
                    lg = jnp.zeros((ATT_BLK, 256), F32)
                    for hh in range(4):
                        qm = jnp.where(lane_head == hh, qg, jnp.zeros_like(qg))
                        s = jnp.where(valid, _nt(qm, kg) * ATT_SCALE, NEG)
                        m = jnp.max(s, axis=-1, keepdims=True)
                        p = jnp.exp(s - m)
                        den = jnp.sum(p, axis=-1, keepdims=True)
                        o = _nn(p.astype(BF16), vg) / den
                        og = jnp.where(lane_head == hh, o, og)
                        lg = jnp.where(lane_head == hh, m + jnp.log(den), lg)
                    for jj in range(2):
                        j = 2 * g + jj
                        o_new, l_new = og[:, 128 * jj:128 * jj + 128], lg[:, 128 * jj:128 * jj + 128]
                        if pi > 0:
                            o_old, l_old = o_ref[j, rows, :], l_ref[j, rows, :]
                            mx = jnp.maximum(l_old, l_new)
                            ea, eb = jnp.exp(l_old - mx), jnp.exp(l_new - mx)
                            den = ea + eb
                            o_new = (ea * o_old + eb * o_new) / den
                            l_new = mx + jnp.log(den)
                        o_ref[j, rows, :] = o_new
                        l_ref[j, rows, :] = l_new
                return carry

            lax.fori_loop(0, S // ATT_BLK, block, 0)

        def to_cat(i, carry):
            rows = _rows(i, 256)
            for j in range(4):
                cat_ref[rows, 128 * j:128 * j + 128] = o_ref[j, rows, :].astype(BF16)
            return carry

        lax.fori_loop(0, S // 256, to_cat, 0)
        xc.finish()

    slab = jax.ShapeDtypeStruct((4, S, 128), F32)
    return _carry("att_fwd", body, exchange, exchange_args, (aq, ak, av), [VMEM] * 3, [VMEM] * 3,
                  [slab, slab, jax.ShapeDtypeStruct((S, 512), BF16)])


def _mix_fwd(cat_r, cat_a, wout, x, g2, g3):
    tm = 256

    def body(cr_ref, ca_ref, w_ref, x_ref, g2_ref, g3_ref, mix_ref, x2_ref, h3_ref):
        mix = _nn(cr_ref[...], w_ref[0:512, :]) + _nn(ca_ref[...], w_ref[512:1024, :])
        mix_ref[...] = mix
        x2 = x_ref[...] + mix * _rstd(mix) * g2_ref[...]
        x2_ref[...] = x2
        h3_ref[...] = (x2 * _rstd(x2) * g3_ref[...]).astype(BF16)

    row = lambda w: pl.BlockSpec((tm, w), lambda i: (i, 0))
    vec = pl.BlockSpec((1, D), lambda i: (0, 0))
    return pl.pallas_call(
        body, grid=(S // tm,), name="mix_fwd",
        in_specs=[row(512), row(512), pl.BlockSpec((D, D), lambda i: (0, 0)), row(D), vec, vec],
        out_specs=[row(D), row(D), row(D)],
        out_shape=[jax.ShapeDtypeStruct((S, D), F32), jax.ShapeDtypeStruct((S, D), F32),
                   jax.ShapeDtypeStruct((S, D), BF16)],
        compiler_params=_params("parallel"),
    )(cat_r, cat_a, wout, x, g2, g3)


def _ffn_fwd(h3, wg, wu, wd):
    tm = 256

    def body(h_ref, wg_ref, wu_ref, wd_ref, gt_ref, up_ref, a_ref, f_ref):
        k, i = pl.program_id(0), pl.program_id(1)
        h = h_ref[...]
        gt = _nt(h, wg_ref[...])
        up = _nt(h, wu_ref[...])
        gt_ref[...] = gt
        up_ref[...] = up
        a = (gt * _sigmoid(gt) * up).astype(BF16)
        a_ref[...] = a
        part = _nn(a, wd_ref[...])
        rows = _rows(i, tm)

        @pl.when(k == 0)
        def _():
            f_ref[rows, :] = part

        @pl.when(k > 0)
        def _():
            f_ref[rows, :] = f_ref[rows, :] + part

    wrow = pl.BlockSpec((None, FF_C, D), lambda k, i: (k, 0, 0))
    act = pl.BlockSpec((None, tm, FF_C), lambda k, i: (k, i, 0))
    return pl.pallas_call(
        body, grid=(N_CHIP, S // tm), name="ffn_fwd",
        in_specs=[pl.BlockSpec((tm, D), lambda k, i: (i, 0)), wrow, wrow, wrow],
        out_specs=[act, act, act, pl.BlockSpec((S, D), lambda k, i: (0, 0))],
        out_shape=[jax.ShapeDtypeStruct((N_CHIP, S, FF_C), F32), jax.ShapeDtypeStruct((N_CHIP, S, FF_C), F32),
                   jax.ShapeDtypeStruct((N_CHIP, S, FF_C), BF16), jax.ShapeDtypeStruct((S, D), F32)],
        compiler_params=_params("arbitrary", "arbitrary"),
    )(h3, wg, wu, wd)


def _head_bwd(f, x2, tgt, g4):
    tm = 256

    def body(f_ref, x2_ref, t_ref, g_ref, loss_ref, dy_ref, df_ref, dg_ref):
        @pl.when(pl.program_id(0) == 0)
        def _():
            loss_ref[...] = jnp.zeros_like(loss_ref)
            dg_ref[...] = jnp.zeros_like(dg_ref)

        fv = f_ref[...]
        r = _rstd(fv)
        fn = fv * r
        e = x2_ref[...] + fn * g_ref[...] - t_ref[...]
        sq = jnp.sum(jnp.sum(e * e, axis=-1, keepdims=True), axis=0, keepdims=True)
        loss_ref[...] = loss_ref[...] + sq
        dy = e * (1.0 / D)
        dy_ref[...] = dy
        dg_ref[...] = dg_ref[...] + jnp.sum(dy * fn, axis=0, keepdims=True)
        t = dy * g_ref[...]
        df_ref[...] = (r * (t - fn * jnp.mean(t * fn, axis=-1, keepdims=True))).astype(BF16)

    row = pl.BlockSpec((tm, D), lambda i: (i, 0))
    vec = pl.BlockSpec((1, D), lambda i: (0, 0))
    return pl.pallas_call(
        body, grid=(S // tm,), name="head_bwd",
        in_specs=[row, row, row, vec],
        out_specs=[pl.BlockSpec((8, 128), lambda i: (0, 0)), row, row, vec],
        out_shape=[jax.ShapeDtypeStruct((8, 128), F32), jax.ShapeDtypeStruct((S, D), F32),
                   jax.ShapeDtypeStruct((S, D), BF16), jax.ShapeDtypeStruct((1, D), F32)],
        compiler_params=_params("arbitrary"),
    )(f, x2, tgt, g4)


def _ffn_bwd_act(df, gt, up, wg, wu, wd):
    tm = 512

    def body(df_ref, gt_ref, up_ref, wg_ref, wu_ref, wd_ref, dgt_ref, dup_ref, dh_ref):
        k = pl.program_id(1)
        da = _nt(df_ref[...], wd_ref[...])
        gt, up = gt_ref[...], up_ref[...]
        sg = _sigmoid(gt)
        dup = (da * gt * sg).astype(BF16)
        dgt = (da * up * (sg * (1.0 + gt * (1.0 - sg)))).astype(BF16)
        dup_ref[...] = dup
        dgt_ref[...] = dgt
        part = _nn(dgt, wg_ref[...]) + _nn(dup, wu_ref[...])

        @pl.when(k == 0)
        def _():
            dh_ref[...] = part

        @pl.when(k > 0)
        def _():
            dh_ref[...] = dh_ref[...] + part

    wrow = pl.BlockSpec((None, FF_C, D), lambda i, k: (k, 0, 0))
    act = pl.BlockSpec((None, tm, FF_C), lambda i, k: (k, i, 0))
    row = pl.BlockSpec((tm, D), lambda i, k: (i, 0))
    return pl.pallas_call(
        body, grid=(S // tm, N_CHIP), name="ffn_bwd_act",
        in_specs=[row, act, act, wrow, wrow, wrow],
        out_specs=[act, act, row],
        out_shape=[jax.ShapeDtypeStruct((N_CHIP, S, FF_C), BF16), jax.ShapeDtypeStruct((N_CHIP, S, FF_C), BF16),
                   jax.ShapeDtypeStruct((S, D), F32)],
        compiler_params=_params("parallel", "arbitrary"),
    )(df, gt, up, wg, wu, wd)


def _ffn_bwd_w(a, df, h3, dgt, dup):
    tm = 512

    def body(a_ref, df_ref, h_ref, dgt_ref, dup_ref, dwd_ref, dwg_ref, dwu_ref, acc_d, acc_g, acc_u):
        i = pl.program_id(1)

        @pl.when(i == 0)
        def _():
            acc_d[...] = jnp.zeros_like(acc_d)
            acc_g[...] = jnp.zeros_like(acc_g)
            acc_u[...] = jnp.zeros_like(acc_u)

        h = h_ref[...]
        acc_d[...] += _tn(a_ref[...], df_ref[...])
        acc_g[...] += _tn(dgt_ref[...], h)
        acc_u[...] += _tn(dup_ref[...], h)

        @pl.when(i == S // tm - 1)
        def _():
            dwd_ref[...] = acc_d[...].astype(BF16)
            dwg_ref[...] = acc_g[...].astype(BF16)
            dwu_ref[...] = acc_u[...].astype(BF16)

    act = pl.BlockSpec((None, tm, FF_C), lambda k, i: (k, i, 0))
    row = pl.BlockSpec((tm, D), lambda k, i: (i, 0))
    wrow = pl.BlockSpec((None, FF_C, D), lambda k, i: (k, 0, 0))
    return pl.pallas_call(
        body, grid=(N_CHIP, S // tm), name="ffn_bwd_w",
        in_specs=[act, row, row, act, act],
        out_specs=[wrow, wrow, wrow],
        out_shape=[jax.ShapeDtypeStruct((N_CHIP, FF_C, D), BF16)] * 3,
        scratch_shapes=[pltpu.VMEM((FF_C, D), F32)] * 3,
        compiler_params=_params("parallel", "arbitrary"),
    )(a, df, h3, dgt, dup)


def _norm_bwd(dh3, dy, x2, mix, g2, g3, exchange, exchange_args):
    tm = 256

    def body(dh_ref, dy_ref, x2_ref, mix_ref, g2_ref, g3_ref, dx2_ref, dmix_ref, dg3_ref, dg2_ref, xc):
        @pl.when(pl.program_id(0) == 0)
        def _():
            xc.start()
            dg3_ref[...] = jnp.zeros_like(dg3_ref)
            dg2_ref[...] = jnp.zeros_like(dg2_ref)

        x2 = x2_ref[...]
        r3 = _rstd(x2)
        xn = x2 * r3
        dh = dh_ref[...]
        dg3_ref[...] = dg3_ref[...] + jnp.sum(dh * xn, axis=0, keepdims=True)
        t = dh * g3_ref[...]
        dx2 = dy_ref[...] + r3 * (t - xn * jnp.mean(t * xn, axis=-1, keepdims=True))
        dx2_ref[...] = dx2
        mix = mix_ref[...]
        r2 = _rstd(mix)
        mn = mix * r2
        dg2_ref[...] = dg2_ref[...] + jnp.sum(dx2 * mn, axis=0, keepdims=True)
        u = dx2 * g2_ref[...]
        dmix_ref[...] = (r2 * (u - mn * jnp.mean(u * mn, axis=-1, keepdims=True))).astype(BF16)

        @pl.when(pl.program_id(0) == S // tm - 1)
        def _():
            xc.middle()
            xc.finish()

    row = pl.BlockSpec((tm, D), lambda i: (i, 0))
    vec = pl.BlockSpec((1, D), lambda i: (0, 0))
    return _carry("norm_bwd", body, exchange, exchange_args, (dh3, dy, x2, mix, g2, g3),
                  [row, row, row, row, vec, vec], [row, row, vec, vec],
                  [jax.ShapeDtypeStruct((S, D), F32), jax.ShapeDtypeStruct((S, D), BF16),
                   jax.ShapeDtypeStruct((1, D), F32), jax.ShapeDtypeStruct((1, D), F32)],
                  grid=(S // tm,), semantics=("arbitrary",))


def _mix_bwd(dmix, cat_r, cat_a, wout):
    tm = 512

    def body(dm_ref, cr_ref, ca_ref, w_ref, dret_ref, datt_ref, dw_ref, acc):
        i = pl.program_id(0)

        @pl.when(i == 0)
        def _():
            acc[...] = jnp.zeros_like(acc)

        dm = dm_ref[...]
        dret_ref[...] = _nt(dm, w_ref[0:512, :])
        datt = _nt(dm, w_ref[512:1024, :])
        for j in range(4):
            datt_ref[j] = datt[:, 128 * j:128 * j + 128]
        acc[0:512, :] += _tn(cr_ref[...], dm)
        acc[512:1024, :] += _tn(ca_ref[...], dm)

        @pl.when(i == S // tm - 1)
        def _():
            dw_ref[...] = acc[...].astype(BF16)

    row = lambda w: pl.BlockSpec((tm, w), lambda i: (i, 0))
    full = pl.BlockSpec((D, D), lambda i: (0, 0))
    return pl.pallas_call(
        body, grid=(S // tm,), name="mix_bwd",
        in_specs=[row(D), row(512), row(512), full],
        out_specs=[row(512), pl.BlockSpec((4, tm, 128), lambda i: (0, i, 0)), full],
        out_shape=[jax.ShapeDtypeStruct((S, 512), F32), jax.ShapeDtypeStruct((4, S, 128), F32),
                   jax.ShapeDtypeStruct((D, D), BF16)],
        scratch_shapes=[pltpu.VMEM((D, D), F32)],
        compiler_params=_params("arbitrary"),
    )(dmix, cat_r, cat_a, wout)


def _att_bwd(aq, ak, av, datt, att_out, lse, exchange, exchange_args):
    def body(q_ref, k_ref, v_ref, do_ref, out_ref, l_ref, dq_ref, dk_ref, dv_ref, xc):
        xc.start()

        def clear(i, carry):
            rows = _rows(i, 256)
            for ref in (dq_ref, dk_ref, dv_ref):
                for j in range(4):
                    ref[j, rows, :] = jnp.zeros((256, 128), F32)
            return carry

        lax.fori_loop(0, S // 256, clear, 0)
        lane_head = lax.broadcasted_iota(jnp.int32, (ATT_BLK, 256), 1) // 64
        for d in PATTERN_DILATIONS:
            nb, has_prev = _att_blocks(d)

            def block(b, carry, d=d, nb=nb, has_prev=has_prev):
                r, ib = b // nb, b % nb
                rows = _class_rows(ib, r, d)
                prow = _class_rows(jnp.maximum(ib - 1, 0), r, d)
                valid = _att_mask(ib, has_prev)
                for g in range(2):
                    qg = _slab_pair(q_ref, g, rows).astype(BF16)
                    kg = _slab_pair(k_ref, g, rows)
                    vg = _slab_pair(v_ref, g, rows)
                    if has_prev:
                        kg = jnp.concatenate([_slab_pair(k_ref, g, prow), kg], axis=0)
                        vg = jnp.concatenate([_slab_pair(v_ref, g, prow), vg], axis=0)
                    kg, vg = kg.astype(BF16), vg.astype(BF16)
                    dog = _slab_pair(do_ref, g, rows)
                    outg = _slab_pair(out_ref, g, rows)
                    lg = _slab_pair(l_ref, g, rows)
                    dq = jnp.zeros((ATT_BLK, 256), F32)
                    dk = jnp.zeros(kg.shape, F32)
                    dv = jnp.zeros(kg.shape, F32)
                    for hh in range(4):
                        mine = lane_head == hh
                        qm = jnp.where(mine, qg, jnp.zeros_like(qg))
                        dom = jnp.where(mine, dog, 0.0)
                        delta = jnp.sum(dom * outg, axis=-1, keepdims=True)
                        lh = jnp.max(jnp.where(mine, lg, NEG), axis=-1, keepdims=True)
                        s = jnp.where(valid, _nt(qm, kg) * ATT_SCALE, NEG)
                        p = jnp.exp(s - lh)
                        domb = dom.astype(BF16)
                        ds = (p * (_nt(domb, vg) - delta) * ATT_SCALE).astype(BF16)
                        dq = jnp.where(mine, _nn(ds, kg), dq)
                        dk = dk + _tn(ds, qm)
                        dv = dv + _tn(p.astype(BF16), domb)
                    for jj in range(2):
                        j, sl = 2 * g + jj, slice(128 * jj, 128 * jj + 128)
                        dq_ref[j, rows, :] += dq[:, sl]
                        if has_prev:
                            dk_ref[j, prow, :] += dk[0:ATT_BLK, sl]
                            dv_ref[j, prow, :] += dv[0:ATT_BLK, sl]
                            dk_ref[j, rows, :] += dk[ATT_BLK:2 * ATT_BLK, sl]
                            dv_ref[j, rows, :] += dv[ATT_BLK:2 * ATT_BLK, sl]
                        else:
                            dk_ref[j, rows, :] += dk[:, sl]
                            dv_ref[j, rows, :] += dv[:, sl]
                return carry

            lax.fori_loop(0, S // ATT_BLK, block, 0)
        xc.middle()
        xc.finish()

    slab = jax.ShapeDtypeStruct((4, S, 128), F32)
    return _carry("att_bwd", body, exchange, exchange_args, (aq, ak, av, datt, att_out, lse), [VMEM] * 6, [VMEM] * 3,
                  [slab, slab, slab])


def _ret_bwd(qr, kr, rv, proj, o_raw, states, dret, tabs, exchange, exchange_args):
    C = RET_C
    nc = S // C
    dtab, a_tab, b_tab, lam, bd = tabs

    def body(q_ref, k_ref, v_ref, g_ref, o_ref, st_ref, dr_ref, dt_ref, a_ref, b_ref, lam_ref, bd_ref,
             dq_ref, dk_ref, dv_ref, dg_ref, dR, exch):
        @pl.when(pl.program_id(0) == 0)
        def _():
            exch.start()
            dR[...] = jnp.zeros_like(dR)

        q, k, v = q_ref[...], k_ref[...], v_ref[...]
        lane_head = lax.broadcasted_iota(jnp.int32, (C, 256), 1) // 32
        col_head = lax.broadcasted_iota(jnp.int32, (C, 256), 1) // 64
        dos = []
        for j in range(4):
            sl = slice(128 * j, 128 * j + 128)
            oj = o_ref[:, sl]
            xc = oj - _seg_mean(oj)
            rs = lax.rsqrt(_seg_mean(xc * xc) + GN_EPS)
            rn = xc * rs
            gj = g_ref[:, sl]
            sg = _sigmoid(gj)
            dret = dr_ref[:, sl]
            dg_ref[:, sl] = dret * rn * (sg * (1.0 + gj * (1.0 - sg)))
            drn = dret * (gj * sg)
            dos.append(rs * (drn - _seg_mean(drn) - rn * _seg_mean(drn * rn)))
        do = [jnp.concatenate(dos[0:2], axis=1), jnp.concatenate(dos[2:4], axis=1)]
        do8 = jnp.concatenate(do, axis=1).astype(BF16)
        drb = dR[...].astype(BF16)
        rb = st_ref[...]
        dq = _nt(do8, rb) * a_ref[...]
        dk = _nt(v, drb) * b_ref[...]
        kb = (k.astype(F32) * b_ref[...]).astype(BF16)
        dvall = _nn(kb, drb)
        dv = [dvall[:, 0:256], dvall[:, 256:512]]
        for h in range(8):
            g = h // 4
            vg = v[:, 256 * g:256 * g + 256]
            mine = lane_head == h
            qm = jnp.where(mine, q, jnp.zeros_like(q))
            dom = jnp.where(col_head == (h % 4), do[g], 0.0).astype(BF16)
            dec = dt_ref[h]
            p = (_nt(qm, k) * dec).astype(BF16)
            ds = (_nt(dom, vg) * dec).astype(BF16)
            dq = jnp.where(mine, dq + _nn(ds, k), dq)
            dk = dk + _tn(ds, qm)
            dv[g] = dv[g] + _tn(p, dom)
        qa = (q.astype(F32) * a_ref[...]).astype(BF16)
        dR[...] = dR[...] * lam_ref[...] + _tn(qa, do8) * bd_ref[...]
        dq_ref[...] = dq
        dk_ref[...] = dk
        dv_ref[:, 0:256] = dv[0]
        dv_ref[:, 256:512] = dv[1]

        @pl.when(pl.program_id(0) == nc - 1)
        def _():
            exch.middle()
            exch.finish()

    rev = lambda w: pl.BlockSpec((C, w), lambda n: (nc - 1 - n, 0))
    full = lambda a: pl.BlockSpec(a.shape, lambda n: (0,) * a.ndim)
    return _carry(
        "ret_bwd", body, exchange, exchange_args, (qr, kr, rv, proj, o_raw, states, dret, dtab, a_tab, b_tab, lam, bd),
        [rev(256), rev(256), rev(512), pl.BlockSpec((C, 512), lambda n: (nc - 1 - n, 2)), rev(512),
         pl.BlockSpec((None, 256, 512), lambda n: (nc - 1 - n, 0, 0)), rev(512),
         full(dtab), full(a_tab), full(b_tab), full(lam), full(bd)],
        [rev(256), rev(256), rev(512), rev(512)],
        [jax.ShapeDtypeStruct((S, 256), F32), jax.ShapeDtypeStruct((S, 256), F32),
         jax.ShapeDtypeStruct((S, 512), F32), jax.ShapeDtypeStruct((S, 512), F32)],
        scratch_shapes=[pltpu.VMEM((256, 512), F32)], grid=(nc,), semantics=("arbitrary",))


def _rot_bwd(pos, ifr, ifa, dqr, dkr, drv, drg, dq_att, dk_att, dv_att):
    tm = 256

    def body(pos_ref, ifr_ref, ifa_ref, dqr_ref, dkr_ref, drv_ref, drg_ref, dqa_ref, dka_ref, dva_ref, dp_ref):
        cr, sr, lo_r, ca, sa, lo_a = _rot_coeffs(pos_ref, ifr_ref, ifa_ref, tm)

        def unrot_r(g):
            gs = g * sr
            return g * cr + pltpu.roll(jnp.where(lo_r, -gs, 0.0), 16, 1) + pltpu.roll(jnp.where(lo_r, 0.0, gs), 240, 1)

        def unrot_a(g):
            gs = g * sa
            return g * ca + pltpu.roll(jnp.where(lo_a, -gs, 0.0), 8, 1) + pltpu.roll(jnp.where(lo_a, 0.0, gs), 504, 1)

        def wide(ref):
            return jnp.concatenate([ref[j] for j in range(4)], axis=1)

        dp_ref[:, 0:256] = unrot_r(dqr_ref[...]).astype(BF16)
        dp_ref[:, 256:512] = unrot_r(dkr_ref[...] * RET_SCALE).astype(BF16)
        dp_ref[:, 512:1024] = drv_ref[...].astype(BF16)
        dp_ref[:, 1024:1536] = drg_ref[...].astype(BF16)
        dp_ref[:, 1536:2048] = unrot_a(wide(dqa_ref)).astype(BF16)
        dp_ref[:, 2048:2560] = unrot_a(wide(dka_ref)).astype(BF16)
        dp_ref[:, 2560:3072] = wide(dva_ref).astype(BF16)

    row = lambda w: pl.BlockSpec((tm, w), lambda i: (i, 0))
    const = lambda w: pl.BlockSpec((1, w), lambda i: (0, 0))
    slab = pl.BlockSpec((4, tm, 128), lambda i: (0, i, 0))
    return pl.pallas_call(
        body, grid=(S // tm,), name="rot_bwd",
        in_specs=[row(1), const(256), const(512), row(256), row(256), row(512), row(512), slab, slab, slab],
        out_specs=row(PW), out_shape=jax.ShapeDtypeStruct((S, PW), BF16),
        compiler_params=_params("parallel"),
    )(pos, ifr, ifa, dqr, dkr, drv, drg, dq_att, dk_att, dv_att)


def _win_bwd_w(h1, dproj):
    tm = 512

    def body(h_ref, dp_ref, dw_ref, acc):
        i = pl.program_id(1)

        @pl.when(i == 0)
        def _():
            acc[...] = jnp.zeros_like(acc)

        acc[...] += _tn(h_ref[...], dp_ref[...])

        @pl.when(i == S // tm - 1)
        def _():
            dw_ref[...] = acc[...].astype(BF16)

    return pl.pallas_call(
        body, grid=(N_CHIP, S // tm), name="win_bwd_w",
        in_specs=[pl.BlockSpec((tm, D), lambda k, i: (i, 0)), pl.BlockSpec((tm, WIN_C), lambda k, i: (i, k))],
        out_specs=pl.BlockSpec((None, D, WIN_C), lambda k, i: (k, 0, 0)),
        out_shape=jax.ShapeDtypeStruct((N_CHIP, D, WIN_C), BF16),
        scratch_shapes=[pltpu.VMEM((D, WIN_C), F32)],
        compiler_params=_params("parallel", "arbitrary"),
    )(h1, dproj)


def _in_bwd(dproj, win_g, x, dx2, g1, exchange, exchange_args):
    tm = 256

    def body(dp_ref, w_ref, x_ref, dx2_ref, g_ref, dx_ref, dg_ref, xc):
        @pl.when(pl.program_id(0) == 0)
        def _():
            xc.start()
            dg_ref[...] = jnp.zeros_like(dg_ref)

        dh = _nt(dp_ref[:, 0:WIN_C], w_ref[0])
        for k in range(1, N_CHIP):
            dh = dh + _nt(dp_ref[:, k * WIN_C:(k + 1) * WIN_C], w_ref[k])
        xv = x_ref[...]
        r = _rstd(xv)
        xn = xv * r
        dg_ref[...] = dg_ref[...] + jnp.sum(dh * xn, axis=0, keepdims=True)
        t = dh * g_ref[...]
        dx_ref[...] = dx2_ref[...] + r * (t - xn * jnp.mean(t * xn, axis=-1, keepdims=True))

        @pl.when(pl.program_id(0) == S // tm - 1)
        def _():
            xc.middle()
            xc.finish()

    row = lambda w: pl.BlockSpec((tm, w), lambda i: (i, 0))
    vec = pl.BlockSpec((1, D), lambda i: (0, 0))
    return _carry("in_bwd", body, exchange, exchange_args, (dproj, win_g, x, dx2, g1),
                  [row(PW), pl.BlockSpec((N_CHIP, D, WIN_C), lambda i: (0, 0, 0)), row(D), row(D), vec],
                  [row(D), vec], [jax.ShapeDtypeStruct((S, D), F32), jax.ShapeDtypeStruct((1, D), F32)],
                  grid=(S // tm,), semantics=("arbitrary",))


ANY = pl.BlockSpec(memory_space=pl.ANY)
VMEM = pl.BlockSpec(memory_space=pltpu.VMEM)
FLIPS = ((1, 0), (0, 1), (1, 1))


def _place():
    x, y, c = lax.axis_index("x"), lax.axis_index("y"), lax.axis_index("c")
    chips = [((1 - x) if fx else x, (1 - y) if fy else y) for fx, fy in FLIPS]
    return x, y, c, 2 * x + y, chips


def _remote(src, dst, send_sem, recv_sem, device):
    return pltpu.make_async_remote_copy(src_ref=src, dst_ref=dst, send_sem=send_sem, recv_sem=recv_sem,
                                        device_id=device, device_id_type=MESH)


class _Exchange:
    aliases = {}

    def middle(self, ins, outs, sems):
        pass


class _GatherShards(_Exchange):
    def __init__(self, shards):
        n = self.n = len(shards)
        self.n_in = self.n_out = n
        self.out_shape = [jax.ShapeDtypeStruct((N_CHIP,) + s.shape, s.dtype) for s in shards]
        dma = pltpu.SemaphoreType.DMA
        self.scratch = [dma((3 * n,)), dma((3 * n,)), dma((3 * n,)), dma((3 * n,)), dma((n,))]

    def _ici(self, ins, outs, sems, a, j, chip):
        x, y, c, me, chips = _place()
        half = ins[a].shape[0] // 2
        return _remote(ins[a].at[pl.ds(c * half, half), :], outs[a].at[me, pl.ds(c * half, half), :],
                       sems[0].at[3 * a + j], sems[1].at[3 * a + j], (*chip, c))

    def _fwd(self, outs, sems, a, j, chip, half_of):
        x, y, c, me, chips = _place()
        half = outs[a].shape[1] // 2
        blk = outs[a].at[2 * chip[0] + chip[1], pl.ds(half_of * half, half), :]
        return _remote(blk, blk, sems[2].at[3 * a + j], sems[3].at[3 * a + j], (x, y, 1 - c))

    def _local(self, ins, outs, sems, a):
        return pltpu.make_async_copy(ins[a], outs[a].at[_place()[3]], sems[4].at[a])

    def start(self, ins, outs, sems):
        chips = _place()[4]
        for a in range(self.n):
            for j, chip in enumerate(chips):
                self._ici(ins, outs, sems, a, j, chip).start()
        for a in range(self.n):
            self._local(ins, outs, sems, a).start()

    def middle(self, ins, outs, sems):
        x, y, c, me, chips = _place()
        for a in range(self.n):
            for j, chip in enumerate(chips):
                half = outs[a].shape[1] // 2
                blk = outs[a].at[2 * chip[0] + chip[1], pl.ds(c * half, half), :]
                _remote(blk, blk, sems[0].at[3 * a + j], sems[1].at[3 * a + j], (x, y, c)).wait_recv()
                self._fwd(outs, sems, a, j, chip, c).start()

    def finish(self, ins, outs, sems):
        x, y, c, me, chips = _place()
        for a in range(self.n):
            for j, chip in enumerate(chips):
                self._fwd(outs, sems, a, j, chip, 1 - c).wait_recv()
        for a in range(self.n):
            for j, chip in enumerate(chips):
                self._ici(ins, outs, sems, a, j, chip).wait_send()
                self._fwd(outs, sems, a, j, chip, c).wait_send()
            self._local(ins, outs, sems, a).wait()


class _HalvesToSibling(_Exchange):
    def __init__(self, grads):
        n = self.n = len(grads)
        self.n_in = self.n_out = n
        self.out_shape = [jax.ShapeDtypeStruct((N_CHIP, g.shape[1] // 2, g.shape[2]), g.dtype) for g in grads]
        self.scratch = [pltpu.SemaphoreType.DMA((n,)), pltpu.SemaphoreType.DMA((n,))]

    def _copy(self, ins, outs, sems, a):
        x, y, c, me, chips = _place()
        half = ins[a].shape[1] // 2
        return _remote(ins[a].at[:, pl.ds((1 - c) * half, half), :], outs[a], sems[0].at[a], sems[1].at[a], (x, y, 1 - c))

    def start(self, ins, outs, sems):
        for a in range(self.n):
            self._copy(ins, outs, sems, a).start()

    def finish(self, ins, outs, sems):
        for a in range(self.n):
            self._copy(ins, outs, sems, a).wait_recv()
        for a in range(self.n):
            self._copy(ins, outs, sems, a).wait_send()


class _OverChips(_Exchange):
    def __init__(self, pre):
        n = self.n = len(pre)
        self.n_in = self.n_out = n
        self.out_shape = [jax.ShapeDtypeStruct(p.shape, p.dtype) for p in pre]
        dma = pltpu.SemaphoreType.DMA
        self.scratch = [dma((3 * n,)), dma((3 * n,)), dma((n,))]

    def _ici(self, ins, outs, sems, a, j, chip):
        x, y, c, me, chips = _place()
        return _remote(ins[a].at[2 * chip[0] + chip[1]], outs[a].at[me], sems[0].at[3 * a + j], sems[1].at[3 * a + j],
                       (*chip, c))

    def _local(self, ins, outs, sems, a):
        me = _place()[3]
        return pltpu.make_async_copy(ins[a].at[me], outs[a].at[me], sems[2].at[a])

    def start(self, ins, outs, sems):
        chips = _place()[4]
        for a in range(self.n):
            for j, chip in enumerate(chips):
                self._ici(ins, outs, sems, a, j, chip).start()
        for a in range(self.n):
            self._local(ins, outs, sems, a).start()

    def finish(self, ins, outs, sems):
        x, y, c, me, chips = _place()
        for a in range(self.n):
            for j, chip in enumerate(chips):
                blk = outs[a].at[2 * chip[0] + chip[1]]
                _remote(blk, blk, sems[0].at[3 * a + j], sems[1].at[3 * a + j], (x, y, c)).wait_recv()
        for a in range(self.n):
            for j, chip in enumerate(chips):
                self._ici(ins, outs, sems, a, j, chip).wait_send()
            self._local(ins, outs, sems, a).wait()


class _ShareHalves(_Exchange):
    def __init__(self, fulls):
        n = self.n = len(fulls)
        self.n_in = self.n_out = n
        self.out_shape = [jax.ShapeDtypeStruct(f.shape, f.dtype) for f in fulls]
        self.scratch = [pltpu.SemaphoreType.DMA((n,)), pltpu.SemaphoreType.DMA((n,))]
        self.aliases = {a: a for a in range(n)}

    def _copy(self, outs, sems, a, half_of):
        x, y, c, me, chips = _place()
        half = outs[a].shape[0] // 2
        rows = outs[a].at[pl.ds(half_of * half, half), :]
        return _remote(rows, rows, sems[0].at[a], sems[1].at[a], (x, y, 1 - c))

    def start(self, ins, outs, sems):
        c = _place()[2]
        for a in range(self.n):
            self._copy(outs, sems, a, c).start()

    def finish(self, ins, outs, sems):
        c = _place()[2]
        for a in range(self.n):
            self._copy(outs, sems, a, 1 - c).wait_recv()
        for a in range(self.n):
            self._copy(outs, sems, a, c).wait_send()


class _GatherBlocks(_Exchange):
    def __init__(self, block):
        self.n_in = self.n_out = 1
        self.out_shape = [jax.ShapeDtypeStruct((8,) + block.shape, block.dtype)]
        dma = pltpu.SemaphoreType.DMA
        self.scratch = [dma((7,)), dma((7,)), dma]

    @staticmethod
    def _peer(f):
        x, y, c, me, chips = _place()
        return ((1 - x) if f & 4 else x, (1 - y) if f & 2 else y, (1 - c) if f & 1 else c)

    def start(self, ins, outs, sems):
        x, y, c, me, chips = _place()
        for f in range(1, 8):
            _remote(ins[0], outs[0].at[2 * me + c], sems[0].at[f - 1], sems[1].at[f - 1], self._peer(f)).start()
        pltpu.make_async_copy(ins[0], outs[0].at[2 * me + c], sems[2]).start()

    def finish(self, ins, outs, sems):
        x, y, c, me, chips = _place()
        for f in range(1, 8):
            px, py, pc = self._peer(f)
            blk = outs[0].at[4 * px + 2 * py + pc]
            _remote(blk, blk, sems[0].at[f - 1], sems[1].at[f - 1], (x, y, c)).wait_recv()
        for f in range(1, 8):
            _remote(ins[0], outs[0].at[2 * me + c], sems[0].at[f - 1], sems[1].at[f - 1], self._peer(f)).wait_send()
        pltpu.make_async_copy(ins[0], outs[0].at[2 * me + c], sems[2]).wait()


class _Both(_Exchange):
    def __init__(self, first, second):
        self.parts = (first, second)
        self.n_in, self.n_out = first.n_in + second.n_in, first.n_out + second.n_out
        self.out_shape = first.out_shape + second.out_shape
        self.scratch = first.scratch + second.scratch
        self.aliases = dict(first.aliases)
        self.aliases.update({first.n_in + i: first.n_out + o for i, o in second.aliases.items()})

    def _split(self, ins, outs, sems):
        a, b = self.parts
        return ((a, ins[:a.n_in], outs[:a.n_out], sems[:len(a.scratch)]),
                (b, ins[a.n_in:], outs[a.n_out:], sems[len(a.scratch):]))

    def start(self, ins, outs, sems):
        for ex, i, o, s in self._split(ins, outs, sems):
            ex.start(i, o, s)

    def middle(self, ins, outs, sems):
        for ex, i, o, s in self._split(ins, outs, sems):
            ex.middle(i, o, s)

    def finish(self, ins, outs, sems):
        for ex, i, o, s in self._split(ins, outs, sems):
            ex.finish(i, o, s)


class _Bound:
    def __init__(self, ex, ins, outs, sems):
        self.start = lambda: ex.start(ins, outs, sems)
        self.middle = lambda: ex.middle(ins, outs, sems)
        self.finish = lambda: ex.finish(ins, outs, sems)


def _carry(name, body, ex, ex_args, args, in_specs, out_specs, out_shape, scratch_shapes=(), grid=None, semantics=()):
    n_a, n_o, n_s = len(args), len(out_shape), len(scratch_shapes)

    def full_body(*refs):
        p = 0
        groups = []
        for size in (n_a, ex.n_in, n_o, ex.n_out, n_s, len(ex.scratch)):
            groups.append(refs[p:p + size])
            p += size
        a, ei, o, eo, s, es = groups
        body(*a, *o, *s, _Bound(ex, ei, eo, es))

    kwargs = {} if grid is None else {"grid": grid}
    outs = pl.pallas_call(
        full_body, name=name,
        in_specs=list(in_specs) + [ANY] * ex.n_in, out_specs=list(out_specs) + [ANY] * ex.n_out,
        out_shape=list(out_shape) + list(ex.out_shape), scratch_shapes=list(scratch_shapes) + list(ex.scratch),
        input_output_aliases={n_a + i: n_o + o for i, o in ex.aliases.items()},
        compiler_params=_params(*semantics) if semantics else pltpu.CompilerParams(vmem_limit_bytes=VMEM_LIMIT),
        **kwargs,
    )(*args, *ex_args)
    return outs[:n_o], outs[n_o:]


def _exchange_alone(name, ex, ex_args):
    def body(xc):
        xc.start()
        xc.middle()
        xc.finish()

    return _carry(name, body, ex, ex_args, (), (), (), ())[1]


def _core_index():
    return lax.axis_index("c").astype(jnp.int32).reshape(1)


def _pair_sum(g, got):
    _, r, cc = g.shape
    half = r // 2
    tr = half // 2

    def body(c_ref, g_ref, got_ref, out_ref):
        out_ref[...] = (g_ref[...].astype(F32) + got_ref[...].astype(F32)).astype(BF16)

    return pl.pallas_call(
        body, name=f"pair_sum_{r}x{cc}",
        grid_spec=pltpu.PrefetchScalarGridSpec(
            num_scalar_prefetch=1, grid=(N_CHIP, 2),
            in_specs=[pl.BlockSpec((None, tr, cc), lambda k, i, c_ref: (k, 2 * c_ref[0] + i, 0)),
                      pl.BlockSpec((None, tr, cc), lambda k, i, c_ref: (k, i, 0))],
            out_specs=pl.BlockSpec((None, tr, cc), lambda k, i, c_ref: (k, i, 0))),
        out_shape=jax.ShapeDtypeStruct((N_CHIP, half, cc), BF16),
        compiler_params=_params("parallel", "parallel"),
    )(_core_index(), g, got)


def _chip_sum(parts):
    _, half, cc = parts.shape
    tr = half // 2

    def body(c_ref, p_ref, out_ref):
        out_ref[...] = ((p_ref[0].astype(F32) + p_ref[1].astype(F32)) + p_ref[2].astype(F32)) + p_ref[3].astype(F32)

    return pl.pallas_call(
        body, name=f"chip_sum_{half}x{cc}",
        grid_spec=pltpu.PrefetchScalarGridSpec(
            num_scalar_prefetch=1, grid=(2,),
            in_specs=[pl.BlockSpec((N_CHIP, tr, cc), lambda i, c_ref: (0, i, 0))],
            out_specs=pl.BlockSpec((tr, cc), lambda i, c_ref: (2 * c_ref[0] + i, 0))),
        out_shape=jax.ShapeDtypeStruct((2 * half, cc), F32),
        compiler_params=_params("parallel"),
    )(_core_index(), parts)


def _adamw_math(w, g, m, v):
    m = ADAM_B1 * m + (1.0 - ADAM_B1) * g
    v = ADAM_B2 * v + (1.0 - ADAM_B2) * (g * g)
    m_hat = m / (1.0 - ADAM_B1 ** ADAM_STEP)
    v_hat = v / (1.0 - ADAM_B2 ** ADAM_STEP)
    delta = -ADAM_LR * (m_hat / (jnp.sqrt(v_hat) + ADAM_EPS) + ADAM_WD * w)
    return delta, m, v


def _adamw(w, g, m, v):
    r, cc = w.shape
    tr = r // 4

    def body(w_ref, g_ref, m_ref, v_ref, go_ref, d_ref, nm_ref, nv_ref):
        g = g_ref[...]
        go_ref[...] = g
        d_ref[...], nm_ref[...], nv_ref[...] = _adamw_math(w_ref[...], g, m_ref[...], v_ref[...])

    blk = pl.BlockSpec((tr, cc), lambda i: (i, 0))
    return pl.pallas_call(
        body, grid=(4,), name=f"adamw_{r}x{cc}",
        in_specs=[blk] * 4, out_specs=[blk] * 4,
        out_shape=[jax.ShapeDtypeStruct((r, cc), F32)] * 4,
        compiler_params=_params("parallel"),
    )(w, g, m, v)


def _pack8(rows):
    def body(*refs):
        out_ref = refs[-1]
        out_ref[...] = jnp.zeros_like(out_ref)
        for i, r in enumerate(refs[:-1]):
            out_ref[i:i + 1, :] = r[...]

    return pl.pallas_call(body, name="pack8", out_shape=jax.ShapeDtypeStruct((8, D), F32))(*rows)


def _adamw_gains(gall, w8, m8, v8):
    def body(ga_ref, w_ref, m_ref, v_ref, g_ref, d_ref, nm_ref, nv_ref):
        g = ga_ref[0]
        for dev in range(1, 8):
            g = g + ga_ref[dev]
        g_ref[...] = g
        d_ref[...], nm_ref[...], nv_ref[...] = _adamw_math(w_ref[...], g, m_ref[...], v_ref[...])

    return pl.pallas_call(
        body, name="adamw_gains",
        out_shape=[jax.ShapeDtypeStruct((8, D), F32)] * 4,
    )(gall, w8, m8, v8)


def kernel(x, positions, w_in, w_out, g_pre_mix, g_post_mix, g_pre_ffn, g_post_ffn, w_gate, w_up, w_down, loss_target, m_w_in, m_w_out, m_g_pre_mix, m_g_post_mix, m_g_pre_ffn, m_g_post_ffn, m_w_gate, m_w_up, m_w_down, v_w_in, v_w_out, v_g_pre_mix, v_g_post_mix, v_g_pre_ffn, v_g_post_ffn, v_w_gate, v_w_up, v_w_down):
    tr = lambda t: jnp.swapaxes(t, 1, 2)[0]
    shards = [w_in[0], w_out[0], tr(w_gate), tr(w_up), w_down[0]]
    moms = [m_w_in[0], m_w_out[0], tr(m_w_gate), tr(m_w_up), m_w_down[0]]
    vels = [v_w_in[0], v_w_out[0], tr(v_w_gate), tr(v_w_up), v_w_down[0]]
    xs, pos, tgt = x[0], positions.reshape(S, 1), loss_target[0]
    g1, g2, g3, g4 = g_pre_mix, g_post_mix, g_pre_ffn, g_post_ffn
    tabs = tuple(jnp.asarray(t) for t in _retention_tables())
    ifr, ifa = (jnp.asarray(t) for t in _rotary_tables())
    bf = [s.astype(BF16) for s in shards]

    win_g, wout_g = _exchange_alone("gather_in", _GatherShards(bf[:2]), bf[:2])
    wout_g = wout_g.reshape(D, D)
    proj, h1 = _proj_fwd(xs, g1, win_g)
    qr, kr, rv, aq, ak, av = _rot_fwd(proj, pos, ifr, ifa)
    o_raw, cat_r, states = _ret_fwd(qr, kr, rv, proj, tabs)
    (att_out, lse, cat_a), (wg_g, wu_g, wd_g) = _att_fwd(aq, ak, av, _GatherShards(bf[2:]), bf[2:])
    mix, x2, h3 = _mix_fwd(cat_r, cat_a, wout_g, xs, g2, g3)
    gt, up, a, f = _ffn_fwd(h3, wg_g, wu_g, wd_g)

    sq, dy, df, dg4 = _head_bwd(f, x2, tgt, g4)
    loss = 0.5 * lax.psum(sq[0, 0], ("x", "y", "c")) / D
    dgt, dup, dh3 = _ffn_bwd_act(df, gt, up, wg_g, wu_g, wd_g)
    ffn_grads = list(_ffn_bwd_w(a, df, h3, dgt, dup))
    (dx2, dmix, dg3, dg2), got = _norm_bwd(dh3, dy, x2, mix, g2, g3, _HalvesToSibling(ffn_grads), ffn_grads)
    pre = [_pair_sum(g, r) for g, r in zip(ffn_grads, got)]
    dret, datt, dwout = _mix_bwd(dmix, cat_r, cat_a, wout_g)
    (dq_att, dk_att, dv_att), parts = _att_bwd(aq, ak, av, datt, att_out, lse, _OverChips(pre), pre)
    sums = [_chip_sum(p) for p in parts]
    (dqr, dkr, drv, drg), ffn_full = _ret_bwd(qr, kr, rv, proj, o_raw, states, dret, tabs, _ShareHalves(sums), sums)
    dproj = _rot_bwd(pos, ifr, ifa, dqr, dkr, drv, drg, dq_att, dk_att, dv_att)
    in_grads = [_win_bwd_w(h1, dproj), dwout.reshape(N_CHIP, WOUT_R, D)]
    (dx, dg1), got = _in_bwd(dproj, win_g, xs, dx2, g1, _HalvesToSibling(in_grads), in_grads)

    pre = [_pair_sum(g, r) for g, r in zip(in_grads, got)]
    gblock = _pack8([dg1, dg2, dg3, dg4])
    *parts, gall = _exchange_alone("reduce_rest", _Both(_OverChips(pre), _GatherBlocks(gblock)), pre + [gblock])
    sums = [_chip_sum(p) for p in parts]
    in_full = _exchange_alone("share_rest", _ShareHalves(sums), sums)

    full = [in_full[0], in_full[1], ffn_full[1], ffn_full[2], ffn_full[0]]
    upd = [_adamw(w, g, m, v) for w, g, m, v in zip(shards, full, moms, vels)]
    gg, gd, gm, gv = _adamw_gains(gall, _pack8([g1, g2, g3, g4]),
                                  _pack8([m_g_pre_mix, m_g_post_mix, m_g_pre_ffn, m_g_post_ffn]),
                                  _pack8([v_g_pre_mix, v_g_post_mix, v_g_pre_ffn, v_g_post_ffn]))

    def order(mats, vecs):
        back = lambda t: jnp.swapaxes(t[None], 1, 2)
        return ([mats[0][None], mats[1][None]] + [vecs[i:i + 1] for i in range(4)]
                + [back(mats[2]), back(mats[3]), mats[4][None]])

    return (loss, dx[None],
            *order([u[0] for u in upd], gg),
            *order([u[1] for u in upd], gd),
            *order([u[2] for u in upd], gm),
            *order([u[3] for u in upd], gv))
```

```python
import functools

import numpy as np
import jax
import jax.numpy as jnp
from jax import lax
from jax.experimental import pallas as pl
from jax.experimental.pallas import tpu as pltpu

F32, BF16 = jnp.float32, jnp.bfloat16
MESH = pl.DeviceIdType.MESH

S = 2048
D = 1024
PW = 3072
N_CHIP = 4
WIN_C = PW // N_CHIP
DFF = 2816
FF_C = DFF // N_CHIP
WOUT_R = D // N_CHIP
RMS_EPS = 1e-6
GN_EPS = 1e-5
RET_C = 128
RET_SCALE = 32 ** -0.5
ATT_BLK = 128
ATT_SCALE = 64 ** -0.5
PATTERN_DILATIONS = (1, 4, 16)
NEG = -1e30
VMEM_LIMIT = 56 * 1024 * 1024

ADAM_LR, ADAM_B1, ADAM_B2, ADAM_EPS, ADAM_WD, ADAM_STEP = 0.001, 0.9, 0.999, 1e-08, 0.01, 10


def _params(*sem):
    return pltpu.CompilerParams(dimension_semantics=sem, vmem_limit_bytes=VMEM_LIMIT)


def _nt(a, b):
    return lax.dot_general(a, b, (((1,), (1,)), ((), ())), preferred_element_type=F32)


def _tn(a, b):
    return lax.dot_general(a, b, (((0,), (0,)), ((), ())), preferred_element_type=F32)


def _nn(a, b):
    return jnp.dot(a, b, preferred_element_type=F32)


def _rstd(v):
    return lax.rsqrt(jnp.mean(v * v, axis=-1, keepdims=True) + RMS_EPS)


def _sigmoid(v):
    return 1.0 / (1.0 + jnp.exp(-v))


def _rows(i, t):
    return pl.ds(pl.multiple_of(i * t, t), t)


def _retention_tables():
    h = np.arange(8, dtype=np.float32)
    log_g = np.log1p(-np.exp2(-5.0 - h)).astype(np.float32)
    idx = np.arange(RET_C, dtype=np.float32)
    diff = idx[:, None] - idx[None, :]
    dtab = np.where(diff >= 0, np.exp(log_g[:, None, None] * np.maximum(diff, 0.0)), 0.0).astype(np.float32)
    lane_head = np.arange(256) // 32
    a_tab = np.exp(log_g[lane_head][None, :] * (idx + 1.0)[:, None]).astype(np.float32)
    b_tab = np.exp(log_g[lane_head][None, :] * (RET_C - 1.0 - idx)[:, None]).astype(np.float32)
    lam = np.exp(log_g[lane_head] * RET_C).astype(np.float32)[:, None]
    bd = (lane_head[:, None] == (np.arange(512) // 64)[None, :]).astype(np.float32)
    return dtab, a_tab, b_tab, lam, bd


def _rotary_tables():
    inv_r = (1.0 / (np.float32(10000.0) ** np.linspace(0.0, 1.0, 16, dtype=np.float32))).astype(np.float32)
    inv_a = (np.float32(500000.0) ** (-np.arange(0, 16, 2, dtype=np.float32) / np.float32(16))).astype(np.float32)
    dr = np.arange(256) % 32
    ifr = inv_r[dr % 16][None, :].astype(np.float32)
    da = np.arange(512) % 64
    ifa = np.where(da < 16, inv_a[da % 8], 0.0)[None, :].astype(np.float32)
    return ifr, ifa


def _proj_fwd(x, g1, win_g):
    tm = 256

    def body(x_ref, g_ref, w_ref, proj_ref, h_ref):
        xv = x_ref[...]
        h = (xv * _rstd(xv) * g_ref[...]).astype(BF16)
        h_ref[...] = h
        for k in range(N_CHIP):
            proj_ref[:, k * WIN_C:(k + 1) * WIN_C] = _nn(h, w_ref[k])

    return pl.pallas_call(
        body, grid=(S // tm,), name="proj_fwd",
        in_specs=[pl.BlockSpec((tm, D), lambda i: (i, 0)), pl.BlockSpec((1, D), lambda i: (0, 0)),
                  pl.BlockSpec((N_CHIP, D, WIN_C), lambda i: (0, 0, 0))],
        out_specs=[pl.BlockSpec((tm, PW), lambda i: (i, 0)), pl.BlockSpec((tm, D), lambda i: (i, 0))],
        out_shape=[jax.ShapeDtypeStruct((S, PW), F32), jax.ShapeDtypeStruct((S, D), BF16)],
        compiler_params=_params("parallel"),
    )(x, g1, win_g)


def _rot_coeffs(pos_ref, ifr_ref, ifa_ref, tm):
    pos = pos_ref[...].astype(F32)
    ang_r = pos * ifr_ref[...]
    ang_a = pos * ifa_ref[...]
    lo_r = (lax.broadcasted_iota(jnp.int32, (tm, 256), 1) % 32) < 16
    lo_a = (lax.broadcasted_iota(jnp.int32, (tm, 512), 1) % 64) < 8
    return jnp.cos(ang_r), jnp.sin(ang_r), lo_r, jnp.cos(ang_a), jnp.sin(ang_a), lo_a


def _rot_fwd(proj, pos, ifr, ifa):
    tm = 256

    def body(p_ref, pos_ref, ifr_ref, ifa_ref, qr_ref, kr_ref, rv_ref, aq_ref, ak_ref, av_ref):
        cr, sr, lo_r, ca, sa, lo_a = _rot_coeffs(pos_ref, ifr_ref, ifa_ref, tm)

        def rot_r(v):
            return v * cr + sr * jnp.where(lo_r, -pltpu.roll(v, 240, 1), pltpu.roll(v, 16, 1))

        def rot_a(v):
            return v * ca + sa * jnp.where(lo_a, -pltpu.roll(v, 504, 1), pltpu.roll(v, 8, 1))

        qr_ref[...] = rot_r(p_ref[:, 0:256]).astype(BF16)
        kr_ref[...] = (rot_r(p_ref[:, 256:512]) * RET_SCALE).astype(BF16)
        rv_ref[...] = p_ref[:, 512:1024].astype(BF16)
        aq, ak = rot_a(p_ref[:, 1536:2048]), rot_a(p_ref[:, 2048:2560])
        for j in range(4):
            aq_ref[j] = aq[:, 128 * j:128 * j + 128]
            ak_ref[j] = ak[:, 128 * j:128 * j + 128]
            av_ref[j] = p_ref[:, 2560 + 128 * j:2560 + 128 * j + 128]

    row = lambda w: pl.BlockSpec((tm, w), lambda i: (i, 0))
    const = lambda w: pl.BlockSpec((1, w), lambda i: (0, 0))
    slab = pl.BlockSpec((4, tm, 128), lambda i: (0, i, 0))
    return pl.pallas_call(
        body, grid=(S // tm,), name="rot_fwd",
        in_specs=[row(PW), row(1), const(256), const(512)],
        out_specs=[row(256), row(256), row(512), slab, slab, slab],
        out_shape=[jax.ShapeDtypeStruct((S, w), BF16) for w in (256, 256, 512)]
                  + [jax.ShapeDtypeStruct((4, S, 128), F32)] * 3,
        compiler_params=_params("parallel"),
    )(proj, pos, ifr, ifa)


def _seg_mean(v):
    lo = lax.broadcasted_iota(jnp.int32, v.shape, 1) < 64
    s_lo = jnp.sum(jnp.where(lo, v, 0.0), axis=-1, keepdims=True)
    s_hi = jnp.sum(jnp.where(lo, 0.0, v), axis=-1, keepdims=True)
    return jnp.where(lo, s_lo, s_hi) * (1.0 / 64.0)


def _ret_fwd(qr, kr, rv, proj, tabs):
    C = RET_C
    dtab, a_tab, b_tab, lam, bd = tabs

    def body(q_ref, k_ref, v_ref, g_ref, dt_ref, a_ref, b_ref, lam_ref, bd_ref, o_ref, cat_ref, st_ref, R):
        @pl.when(pl.program_id(0) == 0)
        def _():
            R[...] = jnp.zeros_like(R)

        q, k, v = q_ref[...], k_ref[...], v_ref[...]
        lane_head = lax.broadcasted_iota(jnp.int32, (C, 256), 1) // 32
        col_head = lax.broadcasted_iota(jnp.int32, (C, 256), 1) // 64
        rb = R[...].astype(BF16)
        st_ref[...] = rb
        qa = (q.astype(F32) * a_ref[...]).astype(BF16)
        cross = _nn(qa, rb)
        og = [cross[:, 0:256], cross[:, 256:512]]
        for h in range(8):
            g = h // 4
            qm = jnp.where(lane_head == h, q, jnp.zeros_like(q))
            p = (_nt(qm, k) * dt_ref[h]).astype(BF16)
            pv = _nn(p, v[:, 256 * g:256 * g + 256])
            og[g] = og[g] + jnp.where(col_head == (h % 4), pv, 0.0)
        kb = (k.astype(F32) * b_ref[...]).astype(BF16)
        R[...] = R[...] * lam_ref[...] + _tn(kb, v) * bd_ref[...]
        o_ref[:, 0:256] = og[0]
        o_ref[:, 256:512] = og[1]
        for j in range(4):
            oj = og[j // 2][:, 128 * (j % 2):128 * (j % 2) + 128]
            xc = oj - _seg_mean(oj)
            rn = xc * lax.rsqrt(_seg_mean(xc * xc) + GN_EPS)
            gj = g_ref[:, 128 * j:128 * j + 128]
            cat_ref[:, 128 * j:128 * j + 128] = (rn * (gj * _sigmoid(gj))).astype(BF16)

    row = lambda w: pl.BlockSpec((C, w), lambda n: (n, 0))
    full = lambda a: pl.BlockSpec(a.shape, lambda n: (0,) * a.ndim)
    return pl.pallas_call(
        body, grid=(S // C,), name="ret_fwd",
        in_specs=[row(256), row(256), row(512), pl.BlockSpec((C, 512), lambda n: (n, 2)),
                  full(dtab), full(a_tab), full(b_tab), full(lam), full(bd)],
        out_specs=[row(512), row(512), pl.BlockSpec((None, 256, 512), lambda n: (n, 0, 0))],
        out_shape=[jax.ShapeDtypeStruct((S, 512), F32), jax.ShapeDtypeStruct((S, 512), BF16),
                   jax.ShapeDtypeStruct((S // C, 256, 512), BF16)],
        scratch_shapes=[pltpu.VMEM((256, 512), F32)],
        compiler_params=_params("arbitrary"),
    )(qr, kr, rv, proj, dtab, a_tab, b_tab, lam, bd)


def _att_mask(ib, has_prev):
    nk = 2 * ATT_BLK if has_prev else ATT_BLK
    a = lax.broadcasted_iota(jnp.int32, (ATT_BLK, nk), 0)
    kk = lax.broadcasted_iota(jnp.int32, (ATT_BLK, nk), 1)
    if has_prev:
        dist = ATT_BLK + a - kk
        return (dist >= 0) & (dist <= ATT_BLK) & ((ib * ATT_BLK - ATT_BLK + kk) >= 0)
    return (a - kk) >= 0


def _class_rows(ib, r, d):
    if d == 1:
        return pl.ds(pl.multiple_of(ib * ATT_BLK, ATT_BLK), ATT_BLK)
    return pl.ds(ib * ATT_BLK * d + r, ATT_BLK, stride=d)


def _slab_pair(ref, g, rows):
    return jnp.concatenate([ref[2 * g, rows, :], ref[2 * g + 1, rows, :]], axis=1)


def _att_blocks(d):
    nb = S // d // ATT_BLK
    return nb, nb > 1


def _att_fwd(aq, ak, av, exchange, exchange_args):
    def body(q_ref, k_ref, v_ref, o_ref, l_ref, cat_ref, xc):
        xc.start()
        lane_head = lax.broadcasted_iota(jnp.int32, (ATT_BLK, 256), 1) // 64
        for pi, d in enumerate(PATTERN_DILATIONS):
            if pi == len(PATTERN_DILATIONS) - 1:
                xc.middle()
            nb, has_prev = _att_blocks(d)

            def block(b, carry, pi=pi, d=d, nb=nb, has_prev=has_prev):
                r, ib = b // nb, b % nb
                rows = _class_rows(ib, r, d)
                prow = _class_rows(jnp.maximum(ib - 1, 0), r, d)
                valid = _att_mask(ib, has_prev)
                for g in range(2):
                    qg = _slab_pair(q_ref, g, rows).astype(BF16)
                    kg = _slab_pair(k_ref, g, rows)
                    vg = _slab_pair(v_ref, g, rows)
                    if has_prev:
                        kg = jnp.concatenate([_slab_pair(k_ref, g, prow), kg], axis=0)
                        vg = jnp.concatenate([_slab_pair(v_ref, g, prow), vg], axis=0)
                    kg, vg = kg.astype(BF16), vg.astype(BF16)
                    og = jnp.zeros((ATT_BLK, 256), F32)
                    lg = jnp.zeros((ATT_BLK, 256), F32)
                    for hh in range(4):
                        qm = jnp.where(lane_head == hh, qg, jnp.zeros_like(qg))
                        s = jnp.where(valid, _nt(qm, kg) * ATT_SCALE, NEG)
                        m = jnp.max(s, axis=-1, keepdims=True)
                        p = jnp.exp(s - m)
                        den = jnp.sum(p, axis=-1, keepdims=True)
                        o = _nn(p.astype(BF16), vg) / den
                        og = jnp.where(lane_head == hh, o, og)
                        lg = jnp.where(lane_head == hh, m + jnp.log(den), lg)
                    for jj in range(2):
                        j = 2 * g + jj
                        o_new, l_new = og[:, 128 * jj:128 * jj + 128], lg[:, 128 * jj:128 * jj + 128]
                        if pi > 0:
                            o_old, l_old = o_ref[j, rows, :], l_ref[j, rows, :]
                            mx = jnp.maximum(l_old, l_new)
                            ea, eb = jnp.exp(l_old - mx), jnp.exp(l_new - mx)
                            den = ea + eb
                            o_new = (ea * o_old + eb * o_new) / den
                            l_new = mx + jnp.log(den)
                        o_ref[j, rows, :] = o_new
                        l_ref[j, rows, :] = l_new
                return carry

            lax.fori_loop(0, S // ATT_BLK, block, 0)

        def to_cat(i, carry):
            rows = _rows(i, 256)
            for j in range(4):
                cat_ref[rows, 128 * j:128 * j + 128] = o_ref[j, rows, :].astype(BF16)
            return carry

        lax.fori_loop(0, S // 256, to_cat, 0)
        xc.finish()

    slab = jax.ShapeDtypeStruct((4, S, 128), F32)
    return _carry("att_fwd", body, exchange, exchange_args, (aq, ak, av), [VMEM] * 3, [VMEM] * 3,
                  [slab, slab, jax.ShapeDtypeStruct((S, 512), BF16)])


def _mix_fwd(cat_r, cat_a, wout, x, g2, g3):
    tm = 256

    def body(cr_ref, ca_ref, w_ref, x_ref, g2_ref, g3_ref, mix_ref, x2_ref, h3_ref):
        mix = _nn(cr_ref[...], w_ref[0:512, :]) + _nn(ca_ref[...], w_ref[512:1024, :])
        mix_ref[...] = mix
        x2 = x_ref[...] + mix * _rstd(mix) * g2_ref[...]
        x2_ref[...] = x2
        h3_ref[...] = (x2 * _rstd(x2) * g3_ref[...]).astype(BF16)

    row = lambda w: pl.BlockSpec((tm, w), lambda i: (i, 0))
    vec = pl.BlockSpec((1, D), lambda i: (0, 0))
    return pl.pallas_call(
        body, grid=(S // tm,), name="mix_fwd",
        in_specs=[row(512), row(512), pl.BlockSpec((D, D), lambda i: (0, 0)), row(D), vec, vec],
        out_specs=[row(D), row(D), row(D)],
        out_shape=[jax.ShapeDtypeStruct((S, D), F32), jax.ShapeDtypeStruct((S, D), F32),
                   jax.ShapeDtypeStruct((S, D), BF16)],
        compiler_params=_params("parallel"),
    )(cat_r, cat_a, wout, x, g2, g3)


def _ffn_fwd(h3, wg, wu, wd):
    tm = 256

    def body(h_ref, wg_ref, wu_ref, wd_ref, gt_ref, up_ref, a_ref, f_ref):
        k, i = pl.program_id(0), pl.program_id(1)
        h = h_ref[...]
        gt = _nt(h, wg_ref[...])
        up = _nt(h, wu_ref[...])
        gt_ref[...] = gt
        up_ref[...] = up
        a = (gt * _sigmoid(gt) * up).astype(BF16)
        a_ref[...] = a
        part = _nn(a, wd_ref[...])
        rows = _rows(i, tm)

        @pl.when(k == 0)
        def _():
            f_ref[rows, :] = part

        @pl.when(k > 0)
        def _():
            f_ref[rows, :] = f_ref[rows, :] + part

    wrow = pl.BlockSpec((None, FF_C, D), lambda k, i: (k, 0, 0))
    act = pl.BlockSpec((None, tm, FF_C), lambda k, i: (k, i, 0))
    return pl.pallas_call(
        body, grid=(N_CHIP, S // tm), name="ffn_fwd",
        in_specs=[pl.BlockSpec((tm, D), lambda k, i: (i, 0)), wrow, wrow, wrow],
        out_specs=[act, act, act, pl.BlockSpec((S, D), lambda k, i: (0, 0))],
        out_shape=[jax.ShapeDtypeStruct((N_CHIP, S, FF_C), F32), jax.ShapeDtypeStruct((N_CHIP, S, FF_C), F32),
                   jax.ShapeDtypeStruct((N_CHIP, S, FF_C), BF16), jax.ShapeDtypeStruct((S, D), F32)],
        compiler_params=_params("arbitrary", "arbitrary"),
    )(h3, wg, wu, wd)


def _head_bwd(f, x2, tgt, g4):
    tm = 256

    def body(f_ref, x2_ref, t_ref, g_ref, loss_ref, dy_ref, df_ref, dg_ref):
        @pl.when(pl.program_id(0) == 0)
        def _():
            loss_ref[...] = jnp.zeros_like(loss_ref)
            dg_ref[...] = jnp.zeros_like(dg_ref)

        fv = f_ref[...]
        r = _rstd(fv)
        fn = fv * r
        e = x2_ref[...] + fn * g_ref[...] - t_ref[...]
        sq = jnp.sum(jnp.sum(e * e, axis=-1, keepdims=True), axis=0, keepdims=True)
        loss_ref[...] = loss_ref[...] + sq
        dy = e * (1.0 / D)
        dy_ref[...] = dy
        dg_ref[...] = dg_ref[...] + jnp.sum(dy * fn, axis=0, keepdims=True)
        t = dy * g_ref[...]
        df_ref[...] = (r * (t - fn * jnp.mean(t * fn, axis=-1, keepdims=True))).astype(BF16)

    row = pl.BlockSpec((tm, D), lambda i: (i, 0))
    vec = pl.BlockSpec((1, D), lambda i: (0, 0))
    return pl.pallas_call(
        body, grid=(S // tm,), name="head_bwd",
        in_specs=[row, row, row, vec],
        out_specs=[pl.BlockSpec((8, 128), lambda i: (0, 0)), row, row, vec],
        out_shape=[jax.ShapeDtypeStruct((8, 128), F32), jax.ShapeDtypeStruct((S, D), F32),
                   jax.ShapeDtypeStruct((S, D), BF16), jax.ShapeDtypeStruct((1, D), F32)],
        compiler_params=_params("arbitrary"),
    )(f, x2, tgt, g4)


def _ffn_bwd_act(df, gt, up, wg, wu, wd):
    tm = 512

    def body(df_ref, gt_ref, up_ref, wg_ref, wu_ref, wd_ref, dgt_ref, dup_ref, dh_ref):
        k = pl.program_id(1)
        da = _nt(df_ref[...], wd_ref[...])
        gt, up = gt_ref[...], up_ref[...]
        sg = _sigmoid(gt)
        dup = (da * gt * sg).astype(BF16)
        dgt = (da * up * (sg * (1.0 + gt * (1.0 - sg)))).astype(BF16)
        dup_ref[...] = dup
        dgt_ref[...] = dgt
        part = _nn(dgt, wg_ref[...]) + _nn(dup, wu_ref[...])

        @pl.when(k == 0)
        def _():
            dh_ref[...] = part

        @pl.when(k > 0)
        def _():
            dh_ref[...] = dh_ref[...] + part

    wrow = pl.BlockSpec((None, FF_C, D), lambda i, k: (k, 0, 0))
    act = pl.BlockSpec((None, tm, FF_C), lambda i, k: (k, i, 0))
    row = pl.BlockSpec((tm, D), lambda i, k: (i, 0))
    return pl.pallas_call(
        body, grid=(S // tm, N_CHIP), name="ffn_bwd_act",
        in_specs=[row, act, act, wrow, wrow, wrow],
        out_specs=[act, act, row],
        out_shape=[jax.ShapeDtypeStruct((N_CHIP, S, FF_C), BF16), jax.ShapeDtypeStruct((N_CHIP, S, FF_C), BF16),
                   jax.ShapeDtypeStruct((S, D), F32)],
        compiler_params=_params("parallel", "arbitrary"),
    )(df, gt, up, wg, wu, wd)


def _ffn_bwd_w(a, df, h3, dgt, dup):
    tm = 512

    def body(a_ref, df_ref, h_ref, dgt_ref, dup_ref, dwd_ref, dwg_ref, dwu_ref, acc_d, acc_g, acc_u):
        i = pl.program_id(1)

        @pl.when(i == 0)
        def _():
            acc_d[...] = jnp.zeros_like(acc_d)
            acc_g[...] = jnp.zeros_like(acc_g)
            acc_u[...] = jnp.zeros_like(acc_u)

        h = h_ref[...]
        acc_d[...] += _tn(a_ref[...], df_ref[...])
        acc_g[...] += _tn(dgt_ref[...], h)
        acc_u[...] += _tn(dup_ref[...], h)

        @pl.when(i == S // tm - 1)
        def _():
            dwd_ref[...] = acc_d[...].astype(BF16)
            dwg_ref[...] = acc_g[...].astype(BF16)
            dwu_ref[...] = acc_u[...].astype(BF16)

    act = pl.BlockSpec((None, tm, FF_C), lambda k, i: (k, i, 0))
    row = pl.BlockSpec((tm, D), lambda k, i: (i, 0))
    wrow = pl.BlockSpec((None, FF_C, D), lambda k, i: (k, 0, 0))
    return pl.pallas_call(
        body, grid=(N_CHIP, S // tm), name="ffn_bwd_w",
        in_specs=[act, row, row, act, act],
        out_specs=[wrow, wrow, wrow],
        out_shape=[jax.ShapeDtypeStruct((N_CHIP, FF_C, D), BF16)] * 3,
        scratch_shapes=[pltpu.VMEM((FF_C, D), F32)] * 3,
        compiler_params=_params("parallel", "arbitrary"),
    )(a, df, h3, dgt, dup)


def _norm_bwd(dh3, dy, x2, mix, g2, g3, exchange, exchange_args):
    tm = 256

    def body(dh_ref, dy_ref, x2_ref, mix_ref, g2_ref, g3_ref, dx2_ref, dmix_ref, dg3_ref, dg2_ref, xc):
        @pl.when(pl.program_id(0) == 0)
        def _():
            xc.start()
            dg3_ref[...] = jnp.zeros_like(dg3_ref)
            dg2_ref[...] = jnp.zeros_like(dg2_ref)

        x2 = x2_ref[...]
        r3 = _rstd(x2)
        xn = x2 * r3
        dh = dh_ref[...]
        dg3_ref[...] = dg3_ref[...] + jnp.sum(dh * xn, axis=0, keepdims=True)
        t = dh * g3_ref[...]
        dx2 = dy_ref[...] + r3 * (t - xn * jnp.mean(t * xn, axis=-1, keepdims=True))
        dx2_ref[...] = dx2
        mix = mix_ref[...]
        r2 = _rstd(mix)
        mn = mix * r2
        dg2_ref[...] = dg2_ref[...] + jnp.sum(dx2 * mn, axis=0, keepdims=True)
        u = dx2 * g2_ref[...]
        dmix_ref[...] = (r2 * (u - mn * jnp.mean(u * mn, axis=-1, keepdims=True))).astype(BF16)

        @pl.when(pl.program_id(0) == S // tm - 1)
        def _():
            xc.middle()
            xc.finish()

    row = pl.BlockSpec((tm, D), lambda i: (i, 0))
    vec = pl.BlockSpec((1, D), lambda i: (0, 0))
    return _carry("norm_bwd", body, exchange, exchange_args, (dh3, dy, x2, mix, g2, g3),
                  [row, row, row, row, vec, vec], [row, row, vec, vec],
                  [jax.ShapeDtypeStruct((S, D), F32), jax.ShapeDtypeStruct((S, D), BF16),
                   jax.ShapeDtypeStruct((1, D), F32), jax.ShapeDtypeStruct((1, D), F32)],
                  grid=(S // tm,), semantics=("arbitrary",))


def _mix_bwd(dmix, cat_r, cat_a, wout):
    tm = 512

    def body(dm_ref, cr_ref, ca_ref, w_ref, dret_ref, datt_ref, dw_ref, acc):
        i = pl.program_id(0)

        @pl.when(i == 0)
        def _():
            acc[...] = jnp.zeros_like(acc)

        dm = dm_ref[...]
        dret_ref[...] = _nt(dm, w_ref[0:512, :])
        datt = _nt(dm, w_ref[512:1024, :])
        for j in range(4):
            datt_ref[j] = datt[:, 128 * j:128 * j + 128]
        acc[0:512, :] += _tn(cr_ref[...], dm)
        acc[512:1024, :] += _tn(ca_ref[...], dm)

        @pl.when(i == S // tm - 1)
        def _():
            dw_ref[...] = acc[...].astype(BF16)

    row = lambda w: pl.BlockSpec((tm, w), lambda i: (i, 0))
    full = pl.BlockSpec((D, D), lambda i: (0, 0))
    return pl.pallas_call(
        body, grid=(S // tm,), name="mix_bwd",
        in_specs=[row(D), row(512), row(512), full],
        out_specs=[row(512), pl.BlockSpec((4, tm, 128), lambda i: (0, i, 0)), full],
        out_shape=[jax.ShapeDtypeStruct((S, 512), F32), jax.ShapeDtypeStruct((4, S, 128), F32),
                   jax.ShapeDtypeStruct((D, D), BF16)],
        scratch_shapes=[pltpu.VMEM((D, D), F32)],
        compiler_params=_params("arbitrary"),
    )(dmix, cat_r, cat_a, wout)


def _att_bwd(aq, ak, av, datt, att_out, lse, exchange, exchange_args):
    def body(q_ref, k_ref, v_ref, do_ref, out_ref, l_ref, dq_ref, dk_ref, dv_ref, xc):
        xc.start()

        def clear(i, carry):
            rows = _rows(i, 256)
            for ref in (dq_ref, dk_ref, dv_ref):
                for j in range(4):
                    ref[j, rows, :] = jnp.zeros((256, 128), F32)
            return carry

        lax.fori_loop(0, S // 256, clear, 0)
        lane_head = lax.broadcasted_iota(jnp.int32, (ATT_BLK, 256), 1) // 64
        for d in PATTERN_DILATIONS:
            nb, has_prev = _att_blocks(d)

            def block(b, carry, d=d, nb=nb, has_prev=has_prev):
                r, ib = b // nb, b % nb
                rows = _class_rows(ib, r, d)
                prow = _class_rows(jnp.maximum(ib - 1, 0), r, d)
                valid = _att_mask(ib, has_prev)
                for g in range(2):
                    qg = _slab_pair(q_ref, g, rows).astype(BF16)
                    kg = _slab_pair(k_ref, g, rows)
                    vg = _slab_pair(v_ref, g, rows)
                    if has_prev:
                        kg = jnp.concatenate([_slab_pair(k_ref, g, prow), kg], axis=0)
                        vg = jnp.concatenate([_slab_pair(v_ref, g, prow), vg], axis=0)
                    kg, vg = kg.astype(BF16), vg.astype(BF16)
                    dog = _slab_pair(do_ref, g, rows)
                    outg = _slab_pair(out_ref, g, rows)
                    lg = _slab_pair(l_ref, g, rows)
                    dq = jnp.zeros((ATT_BLK, 256), F32)
                    dk = jnp.zeros(kg.shape, F32)
                    dv = jnp.zeros(kg.shape, F32)
                    for hh in range(4):
                        mine = lane_head == hh
                        qm = jnp.where(mine, qg, jnp.zeros_like(qg))
                        dom = jnp.where(mine, dog, 0.0)
                        delta = jnp.sum(dom * outg, axis=-1, keepdims=True)
                        lh = jnp.max(jnp.where(mine, lg, NEG), axis=-1, keepdims=True)
                        s = jnp.where(valid, _nt(qm, kg) * ATT_SCALE, NEG)
                        p = jnp.exp(s - lh)
                        domb = dom.astype(BF16)
                        ds = (p * (_nt(domb, vg) - delta) * ATT_SCALE).astype(BF16)
                        dq = jnp.where(mine, _nn(ds, kg), dq)
                        dk = dk + _tn(ds, qm)
                        dv = dv + _tn(p.astype(BF16), domb)
                    for jj in range(2):
                        j, sl = 2 * g + jj, slice(128 * jj, 128 * jj + 128)
                        dq_ref[j, rows, :] += dq[:, sl]
                        if has_prev:
                            dk_ref[j, prow, :] += dk[0:ATT_BLK, sl]
                            dv_ref[j, prow, :] += dv[0:ATT_BLK, sl]
                            dk_ref[j, rows, :] += dk[ATT_BLK:2 * ATT_BLK, sl]
                            dv_ref[j, rows, :] += dv[ATT_BLK:2 * ATT_BLK, sl]
                        else:
                            dk_ref[j, rows, :] += dk[:, sl]
                            dv_ref[j, rows, :] += dv[:, sl]
                return carry

            lax.fori_loop(0, S // ATT_BLK, block, 0)
        xc.middle()
        xc.finish()

    slab = jax.ShapeDtypeStruct((4, S, 128), F32)
    return _carry("att_bwd", body, exchange, exchange_args, (aq, ak, av, datt, att_out, lse), [VMEM] * 6, [VMEM] * 3,
                  [slab, slab, slab])


def _ret_bwd(qr, kr, rv, proj, o_raw, states, dret, tabs, exchange, exchange_args):
    C = RET_C
    nc = S // C
    dtab, a_tab, b_tab, lam, bd = tabs

    def body(q_ref, k_ref, v_ref, g_ref, o_ref, st_ref, dr_ref, dt_ref, a_ref, b_ref, lam_ref, bd_ref,
             dq_ref, dk_ref, dv_ref, dg_ref, dR, exch):
        @pl.when(pl.program_id(0) == 0)
        def _():
            exch.start()
            dR[...] = jnp.zeros_like(dR)

        q, k, v = q_ref[...], k_ref[...], v_ref[...]
        lane_head = lax.broadcasted_iota(jnp.int32, (C, 256), 1) // 32
        col_head = lax.broadcasted_iota(jnp.int32, (C, 256), 1) // 64
        dos = []
        for j in range(4):
            sl = slice(128 * j, 128 * j + 128)
            oj = o_ref[:, sl]
            xc = oj - _seg_mean(oj)
            rs = lax.rsqrt(_seg_mean(xc * xc) + GN_EPS)
            rn = xc * rs
            gj = g_ref[:, sl]
            sg = _sigmoid(gj)
            dret = dr_ref[:, sl]
            dg_ref[:, sl] = dret * rn * (sg * (1.0 + gj * (1.0 - sg)))
            drn = dret * (gj * sg)
            dos.append(rs * (drn - _seg_mean(drn) - rn * _seg_mean(drn * rn)))
        do = [jnp.concatenate(dos[0:2], axis=1), jnp.concatenate(dos[2:4], axis=1)]
        do8 = jnp.concatenate(do, axis=1).astype(BF16)
        drb = dR[...].astype(BF16)
        rb = st_ref[...]
        dq = _nt(do8, rb) * a_ref[...]
        dk = _nt(v, drb) * b_ref[...]
        kb = (k.astype(F32) * b_ref[...]).astype(BF16)
        dvall = _nn(kb, drb)
        dv = [dvall[:, 0:256], dvall[:, 256:512]]
        for h in range(8):
            g = h // 4
            vg = v[:, 256 * g:256 * g + 256]
            mine = lane_head == h
            qm = jnp.where(mine, q, jnp.zeros_like(q))
            dom = jnp.where(col_head == (h % 4), do[g], 0.0).astype(BF16)
            dec = dt_ref[h]
            p = (_nt(qm, k) * dec).astype(BF16)
            ds = (_nt(dom, vg) * dec).astype(BF16)
            dq = jnp.where(mine, dq + _nn(ds, k), dq)
            dk = dk + _tn(ds, qm)
            dv[g] = dv[g] + _tn(p, dom)
        qa = (q.astype(F32) * a_ref[...]).astype(BF16)
        dR[...] = dR[...] * lam_ref[...] + _tn(qa, do8) * bd_ref[...]
        dq_ref[...] = dq
        dk_ref[...] = dk
        dv_ref[:, 0:256] = dv[0]
        dv_ref[:, 256:512] = dv[1]

        @pl.when(pl.program_id(0) == nc - 1)
        def _():
            exch.middle()
            exch.finish()

    rev = lambda w: pl.BlockSpec((C, w), lambda n: (nc - 1 - n, 0))
    full = lambda a: pl.BlockSpec(a.shape, lambda n: (0,) * a.ndim)
    return _carry(
        "ret_bwd", body, exchange, exchange_args, (qr, kr, rv, proj, o_raw, states, dret, dtab, a_tab, b_tab, lam, bd),
        [rev(256), rev(256), rev(512), pl.BlockSpec((C, 512), lambda n: (nc - 1 - n, 2)), rev(512),
         pl.BlockSpec((None, 256, 512), lambda n: (nc - 1 - n, 0, 0)), rev(512),
         full(dtab), full(a_tab), full(b_tab), full(lam), full(bd)],
        [rev(256), rev(256), rev(512), rev(512)],
        [jax.ShapeDtypeStruct((S, 256), F32), jax.ShapeDtypeStruct((S, 256), F32),
         jax.ShapeDtypeStruct((S, 512), F32), jax.ShapeDtypeStruct((S, 512), F32)],
        scratch_shapes=[pltpu.VMEM((256, 512), F32)], grid=(nc,), semantics=("arbitrary",))


def _rot_bwd(pos, ifr, ifa, dqr, dkr, drv, drg, dq_att, dk_att, dv_att):
    tm = 256

    def body(pos_ref, ifr_ref, ifa_ref, dqr_ref, dkr_ref, drv_ref, drg_ref, dqa_ref, dka_ref, dva_ref, dp_ref):
        cr, sr, lo_r, ca, sa, lo_a = _rot_coeffs(pos_ref, ifr_ref, ifa_ref, tm)

        def unrot_r(g):
            gs = g * sr
            return g * cr + pltpu.roll(jnp.where(lo_r, -gs, 0.0), 16, 1) + pltpu.roll(jnp.where(lo_r, 0.0, gs), 240, 1)

        def unrot_a(g):
            gs = g * sa
            return g * ca + pltpu.roll(jnp.where(lo_a, -gs, 0.0), 8, 1) + pltpu.roll(jnp.where(lo_a, 0.0, gs), 504, 1)

        def wide(ref):
            return jnp.concatenate([ref[j] for j in range(4)], axis=1)

        dp_ref[:, 0:256] = unrot_r(dqr_ref[...]).astype(BF16)
        dp_ref[:, 256:512] = unrot_r(dkr_ref[...] * RET_SCALE).astype(BF16)
        dp_ref[:, 512:1024] = drv_ref[...].astype(BF16)
        dp_ref[:, 1024:1536] = drg_ref[...].astype(BF16)
        dp_ref[:, 1536:2048] = unrot_a(wide(dqa_ref)).astype(BF16)
        dp_ref[:, 2048:2560] = unrot_a(wide(dka_ref)).astype(BF16)
        dp_ref[:, 2560:3072] = wide(dva_ref).astype(BF16)

    row = lambda w: pl.BlockSpec((tm, w), lambda i: (i, 0))
    const = lambda w: pl.BlockSpec((1, w), lambda i: (0, 0))
    slab = pl.BlockSpec((4, tm, 128), lambda i: (0, i, 0))
    return pl.pallas_call(
        body, grid=(S // tm,), name="rot_bwd",
        in_specs=[row(1), const(256), const(512), row(256), row(256), row(512), row(512), slab, slab, slab],
        out_specs=row(PW), out_shape=jax.ShapeDtypeStruct((S, PW), BF16),
        compiler_params=_params("parallel"),
    )(pos, ifr, ifa, dqr, dkr, drv, drg, dq_att, dk_att, dv_att)


def _win_bwd_w(h1, dproj):
    tm = 512

    def body(h_ref, dp_ref, dw_ref, acc):
        i = pl.program_id(1)

        @pl.when(i == 0)
        def _():
            acc[...] = jnp.zeros_like(acc)

        acc[...] += _tn(h_ref[...], dp_ref[...])

        @pl.when(i == S // tm - 1)
        def _():
            dw_ref[...] = acc[...].astype(BF16)

    return pl.pallas_call(
        body, grid=(N_CHIP, S // tm), name="win_bwd_w",
        in_specs=[pl.BlockSpec((tm, D), lambda k, i: (i, 0)), pl.BlockSpec((tm, WIN_C), lambda k, i: (i, k))],
        out_specs=pl.BlockSpec((None, D, WIN_C), lambda k, i: (k, 0, 0)),
        out_shape=jax.ShapeDtypeStruct((N_CHIP, D, WIN_C), BF16),
        scratch_shapes=[pltpu.VMEM((D, WIN_C), F32)],
        compiler_params=_params("parallel", "arbitrary"),
    )(h1, dproj)


def _in_bwd(dproj, win_g, x, dx2, g1, exchange, exchange_args):
    tm = 256

    def body(dp_ref, w_ref, x_ref, dx2_ref, g_ref, dx_ref, dg_ref, xc):
        @pl.when(pl.program_id(0) == 0)
        def _():
            xc.start()
            dg_ref[...] = jnp.zeros_like(dg_ref)

        dh = _nt(dp_ref[:, 0:WIN_C], w_ref[0])
        for k in range(1, N_CHIP):
            dh = dh + _nt(dp_ref[:, k * WIN_C:(k + 1) * WIN_C], w_ref[k])
        xv = x_ref[...]
        r = _rstd(xv)
        xn = xv * r
        dg_ref[...] = dg_ref[...] + jnp.sum(dh * xn, axis=0, keepdims=True)
        t = dh * g_ref[...]
        dx_ref[...] = dx2_ref[...] + r * (t - xn * jnp.mean(t * xn, axis=-1, keepdims=True))

        @pl.when(pl.program_id(0) == S // tm - 1)
        def _():
            xc.middle()
            xc.finish()

    row = lambda w: pl.BlockSpec((tm, w), lambda i: (i, 0))
    vec = pl.BlockSpec((1, D), lambda i: (0, 0))
    return _carry("in_bwd", body, exchange, exchange_args, (dproj, win_g, x, dx2, g1),
                  [row(PW), pl.BlockSpec((N_CHIP, D, WIN_C), lambda i: (0, 0, 0)), row(D), row(D), vec],
                  [row(D), vec], [jax.ShapeDtypeStruct((S, D), F32), jax.ShapeDtypeStruct((1, D), F32)],
                  grid=(S // tm,), semantics=("arbitrary",))


ANY = pl.BlockSpec(memory_space=pl.ANY)
VMEM = pl.BlockSpec(memory_space=pltpu.VMEM)
FLIPS = ((1, 0), (0, 1), (1, 1))


def _place():
    x, y, c = lax.axis_index("x"), lax.axis_index("y"), lax.axis_index("c")
    chips = [((1 - x) if fx else x, (1 - y) if fy else y) for fx, fy in FLIPS]
    return x, y, c, 2 * x + y, chips


def _remote(src, dst, send_sem, recv_sem, device):
    return pltpu.make_async_remote_copy(src_ref=src, dst_ref=dst, send_sem=send_sem, recv_sem=recv_sem,
                                        device_id=device, device_id_type=MESH)


def _staggered(issue):
    c = lax.axis_index("c")

    @pl.when(c == 0)
    def _():
        issue((0, 1, 2))

    @pl.when(c == 1)
    def _():
        issue((1, 0, 2))


class _Exchange:
    aliases = {}

    def middle(self, ins, outs, sems):
        pass


class _GatherShards(_Exchange):
    def __init__(self, shards):
        n = self.n = len(shards)
        self.n_in = self.n_out = n
        self.out_shape = [jax.ShapeDtypeStruct((N_CHIP,) + s.shape, s.dtype) for s in shards]
        dma = pltpu.SemaphoreType.DMA
        self.scratch = [dma((3 * n,)), dma((3 * n,)), dma((3 * n,)), dma((3 * n,)), dma((n,))]

    def _ici(self, ins, outs, sems, a, j, chip):
        x, y, c, me, chips = _place()
        half = ins[a].shape[0] // 2
        return _remote(ins[a].at[pl.ds(c * half, half), :], outs[a].at[me, pl.ds(c * half, half), :],
                       sems[0].at[3 * a + j], sems[1].at[3 * a + j], (*chip, c))

    def _fwd(self, outs, sems, a, j, chip, half_of):
        x, y, c, me, chips = _place()
        half = outs[a].shape[1] // 2
        blk = outs[a].at[2 * chip[0] + chip[1], pl.ds(half_of * half, half), :]
        return _remote(blk, blk, sems[2].at[3 * a + j], sems[3].at[3 * a + j], (x, y, 1 - c))

    def _local(self, ins, outs, sems, a):
        return pltpu.make_async_copy(ins[a], outs[a].at[_place()[3]], sems[4].at[a])

    def start(self, ins, outs, sems):
        chips = _place()[4]

        def issue(order):
            for a in range(self.n):
                for j in order:
                    self._ici(ins, outs, sems, a, j, chips[j]).start()

        _staggered(issue)
        for a in range(self.n):
            self._local(ins, outs, sems, a).start()

    def middle(self, ins, outs, sems):
        x, y, c, me, chips = _place()
        for a in range(self.n):
            for j, chip in enumerate(chips):
                half = outs[a].shape[1] // 2
                blk = outs[a].at[2 * chip[0] + chip[1], pl.ds(c * half, half), :]
                _remote(blk, blk, sems[0].at[3 * a + j], sems[1].at[3 * a + j], (x, y, c)).wait_recv()
                self._fwd(outs, sems, a, j, chip, c).start()

    def finish(self, ins, outs, sems):
        x, y, c, me, chips = _place()
        for a in range(self.n):
            for j, chip in enumerate(chips):
                self._fwd(outs, sems, a, j, chip, 1 - c).wait_recv()
        for a in range(self.n):
            for j, chip in enumerate(chips):
                self._ici(ins, outs, sems, a, j, chip).wait_send()
                self._fwd(outs, sems, a, j, chip, c).wait_send()
            self._local(ins, outs, sems, a).wait()


class _HalvesToSibling(_Exchange):
    def __init__(self, grads):
        n = self.n = len(grads)
        self.n_in = self.n_out = n
        self.out_shape = [jax.ShapeDtypeStruct((N_CHIP, g.shape[1] // 2, g.shape[2]), g.dtype) for g in grads]
        self.scratch = [pltpu.SemaphoreType.DMA((n,)), pltpu.SemaphoreType.DMA((n,))]

    def _copy(self, ins, outs, sems, a):
        x, y, c, me, chips = _place()
        half = ins[a].shape[1] // 2
        return _remote(ins[a].at[:, pl.ds((1 - c) * half, half), :], outs[a], sems[0].at[a], sems[1].at[a], (x, y, 1 - c))

    def start(self, ins, outs, sems):
        for a in range(self.n):
            self._copy(ins, outs, sems, a).start()

    def finish(self, ins, outs, sems):
        for a in range(self.n):
            self._copy(ins, outs, sems, a).wait_recv()
        for a in range(self.n):
            self._copy(ins, outs, sems, a).wait_send()


class _OverChips(_Exchange):
    def __init__(self, pre):
        n = self.n = len(pre)
        self.n_in = self.n_out = n
        self.out_shape = [jax.ShapeDtypeStruct(p.shape, p.dtype) for p in pre]
        dma = pltpu.SemaphoreType.DMA
        self.scratch = [dma((3 * n,)), dma((3 * n,)), dma((n,))]

    def _ici(self, ins, outs, sems, a, j, chip):
        x, y, c, me, chips = _place()
        return _remote(ins[a].at[2 * chip[0] + chip[1]], outs[a].at[me], sems[0].at[3 * a + j], sems[1].at[3 * a + j],
                       (*chip, c))

    def _local(self, ins, outs, sems, a):
        me = _place()[3]
        return pltpu.make_async_copy(ins[a].at[me], outs[a].at[me], sems[2].at[a])

    def start(self, ins, outs, sems):
        chips = _place()[4]

        def issue(order):
            for a in range(self.n):
                for j in order:
                    self._ici(ins, outs, sems, a, j, chips[j]).start()

        _staggered(issue)
        for a in range(self.n):
            self._local(ins, outs, sems, a).start()

    def finish(self, ins, outs, sems):
        x, y, c, me, chips = _place()
        for a in range(self.n):
            for j, chip in enumerate(chips):
                blk = outs[a].at[2 * chip[0] + chip[1]]
                _remote(blk, blk, sems[0].at[3 * a + j], sems[1].at[3 * a + j], (x, y, c)).wait_recv()
        for a in range(self.n):
            for j, chip in enumerate(chips):
                self._ici(ins, outs, sems, a, j, chip).wait_send()
            self._local(ins, outs, sems, a).wait()


class _ShareHalves(_Exchange):
    def __init__(self, fulls):
        n = self.n = len(fulls)
        self.n_in = self.n_out = n
        self.out_shape = [jax.ShapeDtypeStruct(f.shape, f.dtype) for f in fulls]
        self.scratch = [pltpu.SemaphoreType.DMA((n,)), pltpu.SemaphoreType.DMA((n,))]
        self.aliases = {a: a for a in range(n)}

    def _copy(self, outs, sems, a, half_of):
        x, y, c, me, chips = _place()
        half = outs[a].shape[0] // 2
        rows = outs[a].at[pl.ds(half_of * half, half), :]
        return _remote(rows, rows, sems[0].at[a], sems[1].at[a], (x, y, 1 - c))

    def start(self, ins, outs, sems):
        c = _place()[2]
        for a in range(self.n):
            self._copy(outs, sems, a, c).start()

    def finish(self, ins, outs, sems):
        c = _place()[2]
        for a in range(self.n):
            self._copy(outs, sems, a, 1 - c).wait_recv()
        for a in range(self.n):
            self._copy(outs, sems, a, c).wait_send()


class _GatherBlocks(_Exchange):
    def __init__(self, block):
        self.n_in = self.n_out = 1
        self.out_shape = [jax.ShapeDtypeStruct((8,) + block.shape, block.dtype)]
        dma = pltpu.SemaphoreType.DMA
        self.scratch = [dma((7,)), dma((7,)), dma]

    @staticmethod
    def _peer(f):
        x, y, c, me, chips = _place()
        return ((1 - x) if f & 4 else x, (1 - y) if f & 2 else y, (1 - c) if f & 1 else c)

    def start(self, ins, outs, sems):
        x, y, c, me, chips = _place()
        for f in range(1, 8):
            _remote(ins[0], outs[0].at[2 * me + c], sems[0].at[f - 1], sems[1].at[f - 1], self._peer(f)).start()
        pltpu.make_async_copy(ins[0], outs[0].at[2 * me + c], sems[2]).start()

    def finish(self, ins, outs, sems):
        x, y, c, me, chips = _place()
        for f in range(1, 8):
            px, py, pc = self._peer(f)
            blk = outs[0].at[4 * px + 2 * py + pc]
            _remote(blk, blk, sems[0].at[f - 1], sems[1].at[f - 1], (x, y, c)).wait_recv()
        for f in range(1, 8):
            _remote(ins[0], outs[0].at[2 * me + c], sems[0].at[f - 1], sems[1].at[f - 1], self._peer(f)).wait_send()
        pltpu.make_async_copy(ins[0], outs[0].at[2 * me + c], sems[2]).wait()


class _Both(_Exchange):
    def __init__(self, first, second):
        self.parts = (first, second)
        self.n_in, self.n_out = first.n_in + second.n_in, first.n_out + second.n_out
        self.out_shape = first.out_shape + second.out_shape
        self.scratch = first.scratch + second.scratch
        self.aliases = dict(first.aliases)
        self.aliases.update({first.n_in + i: first.n_out + o for i, o in second.aliases.items()})

    def _split(self, ins, outs, sems):
        a, b = self.parts
        return ((a, ins[:a.n_in], outs[:a.n_out], sems[:len(a.scratch)]),
                (b, ins[a.n_in:], outs[a.n_out:], sems[len(a.scratch):]))

    def start(self, ins, outs, sems):
        for ex, i, o, s in self._split(ins, outs, sems):
            ex.start(i, o, s)

    def middle(self, ins, outs, sems):
        for ex, i, o, s in self._split(ins, outs, sems):
            ex.middle(i, o, s)

    def finish(self, ins, outs, sems):
        for ex, i, o, s in self._split(ins, outs, sems):
            ex.finish(i, o, s)


class _Bound:
    def __init__(self, ex, ins, outs, sems):
        self.start = lambda: ex.start(ins, outs, sems)
        self.middle = lambda: ex.middle(ins, outs, sems)
        self.finish = lambda: ex.finish(ins, outs, sems)


def _carry(name, body, ex, ex_args, args, in_specs, out_specs, out_shape, scratch_shapes=(), grid=None, semantics=()):
    n_a, n_o, n_s = len(args), len(out_shape), len(scratch_shapes)

    def full_body(*refs):
        p = 0
        groups = []
        for size in (n_a, ex.n_in, n_o, ex.n_out, n_s, len(ex.scratch)):
            groups.append(refs[p:p + size])
            p += size
        a, ei, o, eo, s, es = groups
        body(*a, *o, *s, _Bound(ex, ei, eo, es))

    kwargs = {} if grid is None else {"grid": grid}
    outs = pl.pallas_call(
        full_body, name=name,
        in_specs=list(in_specs) + [ANY] * ex.n_in, out_specs=list(out_specs) + [ANY] * ex.n_out,
        out_shape=list(out_shape) + list(ex.out_shape), scratch_shapes=list(scratch_shapes) + list(ex.scratch),
        input_output_aliases={n_a + i: n_o + o for i, o in ex.aliases.items()},
        compiler_params=_params(*semantics) if semantics else pltpu.CompilerParams(vmem_limit_bytes=VMEM_LIMIT),
        **kwargs,
    )(*args, *ex_args)
    return outs[:n_o], outs[n_o:]


def _exchange_alone(name, ex, ex_args):
    def body(xc):
        xc.start()
        xc.middle()
        xc.finish()

    return _carry(name, body, ex, ex_args, (), (), (), ())[1]


def _core_index():
    return lax.axis_index("c").astype(jnp.int32).reshape(1)


def _pair_sum(g, got):
    _, r, cc = g.shape
    half = r // 2
    tr = half // 2

    def body(c_ref, g_ref, got_ref, out_ref):
        out_ref[...] = (g_ref[...].astype(F32) + got_ref[...].astype(F32)).astype(BF16)

    return pl.pallas_call(
        body, name=f"pair_sum_{r}x{cc}",
        grid_spec=pltpu.PrefetchScalarGridSpec(
            num_scalar_prefetch=1, grid=(N_CHIP, 2),
            in_specs=[pl.BlockSpec((None, tr, cc), lambda k, i, c_ref: (k, 2 * c_ref[0] + i, 0)),
                      pl.BlockSpec((None, tr, cc), lambda k, i, c_ref: (k, i, 0))],
            out_specs=pl.BlockSpec((None, tr, cc), lambda k, i, c_ref: (k, i, 0))),
        out_shape=jax.ShapeDtypeStruct((N_CHIP, half, cc), BF16),
        compiler_params=_params("parallel", "parallel"),
    )(_core_index(), g, got)


def _chip_sum(parts):
    _, half, cc = parts.shape
    tr = half // 2

    def body(c_ref, p_ref, out_ref):
        out_ref[...] = ((p_ref[0].astype(F32) + p_ref[1].astype(F32)) + p_ref[2].astype(F32)) + p_ref[3].astype(F32)

    return pl.pallas_call(
        body, name=f"chip_sum_{half}x{cc}",
        grid_spec=pltpu.PrefetchScalarGridSpec(
            num_scalar_prefetch=1, grid=(2,),
            in_specs=[pl.BlockSpec((N_CHIP, tr, cc), lambda i, c_ref: (0, i, 0))],
            out_specs=pl.BlockSpec((tr, cc), lambda i, c_ref: (2 * c_ref[0] + i, 0))),
        out_shape=jax.ShapeDtypeStruct((2 * half, cc), F32),
        compiler_params=_params("parallel"),
    )(_core_index(), parts)


def _adamw_math(w, g, m, v):
    m = ADAM_B1 * m + (1.0 - ADAM_B1) * g
    v = ADAM_B2 * v + (1.0 - ADAM_B2) * (g * g)
    m_hat = m / (1.0 - ADAM_B1 ** ADAM_STEP)
    v_hat = v / (1.0 - ADAM_B2 ** ADAM_STEP)
    delta = -ADAM_LR * (m_hat / (jnp.sqrt(v_hat) + ADAM_EPS) + ADAM_WD * w)
    return delta, m, v


def _adamw(w, g, m, v):
    r, cc = w.shape
    tr = r // 4

    def body(w_ref, g_ref, m_ref, v_ref, go_ref, d_ref, nm_ref, nv_ref):
        g = g_ref[...]
        go_ref[...] = g
        d_ref[...], nm_ref[...], nv_ref[...] = _adamw_math(w_ref[...], g, m_ref[...], v_ref[...])

    blk = pl.BlockSpec((tr, cc), lambda i: (i, 0))
    return pl.pallas_call(
        body, grid=(4,), name=f"adamw_{r}x{cc}",
        in_specs=[blk] * 4, out_specs=[blk] * 4,
        out_shape=[jax.ShapeDtypeStruct((r, cc), F32)] * 4,
        compiler_params=_params("parallel"),
    )(w, g, m, v)


def _pack8(rows):
    def body(*refs):
        out_ref = refs[-1]
        out_ref[...] = jnp.zeros_like(out_ref)
        for i, r in enumerate(refs[:-1]):
            out_ref[i:i + 1, :] = r[...]

    return pl.pallas_call(body, name="pack8", out_shape=jax.ShapeDtypeStruct((8, D), F32))(*rows)


def _adamw_gains(gall, w8, m8, v8):
    def body(ga_ref, w_ref, m_ref, v_ref, g_ref, d_ref, nm_ref, nv_ref):
        g = ga_ref[0]
        for dev in range(1, 8):
            g = g + ga_ref[dev]
        g_ref[...] = g
        d_ref[...], nm_ref[...], nv_ref[...] = _adamw_math(w_ref[...], g, m_ref[...], v_ref[...])

    return pl.pallas_call(
        body, name="adamw_gains",
        out_shape=[jax.ShapeDtypeStruct((8, D), F32)] * 4,
    )(gall, w8, m8, v8)


def kernel(x, positions, w_in, w_out, g_pre_mix, g_post_mix, g_pre_ffn, g_post_ffn, w_gate, w_up, w_down, loss_target, m_w_in, m_w_out, m_g_pre_mix, m_g_post_mix, m_g_pre_ffn, m_g_post_ffn, m_w_gate, m_w_up, m_w_down, v_w_in, v_w_out, v_g_pre_mix, v_g_post_mix, v_g_pre_ffn, v_g_post_ffn, v_w_gate, v_w_up, v_w_down):
    tr = lambda t: jnp.swapaxes(t, 1, 2)[0]
    shards = [w_in[0], w_out[0], tr(w_gate), tr(w_up), w_down[0]]
    moms = [m_w_in[0], m_w_out[0], tr(m_w_gate), tr(m_w_up), m_w_down[0]]
    vels = [v_w_in[0], v_w_out[0], tr(v_w_gate), tr(v_w_up), v_w_down[0]]
    xs, pos, tgt = x[0], positions.reshape(S, 1), loss_target[0]
    g1, g2, g3, g4 = g_pre_mix, g_post_mix, g_pre_ffn, g_post_ffn
    tabs = tuple(jnp.asarray(t) for t in _retention_tables())
    ifr, ifa = (jnp.asarray(t) for t in _rotary_tables())
    bf = [s.astype(BF16) for s in shards]

    win_g, wout_g = _exchange_alone("gather_in", _GatherShards(bf[:2]), bf[:2])
    wout_g = wout_g.reshape(D, D)
    proj, h1 = _proj_fwd(xs, g1, win_g)
    qr, kr, rv, aq, ak, av = _rot_fwd(proj, pos, ifr, ifa)
    o_raw, cat_r, states = _ret_fwd(qr, kr, rv, proj, tabs)
    (att_out, lse, cat_a), (wg_g, wu_g, wd_g) = _att_fwd(aq, ak, av, _GatherShards(bf[2:]), bf[2:])
    mix, x2, h3 = _mix_fwd(cat_r, cat_a, wout_g, xs, g2, g3)
    gt, up, a, f = _ffn_fwd(h3, wg_g, wu_g, wd_g)

    sq, dy, df, dg4 = _head_bwd(f, x2, tgt, g4)
    loss = 0.5 * lax.psum(sq[0, 0], ("x", "y", "c")) / D
    dgt, dup, dh3 = _ffn_bwd_act(df, gt, up, wg_g, wu_g, wd_g)
    ffn_grads = list(_ffn_bwd_w(a, df, h3, dgt, dup))
    (dx2, dmix, dg3, dg2), got = _norm_bwd(dh3, dy, x2, mix, g2, g3, _HalvesToSibling(ffn_grads), ffn_grads)
    pre = [_pair_sum(g, r) for g, r in zip(ffn_grads, got)]
    dret, datt, dwout = _mix_bwd(dmix, cat_r, cat_a, wout_g)
    (dq_att, dk_att, dv_att), parts = _att_bwd(aq, ak, av, datt, att_out, lse, _OverChips(pre), pre)
    sums = [_chip_sum(p) for p in parts]
    (dqr, dkr, drv, drg), ffn_full = _ret_bwd(qr, kr, rv, proj, o_raw, states, dret, tabs, _ShareHalves(sums), sums)
    dproj = _rot_bwd(pos, ifr, ifa, dqr, dkr, drv, drg, dq_att, dk_att, dv_att)
    in_grads = [_win_bwd_w(h1, dproj), dwout.reshape(N_CHIP, WOUT_R, D)]
    (dx, dg1), got = _in_bwd(dproj, win_g, xs, dx2, g1, _HalvesToSibling(in_grads), in_grads)

    pre = [_pair_sum(g, r) for g, r in zip(in_grads, got)]
    gblock = _pack8([dg1, dg2, dg3, dg4])
    *parts, gall = _exchange_alone("reduce_rest", _Both(_OverChips(pre), _GatherBlocks(gblock)), pre + [gblock])
    sums = [_chip_sum(p) for p in parts]
    in_full = _exchange_alone("share_rest", _ShareHalves(sums), sums)

    full = [in_full[0], in_full[1], ffn_full[1], ffn_full[2], ffn_full[0]]
    upd = [_adamw(w, g, m, v) for w, g, m, v in zip(shards, full, moms, vels)]
    gg, gd, gm, gv = _adamw_gains(gall, _pack8([g1, g2, g3, g4]),
                                  _pack8([m_g_pre_mix, m_g_post_mix, m_g_pre_ffn, m_g_post_ffn]),
                                  _pack8([v_g_pre_mix, v_g_post_mix, v_g_pre_ffn, v_g_post_ffn]))

    def order(mats, vecs):
        back = lambda t: jnp.swapaxes(t[None], 1, 2)
        return ([mats[0][None], mats[1][None]] + [vecs[i:i + 1] for i in range(4)]
                + [back(mats[2]), back(mats[3]), mats[4][None]])

    return (loss, dx[None],
            *order([u[0] for u in upd], gg),
            *order([u[1] for u in upd], gd),
            *order([u[2] for u in upd], gm),
            *order([u[3] for u in upd], gv))
```

```python
import functools

import numpy as np
import jax
import jax.numpy as jnp
from jax import lax
from jax.experimental import pallas as pl
from jax.experimental.pallas import tpu as pltpu

F32, BF16 = jnp.float32, jnp.bfloat16
MESH = pl.DeviceIdType.MESH

S = 2048
D = 1024
PW = 3072
N_CHIP = 4
WIN_C = PW // N_CHIP
DFF = 2816
FF_C = DFF // N_CHIP
WOUT_R = D // N_CHIP
RMS_EPS = 1e-6
GN_EPS = 1e-5
RET_C = 128
RET_SCALE = 32 ** -0.5
ATT_BLK = 128
ATT_SCALE = 64 ** -0.5
PATTERN_DILATIONS = (1, 4, 16)
NEG = -1e30
VMEM_LIMIT = 56 * 1024 * 1024

ADAM_LR, ADAM_B1, ADAM_B2, ADAM_EPS, ADAM_WD, ADAM_STEP = 0.001, 0.9, 0.999, 1e-08, 0.01, 10


def _params(*sem):
    return pltpu.CompilerParams(dimension_semantics=sem, vmem_limit_bytes=VMEM_LIMIT)


def _nt(a, b):
    return lax.dot_general(a, b, (((1,), (1,)), ((), ())), preferred_element_type=F32)


def _tn(a, b):
    return lax.dot_general(a, b, (((0,), (0,)), ((), ())), preferred_element_type=F32)


def _nn(a, b):
    return jnp.dot(a, b, preferred_element_type=F32)


def _rstd(v):
    return lax.rsqrt(jnp.mean(v * v, axis=-1, keepdims=True) + RMS_EPS)


def _sigmoid(v):
    return 1.0 / (1.0 + jnp.exp(-v))


def _rows(i, t):
    return pl.ds(pl.multiple_of(i * t, t), t)


def _retention_tables():
    h = np.arange(8, dtype=np.float32)
    log_g = np.log1p(-np.exp2(-5.0 - h)).astype(np.float32)
    idx = np.arange(RET_C, dtype=np.float32)
    diff = idx[:, None] - idx[None, :]
    dtab = np.where(diff >= 0, np.exp(log_g[:, None, None] * np.maximum(diff, 0.0)), 0.0).astype(np.float32)
    dtab = dtab.reshape(8 * RET_C, RET_C)
    lane_head = np.arange(256) // 32
    a_tab = np.exp(log_g[lane_head][None, :] * (idx + 1.0)[:, None]).astype(np.float32)
    b_tab = np.exp(log_g[lane_head][None, :] * (RET_C - 1.0 - idx)[:, None]).astype(np.float32)
    lam = np.exp(log_g[lane_head] * RET_C).astype(np.float32)[:, None]
    bd = (lane_head[:, None] == (np.arange(512) // 64)[None, :]).astype(np.float32)
    return dtab, a_tab, b_tab, lam, bd


def _rotary_tables():
    inv_r = (1.0 / (np.float32(10000.0) ** np.linspace(0.0, 1.0, 16, dtype=np.float32))).astype(np.float32)
    inv_a = (np.float32(500000.0) ** (-np.arange(0, 16, 2, dtype=np.float32) / np.float32(16))).astype(np.float32)
    ifc = np.zeros((1, 128), np.float32)
    ifc[0, 0:16], ifc[0, 16:24] = inv_r, inv_a
    spread = np.zeros((128, 768), np.float32)
    for lane in range(256):
        spread[(lane % 32) % 16, lane] = 1.0
    for lane in range(512):
        d = lane % 64
        spread[16 + d % 8 if d < 16 else 24, 256 + lane] = 1.0
    return ifc, spread


def _proj_fwd(x, g1, win_g):
    tm = 256

    def body(x_ref, g_ref, w_ref, proj_ref, h_ref):
        xv = x_ref[...]
        h = (xv * _rstd(xv) * g_ref[...]).astype(BF16)
        h_ref[...] = h
        for k in range(N_CHIP):
            proj_ref[:, k * WIN_C:(k + 1) * WIN_C] = _nn(h, w_ref[k])

    return pl.pallas_call(
        body, grid=(S // tm,), name="proj_fwd",
        in_specs=[pl.BlockSpec((tm, D), lambda i: (i, 0)), pl.BlockSpec((1, D), lambda i: (0, 0)),
                  pl.BlockSpec((N_CHIP, D, WIN_C), lambda i: (0, 0, 0))],
        out_specs=[pl.BlockSpec((tm, PW), lambda i: (i, 0)), pl.BlockSpec((tm, D), lambda i: (i, 0))],
        out_shape=[jax.ShapeDtypeStruct((S, PW), F32), jax.ShapeDtypeStruct((S, D), BF16)],
        compiler_params=_params("parallel"),
    )(x, g1, win_g)


def _rot_halves(tm):
    lo_r = (lax.broadcasted_iota(jnp.int32, (tm, 256), 1) % 32) < 16
    lo_a = (lax.broadcasted_iota(jnp.int32, (tm, 512), 1) % 64) < 8
    return lo_r, lo_a


def _spread_exact(t, e):
    hi = t.astype(BF16)
    r1 = t - hi.astype(F32)
    mid = r1.astype(BF16)
    lo = (r1 - mid.astype(F32)).astype(BF16)
    return _nn(hi, e) + _nn(mid, e) + _nn(lo, e)


def _rot_fwd(proj, pos, ifc, spread):
    tm = 256

    def body(p_ref, pos_ref, ifc_ref, e_ref, qr_ref, kr_ref, rv_ref, aq_ref, ak_ref, av_ref, cos_ref, sin_ref):
        ang = pos_ref[...].astype(F32) * ifc_ref[...]
        cs = _spread_exact(jnp.cos(ang), e_ref[...])
        sn = _spread_exact(jnp.sin(ang), e_ref[...])
        cos_ref[...] = cs
        sin_ref[...] = sn
        cr, ca, sr, sa = cs[:, 0:256], cs[:, 256:768], sn[:, 0:256], sn[:, 256:768]
        lo_r, lo_a = _rot_halves(tm)

        def rot_r(v):
            return v * cr + sr * jnp.where(lo_r, -pltpu.roll(v, 240, 1), pltpu.roll(v, 16, 1))

        def rot_a(v):
            return v * ca + sa * jnp.where(lo_a, -pltpu.roll(v, 504, 1), pltpu.roll(v, 8, 1))

        qr_ref[...] = rot_r(p_ref[:, 0:256]).astype(BF16)
        kr_ref[...] = (rot_r(p_ref[:, 256:512]) * RET_SCALE).astype(BF16)
        rv_ref[...] = p_ref[:, 512:1024].astype(BF16)
        aq, ak = rot_a(p_ref[:, 1536:2048]), rot_a(p_ref[:, 2048:2560])
        for j in range(4):
            aq_ref[j] = aq[:, 128 * j:128 * j + 128]
            ak_ref[j] = ak[:, 128 * j:128 * j + 128]
            av_ref[j] = p_ref[:, 2560 + 128 * j:2560 + 128 * j + 128]

    row = lambda w: pl.BlockSpec((tm, w), lambda i: (i, 0))
    const = lambda w: pl.BlockSpec((1, w), lambda i: (0, 0))
    slab = pl.BlockSpec((4, tm, 128), lambda i: (0, i, 0))
    return pl.pallas_call(
        body, grid=(S // tm,), name="rot_fwd",
        in_specs=[row(PW), row(1), const(128), pl.BlockSpec((128, 768), lambda i: (0, 0))],
        out_specs=[row(256), row(256), row(512), slab, slab, slab, row(768), row(768)],
        out_shape=[jax.ShapeDtypeStruct((S, w), BF16) for w in (256, 256, 512)]
                  + [jax.ShapeDtypeStruct((4, S, 128), F32)] * 3 + [jax.ShapeDtypeStruct((S, 768), F32)] * 2,
        compiler_params=_params("parallel"),
    )(proj, pos, ifc, spread)


def _seg_mean(v):
    lo = lax.broadcasted_iota(jnp.int32, v.shape, 1) < 64
    s_lo = jnp.sum(jnp.where(lo, v, 0.0), axis=-1, keepdims=True)
    s_hi = jnp.sum(jnp.where(lo, 0.0, v), axis=-1, keepdims=True)
    return jnp.where(lo, s_lo, s_hi) * (1.0 / 64.0)


def _ret_fwd(qr, kr, rv, proj, tabs):
    C = RET_C
    dtab, a_tab, b_tab, lam, bd = tabs

    def body(q_ref, k_ref, v_ref, g_ref, dt_ref, a_ref, b_ref, lam_ref, bd_ref, o_ref, cat_ref, st_ref, R):
        @pl.when(pl.program_id(0) == 0)
        def _():
            R[...] = jnp.zeros_like(R)

        q, k, v = q_ref[...], k_ref[...], v_ref[...]
        lane_head = lax.broadcasted_iota(jnp.int32, (C, 256), 1) // 32
        col_head = lax.broadcasted_iota(jnp.int32, (C, 256), 1) // 64
        rb = R[...].astype(BF16)
        st_ref[...] = rb
        qa = (q.astype(F32) * a_ref[...]).astype(BF16)
        cross = _nn(qa, rb)
        p = (_nt(_stack_heads(q, lane_head, n=8), k) * dt_ref[...]).astype(BF16)
        og = [cross[:, 256 * g:256 * g + 256]
              + _unstack_heads(_nn(p[4 * C * g:4 * C * (g + 1)], v[:, 256 * g:256 * g + 256]), col_head)
              for g in range(2)]
        kb = (k.astype(F32) * b_ref[...]).astype(BF16)
        R[...] = R[...] * lam_ref[...] + _tn(kb, v) * bd_ref[...]
        o_ref[:, 0:256] = og[0]
        o_ref[:, 256:512] = og[1]
        for j in range(4):
            oj = og[j // 2][:, 128 * (j % 2):128 * (j % 2) + 128]
            xc = oj - _seg_mean(oj)
            rn = xc * lax.rsqrt(_seg_mean(xc * xc) + GN_EPS)
            gj = g_ref[:, 128 * j:128 * j + 128]
            cat_ref[:, 128 * j:128 * j + 128] = (rn * (gj * _sigmoid(gj))).astype(BF16)

    row = lambda w: pl.BlockSpec((C, w), lambda n: (n, 0))
    full = lambda a: pl.BlockSpec(a.shape, lambda n: (0,) * a.ndim)
    return pl.pallas_call(
        body, grid=(S // C,), name="ret_fwd",
        in_specs=[row(256), row(256), row(512), pl.BlockSpec((C, 512), lambda n: (n, 2)),
                  full(dtab), full(a_tab), full(b_tab), full(lam), full(bd)],
        out_specs=[row(512), row(512), pl.BlockSpec((None, 256, 512), lambda n: (n, 0, 0))],
        out_shape=[jax.ShapeDtypeStruct((S, 512), F32), jax.ShapeDtypeStruct((S, 512), BF16),
                   jax.ShapeDtypeStruct((S // C, 256, 512), BF16)],
        scratch_shapes=[pltpu.VMEM((256, 512), F32)],
        compiler_params=_params("arbitrary"),
    )(qr, kr, rv, proj, dtab, a_tab, b_tab, lam, bd)


def _stack_heads(v, lane_head, fill=0.0, n=4):
    return jnp.concatenate([jnp.where(lane_head == h, v, jnp.full_like(v, fill)) for h in range(n)], axis=0)


def _unstack_heads(v, lane_head, n=4):
    out = v[0:ATT_BLK]
    for h in range(1, n):
        out = jnp.where(lane_head == h, v[h * ATT_BLK:(h + 1) * ATT_BLK], out)
    return out


def _att_mask(ib, has_prev):
    nk = 2 * ATT_BLK if has_prev else ATT_BLK
    a = lax.broadcasted_iota(jnp.int32, (4 * ATT_BLK, nk), 0) % ATT_BLK
    kk = lax.broadcasted_iota(jnp.int32, (4 * ATT_BLK, nk), 1)
    if has_prev:
        dist = ATT_BLK + a - kk
        return (dist >= 0) & (dist <= ATT_BLK) & ((ib * ATT_BLK - ATT_BLK + kk) >= 0)
    return (a - kk) >= 0


def _class_rows(ib, r, d):
    if d == 1:
        return pl.ds(pl.multiple_of(ib * ATT_BLK, ATT_BLK), ATT_BLK)
    return pl.ds(ib * ATT_BLK * d + r, ATT_BLK, stride=d)


def _slab_pair(ref, g, rows):
    return jnp.concatenate([ref[2 * g, rows, :], ref[2 * g + 1, rows, :]], axis=1)


def _att_blocks(d):
    nb = S // d // ATT_BLK
    return nb, nb > 1


def _att_fwd(aq, ak, av, exchange, exchange_args):
    def body(q_ref, k_ref, v_ref, o_ref, l_ref, cat_ref, xc):
        xc.start()
        lane_head = lax.broadcasted_iota(jnp.int32, (ATT_BLK, 256), 1) // 64
        for pi, d in enumerate(PATTERN_DILATIONS):
            if pi == len(PATTERN_DILATIONS) - 1:
                xc.middle()
            nb, has_prev = _att_blocks(d)

            def block(b, carry, pi=pi, d=d, nb=nb, has_prev=has_prev):
                r, ib = b // nb, b % nb
                rows = _class_rows(ib, r, d)
                prow = _class_rows(jnp.maximum(ib - 1, 0), r, d)
                valid = _att_mask(ib, has_prev)
                for g in range(2):
                    qg = _slab_pair(q_ref, g, rows).astype(BF16)
                    kg = _slab_pair(k_ref, g, rows)
                    vg = _slab_pair(v_ref, g, rows)
                    if has_prev:
                        kg = jnp.concatenate([_slab_pair(k_ref, g, prow), kg], axis=0)
                        vg = jnp.concatenate([_slab_pair(v_ref, g, prow), vg], axis=0)
                    kg, vg = kg.astype(BF16), vg.astype(BF16)
                    s = jnp.where(valid, _nt(_stack_heads(qg, lane_head), kg) * ATT_SCALE, NEG)
                    m = jnp.max(s, axis=-1, keepdims=True)
                    p = jnp.exp(s - m)
                    den = jnp.sum(p, axis=-1, keepdims=True)
                    og = _unstack_heads(_nn(p.astype(BF16), vg) / den, lane_head)
                    lg = _unstack_heads(jnp.broadcast_to(m + jnp.log(den), (4 * ATT_BLK, 256)), lane_head)
                    for jj in range(2):
                        j = 2 * g + jj
                        o_new, l_new = og[:, 128 * jj:128 * jj + 128], lg[:, 128 * jj:128 * jj + 128]
                        if pi > 0:
                            o_old, l_old = o_ref[j, rows, :], l_ref[j, rows, :]
                            mx = jnp.maximum(l_old, l_new)
                            ea, eb = jnp.exp(l_old - mx), jnp.exp(l_new - mx)
                            den = ea + eb
                            o_new = (ea * o_old + eb * o_new) / den
                            l_new = mx + jnp.log(den)
                        o_ref[j, rows, :] = o_new
                        l_ref[j, rows, :] = l_new
                return carry

            lax.fori_loop(0, S // ATT_BLK, block, 0)

        def to_cat(i, carry):
            rows = _rows(i, 256)
            for j in range(4):
                cat_ref[rows, 128 * j:128 * j + 128] = o_ref[j, rows, :].astype(BF16)
            return carry

        lax.fori_loop(0, S // 256, to_cat, 0)
        xc.finish()

    slab = jax.ShapeDtypeStruct((4, S, 128), F32)
    return _carry("att_fwd", body, exchange, exchange_args, (aq, ak, av), [VMEM] * 3, [VMEM] * 3,
                  [slab, slab, jax.ShapeDtypeStruct((S, 512), BF16)])


def _mix_fwd(cat_r, cat_a, wout, x, g2, g3):
    tm = 256

    def body(cr_ref, ca_ref, w_ref, x_ref, g2_ref, g3_ref, mix_ref, x2_ref, h3_ref):
        mix = _nn(cr_ref[...], w_ref[0:512, :]) + _nn(ca_ref[...], w_ref[512:1024, :])
        mix_ref[...] = mix
        x2 = x_ref[...] + mix * _rstd(mix) * g2_ref[...]
        x2_ref[...] = x2
        h3_ref[...] = (x2 * _rstd(x2) * g3_ref[...]).astype(BF16)

    row = lambda w: pl.BlockSpec((tm, w), lambda i: (i, 0))
    vec = pl.BlockSpec((1, D), lambda i: (0, 0))
    return pl.pallas_call(
        body, grid=(S // tm,), name="mix_fwd",
        in_specs=[row(512), row(512), pl.BlockSpec((D, D), lambda i: (0, 0)), row(D), vec, vec],
        out_specs=[row(D), row(D), row(D)],
        out_shape=[jax.ShapeDtypeStruct((S, D), F32), jax.ShapeDtypeStruct((S, D), F32),
                   jax.ShapeDtypeStruct((S, D), BF16)],
        compiler_params=_params("parallel"),
    )(cat_r, cat_a, wout, x, g2, g3)


def _ffn_fwd(h3, wg, wu, wd):
    tm = 256

    def body(h_ref, wg_ref, wu_ref, wd_ref, gt_ref, up_ref, a_ref, f_ref):
        k, i = pl.program_id(0), pl.program_id(1)
        h = h_ref[...]
        gt = _nt(h, wg_ref[...])
        up = _nt(h, wu_ref[...])
        gt_ref[...] = gt.astype(BF16)
        up_ref[...] = up.astype(BF16)
        a = (gt * _sigmoid(gt) * up).astype(BF16)
        a_ref[...] = a
        part = _nn(a, wd_ref[...])
        rows = _rows(i, tm)

        @pl.when(k == 0)
        def _():
            f_ref[rows, :] = part

        @pl.when(k > 0)
        def _():
            f_ref[rows, :] = f_ref[rows, :] + part

    wrow = pl.BlockSpec((None, FF_C, D), lambda k, i: (k, 0, 0))
    act = pl.BlockSpec((None, tm, FF_C), lambda k, i: (k, i, 0))
    return pl.pallas_call(
        body, grid=(N_CHIP, S // tm), name="ffn_fwd",
        in_specs=[pl.BlockSpec((tm, D), lambda k, i: (i, 0)), wrow, wrow, wrow],
        out_specs=[act, act, act, pl.BlockSpec((S, D), lambda k, i: (0, 0))],
        out_shape=[jax.ShapeDtypeStruct((N_CHIP, S, FF_C), BF16)] * 3 + [jax.ShapeDtypeStruct((S, D), F32)],
        compiler_params=_params("arbitrary", "arbitrary"),
    )(h3, wg, wu, wd)


def _head_bwd(f, x2, tgt, g4):
    tm = 256

    def body(f_ref, x2_ref, t_ref, g_ref, loss_ref, dy_ref, df_ref, dg_ref):
        @pl.when(pl.program_id(0) == 0)
        def _():
            loss_ref[...] = jnp.zeros_like(loss_ref)
            dg_ref[...] = jnp.zeros_like(dg_ref)

        fv = f_ref[...]
        r = _rstd(fv)
        fn = fv * r
        e = x2_ref[...] + fn * g_ref[...] - t_ref[...]
        sq = jnp.sum(jnp.sum(e * e, axis=-1, keepdims=True), axis=0, keepdims=True)
        loss_ref[...] = loss_ref[...] + sq
        dy = e * (1.0 / D)
        dy_ref[...] = dy
        dg_ref[...] = dg_ref[...] + jnp.sum(dy * fn, axis=0, keepdims=True)
        t = dy * g_ref[...]
        df_ref[...] = (r * (t - fn * jnp.mean(t * fn, axis=-1, keepdims=True))).astype(BF16)

    row = pl.BlockSpec((tm, D), lambda i: (i, 0))
    vec = pl.BlockSpec((1, D), lambda i: (0, 0))
    return pl.pallas_call(
        body, grid=(S // tm,), name="head_bwd",
        in_specs=[row, row, row, vec],
        out_specs=[pl.BlockSpec((8, 128), lambda i: (0, 0)), row, row, vec],
        out_shape=[jax.ShapeDtypeStruct((8, 128), F32), jax.ShapeDtypeStruct((S, D), F32),
                   jax.ShapeDtypeStruct((S, D), BF16), jax.ShapeDtypeStruct((1, D), F32)],
        compiler_params=_params("arbitrary"),
    )(f, x2, tgt, g4)


def _ffn_bwd_act(df, gt, up, wg, wu, wd):
    tm = 512

    def body(df_ref, gt_ref, up_ref, wg_ref, wu_ref, wd_ref, dgt_ref, dup_ref, dh_ref):
        k = pl.program_id(1)
        da = _nt(df_ref[...], wd_ref[...])
        gt, up = gt_ref[...].astype(F32), up_ref[...].astype(F32)
        sg = _sigmoid(gt)
        dup = (da * gt * sg).astype(BF16)
        dgt = (da * up * (sg * (1.0 + gt * (1.0 - sg)))).astype(BF16)
        dup_ref[...] = dup
        dgt_ref[...] = dgt
        part = _nn(dgt, wg_ref[...]) + _nn(dup, wu_ref[...])

        @pl.when(k == 0)
        def _():
            dh_ref[...] = part

        @pl.when(k > 0)
        def _():
            dh_ref[...] = dh_ref[...] + part

    wrow = pl.BlockSpec((None, FF_C, D), lambda i, k: (k, 0, 0))
    act = pl.BlockSpec((None, tm, FF_C), lambda i, k: (k, i, 0))
    row = pl.BlockSpec((tm, D), lambda i, k: (i, 0))
    return pl.pallas_call(
        body, grid=(S // tm, N_CHIP), name="ffn_bwd_act",
        in_specs=[row, act, act, wrow, wrow, wrow],
        out_specs=[act, act, row],
        out_shape=[jax.ShapeDtypeStruct((N_CHIP, S, FF_C), BF16), jax.ShapeDtypeStruct((N_CHIP, S, FF_C), BF16),
                   jax.ShapeDtypeStruct((S, D), F32)],
        compiler_params=_params("parallel", "arbitrary"),
    )(df, gt, up, wg, wu, wd)


def _ffn_bwd_w(a, df, h3, dgt, dup):
    tm = 1024
    assert S // tm == 2

    def body(a_ref, df_ref, h_ref, dgt_ref, dup_ref, dwd_ref, dwg_ref, dwu_ref, acc_d, acc_g, acc_u):
        i = pl.program_id(1)
        h = h_ref[...]
        parts = (_tn(a_ref[...], df_ref[...]), _tn(dgt_ref[...], h), _tn(dup_ref[...], h))

        @pl.when(i == 0)
        def _():
            for acc, part in zip((acc_d, acc_g, acc_u), parts):
                acc[...] = part

        @pl.when(i == S // tm - 1)
        def _():
            for out, acc, part in zip((dwd_ref, dwg_ref, dwu_ref), (acc_d, acc_g, acc_u), parts):
                out[...] = (acc[...] + part).astype(BF16)

    act = pl.BlockSpec((None, tm, FF_C), lambda k, i: (k, i, 0))
    row = pl.BlockSpec((tm, D), lambda k, i: (i, 0))
    wrow = pl.BlockSpec((None, FF_C, D), lambda k, i: (k, 0, 0))
    return pl.pallas_call(
        body, grid=(N_CHIP, S // tm), name="ffn_bwd_w",
        in_specs=[act, row, row, act, act],
        out_specs=[wrow, wrow, wrow],
        out_shape=[jax.ShapeDtypeStruct((N_CHIP, FF_C, D), BF16)] * 3,
        scratch_shapes=[pltpu.VMEM((FF_C, D), F32)] * 3,
        compiler_params=_params("parallel", "arbitrary"),
    )(a, df, h3, dgt, dup)


def _norm_bwd(dh3, dy, x2, mix, g2, g3, exchange, exchange_args):
    tm = 256

    def body(dh_ref, dy_ref, x2_ref, mix_ref, g2_ref, g3_ref, dx2_ref, dmix_ref, dg3_ref, dg2_ref, xc):
        @pl.when(pl.program_id(0) == 0)
        def _():
            xc.start()
            dg3_ref[...] = jnp.zeros_like(dg3_ref)
            dg2_ref[...] = jnp.zeros_like(dg2_ref)

        x2 = x2_ref[...]
        r3 = _rstd(x2)
        xn = x2 * r3
        dh = dh_ref[...]
        dg3_ref[...] = dg3_ref[...] + jnp.sum(dh * xn, axis=0, keepdims=True)
        t = dh * g3_ref[...]
        dx2 = dy_ref[...] + r3 * (t - xn * jnp.mean(t * xn, axis=-1, keepdims=True))
        dx2_ref[...] = dx2
        mix = mix_ref[...]
        r2 = _rstd(mix)
        mn = mix * r2
        dg2_ref[...] = dg2_ref[...] + jnp.sum(dx2 * mn, axis=0, keepdims=True)
        u = dx2 * g2_ref[...]
        dmix_ref[...] = (r2 * (u - mn * jnp.mean(u * mn, axis=-1, keepdims=True))).astype(BF16)

        @pl.when(pl.program_id(0) == S // tm - 1)
        def _():
            xc.middle()
            xc.finish()

    row = pl.BlockSpec((tm, D), lambda i: (i, 0))
    vec = pl.BlockSpec((1, D), lambda i: (0, 0))
    return _carry("norm_bwd", body, exchange, exchange_args, (dh3, dy, x2, mix, g2, g3),
                  [row, row, row, row, vec, vec], [row, row, vec, vec],
                  [jax.ShapeDtypeStruct((S, D), F32), jax.ShapeDtypeStruct((S, D), BF16),
                   jax.ShapeDtypeStruct((1, D), F32), jax.ShapeDtypeStruct((1, D), F32)],
                  grid=(S // tm,), semantics=("arbitrary",))


def _mix_bwd(dmix, cat_r, cat_a, wout):
    tm = 512

    def body(dm_ref, cr_ref, ca_ref, w_ref, dret_ref, datt_ref, dw_ref, acc):
        i = pl.program_id(0)

        @pl.when(i == 0)
        def _():
            acc[...] = jnp.zeros_like(acc)

        dm = dm_ref[...]
        dret_ref[...] = _nt(dm, w_ref[0:512, :])
        datt = _nt(dm, w_ref[512:1024, :])
        for j in range(4):
            datt_ref[j] = datt[:, 128 * j:128 * j + 128]
        acc[0:512, :] += _tn(cr_ref[...], dm)
        acc[512:1024, :] += _tn(ca_ref[...], dm)

        @pl.when(i == S // tm - 1)
        def _():
            dw_ref[...] = acc[...].astype(BF16)

    row = lambda w: pl.BlockSpec((tm, w), lambda i: (i, 0))
    full = pl.BlockSpec((D, D), lambda i: (0, 0))
    return pl.pallas_call(
        body, grid=(S // tm,), name="mix_bwd",
        in_specs=[row(D), row(512), row(512), full],
        out_specs=[row(512), pl.BlockSpec((4, tm, 128), lambda i: (0, i, 0)), full],
        out_shape=[jax.ShapeDtypeStruct((S, 512), F32), jax.ShapeDtypeStruct((4, S, 128), F32),
                   jax.ShapeDtypeStruct((D, D), BF16)],
        scratch_shapes=[pltpu.VMEM((D, D), F32)],
        compiler_params=_params("arbitrary"),
    )(dmix, cat_r, cat_a, wout)


def _att_bwd(aq, ak, av, datt, att_out, lse, exchange, exchange_args):
    def body(q_ref, k_ref, v_ref, do_ref, out_ref, l_ref, dq_ref, dk_ref, dv_ref, xc):
        xc.start()

        def clear(i, carry):
            rows = _rows(i, 256)
            for ref in (dq_ref, dk_ref, dv_ref):
                for j in range(4):
                    ref[j, rows, :] = jnp.zeros((256, 128), F32)
            return carry

        lax.fori_loop(0, S // 256, clear, 0)
        lane_head = lax.broadcasted_iota(jnp.int32, (ATT_BLK, 256), 1) // 64
        for d in PATTERN_DILATIONS:
            nb, has_prev = _att_blocks(d)

            def block(b, carry, d=d, nb=nb, has_prev=has_prev):
                r, ib = b // nb, b % nb
                rows = _class_rows(ib, r, d)
                prow = _class_rows(jnp.maximum(ib - 1, 0), r, d)
                valid = _att_mask(ib, has_prev)
                for g in range(2):
                    qg = _slab_pair(q_ref, g, rows).astype(BF16)
                    kg = _slab_pair(k_ref, g, rows)
                    vg = _slab_pair(v_ref, g, rows)
                    if has_prev:
                        kg = jnp.concatenate([_slab_pair(k_ref, g, prow), kg], axis=0)
                        vg = jnp.concatenate([_slab_pair(v_ref, g, prow), vg], axis=0)
                    kg, vg = kg.astype(BF16), vg.astype(BF16)
                    dog = _slab_pair(do_ref, g, rows)
                    outg = _slab_pair(out_ref, g, rows)
                    lg = _slab_pair(l_ref, g, rows)
                    qs = _stack_heads(qg, lane_head)
                    dos = _stack_heads(dog, lane_head)
                    delta = jnp.sum(dos * jnp.concatenate([outg] * 4, axis=0), axis=-1, keepdims=True)
                    lh = jnp.max(_stack_heads(lg, lane_head, NEG), axis=-1, keepdims=True)
                    s = jnp.where(valid, _nt(qs, kg) * ATT_SCALE, NEG)
                    p = jnp.exp(s - lh)
                    dosb = dos.astype(BF16)
                    ds = (p * (_nt(dosb, vg) - delta) * ATT_SCALE).astype(BF16)
                    dq = _unstack_heads(_nn(ds, kg), lane_head)
                    dk = _tn(ds, qs)
                    dv = _tn(p.astype(BF16), dosb)
                    for jj in range(2):
                        j, sl = 2 * g + jj, slice(128 * jj, 128 * jj + 128)
                        dq_ref[j, rows, :] += dq[:, sl]
                        if has_prev:
                            dk_ref[j, prow, :] += dk[0:ATT_BLK, sl]
                            dv_ref[j, prow, :] += dv[0:ATT_BLK, sl]
                            dk_ref[j, rows, :] += dk[ATT_BLK:2 * ATT_BLK, sl]
                            dv_ref[j, rows, :] += dv[ATT_BLK:2 * ATT_BLK, sl]
                        else:
                            dk_ref[j, rows, :] += dk[:, sl]
                            dv_ref[j, rows, :] += dv[:, sl]
                return carry

            lax.fori_loop(0, S // ATT_BLK, block, 0)
        xc.middle()
        xc.finish()

    slab = jax.ShapeDtypeStruct((4, S, 128), F32)
    return _carry("att_bwd", body, exchange, exchange_args, (aq, ak, av, datt, att_out, lse), [VMEM] * 6, [VMEM] * 3,
                  [slab, slab, slab])


def _ret_bwd(qr, kr, rv, proj, o_raw, states, dret, tabs, exchange, exchange_args):
    C = RET_C
    nc = S // C
    dtab, a_tab, b_tab, lam, bd = tabs

    def body(q_ref, k_ref, v_ref, g_ref, o_ref, st_ref, dr_ref, dt_ref, a_ref, b_ref, lam_ref, bd_ref,
             dq_ref, dk_ref, dv_ref, dg_ref, dR, exch):
        @pl.when(pl.program_id(0) == 0)
        def _():
            exch.start()
            dR[...] = jnp.zeros_like(dR)

        q, k, v = q_ref[...], k_ref[...], v_ref[...]
        lane_head = lax.broadcasted_iota(jnp.int32, (C, 256), 1) // 32
        col_head = lax.broadcasted_iota(jnp.int32, (C, 256), 1) // 64
        dos = []
        for j in range(4):
            sl = slice(128 * j, 128 * j + 128)
            oj = o_ref[:, sl]
            xc = oj - _seg_mean(oj)
            rs = lax.rsqrt(_seg_mean(xc * xc) + GN_EPS)
            rn = xc * rs
            gj = g_ref[:, sl]
            sg = _sigmoid(gj)
            dret = dr_ref[:, sl]
            dg_ref[:, sl] = dret * rn * (sg * (1.0 + gj * (1.0 - sg)))
            drn = dret * (gj * sg)
            dos.append(rs * (drn - _seg_mean(drn) - rn * _seg_mean(drn * rn)))
        do = [jnp.concatenate(dos[0:2], axis=1), jnp.concatenate(dos[2:4], axis=1)]
        do8 = jnp.concatenate(do, axis=1).astype(BF16)
        drb = dR[...].astype(BF16)
        rb = st_ref[...]
        dq = _nt(do8, rb) * a_ref[...]
        dk = _nt(v, drb) * b_ref[...]
        kb = (k.astype(F32) * b_ref[...]).astype(BF16)
        dvall = _nn(kb, drb)
        qs = _stack_heads(q, lane_head, n=8)
        dec = dt_ref[...]
        p = (_nt(qs, k) * dec).astype(BF16)
        dos = [_stack_heads(do[g], col_head).astype(BF16) for g in range(2)]
        dp = jnp.concatenate([_nt(dos[g], v[:, 256 * g:256 * g + 256]) for g in range(2)], axis=0)
        ds = (dp * dec).astype(BF16)
        dq = dq + _unstack_heads(_nn(ds, k), lane_head, n=8)
        dk = dk + _tn(ds, qs)
        dv = [dvall[:, 256 * g:256 * g + 256] + _tn(p[4 * C * g:4 * C * (g + 1)], dos[g]) for g in range(2)]
        qa = (q.astype(F32) * a_ref[...]).astype(BF16)
        dR[...] = dR[...] * lam_ref[...] + _tn(qa, do8) * bd_ref[...]
        dq_ref[...] = dq
        dk_ref[...] = dk
        dv_ref[:, 0:256] = dv[0]
        dv_ref[:, 256:512] = dv[1]

        @pl.when(pl.program_id(0) == nc - 1)
        def _():
            exch.middle()
            exch.finish()

    rev = lambda w: pl.BlockSpec((C, w), lambda n: (nc - 1 - n, 0))
    full = lambda a: pl.BlockSpec(a.shape, lambda n: (0,) * a.ndim)
    return _carry(
        "ret_bwd", body, exchange, exchange_args, (qr, kr, rv, proj, o_raw, states, dret, dtab, a_tab, b_tab, lam, bd),
        [rev(256), rev(256), rev(512), pl.BlockSpec((C, 512), lambda n: (nc - 1 - n, 2)), rev(512),
         pl.BlockSpec((None, 256, 512), lambda n: (nc - 1 - n, 0, 0)), rev(512),
         full(dtab), full(a_tab), full(b_tab), full(lam), full(bd)],
        [rev(256), rev(256), rev(512), rev(512)],
        [jax.ShapeDtypeStruct((S, 256), F32), jax.ShapeDtypeStruct((S, 256), F32),
         jax.ShapeDtypeStruct((S, 512), F32), jax.ShapeDtypeStruct((S, 512), F32)],
        scratch_shapes=[pltpu.VMEM((256, 512), F32)], grid=(nc,), semantics=("arbitrary",))


def _rot_bwd(cos, sin, dqr, dkr, drv, drg, dq_att, dk_att, dv_att):
    tm = 256

    def body(cos_ref, sin_ref, dqr_ref, dkr_ref, drv_ref, drg_ref, dqa_ref, dka_ref, dva_ref, dp_ref):
        cr, ca, sr, sa = cos_ref[:, 0:256], cos_ref[:, 256:768], sin_ref[:, 0:256], sin_ref[:, 256:768]
        lo_r, lo_a = _rot_halves(tm)

        def unrot_r(g):
            gs = g * sr
            return g * cr + pltpu.roll(jnp.where(lo_r, -gs, 0.0), 16, 1) + pltpu.roll(jnp.where(lo_r, 0.0, gs), 240, 1)

        def unrot_a(g):
            gs = g * sa
            return g * ca + pltpu.roll(jnp.where(lo_a, -gs, 0.0), 8, 1) + pltpu.roll(jnp.where(lo_a, 0.0, gs), 504, 1)

        def wide(ref):
            return jnp.concatenate([ref[j] for j in range(4)], axis=1)

        dp_ref[:, 0:256] = unrot_r(dqr_ref[...]).astype(BF16)
        dp_ref[:, 256:512] = unrot_r(dkr_ref[...] * RET_SCALE).astype(BF16)
        dp_ref[:, 512:1024] = drv_ref[...].astype(BF16)
        dp_ref[:, 1024:1536] = drg_ref[...].astype(BF16)
        dp_ref[:, 1536:2048] = unrot_a(wide(dqa_ref)).astype(BF16)
        dp_ref[:, 2048:2560] = unrot_a(wide(dka_ref)).astype(BF16)
        dp_ref[:, 2560:3072] = wide(dva_ref).astype(BF16)

    row = lambda w: pl.BlockSpec((tm, w), lambda i: (i, 0))
    slab = pl.BlockSpec((4, tm, 128), lambda i: (0, i, 0))
    return pl.pallas_call(
        body, grid=(S // tm,), name="rot_bwd",
        in_specs=[row(768), row(768), row(256), row(256), row(512), row(512), slab, slab, slab],
        out_specs=row(PW), out_shape=jax.ShapeDtypeStruct((S, PW), BF16),
        compiler_params=_params("parallel"),
    )(cos, sin, dqr, dkr, drv, drg, dq_att, dk_att, dv_att)


def _win_bwd_w(h1, dproj):
    tm = 512

    def body(h_ref, dp_ref, dw_ref, acc):
        i = pl.program_id(1)

        @pl.when(i == 0)
        def _():
            acc[...] = jnp.zeros_like(acc)

        acc[...] += _tn(h_ref[...], dp_ref[...])

        @pl.when(i == S // tm - 1)
        def _():
            dw_ref[...] = acc[...].astype(BF16)

    return pl.pallas_call(
        body, grid=(N_CHIP, S // tm), name="win_bwd_w",
        in_specs=[pl.BlockSpec((tm, D), lambda k, i: (i, 0)), pl.BlockSpec((tm, WIN_C), lambda k, i: (i, k))],
        out_specs=pl.BlockSpec((None, D, WIN_C), lambda k, i: (k, 0, 0)),
        out_shape=jax.ShapeDtypeStruct((N_CHIP, D, WIN_C), BF16),
        scratch_shapes=[pltpu.VMEM((D, WIN_C), F32)],
        compiler_params=_params("parallel", "arbitrary"),
    )(h1, dproj)


def _in_bwd(dproj, win_g, x, dx2, g1, exchange, exchange_args):
    tm = 256

    def body(dp_ref, w_ref, x_ref, dx2_ref, g_ref, dx_ref, dg_ref, xc):
        @pl.when(pl.program_id(0) == 0)
        def _():
            xc.start()
            dg_ref[...] = jnp.zeros_like(dg_ref)

        dh = _nt(dp_ref[:, 0:WIN_C], w_ref[0])
        for k in range(1, N_CHIP):
            dh = dh + _nt(dp_ref[:, k * WIN_C:(k + 1) * WIN_C], w_ref[k])
        xv = x_ref[...]
        r = _rstd(xv)
        xn = xv * r
        dg_ref[...] = dg_ref[...] + jnp.sum(dh * xn, axis=0, keepdims=True)
        t = dh * g_ref[...]
        dx_ref[...] = dx2_ref[...] + r * (t - xn * jnp.mean(t * xn, axis=-1, keepdims=True))

        @pl.when(pl.program_id(0) == S // tm - 1)
        def _():
            xc.middle()
            xc.finish()

    row = lambda w: pl.BlockSpec((tm, w), lambda i: (i, 0))
    vec = pl.BlockSpec((1, D), lambda i: (0, 0))
    return _carry("in_bwd", body, exchange, exchange_args, (dproj, win_g, x, dx2, g1),
                  [row(PW), pl.BlockSpec((N_CHIP, D, WIN_C), lambda i: (0, 0, 0)), row(D), row(D), vec],
                  [row(D), vec], [jax.ShapeDtypeStruct((S, D), F32), jax.ShapeDtypeStruct((1, D), F32)],
                  grid=(S // tm,), semantics=("arbitrary",))


ANY = pl.BlockSpec(memory_space=pl.ANY)
VMEM = pl.BlockSpec(memory_space=pltpu.VMEM)
FLIPS = ((1, 0), (0, 1), (1, 1))


def _place():
    x, y, c = lax.axis_index("x"), lax.axis_index("y"), lax.axis_index("c")
    chips = [((1 - x) if fx else x, (1 - y) if fy else y) for fx, fy in FLIPS]
    return x, y, c, 2 * x + y, chips


def _remote(src, dst, send_sem, recv_sem, device):
    return pltpu.make_async_remote_copy(src_ref=src, dst_ref=dst, send_sem=send_sem, recv_sem=recv_sem,
                                        device_id=device, device_id_type=MESH)


def _staggered(issue):
    c = lax.axis_index("c")

    @pl.when(c == 0)
    def _():
        issue((0, 1, 2))

    @pl.when(c == 1)
    def _():
        issue((1, 0, 2))


class _Exchange:
    aliases = {}

    def middle(self, ins, outs, sems):
        pass


class _GatherShards(_Exchange):
    def __init__(self, shards):
        n = self.n = len(shards)
        self.n_in = self.n_out = n
        self.out_shape = [jax.ShapeDtypeStruct((N_CHIP,) + s.shape, s.dtype) for s in shards]
        dma = pltpu.SemaphoreType.DMA
        self.scratch = [dma((3 * n,)), dma((3 * n,)), dma((3 * n,)), dma((3 * n,)), dma((n,))]

    def _ici(self, ins, outs, sems, a, j, chip):
        x, y, c, me, chips = _place()
        half = ins[a].shape[0] // 2
        return _remote(ins[a].at[pl.ds(c * half, half), :], outs[a].at[me, pl.ds(c * half, half), :],
                       sems[0].at[3 * a + j], sems[1].at[3 * a + j], (*chip, c))

    def _fwd(self, outs, sems, a, j, chip, half_of):
        x, y, c, me, chips = _place()
        half = outs[a].shape[1] // 2
        blk = outs[a].at[2 * chip[0] + chip[1], pl.ds(half_of * half, half), :]
        return _remote(blk, blk, sems[2].at[3 * a + j], sems[3].at[3 * a + j], (x, y, 1 - c))

    def _local(self, ins, outs, sems, a):
        return pltpu.make_async_copy(ins[a], outs[a].at[_place()[3]], sems[4].at[a])

    def start(self, ins, outs, sems):
        chips = _place()[4]

        def issue(order):
            for a in range(self.n):
                for j in order:
                    self._ici(ins, outs, sems, a, j, chips[j]).start()

        _staggered(issue)
        for a in range(self.n):
            self._local(ins, outs, sems, a).start()

    def middle(self, ins, outs, sems):
        x, y, c, me, chips = _place()
        for a in range(self.n):
            for j, chip in enumerate(chips):
                half = outs[a].shape[1] // 2
                blk = outs[a].at[2 * chip[0] + chip[1], pl.ds(c * half, half), :]
                _remote(blk, blk, sems[0].at[3 * a + j], sems[1].at[3 * a + j], (x, y, c)).wait_recv()
                self._fwd(outs, sems, a, j, chip, c).start()

    def finish(self, ins, outs, sems):
        x, y, c, me, chips = _place()
        for a in range(self.n):
            for j, chip in enumerate(chips):
                self._fwd(outs, sems, a, j, chip, 1 - c).wait_recv()
        for a in range(self.n):
            for j, chip in enumerate(chips):
                self._ici(ins, outs, sems, a, j, chip).wait_send()
                self._fwd(outs, sems, a, j, chip, c).wait_send()
            self._local(ins, outs, sems, a).wait()


class _HalvesToSibling(_Exchange):
    def __init__(self, grads):
        n = self.n = len(grads)
        self.n_in = self.n_out = n
        self.out_shape = [jax.ShapeDtypeStruct((N_CHIP, g.shape[1] // 2, g.shape[2]), g.dtype) for g in grads]
        self.scratch = [pltpu.SemaphoreType.DMA((n,)), pltpu.SemaphoreType.DMA((n,))]

    def _copy(self, ins, outs, sems, a):
        x, y, c, me, chips = _place()
        half = ins[a].shape[1] // 2
        return _remote(ins[a].at[:, pl.ds((1 - c) * half, half), :], outs[a], sems[0].at[a], sems[1].at[a], (x, y, 1 - c))

    def start(self, ins, outs, sems):
        for a in range(self.n):
            self._copy(ins, outs, sems, a).start()

    def finish(self, ins, outs, sems):
        for a in range(self.n):
            self._copy(ins, outs, sems, a).wait_recv()
        for a in range(self.n):
            self._copy(ins, outs, sems, a).wait_send()


class _OverChips(_Exchange):
    def __init__(self, pre):
        n = self.n = len(pre)
        self.n_in = self.n_out = n
        self.out_shape = [jax.ShapeDtypeStruct(p.shape, p.dtype) for p in pre]
        dma = pltpu.SemaphoreType.DMA
        self.scratch = [dma((3 * n,)), dma((3 * n,)), dma((n,))]

    def _ici(self, ins, outs, sems, a, j, chip):
        x, y, c, me, chips = _place()
        return _remote(ins[a].at[2 * chip[0] + chip[1]], outs[a].at[me], sems[0].at[3 * a + j], sems[1].at[3 * a + j],
                       (*chip, c))

    def _local(self, ins, outs, sems, a):
        me = _place()[3]
        return pltpu.make_async_copy(ins[a].at[me], outs[a].at[me], sems[2].at[a])

    def start(self, ins, outs, sems):
        chips = _place()[4]

        def issue(order):
            for a in range(self.n):
                for j in order:
                    self._ici(ins, outs, sems, a, j, chips[j]).start()

        _staggered(issue)
        for a in range(self.n):
            self._local(ins, outs, sems, a).start()

    def finish(self, ins, outs, sems):
        x, y, c, me, chips = _place()
        for a in range(self.n):
            for j, chip in enumerate(chips):
                blk = outs[a].at[2 * chip[0] + chip[1]]
                _remote(blk, blk, sems[0].at[3 * a + j], sems[1].at[3 * a + j], (x, y, c)).wait_recv()
        for a in range(self.n):
            for j, chip in enumerate(chips):
                self._ici(ins, outs, sems, a, j, chip).wait_send()
            self._local(ins, outs, sems, a).wait()


class _ShareHalves(_Exchange):
    def __init__(self, fulls):
        n = self.n = len(fulls)
        self.n_in = self.n_out = n
        self.out_shape = [jax.ShapeDtypeStruct(f.shape, f.dtype) for f in fulls]
        self.scratch = [pltpu.SemaphoreType.DMA((n,)), pltpu.SemaphoreType.DMA((n,))]
        self.aliases = {a: a for a in range(n)}

    def _copy(self, outs, sems, a, half_of):
        x, y, c, me, chips = _place()
        half = outs[a].shape[0] // 2
        rows = outs[a].at[pl.ds(half_of * half, half), :]
        return _remote(rows, rows, sems[0].at[a], sems[1].at[a], (x, y, 1 - c))

    def start(self, ins, outs, sems):
        c = _place()[2]
        for a in range(self.n):
            self._copy(outs, sems, a, c).start()

    def finish(self, ins, outs, sems):
        c = _place()[2]
        for a in range(self.n):
            self._copy(outs, sems, a, 1 - c).wait_recv()
        for a in range(self.n):
            self._copy(outs, sems, a, c).wait_send()


class _GatherBlocks(_Exchange):
    def __init__(self, block):
        self.n_in = self.n_out = 1
        self.out_shape = [jax.ShapeDtypeStruct((8,) + block.shape, block.dtype)]
        dma = pltpu.SemaphoreType.DMA
        self.scratch = [dma((7,)), dma((7,)), dma]

    @staticmethod
    def _peer(f):
        x, y, c, me, chips = _place()
        return ((1 - x) if f & 4 else x, (1 - y) if f & 2 else y, (1 - c) if f & 1 else c)

    def start(self, ins, outs, sems):
        x, y, c, me, chips = _place()
        for f in range(1, 8):
            _remote(ins[0], outs[0].at[2 * me + c], sems[0].at[f - 1], sems[1].at[f - 1], self._peer(f)).start()
        pltpu.make_async_copy(ins[0], outs[0].at[2 * me + c], sems[2]).start()

    def finish(self, ins, outs, sems):
        x, y, c, me, chips = _place()
        for f in range(1, 8):
            px, py, pc = self._peer(f)
            blk = outs[0].at[4 * px + 2 * py + pc]
            _remote(blk, blk, sems[0].at[f - 1], sems[1].at[f - 1], (x, y, c)).wait_recv()
        for f in range(1, 8):
            _remote(ins[0], outs[0].at[2 * me + c], sems[0].at[f - 1], sems[1].at[f - 1], self._peer(f)).wait_send()
        pltpu.make_async_copy(ins[0], outs[0].at[2 * me + c], sems[2]).wait()


class _Both(_Exchange):
    def __init__(self, first, second):
        self.parts = (first, second)
        self.n_in, self.n_out = first.n_in + second.n_in, first.n_out + second.n_out
        self.out_shape = first.out_shape + second.out_shape
        self.scratch = first.scratch + second.scratch
        self.aliases = dict(first.aliases)
        self.aliases.update({first.n_in + i: first.n_out + o for i, o in second.aliases.items()})

    def _split(self, ins, outs, sems):
        a, b = self.parts
        return ((a, ins[:a.n_in], outs[:a.n_out], sems[:len(a.scratch)]),
                (b, ins[a.n_in:], outs[a.n_out:], sems[len(a.scratch):]))

    def start(self, ins, outs, sems):
        for ex, i, o, s in self._split(ins, outs, sems):
            ex.start(i, o, s)

    def middle(self, ins, outs, sems):
        for ex, i, o, s in self._split(ins, outs, sems):
            ex.middle(i, o, s)

    def finish(self, ins, outs, sems):
        for ex, i, o, s in self._split(ins, outs, sems):
            ex.finish(i, o, s)


class _Bound:
    def __init__(self, ex, ins, outs, sems):
        self.start = lambda: ex.start(ins, outs, sems)
        self.middle = lambda: ex.middle(ins, outs, sems)
        self.finish = lambda: ex.finish(ins, outs, sems)


def _carry(name, body, ex, ex_args, args, in_specs, out_specs, out_shape, scratch_shapes=(), grid=None, semantics=()):
    n_a, n_o, n_s = len(args), len(out_shape), len(scratch_shapes)

    def full_body(*refs):
        p = 0
        groups = []
        for size in (n_a, ex.n_in, n_o, ex.n_out, n_s, len(ex.scratch)):
            groups.append(refs[p:p + size])
            p += size
        a, ei, o, eo, s, es = groups
        body(*a, *o, *s, _Bound(ex, ei, eo, es))

    kwargs = {} if grid is None else {"grid": grid}
    outs = pl.pallas_call(
        full_body, name=name,
        in_specs=list(in_specs) + [ANY] * ex.n_in, out_specs=list(out_specs) + [ANY] * ex.n_out,
        out_shape=list(out_shape) + list(ex.out_shape), scratch_shapes=list(scratch_shapes) + list(ex.scratch),
        input_output_aliases={n_a + i: n_o + o for i, o in ex.aliases.items()},
        compiler_params=_params(*semantics) if semantics else pltpu.CompilerParams(vmem_limit_bytes=VMEM_LIMIT),
        **kwargs,
    )(*args, *ex_args)
    return outs[:n_o], outs[n_o:]


def _exchange_alone(name, ex, ex_args):
    def body(xc):
        xc.start()
        xc.middle()
        xc.finish()

    return _carry(name, body, ex, ex_args, (), (), (), ())[1]


def _core_index():
    return lax.axis_index("c").astype(jnp.int32).reshape(1)


def _pair_sum(gs, gots):
    n = len(gs)
    _, r, cc = gs[0].shape
    half = r // 2

    def body(c_ref, *refs):
        for a in range(n):
            refs[2 * n + a][...] = (refs[a][...].astype(F32) + refs[n + a][...].astype(F32)).astype(BF16)

    mine = pl.BlockSpec((None, half, cc), lambda k, c_ref: (k, c_ref[0], 0))
    blk = pl.BlockSpec((None, half, cc), lambda k, c_ref: (k, 0, 0))
    return pl.pallas_call(
        body, name=f"pair_sum_{r}x{cc}",
        grid_spec=pltpu.PrefetchScalarGridSpec(
            num_scalar_prefetch=1, grid=(N_CHIP,), in_specs=[mine] * n + [blk] * n, out_specs=[blk] * n),
        out_shape=[jax.ShapeDtypeStruct((N_CHIP, half, cc), BF16)] * n,
        compiler_params=_params("parallel"),
    )(_core_index(), *gs, *gots)


def _chip_sum(parts):
    n = len(parts)
    _, half, cc = parts[0].shape
    tr = half // 2

    def body(c_ref, *refs):
        for a in range(n):
            p_ref = refs[a]
            refs[n + a][...] = ((p_ref[0].astype(F32) + p_ref[1].astype(F32)) + p_ref[2].astype(F32)) + p_ref[3].astype(F32)

    return pl.pallas_call(
        body, name=f"chip_sum_{half}x{cc}",
        grid_spec=pltpu.PrefetchScalarGridSpec(
            num_scalar_prefetch=1, grid=(2,),
            in_specs=[pl.BlockSpec((N_CHIP, tr, cc), lambda i, c_ref: (0, i, 0))] * n,
            out_specs=[pl.BlockSpec((tr, cc), lambda i, c_ref: (2 * c_ref[0] + i, 0))] * n),
        out_shape=[jax.ShapeDtypeStruct((2 * half, cc), F32)] * n,
        compiler_params=_params("parallel"),
    )(_core_index(), *parts)


def _adamw_math(w, g, m, v):
    m = ADAM_B1 * m + (1.0 - ADAM_B1) * g
    v = ADAM_B2 * v + (1.0 - ADAM_B2) * (g * g)
    m_hat = m / (1.0 - ADAM_B1 ** ADAM_STEP)
    v_hat = v / (1.0 - ADAM_B2 ** ADAM_STEP)
    delta = -ADAM_LR * (m_hat / (jnp.sqrt(v_hat) + ADAM_EPS) + ADAM_WD * w)
    return delta, m, v


def _adamw(w, g, m, v):
    r, cc = w.shape
    tr = r // 4

    def body(w_ref, g_ref, m_ref, v_ref, go_ref, d_ref, nm_ref, nv_ref):
        g = g_ref[...]
        go_ref[...] = g
        d_ref[...], nm_ref[...], nv_ref[...] = _adamw_math(w_ref[...], g, m_ref[...], v_ref[...])

    blk = pl.BlockSpec((tr, cc), lambda i: (i, 0))
    return pl.pallas_call(
        body, grid=(4,), name=f"adamw_{r}x{cc}",
        in_specs=[blk] * 4, out_specs=[blk] * 4,
        out_shape=[jax.ShapeDtypeStruct((r, cc), F32)] * 4,
        compiler_params=_params("parallel"),
    )(w, g, m, v)


def _pack8(rows):
    def body(*refs):
        out_ref = refs[-1]
        out_ref[...] = jnp.zeros_like(out_ref)
        for i, r in enumerate(refs[:-1]):
            out_ref[i:i + 1, :] = r[...]

    return pl.pallas_call(body, name="pack8", out_shape=jax.ShapeDtypeStruct((8, D), F32))(*rows)


def _adamw_gains(gall, w8, m8, v8):
    def body(ga_ref, w_ref, m_ref, v_ref, g_ref, d_ref, nm_ref, nv_ref):
        g = ga_ref[0]
        for dev in range(1, 8):
            g = g + ga_ref[dev]
        g_ref[...] = g
        d_ref[...], nm_ref[...], nv_ref[...] = _adamw_math(w_ref[...], g, m_ref[...], v_ref[...])

    return pl.pallas_call(
        body, name="adamw_gains",
        out_shape=[jax.ShapeDtypeStruct((8, D), F32)] * 4,
    )(gall, w8, m8, v8)


def kernel(x, positions, w_in, w_out, g_pre_mix, g_post_mix, g_pre_ffn, g_post_ffn, w_gate, w_up, w_down, loss_target, m_w_in, m_w_out, m_g_pre_mix, m_g_post_mix, m_g_pre_ffn, m_g_post_ffn, m_w_gate, m_w_up, m_w_down, v_w_in, v_w_out, v_g_pre_mix, v_g_post_mix, v_g_pre_ffn, v_g_post_ffn, v_w_gate, v_w_up, v_w_down):
    tr = lambda t: jnp.swapaxes(t, 1, 2)[0]
    shards = [w_in[0], w_out[0], tr(w_gate), tr(w_up), w_down[0]]
    moms = [m_w_in[0], m_w_out[0], tr(m_w_gate), tr(m_w_up), m_w_down[0]]
    vels = [v_w_in[0], v_w_out[0], tr(v_w_gate), tr(v_w_up), v_w_down[0]]
    xs, pos, tgt = x[0], positions.reshape(S, 1), loss_target[0]
    g1, g2, g3, g4 = g_pre_mix, g_post_mix, g_pre_ffn, g_post_ffn
    tabs = tuple(jnp.asarray(t) for t in _retention_tables())
    ifc, spread = _rotary_tables()
    ifc, spread = jnp.asarray(ifc), jnp.asarray(spread, dtype=BF16)
    bf = [s.astype(BF16) for s in shards]

    win_g, wout_g = _exchange_alone("gather_in", _GatherShards(bf[:2]), bf[:2])
    wout_g = wout_g.reshape(D, D)
    proj, h1 = _proj_fwd(xs, g1, win_g)
    qr, kr, rv, aq, ak, av, cos, sin = _rot_fwd(proj, pos, ifc, spread)
    o_raw, cat_r, states = _ret_fwd(qr, kr, rv, proj, tabs)
    (att_out, lse, cat_a), (wg_g, wu_g, wd_g) = _att_fwd(aq, ak, av, _GatherShards(bf[2:]), bf[2:])
    mix, x2, h3 = _mix_fwd(cat_r, cat_a, wout_g, xs, g2, g3)
    gt, up, a, f = _ffn_fwd(h3, wg_g, wu_g, wd_g)

    sq, dy, df, dg4 = _head_bwd(f, x2, tgt, g4)
    loss = 0.5 * lax.psum(sq[0, 0], ("x", "y", "c")) / D
    dgt, dup, dh3 = _ffn_bwd_act(df, gt, up, wg_g, wu_g, wd_g)
    ffn_grads = list(_ffn_bwd_w(a, df, h3, dgt, dup))
    (dx2, dmix, dg3, dg2), got = _norm_bwd(dh3, dy, x2, mix, g2, g3, _HalvesToSibling(ffn_grads), ffn_grads)
    pre = _pair_sum(ffn_grads, got)
    dret, datt, dwout = _mix_bwd(dmix, cat_r, cat_a, wout_g)
    (dq_att, dk_att, dv_att), parts = _att_bwd(aq, ak, av, datt, att_out, lse, _OverChips(pre), pre)
    sums = _chip_sum(parts)
    (dqr, dkr, drv, drg), ffn_full = _ret_bwd(qr, kr, rv, proj, o_raw, states, dret, tabs, _ShareHalves(sums), sums)
    dproj = _rot_bwd(cos, sin, dqr, dkr, drv, drg, dq_att, dk_att, dv_att)
    in_grads = [_win_bwd_w(h1, dproj), dwout.reshape(N_CHIP, WOUT_R, D)]
    (dx, dg1), got = _in_bwd(dproj, win_g, xs, dx2, g1, _HalvesToSibling(in_grads), in_grads)

    pre = [*_pair_sum(in_grads[:1], got[:1]), *_pair_sum(in_grads[1:], got[1:])]
    gblock = _pack8([dg1, dg2, dg3, dg4])
    *parts, gall = _exchange_alone("reduce_rest", _Both(_OverChips(pre), _GatherBlocks(gblock)), pre + [gblock])
    sums = [*_chip_sum(parts[:1]), *_chip_sum(parts[1:])]
    in_full = _exchange_alone("share_rest", _ShareHalves(sums), sums)

    full = [in_full[0], in_full[1], ffn_full[1], ffn_full[2], ffn_full[0]]
    upd = [_adamw(w, g, m, v) for w, g, m, v in zip(shards, full, moms, vels)]
    gg, gd, gm, gv = _adamw_gains(gall, _pack8([g1, g2, g3, g4]),
                                  _pack8([m_g_pre_mix, m_g_post_mix, m_g_pre_ffn, m_g_post_ffn]),
                                  _pack8([v_g_pre_mix, v_g_post_mix, v_g_pre_ffn, v_g_post_ffn]))

    def order(mats, vecs):
        back = lambda t: jnp.swapaxes(t[None], 1, 2)
        return ([mats[0][None], mats[1][None]] + [vecs[i:i + 1] for i in range(4)]
                + [back(mats[2]), back(mats[3]), mats[4][None]])

    return (loss, dx[None],
            *order([u[0] for u in upd], gg),
            *order([u[1] for u in upd], gd),
            *order([u[2] for u in upd], gm),
            *order([u[3] for u in upd], gv))
```

```python
import functools

import numpy as np
import jax
import jax.numpy as jnp
from jax import lax
from jax.experimental import pallas as pl
from jax.experimental.pallas import tpu as pltpu

F32, BF16 = jnp.float32, jnp.bfloat16
MESH = pl.DeviceIdType.MESH

S = 2048
D = 1024
PW = 3072
N_CHIP = 4
WIN_C = PW // N_CHIP
DFF = 2816
FF_C = DFF // N_CHIP
WOUT_R = D // N_CHIP
RMS_EPS = 1e-6
GN_EPS = 1e-5
RET_C = 128
RET_SCALE = 32 ** -0.5
ATT_BLK = 128
ATT_SCALE = 64 ** -0.5
PATTERN_DILATIONS = (1, 4, 16)
NEG = -1e30
VMEM_LIMIT = 56 * 1024 * 1024

ADAM_LR, ADAM_B1, ADAM_B2, ADAM_EPS, ADAM_WD, ADAM_STEP = 0.001, 0.9, 0.999, 1e-08, 0.01, 10


def _params(*sem):
    return pltpu.CompilerParams(dimension_semantics=sem, vmem_limit_bytes=VMEM_LIMIT)


def _nt(a, b):
    return lax.dot_general(a, b, (((1,), (1,)), ((), ())), preferred_element_type=F32)


def _tn(a, b):
    return lax.dot_general(a, b, (((0,), (0,)), ((), ())), preferred_element_type=F32)


def _nn(a, b):
    return jnp.dot(a, b, preferred_element_type=F32)


def _rstd(v):
    return lax.rsqrt(jnp.mean(v * v, axis=-1, keepdims=True) + RMS_EPS)


def _sigmoid(v):
    return 1.0 / (1.0 + jnp.exp(-v))


def _rows(i, t):
    return pl.ds(pl.multiple_of(i * t, t), t)


def _retention_tables():
    h = np.arange(8, dtype=np.float32)
    log_g = np.log1p(-np.exp2(-5.0 - h)).astype(np.float32)
    idx = np.arange(RET_C, dtype=np.float32)
    diff = idx[:, None] - idx[None, :]
    dtab = np.where(diff >= 0, np.exp(log_g[:, None, None] * np.maximum(diff, 0.0)), 0.0).astype(np.float32)
    dtab = dtab.reshape(8 * RET_C, RET_C)
    lane_head = np.arange(256) // 32
    a_tab = np.exp(log_g[lane_head][None, :] * (idx + 1.0)[:, None]).astype(np.float32)
    b_tab = np.exp(log_g[lane_head][None, :] * (RET_C - 1.0 - idx)[:, None]).astype(np.float32)
    lam = np.exp(log_g[lane_head] * RET_C).astype(np.float32)[:, None]
    bd = (lane_head[:, None] == (np.arange(512) // 64)[None, :]).astype(np.float32)
    return dtab, a_tab, b_tab, lam, bd


def _rotary_tables():
    inv_r = (1.0 / (np.float32(10000.0) ** np.linspace(0.0, 1.0, 16, dtype=np.float32))).astype(np.float32)
    inv_a = (np.float32(500000.0) ** (-np.arange(0, 16, 2, dtype=np.float32) / np.float32(16))).astype(np.float32)
    ifc = np.zeros((1, 128), np.float32)
    ifc[0, 0:16], ifc[0, 16:24] = inv_r, inv_a
    spread = np.zeros((128, 768), np.float32)
    for lane in range(256):
        spread[(lane % 32) % 16, lane] = 1.0
    for lane in range(512):
        d = lane % 64
        spread[16 + d % 8 if d < 16 else 24, 256 + lane] = 1.0
    return ifc, spread


def _proj_fwd(x, g1, win_g):
    tm = 256

    def body(x_ref, g_ref, w_ref, proj_ref, h_ref):
        xv = x_ref[...]
        h = (xv * _rstd(xv) * g_ref[...]).astype(BF16)
        h_ref[...] = h
        for k in range(N_CHIP):
            proj_ref[:, k * WIN_C:(k + 1) * WIN_C] = _nn(h, w_ref[k])

    return pl.pallas_call(
        body, grid=(S // tm,), name="proj_fwd",
        in_specs=[pl.BlockSpec((tm, D), lambda i: (i, 0)), pl.BlockSpec((1, D), lambda i: (0, 0)),
                  pl.BlockSpec((N_CHIP, D, WIN_C), lambda i: (0, 0, 0))],
        out_specs=[pl.BlockSpec((tm, PW), lambda i: (i, 0)), pl.BlockSpec((tm, D), lambda i: (i, 0))],
        out_shape=[jax.ShapeDtypeStruct((S, PW), F32), jax.ShapeDtypeStruct((S, D), BF16)],
        compiler_params=_params("parallel"),
    )(x, g1, win_g)


def _rot_halves(tm):
    lo_r = (lax.broadcasted_iota(jnp.int32, (tm, 256), 1) % 32) < 16
    lo_a = (lax.broadcasted_iota(jnp.int32, (tm, 512), 1) % 64) < 8
    return lo_r, lo_a


def _spread_exact(t, e):
    hi = t.astype(BF16)
    r1 = t - hi.astype(F32)
    mid = r1.astype(BF16)
    lo = (r1 - mid.astype(F32)).astype(BF16)
    return _nn(hi, e) + _nn(mid, e) + _nn(lo, e)


def _rot_fwd(proj, pos, ifc, spread):
    tm = 256

    def body(p_ref, pos_ref, ifc_ref, e_ref, qr_ref, kr_ref, rv_ref, aq_ref, ak_ref, av_ref, cos_ref, sin_ref):
        ang = pos_ref[...].astype(F32) * ifc_ref[...]
        cs = _spread_exact(jnp.cos(ang), e_ref[...])
        sn = _spread_exact(jnp.sin(ang), e_ref[...])
        cos_ref[...] = cs
        sin_ref[...] = sn
        cr, ca, sr, sa = cs[:, 0:256], cs[:, 256:768], sn[:, 0:256], sn[:, 256:768]
        lo_r, lo_a = _rot_halves(tm)

        def rot_r(v):
            return v * cr + sr * jnp.where(lo_r, -pltpu.roll(v, 240, 1), pltpu.roll(v, 16, 1))

        def rot_a(v):
            return v * ca + sa * jnp.where(lo_a, -pltpu.roll(v, 504, 1), pltpu.roll(v, 8, 1))

        qr_ref[...] = rot_r(p_ref[:, 0:256]).astype(BF16)
        kr_ref[...] = (rot_r(p_ref[:, 256:512]) * RET_SCALE).astype(BF16)
        rv_ref[...] = p_ref[:, 512:1024].astype(BF16)
        aq, ak = rot_a(p_ref[:, 1536:2048]), rot_a(p_ref[:, 2048:2560])
        for j in range(4):
            aq_ref[j] = aq[:, 128 * j:128 * j + 128]
            ak_ref[j] = ak[:, 128 * j:128 * j + 128]
            av_ref[j] = p_ref[:, 2560 + 128 * j:2560 + 128 * j + 128]

    row = lambda w: pl.BlockSpec((tm, w), lambda i: (i, 0))
    const = lambda w: pl.BlockSpec((1, w), lambda i: (0, 0))
    slab = pl.BlockSpec((4, tm, 128), lambda i: (0, i, 0))
    return pl.pallas_call(
        body, grid=(S // tm,), name="rot_fwd",
        in_specs=[row(PW), row(1), const(128), pl.BlockSpec((128, 768), lambda i: (0, 0))],
        out_specs=[row(256), row(256), row(512), slab, slab, slab, row(768), row(768)],
        out_shape=[jax.ShapeDtypeStruct((S, w), BF16) for w in (256, 256, 512)]
                  + [jax.ShapeDtypeStruct((4, S, 128), F32)] * 3 + [jax.ShapeDtypeStruct((S, 768), F32)] * 2,
        compiler_params=_params("parallel"),
    )(proj, pos, ifc, spread)


def _seg_mean(v):
    lo = lax.broadcasted_iota(jnp.int32, v.shape, 1) < 64
    s_lo = jnp.sum(jnp.where(lo, v, 0.0), axis=-1, keepdims=True)
    s_hi = jnp.sum(jnp.where(lo, 0.0, v), axis=-1, keepdims=True)
    return jnp.where(lo, s_lo, s_hi) * (1.0 / 64.0)


def _ret_fwd(qr, kr, rv, proj, tabs):
    C = RET_C
    dtab, a_tab, b_tab, lam, bd = tabs

    def body(q_ref, k_ref, v_ref, g_ref, dt_ref, a_ref, b_ref, lam_ref, bd_ref, o_ref, cat_ref, st_ref, R):
        @pl.when(pl.program_id(0) == 0)
        def _():
            R[...] = jnp.zeros_like(R)

        q, k, v = q_ref[...], k_ref[...], v_ref[...]
        lane_head = lax.broadcasted_iota(jnp.int32, (C, 256), 1) // 32
        col_head = lax.broadcasted_iota(jnp.int32, (C, 256), 1) // 64
        rb = R[...].astype(BF16)
        st_ref[...] = rb
        qa = (q.astype(F32) * a_ref[...]).astype(BF16)
        cross = _nn(qa, rb)
        p = (_nt(_stack_heads(q, lane_head, n=8), k) * dt_ref[...]).astype(BF16)
        og = [cross[:, 256 * g:256 * g + 256]
              + _unstack_heads(_nn(p[4 * C * g:4 * C * (g + 1)], v[:, 256 * g:256 * g + 256]), col_head)
              for g in range(2)]
        kb = (k.astype(F32) * b_ref[...]).astype(BF16)
        R[...] = R[...] * lam_ref[...] + _tn(kb, v) * bd_ref[...]
        o_ref[:, 0:256] = og[0]
        o_ref[:, 256:512] = og[1]
        for j in range(4):
            oj = og[j // 2][:, 128 * (j % 2):128 * (j % 2) + 128]
            xc = oj - _seg_mean(oj)
            rn = xc * lax.rsqrt(_seg_mean(xc * xc) + GN_EPS)
            gj = g_ref[:, 128 * j:128 * j + 128]
            cat_ref[:, 128 * j:128 * j + 128] = (rn * (gj * _sigmoid(gj))).astype(BF16)

    row = lambda w: pl.BlockSpec((C, w), lambda n: (n, 0))
    full = lambda a: pl.BlockSpec(a.shape, lambda n: (0,) * a.ndim)
    return pl.pallas_call(
        body, grid=(S // C,), name="ret_fwd",
        in_specs=[row(256), row(256), row(512), pl.BlockSpec((C, 512), lambda n: (n, 2)),
                  full(dtab), full(a_tab), full(b_tab), full(lam), full(bd)],
        out_specs=[row(512), row(512), pl.BlockSpec((None, 256, 512), lambda n: (n, 0, 0))],
        out_shape=[jax.ShapeDtypeStruct((S, 512), F32), jax.ShapeDtypeStruct((S, 512), BF16),
                   jax.ShapeDtypeStruct((S // C, 256, 512), BF16)],
        scratch_shapes=[pltpu.VMEM((256, 512), F32)],
        compiler_params=_params("arbitrary"),
    )(qr, kr, rv, proj, dtab, a_tab, b_tab, lam, bd)


def _stack_heads(v, lane_head, fill=0.0, n=4):
    return jnp.concatenate([jnp.where(lane_head == h, v, jnp.full_like(v, fill)) for h in range(n)], axis=0)


def _unstack_heads(v, lane_head, n=4):
    out = v[0:ATT_BLK]
    for h in range(1, n):
        out = jnp.where(lane_head == h, v[h * ATT_BLK:(h + 1) * ATT_BLK], out)
    return out


def _att_mask(ib, has_prev):
    nk = 2 * ATT_BLK if has_prev else ATT_BLK
    a = lax.broadcasted_iota(jnp.int32, (4 * ATT_BLK, nk), 0) % ATT_BLK
    kk = lax.broadcasted_iota(jnp.int32, (4 * ATT_BLK, nk), 1)
    if has_prev:
        dist = ATT_BLK + a - kk
        return (dist >= 0) & (dist <= ATT_BLK) & ((ib * ATT_BLK - ATT_BLK + kk) >= 0)
    return (a - kk) >= 0


def _class_rows(ib, r, d):
    if d == 1:
        return pl.ds(pl.multiple_of(ib * ATT_BLK, ATT_BLK), ATT_BLK)
    return pl.ds(ib * ATT_BLK * d + r, ATT_BLK, stride=d)


def _slab_pair(ref, g, rows):
    return jnp.concatenate([ref[2 * g, rows, :], ref[2 * g + 1, rows, :]], axis=1)


def _att_blocks(d):
    nb = S // d // ATT_BLK
    return nb, nb > 1


def _att_fwd(aq, ak, av, exchange, exchange_args):
    def body(q_ref, k_ref, v_ref, o_ref, l_ref, cat_ref, xc):
        xc.start()
        lane_head = lax.broadcasted_iota(jnp.int32, (ATT_BLK, 256), 1) // 64
        for pi, d in enumerate(PATTERN_DILATIONS):
            if pi == len(PATTERN_DILATIONS) - 1:
                xc.middle()
            nb, has_prev = _att_blocks(d)

            def block(b, carry, pi=pi, d=d, nb=nb, has_prev=has_prev):
                r, ib = b // nb, b % nb
                rows = _class_rows(ib, r, d)
                prow = _class_rows(jnp.maximum(ib - 1, 0), r, d)
                valid = _att_mask(ib, has_prev)
                for g in range(2):
                    qg = _slab_pair(q_ref, g, rows).astype(BF16)
                    kg = _slab_pair(k_ref, g, rows)
                    vg = _slab_pair(v_ref, g, rows)
                    if has_prev:
                        kg = jnp.concatenate([_slab_pair(k_ref, g, prow), kg], axis=0)
                        vg = jnp.concatenate([_slab_pair(v_ref, g, prow), vg], axis=0)
                    kg, vg = kg.astype(BF16), vg.astype(BF16)
                    s = jnp.where(valid, _nt(_stack_heads(qg, lane_head), kg) * ATT_SCALE, NEG)
                    m = jnp.max(s, axis=-1, keepdims=True)
                    p = jnp.exp(s - m)
                    den = jnp.sum(p, axis=-1, keepdims=True)
                    og = _unstack_heads(_nn(p.astype(BF16), vg) / den, lane_head)
                    lg = _unstack_heads(jnp.broadcast_to(m + jnp.log(den), (4 * ATT_BLK, 256)), lane_head)
                    for jj in range(2):
                        j = 2 * g + jj
                        o_new, l_new = og[:, 128 * jj:128 * jj + 128], lg[:, 128 * jj:128 * jj + 128]
                        if pi > 0:
                            o_old, l_old = o_ref[j, rows, :], l_ref[j, rows, :]
                            mx = jnp.maximum(l_old, l_new)
                            ea, eb = jnp.exp(l_old - mx), jnp.exp(l_new - mx)
                            den = ea + eb
                            o_new = (ea * o_old + eb * o_new) / den
                            l_new = mx + jnp.log(den)
                        o_ref[j, rows, :] = o_new
                        l_ref[j, rows, :] = l_new
                return carry

            lax.fori_loop(0, S // ATT_BLK, block, 0)

        def to_cat(i, carry):
            rows = _rows(i, 256)
            for j in range(4):
                cat_ref[rows, 128 * j:128 * j + 128] = o_ref[j, rows, :].astype(BF16)
            return carry

        lax.fori_loop(0, S // 256, to_cat, 0)
        xc.finish()

    slab = jax.ShapeDtypeStruct((4, S, 128), F32)
    return _carry("att_fwd", body, exchange, exchange_args, (aq, ak, av), [VMEM] * 3, [VMEM] * 3,
                  [slab, slab, jax.ShapeDtypeStruct((S, 512), BF16)])


def _mix_fwd(cat_r, cat_a, wout, x, g2, g3):
    tm = 256

    def body(cr_ref, ca_ref, w_ref, x_ref, g2_ref, g3_ref, mix_ref, x2_ref, h3_ref):
        mix = _nn(cr_ref[...], w_ref[0:512, :]) + _nn(ca_ref[...], w_ref[512:1024, :])
        mix_ref[...] = mix
        x2 = x_ref[...] + mix * _rstd(mix) * g2_ref[...]
        x2_ref[...] = x2
        h3_ref[...] = (x2 * _rstd(x2) * g3_ref[...]).astype(BF16)

    row = lambda w: pl.BlockSpec((tm, w), lambda i: (i, 0))
    vec = pl.BlockSpec((1, D), lambda i: (0, 0))
    return pl.pallas_call(
        body, grid=(S // tm,), name="mix_fwd",
        in_specs=[row(512), row(512), pl.BlockSpec((D, D), lambda i: (0, 0)), row(D), vec, vec],
        out_specs=[row(D), row(D), row(D)],
        out_shape=[jax.ShapeDtypeStruct((S, D), F32), jax.ShapeDtypeStruct((S, D), F32),
                   jax.ShapeDtypeStruct((S, D), BF16)],
        compiler_params=_params("parallel"),
    )(cat_r, cat_a, wout, x, g2, g3)


def _ffn_fwd(h3, wg, wu, wd):
    tm = 256

    def body(h_ref, wg_ref, wu_ref, wd_ref, gt_ref, up_ref, a_ref, f_ref):
        k, i = pl.program_id(0), pl.program_id(1)
        h = h_ref[...]
        gt = _nt(h, wg_ref[...])
        up = _nt(h, wu_ref[...])
        gt_ref[...] = gt.astype(BF16)
        up_ref[...] = up.astype(BF16)
        a = (gt * _sigmoid(gt) * up).astype(BF16)
        a_ref[...] = a
        part = _nn(a, wd_ref[...])
        rows = _rows(i, tm)

        @pl.when(k == 0)
        def _():
            f_ref[rows, :] = part

        @pl.when(k > 0)
        def _():
            f_ref[rows, :] = f_ref[rows, :] + part

    wrow = pl.BlockSpec((None, FF_C, D), lambda k, i: (k, 0, 0))
    act = pl.BlockSpec((None, tm, FF_C), lambda k, i: (k, i, 0))
    return pl.pallas_call(
        body, grid=(N_CHIP, S // tm), name="ffn_fwd",
        in_specs=[pl.BlockSpec((tm, D), lambda k, i: (i, 0)), wrow, wrow, wrow],
        out_specs=[act, act, act, pl.BlockSpec((S, D), lambda k, i: (0, 0))],
        out_shape=[jax.ShapeDtypeStruct((N_CHIP, S, FF_C), BF16)] * 3 + [jax.ShapeDtypeStruct((S, D), F32)],
        compiler_params=_params("arbitrary", "arbitrary"),
    )(h3, wg, wu, wd)


def _head_bwd(f, x2, tgt, g4):
    tm = 256

    def body(f_ref, x2_ref, t_ref, g_ref, loss_ref, dy_ref, df_ref, dg_ref):
        @pl.when(pl.program_id(0) == 0)
        def _():
            loss_ref[...] = jnp.zeros_like(loss_ref)
            dg_ref[...] = jnp.zeros_like(dg_ref)

        fv = f_ref[...]
        r = _rstd(fv)
        fn = fv * r
        e = x2_ref[...] + fn * g_ref[...] - t_ref[...]
        sq = jnp.sum(jnp.sum(e * e, axis=-1, keepdims=True), axis=0, keepdims=True)
        loss_ref[...] = loss_ref[...] + sq
        dy = e * (1.0 / D)
        dy_ref[...] = dy
        dg_ref[...] = dg_ref[...] + jnp.sum(dy * fn, axis=0, keepdims=True)
        t = dy * g_ref[...]
        df_ref[...] = (r * (t - fn * jnp.mean(t * fn, axis=-1, keepdims=True))).astype(BF16)

    row = pl.BlockSpec((tm, D), lambda i: (i, 0))
    vec = pl.BlockSpec((1, D), lambda i: (0, 0))
    return pl.pallas_call(
        body, grid=(S // tm,), name="head_bwd",
        in_specs=[row, row, row, vec],
        out_specs=[pl.BlockSpec((8, 128), lambda i: (0, 0)), row, row, vec],
        out_shape=[jax.ShapeDtypeStruct((8, 128), F32), jax.ShapeDtypeStruct((S, D), F32),
                   jax.ShapeDtypeStruct((S, D), BF16), jax.ShapeDtypeStruct((1, D), F32)],
        compiler_params=_params("arbitrary"),
    )(f, x2, tgt, g4)


def _ffn_bwd_act(df, gt, up, wg, wu, wd):
    tm = 512

    def body(df_ref, gt_ref, up_ref, wg_ref, wu_ref, wd_ref, dgt_ref, dup_ref, dh_ref):
        k = pl.program_id(1)
        da = _nt(df_ref[...], wd_ref[...])
        gt, up = gt_ref[...].astype(F32), up_ref[...].astype(F32)
        sg = _sigmoid(gt)
        dup = (da * gt * sg).astype(BF16)
        dgt = (da * up * (sg * (1.0 + gt * (1.0 - sg)))).astype(BF16)
        dup_ref[...] = dup
        dgt_ref[...] = dgt
        part = _nn(dgt, wg_ref[...]) + _nn(dup, wu_ref[...])

        @pl.when(k == 0)
        def _():
            dh_ref[...] = part

        @pl.when(k > 0)
        def _():
            dh_ref[...] = dh_ref[...] + part

    wrow = pl.BlockSpec((None, FF_C, D), lambda i, k: (k, 0, 0))
    act = pl.BlockSpec((None, tm, FF_C), lambda i, k: (k, i, 0))
    row = pl.BlockSpec((tm, D), lambda i, k: (i, 0))
    return pl.pallas_call(
        body, grid=(S // tm, N_CHIP), name="ffn_bwd_act",
        in_specs=[row, act, act, wrow, wrow, wrow],
        out_specs=[act, act, row],
        out_shape=[jax.ShapeDtypeStruct((N_CHIP, S, FF_C), BF16), jax.ShapeDtypeStruct((N_CHIP, S, FF_C), BF16),
                   jax.ShapeDtypeStruct((S, D), F32)],
        compiler_params=_params("parallel", "arbitrary"),
    )(df, gt, up, wg, wu, wd)


def _ffn_bwd_w(a, df, h3, dgt, dup):
    tm = 1024
    assert S // tm == 2

    def body(a_ref, df_ref, h_ref, dgt_ref, dup_ref, dwd_ref, dwg_ref, dwu_ref, acc_d, acc_g, acc_u):
        i = pl.program_id(1)
        h = h_ref[...]
        parts = (_tn(a_ref[...], df_ref[...]), _tn(dgt_ref[...], h), _tn(dup_ref[...], h))

        @pl.when(i == 0)
        def _():
            for acc, part in zip((acc_d, acc_g, acc_u), parts):
                acc[...] = part

        @pl.when(i == S // tm - 1)
        def _():
            for out, acc, part in zip((dwd_ref, dwg_ref, dwu_ref), (acc_d, acc_g, acc_u), parts):
                out[...] = (acc[...] + part).astype(BF16)

    act = pl.BlockSpec((None, tm, FF_C), lambda k, i: (k, i, 0))
    row = pl.BlockSpec((tm, D), lambda k, i: (i, 0))
    wrow = pl.BlockSpec((None, FF_C, D), lambda k, i: (k, 0, 0))
    return pl.pallas_call(
        body, grid=(N_CHIP, S // tm), name="ffn_bwd_w",
        in_specs=[act, row, row, act, act],
        out_specs=[wrow, wrow, wrow],
        out_shape=[jax.ShapeDtypeStruct((N_CHIP, FF_C, D), BF16)] * 3,
        scratch_shapes=[pltpu.VMEM((FF_C, D), F32)] * 3,
        compiler_params=_params("parallel", "arbitrary"),
    )(a, df, h3, dgt, dup)


def _norm_bwd(dh3, dy, x2, mix, g2, g3, exchange, exchange_args):
    tm = 256

    def body(dh_ref, dy_ref, x2_ref, mix_ref, g2_ref, g3_ref, dx2_ref, dmix_ref, dg3_ref, dg2_ref, xc):
        @pl.when(pl.program_id(0) == 0)
        def _():
            xc.start()
            dg3_ref[...] = jnp.zeros_like(dg3_ref)
            dg2_ref[...] = jnp.zeros_like(dg2_ref)

        x2 = x2_ref[...]
        r3 = _rstd(x2)
        xn = x2 * r3
        dh = dh_ref[...]
        dg3_ref[...] = dg3_ref[...] + jnp.sum(dh * xn, axis=0, keepdims=True)
        t = dh * g3_ref[...]
        dx2 = dy_ref[...] + r3 * (t - xn * jnp.mean(t * xn, axis=-1, keepdims=True))
        dx2_ref[...] = dx2
        mix = mix_ref[...]
        r2 = _rstd(mix)
        mn = mix * r2
        dg2_ref[...] = dg2_ref[...] + jnp.sum(dx2 * mn, axis=0, keepdims=True)
        u = dx2 * g2_ref[...]
        dmix_ref[...] = (r2 * (u - mn * jnp.mean(u * mn, axis=-1, keepdims=True))).astype(BF16)

        @pl.when(pl.program_id(0) == S // tm - 1)
        def _():
            xc.middle()
            xc.finish()

    row = pl.BlockSpec((tm, D), lambda i: (i, 0))
    vec = pl.BlockSpec((1, D), lambda i: (0, 0))
    return _carry("norm_bwd", body, exchange, exchange_args, (dh3, dy, x2, mix, g2, g3),
                  [row, row, row, row, vec, vec], [row, row, vec, vec],
                  [jax.ShapeDtypeStruct((S, D), F32), jax.ShapeDtypeStruct((S, D), BF16),
                   jax.ShapeDtypeStruct((1, D), F32), jax.ShapeDtypeStruct((1, D), F32)],
                  grid=(S // tm,), semantics=("arbitrary",))


def _mix_bwd(dmix, cat_r, cat_a, wout):
    tm = 512

    def body(dm_ref, cr_ref, ca_ref, w_ref, dret_ref, datt_ref, dw_ref, acc):
        i = pl.program_id(0)

        @pl.when(i == 0)
        def _():
            acc[...] = jnp.zeros_like(acc)

        dm = dm_ref[...]
        dret_ref[...] = _nt(dm, w_ref[0:512, :])
        datt = _nt(dm, w_ref[512:1024, :])
        for j in range(4):
            datt_ref[j] = datt[:, 128 * j:128 * j + 128]
        acc[0:512, :] += _tn(cr_ref[...], dm)
        acc[512:1024, :] += _tn(ca_ref[...], dm)

        @pl.when(i == S // tm - 1)
        def _():
            dw_ref[...] = acc[...].astype(BF16)

    row = lambda w: pl.BlockSpec((tm, w), lambda i: (i, 0))
    full = pl.BlockSpec((D, D), lambda i: (0, 0))
    return pl.pallas_call(
        body, grid=(S // tm,), name="mix_bwd",
        in_specs=[row(D), row(512), row(512), full],
        out_specs=[row(512), pl.BlockSpec((4, tm, 128), lambda i: (0, i, 0)), full],
        out_shape=[jax.ShapeDtypeStruct((S, 512), F32), jax.ShapeDtypeStruct((4, S, 128), F32),
                   jax.ShapeDtypeStruct((D, D), BF16)],
        scratch_shapes=[pltpu.VMEM((D, D), F32)],
        compiler_params=_params("arbitrary"),
    )(dmix, cat_r, cat_a, wout)


def _att_bwd(aq, ak, av, datt, att_out, lse, exchange, exchange_args):
    def body(q_ref, k_ref, v_ref, do_ref, out_ref, l_ref, dq_ref, dk_ref, dv_ref, xc):
        xc.start()

        def clear(i, carry):
            rows = _rows(i, 256)
            for ref in (dq_ref, dk_ref, dv_ref):
                for j in range(4):
                    ref[j, rows, :] = jnp.zeros((256, 128), F32)
            return carry

        lax.fori_loop(0, S // 256, clear, 0)
        lane_head = lax.broadcasted_iota(jnp.int32, (ATT_BLK, 256), 1) // 64
        for d in PATTERN_DILATIONS:
            nb, has_prev = _att_blocks(d)

            def block(b, carry, d=d, nb=nb, has_prev=has_prev):
                r, ib = b // nb, b % nb
                rows = _class_rows(ib, r, d)
                prow = _class_rows(jnp.maximum(ib - 1, 0), r, d)
                valid = _att_mask(ib, has_prev)
                for g in range(2):
                    qg = _slab_pair(q_ref, g, rows).astype(BF16)
                    kg = _slab_pair(k_ref, g, rows)
                    vg = _slab_pair(v_ref, g, rows)
                    if has_prev:
                        kg = jnp.concatenate([_slab_pair(k_ref, g, prow), kg], axis=0)
                        vg = jnp.concatenate([_slab_pair(v_ref, g, prow), vg], axis=0)
                    kg, vg = kg.astype(BF16), vg.astype(BF16)
                    dog = _slab_pair(do_ref, g, rows)
                    outg = _slab_pair(out_ref, g, rows)
                    lg = _slab_pair(l_ref, g, rows)
                    qs = _stack_heads(qg, lane_head)
                    dos = _stack_heads(dog, lane_head)
                    delta = jnp.sum(dos * jnp.concatenate([outg] * 4, axis=0), axis=-1, keepdims=True)
                    lh = jnp.max(_stack_heads(lg, lane_head, NEG), axis=-1, keepdims=True)
                    s = jnp.where(valid, _nt(qs, kg) * ATT_SCALE, NEG)
                    p = jnp.exp(s - lh)
                    dosb = dos.astype(BF16)
                    ds = (p * (_nt(dosb, vg) - delta) * ATT_SCALE).astype(BF16)
                    dq = _unstack_heads(_nn(ds, kg), lane_head)
                    dk = _tn(ds, qs)
                    dv = _tn(p.astype(BF16), dosb)
                    for jj in range(2):
                        j, sl = 2 * g + jj, slice(128 * jj, 128 * jj + 128)
                        dq_ref[j, rows, :] += dq[:, sl]
                        if has_prev:
                            dk_ref[j, prow, :] += dk[0:ATT_BLK, sl]
                            dv_ref[j, prow, :] += dv[0:ATT_BLK, sl]
                            dk_ref[j, rows, :] += dk[ATT_BLK:2 * ATT_BLK, sl]
                            dv_ref[j, rows, :] += dv[ATT_BLK:2 * ATT_BLK, sl]
                        else:
                            dk_ref[j, rows, :] += dk[:, sl]
                            dv_ref[j, rows, :] += dv[:, sl]
                return carry

            lax.fori_loop(0, S // ATT_BLK, block, 0)
        xc.middle()
        xc.finish()

    slab = jax.ShapeDtypeStruct((4, S, 128), F32)
    return _carry("att_bwd", body, exchange, exchange_args, (aq, ak, av, datt, att_out, lse), [VMEM] * 6, [VMEM] * 3,
                  [slab, slab, slab])


def _ret_bwd(qr, kr, rv, proj, o_raw, states, dret, tabs, exchange, exchange_args):
    C = RET_C
    nc = S // C
    dtab, a_tab, b_tab, lam, bd = tabs

    def body(q_ref, k_ref, v_ref, g_ref, o_ref, st_ref, dr_ref, dt_ref, a_ref, b_ref, lam_ref, bd_ref,
             dq_ref, dk_ref, dv_ref, dg_ref, dR, exch):
        @pl.when(pl.program_id(0) == 0)
        def _():
            exch.start()
            dR[...] = jnp.zeros_like(dR)

        q, k, v = q_ref[...], k_ref[...], v_ref[...]
        lane_head = lax.broadcasted_iota(jnp.int32, (C, 256), 1) // 32
        col_head = lax.broadcasted_iota(jnp.int32, (C, 256), 1) // 64
        dos = []
        for j in range(4):
            sl = slice(128 * j, 128 * j + 128)
            oj = o_ref[:, sl]
            xc = oj - _seg_mean(oj)
            rs = lax.rsqrt(_seg_mean(xc * xc) + GN_EPS)
            rn = xc * rs
            gj = g_ref[:, sl]
            sg = _sigmoid(gj)
            dret = dr_ref[:, sl]
            dg_ref[:, sl] = dret * rn * (sg * (1.0 + gj * (1.0 - sg)))
            drn = dret * (gj * sg)
            dos.append(rs * (drn - _seg_mean(drn) - rn * _seg_mean(drn * rn)))
        do = [jnp.concatenate(dos[0:2], axis=1), jnp.concatenate(dos[2:4], axis=1)]
        do8 = jnp.concatenate(do, axis=1).astype(BF16)
        drb = dR[...].astype(BF16)
        rb = st_ref[...]
        dq = _nt(do8, rb) * a_ref[...]
        dk = _nt(v, drb) * b_ref[...]
        kb = (k.astype(F32) * b_ref[...]).astype(BF16)
        dvall = _nn(kb, drb)
        qs = _stack_heads(q, lane_head, n=8)
        dec = dt_ref[...]
        p = (_nt(qs, k) * dec).astype(BF16)
        dos = [_stack_heads(do[g], col_head).astype(BF16) for g in range(2)]
        dp = jnp.concatenate([_nt(dos[g], v[:, 256 * g:256 * g + 256]) for g in range(2)], axis=0)
        ds = (dp * dec).astype(BF16)
        dq = dq + _unstack_heads(_nn(ds, k), lane_head, n=8)
        dk = dk + _tn(ds, qs)
        dv = [dvall[:, 256 * g:256 * g + 256] + _tn(p[4 * C * g:4 * C * (g + 1)], dos[g]) for g in range(2)]
        qa = (q.astype(F32) * a_ref[...]).astype(BF16)
        dR[...] = dR[...] * lam_ref[...] + _tn(qa, do8) * bd_ref[...]
        dq_ref[...] = dq
        dk_ref[...] = dk
        dv_ref[:, 0:256] = dv[0]
        dv_ref[:, 256:512] = dv[1]

        @pl.when(pl.program_id(0) == nc - 1)
        def _():
            exch.middle()
            exch.finish()

    rev = lambda w: pl.BlockSpec((C, w), lambda n: (nc - 1 - n, 0))
    full = lambda a: pl.BlockSpec(a.shape, lambda n: (0,) * a.ndim)
    return _carry(
        "ret_bwd", body, exchange, exchange_args, (qr, kr, rv, proj, o_raw, states, dret, dtab, a_tab, b_tab, lam, bd),
        [rev(256), rev(256), rev(512), pl.BlockSpec((C, 512), lambda n: (nc - 1 - n, 2)), rev(512),
         pl.BlockSpec((None, 256, 512), lambda n: (nc - 1 - n, 0, 0)), rev(512),
         full(dtab), full(a_tab), full(b_tab), full(lam), full(bd)],
        [rev(256), rev(256), rev(512), rev(512)],
        [jax.ShapeDtypeStruct((S, 256), F32), jax.ShapeDtypeStruct((S, 256), F32),
         jax.ShapeDtypeStruct((S, 512), F32), jax.ShapeDtypeStruct((S, 512), F32)],
        scratch_shapes=[pltpu.VMEM((256, 512), F32)], grid=(nc,), semantics=("arbitrary",))


def _rot_bwd(cos, sin, dqr, dkr, drv, drg, dq_att, dk_att, dv_att):
    tm = 256

    def body(cos_ref, sin_ref, dqr_ref, dkr_ref, drv_ref, drg_ref, dqa_ref, dka_ref, dva_ref, dp_ref):
        cr, ca, sr, sa = cos_ref[:, 0:256], cos_ref[:, 256:768], sin_ref[:, 0:256], sin_ref[:, 256:768]
        lo_r, lo_a = _rot_halves(tm)

        def unrot_r(g):
            gs = g * sr
            return g * cr + pltpu.roll(jnp.where(lo_r, -gs, 0.0), 16, 1) + pltpu.roll(jnp.where(lo_r, 0.0, gs), 240, 1)

        def unrot_a(g):
            gs = g * sa
            return g * ca + pltpu.roll(jnp.where(lo_a, -gs, 0.0), 8, 1) + pltpu.roll(jnp.where(lo_a, 0.0, gs), 504, 1)

        def wide(ref):
            return jnp.concatenate([ref[j] for j in range(4)], axis=1)

        dp_ref[:, 0:256] = unrot_r(dqr_ref[...]).astype(BF16)
        dp_ref[:, 256:512] = unrot_r(dkr_ref[...] * RET_SCALE).astype(BF16)
        dp_ref[:, 512:1024] = drv_ref[...].astype(BF16)
        dp_ref[:, 1024:1536] = drg_ref[...].astype(BF16)
        dp_ref[:, 1536:2048] = unrot_a(wide(dqa_ref)).astype(BF16)
        dp_ref[:, 2048:2560] = unrot_a(wide(dka_ref)).astype(BF16)
        dp_ref[:, 2560:3072] = wide(dva_ref).astype(BF16)

    row = lambda w: pl.BlockSpec((tm, w), lambda i: (i, 0))
    slab = pl.BlockSpec((4, tm, 128), lambda i: (0, i, 0))
    return pl.pallas_call(
        body, grid=(S // tm,), name="rot_bwd",
        in_specs=[row(768), row(768), row(256), row(256), row(512), row(512), slab, slab, slab],
        out_specs=row(PW), out_shape=jax.ShapeDtypeStruct((S, PW), BF16),
        compiler_params=_params("parallel"),
    )(cos, sin, dqr, dkr, drv, drg, dq_att, dk_att, dv_att)


def _win_bwd_w(h1, dproj):
    tm = 512

    def body(h_ref, dp_ref, dw_ref, acc):
        i = pl.program_id(1)

        @pl.when(i == 0)
        def _():
            acc[...] = jnp.zeros_like(acc)

        acc[...] += _tn(h_ref[...], dp_ref[...])

        @pl.when(i == S // tm - 1)
        def _():
            dw_ref[...] = acc[...].astype(BF16)

    return pl.pallas_call(
        body, grid=(N_CHIP, S // tm), name="win_bwd_w",
        in_specs=[pl.BlockSpec((tm, D), lambda k, i: (i, 0)), pl.BlockSpec((tm, WIN_C), lambda k, i: (i, k))],
        out_specs=pl.BlockSpec((None, D, WIN_C), lambda k, i: (k, 0, 0)),
        out_shape=jax.ShapeDtypeStruct((N_CHIP, D, WIN_C), BF16),
        scratch_shapes=[pltpu.VMEM((D, WIN_C), F32)],
        compiler_params=_params("parallel", "arbitrary"),
    )(h1, dproj)


def _in_bwd(dproj, win_g, x, dx2, g1, exchange, exchange_args):
    tm = 256

    def body(dp_ref, w_ref, x_ref, dx2_ref, g_ref, dx_ref, dg_ref, xc):
        @pl.when(pl.program_id(0) == 0)
        def _():
            xc.start()
            dg_ref[...] = jnp.zeros_like(dg_ref)

        dh = _nt(dp_ref[:, 0:WIN_C], w_ref[0])
        for k in range(1, N_CHIP):
            dh = dh + _nt(dp_ref[:, k * WIN_C:(k + 1) * WIN_C], w_ref[k])
        xv = x_ref[...]
        r = _rstd(xv)
        xn = xv * r
        dg_ref[...] = dg_ref[...] + jnp.sum(dh * xn, axis=0, keepdims=True)
        t = dh * g_ref[...]
        dx_ref[...] = dx2_ref[...] + r * (t - xn * jnp.mean(t * xn, axis=-1, keepdims=True))

        @pl.when(pl.program_id(0) == S // tm - 1)
        def _():
            xc.middle()
            xc.finish()

    row = lambda w: pl.BlockSpec((tm, w), lambda i: (i, 0))
    vec = pl.BlockSpec((1, D), lambda i: (0, 0))
    return _carry("in_bwd", body, exchange, exchange_args, (dproj, win_g, x, dx2, g1),
                  [row(PW), pl.BlockSpec((N_CHIP, D, WIN_C), lambda i: (0, 0, 0)), row(D), row(D), vec],
                  [row(D), vec], [jax.ShapeDtypeStruct((S, D), F32), jax.ShapeDtypeStruct((1, D), F32)],
                  grid=(S // tm,), semantics=("arbitrary",))


ANY = pl.BlockSpec(memory_space=pl.ANY)
VMEM = pl.BlockSpec(memory_space=pltpu.VMEM)
FLIPS = ((1, 0), (0, 1), (1, 1))


def _place():
    x, y, c = lax.axis_index("x"), lax.axis_index("y"), lax.axis_index("c")
    chips = [((1 - x) if fx else x, (1 - y) if fy else y) for fx, fy in FLIPS]
    return x, y, c, 2 * x + y, chips


def _remote(src, dst, send_sem, recv_sem, device):
    return pltpu.make_async_remote_copy(src_ref=src, dst_ref=dst, send_sem=send_sem, recv_sem=recv_sem,
                                        device_id=device, device_id_type=MESH)


def _staggered(issue):
    c = lax.axis_index("c")

    @pl.when(c == 0)
    def _():
        issue((0, 1, 2))

    @pl.when(c == 1)
    def _():
        issue((1, 0, 2))


class _Exchange:
    aliases = {}

    def middle(self, ins, outs, sems):
        pass


class _GatherShards(_Exchange):
    def __init__(self, shards):
        n = self.n = len(shards)
        self.n_in = self.n_out = n
        self.out_shape = [jax.ShapeDtypeStruct((N_CHIP,) + s.shape, s.dtype) for s in shards]
        dma = pltpu.SemaphoreType.DMA
        self.scratch = [dma((3 * n,)), dma((3 * n,)), dma((3 * n,)), dma((3 * n,)), dma((n,))]

    def _ici(self, ins, outs, sems, a, j, chip):
        x, y, c, me, chips = _place()
        half = ins[a].shape[0] // 2
        return _remote(ins[a].at[pl.ds(c * half, half), :], outs[a].at[me, pl.ds(c * half, half), :],
                       sems[0].at[3 * a + j], sems[1].at[3 * a + j], (*chip, c))

    def _fwd(self, outs, sems, a, j, chip, half_of):
        x, y, c, me, chips = _place()
        half = outs[a].shape[1] // 2
        blk = outs[a].at[2 * chip[0] + chip[1], pl.ds(half_of * half, half), :]
        return _remote(blk, blk, sems[2].at[3 * a + j], sems[3].at[3 * a + j], (x, y, 1 - c))

    def _local(self, ins, outs, sems, a):
        return pltpu.make_async_copy(ins[a], outs[a].at[_place()[3]], sems[4].at[a])

    def start(self, ins, outs, sems):
        chips = _place()[4]

        def issue(order):
            for a in range(self.n):
                for j in order:
                    self._ici(ins, outs, sems, a, j, chips[j]).start()

        _staggered(issue)
        for a in range(self.n):
            self._local(ins, outs, sems, a).start()

    def middle(self, ins, outs, sems):
        x, y, c, me, chips = _place()
        for a in range(self.n):
            for j, chip in enumerate(chips):
                half = outs[a].shape[1] // 2
                blk = outs[a].at[2 * chip[0] + chip[1], pl.ds(c * half, half), :]
                _remote(blk, blk, sems[0].at[3 * a + j], sems[1].at[3 * a + j], (x, y, c)).wait_recv()
                self._fwd(outs, sems, a, j, chip, c).start()

    def finish(self, ins, outs, sems):
        x, y, c, me, chips = _place()
        for a in range(self.n):
            for j, chip in enumerate(chips):
                self._fwd(outs, sems, a, j, chip, 1 - c).wait_recv()
        for a in range(self.n):
            for j, chip in enumerate(chips):
                self._ici(ins, outs, sems, a, j, chip).wait_send()
                self._fwd(outs, sems, a, j, chip, c).wait_send()
            self._local(ins, outs, sems, a).wait()


class _ForwardGathered(_Exchange):
    def __init__(self, shards):
        n = self.n = len(shards)
        self.n_in, self.n_out = 2 * n, n
        self.out_shape = [jax.ShapeDtypeStruct((N_CHIP,) + s.shape, s.dtype) for s in shards]
        dma = pltpu.SemaphoreType.DMA
        self.scratch = [dma((3 * n,)), dma((3 * n,)), dma((n,))]
        self.aliases = {n + a: a for a in range(n)}

    def _fwd(self, outs, sems, a, j, chip, half_of):
        x, y, c, me, chips = _place()
        half = outs[a].shape[1] // 2
        blk = outs[a].at[2 * chip[0] + chip[1], pl.ds(half_of * half, half), :]
        return _remote(blk, blk, sems[0].at[3 * a + j], sems[1].at[3 * a + j], (x, y, 1 - c))

    def _local(self, ins, outs, sems, a):
        return pltpu.make_async_copy(ins[a], outs[a].at[_place()[3]], sems[2].at[a])

    def start(self, ins, outs, sems):
        x, y, c, me, chips = _place()
        for a in range(self.n):
            for j, chip in enumerate(chips):
                self._fwd(outs, sems, a, j, chip, c).start()
        for a in range(self.n):
            self._local(ins, outs, sems, a).start()

    def finish(self, ins, outs, sems):
        x, y, c, me, chips = _place()
        for a in range(self.n):
            for j, chip in enumerate(chips):
                self._fwd(outs, sems, a, j, chip, 1 - c).wait_recv()
        for a in range(self.n):
            for j, chip in enumerate(chips):
                self._fwd(outs, sems, a, j, chip, c).wait_send()
            self._local(ins, outs, sems, a).wait()


HBM = pl.BlockSpec(memory_space=pltpu.HBM)
SEMS = pl.BlockSpec(memory_space=pltpu.SEMAPHORE)
DATAFLOW = pltpu.SideEffectType.DATAFLOW_SIDE_EFFECTING


def _ici_half(shard_ref, land_ref, sems, a, j, chip, dst_chip):
    c = lax.axis_index("c")
    half = shard_ref.shape[0] // 2
    return _remote(shard_ref.at[pl.ds(c * half, half), :], land_ref.at[dst_chip, pl.ds(c * half, half), :],
                   sems[0].at[3 * a + j], sems[1].at[3 * a + j], (*chip, c))


def _gather_over_chips_start(shards, after):
    n = len(shards)

    def body(*refs):
        ins, lands, token = refs[:n], refs[n:2 * n], refs[-1]
        sems = refs[2 * n + 1:2 * n + 3]
        x, y, c, me, chips = _place()

        def issue(order):
            for a in range(n):
                for j in order:
                    _ici_half(ins[a], lands[a], sems, a, j, chips[j], me).start()

        _staggered(issue)
        token[...] = jnp.zeros_like(token)

    dma = pltpu.SemaphoreType.DMA
    lands = [pltpu.with_memory_space_constraint(lax.empty((N_CHIP,) + s.shape, s.dtype), pltpu.HBM) for s in shards]
    outs = pl.pallas_call(
        body, name="gather_over_chips_start",
        out_shape=[dma((3 * n,)), dma((3 * n,))] + [pltpu.HBM(s.shape, s.dtype) for s in shards]
                  + [pltpu.HBM(l.shape, l.dtype) for l in lands] + [jax.ShapeDtypeStruct((8, 128), F32)],
        in_specs=[HBM] * (2 * n) + [ANY], out_specs=[SEMS, SEMS] + [HBM] * (2 * n) + [VMEM],
        input_output_aliases={a: 2 + a for a in range(2 * n)},
        compiler_params=pltpu.CompilerParams(has_side_effects=DATAFLOW),
    )(*[pltpu.with_memory_space_constraint(s, pltpu.HBM) for s in shards], *lands, after)
    return outs[0:2], outs[2:2 + n], outs[2 + n:2 + 2 * n], outs[-1]


def _gather_over_chips_wait(sems, shards, lands, after):
    n = len(shards)

    def body(*refs):
        ins, lz = refs[:n], refs[n:2 * n]
        sm = refs[2 * n:2 * n + 2]
        x, y, c, me, chips = _place()
        for a in range(n):
            for j, chip in enumerate(chips):
                cp = _ici_half(ins[a], lz[a], sm, a, j, chip, 2 * chip[0] + chip[1])
                cp.wait_send()
                cp.wait_recv()

    outs = pl.pallas_call(
        body, name="gather_over_chips_wait",
        out_shape=[pltpu.HBM(s.shape, s.dtype) for s in shards] + [pltpu.HBM(l.shape, l.dtype) for l in lands],
        in_specs=[HBM] * (2 * n) + [SEMS, SEMS, ANY], out_specs=[HBM] * (2 * n),
        input_output_aliases={a: a for a in range(2 * n)},
        compiler_params=pltpu.CompilerParams(has_side_effects=DATAFLOW),
    )(*shards, *lands, *sems, after)
    return outs[:n], outs[n:]


class _HalvesToSibling(_Exchange):
    def __init__(self, grads):
        n = self.n = len(grads)
        self.n_in = self.n_out = n
        self.out_shape = [jax.ShapeDtypeStruct((N_CHIP, g.shape[1] // 2, g.shape[2]), g.dtype) for g in grads]
        self.scratch = [pltpu.SemaphoreType.DMA((n,)), pltpu.SemaphoreType.DMA((n,))]

    def _copy(self, ins, outs, sems, a):
        x, y, c, me, chips = _place()
        half = ins[a].shape[1] // 2
        return _remote(ins[a].at[:, pl.ds((1 - c) * half, half), :], outs[a], sems[0].at[a], sems[1].at[a], (x, y, 1 - c))

    def start(self, ins, outs, sems):
        for a in range(self.n):
            self._copy(ins, outs, sems, a).start()

    def finish(self, ins, outs, sems):
        for a in range(self.n):
            self._copy(ins, outs, sems, a).wait_recv()
        for a in range(self.n):
            self._copy(ins, outs, sems, a).wait_send()


class _OverChips(_Exchange):
    def __init__(self, pre):
        n = self.n = len(pre)
        self.n_in = self.n_out = n
        self.out_shape = [jax.ShapeDtypeStruct(p.shape, p.dtype) for p in pre]
        dma = pltpu.SemaphoreType.DMA
        self.scratch = [dma((3 * n,)), dma((3 * n,)), dma((n,))]

    def _ici(self, ins, outs, sems, a, j, chip):
        x, y, c, me, chips = _place()
        return _remote(ins[a].at[2 * chip[0] + chip[1]], outs[a].at[me], sems[0].at[3 * a + j], sems[1].at[3 * a + j],
                       (*chip, c))

    def _local(self, ins, outs, sems, a):
        me = _place()[3]
        return pltpu.make_async_copy(ins[a].at[me], outs[a].at[me], sems[2].at[a])

    def start(self, ins, outs, sems):
        chips = _place()[4]

        def issue(order):
            for a in range(self.n):
                for j in order:
                    self._ici(ins, outs, sems, a, j, chips[j]).start()

        _staggered(issue)
        for a in range(self.n):
            self._local(ins, outs, sems, a).start()

    def finish(self, ins, outs, sems):
        x, y, c, me, chips = _place()
        for a in range(self.n):
            for j, chip in enumerate(chips):
                blk = outs[a].at[2 * chip[0] + chip[1]]
                _remote(blk, blk, sems[0].at[3 * a + j], sems[1].at[3 * a + j], (x, y, c)).wait_recv()
        for a in range(self.n):
            for j, chip in enumerate(chips):
                self._ici(ins, outs, sems, a, j, chip).wait_send()
            self._local(ins, outs, sems, a).wait()


class _ShareHalves(_Exchange):
    def __init__(self, fulls):
        n = self.n = len(fulls)
        self.n_in = self.n_out = n
        self.out_shape = [jax.ShapeDtypeStruct(f.shape, f.dtype) for f in fulls]
        self.scratch = [pltpu.SemaphoreType.DMA((n,)), pltpu.SemaphoreType.DMA((n,))]
        self.aliases = {a: a for a in range(n)}

    def _copy(self, outs, sems, a, half_of):
        x, y, c, me, chips = _place()
        half = outs[a].shape[0] // 2
        rows = outs[a].at[pl.ds(half_of * half, half), :]
        return _remote(rows, rows, sems[0].at[a], sems[1].at[a], (x, y, 1 - c))

    def start(self, ins, outs, sems):
        c = _place()[2]
        for a in range(self.n):
            self._copy(outs, sems, a, c).start()

    def finish(self, ins, outs, sems):
        c = _place()[2]
        for a in range(self.n):
            self._copy(outs, sems, a, 1 - c).wait_recv()
        for a in range(self.n):
            self._copy(outs, sems, a, c).wait_send()


class _GatherBlocks(_Exchange):
    def __init__(self, block):
        self.n_in = self.n_out = 1
        self.out_shape = [jax.ShapeDtypeStruct((8,) + block.shape, block.dtype)]
        dma = pltpu.SemaphoreType.DMA
        self.scratch = [dma((7,)), dma((7,)), dma]

    @staticmethod
    def _peer(f):
        x, y, c, me, chips = _place()
        return ((1 - x) if f & 4 else x, (1 - y) if f & 2 else y, (1 - c) if f & 1 else c)

    def start(self, ins, outs, sems):
        x, y, c, me, chips = _place()
        for f in range(1, 8):
            _remote(ins[0], outs[0].at[2 * me + c], sems[0].at[f - 1], sems[1].at[f - 1], self._peer(f)).start()
        pltpu.make_async_copy(ins[0], outs[0].at[2 * me + c], sems[2]).start()

    def finish(self, ins, outs, sems):
        x, y, c, me, chips = _place()
        for f in range(1, 8):
            px, py, pc = self._peer(f)
            blk = outs[0].at[4 * px + 2 * py + pc]
            _remote(blk, blk, sems[0].at[f - 1], sems[1].at[f - 1], (x, y, c)).wait_recv()
        for f in range(1, 8):
            _remote(ins[0], outs[0].at[2 * me + c], sems[0].at[f - 1], sems[1].at[f - 1], self._peer(f)).wait_send()
        pltpu.make_async_copy(ins[0], outs[0].at[2 * me + c], sems[2]).wait()


class _Both(_Exchange):
    def __init__(self, first, second):
        self.parts = (first, second)
        self.n_in, self.n_out = first.n_in + second.n_in, first.n_out + second.n_out
        self.out_shape = first.out_shape + second.out_shape
        self.scratch = first.scratch + second.scratch
        self.aliases = dict(first.aliases)
        self.aliases.update({first.n_in + i: first.n_out + o for i, o in second.aliases.items()})

    def _split(self, ins, outs, sems):
        a, b = self.parts
        return ((a, ins[:a.n_in], outs[:a.n_out], sems[:len(a.scratch)]),
                (b, ins[a.n_in:], outs[a.n_out:], sems[len(a.scratch):]))

    def start(self, ins, outs, sems):
        for ex, i, o, s in self._split(ins, outs, sems):
            ex.start(i, o, s)

    def middle(self, ins, outs, sems):
        for ex, i, o, s in self._split(ins, outs, sems):
            ex.middle(i, o, s)

    def finish(self, ins, outs, sems):
        for ex, i, o, s in self._split(ins, outs, sems):
            ex.finish(i, o, s)


class _Bound:
    def __init__(self, ex, ins, outs, sems):
        self.start = lambda: ex.start(ins, outs, sems)
        self.middle = lambda: ex.middle(ins, outs, sems)
        self.finish = lambda: ex.finish(ins, outs, sems)


def _carry(name, body, ex, ex_args, args, in_specs, out_specs, out_shape, scratch_shapes=(), grid=None, semantics=()):
    n_a, n_o, n_s = len(args), len(out_shape), len(scratch_shapes)

    def full_body(*refs):
        p = 0
        groups = []
        for size in (n_a, ex.n_in, n_o, ex.n_out, n_s, len(ex.scratch)):
            groups.append(refs[p:p + size])
            p += size
        a, ei, o, eo, s, es = groups
        body(*a, *o, *s, _Bound(ex, ei, eo, es))

    kwargs = {} if grid is None else {"grid": grid}
    outs = pl.pallas_call(
        full_body, name=name,
        in_specs=list(in_specs) + [ANY] * ex.n_in, out_specs=list(out_specs) + [ANY] * ex.n_out,
        out_shape=list(out_shape) + list(ex.out_shape), scratch_shapes=list(scratch_shapes) + list(ex.scratch),
        input_output_aliases={n_a + i: n_o + o for i, o in ex.aliases.items()},
        compiler_params=_params(*semantics) if semantics else pltpu.CompilerParams(vmem_limit_bytes=VMEM_LIMIT),
        **kwargs,
    )(*args, *ex_args)
    return outs[:n_o], outs[n_o:]


def _exchange_alone(name, ex, ex_args):
    def body(xc):
        xc.start()
        xc.middle()
        xc.finish()

    return _carry(name, body, ex, ex_args, (), (), (), ())[1]


def _core_index():
    return lax.axis_index("c").astype(jnp.int32).reshape(1)


def _pair_sum(gs, gots):
    n = len(gs)
    _, r, cc = gs[0].shape
    half = r // 2

    def body(c_ref, *refs):
        for a in range(n):
            refs[2 * n + a][...] = (refs[a][...].astype(F32) + refs[n + a][...].astype(F32)).astype(BF16)

    mine = pl.BlockSpec((None, half, cc), lambda k, c_ref: (k, c_ref[0], 0))
    blk = pl.BlockSpec((None, half, cc), lambda k, c_ref: (k, 0, 0))
    return pl.pallas_call(
        body, name=f"pair_sum_{r}x{cc}",
        grid_spec=pltpu.PrefetchScalarGridSpec(
            num_scalar_prefetch=1, grid=(N_CHIP,), in_specs=[mine] * n + [blk] * n, out_specs=[blk] * n),
        out_shape=[jax.ShapeDtypeStruct((N_CHIP, half, cc), BF16)] * n,
        compiler_params=_params("parallel"),
    )(_core_index(), *gs, *gots)


def _chip_sum(parts):
    n = len(parts)
    _, half, cc = parts[0].shape
    tr = half // 2

    def body(c_ref, *refs):
        for a in range(n):
            p_ref = refs[a]
            refs[n + a][...] = ((p_ref[0].astype(F32) + p_ref[1].astype(F32)) + p_ref[2].astype(F32)) + p_ref[3].astype(F32)

    return pl.pallas_call(
        body, name=f"chip_sum_{half}x{cc}",
        grid_spec=pltpu.PrefetchScalarGridSpec(
            num_scalar_prefetch=1, grid=(2,),
            in_specs=[pl.BlockSpec((N_CHIP, tr, cc), lambda i, c_ref: (0, i, 0))] * n,
            out_specs=[pl.BlockSpec((tr, cc), lambda i, c_ref: (2 * c_ref[0] + i, 0))] * n),
        out_shape=[jax.ShapeDtypeStruct((2 * half, cc), F32)] * n,
        compiler_params=_params("parallel"),
    )(_core_index(), *parts)


def _adamw_math(w, g, m, v):
    m = ADAM_B1 * m + (1.0 - ADAM_B1) * g
    v = ADAM_B2 * v + (1.0 - ADAM_B2) * (g * g)
    m_hat = m / (1.0 - ADAM_B1 ** ADAM_STEP)
    v_hat = v / (1.0 - ADAM_B2 ** ADAM_STEP)
    delta = -ADAM_LR * (m_hat / (jnp.sqrt(v_hat) + ADAM_EPS) + ADAM_WD * w)
    return delta, m, v


def _adamw(w, g, m, v):
    r, cc = w.shape
    tr = r // 4

    def body(w_ref, g_ref, m_ref, v_ref, go_ref, d_ref, nm_ref, nv_ref):
        g = g_ref[...]
        go_ref[...] = g
        d_ref[...], nm_ref[...], nv_ref[...] = _adamw_math(w_ref[...], g, m_ref[...], v_ref[...])

    blk = pl.BlockSpec((tr, cc), lambda i: (i, 0))
    return pl.pallas_call(
        body, grid=(4,), name=f"adamw_{r}x{cc}",
        in_specs=[blk] * 4, out_specs=[blk] * 4,
        out_shape=[jax.ShapeDtypeStruct((r, cc), F32)] * 4,
        compiler_params=_params("parallel"),
    )(w, g, m, v)


def _pack8(rows):
    def body(*refs):
        out_ref = refs[-1]
        out_ref[...] = jnp.zeros_like(out_ref)
        for i, r in enumerate(refs[:-1]):
            out_ref[i:i + 1, :] = r[...]

    return pl.pallas_call(body, name="pack8", out_shape=jax.ShapeDtypeStruct((8, D), F32))(*rows)


def _adamw_gains(gall, w8, m8, v8):
    def body(ga_ref, w_ref, m_ref, v_ref, g_ref, d_ref, nm_ref, nv_ref):
        g = ga_ref[0]
        for dev in range(1, 8):
            g = g + ga_ref[dev]
        g_ref[...] = g
        d_ref[...], nm_ref[...], nv_ref[...] = _adamw_math(w_ref[...], g, m_ref[...], v_ref[...])

    return pl.pallas_call(
        body, name="adamw_gains",
        out_shape=[jax.ShapeDtypeStruct((8, D), F32)] * 4,
    )(gall, w8, m8, v8)


def kernel(x, positions, w_in, w_out, g_pre_mix, g_post_mix, g_pre_ffn, g_post_ffn, w_gate, w_up, w_down, loss_target, m_w_in, m_w_out, m_g_pre_mix, m_g_post_mix, m_g_pre_ffn, m_g_post_ffn, m_w_gate, m_w_up, m_w_down, v_w_in, v_w_out, v_g_pre_mix, v_g_post_mix, v_g_pre_ffn, v_g_post_ffn, v_w_gate, v_w_up, v_w_down):
    tr = lambda t: jnp.swapaxes(t, 1, 2)[0]
    shards = [w_in[0], w_out[0], tr(w_gate), tr(w_up), w_down[0]]
    moms = [m_w_in[0], m_w_out[0], tr(m_w_gate), tr(m_w_up), m_w_down[0]]
    vels = [v_w_in[0], v_w_out[0], tr(v_w_gate), tr(v_w_up), v_w_down[0]]
    xs, pos, tgt = x[0], positions.reshape(S, 1), loss_target[0]
    g1, g2, g3, g4 = g_pre_mix, g_post_mix, g_pre_ffn, g_post_ffn
    tabs = tuple(jnp.asarray(t) for t in _retention_tables())
    ifc, spread = _rotary_tables()
    ifc, spread = jnp.asarray(ifc), jnp.asarray(spread, dtype=BF16)
    bf = [s.astype(BF16) for s in shards]

    win_g, wout_g = _exchange_alone("gather_in", _GatherShards(bf[:2]), bf[:2])
    wout_g = wout_g.reshape(D, D)
    sems, ffn_sh, ffn_lands, token = _gather_over_chips_start(bf[2:], win_g)
    proj, h1 = _proj_fwd(xs, g1 + token[0:1, 0:1], win_g)
    qr, kr, rv, aq, ak, av, cos, sin = _rot_fwd(proj, pos, ifc, spread)
    o_raw, cat_r, states = _ret_fwd(qr, kr, rv, proj, tabs)
    ffn_sh, ffn_lands = _gather_over_chips_wait(sems, ffn_sh, ffn_lands, o_raw)
    (att_out, lse, cat_a), (wg_g, wu_g, wd_g) = _att_fwd(aq, ak, av, _ForwardGathered(bf[2:]), [*ffn_sh, *ffn_lands])
    mix, x2, h3 = _mix_fwd(cat_r, cat_a, wout_g, xs, g2, g3)
    gt, up, a, f = _ffn_fwd(h3, wg_g, wu_g, wd_g)

    sq, dy, df, dg4 = _head_bwd(f, x2, tgt, g4)
    loss = 0.5 * lax.psum(sq[0, 0], ("x", "y", "c")) / D
    dgt, dup, dh3 = _ffn_bwd_act(df, gt, up, wg_g, wu_g, wd_g)
    ffn_grads = list(_ffn_bwd_w(a, df, h3, dgt, dup))
    (dx2, dmix, dg3, dg2), got = _norm_bwd(dh3, dy, x2, mix, g2, g3, _HalvesToSibling(ffn_grads), ffn_grads)
    pre = _pair_sum(ffn_grads, got)
    dret, datt, dwout = _mix_bwd(dmix, cat_r, cat_a, wout_g)
    (dq_att, dk_att, dv_att), parts = _att_bwd(aq, ak, av, datt, att_out, lse, _OverChips(pre), pre)
    sums = _chip_sum(parts)
    (dqr, dkr, drv, drg), ffn_full = _ret_bwd(qr, kr, rv, proj, o_raw, states, dret, tabs, _ShareHalves(sums), sums)
    dproj = _rot_bwd(cos, sin, dqr, dkr, drv, drg, dq_att, dk_att, dv_att)
    in_grads = [_win_bwd_w(h1, dproj), dwout.reshape(N_CHIP, WOUT_R, D)]
    (dx, dg1), got = _in_bwd(dproj, win_g, xs, dx2, g1, _HalvesToSibling(in_grads), in_grads)

    pre = [*_pair_sum(in_grads[:1], got[:1]), *_pair_sum(in_grads[1:], got[1:])]
    gblock = _pack8([dg1, dg2, dg3, dg4])
    *parts, gall = _exchange_alone("reduce_rest", _Both(_OverChips(pre), _GatherBlocks(gblock)), pre + [gblock])
    sums = [*_chip_sum(parts[:1]), *_chip_sum(parts[1:])]
    in_full = _exchange_alone("share_rest", _ShareHalves(sums), sums)

    full = [in_full[0], in_full[1], ffn_full[1], ffn_full[2], ffn_full[0]]
    upd = [_adamw(w, g, m, v) for w, g, m, v in zip(shards, full, moms, vels)]
    gg, gd, gm, gv = _adamw_gains(gall, _pack8([g1, g2, g3, g4]),
                                  _pack8([m_g_pre_mix, m_g_post_mix, m_g_pre_ffn, m_g_post_ffn]),
                                  _pack8([v_g_pre_mix, v_g_post_mix, v_g_pre_ffn, v_g_post_ffn]))

    def order(mats, vecs):
        back = lambda t: jnp.swapaxes(t[None], 1, 2)
        return ([mats[0][None], mats[1][None]] + [vecs[i:i + 1] for i in range(4)]
                + [back(mats[2]), back(mats[3]), mats[4][None]])

    return (loss, dx[None],
            *order([u[0] for u in upd], gg),
            *order([u[1] for u in upd], gd),
            *order([u[2] for u in upd], gm),
            *order([u[3] for u in upd], gv))
```

```python
import functools

import numpy as np
import jax
import jax.numpy as jnp
from jax import lax
from jax.experimental import pallas as pl
from jax.experimental.pallas import tpu as pltpu

F32, BF16 = jnp.float32, jnp.bfloat16
MESH = pl.DeviceIdType.MESH

S = 2048
D = 1024
PW = 3072
N_CHIP = 4
WIN_C = PW // N_CHIP
DFF = 2816
FF_C = DFF // N_CHIP
WOUT_R = D // N_CHIP
RMS_EPS = 1e-6
GN_EPS = 1e-5
RET_C = 128
RET_SCALE = 32 ** -0.5
ATT_BLK = 128
ATT_SCALE = 64 ** -0.5
PATTERN_DILATIONS = (1, 4, 16)
NEG = -1e30
VMEM_LIMIT = 56 * 1024 * 1024

ADAM_LR, ADAM_B1, ADAM_B2, ADAM_EPS, ADAM_WD, ADAM_STEP = 0.001, 0.9, 0.999, 1e-08, 0.01, 10


def _params(*sem):
    return pltpu.CompilerParams(dimension_semantics=sem, vmem_limit_bytes=VMEM_LIMIT)


def _nt(a, b):
    return lax.dot_general(a, b, (((1,), (1,)), ((), ())), preferred_element_type=F32)


def _tn(a, b):
    return lax.dot_general(a, b, (((0,), (0,)), ((), ())), preferred_element_type=F32)


def _nn(a, b):
    return jnp.dot(a, b, preferred_element_type=F32)


def _rstd(v):
    return lax.rsqrt(jnp.mean(v * v, axis=-1, keepdims=True) + RMS_EPS)


def _sigmoid(v):
    return 1.0 / (1.0 + jnp.exp(-v))


def _rows(i, t):
    return pl.ds(pl.multiple_of(i * t, t), t)


def _retention_tables():
    h = np.arange(8, dtype=np.float32)
    log_g = np.log1p(-np.exp2(-5.0 - h)).astype(np.float32)
    idx = np.arange(RET_C, dtype=np.float32)
    diff = idx[:, None] - idx[None, :]
    dtab = np.where(diff >= 0, np.exp(log_g[:, None, None] * np.maximum(diff, 0.0)), 0.0).astype(np.float32)
    dtab = dtab.reshape(8 * RET_C, RET_C)
    lane_head = np.arange(256) // 32
    a_tab = np.exp(log_g[lane_head][None, :] * (idx + 1.0)[:, None]).astype(np.float32)
    b_tab = np.exp(log_g[lane_head][None, :] * (RET_C - 1.0 - idx)[:, None]).astype(np.float32)
    lam = np.exp(log_g[lane_head] * RET_C).astype(np.float32)[:, None]
    bd = (lane_head[:, None] == (np.arange(512) // 64)[None, :]).astype(np.float32)
    return dtab, a_tab, b_tab, lam, bd


def _rotary_tables():
    inv_r = (1.0 / (np.float32(10000.0) ** np.linspace(0.0, 1.0, 16, dtype=np.float32))).astype(np.float32)
    inv_a = (np.float32(500000.0) ** (-np.arange(0, 16, 2, dtype=np.float32) / np.float32(16))).astype(np.float32)
    ifc = np.zeros((1, 128), np.float32)
    ifc[0, 0:16], ifc[0, 16:24] = inv_r, inv_a
    spread = np.zeros((128, 768), np.float32)
    for lane in range(256):
        spread[(lane % 32) % 16, lane] = 1.0
    for lane in range(512):
        d = lane % 64
        spread[16 + d % 8 if d < 16 else 24, 256 + lane] = 1.0
    return ifc, spread


def _proj_fwd(x, g1, win_g):
    tm = 256

    def body(x_ref, g_ref, w_ref, proj_ref, h_ref):
        xv = x_ref[...]
        h = (xv * _rstd(xv) * g_ref[...]).astype(BF16)
        h_ref[...] = h
        for k in range(N_CHIP):
            proj_ref[:, k * WIN_C:(k + 1) * WIN_C] = _nn(h, w_ref[k])

    return pl.pallas_call(
        body, grid=(S // tm,), name="proj_fwd",
        in_specs=[pl.BlockSpec((tm, D), lambda i: (i, 0)), pl.BlockSpec((1, D), lambda i: (0, 0)),
                  pl.BlockSpec((N_CHIP, D, WIN_C), lambda i: (0, 0, 0))],
        out_specs=[pl.BlockSpec((tm, PW), lambda i: (i, 0)), pl.BlockSpec((tm, D), lambda i: (i, 0))],
        out_shape=[jax.ShapeDtypeStruct((S, PW), F32), jax.ShapeDtypeStruct((S, D), BF16)],
        compiler_params=_params("parallel"),
    )(x, g1, win_g)


def _rot_halves(tm):
    lo_r = (lax.broadcasted_iota(jnp.int32, (tm, 256), 1) % 32) < 16
    lo_a = (lax.broadcasted_iota(jnp.int32, (tm, 512), 1) % 64) < 8
    return lo_r, lo_a


def _spread_exact(t, e):
    hi = t.astype(BF16)
    r1 = t - hi.astype(F32)
    mid = r1.astype(BF16)
    lo = (r1 - mid.astype(F32)).astype(BF16)
    return _nn(hi, e) + _nn(mid, e) + _nn(lo, e)


def _rot_fwd(proj, pos, ifc, spread):
    tm = 256

    def body(p_ref, pos_ref, ifc_ref, e_ref, qr_ref, kr_ref, rv_ref, aq_ref, ak_ref, av_ref, cos_ref, sin_ref):
        ang = pos_ref[...].astype(F32) * ifc_ref[...]
        cs = _spread_exact(jnp.cos(ang), e_ref[...])
        sn = _spread_exact(jnp.sin(ang), e_ref[...])
        cos_ref[...] = cs
        sin_ref[...] = sn
        cr, ca, sr, sa = cs[:, 0:256], cs[:, 256:768], sn[:, 0:256], sn[:, 256:768]
        lo_r, lo_a = _rot_halves(tm)

        def rot_r(v):
            return v * cr + sr * jnp.where(lo_r, -pltpu.roll(v, 240, 1), pltpu.roll(v, 16, 1))

        def rot_a(v):
            return v * ca + sa * jnp.where(lo_a, -pltpu.roll(v, 504, 1), pltpu.roll(v, 8, 1))

        qr_ref[...] = rot_r(p_ref[:, 0:256]).astype(BF16)
        kr_ref[...] = (rot_r(p_ref[:, 256:512]) * RET_SCALE).astype(BF16)
        rv_ref[...] = p_ref[:, 512:1024].astype(BF16)
        aq, ak = rot_a(p_ref[:, 1536:2048]), rot_a(p_ref[:, 2048:2560])
        for j in range(4):
            aq_ref[j] = aq[:, 128 * j:128 * j + 128]
            ak_ref[j] = ak[:, 128 * j:128 * j + 128]
            av_ref[j] = p_ref[:, 2560 + 128 * j:2560 + 128 * j + 128]

    row = lambda w: pl.BlockSpec((tm, w), lambda i: (i, 0))
    const = lambda w: pl.BlockSpec((1, w), lambda i: (0, 0))
    slab = pl.BlockSpec((4, tm, 128), lambda i: (0, i, 0))
    return pl.pallas_call(
        body, grid=(S // tm,), name="rot_fwd",
        in_specs=[row(PW), row(1), const(128), pl.BlockSpec((128, 768), lambda i: (0, 0))],
        out_specs=[row(256), row(256), row(512), slab, slab, slab, row(768), row(768)],
        out_shape=[jax.ShapeDtypeStruct((S, w), BF16) for w in (256, 256, 512)]
                  + [jax.ShapeDtypeStruct((4, S, 128), F32)] * 3 + [jax.ShapeDtypeStruct((S, 768), F32)] * 2,
        compiler_params=_params("parallel"),
    )(proj, pos, ifc, spread)


def _seg_mean(v):
    lo = lax.broadcasted_iota(jnp.int32, v.shape, 1) < 64
    s_lo = jnp.sum(jnp.where(lo, v, 0.0), axis=-1, keepdims=True)
    s_hi = jnp.sum(jnp.where(lo, 0.0, v), axis=-1, keepdims=True)
    return jnp.where(lo, s_lo, s_hi) * (1.0 / 64.0)


def _ret_fwd(qr, kr, rv, proj, tabs):
    C = RET_C
    dtab, a_tab, b_tab, lam, bd = tabs

    def body(q_ref, k_ref, v_ref, g_ref, dt_ref, a_ref, b_ref, lam_ref, bd_ref, o_ref, cat_ref, st_ref, R):
        @pl.when(pl.program_id(0) == 0)
        def _():
            R[...] = jnp.zeros_like(R)

        q, k, v = q_ref[...], k_ref[...], v_ref[...]
        lane_head = lax.broadcasted_iota(jnp.int32, (C, 256), 1) // 32
        col_head = lax.broadcasted_iota(jnp.int32, (C, 256), 1) // 64
        rb = R[...].astype(BF16)
        st_ref[...] = rb
        qa = (q.astype(F32) * a_ref[...]).astype(BF16)
        cross = _nn(qa, rb)
        p = (_nt(_stack_heads(q, lane_head, n=8), k) * dt_ref[...]).astype(BF16)
        og = [cross[:, 256 * g:256 * g + 256]
              + _unstack_heads(_nn(p[4 * C * g:4 * C * (g + 1)], v[:, 256 * g:256 * g + 256]), col_head)
              for g in range(2)]
        kb = (k.astype(F32) * b_ref[...]).astype(BF16)
        R[...] = R[...] * lam_ref[...] + _tn(kb, v) * bd_ref[...]
        o_ref[:, 0:256] = og[0]
        o_ref[:, 256:512] = og[1]
        for j in range(4):
            oj = og[j // 2][:, 128 * (j % 2):128 * (j % 2) + 128]
            xc = oj - _seg_mean(oj)
            rn = xc * lax.rsqrt(_seg_mean(xc * xc) + GN_EPS)
            gj = g_ref[:, 128 * j:128 * j + 128]
            cat_ref[:, 128 * j:128 * j + 128] = (rn * (gj * _sigmoid(gj))).astype(BF16)

    row = lambda w: pl.BlockSpec((C, w), lambda n: (n, 0))
    full = lambda a: pl.BlockSpec(a.shape, lambda n: (0,) * a.ndim)
    return pl.pallas_call(
        body, grid=(S // C,), name="ret_fwd",
        in_specs=[row(256), row(256), row(512), pl.BlockSpec((C, 512), lambda n: (n, 2)),
                  full(dtab), full(a_tab), full(b_tab), full(lam), full(bd)],
        out_specs=[row(512), row(512), pl.BlockSpec((None, 256, 512), lambda n: (n, 0, 0))],
        out_shape=[jax.ShapeDtypeStruct((S, 512), F32), jax.ShapeDtypeStruct((S, 512), BF16),
                   jax.ShapeDtypeStruct((S // C, 256, 512), BF16)],
        scratch_shapes=[pltpu.VMEM((256, 512), F32)],
        compiler_params=_params("arbitrary"),
    )(qr, kr, rv, proj, dtab, a_tab, b_tab, lam, bd)


def _stack_heads(v, lane_head, fill=0.0, n=4):
    return jnp.concatenate([jnp.where(lane_head == h, v, jnp.full_like(v, fill)) for h in range(n)], axis=0)


def _unstack_heads(v, lane_head, n=4):
    out = v[0:ATT_BLK]
    for h in range(1, n):
        out = jnp.where(lane_head == h, v[h * ATT_BLK:(h + 1) * ATT_BLK], out)
    return out


def _att_mask(ib, has_prev):
    nk = 2 * ATT_BLK if has_prev else ATT_BLK
    a = lax.broadcasted_iota(jnp.int32, (4 * ATT_BLK, nk), 0) % ATT_BLK
    kk = lax.broadcasted_iota(jnp.int32, (4 * ATT_BLK, nk), 1)
    if has_prev:
        dist = ATT_BLK + a - kk
        return (dist >= 0) & (dist <= ATT_BLK) & ((ib * ATT_BLK - ATT_BLK + kk) >= 0)
    return (a - kk) >= 0


def _class_rows(ib, r, d):
    if d == 1:
        return pl.ds(pl.multiple_of(ib * ATT_BLK, ATT_BLK), ATT_BLK)
    return pl.ds(ib * ATT_BLK * d + r, ATT_BLK, stride=d)


def _slab_pair(ref, g, rows):
    return jnp.concatenate([ref[2 * g, rows, :], ref[2 * g + 1, rows, :]], axis=1)


def _att_blocks(d):
    nb = S // d // ATT_BLK
    return nb, nb > 1


def _att_fwd(aq, ak, av, exchange, exchange_args):
    def body(q_ref, k_ref, v_ref, o_ref, l_ref, cat_ref, xc):
        xc.start()
        lane_head = lax.broadcasted_iota(jnp.int32, (ATT_BLK, 256), 1) // 64
        for pi, d in enumerate(PATTERN_DILATIONS):
            if pi == len(PATTERN_DILATIONS) - 1:
                xc.middle()
            nb, has_prev = _att_blocks(d)

            def block(b, carry, pi=pi, d=d, nb=nb, has_prev=has_prev):
                r, ib = b // nb, b % nb
                rows = _class_rows(ib, r, d)
                prow = _class_rows(jnp.maximum(ib - 1, 0), r, d)
                valid = _att_mask(ib, has_prev)
                for g in range(2):
                    qg = _slab_pair(q_ref, g, rows).astype(BF16)
                    kg = _slab_pair(k_ref, g, rows)
                    vg = _slab_pair(v_ref, g, rows)
                    if has_prev:
                        kg = jnp.concatenate([_slab_pair(k_ref, g, prow), kg], axis=0)
                        vg = jnp.concatenate([_slab_pair(v_ref, g, prow), vg], axis=0)
                    kg, vg = kg.astype(BF16), vg.astype(BF16)
                    s = jnp.where(valid, _nt(_stack_heads(qg, lane_head), kg) * ATT_SCALE, NEG)
                    m = jnp.max(s, axis=-1, keepdims=True)
                    p = jnp.exp(s - m)
                    den = jnp.sum(p, axis=-1, keepdims=True)
                    og = _unstack_heads(_nn(p.astype(BF16), vg) / den, lane_head)
                    lg = _unstack_heads(jnp.broadcast_to(m + jnp.log(den), (4 * ATT_BLK, 256)), lane_head)
                    for jj in range(2):
                        j = 2 * g + jj
                        o_new, l_new = og[:, 128 * jj:128 * jj + 128], lg[:, 128 * jj:128 * jj + 128]
                        if pi > 0:
                            o_old, l_old = o_ref[j, rows, :], l_ref[j, rows, :]
                            mx = jnp.maximum(l_old, l_new)
                            ea, eb = jnp.exp(l_old - mx), jnp.exp(l_new - mx)
                            den = ea + eb
                            o_new = (ea * o_old + eb * o_new) / den
                            l_new = mx + jnp.log(den)
                        o_ref[j, rows, :] = o_new
                        l_ref[j, rows, :] = l_new
                return carry

            lax.fori_loop(0, S // ATT_BLK, block, 0)

        def to_cat(i, carry):
            rows = _rows(i, 256)
            for j in range(4):
                cat_ref[rows, 128 * j:128 * j + 128] = o_ref[j, rows, :].astype(BF16)
            return carry

        lax.fori_loop(0, S // 256, to_cat, 0)
        xc.finish()

    slab = jax.ShapeDtypeStruct((4, S, 128), F32)
    return _carry("att_fwd", body, exchange, exchange_args, (aq, ak, av), [VMEM] * 3, [VMEM] * 3,
                  [slab, slab, jax.ShapeDtypeStruct((S, 512), BF16)])


def _mix_fwd(cat_r, cat_a, wout, x, g2, g3):
    tm = 256

    def body(cr_ref, ca_ref, w_ref, x_ref, g2_ref, g3_ref, mix_ref, x2_ref, h3_ref):
        mix = _nn(cr_ref[...], w_ref[0:512, :]) + _nn(ca_ref[...], w_ref[512:1024, :])
        mix_ref[...] = mix
        x2 = x_ref[...] + mix * _rstd(mix) * g2_ref[...]
        x2_ref[...] = x2
        h3_ref[...] = (x2 * _rstd(x2) * g3_ref[...]).astype(BF16)

    row = lambda w: pl.BlockSpec((tm, w), lambda i: (i, 0))
    vec = pl.BlockSpec((1, D), lambda i: (0, 0))
    return pl.pallas_call(
        body, grid=(S // tm,), name="mix_fwd",
        in_specs=[row(512), row(512), pl.BlockSpec((D, D), lambda i: (0, 0)), row(D), vec, vec],
        out_specs=[row(D), row(D), row(D)],
        out_shape=[jax.ShapeDtypeStruct((S, D), F32), jax.ShapeDtypeStruct((S, D), F32),
                   jax.ShapeDtypeStruct((S, D), BF16)],
        compiler_params=_params("parallel"),
    )(cat_r, cat_a, wout, x, g2, g3)


def _ffn_fwd(h3, wg, wu, wd):
    tm = 256

    def body(h_ref, wg_ref, wu_ref, wd_ref, gt_ref, up_ref, a_ref, f_ref):
        k, i = pl.program_id(0), pl.program_id(1)
        h = h_ref[...]
        gt = _nt(h, wg_ref[...])
        up = _nt(h, wu_ref[...])
        gt_ref[...] = gt.astype(BF16)
        up_ref[...] = up.astype(BF16)
        a = (gt * _sigmoid(gt) * up).astype(BF16)
        a_ref[...] = a
        part = _nn(a, wd_ref[...])
        rows = _rows(i, tm)

        @pl.when(k == 0)
        def _():
            f_ref[rows, :] = part

        @pl.when(k > 0)
        def _():
            f_ref[rows, :] = f_ref[rows, :] + part

    wrow = pl.BlockSpec((None, FF_C, D), lambda k, i: (k, 0, 0))
    act = pl.BlockSpec((None, tm, FF_C), lambda k, i: (k, i, 0))
    return pl.pallas_call(
        body, grid=(N_CHIP, S // tm), name="ffn_fwd",
        in_specs=[pl.BlockSpec((tm, D), lambda k, i: (i, 0)), wrow, wrow, wrow],
        out_specs=[act, act, act, pl.BlockSpec((S, D), lambda k, i: (0, 0))],
        out_shape=[jax.ShapeDtypeStruct((N_CHIP, S, FF_C), BF16)] * 3 + [jax.ShapeDtypeStruct((S, D), F32)],
        compiler_params=_params("arbitrary", "arbitrary"),
    )(h3, wg, wu, wd)


def _head_bwd(f, x2, tgt, g4):
    tm = 256

    def body(f_ref, x2_ref, t_ref, g_ref, loss_ref, dy_ref, df_ref, dg_ref):
        @pl.when(pl.program_id(0) == 0)
        def _():
            loss_ref[...] = jnp.zeros_like(loss_ref)
            dg_ref[...] = jnp.zeros_like(dg_ref)

        fv = f_ref[...]
        r = _rstd(fv)
        fn = fv * r
        e = x2_ref[...] + fn * g_ref[...] - t_ref[...]
        sq = jnp.sum(jnp.sum(e * e, axis=-1, keepdims=True), axis=0, keepdims=True)
        loss_ref[...] = loss_ref[...] + sq
        dy = e * (1.0 / D)
        dy_ref[...] = dy
        dg_ref[...] = dg_ref[...] + jnp.sum(dy * fn, axis=0, keepdims=True)
        t = dy * g_ref[...]
        df_ref[...] = (r * (t - fn * jnp.mean(t * fn, axis=-1, keepdims=True))).astype(BF16)

    row = pl.BlockSpec((tm, D), lambda i: (i, 0))
    vec = pl.BlockSpec((1, D), lambda i: (0, 0))
    return pl.pallas_call(
        body, grid=(S // tm,), name="head_bwd",
        in_specs=[row, row, row, vec],
        out_specs=[pl.BlockSpec((8, 128), lambda i: (0, 0)), row, row, vec],
        out_shape=[jax.ShapeDtypeStruct((8, 128), F32), jax.ShapeDtypeStruct((S, D), F32),
                   jax.ShapeDtypeStruct((S, D), BF16), jax.ShapeDtypeStruct((1, D), F32)],
        compiler_params=_params("arbitrary"),
    )(f, x2, tgt, g4)


def _ffn_bwd_act(df, gt, up, wg, wu, wd):
    tm = 512

    def body(df_ref, gt_ref, up_ref, wg_ref, wu_ref, wd_ref, dgt_ref, dup_ref, dh_ref):
        k = pl.program_id(1)
        da = _nt(df_ref[...], wd_ref[...])
        gt, up = gt_ref[...].astype(F32), up_ref[...].astype(F32)
        sg = _sigmoid(gt)
        dup = (da * gt * sg).astype(BF16)
        dgt = (da * up * (sg * (1.0 + gt * (1.0 - sg)))).astype(BF16)
        dup_ref[...] = dup
        dgt_ref[...] = dgt
        part = _nn(dgt, wg_ref[...]) + _nn(dup, wu_ref[...])

        @pl.when(k == 0)
        def _():
            dh_ref[...] = part

        @pl.when(k > 0)
        def _():
            dh_ref[...] = dh_ref[...] + part

    wrow = pl.BlockSpec((None, FF_C, D), lambda i, k: (k, 0, 0))
    act = pl.BlockSpec((None, tm, FF_C), lambda i, k: (k, i, 0))
    row = pl.BlockSpec((tm, D), lambda i, k: (i, 0))
    return pl.pallas_call(
        body, grid=(S // tm, N_CHIP), name="ffn_bwd_act",
        in_specs=[row, act, act, wrow, wrow, wrow],
        out_specs=[act, act, row],
        out_shape=[jax.ShapeDtypeStruct((N_CHIP, S, FF_C), BF16), jax.ShapeDtypeStruct((N_CHIP, S, FF_C), BF16),
                   jax.ShapeDtypeStruct((S, D), F32)],
        compiler_params=_params("parallel", "arbitrary"),
    )(df, gt, up, wg, wu, wd)


def _ffn_bwd_w(a, df, h3, dgt, dup):
    tm = 1024
    assert S // tm == 2

    def body(a_ref, df_ref, h_ref, dgt_ref, dup_ref, dwd_ref, dwg_ref, dwu_ref, acc_d, acc_g, acc_u):
        i = pl.program_id(1)
        h = h_ref[...]
        parts = (_tn(a_ref[...], df_ref[...]), _tn(dgt_ref[...], h), _tn(dup_ref[...], h))

        @pl.when(i == 0)
        def _():
            for acc, part in zip((acc_d, acc_g, acc_u), parts):
                acc[...] = part

        @pl.when(i == S // tm - 1)
        def _():
            for out, acc, part in zip((dwd_ref, dwg_ref, dwu_ref), (acc_d, acc_g, acc_u), parts):
                out[...] = (acc[...] + part).astype(BF16)

    act = pl.BlockSpec((None, tm, FF_C), lambda k, i: (k, i, 0))
    row = pl.BlockSpec((tm, D), lambda k, i: (i, 0))
    wrow = pl.BlockSpec((None, FF_C, D), lambda k, i: (k, 0, 0))
    return pl.pallas_call(
        body, grid=(N_CHIP, S // tm), name="ffn_bwd_w",
        in_specs=[act, row, row, act, act],
        out_specs=[wrow, wrow, wrow],
        out_shape=[jax.ShapeDtypeStruct((N_CHIP, FF_C, D), BF16)] * 3,
        scratch_shapes=[pltpu.VMEM((FF_C, D), F32)] * 3,
        compiler_params=_params("parallel", "arbitrary"),
    )(a, df, h3, dgt, dup)


def _norm_bwd(dh3, dy, x2, mix, g2, g3, exchange, exchange_args):
    tm = 256

    def body(dh_ref, dy_ref, x2_ref, mix_ref, g2_ref, g3_ref, dx2_ref, dmix_ref, dg3_ref, dg2_ref, xc):
        @pl.when(pl.program_id(0) == 0)
        def _():
            xc.start()
            dg3_ref[...] = jnp.zeros_like(dg3_ref)
            dg2_ref[...] = jnp.zeros_like(dg2_ref)

        x2 = x2_ref[...]
        r3 = _rstd(x2)
        xn = x2 * r3
        dh = dh_ref[...]
        dg3_ref[...] = dg3_ref[...] + jnp.sum(dh * xn, axis=0, keepdims=True)
        t = dh * g3_ref[...]
        dx2 = dy_ref[...] + r3 * (t - xn * jnp.mean(t * xn, axis=-1, keepdims=True))
        dx2_ref[...] = dx2
        mix = mix_ref[...]
        r2 = _rstd(mix)
        mn = mix * r2
        dg2_ref[...] = dg2_ref[...] + jnp.sum(dx2 * mn, axis=0, keepdims=True)
        u = dx2 * g2_ref[...]
        dmix_ref[...] = (r2 * (u - mn * jnp.mean(u * mn, axis=-1, keepdims=True))).astype(BF16)

        @pl.when(pl.program_id(0) == S // tm - 1)
        def _():
            xc.middle()
            xc.finish()

    row = pl.BlockSpec((tm, D), lambda i: (i, 0))
    vec = pl.BlockSpec((1, D), lambda i: (0, 0))
    return _carry("norm_bwd", body, exchange, exchange_args, (dh3, dy, x2, mix, g2, g3),
                  [row, row, row, row, vec, vec], [row, row, vec, vec],
                  [jax.ShapeDtypeStruct((S, D), F32), jax.ShapeDtypeStruct((S, D), BF16),
                   jax.ShapeDtypeStruct((1, D), F32), jax.ShapeDtypeStruct((1, D), F32)],
                  grid=(S // tm,), semantics=("arbitrary",))


def _mix_bwd(dmix, cat_r, cat_a, wout):
    tm = 512

    def body(dm_ref, cr_ref, ca_ref, w_ref, dret_ref, datt_ref, dw_ref, acc):
        i = pl.program_id(0)

        @pl.when(i == 0)
        def _():
            acc[...] = jnp.zeros_like(acc)

        dm = dm_ref[...]
        dret_ref[...] = _nt(dm, w_ref[0:512, :])
        datt = _nt(dm, w_ref[512:1024, :])
        for j in range(4):
            datt_ref[j] = datt[:, 128 * j:128 * j + 128]
        acc[0:512, :] += _tn(cr_ref[...], dm)
        acc[512:1024, :] += _tn(ca_ref[...], dm)

        @pl.when(i == S // tm - 1)
        def _():
            dw_ref[...] = acc[...].astype(BF16)

    row = lambda w: pl.BlockSpec((tm, w), lambda i: (i, 0))
    full = pl.BlockSpec((D, D), lambda i: (0, 0))
    return pl.pallas_call(
        body, grid=(S // tm,), name="mix_bwd",
        in_specs=[row(D), row(512), row(512), full],
        out_specs=[row(512), pl.BlockSpec((4, tm, 128), lambda i: (0, i, 0)), full],
        out_shape=[jax.ShapeDtypeStruct((S, 512), F32), jax.ShapeDtypeStruct((4, S, 128), F32),
                   jax.ShapeDtypeStruct((D, D), BF16)],
        scratch_shapes=[pltpu.VMEM((D, D), F32)],
        compiler_params=_params("arbitrary"),
    )(dmix, cat_r, cat_a, wout)


def _att_bwd(aq, ak, av, datt, att_out, lse, exchange, exchange_args):
    def body(q_ref, k_ref, v_ref, do_ref, out_ref, l_ref, dq_ref, dk_ref, dv_ref, xc):
        xc.start()

        def clear(i, carry):
            rows = _rows(i, 256)
            for ref in (dq_ref, dk_ref, dv_ref):
                for j in range(4):
                    ref[j, rows, :] = jnp.zeros((256, 128), F32)
            return carry

        lax.fori_loop(0, S // 256, clear, 0)
        lane_head = lax.broadcasted_iota(jnp.int32, (ATT_BLK, 256), 1) // 64
        for d in PATTERN_DILATIONS:
            nb, has_prev = _att_blocks(d)

            def block(b, carry, d=d, nb=nb, has_prev=has_prev):
                r, ib = b // nb, b % nb
                rows = _class_rows(ib, r, d)
                prow = _class_rows(jnp.maximum(ib - 1, 0), r, d)
                valid = _att_mask(ib, has_prev)
                for g in range(2):
                    qg = _slab_pair(q_ref, g, rows).astype(BF16)
                    kg = _slab_pair(k_ref, g, rows)
                    vg = _slab_pair(v_ref, g, rows)
                    if has_prev:
                        kg = jnp.concatenate([_slab_pair(k_ref, g, prow), kg], axis=0)
                        vg = jnp.concatenate([_slab_pair(v_ref, g, prow), vg], axis=0)
                    kg, vg = kg.astype(BF16), vg.astype(BF16)
                    dog = _slab_pair(do_ref, g, rows)
                    outg = _slab_pair(out_ref, g, rows)
                    lg = _slab_pair(l_ref, g, rows)
                    qs = _stack_heads(qg, lane_head)
                    dos = _stack_heads(dog, lane_head)
                    delta = jnp.sum(dos * jnp.concatenate([outg] * 4, axis=0), axis=-1, keepdims=True)
                    lh = jnp.max(_stack_heads(lg, lane_head, NEG), axis=-1, keepdims=True)
                    s = jnp.where(valid, _nt(qs, kg) * ATT_SCALE, NEG)
                    p = jnp.exp(s - lh)
                    dosb = dos.astype(BF16)
                    ds = (p * (_nt(dosb, vg) - delta) * ATT_SCALE).astype(BF16)
                    dq = _unstack_heads(_nn(ds, kg), lane_head)
                    dk = _tn(ds, qs)
                    dv = _tn(p.astype(BF16), dosb)
                    for jj in range(2):
                        j, sl = 2 * g + jj, slice(128 * jj, 128 * jj + 128)
                        dq_ref[j, rows, :] += dq[:, sl]
                        if has_prev:
                            dk_ref[j, prow, :] += dk[0:ATT_BLK, sl]
                            dv_ref[j, prow, :] += dv[0:ATT_BLK, sl]
                            dk_ref[j, rows, :] += dk[ATT_BLK:2 * ATT_BLK, sl]
                            dv_ref[j, rows, :] += dv[ATT_BLK:2 * ATT_BLK, sl]
                        else:
                            dk_ref[j, rows, :] += dk[:, sl]
                            dv_ref[j, rows, :] += dv[:, sl]
                return carry

            lax.fori_loop(0, S // ATT_BLK, block, 0)
        xc.middle()
        xc.finish()

    slab = jax.ShapeDtypeStruct((4, S, 128), F32)
    return _carry("att_bwd", body, exchange, exchange_args, (aq, ak, av, datt, att_out, lse), [VMEM] * 6, [VMEM] * 3,
                  [slab, slab, slab])


def _ret_bwd(qr, kr, rv, proj, o_raw, states, dret, tabs, exchange, exchange_args):
    C = RET_C
    nc = S // C
    dtab, a_tab, b_tab, lam, bd = tabs

    def body(q_ref, k_ref, v_ref, g_ref, o_ref, st_ref, dr_ref, dt_ref, a_ref, b_ref, lam_ref, bd_ref,
             dq_ref, dk_ref, dv_ref, dg_ref, dR, exch):
        @pl.when(pl.program_id(0) == 0)
        def _():
            exch.start()
            dR[...] = jnp.zeros_like(dR)

        q, k, v = q_ref[...], k_ref[...], v_ref[...]
        lane_head = lax.broadcasted_iota(jnp.int32, (C, 256), 1) // 32
        col_head = lax.broadcasted_iota(jnp.int32, (C, 256), 1) // 64
        dos = []
        for j in range(4):
            sl = slice(128 * j, 128 * j + 128)
            oj = o_ref[:, sl]
            xc = oj - _seg_mean(oj)
            rs = lax.rsqrt(_seg_mean(xc * xc) + GN_EPS)
            rn = xc * rs
            gj = g_ref[:, sl]
            sg = _sigmoid(gj)
            dret = dr_ref[:, sl]
            dg_ref[:, sl] = dret * rn * (sg * (1.0 + gj * (1.0 - sg)))
            drn = dret * (gj * sg)
            dos.append(rs * (drn - _seg_mean(drn) - rn * _seg_mean(drn * rn)))
        do = [jnp.concatenate(dos[0:2], axis=1), jnp.concatenate(dos[2:4], axis=1)]
        do8 = jnp.concatenate(do, axis=1).astype(BF16)
        drb = dR[...].astype(BF16)
        rb = st_ref[...]
        dq = _nt(do8, rb) * a_ref[...]
        dk = _nt(v, drb) * b_ref[...]
        kb = (k.astype(F32) * b_ref[...]).astype(BF16)
        dvall = _nn(kb, drb)
        qs = _stack_heads(q, lane_head, n=8)
        dec = dt_ref[...]
        p = (_nt(qs, k) * dec).astype(BF16)
        dos = [_stack_heads(do[g], col_head).astype(BF16) for g in range(2)]
        dp = jnp.concatenate([_nt(dos[g], v[:, 256 * g:256 * g + 256]) for g in range(2)], axis=0)
        ds = (dp * dec).astype(BF16)
        dq = dq + _unstack_heads(_nn(ds, k), lane_head, n=8)
        dk = dk + _tn(ds, qs)
        dv = [dvall[:, 256 * g:256 * g + 256] + _tn(p[4 * C * g:4 * C * (g + 1)], dos[g]) for g in range(2)]
        qa = (q.astype(F32) * a_ref[...]).astype(BF16)
        dR[...] = dR[...] * lam_ref[...] + _tn(qa, do8) * bd_ref[...]
        dq_ref[...] = dq
        dk_ref[...] = dk
        dv_ref[:, 0:256] = dv[0]
        dv_ref[:, 256:512] = dv[1]

        @pl.when(pl.program_id(0) == nc - 1)
        def _():
            exch.middle()
            exch.finish()

    rev = lambda w: pl.BlockSpec((C, w), lambda n: (nc - 1 - n, 0))
    full = lambda a: pl.BlockSpec(a.shape, lambda n: (0,) * a.ndim)
    return _carry(
        "ret_bwd", body, exchange, exchange_args, (qr, kr, rv, proj, o_raw, states, dret, dtab, a_tab, b_tab, lam, bd),
        [rev(256), rev(256), rev(512), pl.BlockSpec((C, 512), lambda n: (nc - 1 - n, 2)), rev(512),
         pl.BlockSpec((None, 256, 512), lambda n: (nc - 1 - n, 0, 0)), rev(512),
         full(dtab), full(a_tab), full(b_tab), full(lam), full(bd)],
        [rev(256), rev(256), rev(512), rev(512)],
        [jax.ShapeDtypeStruct((S, 256), F32), jax.ShapeDtypeStruct((S, 256), F32),
         jax.ShapeDtypeStruct((S, 512), F32), jax.ShapeDtypeStruct((S, 512), F32)],
        scratch_shapes=[pltpu.VMEM((256, 512), F32)], grid=(nc,), semantics=("arbitrary",))


def _rot_bwd(cos, sin, dqr, dkr, drv, drg, dq_att, dk_att, dv_att):
    tm = 256

    def body(cos_ref, sin_ref, dqr_ref, dkr_ref, drv_ref, drg_ref, dqa_ref, dka_ref, dva_ref, dp_ref):
        cr, ca, sr, sa = cos_ref[:, 0:256], cos_ref[:, 256:768], sin_ref[:, 0:256], sin_ref[:, 256:768]
        lo_r, lo_a = _rot_halves(tm)

        def unrot_r(g):
            gs = g * sr
            return g * cr + pltpu.roll(jnp.where(lo_r, -gs, 0.0), 16, 1) + pltpu.roll(jnp.where(lo_r, 0.0, gs), 240, 1)

        def unrot_a(g):
            gs = g * sa
            return g * ca + pltpu.roll(jnp.where(lo_a, -gs, 0.0), 8, 1) + pltpu.roll(jnp.where(lo_a, 0.0, gs), 504, 1)

        def wide(ref):
            return jnp.concatenate([ref[j] for j in range(4)], axis=1)

        dp_ref[:, 0:256] = unrot_r(dqr_ref[...]).astype(BF16)
        dp_ref[:, 256:512] = unrot_r(dkr_ref[...] * RET_SCALE).astype(BF16)
        dp_ref[:, 512:1024] = drv_ref[...].astype(BF16)
        dp_ref[:, 1024:1536] = drg_ref[...].astype(BF16)
        dp_ref[:, 1536:2048] = unrot_a(wide(dqa_ref)).astype(BF16)
        dp_ref[:, 2048:2560] = unrot_a(wide(dka_ref)).astype(BF16)
        dp_ref[:, 2560:3072] = wide(dva_ref).astype(BF16)

    row = lambda w: pl.BlockSpec((tm, w), lambda i: (i, 0))
    slab = pl.BlockSpec((4, tm, 128), lambda i: (0, i, 0))
    return pl.pallas_call(
        body, grid=(S // tm,), name="rot_bwd",
        in_specs=[row(768), row(768), row(256), row(256), row(512), row(512), slab, slab, slab],
        out_specs=row(PW), out_shape=jax.ShapeDtypeStruct((S, PW), BF16),
        compiler_params=_params("parallel"),
    )(cos, sin, dqr, dkr, drv, drg, dq_att, dk_att, dv_att)


def _win_bwd_w(h1, dproj):
    tm = 512

    def body(h_ref, dp_ref, dw_ref, acc):
        i = pl.program_id(1)

        @pl.when(i == 0)
        def _():
            acc[...] = jnp.zeros_like(acc)

        acc[...] += _tn(h_ref[...], dp_ref[...])

        @pl.when(i == S // tm - 1)
        def _():
            dw_ref[...] = acc[...].astype(BF16)

    return pl.pallas_call(
        body, grid=(N_CHIP, S // tm), name="win_bwd_w",
        in_specs=[pl.BlockSpec((tm, D), lambda k, i: (i, 0)), pl.BlockSpec((tm, WIN_C), lambda k, i: (i, k))],
        out_specs=pl.BlockSpec((None, D, WIN_C), lambda k, i: (k, 0, 0)),
        out_shape=jax.ShapeDtypeStruct((N_CHIP, D, WIN_C), BF16),
        scratch_shapes=[pltpu.VMEM((D, WIN_C), F32)],
        compiler_params=_params("parallel", "arbitrary"),
    )(h1, dproj)


def _in_bwd(dproj, win_g, x, dx2, g1, exchange, exchange_args):
    tm = 256

    def body(dp_ref, w_ref, x_ref, dx2_ref, g_ref, dx_ref, dg_ref, xc):
        @pl.when(pl.program_id(0) == 0)
        def _():
            xc.start()
            dg_ref[...] = jnp.zeros_like(dg_ref)

        dh = _nt(dp_ref[:, 0:WIN_C], w_ref[0])
        for k in range(1, N_CHIP):
            dh = dh + _nt(dp_ref[:, k * WIN_C:(k + 1) * WIN_C], w_ref[k])
        xv = x_ref[...]
        r = _rstd(xv)
        xn = xv * r
        dg_ref[...] = dg_ref[...] + jnp.sum(dh * xn, axis=0, keepdims=True)
        t = dh * g_ref[...]
        dx_ref[...] = dx2_ref[...] + r * (t - xn * jnp.mean(t * xn, axis=-1, keepdims=True))

        @pl.when(pl.program_id(0) == S // tm - 1)
        def _():
            xc.middle()
            xc.finish()

    row = lambda w: pl.BlockSpec((tm, w), lambda i: (i, 0))
    vec = pl.BlockSpec((1, D), lambda i: (0, 0))
    return _carry("in_bwd", body, exchange, exchange_args, (dproj, win_g, x, dx2, g1),
                  [row(PW), pl.BlockSpec((N_CHIP, D, WIN_C), lambda i: (0, 0, 0)), row(D), row(D), vec],
                  [row(D), vec], [jax.ShapeDtypeStruct((S, D), F32), jax.ShapeDtypeStruct((1, D), F32)],
                  grid=(S // tm,), semantics=("arbitrary",))


ANY = pl.BlockSpec(memory_space=pl.ANY)
VMEM = pl.BlockSpec(memory_space=pltpu.VMEM)
FLIPS = ((1, 0), (0, 1), (1, 1))


def _place():
    x, y, c = lax.axis_index("x"), lax.axis_index("y"), lax.axis_index("c")
    chips = [((1 - x) if fx else x, (1 - y) if fy else y) for fx, fy in FLIPS]
    return x, y, c, 2 * x + y, chips


def _remote(src, dst, send_sem, recv_sem, device):
    return pltpu.make_async_remote_copy(src_ref=src, dst_ref=dst, send_sem=send_sem, recv_sem=recv_sem,
                                        device_id=device, device_id_type=MESH)


def _staggered(issue):
    c = lax.axis_index("c")

    @pl.when(c == 0)
    def _():
        issue((0, 1, 2))

    @pl.when(c == 1)
    def _():
        issue((1, 0, 2))


class _Exchange:
    aliases = {}

    def middle(self, ins, outs, sems):
        pass


class _GatherShards(_Exchange):
    def __init__(self, shards):
        n = self.n = len(shards)
        self.n_in = self.n_out = n
        self.out_shape = [jax.ShapeDtypeStruct((N_CHIP,) + s.shape, s.dtype) for s in shards]
        dma = pltpu.SemaphoreType.DMA
        self.scratch = [dma((3 * n,)), dma((3 * n,)), dma((3 * n,)), dma((3 * n,)), dma((n,)), dma((n,))]

    def _ici(self, ins, outs, sems, a, j, chip):
        x, y, c, me, chips = _place()
        half = ins[a].shape[0] // 2
        return _remote(ins[a].at[pl.ds(c * half, half), :], outs[a].at[me, pl.ds(c * half, half), :],
                       sems[0].at[3 * a + j], sems[1].at[3 * a + j], (*chip, c))

    def _fwd(self, outs, sems, a, j, chip, half_of):
        x, y, c, me, chips = _place()
        half = outs[a].shape[1] // 2
        blk = outs[a].at[2 * chip[0] + chip[1], pl.ds(half_of * half, half), :]
        return _remote(blk, blk, sems[2].at[3 * a + j], sems[3].at[3 * a + j], (x, y, 1 - c))

    def _own(self, ins, outs, sems, a):
        return _own_shard_to_sibling(ins[a], outs[a], sems[4].at[a], sems[5].at[a])

    def start(self, ins, outs, sems):
        chips = _place()[4]

        def issue(order):
            for a in range(self.n):
                for j in order:
                    self._ici(ins, outs, sems, a, j, chips[j]).start()

        _staggered(issue)
        for a in range(self.n):
            self._own(ins, outs, sems, a).start()

    def middle(self, ins, outs, sems):
        x, y, c, me, chips = _place()
        for a in range(self.n):
            for j, chip in enumerate(chips):
                half = outs[a].shape[1] // 2
                blk = outs[a].at[2 * chip[0] + chip[1], pl.ds(c * half, half), :]
                _remote(blk, blk, sems[0].at[3 * a + j], sems[1].at[3 * a + j], (x, y, c)).wait_recv()
                self._fwd(outs, sems, a, j, chip, c).start()

    def finish(self, ins, outs, sems):
        x, y, c, me, chips = _place()
        for a in range(self.n):
            for j, chip in enumerate(chips):
                self._fwd(outs, sems, a, j, chip, 1 - c).wait_recv()
        for a in range(self.n):
            for j, chip in enumerate(chips):
                self._ici(ins, outs, sems, a, j, chip).wait_send()
                self._fwd(outs, sems, a, j, chip, c).wait_send()
            self._own(ins, outs, sems, a).wait()


def _own_shard_to_sibling(shard_ref, gathered_ref, send_sem, recv_sem):
    x, y, c, me, chips = _place()
    return _remote(shard_ref, gathered_ref.at[me], send_sem, recv_sem, (x, y, 1 - c))


class _ForwardGathered(_Exchange):
    def __init__(self, shards):
        n = self.n = len(shards)
        self.n_in, self.n_out = 2 * n, n
        self.out_shape = [jax.ShapeDtypeStruct((N_CHIP,) + s.shape, s.dtype) for s in shards]
        dma = pltpu.SemaphoreType.DMA
        self.scratch = [dma((3 * n,)), dma((3 * n,)), dma((n,)), dma((n,))]
        self.aliases = {n + a: a for a in range(n)}

    def _fwd(self, outs, sems, a, j, chip, half_of):
        x, y, c, me, chips = _place()
        half = outs[a].shape[1] // 2
        blk = outs[a].at[2 * chip[0] + chip[1], pl.ds(half_of * half, half), :]
        return _remote(blk, blk, sems[0].at[3 * a + j], sems[1].at[3 * a + j], (x, y, 1 - c))

    def _own(self, ins, outs, sems, a):
        return _own_shard_to_sibling(ins[a], outs[a], sems[2].at[a], sems[3].at[a])

    def start(self, ins, outs, sems):
        x, y, c, me, chips = _place()
        for a in range(self.n):
            for j, chip in enumerate(chips):
                self._fwd(outs, sems, a, j, chip, c).start()
        for a in range(self.n):
            self._own(ins, outs, sems, a).start()

    def finish(self, ins, outs, sems):
        x, y, c, me, chips = _place()
        for a in range(self.n):
            for j, chip in enumerate(chips):
                self._fwd(outs, sems, a, j, chip, 1 - c).wait_recv()
        for a in range(self.n):
            for j, chip in enumerate(chips):
                self._fwd(outs, sems, a, j, chip, c).wait_send()
            self._own(ins, outs, sems, a).wait()


HBM = pl.BlockSpec(memory_space=pltpu.HBM)
SEMS = pl.BlockSpec(memory_space=pltpu.SEMAPHORE)
DATAFLOW = pltpu.SideEffectType.DATAFLOW_SIDE_EFFECTING


def _ici_half(shard_ref, land_ref, sems, a, j, chip, dst_chip):
    c = lax.axis_index("c")
    half = shard_ref.shape[0] // 2
    return _remote(shard_ref.at[pl.ds(c * half, half), :], land_ref.at[dst_chip, pl.ds(c * half, half), :],
                   sems[0].at[3 * a + j], sems[1].at[3 * a + j], (*chip, c))


def _gather_over_chips_start(shards, after):
    n = len(shards)

    def body(*refs):
        ins, lands, token = refs[:n], refs[n:2 * n], refs[-1]
        sems = refs[2 * n + 1:2 * n + 3]
        x, y, c, me, chips = _place()

        def issue(order):
            for a in range(n):
                for j in order:
                    _ici_half(ins[a], lands[a], sems, a, j, chips[j], me).start()

        _staggered(issue)
        token[...] = jnp.zeros_like(token)

    dma = pltpu.SemaphoreType.DMA
    lands = [pltpu.with_memory_space_constraint(lax.empty((N_CHIP,) + s.shape, s.dtype), pltpu.HBM) for s in shards]
    outs = pl.pallas_call(
        body, name="gather_over_chips_start",
        out_shape=[dma((3 * n,)), dma((3 * n,))] + [pltpu.HBM(s.shape, s.dtype) for s in shards]
                  + [pltpu.HBM(l.shape, l.dtype) for l in lands] + [jax.ShapeDtypeStruct((8, 128), F32)],
        in_specs=[HBM] * (2 * n) + [ANY], out_specs=[SEMS, SEMS] + [HBM] * (2 * n) + [VMEM],
        input_output_aliases={a: 2 + a for a in range(2 * n)},
        compiler_params=pltpu.CompilerParams(has_side_effects=DATAFLOW),
    )(*[pltpu.with_memory_space_constraint(s, pltpu.HBM) for s in shards], *lands, after)
    return outs[0:2], outs[2:2 + n], outs[2 + n:2 + 2 * n], outs[-1]


def _gather_over_chips_wait(sems, shards, lands, after):
    n = len(shards)

    def body(*refs):
        ins, lz = refs[:n], refs[n:2 * n]
        sm = refs[2 * n:2 * n + 2]
        x, y, c, me, chips = _place()
        for a in range(n):
            for j, chip in enumerate(chips):
                cp = _ici_half(ins[a], lz[a], sm, a, j, chip, 2 * chip[0] + chip[1])
                cp.wait_send()
                cp.wait_recv()

    outs = pl.pallas_call(
        body, name="gather_over_chips_wait",
        out_shape=[pltpu.HBM(s.shape, s.dtype) for s in shards] + [pltpu.HBM(l.shape, l.dtype) for l in lands],
        in_specs=[HBM] * (2 * n) + [SEMS, SEMS, ANY], out_specs=[HBM] * (2 * n),
        input_output_aliases={a: a for a in range(2 * n)},
        compiler_params=pltpu.CompilerParams(has_side_effects=DATAFLOW),
    )(*shards, *lands, *sems, after)
    return outs[:n], outs[n:]


class _HalvesToSibling(_Exchange):
    def __init__(self, grads):
        n = self.n = len(grads)
        self.n_in = self.n_out = n
        self.out_shape = [jax.ShapeDtypeStruct((N_CHIP, g.shape[1] // 2, g.shape[2]), g.dtype) for g in grads]
        self.scratch = [pltpu.SemaphoreType.DMA((n,)), pltpu.SemaphoreType.DMA((n,))]

    def _copy(self, ins, outs, sems, a):
        x, y, c, me, chips = _place()
        half = ins[a].shape[1] // 2
        return _remote(ins[a].at[:, pl.ds((1 - c) * half, half), :], outs[a], sems[0].at[a], sems[1].at[a], (x, y, 1 - c))

    def start(self, ins, outs, sems):
        for a in range(self.n):
            self._copy(ins, outs, sems, a).start()

    def finish(self, ins, outs, sems):
        for a in range(self.n):
            self._copy(ins, outs, sems, a).wait_recv()
        for a in range(self.n):
            self._copy(ins, outs, sems, a).wait_send()


class _OverChips(_Exchange):
    def __init__(self, pre):
        n = self.n = len(pre)
        self.n_in = self.n_out = n
        self.out_shape = [jax.ShapeDtypeStruct(p.shape, p.dtype) for p in pre]
        dma = pltpu.SemaphoreType.DMA
        self.scratch = [dma((3 * n,)), dma((3 * n,))]

    def _ici(self, ins, outs, sems, a, j, chip):
        x, y, c, me, chips = _place()
        return _remote(ins[a].at[2 * chip[0] + chip[1]], outs[a].at[me], sems[0].at[3 * a + j], sems[1].at[3 * a + j],
                       (*chip, c))

    def start(self, ins, outs, sems):
        chips = _place()[4]

        def issue(order):
            for a in range(self.n):
                for j in order:
                    self._ici(ins, outs, sems, a, j, chips[j]).start()

        _staggered(issue)

    def finish(self, ins, outs, sems):
        x, y, c, me, chips = _place()
        for a in range(self.n):
            for j, chip in enumerate(chips):
                blk = outs[a].at[2 * chip[0] + chip[1]]
                _remote(blk, blk, sems[0].at[3 * a + j], sems[1].at[3 * a + j], (x, y, c)).wait_recv()
        for a in range(self.n):
            for j, chip in enumerate(chips):
                self._ici(ins, outs, sems, a, j, chip).wait_send()


class _ShareHalves(_Exchange):
    def __init__(self, fulls):
        n = self.n = len(fulls)
        self.n_in = self.n_out = n
        self.out_shape = [jax.ShapeDtypeStruct(f.shape, f.dtype) for f in fulls]
        self.scratch = [pltpu.SemaphoreType.DMA((n,)), pltpu.SemaphoreType.DMA((n,))]
        self.aliases = {a: a for a in range(n)}

    def _copy(self, outs, sems, a, half_of):
        x, y, c, me, chips = _place()
        half = outs[a].shape[0] // 2
        rows = outs[a].at[pl.ds(half_of * half, half), :]
        return _remote(rows, rows, sems[0].at[a], sems[1].at[a], (x, y, 1 - c))

    def start(self, ins, outs, sems):
        c = _place()[2]
        for a in range(self.n):
            self._copy(outs, sems, a, c).start()

    def finish(self, ins, outs, sems):
        c = _place()[2]
        for a in range(self.n):
            self._copy(outs, sems, a, 1 - c).wait_recv()
        for a in range(self.n):
            self._copy(outs, sems, a, c).wait_send()


class _GatherBlocks(_Exchange):
    def __init__(self, block):
        self.n_in = self.n_out = 1
        self.out_shape = [jax.ShapeDtypeStruct((8,) + block.shape, block.dtype)]
        dma = pltpu.SemaphoreType.DMA
        self.scratch = [dma((7,)), dma((7,)), dma]

    @staticmethod
    def _peer(f):
        x, y, c, me, chips = _place()
        return ((1 - x) if f & 4 else x, (1 - y) if f & 2 else y, (1 - c) if f & 1 else c)

    def start(self, ins, outs, sems):
        x, y, c, me, chips = _place()
        for f in range(1, 8):
            _remote(ins[0], outs[0].at[2 * me + c], sems[0].at[f - 1], sems[1].at[f - 1], self._peer(f)).start()
        pltpu.make_async_copy(ins[0], outs[0].at[2 * me + c], sems[2]).start()

    def finish(self, ins, outs, sems):
        x, y, c, me, chips = _place()
        for f in range(1, 8):
            px, py, pc = self._peer(f)
            blk = outs[0].at[4 * px + 2 * py + pc]
            _remote(blk, blk, sems[0].at[f - 1], sems[1].at[f - 1], (x, y, c)).wait_recv()
        for f in range(1, 8):
            _remote(ins[0], outs[0].at[2 * me + c], sems[0].at[f - 1], sems[1].at[f - 1], self._peer(f)).wait_send()
        pltpu.make_async_copy(ins[0], outs[0].at[2 * me + c], sems[2]).wait()


class _Both(_Exchange):
    def __init__(self, first, second):
        self.parts = (first, second)
        self.n_in, self.n_out = first.n_in + second.n_in, first.n_out + second.n_out
        self.out_shape = first.out_shape + second.out_shape
        self.scratch = first.scratch + second.scratch
        self.aliases = dict(first.aliases)
        self.aliases.update({first.n_in + i: first.n_out + o for i, o in second.aliases.items()})

    def _split(self, ins, outs, sems):
        a, b = self.parts
        return ((a, ins[:a.n_in], outs[:a.n_out], sems[:len(a.scratch)]),
                (b, ins[a.n_in:], outs[a.n_out:], sems[len(a.scratch):]))

    def start(self, ins, outs, sems):
        for ex, i, o, s in self._split(ins, outs, sems):
            ex.start(i, o, s)

    def middle(self, ins, outs, sems):
        for ex, i, o, s in self._split(ins, outs, sems):
            ex.middle(i, o, s)

    def finish(self, ins, outs, sems):
        for ex, i, o, s in self._split(ins, outs, sems):
            ex.finish(i, o, s)


class _Bound:
    def __init__(self, ex, ins, outs, sems):
        self.start = lambda: ex.start(ins, outs, sems)
        self.middle = lambda: ex.middle(ins, outs, sems)
        self.finish = lambda: ex.finish(ins, outs, sems)


def _carry(name, body, ex, ex_args, args, in_specs, out_specs, out_shape, scratch_shapes=(), grid=None, semantics=()):
    n_a, n_o, n_s = len(args), len(out_shape), len(scratch_shapes)

    def full_body(*refs):
        p = 0
        groups = []
        for size in (n_a, ex.n_in, n_o, ex.n_out, n_s, len(ex.scratch)):
            groups.append(refs[p:p + size])
            p += size
        a, ei, o, eo, s, es = groups
        body(*a, *o, *s, _Bound(ex, ei, eo, es))

    kwargs = {} if grid is None else {"grid": grid}
    outs = pl.pallas_call(
        full_body, name=name,
        in_specs=list(in_specs) + [ANY] * ex.n_in, out_specs=list(out_specs) + [ANY] * ex.n_out,
        out_shape=list(out_shape) + list(ex.out_shape), scratch_shapes=list(scratch_shapes) + list(ex.scratch),
        input_output_aliases={n_a + i: n_o + o for i, o in ex.aliases.items()},
        compiler_params=_params(*semantics) if semantics else pltpu.CompilerParams(vmem_limit_bytes=VMEM_LIMIT),
        **kwargs,
    )(*args, *ex_args)
    return outs[:n_o], outs[n_o:]


def _exchange_alone(name, ex, ex_args):
    def body(xc):
        xc.start()
        xc.middle()
        xc.finish()

    return _carry(name, body, ex, ex_args, (), (), (), ())[1]


def _core_index():
    return lax.axis_index("c").astype(jnp.int32).reshape(1)


def _pair_sum(gs, gots):
    n = len(gs)
    _, r, cc = gs[0].shape
    half = r // 2

    def body(c_ref, *refs):
        for a in range(n):
            refs[2 * n + a][...] = (refs[a][...].astype(F32) + refs[n + a][...].astype(F32)).astype(BF16)

    mine = pl.BlockSpec((None, half, cc), lambda k, c_ref: (k, c_ref[0], 0))
    blk = pl.BlockSpec((None, half, cc), lambda k, c_ref: (k, 0, 0))
    return pl.pallas_call(
        body, name=f"pair_sum_{r}x{cc}",
        grid_spec=pltpu.PrefetchScalarGridSpec(
            num_scalar_prefetch=1, grid=(N_CHIP,), in_specs=[mine] * n + [blk] * n, out_specs=[blk] * n),
        out_shape=[jax.ShapeDtypeStruct((N_CHIP, half, cc), BF16)] * n,
        compiler_params=_params("parallel"),
    )(_core_index(), *gs, *gots)


def _chip_sum(pre, parts):
    n = len(parts)
    _, half, cc = parts[0].shape
    tr = half // 2
    me = 2 * lax.axis_index("x") + lax.axis_index("y")
    others = [k + (k >= me).astype(jnp.int32) for k in range(3)]
    where = jnp.stack([lax.axis_index("c"), me, *others]).astype(jnp.int32)

    def body(w_ref, *refs):
        for a in range(n):
            own, p1, p2, p3 = refs[4 * a:4 * a + 4]
            refs[4 * n + a][...] = ((own[...].astype(F32) + p1[...].astype(F32)) + p2[...].astype(F32)) + p3[...].astype(F32)

    slot = lambda s: pl.BlockSpec((None, tr, cc), lambda i, w_ref: (w_ref[s], i, 0))
    operands = []
    for a in range(n):
        operands += [pre[a], parts[a], parts[a], parts[a]]
    return pl.pallas_call(
        body, name=f"chip_sum_{half}x{cc}",
        grid_spec=pltpu.PrefetchScalarGridSpec(
            num_scalar_prefetch=1, grid=(2,),
            in_specs=[slot(1), slot(2), slot(3), slot(4)] * n,
            out_specs=[pl.BlockSpec((tr, cc), lambda i, w_ref: (2 * w_ref[0] + i, 0))] * n),
        out_shape=[jax.ShapeDtypeStruct((2 * half, cc), F32)] * n,
        compiler_params=_params("parallel"),
    )(where, *operands)


def _adamw_math(w, g, m, v):
    m = ADAM_B1 * m + (1.0 - ADAM_B1) * g
    v = ADAM_B2 * v + (1.0 - ADAM_B2) * (g * g)
    m_hat = m / (1.0 - ADAM_B1 ** ADAM_STEP)
    v_hat = v / (1.0 - ADAM_B2 ** ADAM_STEP)
    delta = -ADAM_LR * (m_hat / (jnp.sqrt(v_hat) + ADAM_EPS) + ADAM_WD * w)
    return delta, m, v


def _adamw(w, g, m, v):
    r, cc = w.shape
    tr = r // 4

    def body(w_ref, g_ref, m_ref, v_ref, go_ref, d_ref, nm_ref, nv_ref):
        g = g_ref[...]
        go_ref[...] = g
        d_ref[...], nm_ref[...], nv_ref[...] = _adamw_math(w_ref[...], g, m_ref[...], v_ref[...])

    blk = pl.BlockSpec((tr, cc), lambda i: (i, 0))
    return pl.pallas_call(
        body, grid=(4,), name=f"adamw_{r}x{cc}",
        in_specs=[blk] * 4, out_specs=[blk] * 4,
        out_shape=[jax.ShapeDtypeStruct((r, cc), F32)] * 4,
        compiler_params=_params("parallel"),
    )(w, g, m, v)


def _pack8(rows):
    def body(*refs):
        out_ref = refs[-1]
        out_ref[...] = jnp.zeros_like(out_ref)
        for i, r in enumerate(refs[:-1]):
            out_ref[i:i + 1, :] = r[...]

    return pl.pallas_call(body, name="pack8", out_shape=jax.ShapeDtypeStruct((8, D), F32))(*rows)


def _adamw_gains(gall, w8, m8, v8):
    def body(ga_ref, w_ref, m_ref, v_ref, g_ref, d_ref, nm_ref, nv_ref):
        g = ga_ref[0]
        for dev in range(1, 8):
            g = g + ga_ref[dev]
        g_ref[...] = g
        d_ref[...], nm_ref[...], nv_ref[...] = _adamw_math(w_ref[...], g, m_ref[...], v_ref[...])

    return pl.pallas_call(
        body, name="adamw_gains",
        out_shape=[jax.ShapeDtypeStruct((8, D), F32)] * 4,
    )(gall, w8, m8, v8)


def kernel(x, positions, w_in, w_out, g_pre_mix, g_post_mix, g_pre_ffn, g_post_ffn, w_gate, w_up, w_down, loss_target, m_w_in, m_w_out, m_g_pre_mix, m_g_post_mix, m_g_pre_ffn, m_g_post_ffn, m_w_gate, m_w_up, m_w_down, v_w_in, v_w_out, v_g_pre_mix, v_g_post_mix, v_g_pre_ffn, v_g_post_ffn, v_w_gate, v_w_up, v_w_down):
    tr = lambda t: jnp.swapaxes(t, 1, 2)[0]
    shards = [w_in[0], w_out[0], tr(w_gate), tr(w_up), w_down[0]]
    moms = [m_w_in[0], m_w_out[0], tr(m_w_gate), tr(m_w_up), m_w_down[0]]
    vels = [v_w_in[0], v_w_out[0], tr(v_w_gate), tr(v_w_up), v_w_down[0]]
    xs, pos, tgt = x[0], positions.reshape(S, 1), loss_target[0]
    g1, g2, g3, g4 = g_pre_mix, g_post_mix, g_pre_ffn, g_post_ffn
    tabs = tuple(jnp.asarray(t) for t in _retention_tables())
    ifc, spread = _rotary_tables()
    ifc, spread = jnp.asarray(ifc), jnp.asarray(spread, dtype=BF16)
    bf = [s.astype(BF16) for s in shards]

    win_g, wout_g = _exchange_alone("gather_in", _GatherShards(bf[:2]), bf[:2])
    wout_g = wout_g.reshape(D, D)
    sems, ffn_sh, ffn_lands, token = _gather_over_chips_start(bf[2:], win_g)
    proj, h1 = _proj_fwd(xs, g1 + token[0:1, 0:1], win_g)
    qr, kr, rv, aq, ak, av, cos, sin = _rot_fwd(proj, pos, ifc, spread)
    o_raw, cat_r, states = _ret_fwd(qr, kr, rv, proj, tabs)
    ffn_sh, ffn_lands = _gather_over_chips_wait(sems, ffn_sh, ffn_lands, o_raw)
    (att_out, lse, cat_a), (wg_g, wu_g, wd_g) = _att_fwd(aq, ak, av, _ForwardGathered(bf[2:]), [*ffn_sh, *ffn_lands])
    mix, x2, h3 = _mix_fwd(cat_r, cat_a, wout_g, xs, g2, g3)
    gt, up, a, f = _ffn_fwd(h3, wg_g, wu_g, wd_g)

    sq, dy, df, dg4 = _head_bwd(f, x2, tgt, g4)
    loss = 0.5 * lax.psum(sq[0, 0], ("x", "y", "c")) / D
    dgt, dup, dh3 = _ffn_bwd_act(df, gt, up, wg_g, wu_g, wd_g)
    ffn_grads = list(_ffn_bwd_w(a, df, h3, dgt, dup))
    (dx2, dmix, dg3, dg2), got = _norm_bwd(dh3, dy, x2, mix, g2, g3, _HalvesToSibling(ffn_grads), ffn_grads)
    pre = _pair_sum(ffn_grads, got)
    dret, datt, dwout = _mix_bwd(dmix, cat_r, cat_a, wout_g)
    (dq_att, dk_att, dv_att), parts = _att_bwd(aq, ak, av, datt, att_out, lse, _OverChips(pre), pre)
    sums = _chip_sum(pre, parts)
    (dqr, dkr, drv, drg), ffn_full = _ret_bwd(qr, kr, rv, proj, o_raw, states, dret, tabs, _ShareHalves(sums), sums)
    dproj = _rot_bwd(cos, sin, dqr, dkr, drv, drg, dq_att, dk_att, dv_att)
    in_grads = [_win_bwd_w(h1, dproj), dwout.reshape(N_CHIP, WOUT_R, D)]
    (dx, dg1), got = _in_bwd(dproj, win_g, xs, dx2, g1, _HalvesToSibling(in_grads), in_grads)

    pre = [*_pair_sum(in_grads[:1], got[:1]), *_pair_sum(in_grads[1:], got[1:])]
    gblock = _pack8([dg1, dg2, dg3, dg4])
    *parts, gall = _exchange_alone("reduce_rest", _Both(_OverChips(pre), _GatherBlocks(gblock)), pre + [gblock])
    sums = [*_chip_sum(pre[:1], parts[:1]), *_chip_sum(pre[1:], parts[1:])]
    in_full = _exchange_alone("share_rest", _ShareHalves(sums), sums)

    full = [in_full[0], in_full[1], ffn_full[1], ffn_full[2], ffn_full[0]]
    upd = [_adamw(w, g, m, v) for w, g, m, v in zip(shards, full, moms, vels)]
    gg, gd, gm, gv = _adamw_gains(gall, _pack8([g1, g2, g3, g4]),
                                  _pack8([m_g_pre_mix, m_g_post_mix, m_g_pre_ffn, m_g_post_ffn]),
                                  _pack8([v_g_pre_mix, v_g_post_mix, v_g_pre_ffn, v_g_post_ffn]))

    def order(mats, vecs):
        back = lambda t: jnp.swapaxes(t[None], 1, 2)
        return ([mats[0][None], mats[1][None]] + [vecs[i:i + 1] for i in range(4)]
                + [back(mats[2]), back(mats[3]), mats[4][None]])

    return (loss, dx[None],
            *order([u[0] for u in upd], gg),
            *order([u[1] for u in upd], gd),
            *order([u[2] for u in upd], gm),
            *order([u[3] for u in upd], gv))
```

```python
import functools

import numpy as np
import jax
import jax.numpy as jnp
from jax import lax
from jax.experimental import pallas as pl
from jax.experimental.pallas import tpu as pltpu

F32, BF16 = jnp.float32, jnp.bfloat16
MESH = pl.DeviceIdType.MESH

S = 2048
D = 1024
PW = 3072
N_CHIP = 4
WIN_C = PW // N_CHIP
DFF = 2816
FF_C = DFF // N_CHIP
WOUT_R = D // N_CHIP
RMS_EPS = 1e-6
GN_EPS = 1e-5
RET_C = 128
RET_SCALE = 32 ** -0.5
ATT_BLK = 128
ATT_SCALE = 64 ** -0.5
PATTERN_DILATIONS = (1, 4, 16)
NEG = -1e30
VMEM_LIMIT = 56 * 1024 * 1024

ADAM_LR, ADAM_B1, ADAM_B2, ADAM_EPS, ADAM_WD, ADAM_STEP = 0.001, 0.9, 0.999, 1e-08, 0.01, 10


def _params(*sem):
    return pltpu.CompilerParams(dimension_semantics=sem, vmem_limit_bytes=VMEM_LIMIT)


def _nt(a, b):
    return lax.dot_general(a, b, (((1,), (1,)), ((), ())), preferred_element_type=F32)


def _tn(a, b):
    return lax.dot_general(a, b, (((0,), (0,)), ((), ())), preferred_element_type=F32)


def _nn(a, b):
    return jnp.dot(a, b, preferred_element_type=F32)


def _rstd(v):
    return lax.rsqrt(jnp.mean(v * v, axis=-1, keepdims=True) + RMS_EPS)


def _sigmoid(v):
    return 1.0 / (1.0 + jnp.exp(-v))


def _rows(i, t):
    return pl.ds(pl.multiple_of(i * t, t), t)


def _retention_tables():
    h = np.arange(8, dtype=np.float32)
    log_g = np.log1p(-np.exp2(-5.0 - h)).astype(np.float32)
    idx = np.arange(RET_C, dtype=np.float32)
    diff = idx[:, None] - idx[None, :]
    dtab = np.where(diff >= 0, np.exp(log_g[:, None, None] * np.maximum(diff, 0.0)), 0.0).astype(np.float32)
    dtab = dtab.reshape(8 * RET_C, RET_C)
    lane_head = np.arange(256) // 32
    a_tab = np.exp(log_g[lane_head][None, :] * (idx + 1.0)[:, None]).astype(np.float32)
    b_tab = np.exp(log_g[lane_head][None, :] * (RET_C - 1.0 - idx)[:, None]).astype(np.float32)
    lam = np.exp(log_g[lane_head] * RET_C).astype(np.float32)[:, None]
    bd = (lane_head[:, None] == (np.arange(512) // 64)[None, :]).astype(np.float32)
    return dtab, a_tab, b_tab, lam, bd


def _rotary_tables():
    inv_r = (1.0 / (np.float32(10000.0) ** np.linspace(0.0, 1.0, 16, dtype=np.float32))).astype(np.float32)
    inv_a = (np.float32(500000.0) ** (-np.arange(0, 16, 2, dtype=np.float32) / np.float32(16))).astype(np.float32)
    ifc = np.zeros((1, 128), np.float32)
    ifc[0, 0:16], ifc[0, 16:24] = inv_r, inv_a
    spread = np.zeros((128, 768), np.float32)
    for lane in range(256):
        spread[(lane % 32) % 16, lane] = 1.0
    for lane in range(512):
        d = lane % 64
        spread[16 + d % 8 if d < 16 else 24, 256 + lane] = 1.0
    return ifc, spread


def _proj_fwd(x, g1, win_g, after):
    tm = 256

    def body(x_ref, g_ref, w_ref, proj_ref, h_ref, _):
        xv = x_ref[...]
        h = (xv * _rstd(xv) * g_ref[...]).astype(BF16)
        h_ref[...] = h
        for k in range(N_CHIP):
            proj_ref[:, k * WIN_C:(k + 1) * WIN_C] = _nn(h, w_ref[k])

    return _carry(
        "proj_fwd", body, _NoExchange(), (), (x, g1, win_g),
        [pl.BlockSpec((tm, D), lambda i: (i, 0)), pl.BlockSpec((1, D), lambda i: (0, 0)),
         pl.BlockSpec((N_CHIP, D, WIN_C), lambda i: (0, 0, 0))],
        [pl.BlockSpec((tm, PW), lambda i: (i, 0)), pl.BlockSpec((tm, D), lambda i: (i, 0))],
        [jax.ShapeDtypeStruct((S, PW), F32), jax.ShapeDtypeStruct((S, D), BF16)],
        grid=(S // tm,), semantics=("parallel",), after=after)[0]


def _rot_halves(tm):
    lo_r = (lax.broadcasted_iota(jnp.int32, (tm, 256), 1) % 32) < 16
    lo_a = (lax.broadcasted_iota(jnp.int32, (tm, 512), 1) % 64) < 8
    return lo_r, lo_a


def _spread_exact(t, e):
    hi = t.astype(BF16)
    r1 = t - hi.astype(F32)
    mid = r1.astype(BF16)
    lo = (r1 - mid.astype(F32)).astype(BF16)
    return _nn(hi, e) + _nn(mid, e) + _nn(lo, e)


def _rot_fwd(proj, pos, ifc, spread):
    tm = 256

    def body(p_ref, pos_ref, ifc_ref, e_ref, qr_ref, kr_ref, rv_ref, aq_ref, ak_ref, av_ref, cos_ref, sin_ref):
        ang = pos_ref[...].astype(F32) * ifc_ref[...]
        cs = _spread_exact(jnp.cos(ang), e_ref[...])
        sn = _spread_exact(jnp.sin(ang), e_ref[...])
        cos_ref[...] = cs
        sin_ref[...] = sn
        cr, ca, sr, sa = cs[:, 0:256], cs[:, 256:768], sn[:, 0:256], sn[:, 256:768]
        lo_r, lo_a = _rot_halves(tm)

        def rot_r(v):
            return v * cr + sr * jnp.where(lo_r, -pltpu.roll(v, 240, 1), pltpu.roll(v, 16, 1))

        def rot_a(v):
            return v * ca + sa * jnp.where(lo_a, -pltpu.roll(v, 504, 1), pltpu.roll(v, 8, 1))

        qr_ref[...] = rot_r(p_ref[:, 0:256]).astype(BF16)
        kr_ref[...] = (rot_r(p_ref[:, 256:512]) * RET_SCALE).astype(BF16)
        rv_ref[...] = p_ref[:, 512:1024].astype(BF16)
        aq, ak = rot_a(p_ref[:, 1536:2048]), rot_a(p_ref[:, 2048:2560])
        for j in range(4):
            aq_ref[j] = aq[:, 128 * j:128 * j + 128]
            ak_ref[j] = ak[:, 128 * j:128 * j + 128]
            av_ref[j] = p_ref[:, 2560 + 128 * j:2560 + 128 * j + 128]

    row = lambda w: pl.BlockSpec((tm, w), lambda i: (i, 0))
    const = lambda w: pl.BlockSpec((1, w), lambda i: (0, 0))
    slab = pl.BlockSpec((4, tm, 128), lambda i: (0, i, 0))
    return pl.pallas_call(
        body, grid=(S // tm,), name="rot_fwd",
        in_specs=[row(PW), row(1), const(128), pl.BlockSpec((128, 768), lambda i: (0, 0))],
        out_specs=[row(256), row(256), row(512), slab, slab, slab, row(768), row(768)],
        out_shape=[jax.ShapeDtypeStruct((S, w), BF16) for w in (256, 256, 512)]
                  + [jax.ShapeDtypeStruct((4, S, 128), F32)] * 3 + [jax.ShapeDtypeStruct((S, 768), F32)] * 2,
        compiler_params=_params("parallel"),
    )(proj, pos, ifc, spread)


def _seg_mean(v):
    lo = lax.broadcasted_iota(jnp.int32, v.shape, 1) < 64
    s_lo = jnp.sum(jnp.where(lo, v, 0.0), axis=-1, keepdims=True)
    s_hi = jnp.sum(jnp.where(lo, 0.0, v), axis=-1, keepdims=True)
    return jnp.where(lo, s_lo, s_hi) * (1.0 / 64.0)


def _ret_fwd(qr, kr, rv, proj, tabs):
    C = RET_C
    dtab, a_tab, b_tab, lam, bd = tabs

    def body(q_ref, k_ref, v_ref, g_ref, dt_ref, a_ref, b_ref, lam_ref, bd_ref, o_ref, cat_ref, st_ref, R):
        @pl.when(pl.program_id(0) == 0)
        def _():
            R[...] = jnp.zeros_like(R)

        q, k, v = q_ref[...], k_ref[...], v_ref[...]
        lane_head = lax.broadcasted_iota(jnp.int32, (C, 256), 1) // 32
        col_head = lax.broadcasted_iota(jnp.int32, (C, 256), 1) // 64
        rb = R[...].astype(BF16)
        st_ref[...] = rb
        qa = (q.astype(F32) * a_ref[...]).astype(BF16)
        cross = _nn(qa, rb)
        p = (_nt(_stack_heads(q, lane_head, n=8), k) * dt_ref[...]).astype(BF16)
        og = [cross[:, 256 * g:256 * g + 256]
              + _unstack_heads(_nn(p[4 * C * g:4 * C * (g + 1)], v[:, 256 * g:256 * g + 256]), col_head)
              for g in range(2)]
        kb = (k.astype(F32) * b_ref[...]).astype(BF16)
        R[...] = R[...] * lam_ref[...] + _tn(kb, v) * bd_ref[...]
        o_ref[:, 0:256] = og[0]
        o_ref[:, 256:512] = og[1]
        for j in range(4):
            oj = og[j // 2][:, 128 * (j % 2):128 * (j % 2) + 128]
            xc = oj - _seg_mean(oj)
            rn = xc * lax.rsqrt(_seg_mean(xc * xc) + GN_EPS)
            gj = g_ref[:, 128 * j:128 * j + 128]
            cat_ref[:, 128 * j:128 * j + 128] = (rn * (gj * _sigmoid(gj))).astype(BF16)

    row = lambda w: pl.BlockSpec((C, w), lambda n: (n, 0))
    full = lambda a: pl.BlockSpec(a.shape, lambda n: (0,) * a.ndim)
    return pl.pallas_call(
        body, grid=(S // C,), name="ret_fwd",
        in_specs=[row(256), row(256), row(512), pl.BlockSpec((C, 512), lambda n: (n, 2)),
                  full(dtab), full(a_tab), full(b_tab), full(lam), full(bd)],
        out_specs=[row(512), row(512), pl.BlockSpec((None, 256, 512), lambda n: (n, 0, 0))],
        out_shape=[jax.ShapeDtypeStruct((S, 512), F32), jax.ShapeDtypeStruct((S, 512), BF16),
                   jax.ShapeDtypeStruct((S // C, 256, 512), BF16)],
        scratch_shapes=[pltpu.VMEM((256, 512), F32)],
        compiler_params=_params("arbitrary"),
    )(qr, kr, rv, proj, dtab, a_tab, b_tab, lam, bd)


def _stack_heads(v, lane_head, fill=0.0, n=4):
    return jnp.concatenate([jnp.where(lane_head == h, v, jnp.full_like(v, fill)) for h in range(n)], axis=0)


def _unstack_heads(v, lane_head, n=4):
    out = v[0:ATT_BLK]
    for h in range(1, n):
        out = jnp.where(lane_head == h, v[h * ATT_BLK:(h + 1) * ATT_BLK], out)
    return out


def _att_mask(ib, has_prev):
    nk = 2 * ATT_BLK if has_prev else ATT_BLK
    a = lax.broadcasted_iota(jnp.int32, (4 * ATT_BLK, nk), 0) % ATT_BLK
    kk = lax.broadcasted_iota(jnp.int32, (4 * ATT_BLK, nk), 1)
    if has_prev:
        dist = ATT_BLK + a - kk
        return (dist >= 0) & (dist <= ATT_BLK) & ((ib * ATT_BLK - ATT_BLK + kk) >= 0)
    return (a - kk) >= 0


def _class_rows(ib, r, d):
    if d == 1:
        return pl.ds(pl.multiple_of(ib * ATT_BLK, ATT_BLK), ATT_BLK)
    return pl.ds(ib * ATT_BLK * d + r, ATT_BLK, stride=d)


def _slab_pair(ref, g, rows):
    return jnp.concatenate([ref[2 * g, rows, :], ref[2 * g + 1, rows, :]], axis=1)


def _att_blocks(d):
    nb = S // d // ATT_BLK
    return nb, nb > 1


def _att_fwd(aq, ak, av, exchange, exchange_args):
    def body(q_ref, k_ref, v_ref, o_ref, l_ref, cat_ref, xc):
        xc.start()
        lane_head = lax.broadcasted_iota(jnp.int32, (ATT_BLK, 256), 1) // 64
        for pi, d in enumerate(PATTERN_DILATIONS):
            if pi == len(PATTERN_DILATIONS) - 1:
                xc.middle()
            nb, has_prev = _att_blocks(d)

            def block(b, carry, pi=pi, d=d, nb=nb, has_prev=has_prev):
                r, ib = b // nb, b % nb
                rows = _class_rows(ib, r, d)
                prow = _class_rows(jnp.maximum(ib - 1, 0), r, d)
                valid = _att_mask(ib, has_prev)
                for g in range(2):
                    qg = _slab_pair(q_ref, g, rows).astype(BF16)
                    kg = _slab_pair(k_ref, g, rows)
                    vg = _slab_pair(v_ref, g, rows)
                    if has_prev:
                        kg = jnp.concatenate([_slab_pair(k_ref, g, prow), kg], axis=0)
                        vg = jnp.concatenate([_slab_pair(v_ref, g, prow), vg], axis=0)
                    kg, vg = kg.astype(BF16), vg.astype(BF16)
                    s = jnp.where(valid, _nt(_stack_heads(qg, lane_head), kg) * ATT_SCALE, NEG)
                    m = jnp.max(s, axis=-1, keepdims=True)
                    p = jnp.exp(s - m)
                    den = jnp.sum(p, axis=-1, keepdims=True)
                    og = _unstack_heads(_nn(p.astype(BF16), vg) / den, lane_head)
                    lg = _unstack_heads(jnp.broadcast_to(m + jnp.log(den), (4 * ATT_BLK, 256)), lane_head)
                    for jj in range(2):
                        j = 2 * g + jj
                        o_new, l_new = og[:, 128 * jj:128 * jj + 128], lg[:, 128 * jj:128 * jj + 128]
                        if pi > 0:
                            o_old, l_old = o_ref[j, rows, :], l_ref[j, rows, :]
                            mx = jnp.maximum(l_old, l_new)
                            ea, eb = jnp.exp(l_old - mx), jnp.exp(l_new - mx)
                            den = ea + eb
                            o_new = (ea * o_old + eb * o_new) / den
                            l_new = mx + jnp.log(den)
                        o_ref[j, rows, :] = o_new
                        l_ref[j, rows, :] = l_new
                return carry

            lax.fori_loop(0, S // ATT_BLK, block, 0)

        def to_cat(i, carry):
            rows = _rows(i, 256)
            for j in range(4):
                cat_ref[rows, 128 * j:128 * j + 128] = o_ref[j, rows, :].astype(BF16)
            return carry

        lax.fori_loop(0, S // 256, to_cat, 0)
        xc.finish()

    slab = jax.ShapeDtypeStruct((4, S, 128), F32)
    return _carry("att_fwd", body, exchange, exchange_args, (aq, ak, av), [VMEM] * 3, [VMEM] * 3,
                  [slab, slab, jax.ShapeDtypeStruct((S, 512), BF16)])


def _mix_fwd(cat_r, cat_a, wout, x, g2, g3, exchange, exchange_args):
    tm = 256

    def body(cr_ref, ca_ref, w_ref, x_ref, g2_ref, g3_ref, mix_ref, x2_ref, h3_ref, xc):
        @pl.when(pl.program_id(0) == 0)
        def _():
            xc.start()

        mix = _nn(cr_ref[...], w_ref[0:512, :]) + _nn(ca_ref[...], w_ref[512:1024, :])
        mix_ref[...] = mix
        x2 = x_ref[...] + mix * _rstd(mix) * g2_ref[...]
        x2_ref[...] = x2
        h3_ref[...] = (x2 * _rstd(x2) * g3_ref[...]).astype(BF16)

        @pl.when(pl.program_id(0) == S // tm - 1)
        def _():
            xc.middle()
            xc.finish()

    row = lambda w: pl.BlockSpec((tm, w), lambda i: (i, 0))
    vec = pl.BlockSpec((1, D), lambda i: (0, 0))
    return _carry("mix_fwd", body, exchange, exchange_args, (cat_r, cat_a, wout, x, g2, g3),
                  [row(512), row(512), pl.BlockSpec((D, D), lambda i: (0, 0)), row(D), vec, vec],
                  [row(D), row(D), row(D)],
                  [jax.ShapeDtypeStruct((S, D), F32), jax.ShapeDtypeStruct((S, D), F32),
                   jax.ShapeDtypeStruct((S, D), BF16)],
                  grid=(S // tm,), semantics=("arbitrary",))


def _ffn_fwd(h3, wg, wu, wd):
    tm = 256

    def body(h_ref, wg_ref, wu_ref, wd_ref, gt_ref, up_ref, a_ref, f_ref):
        k, i = pl.program_id(0), pl.program_id(1)
        h = h_ref[...]
        gt = _nt(h, wg_ref[...])
        up = _nt(h, wu_ref[...])
        gt_ref[...] = gt.astype(BF16)
        up_ref[...] = up.astype(BF16)
        a = (gt * _sigmoid(gt) * up).astype(BF16)
        a_ref[...] = a
        part = _nn(a, wd_ref[...])
        rows = _rows(i, tm)

        @pl.when(k == 0)
        def _():
            f_ref[rows, :] = part

        @pl.when(k > 0)
        def _():
            f_ref[rows, :] = f_ref[rows, :] + part

    wrow = pl.BlockSpec((None, FF_C, D), lambda k, i: (k, 0, 0))
    act = pl.BlockSpec((None, tm, FF_C), lambda k, i: (k, i, 0))
    return pl.pallas_call(
        body, grid=(N_CHIP, S // tm), name="ffn_fwd",
        in_specs=[pl.BlockSpec((tm, D), lambda k, i: (i, 0)), wrow, wrow, wrow],
        out_specs=[act, act, act, pl.BlockSpec((S, D), lambda k, i: (0, 0))],
        out_shape=[jax.ShapeDtypeStruct((N_CHIP, S, FF_C), BF16)] * 3 + [jax.ShapeDtypeStruct((S, D), F32)],
        compiler_params=_params("arbitrary", "arbitrary"),
    )(h3, wg, wu, wd)


def _head_bwd(f, x2, tgt, g4):
    tm = 256

    def body(f_ref, x2_ref, t_ref, g_ref, loss_ref, dy_ref, df_ref, dg_ref):
        @pl.when(pl.program_id(0) == 0)
        def _():
            loss_ref[...] = jnp.zeros_like(loss_ref)
            dg_ref[...] = jnp.zeros_like(dg_ref)

        fv = f_ref[...]
        r = _rstd(fv)
        fn = fv * r
        e = x2_ref[...] + fn * g_ref[...] - t_ref[...]
        sq = jnp.sum(jnp.sum(e * e, axis=-1, keepdims=True), axis=0, keepdims=True)
        loss_ref[...] = loss_ref[...] + sq
        dy = e * (1.0 / D)
        dy_ref[...] = dy
        dg_ref[...] = dg_ref[...] + jnp.sum(dy * fn, axis=0, keepdims=True)
        t = dy * g_ref[...]
        df_ref[...] = (r * (t - fn * jnp.mean(t * fn, axis=-1, keepdims=True))).astype(BF16)

    row = pl.BlockSpec((tm, D), lambda i: (i, 0))
    vec = pl.BlockSpec((1, D), lambda i: (0, 0))
    return pl.pallas_call(
        body, grid=(S // tm,), name="head_bwd",
        in_specs=[row, row, row, vec],
        out_specs=[pl.BlockSpec((8, 128), lambda i: (0, 0)), row, row, vec],
        out_shape=[jax.ShapeDtypeStruct((8, 128), F32), jax.ShapeDtypeStruct((S, D), F32),
                   jax.ShapeDtypeStruct((S, D), BF16), jax.ShapeDtypeStruct((1, D), F32)],
        compiler_params=_params("arbitrary"),
    )(f, x2, tgt, g4)


def _ffn_bwd_act(df, gt, up, wg, wu, wd):
    tm = 512

    def body(df_ref, gt_ref, up_ref, wg_ref, wu_ref, wd_ref, dgt_ref, dup_ref, dh_ref):
        k = pl.program_id(1)
        da = _nt(df_ref[...], wd_ref[...])
        gt, up = gt_ref[...].astype(F32), up_ref[...].astype(F32)
        sg = _sigmoid(gt)
        dup = (da * gt * sg).astype(BF16)
        dgt = (da * up * (sg * (1.0 + gt * (1.0 - sg)))).astype(BF16)
        dup_ref[...] = dup
        dgt_ref[...] = dgt
        part = _nn(dgt, wg_ref[...]) + _nn(dup, wu_ref[...])

        @pl.when(k == 0)
        def _():
            dh_ref[...] = part

        @pl.when(k > 0)
        def _():
            dh_ref[...] = dh_ref[...] + part

    wrow = pl.BlockSpec((None, FF_C, D), lambda i, k: (k, 0, 0))
    act = pl.BlockSpec((None, tm, FF_C), lambda i, k: (k, i, 0))
    row = pl.BlockSpec((tm, D), lambda i, k: (i, 0))
    return pl.pallas_call(
        body, grid=(S // tm, N_CHIP), name="ffn_bwd_act",
        in_specs=[row, act, act, wrow, wrow, wrow],
        out_specs=[act, act, row],
        out_shape=[jax.ShapeDtypeStruct((N_CHIP, S, FF_C), BF16), jax.ShapeDtypeStruct((N_CHIP, S, FF_C), BF16),
                   jax.ShapeDtypeStruct((S, D), F32)],
        compiler_params=_params("parallel", "arbitrary"),
    )(df, gt, up, wg, wu, wd)


def _ffn_bwd_w(a, df, h3, dgt, dup):
    tm = 1024
    assert S // tm == 2

    def body(a_ref, df_ref, h_ref, dgt_ref, dup_ref, dwd_ref, dwg_ref, dwu_ref, acc_d, acc_g, acc_u):
        i = pl.program_id(1)
        h = h_ref[...]
        parts = (_tn(a_ref[...], df_ref[...]), _tn(dgt_ref[...], h), _tn(dup_ref[...], h))

        @pl.when(i == 0)
        def _():
            for acc, part in zip((acc_d, acc_g, acc_u), parts):
                acc[...] = part

        @pl.when(i == S // tm - 1)
        def _():
            for out, acc, part in zip((dwd_ref, dwg_ref, dwu_ref), (acc_d, acc_g, acc_u), parts):
                out[...] = (acc[...] + part).astype(BF16)

    act = pl.BlockSpec((None, tm, FF_C), lambda k, i: (k, i, 0))
    row = pl.BlockSpec((tm, D), lambda k, i: (i, 0))
    wrow = pl.BlockSpec((None, FF_C, D), lambda k, i: (k, 0, 0))
    return pl.pallas_call(
        body, grid=(N_CHIP, S // tm), name="ffn_bwd_w",
        in_specs=[act, row, row, act, act],
        out_specs=[wrow, wrow, wrow],
        out_shape=[jax.ShapeDtypeStruct((N_CHIP, FF_C, D), BF16)] * 3,
        scratch_shapes=[pltpu.VMEM((FF_C, D), F32)] * 3,
        compiler_params=_params("parallel", "arbitrary"),
    )(a, df, h3, dgt, dup)


def _norm_bwd(dh3, dy, x2, mix, g2, g3, exchange, exchange_args):
    tm = 256

    def body(dh_ref, dy_ref, x2_ref, mix_ref, g2_ref, g3_ref, dx2_ref, dmix_ref, dg3_ref, dg2_ref, xc):
        @pl.when(pl.program_id(0) == 0)
        def _():
            xc.start()
            dg3_ref[...] = jnp.zeros_like(dg3_ref)
            dg2_ref[...] = jnp.zeros_like(dg2_ref)

        x2 = x2_ref[...]
        r3 = _rstd(x2)
        xn = x2 * r3
        dh = dh_ref[...]
        dg3_ref[...] = dg3_ref[...] + jnp.sum(dh * xn, axis=0, keepdims=True)
        t = dh * g3_ref[...]
        dx2 = dy_ref[...] + r3 * (t - xn * jnp.mean(t * xn, axis=-1, keepdims=True))
        dx2_ref[...] = dx2
        mix = mix_ref[...]
        r2 = _rstd(mix)
        mn = mix * r2
        dg2_ref[...] = dg2_ref[...] + jnp.sum(dx2 * mn, axis=0, keepdims=True)
        u = dx2 * g2_ref[...]
        dmix_ref[...] = (r2 * (u - mn * jnp.mean(u * mn, axis=-1, keepdims=True))).astype(BF16)

        @pl.when(pl.program_id(0) == S // tm - 1)
        def _():
            xc.middle()
            xc.finish()

    row = pl.BlockSpec((tm, D), lambda i: (i, 0))
    vec = pl.BlockSpec((1, D), lambda i: (0, 0))
    return _carry("norm_bwd", body, exchange, exchange_args, (dh3, dy, x2, mix, g2, g3),
                  [row, row, row, row, vec, vec], [row, row, vec, vec],
                  [jax.ShapeDtypeStruct((S, D), F32), jax.ShapeDtypeStruct((S, D), BF16),
                   jax.ShapeDtypeStruct((1, D), F32), jax.ShapeDtypeStruct((1, D), F32)],
                  grid=(S // tm,), semantics=("arbitrary",))


def _mix_bwd(dmix, cat_r, cat_a, wout, after):
    tm = 512

    def body(dm_ref, cr_ref, ca_ref, w_ref, dret_ref, datt_ref, dw_ref, acc, _):
        i = pl.program_id(0)

        @pl.when(i == 0)
        def _():
            acc[...] = jnp.zeros_like(acc)

        dm = dm_ref[...]
        dret_ref[...] = _nt(dm, w_ref[0:512, :])
        datt = _nt(dm, w_ref[512:1024, :])
        for j in range(4):
            datt_ref[j] = datt[:, 128 * j:128 * j + 128]
        acc[0:512, :] += _tn(cr_ref[...], dm)
        acc[512:1024, :] += _tn(ca_ref[...], dm)

        @pl.when(i == S // tm - 1)
        def _():
            dw_ref[...] = acc[...].astype(BF16)

    row = lambda w: pl.BlockSpec((tm, w), lambda i: (i, 0))
    full = pl.BlockSpec((D, D), lambda i: (0, 0))
    return _carry("mix_bwd", body, _NoExchange(), (), (dmix, cat_r, cat_a, wout),
                  [row(D), row(512), row(512), full],
                  [row(512), pl.BlockSpec((4, tm, 128), lambda i: (0, i, 0)), full],
                  [jax.ShapeDtypeStruct((S, 512), F32), jax.ShapeDtypeStruct((4, S, 128), F32),
                   jax.ShapeDtypeStruct((D, D), BF16)],
                  scratch_shapes=[pltpu.VMEM((D, D), F32)], grid=(S // tm,), semantics=("arbitrary",), after=after)[0]


def _att_bwd(aq, ak, av, datt, att_out, lse, exchange, exchange_args):
    def body(q_ref, k_ref, v_ref, do_ref, out_ref, l_ref, dq_ref, dk_ref, dv_ref, xc):
        xc.start()

        def clear(i, carry):
            rows = _rows(i, 256)
            for ref in (dq_ref, dk_ref, dv_ref):
                for j in range(4):
                    ref[j, rows, :] = jnp.zeros((256, 128), F32)
            return carry

        lax.fori_loop(0, S // 256, clear, 0)
        lane_head = lax.broadcasted_iota(jnp.int32, (ATT_BLK, 256), 1) // 64
        for d in PATTERN_DILATIONS:
            nb, has_prev = _att_blocks(d)

            def block(b, carry, d=d, nb=nb, has_prev=has_prev):
                r, ib = b // nb, b % nb
                rows = _class_rows(ib, r, d)
                prow = _class_rows(jnp.maximum(ib - 1, 0), r, d)
                valid = _att_mask(ib, has_prev)
                for g in range(2):
                    qg = _slab_pair(q_ref, g, rows).astype(BF16)
                    kg = _slab_pair(k_ref, g, rows)
                    vg = _slab_pair(v_ref, g, rows)
                    if has_prev:
                        kg = jnp.concatenate([_slab_pair(k_ref, g, prow), kg], axis=0)
                        vg = jnp.concatenate([_slab_pair(v_ref, g, prow), vg], axis=0)
                    kg, vg = kg.astype(BF16), vg.astype(BF16)
                    dog = _slab_pair(do_ref, g, rows)
                    outg = _slab_pair(out_ref, g, rows)
                    lg = _slab_pair(l_ref, g, rows)
                    qs = _stack_heads(qg, lane_head)
                    dos = _stack_heads(dog, lane_head)
                    delta = jnp.sum(dos * jnp.concatenate([outg] * 4, axis=0), axis=-1, keepdims=True)
                    lh = jnp.max(_stack_heads(lg, lane_head, NEG), axis=-1, keepdims=True)
                    s = jnp.where(valid, _nt(qs, kg) * ATT_SCALE, NEG)
                    p = jnp.exp(s - lh)
                    dosb = dos.astype(BF16)
                    ds = (p * (_nt(dosb, vg) - delta) * ATT_SCALE).astype(BF16)
                    dq = _unstack_heads(_nn(ds, kg), lane_head)
                    dk = _tn(ds, qs)
                    dv = _tn(p.astype(BF16), dosb)
                    for jj in range(2):
                        j, sl = 2 * g + jj, slice(128 * jj, 128 * jj + 128)
                        dq_ref[j, rows, :] += dq[:, sl]
                        if has_prev:
                            dk_ref[j, prow, :] += dk[0:ATT_BLK, sl]
                            dv_ref[j, prow, :] += dv[0:ATT_BLK, sl]
                            dk_ref[j, rows, :] += dk[ATT_BLK:2 * ATT_BLK, sl]
                            dv_ref[j, rows, :] += dv[ATT_BLK:2 * ATT_BLK, sl]
                        else:
                            dk_ref[j, rows, :] += dk[:, sl]
                            dv_ref[j, rows, :] += dv[:, sl]
                return carry

            lax.fori_loop(0, S // ATT_BLK, block, 0)
        xc.middle()
        xc.finish()

    slab = jax.ShapeDtypeStruct((4, S, 128), F32)
    return _carry("att_bwd", body, exchange, exchange_args, (aq, ak, av, datt, att_out, lse), [VMEM] * 6, [VMEM] * 3,
                  [slab, slab, slab])


def _ret_bwd(qr, kr, rv, proj, o_raw, states, dret, tabs, exchange, exchange_args):
    C = RET_C
    nc = S // C
    dtab, a_tab, b_tab, lam, bd = tabs

    def body(q_ref, k_ref, v_ref, g_ref, o_ref, st_ref, dr_ref, dt_ref, a_ref, b_ref, lam_ref, bd_ref,
             dq_ref, dk_ref, dv_ref, dg_ref, dR, exch):
        @pl.when(pl.program_id(0) == 0)
        def _():
            exch.start()
            dR[...] = jnp.zeros_like(dR)

        q, k, v = q_ref[...], k_ref[...], v_ref[...]
        lane_head = lax.broadcasted_iota(jnp.int32, (C, 256), 1) // 32
        col_head = lax.broadcasted_iota(jnp.int32, (C, 256), 1) // 64
        dos = []
        for j in range(4):
            sl = slice(128 * j, 128 * j + 128)
            oj = o_ref[:, sl]
            xc = oj - _seg_mean(oj)
            rs = lax.rsqrt(_seg_mean(xc * xc) + GN_EPS)
            rn = xc * rs
            gj = g_ref[:, sl]
            sg = _sigmoid(gj)
            dret = dr_ref[:, sl]
            dg_ref[:, sl] = dret * rn * (sg * (1.0 + gj * (1.0 - sg)))
            drn = dret * (gj * sg)
            dos.append(rs * (drn - _seg_mean(drn) - rn * _seg_mean(drn * rn)))
        do = [jnp.concatenate(dos[0:2], axis=1), jnp.concatenate(dos[2:4], axis=1)]
        do8 = jnp.concatenate(do, axis=1).astype(BF16)
        drb = dR[...].astype(BF16)
        rb = st_ref[...]
        dq = _nt(do8, rb) * a_ref[...]
        dk = _nt(v, drb) * b_ref[...]
        kb = (k.astype(F32) * b_ref[...]).astype(BF16)
        dvall = _nn(kb, drb)
        qs = _stack_heads(q, lane_head, n=8)
        dec = dt_ref[...]
        p = (_nt(qs, k) * dec).astype(BF16)
        dos = [_stack_heads(do[g], col_head).astype(BF16) for g in range(2)]
        dp = jnp.concatenate([_nt(dos[g], v[:, 256 * g:256 * g + 256]) for g in range(2)], axis=0)
        ds = (dp * dec).astype(BF16)
        dq = dq + _unstack_heads(_nn(ds, k), lane_head, n=8)
        dk = dk + _tn(ds, qs)
        dv = [dvall[:, 256 * g:256 * g + 256] + _tn(p[4 * C * g:4 * C * (g + 1)], dos[g]) for g in range(2)]
        qa = (q.astype(F32) * a_ref[...]).astype(BF16)
        dR[...] = dR[...] * lam_ref[...] + _tn(qa, do8) * bd_ref[...]
        dq_ref[...] = dq
        dk_ref[...] = dk
        dv_ref[:, 0:256] = dv[0]
        dv_ref[:, 256:512] = dv[1]

        @pl.when(pl.program_id(0) == nc - 1)
        def _():
            exch.middle()
            exch.finish()

    rev = lambda w: pl.BlockSpec((C, w), lambda n: (nc - 1 - n, 0))
    full = lambda a: pl.BlockSpec(a.shape, lambda n: (0,) * a.ndim)
    return _carry(
        "ret_bwd", body, exchange, exchange_args, (qr, kr, rv, proj, o_raw, states, dret, dtab, a_tab, b_tab, lam, bd),
        [rev(256), rev(256), rev(512), pl.BlockSpec((C, 512), lambda n: (nc - 1 - n, 2)), rev(512),
         pl.BlockSpec((None, 256, 512), lambda n: (nc - 1 - n, 0, 0)), rev(512),
         full(dtab), full(a_tab), full(b_tab), full(lam), full(bd)],
        [rev(256), rev(256), rev(512), rev(512)],
        [jax.ShapeDtypeStruct((S, 256), F32), jax.ShapeDtypeStruct((S, 256), F32),
         jax.ShapeDtypeStruct((S, 512), F32), jax.ShapeDtypeStruct((S, 512), F32)],
        scratch_shapes=[pltpu.VMEM((256, 512), F32)], grid=(nc,), semantics=("arbitrary",))


def _rot_bwd(cos, sin, dqr, dkr, drv, drg, dq_att, dk_att, dv_att):
    tm = 256

    def body(cos_ref, sin_ref, dqr_ref, dkr_ref, drv_ref, drg_ref, dqa_ref, dka_ref, dva_ref, dp_ref):
        cr, ca, sr, sa = cos_ref[:, 0:256], cos_ref[:, 256:768], sin_ref[:, 0:256], sin_ref[:, 256:768]
        lo_r, lo_a = _rot_halves(tm)

        def unrot_r(g):
            gs = g * sr
            return g * cr + pltpu.roll(jnp.where(lo_r, -gs, 0.0), 16, 1) + pltpu.roll(jnp.where(lo_r, 0.0, gs), 240, 1)

        def unrot_a(g):
            gs = g * sa
            return g * ca + pltpu.roll(jnp.where(lo_a, -gs, 0.0), 8, 1) + pltpu.roll(jnp.where(lo_a, 0.0, gs), 504, 1)

        def wide(ref):
            return jnp.concatenate([ref[j] for j in range(4)], axis=1)

        dp_ref[:, 0:256] = unrot_r(dqr_ref[...]).astype(BF16)
        dp_ref[:, 256:512] = unrot_r(dkr_ref[...] * RET_SCALE).astype(BF16)
        dp_ref[:, 512:1024] = drv_ref[...].astype(BF16)
        dp_ref[:, 1024:1536] = drg_ref[...].astype(BF16)
        dp_ref[:, 1536:2048] = unrot_a(wide(dqa_ref)).astype(BF16)
        dp_ref[:, 2048:2560] = unrot_a(wide(dka_ref)).astype(BF16)
        dp_ref[:, 2560:3072] = wide(dva_ref).astype(BF16)

    row = lambda w: pl.BlockSpec((tm, w), lambda i: (i, 0))
    slab = pl.BlockSpec((4, tm, 128), lambda i: (0, i, 0))
    return pl.pallas_call(
        body, grid=(S // tm,), name="rot_bwd",
        in_specs=[row(768), row(768), row(256), row(256), row(512), row(512), slab, slab, slab],
        out_specs=row(PW), out_shape=jax.ShapeDtypeStruct((S, PW), BF16),
        compiler_params=_params("parallel"),
    )(cos, sin, dqr, dkr, drv, drg, dq_att, dk_att, dv_att)


def _win_bwd_w(h1, dproj, exchange, exchange_args):
    tm = 512

    def body(h_ref, dp_ref, dw_ref, acc, xc):
        k, i = pl.program_id(0), pl.program_id(1)

        @pl.when((k == 0) & (i == 0))
        def _():
            xc.start()

        @pl.when(i == 0)
        def _():
            acc[...] = jnp.zeros_like(acc)

        acc[...] += _tn(h_ref[...], dp_ref[...])

        @pl.when(i == S // tm - 1)
        def _():
            dw_ref[...] = acc[...].astype(BF16)

        @pl.when((k == N_CHIP - 1) & (i == S // tm - 1))
        def _():
            xc.middle()
            xc.finish()

    (dw,), out = _carry(
        "win_bwd_w", body, exchange, exchange_args, (h1, dproj),
        [pl.BlockSpec((tm, D), lambda k, i: (i, 0)), pl.BlockSpec((tm, WIN_C), lambda k, i: (i, k))],
        [pl.BlockSpec((None, D, WIN_C), lambda k, i: (k, 0, 0))],
        [jax.ShapeDtypeStruct((N_CHIP, D, WIN_C), BF16)],
        scratch_shapes=[pltpu.VMEM((D, WIN_C), F32)], grid=(N_CHIP, S // tm), semantics=("arbitrary", "arbitrary"))
    return dw, out


def _in_bwd(dproj, win_g, x, dx2, g1, exchange, exchange_args):
    tm = 256

    def body(dp_ref, w_ref, x_ref, dx2_ref, g_ref, dx_ref, dg_ref, xc):
        @pl.when(pl.program_id(0) == 0)
        def _():
            xc.start()
            dg_ref[...] = jnp.zeros_like(dg_ref)

        dh = _nt(dp_ref[:, 0:WIN_C], w_ref[0])
        for k in range(1, N_CHIP):
            dh = dh + _nt(dp_ref[:, k * WIN_C:(k + 1) * WIN_C], w_ref[k])
        xv = x_ref[...]
        r = _rstd(xv)
        xn = xv * r
        dg_ref[...] = dg_ref[...] + jnp.sum(dh * xn, axis=0, keepdims=True)
        t = dh * g_ref[...]
        dx_ref[...] = dx2_ref[...] + r * (t - xn * jnp.mean(t * xn, axis=-1, keepdims=True))

        @pl.when(pl.program_id(0) == S // tm - 1)
        def _():
            xc.middle()
            xc.finish()

    row = lambda w: pl.BlockSpec((tm, w), lambda i: (i, 0))
    vec = pl.BlockSpec((1, D), lambda i: (0, 0))
    return _carry("in_bwd", body, exchange, exchange_args, (dproj, win_g, x, dx2, g1),
                  [row(PW), pl.BlockSpec((N_CHIP, D, WIN_C), lambda i: (0, 0, 0)), row(D), row(D), vec],
                  [row(D), vec], [jax.ShapeDtypeStruct((S, D), F32), jax.ShapeDtypeStruct((1, D), F32)],
                  grid=(S // tm,), semantics=("arbitrary",))


ANY = pl.BlockSpec(memory_space=pl.ANY)
VMEM = pl.BlockSpec(memory_space=pltpu.VMEM)
FLIPS = ((1, 0), (0, 1), (1, 1))


def _place():
    x, y, c = lax.axis_index("x"), lax.axis_index("y"), lax.axis_index("c")
    chips = [((1 - x) if fx else x, (1 - y) if fy else y) for fx, fy in FLIPS]
    return x, y, c, 2 * x + y, chips


def _remote(src, dst, send_sem, recv_sem, device):
    return pltpu.make_async_remote_copy(src_ref=src, dst_ref=dst, send_sem=send_sem, recv_sem=recv_sem,
                                        device_id=device, device_id_type=MESH)


def _staggered(issue):
    c = lax.axis_index("c")

    @pl.when(c == 0)
    def _():
        issue((0, 1, 2))

    @pl.when(c == 1)
    def _():
        issue((1, 0, 2))


class _Exchange:
    aliases = {}

    def middle(self, ins, outs, sems):
        pass


class _GatherShards(_Exchange):
    def __init__(self, shards):
        n = self.n = len(shards)
        self.n_in = self.n_out = n
        self.out_shape = [jax.ShapeDtypeStruct((N_CHIP,) + s.shape, s.dtype) for s in shards]
        dma = pltpu.SemaphoreType.DMA
        self.scratch = [dma((3 * n,)), dma((3 * n,)), dma((3 * n,)), dma((3 * n,)), dma((n,)), dma((n,))]

    def _ici(self, ins, outs, sems, a, j, chip):
        x, y, c, me, chips = _place()
        half = ins[a].shape[0] // 2
        return _remote(ins[a].at[pl.ds(c * half, half), :], outs[a].at[me, pl.ds(c * half, half), :],
                       sems[0].at[3 * a + j], sems[1].at[3 * a + j], (*chip, c))

    def _fwd(self, outs, sems, a, j, chip, half_of):
        x, y, c, me, chips = _place()
        half = outs[a].shape[1] // 2
        blk = outs[a].at[2 * chip[0] + chip[1], pl.ds(half_of * half, half), :]
        return _remote(blk, blk, sems[2].at[3 * a + j], sems[3].at[3 * a + j], (x, y, 1 - c))

    def _own(self, ins, outs, sems, a):
        return _own_shard_to_sibling(ins[a], outs[a], sems[4].at[a], sems[5].at[a])

    def start(self, ins, outs, sems):
        chips = _place()[4]

        def issue(order):
            for a in range(self.n):
                for j in order:
                    self._ici(ins, outs, sems, a, j, chips[j]).start()

        _staggered(issue)
        for a in range(self.n):
            self._own(ins, outs, sems, a).start()

    def middle(self, ins, outs, sems):
        x, y, c, me, chips = _place()
        for a in range(self.n):
            for j, chip in enumerate(chips):
                half = outs[a].shape[1] // 2
                blk = outs[a].at[2 * chip[0] + chip[1], pl.ds(c * half, half), :]
                _remote(blk, blk, sems[0].at[3 * a + j], sems[1].at[3 * a + j], (x, y, c)).wait_recv()
                self._fwd(outs, sems, a, j, chip, c).start()

    def finish(self, ins, outs, sems):
        x, y, c, me, chips = _place()
        for a in range(self.n):
            for j, chip in enumerate(chips):
                self._fwd(outs, sems, a, j, chip, 1 - c).wait_recv()
        for a in range(self.n):
            for j, chip in enumerate(chips):
                self._ici(ins, outs, sems, a, j, chip).wait_send()
                self._fwd(outs, sems, a, j, chip, c).wait_send()
            self._own(ins, outs, sems, a).wait()


def _own_shard_to_sibling(shard_ref, gathered_ref, send_sem, recv_sem):
    x, y, c, me, chips = _place()
    return _remote(shard_ref, gathered_ref.at[me], send_sem, recv_sem, (x, y, 1 - c))


class _NoExchange(_Exchange):
    n_in = n_out = 0
    out_shape = ()
    scratch = ()

    def start(self, ins, outs, sems):
        pass

    def finish(self, ins, outs, sems):
        pass


class _ForwardGathered(_Exchange):
    def __init__(self, shards, own=True, forward=True):
        self.own, self.forward = own, forward
        n = self.n = len(shards)
        self.n_in, self.n_out = 2 * n, n
        self.out_shape = [jax.ShapeDtypeStruct((N_CHIP,) + s.shape, s.dtype) for s in shards]
        dma = pltpu.SemaphoreType.DMA
        self.scratch = [dma((3 * n,)), dma((3 * n,)), dma((n,)), dma((n,))]
        self.aliases = {n + a: a for a in range(n)}

    def _fwd(self, outs, sems, a, j, chip, half_of):
        x, y, c, me, chips = _place()
        half = outs[a].shape[1] // 2
        blk = outs[a].at[2 * chip[0] + chip[1], pl.ds(half_of * half, half), :]
        return _remote(blk, blk, sems[0].at[3 * a + j], sems[1].at[3 * a + j], (x, y, 1 - c))

    def _own(self, ins, outs, sems, a):
        return _own_shard_to_sibling(ins[a], outs[a], sems[2].at[a], sems[3].at[a])

    def start(self, ins, outs, sems):
        x, y, c, me, chips = _place()
        for a in range(self.n):
            for j, chip in enumerate(chips if self.forward else ()):
                self._fwd(outs, sems, a, j, chip, c).start()
        for a in range(self.n if self.own else 0):
            self._own(ins, outs, sems, a).start()

    def finish(self, ins, outs, sems):
        x, y, c, me, chips = _place()
        for a in range(self.n):
            for j, chip in enumerate(chips if self.forward else ()):
                self._fwd(outs, sems, a, j, chip, 1 - c).wait_recv()
        for a in range(self.n):
            for j, chip in enumerate(chips if self.forward else ()):
                self._fwd(outs, sems, a, j, chip, c).wait_send()
            if self.own:
                self._own(ins, outs, sems, a).wait()


HBM = pl.BlockSpec(memory_space=pltpu.HBM)
SEMS = pl.BlockSpec(memory_space=pltpu.SEMAPHORE)
DATAFLOW = pltpu.SideEffectType.DATAFLOW_SIDE_EFFECTING


class _OverIci:
    def __init__(self, name, sources, lands):
        self.name, self.n = name, len(sources)
        hbm = lambda t: pltpu.with_memory_space_constraint(t, pltpu.HBM)
        self.arrays = [hbm(t) for t in sources] + [hbm(t) for t in lands]

    def sent(self, src, land, a, chip):
        raise NotImplementedError

    def landed(self, land, a, chip):
        raise NotImplementedError

    def _copy(self, arr, sems, a, j, receiving):
        x, y, c, me, chips = _place()
        src, dst = self.sent(arr[a], arr[self.n + a], a, chips[j])
        if receiving:
            dst = self.landed(arr[self.n + a], a, chips[j])
        return _remote(src, dst, sems[0].at[3 * a + j], sems[1].at[3 * a + j], (*chips[j], c))

    def start(self, after):
        m = len(self.arrays)

        def body(*refs):
            arr, sems, token = refs[:m], refs[m + 1:m + 3], refs[-1]

            def issue(order):
                for a in range(self.n):
                    for j in order:
                        self._copy(arr, sems, a, j, False).start()

            _staggered(issue)
            token[...] = jnp.zeros_like(token)

        dma = pltpu.SemaphoreType.DMA
        outs = pl.pallas_call(
            body, name=self.name + "_start",
            out_shape=[dma((3 * self.n,)), dma((3 * self.n,))] + [pltpu.HBM(t.shape, t.dtype) for t in self.arrays]
                      + [jax.ShapeDtypeStruct((8, 128), F32)],
            in_specs=[HBM] * m + [ANY], out_specs=[SEMS, SEMS] + [HBM] * m + [VMEM],
            input_output_aliases={i: 2 + i for i in range(m)},
            compiler_params=pltpu.CompilerParams(has_side_effects=DATAFLOW),
        )(*self.arrays, after)
        self.sems, self.arrays = outs[0:2], list(outs[2:2 + m])
        return outs[-1]

    def wait(self, after):
        m = len(self.arrays)

        def body(*refs):
            arr, sems = refs[:m], refs[m:m + 2]
            for a in range(self.n):
                for j in range(3):
                    self._copy(arr, sems, a, j, False).wait_send()
                    self._copy(arr, sems, a, j, True).wait_recv()

        outs = pl.pallas_call(
            body, name=self.name + "_wait",
            out_shape=[pltpu.HBM(t.shape, t.dtype) for t in self.arrays],
            in_specs=[HBM] * m + [SEMS, SEMS, ANY], out_specs=[HBM] * m,
            input_output_aliases={i: i for i in range(m)},
            compiler_params=pltpu.CompilerParams(has_side_effects=DATAFLOW),
        )(*self.arrays, *self.sems, after)
        return list(outs[:self.n]), list(outs[self.n:])


class _GatherOverIci(_OverIci):
    def __init__(self, name, shards):
        super().__init__(name, shards, [lax.empty((N_CHIP,) + s.shape, s.dtype) for s in shards])

    @staticmethod
    def _half(ref):
        c = lax.axis_index("c")
        half = ref.shape[-2] // 2
        return pl.ds(c * half, half)

    def sent(self, src, land, a, chip):
        return src.at[self._half(src), :], land.at[_place()[3], self._half(src), :]

    def landed(self, land, a, chip):
        return land.at[2 * chip[0] + chip[1], self._half(land), :]


class _SumOverIci(_OverIci):
    def __init__(self, name, pre):
        super().__init__(name, pre, [lax.empty(p.shape, p.dtype) for p in pre])

    def sent(self, src, land, a, chip):
        return src.at[2 * chip[0] + chip[1]], land.at[_place()[3]]

    def landed(self, land, a, chip):
        return land.at[2 * chip[0] + chip[1]]


class _HalvesToSibling(_Exchange):
    def __init__(self, grads):
        n = self.n = len(grads)
        self.n_in = self.n_out = n
        self.out_shape = [jax.ShapeDtypeStruct((N_CHIP, g.shape[1] // 2, g.shape[2]), g.dtype) for g in grads]
        self.scratch = [pltpu.SemaphoreType.DMA((n,)), pltpu.SemaphoreType.DMA((n,))]

    def _copy(self, ins, outs, sems, a):
        x, y, c, me, chips = _place()
        half = ins[a].shape[1] // 2
        return _remote(ins[a].at[:, pl.ds((1 - c) * half, half), :], outs[a], sems[0].at[a], sems[1].at[a], (x, y, 1 - c))

    def start(self, ins, outs, sems):
        for a in range(self.n):
            self._copy(ins, outs, sems, a).start()

    def finish(self, ins, outs, sems):
        for a in range(self.n):
            self._copy(ins, outs, sems, a).wait_recv()
        for a in range(self.n):
            self._copy(ins, outs, sems, a).wait_send()


class _OverChips(_Exchange):
    def __init__(self, pre):
        n = self.n = len(pre)
        self.n_in = self.n_out = n
        self.out_shape = [jax.ShapeDtypeStruct(p.shape, p.dtype) for p in pre]
        dma = pltpu.SemaphoreType.DMA
        self.scratch = [dma((3 * n,)), dma((3 * n,))]

    def _ici(self, ins, outs, sems, a, j, chip):
        x, y, c, me, chips = _place()
        return _remote(ins[a].at[2 * chip[0] + chip[1]], outs[a].at[me], sems[0].at[3 * a + j], sems[1].at[3 * a + j],
                       (*chip, c))

    def start(self, ins, outs, sems):
        chips = _place()[4]

        def issue(order):
            for a in range(self.n):
                for j in order:
                    self._ici(ins, outs, sems, a, j, chips[j]).start()

        _staggered(issue)

    def finish(self, ins, outs, sems):
        x, y, c, me, chips = _place()
        for a in range(self.n):
            for j, chip in enumerate(chips):
                blk = outs[a].at[2 * chip[0] + chip[1]]
                _remote(blk, blk, sems[0].at[3 * a + j], sems[1].at[3 * a + j], (x, y, c)).wait_recv()
        for a in range(self.n):
            for j, chip in enumerate(chips):
                self._ici(ins, outs, sems, a, j, chip).wait_send()


class _ShareHalves(_Exchange):
    def __init__(self, fulls):
        n = self.n = len(fulls)
        self.n_in = self.n_out = n
        self.out_shape = [jax.ShapeDtypeStruct(f.shape, f.dtype) for f in fulls]
        self.scratch = [pltpu.SemaphoreType.DMA((n,)), pltpu.SemaphoreType.DMA((n,))]
        self.aliases = {a: a for a in range(n)}

    def _copy(self, outs, sems, a, half_of):
        x, y, c, me, chips = _place()
        half = outs[a].shape[0] // 2
        rows = outs[a].at[pl.ds(half_of * half, half), :]
        return _remote(rows, rows, sems[0].at[a], sems[1].at[a], (x, y, 1 - c))

    def start(self, ins, outs, sems):
        c = _place()[2]
        for a in range(self.n):
            self._copy(outs, sems, a, c).start()

    def finish(self, ins, outs, sems):
        c = _place()[2]
        for a in range(self.n):
            self._copy(outs, sems, a, 1 - c).wait_recv()
        for a in range(self.n):
            self._copy(outs, sems, a, c).wait_send()


class _GatherBlocks(_Exchange):
    def __init__(self, block):
        self.n_in = self.n_out = 1
        self.out_shape = [jax.ShapeDtypeStruct((8,) + block.shape, block.dtype)]
        dma = pltpu.SemaphoreType.DMA
        self.scratch = [dma((7,)), dma((7,)), dma]

    @staticmethod
    def _peer(f):
        x, y, c, me, chips = _place()
        return ((1 - x) if f & 4 else x, (1 - y) if f & 2 else y, (1 - c) if f & 1 else c)

    def start(self, ins, outs, sems):
        x, y, c, me, chips = _place()
        for f in range(1, 8):
            _remote(ins[0], outs[0].at[2 * me + c], sems[0].at[f - 1], sems[1].at[f - 1], self._peer(f)).start()
        pltpu.make_async_copy(ins[0], outs[0].at[2 * me + c], sems[2]).start()

    def finish(self, ins, outs, sems):
        x, y, c, me, chips = _place()
        for f in range(1, 8):
            px, py, pc = self._peer(f)
            blk = outs[0].at[4 * px + 2 * py + pc]
            _remote(blk, blk, sems[0].at[f - 1], sems[1].at[f - 1], (x, y, c)).wait_recv()
        for f in range(1, 8):
            _remote(ins[0], outs[0].at[2 * me + c], sems[0].at[f - 1], sems[1].at[f - 1], self._peer(f)).wait_send()
        pltpu.make_async_copy(ins[0], outs[0].at[2 * me + c], sems[2]).wait()


class _Both(_Exchange):
    def __init__(self, first, second):
        self.parts = (first, second)
        self.n_in, self.n_out = first.n_in + second.n_in, first.n_out + second.n_out
        self.out_shape = first.out_shape + second.out_shape
        self.scratch = first.scratch + second.scratch
        self.aliases = dict(first.aliases)
        self.aliases.update({first.n_in + i: first.n_out + o for i, o in second.aliases.items()})

    def _split(self, ins, outs, sems):
        a, b = self.parts
        return ((a, ins[:a.n_in], outs[:a.n_out], sems[:len(a.scratch)]),
                (b, ins[a.n_in:], outs[a.n_out:], sems[len(a.scratch):]))

    def start(self, ins, outs, sems):
        for ex, i, o, s in self._split(ins, outs, sems):
            ex.start(i, o, s)

    def middle(self, ins, outs, sems):
        for ex, i, o, s in self._split(ins, outs, sems):
            ex.middle(i, o, s)

    def finish(self, ins, outs, sems):
        for ex, i, o, s in self._split(ins, outs, sems):
            ex.finish(i, o, s)


class _Bound:
    def __init__(self, ex, ins, outs, sems):
        self.start = lambda: ex.start(ins, outs, sems)
        self.middle = lambda: ex.middle(ins, outs, sems)
        self.finish = lambda: ex.finish(ins, outs, sems)


def _carry(name, body, ex, ex_args, args, in_specs, out_specs, out_shape, scratch_shapes=(), grid=None, semantics=(),
           after=None):
    n_a, n_o, n_s = len(args), len(out_shape), len(scratch_shapes)
    behind = [] if after is None else [after]

    def full_body(*refs):
        p = 0
        groups = []
        for size in (n_a, ex.n_in, len(behind), n_o, ex.n_out, n_s, len(ex.scratch)):
            groups.append(refs[p:p + size])
            p += size
        a, ei, _, o, eo, s, es = groups
        body(*a, *o, *s, _Bound(ex, ei, eo, es))

    kwargs = {} if grid is None else {"grid": grid}
    outs = pl.pallas_call(
        full_body, name=name,
        in_specs=list(in_specs) + [ANY] * (ex.n_in + len(behind)), out_specs=list(out_specs) + [ANY] * ex.n_out,
        out_shape=list(out_shape) + list(ex.out_shape), scratch_shapes=list(scratch_shapes) + list(ex.scratch),
        input_output_aliases={n_a + i: n_o + o for i, o in ex.aliases.items()},
        compiler_params=_params(*semantics) if semantics else pltpu.CompilerParams(vmem_limit_bytes=VMEM_LIMIT),
        **kwargs,
    )(*args, *ex_args, *behind)
    return outs[:n_o], outs[n_o:]


def _exchange_alone(name, ex, ex_args):
    def body(xc):
        xc.start()
        xc.middle()
        xc.finish()

    return _carry(name, body, ex, ex_args, (), (), (), ())[1]


def _core_index():
    return lax.axis_index("c").astype(jnp.int32).reshape(1)


def _pair_sum(gs, gots):
    n = len(gs)
    _, r, cc = gs[0].shape
    half = r // 2

    def body(c_ref, *refs):
        for a in range(n):
            refs[2 * n + a][...] = (refs[a][...].astype(F32) + refs[n + a][...].astype(F32)).astype(BF16)

    mine = pl.BlockSpec((None, half, cc), lambda k, c_ref: (k, c_ref[0], 0))
    blk = pl.BlockSpec((None, half, cc), lambda k, c_ref: (k, 0, 0))
    return pl.pallas_call(
        body, name=f"pair_sum_{r}x{cc}",
        grid_spec=pltpu.PrefetchScalarGridSpec(
            num_scalar_prefetch=1, grid=(N_CHIP,), in_specs=[mine] * n + [blk] * n, out_specs=[blk] * n),
        out_shape=[jax.ShapeDtypeStruct((N_CHIP, half, cc), BF16)] * n,
        compiler_params=_params("parallel"),
    )(_core_index(), *gs, *gots)


def _chip_sum(pre, parts):
    n = len(parts)
    _, half, cc = parts[0].shape
    tr = half // 2
    me = 2 * lax.axis_index("x") + lax.axis_index("y")
    others = [k + (k >= me).astype(jnp.int32) for k in range(3)]
    where = jnp.stack([lax.axis_index("c"), me, *others]).astype(jnp.int32)

    def body(w_ref, *refs):
        for a in range(n):
            own, p1, p2, p3 = refs[4 * a:4 * a + 4]
            refs[4 * n + a][...] = ((own[...].astype(F32) + p1[...].astype(F32)) + p2[...].astype(F32)) + p3[...].astype(F32)

    slot = lambda s: pl.BlockSpec((None, tr, cc), lambda i, w_ref: (w_ref[s], i, 0))
    operands = []
    for a in range(n):
        operands += [pre[a], parts[a], parts[a], parts[a]]
    return pl.pallas_call(
        body, name=f"chip_sum_{half}x{cc}",
        grid_spec=pltpu.PrefetchScalarGridSpec(
            num_scalar_prefetch=1, grid=(2,),
            in_specs=[slot(1), slot(2), slot(3), slot(4)] * n,
            out_specs=[pl.BlockSpec((tr, cc), lambda i, w_ref: (2 * w_ref[0] + i, 0))] * n),
        out_shape=[jax.ShapeDtypeStruct((2 * half, cc), F32)] * n,
        compiler_params=_params("parallel"),
    )(where, *operands)


def _adamw_math(w, g, m, v):
    m = ADAM_B1 * m + (1.0 - ADAM_B1) * g
    v = ADAM_B2 * v + (1.0 - ADAM_B2) * (g * g)
    m_hat = m / (1.0 - ADAM_B1 ** ADAM_STEP)
    v_hat = v / (1.0 - ADAM_B2 ** ADAM_STEP)
    delta = -ADAM_LR * (m_hat / (jnp.sqrt(v_hat) + ADAM_EPS) + ADAM_WD * w)
    return delta, m, v


def _adamw(w, g, m, v, after=None):
    r, cc = w.shape
    tr = r // 4

    def body(w_ref, g_ref, m_ref, v_ref, go_ref, d_ref, nm_ref, nv_ref, _):
        g = g_ref[...]
        go_ref[...] = g
        d_ref[...], nm_ref[...], nv_ref[...] = _adamw_math(w_ref[...], g, m_ref[...], v_ref[...])

    blk = pl.BlockSpec((tr, cc), lambda i: (i, 0))
    return _carry(f"adamw_{r}x{cc}", body, _NoExchange(), (), (w, g, m, v), [blk] * 4, [blk] * 4,
                  [jax.ShapeDtypeStruct((r, cc), F32)] * 4, grid=(4,), semantics=("parallel",), after=after)[0]


def _pack8(rows):
    def body(*refs):
        out_ref = refs[-1]
        out_ref[...] = jnp.zeros_like(out_ref)
        for i, r in enumerate(refs[:-1]):
            out_ref[i:i + 1, :] = r[...]

    return pl.pallas_call(body, name="pack8", out_shape=jax.ShapeDtypeStruct((8, D), F32))(*rows)


def _adamw_gains(gall, w8, m8, v8):
    def body(ga_ref, w_ref, m_ref, v_ref, g_ref, d_ref, nm_ref, nv_ref):
        g = ga_ref[0]
        for dev in range(1, 8):
            g = g + ga_ref[dev]
        g_ref[...] = g
        d_ref[...], nm_ref[...], nv_ref[...] = _adamw_math(w_ref[...], g, m_ref[...], v_ref[...])

    return pl.pallas_call(
        body, name="adamw_gains",
        out_shape=[jax.ShapeDtypeStruct((8, D), F32)] * 4,
    )(gall, w8, m8, v8)


def kernel(x, positions, w_in, w_out, g_pre_mix, g_post_mix, g_pre_ffn, g_post_ffn, w_gate, w_up, w_down, loss_target, m_w_in, m_w_out, m_g_pre_mix, m_g_post_mix, m_g_pre_ffn, m_g_post_ffn, m_w_gate, m_w_up, m_w_down, v_w_in, v_w_out, v_g_pre_mix, v_g_post_mix, v_g_pre_ffn, v_g_post_ffn, v_w_gate, v_w_up, v_w_down):
    tr = lambda t: jnp.swapaxes(t, 1, 2)[0]
    shards = [w_in[0], w_out[0], tr(w_gate), tr(w_up), w_down[0]]
    moms = [m_w_in[0], m_w_out[0], tr(m_w_gate), tr(m_w_up), m_w_down[0]]
    vels = [v_w_in[0], v_w_out[0], tr(v_w_gate), tr(v_w_up), v_w_down[0]]
    xs, pos, tgt = x[0], positions.reshape(S, 1), loss_target[0]
    g1, g2, g3, g4 = g_pre_mix, g_post_mix, g_pre_ffn, g_post_ffn
    tabs = tuple(jnp.asarray(t) for t in _retention_tables())
    ifc, spread = _rotary_tables()
    ifc, spread = jnp.asarray(ifc), jnp.asarray(spread, dtype=BF16)
    bf = [s.astype(BF16) for s in shards]

    win_g, wout_g = _exchange_alone("gather_in", _GatherShards(bf[:2]), bf[:2])
    wout_g = wout_g.reshape(D, D)
    ffn_gather = _GatherOverIci("ffn_gather", bf[2:])
    token = ffn_gather.start(win_g)
    proj, h1 = _proj_fwd(xs, g1, win_g, token)
    qr, kr, rv, aq, ak, av, cos, sin = _rot_fwd(proj, pos, ifc, spread)
    o_raw, cat_r, states = _ret_fwd(qr, kr, rv, proj, tabs)
    n_ffn = len(bf[2:])
    (att_out, lse, cat_a), ffn_gather.arrays[n_ffn:] = _att_fwd(
        aq, ak, av, _ForwardGathered(bf[2:], forward=False), ffn_gather.arrays)
    ffn_sh, ffn_lands = ffn_gather.wait(cat_a)
    (mix, x2, h3), (wg_g, wu_g, wd_g) = _mix_fwd(cat_r, cat_a, wout_g, xs, g2, g3,
                                                _ForwardGathered(bf[2:], own=False), [*ffn_sh, *ffn_lands])
    gt, up, a, f = _ffn_fwd(h3, wg_g, wu_g, wd_g)

    sq, dy, df, dg4 = _head_bwd(f, x2, tgt, g4)
    loss = 0.5 * lax.psum(sq[0, 0], ("x", "y", "c")) / D
    dgt, dup, dh3 = _ffn_bwd_act(df, gt, up, wg_g, wu_g, wd_g)
    ffn_grads = list(_ffn_bwd_w(a, df, h3, dgt, dup))
    (dx2, dmix, dg3, dg2), got = _norm_bwd(dh3, dy, x2, mix, g2, g3, _HalvesToSibling(ffn_grads), ffn_grads)
    ffn_sum = _SumOverIci("ffn_sum", _pair_sum(ffn_grads, got))
    token = ffn_sum.start(dmix)
    dret, datt, dwout = _mix_bwd(dmix, cat_r, cat_a, wout_g, token)
    (dq_att, dk_att, dv_att), _ = _att_bwd(aq, ak, av, datt, att_out, lse, _NoExchange(), ())
    (dqr, dkr, drv, drg), _ = _ret_bwd(qr, kr, rv, proj, o_raw, states, dret, tabs, _NoExchange(), ())
    sums = _chip_sum(*ffn_sum.wait(dqr))
    dproj = _rot_bwd(cos, sin, dqr, dkr, drv, drg, dq_att, dk_att, dv_att)
    dwin, ffn_full = _win_bwd_w(h1, dproj, _ShareHalves(sums), sums)
    in_grads = [dwin, dwout.reshape(N_CHIP, WOUT_R, D)]
    (dx, dg1), got = _in_bwd(dproj, win_g, xs, dx2, g1, _HalvesToSibling(in_grads), in_grads)

    in_sum = _SumOverIci("in_sum", [*_pair_sum(in_grads[:1], got[:1]), *_pair_sum(in_grads[1:], got[1:])])
    token = in_sum.start(dx)
    gblock = _pack8([dg1, dg2, dg3, dg4])
    (gall,) = _exchange_alone("gather_gains", _GatherBlocks(gblock), [gblock])
    ffn_upd = [_adamw(shards[2 + i], ffn_full[o], moms[2 + i], vels[2 + i], token)
               for i, o in enumerate((1, 2, 0))]
    pre, parts = in_sum.wait(ffn_upd[2][0])
    sums = [*_chip_sum(pre[:1], parts[:1]), *_chip_sum(pre[1:], parts[1:])]
    in_full = _exchange_alone("share_rest", _ShareHalves(sums), sums)
    upd = [_adamw(w, g, m, v) for w, g, m, v in zip(shards[:2], in_full, moms[:2], vels[:2])] + ffn_upd
    gg, gd, gm, gv = _adamw_gains(gall, _pack8([g1, g2, g3, g4]),
                                  _pack8([m_g_pre_mix, m_g_post_mix, m_g_pre_ffn, m_g_post_ffn]),
                                  _pack8([v_g_pre_mix, v_g_post_mix, v_g_pre_ffn, v_g_post_ffn]))

    def order(mats, vecs):
        back = lambda t: jnp.swapaxes(t[None], 1, 2)
        return ([mats[0][None], mats[1][None]] + [vecs[i:i + 1] for i in range(4)]
                + [back(mats[2]), back(mats[3]), mats[4][None]])

    return (loss, dx[None],
            *order([u[0] for u in upd], gg),
            *order([u[1] for u in upd], gd),
            *order([u[2] for u in upd], gm),
            *order([u[3] for u in upd], gv))
```

```python
import functools

import numpy as np
import jax
import jax.numpy as jnp
from jax import lax
from jax.experimental import pallas as pl
from jax.experimental.pallas import tpu as pltpu

F32, BF16 = jnp.float32, jnp.bfloat16
MESH = pl.DeviceIdType.MESH

S = 2048
D = 1024
PW = 3072
N_CHIP = 4
WIN_C = PW // N_CHIP
DFF = 2816
FF_C = DFF // N_CHIP
WOUT_R = D // N_CHIP
RMS_EPS = 1e-6
GN_EPS = 1e-5
RET_C = 128
RET_SCALE = 32 ** -0.5
ATT_BLK = 128
ATT_SCALE = 64 ** -0.5
PATTERN_DILATIONS = (1, 4, 16)
NEG = -1e30
VMEM_LIMIT = 56 * 1024 * 1024

ADAM_LR, ADAM_B1, ADAM_B2, ADAM_EPS, ADAM_WD, ADAM_STEP = 0.001, 0.9, 0.999, 1e-08, 0.01, 10


def _params(*sem):
    return pltpu.CompilerParams(dimension_semantics=sem, vmem_limit_bytes=VMEM_LIMIT)


def _nt(a, b):
    return lax.dot_general(a, b, (((1,), (1,)), ((), ())), preferred_element_type=F32)


def _tn(a, b):
    return lax.dot_general(a, b, (((0,), (0,)), ((), ())), preferred_element_type=F32)


def _nn(a, b):
    return jnp.dot(a, b, preferred_element_type=F32)


def _rstd(v):
    return lax.rsqrt(jnp.mean(v * v, axis=-1, keepdims=True) + RMS_EPS)


def _sigmoid(v):
    return 1.0 / (1.0 + jnp.exp(-v))


def _rows(i, t):
    return pl.ds(pl.multiple_of(i * t, t), t)


def _retention_tables():
    h = np.arange(8, dtype=np.float32)
    log_g = np.log1p(-np.exp2(-5.0 - h)).astype(np.float32)
    idx = np.arange(RET_C, dtype=np.float32)
    diff = idx[:, None] - idx[None, :]
    dtab = np.where(diff >= 0, np.exp(log_g[:, None, None] * np.maximum(diff, 0.0)), 0.0).astype(np.float32)
    dtab = dtab.reshape(8 * RET_C, RET_C)
    lane_head = np.arange(256) // 32
    a_tab = np.exp(log_g[lane_head][None, :] * (idx + 1.0)[:, None]).astype(np.float32)
    b_tab = np.exp(log_g[lane_head][None, :] * (RET_C - 1.0 - idx)[:, None]).astype(np.float32)
    lam = np.exp(log_g[lane_head] * RET_C).astype(np.float32)[:, None]
    bd = (lane_head[:, None] == (np.arange(512) // 64)[None, :]).astype(np.float32)
    return dtab, a_tab, b_tab, lam, bd


def _rotary_tables():
    inv_r = (1.0 / (np.float32(10000.0) ** np.linspace(0.0, 1.0, 16, dtype=np.float32))).astype(np.float32)
    inv_a = (np.float32(500000.0) ** (-np.arange(0, 16, 2, dtype=np.float32) / np.float32(16))).astype(np.float32)
    ifc = np.zeros((1, 128), np.float32)
    ifc[0, 0:16], ifc[0, 16:24] = inv_r, inv_a
    spread = np.zeros((128, 768), np.float32)
    for lane in range(256):
        spread[(lane % 32) % 16, lane] = 1.0
    for lane in range(512):
        d = lane % 64
        spread[16 + d % 8 if d < 16 else 24, 256 + lane] = 1.0
    return ifc, spread


def _proj_fwd(x, g1, win_g, after):
    tm = 512

    def body(x_ref, g_ref, w_ref, proj_ref, h_ref, _):
        xv = x_ref[...]
        h = (xv * _rstd(xv) * g_ref[...]).astype(BF16)
        h_ref[...] = h
        for k in range(N_CHIP):
            proj_ref[:, k * WIN_C:(k + 1) * WIN_C] = _nn(h, w_ref[k])

    return _carry(
        "proj_fwd", body, _NoExchange(), (), (x, g1, win_g),
        [pl.BlockSpec((tm, D), lambda i: (i, 0)), pl.BlockSpec((1, D), lambda i: (0, 0)),
         pl.BlockSpec((N_CHIP, D, WIN_C), lambda i: (0, 0, 0))],
        [pl.BlockSpec((tm, PW), lambda i: (i, 0)), pl.BlockSpec((tm, D), lambda i: (i, 0))],
        [jax.ShapeDtypeStruct((S, PW), F32), jax.ShapeDtypeStruct((S, D), BF16)],
        grid=(S // tm,), semantics=("parallel",), after=after)[0]


def _rot_halves(tm):
    lo_r = (lax.broadcasted_iota(jnp.int32, (tm, 256), 1) % 32) < 16
    lo_a = (lax.broadcasted_iota(jnp.int32, (tm, 512), 1) % 64) < 8
    return lo_r, lo_a


def _spread_exact(t, e):
    hi = t.astype(BF16)
    r1 = t - hi.astype(F32)
    mid = r1.astype(BF16)
    lo = (r1 - mid.astype(F32)).astype(BF16)
    return _nn(hi, e) + _nn(mid, e) + _nn(lo, e)


def _rot_fwd(proj, pos, ifc, spread):
    tm = 256

    def body(p_ref, pos_ref, ifc_ref, e_ref, qr_ref, kr_ref, rv_ref, aq_ref, ak_ref, av_ref, cos_ref, sin_ref):
        ang = pos_ref[...].astype(F32) * ifc_ref[...]
        cs = _spread_exact(jnp.cos(ang), e_ref[...])
        sn = _spread_exact(jnp.sin(ang), e_ref[...])
        cos_ref[...] = cs
        sin_ref[...] = sn
        cr, ca, sr, sa = cs[:, 0:256], cs[:, 256:768], sn[:, 0:256], sn[:, 256:768]
        lo_r, lo_a = _rot_halves(tm)

        def rot_r(v):
            return v * cr + sr * jnp.where(lo_r, -pltpu.roll(v, 240, 1), pltpu.roll(v, 16, 1))

        def rot_a(v):
            return v * ca + sa * jnp.where(lo_a, -pltpu.roll(v, 504, 1), pltpu.roll(v, 8, 1))

        qr_ref[...] = rot_r(p_ref[:, 0:256]).astype(BF16)
        kr_ref[...] = (rot_r(p_ref[:, 256:512]) * RET_SCALE).astype(BF16)
        rv_ref[...] = p_ref[:, 512:1024].astype(BF16)
        aq, ak = rot_a(p_ref[:, 1536:2048]), rot_a(p_ref[:, 2048:2560])
        for j in range(4):
            aq_ref[j] = aq[:, 128 * j:128 * j + 128]
            ak_ref[j] = ak[:, 128 * j:128 * j + 128]
            av_ref[j] = p_ref[:, 2560 + 128 * j:2560 + 128 * j + 128]

    row = lambda w: pl.BlockSpec((tm, w), lambda i: (i, 0))
    const = lambda w: pl.BlockSpec((1, w), lambda i: (0, 0))
    slab = pl.BlockSpec((4, tm, 128), lambda i: (0, i, 0))
    return pl.pallas_call(
        body, grid=(S // tm,), name="rot_fwd",
        in_specs=[row(PW), row(1), const(128), pl.BlockSpec((128, 768), lambda i: (0, 0))],
        out_specs=[row(256), row(256), row(512), slab, slab, slab, row(768), row(768)],
        out_shape=[jax.ShapeDtypeStruct((S, w), BF16) for w in (256, 256, 512)]
                  + [jax.ShapeDtypeStruct((4, S, 128), F32)] * 3 + [jax.ShapeDtypeStruct((S, 768), F32)] * 2,
        compiler_params=_params("parallel"),
    )(proj, pos, ifc, spread)


def _seg_mean(v):
    lo = lax.broadcasted_iota(jnp.int32, v.shape, 1) < 64
    s_lo = jnp.sum(jnp.where(lo, v, 0.0), axis=-1, keepdims=True)
    s_hi = jnp.sum(jnp.where(lo, 0.0, v), axis=-1, keepdims=True)
    return jnp.where(lo, s_lo, s_hi) * (1.0 / 64.0)


def _ret_fwd(qr, kr, rv, proj, tabs, exchange, exchange_args):
    C = RET_C
    dtab, a_tab, b_tab, lam, bd = tabs

    def body(q_ref, k_ref, v_ref, g_ref, dt_ref, a_ref, b_ref, lam_ref, bd_ref, o_ref, cat_ref, st_ref, R, exch):
        @pl.when(pl.program_id(0) == 0)
        def _():
            exch.start()
            R[...] = jnp.zeros_like(R)

        @pl.when(pl.program_id(0) == S // C // 2)
        def _():
            exch.middle()

        q, k, v = q_ref[...], k_ref[...], v_ref[...]
        lane_head = lax.broadcasted_iota(jnp.int32, (C, 256), 1) // 32
        col_head = lax.broadcasted_iota(jnp.int32, (C, 256), 1) // 64
        rb = R[...].astype(BF16)
        st_ref[...] = rb
        qa = (q.astype(F32) * a_ref[...]).astype(BF16)
        cross = _nn(qa, rb)
        p = (_nt(_stack_heads(q, lane_head, n=8), k) * dt_ref[...]).astype(BF16)
        og = [cross[:, 256 * g:256 * g + 256]
              + _unstack_heads(_nn(p[4 * C * g:4 * C * (g + 1)], v[:, 256 * g:256 * g + 256]), col_head)
              for g in range(2)]
        kb = (k.astype(F32) * b_ref[...]).astype(BF16)
        R[...] = R[...] * lam_ref[...] + _tn(kb, v) * bd_ref[...]
        o_ref[:, 0:256] = og[0]
        o_ref[:, 256:512] = og[1]
        for j in range(4):
            oj = og[j // 2][:, 128 * (j % 2):128 * (j % 2) + 128]
            xc = oj - _seg_mean(oj)
            rn = xc * lax.rsqrt(_seg_mean(xc * xc) + GN_EPS)
            gj = g_ref[:, 128 * j:128 * j + 128]
            cat_ref[:, 128 * j:128 * j + 128] = (rn * (gj * _sigmoid(gj))).astype(BF16)

        @pl.when(pl.program_id(0) == S // C - 1)
        def _():
            exch.finish()

    row = lambda w: pl.BlockSpec((C, w), lambda n: (n, 0))
    full = lambda a: pl.BlockSpec(a.shape, lambda n: (0,) * a.ndim)
    return _carry(
        "ret_fwd", body, exchange, exchange_args, (qr, kr, rv, proj, dtab, a_tab, b_tab, lam, bd),
        [row(256), row(256), row(512), pl.BlockSpec((C, 512), lambda n: (n, 2)),
         full(dtab), full(a_tab), full(b_tab), full(lam), full(bd)],
        [row(512), row(512), pl.BlockSpec((None, 256, 512), lambda n: (n, 0, 0))],
        [jax.ShapeDtypeStruct((S, 512), F32), jax.ShapeDtypeStruct((S, 512), BF16),
         jax.ShapeDtypeStruct((S // C, 256, 512), BF16)],
        scratch_shapes=[pltpu.VMEM((256, 512), F32)], grid=(S // C,), semantics=("arbitrary",))


def _stack_heads(v, lane_head, fill=0.0, n=4):
    return jnp.concatenate([jnp.where(lane_head == h, v, jnp.full_like(v, fill)) for h in range(n)], axis=0)


def _unstack_heads(v, lane_head, n=4):
    out = v[0:ATT_BLK]
    for h in range(1, n):
        out = jnp.where(lane_head == h, v[h * ATT_BLK:(h + 1) * ATT_BLK], out)
    return out


def _att_mask(ib, has_prev):
    nk = 2 * ATT_BLK if has_prev else ATT_BLK
    a = lax.broadcasted_iota(jnp.int32, (4 * ATT_BLK, nk), 0) % ATT_BLK
    kk = lax.broadcasted_iota(jnp.int32, (4 * ATT_BLK, nk), 1)
    if has_prev:
        dist = ATT_BLK + a - kk
        return (dist >= 0) & (dist <= ATT_BLK) & ((ib * ATT_BLK - ATT_BLK + kk) >= 0)
    return (a - kk) >= 0


def _class_rows(ib, r, d):
    if d == 1:
        return pl.ds(pl.multiple_of(ib * ATT_BLK, ATT_BLK), ATT_BLK)
    return pl.ds(ib * ATT_BLK * d + r, ATT_BLK, stride=d)


def _slab_pair(ref, g, rows):
    return jnp.concatenate([ref[2 * g, rows, :], ref[2 * g + 1, rows, :]], axis=1)


def _att_blocks(d):
    nb = S // d // ATT_BLK
    return nb, nb > 1


def _att_fwd(aq, ak, av, exchange, exchange_args):
    def body(q_ref, k_ref, v_ref, o_ref, l_ref, cat_ref, xc):
        xc.start()
        lane_head = lax.broadcasted_iota(jnp.int32, (ATT_BLK, 256), 1) // 64
        for pi, d in enumerate(PATTERN_DILATIONS):
            if pi == len(PATTERN_DILATIONS) - 1:
                xc.middle()
            nb, has_prev = _att_blocks(d)

            def block(b, carry, pi=pi, d=d, nb=nb, has_prev=has_prev):
                r, ib = b // nb, b % nb
                rows = _class_rows(ib, r, d)
                prow = _class_rows(jnp.maximum(ib - 1, 0), r, d)
                valid = _att_mask(ib, has_prev)
                for g in range(2):
                    qg = _slab_pair(q_ref, g, rows).astype(BF16)
                    kg = _slab_pair(k_ref, g, rows)
                    vg = _slab_pair(v_ref, g, rows)
                    if has_prev:
                        kg = jnp.concatenate([_slab_pair(k_ref, g, prow), kg], axis=0)
                        vg = jnp.concatenate([_slab_pair(v_ref, g, prow), vg], axis=0)
                    kg, vg = kg.astype(BF16), vg.astype(BF16)
                    s = jnp.where(valid, _nt(_stack_heads(qg, lane_head), kg) * ATT_SCALE, NEG)
                    m = jnp.max(s, axis=-1, keepdims=True)
                    p = jnp.exp(s - m)
                    den = jnp.sum(p, axis=-1, keepdims=True)
                    og = _unstack_heads(_nn(p.astype(BF16), vg) / den, lane_head)
                    lg = _unstack_heads(jnp.broadcast_to(m + jnp.log(den), (4 * ATT_BLK, 256)), lane_head)
                    for jj in range(2):
                        j = 2 * g + jj
                        o_new, l_new = og[:, 128 * jj:128 * jj + 128], lg[:, 128 * jj:128 * jj + 128]
                        if pi > 0:
                            o_old, l_old = o_ref[j, rows, :], l_ref[j, rows, :]
                            mx = jnp.maximum(l_old, l_new)
                            ea, eb = jnp.exp(l_old - mx), jnp.exp(l_new - mx)
                            den = ea + eb
                            o_new = (ea * o_old + eb * o_new) / den
                            l_new = mx + jnp.log(den)
                        o_ref[j, rows, :] = o_new
                        l_ref[j, rows, :] = l_new
                return carry

            lax.fori_loop(0, S // ATT_BLK, block, 0)

        def to_cat(i, carry):
            rows = _rows(i, 256)
            for j in range(4):
                cat_ref[rows, 128 * j:128 * j + 128] = o_ref[j, rows, :].astype(BF16)
            return carry

        lax.fori_loop(0, S // 256, to_cat, 0)
        xc.finish()

    slab = jax.ShapeDtypeStruct((4, S, 128), F32)
    return _carry("att_fwd", body, exchange, exchange_args, (aq, ak, av), [VMEM] * 3, [VMEM] * 3,
                  [slab, slab, jax.ShapeDtypeStruct((S, 512), BF16)])


def _mix_fwd(cat_r, cat_a, wout, x, g2, g3, exchange, exchange_args):
    tm = 512

    def body(cr_ref, ca_ref, w_ref, x_ref, g2_ref, g3_ref, mix_ref, x2_ref, h3_ref, xc):
        @pl.when(pl.program_id(0) == 0)
        def _():
            xc.start()

        mix = _nn(cr_ref[...], w_ref[0:512, :]) + _nn(ca_ref[...], w_ref[512:1024, :])
        mix_ref[...] = mix
        x2 = x_ref[...] + mix * _rstd(mix) * g2_ref[...]
        x2_ref[...] = x2
        h3_ref[...] = (x2 * _rstd(x2) * g3_ref[...]).astype(BF16)

        @pl.when(pl.program_id(0) == S // tm - 1)
        def _():
            xc.middle()
            xc.finish()

    row = lambda w: pl.BlockSpec((tm, w), lambda i: (i, 0))
    vec = pl.BlockSpec((1, D), lambda i: (0, 0))
    return _carry("mix_fwd", body, exchange, exchange_args, (cat_r, cat_a, wout, x, g2, g3),
                  [row(512), row(512), pl.BlockSpec((D, D), lambda i: (0, 0)), row(D), vec, vec],
                  [row(D), row(D), row(D)],
                  [jax.ShapeDtypeStruct((S, D), F32), jax.ShapeDtypeStruct((S, D), F32),
                   jax.ShapeDtypeStruct((S, D), BF16)],
                  grid=(S // tm,), semantics=("arbitrary",))


def _ffn_fwd(h3, wg, wu, wd):
    tm = 512

    def body(h_ref, wg_ref, wu_ref, wd_ref, gt_ref, up_ref, a_ref, f_ref):
        k, i = pl.program_id(0), pl.program_id(1)
        h = h_ref[...]
        gt = _nt(h, wg_ref[...])
        up = _nt(h, wu_ref[...])
        gt_ref[...] = gt.astype(BF16)
        up_ref[...] = up.astype(BF16)
        a = (gt * _sigmoid(gt) * up).astype(BF16)
        a_ref[...] = a
        part = _nn(a, wd_ref[...])
        rows = _rows(i, tm)

        @pl.when(k == 0)
        def _():
            f_ref[rows, :] = part

        @pl.when(k > 0)
        def _():
            f_ref[rows, :] = f_ref[rows, :] + part

    wrow = pl.BlockSpec((None, FF_C, D), lambda k, i: (k, 0, 0))
    act = pl.BlockSpec((None, tm, FF_C), lambda k, i: (k, i, 0))
    return pl.pallas_call(
        body, grid=(N_CHIP, S // tm), name="ffn_fwd",
        in_specs=[pl.BlockSpec((tm, D), lambda k, i: (i, 0)), wrow, wrow, wrow],
        out_specs=[act, act, act, pl.BlockSpec((S, D), lambda k, i: (0, 0))],
        out_shape=[jax.ShapeDtypeStruct((N_CHIP, S, FF_C), BF16)] * 3 + [jax.ShapeDtypeStruct((S, D), F32)],
        compiler_params=_params("arbitrary", "arbitrary"),
    )(h3, wg, wu, wd)


def _head_bwd(f, x2, tgt, g4):
    tm = 256

    def body(f_ref, x2_ref, t_ref, g_ref, loss_ref, dy_ref, df_ref, dg_ref):
        @pl.when(pl.program_id(0) == 0)
        def _():
            loss_ref[...] = jnp.zeros_like(loss_ref)
            dg_ref[...] = jnp.zeros_like(dg_ref)

        fv = f_ref[...]
        r = _rstd(fv)
        fn = fv * r
        e = x2_ref[...] + fn * g_ref[...] - t_ref[...]
        sq = jnp.sum(jnp.sum(e * e, axis=-1, keepdims=True), axis=0, keepdims=True)
        loss_ref[...] = loss_ref[...] + sq
        dy = e * (1.0 / D)
        dy_ref[...] = dy
        dg_ref[...] = dg_ref[...] + jnp.sum(dy * fn, axis=0, keepdims=True)
        t = dy * g_ref[...]
        df_ref[...] = (r * (t - fn * jnp.mean(t * fn, axis=-1, keepdims=True))).astype(BF16)

    row = pl.BlockSpec((tm, D), lambda i: (i, 0))
    vec = pl.BlockSpec((1, D), lambda i: (0, 0))
    return pl.pallas_call(
        body, grid=(S // tm,), name="head_bwd",
        in_specs=[row, row, row, vec],
        out_specs=[pl.BlockSpec((8, 128), lambda i: (0, 0)), row, row, vec],
        out_shape=[jax.ShapeDtypeStruct((8, 128), F32), jax.ShapeDtypeStruct((S, D), F32),
                   jax.ShapeDtypeStruct((S, D), BF16), jax.ShapeDtypeStruct((1, D), F32)],
        compiler_params=_params("arbitrary"),
    )(f, x2, tgt, g4)


def _ffn_bwd_act(df, gt, up, wg, wu, wd):
    tm = 512

    def body(df_ref, gt_ref, up_ref, wg_ref, wu_ref, wd_ref, dgt_ref, dup_ref, dh_ref):
        k = pl.program_id(1)
        da = _nt(df_ref[...], wd_ref[...])
        gt, up = gt_ref[...].astype(F32), up_ref[...].astype(F32)
        sg = _sigmoid(gt)
        dup = (da * gt * sg).astype(BF16)
        dgt = (da * up * (sg * (1.0 + gt * (1.0 - sg)))).astype(BF16)
        dup_ref[...] = dup
        dgt_ref[...] = dgt
        part = _nn(dgt, wg_ref[...]) + _nn(dup, wu_ref[...])

        @pl.when(k == 0)
        def _():
            dh_ref[...] = part

        @pl.when(k > 0)
        def _():
            dh_ref[...] = dh_ref[...] + part

    wrow = pl.BlockSpec((None, FF_C, D), lambda i, k: (k, 0, 0))
    act = pl.BlockSpec((None, tm, FF_C), lambda i, k: (k, i, 0))
    row = pl.BlockSpec((tm, D), lambda i, k: (i, 0))
    return pl.pallas_call(
        body, grid=(S // tm, N_CHIP), name="ffn_bwd_act",
        in_specs=[row, act, act, wrow, wrow, wrow],
        out_specs=[act, act, row],
        out_shape=[jax.ShapeDtypeStruct((N_CHIP, S, FF_C), BF16), jax.ShapeDtypeStruct((N_CHIP, S, FF_C), BF16),
                   jax.ShapeDtypeStruct((S, D), F32)],
        compiler_params=_params("parallel", "arbitrary"),
    )(df, gt, up, wg, wu, wd)


def _ffn_bwd_w(a, df, h3, dgt, dup):
    tm = 1024
    assert S // tm == 2

    def body(a_ref, df_ref, h_ref, dgt_ref, dup_ref, dwd_ref, dwg_ref, dwu_ref, acc_d, acc_g, acc_u):
        i = pl.program_id(1)
        h = h_ref[...]
        parts = (_tn(a_ref[...], df_ref[...]), _tn(dgt_ref[...], h), _tn(dup_ref[...], h))

        @pl.when(i == 0)
        def _():
            for acc, part in zip((acc_d, acc_g, acc_u), parts):
                acc[...] = part

        @pl.when(i == S // tm - 1)
        def _():
            for out, acc, part in zip((dwd_ref, dwg_ref, dwu_ref), (acc_d, acc_g, acc_u), parts):
                out[...] = (acc[...] + part).astype(BF16)

    act = pl.BlockSpec((None, tm, FF_C), lambda k, i: (k, i, 0))
    row = pl.BlockSpec((tm, D), lambda k, i: (i, 0))
    wrow = pl.BlockSpec((None, FF_C, D), lambda k, i: (k, 0, 0))
    return pl.pallas_call(
        body, grid=(N_CHIP, S // tm), name="ffn_bwd_w",
        in_specs=[act, row, row, act, act],
        out_specs=[wrow, wrow, wrow],
        out_shape=[jax.ShapeDtypeStruct((N_CHIP, FF_C, D), BF16)] * 3,
        scratch_shapes=[pltpu.VMEM((FF_C, D), F32)] * 3,
        compiler_params=_params("parallel", "arbitrary"),
    )(a, df, h3, dgt, dup)


def _norm_bwd(dh3, dy, x2, mix, g2, g3, exchange, exchange_args):
    tm = 256

    def body(dh_ref, dy_ref, x2_ref, mix_ref, g2_ref, g3_ref, dx2_ref, dmix_ref, dg3_ref, dg2_ref, xc):
        @pl.when(pl.program_id(0) == 0)
        def _():
            xc.start()
            dg3_ref[...] = jnp.zeros_like(dg3_ref)
            dg2_ref[...] = jnp.zeros_like(dg2_ref)

        x2 = x2_ref[...]
        r3 = _rstd(x2)
        xn = x2 * r3
        dh = dh_ref[...]
        dg3_ref[...] = dg3_ref[...] + jnp.sum(dh * xn, axis=0, keepdims=True)
        t = dh * g3_ref[...]
        dx2 = dy_ref[...] + r3 * (t - xn * jnp.mean(t * xn, axis=-1, keepdims=True))
        dx2_ref[...] = dx2
        mix = mix_ref[...]
        r2 = _rstd(mix)
        mn = mix * r2
        dg2_ref[...] = dg2_ref[...] + jnp.sum(dx2 * mn, axis=0, keepdims=True)
        u = dx2 * g2_ref[...]
        dmix_ref[...] = (r2 * (u - mn * jnp.mean(u * mn, axis=-1, keepdims=True))).astype(BF16)

        @pl.when(pl.program_id(0) == S // tm - 1)
        def _():
            xc.middle()
            xc.finish()

    row = pl.BlockSpec((tm, D), lambda i: (i, 0))
    vec = pl.BlockSpec((1, D), lambda i: (0, 0))
    return _carry("norm_bwd", body, exchange, exchange_args, (dh3, dy, x2, mix, g2, g3),
                  [row, row, row, row, vec, vec], [row, row, vec, vec],
                  [jax.ShapeDtypeStruct((S, D), F32), jax.ShapeDtypeStruct((S, D), BF16),
                   jax.ShapeDtypeStruct((1, D), F32), jax.ShapeDtypeStruct((1, D), F32)],
                  grid=(S // tm,), semantics=("arbitrary",))


def _mix_bwd(dmix, cat_r, cat_a, wout, after):
    tm = 512

    def body(dm_ref, cr_ref, ca_ref, w_ref, dret_ref, datt_ref, dw_ref, acc, _):
        i = pl.program_id(0)

        @pl.when(i == 0)
        def _():
            acc[...] = jnp.zeros_like(acc)

        dm = dm_ref[...]
        dret_ref[...] = _nt(dm, w_ref[0:512, :])
        datt = _nt(dm, w_ref[512:1024, :])
        for j in range(4):
            datt_ref[j] = datt[:, 128 * j:128 * j + 128]
        acc[0:512, :] += _tn(cr_ref[...], dm)
        acc[512:1024, :] += _tn(ca_ref[...], dm)

        @pl.when(i == S // tm - 1)
        def _():
            dw_ref[...] = acc[...].astype(BF16)

    row = lambda w: pl.BlockSpec((tm, w), lambda i: (i, 0))
    full = pl.BlockSpec((D, D), lambda i: (0, 0))
    return _carry("mix_bwd", body, _NoExchange(), (), (dmix, cat_r, cat_a, wout),
                  [row(D), row(512), row(512), full],
                  [row(512), pl.BlockSpec((4, tm, 128), lambda i: (0, i, 0)), full],
                  [jax.ShapeDtypeStruct((S, 512), F32), jax.ShapeDtypeStruct((4, S, 128), F32),
                   jax.ShapeDtypeStruct((D, D), BF16)],
                  scratch_shapes=[pltpu.VMEM((D, D), F32)], grid=(S // tm,), semantics=("arbitrary",), after=after)[0]


def _att_bwd(aq, ak, av, datt, att_out, lse, exchange, exchange_args):
    def body(q_ref, k_ref, v_ref, do_ref, out_ref, l_ref, dq_ref, dk_ref, dv_ref, xc):
        xc.start()

        def clear(i, carry):
            rows = _rows(i, 256)
            for ref in (dq_ref, dk_ref, dv_ref):
                for j in range(4):
                    ref[j, rows, :] = jnp.zeros((256, 128), F32)
            return carry

        lax.fori_loop(0, S // 256, clear, 0)
        lane_head = lax.broadcasted_iota(jnp.int32, (ATT_BLK, 256), 1) // 64
        for d in PATTERN_DILATIONS:
            nb, has_prev = _att_blocks(d)

            def block(b, carry, d=d, nb=nb, has_prev=has_prev):
                r, ib = b // nb, b % nb
                rows = _class_rows(ib, r, d)
                prow = _class_rows(jnp.maximum(ib - 1, 0), r, d)
                valid = _att_mask(ib, has_prev)
                for g in range(2):
                    qg = _slab_pair(q_ref, g, rows).astype(BF16)
                    kg = _slab_pair(k_ref, g, rows)
                    vg = _slab_pair(v_ref, g, rows)
                    if has_prev:
                        kg = jnp.concatenate([_slab_pair(k_ref, g, prow), kg], axis=0)
                        vg = jnp.concatenate([_slab_pair(v_ref, g, prow), vg], axis=0)
                    kg, vg = kg.astype(BF16), vg.astype(BF16)
                    dog = _slab_pair(do_ref, g, rows)
                    outg = _slab_pair(out_ref, g, rows)
                    lg = _slab_pair(l_ref, g, rows)
                    qs = _stack_heads(qg, lane_head)
                    dos = _stack_heads(dog, lane_head)
                    delta = jnp.sum(dos * jnp.concatenate([outg] * 4, axis=0), axis=-1, keepdims=True)
                    lh = jnp.max(_stack_heads(lg, lane_head, NEG), axis=-1, keepdims=True)
                    s = jnp.where(valid, _nt(qs, kg) * ATT_SCALE, NEG)
                    p = jnp.exp(s - lh)
                    dosb = dos.astype(BF16)
                    ds = (p * (_nt(dosb, vg) - delta) * ATT_SCALE).astype(BF16)
                    dq = _unstack_heads(_nn(ds, kg), lane_head)
                    dk = _tn(ds, qs)
                    dv = _tn(p.astype(BF16), dosb)
                    for jj in range(2):
                        j, sl = 2 * g + jj, slice(128 * jj, 128 * jj + 128)
                        dq_ref[j, rows, :] += dq[:, sl]
                        if has_prev:
                            dk_ref[j, prow, :] += dk[0:ATT_BLK, sl]
                            dv_ref[j, prow, :] += dv[0:ATT_BLK, sl]
                            dk_ref[j, rows, :] += dk[ATT_BLK:2 * ATT_BLK, sl]
                            dv_ref[j, rows, :] += dv[ATT_BLK:2 * ATT_BLK, sl]
                        else:
                            dk_ref[j, rows, :] += dk[:, sl]
                            dv_ref[j, rows, :] += dv[:, sl]
                return carry

            lax.fori_loop(0, S // ATT_BLK, block, 0)
        xc.middle()
        xc.finish()

    slab = jax.ShapeDtypeStruct((4, S, 128), F32)
    return _carry("att_bwd", body, exchange, exchange_args, (aq, ak, av, datt, att_out, lse), [VMEM] * 6, [VMEM] * 3,
                  [slab, slab, slab])


def _ret_bwd(qr, kr, rv, proj, o_raw, states, dret, tabs, exchange, exchange_args):
    C = RET_C
    nc = S // C
    dtab, a_tab, b_tab, lam, bd = tabs

    def body(q_ref, k_ref, v_ref, g_ref, o_ref, st_ref, dr_ref, dt_ref, a_ref, b_ref, lam_ref, bd_ref,
             dq_ref, dk_ref, dv_ref, dg_ref, dR, exch):
        @pl.when(pl.program_id(0) == 0)
        def _():
            exch.start()
            dR[...] = jnp.zeros_like(dR)

        q, k, v = q_ref[...], k_ref[...], v_ref[...]
        lane_head = lax.broadcasted_iota(jnp.int32, (C, 256), 1) // 32
        col_head = lax.broadcasted_iota(jnp.int32, (C, 256), 1) // 64
        dos = []
        for j in range(4):
            sl = slice(128 * j, 128 * j + 128)
            oj = o_ref[:, sl]
            xc = oj - _seg_mean(oj)
            rs = lax.rsqrt(_seg_mean(xc * xc) + GN_EPS)
            rn = xc * rs
            gj = g_ref[:, sl]
            sg = _sigmoid(gj)
            dret = dr_ref[:, sl]
            dg_ref[:, sl] = dret * rn * (sg * (1.0 + gj * (1.0 - sg)))
            drn = dret * (gj * sg)
            dos.append(rs * (drn - _seg_mean(drn) - rn * _seg_mean(drn * rn)))
        do = [jnp.concatenate(dos[0:2], axis=1), jnp.concatenate(dos[2:4], axis=1)]
        do8 = jnp.concatenate(do, axis=1).astype(BF16)
        drb = dR[...].astype(BF16)
        rb = st_ref[...]
        dq = _nt(do8, rb) * a_ref[...]
        dk = _nt(v, drb) * b_ref[...]
        kb = (k.astype(F32) * b_ref[...]).astype(BF16)
        dvall = _nn(kb, drb)
        qs = _stack_heads(q, lane_head, n=8)
        dec = dt_ref[...]
        p = (_nt(qs, k) * dec).astype(BF16)
        dos = [_stack_heads(do[g], col_head).astype(BF16) for g in range(2)]
        dp = jnp.concatenate([_nt(dos[g], v[:, 256 * g:256 * g + 256]) for g in range(2)], axis=0)
        ds = (dp * dec).astype(BF16)
        dq = dq + _unstack_heads(_nn(ds, k), lane_head, n=8)
        dk = dk + _tn(ds, qs)
        dv = [dvall[:, 256 * g:256 * g + 256] + _tn(p[4 * C * g:4 * C * (g + 1)], dos[g]) for g in range(2)]
        qa = (q.astype(F32) * a_ref[...]).astype(BF16)
        dR[...] = dR[...] * lam_ref[...] + _tn(qa, do8) * bd_ref[...]
        dq_ref[...] = dq
        dk_ref[...] = dk
        dv_ref[:, 0:256] = dv[0]
        dv_ref[:, 256:512] = dv[1]

        @pl.when(pl.program_id(0) == nc - 1)
        def _():
            exch.middle()
            exch.finish()

    rev = lambda w: pl.BlockSpec((C, w), lambda n: (nc - 1 - n, 0))
    full = lambda a: pl.BlockSpec(a.shape, lambda n: (0,) * a.ndim)
    return _carry(
        "ret_bwd", body, exchange, exchange_args, (qr, kr, rv, proj, o_raw, states, dret, dtab, a_tab, b_tab, lam, bd),
        [rev(256), rev(256), rev(512), pl.BlockSpec((C, 512), lambda n: (nc - 1 - n, 2)), rev(512),
         pl.BlockSpec((None, 256, 512), lambda n: (nc - 1 - n, 0, 0)), rev(512),
         full(dtab), full(a_tab), full(b_tab), full(lam), full(bd)],
        [rev(256), rev(256), rev(512), rev(512)],
        [jax.ShapeDtypeStruct((S, 256), F32), jax.ShapeDtypeStruct((S, 256), F32),
         jax.ShapeDtypeStruct((S, 512), F32), jax.ShapeDtypeStruct((S, 512), F32)],
        scratch_shapes=[pltpu.VMEM((256, 512), F32)], grid=(nc,), semantics=("arbitrary",))


def _rot_bwd(cos, sin, dqr, dkr, drv, drg, dq_att, dk_att, dv_att):
    tm = 256

    def body(cos_ref, sin_ref, dqr_ref, dkr_ref, drv_ref, drg_ref, dqa_ref, dka_ref, dva_ref, dp_ref):
        cr, ca, sr, sa = cos_ref[:, 0:256], cos_ref[:, 256:768], sin_ref[:, 0:256], sin_ref[:, 256:768]
        lo_r, lo_a = _rot_halves(tm)

        def unrot_r(g):
            gs = g * sr
            return g * cr + pltpu.roll(jnp.where(lo_r, -gs, 0.0), 16, 1) + pltpu.roll(jnp.where(lo_r, 0.0, gs), 240, 1)

        def unrot_a(g):
            gs = g * sa
            return g * ca + pltpu.roll(jnp.where(lo_a, -gs, 0.0), 8, 1) + pltpu.roll(jnp.where(lo_a, 0.0, gs), 504, 1)

        def wide(ref):
            return jnp.concatenate([ref[j] for j in range(4)], axis=1)

        dp_ref[:, 0:256] = unrot_r(dqr_ref[...]).astype(BF16)
        dp_ref[:, 256:512] = unrot_r(dkr_ref[...] * RET_SCALE).astype(BF16)
        dp_ref[:, 512:1024] = drv_ref[...].astype(BF16)
        dp_ref[:, 1024:1536] = drg_ref[...].astype(BF16)
        dp_ref[:, 1536:2048] = unrot_a(wide(dqa_ref)).astype(BF16)
        dp_ref[:, 2048:2560] = unrot_a(wide(dka_ref)).astype(BF16)
        dp_ref[:, 2560:3072] = wide(dva_ref).astype(BF16)

    row = lambda w: pl.BlockSpec((tm, w), lambda i: (i, 0))
    slab = pl.BlockSpec((4, tm, 128), lambda i: (0, i, 0))
    return pl.pallas_call(
        body, grid=(S // tm,), name="rot_bwd",
        in_specs=[row(768), row(768), row(256), row(256), row(512), row(512), slab, slab, slab],
        out_specs=row(PW), out_shape=jax.ShapeDtypeStruct((S, PW), BF16),
        compiler_params=_params("parallel"),
    )(cos, sin, dqr, dkr, drv, drg, dq_att, dk_att, dv_att)


def _win_bwd_w(h1, dproj, exchange, exchange_args):
    tm = 512

    def body(h_ref, dp_ref, dw_ref, acc, xc):
        k, i = pl.program_id(0), pl.program_id(1)

        @pl.when((k == 0) & (i == 0))
        def _():
            xc.start()

        @pl.when(i == 0)
        def _():
            acc[...] = jnp.zeros_like(acc)

        acc[...] += _tn(h_ref[...], dp_ref[...])

        @pl.when(i == S // tm - 1)
        def _():
            dw_ref[...] = acc[...].astype(BF16)

        @pl.when((k == N_CHIP - 1) & (i == S // tm - 1))
        def _():
            xc.middle()
            xc.finish()

    (dw,), out = _carry(
        "win_bwd_w", body, exchange, exchange_args, (h1, dproj),
        [pl.BlockSpec((tm, D), lambda k, i: (i, 0)), pl.BlockSpec((tm, WIN_C), lambda k, i: (i, k))],
        [pl.BlockSpec((None, D, WIN_C), lambda k, i: (k, 0, 0))],
        [jax.ShapeDtypeStruct((N_CHIP, D, WIN_C), BF16)],
        scratch_shapes=[pltpu.VMEM((D, WIN_C), F32)], grid=(N_CHIP, S // tm), semantics=("arbitrary", "arbitrary"))
    return dw, out


def _in_bwd(dproj, win_g, x, dx2, g1, exchange, exchange_args):
    tm = 512

    def body(dp_ref, w_ref, x_ref, dx2_ref, g_ref, dx_ref, dg_ref, xc):
        @pl.when(pl.program_id(0) == 0)
        def _():
            xc.start()
            dg_ref[...] = jnp.zeros_like(dg_ref)

        dh = _nt(dp_ref[:, 0:WIN_C], w_ref[0])
        for k in range(1, N_CHIP):
            dh = dh + _nt(dp_ref[:, k * WIN_C:(k + 1) * WIN_C], w_ref[k])
        xv = x_ref[...]
        r = _rstd(xv)
        xn = xv * r
        dg_ref[...] = dg_ref[...] + jnp.sum(dh * xn, axis=0, keepdims=True)
        t = dh * g_ref[...]
        dx_ref[...] = dx2_ref[...] + r * (t - xn * jnp.mean(t * xn, axis=-1, keepdims=True))

        @pl.when(pl.program_id(0) == S // tm - 1)
        def _():
            xc.middle()
            xc.finish()

    row = lambda w: pl.BlockSpec((tm, w), lambda i: (i, 0))
    vec = pl.BlockSpec((1, D), lambda i: (0, 0))
    return _carry("in_bwd", body, exchange, exchange_args, (dproj, win_g, x, dx2, g1),
                  [row(PW), pl.BlockSpec((N_CHIP, D, WIN_C), lambda i: (0, 0, 0)), row(D), row(D), vec],
                  [row(D), vec], [jax.ShapeDtypeStruct((S, D), F32), jax.ShapeDtypeStruct((1, D), F32)],
                  grid=(S // tm,), semantics=("arbitrary",))


ANY = pl.BlockSpec(memory_space=pl.ANY)
VMEM = pl.BlockSpec(memory_space=pltpu.VMEM)
FLIPS = ((1, 0), (0, 1), (1, 1))


def _place():
    x, y, c = lax.axis_index("x"), lax.axis_index("y"), lax.axis_index("c")
    chips = [((1 - x) if fx else x, (1 - y) if fy else y) for fx, fy in FLIPS]
    return x, y, c, 2 * x + y, chips


def _remote(src, dst, send_sem, recv_sem, device):
    return pltpu.make_async_remote_copy(src_ref=src, dst_ref=dst, send_sem=send_sem, recv_sem=recv_sem,
                                        device_id=device, device_id_type=MESH)


def _staggered(issue):
    c = lax.axis_index("c")

    @pl.when(c == 0)
    def _():
        issue((0, 1, 2))

    @pl.when(c == 1)
    def _():
        issue((1, 0, 2))


class _Exchange:
    aliases = {}

    def middle(self, ins, outs, sems):
        pass


class _GatherShards(_Exchange):
    def __init__(self, shards):
        n = self.n = len(shards)
        self.n_in = self.n_out = n
        self.out_shape = [jax.ShapeDtypeStruct((N_CHIP,) + s.shape, s.dtype) for s in shards]
        dma = pltpu.SemaphoreType.DMA
        self.scratch = [dma((3 * n,)), dma((3 * n,)), dma((3 * n,)), dma((3 * n,)), dma((n,)), dma((n,))]

    def _ici(self, ins, outs, sems, a, j, chip):
        x, y, c, me, chips = _place()
        half = ins[a].shape[0] // 2
        return _remote(ins[a].at[pl.ds(c * half, half), :], outs[a].at[me, pl.ds(c * half, half), :],
                       sems[0].at[3 * a + j], sems[1].at[3 * a + j], (*chip, c))

    def _fwd(self, outs, sems, a, j, chip, half_of):
        x, y, c, me, chips = _place()
        half = outs[a].shape[1] // 2
        blk = outs[a].at[2 * chip[0] + chip[1], pl.ds(half_of * half, half), :]
        return _remote(blk, blk, sems[2].at[3 * a + j], sems[3].at[3 * a + j], (x, y, 1 - c))

    def _own(self, ins, outs, sems, a):
        return _own_shard_to_sibling(ins[a], outs[a], sems[4].at[a], sems[5].at[a])

    def start(self, ins, outs, sems):
        chips = _place()[4]

        def issue(order):
            for a in range(self.n):
                for j in order:
                    self._ici(ins, outs, sems, a, j, chips[j]).start()

        _staggered(issue)
        for a in range(self.n):
            self._own(ins, outs, sems, a).start()

    def middle(self, ins, outs, sems):
        x, y, c, me, chips = _place()
        for a in range(self.n):
            for j, chip in enumerate(chips):
                half = outs[a].shape[1] // 2
                blk = outs[a].at[2 * chip[0] + chip[1], pl.ds(c * half, half), :]
                _remote(blk, blk, sems[0].at[3 * a + j], sems[1].at[3 * a + j], (x, y, c)).wait_recv()
                self._fwd(outs, sems, a, j, chip, c).start()

    def finish(self, ins, outs, sems):
        x, y, c, me, chips = _place()
        for a in range(self.n):
            for j, chip in enumerate(chips):
                self._fwd(outs, sems, a, j, chip, 1 - c).wait_recv()
        for a in range(self.n):
            for j, chip in enumerate(chips):
                self._ici(ins, outs, sems, a, j, chip).wait_send()
                self._fwd(outs, sems, a, j, chip, c).wait_send()
            self._own(ins, outs, sems, a).wait()


def _own_shard_to_sibling(shard_ref, gathered_ref, send_sem, recv_sem):
    x, y, c, me, chips = _place()
    return _remote(shard_ref, gathered_ref.at[me], send_sem, recv_sem, (x, y, 1 - c))


class _NoExchange(_Exchange):
    n_in = n_out = 0
    out_shape = ()
    scratch = ()

    def start(self, ins, outs, sems):
        pass

    def finish(self, ins, outs, sems):
        pass


class _ForwardGathered(_Exchange):
    def __init__(self, shards, own=True, forward=True):
        self.own, self.forward = own, forward
        n = self.n = len(shards)
        self.n_in, self.n_out = 2 * n, n
        self.out_shape = [jax.ShapeDtypeStruct((N_CHIP,) + s.shape, s.dtype) for s in shards]
        dma = pltpu.SemaphoreType.DMA
        self.scratch = [dma((3 * n,)), dma((3 * n,)), dma((n,)), dma((n,))]
        self.aliases = {n + a: a for a in range(n)}

    def _fwd(self, outs, sems, a, j, chip, half_of):
        x, y, c, me, chips = _place()
        half = outs[a].shape[1] // 2
        blk = outs[a].at[2 * chip[0] + chip[1], pl.ds(half_of * half, half), :]
        return _remote(blk, blk, sems[0].at[3 * a + j], sems[1].at[3 * a + j], (x, y, 1 - c))

    def _own(self, ins, outs, sems, a):
        return _own_shard_to_sibling(ins[a], outs[a], sems[2].at[a], sems[3].at[a])

    def start(self, ins, outs, sems):
        x, y, c, me, chips = _place()
        for a in range(self.n):
            for j, chip in enumerate(chips if self.forward else ()):
                self._fwd(outs, sems, a, j, chip, c).start()
        for a in range(self.n if self.own else 0):
            self._own(ins, outs, sems, a).start()

    def finish(self, ins, outs, sems):
        x, y, c, me, chips = _place()
        for a in range(self.n):
            for j, chip in enumerate(chips if self.forward else ()):
                self._fwd(outs, sems, a, j, chip, 1 - c).wait_recv()
        for a in range(self.n):
            for j, chip in enumerate(chips if self.forward else ()):
                self._fwd(outs, sems, a, j, chip, c).wait_send()
            if self.own:
                self._own(ins, outs, sems, a).wait()


HBM = pl.BlockSpec(memory_space=pltpu.HBM)
SEMS = pl.BlockSpec(memory_space=pltpu.SEMAPHORE)
DATAFLOW = pltpu.SideEffectType.DATAFLOW_SIDE_EFFECTING


class _OverIci:
    def __init__(self, name, sources, lands):
        self.name, self.n = name, len(sources)
        hbm = lambda t: pltpu.with_memory_space_constraint(t, pltpu.HBM)
        self.arrays = [hbm(t) for t in sources] + [hbm(t) for t in lands]

    def sent(self, src, land, a, chip):
        raise NotImplementedError

    def landed(self, land, a, chip):
        raise NotImplementedError

    def _copy(self, arr, sems, a, j, receiving):
        x, y, c, me, chips = _place()
        src, dst = self.sent(arr[a], arr[self.n + a], a, chips[j])
        if receiving:
            dst = self.landed(arr[self.n + a], a, chips[j])
        return _remote(src, dst, sems[0].at[3 * a + j], sems[1].at[3 * a + j], (*chips[j], c))

    def start(self, after):
        m = len(self.arrays)

        def body(*refs):
            arr, sems, token = refs[:m], refs[m + 1:m + 3], refs[-1]

            def issue(order):
                for a in range(self.n):
                    for j in order:
                        self._copy(arr, sems, a, j, False).start()

            _staggered(issue)
            token[...] = jnp.zeros_like(token)

        dma = pltpu.SemaphoreType.DMA
        outs = pl.pallas_call(
            body, name=self.name + "_start",
            out_shape=[dma((3 * self.n,)), dma((3 * self.n,))] + [pltpu.HBM(t.shape, t.dtype) for t in self.arrays]
                      + [jax.ShapeDtypeStruct((8, 128), F32)],
            in_specs=[HBM] * m + [ANY], out_specs=[SEMS, SEMS] + [HBM] * m + [VMEM],
            input_output_aliases={i: 2 + i for i in range(m)},
            compiler_params=pltpu.CompilerParams(has_side_effects=DATAFLOW),
        )(*self.arrays, after)
        self.sems, self.arrays = outs[0:2], list(outs[2:2 + m])
        return outs[-1]

    def wait(self, after):
        m = len(self.arrays)

        def body(*refs):
            arr, sems = refs[:m], refs[m:m + 2]
            for a in range(self.n):
                for j in range(3):
                    self._copy(arr, sems, a, j, False).wait_send()
                    self._copy(arr, sems, a, j, True).wait_recv()

        outs = pl.pallas_call(
            body, name=self.name + "_wait",
            out_shape=[pltpu.HBM(t.shape, t.dtype) for t in self.arrays],
            in_specs=[HBM] * m + [SEMS, SEMS, ANY], out_specs=[HBM] * m,
            input_output_aliases={i: i for i in range(m)},
            compiler_params=pltpu.CompilerParams(has_side_effects=DATAFLOW),
        )(*self.arrays, *self.sems, after)
        return list(outs[:self.n]), list(outs[self.n:])


class _GatherOverIci(_OverIci):
    def __init__(self, name, shards):
        super().__init__(name, shards, [lax.empty((N_CHIP,) + s.shape, s.dtype) for s in shards])

    @staticmethod
    def _half(ref):
        c = lax.axis_index("c")
        half = ref.shape[-2] // 2
        return pl.ds(c * half, half)

    def sent(self, src, land, a, chip):
        return src.at[self._half(src), :], land.at[_place()[3], self._half(src), :]

    def landed(self, land, a, chip):
        return land.at[2 * chip[0] + chip[1], self._half(land), :]


class _SumOverIci(_OverIci):
    def __init__(self, name, pre):
        super().__init__(name, pre, [lax.empty(p.shape, p.dtype) for p in pre])

    def sent(self, src, land, a, chip):
        return src.at[2 * chip[0] + chip[1]], land.at[_place()[3]]

    def landed(self, land, a, chip):
        return land.at[2 * chip[0] + chip[1]]


class _HalvesToSibling(_Exchange):
    def __init__(self, grads):
        n = self.n = len(grads)
        self.n_in = self.n_out = n
        self.out_shape = [jax.ShapeDtypeStruct((N_CHIP, g.shape[1] // 2, g.shape[2]), g.dtype) for g in grads]
        self.scratch = [pltpu.SemaphoreType.DMA((n,)), pltpu.SemaphoreType.DMA((n,))]

    def _copy(self, ins, outs, sems, a):
        x, y, c, me, chips = _place()
        half = ins[a].shape[1] // 2
        return _remote(ins[a].at[:, pl.ds((1 - c) * half, half), :], outs[a], sems[0].at[a], sems[1].at[a], (x, y, 1 - c))

    def start(self, ins, outs, sems):
        for a in range(self.n):
            self._copy(ins, outs, sems, a).start()

    def finish(self, ins, outs, sems):
        for a in range(self.n):
            self._copy(ins, outs, sems, a).wait_recv()
        for a in range(self.n):
            self._copy(ins, outs, sems, a).wait_send()


class _OverChips(_Exchange):
    def __init__(self, pre):
        n = self.n = len(pre)
        self.n_in = self.n_out = n
        self.out_shape = [jax.ShapeDtypeStruct(p.shape, p.dtype) for p in pre]
        dma = pltpu.SemaphoreType.DMA
        self.scratch = [dma((3 * n,)), dma((3 * n,))]

    def _ici(self, ins, outs, sems, a, j, chip):
        x, y, c, me, chips = _place()
        return _remote(ins[a].at[2 * chip[0] + chip[1]], outs[a].at[me], sems[0].at[3 * a + j], sems[1].at[3 * a + j],
                       (*chip, c))

    def start(self, ins, outs, sems):
        chips = _place()[4]

        def issue(order):
            for a in range(self.n):
                for j in order:
                    self._ici(ins, outs, sems, a, j, chips[j]).start()

        _staggered(issue)

    def finish(self, ins, outs, sems):
        x, y, c, me, chips = _place()
        for a in range(self.n):
            for j, chip in enumerate(chips):
                blk = outs[a].at[2 * chip[0] + chip[1]]
                _remote(blk, blk, sems[0].at[3 * a + j], sems[1].at[3 * a + j], (x, y, c)).wait_recv()
        for a in range(self.n):
            for j, chip in enumerate(chips):
                self._ici(ins, outs, sems, a, j, chip).wait_send()


class _ShareHalves(_Exchange):
    def __init__(self, fulls):
        n = self.n = len(fulls)
        self.n_in = self.n_out = n
        self.out_shape = [jax.ShapeDtypeStruct(f.shape, f.dtype) for f in fulls]
        self.scratch = [pltpu.SemaphoreType.DMA((n,)), pltpu.SemaphoreType.DMA((n,))]
        self.aliases = {a: a for a in range(n)}

    def _copy(self, outs, sems, a, half_of):
        x, y, c, me, chips = _place()
        half = outs[a].shape[0] // 2
        rows = outs[a].at[pl.ds(half_of * half, half), :]
        return _remote(rows, rows, sems[0].at[a], sems[1].at[a], (x, y, 1 - c))

    def start(self, ins, outs, sems):
        c = _place()[2]
        for a in range(self.n):
            self._copy(outs, sems, a, c).start()

    def finish(self, ins, outs, sems):
        c = _place()[2]
        for a in range(self.n):
            self._copy(outs, sems, a, 1 - c).wait_recv()
        for a in range(self.n):
            self._copy(outs, sems, a, c).wait_send()


class _GatherBlocks(_Exchange):
    def __init__(self, block):
        self.n_in = self.n_out = 1
        self.out_shape = [jax.ShapeDtypeStruct((8,) + block.shape, block.dtype)]
        dma = pltpu.SemaphoreType.DMA
        self.scratch = [dma((7,)), dma((7,)), dma]

    @staticmethod
    def _peer(f):
        x, y, c, me, chips = _place()
        return ((1 - x) if f & 4 else x, (1 - y) if f & 2 else y, (1 - c) if f & 1 else c)

    def start(self, ins, outs, sems):
        x, y, c, me, chips = _place()
        for f in range(1, 8):
            _remote(ins[0], outs[0].at[2 * me + c], sems[0].at[f - 1], sems[1].at[f - 1], self._peer(f)).start()
        pltpu.make_async_copy(ins[0], outs[0].at[2 * me + c], sems[2]).start()

    def finish(self, ins, outs, sems):
        x, y, c, me, chips = _place()
        for f in range(1, 8):
            px, py, pc = self._peer(f)
            blk = outs[0].at[4 * px + 2 * py + pc]
            _remote(blk, blk, sems[0].at[f - 1], sems[1].at[f - 1], (x, y, c)).wait_recv()
        for f in range(1, 8):
            _remote(ins[0], outs[0].at[2 * me + c], sems[0].at[f - 1], sems[1].at[f - 1], self._peer(f)).wait_send()
        pltpu.make_async_copy(ins[0], outs[0].at[2 * me + c], sems[2]).wait()


class _Both(_Exchange):
    def __init__(self, first, second):
        self.parts = (first, second)
        self.n_in, self.n_out = first.n_in + second.n_in, first.n_out + second.n_out
        self.out_shape = first.out_shape + second.out_shape
        self.scratch = first.scratch + second.scratch
        self.aliases = dict(first.aliases)
        self.aliases.update({first.n_in + i: first.n_out + o for i, o in second.aliases.items()})

    def _split(self, ins, outs, sems):
        a, b = self.parts
        return ((a, ins[:a.n_in], outs[:a.n_out], sems[:len(a.scratch)]),
                (b, ins[a.n_in:], outs[a.n_out:], sems[len(a.scratch):]))

    def start(self, ins, outs, sems):
        for ex, i, o, s in self._split(ins, outs, sems):
            ex.start(i, o, s)

    def middle(self, ins, outs, sems):
        for ex, i, o, s in self._split(ins, outs, sems):
            ex.middle(i, o, s)

    def finish(self, ins, outs, sems):
        for ex, i, o, s in self._split(ins, outs, sems):
            ex.finish(i, o, s)


class _Bound:
    def __init__(self, ex, ins, outs, sems):
        self.start = lambda: ex.start(ins, outs, sems)
        self.middle = lambda: ex.middle(ins, outs, sems)
        self.finish = lambda: ex.finish(ins, outs, sems)


def _carry(name, body, ex, ex_args, args, in_specs, out_specs, out_shape, scratch_shapes=(), grid=None, semantics=(),
           after=None):
    n_a, n_o, n_s = len(args), len(out_shape), len(scratch_shapes)
    behind = [] if after is None else [after]

    def full_body(*refs):
        p = 0
        groups = []
        for size in (n_a, ex.n_in, len(behind), n_o, ex.n_out, n_s, len(ex.scratch)):
            groups.append(refs[p:p + size])
            p += size
        a, ei, _, o, eo, s, es = groups
        body(*a, *o, *s, _Bound(ex, ei, eo, es))

    kwargs = {} if grid is None else {"grid": grid}
    outs = pl.pallas_call(
        full_body, name=name,
        in_specs=list(in_specs) + [ANY] * (ex.n_in + len(behind)), out_specs=list(out_specs) + [ANY] * ex.n_out,
        out_shape=list(out_shape) + list(ex.out_shape), scratch_shapes=list(scratch_shapes) + list(ex.scratch),
        input_output_aliases={n_a + i: n_o + o for i, o in ex.aliases.items()},
        compiler_params=_params(*semantics) if semantics else pltpu.CompilerParams(vmem_limit_bytes=VMEM_LIMIT),
        **kwargs,
    )(*args, *ex_args, *behind)
    return outs[:n_o], outs[n_o:]


def _exchange_alone(name, ex, ex_args):
    def body(xc):
        xc.start()
        xc.middle()
        xc.finish()

    return _carry(name, body, ex, ex_args, (), (), (), ())[1]


def _core_index():
    return lax.axis_index("c").astype(jnp.int32).reshape(1)


def _pair_sum(gs, gots):
    n = len(gs)
    _, r, cc = gs[0].shape
    half = r // 2

    def body(c_ref, *refs):
        for a in range(n):
            refs[2 * n + a][...] = (refs[a][...].astype(F32) + refs[n + a][...].astype(F32)).astype(BF16)

    mine = pl.BlockSpec((None, half, cc), lambda k, c_ref: (k, c_ref[0], 0))
    blk = pl.BlockSpec((None, half, cc), lambda k, c_ref: (k, 0, 0))
    return pl.pallas_call(
        body, name=f"pair_sum_{r}x{cc}",
        grid_spec=pltpu.PrefetchScalarGridSpec(
            num_scalar_prefetch=1, grid=(N_CHIP,), in_specs=[mine] * n + [blk] * n, out_specs=[blk] * n),
        out_shape=[jax.ShapeDtypeStruct((N_CHIP, half, cc), BF16)] * n,
        compiler_params=_params("parallel"),
    )(_core_index(), *gs, *gots)


def _chip_sum(pre, parts):
    n = len(parts)
    _, half, cc = parts[0].shape
    tr = half // 2
    me = 2 * lax.axis_index("x") + lax.axis_index("y")
    others = [k + (k >= me).astype(jnp.int32) for k in range(3)]
    where = jnp.stack([lax.axis_index("c"), me, *others]).astype(jnp.int32)

    def body(w_ref, *refs):
        for a in range(n):
            own, p1, p2, p3 = refs[4 * a:4 * a + 4]
            refs[4 * n + a][...] = ((own[...].astype(F32) + p1[...].astype(F32)) + p2[...].astype(F32)) + p3[...].astype(F32)

    slot = lambda s: pl.BlockSpec((None, tr, cc), lambda i, w_ref: (w_ref[s], i, 0))
    operands = []
    for a in range(n):
        operands += [pre[a], parts[a], parts[a], parts[a]]
    return pl.pallas_call(
        body, name=f"chip_sum_{half}x{cc}",
        grid_spec=pltpu.PrefetchScalarGridSpec(
            num_scalar_prefetch=1, grid=(2,),
            in_specs=[slot(1), slot(2), slot(3), slot(4)] * n,
            out_specs=[pl.BlockSpec((tr, cc), lambda i, w_ref: (2 * w_ref[0] + i, 0))] * n),
        out_shape=[jax.ShapeDtypeStruct((2 * half, cc), F32)] * n,
        compiler_params=_params("parallel"),
    )(where, *operands)


def _adamw_math(w, g, m, v):
    m = ADAM_B1 * m + (1.0 - ADAM_B1) * g
    v = ADAM_B2 * v + (1.0 - ADAM_B2) * (g * g)
    m_hat = m / (1.0 - ADAM_B1 ** ADAM_STEP)
    v_hat = v / (1.0 - ADAM_B2 ** ADAM_STEP)
    delta = -ADAM_LR * (m_hat / (jnp.sqrt(v_hat) + ADAM_EPS) + ADAM_WD * w)
    return delta, m, v


def _adamw(w, g, m, v, after=None):
    r, cc = w.shape
    tr = r // 4

    def body(w_ref, g_ref, m_ref, v_ref, go_ref, d_ref, nm_ref, nv_ref, _):
        g = g_ref[...]
        go_ref[...] = g
        d_ref[...], nm_ref[...], nv_ref[...] = _adamw_math(w_ref[...], g, m_ref[...], v_ref[...])

    blk = pl.BlockSpec((tr, cc), lambda i: (i, 0))
    return _carry(f"adamw_{r}x{cc}", body, _NoExchange(), (), (w, g, m, v), [blk] * 4, [blk] * 4,
                  [jax.ShapeDtypeStruct((r, cc), F32)] * 4, grid=(4,), semantics=("parallel",), after=after)[0]


def _pack8(rows):
    def body(*refs):
        out_ref = refs[-1]
        out_ref[...] = jnp.zeros_like(out_ref)
        for i, r in enumerate(refs[:-1]):
            out_ref[i:i + 1, :] = r[...]

    return pl.pallas_call(body, name="pack8", out_shape=jax.ShapeDtypeStruct((8, D), F32))(*rows)


def _adamw_gains(gall, w8, m8, v8):
    def body(ga_ref, w_ref, m_ref, v_ref, g_ref, d_ref, nm_ref, nv_ref):
        g = ga_ref[0]
        for dev in range(1, 8):
            g = g + ga_ref[dev]
        g_ref[...] = g
        d_ref[...], nm_ref[...], nv_ref[...] = _adamw_math(w_ref[...], g, m_ref[...], v_ref[...])

    return pl.pallas_call(
        body, name="adamw_gains",
        out_shape=[jax.ShapeDtypeStruct((8, D), F32)] * 4,
    )(gall, w8, m8, v8)


def kernel(x, positions, w_in, w_out, g_pre_mix, g_post_mix, g_pre_ffn, g_post_ffn, w_gate, w_up, w_down, loss_target, m_w_in, m_w_out, m_g_pre_mix, m_g_post_mix, m_g_pre_ffn, m_g_post_ffn, m_w_gate, m_w_up, m_w_down, v_w_in, v_w_out, v_g_pre_mix, v_g_post_mix, v_g_pre_ffn, v_g_post_ffn, v_w_gate, v_w_up, v_w_down):
    tr = lambda t: jnp.swapaxes(t, 1, 2)[0]
    shards = [w_in[0], w_out[0], tr(w_gate), tr(w_up), w_down[0]]
    moms = [m_w_in[0], m_w_out[0], tr(m_w_gate), tr(m_w_up), m_w_down[0]]
    vels = [v_w_in[0], v_w_out[0], tr(v_w_gate), tr(v_w_up), v_w_down[0]]
    xs, pos, tgt = x[0], positions.reshape(S, 1), loss_target[0]
    g1, g2, g3, g4 = g_pre_mix, g_post_mix, g_pre_ffn, g_post_ffn
    tabs = tuple(jnp.asarray(t) for t in _retention_tables())
    ifc, spread = _rotary_tables()
    ifc, spread = jnp.asarray(ifc), jnp.asarray(spread, dtype=BF16)
    bf = [s.astype(BF16) for s in shards]

    (win_g,) = _exchange_alone("gather_in", _GatherShards(bf[:1]), bf[:1])
    ffn_gather = _GatherOverIci("ffn_gather", bf[2:])
    token = ffn_gather.start(win_g)
    proj, h1 = _proj_fwd(xs, g1, win_g, token)
    qr, kr, rv, aq, ak, av, cos, sin = _rot_fwd(proj, pos, ifc, spread)
    (o_raw, cat_r, states), (wout_g,) = _ret_fwd(qr, kr, rv, proj, tabs, _GatherShards(bf[1:2]), bf[1:2])
    wout_g = wout_g.reshape(D, D)
    n_ffn = len(bf[2:])
    (att_out, lse, cat_a), ffn_gather.arrays[n_ffn:] = _att_fwd(
        aq, ak, av, _ForwardGathered(bf[2:], forward=False), ffn_gather.arrays)
    ffn_sh, ffn_lands = ffn_gather.wait(cat_a)
    (mix, x2, h3), (wg_g, wu_g, wd_g) = _mix_fwd(cat_r, cat_a, wout_g, xs, g2, g3,
                                                _ForwardGathered(bf[2:], own=False), [*ffn_sh, *ffn_lands])
    gt, up, a, f = _ffn_fwd(h3, wg_g, wu_g, wd_g)

    sq, dy, df, dg4 = _head_bwd(f, x2, tgt, g4)
    loss = 0.5 * lax.psum(sq[0, 0], ("x", "y", "c")) / D
    dgt, dup, dh3 = _ffn_bwd_act(df, gt, up, wg_g, wu_g, wd_g)
    ffn_grads = list(_ffn_bwd_w(a, df, h3, dgt, dup))
    (dx2, dmix, dg3, dg2), got = _norm_bwd(dh3, dy, x2, mix, g2, g3, _HalvesToSibling(ffn_grads), ffn_grads)
    ffn_sum = _SumOverIci("ffn_sum", _pair_sum(ffn_grads, got))
    token = ffn_sum.start(dmix)
    dret, datt, dwout = _mix_bwd(dmix, cat_r, cat_a, wout_g, token)
    (dq_att, dk_att, dv_att), _ = _att_bwd(aq, ak, av, datt, att_out, lse, _NoExchange(), ())
    (dqr, dkr, drv, drg), _ = _ret_bwd(qr, kr, rv, proj, o_raw, states, dret, tabs, _NoExchange(), ())
    sums = _chip_sum(*ffn_sum.wait(dqr))
    dproj = _rot_bwd(cos, sin, dqr, dkr, drv, drg, dq_att, dk_att, dv_att)
    dwin, ffn_full = _win_bwd_w(h1, dproj, _ShareHalves(sums), sums)
    in_grads = [dwin, dwout.reshape(N_CHIP, WOUT_R, D)]
    (dx, dg1), got = _in_bwd(dproj, win_g, xs, dx2, g1, _HalvesToSibling(in_grads), in_grads)

    in_sum = _SumOverIci("in_sum", [*_pair_sum(in_grads[:1], got[:1]), *_pair_sum(in_grads[1:], got[1:])])
    token = in_sum.start(dx)
    gblock = _pack8([dg1, dg2, dg3, dg4])
    (gall,) = _exchange_alone("gather_gains", _GatherBlocks(gblock), [gblock])
    ffn_upd = [_adamw(shards[2 + i], ffn_full[o], moms[2 + i], vels[2 + i], token)
               for i, o in enumerate((1, 2, 0))]
    pre, parts = in_sum.wait(ffn_upd[2][0])
    sums = [*_chip_sum(pre[:1], parts[:1]), *_chip_sum(pre[1:], parts[1:])]
    in_full = _exchange_alone("share_rest", _ShareHalves(sums), sums)
    upd = [_adamw(w, g, m, v) for w, g, m, v in zip(shards[:2], in_full, moms[:2], vels[:2])] + ffn_upd
    gg, gd, gm, gv = _adamw_gains(gall, _pack8([g1, g2, g3, g4]),
                                  _pack8([m_g_pre_mix, m_g_post_mix, m_g_pre_ffn, m_g_post_ffn]),
                                  _pack8([v_g_pre_mix, v_g_post_mix, v_g_pre_ffn, v_g_post_ffn]))

    def order(mats, vecs):
        back = lambda t: jnp.swapaxes(t[None], 1, 2)
        return ([mats[0][None], mats[1][None]] + [vecs[i:i + 1] for i in range(4)]
                + [back(mats[2]), back(mats[3]), mats[4][None]])

    return (loss, dx[None],
            *order([u[0] for u in upd], gg),
            *order([u[1] for u in upd], gd),
            *order([u[2] for u in upd], gm),
            *order([u[3] for u in upd], gv))
```

```python
import functools

import numpy as np
import jax
import jax.numpy as jnp
from jax import lax
from jax.experimental import pallas as pl
from jax.experimental.pallas import tpu as pltpu

F32, BF16 = jnp.float32, jnp.bfloat16
MESH = pl.DeviceIdType.MESH

S = 2048
D = 1024
PW = 3072
N_CHIP = 4
WIN_C = PW // N_CHIP
DFF = 2816
FF_C = DFF // N_CHIP
WOUT_R = D // N_CHIP
RMS_EPS = 1e-6
GN_EPS = 1e-5
RET_C = 128
RET_SCALE = 32 ** -0.5
ATT_BLK = 128
ATT_SCALE = 64 ** -0.5
PATTERN_DILATIONS = (1, 4, 16)
NEG = -1e30
VMEM_LIMIT = 56 * 1024 * 1024

ADAM_LR, ADAM_B1, ADAM_B2, ADAM_EPS, ADAM_WD, ADAM_STEP = 0.001, 0.9, 0.999, 1e-08, 0.01, 10


def _params(*sem):
    return pltpu.CompilerParams(dimension_semantics=sem, vmem_limit_bytes=VMEM_LIMIT)


def _nt(a, b):
    return lax.dot_general(a, b, (((1,), (1,)), ((), ())), preferred_element_type=F32)


def _tn(a, b):
    return lax.dot_general(a, b, (((0,), (0,)), ((), ())), preferred_element_type=F32)


def _nn(a, b):
    return jnp.dot(a, b, preferred_element_type=F32)


def _rstd(v):
    return lax.rsqrt(jnp.mean(v * v, axis=-1, keepdims=True) + RMS_EPS)


def _sigmoid(v):
    return 1.0 / (1.0 + jnp.exp(-v))


def _rows(i, t):
    return pl.ds(pl.multiple_of(i * t, t), t)


def _retention_tables():
    h = np.arange(8, dtype=np.float32)
    log_g = np.log1p(-np.exp2(-5.0 - h)).astype(np.float32)
    idx = np.arange(RET_C, dtype=np.float32)
    diff = idx[:, None] - idx[None, :]
    dtab = np.where(diff >= 0, np.exp(log_g[:, None, None] * np.maximum(diff, 0.0)), 0.0).astype(np.float32)
    dtab = dtab.reshape(8 * RET_C, RET_C)
    lane_head = np.arange(256) // 32
    a_tab = np.exp(log_g[lane_head][None, :] * (idx + 1.0)[:, None]).astype(np.float32)
    b_tab = np.exp(log_g[lane_head][None, :] * (RET_C - 1.0 - idx)[:, None]).astype(np.float32)
    lam = np.exp(log_g[lane_head] * RET_C).astype(np.float32)[:, None]
    bd = (lane_head[:, None] == (np.arange(512) // 64)[None, :]).astype(np.float32)
    return dtab, a_tab, b_tab, lam, bd


def _rotary_tables():
    inv_r = (1.0 / (np.float32(10000.0) ** np.linspace(0.0, 1.0, 16, dtype=np.float32))).astype(np.float32)
    inv_a = (np.float32(500000.0) ** (-np.arange(0, 16, 2, dtype=np.float32) / np.float32(16))).astype(np.float32)
    ifc = np.zeros((1, 128), np.float32)
    ifc[0, 0:16], ifc[0, 16:24] = inv_r, inv_a
    spread = np.zeros((128, 768), np.float32)
    for lane in range(256):
        spread[(lane % 32) % 16, lane] = 1.0
    for lane in range(512):
        d = lane % 64
        spread[16 + d % 8 if d < 16 else 24, 256 + lane] = 1.0
    return ifc, spread


def _proj_fwd(x, g1, win_g, after):
    tm = 512

    def body(x_ref, g_ref, w_ref, proj_ref, h_ref, _):
        xv = x_ref[...]
        h = (xv * _rstd(xv) * g_ref[...]).astype(BF16)
        h_ref[...] = h
        for k in range(N_CHIP):
            proj_ref[:, k * WIN_C:(k + 1) * WIN_C] = _nn(h, w_ref[k])

    return _carry(
        "proj_fwd", body, _NoExchange(), (), (x, g1, win_g),
        [pl.BlockSpec((tm, D), lambda i: (i, 0)), pl.BlockSpec((1, D), lambda i: (0, 0)),
         pl.BlockSpec((N_CHIP, D, WIN_C), lambda i: (0, 0, 0))],
        [pl.BlockSpec((tm, PW), lambda i: (i, 0)), pl.BlockSpec((tm, D), lambda i: (i, 0))],
        [jax.ShapeDtypeStruct((S, PW), F32), jax.ShapeDtypeStruct((S, D), BF16)],
        grid=(S // tm,), semantics=("parallel",), after=after)[0]


def _rot_halves(tm):
    lo_r = (lax.broadcasted_iota(jnp.int32, (tm, 256), 1) % 32) < 16
    lo_a = (lax.broadcasted_iota(jnp.int32, (tm, 512), 1) % 64) < 8
    return lo_r, lo_a


def _spread_exact(t, e):
    hi = t.astype(BF16)
    r1 = t - hi.astype(F32)
    mid = r1.astype(BF16)
    lo = (r1 - mid.astype(F32)).astype(BF16)
    return _nn(hi, e) + _nn(mid, e) + _nn(lo, e)


def _rot_fwd(proj, pos, ifc, spread):
    tm = 256

    def body(p_ref, pos_ref, ifc_ref, e_ref, qr_ref, kr_ref, rv_ref, aq_ref, ak_ref, av_ref, cos_ref, sin_ref):
        ang = pos_ref[...].astype(F32) * ifc_ref[...]
        cs = _spread_exact(jnp.cos(ang), e_ref[...])
        sn = _spread_exact(jnp.sin(ang), e_ref[...])
        cos_ref[...] = cs
        sin_ref[...] = sn
        cr, ca, sr, sa = cs[:, 0:256], cs[:, 256:768], sn[:, 0:256], sn[:, 256:768]
        lo_r, lo_a = _rot_halves(tm)

        def rot_r(v):
            return v * cr + sr * jnp.where(lo_r, -pltpu.roll(v, 240, 1), pltpu.roll(v, 16, 1))

        def rot_a(v):
            return v * ca + sa * jnp.where(lo_a, -pltpu.roll(v, 504, 1), pltpu.roll(v, 8, 1))

        qr_ref[...] = rot_r(p_ref[:, 0:256]).astype(BF16)
        kr_ref[...] = (rot_r(p_ref[:, 256:512]) * RET_SCALE).astype(BF16)
        rv_ref[...] = p_ref[:, 512:1024].astype(BF16)
        aq, ak = rot_a(p_ref[:, 1536:2048]), rot_a(p_ref[:, 2048:2560])
        for j in range(4):
            aq_ref[j] = aq[:, 128 * j:128 * j + 128]
            ak_ref[j] = ak[:, 128 * j:128 * j + 128]
            av_ref[j] = p_ref[:, 2560 + 128 * j:2560 + 128 * j + 128]

    row = lambda w: pl.BlockSpec((tm, w), lambda i: (i, 0))
    const = lambda w: pl.BlockSpec((1, w), lambda i: (0, 0))
    slab = pl.BlockSpec((4, tm, 128), lambda i: (0, i, 0))
    return pl.pallas_call(
        body, grid=(S // tm,), name="rot_fwd",
        in_specs=[row(PW), row(1), const(128), pl.BlockSpec((128, 768), lambda i: (0, 0))],
        out_specs=[row(256), row(256), row(512), slab, slab, slab, row(768), row(768)],
        out_shape=[jax.ShapeDtypeStruct((S, w), BF16) for w in (256, 256, 512)]
                  + [jax.ShapeDtypeStruct((4, S, 128), F32)] * 3 + [jax.ShapeDtypeStruct((S, 768), F32)] * 2,
        compiler_params=_params("parallel"),
    )(proj, pos, ifc, spread)


def _seg_mean(v):
    lo = lax.broadcasted_iota(jnp.int32, v.shape, 1) < 64
    s_lo = jnp.sum(jnp.where(lo, v, 0.0), axis=-1, keepdims=True)
    s_hi = jnp.sum(jnp.where(lo, 0.0, v), axis=-1, keepdims=True)
    return jnp.where(lo, s_lo, s_hi) * (1.0 / 64.0)


def _ret_fwd(qr, kr, rv, proj, tabs, exchange, exchange_args):
    C = RET_C
    dtab, a_tab, b_tab, lam, bd = tabs

    def body(q_ref, k_ref, v_ref, g_ref, dt_ref, a_ref, b_ref, lam_ref, bd_ref, o_ref, cat_ref, st_ref, R, exch):
        @pl.when(pl.program_id(0) == 0)
        def _():
            exch.start()
            R[...] = jnp.zeros_like(R)

        @pl.when(pl.program_id(0) == S // C // 2)
        def _():
            exch.middle()

        q, k, v = q_ref[...], k_ref[...], v_ref[...]
        lane_head = lax.broadcasted_iota(jnp.int32, (C, 256), 1) // 32
        col_head = lax.broadcasted_iota(jnp.int32, (C, 256), 1) // 64
        rb = R[...].astype(BF16)
        st_ref[...] = rb
        qa = (q.astype(F32) * a_ref[...]).astype(BF16)
        cross = _nn(qa, rb)
        p = (_nt(_stack_heads(q, lane_head, n=8), k) * dt_ref[...]).astype(BF16)
        og = [cross[:, 256 * g:256 * g + 256]
              + _unstack_heads(_nn(p[4 * C * g:4 * C * (g + 1)], v[:, 256 * g:256 * g + 256]), col_head)
              for g in range(2)]
        kb = (k.astype(F32) * b_ref[...]).astype(BF16)
        R[...] = R[...] * lam_ref[...] + _tn(kb, v) * bd_ref[...]
        o_ref[:, 0:256] = og[0]
        o_ref[:, 256:512] = og[1]
        for j in range(4):
            oj = og[j // 2][:, 128 * (j % 2):128 * (j % 2) + 128]
            xc = oj - _seg_mean(oj)
            rn = xc * lax.rsqrt(_seg_mean(xc * xc) + GN_EPS)
            gj = g_ref[:, 128 * j:128 * j + 128]
            cat_ref[:, 128 * j:128 * j + 128] = (rn * (gj * _sigmoid(gj))).astype(BF16)

        @pl.when(pl.program_id(0) == S // C - 1)
        def _():
            exch.finish()

    row = lambda w: pl.BlockSpec((C, w), lambda n: (n, 0))
    full = lambda a: pl.BlockSpec(a.shape, lambda n: (0,) * a.ndim)
    return _carry(
        "ret_fwd", body, exchange, exchange_args, (qr, kr, rv, proj, dtab, a_tab, b_tab, lam, bd),
        [row(256), row(256), row(512), pl.BlockSpec((C, 512), lambda n: (n, 2)),
         full(dtab), full(a_tab), full(b_tab), full(lam), full(bd)],
        [row(512), row(512), pl.BlockSpec((None, 256, 512), lambda n: (n, 0, 0))],
        [jax.ShapeDtypeStruct((S, 512), F32), jax.ShapeDtypeStruct((S, 512), BF16),
         jax.ShapeDtypeStruct((S // C, 256, 512), BF16)],
        scratch_shapes=[pltpu.VMEM((256, 512), F32)], grid=(S // C,), semantics=("arbitrary",))


def _stack_heads(v, lane_head, fill=0.0, n=4):
    return jnp.concatenate([jnp.where(lane_head == h, v, jnp.full_like(v, fill)) for h in range(n)], axis=0)


def _unstack_heads(v, lane_head, n=4):
    out = v[0:ATT_BLK]
    for h in range(1, n):
        out = jnp.where(lane_head == h, v[h * ATT_BLK:(h + 1) * ATT_BLK], out)
    return out


def _att_mask(ib, has_prev):
    nk = 2 * ATT_BLK if has_prev else ATT_BLK
    a = lax.broadcasted_iota(jnp.int32, (4 * ATT_BLK, nk), 0) % ATT_BLK
    kk = lax.broadcasted_iota(jnp.int32, (4 * ATT_BLK, nk), 1)
    if has_prev:
        dist = ATT_BLK + a - kk
        return (dist >= 0) & (dist <= ATT_BLK) & ((ib * ATT_BLK - ATT_BLK + kk) >= 0)
    return (a - kk) >= 0


def _class_rows(ib, r, d):
    if d == 1:
        return pl.ds(pl.multiple_of(ib * ATT_BLK, ATT_BLK), ATT_BLK)
    return pl.ds(ib * ATT_BLK * d + r, ATT_BLK, stride=d)


def _slab_pair(ref, g, rows):
    return jnp.concatenate([ref[2 * g, rows, :], ref[2 * g + 1, rows, :]], axis=1)


def _att_blocks(d):
    nb = S // d // ATT_BLK
    return nb, nb > 1


def _att_fwd(aq, ak, av, exchange, exchange_args):
    def body(q_ref, k_ref, v_ref, o_ref, l_ref, cat_ref, xc):
        xc.start()
        lane_head = lax.broadcasted_iota(jnp.int32, (ATT_BLK, 256), 1) // 64
        for pi, d in enumerate(PATTERN_DILATIONS):
            if pi == len(PATTERN_DILATIONS) - 1:
                xc.middle()
            nb, has_prev = _att_blocks(d)

            def block(b, carry, pi=pi, d=d, nb=nb, has_prev=has_prev):
                r, ib = b // nb, b % nb
                rows = _class_rows(ib, r, d)
                prow = _class_rows(jnp.maximum(ib - 1, 0), r, d)
                valid = _att_mask(ib, has_prev)
                for g in range(2):
                    qg = _slab_pair(q_ref, g, rows).astype(BF16)
                    kg = _slab_pair(k_ref, g, rows)
                    vg = _slab_pair(v_ref, g, rows)
                    if has_prev:
                        kg = jnp.concatenate([_slab_pair(k_ref, g, prow), kg], axis=0)
                        vg = jnp.concatenate([_slab_pair(v_ref, g, prow), vg], axis=0)
                    kg, vg = kg.astype(BF16), vg.astype(BF16)
                    s = jnp.where(valid, _nt(_stack_heads(qg, lane_head), kg) * ATT_SCALE, NEG)
                    m = jnp.max(s, axis=-1, keepdims=True)
                    p = jnp.exp(s - m)
                    den = jnp.sum(p, axis=-1, keepdims=True)
                    og = _unstack_heads(_nn(p.astype(BF16), vg) / den, lane_head)
                    lg = _unstack_heads(jnp.broadcast_to(m + jnp.log(den), (4 * ATT_BLK, 256)), lane_head)
                    for jj in range(2):
                        j = 2 * g + jj
                        o_new, l_new = og[:, 128 * jj:128 * jj + 128], lg[:, 128 * jj:128 * jj + 128]
                        if pi > 0:
                            o_old, l_old = o_ref[j, rows, :], l_ref[j, rows, :]
                            mx = jnp.maximum(l_old, l_new)
                            ea, eb = jnp.exp(l_old - mx), jnp.exp(l_new - mx)
                            den = ea + eb
                            o_new = (ea * o_old + eb * o_new) / den
                            l_new = mx + jnp.log(den)
                        o_ref[j, rows, :] = o_new
                        l_ref[j, rows, :] = l_new
                return carry

            lax.fori_loop(0, S // ATT_BLK, block, 0)

        def to_cat(i, carry):
            rows = _rows(i, 256)
            for j in range(4):
                cat_ref[rows, 128 * j:128 * j + 128] = o_ref[j, rows, :].astype(BF16)
            return carry

        lax.fori_loop(0, S // 256, to_cat, 0)
        xc.finish()

    slab = jax.ShapeDtypeStruct((4, S, 128), F32)
    return _carry("att_fwd", body, exchange, exchange_args, (aq, ak, av), [VMEM] * 3, [VMEM] * 3,
                  [slab, slab, jax.ShapeDtypeStruct((S, 512), BF16)])


def _mix_fwd(cat_r, cat_a, wout, x, g2, g3, exchange, exchange_args):
    tm = 512

    def body(cr_ref, ca_ref, w_ref, x_ref, g2_ref, g3_ref, mix_ref, x2_ref, h3_ref, xc):
        @pl.when(pl.program_id(0) == 0)
        def _():
            xc.start()

        mix = _nn(cr_ref[...], w_ref[0:512, :]) + _nn(ca_ref[...], w_ref[512:1024, :])
        mix_ref[...] = mix
        x2 = x_ref[...] + mix * _rstd(mix) * g2_ref[...]
        x2_ref[...] = x2
        h3_ref[...] = (x2 * _rstd(x2) * g3_ref[...]).astype(BF16)

        @pl.when(pl.program_id(0) == S // tm - 1)
        def _():
            xc.middle()
            xc.finish()

    row = lambda w: pl.BlockSpec((tm, w), lambda i: (i, 0))
    vec = pl.BlockSpec((1, D), lambda i: (0, 0))
    return _carry("mix_fwd", body, exchange, exchange_args, (cat_r, cat_a, wout, x, g2, g3),
                  [row(512), row(512), pl.BlockSpec((D, D), lambda i: (0, 0)), row(D), vec, vec],
                  [row(D), row(D), row(D)],
                  [jax.ShapeDtypeStruct((S, D), F32), jax.ShapeDtypeStruct((S, D), F32),
                   jax.ShapeDtypeStruct((S, D), BF16)],
                  grid=(S // tm,), semantics=("arbitrary",))


def _ffn_fwd(h3, wg, wu, wd):
    tm = 512

    def body(h_ref, wg_ref, wu_ref, wd_ref, gt_ref, up_ref, a_ref, f_ref):
        k, i = pl.program_id(0), pl.program_id(1)
        h = h_ref[...]
        gt = _nt(h, wg_ref[...])
        up = _nt(h, wu_ref[...])
        gt_ref[...] = gt.astype(BF16)
        up_ref[...] = up.astype(BF16)
        a = (gt * _sigmoid(gt) * up).astype(BF16)
        a_ref[...] = a
        part = _nn(a, wd_ref[...])
        rows = _rows(i, tm)

        @pl.when(k == 0)
        def _():
            f_ref[rows, :] = part

        @pl.when(k > 0)
        def _():
            f_ref[rows, :] = f_ref[rows, :] + part

    wrow = pl.BlockSpec((None, FF_C, D), lambda k, i: (k, 0, 0))
    act = pl.BlockSpec((None, tm, FF_C), lambda k, i: (k, i, 0))
    return pl.pallas_call(
        body, grid=(N_CHIP, S // tm), name="ffn_fwd",
        in_specs=[pl.BlockSpec((tm, D), lambda k, i: (i, 0)), wrow, wrow, wrow],
        out_specs=[act, act, act, pl.BlockSpec((S, D), lambda k, i: (0, 0))],
        out_shape=[jax.ShapeDtypeStruct((N_CHIP, S, FF_C), BF16)] * 3 + [jax.ShapeDtypeStruct((S, D), F32)],
        compiler_params=_params("arbitrary", "arbitrary"),
    )(h3, wg, wu, wd)


def _head_bwd(f, x2, tgt, g4):
    tm = 256

    def body(f_ref, x2_ref, t_ref, g_ref, loss_ref, dy_ref, df_ref, dg_ref):
        @pl.when(pl.program_id(0) == 0)
        def _():
            loss_ref[...] = jnp.zeros_like(loss_ref)
            dg_ref[...] = jnp.zeros_like(dg_ref)

        fv = f_ref[...]
        r = _rstd(fv)
        fn = fv * r
        e = x2_ref[...] + fn * g_ref[...] - t_ref[...]
        sq = jnp.sum(jnp.sum(e * e, axis=-1, keepdims=True), axis=0, keepdims=True)
        loss_ref[...] = loss_ref[...] + sq
        dy = e * (1.0 / D)
        dy_ref[...] = dy
        dg_ref[...] = dg_ref[...] + jnp.sum(dy * fn, axis=0, keepdims=True)
        t = dy * g_ref[...]
        df_ref[...] = (r * (t - fn * jnp.mean(t * fn, axis=-1, keepdims=True))).astype(BF16)

    row = pl.BlockSpec((tm, D), lambda i: (i, 0))
    vec = pl.BlockSpec((1, D), lambda i: (0, 0))
    return pl.pallas_call(
        body, grid=(S // tm,), name="head_bwd",
        in_specs=[row, row, row, vec],
        out_specs=[pl.BlockSpec((8, 128), lambda i: (0, 0)), row, row, vec],
        out_shape=[jax.ShapeDtypeStruct((8, 128), F32), jax.ShapeDtypeStruct((S, D), F32),
                   jax.ShapeDtypeStruct((S, D), BF16), jax.ShapeDtypeStruct((1, D), F32)],
        compiler_params=_params("arbitrary"),
    )(f, x2, tgt, g4)


def _ffn_bwd_act(df, gt, up, wg, wu, wd):
    tm = 512

    def body(df_ref, gt_ref, up_ref, wg_ref, wu_ref, wd_ref, dgt_ref, dup_ref, dh_ref):
        k = pl.program_id(1)
        da = _nt(df_ref[...], wd_ref[...])
        gt, up = gt_ref[...].astype(F32), up_ref[...].astype(F32)
        sg = _sigmoid(gt)
        dup = (da * gt * sg).astype(BF16)
        dgt = (da * up * (sg * (1.0 + gt * (1.0 - sg)))).astype(BF16)
        dup_ref[...] = dup
        dgt_ref[...] = dgt
        part = _nn(dgt, wg_ref[...]) + _nn(dup, wu_ref[...])

        @pl.when(k == 0)
        def _():
            dh_ref[...] = part

        @pl.when(k > 0)
        def _():
            dh_ref[...] = dh_ref[...] + part

    wrow = pl.BlockSpec((None, FF_C, D), lambda i, k: (k, 0, 0))
    act = pl.BlockSpec((None, tm, FF_C), lambda i, k: (k, i, 0))
    row = pl.BlockSpec((tm, D), lambda i, k: (i, 0))
    return pl.pallas_call(
        body, grid=(S // tm, N_CHIP), name="ffn_bwd_act",
        in_specs=[row, act, act, wrow, wrow, wrow],
        out_specs=[act, act, row],
        out_shape=[jax.ShapeDtypeStruct((N_CHIP, S, FF_C), BF16), jax.ShapeDtypeStruct((N_CHIP, S, FF_C), BF16),
                   jax.ShapeDtypeStruct((S, D), F32)],
        compiler_params=_params("parallel", "arbitrary"),
    )(df, gt, up, wg, wu, wd)


def _ffn_bwd_w(a, df, h3, dgt, dup):
    tm = 1024
    assert S // tm == 2

    def body(a_ref, df_ref, h_ref, dgt_ref, dup_ref, dwd_ref, dwg_ref, dwu_ref, acc_d, acc_g, acc_u):
        i = pl.program_id(1)
        h = h_ref[...]
        parts = (_tn(a_ref[...], df_ref[...]), _tn(dgt_ref[...], h), _tn(dup_ref[...], h))

        @pl.when(i == 0)
        def _():
            for acc, part in zip((acc_d, acc_g, acc_u), parts):
                acc[...] = part

        @pl.when(i == S // tm - 1)
        def _():
            for out, acc, part in zip((dwd_ref, dwg_ref, dwu_ref), (acc_d, acc_g, acc_u), parts):
                out[...] = (acc[...] + part).astype(BF16)

    act = pl.BlockSpec((None, tm, FF_C), lambda k, i: (k, i, 0))
    row = pl.BlockSpec((tm, D), lambda k, i: (i, 0))
    wrow = pl.BlockSpec((None, FF_C, D), lambda k, i: (k, 0, 0))
    return pl.pallas_call(
        body, grid=(N_CHIP, S // tm), name="ffn_bwd_w",
        in_specs=[act, row, row, act, act],
        out_specs=[wrow, wrow, wrow],
        out_shape=[jax.ShapeDtypeStruct((N_CHIP, FF_C, D), BF16)] * 3,
        scratch_shapes=[pltpu.VMEM((FF_C, D), F32)] * 3,
        compiler_params=_params("parallel", "arbitrary"),
    )(a, df, h3, dgt, dup)


def _norm_bwd(dh3, dy, x2, mix, g2, g3, exchange, exchange_args):
    tm = 256

    def body(dh_ref, dy_ref, x2_ref, mix_ref, g2_ref, g3_ref, dx2_ref, dmix_ref, dg3_ref, dg2_ref, xc):
        @pl.when(pl.program_id(0) == 0)
        def _():
            xc.start()
            dg3_ref[...] = jnp.zeros_like(dg3_ref)
            dg2_ref[...] = jnp.zeros_like(dg2_ref)

        x2 = x2_ref[...]
        r3 = _rstd(x2)
        xn = x2 * r3
        dh = dh_ref[...]
        dg3_ref[...] = dg3_ref[...] + jnp.sum(dh * xn, axis=0, keepdims=True)
        t = dh * g3_ref[...]
        dx2 = dy_ref[...] + r3 * (t - xn * jnp.mean(t * xn, axis=-1, keepdims=True))
        dx2_ref[...] = dx2
        mix = mix_ref[...]
        r2 = _rstd(mix)
        mn = mix * r2
        dg2_ref[...] = dg2_ref[...] + jnp.sum(dx2 * mn, axis=0, keepdims=True)
        u = dx2 * g2_ref[...]
        dmix_ref[...] = (r2 * (u - mn * jnp.mean(u * mn, axis=-1, keepdims=True))).astype(BF16)

        @pl.when(pl.program_id(0) == S // tm - 1)
        def _():
            xc.middle()
            xc.finish()

    row = pl.BlockSpec((tm, D), lambda i: (i, 0))
    vec = pl.BlockSpec((1, D), lambda i: (0, 0))
    return _carry("norm_bwd", body, exchange, exchange_args, (dh3, dy, x2, mix, g2, g3),
                  [row, row, row, row, vec, vec], [row, row, vec, vec],
                  [jax.ShapeDtypeStruct((S, D), F32), jax.ShapeDtypeStruct((S, D), BF16),
                   jax.ShapeDtypeStruct((1, D), F32), jax.ShapeDtypeStruct((1, D), F32)],
                  grid=(S // tm,), semantics=("arbitrary",))


def _mix_bwd(dmix, cat_r, cat_a, wout, after):
    tm = 512

    def body(dm_ref, cr_ref, ca_ref, w_ref, dret_ref, datt_ref, dw_ref, acc, _):
        i = pl.program_id(0)

        @pl.when(i == 0)
        def _():
            acc[...] = jnp.zeros_like(acc)

        dm = dm_ref[...]
        dret_ref[...] = _nt(dm, w_ref[0:512, :])
        datt = _nt(dm, w_ref[512:1024, :])
        for j in range(4):
            datt_ref[j] = datt[:, 128 * j:128 * j + 128]
        acc[0:512, :] += _tn(cr_ref[...], dm)
        acc[512:1024, :] += _tn(ca_ref[...], dm)

        @pl.when(i == S // tm - 1)
        def _():
            dw_ref[...] = acc[...].astype(BF16)

    row = lambda w: pl.BlockSpec((tm, w), lambda i: (i, 0))
    full = pl.BlockSpec((D, D), lambda i: (0, 0))
    return _carry("mix_bwd", body, _NoExchange(), (), (dmix, cat_r, cat_a, wout),
                  [row(D), row(512), row(512), full],
                  [row(512), pl.BlockSpec((4, tm, 128), lambda i: (0, i, 0)), full],
                  [jax.ShapeDtypeStruct((S, 512), F32), jax.ShapeDtypeStruct((4, S, 128), F32),
                   jax.ShapeDtypeStruct((D, D), BF16)],
                  scratch_shapes=[pltpu.VMEM((D, D), F32)], grid=(S // tm,), semantics=("arbitrary",), after=after)[0]


def _att_bwd(aq, ak, av, datt, att_out, lse, exchange, exchange_args):
    def body(q_ref, k_ref, v_ref, do_ref, out_ref, l_ref, dq_ref, dk_ref, dv_ref, xc):
        xc.start()

        def clear(i, carry):
            rows = _rows(i, 256)
            for ref in (dq_ref, dk_ref, dv_ref):
                for j in range(4):
                    ref[j, rows, :] = jnp.zeros((256, 128), F32)
            return carry

        lax.fori_loop(0, S // 256, clear, 0)
        lane_head = lax.broadcasted_iota(jnp.int32, (ATT_BLK, 256), 1) // 64
        for d in PATTERN_DILATIONS:
            nb, has_prev = _att_blocks(d)

            def block(b, carry, d=d, nb=nb, has_prev=has_prev):
                r, ib = b // nb, b % nb
                rows = _class_rows(ib, r, d)
                prow = _class_rows(jnp.maximum(ib - 1, 0), r, d)
                valid = _att_mask(ib, has_prev)
                for g in range(2):
                    qg = _slab_pair(q_ref, g, rows).astype(BF16)
                    kg = _slab_pair(k_ref, g, rows)
                    vg = _slab_pair(v_ref, g, rows)
                    if has_prev:
                        kg = jnp.concatenate([_slab_pair(k_ref, g, prow), kg], axis=0)
                        vg = jnp.concatenate([_slab_pair(v_ref, g, prow), vg], axis=0)
                    kg, vg = kg.astype(BF16), vg.astype(BF16)
                    dog = _slab_pair(do_ref, g, rows)
                    outg = _slab_pair(out_ref, g, rows)
                    lg = _slab_pair(l_ref, g, rows)
                    qs = _stack_heads(qg, lane_head)
                    dos = _stack_heads(dog, lane_head)
                    delta = jnp.sum(dos * jnp.concatenate([outg] * 4, axis=0), axis=-1, keepdims=True)
                    lh = jnp.max(_stack_heads(lg, lane_head, NEG), axis=-1, keepdims=True)
                    s = jnp.where(valid, _nt(qs, kg) * ATT_SCALE, NEG)
                    p = jnp.exp(s - lh)
                    dosb = dos.astype(BF16)
                    ds = (p * (_nt(dosb, vg) - delta) * ATT_SCALE).astype(BF16)
                    dq = _unstack_heads(_nn(ds, kg), lane_head)
                    dk = _tn(ds, qs)
                    dv = _tn(p.astype(BF16), dosb)
                    for jj in range(2):
                        j, sl = 2 * g + jj, slice(128 * jj, 128 * jj + 128)
                        dq_ref[j, rows, :] += dq[:, sl]
                        if has_prev:
                            dk_ref[j, prow, :] += dk[0:ATT_BLK, sl]
                            dv_ref[j, prow, :] += dv[0:ATT_BLK, sl]
                            dk_ref[j, rows, :] += dk[ATT_BLK:2 * ATT_BLK, sl]
                            dv_ref[j, rows, :] += dv[ATT_BLK:2 * ATT_BLK, sl]
                        else:
                            dk_ref[j, rows, :] += dk[:, sl]
                            dv_ref[j, rows, :] += dv[:, sl]
                return carry

            lax.fori_loop(0, S // ATT_BLK, block, 0)
        xc.middle()
        xc.finish()

    slab = jax.ShapeDtypeStruct((4, S, 128), F32)
    return _carry("att_bwd", body, exchange, exchange_args, (aq, ak, av, datt, att_out, lse), [VMEM] * 6, [VMEM] * 3,
                  [slab, slab, slab])


def _ret_bwd(qr, kr, rv, proj, o_raw, states, dret, tabs, exchange, exchange_args):
    C = RET_C
    nc = S // C
    dtab, a_tab, b_tab, lam, bd = tabs

    def body(q_ref, k_ref, v_ref, g_ref, o_ref, st_ref, dr_ref, dt_ref, a_ref, b_ref, lam_ref, bd_ref,
             dq_ref, dk_ref, dv_ref, dg_ref, dR, exch):
        @pl.when(pl.program_id(0) == 0)
        def _():
            exch.start()
            dR[...] = jnp.zeros_like(dR)

        q, k, v = q_ref[...], k_ref[...], v_ref[...]
        lane_head = lax.broadcasted_iota(jnp.int32, (C, 256), 1) // 32
        col_head = lax.broadcasted_iota(jnp.int32, (C, 256), 1) // 64
        dos = []
        for j in range(4):
            sl = slice(128 * j, 128 * j + 128)
            oj = o_ref[:, sl]
            xc = oj - _seg_mean(oj)
            rs = lax.rsqrt(_seg_mean(xc * xc) + GN_EPS)
            rn = xc * rs
            gj = g_ref[:, sl]
            sg = _sigmoid(gj)
            dret = dr_ref[:, sl]
            dg_ref[:, sl] = dret * rn * (sg * (1.0 + gj * (1.0 - sg)))
            drn = dret * (gj * sg)
            dos.append(rs * (drn - _seg_mean(drn) - rn * _seg_mean(drn * rn)))
        do = [jnp.concatenate(dos[0:2], axis=1), jnp.concatenate(dos[2:4], axis=1)]
        do8 = jnp.concatenate(do, axis=1).astype(BF16)
        drb = dR[...].astype(BF16)
        rb = st_ref[...]
        dq = _nt(do8, rb) * a_ref[...]
        dk = _nt(v, drb) * b_ref[...]
        kb = (k.astype(F32) * b_ref[...]).astype(BF16)
        dvall = _nn(kb, drb)
        qs = _stack_heads(q, lane_head, n=8)
        dec = dt_ref[...]
        p = (_nt(qs, k) * dec).astype(BF16)
        dos = [_stack_heads(do[g], col_head).astype(BF16) for g in range(2)]
        dp = jnp.concatenate([_nt(dos[g], v[:, 256 * g:256 * g + 256]) for g in range(2)], axis=0)
        ds = (dp * dec).astype(BF16)
        dq = dq + _unstack_heads(_nn(ds, k), lane_head, n=8)
        dk = dk + _tn(ds, qs)
        dv = [dvall[:, 256 * g:256 * g + 256] + _tn(p[4 * C * g:4 * C * (g + 1)], dos[g]) for g in range(2)]
        qa = (q.astype(F32) * a_ref[...]).astype(BF16)
        dR[...] = dR[...] * lam_ref[...] + _tn(qa, do8) * bd_ref[...]
        dq_ref[...] = dq
        dk_ref[...] = dk
        dv_ref[:, 0:256] = dv[0]
        dv_ref[:, 256:512] = dv[1]

        @pl.when(pl.program_id(0) == nc - 1)
        def _():
            exch.middle()
            exch.finish()

    rev = lambda w: pl.BlockSpec((C, w), lambda n: (nc - 1 - n, 0))
    full = lambda a: pl.BlockSpec(a.shape, lambda n: (0,) * a.ndim)
    return _carry(
        "ret_bwd", body, exchange, exchange_args, (qr, kr, rv, proj, o_raw, states, dret, dtab, a_tab, b_tab, lam, bd),
        [rev(256), rev(256), rev(512), pl.BlockSpec((C, 512), lambda n: (nc - 1 - n, 2)), rev(512),
         pl.BlockSpec((None, 256, 512), lambda n: (nc - 1 - n, 0, 0)), rev(512),
         full(dtab), full(a_tab), full(b_tab), full(lam), full(bd)],
        [rev(256), rev(256), rev(512), rev(512)],
        [jax.ShapeDtypeStruct((S, 256), F32), jax.ShapeDtypeStruct((S, 256), F32),
         jax.ShapeDtypeStruct((S, 512), F32), jax.ShapeDtypeStruct((S, 512), F32)],
        scratch_shapes=[pltpu.VMEM((256, 512), F32)], grid=(nc,), semantics=("arbitrary",))


def _rot_bwd(cos, sin, dqr, dkr, drv, drg, dq_att, dk_att, dv_att):
    tm = 256

    def body(cos_ref, sin_ref, dqr_ref, dkr_ref, drv_ref, drg_ref, dqa_ref, dka_ref, dva_ref, dp_ref):
        cr, ca, sr, sa = cos_ref[:, 0:256], cos_ref[:, 256:768], sin_ref[:, 0:256], sin_ref[:, 256:768]
        lo_r, lo_a = _rot_halves(tm)

        def unrot_r(g):
            gs = g * sr
            return g * cr + pltpu.roll(jnp.where(lo_r, -gs, 0.0), 16, 1) + pltpu.roll(jnp.where(lo_r, 0.0, gs), 240, 1)

        def unrot_a(g):
            gs = g * sa
            return g * ca + pltpu.roll(jnp.where(lo_a, -gs, 0.0), 8, 1) + pltpu.roll(jnp.where(lo_a, 0.0, gs), 504, 1)

        def wide(ref):
            return jnp.concatenate([ref[j] for j in range(4)], axis=1)

        dp_ref[:, 0:256] = unrot_r(dqr_ref[...]).astype(BF16)
        dp_ref[:, 256:512] = unrot_r(dkr_ref[...] * RET_SCALE).astype(BF16)
        dp_ref[:, 512:1024] = drv_ref[...].astype(BF16)
        dp_ref[:, 1024:1536] = drg_ref[...].astype(BF16)
        dp_ref[:, 1536:2048] = unrot_a(wide(dqa_ref)).astype(BF16)
        dp_ref[:, 2048:2560] = unrot_a(wide(dka_ref)).astype(BF16)
        dp_ref[:, 2560:3072] = wide(dva_ref).astype(BF16)

    row = lambda w: pl.BlockSpec((tm, w), lambda i: (i, 0))
    slab = pl.BlockSpec((4, tm, 128), lambda i: (0, i, 0))
    return pl.pallas_call(
        body, grid=(S // tm,), name="rot_bwd",
        in_specs=[row(768), row(768), row(256), row(256), row(512), row(512), slab, slab, slab],
        out_specs=row(PW), out_shape=jax.ShapeDtypeStruct((S, PW), BF16),
        compiler_params=_params("parallel"),
    )(cos, sin, dqr, dkr, drv, drg, dq_att, dk_att, dv_att)


def _win_bwd_w(h1, dproj, exchange, exchange_args):
    tm = 512

    def body(h_ref, dp_ref, dw_ref, acc, xc):
        k, i = pl.program_id(0), pl.program_id(1)

        @pl.when((k == 0) & (i == 0))
        def _():
            xc.start()

        @pl.when(i == 0)
        def _():
            acc[...] = jnp.zeros_like(acc)

        acc[...] += _tn(h_ref[...], dp_ref[...])

        @pl.when(i == S // tm - 1)
        def _():
            dw_ref[...] = acc[...].astype(BF16)

        @pl.when((k == N_CHIP - 1) & (i == S // tm - 1))
        def _():
            xc.middle()
            xc.finish()

    (dw,), out = _carry(
        "win_bwd_w", body, exchange, exchange_args, (h1, dproj),
        [pl.BlockSpec((tm, D), lambda k, i: (i, 0)), pl.BlockSpec((tm, WIN_C), lambda k, i: (i, k))],
        [pl.BlockSpec((None, D, WIN_C), lambda k, i: (k, 0, 0))],
        [jax.ShapeDtypeStruct((N_CHIP, D, WIN_C), BF16)],
        scratch_shapes=[pltpu.VMEM((D, WIN_C), F32)], grid=(N_CHIP, S // tm), semantics=("arbitrary", "arbitrary"))
    return dw, out


def _in_bwd(dproj, win_g, x, dx2, g1, after):
    tm = 512

    def body(dp_ref, w_ref, x_ref, dx2_ref, g_ref, dx_ref, dg_ref, _):
        @pl.when(pl.program_id(0) == 0)
        def _():
            dg_ref[...] = jnp.zeros_like(dg_ref)

        dh = _nt(dp_ref[:, 0:WIN_C], w_ref[0])
        for k in range(1, N_CHIP):
            dh = dh + _nt(dp_ref[:, k * WIN_C:(k + 1) * WIN_C], w_ref[k])
        xv = x_ref[...]
        r = _rstd(xv)
        xn = xv * r
        dg_ref[...] = dg_ref[...] + jnp.sum(dh * xn, axis=0, keepdims=True)
        t = dh * g_ref[...]
        dx_ref[...] = dx2_ref[...] + r * (t - xn * jnp.mean(t * xn, axis=-1, keepdims=True))

    row = lambda w: pl.BlockSpec((tm, w), lambda i: (i, 0))
    vec = pl.BlockSpec((1, D), lambda i: (0, 0))
    return _carry("in_bwd", body, _NoExchange(), (), (dproj, win_g, x, dx2, g1),
                  [row(PW), pl.BlockSpec((N_CHIP, D, WIN_C), lambda i: (0, 0, 0)), row(D), row(D), vec],
                  [row(D), vec], [jax.ShapeDtypeStruct((S, D), F32), jax.ShapeDtypeStruct((1, D), F32)],
                  grid=(S // tm,), semantics=("arbitrary",), after=after)[0]


ANY = pl.BlockSpec(memory_space=pl.ANY)
VMEM = pl.BlockSpec(memory_space=pltpu.VMEM)
FLIPS = ((1, 0), (0, 1), (1, 1))


def _place():
    x, y, c = lax.axis_index("x"), lax.axis_index("y"), lax.axis_index("c")
    chips = [((1 - x) if fx else x, (1 - y) if fy else y) for fx, fy in FLIPS]
    return x, y, c, 2 * x + y, chips


def _remote(src, dst, send_sem, recv_sem, device):
    return pltpu.make_async_remote_copy(src_ref=src, dst_ref=dst, send_sem=send_sem, recv_sem=recv_sem,
                                        device_id=device, device_id_type=MESH)


def _staggered(issue):
    c = lax.axis_index("c")

    @pl.when(c == 0)
    def _():
        issue((0, 1, 2))

    @pl.when(c == 1)
    def _():
        issue((1, 0, 2))


class _Exchange:
    aliases = {}

    def middle(self, ins, outs, sems):
        pass


class _GatherShards(_Exchange):
    def __init__(self, shards):
        n = self.n = len(shards)
        self.n_in = self.n_out = n
        self.out_shape = [jax.ShapeDtypeStruct((N_CHIP,) + s.shape, s.dtype) for s in shards]
        dma = pltpu.SemaphoreType.DMA
        self.scratch = [dma((3 * n,)), dma((3 * n,)), dma((3 * n,)), dma((3 * n,)), dma((n,)), dma((n,))]

    def _ici(self, ins, outs, sems, a, j, chip):
        x, y, c, me, chips = _place()
        half = ins[a].shape[0] // 2
        return _remote(ins[a].at[pl.ds(c * half, half), :], outs[a].at[me, pl.ds(c * half, half), :],
                       sems[0].at[3 * a + j], sems[1].at[3 * a + j], (*chip, c))

    def _fwd(self, outs, sems, a, j, chip, half_of):
        x, y, c, me, chips = _place()
        half = outs[a].shape[1] // 2
        blk = outs[a].at[2 * chip[0] + chip[1], pl.ds(half_of * half, half), :]
        return _remote(blk, blk, sems[2].at[3 * a + j], sems[3].at[3 * a + j], (x, y, 1 - c))

    def _own(self, ins, outs, sems, a):
        return _own_shard_to_sibling(ins[a], outs[a], sems[4].at[a], sems[5].at[a])

    def start(self, ins, outs, sems):
        chips = _place()[4]

        def issue(order):
            for a in range(self.n):
                for j in order:
                    self._ici(ins, outs, sems, a, j, chips[j]).start()

        _staggered(issue)
        for a in range(self.n):
            self._own(ins, outs, sems, a).start()

    def middle(self, ins, outs, sems):
        x, y, c, me, chips = _place()
        for a in range(self.n):
            for j, chip in enumerate(chips):
                half = outs[a].shape[1] // 2
                blk = outs[a].at[2 * chip[0] + chip[1], pl.ds(c * half, half), :]
                _remote(blk, blk, sems[0].at[3 * a + j], sems[1].at[3 * a + j], (x, y, c)).wait_recv()
                self._fwd(outs, sems, a, j, chip, c).start()

    def finish(self, ins, outs, sems):
        x, y, c, me, chips = _place()
        for a in range(self.n):
            for j, chip in enumerate(chips):
                self._fwd(outs, sems, a, j, chip, 1 - c).wait_recv()
        for a in range(self.n):
            for j, chip in enumerate(chips):
                self._ici(ins, outs, sems, a, j, chip).wait_send()
                self._fwd(outs, sems, a, j, chip, c).wait_send()
            self._own(ins, outs, sems, a).wait()


def _own_shard_to_sibling(shard_ref, gathered_ref, send_sem, recv_sem):
    x, y, c, me, chips = _place()
    return _remote(shard_ref, gathered_ref.at[me], send_sem, recv_sem, (x, y, 1 - c))


class _NoExchange(_Exchange):
    n_in = n_out = 0
    out_shape = ()
    scratch = ()

    def start(self, ins, outs, sems):
        pass

    def finish(self, ins, outs, sems):
        pass


class _ForwardGathered(_Exchange):
    def __init__(self, shards, own=True, forward=True):
        self.own, self.forward = own, forward
        n = self.n = len(shards)
        self.n_in, self.n_out = 2 * n, n
        self.out_shape = [jax.ShapeDtypeStruct((N_CHIP,) + s.shape, s.dtype) for s in shards]
        dma = pltpu.SemaphoreType.DMA
        self.scratch = [dma((3 * n,)), dma((3 * n,)), dma((n,)), dma((n,))]
        self.aliases = {n + a: a for a in range(n)}

    def _fwd(self, outs, sems, a, j, chip, half_of):
        x, y, c, me, chips = _place()
        half = outs[a].shape[1] // 2
        blk = outs[a].at[2 * chip[0] + chip[1], pl.ds(half_of * half, half), :]
        return _remote(blk, blk, sems[0].at[3 * a + j], sems[1].at[3 * a + j], (x, y, 1 - c))

    def _own(self, ins, outs, sems, a):
        return _own_shard_to_sibling(ins[a], outs[a], sems[2].at[a], sems[3].at[a])

    def start(self, ins, outs, sems):
        x, y, c, me, chips = _place()
        for a in range(self.n):
            for j, chip in enumerate(chips if self.forward else ()):
                self._fwd(outs, sems, a, j, chip, c).start()
        for a in range(self.n if self.own else 0):
            self._own(ins, outs, sems, a).start()

    def finish(self, ins, outs, sems):
        x, y, c, me, chips = _place()
        for a in range(self.n):
            for j, chip in enumerate(chips if self.forward else ()):
                self._fwd(outs, sems, a, j, chip, 1 - c).wait_recv()
        for a in range(self.n):
            for j, chip in enumerate(chips if self.forward else ()):
                self._fwd(outs, sems, a, j, chip, c).wait_send()
            if self.own:
                self._own(ins, outs, sems, a).wait()


HBM = pl.BlockSpec(memory_space=pltpu.HBM)
SEMS = pl.BlockSpec(memory_space=pltpu.SEMAPHORE)
DATAFLOW = pltpu.SideEffectType.DATAFLOW_SIDE_EFFECTING


class _OverIci:
    def __init__(self, name, sources, lands):
        self.name, self.n = name, len(sources)
        hbm = lambda t: pltpu.with_memory_space_constraint(t, pltpu.HBM)
        self.arrays = [hbm(t) for t in sources] + [hbm(t) for t in lands]

    def sent(self, src, land, a, chip):
        raise NotImplementedError

    def landed(self, land, a, chip):
        raise NotImplementedError

    def _copy(self, arr, sems, a, j, receiving):
        x, y, c, me, chips = _place()
        src, dst = self.sent(arr[a], arr[self.n + a], a, chips[j])
        if receiving:
            dst = self.landed(arr[self.n + a], a, chips[j])
        return _remote(src, dst, sems[0].at[3 * a + j], sems[1].at[3 * a + j], (*chips[j], c))

    def start(self, after):
        m = len(self.arrays)

        def body(*refs):
            arr, sems, token = refs[:m], refs[m + 1:m + 3], refs[-1]

            def issue(order):
                for a in range(self.n):
                    for j in order:
                        self._copy(arr, sems, a, j, False).start()

            _staggered(issue)
            token[...] = jnp.zeros_like(token)

        dma = pltpu.SemaphoreType.DMA
        outs = pl.pallas_call(
            body, name=self.name + "_start",
            out_shape=[dma((3 * self.n,)), dma((3 * self.n,))] + [pltpu.HBM(t.shape, t.dtype) for t in self.arrays]
                      + [jax.ShapeDtypeStruct((8, 128), F32)],
            in_specs=[HBM] * m + [ANY], out_specs=[SEMS, SEMS] + [HBM] * m + [VMEM],
            input_output_aliases={i: 2 + i for i in range(m)},
            compiler_params=pltpu.CompilerParams(has_side_effects=DATAFLOW),
        )(*self.arrays, after)
        self.sems, self.arrays = outs[0:2], list(outs[2:2 + m])
        return outs[-1]

    def wait(self, after):
        m = len(self.arrays)

        def body(*refs):
            arr, sems = refs[:m], refs[m:m + 2]
            for a in range(self.n):
                for j in range(3):
                    self._copy(arr, sems, a, j, False).wait_send()
                    self._copy(arr, sems, a, j, True).wait_recv()

        outs = pl.pallas_call(
            body, name=self.name + "_wait",
            out_shape=[pltpu.HBM(t.shape, t.dtype) for t in self.arrays],
            in_specs=[HBM] * m + [SEMS, SEMS, ANY], out_specs=[HBM] * m,
            input_output_aliases={i: i for i in range(m)},
            compiler_params=pltpu.CompilerParams(has_side_effects=DATAFLOW),
        )(*self.arrays, *self.sems, after)
        return list(outs[:self.n]), list(outs[self.n:])


class _GatherOverIci(_OverIci):
    def __init__(self, name, shards):
        super().__init__(name, shards, [lax.empty((N_CHIP,) + s.shape, s.dtype) for s in shards])

    @staticmethod
    def _half(ref):
        c = lax.axis_index("c")
        half = ref.shape[-2] // 2
        return pl.ds(c * half, half)

    def sent(self, src, land, a, chip):
        return src.at[self._half(src), :], land.at[_place()[3], self._half(src), :]

    def landed(self, land, a, chip):
        return land.at[2 * chip[0] + chip[1], self._half(land), :]


class _SumOverIci(_OverIci):
    def __init__(self, name, pre):
        super().__init__(name, pre, [lax.empty(p.shape, p.dtype) for p in pre])

    def sent(self, src, land, a, chip):
        return src.at[2 * chip[0] + chip[1]], land.at[_place()[3]]

    def landed(self, land, a, chip):
        return land.at[2 * chip[0] + chip[1]]


class _HalvesToSibling(_Exchange):
    def __init__(self, grads):
        n = self.n = len(grads)
        self.n_in = self.n_out = n
        self.out_shape = [jax.ShapeDtypeStruct((N_CHIP, g.shape[1] // 2, g.shape[2]), g.dtype) for g in grads]
        self.scratch = [pltpu.SemaphoreType.DMA((n,)), pltpu.SemaphoreType.DMA((n,))]

    def _copy(self, ins, outs, sems, a):
        x, y, c, me, chips = _place()
        half = ins[a].shape[1] // 2
        return _remote(ins[a].at[:, pl.ds((1 - c) * half, half), :], outs[a], sems[0].at[a], sems[1].at[a], (x, y, 1 - c))

    def start(self, ins, outs, sems):
        for a in range(self.n):
            self._copy(ins, outs, sems, a).start()

    def finish(self, ins, outs, sems):
        for a in range(self.n):
            self._copy(ins, outs, sems, a).wait_recv()
        for a in range(self.n):
            self._copy(ins, outs, sems, a).wait_send()


class _OverChips(_Exchange):
    def __init__(self, pre):
        n = self.n = len(pre)
        self.n_in = self.n_out = n
        self.out_shape = [jax.ShapeDtypeStruct(p.shape, p.dtype) for p in pre]
        dma = pltpu.SemaphoreType.DMA
        self.scratch = [dma((3 * n,)), dma((3 * n,))]

    def _ici(self, ins, outs, sems, a, j, chip):
        x, y, c, me, chips = _place()
        return _remote(ins[a].at[2 * chip[0] + chip[1]], outs[a].at[me], sems[0].at[3 * a + j], sems[1].at[3 * a + j],
                       (*chip, c))

    def start(self, ins, outs, sems):
        chips = _place()[4]

        def issue(order):
            for a in range(self.n):
                for j in order:
                    self._ici(ins, outs, sems, a, j, chips[j]).start()

        _staggered(issue)

    def finish(self, ins, outs, sems):
        x, y, c, me, chips = _place()
        for a in range(self.n):
            for j, chip in enumerate(chips):
                blk = outs[a].at[2 * chip[0] + chip[1]]
                _remote(blk, blk, sems[0].at[3 * a + j], sems[1].at[3 * a + j], (x, y, c)).wait_recv()
        for a in range(self.n):
            for j, chip in enumerate(chips):
                self._ici(ins, outs, sems, a, j, chip).wait_send()


class _ShareHalves(_Exchange):
    def __init__(self, fulls):
        n = self.n = len(fulls)
        self.n_in = self.n_out = n
        self.out_shape = [jax.ShapeDtypeStruct(f.shape, f.dtype) for f in fulls]
        self.scratch = [pltpu.SemaphoreType.DMA((n,)), pltpu.SemaphoreType.DMA((n,))]
        self.aliases = {a: a for a in range(n)}

    def _copy(self, outs, sems, a, half_of):
        x, y, c, me, chips = _place()
        half = outs[a].shape[0] // 2
        rows = outs[a].at[pl.ds(half_of * half, half), :]
        return _remote(rows, rows, sems[0].at[a], sems[1].at[a], (x, y, 1 - c))

    def start(self, ins, outs, sems):
        c = _place()[2]
        for a in range(self.n):
            self._copy(outs, sems, a, c).start()

    def finish(self, ins, outs, sems):
        c = _place()[2]
        for a in range(self.n):
            self._copy(outs, sems, a, 1 - c).wait_recv()
        for a in range(self.n):
            self._copy(outs, sems, a, c).wait_send()


class _GatherBlocks(_Exchange):
    def __init__(self, block):
        self.n_in = self.n_out = 1
        self.out_shape = [jax.ShapeDtypeStruct((8,) + block.shape, block.dtype)]
        dma = pltpu.SemaphoreType.DMA
        self.scratch = [dma((7,)), dma((7,)), dma]

    @staticmethod
    def _peer(f):
        x, y, c, me, chips = _place()
        return ((1 - x) if f & 4 else x, (1 - y) if f & 2 else y, (1 - c) if f & 1 else c)

    def start(self, ins, outs, sems):
        x, y, c, me, chips = _place()
        for f in range(1, 8):
            _remote(ins[0], outs[0].at[2 * me + c], sems[0].at[f - 1], sems[1].at[f - 1], self._peer(f)).start()
        pltpu.make_async_copy(ins[0], outs[0].at[2 * me + c], sems[2]).start()

    def finish(self, ins, outs, sems):
        x, y, c, me, chips = _place()
        for f in range(1, 8):
            px, py, pc = self._peer(f)
            blk = outs[0].at[4 * px + 2 * py + pc]
            _remote(blk, blk, sems[0].at[f - 1], sems[1].at[f - 1], (x, y, c)).wait_recv()
        for f in range(1, 8):
            _remote(ins[0], outs[0].at[2 * me + c], sems[0].at[f - 1], sems[1].at[f - 1], self._peer(f)).wait_send()
        pltpu.make_async_copy(ins[0], outs[0].at[2 * me + c], sems[2]).wait()


class _Both(_Exchange):
    def __init__(self, first, second):
        self.parts = (first, second)
        self.n_in, self.n_out = first.n_in + second.n_in, first.n_out + second.n_out
        self.out_shape = first.out_shape + second.out_shape
        self.scratch = first.scratch + second.scratch
        self.aliases = dict(first.aliases)
        self.aliases.update({first.n_in + i: first.n_out + o for i, o in second.aliases.items()})

    def _split(self, ins, outs, sems):
        a, b = self.parts
        return ((a, ins[:a.n_in], outs[:a.n_out], sems[:len(a.scratch)]),
                (b, ins[a.n_in:], outs[a.n_out:], sems[len(a.scratch):]))

    def start(self, ins, outs, sems):
        for ex, i, o, s in self._split(ins, outs, sems):
            ex.start(i, o, s)

    def middle(self, ins, outs, sems):
        for ex, i, o, s in self._split(ins, outs, sems):
            ex.middle(i, o, s)

    def finish(self, ins, outs, sems):
        for ex, i, o, s in self._split(ins, outs, sems):
            ex.finish(i, o, s)


class _Bound:
    def __init__(self, ex, ins, outs, sems):
        self.start = lambda: ex.start(ins, outs, sems)
        self.middle = lambda: ex.middle(ins, outs, sems)
        self.finish = lambda: ex.finish(ins, outs, sems)


def _carry(name, body, ex, ex_args, args, in_specs, out_specs, out_shape, scratch_shapes=(), grid=None, semantics=(),
           after=None):
    n_a, n_o, n_s = len(args), len(out_shape), len(scratch_shapes)
    behind = [] if after is None else [after]

    def full_body(*refs):
        p = 0
        groups = []
        for size in (n_a, ex.n_in, len(behind), n_o, ex.n_out, n_s, len(ex.scratch)):
            groups.append(refs[p:p + size])
            p += size
        a, ei, _, o, eo, s, es = groups
        body(*a, *o, *s, _Bound(ex, ei, eo, es))

    kwargs = {} if grid is None else {"grid": grid}
    outs = pl.pallas_call(
        full_body, name=name,
        in_specs=list(in_specs) + [ANY] * (ex.n_in + len(behind)), out_specs=list(out_specs) + [ANY] * ex.n_out,
        out_shape=list(out_shape) + list(ex.out_shape), scratch_shapes=list(scratch_shapes) + list(ex.scratch),
        input_output_aliases={n_a + i: n_o + o for i, o in ex.aliases.items()},
        compiler_params=_params(*semantics) if semantics else pltpu.CompilerParams(vmem_limit_bytes=VMEM_LIMIT),
        **kwargs,
    )(*args, *ex_args, *behind)
    return outs[:n_o], outs[n_o:]


def _exchange_alone(name, ex, ex_args):
    def body(xc):
        xc.start()
        xc.middle()
        xc.finish()

    return _carry(name, body, ex, ex_args, (), (), (), ())[1]


def _core_index():
    return lax.axis_index("c").astype(jnp.int32).reshape(1)


def _pair_sum(gs, gots):
    n = len(gs)
    _, r, cc = gs[0].shape
    half = r // 2

    def body(c_ref, *refs):
        for a in range(n):
            refs[2 * n + a][...] = (refs[a][...].astype(F32) + refs[n + a][...].astype(F32)).astype(BF16)

    mine = pl.BlockSpec((None, half, cc), lambda k, c_ref: (k, c_ref[0], 0))
    blk = pl.BlockSpec((None, half, cc), lambda k, c_ref: (k, 0, 0))
    return pl.pallas_call(
        body, name=f"pair_sum_{r}x{cc}",
        grid_spec=pltpu.PrefetchScalarGridSpec(
            num_scalar_prefetch=1, grid=(N_CHIP,), in_specs=[mine] * n + [blk] * n, out_specs=[blk] * n),
        out_shape=[jax.ShapeDtypeStruct((N_CHIP, half, cc), BF16)] * n,
        compiler_params=_params("parallel"),
    )(_core_index(), *gs, *gots)


def _chip_sum(pre, parts):
    n = len(parts)
    _, half, cc = parts[0].shape
    tr = half // 2
    me = 2 * lax.axis_index("x") + lax.axis_index("y")
    others = [k + (k >= me).astype(jnp.int32) for k in range(3)]
    where = jnp.stack([lax.axis_index("c"), me, *others]).astype(jnp.int32)

    def body(w_ref, *refs):
        for a in range(n):
            own, p1, p2, p3 = refs[4 * a:4 * a + 4]
            refs[4 * n + a][...] = ((own[...].astype(F32) + p1[...].astype(F32)) + p2[...].astype(F32)) + p3[...].astype(F32)

    slot = lambda s: pl.BlockSpec((None, tr, cc), lambda i, w_ref: (w_ref[s], i, 0))
    operands = []
    for a in range(n):
        operands += [pre[a], parts[a], parts[a], parts[a]]
    return pl.pallas_call(
        body, name=f"chip_sum_{half}x{cc}",
        grid_spec=pltpu.PrefetchScalarGridSpec(
            num_scalar_prefetch=1, grid=(2,),
            in_specs=[slot(1), slot(2), slot(3), slot(4)] * n,
            out_specs=[pl.BlockSpec((tr, cc), lambda i, w_ref: (2 * w_ref[0] + i, 0))] * n),
        out_shape=[jax.ShapeDtypeStruct((2 * half, cc), F32)] * n,
        compiler_params=_params("parallel"),
    )(where, *operands)


def _adamw_math(w, g, m, v):
    m = ADAM_B1 * m + (1.0 - ADAM_B1) * g
    v = ADAM_B2 * v + (1.0 - ADAM_B2) * (g * g)
    m_hat = m / (1.0 - ADAM_B1 ** ADAM_STEP)
    v_hat = v / (1.0 - ADAM_B2 ** ADAM_STEP)
    delta = -ADAM_LR * (m_hat / (jnp.sqrt(v_hat) + ADAM_EPS) + ADAM_WD * w)
    return delta, m, v


def _adamw(w, g, m, v, after=None):
    r, cc = w.shape
    tr = r // 4

    def body(w_ref, g_ref, m_ref, v_ref, go_ref, d_ref, nm_ref, nv_ref, _):
        g = g_ref[...]
        go_ref[...] = g
        d_ref[...], nm_ref[...], nv_ref[...] = _adamw_math(w_ref[...], g, m_ref[...], v_ref[...])

    blk = pl.BlockSpec((tr, cc), lambda i: (i, 0))
    return _carry(f"adamw_{r}x{cc}", body, _NoExchange(), (), (w, g, m, v), [blk] * 4, [blk] * 4,
                  [jax.ShapeDtypeStruct((r, cc), F32)] * 4, grid=(4,), semantics=("parallel",), after=after)[0]


def _pack8(rows):
    def body(*refs):
        out_ref = refs[-1]
        out_ref[...] = jnp.zeros_like(out_ref)
        for i, r in enumerate(refs[:-1]):
            out_ref[i:i + 1, :] = r[...]

    return pl.pallas_call(body, name="pack8", out_shape=jax.ShapeDtypeStruct((8, D), F32))(*rows)


def _adamw_gains(gall, w8, m8, v8):
    def body(ga_ref, w_ref, m_ref, v_ref, g_ref, d_ref, nm_ref, nv_ref):
        g = ga_ref[0]
        for dev in range(1, 8):
            g = g + ga_ref[dev]
        g_ref[...] = g
        d_ref[...], nm_ref[...], nv_ref[...] = _adamw_math(w_ref[...], g, m_ref[...], v_ref[...])

    return pl.pallas_call(
        body, name="adamw_gains",
        out_shape=[jax.ShapeDtypeStruct((8, D), F32)] * 4,
    )(gall, w8, m8, v8)


def kernel(x, positions, w_in, w_out, g_pre_mix, g_post_mix, g_pre_ffn, g_post_ffn, w_gate, w_up, w_down, loss_target, m_w_in, m_w_out, m_g_pre_mix, m_g_post_mix, m_g_pre_ffn, m_g_post_ffn, m_w_gate, m_w_up, m_w_down, v_w_in, v_w_out, v_g_pre_mix, v_g_post_mix, v_g_pre_ffn, v_g_post_ffn, v_w_gate, v_w_up, v_w_down):
    tr = lambda t: jnp.swapaxes(t, 1, 2)[0]
    shards = [w_in[0], w_out[0], tr(w_gate), tr(w_up), w_down[0]]
    moms = [m_w_in[0], m_w_out[0], tr(m_w_gate), tr(m_w_up), m_w_down[0]]
    vels = [v_w_in[0], v_w_out[0], tr(v_w_gate), tr(v_w_up), v_w_down[0]]
    xs, pos, tgt = x[0], positions.reshape(S, 1), loss_target[0]
    g1, g2, g3, g4 = g_pre_mix, g_post_mix, g_pre_ffn, g_post_ffn
    tabs = tuple(jnp.asarray(t) for t in _retention_tables())
    ifc, spread = _rotary_tables()
    ifc, spread = jnp.asarray(ifc), jnp.asarray(spread, dtype=BF16)
    bf = [s.astype(BF16) for s in shards]

    win_g, wout_g = _exchange_alone("gather_in", _GatherShards(bf[:2]), bf[:2])
    wout_g = wout_g.reshape(D, D)
    ffn_gather = _GatherOverIci("ffn_gather", bf[2:])
    token = ffn_gather.start(win_g)
    proj, h1 = _proj_fwd(xs, g1, win_g, token)
    qr, kr, rv, aq, ak, av, cos, sin = _rot_fwd(proj, pos, ifc, spread)
    (o_raw, cat_r, states), _ = _ret_fwd(qr, kr, rv, proj, tabs, _NoExchange(), ())
    n_ffn = len(bf[2:])
    (att_out, lse, cat_a), ffn_gather.arrays[n_ffn:] = _att_fwd(
        aq, ak, av, _ForwardGathered(bf[2:], forward=False), ffn_gather.arrays)
    ffn_sh, ffn_lands = ffn_gather.wait(cat_a)
    (mix, x2, h3), (wg_g, wu_g, wd_g) = _mix_fwd(cat_r, cat_a, wout_g, xs, g2, g3,
                                                _ForwardGathered(bf[2:], own=False), [*ffn_sh, *ffn_lands])
    gt, up, a, f = _ffn_fwd(h3, wg_g, wu_g, wd_g)

    sq, dy, df, dg4 = _head_bwd(f, x2, tgt, g4)
    loss = 0.5 * lax.psum(sq[0, 0], ("x", "y", "c")) / D
    dgt, dup, dh3 = _ffn_bwd_act(df, gt, up, wg_g, wu_g, wd_g)
    ffn_grads = list(_ffn_bwd_w(a, df, h3, dgt, dup))
    (dx2, dmix, dg3, dg2), got = _norm_bwd(dh3, dy, x2, mix, g2, g3, _HalvesToSibling(ffn_grads), ffn_grads)
    ffn_sum = _SumOverIci("ffn_sum", _pair_sum(ffn_grads, got))
    token = ffn_sum.start(dmix)
    dret, datt, dwout = _mix_bwd(dmix, cat_r, cat_a, wout_g, token)
    (dq_att, dk_att, dv_att), _ = _att_bwd(aq, ak, av, datt, att_out, lse, _NoExchange(), ())
    (dqr, dkr, drv, drg), _ = _ret_bwd(qr, kr, rv, proj, o_raw, states, dret, tabs, _NoExchange(), ())
    sums = _chip_sum(*ffn_sum.wait(dqr))
    dproj = _rot_bwd(cos, sin, dqr, dkr, drv, drg, dq_att, dk_att, dv_att)
    dwin, ffn_full = _win_bwd_w(h1, dproj, _ShareHalves(sums), sums)
    in_grads = [dwin, dwout.reshape(N_CHIP, WOUT_R, D)]

    got = _exchange_alone("halves_to_sibling", _HalvesToSibling(in_grads), in_grads)
    in_sum = _SumOverIci("in_sum", [*_pair_sum(in_grads[:1], got[:1]), *_pair_sum(in_grads[1:], got[1:])])
    token = in_sum.start(dproj)
    dx, dg1 = _in_bwd(dproj, win_g, xs, dx2, g1, token)
    gblock = _pack8([dg1, dg2, dg3, dg4])
    (gall,) = _exchange_alone("gather_gains", _GatherBlocks(gblock), [gblock])
    ffn_upd = [_adamw(shards[2 + i], ffn_full[o], moms[2 + i], vels[2 + i], token)
               for i, o in enumerate((1, 2, 0))]
    pre, parts = in_sum.wait(ffn_upd[2][0])
    sums = [*_chip_sum(pre[:1], parts[:1]), *_chip_sum(pre[1:], parts[1:])]
    in_full = _exchange_alone("share_rest", _ShareHalves(sums), sums)
    upd = [_adamw(w, g, m, v) for w, g, m, v in zip(shards[:2], in_full, moms[:2], vels[:2])] + ffn_upd
    gg, gd, gm, gv = _adamw_gains(gall, _pack8([g1, g2, g3, g4]),
                                  _pack8([m_g_pre_mix, m_g_post_mix, m_g_pre_ffn, m_g_post_ffn]),
                                  _pack8([v_g_pre_mix, v_g_post_mix, v_g_pre_ffn, v_g_post_ffn]))

    def order(mats, vecs):
        back = lambda t: jnp.swapaxes(t[None], 1, 2)
        return ([mats[0][None], mats[1][None]] + [vecs[i:i + 1] for i in range(4)]
                + [back(mats[2]), back(mats[3]), mats[4][None]])

    return (loss, dx[None],
            *order([u[0] for u in upd], gg),
            *order([u[1] for u in upd], gd),
            *order([u[2] for u in upd], gm),
            *order([u[3] for u in upd], gv))
```

```python
import functools

import numpy as np
import jax
import jax.numpy as jnp
from jax import lax
from jax.experimental import pallas as pl
from jax.experimental.pallas import tpu as pltpu

F32, BF16 = jnp.float32, jnp.bfloat16
MESH = pl.DeviceIdType.MESH

S = 2048
D = 1024
PW = 3072
N_CHIP = 4
WIN_C = PW // N_CHIP
DFF = 2816
FF_C = DFF // N_CHIP
WOUT_R = D // N_CHIP
RMS_EPS = 1e-6
GN_EPS = 1e-5
RET_C = 128
RET_SCALE = 32 ** -0.5
ATT_BLK = 128
ATT_SCALE = 64 ** -0.5
PATTERN_DILATIONS = (1, 4, 16)
NEG = -1e30
VMEM_LIMIT = 56 * 1024 * 1024

ADAM_LR, ADAM_B1, ADAM_B2, ADAM_EPS, ADAM_WD, ADAM_STEP = 0.001, 0.9, 0.999, 1e-08, 0.01, 10


def _params(*sem):
    return pltpu.CompilerParams(dimension_semantics=sem, vmem_limit_bytes=VMEM_LIMIT)


def _nt(a, b):
    return lax.dot_general(a, b, (((1,), (1,)), ((), ())), preferred_element_type=F32)


def _tn(a, b):
    return lax.dot_general(a, b, (((0,), (0,)), ((), ())), preferred_element_type=F32)


def _nn(a, b):
    return jnp.dot(a, b, preferred_element_type=F32)


def _rstd(v):
    return lax.rsqrt(jnp.mean(v * v, axis=-1, keepdims=True) + RMS_EPS)


def _sigmoid(v):
    return 1.0 / (1.0 + jnp.exp(-v))


def _rows(i, t):
    return pl.ds(pl.multiple_of(i * t, t), t)


def _retention_tables():
    h = np.arange(8, dtype=np.float32)
    log_g = np.log1p(-np.exp2(-5.0 - h)).astype(np.float32)
    idx = np.arange(RET_C, dtype=np.float32)
    diff = idx[:, None] - idx[None, :]
    dtab = np.where(diff >= 0, np.exp(log_g[:, None, None] * np.maximum(diff, 0.0)), 0.0).astype(np.float32)
    dtab = dtab.reshape(8 * RET_C, RET_C)
    lane_head = np.arange(256) // 32
    a_tab = np.exp(log_g[lane_head][None, :] * (idx + 1.0)[:, None]).astype(np.float32)
    b_tab = np.exp(log_g[lane_head][None, :] * (RET_C - 1.0 - idx)[:, None]).astype(np.float32)
    lam = np.exp(log_g[lane_head] * RET_C).astype(np.float32)[:, None]
    bd = (lane_head[:, None] == (np.arange(512) // 64)[None, :]).astype(np.float32)
    return dtab, a_tab, b_tab, lam, bd


def _rotary_tables():
    inv_r = (1.0 / (np.float32(10000.0) ** np.linspace(0.0, 1.0, 16, dtype=np.float32))).astype(np.float32)
    inv_a = (np.float32(500000.0) ** (-np.arange(0, 16, 2, dtype=np.float32) / np.float32(16))).astype(np.float32)
    ifc = np.zeros((1, 128), np.float32)
    ifc[0, 0:16], ifc[0, 16:24] = inv_r, inv_a
    spread = np.zeros((128, 768), np.float32)
    for lane in range(256):
        spread[(lane % 32) % 16, lane] = 1.0
    for lane in range(512):
        d = lane % 64
        spread[16 + d % 8 if d < 16 else 24, 256 + lane] = 1.0
    return ifc, spread


def _proj_fwd(x, g1, win_g, after):
    tm = 512

    def body(x_ref, g_ref, w_ref, proj_ref, h_ref, _):
        xv = x_ref[...]
        h = (xv * _rstd(xv) * g_ref[...]).astype(BF16)
        h_ref[...] = h
        for k in range(N_CHIP):
            proj_ref[:, k * WIN_C:(k + 1) * WIN_C] = _nn(h, w_ref[k])

    return _carry(
        "proj_fwd", body, _NoExchange(), (), (x, g1, win_g),
        [pl.BlockSpec((tm, D), lambda i: (i, 0)), pl.BlockSpec((1, D), lambda i: (0, 0)),
         pl.BlockSpec((N_CHIP, D, WIN_C), lambda i: (0, 0, 0))],
        [pl.BlockSpec((tm, PW), lambda i: (i, 0)), pl.BlockSpec((tm, D), lambda i: (i, 0))],
        [jax.ShapeDtypeStruct((S, PW), F32), jax.ShapeDtypeStruct((S, D), BF16)],
        grid=(S // tm,), semantics=("parallel",), after=after)[0]


def _rot_halves(tm):
    lo_r = (lax.broadcasted_iota(jnp.int32, (tm, 256), 1) % 32) < 16
    lo_a = (lax.broadcasted_iota(jnp.int32, (tm, 512), 1) % 64) < 8
    return lo_r, lo_a


def _spread_exact(t, e):
    hi = t.astype(BF16)
    r1 = t - hi.astype(F32)
    mid = r1.astype(BF16)
    lo = (r1 - mid.astype(F32)).astype(BF16)
    return _nn(hi, e) + _nn(mid, e) + _nn(lo, e)


def _rot_fwd(proj, pos, ifc, spread):
    tm = 256

    def body(p_ref, pos_ref, ifc_ref, e_ref, qr_ref, kr_ref, rv_ref, aq_ref, ak_ref, av_ref, cos_ref, sin_ref):
        ang = pos_ref[...].astype(F32) * ifc_ref[...]
        cs = _spread_exact(jnp.cos(ang), e_ref[...])
        sn = _spread_exact(jnp.sin(ang), e_ref[...])
        cos_ref[...] = cs
        sin_ref[...] = sn
        cr, ca, sr, sa = cs[:, 0:256], cs[:, 256:768], sn[:, 0:256], sn[:, 256:768]
        lo_r, lo_a = _rot_halves(tm)

        def rot_r(v):
            return v * cr + sr * jnp.where(lo_r, -pltpu.roll(v, 240, 1), pltpu.roll(v, 16, 1))

        def rot_a(v):
            return v * ca + sa * jnp.where(lo_a, -pltpu.roll(v, 504, 1), pltpu.roll(v, 8, 1))

        qr_ref[...] = rot_r(p_ref[:, 0:256]).astype(BF16)
        kr_ref[...] = (rot_r(p_ref[:, 256:512]) * RET_SCALE).astype(BF16)
        rv_ref[...] = p_ref[:, 512:1024].astype(BF16)
        aq, ak = rot_a(p_ref[:, 1536:2048]), rot_a(p_ref[:, 2048:2560])
        for j in range(4):
            aq_ref[j] = aq[:, 128 * j:128 * j + 128]
            ak_ref[j] = ak[:, 128 * j:128 * j + 128]
            av_ref[j] = p_ref[:, 2560 + 128 * j:2560 + 128 * j + 128]

    row = lambda w: pl.BlockSpec((tm, w), lambda i: (i, 0))
    const = lambda w: pl.BlockSpec((1, w), lambda i: (0, 0))
    slab = pl.BlockSpec((4, tm, 128), lambda i: (0, i, 0))
    return pl.pallas_call(
        body, grid=(S // tm,), name="rot_fwd",
        in_specs=[row(PW), row(1), const(128), pl.BlockSpec((128, 768), lambda i: (0, 0))],
        out_specs=[row(256), row(256), row(512), slab, slab, slab, row(768), row(768)],
        out_shape=[jax.ShapeDtypeStruct((S, w), BF16) for w in (256, 256, 512)]
                  + [jax.ShapeDtypeStruct((4, S, 128), F32)] * 3 + [jax.ShapeDtypeStruct((S, 768), F32)] * 2,
        compiler_params=_params("parallel"),
    )(proj, pos, ifc, spread)


def _seg_mean(v):
    lo = lax.broadcasted_iota(jnp.int32, v.shape, 1) < 64
    s_lo = jnp.sum(jnp.where(lo, v, 0.0), axis=-1, keepdims=True)
    s_hi = jnp.sum(jnp.where(lo, 0.0, v), axis=-1, keepdims=True)
    return jnp.where(lo, s_lo, s_hi) * (1.0 / 64.0)


def _ret_fwd(qr, kr, rv, proj, tabs, exchange, exchange_args):
    C = RET_C
    dtab, a_tab, b_tab, lam, bd = tabs

    def body(q_ref, k_ref, v_ref, g_ref, dt_ref, a_ref, b_ref, lam_ref, bd_ref, o_ref, cat_ref, st_ref, R, exch):
        @pl.when(pl.program_id(0) == 0)
        def _():
            exch.start()
            R[...] = jnp.zeros_like(R)

        @pl.when(pl.program_id(0) == S // C // 2)
        def _():
            exch.middle()

        q, k, v = q_ref[...], k_ref[...], v_ref[...]
        lane_head = lax.broadcasted_iota(jnp.int32, (C, 256), 1) // 32
        col_head = lax.broadcasted_iota(jnp.int32, (C, 256), 1) // 64
        rb = R[...].astype(BF16)
        st_ref[...] = rb
        qa = (q.astype(F32) * a_ref[...]).astype(BF16)
        cross = _nn(qa, rb)
        p = (_nt(_stack_heads(q, lane_head, n=8), k) * dt_ref[...]).astype(BF16)
        og = [cross[:, 256 * g:256 * g + 256]
              + _unstack_heads(_nn(p[4 * C * g:4 * C * (g + 1)], v[:, 256 * g:256 * g + 256]), col_head)
              for g in range(2)]
        kb = (k.astype(F32) * b_ref[...]).astype(BF16)
        R[...] = R[...] * lam_ref[...] + _tn(kb, v) * bd_ref[...]
        o_ref[:, 0:256] = og[0]
        o_ref[:, 256:512] = og[1]
        for j in range(4):
            oj = og[j // 2][:, 128 * (j % 2):128 * (j % 2) + 128]
            xc = oj - _seg_mean(oj)
            rn = xc * lax.rsqrt(_seg_mean(xc * xc) + GN_EPS)
            gj = g_ref[:, 128 * j:128 * j + 128]
            cat_ref[:, 128 * j:128 * j + 128] = (rn * (gj * _sigmoid(gj))).astype(BF16)

        @pl.when(pl.program_id(0) == S // C - 1)
        def _():
            exch.finish()

    row = lambda w: pl.BlockSpec((C, w), lambda n: (n, 0))
    full = lambda a: pl.BlockSpec(a.shape, lambda n: (0,) * a.ndim)
    return _carry(
        "ret_fwd", body, exchange, exchange_args, (qr, kr, rv, proj, dtab, a_tab, b_tab, lam, bd),
        [row(256), row(256), row(512), pl.BlockSpec((C, 512), lambda n: (n, 2)),
         full(dtab), full(a_tab), full(b_tab), full(lam), full(bd)],
        [row(512), row(512), pl.BlockSpec((None, 256, 512), lambda n: (n, 0, 0))],
        [jax.ShapeDtypeStruct((S, 512), F32), jax.ShapeDtypeStruct((S, 512), BF16),
         jax.ShapeDtypeStruct((S // C, 256, 512), BF16)],
        scratch_shapes=[pltpu.VMEM((256, 512), F32)], grid=(S // C,), semantics=("arbitrary",))


def _stack_heads(v, lane_head, fill=0.0, n=4):
    return jnp.concatenate([jnp.where(lane_head == h, v, jnp.full_like(v, fill)) for h in range(n)], axis=0)


def _unstack_heads(v, lane_head, n=4):
    out = v[0:ATT_BLK]
    for h in range(1, n):
        out = jnp.where(lane_head == h, v[h * ATT_BLK:(h + 1) * ATT_BLK], out)
    return out


def _att_bias(has_prev):
    nk = 2 * ATT_BLK if has_prev else ATT_BLK
    a = lax.broadcasted_iota(jnp.int32, (4 * ATT_BLK, nk), 0) % ATT_BLK
    kk = lax.broadcasted_iota(jnp.int32, (4 * ATT_BLK, nk), 1)
    if not has_prev:
        return None, jnp.where((a - kk) >= 0, 0.0, NEG)
    dist = ATT_BLK + a - kk
    inside = (dist >= 0) & (dist <= ATT_BLK)
    return jnp.where(inside, 0.0, NEG), jnp.where(inside & (kk >= ATT_BLK), 0.0, NEG)


def _class_rows(ib, r, d):
    if d == 1:
        return pl.ds(pl.multiple_of(ib * ATT_BLK, ATT_BLK), ATT_BLK)
    return pl.ds(ib * ATT_BLK * d + r, ATT_BLK, stride=d)


def _slab_pair(ref, g, rows):
    return jnp.concatenate([ref[2 * g, rows, :], ref[2 * g + 1, rows, :]], axis=1)


def _att_blocks(d):
    nb = S // d // ATT_BLK
    return nb, nb > 1


def _att_fwd(aq, ak, av, exchange, exchange_args):
    def body(q_ref, k_ref, v_ref, o_ref, l_ref, cat_ref, xc):
        xc.start()
        lane_head = lax.broadcasted_iota(jnp.int32, (ATT_BLK, 256), 1) // 64
        for pi, d in enumerate(PATTERN_DILATIONS):
            if pi == len(PATTERN_DILATIONS) - 1:
                xc.middle()
            nb, has_prev = _att_blocks(d)
            bias_rest, bias_first = _att_bias(has_prev)

            def block(b, carry, pi=pi, d=d, nb=nb, has_prev=has_prev, bias_rest=bias_rest, bias_first=bias_first):
                r, ib = b // nb, b % nb
                rows = _class_rows(ib, r, d)
                prow = _class_rows(jnp.maximum(ib - 1, 0), r, d)
                bias = jnp.where(ib == 0, bias_first, bias_rest) if has_prev else bias_first
                for g in range(2):
                    qg = _slab_pair(q_ref, g, rows).astype(BF16)
                    kg = _slab_pair(k_ref, g, rows)
                    vg = _slab_pair(v_ref, g, rows)
                    if has_prev:
                        kg = jnp.concatenate([_slab_pair(k_ref, g, prow), kg], axis=0)
                        vg = jnp.concatenate([_slab_pair(v_ref, g, prow), vg], axis=0)
                    kg, vg = kg.astype(BF16), vg.astype(BF16)
                    s = _nt(_stack_heads(qg, lane_head), kg) * ATT_SCALE + bias
                    m = jnp.max(s, axis=-1, keepdims=True)
                    p = jnp.exp(s - m)
                    den = jnp.sum(p, axis=-1, keepdims=True)
                    og = _unstack_heads(_nn(p.astype(BF16), vg) / den, lane_head)
                    lg = _unstack_heads(jnp.broadcast_to(m + jnp.log(den), (4 * ATT_BLK, 256)), lane_head)
                    for jj in range(2):
                        j = 2 * g + jj
                        o_new, l_new = og[:, 128 * jj:128 * jj + 128], lg[:, 128 * jj:128 * jj + 128]
                        if pi > 0:
                            o_old, l_old = o_ref[j, rows, :], l_ref[j, rows, :]
                            mx = jnp.maximum(l_old, l_new)
                            ea, eb = jnp.exp(l_old - mx), jnp.exp(l_new - mx)
                            den = ea + eb
                            o_new = (ea * o_old + eb * o_new) / den
                            l_new = mx + jnp.log(den)
                        o_ref[j, rows, :] = o_new
                        l_ref[j, rows, :] = l_new
                return carry

            lax.fori_loop(0, S // ATT_BLK, block, 0)

        def to_cat(i, carry):
            rows = _rows(i, 256)
            for j in range(4):
                cat_ref[rows, 128 * j:128 * j + 128] = o_ref[j, rows, :].astype(BF16)
            return carry

        lax.fori_loop(0, S // 256, to_cat, 0)
        xc.finish()

    slab = jax.ShapeDtypeStruct((4, S, 128), F32)
    return _carry("att_fwd", body, exchange, exchange_args, (aq, ak, av), [VMEM] * 3, [VMEM] * 3,
                  [slab, slab, jax.ShapeDtypeStruct((S, 512), BF16)])


def _mix_fwd(cat_r, cat_a, wout, x, g2, g3, exchange, exchange_args):
    tm = 512

    def body(cr_ref, ca_ref, w_ref, x_ref, g2_ref, g3_ref, mix_ref, x2_ref, h3_ref, xc):
        @pl.when(pl.program_id(0) == 0)
        def _():
            xc.start()

        mix = _nn(cr_ref[...], w_ref[0:512, :]) + _nn(ca_ref[...], w_ref[512:1024, :])
        mix_ref[...] = mix
        x2 = x_ref[...] + mix * _rstd(mix) * g2_ref[...]
        x2_ref[...] = x2
        h3_ref[...] = (x2 * _rstd(x2) * g3_ref[...]).astype(BF16)

        @pl.when(pl.program_id(0) == S // tm - 1)
        def _():
            xc.middle()
            xc.finish()

    row = lambda w: pl.BlockSpec((tm, w), lambda i: (i, 0))
    vec = pl.BlockSpec((1, D), lambda i: (0, 0))
    return _carry("mix_fwd", body, exchange, exchange_args, (cat_r, cat_a, wout, x, g2, g3),
                  [row(512), row(512), pl.BlockSpec((D, D), lambda i: (0, 0)), row(D), vec, vec],
                  [row(D), row(D), row(D)],
                  [jax.ShapeDtypeStruct((S, D), F32), jax.ShapeDtypeStruct((S, D), F32),
                   jax.ShapeDtypeStruct((S, D), BF16)],
                  grid=(S // tm,), semantics=("arbitrary",))


def _ffn_fwd(h3, wg, wu, wd):
    tm = 512

    def body(h_ref, wg_ref, wu_ref, wd_ref, gt_ref, up_ref, a_ref, f_ref):
        k, i = pl.program_id(0), pl.program_id(1)
        h = h_ref[...]
        gt = _nt(h, wg_ref[...])
        up = _nt(h, wu_ref[...])
        gt_ref[...] = gt.astype(BF16)
        up_ref[...] = up.astype(BF16)
        a = (gt * _sigmoid(gt) * up).astype(BF16)
        a_ref[...] = a
        part = _nn(a, wd_ref[...])
        rows = _rows(i, tm)

        @pl.when(k == 0)
        def _():
            f_ref[rows, :] = part

        @pl.when(k > 0)
        def _():
            f_ref[rows, :] = f_ref[rows, :] + part

    wrow = pl.BlockSpec((None, FF_C, D), lambda k, i: (k, 0, 0))
    act = pl.BlockSpec((None, tm, FF_C), lambda k, i: (k, i, 0))
    return pl.pallas_call(
        body, grid=(N_CHIP, S // tm), name="ffn_fwd",
        in_specs=[pl.BlockSpec((tm, D), lambda k, i: (i, 0)), wrow, wrow, wrow],
        out_specs=[act, act, act, pl.BlockSpec((S, D), lambda k, i: (0, 0))],
        out_shape=[jax.ShapeDtypeStruct((N_CHIP, S, FF_C), BF16)] * 3 + [jax.ShapeDtypeStruct((S, D), F32)],
        compiler_params=_params("arbitrary", "arbitrary"),
    )(h3, wg, wu, wd)


def _head_bwd(f, x2, tgt, g4):
    tm = 256

    def body(f_ref, x2_ref, t_ref, g_ref, loss_ref, dy_ref, df_ref, dg_ref):
        @pl.when(pl.program_id(0) == 0)
        def _():
            loss_ref[...] = jnp.zeros_like(loss_ref)
            dg_ref[...] = jnp.zeros_like(dg_ref)

        fv = f_ref[...]
        r = _rstd(fv)
        fn = fv * r
        e = x2_ref[...] + fn * g_ref[...] - t_ref[...]
        sq = jnp.sum(jnp.sum(e * e, axis=-1, keepdims=True), axis=0, keepdims=True)
        loss_ref[...] = loss_ref[...] + sq
        dy = e * (1.0 / D)
        dy_ref[...] = dy
        dg_ref[...] = dg_ref[...] + jnp.sum(dy * fn, axis=0, keepdims=True)
        t = dy * g_ref[...]
        df_ref[...] = (r * (t - fn * jnp.mean(t * fn, axis=-1, keepdims=True))).astype(BF16)

    row = pl.BlockSpec((tm, D), lambda i: (i, 0))
    vec = pl.BlockSpec((1, D), lambda i: (0, 0))
    return pl.pallas_call(
        body, grid=(S // tm,), name="head_bwd",
        in_specs=[row, row, row, vec],
        out_specs=[pl.BlockSpec((8, 128), lambda i: (0, 0)), row, row, vec],
        out_shape=[jax.ShapeDtypeStruct((8, 128), F32), jax.ShapeDtypeStruct((S, D), F32),
                   jax.ShapeDtypeStruct((S, D), BF16), jax.ShapeDtypeStruct((1, D), F32)],
        compiler_params=_params("arbitrary"),
    )(f, x2, tgt, g4)


def _ffn_bwd_act(df, gt, up, wg, wu, wd):
    tm, sub = 512, 128

    def body(df_ref, gt_ref, up_ref, wg_ref, wu_ref, wd_ref, dgt_ref, dup_ref, dh_ref):
        k = pl.program_id(1)
        parts = []
        for s in range(tm // sub):
            rows = slice(s * sub, (s + 1) * sub)
            da = _nt(df_ref[rows, :], wd_ref[...])
            gt, up = gt_ref[rows, :].astype(F32), up_ref[rows, :].astype(F32)
            sg = _sigmoid(gt)
            dup = (da * gt * sg).astype(BF16)
            dgt = (da * up * (sg * (1.0 + gt * (1.0 - sg)))).astype(BF16)
            dup_ref[rows, :] = dup
            dgt_ref[rows, :] = dgt
            parts.append(_nn(dgt, wg_ref[...]) + _nn(dup, wu_ref[...]))
        part = jnp.concatenate(parts, axis=0)

        @pl.when(k == 0)
        def _():
            dh_ref[...] = part

        @pl.when(k > 0)
        def _():
            dh_ref[...] = dh_ref[...] + part

    wrow = pl.BlockSpec((None, FF_C, D), lambda i, k: (k, 0, 0))
    act = pl.BlockSpec((None, tm, FF_C), lambda i, k: (k, i, 0))
    row = pl.BlockSpec((tm, D), lambda i, k: (i, 0))
    return pl.pallas_call(
        body, grid=(S // tm, N_CHIP), name="ffn_bwd_act",
        in_specs=[row, act, act, wrow, wrow, wrow],
        out_specs=[act, act, row],
        out_shape=[jax.ShapeDtypeStruct((N_CHIP, S, FF_C), BF16), jax.ShapeDtypeStruct((N_CHIP, S, FF_C), BF16),
                   jax.ShapeDtypeStruct((S, D), F32)],
        compiler_params=_params("parallel", "arbitrary"),
    )(df, gt, up, wg, wu, wd)


def _ffn_bwd_w(a, df, h3, dgt, dup):
    tm = 1024
    assert S // tm == 2

    def body(a_ref, df_ref, h_ref, dgt_ref, dup_ref, dwd_ref, dwg_ref, dwu_ref, acc_d, acc_g, acc_u):
        i = pl.program_id(1)
        h = h_ref[...]
        parts = (_tn(a_ref[...], df_ref[...]), _tn(dgt_ref[...], h), _tn(dup_ref[...], h))

        @pl.when(i == 0)
        def _():
            for acc, part in zip((acc_d, acc_g, acc_u), parts):
                acc[...] = part

        @pl.when(i == S // tm - 1)
        def _():
            for out, acc, part in zip((dwd_ref, dwg_ref, dwu_ref), (acc_d, acc_g, acc_u), parts):
                out[...] = (acc[...] + part).astype(BF16)

    act = pl.BlockSpec((None, tm, FF_C), lambda k, i: (k, i, 0))
    row = pl.BlockSpec((tm, D), lambda k, i: (i, 0))
    wrow = pl.BlockSpec((None, FF_C, D), lambda k, i: (k, 0, 0))
    return pl.pallas_call(
        body, grid=(N_CHIP, S // tm), name="ffn_bwd_w",
        in_specs=[act, row, row, act, act],
        out_specs=[wrow, wrow, wrow],
        out_shape=[jax.ShapeDtypeStruct((N_CHIP, FF_C, D), BF16)] * 3,
        scratch_shapes=[pltpu.VMEM((FF_C, D), F32)] * 3,
        compiler_params=_params("parallel", "arbitrary"),
    )(a, df, h3, dgt, dup)


def _norm_bwd(dh3, dy, x2, mix, g2, g3, exchange, exchange_args):
    tm = 256

    def body(dh_ref, dy_ref, x2_ref, mix_ref, g2_ref, g3_ref, dx2_ref, dmix_ref, dg3_ref, dg2_ref, xc):
        @pl.when(pl.program_id(0) == 0)
        def _():
            xc.start()
            dg3_ref[...] = jnp.zeros_like(dg3_ref)
            dg2_ref[...] = jnp.zeros_like(dg2_ref)

        x2 = x2_ref[...]
        r3 = _rstd(x2)
        xn = x2 * r3
        dh = dh_ref[...]
        dg3_ref[...] = dg3_ref[...] + jnp.sum(dh * xn, axis=0, keepdims=True)
        t = dh * g3_ref[...]
        dx2 = dy_ref[...] + r3 * (t - xn * jnp.mean(t * xn, axis=-1, keepdims=True))
        dx2_ref[...] = dx2
        mix = mix_ref[...]
        r2 = _rstd(mix)
        mn = mix * r2
        dg2_ref[...] = dg2_ref[...] + jnp.sum(dx2 * mn, axis=0, keepdims=True)
        u = dx2 * g2_ref[...]
        dmix_ref[...] = (r2 * (u - mn * jnp.mean(u * mn, axis=-1, keepdims=True))).astype(BF16)

        @pl.when(pl.program_id(0) == S // tm - 1)
        def _():
            xc.middle()
            xc.finish()

    row = pl.BlockSpec((tm, D), lambda i: (i, 0))
    vec = pl.BlockSpec((1, D), lambda i: (0, 0))
    return _carry("norm_bwd", body, exchange, exchange_args, (dh3, dy, x2, mix, g2, g3),
                  [row, row, row, row, vec, vec], [row, row, vec, vec],
                  [jax.ShapeDtypeStruct((S, D), F32), jax.ShapeDtypeStruct((S, D), BF16),
                   jax.ShapeDtypeStruct((1, D), F32), jax.ShapeDtypeStruct((1, D), F32)],
                  grid=(S // tm,), semantics=("arbitrary",))


def _mix_bwd(dmix, cat_r, cat_a, wout, after):
    tm = 512

    def body(dm_ref, cr_ref, ca_ref, w_ref, dret_ref, datt_ref, dw_ref, acc, _):
        i = pl.program_id(0)

        @pl.when(i == 0)
        def _():
            acc[...] = jnp.zeros_like(acc)

        dm = dm_ref[...]
        dret_ref[...] = _nt(dm, w_ref[0:512, :])
        datt = _nt(dm, w_ref[512:1024, :])
        for j in range(4):
            datt_ref[j] = datt[:, 128 * j:128 * j + 128]
        acc[0:512, :] += _tn(cr_ref[...], dm)
        acc[512:1024, :] += _tn(ca_ref[...], dm)

        @pl.when(i == S // tm - 1)
        def _():
            dw_ref[...] = acc[...].astype(BF16)

    row = lambda w: pl.BlockSpec((tm, w), lambda i: (i, 0))
    full = pl.BlockSpec((D, D), lambda i: (0, 0))
    return _carry("mix_bwd", body, _NoExchange(), (), (dmix, cat_r, cat_a, wout),
                  [row(D), row(512), row(512), full],
                  [row(512), pl.BlockSpec((4, tm, 128), lambda i: (0, i, 0)), full],
                  [jax.ShapeDtypeStruct((S, 512), F32), jax.ShapeDtypeStruct((4, S, 128), F32),
                   jax.ShapeDtypeStruct((D, D), BF16)],
                  scratch_shapes=[pltpu.VMEM((D, D), F32)], grid=(S // tm,), semantics=("arbitrary",), after=after)[0]


def _att_bwd(aq, ak, av, datt, att_out, lse, exchange, exchange_args):
    def body(q_ref, k_ref, v_ref, do_ref, out_ref, l_ref, dq_ref, dk_ref, dv_ref, xc):
        xc.start()

        def clear(i, carry):
            rows = _rows(i, 256)
            for ref in (dq_ref, dk_ref, dv_ref):
                for j in range(4):
                    ref[j, rows, :] = jnp.zeros((256, 128), F32)
            return carry

        lax.fori_loop(0, S // 256, clear, 0)
        lane_head = lax.broadcasted_iota(jnp.int32, (ATT_BLK, 256), 1) // 64
        for d in PATTERN_DILATIONS:
            nb, has_prev = _att_blocks(d)
            bias_rest, bias_first = _att_bias(has_prev)

            def block(b, carry, d=d, nb=nb, has_prev=has_prev, bias_rest=bias_rest, bias_first=bias_first):
                r, ib = b // nb, b % nb
                rows = _class_rows(ib, r, d)
                prow = _class_rows(jnp.maximum(ib - 1, 0), r, d)
                bias = jnp.where(ib == 0, bias_first, bias_rest) if has_prev else bias_first
                for g in range(2):
                    qg = _slab_pair(q_ref, g, rows).astype(BF16)
                    kg = _slab_pair(k_ref, g, rows)
                    vg = _slab_pair(v_ref, g, rows)
                    if has_prev:
                        kg = jnp.concatenate([_slab_pair(k_ref, g, prow), kg], axis=0)
                        vg = jnp.concatenate([_slab_pair(v_ref, g, prow), vg], axis=0)
                    kg, vg = kg.astype(BF16), vg.astype(BF16)
                    dog = _slab_pair(do_ref, g, rows)
                    outg = _slab_pair(out_ref, g, rows)
                    lg = _slab_pair(l_ref, g, rows)
                    qs = _stack_heads(qg, lane_head)
                    dos = _stack_heads(dog, lane_head)
                    delta = jnp.sum(dos * jnp.concatenate([outg] * 4, axis=0), axis=-1, keepdims=True)
                    lh = jnp.max(_stack_heads(lg, lane_head, NEG), axis=-1, keepdims=True)
                    s = _nt(qs, kg) * ATT_SCALE + bias
                    p = jnp.exp(s - lh)
                    dosb = dos.astype(BF16)
                    ds = (p * (_nt(dosb, vg) - delta) * ATT_SCALE).astype(BF16)
                    dq = _unstack_heads(_nn(ds, kg), lane_head)
                    dk = _tn(ds, qs)
                    dv = _tn(p.astype(BF16), dosb)
                    for jj in range(2):
                        j, sl = 2 * g + jj, slice(128 * jj, 128 * jj + 128)
                        dq_ref[j, rows, :] += dq[:, sl]
                        if has_prev:
                            dk_ref[j, prow, :] += dk[0:ATT_BLK, sl]
                            dv_ref[j, prow, :] += dv[0:ATT_BLK, sl]
                            dk_ref[j, rows, :] += dk[ATT_BLK:2 * ATT_BLK, sl]
                            dv_ref[j, rows, :] += dv[ATT_BLK:2 * ATT_BLK, sl]
                        else:
                            dk_ref[j, rows, :] += dk[:, sl]
                            dv_ref[j, rows, :] += dv[:, sl]
                return carry

            lax.fori_loop(0, S // ATT_BLK, block, 0)
        xc.middle()
        xc.finish()

    slab = jax.ShapeDtypeStruct((4, S, 128), F32)
    return _carry("att_bwd", body, exchange, exchange_args, (aq, ak, av, datt, att_out, lse), [VMEM] * 6, [VMEM] * 3,
                  [slab, slab, slab])


def _ret_bwd(qr, kr, rv, proj, o_raw, states, dret, tabs, exchange, exchange_args):
    C = RET_C
    nc = S // C
    dtab, a_tab, b_tab, lam, bd = tabs

    def body(q_ref, k_ref, v_ref, g_ref, o_ref, st_ref, dr_ref, dt_ref, a_ref, b_ref, lam_ref, bd_ref,
             dq_ref, dk_ref, dv_ref, dg_ref, dR, exch):
        @pl.when(pl.program_id(0) == 0)
        def _():
            exch.start()
            dR[...] = jnp.zeros_like(dR)

        q, k, v = q_ref[...], k_ref[...], v_ref[...]
        lane_head = lax.broadcasted_iota(jnp.int32, (C, 256), 1) // 32
        col_head = lax.broadcasted_iota(jnp.int32, (C, 256), 1) // 64
        dos = []
        for j in range(4):
            sl = slice(128 * j, 128 * j + 128)
            oj = o_ref[:, sl]
            xc = oj - _seg_mean(oj)
            rs = lax.rsqrt(_seg_mean(xc * xc) + GN_EPS)
            rn = xc * rs
            gj = g_ref[:, sl]
            sg = _sigmoid(gj)
            dret = dr_ref[:, sl]
            dg_ref[:, sl] = dret * rn * (sg * (1.0 + gj * (1.0 - sg)))
            drn = dret * (gj * sg)
            dos.append(rs * (drn - _seg_mean(drn) - rn * _seg_mean(drn * rn)))
        do = [jnp.concatenate(dos[0:2], axis=1), jnp.concatenate(dos[2:4], axis=1)]
        do8 = jnp.concatenate(do, axis=1).astype(BF16)
        drb = dR[...].astype(BF16)
        rb = st_ref[...]
        dq = _nt(do8, rb) * a_ref[...]
        dk = _nt(v, drb) * b_ref[...]
        kb = (k.astype(F32) * b_ref[...]).astype(BF16)
        dvall = _nn(kb, drb)
        qs = _stack_heads(q, lane_head, n=8)
        dec = dt_ref[...]
        p = (_nt(qs, k) * dec).astype(BF16)
        dos = [_stack_heads(do[g], col_head).astype(BF16) for g in range(2)]
        dp = jnp.concatenate([_nt(dos[g], v[:, 256 * g:256 * g + 256]) for g in range(2)], axis=0)
        ds = (dp * dec).astype(BF16)
        dq = dq + _unstack_heads(_nn(ds, k), lane_head, n=8)
        dk = dk + _tn(ds, qs)
        dv = [dvall[:, 256 * g:256 * g + 256] + _tn(p[4 * C * g:4 * C * (g + 1)], dos[g]) for g in range(2)]
        qa = (q.astype(F32) * a_ref[...]).astype(BF16)
        dR[...] = dR[...] * lam_ref[...] + _tn(qa, do8) * bd_ref[...]
        dq_ref[...] = dq
        dk_ref[...] = dk
        dv_ref[:, 0:256] = dv[0]
        dv_ref[:, 256:512] = dv[1]

        @pl.when(pl.program_id(0) == nc - 1)
        def _():
            exch.middle()
            exch.finish()

    rev = lambda w: pl.BlockSpec((C, w), lambda n: (nc - 1 - n, 0))
    full = lambda a: pl.BlockSpec(a.shape, lambda n: (0,) * a.ndim)
    return _carry(
        "ret_bwd", body, exchange, exchange_args, (qr, kr, rv, proj, o_raw, states, dret, dtab, a_tab, b_tab, lam, bd),
        [rev(256), rev(256), rev(512), pl.BlockSpec((C, 512), lambda n: (nc - 1 - n, 2)), rev(512),
         pl.BlockSpec((None, 256, 512), lambda n: (nc - 1 - n, 0, 0)), rev(512),
         full(dtab), full(a_tab), full(b_tab), full(lam), full(bd)],
        [rev(256), rev(256), rev(512), rev(512)],
        [jax.ShapeDtypeStruct((S, 256), F32), jax.ShapeDtypeStruct((S, 256), F32),
         jax.ShapeDtypeStruct((S, 512), F32), jax.ShapeDtypeStruct((S, 512), F32)],
        scratch_shapes=[pltpu.VMEM((256, 512), F32)], grid=(nc,), semantics=("arbitrary",))


def _rot_bwd(cos, sin, dqr, dkr, drv, drg, dq_att, dk_att, dv_att):
    tm = 256

    def body(cos_ref, sin_ref, dqr_ref, dkr_ref, drv_ref, drg_ref, dqa_ref, dka_ref, dva_ref, dp_ref):
        cr, ca, sr, sa = cos_ref[:, 0:256], cos_ref[:, 256:768], sin_ref[:, 0:256], sin_ref[:, 256:768]
        lo_r, lo_a = _rot_halves(tm)

        def unrot_r(g):
            gs = g * sr
            return g * cr + pltpu.roll(jnp.where(lo_r, -gs, 0.0), 16, 1) + pltpu.roll(jnp.where(lo_r, 0.0, gs), 240, 1)

        def unrot_a(g):
            gs = g * sa
            return g * ca + pltpu.roll(jnp.where(lo_a, -gs, 0.0), 8, 1) + pltpu.roll(jnp.where(lo_a, 0.0, gs), 504, 1)

        def wide(ref):
            return jnp.concatenate([ref[j] for j in range(4)], axis=1)

        dp_ref[:, 0:256] = unrot_r(dqr_ref[...]).astype(BF16)
        dp_ref[:, 256:512] = unrot_r(dkr_ref[...] * RET_SCALE).astype(BF16)
        dp_ref[:, 512:1024] = drv_ref[...].astype(BF16)
        dp_ref[:, 1024:1536] = drg_ref[...].astype(BF16)
        dp_ref[:, 1536:2048] = unrot_a(wide(dqa_ref)).astype(BF16)
        dp_ref[:, 2048:2560] = unrot_a(wide(dka_ref)).astype(BF16)
        dp_ref[:, 2560:3072] = wide(dva_ref).astype(BF16)

    row = lambda w: pl.BlockSpec((tm, w), lambda i: (i, 0))
    slab = pl.BlockSpec((4, tm, 128), lambda i: (0, i, 0))
    return pl.pallas_call(
        body, grid=(S // tm,), name="rot_bwd",
        in_specs=[row(768), row(768), row(256), row(256), row(512), row(512), slab, slab, slab],
        out_specs=row(PW), out_shape=jax.ShapeDtypeStruct((S, PW), BF16),
        compiler_params=_params("parallel"),
    )(cos, sin, dqr, dkr, drv, drg, dq_att, dk_att, dv_att)


def _win_bwd_w(h1, dproj, exchange, exchange_args):
    tm = 512

    def body(h_ref, dp_ref, dw_ref, acc, xc):
        k, i = pl.program_id(0), pl.program_id(1)

        @pl.when((k == 0) & (i == 0))
        def _():
            xc.start()

        @pl.when(i == 0)
        def _():
            acc[...] = jnp.zeros_like(acc)

        acc[...] += _tn(h_ref[...], dp_ref[...])

        @pl.when(i == S // tm - 1)
        def _():
            dw_ref[...] = acc[...].astype(BF16)

        @pl.when((k == N_CHIP - 1) & (i == S // tm - 1))
        def _():
            xc.middle()
            xc.finish()

    (dw,), out = _carry(
        "win_bwd_w", body, exchange, exchange_args, (h1, dproj),
        [pl.BlockSpec((tm, D), lambda k, i: (i, 0)), pl.BlockSpec((tm, WIN_C), lambda k, i: (i, k))],
        [pl.BlockSpec((None, D, WIN_C), lambda k, i: (k, 0, 0))],
        [jax.ShapeDtypeStruct((N_CHIP, D, WIN_C), BF16)],
        scratch_shapes=[pltpu.VMEM((D, WIN_C), F32)], grid=(N_CHIP, S // tm), semantics=("arbitrary", "arbitrary"))
    return dw, out


def _in_bwd(dproj, win_g, x, dx2, g1, after):
    tm = 512

    def body(dp_ref, w_ref, x_ref, dx2_ref, g_ref, dx_ref, dg_ref, _):
        @pl.when(pl.program_id(0) == 0)
        def _():
            dg_ref[...] = jnp.zeros_like(dg_ref)

        dh = _nt(dp_ref[:, 0:WIN_C], w_ref[0])
        for k in range(1, N_CHIP):
            dh = dh + _nt(dp_ref[:, k * WIN_C:(k + 1) * WIN_C], w_ref[k])
        xv = x_ref[...]
        r = _rstd(xv)
        xn = xv * r
        dg_ref[...] = dg_ref[...] + jnp.sum(dh * xn, axis=0, keepdims=True)
        t = dh * g_ref[...]
        dx_ref[...] = dx2_ref[...] + r * (t - xn * jnp.mean(t * xn, axis=-1, keepdims=True))

    row = lambda w: pl.BlockSpec((tm, w), lambda i: (i, 0))
    vec = pl.BlockSpec((1, D), lambda i: (0, 0))
    return _carry("in_bwd", body, _NoExchange(), (), (dproj, win_g, x, dx2, g1),
                  [row(PW), pl.BlockSpec((N_CHIP, D, WIN_C), lambda i: (0, 0, 0)), row(D), row(D), vec],
                  [row(D), vec], [jax.ShapeDtypeStruct((S, D), F32), jax.ShapeDtypeStruct((1, D), F32)],
                  grid=(S // tm,), semantics=("arbitrary",), after=after)[0]


ANY = pl.BlockSpec(memory_space=pl.ANY)
VMEM = pl.BlockSpec(memory_space=pltpu.VMEM)
FLIPS = ((1, 0), (0, 1), (1, 1))


def _place():
    x, y, c = lax.axis_index("x"), lax.axis_index("y"), lax.axis_index("c")
    chips = [((1 - x) if fx else x, (1 - y) if fy else y) for fx, fy in FLIPS]
    return x, y, c, 2 * x + y, chips


def _remote(src, dst, send_sem, recv_sem, device):
    return pltpu.make_async_remote_copy(src_ref=src, dst_ref=dst, send_sem=send_sem, recv_sem=recv_sem,
                                        device_id=device, device_id_type=MESH)


def _staggered(issue):
    c = lax.axis_index("c")

    @pl.when(c == 0)
    def _():
        issue((0, 1, 2))

    @pl.when(c == 1)
    def _():
        issue((1, 0, 2))


class _Exchange:
    aliases = {}

    def middle(self, ins, outs, sems):
        pass


class _GatherShards(_Exchange):
    def __init__(self, shards):
        n = self.n = len(shards)
        self.n_in = self.n_out = n
        self.out_shape = [jax.ShapeDtypeStruct((N_CHIP,) + s.shape, s.dtype) for s in shards]
        dma = pltpu.SemaphoreType.DMA
        self.scratch = [dma((3 * n,)), dma((3 * n,)), dma((3 * n,)), dma((3 * n,)), dma((n,)), dma((n,))]

    def _ici(self, ins, outs, sems, a, j, chip):
        x, y, c, me, chips = _place()
        half = ins[a].shape[0] // 2
        return _remote(ins[a].at[pl.ds(c * half, half), :], outs[a].at[me, pl.ds(c * half, half), :],
                       sems[0].at[3 * a + j], sems[1].at[3 * a + j], (*chip, c))

    def _fwd(self, outs, sems, a, j, chip, half_of):
        x, y, c, me, chips = _place()
        half = outs[a].shape[1] // 2
        blk = outs[a].at[2 * chip[0] + chip[1], pl.ds(half_of * half, half), :]
        return _remote(blk, blk, sems[2].at[3 * a + j], sems[3].at[3 * a + j], (x, y, 1 - c))

    def _own(self, ins, outs, sems, a):
        return _own_shard_to_sibling(ins[a], outs[a], sems[4].at[a], sems[5].at[a])

    def start(self, ins, outs, sems):
        chips = _place()[4]

        def issue(order):
            for a in range(self.n):
                for j in order:
                    self._ici(ins, outs, sems, a, j, chips[j]).start()

        _staggered(issue)
        for a in range(self.n):
            self._own(ins, outs, sems, a).start()

    def middle(self, ins, outs, sems):
        x, y, c, me, chips = _place()
        for a in range(self.n):
            for j, chip in enumerate(chips):
                half = outs[a].shape[1] // 2
                blk = outs[a].at[2 * chip[0] + chip[1], pl.ds(c * half, half), :]
                _remote(blk, blk, sems[0].at[3 * a + j], sems[1].at[3 * a + j], (x, y, c)).wait_recv()
                self._fwd(outs, sems, a, j, chip, c).start()

    def finish(self, ins, outs, sems):
        x, y, c, me, chips = _place()
        for a in range(self.n):
            for j, chip in enumerate(chips):
                self._fwd(outs, sems, a, j, chip, 1 - c).wait_recv()
        for a in range(self.n):
            for j, chip in enumerate(chips):
                self._ici(ins, outs, sems, a, j, chip).wait_send()
                self._fwd(outs, sems, a, j, chip, c).wait_send()
            self._own(ins, outs, sems, a).wait()


def _own_shard_to_sibling(shard_ref, gathered_ref, send_sem, recv_sem):
    x, y, c, me, chips = _place()
    return _remote(shard_ref, gathered_ref.at[me], send_sem, recv_sem, (x, y, 1 - c))


class _NoExchange(_Exchange):
    n_in = n_out = 0
    out_shape = ()
    scratch = ()

    def start(self, ins, outs, sems):
        pass

    def finish(self, ins, outs, sems):
        pass


class _ForwardGathered(_Exchange):
    def __init__(self, shards, own=True, forward=True):
        self.own, self.forward = own, forward
        n = self.n = len(shards)
        self.n_in, self.n_out = 2 * n, n
        self.out_shape = [jax.ShapeDtypeStruct((N_CHIP,) + s.shape, s.dtype) for s in shards]
        dma = pltpu.SemaphoreType.DMA
        self.scratch = [dma((3 * n,)), dma((3 * n,)), dma((n,)), dma((n,))]
        self.aliases = {n + a: a for a in range(n)}

    def _fwd(self, outs, sems, a, j, chip, half_of):
        x, y, c, me, chips = _place()
        half = outs[a].shape[1] // 2
        blk = outs[a].at[2 * chip[0] + chip[1], pl.ds(half_of * half, half), :]
        return _remote(blk, blk, sems[0].at[3 * a + j], sems[1].at[3 * a + j], (x, y, 1 - c))

    def _own(self, ins, outs, sems, a):
        return _own_shard_to_sibling(ins[a], outs[a], sems[2].at[a], sems[3].at[a])

    def start(self, ins, outs, sems):
        x, y, c, me, chips = _place()
        for a in range(self.n):
            for j, chip in enumerate(chips if self.forward else ()):
                self._fwd(outs, sems, a, j, chip, c).start()
        for a in range(self.n if self.own else 0):
            self._own(ins, outs, sems, a).start()

    def finish(self, ins, outs, sems):
        x, y, c, me, chips = _place()
        for a in range(self.n):
            for j, chip in enumerate(chips if self.forward else ()):
                self._fwd(outs, sems, a, j, chip, 1 - c).wait_recv()
        for a in range(self.n):
            for j, chip in enumerate(chips if self.forward else ()):
                self._fwd(outs, sems, a, j, chip, c).wait_send()
            if self.own:
                self._own(ins, outs, sems, a).wait()


HBM = pl.BlockSpec(memory_space=pltpu.HBM)
SEMS = pl.BlockSpec(memory_space=pltpu.SEMAPHORE)
DATAFLOW = pltpu.SideEffectType.DATAFLOW_SIDE_EFFECTING


class _OverIci:
    def __init__(self, name, sources, lands):
        self.name, self.n = name, len(sources)
        hbm = lambda t: pltpu.with_memory_space_constraint(t, pltpu.HBM)
        self.arrays = [hbm(t) for t in sources] + [hbm(t) for t in lands]

    def sent(self, src, land, a, chip):
        raise NotImplementedError

    def landed(self, land, a, chip):
        raise NotImplementedError

    def _copy(self, arr, sems, a, j, receiving):
        x, y, c, me, chips = _place()
        src, dst = self.sent(arr[a], arr[self.n + a], a, chips[j])
        if receiving:
            dst = self.landed(arr[self.n + a], a, chips[j])
        return _remote(src, dst, sems[0].at[3 * a + j], sems[1].at[3 * a + j], (*chips[j], c))

    def start(self, after):
        m = len(self.arrays)

        def body(*refs):
            arr, sems, token = refs[:m], refs[m + 1:m + 3], refs[-1]

            def issue(order):
                for a in range(self.n):
                    for j in order:
                        self._copy(arr, sems, a, j, False).start()

            _staggered(issue)
            token[...] = jnp.zeros_like(token)

        dma = pltpu.SemaphoreType.DMA
        outs = pl.pallas_call(
            body, name=self.name + "_start",
            out_shape=[dma((3 * self.n,)), dma((3 * self.n,))] + [pltpu.HBM(t.shape, t.dtype) for t in self.arrays]
                      + [jax.ShapeDtypeStruct((8, 128), F32)],
            in_specs=[HBM] * m + [ANY], out_specs=[SEMS, SEMS] + [HBM] * m + [VMEM],
            input_output_aliases={i: 2 + i for i in range(m)},
            compiler_params=pltpu.CompilerParams(has_side_effects=DATAFLOW),
        )(*self.arrays, after)
        self.sems, self.arrays = outs[0:2], list(outs[2:2 + m])
        return outs[-1]

    def wait(self, after):
        m = len(self.arrays)

        def body(*refs):
            arr, sems = refs[:m], refs[m:m + 2]
            for a in range(self.n):
                for j in range(3):
                    self._copy(arr, sems, a, j, False).wait_send()
                    self._copy(arr, sems, a, j, True).wait_recv()

        outs = pl.pallas_call(
            body, name=self.name + "_wait",
            out_shape=[pltpu.HBM(t.shape, t.dtype) for t in self.arrays],
            in_specs=[HBM] * m + [SEMS, SEMS, ANY], out_specs=[HBM] * m,
            input_output_aliases={i: i for i in range(m)},
            compiler_params=pltpu.CompilerParams(has_side_effects=DATAFLOW),
        )(*self.arrays, *self.sems, after)
        return list(outs[:self.n]), list(outs[self.n:])


class _GatherOverIci(_OverIci):
    def __init__(self, name, shards):
        super().__init__(name, shards, [lax.empty((N_CHIP,) + s.shape, s.dtype) for s in shards])

    @staticmethod
    def _half(ref):
        c = lax.axis_index("c")
        half = ref.shape[-2] // 2
        return pl.ds(c * half, half)

    def sent(self, src, land, a, chip):
        return src.at[self._half(src), :], land.at[_place()[3], self._half(src), :]

    def landed(self, land, a, chip):
        return land.at[2 * chip[0] + chip[1], self._half(land), :]


class _SumOverIci(_OverIci):
    def __init__(self, name, pre):
        super().__init__(name, pre, [lax.empty(p.shape, p.dtype) for p in pre])

    def sent(self, src, land, a, chip):
        return src.at[2 * chip[0] + chip[1]], land.at[_place()[3]]

    def landed(self, land, a, chip):
        return land.at[2 * chip[0] + chip[1]]


class _HalvesToSibling(_Exchange):
    def __init__(self, grads):
        n = self.n = len(grads)
        self.n_in = self.n_out = n
        self.out_shape = [jax.ShapeDtypeStruct((N_CHIP, g.shape[1] // 2, g.shape[2]), g.dtype) for g in grads]
        self.scratch = [pltpu.SemaphoreType.DMA((n,)), pltpu.SemaphoreType.DMA((n,))]

    def _copy(self, ins, outs, sems, a):
        x, y, c, me, chips = _place()
        half = ins[a].shape[1] // 2
        return _remote(ins[a].at[:, pl.ds((1 - c) * half, half), :], outs[a], sems[0].at[a], sems[1].at[a], (x, y, 1 - c))

    def start(self, ins, outs, sems):
        for a in range(self.n):
            self._copy(ins, outs, sems, a).start()

    def finish(self, ins, outs, sems):
        for a in range(self.n):
            self._copy(ins, outs, sems, a).wait_recv()
        for a in range(self.n):
            self._copy(ins, outs, sems, a).wait_send()


class _OverChips(_Exchange):
    def __init__(self, pre):
        n = self.n = len(pre)
        self.n_in = self.n_out = n
        self.out_shape = [jax.ShapeDtypeStruct(p.shape, p.dtype) for p in pre]
        dma = pltpu.SemaphoreType.DMA
        self.scratch = [dma((3 * n,)), dma((3 * n,))]

    def _ici(self, ins, outs, sems, a, j, chip):
        x, y, c, me, chips = _place()
        return _remote(ins[a].at[2 * chip[0] + chip[1]], outs[a].at[me], sems[0].at[3 * a + j], sems[1].at[3 * a + j],
                       (*chip, c))

    def start(self, ins, outs, sems):
        chips = _place()[4]

        def issue(order):
            for a in range(self.n):
                for j in order:
                    self._ici(ins, outs, sems, a, j, chips[j]).start()

        _staggered(issue)

    def finish(self, ins, outs, sems):
        x, y, c, me, chips = _place()
        for a in range(self.n):
            for j, chip in enumerate(chips):
                blk = outs[a].at[2 * chip[0] + chip[1]]
                _remote(blk, blk, sems[0].at[3 * a + j], sems[1].at[3 * a + j], (x, y, c)).wait_recv()
        for a in range(self.n):
            for j, chip in enumerate(chips):
                self._ici(ins, outs, sems, a, j, chip).wait_send()


class _ShareHalves(_Exchange):
    def __init__(self, fulls):
        n = self.n = len(fulls)
        self.n_in = self.n_out = n
        self.out_shape = [jax.ShapeDtypeStruct(f.shape, f.dtype) for f in fulls]
        self.scratch = [pltpu.SemaphoreType.DMA((n,)), pltpu.SemaphoreType.DMA((n,))]
        self.aliases = {a: a for a in range(n)}

    def _copy(self, outs, sems, a, half_of):
        x, y, c, me, chips = _place()
        half = outs[a].shape[0] // 2
        rows = outs[a].at[pl.ds(half_of * half, half), :]
        return _remote(rows, rows, sems[0].at[a], sems[1].at[a], (x, y, 1 - c))

    def start(self, ins, outs, sems):
        c = _place()[2]
        for a in range(self.n):
            self._copy(outs, sems, a, c).start()

    def finish(self, ins, outs, sems):
        c = _place()[2]
        for a in range(self.n):
            self._copy(outs, sems, a, 1 - c).wait_recv()
        for a in range(self.n):
            self._copy(outs, sems, a, c).wait_send()


class _GatherBlocks(_Exchange):
    def __init__(self, block):
        self.n_in = self.n_out = 1
        self.out_shape = [jax.ShapeDtypeStruct((8,) + block.shape, block.dtype)]
        dma = pltpu.SemaphoreType.DMA
        self.scratch = [dma((7,)), dma((7,)), dma]

    @staticmethod
    def _peer(f):
        x, y, c, me, chips = _place()
        return ((1 - x) if f & 4 else x, (1 - y) if f & 2 else y, (1 - c) if f & 1 else c)

    def start(self, ins, outs, sems):
        x, y, c, me, chips = _place()
        for f in range(1, 8):
            _remote(ins[0], outs[0].at[2 * me + c], sems[0].at[f - 1], sems[1].at[f - 1], self._peer(f)).start()
        pltpu.make_async_copy(ins[0], outs[0].at[2 * me + c], sems[2]).start()

    def finish(self, ins, outs, sems):
        x, y, c, me, chips = _place()
        for f in range(1, 8):
            px, py, pc = self._peer(f)
            blk = outs[0].at[4 * px + 2 * py + pc]
            _remote(blk, blk, sems[0].at[f - 1], sems[1].at[f - 1], (x, y, c)).wait_recv()
        for f in range(1, 8):
            _remote(ins[0], outs[0].at[2 * me + c], sems[0].at[f - 1], sems[1].at[f - 1], self._peer(f)).wait_send()
        pltpu.make_async_copy(ins[0], outs[0].at[2 * me + c], sems[2]).wait()


class _Both(_Exchange):
    def __init__(self, first, second):
        self.parts = (first, second)
        self.n_in, self.n_out = first.n_in + second.n_in, first.n_out + second.n_out
        self.out_shape = first.out_shape + second.out_shape
        self.scratch = first.scratch + second.scratch
        self.aliases = dict(first.aliases)
        self.aliases.update({first.n_in + i: first.n_out + o for i, o in second.aliases.items()})

    def _split(self, ins, outs, sems):
        a, b = self.parts
        return ((a, ins[:a.n_in], outs[:a.n_out], sems[:len(a.scratch)]),
                (b, ins[a.n_in:], outs[a.n_out:], sems[len(a.scratch):]))

    def start(self, ins, outs, sems):
        for ex, i, o, s in self._split(ins, outs, sems):
            ex.start(i, o, s)

    def middle(self, ins, outs, sems):
        for ex, i, o, s in self._split(ins, outs, sems):
            ex.middle(i, o, s)

    def finish(self, ins, outs, sems):
        for ex, i, o, s in self._split(ins, outs, sems):
            ex.finish(i, o, s)


class _Bound:
    def __init__(self, ex, ins, outs, sems):
        self.start = lambda: ex.start(ins, outs, sems)
        self.middle = lambda: ex.middle(ins, outs, sems)
        self.finish = lambda: ex.finish(ins, outs, sems)


def _carry(name, body, ex, ex_args, args, in_specs, out_specs, out_shape, scratch_shapes=(), grid=None, semantics=(),
           after=None):
    n_a, n_o, n_s = len(args), len(out_shape), len(scratch_shapes)
    behind = [] if after is None else [after]

    def full_body(*refs):
        p = 0
        groups = []
        for size in (n_a, ex.n_in, len(behind), n_o, ex.n_out, n_s, len(ex.scratch)):
            groups.append(refs[p:p + size])
            p += size
        a, ei, _, o, eo, s, es = groups
        body(*a, *o, *s, _Bound(ex, ei, eo, es))

    kwargs = {} if grid is None else {"grid": grid}
    outs = pl.pallas_call(
        full_body, name=name,
        in_specs=list(in_specs) + [ANY] * (ex.n_in + len(behind)), out_specs=list(out_specs) + [ANY] * ex.n_out,
        out_shape=list(out_shape) + list(ex.out_shape), scratch_shapes=list(scratch_shapes) + list(ex.scratch),
        input_output_aliases={n_a + i: n_o + o for i, o in ex.aliases.items()},
        compiler_params=_params(*semantics) if semantics else pltpu.CompilerParams(vmem_limit_bytes=VMEM_LIMIT),
        **kwargs,
    )(*args, *ex_args, *behind)
    return outs[:n_o], outs[n_o:]


def _exchange_alone(name, ex, ex_args):
    def body(xc):
        xc.start()
        xc.middle()
        xc.finish()

    return _carry(name, body, ex, ex_args, (), (), (), ())[1]


def _core_index():
    return lax.axis_index("c").astype(jnp.int32).reshape(1)


def _pair_sum(gs, gots):
    n = len(gs)
    _, r, cc = gs[0].shape
    half = r // 2

    def body(c_ref, *refs):
        for a in range(n):
            refs[2 * n + a][...] = (refs[a][...].astype(F32) + refs[n + a][...].astype(F32)).astype(BF16)

    mine = pl.BlockSpec((None, half, cc), lambda k, c_ref: (k, c_ref[0], 0))
    blk = pl.BlockSpec((None, half, cc), lambda k, c_ref: (k, 0, 0))
    return pl.pallas_call(
        body, name=f"pair_sum_{r}x{cc}",
        grid_spec=pltpu.PrefetchScalarGridSpec(
            num_scalar_prefetch=1, grid=(N_CHIP,), in_specs=[mine] * n + [blk] * n, out_specs=[blk] * n),
        out_shape=[jax.ShapeDtypeStruct((N_CHIP, half, cc), BF16)] * n,
        compiler_params=_params("parallel"),
    )(_core_index(), *gs, *gots)


def _chip_sum(pre, parts):
    n = len(parts)
    _, half, cc = parts[0].shape
    tr = half // 2
    me = 2 * lax.axis_index("x") + lax.axis_index("y")
    others = [k + (k >= me).astype(jnp.int32) for k in range(3)]
    where = jnp.stack([lax.axis_index("c"), me, *others]).astype(jnp.int32)

    def body(w_ref, *refs):
        for a in range(n):
            own, p1, p2, p3 = refs[4 * a:4 * a + 4]
            refs[4 * n + a][...] = ((own[...].astype(F32) + p1[...].astype(F32)) + p2[...].astype(F32)) + p3[...].astype(F32)

    slot = lambda s: pl.BlockSpec((None, tr, cc), lambda i, w_ref: (w_ref[s], i, 0))
    operands = []
    for a in range(n):
        operands += [pre[a], parts[a], parts[a], parts[a]]
    return pl.pallas_call(
        body, name=f"chip_sum_{half}x{cc}",
        grid_spec=pltpu.PrefetchScalarGridSpec(
            num_scalar_prefetch=1, grid=(2,),
            in_specs=[slot(1), slot(2), slot(3), slot(4)] * n,
            out_specs=[pl.BlockSpec((tr, cc), lambda i, w_ref: (2 * w_ref[0] + i, 0))] * n),
        out_shape=[jax.ShapeDtypeStruct((2 * half, cc), F32)] * n,
        compiler_params=_params("parallel"),
    )(where, *operands)


def _adamw_math(w, g, m, v):
    m = ADAM_B1 * m + (1.0 - ADAM_B1) * g
    v = ADAM_B2 * v + (1.0 - ADAM_B2) * (g * g)
    m_hat = m / (1.0 - ADAM_B1 ** ADAM_STEP)
    v_hat = v / (1.0 - ADAM_B2 ** ADAM_STEP)
    delta = -ADAM_LR * (m_hat / (jnp.sqrt(v_hat) + ADAM_EPS) + ADAM_WD * w)
    return delta, m, v


def _adamw(w, g, m, v, after=None):
    r, cc = w.shape
    tr = r // 4

    def body(w_ref, g_ref, m_ref, v_ref, go_ref, d_ref, nm_ref, nv_ref, _):
        g = g_ref[...]
        go_ref[...] = g
        d_ref[...], nm_ref[...], nv_ref[...] = _adamw_math(w_ref[...], g, m_ref[...], v_ref[...])

    blk = pl.BlockSpec((tr, cc), lambda i: (i, 0))
    return _carry(f"adamw_{r}x{cc}", body, _NoExchange(), (), (w, g, m, v), [blk] * 4, [blk] * 4,
                  [jax.ShapeDtypeStruct((r, cc), F32)] * 4, grid=(4,), semantics=("parallel",), after=after)[0]


def _pack8(rows):
    def body(*refs):
        out_ref = refs[-1]
        out_ref[...] = jnp.zeros_like(out_ref)
        for i, r in enumerate(refs[:-1]):
            out_ref[i:i + 1, :] = r[...]

    return pl.pallas_call(body, name="pack8", out_shape=jax.ShapeDtypeStruct((8, D), F32))(*rows)


def _adamw_gains(gall, w8, m8, v8):
    def body(ga_ref, w_ref, m_ref, v_ref, g_ref, d_ref, nm_ref, nv_ref):
        g = ga_ref[0]
        for dev in range(1, 8):
            g = g + ga_ref[dev]
        g_ref[...] = g
        d_ref[...], nm_ref[...], nv_ref[...] = _adamw_math(w_ref[...], g, m_ref[...], v_ref[...])

    return pl.pallas_call(
        body, name="adamw_gains",
        out_shape=[jax.ShapeDtypeStruct((8, D), F32)] * 4,
    )(gall, w8, m8, v8)


def kernel(x, positions, w_in, w_out, g_pre_mix, g_post_mix, g_pre_ffn, g_post_ffn, w_gate, w_up, w_down, loss_target, m_w_in, m_w_out, m_g_pre_mix, m_g_post_mix, m_g_pre_ffn, m_g_post_ffn, m_w_gate, m_w_up, m_w_down, v_w_in, v_w_out, v_g_pre_mix, v_g_post_mix, v_g_pre_ffn, v_g_post_ffn, v_w_gate, v_w_up, v_w_down):
    tr = lambda t: jnp.swapaxes(t, 1, 2)[0]
    shards = [w_in[0], w_out[0], tr(w_gate), tr(w_up), w_down[0]]
    moms = [m_w_in[0], m_w_out[0], tr(m_w_gate), tr(m_w_up), m_w_down[0]]
    vels = [v_w_in[0], v_w_out[0], tr(v_w_gate), tr(v_w_up), v_w_down[0]]
    xs, pos, tgt = x[0], positions.reshape(S, 1), loss_target[0]
    g1, g2, g3, g4 = g_pre_mix, g_post_mix, g_pre_ffn, g_post_ffn
    tabs = tuple(jnp.asarray(t) for t in _retention_tables())
    ifc, spread = _rotary_tables()
    ifc, spread = jnp.asarray(ifc), jnp.asarray(spread, dtype=BF16)
    bf = [s.astype(BF16) for s in shards]

    win_g, wout_g = _exchange_alone("gather_in", _GatherShards(bf[:2]), bf[:2])
    wout_g = wout_g.reshape(D, D)
    ffn_gather = _GatherOverIci("ffn_gather", bf[2:])
    token = ffn_gather.start(win_g)
    proj, h1 = _proj_fwd(xs, g1, win_g, token)
    qr, kr, rv, aq, ak, av, cos, sin = _rot_fwd(proj, pos, ifc, spread)
    (o_raw, cat_r, states), _ = _ret_fwd(qr, kr, rv, proj, tabs, _NoExchange(), ())
    n_ffn = len(bf[2:])
    (att_out, lse, cat_a), ffn_gather.arrays[n_ffn:] = _att_fwd(
        aq, ak, av, _ForwardGathered(bf[2:], forward=False), ffn_gather.arrays)
    ffn_sh, ffn_lands = ffn_gather.wait(cat_a)
    (mix, x2, h3), (wg_g, wu_g, wd_g) = _mix_fwd(cat_r, cat_a, wout_g, xs, g2, g3,
                                                _ForwardGathered(bf[2:], own=False), [*ffn_sh, *ffn_lands])
    gt, up, a, f = _ffn_fwd(h3, wg_g, wu_g, wd_g)

    sq, dy, df, dg4 = _head_bwd(f, x2, tgt, g4)
    loss = 0.5 * lax.psum(sq[0, 0], ("x", "y", "c")) / D
    dgt, dup, dh3 = _ffn_bwd_act(df, gt, up, wg_g, wu_g, wd_g)
    ffn_grads = list(_ffn_bwd_w(a, df, h3, dgt, dup))
    (dx2, dmix, dg3, dg2), got = _norm_bwd(dh3, dy, x2, mix, g2, g3, _HalvesToSibling(ffn_grads), ffn_grads)
    ffn_sum = _SumOverIci("ffn_sum", _pair_sum(ffn_grads, got))
    token = ffn_sum.start(dmix)
    dret, datt, dwout = _mix_bwd(dmix, cat_r, cat_a, wout_g, token)
    (dq_att, dk_att, dv_att), _ = _att_bwd(aq, ak, av, datt, att_out, lse, _NoExchange(), ())
    (dqr, dkr, drv, drg), _ = _ret_bwd(qr, kr, rv, proj, o_raw, states, dret, tabs, _NoExchange(), ())
    sums = _chip_sum(*ffn_sum.wait(dqr))
    dproj = _rot_bwd(cos, sin, dqr, dkr, drv, drg, dq_att, dk_att, dv_att)
    dwin, ffn_full = _win_bwd_w(h1, dproj, _ShareHalves(sums), sums)
    in_grads = [dwin, dwout.reshape(N_CHIP, WOUT_R, D)]

    got = _exchange_alone("halves_to_sibling", _HalvesToSibling(in_grads), in_grads)
    in_sum = _SumOverIci("in_sum", [*_pair_sum(in_grads[:1], got[:1]), *_pair_sum(in_grads[1:], got[1:])])
    token = in_sum.start(dproj)
    dx, dg1 = _in_bwd(dproj, win_g, xs, dx2, g1, token)
    gblock = _pack8([dg1, dg2, dg3, dg4])
    (gall,) = _exchange_alone("gather_gains", _GatherBlocks(gblock), [gblock])
    ffn_upd = [_adamw(shards[2 + i], ffn_full[o], moms[2 + i], vels[2 + i], token)
               for i, o in enumerate((1, 2, 0))]
    pre, parts = in_sum.wait(ffn_upd[2][0])
    sums = [*_chip_sum(pre[:1], parts[:1]), *_chip_sum(pre[1:], parts[1:])]
    in_full = _exchange_alone("share_rest", _ShareHalves(sums), sums)
    upd = [_adamw(w, g, m, v) for w, g, m, v in zip(shards[:2], in_full, moms[:2], vels[:2])] + ffn_upd
    gg, gd, gm, gv = _adamw_gains(gall, _pack8([g1, g2, g3, g4]),
                                  _pack8([m_g_pre_mix, m_g_post_mix, m_g_pre_ffn, m_g_post_ffn]),
                                  _pack8([v_g_pre_mix, v_g_post_mix, v_g_pre_ffn, v_g_post_ffn]))

    def order(mats, vecs):
        back = lambda t: jnp.swapaxes(t[None], 1, 2)
        return ([mats[0][None], mats[1][None]] + [vecs[i:i + 1] for i in range(4)]
                + [back(mats[2]), back(mats[3]), mats[4][None]])

    return (loss, dx[None],
            *order([u[0] for u in upd], gg),
            *order([u[1] for u in upd], gd),
            *order([u[2] for u in upd], gm),
            *order([u[3] for u in upd], gv))
```

```python
import functools

import numpy as np
import jax
import jax.numpy as jnp
from jax import lax
from jax.experimental import pallas as pl
from jax.experimental.pallas import tpu as pltpu

F32, BF16 = jnp.float32, jnp.bfloat16
MESH = pl.DeviceIdType.MESH

S = 2048
D = 1024
PW = 3072
N_CHIP = 4
WIN_C = PW // N_CHIP
DFF = 2816
FF_C = DFF // N_CHIP
WOUT_R = D // N_CHIP
RMS_EPS = 1e-6
GN_EPS = 1e-5
RET_C = 128
RET_SCALE = 32 ** -0.5
ATT_BLK = 128
ATT_SCALE = 64 ** -0.5
PATTERN_DILATIONS = (1, 4, 16)
NEG = -1e30
VMEM_LIMIT = 56 * 1024 * 1024

ADAM_LR, ADAM_B1, ADAM_B2, ADAM_EPS, ADAM_WD, ADAM_STEP = 0.001, 0.9, 0.999, 1e-08, 0.01, 10


def _params(*sem):
    return pltpu.CompilerParams(dimension_semantics=sem, vmem_limit_bytes=VMEM_LIMIT)


def _nt(a, b):
    return lax.dot_general(a, b, (((1,), (1,)), ((), ())), preferred_element_type=F32)


def _tn(a, b):
    return lax.dot_general(a, b, (((0,), (0,)), ((), ())), preferred_element_type=F32)


def _nn(a, b):
    return jnp.dot(a, b, preferred_element_type=F32)


def _rstd(v):
    return lax.rsqrt(jnp.mean(v * v, axis=-1, keepdims=True) + RMS_EPS)


def _sigmoid(v):
    return 1.0 / (1.0 + jnp.exp(-v))


def _rows(i, t):
    return pl.ds(pl.multiple_of(i * t, t), t)


def _retention_tables():
    h = np.arange(8, dtype=np.float32)
    log_g = np.log1p(-np.exp2(-5.0 - h)).astype(np.float32)
    idx = np.arange(RET_C, dtype=np.float32)
    diff = idx[:, None] - idx[None, :]
    dtab = np.where(diff >= 0, np.exp(log_g[:, None, None] * np.maximum(diff, 0.0)), 0.0).astype(np.float32)
    dtab = dtab.reshape(8 * RET_C, RET_C)
    lane_head = np.arange(256) // 32
    a_tab = np.exp(log_g[lane_head][None, :] * (idx + 1.0)[:, None]).astype(np.float32)
    b_tab = np.exp(log_g[lane_head][None, :] * (RET_C - 1.0 - idx)[:, None]).astype(np.float32)
    lam = np.exp(log_g[lane_head] * RET_C).astype(np.float32)[:, None]
    bd = (lane_head[:, None] == (np.arange(512) // 64)[None, :]).astype(np.float32)
    return dtab, a_tab, b_tab, lam, bd


def _rotary_tables():
    inv_r = (1.0 / (np.float32(10000.0) ** np.linspace(0.0, 1.0, 16, dtype=np.float32))).astype(np.float32)
    inv_a = (np.float32(500000.0) ** (-np.arange(0, 16, 2, dtype=np.float32) / np.float32(16))).astype(np.float32)
    ifc = np.zeros((1, 128), np.float32)
    ifc[0, 0:16], ifc[0, 16:24] = inv_r, inv_a
    spread = np.zeros((128, 768), np.float32)
    for lane in range(256):
        spread[(lane % 32) % 16, lane] = 1.0
    for lane in range(512):
        d = lane % 64
        spread[16 + d % 8 if d < 16 else 24, 256 + lane] = 1.0
    return ifc, spread


def _rot_halves(tm):
    lo_r = (lax.broadcasted_iota(jnp.int32, (tm, 256), 1) % 32) < 16
    lo_a = (lax.broadcasted_iota(jnp.int32, (tm, 512), 1) % 64) < 8
    return lo_r, lo_a


def _spread_exact(t, e):
    hi = t.astype(BF16)
    r1 = t - hi.astype(F32)
    mid = r1.astype(BF16)
    lo = (r1 - mid.astype(F32)).astype(BF16)
    return _nn(hi, e) + _nn(mid, e) + _nn(lo, e)


def _proj_fwd(x, g1, win_g, pos, ifc, spread, after):
    tm = 256

    def body(x_ref, g_ref, w_ref, pos_ref, ifc_ref, e_ref,
             h_ref, qr_ref, kr_ref, rv_ref, rg_ref, aq_ref, ak_ref, av_ref, cos_ref, sin_ref, p_ref, _):
        xv = x_ref[...]
        h = (xv * _rstd(xv) * g_ref[...]).astype(BF16)
        h_ref[...] = h
        for k in range(N_CHIP):
            p_ref[:, k * WIN_C:(k + 1) * WIN_C] = _nn(h, w_ref[k])
        ang = pos_ref[...].astype(F32) * ifc_ref[...]
        cs = _spread_exact(jnp.cos(ang), e_ref[...])
        sn = _spread_exact(jnp.sin(ang), e_ref[...])
        cos_ref[...] = cs
        sin_ref[...] = sn
        cr, ca, sr, sa = cs[:, 0:256], cs[:, 256:768], sn[:, 0:256], sn[:, 256:768]
        lo_r, lo_a = _rot_halves(tm)

        def rot_r(v):
            return v * cr + sr * jnp.where(lo_r, -pltpu.roll(v, 240, 1), pltpu.roll(v, 16, 1))

        def rot_a(v):
            return v * ca + sa * jnp.where(lo_a, -pltpu.roll(v, 504, 1), pltpu.roll(v, 8, 1))

        qr_ref[...] = rot_r(p_ref[:, 0:256]).astype(BF16)
        kr_ref[...] = (rot_r(p_ref[:, 256:512]) * RET_SCALE).astype(BF16)
        rv_ref[...] = p_ref[:, 512:1024].astype(BF16)
        rg_ref[...] = p_ref[:, 1024:1536]
        aq, ak = rot_a(p_ref[:, 1536:2048]), rot_a(p_ref[:, 2048:2560])
        for j in range(4):
            aq_ref[j] = aq[:, 128 * j:128 * j + 128]
            ak_ref[j] = ak[:, 128 * j:128 * j + 128]
            av_ref[j] = p_ref[:, 2560 + 128 * j:2560 + 128 * j + 128]

    row = lambda w: pl.BlockSpec((tm, w), lambda i: (i, 0))
    const = lambda w: pl.BlockSpec((1, w), lambda i: (0, 0))
    slab = pl.BlockSpec((4, tm, 128), lambda i: (0, i, 0))
    return _carry(
        "proj_fwd", body, _NoExchange(), (), (x, g1, win_g, pos, ifc, spread),
        [row(D), const(D), pl.BlockSpec((N_CHIP, D, WIN_C), lambda i: (0, 0, 0)), row(1), const(128),
         pl.BlockSpec((128, 768), lambda i: (0, 0))],
        [row(D), row(256), row(256), row(512), row(512), slab, slab, slab, row(768), row(768)],
        [jax.ShapeDtypeStruct((S, D), BF16)] + [jax.ShapeDtypeStruct((S, w), BF16) for w in (256, 256, 512)]
        + [jax.ShapeDtypeStruct((S, 512), F32)] + [jax.ShapeDtypeStruct((4, S, 128), F32)] * 3
        + [jax.ShapeDtypeStruct((S, 768), F32)] * 2,
        scratch_shapes=[pltpu.VMEM((tm, PW), F32)], grid=(S // tm,), semantics=("parallel",), after=after)[0]


def _seg_mean(v):
    lo = lax.broadcasted_iota(jnp.int32, v.shape, 1) < 64
    s_lo = jnp.sum(jnp.where(lo, v, 0.0), axis=-1, keepdims=True)
    s_hi = jnp.sum(jnp.where(lo, 0.0, v), axis=-1, keepdims=True)
    return jnp.where(lo, s_lo, s_hi) * (1.0 / 64.0)


def _ret_fwd(qr, kr, rv, proj, tabs, exchange, exchange_args):
    C = RET_C
    dtab, a_tab, b_tab, lam, bd = tabs

    def body(q_ref, k_ref, v_ref, g_ref, dt_ref, a_ref, b_ref, lam_ref, bd_ref, o_ref, cat_ref, st_ref, R, exch):
        @pl.when(pl.program_id(0) == 0)
        def _():
            exch.start()
            R[...] = jnp.zeros_like(R)

        @pl.when(pl.program_id(0) == S // C // 2)
        def _():
            exch.middle()

        q, k, v = q_ref[...], k_ref[...], v_ref[...]
        lane_head = lax.broadcasted_iota(jnp.int32, (C, 256), 1) // 32
        col_head = lax.broadcasted_iota(jnp.int32, (C, 256), 1) // 64
        rb = R[...].astype(BF16)
        st_ref[...] = rb
        qa = (q.astype(F32) * a_ref[...]).astype(BF16)
        cross = _nn(qa, rb)
        p = (_nt(_stack_heads(q, lane_head, n=8), k) * dt_ref[...]).astype(BF16)
        og = [cross[:, 256 * g:256 * g + 256]
              + _unstack_heads(_nn(p[4 * C * g:4 * C * (g + 1)], v[:, 256 * g:256 * g + 256]), col_head)
              for g in range(2)]
        kb = (k.astype(F32) * b_ref[...]).astype(BF16)
        R[...] = R[...] * lam_ref[...] + _tn(kb, v) * bd_ref[...]
        o_ref[:, 0:256] = og[0]
        o_ref[:, 256:512] = og[1]
        for j in range(4):
            oj = og[j // 2][:, 128 * (j % 2):128 * (j % 2) + 128]
            xc = oj - _seg_mean(oj)
            rn = xc * lax.rsqrt(_seg_mean(xc * xc) + GN_EPS)
            gj = g_ref[:, 128 * j:128 * j + 128]
            cat_ref[:, 128 * j:128 * j + 128] = (rn * (gj * _sigmoid(gj))).astype(BF16)

        @pl.when(pl.program_id(0) == S // C - 1)
        def _():
            exch.finish()

    row = lambda w: pl.BlockSpec((C, w), lambda n: (n, 0))
    full = lambda a: pl.BlockSpec(a.shape, lambda n: (0,) * a.ndim)
    return _carry(
        "ret_fwd", body, exchange, exchange_args, (qr, kr, rv, proj, dtab, a_tab, b_tab, lam, bd),
        [row(256), row(256), row(512), row(512),
         full(dtab), full(a_tab), full(b_tab), full(lam), full(bd)],
        [row(512), row(512), pl.BlockSpec((None, 256, 512), lambda n: (n, 0, 0))],
        [jax.ShapeDtypeStruct((S, 512), F32), jax.ShapeDtypeStruct((S, 512), BF16),
         jax.ShapeDtypeStruct((S // C, 256, 512), BF16)],
        scratch_shapes=[pltpu.VMEM((256, 512), F32)], grid=(S // C,), semantics=("arbitrary",))


def _stack_heads(v, lane_head, fill=0.0, n=4):
    return jnp.concatenate([jnp.where(lane_head == h, v, jnp.full_like(v, fill)) for h in range(n)], axis=0)


def _unstack_heads(v, lane_head, n=4):
    out = v[0:ATT_BLK]
    for h in range(1, n):
        out = jnp.where(lane_head == h, v[h * ATT_BLK:(h + 1) * ATT_BLK], out)
    return out


def _att_bias(has_prev):
    nk = 2 * ATT_BLK if has_prev else ATT_BLK
    a = lax.broadcasted_iota(jnp.int32, (4 * ATT_BLK, nk), 0) % ATT_BLK
    kk = lax.broadcasted_iota(jnp.int32, (4 * ATT_BLK, nk), 1)
    if not has_prev:
        return None, jnp.where((a - kk) >= 0, 0.0, NEG)
    dist = ATT_BLK + a - kk
    inside = (dist >= 0) & (dist <= ATT_BLK)
    return jnp.where(inside, 0.0, NEG), jnp.where(inside & (kk >= ATT_BLK), 0.0, NEG)


def _class_rows(ib, r, d):
    if d == 1:
        return pl.ds(pl.multiple_of(ib * ATT_BLK, ATT_BLK), ATT_BLK)
    return pl.ds(ib * ATT_BLK * d + r, ATT_BLK, stride=d)


def _slab_pair(ref, g, rows):
    return jnp.concatenate([ref[2 * g, rows, :], ref[2 * g + 1, rows, :]], axis=1)


def _att_blocks(d):
    nb = S // d // ATT_BLK
    return nb, nb > 1


def _att_fwd(aq, ak, av, exchange, exchange_args):
    def body(q_ref, k_ref, v_ref, o_ref, l_ref, cat_ref, xc):
        xc.start()
        lane_head = lax.broadcasted_iota(jnp.int32, (ATT_BLK, 256), 1) // 64
        for pi, d in enumerate(PATTERN_DILATIONS):
            if pi == len(PATTERN_DILATIONS) - 1:
                xc.middle()
            nb, has_prev = _att_blocks(d)
            bias_rest, bias_first = _att_bias(has_prev)

            def block(b, carry, pi=pi, d=d, nb=nb, has_prev=has_prev, bias_rest=bias_rest, bias_first=bias_first):
                r, ib = b // nb, b % nb
                rows = _class_rows(ib, r, d)
                prow = _class_rows(jnp.maximum(ib - 1, 0), r, d)
                bias = jnp.where(ib == 0, bias_first, bias_rest) if has_prev else bias_first
                for g in range(2):
                    qg = _slab_pair(q_ref, g, rows).astype(BF16)
                    kg = _slab_pair(k_ref, g, rows)
                    vg = _slab_pair(v_ref, g, rows)
                    if has_prev:
                        kg = jnp.concatenate([_slab_pair(k_ref, g, prow), kg], axis=0)
                        vg = jnp.concatenate([_slab_pair(v_ref, g, prow), vg], axis=0)
                    kg, vg = kg.astype(BF16), vg.astype(BF16)
                    s = _nt(_stack_heads(qg, lane_head), kg) * ATT_SCALE + bias
                    m = jnp.max(s, axis=-1, keepdims=True)
                    p = jnp.exp(s - m)
                    den = jnp.sum(p, axis=-1, keepdims=True)
                    og = _unstack_heads(_nn(p.astype(BF16), vg) / den, lane_head)
                    lg = _unstack_heads(jnp.broadcast_to(m + jnp.log(den), (4 * ATT_BLK, 256)), lane_head)
                    for jj in range(2):
                        j = 2 * g + jj
                        o_new, l_new = og[:, 128 * jj:128 * jj + 128], lg[:, 128 * jj:128 * jj + 128]
                        if pi > 0:
                            o_old, l_old = o_ref[j, rows, :], l_ref[j, rows, :]
                            mx = jnp.maximum(l_old, l_new)
                            ea, eb = jnp.exp(l_old - mx), jnp.exp(l_new - mx)
                            den = ea + eb
                            o_new = (ea * o_old + eb * o_new) / den
                            l_new = mx + jnp.log(den)
                        o_ref[j, rows, :] = o_new
                        l_ref[j, rows, :] = l_new
                return carry

            lax.fori_loop(0, S // ATT_BLK, block, 0)

        def to_cat(i, carry):
            rows = _rows(i, 256)
            for j in range(4):
                cat_ref[rows, 128 * j:128 * j + 128] = o_ref[j, rows, :].astype(BF16)
            return carry

        lax.fori_loop(0, S // 256, to_cat, 0)
        xc.finish()

    slab = jax.ShapeDtypeStruct((4, S, 128), F32)
    return _carry("att_fwd", body, exchange, exchange_args, (aq, ak, av), [VMEM] * 3, [VMEM] * 3,
                  [slab, slab, jax.ShapeDtypeStruct((S, 512), BF16)])


def _mix_fwd(cat_r, cat_a, wout, x, g2, g3, exchange, exchange_args):
    tm = 512

    def body(cr_ref, ca_ref, w_ref, x_ref, g2_ref, g3_ref, mix_ref, x2_ref, h3_ref, xc):
        @pl.when(pl.program_id(0) == 0)
        def _():
            xc.start()

        mix = _nn(cr_ref[...], w_ref[0:512, :]) + _nn(ca_ref[...], w_ref[512:1024, :])
        mix_ref[...] = mix
        x2 = x_ref[...] + mix * _rstd(mix) * g2_ref[...]
        x2_ref[...] = x2
        h3_ref[...] = (x2 * _rstd(x2) * g3_ref[...]).astype(BF16)

        @pl.when(pl.program_id(0) == S // tm - 1)
        def _():
            xc.middle()
            xc.finish()

    row = lambda w: pl.BlockSpec((tm, w), lambda i: (i, 0))
    vec = pl.BlockSpec((1, D), lambda i: (0, 0))
    return _carry("mix_fwd", body, exchange, exchange_args, (cat_r, cat_a, wout, x, g2, g3),
                  [row(512), row(512), pl.BlockSpec((D, D), lambda i: (0, 0)), row(D), vec, vec],
                  [row(D), row(D), row(D)],
                  [jax.ShapeDtypeStruct((S, D), F32), jax.ShapeDtypeStruct((S, D), F32),
                   jax.ShapeDtypeStruct((S, D), BF16)],
                  grid=(S // tm,), semantics=("arbitrary",))


def _ffn_fwd(h3, wg, wu, wd):
    tm = 512

    def body(h_ref, wg_ref, wu_ref, wd_ref, gt_ref, up_ref, a_ref, f_ref):
        k, i = pl.program_id(0), pl.program_id(1)
        h = h_ref[...]
        gt = _nt(h, wg_ref[...])
        up = _nt(h, wu_ref[...])
        gt_ref[...] = gt.astype(BF16)
        up_ref[...] = up.astype(BF16)
        a = (gt * _sigmoid(gt) * up).astype(BF16)
        a_ref[...] = a
        part = _nn(a, wd_ref[...])
        rows = _rows(i, tm)

        @pl.when(k == 0)
        def _():
            f_ref[rows, :] = part

        @pl.when(k > 0)
        def _():
            f_ref[rows, :] = f_ref[rows, :] + part

    wrow = pl.BlockSpec((None, FF_C, D), lambda k, i: (k, 0, 0))
    act = pl.BlockSpec((None, tm, FF_C), lambda k, i: (k, i, 0))
    return pl.pallas_call(
        body, grid=(N_CHIP, S // tm), name="ffn_fwd",
        in_specs=[pl.BlockSpec((tm, D), lambda k, i: (i, 0)), wrow, wrow, wrow],
        out_specs=[act, act, act, pl.BlockSpec((S, D), lambda k, i: (0, 0))],
        out_shape=[jax.ShapeDtypeStruct((N_CHIP, S, FF_C), BF16)] * 3 + [jax.ShapeDtypeStruct((S, D), F32)],
        compiler_params=_params("arbitrary", "arbitrary"),
    )(h3, wg, wu, wd)


def _head_bwd(f, x2, tgt, g4):
    tm = 256

    def body(f_ref, x2_ref, t_ref, g_ref, loss_ref, dy_ref, df_ref, dg_ref):
        @pl.when(pl.program_id(0) == 0)
        def _():
            loss_ref[...] = jnp.zeros_like(loss_ref)
            dg_ref[...] = jnp.zeros_like(dg_ref)

        fv = f_ref[...]
        r = _rstd(fv)
        fn = fv * r
        e = x2_ref[...] + fn * g_ref[...] - t_ref[...]
        sq = jnp.sum(jnp.sum(e * e, axis=-1, keepdims=True), axis=0, keepdims=True)
        loss_ref[...] = loss_ref[...] + sq
        dy = e * (1.0 / D)
        dy_ref[...] = dy
        dg_ref[...] = dg_ref[...] + jnp.sum(dy * fn, axis=0, keepdims=True)
        t = dy * g_ref[...]
        df_ref[...] = (r * (t - fn * jnp.mean(t * fn, axis=-1, keepdims=True))).astype(BF16)

    row = pl.BlockSpec((tm, D), lambda i: (i, 0))
    vec = pl.BlockSpec((1, D), lambda i: (0, 0))
    return pl.pallas_call(
        body, grid=(S // tm,), name="head_bwd",
        in_specs=[row, row, row, vec],
        out_specs=[vec, row, row, vec],
        out_shape=[jax.ShapeDtypeStruct((1, D), F32), jax.ShapeDtypeStruct((S, D), F32),
                   jax.ShapeDtypeStruct((S, D), BF16), jax.ShapeDtypeStruct((1, D), F32)],
        compiler_params=_params("arbitrary"),
    )(f, x2, tgt, g4)


def _ffn_bwd_act(df, gt, up, wg, wu, wd):
    tm, sub = 512, 256

    def body(df_ref, gt_ref, up_ref, wg_ref, wu_ref, wd_ref, dgt_ref, dup_ref, dh_ref):
        k, i = pl.program_id(0), pl.program_id(1)
        parts = []
        for s in range(tm // sub):
            rows = slice(s * sub, (s + 1) * sub)
            da = _nt(df_ref[rows, :], wd_ref[...])
            gt, up = gt_ref[rows, :].astype(F32), up_ref[rows, :].astype(F32)
            sg = _sigmoid(gt)
            dup = (da * gt * sg).astype(BF16)
            dgt = (da * up * (sg * (1.0 + gt * (1.0 - sg)))).astype(BF16)
            dup_ref[rows, :] = dup
            dgt_ref[rows, :] = dgt
            parts.append(_nn(dgt, wg_ref[...]) + _nn(dup, wu_ref[...]))
        part = jnp.concatenate(parts, axis=0)
        rows = _rows(i, tm)

        @pl.when(k == 0)
        def _():
            dh_ref[rows, :] = part

        @pl.when(k > 0)
        def _():
            dh_ref[rows, :] = dh_ref[rows, :] + part

    wrow = pl.BlockSpec((None, FF_C, D), lambda k, i: (k, 0, 0))
    act = pl.BlockSpec((None, tm, FF_C), lambda k, i: (k, i, 0))
    row = pl.BlockSpec((tm, D), lambda k, i: (i, 0))
    return pl.pallas_call(
        body, grid=(N_CHIP, S // tm), name="ffn_bwd_act",
        in_specs=[row, act, act, wrow, wrow, wrow],
        out_specs=[act, act, pl.BlockSpec((S, D), lambda k, i: (0, 0))],
        out_shape=[jax.ShapeDtypeStruct((N_CHIP, S, FF_C), BF16), jax.ShapeDtypeStruct((N_CHIP, S, FF_C), BF16),
                   jax.ShapeDtypeStruct((S, D), F32)],
        compiler_params=_params("arbitrary", "arbitrary"),
    )(df, gt, up, wg, wu, wd)


def _ffn_bwd_w(a, df, h3, dgt, dup):
    tm = 1024
    assert S // tm == 2

    def body(a_ref, df_ref, h_ref, dgt_ref, dup_ref, dwd_ref, dwg_ref, dwu_ref, acc_d, acc_g, acc_u):
        i = pl.program_id(1)
        h = h_ref[...]
        parts = (_tn(a_ref[...], df_ref[...]), _tn(dgt_ref[...], h), _tn(dup_ref[...], h))

        @pl.when(i == 0)
        def _():
            for acc, part in zip((acc_d, acc_g, acc_u), parts):
                acc[...] = part

        @pl.when(i == S // tm - 1)
        def _():
            for out, acc, part in zip((dwd_ref, dwg_ref, dwu_ref), (acc_d, acc_g, acc_u), parts):
                out[...] = (acc[...] + part).astype(BF16)

    act = pl.BlockSpec((None, tm, FF_C), lambda k, i: (k, i, 0))
    row = pl.BlockSpec((tm, D), lambda k, i: (i, 0))
    wrow = pl.BlockSpec((None, FF_C, D), lambda k, i: (k, 0, 0))
    return pl.pallas_call(
        body, grid=(N_CHIP, S // tm), name="ffn_bwd_w",
        in_specs=[act, row, row, act, act],
        out_specs=[wrow, wrow, wrow],
        out_shape=[jax.ShapeDtypeStruct((N_CHIP, FF_C, D), BF16)] * 3,
        scratch_shapes=[pltpu.VMEM((FF_C, D), F32)] * 3,
        compiler_params=_params("parallel", "arbitrary"),
    )(a, df, h3, dgt, dup)


def _norm_bwd(dh3, dy, x2, mix, g2, g3, exchange, exchange_args):
    tm = 256

    def body(dh_ref, dy_ref, x2_ref, mix_ref, g2_ref, g3_ref, dx2_ref, dmix_ref, dg3_ref, dg2_ref, xc):
        @pl.when(pl.program_id(0) == 0)
        def _():
            xc.start()
            dg3_ref[...] = jnp.zeros_like(dg3_ref)
            dg2_ref[...] = jnp.zeros_like(dg2_ref)

        x2 = x2_ref[...]
        r3 = _rstd(x2)
        xn = x2 * r3
        dh = dh_ref[...]
        dg3_ref[...] = dg3_ref[...] + jnp.sum(dh * xn, axis=0, keepdims=True)
        t = dh * g3_ref[...]
        dx2 = dy_ref[...] + r3 * (t - xn * jnp.mean(t * xn, axis=-1, keepdims=True))
        dx2_ref[...] = dx2
        mix = mix_ref[...]
        r2 = _rstd(mix)
        mn = mix * r2
        dg2_ref[...] = dg2_ref[...] + jnp.sum(dx2 * mn, axis=0, keepdims=True)
        u = dx2 * g2_ref[...]
        dmix_ref[...] = (r2 * (u - mn * jnp.mean(u * mn, axis=-1, keepdims=True))).astype(BF16)

        @pl.when(pl.program_id(0) == S // tm - 1)
        def _():
            xc.middle()
            xc.finish()

    row = pl.BlockSpec((tm, D), lambda i: (i, 0))
    vec = pl.BlockSpec((1, D), lambda i: (0, 0))
    return _carry("norm_bwd", body, exchange, exchange_args, (dh3, dy, x2, mix, g2, g3),
                  [row, row, row, row, vec, vec], [row, row, vec, vec],
                  [jax.ShapeDtypeStruct((S, D), F32), jax.ShapeDtypeStruct((S, D), BF16),
                   jax.ShapeDtypeStruct((1, D), F32), jax.ShapeDtypeStruct((1, D), F32)],
                  grid=(S // tm,), semantics=("arbitrary",))


def _mix_bwd(dmix, cat_r, cat_a, wout, after):
    tm = 512

    def body(dm_ref, cr_ref, ca_ref, w_ref, dret_ref, datt_ref, dw_ref, acc, _):
        i = pl.program_id(0)

        @pl.when(i == 0)
        def _():
            acc[...] = jnp.zeros_like(acc)

        dm = dm_ref[...]
        dret_ref[...] = _nt(dm, w_ref[0:512, :])
        datt = _nt(dm, w_ref[512:1024, :])
        for j in range(4):
            datt_ref[j] = datt[:, 128 * j:128 * j + 128]
        acc[0:512, :] += _tn(cr_ref[...], dm)
        acc[512:1024, :] += _tn(ca_ref[...], dm)

        @pl.when(i == S // tm - 1)
        def _():
            dw_ref[...] = acc[...].astype(BF16)

    row = lambda w: pl.BlockSpec((tm, w), lambda i: (i, 0))
    full = pl.BlockSpec((D, D), lambda i: (0, 0))
    return _carry("mix_bwd", body, _NoExchange(), (), (dmix, cat_r, cat_a, wout),
                  [row(D), row(512), row(512), full],
                  [row(512), pl.BlockSpec((4, tm, 128), lambda i: (0, i, 0)), full],
                  [jax.ShapeDtypeStruct((S, 512), F32), jax.ShapeDtypeStruct((4, S, 128), F32),
                   jax.ShapeDtypeStruct((D, D), BF16)],
                  scratch_shapes=[pltpu.VMEM((D, D), F32)], grid=(S // tm,), semantics=("arbitrary",), after=after)[0]


def _att_bwd(aq, ak, av, datt, att_out, lse, exchange, exchange_args):
    def body(q_ref, k_ref, v_ref, do_ref, out_ref, l_ref, dq_ref, dk_ref, dv_ref, xc):
        xc.start()

        def clear(i, carry):
            rows = _rows(i, 256)
            for ref in (dq_ref, dk_ref, dv_ref):
                for j in range(4):
                    ref[j, rows, :] = jnp.zeros((256, 128), F32)
            return carry

        lax.fori_loop(0, S // 256, clear, 0)
        lane_head = lax.broadcasted_iota(jnp.int32, (ATT_BLK, 256), 1) // 64
        for d in PATTERN_DILATIONS:
            nb, has_prev = _att_blocks(d)
            bias_rest, bias_first = _att_bias(has_prev)

            def block(b, carry, d=d, nb=nb, has_prev=has_prev, bias_rest=bias_rest, bias_first=bias_first):
                r, ib = b // nb, b % nb
                rows = _class_rows(ib, r, d)
                prow = _class_rows(jnp.maximum(ib - 1, 0), r, d)
                bias = jnp.where(ib == 0, bias_first, bias_rest) if has_prev else bias_first
                for g in range(2):
                    qg = _slab_pair(q_ref, g, rows).astype(BF16)
                    kg = _slab_pair(k_ref, g, rows)
                    vg = _slab_pair(v_ref, g, rows)
                    if has_prev:
                        kg = jnp.concatenate([_slab_pair(k_ref, g, prow), kg], axis=0)
                        vg = jnp.concatenate([_slab_pair(v_ref, g, prow), vg], axis=0)
                    kg, vg = kg.astype(BF16), vg.astype(BF16)
                    dog = _slab_pair(do_ref, g, rows)
                    outg = _slab_pair(out_ref, g, rows)
                    lg = _slab_pair(l_ref, g, rows)
                    qs = _stack_heads(qg, lane_head)
                    dos = _stack_heads(dog, lane_head)
                    delta = jnp.sum(dos * jnp.concatenate([outg] * 4, axis=0), axis=-1, keepdims=True)
                    lh = jnp.max(_stack_heads(lg, lane_head, NEG), axis=-1, keepdims=True)
                    s = _nt(qs, kg) * ATT_SCALE + bias
                    p = jnp.exp(s - lh)
                    dosb = dos.astype(BF16)
                    ds = (p * (_nt(dosb, vg) - delta) * ATT_SCALE).astype(BF16)
                    dq = _unstack_heads(_nn(ds, kg), lane_head)
                    dk = _tn(ds, qs)
                    dv = _tn(p.astype(BF16), dosb)
                    for jj in range(2):
                        j, sl = 2 * g + jj, slice(128 * jj, 128 * jj + 128)
                        dq_ref[j, rows, :] += dq[:, sl]
                        if has_prev:
                            dk_ref[j, prow, :] += dk[0:ATT_BLK, sl]
                            dv_ref[j, prow, :] += dv[0:ATT_BLK, sl]
                            dk_ref[j, rows, :] += dk[ATT_BLK:2 * ATT_BLK, sl]
                            dv_ref[j, rows, :] += dv[ATT_BLK:2 * ATT_BLK, sl]
                        else:
                            dk_ref[j, rows, :] += dk[:, sl]
                            dv_ref[j, rows, :] += dv[:, sl]
                return carry

            lax.fori_loop(0, S // ATT_BLK, block, 0)
        xc.middle()
        xc.finish()

    slab = jax.ShapeDtypeStruct((4, S, 128), F32)
    return _carry("att_bwd", body, exchange, exchange_args, (aq, ak, av, datt, att_out, lse), [VMEM] * 6, [VMEM] * 3,
                  [slab, slab, slab])


def _ret_bwd(qr, kr, rv, proj, o_raw, states, dret, tabs, exchange, exchange_args):
    C = RET_C
    nc = S // C
    dtab, a_tab, b_tab, lam, bd = tabs

    def body(q_ref, k_ref, v_ref, g_ref, o_ref, st_ref, dr_ref, dt_ref, a_ref, b_ref, lam_ref, bd_ref,
             dq_ref, dk_ref, dv_ref, dg_ref, dR, exch):
        @pl.when(pl.program_id(0) == 0)
        def _():
            exch.start()
            dR[...] = jnp.zeros_like(dR)

        q, k, v = q_ref[...], k_ref[...], v_ref[...]
        lane_head = lax.broadcasted_iota(jnp.int32, (C, 256), 1) // 32
        col_head = lax.broadcasted_iota(jnp.int32, (C, 256), 1) // 64
        dos = []
        for j in range(4):
            sl = slice(128 * j, 128 * j + 128)
            oj = o_ref[:, sl]
            xc = oj - _seg_mean(oj)
            rs = lax.rsqrt(_seg_mean(xc * xc) + GN_EPS)
            rn = xc * rs
            gj = g_ref[:, sl]
            sg = _sigmoid(gj)
            dret = dr_ref[:, sl]
            dg_ref[:, sl] = dret * rn * (sg * (1.0 + gj * (1.0 - sg)))
            drn = dret * (gj * sg)
            dos.append(rs * (drn - _seg_mean(drn) - rn * _seg_mean(drn * rn)))
        do = [jnp.concatenate(dos[0:2], axis=1), jnp.concatenate(dos[2:4], axis=1)]
        do8 = jnp.concatenate(do, axis=1).astype(BF16)
        drb = dR[...].astype(BF16)
        rb = st_ref[...]
        dq = _nt(do8, rb) * a_ref[...]
        dk = _nt(v, drb) * b_ref[...]
        kb = (k.astype(F32) * b_ref[...]).astype(BF16)
        dvall = _nn(kb, drb)
        qs = _stack_heads(q, lane_head, n=8)
        dec = dt_ref[...]
        p = (_nt(qs, k) * dec).astype(BF16)
        dos = [_stack_heads(do[g], col_head).astype(BF16) for g in range(2)]
        dp = jnp.concatenate([_nt(dos[g], v[:, 256 * g:256 * g + 256]) for g in range(2)], axis=0)
        ds = (dp * dec).astype(BF16)
        dq = dq + _unstack_heads(_nn(ds, k), lane_head, n=8)
        dk = dk + _tn(ds, qs)
        dv = [dvall[:, 256 * g:256 * g + 256] + _tn(p[4 * C * g:4 * C * (g + 1)], dos[g]) for g in range(2)]
        qa = (q.astype(F32) * a_ref[...]).astype(BF16)
        dR[...] = dR[...] * lam_ref[...] + _tn(qa, do8) * bd_ref[...]
        dq_ref[...] = dq
        dk_ref[...] = dk
        dv_ref[:, 0:256] = dv[0]
        dv_ref[:, 256:512] = dv[1]

        @pl.when(pl.program_id(0) == nc - 1)
        def _():
            exch.middle()
            exch.finish()

    rev = lambda w: pl.BlockSpec((C, w), lambda n: (nc - 1 - n, 0))
    full = lambda a: pl.BlockSpec(a.shape, lambda n: (0,) * a.ndim)
    return _carry(
        "ret_bwd", body, exchange, exchange_args, (qr, kr, rv, proj, o_raw, states, dret, dtab, a_tab, b_tab, lam, bd),
        [rev(256), rev(256), rev(512), rev(512), rev(512),
         pl.BlockSpec((None, 256, 512), lambda n: (nc - 1 - n, 0, 0)), rev(512),
         full(dtab), full(a_tab), full(b_tab), full(lam), full(bd)],
        [rev(256), rev(256), rev(512), rev(512)],
        [jax.ShapeDtypeStruct((S, 256), F32), jax.ShapeDtypeStruct((S, 256), F32),
         jax.ShapeDtypeStruct((S, 512), F32), jax.ShapeDtypeStruct((S, 512), F32)],
        scratch_shapes=[pltpu.VMEM((256, 512), F32)], grid=(nc,), semantics=("arbitrary",))


def _rot_bwd(cos, sin, dqr, dkr, drv, drg, dq_att, dk_att, dv_att):
    tm = 256

    def body(cos_ref, sin_ref, dqr_ref, dkr_ref, drv_ref, drg_ref, dqa_ref, dka_ref, dva_ref, dp_ref):
        cr, ca, sr, sa = cos_ref[:, 0:256], cos_ref[:, 256:768], sin_ref[:, 0:256], sin_ref[:, 256:768]
        lo_r, lo_a = _rot_halves(tm)

        def unrot_r(g):
            gs = g * sr
            return g * cr + pltpu.roll(jnp.where(lo_r, -gs, 0.0), 16, 1) + pltpu.roll(jnp.where(lo_r, 0.0, gs), 240, 1)

        def unrot_a(g):
            gs = g * sa
            return g * ca + pltpu.roll(jnp.where(lo_a, -gs, 0.0), 8, 1) + pltpu.roll(jnp.where(lo_a, 0.0, gs), 504, 1)

        def wide(ref):
            return jnp.concatenate([ref[j] for j in range(4)], axis=1)

        dp_ref[:, 0:256] = unrot_r(dqr_ref[...]).astype(BF16)
        dp_ref[:, 256:512] = unrot_r(dkr_ref[...] * RET_SCALE).astype(BF16)
        dp_ref[:, 512:1024] = drv_ref[...].astype(BF16)
        dp_ref[:, 1024:1536] = drg_ref[...].astype(BF16)
        dp_ref[:, 1536:2048] = unrot_a(wide(dqa_ref)).astype(BF16)
        dp_ref[:, 2048:2560] = unrot_a(wide(dka_ref)).astype(BF16)
        dp_ref[:, 2560:3072] = wide(dva_ref).astype(BF16)

    row = lambda w: pl.BlockSpec((tm, w), lambda i: (i, 0))
    slab = pl.BlockSpec((4, tm, 128), lambda i: (0, i, 0))
    return pl.pallas_call(
        body, grid=(S // tm,), name="rot_bwd",
        in_specs=[row(768), row(768), row(256), row(256), row(512), row(512), slab, slab, slab],
        out_specs=row(PW), out_shape=jax.ShapeDtypeStruct((S, PW), BF16),
        compiler_params=_params("parallel"),
    )(cos, sin, dqr, dkr, drv, drg, dq_att, dk_att, dv_att)


def _win_bwd_w(h1, dproj, exchange, exchange_args):
    tm = 512

    def body(h_ref, dp_ref, dw_ref, acc, xc):
        k, i = pl.program_id(0), pl.program_id(1)

        @pl.when((k == 0) & (i == 0))
        def _():
            xc.start()

        @pl.when(i == 0)
        def _():
            acc[...] = jnp.zeros_like(acc)

        acc[...] += _tn(h_ref[...], dp_ref[...])

        @pl.when(i == S // tm - 1)
        def _():
            dw_ref[...] = acc[...].astype(BF16)

        @pl.when((k == N_CHIP - 1) & (i == S // tm - 1))
        def _():
            xc.middle()
            xc.finish()

    (dw,), out = _carry(
        "win_bwd_w", body, exchange, exchange_args, (h1, dproj),
        [pl.BlockSpec((tm, D), lambda k, i: (i, 0)), pl.BlockSpec((tm, WIN_C), lambda k, i: (i, k))],
        [pl.BlockSpec((None, D, WIN_C), lambda k, i: (k, 0, 0))],
        [jax.ShapeDtypeStruct((N_CHIP, D, WIN_C), BF16)],
        scratch_shapes=[pltpu.VMEM((D, WIN_C), F32)], grid=(N_CHIP, S // tm), semantics=("arbitrary", "arbitrary"))
    return dw, out


def _in_bwd(dproj, win_g, x, dx2, g1, after):
    tm = 512

    def body(dp_ref, w_ref, x_ref, dx2_ref, g_ref, dx_ref, dg_ref, _):
        @pl.when(pl.program_id(0) == 0)
        def _():
            dg_ref[...] = jnp.zeros_like(dg_ref)

        dh = _nt(dp_ref[:, 0:WIN_C], w_ref[0])
        for k in range(1, N_CHIP):
            dh = dh + _nt(dp_ref[:, k * WIN_C:(k + 1) * WIN_C], w_ref[k])
        xv = x_ref[...]
        r = _rstd(xv)
        xn = xv * r
        dg_ref[...] = dg_ref[...] + jnp.sum(dh * xn, axis=0, keepdims=True)
        t = dh * g_ref[...]
        dx_ref[...] = dx2_ref[...] + r * (t - xn * jnp.mean(t * xn, axis=-1, keepdims=True))

    row = lambda w: pl.BlockSpec((tm, w), lambda i: (i, 0))
    vec = pl.BlockSpec((1, D), lambda i: (0, 0))
    return _carry("in_bwd", body, _NoExchange(), (), (dproj, win_g, x, dx2, g1),
                  [row(PW), pl.BlockSpec((N_CHIP, D, WIN_C), lambda i: (0, 0, 0)), row(D), row(D), vec],
                  [row(D), vec], [jax.ShapeDtypeStruct((S, D), F32), jax.ShapeDtypeStruct((1, D), F32)],
                  grid=(S // tm,), semantics=("arbitrary",), after=after)[0]


ANY = pl.BlockSpec(memory_space=pl.ANY)
VMEM = pl.BlockSpec(memory_space=pltpu.VMEM)
FLIPS = ((1, 0), (0, 1), (1, 1))


def _place():
    x, y, c = lax.axis_index("x"), lax.axis_index("y"), lax.axis_index("c")
    chips = [((1 - x) if fx else x, (1 - y) if fy else y) for fx, fy in FLIPS]
    return x, y, c, 2 * x + y, chips


def _remote(src, dst, send_sem, recv_sem, device):
    return pltpu.make_async_remote_copy(src_ref=src, dst_ref=dst, send_sem=send_sem, recv_sem=recv_sem,
                                        device_id=device, device_id_type=MESH)


def _staggered(issue):
    c = lax.axis_index("c")

    @pl.when(c == 0)
    def _():
        issue((0, 1, 2))

    @pl.when(c == 1)
    def _():
        issue((1, 0, 2))


class _Exchange:
    aliases = {}

    def middle(self, ins, outs, sems):
        pass


class _GatherShards(_Exchange):
    def __init__(self, shards):
        n = self.n = len(shards)
        self.n_in = self.n_out = n
        self.out_shape = [jax.ShapeDtypeStruct((N_CHIP,) + s.shape, s.dtype) for s in shards]
        dma = pltpu.SemaphoreType.DMA
        self.scratch = [dma((3 * n,)), dma((3 * n,)), dma((3 * n,)), dma((3 * n,)), dma((n,)), dma((n,))]

    def _ici(self, ins, outs, sems, a, j, chip):
        x, y, c, me, chips = _place()
        half = ins[a].shape[0] // 2
        return _remote(ins[a].at[pl.ds(c * half, half), :], outs[a].at[me, pl.ds(c * half, half), :],
                       sems[0].at[3 * a + j], sems[1].at[3 * a + j], (*chip, c))

    def _fwd(self, outs, sems, a, j, chip, half_of):
        x, y, c, me, chips = _place()
        half = outs[a].shape[1] // 2
        blk = outs[a].at[2 * chip[0] + chip[1], pl.ds(half_of * half, half), :]
        return _remote(blk, blk, sems[2].at[3 * a + j], sems[3].at[3 * a + j], (x, y, 1 - c))

    def _own(self, ins, outs, sems, a):
        return _own_shard_to_sibling(ins[a], outs[a], sems[4].at[a], sems[5].at[a])

    def start(self, ins, outs, sems):
        chips = _place()[4]

        def issue(order):
            for a in range(self.n):
                for j in order:
                    self._ici(ins, outs, sems, a, j, chips[j]).start()

        _staggered(issue)
        for a in range(self.n):
            self._own(ins, outs, sems, a).start()

    def middle(self, ins, outs, sems):
        x, y, c, me, chips = _place()
        for a in range(self.n):
            for j, chip in enumerate(chips):
                half = outs[a].shape[1] // 2
                blk = outs[a].at[2 * chip[0] + chip[1], pl.ds(c * half, half), :]
                _remote(blk, blk, sems[0].at[3 * a + j], sems[1].at[3 * a + j], (x, y, c)).wait_recv()
                self._fwd(outs, sems, a, j, chip, c).start()

    def finish(self, ins, outs, sems):
        x, y, c, me, chips = _place()
        for a in range(self.n):
            for j, chip in enumerate(chips):
                self._fwd(outs, sems, a, j, chip, 1 - c).wait_recv()
        for a in range(self.n):
            for j, chip in enumerate(chips):
                self._ici(ins, outs, sems, a, j, chip).wait_send()
                self._fwd(outs, sems, a, j, chip, c).wait_send()
            self._own(ins, outs, sems, a).wait()


def _own_shard_to_sibling(shard_ref, gathered_ref, send_sem, recv_sem):
    x, y, c, me, chips = _place()
    return _remote(shard_ref, gathered_ref.at[me], send_sem, recv_sem, (x, y, 1 - c))


class _NoExchange(_Exchange):
    n_in = n_out = 0
    out_shape = ()
    scratch = ()

    def start(self, ins, outs, sems):
        pass

    def finish(self, ins, outs, sems):
        pass


class _ForwardGathered(_Exchange):
    def __init__(self, shards, own=True, forward=True):
        self.own, self.forward = own, forward
        n = self.n = len(shards)
        self.n_in, self.n_out = 2 * n, n
        self.out_shape = [jax.ShapeDtypeStruct((N_CHIP,) + s.shape, s.dtype) for s in shards]
        dma = pltpu.SemaphoreType.DMA
        self.scratch = [dma((3 * n,)), dma((3 * n,)), dma((n,)), dma((n,))]
        self.aliases = {n + a: a for a in range(n)}

    def _fwd(self, outs, sems, a, j, chip, half_of):
        x, y, c, me, chips = _place()
        half = outs[a].shape[1] // 2
        blk = outs[a].at[2 * chip[0] + chip[1], pl.ds(half_of * half, half), :]
        return _remote(blk, blk, sems[0].at[3 * a + j], sems[1].at[3 * a + j], (x, y, 1 - c))

    def _own(self, ins, outs, sems, a):
        return _own_shard_to_sibling(ins[a], outs[a], sems[2].at[a], sems[3].at[a])

    def start(self, ins, outs, sems):
        x, y, c, me, chips = _place()
        for a in range(self.n):
            for j, chip in enumerate(chips if self.forward else ()):
                self._fwd(outs, sems, a, j, chip, c).start()
        for a in range(self.n if self.own else 0):
            self._own(ins, outs, sems, a).start()

    def finish(self, ins, outs, sems):
        x, y, c, me, chips = _place()
        for a in range(self.n):
            for j, chip in enumerate(chips if self.forward else ()):
                self._fwd(outs, sems, a, j, chip, 1 - c).wait_recv()
        for a in range(self.n):
            for j, chip in enumerate(chips if self.forward else ()):
                self._fwd(outs, sems, a, j, chip, c).wait_send()
            if self.own:
                self._own(ins, outs, sems, a).wait()


HBM = pl.BlockSpec(memory_space=pltpu.HBM)
SEMS = pl.BlockSpec(memory_space=pltpu.SEMAPHORE)
DATAFLOW = pltpu.SideEffectType.DATAFLOW_SIDE_EFFECTING


class _OverIci:
    def __init__(self, name, sources, lands):
        self.name, self.n = name, len(sources)
        hbm = lambda t: pltpu.with_memory_space_constraint(t, pltpu.HBM)
        self.arrays = [hbm(t) for t in sources] + [hbm(t) for t in lands]

    def sent(self, src, land, a, chip):
        raise NotImplementedError

    def landed(self, land, a, chip):
        raise NotImplementedError

    def _copy(self, arr, sems, a, j, receiving):
        x, y, c, me, chips = _place()
        src, dst = self.sent(arr[a], arr[self.n + a], a, chips[j])
        if receiving:
            dst = self.landed(arr[self.n + a], a, chips[j])
        return _remote(src, dst, sems[0].at[3 * a + j], sems[1].at[3 * a + j], (*chips[j], c))

    def start(self, after):
        m = len(self.arrays)

        def body(*refs):
            arr, sems, token = refs[:m], refs[m + 1:m + 3], refs[-1]

            def issue(order):
                for a in range(self.n):
                    for j in order:
                        self._copy(arr, sems, a, j, False).start()

            _staggered(issue)
            token[...] = jnp.zeros_like(token)

        dma = pltpu.SemaphoreType.DMA
        outs = pl.pallas_call(
            body, name=self.name + "_start",
            out_shape=[dma((3 * self.n,)), dma((3 * self.n,))] + [pltpu.HBM(t.shape, t.dtype) for t in self.arrays]
                      + [jax.ShapeDtypeStruct((8, 128), F32)],
            in_specs=[HBM] * m + [ANY], out_specs=[SEMS, SEMS] + [HBM] * m + [VMEM],
            input_output_aliases={i: 2 + i for i in range(m)},
            compiler_params=pltpu.CompilerParams(has_side_effects=DATAFLOW),
        )(*self.arrays, after)
        self.sems, self.arrays = outs[0:2], list(outs[2:2 + m])
        return outs[-1]

    def wait(self, after):
        m = len(self.arrays)

        def body(*refs):
            arr, sems = refs[:m], refs[m:m + 2]
            for a in range(self.n):
                for j in range(3):
                    self._copy(arr, sems, a, j, False).wait_send()
                    self._copy(arr, sems, a, j, True).wait_recv()

        outs = pl.pallas_call(
            body, name=self.name + "_wait",
            out_shape=[pltpu.HBM(t.shape, t.dtype) for t in self.arrays],
            in_specs=[HBM] * m + [SEMS, SEMS, ANY], out_specs=[HBM] * m,
            input_output_aliases={i: i for i in range(m)},
            compiler_params=pltpu.CompilerParams(has_side_effects=DATAFLOW),
        )(*self.arrays, *self.sems, after)
        return list(outs[:self.n]), list(outs[self.n:])


class _GatherOverIci(_OverIci):
    def __init__(self, name, shards):
        super().__init__(name, shards, [lax.empty((N_CHIP,) + s.shape, s.dtype) for s in shards])

    @staticmethod
    def _half(ref):
        c = lax.axis_index("c")
        half = ref.shape[-2] // 2
        return pl.ds(c * half, half)

    def sent(self, src, land, a, chip):
        return src.at[self._half(src), :], land.at[_place()[3], self._half(src), :]

    def landed(self, land, a, chip):
        return land.at[2 * chip[0] + chip[1], self._half(land), :]


class _SumOverIci(_OverIci):
    def __init__(self, name, pre):
        super().__init__(name, pre, [lax.empty(p.shape, p.dtype) for p in pre])

    def sent(self, src, land, a, chip):
        return src.at[2 * chip[0] + chip[1]], land.at[_place()[3]]

    def landed(self, land, a, chip):
        return land.at[2 * chip[0] + chip[1]]


class _HalvesToSibling(_Exchange):
    def __init__(self, grads):
        n = self.n = len(grads)
        self.n_in = self.n_out = n
        self.out_shape = [jax.ShapeDtypeStruct((N_CHIP, g.shape[1] // 2, g.shape[2]), g.dtype) for g in grads]
        self.scratch = [pltpu.SemaphoreType.DMA((n,)), pltpu.SemaphoreType.DMA((n,))]

    def _copy(self, ins, outs, sems, a):
        x, y, c, me, chips = _place()
        half = ins[a].shape[1] // 2
        return _remote(ins[a].at[:, pl.ds((1 - c) * half, half), :], outs[a], sems[0].at[a], sems[1].at[a], (x, y, 1 - c))

    def start(self, ins, outs, sems):
        for a in range(self.n):
            self._copy(ins, outs, sems, a).start()

    def finish(self, ins, outs, sems):
        for a in range(self.n):
            self._copy(ins, outs, sems, a).wait_recv()
        for a in range(self.n):
            self._copy(ins, outs, sems, a).wait_send()


class _OverChips(_Exchange):
    def __init__(self, pre):
        n = self.n = len(pre)
        self.n_in = self.n_out = n
        self.out_shape = [jax.ShapeDtypeStruct(p.shape, p.dtype) for p in pre]
        dma = pltpu.SemaphoreType.DMA
        self.scratch = [dma((3 * n,)), dma((3 * n,))]

    def _ici(self, ins, outs, sems, a, j, chip):
        x, y, c, me, chips = _place()
        return _remote(ins[a].at[2 * chip[0] + chip[1]], outs[a].at[me], sems[0].at[3 * a + j], sems[1].at[3 * a + j],
                       (*chip, c))

    def start(self, ins, outs, sems):
        chips = _place()[4]

        def issue(order):
            for a in range(self.n):
                for j in order:
                    self._ici(ins, outs, sems, a, j, chips[j]).start()

        _staggered(issue)

    def finish(self, ins, outs, sems):
        x, y, c, me, chips = _place()
        for a in range(self.n):
            for j, chip in enumerate(chips):
                blk = outs[a].at[2 * chip[0] + chip[1]]
                _remote(blk, blk, sems[0].at[3 * a + j], sems[1].at[3 * a + j], (x, y, c)).wait_recv()
        for a in range(self.n):
            for j, chip in enumerate(chips):
                self._ici(ins, outs, sems, a, j, chip).wait_send()


class _ShareHalves(_Exchange):
    def __init__(self, fulls):
        n = self.n = len(fulls)
        self.n_in = self.n_out = n
        self.out_shape = [jax.ShapeDtypeStruct(f.shape, f.dtype) for f in fulls]
        self.scratch = [pltpu.SemaphoreType.DMA((n,)), pltpu.SemaphoreType.DMA((n,))]
        self.aliases = {a: a for a in range(n)}

    def _copy(self, outs, sems, a, half_of):
        x, y, c, me, chips = _place()
        half = outs[a].shape[0] // 2
        rows = outs[a].at[pl.ds(half_of * half, half), :]
        return _remote(rows, rows, sems[0].at[a], sems[1].at[a], (x, y, 1 - c))

    def start(self, ins, outs, sems):
        c = _place()[2]
        for a in range(self.n):
            self._copy(outs, sems, a, c).start()

    def finish(self, ins, outs, sems):
        c = _place()[2]
        for a in range(self.n):
            self._copy(outs, sems, a, 1 - c).wait_recv()
        for a in range(self.n):
            self._copy(outs, sems, a, c).wait_send()


class _GatherBlocks(_Exchange):
    def __init__(self, block):
        self.n_in = self.n_out = 1
        self.out_shape = [jax.ShapeDtypeStruct((8,) + block.shape, block.dtype)]
        dma = pltpu.SemaphoreType.DMA
        self.scratch = [dma((7,)), dma((7,)), dma]

    @staticmethod
    def _peer(f):
        x, y, c, me, chips = _place()
        return ((1 - x) if f & 4 else x, (1 - y) if f & 2 else y, (1 - c) if f & 1 else c)

    def start(self, ins, outs, sems):
        x, y, c, me, chips = _place()
        for f in range(1, 8):
            _remote(ins[0], outs[0].at[2 * me + c], sems[0].at[f - 1], sems[1].at[f - 1], self._peer(f)).start()
        pltpu.make_async_copy(ins[0], outs[0].at[2 * me + c], sems[2]).start()

    def finish(self, ins, outs, sems):
        x, y, c, me, chips = _place()
        for f in range(1, 8):
            px, py, pc = self._peer(f)
            blk = outs[0].at[4 * px + 2 * py + pc]
            _remote(blk, blk, sems[0].at[f - 1], sems[1].at[f - 1], (x, y, c)).wait_recv()
        for f in range(1, 8):
            _remote(ins[0], outs[0].at[2 * me + c], sems[0].at[f - 1], sems[1].at[f - 1], self._peer(f)).wait_send()
        pltpu.make_async_copy(ins[0], outs[0].at[2 * me + c], sems[2]).wait()


class _Both(_Exchange):
    def __init__(self, first, second):
        self.parts = (first, second)
        self.n_in, self.n_out = first.n_in + second.n_in, first.n_out + second.n_out
        self.out_shape = first.out_shape + second.out_shape
        self.scratch = first.scratch + second.scratch
        self.aliases = dict(first.aliases)
        self.aliases.update({first.n_in + i: first.n_out + o for i, o in second.aliases.items()})

    def _split(self, ins, outs, sems):
        a, b = self.parts
        return ((a, ins[:a.n_in], outs[:a.n_out], sems[:len(a.scratch)]),
                (b, ins[a.n_in:], outs[a.n_out:], sems[len(a.scratch):]))

    def start(self, ins, outs, sems):
        for ex, i, o, s in self._split(ins, outs, sems):
            ex.start(i, o, s)

    def middle(self, ins, outs, sems):
        for ex, i, o, s in self._split(ins, outs, sems):
            ex.middle(i, o, s)

    def finish(self, ins, outs, sems):
        for ex, i, o, s in self._split(ins, outs, sems):
            ex.finish(i, o, s)


class _Bound:
    def __init__(self, ex, ins, outs, sems):
        self.start = lambda: ex.start(ins, outs, sems)
        self.middle = lambda: ex.middle(ins, outs, sems)
        self.finish = lambda: ex.finish(ins, outs, sems)


def _carry(name, body, ex, ex_args, args, in_specs, out_specs, out_shape, scratch_shapes=(), grid=None, semantics=(),
           after=None):
    n_a, n_o, n_s = len(args), len(out_shape), len(scratch_shapes)
    behind = [] if after is None else [after]

    def full_body(*refs):
        p = 0
        groups = []
        for size in (n_a, ex.n_in, len(behind), n_o, ex.n_out, n_s, len(ex.scratch)):
            groups.append(refs[p:p + size])
            p += size
        a, ei, _, o, eo, s, es = groups
        body(*a, *o, *s, _Bound(ex, ei, eo, es))

    kwargs = {} if grid is None else {"grid": grid}
    outs = pl.pallas_call(
        full_body, name=name,
        in_specs=list(in_specs) + [ANY] * (ex.n_in + len(behind)), out_specs=list(out_specs) + [ANY] * ex.n_out,
        out_shape=list(out_shape) + list(ex.out_shape), scratch_shapes=list(scratch_shapes) + list(ex.scratch),
        input_output_aliases={n_a + i: n_o + o for i, o in ex.aliases.items()},
        compiler_params=_params(*semantics) if semantics else pltpu.CompilerParams(vmem_limit_bytes=VMEM_LIMIT),
        **kwargs,
    )(*args, *ex_args, *behind)
    return outs[:n_o], outs[n_o:]


def _exchange_alone(name, ex, ex_args):
    def body(xc):
        xc.start()
        xc.middle()
        xc.finish()

    return _carry(name, body, ex, ex_args, (), (), (), ())[1]


def _core_index():
    return lax.axis_index("c").astype(jnp.int32).reshape(1)


def _pair_sum(gs, gots):
    n = len(gs)
    _, r, cc = gs[0].shape
    half = r // 2

    def body(c_ref, *refs):
        for a in range(n):
            refs[2 * n + a][...] = (refs[a][...].astype(F32) + refs[n + a][...].astype(F32)).astype(BF16)

    mine = pl.BlockSpec((None, half, cc), lambda k, c_ref: (k, c_ref[0], 0))
    blk = pl.BlockSpec((None, half, cc), lambda k, c_ref: (k, 0, 0))
    return pl.pallas_call(
        body, name=f"pair_sum_{r}x{cc}",
        grid_spec=pltpu.PrefetchScalarGridSpec(
            num_scalar_prefetch=1, grid=(N_CHIP,), in_specs=[mine] * n + [blk] * n, out_specs=[blk] * n),
        out_shape=[jax.ShapeDtypeStruct((N_CHIP, half, cc), BF16)] * n,
        compiler_params=_params("parallel"),
    )(_core_index(), *gs, *gots)


def _chip_sum(pre, parts):
    n = len(parts)
    _, half, cc = parts[0].shape
    tr = half // 2
    me = 2 * lax.axis_index("x") + lax.axis_index("y")
    others = [k + (k >= me).astype(jnp.int32) for k in range(3)]
    where = jnp.stack([lax.axis_index("c"), me, *others]).astype(jnp.int32)

    def body(w_ref, *refs):
        for a in range(n):
            own, p1, p2, p3 = refs[4 * a:4 * a + 4]
            refs[4 * n + a][...] = ((own[...].astype(F32) + p1[...].astype(F32)) + p2[...].astype(F32)) + p3[...].astype(F32)

    slot = lambda s: pl.BlockSpec((None, tr, cc), lambda i, w_ref: (w_ref[s], i, 0))
    operands = []
    for a in range(n):
        operands += [pre[a], parts[a], parts[a], parts[a]]
    return pl.pallas_call(
        body, name=f"chip_sum_{half}x{cc}",
        grid_spec=pltpu.PrefetchScalarGridSpec(
            num_scalar_prefetch=1, grid=(2,),
            in_specs=[slot(1), slot(2), slot(3), slot(4)] * n,
            out_specs=[pl.BlockSpec((tr, cc), lambda i, w_ref: (2 * w_ref[0] + i, 0))] * n),
        out_shape=[jax.ShapeDtypeStruct((2 * half, cc), F32)] * n,
        compiler_params=_params("parallel"),
    )(where, *operands)


def _adamw_math(w, g, m, v):
    m = ADAM_B1 * m + (1.0 - ADAM_B1) * g
    v = ADAM_B2 * v + (1.0 - ADAM_B2) * (g * g)
    m_hat = m / (1.0 - ADAM_B1 ** ADAM_STEP)
    v_hat = v / (1.0 - ADAM_B2 ** ADAM_STEP)
    delta = -ADAM_LR * (m_hat / (jnp.sqrt(v_hat) + ADAM_EPS) + ADAM_WD * w)
    return delta, m, v


def _adamw(w, g, m, v, after=None):
    r, cc = w.shape
    tr = r // 4

    def body(w_ref, g_ref, m_ref, v_ref, go_ref, d_ref, nm_ref, nv_ref, _):
        g = g_ref[...]
        go_ref[...] = g
        d_ref[...], nm_ref[...], nv_ref[...] = _adamw_math(w_ref[...], g, m_ref[...], v_ref[...])

    blk = pl.BlockSpec((tr, cc), lambda i: (i, 0))
    return _carry(f"adamw_{r}x{cc}", body, _NoExchange(), (), (w, g, m, v), [blk] * 4, [blk] * 4,
                  [jax.ShapeDtypeStruct((r, cc), F32)] * 4, grid=(4,), semantics=("parallel",), after=after)[0]


def _pack8(rows):
    def body(*refs):
        out_ref = refs[-1]
        out_ref[...] = jnp.zeros_like(out_ref)
        for i, r in enumerate(refs[:-1]):
            out_ref[i:i + 1, :] = r[...]

    return pl.pallas_call(body, name="pack8", out_shape=jax.ShapeDtypeStruct((8, D), F32))(*rows)


def _adamw_gains(gall, w8, m8, v8):
    def body(ga_ref, w_ref, m_ref, v_ref, g_ref, d_ref, nm_ref, nv_ref):
        g = ga_ref[0]
        for dev in range(1, 8):
            g = g + ga_ref[dev]
        g_ref[...] = g
        d_ref[...], nm_ref[...], nv_ref[...] = _adamw_math(w_ref[...], g, m_ref[...], v_ref[...])

    return pl.pallas_call(
        body, name="adamw_gains",
        out_shape=[jax.ShapeDtypeStruct((8, D), F32)] * 4,
    )(gall, w8, m8, v8)


def kernel(x, positions, w_in, w_out, g_pre_mix, g_post_mix, g_pre_ffn, g_post_ffn, w_gate, w_up, w_down, loss_target, m_w_in, m_w_out, m_g_pre_mix, m_g_post_mix, m_g_pre_ffn, m_g_post_ffn, m_w_gate, m_w_up, m_w_down, v_w_in, v_w_out, v_g_pre_mix, v_g_post_mix, v_g_pre_ffn, v_g_post_ffn, v_w_gate, v_w_up, v_w_down):
    tr = lambda t: jnp.swapaxes(t, 1, 2)[0]
    shards = [w_in[0], w_out[0], tr(w_gate), tr(w_up), w_down[0]]
    moms = [m_w_in[0], m_w_out[0], tr(m_w_gate), tr(m_w_up), m_w_down[0]]
    vels = [v_w_in[0], v_w_out[0], tr(v_w_gate), tr(v_w_up), v_w_down[0]]
    xs, pos, tgt = x[0], positions.reshape(S, 1), loss_target[0]
    g1, g2, g3, g4 = g_pre_mix, g_post_mix, g_pre_ffn, g_post_ffn
    tabs = tuple(jnp.asarray(t) for t in _retention_tables())
    ifc, spread = _rotary_tables()
    ifc, spread = jnp.asarray(ifc), jnp.asarray(spread, dtype=BF16)
    bf = [s.astype(BF16) for s in shards]

    win_g, wout_g = _exchange_alone("gather_in", _GatherShards(bf[:2]), bf[:2])
    wout_g = wout_g.reshape(D, D)
    ffn_gather = _GatherOverIci("ffn_gather", bf[2:])
    token = ffn_gather.start(win_g)
    h1, qr, kr, rv, rg, aq, ak, av, cos, sin = _proj_fwd(xs, g1, win_g, pos, ifc, spread, token)
    (o_raw, cat_r, states), _ = _ret_fwd(qr, kr, rv, rg, tabs, _NoExchange(), ())
    n_ffn = len(bf[2:])
    (att_out, lse, cat_a), ffn_gather.arrays[n_ffn:] = _att_fwd(
        aq, ak, av, _ForwardGathered(bf[2:], forward=False), ffn_gather.arrays)
    ffn_sh, ffn_lands = ffn_gather.wait(cat_a)
    (mix, x2, h3), (wg_g, wu_g, wd_g) = _mix_fwd(cat_r, cat_a, wout_g, xs, g2, g3,
                                                _ForwardGathered(bf[2:], own=False), [*ffn_sh, *ffn_lands])
    gt, up, a, f = _ffn_fwd(h3, wg_g, wu_g, wd_g)

    sq, dy, df, dg4 = _head_bwd(f, x2, tgt, g4)
    dgt, dup, dh3 = _ffn_bwd_act(df, gt, up, wg_g, wu_g, wd_g)
    ffn_grads = list(_ffn_bwd_w(a, df, h3, dgt, dup))
    (dx2, dmix, dg3, dg2), got = _norm_bwd(dh3, dy, x2, mix, g2, g3, _HalvesToSibling(ffn_grads), ffn_grads)
    ffn_sum = _SumOverIci("ffn_sum", _pair_sum(ffn_grads, got))
    token = ffn_sum.start(dmix)
    dret, datt, dwout = _mix_bwd(dmix, cat_r, cat_a, wout_g, token)
    (dq_att, dk_att, dv_att), _ = _att_bwd(aq, ak, av, datt, att_out, lse, _NoExchange(), ())
    (dqr, dkr, drv, drg), _ = _ret_bwd(qr, kr, rv, rg, o_raw, states, dret, tabs, _NoExchange(), ())
    sums = _chip_sum(*ffn_sum.wait(dqr))
    dproj = _rot_bwd(cos, sin, dqr, dkr, drv, drg, dq_att, dk_att, dv_att)
    dwin, ffn_full = _win_bwd_w(h1, dproj, _ShareHalves(sums), sums)
    in_grads = [dwin, dwout.reshape(N_CHIP, WOUT_R, D)]

    got = _exchange_alone("halves_to_sibling", _HalvesToSibling(in_grads), in_grads)
    in_sum = _SumOverIci("in_sum", [*_pair_sum(in_grads[:1], got[:1]), *_pair_sum(in_grads[1:], got[1:])])
    token = in_sum.start(dproj)
    dx, dg1 = _in_bwd(dproj, win_g, xs, dx2, g1, token)
    ffn_upd = [_adamw(shards[2 + i], ffn_full[o], moms[2 + i], vels[2 + i], token)
               for i, o in enumerate((1, 2, 0))]
    pre, parts = in_sum.wait(ffn_upd[2][0])
    sums = [*_chip_sum(pre[:1], parts[:1]), *_chip_sum(pre[1:], parts[1:])]
    gblock = _pack8([dg1, dg2, dg3, dg4, sq])
    *in_full, gall = _exchange_alone("share_rest", _Both(_ShareHalves(sums), _GatherBlocks(gblock)), [*sums, gblock])
    upd = [_adamw(w, g, m, v) for w, g, m, v in zip(shards[:2], in_full, moms[:2], vels[:2])] + ffn_upd
    gg, gd, gm, gv = _adamw_gains(gall, _pack8([g1, g2, g3, g4]),
                                  _pack8([m_g_pre_mix, m_g_post_mix, m_g_pre_ffn, m_g_post_ffn]),
                                  _pack8([v_g_pre_mix, v_g_post_mix, v_g_pre_ffn, v_g_post_ffn]))
    loss = 0.5 * gg[4, 0] / D

    def order(mats, vecs):
        back = lambda t: jnp.swapaxes(t[None], 1, 2)
        return ([mats[0][None], mats[1][None]] + [vecs[i:i + 1] for i in range(4)]
                + [back(mats[2]), back(mats[3]), mats[4][None]])

    return (loss, dx[None],
            *order([u[0] for u in upd], gg),
            *order([u[1] for u in upd], gd),
            *order([u[2] for u in upd], gm),
            *order([u[3] for u in upd], gv))
```

```python
import functools

import numpy as np
import jax
import jax.numpy as jnp
from jax import lax
from jax.experimental import pallas as pl
from jax.experimental.pallas import tpu as pltpu

F32, BF16 = jnp.float32, jnp.bfloat16
MESH = pl.DeviceIdType.MESH

S = 2048
D = 1024
PW = 3072
N_CHIP = 4
WIN_C = PW // N_CHIP
DFF = 2816
FF_C = DFF // N_CHIP
WOUT_R = D // N_CHIP
RMS_EPS = 1e-6
GN_EPS = 1e-5
RET_C = 128
RET_SCALE = 32 ** -0.5
ATT_BLK = 128
ATT_SCALE = 64 ** -0.5
PATTERN_DILATIONS = (1, 4, 16)
NEG = -1e30
VMEM_LIMIT = 56 * 1024 * 1024

ADAM_LR, ADAM_B1, ADAM_B2, ADAM_EPS, ADAM_WD, ADAM_STEP = 0.001, 0.9, 0.999, 1e-08, 0.01, 10


def _params(*sem):
    return pltpu.CompilerParams(dimension_semantics=sem, vmem_limit_bytes=VMEM_LIMIT)


def _nt(a, b):
    return lax.dot_general(a, b, (((1,), (1,)), ((), ())), preferred_element_type=F32)


def _tn(a, b):
    return lax.dot_general(a, b, (((0,), (0,)), ((), ())), preferred_element_type=F32)


def _nn(a, b):
    return jnp.dot(a, b, preferred_element_type=F32)


def _rstd(v):
    return lax.rsqrt(jnp.mean(v * v, axis=-1, keepdims=True) + RMS_EPS)


def _sigmoid(v):
    return 1.0 / (1.0 + jnp.exp(-v))


def _rows(i, t):
    return pl.ds(pl.multiple_of(i * t, t), t)


def _retention_tables():
    h = np.arange(8, dtype=np.float32)
    log_g = np.log1p(-np.exp2(-5.0 - h)).astype(np.float32)
    idx = np.arange(RET_C, dtype=np.float32)
    diff = idx[:, None] - idx[None, :]
    dtab = np.where(diff >= 0, np.exp(log_g[:, None, None] * np.maximum(diff, 0.0)), 0.0).astype(np.float32)
    dtab = dtab.reshape(8 * RET_C, RET_C)
    lane_head = np.arange(256) // 32
    a_tab = np.exp(log_g[lane_head][None, :] * (idx + 1.0)[:, None]).astype(np.float32)
    b_tab = np.exp(log_g[lane_head][None, :] * (RET_C - 1.0 - idx)[:, None]).astype(np.float32)
    lam = np.exp(log_g[lane_head] * RET_C).astype(np.float32)[:, None]
    bd = (lane_head[:, None] == (np.arange(512) // 64)[None, :]).astype(np.float32)
    return dtab, a_tab, b_tab, lam, bd


def _rotary_tables():
    inv_r = (1.0 / (np.float32(10000.0) ** np.linspace(0.0, 1.0, 16, dtype=np.float32))).astype(np.float32)
    inv_a = (np.float32(500000.0) ** (-np.arange(0, 16, 2, dtype=np.float32) / np.float32(16))).astype(np.float32)
    ifc = np.zeros((1, 128), np.float32)
    ifc[0, 0:16], ifc[0, 16:24] = inv_r, inv_a
    spread = np.zeros((128, 768), np.float32)
    for lane in range(256):
        spread[(lane % 32) % 16, lane] = 1.0
    for lane in range(512):
        d = lane % 64
        spread[16 + d % 8 if d < 16 else 24, 256 + lane] = 1.0
    return ifc, spread


def _rot_halves(tm):
    lo_r = (lax.broadcasted_iota(jnp.int32, (tm, 256), 1) % 32) < 16
    lo_a = (lax.broadcasted_iota(jnp.int32, (tm, 512), 1) % 64) < 8
    return lo_r, lo_a


def _spread_exact(t, e):
    hi = t.astype(BF16)
    r1 = t - hi.astype(F32)
    mid = r1.astype(BF16)
    lo = (r1 - mid.astype(F32)).astype(BF16)
    return _nn(hi, e) + _nn(mid, e) + _nn(lo, e)


def _proj_fwd(x, g1, win_g, pos, ifc, spread, after):
    tm = 256

    def body(x_ref, g_ref, w_ref, pos_ref, ifc_ref, e_ref,
             h_ref, qr_ref, kr_ref, rv_ref, rg_ref, aq_ref, ak_ref, av_ref, cos_ref, sin_ref, p_ref, _):
        xv = x_ref[...]
        h = (xv * _rstd(xv) * g_ref[...]).astype(BF16)
        h_ref[...] = h
        for k in range(N_CHIP):
            p_ref[:, k * WIN_C:(k + 1) * WIN_C] = _nn(h, w_ref[k])
        ang = pos_ref[...].astype(F32) * ifc_ref[...]
        cs = _spread_exact(jnp.cos(ang), e_ref[...])
        sn = _spread_exact(jnp.sin(ang), e_ref[...])
        cos_ref[...] = cs
        sin_ref[...] = sn
        cr, ca, sr, sa = cs[:, 0:256], cs[:, 256:768], sn[:, 0:256], sn[:, 256:768]
        lo_r, lo_a = _rot_halves(tm)

        def rot_r(v):
            return v * cr + sr * jnp.where(lo_r, -pltpu.roll(v, 240, 1), pltpu.roll(v, 16, 1))

        def rot_a(v):
            return v * ca + sa * jnp.where(lo_a, -pltpu.roll(v, 504, 1), pltpu.roll(v, 8, 1))

        qr_ref[...] = rot_r(p_ref[:, 0:256]).astype(BF16)
        kr_ref[...] = (rot_r(p_ref[:, 256:512]) * RET_SCALE).astype(BF16)
        rv_ref[...] = p_ref[:, 512:1024].astype(BF16)
        rg_ref[...] = p_ref[:, 1024:1536]
        aq, ak = rot_a(p_ref[:, 1536:2048]), rot_a(p_ref[:, 2048:2560])
        for j in range(4):
            aq_ref[j] = aq[:, 128 * j:128 * j + 128]
            ak_ref[j] = ak[:, 128 * j:128 * j + 128]
            av_ref[j] = p_ref[:, 2560 + 128 * j:2560 + 128 * j + 128]

    row = lambda w: pl.BlockSpec((tm, w), lambda i: (i, 0))
    const = lambda w: pl.BlockSpec((1, w), lambda i: (0, 0))
    slab = pl.BlockSpec((4, tm, 128), lambda i: (0, i, 0))
    return _carry(
        "proj_fwd", body, _NoExchange(), (), (x, g1, win_g, pos, ifc, spread),
        [row(D), const(D), pl.BlockSpec((N_CHIP, D, WIN_C), lambda i: (0, 0, 0)), row(1), const(128),
         pl.BlockSpec((128, 768), lambda i: (0, 0))],
        [row(D), row(256), row(256), row(512), row(512), slab, slab, slab, row(768), row(768)],
        [jax.ShapeDtypeStruct((S, D), BF16)] + [jax.ShapeDtypeStruct((S, w), BF16) for w in (256, 256, 512)]
        + [jax.ShapeDtypeStruct((S, 512), F32)] + [jax.ShapeDtypeStruct((4, S, 128), F32)] * 3
        + [jax.ShapeDtypeStruct((S, 768), F32)] * 2,
        scratch_shapes=[pltpu.VMEM((tm, PW), F32)], grid=(S // tm,), semantics=("parallel",), after=after)[0]


def _seg_mean(v):
    lo = lax.broadcasted_iota(jnp.int32, v.shape, 1) < 64
    s_lo = jnp.sum(jnp.where(lo, v, 0.0), axis=-1, keepdims=True)
    s_hi = jnp.sum(jnp.where(lo, 0.0, v), axis=-1, keepdims=True)
    return jnp.where(lo, s_lo, s_hi) * (1.0 / 64.0)


def _ret_fwd(qr, kr, rv, proj, tabs, exchange, exchange_args):
    C = RET_C
    dtab, a_tab, b_tab, lam, bd = tabs

    def body(q_ref, k_ref, v_ref, g_ref, dt_ref, a_ref, b_ref, lam_ref, bd_ref, o_ref, cat_ref, st_ref, R, exch):
        @pl.when(pl.program_id(0) == 0)
        def _():
            exch.start()
            R[...] = jnp.zeros_like(R)

        @pl.when(pl.program_id(0) == S // C // 2)
        def _():
            exch.middle()

        q, k, v = q_ref[...], k_ref[...], v_ref[...]
        lane_head = lax.broadcasted_iota(jnp.int32, (C, 256), 1) // 32
        col_head = lax.broadcasted_iota(jnp.int32, (C, 256), 1) // 64
        rb = R[...].astype(BF16)
        st_ref[...] = rb
        qa = (q.astype(F32) * a_ref[...]).astype(BF16)
        cross = _nn(qa, rb)
        p = (_nt(_stack_heads(q, lane_head, n=8), k) * dt_ref[...]).astype(BF16)
        og = [cross[:, 256 * g:256 * g + 256]
              + _unstack_heads(_nn(p[4 * C * g:4 * C * (g + 1)], v[:, 256 * g:256 * g + 256]), col_head)
              for g in range(2)]
        kb = (k.astype(F32) * b_ref[...]).astype(BF16)
        R[...] = R[...] * lam_ref[...] + _tn(kb, v) * bd_ref[...]
        o_ref[:, 0:256] = og[0]
        o_ref[:, 256:512] = og[1]
        for j in range(4):
            oj = og[j // 2][:, 128 * (j % 2):128 * (j % 2) + 128]
            xc = oj - _seg_mean(oj)
            rn = xc * lax.rsqrt(_seg_mean(xc * xc) + GN_EPS)
            gj = g_ref[:, 128 * j:128 * j + 128]
            cat_ref[:, 128 * j:128 * j + 128] = (rn * (gj * _sigmoid(gj))).astype(BF16)

        @pl.when(pl.program_id(0) == S // C - 1)
        def _():
            exch.finish()

    row = lambda w: pl.BlockSpec((C, w), lambda n: (n, 0))
    full = lambda a: pl.BlockSpec(a.shape, lambda n: (0,) * a.ndim)
    return _carry(
        "ret_fwd", body, exchange, exchange_args, (qr, kr, rv, proj, dtab, a_tab, b_tab, lam, bd),
        [row(256), row(256), row(512), row(512),
         full(dtab), full(a_tab), full(b_tab), full(lam), full(bd)],
        [row(512), row(512), pl.BlockSpec((None, 256, 512), lambda n: (n, 0, 0))],
        [jax.ShapeDtypeStruct((S, 512), F32), jax.ShapeDtypeStruct((S, 512), BF16),
         jax.ShapeDtypeStruct((S // C, 256, 512), BF16)],
        scratch_shapes=[pltpu.VMEM((256, 512), F32)], grid=(S // C,), semantics=("arbitrary",))


def _stack_heads(v, lane_head, fill=0.0, n=4):
    return jnp.concatenate([jnp.where(lane_head == h, v, jnp.full_like(v, fill)) for h in range(n)], axis=0)


def _unstack_heads(v, lane_head, n=4):
    out = v[0:ATT_BLK]
    for h in range(1, n):
        out = jnp.where(lane_head == h, v[h * ATT_BLK:(h + 1) * ATT_BLK], out)
    return out


def _att_bias(has_prev):
    nk = 2 * ATT_BLK if has_prev else ATT_BLK
    a = lax.broadcasted_iota(jnp.int32, (4 * ATT_BLK, nk), 0) % ATT_BLK
    kk = lax.broadcasted_iota(jnp.int32, (4 * ATT_BLK, nk), 1)
    if not has_prev:
        return None, jnp.where((a - kk) >= 0, 0.0, NEG)
    dist = ATT_BLK + a - kk
    inside = (dist >= 0) & (dist <= ATT_BLK)
    return jnp.where(inside, 0.0, NEG), jnp.where(inside & (kk >= ATT_BLK), 0.0, NEG)


def _class_rows(ib, r, d):
    if d == 1:
        return pl.ds(pl.multiple_of(ib * ATT_BLK, ATT_BLK), ATT_BLK)
    return pl.ds(ib * ATT_BLK * d + r, ATT_BLK, stride=d)


def _slab_pair(ref, g, rows):
    return jnp.concatenate([ref[2 * g, rows, :], ref[2 * g + 1, rows, :]], axis=1)


def _att_blocks(d):
    nb = S // d // ATT_BLK
    return nb, nb > 1


def _att_fwd(aq, ak, av, exchange, exchange_args):
    def body(q_ref, k_ref, v_ref, o_ref, l_ref, cat_ref, xc):
        xc.start()
        lane_head = lax.broadcasted_iota(jnp.int32, (ATT_BLK, 256), 1) // 64
        for pi, d in enumerate(PATTERN_DILATIONS):
            if pi == len(PATTERN_DILATIONS) - 1:
                xc.middle()
            nb, has_prev = _att_blocks(d)
            bias_rest, bias_first = _att_bias(has_prev)

            def block(b, carry, pi=pi, d=d, nb=nb, has_prev=has_prev, bias_rest=bias_rest, bias_first=bias_first):
                r, ib = b // nb, b % nb
                rows = _class_rows(ib, r, d)
                prow = _class_rows(jnp.maximum(ib - 1, 0), r, d)
                bias = jnp.where(ib == 0, bias_first, bias_rest) if has_prev else bias_first
                for g in range(2):
                    qg = _slab_pair(q_ref, g, rows).astype(BF16)
                    kg = _slab_pair(k_ref, g, rows)
                    vg = _slab_pair(v_ref, g, rows)
                    if has_prev:
                        kg = jnp.concatenate([_slab_pair(k_ref, g, prow), kg], axis=0)
                        vg = jnp.concatenate([_slab_pair(v_ref, g, prow), vg], axis=0)
                    kg, vg = kg.astype(BF16), vg.astype(BF16)
                    s = _nt(_stack_heads(qg, lane_head), kg) * ATT_SCALE + bias
                    m = jnp.max(s, axis=-1, keepdims=True)
                    p = jnp.exp(s - m)
                    den = jnp.sum(p, axis=-1, keepdims=True)
                    og = _unstack_heads(_nn(p.astype(BF16), vg) / den, lane_head)
                    lg = _unstack_heads(jnp.broadcast_to(m + jnp.log(den), (4 * ATT_BLK, 256)), lane_head)
                    for jj in range(2):
                        j = 2 * g + jj
                        o_new, l_new = og[:, 128 * jj:128 * jj + 128], lg[:, 128 * jj:128 * jj + 128]
                        if pi > 0:
                            o_old, l_old = o_ref[j, rows, :], l_ref[j, rows, :]
                            mx = jnp.maximum(l_old, l_new)
                            ea, eb = jnp.exp(l_old - mx), jnp.exp(l_new - mx)
                            den = ea + eb
                            o_new = (ea * o_old + eb * o_new) / den
                            l_new = mx + jnp.log(den)
                        o_ref[j, rows, :] = o_new
                        l_ref[j, rows, :] = l_new
                return carry

            lax.fori_loop(0, S // ATT_BLK, block, 0)

        def to_cat(i, carry):
            rows = _rows(i, 256)
            for j in range(4):
                cat_ref[rows, 128 * j:128 * j + 128] = o_ref[j, rows, :].astype(BF16)
            return carry

        lax.fori_loop(0, S // 256, to_cat, 0)
        xc.finish()

    slab = jax.ShapeDtypeStruct((4, S, 128), F32)
    return _carry("att_fwd", body, exchange, exchange_args, (aq, ak, av), [VMEM] * 3, [VMEM] * 3,
                  [slab, slab, jax.ShapeDtypeStruct((S, 512), BF16)])


def _mix_fwd(cat_r, cat_a, wout, x, g2, g3, exchange, exchange_args):
    tm = 512

    def body(cr_ref, ca_ref, w_ref, x_ref, g2_ref, g3_ref, mix_ref, x2_ref, h3_ref, xc):
        @pl.when(pl.program_id(0) == 0)
        def _():
            xc.start()

        mix = _nn(cr_ref[...], w_ref[0:512, :]) + _nn(ca_ref[...], w_ref[512:1024, :])
        mix_ref[...] = mix
        x2 = x_ref[...] + mix * _rstd(mix) * g2_ref[...]
        x2_ref[...] = x2
        h3_ref[...] = (x2 * _rstd(x2) * g3_ref[...]).astype(BF16)

        @pl.when(pl.program_id(0) == S // tm - 1)
        def _():
            xc.middle()
            xc.finish()

    row = lambda w: pl.BlockSpec((tm, w), lambda i: (i, 0))
    vec = pl.BlockSpec((1, D), lambda i: (0, 0))
    return _carry("mix_fwd", body, exchange, exchange_args, (cat_r, cat_a, wout, x, g2, g3),
                  [row(512), row(512), pl.BlockSpec((D, D), lambda i: (0, 0)), row(D), vec, vec],
                  [row(D), row(D), row(D)],
                  [jax.ShapeDtypeStruct((S, D), F32), jax.ShapeDtypeStruct((S, D), F32),
                   jax.ShapeDtypeStruct((S, D), BF16)],
                  grid=(S // tm,), semantics=("arbitrary",))


def _ffn_fwd(h3, wg, wu, wd):
    tm = 512

    def body(h_ref, wg_ref, wu_ref, wd_ref, gt_ref, up_ref, a_ref, f_ref):
        k, i = pl.program_id(0), pl.program_id(1)
        h = h_ref[...]
        gt = _nt(h, wg_ref[...])
        up = _nt(h, wu_ref[...])
        gt_ref[...] = gt.astype(BF16)
        up_ref[...] = up.astype(BF16)
        a = (gt * _sigmoid(gt) * up).astype(BF16)
        a_ref[...] = a
        part = _nn(a, wd_ref[...])
        rows = _rows(i, tm)

        @pl.when(k == 0)
        def _():
            f_ref[rows, :] = part

        @pl.when(k > 0)
        def _():
            f_ref[rows, :] = f_ref[rows, :] + part

    wrow = pl.BlockSpec((None, FF_C, D), lambda k, i: (k, 0, 0))
    act = pl.BlockSpec((None, tm, FF_C), lambda k, i: (k, i, 0))
    return pl.pallas_call(
        body, grid=(N_CHIP, S // tm), name="ffn_fwd",
        in_specs=[pl.BlockSpec((tm, D), lambda k, i: (i, 0)), wrow, wrow, wrow],
        out_specs=[act, act, act, pl.BlockSpec((S, D), lambda k, i: (0, 0))],
        out_shape=[jax.ShapeDtypeStruct((N_CHIP, S, FF_C), BF16)] * 3 + [jax.ShapeDtypeStruct((S, D), F32)],
        compiler_params=_params("arbitrary", "arbitrary"),
    )(h3, wg, wu, wd)


def _head_bwd(f, x2, tgt, g4):
    tm = 256

    def body(f_ref, x2_ref, t_ref, g_ref, loss_ref, dy_ref, df_ref, dg_ref):
        @pl.when(pl.program_id(0) == 0)
        def _():
            loss_ref[...] = jnp.zeros_like(loss_ref)
            dg_ref[...] = jnp.zeros_like(dg_ref)

        fv = f_ref[...]
        r = _rstd(fv)
        fn = fv * r
        e = x2_ref[...] + fn * g_ref[...] - t_ref[...]
        sq = jnp.sum(jnp.sum(e * e, axis=-1, keepdims=True), axis=0, keepdims=True)
        loss_ref[...] = loss_ref[...] + sq
        dy = e * (1.0 / D)
        dy_ref[...] = dy
        dg_ref[...] = dg_ref[...] + jnp.sum(dy * fn, axis=0, keepdims=True)
        t = dy * g_ref[...]
        df_ref[...] = (r * (t - fn * jnp.mean(t * fn, axis=-1, keepdims=True))).astype(BF16)

    row = pl.BlockSpec((tm, D), lambda i: (i, 0))
    vec = pl.BlockSpec((1, D), lambda i: (0, 0))
    return pl.pallas_call(
        body, grid=(S // tm,), name="head_bwd",
        in_specs=[row, row, row, vec],
        out_specs=[vec, row, row, vec],
        out_shape=[jax.ShapeDtypeStruct((1, D), F32), jax.ShapeDtypeStruct((S, D), F32),
                   jax.ShapeDtypeStruct((S, D), BF16), jax.ShapeDtypeStruct((1, D), F32)],
        compiler_params=_params("arbitrary"),
    )(f, x2, tgt, g4)


def _ffn_bwd_act(df, gt, up, wg, wu, wd):
    tm, sub = 512, 256

    def body(df_ref, gt_ref, up_ref, wg_ref, wu_ref, wd_ref, dgt_ref, dup_ref, dh_ref):
        k, i = pl.program_id(0), pl.program_id(1)
        parts = []
        for s in range(tm // sub):
            rows = slice(s * sub, (s + 1) * sub)
            da = _nt(df_ref[rows, :], wd_ref[...])
            gt, up = gt_ref[rows, :].astype(F32), up_ref[rows, :].astype(F32)
            sg = _sigmoid(gt)
            dup = (da * gt * sg).astype(BF16)
            dgt = (da * up * (sg * (1.0 + gt * (1.0 - sg)))).astype(BF16)
            dup_ref[rows, :] = dup
            dgt_ref[rows, :] = dgt
            parts.append(_nn(dgt, wg_ref[...]) + _nn(dup, wu_ref[...]))
        part = jnp.concatenate(parts, axis=0)
        rows = _rows(i, tm)

        @pl.when(k == 0)
        def _():
            dh_ref[rows, :] = part

        @pl.when(k > 0)
        def _():
            dh_ref[rows, :] = dh_ref[rows, :] + part

    wrow = pl.BlockSpec((None, FF_C, D), lambda k, i: (k, 0, 0))
    act = pl.BlockSpec((None, tm, FF_C), lambda k, i: (k, i, 0))
    row = pl.BlockSpec((tm, D), lambda k, i: (i, 0))
    return pl.pallas_call(
        body, grid=(N_CHIP, S // tm), name="ffn_bwd_act",
        in_specs=[row, act, act, wrow, wrow, wrow],
        out_specs=[act, act, pl.BlockSpec((S, D), lambda k, i: (0, 0))],
        out_shape=[jax.ShapeDtypeStruct((N_CHIP, S, FF_C), BF16), jax.ShapeDtypeStruct((N_CHIP, S, FF_C), BF16),
                   jax.ShapeDtypeStruct((S, D), F32)],
        compiler_params=_params("arbitrary", "arbitrary"),
    )(df, gt, up, wg, wu, wd)


def _ffn_bwd_w(a, df, h3, dgt, dup):
    tm = 1024
    assert S // tm == 2

    def body(a_ref, df_ref, h_ref, dgt_ref, dup_ref, dwd_ref, dwg_ref, dwu_ref, acc_d, acc_g, acc_u):
        i = pl.program_id(1)
        h = h_ref[...]
        parts = (_tn(a_ref[...], df_ref[...]), _tn(dgt_ref[...], h), _tn(dup_ref[...], h))

        @pl.when(i == 0)
        def _():
            for acc, part in zip((acc_d, acc_g, acc_u), parts):
                acc[...] = part

        @pl.when(i == S // tm - 1)
        def _():
            for out, acc, part in zip((dwd_ref, dwg_ref, dwu_ref), (acc_d, acc_g, acc_u), parts):
                out[...] = (acc[...] + part).astype(BF16)

    act = pl.BlockSpec((None, tm, FF_C), lambda k, i: (k, i, 0))
    row = pl.BlockSpec((tm, D), lambda k, i: (i, 0))
    wrow = pl.BlockSpec((None, FF_C, D), lambda k, i: (k, 0, 0))
    return pl.pallas_call(
        body, grid=(N_CHIP, S // tm), name="ffn_bwd_w",
        in_specs=[act, row, row, act, act],
        out_specs=[wrow, wrow, wrow],
        out_shape=[jax.ShapeDtypeStruct((N_CHIP, FF_C, D), BF16)] * 3,
        scratch_shapes=[pltpu.VMEM((FF_C, D), F32)] * 3,
        compiler_params=_params("parallel", "arbitrary"),
    )(a, df, h3, dgt, dup)


def _norm_bwd(dh3, dy, x2, mix, g2, g3, exchange, exchange_args):
    tm = 256

    def body(dh_ref, dy_ref, x2_ref, mix_ref, g2_ref, g3_ref, dx2_ref, dmix_ref, dg3_ref, dg2_ref, xc):
        @pl.when(pl.program_id(0) == 0)
        def _():
            xc.start()
            dg3_ref[...] = jnp.zeros_like(dg3_ref)
            dg2_ref[...] = jnp.zeros_like(dg2_ref)

        x2 = x2_ref[...]
        r3 = _rstd(x2)
        xn = x2 * r3
        dh = dh_ref[...]
        dg3_ref[...] = dg3_ref[...] + jnp.sum(dh * xn, axis=0, keepdims=True)
        t = dh * g3_ref[...]
        dx2 = dy_ref[...] + r3 * (t - xn * jnp.mean(t * xn, axis=-1, keepdims=True))
        dx2_ref[...] = dx2
        mix = mix_ref[...]
        r2 = _rstd(mix)
        mn = mix * r2
        dg2_ref[...] = dg2_ref[...] + jnp.sum(dx2 * mn, axis=0, keepdims=True)
        u = dx2 * g2_ref[...]
        dmix_ref[...] = (r2 * (u - mn * jnp.mean(u * mn, axis=-1, keepdims=True))).astype(BF16)

        @pl.when(pl.program_id(0) == S // tm - 1)
        def _():
            xc.middle()
            xc.finish()

    row = pl.BlockSpec((tm, D), lambda i: (i, 0))
    vec = pl.BlockSpec((1, D), lambda i: (0, 0))
    return _carry("norm_bwd", body, exchange, exchange_args, (dh3, dy, x2, mix, g2, g3),
                  [row, row, row, row, vec, vec], [row, row, vec, vec],
                  [jax.ShapeDtypeStruct((S, D), F32), jax.ShapeDtypeStruct((S, D), BF16),
                   jax.ShapeDtypeStruct((1, D), F32), jax.ShapeDtypeStruct((1, D), F32)],
                  grid=(S // tm,), semantics=("arbitrary",))


def _mix_bwd(dmix, cat_r, cat_a, wout, after):
    tm = 512

    def body(dm_ref, cr_ref, ca_ref, w_ref, dret_ref, datt_ref, dw_ref, acc, _):
        i = pl.program_id(0)

        @pl.when(i == 0)
        def _():
            acc[...] = jnp.zeros_like(acc)

        dm = dm_ref[...]
        dret_ref[...] = _nt(dm, w_ref[0:512, :])
        datt = _nt(dm, w_ref[512:1024, :])
        for j in range(4):
            datt_ref[j] = datt[:, 128 * j:128 * j + 128]
        acc[0:512, :] += _tn(cr_ref[...], dm)
        acc[512:1024, :] += _tn(ca_ref[...], dm)

        @pl.when(i == S // tm - 1)
        def _():
            dw_ref[...] = acc[...].astype(BF16)

    row = lambda w: pl.BlockSpec((tm, w), lambda i: (i, 0))
    full = pl.BlockSpec((D, D), lambda i: (0, 0))
    return _carry("mix_bwd", body, _NoExchange(), (), (dmix, cat_r, cat_a, wout),
                  [row(D), row(512), row(512), full],
                  [row(512), pl.BlockSpec((4, tm, 128), lambda i: (0, i, 0)), full],
                  [jax.ShapeDtypeStruct((S, 512), F32), jax.ShapeDtypeStruct((4, S, 128), F32),
                   jax.ShapeDtypeStruct((D, D), BF16)],
                  scratch_shapes=[pltpu.VMEM((D, D), F32)], grid=(S // tm,), semantics=("arbitrary",), after=after)[0]


def _att_bwd(aq, ak, av, datt, att_out, lse, exchange, exchange_args):
    def body(q_ref, k_ref, v_ref, do_ref, out_ref, l_ref, dq_ref, dk_ref, dv_ref, xc):
        xc.start()

        def clear(i, carry):
            rows = _rows(i, 256)
            for ref in (dq_ref, dk_ref, dv_ref):
                for j in range(4):
                    ref[j, rows, :] = jnp.zeros((256, 128), F32)
            return carry

        lax.fori_loop(0, S // 256, clear, 0)
        lane_head = lax.broadcasted_iota(jnp.int32, (ATT_BLK, 256), 1) // 64
        for d in PATTERN_DILATIONS:
            nb, has_prev = _att_blocks(d)
            bias_rest, bias_first = _att_bias(has_prev)

            def block(b, carry, d=d, nb=nb, has_prev=has_prev, bias_rest=bias_rest, bias_first=bias_first):
                r, ib = b // nb, b % nb
                rows = _class_rows(ib, r, d)
                prow = _class_rows(jnp.maximum(ib - 1, 0), r, d)
                bias = jnp.where(ib == 0, bias_first, bias_rest) if has_prev else bias_first
                for g in range(2):
                    qg = _slab_pair(q_ref, g, rows).astype(BF16)
                    kg = _slab_pair(k_ref, g, rows)
                    vg = _slab_pair(v_ref, g, rows)
                    if has_prev:
                        kg = jnp.concatenate([_slab_pair(k_ref, g, prow), kg], axis=0)
                        vg = jnp.concatenate([_slab_pair(v_ref, g, prow), vg], axis=0)
                    kg, vg = kg.astype(BF16), vg.astype(BF16)
                    dog = _slab_pair(do_ref, g, rows)
                    outg = _slab_pair(out_ref, g, rows)
                    lg = _slab_pair(l_ref, g, rows)
                    qs = _stack_heads(qg, lane_head)
                    dos = _stack_heads(dog, lane_head)
                    delta = jnp.sum(dos * jnp.concatenate([outg] * 4, axis=0), axis=-1, keepdims=True)
                    lh = jnp.max(_stack_heads(lg, lane_head, NEG), axis=-1, keepdims=True)
                    s = _nt(qs, kg) * ATT_SCALE + bias
                    p = jnp.exp(s - lh)
                    dosb = dos.astype(BF16)
                    ds = (p * (_nt(dosb, vg) - delta) * ATT_SCALE).astype(BF16)
                    dq = _unstack_heads(_nn(ds, kg), lane_head)
                    dk = _tn(ds, qs)
                    dv = _tn(p.astype(BF16), dosb)
                    for jj in range(2):
                        j, sl = 2 * g + jj, slice(128 * jj, 128 * jj + 128)
                        dq_ref[j, rows, :] += dq[:, sl]
                        if has_prev:
                            dk_ref[j, prow, :] += dk[0:ATT_BLK, sl]
                            dv_ref[j, prow, :] += dv[0:ATT_BLK, sl]
                            dk_ref[j, rows, :] += dk[ATT_BLK:2 * ATT_BLK, sl]
                            dv_ref[j, rows, :] += dv[ATT_BLK:2 * ATT_BLK, sl]
                        else:
                            dk_ref[j, rows, :] += dk[:, sl]
                            dv_ref[j, rows, :] += dv[:, sl]
                return carry

            lax.fori_loop(0, S // ATT_BLK, block, 0)
        xc.middle()
        xc.finish()

    slab = jax.ShapeDtypeStruct((4, S, 128), F32)
    return _carry("att_bwd", body, exchange, exchange_args, (aq, ak, av, datt, att_out, lse), [VMEM] * 6, [VMEM] * 3,
                  [slab, slab, slab])


def _ret_bwd(qr, kr, rv, proj, o_raw, states, dret, tabs, exchange, exchange_args):
    C = RET_C
    nc = S // C
    dtab, a_tab, b_tab, lam, bd = tabs

    def body(q_ref, k_ref, v_ref, g_ref, o_ref, st_ref, dr_ref, dt_ref, a_ref, b_ref, lam_ref, bd_ref,
             dq_ref, dk_ref, dv_ref, dg_ref, dR, exch):
        @pl.when(pl.program_id(0) == 0)
        def _():
            exch.start()
            dR[...] = jnp.zeros_like(dR)

        q, k, v = q_ref[...], k_ref[...], v_ref[...]
        lane_head = lax.broadcasted_iota(jnp.int32, (C, 256), 1) // 32
        col_head = lax.broadcasted_iota(jnp.int32, (C, 256), 1) // 64
        dos = []
        for j in range(4):
            sl = slice(128 * j, 128 * j + 128)
            oj = o_ref[:, sl]
            xc = oj - _seg_mean(oj)
            rs = lax.rsqrt(_seg_mean(xc * xc) + GN_EPS)
            rn = xc * rs
            gj = g_ref[:, sl]
            sg = _sigmoid(gj)
            dret = dr_ref[:, sl]
            dg_ref[:, sl] = dret * rn * (sg * (1.0 + gj * (1.0 - sg)))
            drn = dret * (gj * sg)
            dos.append(rs * (drn - _seg_mean(drn) - rn * _seg_mean(drn * rn)))
        do = [jnp.concatenate(dos[0:2], axis=1), jnp.concatenate(dos[2:4], axis=1)]
        do8 = jnp.concatenate(do, axis=1).astype(BF16)
        drb = dR[...].astype(BF16)
        rb = st_ref[...]
        dq = _nt(do8, rb) * a_ref[...]
        dk = _nt(v, drb) * b_ref[...]
        kb = (k.astype(F32) * b_ref[...]).astype(BF16)
        dvall = _nn(kb, drb)
        qs = _stack_heads(q, lane_head, n=8)
        dec = dt_ref[...]
        p = (_nt(qs, k) * dec).astype(BF16)
        dos = [_stack_heads(do[g], col_head).astype(BF16) for g in range(2)]
        dp = jnp.concatenate([_nt(dos[g], v[:, 256 * g:256 * g + 256]) for g in range(2)], axis=0)
        ds = (dp * dec).astype(BF16)
        dq = dq + _unstack_heads(_nn(ds, k), lane_head, n=8)
        dk = dk + _tn(ds, qs)
        dv = [dvall[:, 256 * g:256 * g + 256] + _tn(p[4 * C * g:4 * C * (g + 1)], dos[g]) for g in range(2)]
        qa = (q.astype(F32) * a_ref[...]).astype(BF16)
        dR[...] = dR[...] * lam_ref[...] + _tn(qa, do8) * bd_ref[...]
        dq_ref[...] = dq
        dk_ref[...] = dk
        dv_ref[:, 0:256] = dv[0]
        dv_ref[:, 256:512] = dv[1]

        @pl.when(pl.program_id(0) == nc - 1)
        def _():
            exch.middle()
            exch.finish()

    rev = lambda w: pl.BlockSpec((C, w), lambda n: (nc - 1 - n, 0))
    full = lambda a: pl.BlockSpec(a.shape, lambda n: (0,) * a.ndim)
    return _carry(
        "ret_bwd", body, exchange, exchange_args, (qr, kr, rv, proj, o_raw, states, dret, dtab, a_tab, b_tab, lam, bd),
        [rev(256), rev(256), rev(512), rev(512), rev(512),
         pl.BlockSpec((None, 256, 512), lambda n: (nc - 1 - n, 0, 0)), rev(512),
         full(dtab), full(a_tab), full(b_tab), full(lam), full(bd)],
        [rev(256), rev(256), rev(512), rev(512)],
        [jax.ShapeDtypeStruct((S, 256), F32), jax.ShapeDtypeStruct((S, 256), F32),
         jax.ShapeDtypeStruct((S, 512), F32), jax.ShapeDtypeStruct((S, 512), F32)],
        scratch_shapes=[pltpu.VMEM((256, 512), F32)], grid=(nc,), semantics=("arbitrary",))


def _rot_bwd(cos, sin, dqr, dkr, drv, drg, dq_att, dk_att, dv_att):
    tm = 256

    def body(cos_ref, sin_ref, dqr_ref, dkr_ref, drv_ref, drg_ref, dqa_ref, dka_ref, dva_ref, dp_ref):
        cr, ca, sr, sa = cos_ref[:, 0:256], cos_ref[:, 256:768], sin_ref[:, 0:256], sin_ref[:, 256:768]
        lo_r, lo_a = _rot_halves(tm)

        def unrot_r(g):
            gs = g * sr
            return g * cr + pltpu.roll(jnp.where(lo_r, -gs, 0.0), 16, 1) + pltpu.roll(jnp.where(lo_r, 0.0, gs), 240, 1)

        def unrot_a(g):
            gs = g * sa
            return g * ca + pltpu.roll(jnp.where(lo_a, -gs, 0.0), 8, 1) + pltpu.roll(jnp.where(lo_a, 0.0, gs), 504, 1)

        def wide(ref):
            return jnp.concatenate([ref[j] for j in range(4)], axis=1)

        dp_ref[:, 0:256] = unrot_r(dqr_ref[...]).astype(BF16)
        dp_ref[:, 256:512] = unrot_r(dkr_ref[...] * RET_SCALE).astype(BF16)
        dp_ref[:, 512:1024] = drv_ref[...].astype(BF16)
        dp_ref[:, 1024:1536] = drg_ref[...].astype(BF16)
        dp_ref[:, 1536:2048] = unrot_a(wide(dqa_ref)).astype(BF16)
        dp_ref[:, 2048:2560] = unrot_a(wide(dka_ref)).astype(BF16)
        dp_ref[:, 2560:3072] = wide(dva_ref).astype(BF16)

    row = lambda w: pl.BlockSpec((tm, w), lambda i: (i, 0))
    slab = pl.BlockSpec((4, tm, 128), lambda i: (0, i, 0))
    return pl.pallas_call(
        body, grid=(S // tm,), name="rot_bwd",
        in_specs=[row(768), row(768), row(256), row(256), row(512), row(512), slab, slab, slab],
        out_specs=row(PW), out_shape=jax.ShapeDtypeStruct((S, PW), BF16),
        compiler_params=_params("parallel"),
    )(cos, sin, dqr, dkr, drv, drg, dq_att, dk_att, dv_att)


def _win_bwd_w(h1, dproj, exchange, exchange_args):
    tm = 512

    def body(h_ref, dp_ref, dw_ref, acc, xc):
        k, i = pl.program_id(0), pl.program_id(1)

        @pl.when((k == 0) & (i == 0))
        def _():
            xc.start()

        @pl.when(i == 0)
        def _():
            acc[...] = jnp.zeros_like(acc)

        acc[...] += _tn(h_ref[...], dp_ref[...])

        @pl.when(i == S // tm - 1)
        def _():
            dw_ref[...] = acc[...].astype(BF16)

        @pl.when((k == N_CHIP - 1) & (i == S // tm - 1))
        def _():
            xc.middle()
            xc.finish()

    (dw,), out = _carry(
        "win_bwd_w", body, exchange, exchange_args, (h1, dproj),
        [pl.BlockSpec((tm, D), lambda k, i: (i, 0)), pl.BlockSpec((tm, WIN_C), lambda k, i: (i, k))],
        [pl.BlockSpec((None, D, WIN_C), lambda k, i: (k, 0, 0))],
        [jax.ShapeDtypeStruct((N_CHIP, D, WIN_C), BF16)],
        scratch_shapes=[pltpu.VMEM((D, WIN_C), F32)], grid=(N_CHIP, S // tm), semantics=("arbitrary", "arbitrary"))
    return dw, out


def _in_bwd(dproj, win_g, x, dx2, g1, after):
    tm = 512

    def body(dp_ref, w_ref, x_ref, dx2_ref, g_ref, dx_ref, dg_ref, _):
        @pl.when(pl.program_id(0) == 0)
        def _():
            dg_ref[...] = jnp.zeros_like(dg_ref)

        dh = _nt(dp_ref[:, 0:WIN_C], w_ref[0])
        for k in range(1, N_CHIP):
            dh = dh + _nt(dp_ref[:, k * WIN_C:(k + 1) * WIN_C], w_ref[k])
        xv = x_ref[...]
        r = _rstd(xv)
        xn = xv * r
        dg_ref[...] = dg_ref[...] + jnp.sum(dh * xn, axis=0, keepdims=True)
        t = dh * g_ref[...]
        dx_ref[...] = dx2_ref[...] + r * (t - xn * jnp.mean(t * xn, axis=-1, keepdims=True))

    row = lambda w: pl.BlockSpec((tm, w), lambda i: (i, 0))
    vec = pl.BlockSpec((1, D), lambda i: (0, 0))
    return _carry("in_bwd", body, _NoExchange(), (), (dproj, win_g, x, dx2, g1),
                  [row(PW), pl.BlockSpec((N_CHIP, D, WIN_C), lambda i: (0, 0, 0)), row(D), row(D), vec],
                  [row(D), vec], [jax.ShapeDtypeStruct((S, D), F32), jax.ShapeDtypeStruct((1, D), F32)],
                  grid=(S // tm,), semantics=("arbitrary",), after=after)[0]


ANY = pl.BlockSpec(memory_space=pl.ANY)
VMEM = pl.BlockSpec(memory_space=pltpu.VMEM)
FLIPS = ((1, 0), (0, 1), (1, 1))


def _place():
    x, y, c = lax.axis_index("x"), lax.axis_index("y"), lax.axis_index("c")
    chips = [((1 - x) if fx else x, (1 - y) if fy else y) for fx, fy in FLIPS]
    return x, y, c, 2 * x + y, chips


def _remote(src, dst, send_sem, recv_sem, device):
    return pltpu.make_async_remote_copy(src_ref=src, dst_ref=dst, send_sem=send_sem, recv_sem=recv_sem,
                                        device_id=device, device_id_type=MESH)


def _staggered(issue):
    c = lax.axis_index("c")

    @pl.when(c == 0)
    def _():
        issue((0, 1, 2))

    @pl.when(c == 1)
    def _():
        issue((1, 0, 2))


class _Exchange:
    aliases = {}

    def middle(self, ins, outs, sems):
        pass


class _GatherShards(_Exchange):
    def __init__(self, shards):
        n = self.n = len(shards)
        self.n_in = self.n_out = n
        self.out_shape = [jax.ShapeDtypeStruct((N_CHIP,) + s.shape, s.dtype) for s in shards]
        dma = pltpu.SemaphoreType.DMA
        self.scratch = [dma((3 * n,)), dma((3 * n,)), dma((3 * n,)), dma((3 * n,)), dma((n,)), dma((n,))]

    def _ici(self, ins, outs, sems, a, j, chip):
        x, y, c, me, chips = _place()
        half = ins[a].shape[0] // 2
        return _remote(ins[a].at[pl.ds(c * half, half), :], outs[a].at[me, pl.ds(c * half, half), :],
                       sems[0].at[3 * a + j], sems[1].at[3 * a + j], (*chip, c))

    def _fwd(self, outs, sems, a, j, chip, half_of):
        x, y, c, me, chips = _place()
        half = outs[a].shape[1] // 2
        blk = outs[a].at[2 * chip[0] + chip[1], pl.ds(half_of * half, half), :]
        return _remote(blk, blk, sems[2].at[3 * a + j], sems[3].at[3 * a + j], (x, y, 1 - c))

    def _own(self, ins, outs, sems, a):
        return _own_shard_to_sibling(ins[a], outs[a], sems[4].at[a], sems[5].at[a])

    def start(self, ins, outs, sems):
        chips = _place()[4]

        def issue(order):
            for a in range(self.n):
                for j in order:
                    self._ici(ins, outs, sems, a, j, chips[j]).start()

        _staggered(issue)
        for a in range(self.n):
            self._own(ins, outs, sems, a).start()

    def middle(self, ins, outs, sems):
        x, y, c, me, chips = _place()
        for a in range(self.n):
            for j, chip in enumerate(chips):
                half = outs[a].shape[1] // 2
                blk = outs[a].at[2 * chip[0] + chip[1], pl.ds(c * half, half), :]
                _remote(blk, blk, sems[0].at[3 * a + j], sems[1].at[3 * a + j], (x, y, c)).wait_recv()
                self._fwd(outs, sems, a, j, chip, c).start()

    def finish(self, ins, outs, sems):
        x, y, c, me, chips = _place()
        for a in range(self.n):
            for j, chip in enumerate(chips):
                self._fwd(outs, sems, a, j, chip, 1 - c).wait_recv()
        for a in range(self.n):
            for j, chip in enumerate(chips):
                self._ici(ins, outs, sems, a, j, chip).wait_send()
                self._fwd(outs, sems, a, j, chip, c).wait_send()
            self._own(ins, outs, sems, a).wait()


def _own_shard_to_sibling(shard_ref, gathered_ref, send_sem, recv_sem):
    x, y, c, me, chips = _place()
    return _remote(shard_ref, gathered_ref.at[me], send_sem, recv_sem, (x, y, 1 - c))


class _NoExchange(_Exchange):
    n_in = n_out = 0
    out_shape = ()
    scratch = ()

    def start(self, ins, outs, sems):
        pass

    def finish(self, ins, outs, sems):
        pass


class _ForwardGathered(_Exchange):
    def __init__(self, shards, own=True, forward=True):
        self.own, self.forward = own, forward
        n = self.n = len(shards)
        self.n_in, self.n_out = 2 * n, n
        self.out_shape = [jax.ShapeDtypeStruct((N_CHIP,) + s.shape, s.dtype) for s in shards]
        dma = pltpu.SemaphoreType.DMA
        self.scratch = [dma((3 * n,)), dma((3 * n,)), dma((n,)), dma((n,))]
        self.aliases = {n + a: a for a in range(n)}

    def _fwd(self, outs, sems, a, j, chip, half_of):
        x, y, c, me, chips = _place()
        half = outs[a].shape[1] // 2
        blk = outs[a].at[2 * chip[0] + chip[1], pl.ds(half_of * half, half), :]
        return _remote(blk, blk, sems[0].at[3 * a + j], sems[1].at[3 * a + j], (x, y, 1 - c))

    def _own(self, ins, outs, sems, a):
        return _own_shard_to_sibling(ins[a], outs[a], sems[2].at[a], sems[3].at[a])

    def start(self, ins, outs, sems):
        x, y, c, me, chips = _place()
        for a in range(self.n):
            for j, chip in enumerate(chips if self.forward else ()):
                self._fwd(outs, sems, a, j, chip, c).start()
        for a in range(self.n if self.own else 0):
            self._own(ins, outs, sems, a).start()

    def finish(self, ins, outs, sems):
        x, y, c, me, chips = _place()
        for a in range(self.n):
            for j, chip in enumerate(chips if self.forward else ()):
                self._fwd(outs, sems, a, j, chip, 1 - c).wait_recv()
        for a in range(self.n):
            for j, chip in enumerate(chips if self.forward else ()):
                self._fwd(outs, sems, a, j, chip, c).wait_send()
            if self.own:
                self._own(ins, outs, sems, a).wait()


HBM = pl.BlockSpec(memory_space=pltpu.HBM)
SEMS = pl.BlockSpec(memory_space=pltpu.SEMAPHORE)
DATAFLOW = pltpu.SideEffectType.DATAFLOW_SIDE_EFFECTING


class _OverIci:
    def __init__(self, name, sources, lands):
        self.name, self.n = name, len(sources)
        hbm = lambda t: pltpu.with_memory_space_constraint(t, pltpu.HBM)
        self.arrays = [hbm(t) for t in sources] + [hbm(t) for t in lands]

    def sent(self, src, land, a, chip):
        raise NotImplementedError

    def landed(self, land, a, chip):
        raise NotImplementedError

    def _copy(self, arr, sems, a, j, receiving):
        x, y, c, me, chips = _place()
        src, dst = self.sent(arr[a], arr[self.n + a], a, chips[j])
        if receiving:
            dst = self.landed(arr[self.n + a], a, chips[j])
        return _remote(src, dst, sems[0].at[3 * a + j], sems[1].at[3 * a + j], (*chips[j], c))

    def start(self, after):
        m = len(self.arrays)

        def body(*refs):
            arr, sems, token = refs[:m], refs[m + 1:m + 3], refs[-1]

            def issue(order):
                for a in range(self.n):
                    for j in order:
                        self._copy(arr, sems, a, j, False).start()

            _staggered(issue)
            token[...] = jnp.zeros_like(token)

        dma = pltpu.SemaphoreType.DMA
        outs = pl.pallas_call(
            body, name=self.name + "_start",
            out_shape=[dma((3 * self.n,)), dma((3 * self.n,))] + [pltpu.HBM(t.shape, t.dtype) for t in self.arrays]
                      + [jax.ShapeDtypeStruct((8, 128), F32)],
            in_specs=[HBM] * m + [ANY], out_specs=[SEMS, SEMS] + [HBM] * m + [VMEM],
            input_output_aliases={i: 2 + i for i in range(m)},
            compiler_params=pltpu.CompilerParams(has_side_effects=DATAFLOW),
        )(*self.arrays, after)
        self.sems, self.arrays = outs[0:2], list(outs[2:2 + m])
        return outs[-1]

    def wait(self, after):
        m = len(self.arrays)

        def body(*refs):
            arr, sems = refs[:m], refs[m:m + 2]
            for a in range(self.n):
                for j in range(3):
                    self._copy(arr, sems, a, j, False).wait_send()
                    self._copy(arr, sems, a, j, True).wait_recv()

        outs = pl.pallas_call(
            body, name=self.name + "_wait",
            out_shape=[pltpu.HBM(t.shape, t.dtype) for t in self.arrays],
            in_specs=[HBM] * m + [SEMS, SEMS, ANY], out_specs=[HBM] * m,
            input_output_aliases={i: i for i in range(m)},
            compiler_params=pltpu.CompilerParams(has_side_effects=DATAFLOW),
        )(*self.arrays, *self.sems, after)
        return list(outs[:self.n]), list(outs[self.n:])


class _GatherOverIci(_OverIci):
    def __init__(self, name, shards):
        super().__init__(name, shards, [lax.empty((N_CHIP,) + s.shape, s.dtype) for s in shards])

    @staticmethod
    def _half(ref):
        c = lax.axis_index("c")
        half = ref.shape[-2] // 2
        return pl.ds(c * half, half)

    def sent(self, src, land, a, chip):
        return src.at[self._half(src), :], land.at[_place()[3], self._half(src), :]

    def landed(self, land, a, chip):
        return land.at[2 * chip[0] + chip[1], self._half(land), :]


class _SumOverIci(_OverIci):
    def __init__(self, name, pre):
        super().__init__(name, pre, [lax.empty(p.shape, p.dtype) for p in pre])

    def sent(self, src, land, a, chip):
        return src.at[2 * chip[0] + chip[1]], land.at[_place()[3]]

    def landed(self, land, a, chip):
        return land.at[2 * chip[0] + chip[1]]


class _HalvesToSibling(_Exchange):
    def __init__(self, grads):
        n = self.n = len(grads)
        self.n_in = self.n_out = n
        self.out_shape = [jax.ShapeDtypeStruct((N_CHIP, g.shape[1] // 2, g.shape[2]), g.dtype) for g in grads]
        self.scratch = [pltpu.SemaphoreType.DMA((n,)), pltpu.SemaphoreType.DMA((n,))]

    def _copy(self, ins, outs, sems, a):
        x, y, c, me, chips = _place()
        half = ins[a].shape[1] // 2
        return _remote(ins[a].at[:, pl.ds((1 - c) * half, half), :], outs[a], sems[0].at[a], sems[1].at[a], (x, y, 1 - c))

    def start(self, ins, outs, sems):
        for a in range(self.n):
            self._copy(ins, outs, sems, a).start()

    def finish(self, ins, outs, sems):
        for a in range(self.n):
            self._copy(ins, outs, sems, a).wait_recv()
        for a in range(self.n):
            self._copy(ins, outs, sems, a).wait_send()


class _OverChips(_Exchange):
    def __init__(self, pre):
        n = self.n = len(pre)
        self.n_in = self.n_out = n
        self.out_shape = [jax.ShapeDtypeStruct(p.shape, p.dtype) for p in pre]
        dma = pltpu.SemaphoreType.DMA
        self.scratch = [dma((3 * n,)), dma((3 * n,))]

    def _ici(self, ins, outs, sems, a, j, chip):
        x, y, c, me, chips = _place()
        return _remote(ins[a].at[2 * chip[0] + chip[1]], outs[a].at[me], sems[0].at[3 * a + j], sems[1].at[3 * a + j],
                       (*chip, c))

    def start(self, ins, outs, sems):
        chips = _place()[4]

        def issue(order):
            for a in range(self.n):
                for j in order:
                    self._ici(ins, outs, sems, a, j, chips[j]).start()

        _staggered(issue)

    def finish(self, ins, outs, sems):
        x, y, c, me, chips = _place()
        for a in range(self.n):
            for j, chip in enumerate(chips):
                blk = outs[a].at[2 * chip[0] + chip[1]]
                _remote(blk, blk, sems[0].at[3 * a + j], sems[1].at[3 * a + j], (x, y, c)).wait_recv()
        for a in range(self.n):
            for j, chip in enumerate(chips):
                self._ici(ins, outs, sems, a, j, chip).wait_send()


class _ShareHalves(_Exchange):
    def __init__(self, fulls):
        n = self.n = len(fulls)
        self.n_in = self.n_out = n
        self.out_shape = [jax.ShapeDtypeStruct(f.shape, f.dtype) for f in fulls]
        self.scratch = [pltpu.SemaphoreType.DMA((n,)), pltpu.SemaphoreType.DMA((n,))]
        self.aliases = {a: a for a in range(n)}

    def _copy(self, outs, sems, a, half_of):
        x, y, c, me, chips = _place()
        half = outs[a].shape[0] // 2
        rows = outs[a].at[pl.ds(half_of * half, half), :]
        return _remote(rows, rows, sems[0].at[a], sems[1].at[a], (x, y, 1 - c))

    def start(self, ins, outs, sems):
        c = _place()[2]
        for a in range(self.n):
            self._copy(outs, sems, a, c).start()

    def finish(self, ins, outs, sems):
        c = _place()[2]
        for a in range(self.n):
            self._copy(outs, sems, a, 1 - c).wait_recv()
        for a in range(self.n):
            self._copy(outs, sems, a, c).wait_send()


class _GatherBlocks(_Exchange):
    def __init__(self, block):
        self.n_in = self.n_out = 1
        self.out_shape = [jax.ShapeDtypeStruct((8,) + block.shape, block.dtype)]
        dma = pltpu.SemaphoreType.DMA
        self.scratch = [dma((7,)), dma((7,)), dma]

    @staticmethod
    def _peer(f):
        x, y, c, me, chips = _place()
        return ((1 - x) if f & 4 else x, (1 - y) if f & 2 else y, (1 - c) if f & 1 else c)

    def start(self, ins, outs, sems):
        x, y, c, me, chips = _place()
        for f in range(1, 8):
            _remote(ins[0], outs[0].at[2 * me + c], sems[0].at[f - 1], sems[1].at[f - 1], self._peer(f)).start()
        pltpu.make_async_copy(ins[0], outs[0].at[2 * me + c], sems[2]).start()

    def finish(self, ins, outs, sems):
        x, y, c, me, chips = _place()
        for f in range(1, 8):
            px, py, pc = self._peer(f)
            blk = outs[0].at[4 * px + 2 * py + pc]
            _remote(blk, blk, sems[0].at[f - 1], sems[1].at[f - 1], (x, y, c)).wait_recv()
        for f in range(1, 8):
            _remote(ins[0], outs[0].at[2 * me + c], sems[0].at[f - 1], sems[1].at[f - 1], self._peer(f)).wait_send()
        pltpu.make_async_copy(ins[0], outs[0].at[2 * me + c], sems[2]).wait()


class _Both(_Exchange):
    def __init__(self, first, second):
        self.parts = (first, second)
        self.n_in, self.n_out = first.n_in + second.n_in, first.n_out + second.n_out
        self.out_shape = first.out_shape + second.out_shape
        self.scratch = first.scratch + second.scratch
        self.aliases = dict(first.aliases)
        self.aliases.update({first.n_in + i: first.n_out + o for i, o in second.aliases.items()})

    def _split(self, ins, outs, sems):
        a, b = self.parts
        return ((a, ins[:a.n_in], outs[:a.n_out], sems[:len(a.scratch)]),
                (b, ins[a.n_in:], outs[a.n_out:], sems[len(a.scratch):]))

    def start(self, ins, outs, sems):
        for ex, i, o, s in self._split(ins, outs, sems):
            ex.start(i, o, s)

    def middle(self, ins, outs, sems):
        for ex, i, o, s in self._split(ins, outs, sems):
            ex.middle(i, o, s)

    def finish(self, ins, outs, sems):
        for ex, i, o, s in self._split(ins, outs, sems):
            ex.finish(i, o, s)


class _Bound:
    def __init__(self, ex, ins, outs, sems):
        self.start = lambda: ex.start(ins, outs, sems)
        self.middle = lambda: ex.middle(ins, outs, sems)
        self.finish = lambda: ex.finish(ins, outs, sems)


def _carry(name, body, ex, ex_args, args, in_specs, out_specs, out_shape, scratch_shapes=(), grid=None, semantics=(),
           after=None):
    n_a, n_o, n_s = len(args), len(out_shape), len(scratch_shapes)
    behind = [] if after is None else [after]

    def full_body(*refs):
        p = 0
        groups = []
        for size in (n_a, ex.n_in, len(behind), n_o, ex.n_out, n_s, len(ex.scratch)):
            groups.append(refs[p:p + size])
            p += size
        a, ei, _, o, eo, s, es = groups
        body(*a, *o, *s, _Bound(ex, ei, eo, es))

    kwargs = {} if grid is None else {"grid": grid}
    outs = pl.pallas_call(
        full_body, name=name,
        in_specs=list(in_specs) + [ANY] * (ex.n_in + len(behind)), out_specs=list(out_specs) + [ANY] * ex.n_out,
        out_shape=list(out_shape) + list(ex.out_shape), scratch_shapes=list(scratch_shapes) + list(ex.scratch),
        input_output_aliases={n_a + i: n_o + o for i, o in ex.aliases.items()},
        compiler_params=_params(*semantics) if semantics else pltpu.CompilerParams(vmem_limit_bytes=VMEM_LIMIT),
        **kwargs,
    )(*args, *ex_args, *behind)
    return outs[:n_o], outs[n_o:]


def _cast_carrying(name, arrays, ex, ex_args):
    n = len(arrays)
    r, cc = arrays[0].shape
    steps = 4
    tr = r // steps

    def body(*refs):
        xc = refs[-1]

        @pl.when(pl.program_id(0) == 0)
        def _():
            xc.start()

        for a in range(n):
            refs[n + a][...] = refs[a][...].astype(BF16)

        @pl.when(pl.program_id(0) == steps - 1)
        def _():
            xc.middle()
            xc.finish()

    blk = pl.BlockSpec((tr, cc), lambda i: (i, 0))
    return _carry(name, body, ex, ex_args, arrays, [blk] * n, [blk] * n, [jax.ShapeDtypeStruct((r, cc), BF16)] * n,
                  grid=(steps,), semantics=("arbitrary",))


def _exchange_alone(name, ex, ex_args):
    def body(xc):
        xc.start()
        xc.middle()
        xc.finish()

    return _carry(name, body, ex, ex_args, (), (), (), ())[1]


def _core_index():
    return lax.axis_index("c").astype(jnp.int32).reshape(1)


def _pair_sum(gs, gots):
    n = len(gs)
    _, r, cc = gs[0].shape
    half = r // 2

    def body(c_ref, *refs):
        for a in range(n):
            refs[2 * n + a][...] = (refs[a][...].astype(F32) + refs[n + a][...].astype(F32)).astype(BF16)

    mine = pl.BlockSpec((None, half, cc), lambda k, c_ref: (k, c_ref[0], 0))
    blk = pl.BlockSpec((None, half, cc), lambda k, c_ref: (k, 0, 0))
    return pl.pallas_call(
        body, name=f"pair_sum_{r}x{cc}",
        grid_spec=pltpu.PrefetchScalarGridSpec(
            num_scalar_prefetch=1, grid=(N_CHIP,), in_specs=[mine] * n + [blk] * n, out_specs=[blk] * n),
        out_shape=[jax.ShapeDtypeStruct((N_CHIP, half, cc), BF16)] * n,
        compiler_params=_params("parallel"),
    )(_core_index(), *gs, *gots)


def _chip_sum(pre, parts):
    n = len(parts)
    _, half, cc = parts[0].shape
    tr = half // 2
    me = 2 * lax.axis_index("x") + lax.axis_index("y")
    others = [k + (k >= me).astype(jnp.int32) for k in range(3)]
    where = jnp.stack([lax.axis_index("c"), me, *others]).astype(jnp.int32)

    def body(w_ref, *refs):
        for a in range(n):
            own, p1, p2, p3 = refs[4 * a:4 * a + 4]
            refs[4 * n + a][...] = ((own[...].astype(F32) + p1[...].astype(F32)) + p2[...].astype(F32)) + p3[...].astype(F32)

    slot = lambda s: pl.BlockSpec((None, tr, cc), lambda i, w_ref: (w_ref[s], i, 0))
    operands = []
    for a in range(n):
        operands += [pre[a], parts[a], parts[a], parts[a]]
    return pl.pallas_call(
        body, name=f"chip_sum_{half}x{cc}",
        grid_spec=pltpu.PrefetchScalarGridSpec(
            num_scalar_prefetch=1, grid=(2,),
            in_specs=[slot(1), slot(2), slot(3), slot(4)] * n,
            out_specs=[pl.BlockSpec((tr, cc), lambda i, w_ref: (2 * w_ref[0] + i, 0))] * n),
        out_shape=[jax.ShapeDtypeStruct((2 * half, cc), F32)] * n,
        compiler_params=_params("parallel"),
    )(where, *operands)


def _adamw_math(w, g, m, v):
    m = ADAM_B1 * m + (1.0 - ADAM_B1) * g
    v = ADAM_B2 * v + (1.0 - ADAM_B2) * (g * g)
    m_hat = m / (1.0 - ADAM_B1 ** ADAM_STEP)
    v_hat = v / (1.0 - ADAM_B2 ** ADAM_STEP)
    delta = -ADAM_LR * (m_hat / (jnp.sqrt(v_hat) + ADAM_EPS) + ADAM_WD * w)
    return delta, m, v


def _adamw(w, g, m, v, after=None):
    r, cc = w.shape
    tr = r // 4

    def body(w_ref, g_ref, m_ref, v_ref, go_ref, d_ref, nm_ref, nv_ref, _):
        g = g_ref[...]
        go_ref[...] = g
        d_ref[...], nm_ref[...], nv_ref[...] = _adamw_math(w_ref[...], g, m_ref[...], v_ref[...])

    blk = pl.BlockSpec((tr, cc), lambda i: (i, 0))
    return _carry(f"adamw_{r}x{cc}", body, _NoExchange(), (), (w, g, m, v), [blk] * 4, [blk] * 4,
                  [jax.ShapeDtypeStruct((r, cc), F32)] * 4, grid=(4,), semantics=("parallel",), after=after)[0]


def _pack8(rows):
    def body(*refs):
        out_ref = refs[-1]
        out_ref[...] = jnp.zeros_like(out_ref)
        for i, r in enumerate(refs[:-1]):
            out_ref[i:i + 1, :] = r[...]

    return pl.pallas_call(body, name="pack8", out_shape=jax.ShapeDtypeStruct((8, D), F32))(*rows)


def _adamw_gains(gall, ws, ms, vs):
    def body(ga_ref, *refs):
        w, m, v = refs[0:4], refs[4:8], refs[8:12]
        outs, loss_ref, total = refs[12:28], refs[28], refs[29]
        g = ga_ref[0]
        for dev in range(1, 8):
            g = g + ga_ref[dev]
        total[...] = g
        for i in range(4):
            gi = total[i:i + 1, :]
            outs[i][...] = gi
            outs[4 + i][...], outs[8 + i][...], outs[12 + i][...] = _adamw_math(w[i][...], gi, m[i][...], v[i][...])
        loss_ref[...] = total[4:5, 0:128] * (0.5 / D)

    outs = pl.pallas_call(
        body, name="adamw_gains",
        out_shape=[jax.ShapeDtypeStruct((1, D), F32)] * 16 + [jax.ShapeDtypeStruct((1, 128), F32)],
        scratch_shapes=[pltpu.VMEM((8, D), F32)],
    )(gall, *ws, *ms, *vs)
    return outs[0:4], outs[4:8], outs[8:12], outs[12:16], outs[16]


def kernel(x, positions, w_in, w_out, g_pre_mix, g_post_mix, g_pre_ffn, g_post_ffn, w_gate, w_up, w_down, loss_target, m_w_in, m_w_out, m_g_pre_mix, m_g_post_mix, m_g_pre_ffn, m_g_post_ffn, m_w_gate, m_w_up, m_w_down, v_w_in, v_w_out, v_g_pre_mix, v_g_post_mix, v_g_pre_ffn, v_g_post_ffn, v_w_gate, v_w_up, v_w_down):
    tr = lambda t: jnp.swapaxes(t, 1, 2)[0]
    shards = [w_in[0], w_out[0], tr(w_gate), tr(w_up), w_down[0]]
    moms = [m_w_in[0], m_w_out[0], tr(m_w_gate), tr(m_w_up), m_w_down[0]]
    vels = [v_w_in[0], v_w_out[0], tr(v_w_gate), tr(v_w_up), v_w_down[0]]
    xs, pos, tgt = x[0], positions.reshape(S, 1), loss_target[0]
    g1, g2, g3, g4 = g_pre_mix, g_post_mix, g_pre_ffn, g_post_ffn
    tabs = tuple(jnp.asarray(t) for t in _retention_tables())
    ifc, spread = _rotary_tables()
    ifc, spread = jnp.asarray(ifc), jnp.asarray(spread, dtype=BF16)
    bf = [s.astype(BF16) for s in shards[:2]]

    ffn_bf, (win_g, wout_g) = _cast_carrying("gather_in", shards[2:], _GatherShards(bf), bf)
    bf += list(ffn_bf)
    wout_g = wout_g.reshape(D, D)
    ffn_gather = _GatherOverIci("ffn_gather", bf[2:])
    token = ffn_gather.start(win_g)
    h1, qr, kr, rv, rg, aq, ak, av, cos, sin = _proj_fwd(xs, g1, win_g, pos, ifc, spread, token)
    (o_raw, cat_r, states), _ = _ret_fwd(qr, kr, rv, rg, tabs, _NoExchange(), ())
    n_ffn = len(bf[2:])
    (att_out, lse, cat_a), ffn_gather.arrays[n_ffn:] = _att_fwd(
        aq, ak, av, _ForwardGathered(bf[2:], forward=False), ffn_gather.arrays)
    ffn_sh, ffn_lands = ffn_gather.wait(cat_a)
    (mix, x2, h3), (wg_g, wu_g, wd_g) = _mix_fwd(cat_r, cat_a, wout_g, xs, g2, g3,
                                                _ForwardGathered(bf[2:], own=False), [*ffn_sh, *ffn_lands])
    gt, up, a, f = _ffn_fwd(h3, wg_g, wu_g, wd_g)

    sq, dy, df, dg4 = _head_bwd(f, x2, tgt, g4)
    dgt, dup, dh3 = _ffn_bwd_act(df, gt, up, wg_g, wu_g, wd_g)
    ffn_grads = list(_ffn_bwd_w(a, df, h3, dgt, dup))
    (dx2, dmix, dg3, dg2), got = _norm_bwd(dh3, dy, x2, mix, g2, g3, _HalvesToSibling(ffn_grads), ffn_grads)
    ffn_sum = _SumOverIci("ffn_sum", _pair_sum(ffn_grads, got))
    token = ffn_sum.start(dmix)
    dret, datt, dwout = _mix_bwd(dmix, cat_r, cat_a, wout_g, token)
    (dq_att, dk_att, dv_att), _ = _att_bwd(aq, ak, av, datt, att_out, lse, _NoExchange(), ())
    (dqr, dkr, drv, drg), _ = _ret_bwd(qr, kr, rv, rg, o_raw, states, dret, tabs, _NoExchange(), ())
    dproj = _rot_bwd(cos, sin, dqr, dkr, drv, drg, dq_att, dk_att, dv_att)
    sums = _chip_sum(*ffn_sum.wait(dproj))
    dwin, ffn_full = _win_bwd_w(h1, dproj, _ShareHalves(sums), sums)
    in_grads = [dwin, dwout.reshape(N_CHIP, WOUT_R, D)]

    got = _exchange_alone("halves_to_sibling", _HalvesToSibling(in_grads), in_grads)
    in_sum = _SumOverIci("in_sum", [*_pair_sum(in_grads[:1], got[:1]), *_pair_sum(in_grads[1:], got[1:])])
    token = in_sum.start(dproj)
    dx, dg1 = _in_bwd(dproj, win_g, xs, dx2, g1, token)
    ffn_upd = [_adamw(shards[2 + i], ffn_full[o], moms[2 + i], vels[2 + i], token)
               for i, o in enumerate((1, 2, 0))]
    pre, parts = in_sum.wait(ffn_upd[2][0])
    sums = [*_chip_sum(pre[:1], parts[:1]), *_chip_sum(pre[1:], parts[1:])]
    gblock = _pack8([dg1, dg2, dg3, dg4, sq])
    *in_full, gall = _exchange_alone("share_rest", _Both(_ShareHalves(sums), _GatherBlocks(gblock)), [*sums, gblock])
    upd = [_adamw(w, g, m, v) for w, g, m, v in zip(shards[:2], in_full, moms[:2], vels[:2])] + ffn_upd
    gg, gd, gm, gv, loss_row = _adamw_gains(gall, [g1, g2, g3, g4],
                                            [m_g_pre_mix, m_g_post_mix, m_g_pre_ffn, m_g_post_ffn],
                                            [v_g_pre_mix, v_g_post_mix, v_g_pre_ffn, v_g_post_ffn])

    def order(mats, vecs):
        back = lambda t: jnp.swapaxes(t[None], 1, 2)
        return [mats[0][None], mats[1][None], *vecs, back(mats[2]), back(mats[3]), mats[4][None]]

    return (loss_row[0, 0], dx[None],
            *order([u[0] for u in upd], gg),
            *order([u[1] for u in upd], gd),
            *order([u[2] for u in upd], gm),
            *order([u[3] for u in upd], gv))
```

```python
import functools

import numpy as np
import jax
import jax.numpy as jnp
from jax import lax
from jax.experimental import pallas as pl
from jax.experimental.pallas import tpu as pltpu

F32, BF16 = jnp.float32, jnp.bfloat16
MESH = pl.DeviceIdType.MESH

S = 2048
D = 1024
PW = 3072
N_CHIP = 4
WIN_C = PW // N_CHIP
DFF = 2816
FF_C = DFF // N_CHIP
WOUT_R = D // N_CHIP
RMS_EPS = 1e-6
GN_EPS = 1e-5
RET_C = 128
RET_SCALE = 32 ** -0.5
ATT_BLK = 128
ATT_SCALE = 64 ** -0.5
PATTERN_DILATIONS = (1, 4, 16)
NEG = -1e30
VMEM_LIMIT = 56 * 1024 * 1024

ADAM_LR, ADAM_B1, ADAM_B2, ADAM_EPS, ADAM_WD, ADAM_STEP = 0.001, 0.9, 0.999, 1e-08, 0.01, 10


def _params(*sem):
    return pltpu.CompilerParams(dimension_semantics=sem, vmem_limit_bytes=VMEM_LIMIT)


def _nt(a, b):
    return lax.dot_general(a, b, (((1,), (1,)), ((), ())), preferred_element_type=F32)


def _tn(a, b):
    return lax.dot_general(a, b, (((0,), (0,)), ((), ())), preferred_element_type=F32)


def _nn(a, b):
    return jnp.dot(a, b, preferred_element_type=F32)


def _rstd(v):
    return lax.rsqrt(jnp.mean(v * v, axis=-1, keepdims=True) + RMS_EPS)


def _sigmoid(v):
    return 1.0 / (1.0 + jnp.exp(-v))


def _rows(i, t):
    return pl.ds(pl.multiple_of(i * t, t), t)


def _retention_tables():
    h = np.arange(8, dtype=np.float32)
    log_g = np.log1p(-np.exp2(-5.0 - h)).astype(np.float32)
    idx = np.arange(RET_C, dtype=np.float32)
    diff = idx[:, None] - idx[None, :]
    dtab = np.where(diff >= 0, np.exp(log_g[:, None, None] * np.maximum(diff, 0.0)), 0.0).astype(np.float32)
    dtab = dtab.reshape(8 * RET_C, RET_C)
    lane_head = np.arange(256) // 32
    a_tab = np.exp(log_g[lane_head][None, :] * (idx + 1.0)[:, None]).astype(np.float32)
    b_tab = np.exp(log_g[lane_head][None, :] * (RET_C - 1.0 - idx)[:, None]).astype(np.float32)
    lam = np.exp(log_g[lane_head] * RET_C).astype(np.float32)[:, None]
    bd = (lane_head[:, None] == (np.arange(512) // 64)[None, :]).astype(np.float32)
    return dtab, a_tab, b_tab, lam, bd


def _rotary_tables():
    inv_r = (1.0 / (np.float32(10000.0) ** np.linspace(0.0, 1.0, 16, dtype=np.float32))).astype(np.float32)
    inv_a = (np.float32(500000.0) ** (-np.arange(0, 16, 2, dtype=np.float32) / np.float32(16))).astype(np.float32)
    ifc = np.zeros((1, 128), np.float32)
    ifc[0, 0:16], ifc[0, 16:24] = inv_r, inv_a
    spread = np.zeros((128, 768), np.float32)
    for lane in range(256):
        spread[(lane % 32) % 16, lane] = 1.0
    for lane in range(512):
        d = lane % 64
        spread[16 + d % 8 if d < 16 else 24, 256 + lane] = 1.0
    return ifc, spread


def _rot_halves(tm):
    lo_r = (lax.broadcasted_iota(jnp.int32, (tm, 256), 1) % 32) < 16
    lo_a = (lax.broadcasted_iota(jnp.int32, (tm, 512), 1) % 64) < 8
    return lo_r, lo_a


def _spread_exact(t, e):
    hi = t.astype(BF16)
    r1 = t - hi.astype(F32)
    mid = r1.astype(BF16)
    lo = (r1 - mid.astype(F32)).astype(BF16)
    return _nn(hi, e) + _nn(mid, e) + _nn(lo, e)


def _proj_fwd(h1, win_g, cos, sin, after):
    tm = 256

    def body(h_ref, w_ref, cos_ref, sin_ref, qr_ref, kr_ref, rv_ref, rg_ref, aq_ref, ak_ref, av_ref, p_ref, _):
        h = h_ref[...]
        for k in range(N_CHIP):
            p_ref[:, k * WIN_C:(k + 1) * WIN_C] = _nn(h, w_ref[k])
        cr, ca, sr, sa = cos_ref[:, 0:256], cos_ref[:, 256:768], sin_ref[:, 0:256], sin_ref[:, 256:768]
        lo_r, lo_a = _rot_halves(tm)

        def rot_r(v):
            return v * cr + sr * jnp.where(lo_r, -pltpu.roll(v, 240, 1), pltpu.roll(v, 16, 1))

        def rot_a(v):
            return v * ca + sa * jnp.where(lo_a, -pltpu.roll(v, 504, 1), pltpu.roll(v, 8, 1))

        qr_ref[...] = rot_r(p_ref[:, 0:256]).astype(BF16)
        kr_ref[...] = (rot_r(p_ref[:, 256:512]) * RET_SCALE).astype(BF16)
        rv_ref[...] = p_ref[:, 512:1024].astype(BF16)
        rg_ref[...] = p_ref[:, 1024:1536]
        aq, ak = rot_a(p_ref[:, 1536:2048]), rot_a(p_ref[:, 2048:2560])
        for j in range(4):
            aq_ref[j] = aq[:, 128 * j:128 * j + 128]
            ak_ref[j] = ak[:, 128 * j:128 * j + 128]
            av_ref[j] = p_ref[:, 2560 + 128 * j:2560 + 128 * j + 128]

    row = lambda w: pl.BlockSpec((tm, w), lambda i: (i, 0))
    slab = pl.BlockSpec((4, tm, 128), lambda i: (0, i, 0))
    return _carry(
        "proj_fwd", body, _NoExchange(), (), (h1, win_g, cos, sin),
        [row(D), pl.BlockSpec((N_CHIP, D, WIN_C), lambda i: (0, 0, 0)), row(768), row(768)],
        [row(256), row(256), row(512), row(512), slab, slab, slab],
        [jax.ShapeDtypeStruct((S, w), BF16) for w in (256, 256, 512)]
        + [jax.ShapeDtypeStruct((S, 512), F32)] + [jax.ShapeDtypeStruct((4, S, 128), F32)] * 3,
        scratch_shapes=[pltpu.VMEM((tm, PW), F32)], grid=(S // tm,), semantics=("parallel",), after=after)[0]


def _seg_mean(v):
    lo = lax.broadcasted_iota(jnp.int32, v.shape, 1) < 64
    s_lo = jnp.sum(jnp.where(lo, v, 0.0), axis=-1, keepdims=True)
    s_hi = jnp.sum(jnp.where(lo, 0.0, v), axis=-1, keepdims=True)
    return jnp.where(lo, s_lo, s_hi) * (1.0 / 64.0)


def _ret_fwd(qr, kr, rv, proj, tabs, exchange, exchange_args):
    C = RET_C
    dtab, a_tab, b_tab, lam, bd = tabs

    def body(q_ref, k_ref, v_ref, g_ref, dt_ref, a_ref, b_ref, lam_ref, bd_ref, o_ref, cat_ref, st_ref, R, exch):
        @pl.when(pl.program_id(0) == 0)
        def _():
            exch.start()
            R[...] = jnp.zeros_like(R)

        @pl.when(pl.program_id(0) == S // C // 2)
        def _():
            exch.middle()

        q, k, v = q_ref[...], k_ref[...], v_ref[...]
        lane_head = lax.broadcasted_iota(jnp.int32, (C, 256), 1) // 32
        col_head = lax.broadcasted_iota(jnp.int32, (C, 256), 1) // 64
        rb = R[...].astype(BF16)
        st_ref[...] = rb
        qa = (q.astype(F32) * a_ref[...]).astype(BF16)
        cross = _nn(qa, rb)
        p = (_nt(_stack_heads(q, lane_head, n=8), k) * dt_ref[...]).astype(BF16)
        og = [cross[:, 256 * g:256 * g + 256]
              + _unstack_heads(_nn(p[4 * C * g:4 * C * (g + 1)], v[:, 256 * g:256 * g + 256]), col_head)
              for g in range(2)]
        kb = (k.astype(F32) * b_ref[...]).astype(BF16)
        R[...] = R[...] * lam_ref[...] + _tn(kb, v) * bd_ref[...]
        o_ref[:, 0:256] = og[0]
        o_ref[:, 256:512] = og[1]
        for j in range(4):
            oj = og[j // 2][:, 128 * (j % 2):128 * (j % 2) + 128]
            xc = oj - _seg_mean(oj)
            rn = xc * lax.rsqrt(_seg_mean(xc * xc) + GN_EPS)
            gj = g_ref[:, 128 * j:128 * j + 128]
            cat_ref[:, 128 * j:128 * j + 128] = (rn * (gj * _sigmoid(gj))).astype(BF16)

        @pl.when(pl.program_id(0) == S // C - 1)
        def _():
            exch.finish()

    row = lambda w: pl.BlockSpec((C, w), lambda n: (n, 0))
    full = lambda a: pl.BlockSpec(a.shape, lambda n: (0,) * a.ndim)
    return _carry(
        "ret_fwd", body, exchange, exchange_args, (qr, kr, rv, proj, dtab, a_tab, b_tab, lam, bd),
        [row(256), row(256), row(512), row(512),
         full(dtab), full(a_tab), full(b_tab), full(lam), full(bd)],
        [row(512), row(512), pl.BlockSpec((None, 256, 512), lambda n: (n, 0, 0))],
        [jax.ShapeDtypeStruct((S, 512), F32), jax.ShapeDtypeStruct((S, 512), BF16),
         jax.ShapeDtypeStruct((S // C, 256, 512), BF16)],
        scratch_shapes=[pltpu.VMEM((256, 512), F32)], grid=(S // C,), semantics=("arbitrary",))


def _stack_heads(v, lane_head, fill=0.0, n=4):
    return jnp.concatenate([jnp.where(lane_head == h, v, jnp.full_like(v, fill)) for h in range(n)], axis=0)


def _unstack_heads(v, lane_head, n=4):
    out = v[0:ATT_BLK]
    for h in range(1, n):
        out = jnp.where(lane_head == h, v[h * ATT_BLK:(h + 1) * ATT_BLK], out)
    return out


def _att_bias(has_prev):
    nk = 2 * ATT_BLK if has_prev else ATT_BLK
    a = lax.broadcasted_iota(jnp.int32, (4 * ATT_BLK, nk), 0) % ATT_BLK
    kk = lax.broadcasted_iota(jnp.int32, (4 * ATT_BLK, nk), 1)
    if not has_prev:
        return None, jnp.where((a - kk) >= 0, 0.0, NEG)
    dist = ATT_BLK + a - kk
    inside = (dist >= 0) & (dist <= ATT_BLK)
    return jnp.where(inside, 0.0, NEG), jnp.where(inside & (kk >= ATT_BLK), 0.0, NEG)


def _class_rows(ib, r, d):
    if d == 1:
        return pl.ds(pl.multiple_of(ib * ATT_BLK, ATT_BLK), ATT_BLK)
    return pl.ds(ib * ATT_BLK * d + r, ATT_BLK, stride=d)


def _slab_pair(ref, g, rows):
    return jnp.concatenate([ref[2 * g, rows, :], ref[2 * g + 1, rows, :]], axis=1)


def _att_blocks(d):
    nb = S // d // ATT_BLK
    return nb, nb > 1


def _att_fwd(aq, ak, av, exchange, exchange_args):
    def body(q_ref, k_ref, v_ref, o_ref, l_ref, cat_ref, xc):
        xc.start()
        lane_head = lax.broadcasted_iota(jnp.int32, (ATT_BLK, 256), 1) // 64
        for pi, d in enumerate(PATTERN_DILATIONS):
            if pi == len(PATTERN_DILATIONS) - 1:
                xc.middle()
            nb, has_prev = _att_blocks(d)
            bias_rest, bias_first = _att_bias(has_prev)

            def block(b, carry, pi=pi, d=d, nb=nb, has_prev=has_prev, bias_rest=bias_rest, bias_first=bias_first):
                r, ib = b // nb, b % nb
                rows = _class_rows(ib, r, d)
                prow = _class_rows(jnp.maximum(ib - 1, 0), r, d)
                bias = jnp.where(ib == 0, bias_first, bias_rest) if has_prev else bias_first
                for g in range(2):
                    qg = _slab_pair(q_ref, g, rows).astype(BF16)
                    kg = _slab_pair(k_ref, g, rows)
                    vg = _slab_pair(v_ref, g, rows)
                    if has_prev:
                        kg = jnp.concatenate([_slab_pair(k_ref, g, prow), kg], axis=0)
                        vg = jnp.concatenate([_slab_pair(v_ref, g, prow), vg], axis=0)
                    kg, vg = kg.astype(BF16), vg.astype(BF16)
                    s = _nt(_stack_heads(qg, lane_head), kg) * ATT_SCALE + bias
                    m = jnp.max(s, axis=-1, keepdims=True)
                    p = jnp.exp(s - m)
                    den = jnp.sum(p, axis=-1, keepdims=True)
                    og = _unstack_heads(_nn(p.astype(BF16), vg) / den, lane_head)
                    lg = _unstack_heads(jnp.broadcast_to(m + jnp.log(den), (4 * ATT_BLK, 256)), lane_head)
                    for jj in range(2):
                        j = 2 * g + jj
                        o_new, l_new = og[:, 128 * jj:128 * jj + 128], lg[:, 128 * jj:128 * jj + 128]
                        if pi > 0:
                            o_old, l_old = o_ref[j, rows, :], l_ref[j, rows, :]
                            mx = jnp.maximum(l_old, l_new)
                            ea, eb = jnp.exp(l_old - mx), jnp.exp(l_new - mx)
                            den = ea + eb
                            o_new = (ea * o_old + eb * o_new) / den
                            l_new = mx + jnp.log(den)
                        o_ref[j, rows, :] = o_new
                        l_ref[j, rows, :] = l_new
                return carry

            lax.fori_loop(0, S // ATT_BLK, block, 0)

        def to_cat(i, carry):
            rows = _rows(i, 256)
            for j in range(4):
                cat_ref[rows, 128 * j:128 * j + 128] = o_ref[j, rows, :].astype(BF16)
            return carry

        lax.fori_loop(0, S // 256, to_cat, 0)
        xc.finish()

    slab = jax.ShapeDtypeStruct((4, S, 128), F32)
    return _carry("att_fwd", body, exchange, exchange_args, (aq, ak, av), [VMEM] * 3, [VMEM] * 3,
                  [slab, slab, jax.ShapeDtypeStruct((S, 512), BF16)])


def _mix_fwd(cat_r, cat_a, wout, x, g2, g3, exchange, exchange_args):
    tm = 512

    def body(cr_ref, ca_ref, w_ref, x_ref, g2_ref, g3_ref, mix_ref, x2_ref, h3_ref, xc):
        @pl.when(pl.program_id(0) == 0)
        def _():
            xc.start()

        mix = _nn(cr_ref[...], w_ref[0:512, :]) + _nn(ca_ref[...], w_ref[512:1024, :])
        mix_ref[...] = mix
        x2 = x_ref[...] + mix * _rstd(mix) * g2_ref[...]
        x2_ref[...] = x2
        h3_ref[...] = (x2 * _rstd(x2) * g3_ref[...]).astype(BF16)

        @pl.when(pl.program_id(0) == S // tm - 1)
        def _():
            xc.middle()
            xc.finish()

    row = lambda w: pl.BlockSpec((tm, w), lambda i: (i, 0))
    vec = pl.BlockSpec((1, D), lambda i: (0, 0))
    return _carry("mix_fwd", body, exchange, exchange_args, (cat_r, cat_a, wout, x, g2, g3),
                  [row(512), row(512), pl.BlockSpec((D, D), lambda i: (0, 0)), row(D), vec, vec],
                  [row(D), row(D), row(D)],
                  [jax.ShapeDtypeStruct((S, D), F32), jax.ShapeDtypeStruct((S, D), F32),
                   jax.ShapeDtypeStruct((S, D), BF16)],
                  grid=(S // tm,), semantics=("arbitrary",))


def _ffn_fwd(h3, wg, wu, wd):
    tm = 512

    def body(h_ref, wg_ref, wu_ref, wd_ref, gt_ref, up_ref, a_ref, f_ref):
        k, i = pl.program_id(0), pl.program_id(1)
        h = h_ref[...]
        gt = _nt(h, wg_ref[...])
        up = _nt(h, wu_ref[...])
        gt_ref[...] = gt.astype(BF16)
        up_ref[...] = up.astype(BF16)
        a = (gt * _sigmoid(gt) * up).astype(BF16)
        a_ref[...] = a
        part = _nn(a, wd_ref[...])
        rows = _rows(i, tm)

        @pl.when(k == 0)
        def _():
            f_ref[rows, :] = part

        @pl.when(k > 0)
        def _():
            f_ref[rows, :] = f_ref[rows, :] + part

    wrow = pl.BlockSpec((None, FF_C, D), lambda k, i: (k, 0, 0))
    act = pl.BlockSpec((None, tm, FF_C), lambda k, i: (k, i, 0))
    return pl.pallas_call(
        body, grid=(N_CHIP, S // tm), name="ffn_fwd",
        in_specs=[pl.BlockSpec((tm, D), lambda k, i: (i, 0)), wrow, wrow, wrow],
        out_specs=[act, act, act, pl.BlockSpec((S, D), lambda k, i: (0, 0))],
        out_shape=[jax.ShapeDtypeStruct((N_CHIP, S, FF_C), BF16)] * 3 + [jax.ShapeDtypeStruct((S, D), F32)],
        compiler_params=_params("arbitrary", "arbitrary"),
    )(h3, wg, wu, wd)


def _head_bwd(f, x2, tgt, g4):
    tm = 256

    def body(f_ref, x2_ref, t_ref, g_ref, loss_ref, dy_ref, df_ref, dg_ref):
        @pl.when(pl.program_id(0) == 0)
        def _():
            loss_ref[...] = jnp.zeros_like(loss_ref)
            dg_ref[...] = jnp.zeros_like(dg_ref)

        fv = f_ref[...]
        r = _rstd(fv)
        fn = fv * r
        e = x2_ref[...] + fn * g_ref[...] - t_ref[...]
        sq = jnp.sum(jnp.sum(e * e, axis=-1, keepdims=True), axis=0, keepdims=True)
        loss_ref[...] = loss_ref[...] + sq
        dy = e * (1.0 / D)
        dy_ref[...] = dy
        dg_ref[...] = dg_ref[...] + jnp.sum(dy * fn, axis=0, keepdims=True)
        t = dy * g_ref[...]
        df_ref[...] = (r * (t - fn * jnp.mean(t * fn, axis=-1, keepdims=True))).astype(BF16)

    row = pl.BlockSpec((tm, D), lambda i: (i, 0))
    vec = pl.BlockSpec((1, D), lambda i: (0, 0))
    return pl.pallas_call(
        body, grid=(S // tm,), name="head_bwd",
        in_specs=[row, row, row, vec],
        out_specs=[vec, row, row, vec],
        out_shape=[jax.ShapeDtypeStruct((1, D), F32), jax.ShapeDtypeStruct((S, D), F32),
                   jax.ShapeDtypeStruct((S, D), BF16), jax.ShapeDtypeStruct((1, D), F32)],
        compiler_params=_params("arbitrary"),
    )(f, x2, tgt, g4)


def _ffn_bwd_act(df, gt, up, wg, wu, wd):
    tm, sub = 512, 256

    def body(df_ref, gt_ref, up_ref, wg_ref, wu_ref, wd_ref, dgt_ref, dup_ref, dh_ref):
        k, i = pl.program_id(0), pl.program_id(1)
        parts = []
        for s in range(tm // sub):
            rows = slice(s * sub, (s + 1) * sub)
            da = _nt(df_ref[rows, :], wd_ref[...])
            gt, up = gt_ref[rows, :].astype(F32), up_ref[rows, :].astype(F32)
            sg = _sigmoid(gt)
            dup = (da * gt * sg).astype(BF16)
            dgt = (da * up * (sg * (1.0 + gt * (1.0 - sg)))).astype(BF16)
            dup_ref[rows, :] = dup
            dgt_ref[rows, :] = dgt
            parts.append(_nn(dgt, wg_ref[...]) + _nn(dup, wu_ref[...]))
        part = jnp.concatenate(parts, axis=0)
        rows = _rows(i, tm)

        @pl.when(k == 0)
        def _():
            dh_ref[rows, :] = part

        @pl.when(k > 0)
        def _():
            dh_ref[rows, :] = dh_ref[rows, :] + part

    wrow = pl.BlockSpec((None, FF_C, D), lambda k, i: (k, 0, 0))
    act = pl.BlockSpec((None, tm, FF_C), lambda k, i: (k, i, 0))
    row = pl.BlockSpec((tm, D), lambda k, i: (i, 0))
    return pl.pallas_call(
        body, grid=(N_CHIP, S // tm), name="ffn_bwd_act",
        in_specs=[row, act, act, wrow, wrow, wrow],
        out_specs=[act, act, pl.BlockSpec((S, D), lambda k, i: (0, 0))],
        out_shape=[jax.ShapeDtypeStruct((N_CHIP, S, FF_C), BF16), jax.ShapeDtypeStruct((N_CHIP, S, FF_C), BF16),
                   jax.ShapeDtypeStruct((S, D), F32)],
        compiler_params=_params("arbitrary", "arbitrary"),
    )(df, gt, up, wg, wu, wd)


def _ffn_bwd_w(a, df, h3, dgt, dup):
    tm = 1024
    assert S // tm == 2

    def body(a_ref, df_ref, h_ref, dgt_ref, dup_ref, dwd_ref, dwg_ref, dwu_ref, acc_d, acc_g, acc_u):
        i = pl.program_id(1)
        h = h_ref[...]
        parts = (_tn(a_ref[...], df_ref[...]), _tn(dgt_ref[...], h), _tn(dup_ref[...], h))

        @pl.when(i == 0)
        def _():
            for acc, part in zip((acc_d, acc_g, acc_u), parts):
                acc[...] = part

        @pl.when(i == S // tm - 1)
        def _():
            for out, acc, part in zip((dwd_ref, dwg_ref, dwu_ref), (acc_d, acc_g, acc_u), parts):
                out[...] = (acc[...] + part).astype(BF16)

    act = pl.BlockSpec((None, tm, FF_C), lambda k, i: (k, i, 0))
    row = pl.BlockSpec((tm, D), lambda k, i: (i, 0))
    wrow = pl.BlockSpec((None, FF_C, D), lambda k, i: (k, 0, 0))
    return pl.pallas_call(
        body, grid=(N_CHIP, S // tm), name="ffn_bwd_w",
        in_specs=[act, row, row, act, act],
        out_specs=[wrow, wrow, wrow],
        out_shape=[jax.ShapeDtypeStruct((N_CHIP, FF_C, D), BF16)] * 3,
        scratch_shapes=[pltpu.VMEM((FF_C, D), F32)] * 3,
        compiler_params=_params("parallel", "arbitrary"),
    )(a, df, h3, dgt, dup)


def _norm_bwd(dh3, dy, x2, mix, g2, g3, exchange, exchange_args):
    tm = 256

    def body(dh_ref, dy_ref, x2_ref, mix_ref, g2_ref, g3_ref, dx2_ref, dmix_ref, dg3_ref, dg2_ref, xc):
        @pl.when(pl.program_id(0) == 0)
        def _():
            xc.start()
            dg3_ref[...] = jnp.zeros_like(dg3_ref)
            dg2_ref[...] = jnp.zeros_like(dg2_ref)

        x2 = x2_ref[...]
        r3 = _rstd(x2)
        xn = x2 * r3
        dh = dh_ref[...]
        dg3_ref[...] = dg3_ref[...] + jnp.sum(dh * xn, axis=0, keepdims=True)
        t = dh * g3_ref[...]
        dx2 = dy_ref[...] + r3 * (t - xn * jnp.mean(t * xn, axis=-1, keepdims=True))
        dx2_ref[...] = dx2
        mix = mix_ref[...]
        r2 = _rstd(mix)
        mn = mix * r2
        dg2_ref[...] = dg2_ref[...] + jnp.sum(dx2 * mn, axis=0, keepdims=True)
        u = dx2 * g2_ref[...]
        dmix_ref[...] = (r2 * (u - mn * jnp.mean(u * mn, axis=-1, keepdims=True))).astype(BF16)

        @pl.when(pl.program_id(0) == S // tm - 1)
        def _():
            xc.middle()
            xc.finish()

    row = pl.BlockSpec((tm, D), lambda i: (i, 0))
    vec = pl.BlockSpec((1, D), lambda i: (0, 0))
    return _carry("norm_bwd", body, exchange, exchange_args, (dh3, dy, x2, mix, g2, g3),
                  [row, row, row, row, vec, vec], [row, row, vec, vec],
                  [jax.ShapeDtypeStruct((S, D), F32), jax.ShapeDtypeStruct((S, D), BF16),
                   jax.ShapeDtypeStruct((1, D), F32), jax.ShapeDtypeStruct((1, D), F32)],
                  grid=(S // tm,), semantics=("arbitrary",))


def _mix_bwd(dmix, cat_r, cat_a, wout, after):
    tm = 512

    def body(dm_ref, cr_ref, ca_ref, w_ref, dret_ref, datt_ref, dw_ref, acc, _):
        i = pl.program_id(0)

        @pl.when(i == 0)
        def _():
            acc[...] = jnp.zeros_like(acc)

        dm = dm_ref[...]
        dret_ref[...] = _nt(dm, w_ref[0:512, :])
        datt = _nt(dm, w_ref[512:1024, :])
        for j in range(4):
            datt_ref[j] = datt[:, 128 * j:128 * j + 128]
        acc[0:512, :] += _tn(cr_ref[...], dm)
        acc[512:1024, :] += _tn(ca_ref[...], dm)

        @pl.when(i == S // tm - 1)
        def _():
            dw_ref[...] = acc[...].astype(BF16)

    row = lambda w: pl.BlockSpec((tm, w), lambda i: (i, 0))
    full = pl.BlockSpec((D, D), lambda i: (0, 0))
    return _carry("mix_bwd", body, _NoExchange(), (), (dmix, cat_r, cat_a, wout),
                  [row(D), row(512), row(512), full],
                  [row(512), pl.BlockSpec((4, tm, 128), lambda i: (0, i, 0)), full],
                  [jax.ShapeDtypeStruct((S, 512), F32), jax.ShapeDtypeStruct((4, S, 128), F32),
                   jax.ShapeDtypeStruct((D, D), BF16)],
                  scratch_shapes=[pltpu.VMEM((D, D), F32)], grid=(S // tm,), semantics=("arbitrary",), after=after)[0]


def _att_bwd(aq, ak, av, datt, att_out, lse, exchange, exchange_args):
    def body(q_ref, k_ref, v_ref, do_ref, out_ref, l_ref, dq_ref, dk_ref, dv_ref, xc):
        xc.start()

        def clear(i, carry):
            rows = _rows(i, 256)
            for ref in (dq_ref, dk_ref, dv_ref):
                for j in range(4):
                    ref[j, rows, :] = jnp.zeros((256, 128), F32)
            return carry

        lax.fori_loop(0, S // 256, clear, 0)
        lane_head = lax.broadcasted_iota(jnp.int32, (ATT_BLK, 256), 1) // 64
        for d in PATTERN_DILATIONS:
            nb, has_prev = _att_blocks(d)
            bias_rest, bias_first = _att_bias(has_prev)

            def block(b, carry, d=d, nb=nb, has_prev=has_prev, bias_rest=bias_rest, bias_first=bias_first):
                r, ib = b // nb, b % nb
                rows = _class_rows(ib, r, d)
                prow = _class_rows(jnp.maximum(ib - 1, 0), r, d)
                bias = jnp.where(ib == 0, bias_first, bias_rest) if has_prev else bias_first
                for g in range(2):
                    qg = _slab_pair(q_ref, g, rows).astype(BF16)
                    kg = _slab_pair(k_ref, g, rows)
                    vg = _slab_pair(v_ref, g, rows)
                    if has_prev:
                        kg = jnp.concatenate([_slab_pair(k_ref, g, prow), kg], axis=0)
                        vg = jnp.concatenate([_slab_pair(v_ref, g, prow), vg], axis=0)
                    kg, vg = kg.astype(BF16), vg.astype(BF16)
                    dog = _slab_pair(do_ref, g, rows)
                    outg = _slab_pair(out_ref, g, rows)
                    lg = _slab_pair(l_ref, g, rows)
                    qs = _stack_heads(qg, lane_head)
                    dos = _stack_heads(dog, lane_head)
                    delta = jnp.sum(dos * jnp.concatenate([outg] * 4, axis=0), axis=-1, keepdims=True)
                    lh = jnp.max(_stack_heads(lg, lane_head, NEG), axis=-1, keepdims=True)
                    s = _nt(qs, kg) * ATT_SCALE + bias
                    p = jnp.exp(s - lh)
                    dosb = dos.astype(BF16)
                    ds = (p * (_nt(dosb, vg) - delta) * ATT_SCALE).astype(BF16)
                    dq = _unstack_heads(_nn(ds, kg), lane_head)
                    dk = _tn(ds, qs)
                    dv = _tn(p.astype(BF16), dosb)
                    for jj in range(2):
                        j, sl = 2 * g + jj, slice(128 * jj, 128 * jj + 128)
                        dq_ref[j, rows, :] += dq[:, sl]
                        if has_prev:
                            dk_ref[j, prow, :] += dk[0:ATT_BLK, sl]
                            dv_ref[j, prow, :] += dv[0:ATT_BLK, sl]
                            dk_ref[j, rows, :] += dk[ATT_BLK:2 * ATT_BLK, sl]
                            dv_ref[j, rows, :] += dv[ATT_BLK:2 * ATT_BLK, sl]
                        else:
                            dk_ref[j, rows, :] += dk[:, sl]
                            dv_ref[j, rows, :] += dv[:, sl]
                return carry

            lax.fori_loop(0, S // ATT_BLK, block, 0)
        xc.middle()
        xc.finish()

    slab = jax.ShapeDtypeStruct((4, S, 128), F32)
    return _carry("att_bwd", body, exchange, exchange_args, (aq, ak, av, datt, att_out, lse), [VMEM] * 6, [VMEM] * 3,
                  [slab, slab, slab])


def _ret_bwd(qr, kr, rv, proj, o_raw, states, dret, tabs, exchange, exchange_args):
    C = RET_C
    nc = S // C
    dtab, a_tab, b_tab, lam, bd = tabs

    def body(q_ref, k_ref, v_ref, g_ref, o_ref, st_ref, dr_ref, dt_ref, a_ref, b_ref, lam_ref, bd_ref,
             dq_ref, dk_ref, dv_ref, dg_ref, dR, exch):
        @pl.when(pl.program_id(0) == 0)
        def _():
            exch.start()
            dR[...] = jnp.zeros_like(dR)

        q, k, v = q_ref[...], k_ref[...], v_ref[...]
        lane_head = lax.broadcasted_iota(jnp.int32, (C, 256), 1) // 32
        col_head = lax.broadcasted_iota(jnp.int32, (C, 256), 1) // 64
        dos = []
        for j in range(4):
            sl = slice(128 * j, 128 * j + 128)
            oj = o_ref[:, sl]
            xc = oj - _seg_mean(oj)
            rs = lax.rsqrt(_seg_mean(xc * xc) + GN_EPS)
            rn = xc * rs
            gj = g_ref[:, sl]
            sg = _sigmoid(gj)
            dret = dr_ref[:, sl]
            dg_ref[:, sl] = dret * rn * (sg * (1.0 + gj * (1.0 - sg)))
            drn = dret * (gj * sg)
            dos.append(rs * (drn - _seg_mean(drn) - rn * _seg_mean(drn * rn)))
        do = [jnp.concatenate(dos[0:2], axis=1), jnp.concatenate(dos[2:4], axis=1)]
        do8 = jnp.concatenate(do, axis=1).astype(BF16)
        drb = dR[...].astype(BF16)
        rb = st_ref[...]
        dq = _nt(do8, rb) * a_ref[...]
        dk = _nt(v, drb) * b_ref[...]
        kb = (k.astype(F32) * b_ref[...]).astype(BF16)
        dvall = _nn(kb, drb)
        qs = _stack_heads(q, lane_head, n=8)
        dec = dt_ref[...]
        p = (_nt(qs, k) * dec).astype(BF16)
        dos = [_stack_heads(do[g], col_head).astype(BF16) for g in range(2)]
        dp = jnp.concatenate([_nt(dos[g], v[:, 256 * g:256 * g + 256]) for g in range(2)], axis=0)
        ds = (dp * dec).astype(BF16)
        dq = dq + _unstack_heads(_nn(ds, k), lane_head, n=8)
        dk = dk + _tn(ds, qs)
        dv = [dvall[:, 256 * g:256 * g + 256] + _tn(p[4 * C * g:4 * C * (g + 1)], dos[g]) for g in range(2)]
        qa = (q.astype(F32) * a_ref[...]).astype(BF16)
        dR[...] = dR[...] * lam_ref[...] + _tn(qa, do8) * bd_ref[...]
        dq_ref[...] = dq
        dk_ref[...] = dk
        dv_ref[:, 0:256] = dv[0]
        dv_ref[:, 256:512] = dv[1]

        @pl.when(pl.program_id(0) == nc - 1)
        def _():
            exch.middle()
            exch.finish()

    rev = lambda w: pl.BlockSpec((C, w), lambda n: (nc - 1 - n, 0))
    full = lambda a: pl.BlockSpec(a.shape, lambda n: (0,) * a.ndim)
    return _carry(
        "ret_bwd", body, exchange, exchange_args, (qr, kr, rv, proj, o_raw, states, dret, dtab, a_tab, b_tab, lam, bd),
        [rev(256), rev(256), rev(512), rev(512), rev(512),
         pl.BlockSpec((None, 256, 512), lambda n: (nc - 1 - n, 0, 0)), rev(512),
         full(dtab), full(a_tab), full(b_tab), full(lam), full(bd)],
        [rev(256), rev(256), rev(512), rev(512)],
        [jax.ShapeDtypeStruct((S, 256), F32), jax.ShapeDtypeStruct((S, 256), F32),
         jax.ShapeDtypeStruct((S, 512), F32), jax.ShapeDtypeStruct((S, 512), F32)],
        scratch_shapes=[pltpu.VMEM((256, 512), F32)], grid=(nc,), semantics=("arbitrary",))


def _rot_bwd(cos, sin, dqr, dkr, drv, drg, dq_att, dk_att, dv_att):
    tm = 256

    def body(cos_ref, sin_ref, dqr_ref, dkr_ref, drv_ref, drg_ref, dqa_ref, dka_ref, dva_ref, dp_ref):
        cr, ca, sr, sa = cos_ref[:, 0:256], cos_ref[:, 256:768], sin_ref[:, 0:256], sin_ref[:, 256:768]
        lo_r, lo_a = _rot_halves(tm)

        def unrot_r(g):
            gs = g * sr
            return g * cr + pltpu.roll(jnp.where(lo_r, -gs, 0.0), 16, 1) + pltpu.roll(jnp.where(lo_r, 0.0, gs), 240, 1)

        def unrot_a(g):
            gs = g * sa
            return g * ca + pltpu.roll(jnp.where(lo_a, -gs, 0.0), 8, 1) + pltpu.roll(jnp.where(lo_a, 0.0, gs), 504, 1)

        def wide(ref):
            return jnp.concatenate([ref[j] for j in range(4)], axis=1)

        dp_ref[:, 0:256] = unrot_r(dqr_ref[...]).astype(BF16)
        dp_ref[:, 256:512] = unrot_r(dkr_ref[...] * RET_SCALE).astype(BF16)
        dp_ref[:, 512:1024] = drv_ref[...].astype(BF16)
        dp_ref[:, 1024:1536] = drg_ref[...].astype(BF16)
        dp_ref[:, 1536:2048] = unrot_a(wide(dqa_ref)).astype(BF16)
        dp_ref[:, 2048:2560] = unrot_a(wide(dka_ref)).astype(BF16)
        dp_ref[:, 2560:3072] = wide(dva_ref).astype(BF16)

    row = lambda w: pl.BlockSpec((tm, w), lambda i: (i, 0))
    slab = pl.BlockSpec((4, tm, 128), lambda i: (0, i, 0))
    return pl.pallas_call(
        body, grid=(S // tm,), name="rot_bwd",
        in_specs=[row(768), row(768), row(256), row(256), row(512), row(512), slab, slab, slab],
        out_specs=row(PW), out_shape=jax.ShapeDtypeStruct((S, PW), BF16),
        compiler_params=_params("parallel"),
    )(cos, sin, dqr, dkr, drv, drg, dq_att, dk_att, dv_att)


def _win_bwd_w(h1, dproj, exchange, exchange_args):
    tm = 512

    def body(h_ref, dp_ref, dw_ref, acc, xc):
        k, i = pl.program_id(0), pl.program_id(1)

        @pl.when((k == 0) & (i == 0))
        def _():
            xc.start()

        @pl.when(i == 0)
        def _():
            acc[...] = jnp.zeros_like(acc)

        acc[...] += _tn(h_ref[...], dp_ref[...])

        @pl.when(i == S // tm - 1)
        def _():
            dw_ref[...] = acc[...].astype(BF16)

        @pl.when((k == N_CHIP - 1) & (i == S // tm - 1))
        def _():
            xc.middle()
            xc.finish()

    (dw,), out = _carry(
        "win_bwd_w", body, exchange, exchange_args, (h1, dproj),
        [pl.BlockSpec((tm, D), lambda k, i: (i, 0)), pl.BlockSpec((tm, WIN_C), lambda k, i: (i, k))],
        [pl.BlockSpec((None, D, WIN_C), lambda k, i: (k, 0, 0))],
        [jax.ShapeDtypeStruct((N_CHIP, D, WIN_C), BF16)],
        scratch_shapes=[pltpu.VMEM((D, WIN_C), F32)], grid=(N_CHIP, S // tm), semantics=("arbitrary", "arbitrary"))
    return dw, out


def _in_bwd(dproj, win_g, x, dx2, g1, after):
    tm = 512

    def body(dp_ref, w_ref, x_ref, dx2_ref, g_ref, dx_ref, dg_ref, _):
        @pl.when(pl.program_id(0) == 0)
        def _():
            dg_ref[...] = jnp.zeros_like(dg_ref)

        dh = _nt(dp_ref[:, 0:WIN_C], w_ref[0])
        for k in range(1, N_CHIP):
            dh = dh + _nt(dp_ref[:, k * WIN_C:(k + 1) * WIN_C], w_ref[k])
        xv = x_ref[...]
        r = _rstd(xv)
        xn = xv * r
        dg_ref[...] = dg_ref[...] + jnp.sum(dh * xn, axis=0, keepdims=True)
        t = dh * g_ref[...]
        dx_ref[...] = dx2_ref[...] + r * (t - xn * jnp.mean(t * xn, axis=-1, keepdims=True))

    row = lambda w: pl.BlockSpec((tm, w), lambda i: (i, 0))
    vec = pl.BlockSpec((1, D), lambda i: (0, 0))
    return _carry("in_bwd", body, _NoExchange(), (), (dproj, win_g, x, dx2, g1),
                  [row(PW), pl.BlockSpec((N_CHIP, D, WIN_C), lambda i: (0, 0, 0)), row(D), row(D), vec],
                  [row(D), vec], [jax.ShapeDtypeStruct((S, D), F32), jax.ShapeDtypeStruct((1, D), F32)],
                  grid=(S // tm,), semantics=("arbitrary",), after=after)[0]


ANY = pl.BlockSpec(memory_space=pl.ANY)
VMEM = pl.BlockSpec(memory_space=pltpu.VMEM)
FLIPS = ((1, 0), (0, 1), (1, 1))


def _place():
    x, y, c = lax.axis_index("x"), lax.axis_index("y"), lax.axis_index("c")
    chips = [((1 - x) if fx else x, (1 - y) if fy else y) for fx, fy in FLIPS]
    return x, y, c, 2 * x + y, chips


def _remote(src, dst, send_sem, recv_sem, device):
    return pltpu.make_async_remote_copy(src_ref=src, dst_ref=dst, send_sem=send_sem, recv_sem=recv_sem,
                                        device_id=device, device_id_type=MESH)


def _staggered(issue):
    c = lax.axis_index("c")

    @pl.when(c == 0)
    def _():
        issue((0, 1, 2))

    @pl.when(c == 1)
    def _():
        issue((1, 0, 2))


class _Exchange:
    aliases = {}

    def middle(self, ins, outs, sems):
        pass


class _GatherShards(_Exchange):
    def __init__(self, shards):
        n = self.n = len(shards)
        self.n_in = self.n_out = n
        self.out_shape = [jax.ShapeDtypeStruct((N_CHIP,) + s.shape, s.dtype) for s in shards]
        dma = pltpu.SemaphoreType.DMA
        self.scratch = [dma((3 * n,)), dma((3 * n,)), dma((3 * n,)), dma((3 * n,)), dma((n,)), dma((n,))]

    def _ici(self, ins, outs, sems, a, j, chip):
        x, y, c, me, chips = _place()
        half = ins[a].shape[0] // 2
        return _remote(ins[a].at[pl.ds(c * half, half), :], outs[a].at[me, pl.ds(c * half, half), :],
                       sems[0].at[3 * a + j], sems[1].at[3 * a + j], (*chip, c))

    def _fwd(self, outs, sems, a, j, chip, half_of):
        x, y, c, me, chips = _place()
        half = outs[a].shape[1] // 2
        blk = outs[a].at[2 * chip[0] + chip[1], pl.ds(half_of * half, half), :]
        return _remote(blk, blk, sems[2].at[3 * a + j], sems[3].at[3 * a + j], (x, y, 1 - c))

    def _own(self, ins, outs, sems, a):
        return _own_shard_to_sibling(ins[a], outs[a], sems[4].at[a], sems[5].at[a])

    def start(self, ins, outs, sems):
        chips = _place()[4]

        def issue(order):
            for a in range(self.n):
                for j in order:
                    self._ici(ins, outs, sems, a, j, chips[j]).start()

        _staggered(issue)
        for a in range(self.n):
            self._own(ins, outs, sems, a).start()

    def middle(self, ins, outs, sems):
        x, y, c, me, chips = _place()
        for a in range(self.n):
            for j, chip in enumerate(chips):
                half = outs[a].shape[1] // 2
                blk = outs[a].at[2 * chip[0] + chip[1], pl.ds(c * half, half), :]
                _remote(blk, blk, sems[0].at[3 * a + j], sems[1].at[3 * a + j], (x, y, c)).wait_recv()
                self._fwd(outs, sems, a, j, chip, c).start()

    def finish(self, ins, outs, sems):
        x, y, c, me, chips = _place()
        for a in range(self.n):
            for j, chip in enumerate(chips):
                self._fwd(outs, sems, a, j, chip, 1 - c).wait_recv()
        for a in range(self.n):
            for j, chip in enumerate(chips):
                self._ici(ins, outs, sems, a, j, chip).wait_send()
                self._fwd(outs, sems, a, j, chip, c).wait_send()
            self._own(ins, outs, sems, a).wait()


def _own_shard_to_sibling(shard_ref, gathered_ref, send_sem, recv_sem):
    x, y, c, me, chips = _place()
    return _remote(shard_ref, gathered_ref.at[me], send_sem, recv_sem, (x, y, 1 - c))


class _NoExchange(_Exchange):
    n_in = n_out = 0
    out_shape = ()
    scratch = ()

    def start(self, ins, outs, sems):
        pass

    def finish(self, ins, outs, sems):
        pass


class _ForwardGathered(_Exchange):
    def __init__(self, shards, own=True, forward=True):
        self.own, self.forward = own, forward
        n = self.n = len(shards)
        self.n_in, self.n_out = 2 * n, n
        self.out_shape = [jax.ShapeDtypeStruct((N_CHIP,) + s.shape, s.dtype) for s in shards]
        dma = pltpu.SemaphoreType.DMA
        self.scratch = [dma((3 * n,)), dma((3 * n,)), dma((n,)), dma((n,))]
        self.aliases = {n + a: a for a in range(n)}

    def _fwd(self, outs, sems, a, j, chip, half_of):
        x, y, c, me, chips = _place()
        half = outs[a].shape[1] // 2
        blk = outs[a].at[2 * chip[0] + chip[1], pl.ds(half_of * half, half), :]
        return _remote(blk, blk, sems[0].at[3 * a + j], sems[1].at[3 * a + j], (x, y, 1 - c))

    def _own(self, ins, outs, sems, a):
        return _own_shard_to_sibling(ins[a], outs[a], sems[2].at[a], sems[3].at[a])

    def start(self, ins, outs, sems):
        x, y, c, me, chips = _place()
        for a in range(self.n):
            for j, chip in enumerate(chips if self.forward else ()):
                self._fwd(outs, sems, a, j, chip, c).start()
        for a in range(self.n if self.own else 0):
            self._own(ins, outs, sems, a).start()

    def finish(self, ins, outs, sems):
        x, y, c, me, chips = _place()
        for a in range(self.n):
            for j, chip in enumerate(chips if self.forward else ()):
                self._fwd(outs, sems, a, j, chip, 1 - c).wait_recv()
        for a in range(self.n):
            for j, chip in enumerate(chips if self.forward else ()):
                self._fwd(outs, sems, a, j, chip, c).wait_send()
            if self.own:
                self._own(ins, outs, sems, a).wait()


HBM = pl.BlockSpec(memory_space=pltpu.HBM)
SEMS = pl.BlockSpec(memory_space=pltpu.SEMAPHORE)
DATAFLOW = pltpu.SideEffectType.DATAFLOW_SIDE_EFFECTING


class _OverIci:
    def __init__(self, name, sources, lands):
        self.name, self.n = name, len(sources)
        hbm = lambda t: pltpu.with_memory_space_constraint(t, pltpu.HBM)
        self.arrays = [hbm(t) for t in sources] + [hbm(t) for t in lands]

    def sent(self, src, land, a, chip):
        raise NotImplementedError

    def landed(self, land, a, chip):
        raise NotImplementedError

    def _copy(self, arr, sems, a, j, receiving):
        x, y, c, me, chips = _place()
        src, dst = self.sent(arr[a], arr[self.n + a], a, chips[j])
        if receiving:
            dst = self.landed(arr[self.n + a], a, chips[j])
        return _remote(src, dst, sems[0].at[3 * a + j], sems[1].at[3 * a + j], (*chips[j], c))

    def start(self, after):
        m = len(self.arrays)

        def body(*refs):
            arr, sems, token = refs[:m], refs[m + 1:m + 3], refs[-1]

            def issue(order):
                for a in range(self.n):
                    for j in order:
                        self._copy(arr, sems, a, j, False).start()

            _staggered(issue)
            token[...] = jnp.zeros_like(token)

        dma = pltpu.SemaphoreType.DMA
        outs = pl.pallas_call(
            body, name=self.name + "_start",
            out_shape=[dma((3 * self.n,)), dma((3 * self.n,))] + [pltpu.HBM(t.shape, t.dtype) for t in self.arrays]
                      + [jax.ShapeDtypeStruct((8, 128), F32)],
            in_specs=[HBM] * m + [ANY], out_specs=[SEMS, SEMS] + [HBM] * m + [VMEM],
            input_output_aliases={i: 2 + i for i in range(m)},
            compiler_params=pltpu.CompilerParams(has_side_effects=DATAFLOW),
        )(*self.arrays, after)
        self.sems, self.arrays = outs[0:2], list(outs[2:2 + m])
        return outs[-1]

    def wait(self, after):
        m = len(self.arrays)

        def body(*refs):
            arr, sems = refs[:m], refs[m:m + 2]
            for a in range(self.n):
                for j in range(3):
                    self._copy(arr, sems, a, j, False).wait_send()
                    self._copy(arr, sems, a, j, True).wait_recv()

        outs = pl.pallas_call(
            body, name=self.name + "_wait",
            out_shape=[pltpu.HBM(t.shape, t.dtype) for t in self.arrays],
            in_specs=[HBM] * m + [SEMS, SEMS, ANY], out_specs=[HBM] * m,
            input_output_aliases={i: i for i in range(m)},
            compiler_params=pltpu.CompilerParams(has_side_effects=DATAFLOW),
        )(*self.arrays, *self.sems, after)
        return list(outs[:self.n]), list(outs[self.n:])


class _GatherOverIci(_OverIci):
    def __init__(self, name, shards):
        super().__init__(name, shards, [lax.empty((N_CHIP,) + s.shape, s.dtype) for s in shards])

    @staticmethod
    def _half(ref):
        c = lax.axis_index("c")
        half = ref.shape[-2] // 2
        return pl.ds(c * half, half)

    def sent(self, src, land, a, chip):
        return src.at[self._half(src), :], land.at[_place()[3], self._half(src), :]

    def landed(self, land, a, chip):
        return land.at[2 * chip[0] + chip[1], self._half(land), :]


class _SumOverIci(_OverIci):
    def __init__(self, name, pre):
        super().__init__(name, pre, [lax.empty(p.shape, p.dtype) for p in pre])

    def sent(self, src, land, a, chip):
        return src.at[2 * chip[0] + chip[1]], land.at[_place()[3]]

    def landed(self, land, a, chip):
        return land.at[2 * chip[0] + chip[1]]


class _HalvesToSibling(_Exchange):
    def __init__(self, grads):
        n = self.n = len(grads)
        self.n_in = self.n_out = n
        self.out_shape = [jax.ShapeDtypeStruct((N_CHIP, g.shape[1] // 2, g.shape[2]), g.dtype) for g in grads]
        self.scratch = [pltpu.SemaphoreType.DMA((n,)), pltpu.SemaphoreType.DMA((n,))]

    def _copy(self, ins, outs, sems, a):
        x, y, c, me, chips = _place()
        half = ins[a].shape[1] // 2
        return _remote(ins[a].at[:, pl.ds((1 - c) * half, half), :], outs[a], sems[0].at[a], sems[1].at[a], (x, y, 1 - c))

    def start(self, ins, outs, sems):
        for a in range(self.n):
            self._copy(ins, outs, sems, a).start()

    def finish(self, ins, outs, sems):
        for a in range(self.n):
            self._copy(ins, outs, sems, a).wait_recv()
        for a in range(self.n):
            self._copy(ins, outs, sems, a).wait_send()


class _OverChips(_Exchange):
    def __init__(self, pre):
        n = self.n = len(pre)
        self.n_in = self.n_out = n
        self.out_shape = [jax.ShapeDtypeStruct(p.shape, p.dtype) for p in pre]
        dma = pltpu.SemaphoreType.DMA
        self.scratch = [dma((3 * n,)), dma((3 * n,))]

    def _ici(self, ins, outs, sems, a, j, chip):
        x, y, c, me, chips = _place()
        return _remote(ins[a].at[2 * chip[0] + chip[1]], outs[a].at[me], sems[0].at[3 * a + j], sems[1].at[3 * a + j],
                       (*chip, c))

    def start(self, ins, outs, sems):
        chips = _place()[4]

        def issue(order):
            for a in range(self.n):
                for j in order:
                    self._ici(ins, outs, sems, a, j, chips[j]).start()

        _staggered(issue)

    def finish(self, ins, outs, sems):
        x, y, c, me, chips = _place()
        for a in range(self.n):
            for j, chip in enumerate(chips):
                blk = outs[a].at[2 * chip[0] + chip[1]]
                _remote(blk, blk, sems[0].at[3 * a + j], sems[1].at[3 * a + j], (x, y, c)).wait_recv()
        for a in range(self.n):
            for j, chip in enumerate(chips):
                self._ici(ins, outs, sems, a, j, chip).wait_send()


class _ShareHalves(_Exchange):
    def __init__(self, fulls):
        n = self.n = len(fulls)
        self.n_in = self.n_out = n
        self.out_shape = [jax.ShapeDtypeStruct(f.shape, f.dtype) for f in fulls]
        self.scratch = [pltpu.SemaphoreType.DMA((n,)), pltpu.SemaphoreType.DMA((n,))]
        self.aliases = {a: a for a in range(n)}

    def _copy(self, outs, sems, a, half_of):
        x, y, c, me, chips = _place()
        half = outs[a].shape[0] // 2
        rows = outs[a].at[pl.ds(half_of * half, half), :]
        return _remote(rows, rows, sems[0].at[a], sems[1].at[a], (x, y, 1 - c))

    def start(self, ins, outs, sems):
        c = _place()[2]
        for a in range(self.n):
            self._copy(outs, sems, a, c).start()

    def finish(self, ins, outs, sems):
        c = _place()[2]
        for a in range(self.n):
            self._copy(outs, sems, a, 1 - c).wait_recv()
        for a in range(self.n):
            self._copy(outs, sems, a, c).wait_send()


class _GatherBlocks(_Exchange):
    def __init__(self, block):
        self.n_in = self.n_out = 1
        self.out_shape = [jax.ShapeDtypeStruct((8,) + block.shape, block.dtype)]
        dma = pltpu.SemaphoreType.DMA
        self.scratch = [dma((7,)), dma((7,)), dma]

    @staticmethod
    def _peer(f):
        x, y, c, me, chips = _place()
        return ((1 - x) if f & 4 else x, (1 - y) if f & 2 else y, (1 - c) if f & 1 else c)

    def start(self, ins, outs, sems):
        x, y, c, me, chips = _place()
        for f in range(1, 8):
            _remote(ins[0], outs[0].at[2 * me + c], sems[0].at[f - 1], sems[1].at[f - 1], self._peer(f)).start()
        pltpu.make_async_copy(ins[0], outs[0].at[2 * me + c], sems[2]).start()

    def finish(self, ins, outs, sems):
        x, y, c, me, chips = _place()
        for f in range(1, 8):
            px, py, pc = self._peer(f)
            blk = outs[0].at[4 * px + 2 * py + pc]
            _remote(blk, blk, sems[0].at[f - 1], sems[1].at[f - 1], (x, y, c)).wait_recv()
        for f in range(1, 8):
            _remote(ins[0], outs[0].at[2 * me + c], sems[0].at[f - 1], sems[1].at[f - 1], self._peer(f)).wait_send()
        pltpu.make_async_copy(ins[0], outs[0].at[2 * me + c], sems[2]).wait()


class _Both(_Exchange):
    def __init__(self, first, second):
        self.parts = (first, second)
        self.n_in, self.n_out = first.n_in + second.n_in, first.n_out + second.n_out
        self.out_shape = first.out_shape + second.out_shape
        self.scratch = first.scratch + second.scratch
        self.aliases = dict(first.aliases)
        self.aliases.update({first.n_in + i: first.n_out + o for i, o in second.aliases.items()})

    def _split(self, ins, outs, sems):
        a, b = self.parts
        return ((a, ins[:a.n_in], outs[:a.n_out], sems[:len(a.scratch)]),
                (b, ins[a.n_in:], outs[a.n_out:], sems[len(a.scratch):]))

    def start(self, ins, outs, sems):
        for ex, i, o, s in self._split(ins, outs, sems):
            ex.start(i, o, s)

    def middle(self, ins, outs, sems):
        for ex, i, o, s in self._split(ins, outs, sems):
            ex.middle(i, o, s)

    def finish(self, ins, outs, sems):
        for ex, i, o, s in self._split(ins, outs, sems):
            ex.finish(i, o, s)


class _Bound:
    def __init__(self, ex, ins, outs, sems):
        self.start = lambda: ex.start(ins, outs, sems)
        self.middle = lambda: ex.middle(ins, outs, sems)
        self.finish = lambda: ex.finish(ins, outs, sems)


def _carry(name, body, ex, ex_args, args, in_specs, out_specs, out_shape, scratch_shapes=(), grid=None, semantics=(),
           after=None):
    n_a, n_o, n_s = len(args), len(out_shape), len(scratch_shapes)
    behind = [] if after is None else [after]

    def full_body(*refs):
        p = 0
        groups = []
        for size in (n_a, ex.n_in, len(behind), n_o, ex.n_out, n_s, len(ex.scratch)):
            groups.append(refs[p:p + size])
            p += size
        a, ei, _, o, eo, s, es = groups
        body(*a, *o, *s, _Bound(ex, ei, eo, es))

    kwargs = {} if grid is None else {"grid": grid}
    outs = pl.pallas_call(
        full_body, name=name,
        in_specs=list(in_specs) + [ANY] * (ex.n_in + len(behind)), out_specs=list(out_specs) + [ANY] * ex.n_out,
        out_shape=list(out_shape) + list(ex.out_shape), scratch_shapes=list(scratch_shapes) + list(ex.scratch),
        input_output_aliases={n_a + i: n_o + o for i, o in ex.aliases.items()},
        compiler_params=_params(*semantics) if semantics else pltpu.CompilerParams(vmem_limit_bytes=VMEM_LIMIT),
        **kwargs,
    )(*args, *ex_args, *behind)
    return outs[:n_o], outs[n_o:]


def _prepare_carrying(name, x, g1, pos, ifc, spread, arrays, ex, ex_args):
    n = len(arrays)
    r, cc = arrays[0].shape
    steps = 4
    tr, tm = r // steps, S // steps

    def body(x_ref, g_ref, pos_ref, ifc_ref, e_ref, *refs):
        src, h_ref, cos_ref, sin_ref, dst, xc = refs[:n], refs[n], refs[n + 1], refs[n + 2], refs[n + 3:2 * n + 3], refs[-1]

        @pl.when(pl.program_id(0) == 0)
        def _():
            xc.start()

        xv = x_ref[...]
        h_ref[...] = (xv * _rstd(xv) * g_ref[...]).astype(BF16)
        ang = pos_ref[...].astype(F32) * ifc_ref[...]
        cos_ref[...] = _spread_exact(jnp.cos(ang), e_ref[...])
        sin_ref[...] = _spread_exact(jnp.sin(ang), e_ref[...])
        for a in range(n):
            dst[a][...] = src[a][...].astype(BF16)

        @pl.when(pl.program_id(0) == steps - 1)
        def _():
            xc.middle()
            xc.finish()

    row = lambda w: pl.BlockSpec((tm, w), lambda i: (i, 0))
    const = lambda w: pl.BlockSpec((1, w), lambda i: (0, 0))
    blk = pl.BlockSpec((tr, cc), lambda i: (i, 0))
    return _carry(name, body, ex, ex_args, (x, g1, pos, ifc, spread, *arrays),
                  [row(D), const(D), row(1), const(128), pl.BlockSpec((128, 768), lambda i: (0, 0))] + [blk] * n,
                  [row(D), row(768), row(768)] + [blk] * n,
                  [jax.ShapeDtypeStruct((S, D), BF16)] + [jax.ShapeDtypeStruct((S, 768), F32)] * 2
                  + [jax.ShapeDtypeStruct((r, cc), BF16)] * n,
                  grid=(steps,), semantics=("arbitrary",))


def _exchange_alone(name, ex, ex_args):
    def body(xc):
        xc.start()
        xc.middle()
        xc.finish()

    return _carry(name, body, ex, ex_args, (), (), (), ())[1]


def _core_index():
    return lax.axis_index("c").astype(jnp.int32).reshape(1)


def _pair_sum(gs, gots):
    n = len(gs)
    _, r, cc = gs[0].shape
    half = r // 2

    def body(c_ref, *refs):
        for a in range(n):
            refs[2 * n + a][...] = (refs[a][...].astype(F32) + refs[n + a][...].astype(F32)).astype(BF16)

    mine = pl.BlockSpec((None, half, cc), lambda k, c_ref: (k, c_ref[0], 0))
    blk = pl.BlockSpec((None, half, cc), lambda k, c_ref: (k, 0, 0))
    return pl.pallas_call(
        body, name=f"pair_sum_{r}x{cc}",
        grid_spec=pltpu.PrefetchScalarGridSpec(
            num_scalar_prefetch=1, grid=(N_CHIP,), in_specs=[mine] * n + [blk] * n, out_specs=[blk] * n),
        out_shape=[jax.ShapeDtypeStruct((N_CHIP, half, cc), BF16)] * n,
        compiler_params=_params("parallel"),
    )(_core_index(), *gs, *gots)


def _chip_sum(pre, parts):
    n = len(parts)
    _, half, cc = parts[0].shape
    tr = half // 2
    me = 2 * lax.axis_index("x") + lax.axis_index("y")
    others = [k + (k >= me).astype(jnp.int32) for k in range(3)]
    where = jnp.stack([lax.axis_index("c"), me, *others]).astype(jnp.int32)

    def body(w_ref, *refs):
        for a in range(n):
            own, p1, p2, p3 = refs[4 * a:4 * a + 4]
            refs[4 * n + a][...] = ((own[...].astype(F32) + p1[...].astype(F32)) + p2[...].astype(F32)) + p3[...].astype(F32)

    slot = lambda s: pl.BlockSpec((None, tr, cc), lambda i, w_ref: (w_ref[s], i, 0))
    operands = []
    for a in range(n):
        operands += [pre[a], parts[a], parts[a], parts[a]]
    return pl.pallas_call(
        body, name=f"chip_sum_{half}x{cc}",
        grid_spec=pltpu.PrefetchScalarGridSpec(
            num_scalar_prefetch=1, grid=(2,),
            in_specs=[slot(1), slot(2), slot(3), slot(4)] * n,
            out_specs=[pl.BlockSpec((tr, cc), lambda i, w_ref: (2 * w_ref[0] + i, 0))] * n),
        out_shape=[jax.ShapeDtypeStruct((2 * half, cc), F32)] * n,
        compiler_params=_params("parallel"),
    )(where, *operands)


def _adamw_math(w, g, m, v):
    m = ADAM_B1 * m + (1.0 - ADAM_B1) * g
    v = ADAM_B2 * v + (1.0 - ADAM_B2) * (g * g)
    m_hat = m / (1.0 - ADAM_B1 ** ADAM_STEP)
    v_hat = v / (1.0 - ADAM_B2 ** ADAM_STEP)
    delta = -ADAM_LR * (m_hat / (jnp.sqrt(v_hat) + ADAM_EPS) + ADAM_WD * w)
    return delta, m, v


def _adamw(w, g, m, v, after=None):
    r, cc = w.shape
    tr = r // 4

    def body(w_ref, g_ref, m_ref, v_ref, go_ref, d_ref, nm_ref, nv_ref, _):
        g = g_ref[...]
        go_ref[...] = g
        d_ref[...], nm_ref[...], nv_ref[...] = _adamw_math(w_ref[...], g, m_ref[...], v_ref[...])

    blk = pl.BlockSpec((tr, cc), lambda i: (i, 0))
    return _carry(f"adamw_{r}x{cc}", body, _NoExchange(), (), (w, g, m, v), [blk] * 4, [blk] * 4,
                  [jax.ShapeDtypeStruct((r, cc), F32)] * 4, grid=(4,), semantics=("parallel",), after=after)[0]


def _pack8(rows):
    def body(*refs):
        out_ref = refs[-1]
        out_ref[...] = jnp.zeros_like(out_ref)
        for i, r in enumerate(refs[:-1]):
            out_ref[i:i + 1, :] = r[...]

    return pl.pallas_call(body, name="pack8", out_shape=jax.ShapeDtypeStruct((8, D), F32))(*rows)


def _adamw_gains(gall, ws, ms, vs):
    def body(ga_ref, *refs):
        w, m, v = refs[0:4], refs[4:8], refs[8:12]
        outs, loss_ref, total = refs[12:28], refs[28], refs[29]
        g = ga_ref[0]
        for dev in range(1, 8):
            g = g + ga_ref[dev]
        total[...] = g
        for i in range(4):
            gi = total[i:i + 1, :]
            outs[i][...] = gi
            outs[4 + i][...], outs[8 + i][...], outs[12 + i][...] = _adamw_math(w[i][...], gi, m[i][...], v[i][...])
        loss_ref[...] = total[4:5, 0:128] * (0.5 / D)

    outs = pl.pallas_call(
        body, name="adamw_gains",
        out_shape=[jax.ShapeDtypeStruct((1, D), F32)] * 16 + [jax.ShapeDtypeStruct((1, 128), F32)],
        scratch_shapes=[pltpu.VMEM((8, D), F32)],
    )(gall, *ws, *ms, *vs)
    return outs[0:4], outs[4:8], outs[8:12], outs[12:16], outs[16]


def kernel(x, positions, w_in, w_out, g_pre_mix, g_post_mix, g_pre_ffn, g_post_ffn, w_gate, w_up, w_down, loss_target, m_w_in, m_w_out, m_g_pre_mix, m_g_post_mix, m_g_pre_ffn, m_g_post_ffn, m_w_gate, m_w_up, m_w_down, v_w_in, v_w_out, v_g_pre_mix, v_g_post_mix, v_g_pre_ffn, v_g_post_ffn, v_w_gate, v_w_up, v_w_down):
    tr = lambda t: jnp.swapaxes(t, 1, 2)[0]
    shards = [w_in[0], w_out[0], tr(w_gate), tr(w_up), w_down[0]]
    moms = [m_w_in[0], m_w_out[0], tr(m_w_gate), tr(m_w_up), m_w_down[0]]
    vels = [v_w_in[0], v_w_out[0], tr(v_w_gate), tr(v_w_up), v_w_down[0]]
    xs, pos, tgt = x[0], positions.reshape(S, 1), loss_target[0]
    g1, g2, g3, g4 = g_pre_mix, g_post_mix, g_pre_ffn, g_post_ffn
    tabs = tuple(jnp.asarray(t) for t in _retention_tables())
    ifc, spread = _rotary_tables()
    ifc, spread = jnp.asarray(ifc), jnp.asarray(spread, dtype=BF16)
    bf = [s.astype(BF16) for s in shards[:2]]

    (h1, cos, sin, *ffn_bf), (win_g, wout_g) = _prepare_carrying(
        "gather_in", xs, g1, pos, ifc, spread, shards[2:], _GatherShards(bf), bf)
    bf += list(ffn_bf)
    wout_g = wout_g.reshape(D, D)
    ffn_gather = _GatherOverIci("ffn_gather", bf[2:])
    token = ffn_gather.start(win_g)
    qr, kr, rv, rg, aq, ak, av = _proj_fwd(h1, win_g, cos, sin, token)
    (o_raw, cat_r, states), _ = _ret_fwd(qr, kr, rv, rg, tabs, _NoExchange(), ())
    n_ffn = len(bf[2:])
    (att_out, lse, cat_a), ffn_gather.arrays[n_ffn:] = _att_fwd(
        aq, ak, av, _ForwardGathered(bf[2:], forward=False), ffn_gather.arrays)
    ffn_sh, ffn_lands = ffn_gather.wait(cat_a)
    (mix, x2, h3), (wg_g, wu_g, wd_g) = _mix_fwd(cat_r, cat_a, wout_g, xs, g2, g3,
                                                _ForwardGathered(bf[2:], own=False), [*ffn_sh, *ffn_lands])
    gt, up, a, f = _ffn_fwd(h3, wg_g, wu_g, wd_g)

    sq, dy, df, dg4 = _head_bwd(f, x2, tgt, g4)
    dgt, dup, dh3 = _ffn_bwd_act(df, gt, up, wg_g, wu_g, wd_g)
    ffn_grads = list(_ffn_bwd_w(a, df, h3, dgt, dup))
    (dx2, dmix, dg3, dg2), got = _norm_bwd(dh3, dy, x2, mix, g2, g3, _HalvesToSibling(ffn_grads), ffn_grads)
    ffn_sum = _SumOverIci("ffn_sum", _pair_sum(ffn_grads, got))
    token = ffn_sum.start(dmix)
    dret, datt, dwout = _mix_bwd(dmix, cat_r, cat_a, wout_g, token)
    (dq_att, dk_att, dv_att), _ = _att_bwd(aq, ak, av, datt, att_out, lse, _NoExchange(), ())
    (dqr, dkr, drv, drg), _ = _ret_bwd(qr, kr, rv, rg, o_raw, states, dret, tabs, _NoExchange(), ())
    dproj = _rot_bwd(cos, sin, dqr, dkr, drv, drg, dq_att, dk_att, dv_att)
    sums = _chip_sum(*ffn_sum.wait(dproj))
    dwin, ffn_full = _win_bwd_w(h1, dproj, _ShareHalves(sums), sums)
    in_grads = [dwin, dwout.reshape(N_CHIP, WOUT_R, D)]

    got = _exchange_alone("halves_to_sibling", _HalvesToSibling(in_grads), in_grads)
    in_sum = _SumOverIci("in_sum", [*_pair_sum(in_grads[:1], got[:1]), *_pair_sum(in_grads[1:], got[1:])])
    token = in_sum.start(dproj)
    dx, dg1 = _in_bwd(dproj, win_g, xs, dx2, g1, token)
    ffn_upd = [_adamw(shards[2 + i], ffn_full[o], moms[2 + i], vels[2 + i], token)
               for i, o in enumerate((1, 2, 0))]
    pre, parts = in_sum.wait(ffn_upd[2][0])
    sums = [*_chip_sum(pre[:1], parts[:1]), *_chip_sum(pre[1:], parts[1:])]
    gblock = _pack8([dg1, dg2, dg3, dg4, sq])
    *in_full, gall = _exchange_alone("share_rest", _Both(_ShareHalves(sums), _GatherBlocks(gblock)), [*sums, gblock])
    upd = [_adamw(w, g, m, v) for w, g, m, v in zip(shards[:2], in_full, moms[:2], vels[:2])] + ffn_upd
    gg, gd, gm, gv, loss_row = _adamw_gains(gall, [g1, g2, g3, g4],
                                            [m_g_pre_mix, m_g_post_mix, m_g_pre_ffn, m_g_post_ffn],
                                            [v_g_pre_mix, v_g_post_mix, v_g_pre_ffn, v_g_post_ffn])

    def order(mats, vecs):
        back = lambda t: jnp.swapaxes(t[None], 1, 2)
        return [mats[0][None], mats[1][None], *vecs, back(mats[2]), back(mats[3]), mats[4][None]]

    return (loss_row[0, 0], dx[None],
            *order([u[0] for u in upd], gg),
            *order([u[1] for u in upd], gd),
            *order([u[2] for u in upd], gm),
            *order([u[3] for u in upd], gv))
```

```python
import numpy as np
import jax
import jax.numpy as jnp
from jax import lax
from jax.experimental import pallas as pl
from jax.experimental.pallas import tpu as pltpu

F32, BF16 = jnp.float32, jnp.bfloat16
MESH = pl.DeviceIdType.MESH

S = 2048
D = 1024
PW = 3072
N_CHIP = 4
WIN_C = PW // N_CHIP
DFF = 2816
FF_C = DFF // N_CHIP
WOUT_R = D // N_CHIP
RMS_EPS = 1e-6
GN_EPS = 1e-5
RET_C = 128
RET_SCALE = 32 ** -0.5
ATT_BLK = 128
ATT_SCALE = 64 ** -0.5
PATTERN_DILATIONS = (1, 4, 16)
NEG = -1e30
VMEM_LIMIT = 56 * 1024 * 1024

ADAM_LR, ADAM_B1, ADAM_B2, ADAM_EPS, ADAM_WD, ADAM_STEP = 0.001, 0.9, 0.999, 1e-08, 0.01, 10


def _params(*sem):
    return pltpu.CompilerParams(dimension_semantics=sem, vmem_limit_bytes=VMEM_LIMIT)


def _nt(a, b):
    return lax.dot_general(a, b, (((1,), (1,)), ((), ())), preferred_element_type=F32)


def _tn(a, b):
    return lax.dot_general(a, b, (((0,), (0,)), ((), ())), preferred_element_type=F32)


def _nn(a, b):
    return jnp.dot(a, b, preferred_element_type=F32)


def _rstd(v):
    return lax.rsqrt(jnp.mean(v * v, axis=-1, keepdims=True) + RMS_EPS)


def _sigmoid(v):
    return 1.0 / (1.0 + jnp.exp(-v))


def _rows(i, t):
    return pl.ds(pl.multiple_of(i * t, t), t)


def _retention_tables():
    h = np.arange(8, dtype=np.float32)
    log_g = np.log1p(-np.exp2(-5.0 - h)).astype(np.float32)
    idx = np.arange(RET_C, dtype=np.float32)
    diff = idx[:, None] - idx[None, :]
    dtab = np.where(diff >= 0, np.exp(log_g[:, None, None] * np.maximum(diff, 0.0)), 0.0).astype(np.float32)
    dtab = dtab.reshape(8 * RET_C, RET_C)
    lane_head = np.arange(256) // 32
    a_tab = np.exp(log_g[lane_head][None, :] * (idx + 1.0)[:, None]).astype(np.float32)
    b_tab = np.exp(log_g[lane_head][None, :] * (RET_C - 1.0 - idx)[:, None]).astype(np.float32)
    lam = np.exp(log_g[lane_head] * RET_C).astype(np.float32)[:, None]
    bd = (lane_head[:, None] == (np.arange(512) // 64)[None, :]).astype(np.float32)
    return dtab, a_tab, b_tab, lam, bd


def _rotary_tables():
    inv_r = (1.0 / (np.float32(10000.0) ** np.linspace(0.0, 1.0, 16, dtype=np.float32))).astype(np.float32)
    inv_a = (np.float32(500000.0) ** (-np.arange(0, 16, 2, dtype=np.float32) / np.float32(16))).astype(np.float32)
    ifc = np.zeros((1, 128), np.float32)
    ifc[0, 0:16], ifc[0, 16:24] = inv_r, inv_a
    spread = np.zeros((128, 768), np.float32)
    for lane in range(256):
        spread[(lane % 32) % 16, lane] = 1.0
    for lane in range(512):
        d = lane % 64
        spread[16 + d % 8 if d < 16 else 24, 256 + lane] = 1.0
    return ifc, spread


def _rot_halves(tm):
    lo_r = (lax.broadcasted_iota(jnp.int32, (tm, 256), 1) % 32) < 16
    lo_a = (lax.broadcasted_iota(jnp.int32, (tm, 512), 1) % 64) < 8
    return lo_r, lo_a


def _spread_exact(t, e):
    hi = t.astype(BF16)
    r1 = t - hi.astype(F32)
    mid = r1.astype(BF16)
    lo = (r1 - mid.astype(F32)).astype(BF16)
    return _nn(hi, e) + _nn(mid, e) + _nn(lo, e)


def _proj_fwd(h1, win_g, cos, sin, after):
    tm = 256

    def body(h_ref, w_ref, cos_ref, sin_ref, qr_ref, kr_ref, rv_ref, rg_ref, aq_ref, ak_ref, av_ref, p_ref, _):
        h = h_ref[...]
        for k in range(N_CHIP):
            p_ref[:, k * WIN_C:(k + 1) * WIN_C] = _nn(h, w_ref[k])
        cr, ca, sr, sa = cos_ref[:, 0:256], cos_ref[:, 256:768], sin_ref[:, 0:256], sin_ref[:, 256:768]
        lo_r, lo_a = _rot_halves(tm)

        def rot_r(v):
            return v * cr + sr * jnp.where(lo_r, -pltpu.roll(v, 240, 1), pltpu.roll(v, 16, 1))

        def rot_a(v):
            return v * ca + sa * jnp.where(lo_a, -pltpu.roll(v, 504, 1), pltpu.roll(v, 8, 1))

        qr_ref[...] = rot_r(p_ref[:, 0:256]).astype(BF16)
        kr_ref[...] = (rot_r(p_ref[:, 256:512]) * RET_SCALE).astype(BF16)
        rv_ref[...] = p_ref[:, 512:1024].astype(BF16)
        rg_ref[...] = p_ref[:, 1024:1536]
        aq, ak = rot_a(p_ref[:, 1536:2048]), rot_a(p_ref[:, 2048:2560])
        for j in range(4):
            aq_ref[j] = aq[:, 128 * j:128 * j + 128]
            ak_ref[j] = ak[:, 128 * j:128 * j + 128]
            av_ref[j] = p_ref[:, 2560 + 128 * j:2560 + 128 * j + 128]

    row = lambda w: pl.BlockSpec((tm, w), lambda i: (i, 0))
    slab = pl.BlockSpec((4, tm, 128), lambda i: (0, i, 0))
    return _carry(
        "proj_fwd", body, _NoExchange(), (), (h1, win_g, cos, sin),
        [row(D), pl.BlockSpec((N_CHIP, D, WIN_C), lambda i: (0, 0, 0)), row(768), row(768)],
        [row(256), row(256), row(512), row(512), slab, slab, slab],
        [jax.ShapeDtypeStruct((S, w), BF16) for w in (256, 256, 512)]
        + [jax.ShapeDtypeStruct((S, 512), F32)] + [jax.ShapeDtypeStruct((4, S, 128), F32)] * 3,
        scratch_shapes=[pltpu.VMEM((tm, PW), F32)], grid=(S // tm,), semantics=("parallel",), after=after)[0]


def _seg_mean(v):
    lo = lax.broadcasted_iota(jnp.int32, v.shape, 1) < 64
    s_lo = jnp.sum(jnp.where(lo, v, 0.0), axis=-1, keepdims=True)
    s_hi = jnp.sum(jnp.where(lo, 0.0, v), axis=-1, keepdims=True)
    return jnp.where(lo, s_lo, s_hi) * (1.0 / 64.0)


def _ret_fwd(qr, kr, rv, proj, tabs, exchange, exchange_args):
    C = RET_C
    dtab, a_tab, b_tab, lam, bd = tabs

    def body(q_ref, k_ref, v_ref, g_ref, dt_ref, a_ref, b_ref, lam_ref, bd_ref, o_ref, cat_ref, st_ref, R, exch):
        @pl.when(pl.program_id(0) == 0)
        def _():
            exch.start()
            R[...] = jnp.zeros_like(R)

        @pl.when(pl.program_id(0) == S // C // 2)
        def _():
            exch.middle()

        q, k, v = q_ref[...], k_ref[...], v_ref[...]
        lane_head = lax.broadcasted_iota(jnp.int32, (C, 256), 1) // 32
        col_head = lax.broadcasted_iota(jnp.int32, (C, 256), 1) // 64
        rb = R[...].astype(BF16)
        st_ref[...] = rb
        qa = (q.astype(F32) * a_ref[...]).astype(BF16)
        cross = _nn(qa, rb)
        p = (_nt(_stack_heads(q, lane_head, n=8), k) * dt_ref[...]).astype(BF16)
        og = [cross[:, 256 * g:256 * g + 256]
              + _unstack_heads(_nn(p[4 * C * g:4 * C * (g + 1)], v[:, 256 * g:256 * g + 256]), col_head)
              for g in range(2)]
        kb = (k.astype(F32) * b_ref[...]).astype(BF16)
        R[...] = R[...] * lam_ref[...] + _tn(kb, v) * bd_ref[...]
        o_ref[:, 0:256] = og[0]
        o_ref[:, 256:512] = og[1]
        for j in range(4):
            oj = og[j // 2][:, 128 * (j % 2):128 * (j % 2) + 128]
            xc = oj - _seg_mean(oj)
            rn = xc * lax.rsqrt(_seg_mean(xc * xc) + GN_EPS)
            gj = g_ref[:, 128 * j:128 * j + 128]
            cat_ref[:, 128 * j:128 * j + 128] = (rn * (gj * _sigmoid(gj))).astype(BF16)

        @pl.when(pl.program_id(0) == S // C - 1)
        def _():
            exch.finish()

    row = lambda w: pl.BlockSpec((C, w), lambda n: (n, 0))
    full = lambda a: pl.BlockSpec(a.shape, lambda n: (0,) * a.ndim)
    return _carry(
        "ret_fwd", body, exchange, exchange_args, (qr, kr, rv, proj, dtab, a_tab, b_tab, lam, bd),
        [row(256), row(256), row(512), row(512),
         full(dtab), full(a_tab), full(b_tab), full(lam), full(bd)],
        [row(512), row(512), pl.BlockSpec((None, 256, 512), lambda n: (n, 0, 0))],
        [jax.ShapeDtypeStruct((S, 512), F32), jax.ShapeDtypeStruct((S, 512), BF16),
         jax.ShapeDtypeStruct((S // C, 256, 512), BF16)],
        scratch_shapes=[pltpu.VMEM((256, 512), F32)], grid=(S // C,), semantics=("arbitrary",))


def _stack_heads(v, lane_head, fill=0.0, n=4):
    return jnp.concatenate([jnp.where(lane_head == h, v, jnp.full_like(v, fill)) for h in range(n)], axis=0)


def _unstack_heads(v, lane_head, n=4):
    out = v[0:ATT_BLK]
    for h in range(1, n):
        out = jnp.where(lane_head == h, v[h * ATT_BLK:(h + 1) * ATT_BLK], out)
    return out


def _att_bias(has_prev):
    nk = 2 * ATT_BLK if has_prev else ATT_BLK
    a = lax.broadcasted_iota(jnp.int32, (4 * ATT_BLK, nk), 0) % ATT_BLK
    kk = lax.broadcasted_iota(jnp.int32, (4 * ATT_BLK, nk), 1)
    if not has_prev:
        return None, jnp.where((a - kk) >= 0, 0.0, NEG)
    dist = ATT_BLK + a - kk
    inside = (dist >= 0) & (dist <= ATT_BLK)
    return jnp.where(inside, 0.0, NEG), jnp.where(inside & (kk >= ATT_BLK), 0.0, NEG)


def _class_rows(ib, r, d):
    if d == 1:
        return pl.ds(pl.multiple_of(ib * ATT_BLK, ATT_BLK), ATT_BLK)
    return pl.ds(ib * ATT_BLK * d + r, ATT_BLK, stride=d)


def _slab_pair(ref, g, rows):
    return jnp.concatenate([ref[2 * g, rows, :], ref[2 * g + 1, rows, :]], axis=1)


def _att_blocks(d):
    nb = S // d // ATT_BLK
    return nb, nb > 1


def _att_fwd(aq, ak, av, exchange, exchange_args):
    def body(q_ref, k_ref, v_ref, o_ref, l_ref, cat_ref, xc):
        xc.start()
        lane_head = lax.broadcasted_iota(jnp.int32, (ATT_BLK, 256), 1) // 64
        for pi, d in enumerate(PATTERN_DILATIONS):
            if pi == len(PATTERN_DILATIONS) - 1:
                xc.middle()
            nb, has_prev = _att_blocks(d)
            bias_rest, bias_first = _att_bias(has_prev)

            def block(b, carry, pi=pi, d=d, nb=nb, has_prev=has_prev, bias_rest=bias_rest, bias_first=bias_first):
                r, ib = b // nb, b % nb
                rows = _class_rows(ib, r, d)
                prow = _class_rows(jnp.maximum(ib - 1, 0), r, d)
                bias = jnp.where(ib == 0, bias_first, bias_rest) if has_prev else bias_first
                for g in range(2):
                    qg = _slab_pair(q_ref, g, rows).astype(BF16)
                    kg = _slab_pair(k_ref, g, rows)
                    vg = _slab_pair(v_ref, g, rows)
                    if has_prev:
                        kg = jnp.concatenate([_slab_pair(k_ref, g, prow), kg], axis=0)
                        vg = jnp.concatenate([_slab_pair(v_ref, g, prow), vg], axis=0)
                    kg, vg = kg.astype(BF16), vg.astype(BF16)
                    s = _nt(_stack_heads(qg, lane_head), kg) * ATT_SCALE + bias
                    m = jnp.max(s, axis=-1, keepdims=True)
                    p = jnp.exp(s - m)
                    den = jnp.sum(p, axis=-1, keepdims=True)
                    og = _unstack_heads(_nn(p.astype(BF16), vg) / den, lane_head)
                    lg = _unstack_heads(jnp.broadcast_to(m + jnp.log(den), (4 * ATT_BLK, 256)), lane_head)
                    for jj in range(2):
                        j = 2 * g + jj
                        o_new, l_new = og[:, 128 * jj:128 * jj + 128], lg[:, 128 * jj:128 * jj + 128]
                        if pi > 0:
                            o_old, l_old = o_ref[j, rows, :], l_ref[j, rows, :]
                            mx = jnp.maximum(l_old, l_new)
                            ea, eb = jnp.exp(l_old - mx), jnp.exp(l_new - mx)
                            den = ea + eb
                            o_new = (ea * o_old + eb * o_new) / den
                            l_new = mx + jnp.log(den)
                        o_ref[j, rows, :] = o_new
                        l_ref[j, rows, :] = l_new
                return carry

            lax.fori_loop(0, S // ATT_BLK, block, 0)

        def to_cat(i, carry):
            rows = _rows(i, 256)
            for j in range(4):
                cat_ref[rows, 128 * j:128 * j + 128] = o_ref[j, rows, :].astype(BF16)
            return carry

        lax.fori_loop(0, S // 256, to_cat, 0)
        xc.finish()

    slab = jax.ShapeDtypeStruct((4, S, 128), F32)
    return _carry("att_fwd", body, exchange, exchange_args, (aq, ak, av), [VMEM] * 3, [VMEM] * 3,
                  [slab, slab, jax.ShapeDtypeStruct((S, 512), BF16)])


def _mix_fwd(cat_r, cat_a, wout, x, g2, g3, exchange, exchange_args):
    tm = 512

    def body(cr_ref, ca_ref, w_ref, x_ref, g2_ref, g3_ref, mix_ref, x2_ref, h3_ref, xc):
        @pl.when(pl.program_id(0) == 0)
        def _():
            xc.start()

        mix = _nn(cr_ref[...], w_ref[0:512, :]) + _nn(ca_ref[...], w_ref[512:1024, :])
        mix_ref[...] = mix
        x2 = x_ref[...] + mix * _rstd(mix) * g2_ref[...]
        x2_ref[...] = x2
        h3_ref[...] = (x2 * _rstd(x2) * g3_ref[...]).astype(BF16)

        @pl.when(pl.program_id(0) == S // tm - 1)
        def _():
            xc.middle()
            xc.finish()

    row = lambda w: pl.BlockSpec((tm, w), lambda i: (i, 0))
    vec = pl.BlockSpec((1, D), lambda i: (0, 0))
    return _carry("mix_fwd", body, exchange, exchange_args, (cat_r, cat_a, wout, x, g2, g3),
                  [row(512), row(512), pl.BlockSpec((D, D), lambda i: (0, 0)), row(D), vec, vec],
                  [row(D), row(D), row(D)],
                  [jax.ShapeDtypeStruct((S, D), F32), jax.ShapeDtypeStruct((S, D), F32),
                   jax.ShapeDtypeStruct((S, D), BF16)],
                  grid=(S // tm,), semantics=("arbitrary",))


def _ffn_fwd(h3, wg, wu, wd, x2, tgt, g4):
    tm = 512
    last = N_CHIP - 1

    def body(h_ref, wg_ref, wu_ref, wd_ref, x2_ref, t_ref, g_ref,
             gt_ref, up_ref, a_ref, loss_ref, dy_ref, df_ref, dg_ref, f_ref):
        k, i = pl.program_id(0), pl.program_id(1)
        h = h_ref[...]
        gt = _nt(h, wg_ref[...])
        up = _nt(h, wu_ref[...])
        gt_ref[...] = gt.astype(BF16)
        up_ref[...] = up.astype(BF16)
        a = (gt * _sigmoid(gt) * up).astype(BF16)
        a_ref[...] = a
        part = _nn(a, wd_ref[...])
        rows = _rows(i, tm)

        @pl.when(k == 0)
        def _():
            f_ref[rows, :] = part

        @pl.when((k > 0) & (k < last))
        def _():
            f_ref[rows, :] = f_ref[rows, :] + part

        @pl.when((k == last) & (i == 0))
        def _():
            loss_ref[...] = jnp.zeros_like(loss_ref)
            dg_ref[...] = jnp.zeros_like(dg_ref)

        @pl.when(k == last)
        def _():
            fv = f_ref[rows, :] + part
            r = _rstd(fv)
            fn = fv * r
            e = x2_ref[...] + fn * g_ref[...] - t_ref[...]
            loss_ref[...] = loss_ref[...] + jnp.sum(jnp.sum(e * e, axis=-1, keepdims=True), axis=0, keepdims=True)
            dy = e * (1.0 / D)
            dy_ref[...] = dy
            dg_ref[...] = dg_ref[...] + jnp.sum(dy * fn, axis=0, keepdims=True)
            t = dy * g_ref[...]
            df_ref[...] = (r * (t - fn * jnp.mean(t * fn, axis=-1, keepdims=True))).astype(BF16)

    wrow = pl.BlockSpec((None, FF_C, D), lambda k, i: (k, 0, 0))
    act = pl.BlockSpec((None, tm, FF_C), lambda k, i: (k, i, 0))
    late = pl.BlockSpec((tm, D), lambda k, i: (jnp.where(k == last, i, 0), 0))
    vec = pl.BlockSpec((1, D), lambda k, i: (0, 0))
    return pl.pallas_call(
        body, grid=(N_CHIP, S // tm), name="ffn_fwd",
        in_specs=[pl.BlockSpec((tm, D), lambda k, i: (i, 0)), wrow, wrow, wrow, late, late, vec],
        out_specs=[act, act, act, vec, late, late, vec],
        out_shape=[jax.ShapeDtypeStruct((N_CHIP, S, FF_C), BF16)] * 3
                  + [jax.ShapeDtypeStruct((1, D), F32), jax.ShapeDtypeStruct((S, D), F32),
                     jax.ShapeDtypeStruct((S, D), BF16), jax.ShapeDtypeStruct((1, D), F32)],
        scratch_shapes=[pltpu.VMEM((S, D), F32)],
        compiler_params=_params("arbitrary", "arbitrary"),
    )(h3, wg, wu, wd, x2, tgt, g4)


def _ffn_bwd_act(df, gt, up, wg, wu, wd):
    tm, sub = 512, 256

    def body(df_ref, gt_ref, up_ref, wg_ref, wu_ref, wd_ref, dgt_ref, dup_ref, dh_ref):
        k, i = pl.program_id(0), pl.program_id(1)
        parts = []
        for s in range(tm // sub):
            rows = slice(s * sub, (s + 1) * sub)
            da = _nt(df_ref[rows, :], wd_ref[...])
            gt, up = gt_ref[rows, :].astype(F32), up_ref[rows, :].astype(F32)
            sg = _sigmoid(gt)
            dup = (da * gt * sg).astype(BF16)
            dgt = (da * up * (sg * (1.0 + gt * (1.0 - sg)))).astype(BF16)
            dup_ref[rows, :] = dup
            dgt_ref[rows, :] = dgt
            parts.append(_nn(dgt, wg_ref[...]) + _nn(dup, wu_ref[...]))
        part = jnp.concatenate(parts, axis=0)
        rows = _rows(i, tm)

        @pl.when(k == 0)
        def _():
            dh_ref[rows, :] = part

        @pl.when(k > 0)
        def _():
            dh_ref[rows, :] = dh_ref[rows, :] + part

    wrow = pl.BlockSpec((None, FF_C, D), lambda k, i: (k, 0, 0))
    act = pl.BlockSpec((None, tm, FF_C), lambda k, i: (k, i, 0))
    row = pl.BlockSpec((tm, D), lambda k, i: (i, 0))
    return pl.pallas_call(
        body, grid=(N_CHIP, S // tm), name="ffn_bwd_act",
        in_specs=[row, act, act, wrow, wrow, wrow],
        out_specs=[act, act, pl.BlockSpec((S, D), lambda k, i: (0, 0))],
        out_shape=[jax.ShapeDtypeStruct((N_CHIP, S, FF_C), BF16), jax.ShapeDtypeStruct((N_CHIP, S, FF_C), BF16),
                   jax.ShapeDtypeStruct((S, D), F32)],
        compiler_params=_params("arbitrary", "arbitrary"),
    )(df, gt, up, wg, wu, wd)


def _ffn_bwd_w(a, df, h3, dgt, dup):
    tm = 1024
    assert S // tm == 2

    def body(a_ref, df_ref, h_ref, dgt_ref, dup_ref, dwd_ref, dwg_ref, dwu_ref, acc_d, acc_g, acc_u):
        i = pl.program_id(1)
        h = h_ref[...]
        parts = (_tn(a_ref[...], df_ref[...]), _tn(dgt_ref[...], h), _tn(dup_ref[...], h))

        @pl.when(i == 0)
        def _():
            for acc, part in zip((acc_d, acc_g, acc_u), parts):
                acc[...] = part

        @pl.when(i == S // tm - 1)
        def _():
            for out, acc, part in zip((dwd_ref, dwg_ref, dwu_ref), (acc_d, acc_g, acc_u), parts):
                out[...] = (acc[...] + part).astype(BF16)

    act = pl.BlockSpec((None, tm, FF_C), lambda k, i: (k, i, 0))
    row = pl.BlockSpec((tm, D), lambda k, i: (i, 0))
    wrow = pl.BlockSpec((None, FF_C, D), lambda k, i: (k, 0, 0))
    return pl.pallas_call(
        body, grid=(N_CHIP, S // tm), name="ffn_bwd_w",
        in_specs=[act, row, row, act, act],
        out_specs=[wrow, wrow, wrow],
        out_shape=[jax.ShapeDtypeStruct((N_CHIP, FF_C, D), BF16)] * 3,
        scratch_shapes=[pltpu.VMEM((FF_C, D), F32)] * 3,
        compiler_params=_params("parallel", "arbitrary"),
    )(a, df, h3, dgt, dup)


def _norm_bwd(dh3, dy, x2, mix, g2, g3, exchange, exchange_args):
    tm = 256

    def body(dh_ref, dy_ref, x2_ref, mix_ref, g2_ref, g3_ref, dx2_ref, dmix_ref, dg3_ref, dg2_ref, xc):
        @pl.when(pl.program_id(0) == 0)
        def _():
            xc.start()
            dg3_ref[...] = jnp.zeros_like(dg3_ref)
            dg2_ref[...] = jnp.zeros_like(dg2_ref)

        x2 = x2_ref[...]
        r3 = _rstd(x2)
        xn = x2 * r3
        dh = dh_ref[...]
        dg3_ref[...] = dg3_ref[...] + jnp.sum(dh * xn, axis=0, keepdims=True)
        t = dh * g3_ref[...]
        dx2 = dy_ref[...] + r3 * (t - xn * jnp.mean(t * xn, axis=-1, keepdims=True))
        dx2_ref[...] = dx2
        mix = mix_ref[...]
        r2 = _rstd(mix)
        mn = mix * r2
        dg2_ref[...] = dg2_ref[...] + jnp.sum(dx2 * mn, axis=0, keepdims=True)
        u = dx2 * g2_ref[...]
        dmix_ref[...] = (r2 * (u - mn * jnp.mean(u * mn, axis=-1, keepdims=True))).astype(BF16)

        @pl.when(pl.program_id(0) == S // tm - 1)
        def _():
            xc.middle()
            xc.finish()

    row = pl.BlockSpec((tm, D), lambda i: (i, 0))
    vec = pl.BlockSpec((1, D), lambda i: (0, 0))
    return _carry("norm_bwd", body, exchange, exchange_args, (dh3, dy, x2, mix, g2, g3),
                  [row, row, row, row, vec, vec], [row, row, vec, vec],
                  [jax.ShapeDtypeStruct((S, D), F32), jax.ShapeDtypeStruct((S, D), BF16),
                   jax.ShapeDtypeStruct((1, D), F32), jax.ShapeDtypeStruct((1, D), F32)],
                  grid=(S // tm,), semantics=("arbitrary",))


def _mix_bwd(dmix, cat_r, cat_a, wout, after):
    tm = 512

    def body(dm_ref, cr_ref, ca_ref, w_ref, dret_ref, datt_ref, dw_ref, acc, _):
        i = pl.program_id(0)

        @pl.when(i == 0)
        def _():
            acc[...] = jnp.zeros_like(acc)

        dm = dm_ref[...]
        dret_ref[...] = _nt(dm, w_ref[0:512, :])
        datt = _nt(dm, w_ref[512:1024, :])
        for j in range(4):
            datt_ref[j] = datt[:, 128 * j:128 * j + 128]
        acc[0:512, :] += _tn(cr_ref[...], dm)
        acc[512:1024, :] += _tn(ca_ref[...], dm)

        @pl.when(i == S // tm - 1)
        def _():
            dw_ref[...] = acc[...].astype(BF16)

    row = lambda w: pl.BlockSpec((tm, w), lambda i: (i, 0))
    full = pl.BlockSpec((D, D), lambda i: (0, 0))
    return _carry("mix_bwd", body, _NoExchange(), (), (dmix, cat_r, cat_a, wout),
                  [row(D), row(512), row(512), full],
                  [row(512), pl.BlockSpec((4, tm, 128), lambda i: (0, i, 0)), full],
                  [jax.ShapeDtypeStruct((S, 512), F32), jax.ShapeDtypeStruct((4, S, 128), F32),
                   jax.ShapeDtypeStruct((D, D), BF16)],
                  scratch_shapes=[pltpu.VMEM((D, D), F32)], grid=(S // tm,), semantics=("arbitrary",), after=after)[0]


def _att_bwd(aq, ak, av, datt, att_out, lse, exchange, exchange_args):
    def body(q_ref, k_ref, v_ref, do_ref, out_ref, l_ref, dq_ref, dk_ref, dv_ref, xc):
        xc.start()

        def clear(i, carry):
            rows = _rows(i, 256)
            for ref in (dq_ref, dk_ref, dv_ref):
                for j in range(4):
                    ref[j, rows, :] = jnp.zeros((256, 128), F32)
            return carry

        lax.fori_loop(0, S // 256, clear, 0)
        lane_head = lax.broadcasted_iota(jnp.int32, (ATT_BLK, 256), 1) // 64
        for d in PATTERN_DILATIONS:
            nb, has_prev = _att_blocks(d)
            bias_rest, bias_first = _att_bias(has_prev)

            def block(b, carry, d=d, nb=nb, has_prev=has_prev, bias_rest=bias_rest, bias_first=bias_first):
                r, ib = b // nb, b % nb
                rows = _class_rows(ib, r, d)
                prow = _class_rows(jnp.maximum(ib - 1, 0), r, d)
                bias = jnp.where(ib == 0, bias_first, bias_rest) if has_prev else bias_first
                for g in range(2):
                    qg = _slab_pair(q_ref, g, rows).astype(BF16)
                    kg = _slab_pair(k_ref, g, rows)
                    vg = _slab_pair(v_ref, g, rows)
                    if has_prev:
                        kg = jnp.concatenate([_slab_pair(k_ref, g, prow), kg], axis=0)
                        vg = jnp.concatenate([_slab_pair(v_ref, g, prow), vg], axis=0)
                    kg, vg = kg.astype(BF16), vg.astype(BF16)
                    dog = _slab_pair(do_ref, g, rows)
                    outg = _slab_pair(out_ref, g, rows)
                    lg = _slab_pair(l_ref, g, rows)
                    qs = _stack_heads(qg, lane_head)
                    dos = _stack_heads(dog, lane_head)
                    delta = jnp.sum(dos * jnp.concatenate([outg] * 4, axis=0), axis=-1, keepdims=True)
                    lh = jnp.max(_stack_heads(lg, lane_head, NEG), axis=-1, keepdims=True)
                    s = _nt(qs, kg) * ATT_SCALE + bias
                    p = jnp.exp(s - lh)
                    dosb = dos.astype(BF16)
                    ds = (p * (_nt(dosb, vg) - delta) * ATT_SCALE).astype(BF16)
                    dq = _unstack_heads(_nn(ds, kg), lane_head)
                    dk = _tn(ds, qs)
                    dv = _tn(p.astype(BF16), dosb)
                    for jj in range(2):
                        j, sl = 2 * g + jj, slice(128 * jj, 128 * jj + 128)
                        dq_ref[j, rows, :] += dq[:, sl]
                        if has_prev:
                            dk_ref[j, prow, :] += dk[0:ATT_BLK, sl]
                            dv_ref[j, prow, :] += dv[0:ATT_BLK, sl]
                            dk_ref[j, rows, :] += dk[ATT_BLK:2 * ATT_BLK, sl]
                            dv_ref[j, rows, :] += dv[ATT_BLK:2 * ATT_BLK, sl]
                        else:
                            dk_ref[j, rows, :] += dk[:, sl]
                            dv_ref[j, rows, :] += dv[:, sl]
                return carry

            lax.fori_loop(0, S // ATT_BLK, block, 0)
        xc.middle()
        xc.finish()

    slab = jax.ShapeDtypeStruct((4, S, 128), F32)
    return _carry("att_bwd", body, exchange, exchange_args, (aq, ak, av, datt, att_out, lse), [VMEM] * 6, [VMEM] * 3,
                  [slab, slab, slab])


def _ret_bwd(qr, kr, rv, proj, o_raw, states, dret, tabs, exchange, exchange_args):
    C = RET_C
    nc = S // C
    dtab, a_tab, b_tab, lam, bd = tabs

    def body(q_ref, k_ref, v_ref, g_ref, o_ref, st_ref, dr_ref, dt_ref, a_ref, b_ref, lam_ref, bd_ref,
             dq_ref, dk_ref, dv_ref, dg_ref, dR, exch):
        @pl.when(pl.program_id(0) == 0)
        def _():
            exch.start()
            dR[...] = jnp.zeros_like(dR)

        q, k, v = q_ref[...], k_ref[...], v_ref[...]
        lane_head = lax.broadcasted_iota(jnp.int32, (C, 256), 1) // 32
        col_head = lax.broadcasted_iota(jnp.int32, (C, 256), 1) // 64
        dos = []
        for j in range(4):
            sl = slice(128 * j, 128 * j + 128)
            oj = o_ref[:, sl]
            xc = oj - _seg_mean(oj)
            rs = lax.rsqrt(_seg_mean(xc * xc) + GN_EPS)
            rn = xc * rs
            gj = g_ref[:, sl]
            sg = _sigmoid(gj)
            dret = dr_ref[:, sl]
            dg_ref[:, sl] = dret * rn * (sg * (1.0 + gj * (1.0 - sg)))
            drn = dret * (gj * sg)
            dos.append(rs * (drn - _seg_mean(drn) - rn * _seg_mean(drn * rn)))
        do = [jnp.concatenate(dos[0:2], axis=1), jnp.concatenate(dos[2:4], axis=1)]
        do8 = jnp.concatenate(do, axis=1).astype(BF16)
        drb = dR[...].astype(BF16)
        rb = st_ref[...]
        dq = _nt(do8, rb) * a_ref[...]
        dk = _nt(v, drb) * b_ref[...]
        kb = (k.astype(F32) * b_ref[...]).astype(BF16)
        dvall = _nn(kb, drb)
        qs = _stack_heads(q, lane_head, n=8)
        dec = dt_ref[...]
        p = (_nt(qs, k) * dec).astype(BF16)
        dos = [_stack_heads(do[g], col_head).astype(BF16) for g in range(2)]
        dp = jnp.concatenate([_nt(dos[g], v[:, 256 * g:256 * g + 256]) for g in range(2)], axis=0)
        ds = (dp * dec).astype(BF16)
        dq = dq + _unstack_heads(_nn(ds, k), lane_head, n=8)
        dk = dk + _tn(ds, qs)
        dv = [dvall[:, 256 * g:256 * g + 256] + _tn(p[4 * C * g:4 * C * (g + 1)], dos[g]) for g in range(2)]
        qa = (q.astype(F32) * a_ref[...]).astype(BF16)
        dR[...] = dR[...] * lam_ref[...] + _tn(qa, do8) * bd_ref[...]
        dq_ref[...] = dq
        dk_ref[...] = dk
        dv_ref[:, 0:256] = dv[0]
        dv_ref[:, 256:512] = dv[1]

        @pl.when(pl.program_id(0) == nc - 1)
        def _():
            exch.middle()
            exch.finish()

    rev = lambda w: pl.BlockSpec((C, w), lambda n: (nc - 1 - n, 0))
    full = lambda a: pl.BlockSpec(a.shape, lambda n: (0,) * a.ndim)
    return _carry(
        "ret_bwd", body, exchange, exchange_args, (qr, kr, rv, proj, o_raw, states, dret, dtab, a_tab, b_tab, lam, bd),
        [rev(256), rev(256), rev(512), rev(512), rev(512),
         pl.BlockSpec((None, 256, 512), lambda n: (nc - 1 - n, 0, 0)), rev(512),
         full(dtab), full(a_tab), full(b_tab), full(lam), full(bd)],
        [rev(256), rev(256), rev(512), rev(512)],
        [jax.ShapeDtypeStruct((S, 256), F32), jax.ShapeDtypeStruct((S, 256), F32),
         jax.ShapeDtypeStruct((S, 512), F32), jax.ShapeDtypeStruct((S, 512), F32)],
        scratch_shapes=[pltpu.VMEM((256, 512), F32)], grid=(nc,), semantics=("arbitrary",))


def _rot_bwd(cos, sin, dqr, dkr, drv, drg, dq_att, dk_att, dv_att):
    tm = 256

    def body(cos_ref, sin_ref, dqr_ref, dkr_ref, drv_ref, drg_ref, dqa_ref, dka_ref, dva_ref, dp_ref):
        cr, ca, sr, sa = cos_ref[:, 0:256], cos_ref[:, 256:768], sin_ref[:, 0:256], sin_ref[:, 256:768]
        lo_r, lo_a = _rot_halves(tm)

        def unrot_r(g):
            gs = g * sr
            return g * cr + pltpu.roll(jnp.where(lo_r, -gs, 0.0), 16, 1) + pltpu.roll(jnp.where(lo_r, 0.0, gs), 240, 1)

        def unrot_a(g):
            gs = g * sa
            return g * ca + pltpu.roll(jnp.where(lo_a, -gs, 0.0), 8, 1) + pltpu.roll(jnp.where(lo_a, 0.0, gs), 504, 1)

        def wide(ref):
            return jnp.concatenate([ref[j] for j in range(4)], axis=1)

        dp_ref[:, 0:256] = unrot_r(dqr_ref[...]).astype(BF16)
        dp_ref[:, 256:512] = unrot_r(dkr_ref[...] * RET_SCALE).astype(BF16)
        dp_ref[:, 512:1024] = drv_ref[...].astype(BF16)
        dp_ref[:, 1024:1536] = drg_ref[...].astype(BF16)
        dp_ref[:, 1536:2048] = unrot_a(wide(dqa_ref)).astype(BF16)
        dp_ref[:, 2048:2560] = unrot_a(wide(dka_ref)).astype(BF16)
        dp_ref[:, 2560:3072] = wide(dva_ref).astype(BF16)

    row = lambda w: pl.BlockSpec((tm, w), lambda i: (i, 0))
    slab = pl.BlockSpec((4, tm, 128), lambda i: (0, i, 0))
    return pl.pallas_call(
        body, grid=(S // tm,), name="rot_bwd",
        in_specs=[row(768), row(768), row(256), row(256), row(512), row(512), slab, slab, slab],
        out_specs=row(PW), out_shape=jax.ShapeDtypeStruct((S, PW), BF16),
        compiler_params=_params("parallel"),
    )(cos, sin, dqr, dkr, drv, drg, dq_att, dk_att, dv_att)


def _win_bwd_w(h1, dproj, exchange, exchange_args):
    tm = 512

    def body(h_ref, dp_ref, dw_ref, acc, xc):
        k, i = pl.program_id(0), pl.program_id(1)

        @pl.when((k == 0) & (i == 0))
        def _():
            xc.start()

        @pl.when(i == 0)
        def _():
            acc[...] = jnp.zeros_like(acc)

        acc[...] += _tn(h_ref[...], dp_ref[...])

        @pl.when(i == S // tm - 1)
        def _():
            dw_ref[...] = acc[...].astype(BF16)

        @pl.when((k == N_CHIP - 1) & (i == S // tm - 1))
        def _():
            xc.middle()
            xc.finish()

    (dw,), out = _carry(
        "win_bwd_w", body, exchange, exchange_args, (h1, dproj),
        [pl.BlockSpec((tm, D), lambda k, i: (i, 0)), pl.BlockSpec((tm, WIN_C), lambda k, i: (i, k))],
        [pl.BlockSpec((None, D, WIN_C), lambda k, i: (k, 0, 0))],
        [jax.ShapeDtypeStruct((N_CHIP, D, WIN_C), BF16)],
        scratch_shapes=[pltpu.VMEM((D, WIN_C), F32)], grid=(N_CHIP, S // tm), semantics=("arbitrary", "arbitrary"))
    return dw, out


def _in_bwd(dproj, win_g, x, dx2, g1, after):
    tm = 512

    def body(dp_ref, w_ref, x_ref, dx2_ref, g_ref, dx_ref, dg_ref, _):
        @pl.when(pl.program_id(0) == 0)
        def _():
            dg_ref[...] = jnp.zeros_like(dg_ref)

        dh = _nt(dp_ref[:, 0:WIN_C], w_ref[0])
        for k in range(1, N_CHIP):
            dh = dh + _nt(dp_ref[:, k * WIN_C:(k + 1) * WIN_C], w_ref[k])
        xv = x_ref[...]
        r = _rstd(xv)
        xn = xv * r
        dg_ref[...] = dg_ref[...] + jnp.sum(dh * xn, axis=0, keepdims=True)
        t = dh * g_ref[...]
        dx_ref[...] = dx2_ref[...] + r * (t - xn * jnp.mean(t * xn, axis=-1, keepdims=True))

    row = lambda w: pl.BlockSpec((tm, w), lambda i: (i, 0))
    vec = pl.BlockSpec((1, D), lambda i: (0, 0))
    return _carry("in_bwd", body, _NoExchange(), (), (dproj, win_g, x, dx2, g1),
                  [row(PW), pl.BlockSpec((N_CHIP, D, WIN_C), lambda i: (0, 0, 0)), row(D), row(D), vec],
                  [row(D), vec], [jax.ShapeDtypeStruct((S, D), F32), jax.ShapeDtypeStruct((1, D), F32)],
                  grid=(S // tm,), semantics=("arbitrary",), after=after)[0]


ANY = pl.BlockSpec(memory_space=pl.ANY)
VMEM = pl.BlockSpec(memory_space=pltpu.VMEM)
FLIPS = ((1, 0), (0, 1), (1, 1))


def _place():
    x, y, c = lax.axis_index("x"), lax.axis_index("y"), lax.axis_index("c")
    chips = [((1 - x) if fx else x, (1 - y) if fy else y) for fx, fy in FLIPS]
    return x, y, c, 2 * x + y, chips


def _remote(src, dst, send_sem, recv_sem, device):
    return pltpu.make_async_remote_copy(src_ref=src, dst_ref=dst, send_sem=send_sem, recv_sem=recv_sem,
                                        device_id=device, device_id_type=MESH)


class _Exchange:
    aliases = {}

    def middle(self, ins, outs, sems):
        pass


class _GatherShards(_Exchange):
    def __init__(self, shards):
        n = self.n = len(shards)
        self.n_in = self.n_out = n
        self.out_shape = [jax.ShapeDtypeStruct((N_CHIP,) + s.shape, s.dtype) for s in shards]
        dma = pltpu.SemaphoreType.DMA
        self.scratch = [dma((3 * n,)), dma((3 * n,)), dma((3 * n,)), dma((3 * n,)), dma((n,)), dma((n,))]

    def _ici(self, ins, outs, sems, a, j, chip):
        x, y, c, me, chips = _place()
        half = ins[a].shape[0] // 2
        return _remote(ins[a].at[pl.ds(c * half, half), :], outs[a].at[me, pl.ds(c * half, half), :],
                       sems[0].at[3 * a + j], sems[1].at[3 * a + j], (*chip, c))

    def _fwd(self, outs, sems, a, j, chip, half_of):
        x, y, c, me, chips = _place()
        half = outs[a].shape[1] // 2
        blk = outs[a].at[2 * chip[0] + chip[1], pl.ds(half_of * half, half), :]
        return _remote(blk, blk, sems[2].at[3 * a + j], sems[3].at[3 * a + j], (x, y, 1 - c))

    def _own(self, ins, outs, sems, a):
        return _own_shard_to_sibling(ins[a], outs[a], sems[4].at[a], sems[5].at[a])

    def start(self, ins, outs, sems):
        chips = _place()[4]
        for a in range(self.n):
            for j, chip in enumerate(chips):
                self._ici(ins, outs, sems, a, j, chip).start()
        for a in range(self.n):
            self._own(ins, outs, sems, a).start()

    def middle(self, ins, outs, sems):
        x, y, c, me, chips = _place()
        for a in range(self.n):
            for j, chip in enumerate(chips):
                half = outs[a].shape[1] // 2
                blk = outs[a].at[2 * chip[0] + chip[1], pl.ds(c * half, half), :]
                _remote(blk, blk, sems[0].at[3 * a + j], sems[1].at[3 * a + j], (x, y, c)).wait_recv()
                self._fwd(outs, sems, a, j, chip, c).start()

    def finish(self, ins, outs, sems):
        x, y, c, me, chips = _place()
        for a in range(self.n):
            for j, chip in enumerate(chips):
                self._fwd(outs, sems, a, j, chip, 1 - c).wait_recv()
        for a in range(self.n):
            for j, chip in enumerate(chips):
                self._ici(ins, outs, sems, a, j, chip).wait_send()
                self._fwd(outs, sems, a, j, chip, c).wait_send()
            self._own(ins, outs, sems, a).wait()


def _own_shard_to_sibling(shard_ref, gathered_ref, send_sem, recv_sem):
    x, y, c, me, chips = _place()
    return _remote(shard_ref, gathered_ref.at[me], send_sem, recv_sem, (x, y, 1 - c))


class _NoExchange(_Exchange):
    n_in = n_out = 0
    out_shape = ()
    scratch = ()

    def start(self, ins, outs, sems):
        pass

    def finish(self, ins, outs, sems):
        pass


class _ForwardGathered(_Exchange):
    def __init__(self, shards, own=True, forward=True):
        self.own, self.forward = own, forward
        n = self.n = len(shards)
        self.n_in, self.n_out = 2 * n, n
        self.out_shape = [jax.ShapeDtypeStruct((N_CHIP,) + s.shape, s.dtype) for s in shards]
        dma = pltpu.SemaphoreType.DMA
        self.scratch = [dma((3 * n,)), dma((3 * n,)), dma((n,)), dma((n,))]
        self.aliases = {n + a: a for a in range(n)}

    def _fwd(self, outs, sems, a, j, chip, half_of):
        x, y, c, me, chips = _place()
        half = outs[a].shape[1] // 2
        blk = outs[a].at[2 * chip[0] + chip[1], pl.ds(half_of * half, half), :]
        return _remote(blk, blk, sems[0].at[3 * a + j], sems[1].at[3 * a + j], (x, y, 1 - c))

    def _own(self, ins, outs, sems, a):
        return _own_shard_to_sibling(ins[a], outs[a], sems[2].at[a], sems[3].at[a])

    def start(self, ins, outs, sems):
        x, y, c, me, chips = _place()
        for a in range(self.n):
            for j, chip in enumerate(chips if self.forward else ()):
                self._fwd(outs, sems, a, j, chip, c).start()
        for a in range(self.n if self.own else 0):
            self._own(ins, outs, sems, a).start()

    def finish(self, ins, outs, sems):
        x, y, c, me, chips = _place()
        for a in range(self.n):
            for j, chip in enumerate(chips if self.forward else ()):
                self._fwd(outs, sems, a, j, chip, 1 - c).wait_recv()
        for a in range(self.n):
            for j, chip in enumerate(chips if self.forward else ()):
                self._fwd(outs, sems, a, j, chip, c).wait_send()
            if self.own:
                self._own(ins, outs, sems, a).wait()


HBM = pl.BlockSpec(memory_space=pltpu.HBM)
SEMS = pl.BlockSpec(memory_space=pltpu.SEMAPHORE)
DATAFLOW = pltpu.SideEffectType.DATAFLOW_SIDE_EFFECTING


class _OverIci:
    def __init__(self, name, sources, lands):
        self.name, self.n = name, len(sources)
        hbm = lambda t: pltpu.with_memory_space_constraint(t, pltpu.HBM)
        self.arrays = [hbm(t) for t in sources] + [hbm(t) for t in lands]

    def sent(self, src, land, a, chip):
        raise NotImplementedError

    def landed(self, land, a, chip):
        raise NotImplementedError

    def _copy(self, arr, sems, a, j, receiving):
        x, y, c, me, chips = _place()
        src, dst = self.sent(arr[a], arr[self.n + a], a, chips[j])
        if receiving:
            dst = self.landed(arr[self.n + a], a, chips[j])
        return _remote(src, dst, sems[0].at[3 * a + j], sems[1].at[3 * a + j], (*chips[j], c))

    def start(self, after):
        m = len(self.arrays)

        def body(*refs):
            arr, sems, token = refs[:m], refs[m + 1:m + 3], refs[-1]
            for a in range(self.n):
                for j in range(3):
                    self._copy(arr, sems, a, j, False).start()
            token[...] = jnp.zeros_like(token)

        dma = pltpu.SemaphoreType.DMA
        outs = pl.pallas_call(
            body, name=self.name + "_start",
            out_shape=[dma((3 * self.n,)), dma((3 * self.n,))] + [pltpu.HBM(t.shape, t.dtype) for t in self.arrays]
                      + [jax.ShapeDtypeStruct((8, 128), F32)],
            in_specs=[HBM] * m + [ANY], out_specs=[SEMS, SEMS] + [HBM] * m + [VMEM],
            input_output_aliases={i: 2 + i for i in range(m)},
            compiler_params=pltpu.CompilerParams(has_side_effects=DATAFLOW),
        )(*self.arrays, after)
        self.sems, self.arrays = outs[0:2], list(outs[2:2 + m])
        return outs[-1]

    def wait(self, after):
        m = len(self.arrays)

        def body(*refs):
            arr, sems = refs[:m], refs[m:m + 2]
            for a in range(self.n):
                for j in range(3):
                    self._copy(arr, sems, a, j, False).wait_send()
                    self._copy(arr, sems, a, j, True).wait_recv()

        outs = pl.pallas_call(
            body, name=self.name + "_wait",
            out_shape=[pltpu.HBM(t.shape, t.dtype) for t in self.arrays],
            in_specs=[HBM] * m + [SEMS, SEMS, ANY], out_specs=[HBM] * m,
            input_output_aliases={i: i for i in range(m)},
            compiler_params=pltpu.CompilerParams(has_side_effects=DATAFLOW),
        )(*self.arrays, *self.sems, after)
        return list(outs[:self.n]), list(outs[self.n:])


class _GatherOverIci(_OverIci):
    def __init__(self, name, shards):
        super().__init__(name, shards, [lax.empty((N_CHIP,) + s.shape, s.dtype) for s in shards])

    @staticmethod
    def _half(ref):
        c = lax.axis_index("c")
        half = ref.shape[-2] // 2
        return pl.ds(c * half, half)

    def sent(self, src, land, a, chip):
        return src.at[self._half(src), :], land.at[_place()[3], self._half(src), :]

    def landed(self, land, a, chip):
        return land.at[2 * chip[0] + chip[1], self._half(land), :]


class _SumOverIci(_OverIci):
    def __init__(self, name, pre):
        super().__init__(name, pre, [lax.empty(p.shape, p.dtype) for p in pre])

    def sent(self, src, land, a, chip):
        return src.at[2 * chip[0] + chip[1]], land.at[_place()[3]]

    def landed(self, land, a, chip):
        return land.at[2 * chip[0] + chip[1]]


class _HalvesToSibling(_Exchange):
    def __init__(self, grads):
        n = self.n = len(grads)
        self.n_in = self.n_out = n
        self.out_shape = [jax.ShapeDtypeStruct((N_CHIP, g.shape[1] // 2, g.shape[2]), g.dtype) for g in grads]
        self.scratch = [pltpu.SemaphoreType.DMA((n,)), pltpu.SemaphoreType.DMA((n,))]

    def _copy(self, ins, outs, sems, a):
        x, y, c, me, chips = _place()
        half = ins[a].shape[1] // 2
        return _remote(ins[a].at[:, pl.ds((1 - c) * half, half), :], outs[a], sems[0].at[a], sems[1].at[a], (x, y, 1 - c))

    def start(self, ins, outs, sems):
        for a in range(self.n):
            self._copy(ins, outs, sems, a).start()

    def finish(self, ins, outs, sems):
        for a in range(self.n):
            self._copy(ins, outs, sems, a).wait_recv()
        for a in range(self.n):
            self._copy(ins, outs, sems, a).wait_send()


class _ShareHalves(_Exchange):
    def __init__(self, fulls):
        n = self.n = len(fulls)
        self.n_in = self.n_out = n
        self.out_shape = [jax.ShapeDtypeStruct(f.shape, f.dtype) for f in fulls]
        self.scratch = [pltpu.SemaphoreType.DMA((n,)), pltpu.SemaphoreType.DMA((n,))]
        self.aliases = {a: a for a in range(n)}

    def _copy(self, outs, sems, a, half_of):
        x, y, c, me, chips = _place()
        half = outs[a].shape[0] // 2
        rows = outs[a].at[pl.ds(half_of * half, half), :]
        return _remote(rows, rows, sems[0].at[a], sems[1].at[a], (x, y, 1 - c))

    def start(self, ins, outs, sems):
        c = _place()[2]
        for a in range(self.n):
            self._copy(outs, sems, a, c).start()

    def finish(self, ins, outs, sems):
        c = _place()[2]
        for a in range(self.n):
            self._copy(outs, sems, a, 1 - c).wait_recv()
        for a in range(self.n):
            self._copy(outs, sems, a, c).wait_send()


class _GatherBlocks(_Exchange):
    def __init__(self, block):
        self.n_in = self.n_out = 1
        self.out_shape = [jax.ShapeDtypeStruct((8,) + block.shape, block.dtype)]
        dma = pltpu.SemaphoreType.DMA
        self.scratch = [dma((7,)), dma((7,)), dma]

    @staticmethod
    def _peer(f):
        x, y, c, me, chips = _place()
        return ((1 - x) if f & 4 else x, (1 - y) if f & 2 else y, (1 - c) if f & 1 else c)

    def start(self, ins, outs, sems):
        x, y, c, me, chips = _place()
        for f in range(1, 8):
            _remote(ins[0], outs[0].at[2 * me + c], sems[0].at[f - 1], sems[1].at[f - 1], self._peer(f)).start()
        pltpu.make_async_copy(ins[0], outs[0].at[2 * me + c], sems[2]).start()

    def finish(self, ins, outs, sems):
        x, y, c, me, chips = _place()
        for f in range(1, 8):
            px, py, pc = self._peer(f)
            blk = outs[0].at[4 * px + 2 * py + pc]
            _remote(blk, blk, sems[0].at[f - 1], sems[1].at[f - 1], (x, y, c)).wait_recv()
        for f in range(1, 8):
            _remote(ins[0], outs[0].at[2 * me + c], sems[0].at[f - 1], sems[1].at[f - 1], self._peer(f)).wait_send()
        pltpu.make_async_copy(ins[0], outs[0].at[2 * me + c], sems[2]).wait()


class _Both(_Exchange):
    def __init__(self, first, second):
        self.parts = (first, second)
        self.n_in, self.n_out = first.n_in + second.n_in, first.n_out + second.n_out
        self.out_shape = first.out_shape + second.out_shape
        self.scratch = first.scratch + second.scratch
        self.aliases = dict(first.aliases)
        self.aliases.update({first.n_in + i: first.n_out + o for i, o in second.aliases.items()})

    def _split(self, ins, outs, sems):
        a, b = self.parts
        return ((a, ins[:a.n_in], outs[:a.n_out], sems[:len(a.scratch)]),
                (b, ins[a.n_in:], outs[a.n_out:], sems[len(a.scratch):]))

    def start(self, ins, outs, sems):
        for ex, i, o, s in self._split(ins, outs, sems):
            ex.start(i, o, s)

    def middle(self, ins, outs, sems):
        for ex, i, o, s in self._split(ins, outs, sems):
            ex.middle(i, o, s)

    def finish(self, ins, outs, sems):
        for ex, i, o, s in self._split(ins, outs, sems):
            ex.finish(i, o, s)


class _Bound:
    def __init__(self, ex, ins, outs, sems):
        self.start = lambda: ex.start(ins, outs, sems)
        self.middle = lambda: ex.middle(ins, outs, sems)
        self.finish = lambda: ex.finish(ins, outs, sems)


def _carry(name, body, ex, ex_args, args, in_specs, out_specs, out_shape, scratch_shapes=(), grid=None, semantics=(),
           after=None):
    n_a, n_o, n_s = len(args), len(out_shape), len(scratch_shapes)
    behind = [] if after is None else [after]

    def full_body(*refs):
        p = 0
        groups = []
        for size in (n_a, ex.n_in, len(behind), n_o, ex.n_out, n_s, len(ex.scratch)):
            groups.append(refs[p:p + size])
            p += size
        a, ei, _, o, eo, s, es = groups
        body(*a, *o, *s, _Bound(ex, ei, eo, es))

    kwargs = {} if grid is None else {"grid": grid}
    outs = pl.pallas_call(
        full_body, name=name,
        in_specs=list(in_specs) + [ANY] * (ex.n_in + len(behind)), out_specs=list(out_specs) + [ANY] * ex.n_out,
        out_shape=list(out_shape) + list(ex.out_shape), scratch_shapes=list(scratch_shapes) + list(ex.scratch),
        input_output_aliases={n_a + i: n_o + o for i, o in ex.aliases.items()},
        compiler_params=_params(*semantics) if semantics else pltpu.CompilerParams(vmem_limit_bytes=VMEM_LIMIT),
        **kwargs,
    )(*args, *ex_args, *behind)
    return outs[:n_o], outs[n_o:]


def _prepare_carrying(name, x, g1, pos, ifc, spread, arrays, ex, ex_args):
    n = len(arrays)
    r, cc = arrays[0].shape
    steps = 4
    tr, tm = r // steps, S // steps

    def body(x_ref, g_ref, pos_ref, ifc_ref, e_ref, *refs):
        src, h_ref, cos_ref, sin_ref, dst, xc = refs[:n], refs[n], refs[n + 1], refs[n + 2], refs[n + 3:2 * n + 3], refs[-1]

        @pl.when(pl.program_id(0) == 0)
        def _():
            xc.start()

        xv = x_ref[...]
        h_ref[...] = (xv * _rstd(xv) * g_ref[...]).astype(BF16)
        ang = pos_ref[...].astype(F32) * ifc_ref[...]
        cos_ref[...] = _spread_exact(jnp.cos(ang), e_ref[...])
        sin_ref[...] = _spread_exact(jnp.sin(ang), e_ref[...])
        for a in range(n):
            dst[a][...] = src[a][...].astype(BF16)

        @pl.when(pl.program_id(0) == steps - 1)
        def _():
            xc.middle()
            xc.finish()

    row = lambda w: pl.BlockSpec((tm, w), lambda i: (i, 0))
    const = lambda w: pl.BlockSpec((1, w), lambda i: (0, 0))
    blk = pl.BlockSpec((tr, cc), lambda i: (i, 0))
    return _carry(name, body, ex, ex_args, (x, g1, pos, ifc, spread, *arrays),
                  [row(D), const(D), row(1), const(128), pl.BlockSpec((128, 768), lambda i: (0, 0))] + [blk] * n,
                  [row(D), row(768), row(768)] + [blk] * n,
                  [jax.ShapeDtypeStruct((S, D), BF16)] + [jax.ShapeDtypeStruct((S, 768), F32)] * 2
                  + [jax.ShapeDtypeStruct((r, cc), BF16)] * n,
                  grid=(steps,), semantics=("arbitrary",))


def _exchange_alone(name, ex, ex_args):
    def body(xc):
        xc.start()
        xc.middle()
        xc.finish()

    return _carry(name, body, ex, ex_args, (), (), (), ())[1]


def _core_index():
    return lax.axis_index("c").astype(jnp.int32).reshape(1)


def _pair_sum(gs, gots):
    n = len(gs)
    _, r, cc = gs[0].shape
    half = r // 2

    def body(c_ref, *refs):
        for a in range(n):
            refs[2 * n + a][...] = (refs[a][...].astype(F32) + refs[n + a][...].astype(F32)).astype(BF16)

    mine = pl.BlockSpec((None, half, cc), lambda k, c_ref: (k, c_ref[0], 0))
    blk = pl.BlockSpec((None, half, cc), lambda k, c_ref: (k, 0, 0))
    return pl.pallas_call(
        body, name=f"pair_sum_{r}x{cc}",
        grid_spec=pltpu.PrefetchScalarGridSpec(
            num_scalar_prefetch=1, grid=(N_CHIP,), in_specs=[mine] * n + [blk] * n, out_specs=[blk] * n),
        out_shape=[jax.ShapeDtypeStruct((N_CHIP, half, cc), BF16)] * n,
        compiler_params=_params("parallel"),
    )(_core_index(), *gs, *gots)


def _chip_sum(pre, parts):
    n = len(parts)
    _, half, cc = parts[0].shape
    tr = half // 2
    me = 2 * lax.axis_index("x") + lax.axis_index("y")
    others = [k + (k >= me).astype(jnp.int32) for k in range(3)]
    where = jnp.stack([lax.axis_index("c"), me, *others]).astype(jnp.int32)

    def body(w_ref, *refs):
        for a in range(n):
            own, p1, p2, p3 = refs[4 * a:4 * a + 4]
            refs[4 * n + a][...] = ((own[...].astype(F32) + p1[...].astype(F32)) + p2[...].astype(F32)) + p3[...].astype(F32)

    slot = lambda s: pl.BlockSpec((None, tr, cc), lambda i, w_ref: (w_ref[s], i, 0))
    operands = []
    for a in range(n):
        operands += [pre[a], parts[a], parts[a], parts[a]]
    return pl.pallas_call(
        body, name=f"chip_sum_{half}x{cc}",
        grid_spec=pltpu.PrefetchScalarGridSpec(
            num_scalar_prefetch=1, grid=(2,),
            in_specs=[slot(1), slot(2), slot(3), slot(4)] * n,
            out_specs=[pl.BlockSpec((tr, cc), lambda i, w_ref: (2 * w_ref[0] + i, 0))] * n),
        out_shape=[jax.ShapeDtypeStruct((2 * half, cc), F32)] * n,
        compiler_params=_params("parallel"),
    )(where, *operands)


def _adamw_math(w, g, m, v):
    m = ADAM_B1 * m + (1.0 - ADAM_B1) * g
    v = ADAM_B2 * v + (1.0 - ADAM_B2) * (g * g)
    m_hat = m / (1.0 - ADAM_B1 ** ADAM_STEP)
    v_hat = v / (1.0 - ADAM_B2 ** ADAM_STEP)
    delta = -ADAM_LR * (m_hat / (jnp.sqrt(v_hat) + ADAM_EPS) + ADAM_WD * w)
    return delta, m, v


def _adamw(w, g, m, v, after=None):
    r, cc = w.shape
    tr = r // 4

    def body(w_ref, g_ref, m_ref, v_ref, go_ref, d_ref, nm_ref, nv_ref, _):
        g = g_ref[...]
        go_ref[...] = g
        d_ref[...], nm_ref[...], nv_ref[...] = _adamw_math(w_ref[...], g, m_ref[...], v_ref[...])

    blk = pl.BlockSpec((tr, cc), lambda i: (i, 0))
    return _carry(f"adamw_{r}x{cc}", body, _NoExchange(), (), (w, g, m, v), [blk] * 4, [blk] * 4,
                  [jax.ShapeDtypeStruct((r, cc), F32)] * 4, grid=(4,), semantics=("parallel",), after=after)[0]


def _pack8(rows):
    def body(*refs):
        out_ref = refs[-1]
        out_ref[...] = jnp.zeros_like(out_ref)
        for i, r in enumerate(refs[:-1]):
            out_ref[i:i + 1, :] = r[...]

    return pl.pallas_call(body, name="pack8", out_shape=jax.ShapeDtypeStruct((8, D), F32))(*rows)


def _adamw_gains(gall, ws, ms, vs):
    def body(ga_ref, *refs):
        w, m, v = refs[0:4], refs[4:8], refs[8:12]
        outs, loss_ref, total = refs[12:28], refs[28], refs[29]
        g = ga_ref[0]
        for dev in range(1, 8):
            g = g + ga_ref[dev]
        total[...] = g
        for i in range(4):
            gi = total[i:i + 1, :]
            outs[i][...] = gi
            outs[4 + i][...], outs[8 + i][...], outs[12 + i][...] = _adamw_math(w[i][...], gi, m[i][...], v[i][...])
        loss_ref[...] = total[4:5, 0:128] * (0.5 / D)

    outs = pl.pallas_call(
        body, name="adamw_gains",
        out_shape=[jax.ShapeDtypeStruct((1, D), F32)] * 16 + [jax.ShapeDtypeStruct((1, 128), F32)],
        scratch_shapes=[pltpu.VMEM((8, D), F32)],
    )(gall, *ws, *ms, *vs)
    return outs[0:4], outs[4:8], outs[8:12], outs[12:16], outs[16]


def kernel(x, positions, w_in, w_out, g_pre_mix, g_post_mix, g_pre_ffn, g_post_ffn, w_gate, w_up, w_down, loss_target, m_w_in, m_w_out, m_g_pre_mix, m_g_post_mix, m_g_pre_ffn, m_g_post_ffn, m_w_gate, m_w_up, m_w_down, v_w_in, v_w_out, v_g_pre_mix, v_g_post_mix, v_g_pre_ffn, v_g_post_ffn, v_w_gate, v_w_up, v_w_down):
    tr = lambda t: jnp.swapaxes(t, 1, 2)[0]
    shards = [w_in[0], w_out[0], tr(w_gate), tr(w_up), w_down[0]]
    moms = [m_w_in[0], m_w_out[0], tr(m_w_gate), tr(m_w_up), m_w_down[0]]
    vels = [v_w_in[0], v_w_out[0], tr(v_w_gate), tr(v_w_up), v_w_down[0]]
    xs, pos, tgt = x[0], positions.reshape(S, 1), loss_target[0]
    g1, g2, g3, g4 = g_pre_mix, g_post_mix, g_pre_ffn, g_post_ffn
    tabs = tuple(jnp.asarray(t) for t in _retention_tables())
    ifc, spread = _rotary_tables()
    ifc, spread = jnp.asarray(ifc), jnp.asarray(spread, dtype=BF16)
    bf = [s.astype(BF16) for s in shards[:2]]

    (h1, cos, sin, *ffn_bf), (win_g, wout_g) = _prepare_carrying(
        "gather_in", xs, g1, pos, ifc, spread, shards[2:], _GatherShards(bf), bf)
    bf += list(ffn_bf)
    wout_g = wout_g.reshape(D, D)
    ffn_gather = _GatherOverIci("ffn_gather", bf[2:])
    token = ffn_gather.start(win_g)
    qr, kr, rv, rg, aq, ak, av = _proj_fwd(h1, win_g, cos, sin, token)
    (o_raw, cat_r, states), _ = _ret_fwd(qr, kr, rv, rg, tabs, _NoExchange(), ())
    n_ffn = len(bf[2:])
    (att_out, lse, cat_a), ffn_gather.arrays[n_ffn:] = _att_fwd(
        aq, ak, av, _ForwardGathered(bf[2:], forward=False), ffn_gather.arrays)
    ffn_sh, ffn_lands = ffn_gather.wait(cat_a)
    (mix, x2, h3), (wg_g, wu_g, wd_g) = _mix_fwd(cat_r, cat_a, wout_g, xs, g2, g3,
                                                _ForwardGathered(bf[2:], own=False), [*ffn_sh, *ffn_lands])
    gt, up, a, sq, dy, df, dg4 = _ffn_fwd(h3, wg_g, wu_g, wd_g, x2, tgt, g4)

    dgt, dup, dh3 = _ffn_bwd_act(df, gt, up, wg_g, wu_g, wd_g)
    ffn_grads = list(_ffn_bwd_w(a, df, h3, dgt, dup))
    (dx2, dmix, dg3, dg2), got = _norm_bwd(dh3, dy, x2, mix, g2, g3, _HalvesToSibling(ffn_grads), ffn_grads)
    ffn_sum = _SumOverIci("ffn_sum", _pair_sum(ffn_grads, got))
    token = ffn_sum.start(dmix)
    dret, datt, dwout = _mix_bwd(dmix, cat_r, cat_a, wout_g, token)
    (dq_att, dk_att, dv_att), _ = _att_bwd(aq, ak, av, datt, att_out, lse, _NoExchange(), ())
    (dqr, dkr, drv, drg), _ = _ret_bwd(qr, kr, rv, rg, o_raw, states, dret, tabs, _NoExchange(), ())
    dproj = _rot_bwd(cos, sin, dqr, dkr, drv, drg, dq_att, dk_att, dv_att)
    sums = _chip_sum(*ffn_sum.wait(dproj))
    dwin, ffn_full = _win_bwd_w(h1, dproj, _ShareHalves(sums), sums)
    in_grads = [dwin, dwout.reshape(N_CHIP, WOUT_R, D)]

    got = _exchange_alone("halves_to_sibling", _HalvesToSibling(in_grads), in_grads)
    in_sum = _SumOverIci("in_sum", [*_pair_sum(in_grads[:1], got[:1]), *_pair_sum(in_grads[1:], got[1:])])
    token = in_sum.start(dproj)
    dx, dg1 = _in_bwd(dproj, win_g, xs, dx2, g1, token)
    ffn_upd = [_adamw(shards[2 + i], ffn_full[o], moms[2 + i], vels[2 + i], token)
               for i, o in enumerate((1, 2, 0))]
    pre, parts = in_sum.wait(ffn_upd[2][0])
    sums = [*_chip_sum(pre[:1], parts[:1]), *_chip_sum(pre[1:], parts[1:])]
    gblock = _pack8([dg1, dg2, dg3, dg4, sq])
    *in_full, gall = _exchange_alone("share_rest", _Both(_ShareHalves(sums), _GatherBlocks(gblock)), [*sums, gblock])
    upd = [_adamw(w, g, m, v) for w, g, m, v in zip(shards[:2], in_full, moms[:2], vels[:2])] + ffn_upd
    gg, gd, gm, gv, loss_row = _adamw_gains(gall, [g1, g2, g3, g4],
                                            [m_g_pre_mix, m_g_post_mix, m_g_pre_ffn, m_g_post_ffn],
                                            [v_g_pre_mix, v_g_post_mix, v_g_pre_ffn, v_g_post_ffn])

    def order(mats, vecs):
        back = lambda t: jnp.swapaxes(t[None], 1, 2)
        return [mats[0][None], mats[1][None], *vecs, back(mats[2]), back(mats[3]), mats[4][None]]

    return (loss_row[0, 0], dx[None],
            *order([u[0] for u in upd], gg),
            *order([u[1] for u in upd], gd),
            *order([u[2] for u in upd], gm),
            *order([u[3] for u in upd], gv))
```

```python
import numpy as np
import jax
import jax.numpy as jnp
from jax import lax
from jax.experimental import pallas as pl
from jax.experimental.pallas import tpu as pltpu

F32, BF16 = jnp.float32, jnp.bfloat16
MESH = pl.DeviceIdType.MESH

S = 2048
D = 1024
PW = 3072
N_CHIP = 4
WIN_C = PW // N_CHIP
DFF = 2816
FF_C = DFF // N_CHIP
WOUT_R = D // N_CHIP
RMS_EPS = 1e-6
GN_EPS = 1e-5
RET_C = 128
RET_SCALE = 32 ** -0.5
ATT_BLK = 128
ATT_SCALE = 64 ** -0.5
PATTERN_DILATIONS = (1, 4, 16)
NEG = -1e30
VMEM_LIMIT = 56 * 1024 * 1024

ADAM_LR, ADAM_B1, ADAM_B2, ADAM_EPS, ADAM_WD, ADAM_STEP = 0.001, 0.9, 0.999, 1e-08, 0.01, 10


def _params(*sem):
    return pltpu.CompilerParams(dimension_semantics=sem, vmem_limit_bytes=VMEM_LIMIT)


def _nt(a, b):
    return lax.dot_general(a, b, (((1,), (1,)), ((), ())), preferred_element_type=F32)


def _tn(a, b):
    return lax.dot_general(a, b, (((0,), (0,)), ((), ())), preferred_element_type=F32)


def _nn(a, b):
    return jnp.dot(a, b, preferred_element_type=F32)


def _rstd(v):
    return lax.rsqrt(jnp.mean(v * v, axis=-1, keepdims=True) + RMS_EPS)


def _sigmoid(v):
    return 1.0 / (1.0 + jnp.exp(-v))


def _rows(i, t):
    return pl.ds(pl.multiple_of(i * t, t), t)


def _retention_tables():
    h = np.arange(8, dtype=np.float32)
    log_g = np.log1p(-np.exp2(-5.0 - h)).astype(np.float32)
    idx = np.arange(RET_C, dtype=np.float32)
    diff = idx[:, None] - idx[None, :]
    dtab = np.where(diff >= 0, np.exp(log_g[:, None, None] * np.maximum(diff, 0.0)), 0.0).astype(np.float32)
    dtab = dtab.reshape(8 * RET_C, RET_C)
    lane_head = np.arange(256) // 32
    a_tab = np.exp(log_g[lane_head][None, :] * (idx + 1.0)[:, None]).astype(np.float32)
    b_tab = np.exp(log_g[lane_head][None, :] * (RET_C - 1.0 - idx)[:, None]).astype(np.float32)
    lam = np.exp(log_g[lane_head] * RET_C).astype(np.float32)[:, None]
    bd = (lane_head[:, None] == (np.arange(512) // 64)[None, :]).astype(np.float32)
    return dtab, a_tab, b_tab, lam, bd


def _rotary_tables():
    inv_r = (1.0 / (np.float32(10000.0) ** np.linspace(0.0, 1.0, 16, dtype=np.float32))).astype(np.float32)
    inv_a = (np.float32(500000.0) ** (-np.arange(0, 16, 2, dtype=np.float32) / np.float32(16))).astype(np.float32)
    ifc = np.zeros((1, 128), np.float32)
    ifc[0, 0:16], ifc[0, 16:24] = inv_r, inv_a
    spread = np.zeros((128, 768), np.float32)
    for lane in range(256):
        spread[(lane % 32) % 16, lane] = 1.0
    for lane in range(512):
        d = lane % 64
        spread[16 + d % 8 if d < 16 else 24, 256 + lane] = 1.0
    return ifc, spread


def _rot_halves(tm):
    lo_r = (lax.broadcasted_iota(jnp.int32, (tm, 256), 1) % 32) < 16
    lo_a = (lax.broadcasted_iota(jnp.int32, (tm, 512), 1) % 64) < 8
    return lo_r, lo_a


def _spread_exact(t, e):
    hi = t.astype(BF16)
    r1 = t - hi.astype(F32)
    mid = r1.astype(BF16)
    lo = (r1 - mid.astype(F32)).astype(BF16)
    return _nn(hi, e) + _nn(mid, e) + _nn(lo, e)


def _proj_fwd(h1, win_g, cos, sin, after):
    tm = 256

    def body(h_ref, w_ref, cos_ref, sin_ref, qr_ref, kr_ref, rv_ref, rg_ref, aq_ref, ak_ref, av_ref, p_ref, _):
        h = h_ref[...]
        for k in range(N_CHIP):
            p_ref[:, k * WIN_C:(k + 1) * WIN_C] = _nn(h, w_ref[k])
        cr, ca, sr, sa = cos_ref[:, 0:256], cos_ref[:, 256:768], sin_ref[:, 0:256], sin_ref[:, 256:768]
        lo_r, lo_a = _rot_halves(tm)

        def rot_r(v):
            return v * cr + sr * jnp.where(lo_r, -pltpu.roll(v, 240, 1), pltpu.roll(v, 16, 1))

        def rot_a(v):
            return v * ca + sa * jnp.where(lo_a, -pltpu.roll(v, 504, 1), pltpu.roll(v, 8, 1))

        qr_ref[...] = rot_r(p_ref[:, 0:256]).astype(BF16)
        kr_ref[...] = (rot_r(p_ref[:, 256:512]) * RET_SCALE).astype(BF16)
        rv_ref[...] = p_ref[:, 512:1024].astype(BF16)
        rg_ref[...] = p_ref[:, 1024:1536]
        aq, ak = rot_a(p_ref[:, 1536:2048]), rot_a(p_ref[:, 2048:2560])
        for j in range(4):
            aq_ref[j] = aq[:, 128 * j:128 * j + 128]
            ak_ref[j] = ak[:, 128 * j:128 * j + 128]
            av_ref[j] = p_ref[:, 2560 + 128 * j:2560 + 128 * j + 128]

    row = lambda w: pl.BlockSpec((tm, w), lambda i: (i, 0))
    slab = pl.BlockSpec((4, tm, 128), lambda i: (0, i, 0))
    return _carry(
        "proj_fwd", body, _NoExchange(), (), (h1, win_g, cos, sin),
        [row(D), pl.BlockSpec((N_CHIP, D, WIN_C), lambda i: (0, 0, 0)), row(768), row(768)],
        [row(256), row(256), row(512), row(512), slab, slab, slab],
        [jax.ShapeDtypeStruct((S, w), BF16) for w in (256, 256, 512)]
        + [jax.ShapeDtypeStruct((S, 512), F32)] + [jax.ShapeDtypeStruct((4, S, 128), F32)] * 3,
        scratch_shapes=[pltpu.VMEM((tm, PW), F32)], grid=(S // tm,), semantics=("parallel",), after=after)[0]


def _seg_mean(v):
    lo = lax.broadcasted_iota(jnp.int32, v.shape, 1) < 64
    s_lo = jnp.sum(jnp.where(lo, v, 0.0), axis=-1, keepdims=True)
    s_hi = jnp.sum(jnp.where(lo, 0.0, v), axis=-1, keepdims=True)
    return jnp.where(lo, s_lo, s_hi) * (1.0 / 64.0)


def _ret_fwd(qr, kr, rv, proj, tabs, exchange, exchange_args):
    C = RET_C
    dtab, a_tab, b_tab, lam, bd = tabs

    def body(q_ref, k_ref, v_ref, g_ref, dt_ref, a_ref, b_ref, lam_ref, bd_ref, o_ref, cat_ref, st_ref, R, exch):
        @pl.when(pl.program_id(0) == 0)
        def _():
            exch.start()
            R[...] = jnp.zeros_like(R)

        @pl.when(pl.program_id(0) == S // C // 2)
        def _():
            exch.middle()

        q, k, v = q_ref[...], k_ref[...], v_ref[...]
        lane_head = lax.broadcasted_iota(jnp.int32, (C, 256), 1) // 32
        col_head = lax.broadcasted_iota(jnp.int32, (C, 256), 1) // 64
        rb = R[...].astype(BF16)
        st_ref[...] = rb
        qa = (q.astype(F32) * a_ref[...]).astype(BF16)
        cross = _nn(qa, rb)
        p = (_nt(_stack_heads(q, lane_head, n=8), k) * dt_ref[...]).astype(BF16)
        og = [cross[:, 256 * g:256 * g + 256]
              + _unstack_heads(_nn(p[4 * C * g:4 * C * (g + 1)], v[:, 256 * g:256 * g + 256]), col_head)
              for g in range(2)]
        kb = (k.astype(F32) * b_ref[...]).astype(BF16)
        R[...] = R[...] * lam_ref[...] + _tn(kb, v) * bd_ref[...]
        o_ref[:, 0:256] = og[0]
        o_ref[:, 256:512] = og[1]
        for j in range(4):
            oj = og[j // 2][:, 128 * (j % 2):128 * (j % 2) + 128]
            xc = oj - _seg_mean(oj)
            rn = xc * lax.rsqrt(_seg_mean(xc * xc) + GN_EPS)
            gj = g_ref[:, 128 * j:128 * j + 128]
            cat_ref[:, 128 * j:128 * j + 128] = (rn * (gj * _sigmoid(gj))).astype(BF16)

        @pl.when(pl.program_id(0) == S // C - 1)
        def _():
            exch.finish()

    row = lambda w: pl.BlockSpec((C, w), lambda n: (n, 0))
    full = lambda a: pl.BlockSpec(a.shape, lambda n: (0,) * a.ndim)
    return _carry(
        "ret_fwd", body, exchange, exchange_args, (qr, kr, rv, proj, dtab, a_tab, b_tab, lam, bd),
        [row(256), row(256), row(512), row(512),
         full(dtab), full(a_tab), full(b_tab), full(lam), full(bd)],
        [row(512), row(512), pl.BlockSpec((None, 256, 512), lambda n: (n, 0, 0))],
        [jax.ShapeDtypeStruct((S, 512), F32), jax.ShapeDtypeStruct((S, 512), BF16),
         jax.ShapeDtypeStruct((S // C, 256, 512), BF16)],
        scratch_shapes=[pltpu.VMEM((256, 512), F32)], grid=(S // C,), semantics=("arbitrary",))


def _stack_heads(v, lane_head, fill=0.0, n=4):
    return jnp.concatenate([jnp.where(lane_head == h, v, jnp.full_like(v, fill)) for h in range(n)], axis=0)


def _unstack_heads(v, lane_head, n=4):
    out = v[0:ATT_BLK]
    for h in range(1, n):
        out = jnp.where(lane_head == h, v[h * ATT_BLK:(h + 1) * ATT_BLK], out)
    return out


def _att_bias(has_prev):
    nk = 2 * ATT_BLK if has_prev else ATT_BLK
    a = lax.broadcasted_iota(jnp.int32, (4 * ATT_BLK, nk), 0) % ATT_BLK
    kk = lax.broadcasted_iota(jnp.int32, (4 * ATT_BLK, nk), 1)
    if not has_prev:
        return None, jnp.where((a - kk) >= 0, 0.0, NEG)
    dist = ATT_BLK + a - kk
    inside = (dist >= 0) & (dist <= ATT_BLK)
    return jnp.where(inside, 0.0, NEG), jnp.where(inside & (kk >= ATT_BLK), 0.0, NEG)


def _class_rows(ib, r, d):
    if d == 1:
        return pl.ds(pl.multiple_of(ib * ATT_BLK, ATT_BLK), ATT_BLK)
    return pl.ds(ib * ATT_BLK * d + r, ATT_BLK, stride=d)


def _slab_pair(ref, g, rows):
    return jnp.concatenate([ref[2 * g, rows, :], ref[2 * g + 1, rows, :]], axis=1)


def _att_blocks(d):
    nb = S // d // ATT_BLK
    return nb, nb > 1


def _att_fwd(aq, ak, av, exchange, exchange_args):
    def body(q_ref, k_ref, v_ref, o_ref, l_ref, cat_ref, xc):
        xc.start()
        lane_head = lax.broadcasted_iota(jnp.int32, (ATT_BLK, 256), 1) // 64
        for pi, d in enumerate(PATTERN_DILATIONS):
            if pi == len(PATTERN_DILATIONS) - 1:
                xc.middle()
            nb, has_prev = _att_blocks(d)
            bias_rest, bias_first = _att_bias(has_prev)

            def block(b, carry, pi=pi, d=d, nb=nb, has_prev=has_prev, bias_rest=bias_rest, bias_first=bias_first):
                r, ib = b // nb, b % nb
                rows = _class_rows(ib, r, d)
                prow = _class_rows(jnp.maximum(ib - 1, 0), r, d)
                bias = jnp.where(ib == 0, bias_first, bias_rest) if has_prev else bias_first
                for g in range(2):
                    qg = _slab_pair(q_ref, g, rows).astype(BF16)
                    kg = _slab_pair(k_ref, g, rows)
                    vg = _slab_pair(v_ref, g, rows)
                    if has_prev:
                        kg = jnp.concatenate([_slab_pair(k_ref, g, prow), kg], axis=0)
                        vg = jnp.concatenate([_slab_pair(v_ref, g, prow), vg], axis=0)
                    kg, vg = kg.astype(BF16), vg.astype(BF16)
                    s = _nt(_stack_heads(qg, lane_head), kg) * ATT_SCALE + bias
                    m = jnp.max(s, axis=-1, keepdims=True)
                    p = jnp.exp(s - m)
                    den = jnp.sum(p, axis=-1, keepdims=True)
                    og = _unstack_heads(_nn(p.astype(BF16), vg) / den, lane_head)
                    lg = _unstack_heads(jnp.broadcast_to(m + jnp.log(den), (4 * ATT_BLK, 256)), lane_head)
                    for jj in range(2):
                        j = 2 * g + jj
                        o_new, l_new = og[:, 128 * jj:128 * jj + 128], lg[:, 128 * jj:128 * jj + 128]
                        if pi > 0:
                            o_old, l_old = o_ref[j, rows, :], l_ref[j, rows, :]
                            mx = jnp.maximum(l_old, l_new)
                            ea, eb = jnp.exp(l_old - mx), jnp.exp(l_new - mx)
                            den = ea + eb
                            o_new = (ea * o_old + eb * o_new) / den
                            l_new = mx + jnp.log(den)
                        o_ref[j, rows, :] = o_new
                        l_ref[j, rows, :] = l_new
                return carry

            lax.fori_loop(0, S // ATT_BLK, block, 0)

        def to_cat(i, carry):
            rows = _rows(i, 256)
            for j in range(4):
                cat_ref[rows, 128 * j:128 * j + 128] = o_ref[j, rows, :].astype(BF16)
            return carry

        lax.fori_loop(0, S // 256, to_cat, 0)
        xc.finish()

    slab = jax.ShapeDtypeStruct((4, S, 128), F32)
    return _carry("att_fwd", body, exchange, exchange_args, (aq, ak, av), [VMEM] * 3, [VMEM] * 3,
                  [slab, slab, jax.ShapeDtypeStruct((S, 512), BF16)])


def _mix_fwd(cat_r, cat_a, wout, x, g2, g3, exchange, exchange_args):
    tm = 512

    def body(cr_ref, ca_ref, w_ref, x_ref, g2_ref, g3_ref, mix_ref, x2_ref, h3_ref, xc):
        @pl.when(pl.program_id(0) == 0)
        def _():
            xc.start()

        mix = _nn(cr_ref[...], w_ref[0:512, :]) + _nn(ca_ref[...], w_ref[512:1024, :])
        mix_ref[...] = mix
        x2 = x_ref[...] + mix * _rstd(mix) * g2_ref[...]
        x2_ref[...] = x2
        h3_ref[...] = (x2 * _rstd(x2) * g3_ref[...]).astype(BF16)

        @pl.when(pl.program_id(0) == S // tm - 1)
        def _():
            xc.middle()
            xc.finish()

    row = lambda w: pl.BlockSpec((tm, w), lambda i: (i, 0))
    vec = pl.BlockSpec((1, D), lambda i: (0, 0))
    return _carry("mix_fwd", body, exchange, exchange_args, (cat_r, cat_a, wout, x, g2, g3),
                  [row(512), row(512), pl.BlockSpec((D, D), lambda i: (0, 0)), row(D), vec, vec],
                  [row(D), row(D), row(D)],
                  [jax.ShapeDtypeStruct((S, D), F32), jax.ShapeDtypeStruct((S, D), F32),
                   jax.ShapeDtypeStruct((S, D), BF16)],
                  grid=(S // tm,), semantics=("arbitrary",))


def _ffn_fwd(h3, wg, wu, wd, x2, tgt, g4):
    tm = 512
    last = N_CHIP - 1

    def body(h_ref, wg_ref, wu_ref, wd_ref, x2_ref, t_ref, g_ref,
             gt_ref, up_ref, a_ref, loss_ref, dy_ref, df_ref, dg_ref, f_ref):
        k, i = pl.program_id(0), pl.program_id(1)
        h = h_ref[...]
        gt = _nt(h, wg_ref[...])
        up = _nt(h, wu_ref[...])
        gt_ref[...] = gt.astype(BF16)
        up_ref[...] = up.astype(BF16)
        a = (gt * _sigmoid(gt) * up).astype(BF16)
        a_ref[...] = a
        part = _nn(a, wd_ref[...])
        rows = _rows(i, tm)

        @pl.when(k == 0)
        def _():
            f_ref[rows, :] = part

        @pl.when((k > 0) & (k < last))
        def _():
            f_ref[rows, :] = f_ref[rows, :] + part

        @pl.when((k == last) & (i == 0))
        def _():
            loss_ref[...] = jnp.zeros_like(loss_ref)
            dg_ref[...] = jnp.zeros_like(dg_ref)

        @pl.when(k == last)
        def _():
            fv = f_ref[rows, :] + part
            r = _rstd(fv)
            fn = fv * r
            e = x2_ref[...] + fn * g_ref[...] - t_ref[...]
            loss_ref[...] = loss_ref[...] + jnp.sum(jnp.sum(e * e, axis=-1, keepdims=True), axis=0, keepdims=True)
            dy = e * (1.0 / D)
            dy_ref[...] = dy
            dg_ref[...] = dg_ref[...] + jnp.sum(dy * fn, axis=0, keepdims=True)
            t = dy * g_ref[...]
            df_ref[...] = (r * (t - fn * jnp.mean(t * fn, axis=-1, keepdims=True))).astype(BF16)

    wrow = pl.BlockSpec((None, FF_C, D), lambda k, i: (k, 0, 0))
    act = pl.BlockSpec((None, tm, FF_C), lambda k, i: (k, i, 0))
    late = pl.BlockSpec((tm, D), lambda k, i: (jnp.where(k == last, i, 0), 0))
    vec = pl.BlockSpec((1, D), lambda k, i: (0, 0))
    return pl.pallas_call(
        body, grid=(N_CHIP, S // tm), name="ffn_fwd",
        in_specs=[pl.BlockSpec((tm, D), lambda k, i: (i, 0)), wrow, wrow, wrow, late, late, vec],
        out_specs=[act, act, act, vec, late, late, vec],
        out_shape=[jax.ShapeDtypeStruct((N_CHIP, S, FF_C), BF16)] * 3
                  + [jax.ShapeDtypeStruct((1, D), F32), jax.ShapeDtypeStruct((S, D), F32),
                     jax.ShapeDtypeStruct((S, D), BF16), jax.ShapeDtypeStruct((1, D), F32)],
        scratch_shapes=[pltpu.VMEM((S, D), F32)],
        compiler_params=_params("arbitrary", "arbitrary"),
    )(h3, wg, wu, wd, x2, tgt, g4)


def _ffn_bwd_act(df, gt, up, wg, wu, wd, dy, x2, mix, g2, g3):
    tm, sub = 512, 256
    last = N_CHIP - 1

    def body(df_ref, gt_ref, up_ref, wg_ref, wu_ref, wd_ref, dy_ref, x2_ref, mix_ref, g2_ref, g3_ref,
             dgt_ref, dup_ref, dx2_ref, dmix_ref, dg3_ref, dg2_ref, dh_ref):
        k, i = pl.program_id(0), pl.program_id(1)
        parts = []
        for s in range(tm // sub):
            rows = slice(s * sub, (s + 1) * sub)
            da = _nt(df_ref[rows, :], wd_ref[...])
            gt, up = gt_ref[rows, :].astype(F32), up_ref[rows, :].astype(F32)
            sg = _sigmoid(gt)
            dup = (da * gt * sg).astype(BF16)
            dgt = (da * up * (sg * (1.0 + gt * (1.0 - sg)))).astype(BF16)
            dup_ref[rows, :] = dup
            dgt_ref[rows, :] = dgt
            parts.append(_nn(dgt, wg_ref[...]) + _nn(dup, wu_ref[...]))
        part = jnp.concatenate(parts, axis=0)
        rows = _rows(i, tm)

        @pl.when(k == 0)
        def _():
            dh_ref[rows, :] = part

        @pl.when((k > 0) & (k < last))
        def _():
            dh_ref[rows, :] = dh_ref[rows, :] + part

        @pl.when((k == last) & (i == 0))
        def _():
            dg3_ref[...] = jnp.zeros_like(dg3_ref)
            dg2_ref[...] = jnp.zeros_like(dg2_ref)

        @pl.when(k == last)
        def _():
            dh = dh_ref[rows, :] + part
            x2 = x2_ref[...]
            r3 = _rstd(x2)
            xn = x2 * r3
            dg3_ref[...] = dg3_ref[...] + jnp.sum(dh * xn, axis=0, keepdims=True)
            t = dh * g3_ref[...]
            dx2 = dy_ref[...] + r3 * (t - xn * jnp.mean(t * xn, axis=-1, keepdims=True))
            dx2_ref[...] = dx2
            mix = mix_ref[...]
            r2 = _rstd(mix)
            mn = mix * r2
            dg2_ref[...] = dg2_ref[...] + jnp.sum(dx2 * mn, axis=0, keepdims=True)
            u = dx2 * g2_ref[...]
            dmix_ref[...] = (r2 * (u - mn * jnp.mean(u * mn, axis=-1, keepdims=True))).astype(BF16)

    wrow = pl.BlockSpec((None, FF_C, D), lambda k, i: (k, 0, 0))
    act = pl.BlockSpec((None, tm, FF_C), lambda k, i: (k, i, 0))
    row = pl.BlockSpec((tm, D), lambda k, i: (i, 0))
    late = pl.BlockSpec((tm, D), lambda k, i: (jnp.where(k == last, i, 0), 0))
    vec = pl.BlockSpec((1, D), lambda k, i: (0, 0))
    return pl.pallas_call(
        body, grid=(N_CHIP, S // tm), name="ffn_bwd_act",
        in_specs=[row, act, act, wrow, wrow, wrow, late, late, late, vec, vec],
        out_specs=[act, act, late, late, vec, vec],
        out_shape=[jax.ShapeDtypeStruct((N_CHIP, S, FF_C), BF16), jax.ShapeDtypeStruct((N_CHIP, S, FF_C), BF16),
                   jax.ShapeDtypeStruct((S, D), F32), jax.ShapeDtypeStruct((S, D), BF16),
                   jax.ShapeDtypeStruct((1, D), F32), jax.ShapeDtypeStruct((1, D), F32)],
        scratch_shapes=[pltpu.VMEM((S, D), F32)],
        compiler_params=_params("arbitrary", "arbitrary"),
    )(df, gt, up, wg, wu, wd, dy, x2, mix, g2, g3)


def _ffn_bwd_w(a, df, h3, dgt, dup):
    tm = 1024
    assert S // tm == 2

    def body(a_ref, df_ref, h_ref, dgt_ref, dup_ref, dwd_ref, dwg_ref, dwu_ref, acc_d, acc_g, acc_u):
        i = pl.program_id(1)
        h = h_ref[...]
        parts = (_tn(a_ref[...], df_ref[...]), _tn(dgt_ref[...], h), _tn(dup_ref[...], h))

        @pl.when(i == 0)
        def _():
            for acc, part in zip((acc_d, acc_g, acc_u), parts):
                acc[...] = part

        @pl.when(i == S // tm - 1)
        def _():
            for out, acc, part in zip((dwd_ref, dwg_ref, dwu_ref), (acc_d, acc_g, acc_u), parts):
                out[...] = (acc[...] + part).astype(BF16)

    act = pl.BlockSpec((None, tm, FF_C), lambda k, i: (k, i, 0))
    row = pl.BlockSpec((tm, D), lambda k, i: (i, 0))
    wrow = pl.BlockSpec((None, FF_C, D), lambda k, i: (k, 0, 0))
    return pl.pallas_call(
        body, grid=(N_CHIP, S // tm), name="ffn_bwd_w",
        in_specs=[act, row, row, act, act],
        out_specs=[wrow, wrow, wrow],
        out_shape=[jax.ShapeDtypeStruct((N_CHIP, FF_C, D), BF16)] * 3,
        scratch_shapes=[pltpu.VMEM((FF_C, D), F32)] * 3,
        compiler_params=_params("parallel", "arbitrary"),
    )(a, df, h3, dgt, dup)


def _mix_bwd(dmix, cat_r, cat_a, wout, exchange, exchange_args):
    tm = 512

    def body(dm_ref, cr_ref, ca_ref, w_ref, dret_ref, datt_ref, dw_ref, acc, xc):
        i = pl.program_id(0)

        @pl.when(i == 0)
        def _():
            xc.start()
            acc[...] = jnp.zeros_like(acc)

        dm = dm_ref[...]
        dret_ref[...] = _nt(dm, w_ref[0:512, :])
        datt = _nt(dm, w_ref[512:1024, :])
        for j in range(4):
            datt_ref[j] = datt[:, 128 * j:128 * j + 128]
        acc[0:512, :] += _tn(cr_ref[...], dm)
        acc[512:1024, :] += _tn(ca_ref[...], dm)

        @pl.when(i == S // tm - 1)
        def _():
            dw_ref[...] = acc[...].astype(BF16)
            xc.middle()
            xc.finish()

    row = lambda w: pl.BlockSpec((tm, w), lambda i: (i, 0))
    full = pl.BlockSpec((D, D), lambda i: (0, 0))
    return _carry("mix_bwd", body, exchange, exchange_args, (dmix, cat_r, cat_a, wout),
                  [row(D), row(512), row(512), full],
                  [row(512), pl.BlockSpec((4, tm, 128), lambda i: (0, i, 0)), full],
                  [jax.ShapeDtypeStruct((S, 512), F32), jax.ShapeDtypeStruct((4, S, 128), F32),
                   jax.ShapeDtypeStruct((D, D), BF16)],
                  scratch_shapes=[pltpu.VMEM((D, D), F32)], grid=(S // tm,), semantics=("arbitrary",))


def _att_bwd(aq, ak, av, datt, att_out, lse, exchange, exchange_args, after=None):
    def body(q_ref, k_ref, v_ref, do_ref, out_ref, l_ref, dq_ref, dk_ref, dv_ref, xc):
        xc.start()

        def clear(i, carry):
            rows = _rows(i, 256)
            for ref in (dq_ref, dk_ref, dv_ref):
                for j in range(4):
                    ref[j, rows, :] = jnp.zeros((256, 128), F32)
            return carry

        lax.fori_loop(0, S // 256, clear, 0)
        lane_head = lax.broadcasted_iota(jnp.int32, (ATT_BLK, 256), 1) // 64
        for d in PATTERN_DILATIONS:
            nb, has_prev = _att_blocks(d)
            bias_rest, bias_first = _att_bias(has_prev)

            def block(b, carry, d=d, nb=nb, has_prev=has_prev, bias_rest=bias_rest, bias_first=bias_first):
                r, ib = b // nb, b % nb
                rows = _class_rows(ib, r, d)
                prow = _class_rows(jnp.maximum(ib - 1, 0), r, d)
                bias = jnp.where(ib == 0, bias_first, bias_rest) if has_prev else bias_first
                for g in range(2):
                    qg = _slab_pair(q_ref, g, rows).astype(BF16)
                    kg = _slab_pair(k_ref, g, rows)
                    vg = _slab_pair(v_ref, g, rows)
                    if has_prev:
                        kg = jnp.concatenate([_slab_pair(k_ref, g, prow), kg], axis=0)
                        vg = jnp.concatenate([_slab_pair(v_ref, g, prow), vg], axis=0)
                    kg, vg = kg.astype(BF16), vg.astype(BF16)
                    dog = _slab_pair(do_ref, g, rows)
                    outg = _slab_pair(out_ref, g, rows)
                    lg = _slab_pair(l_ref, g, rows)
                    qs = _stack_heads(qg, lane_head)
                    dos = _stack_heads(dog, lane_head)
                    delta = jnp.sum(dos * jnp.concatenate([outg] * 4, axis=0), axis=-1, keepdims=True)
                    lh = jnp.max(_stack_heads(lg, lane_head, NEG), axis=-1, keepdims=True)
                    s = _nt(qs, kg) * ATT_SCALE + bias
                    p = jnp.exp(s - lh)
                    dosb = dos.astype(BF16)
                    ds = (p * (_nt(dosb, vg) - delta) * ATT_SCALE).astype(BF16)
                    dq = _unstack_heads(_nn(ds, kg), lane_head)
                    dk = _tn(ds, qs)
                    dv = _tn(p.astype(BF16), dosb)
                    for jj in range(2):
                        j, sl = 2 * g + jj, slice(128 * jj, 128 * jj + 128)
                        dq_ref[j, rows, :] += dq[:, sl]
                        if has_prev:
                            dk_ref[j, prow, :] += dk[0:ATT_BLK, sl]
                            dv_ref[j, prow, :] += dv[0:ATT_BLK, sl]
                            dk_ref[j, rows, :] += dk[ATT_BLK:2 * ATT_BLK, sl]
                            dv_ref[j, rows, :] += dv[ATT_BLK:2 * ATT_BLK, sl]
                        else:
                            dk_ref[j, rows, :] += dk[:, sl]
                            dv_ref[j, rows, :] += dv[:, sl]
                return carry

            lax.fori_loop(0, S // ATT_BLK, block, 0)
        xc.middle()
        xc.finish()

    slab = jax.ShapeDtypeStruct((4, S, 128), F32)
    return _carry("att_bwd", body, exchange, exchange_args, (aq, ak, av, datt, att_out, lse), [VMEM] * 6, [VMEM] * 3,
                  [slab, slab, slab], after=after)


def _ret_bwd(qr, kr, rv, proj, o_raw, states, dret, tabs, exchange, exchange_args, after=None):
    C = RET_C
    nc = S // C
    dtab, a_tab, b_tab, lam, bd = tabs

    def body(q_ref, k_ref, v_ref, g_ref, o_ref, st_ref, dr_ref, dt_ref, a_ref, b_ref, lam_ref, bd_ref,
             dq_ref, dk_ref, dv_ref, dg_ref, dR, exch):
        @pl.when(pl.program_id(0) == 0)
        def _():
            exch.start()
            dR[...] = jnp.zeros_like(dR)

        q, k, v = q_ref[...], k_ref[...], v_ref[...]
        lane_head = lax.broadcasted_iota(jnp.int32, (C, 256), 1) // 32
        col_head = lax.broadcasted_iota(jnp.int32, (C, 256), 1) // 64
        dos = []
        for j in range(4):
            sl = slice(128 * j, 128 * j + 128)
            oj = o_ref[:, sl]
            xc = oj - _seg_mean(oj)
            rs = lax.rsqrt(_seg_mean(xc * xc) + GN_EPS)
            rn = xc * rs
            gj = g_ref[:, sl]
            sg = _sigmoid(gj)
            dret = dr_ref[:, sl]
            dg_ref[:, sl] = dret * rn * (sg * (1.0 + gj * (1.0 - sg)))
            drn = dret * (gj * sg)
            dos.append(rs * (drn - _seg_mean(drn) - rn * _seg_mean(drn * rn)))
        do = [jnp.concatenate(dos[0:2], axis=1), jnp.concatenate(dos[2:4], axis=1)]
        do8 = jnp.concatenate(do, axis=1).astype(BF16)
        drb = dR[...].astype(BF16)
        rb = st_ref[...]
        dq = _nt(do8, rb) * a_ref[...]
        dk = _nt(v, drb) * b_ref[...]
        kb = (k.astype(F32) * b_ref[...]).astype(BF16)
        dvall = _nn(kb, drb)
        qs = _stack_heads(q, lane_head, n=8)
        dec = dt_ref[...]
        p = (_nt(qs, k) * dec).astype(BF16)
        dos = [_stack_heads(do[g], col_head).astype(BF16) for g in range(2)]
        dp = jnp.concatenate([_nt(dos[g], v[:, 256 * g:256 * g + 256]) for g in range(2)], axis=0)
        ds = (dp * dec).astype(BF16)
        dq = dq + _unstack_heads(_nn(ds, k), lane_head, n=8)
        dk = dk + _tn(ds, qs)
        dv = [dvall[:, 256 * g:256 * g + 256] + _tn(p[4 * C * g:4 * C * (g + 1)], dos[g]) for g in range(2)]
        qa = (q.astype(F32) * a_ref[...]).astype(BF16)
        dR[...] = dR[...] * lam_ref[...] + _tn(qa, do8) * bd_ref[...]
        dq_ref[...] = dq
        dk_ref[...] = dk
        dv_ref[:, 0:256] = dv[0]
        dv_ref[:, 256:512] = dv[1]

        @pl.when(pl.program_id(0) == nc - 1)
        def _():
            exch.middle()
            exch.finish()

    rev = lambda w: pl.BlockSpec((C, w), lambda n: (nc - 1 - n, 0))
    full = lambda a: pl.BlockSpec(a.shape, lambda n: (0,) * a.ndim)
    return _carry(
        "ret_bwd", body, exchange, exchange_args, (qr, kr, rv, proj, o_raw, states, dret, dtab, a_tab, b_tab, lam, bd),
        [rev(256), rev(256), rev(512), rev(512), rev(512),
         pl.BlockSpec((None, 256, 512), lambda n: (nc - 1 - n, 0, 0)), rev(512),
         full(dtab), full(a_tab), full(b_tab), full(lam), full(bd)],
        [rev(256), rev(256), rev(512), rev(512)],
        [jax.ShapeDtypeStruct((S, 256), F32), jax.ShapeDtypeStruct((S, 256), F32),
         jax.ShapeDtypeStruct((S, 512), F32), jax.ShapeDtypeStruct((S, 512), F32)],
        scratch_shapes=[pltpu.VMEM((256, 512), F32)], grid=(nc,), semantics=("arbitrary",), after=after)


def _rot_bwd(cos, sin, dqr, dkr, drv, drg, dq_att, dk_att, dv_att):
    tm = 256

    def body(cos_ref, sin_ref, dqr_ref, dkr_ref, drv_ref, drg_ref, dqa_ref, dka_ref, dva_ref, dp_ref):
        cr, ca, sr, sa = cos_ref[:, 0:256], cos_ref[:, 256:768], sin_ref[:, 0:256], sin_ref[:, 256:768]
        lo_r, lo_a = _rot_halves(tm)

        def unrot_r(g):
            gs = g * sr
            return g * cr + pltpu.roll(jnp.where(lo_r, -gs, 0.0), 16, 1) + pltpu.roll(jnp.where(lo_r, 0.0, gs), 240, 1)

        def unrot_a(g):
            gs = g * sa
            return g * ca + pltpu.roll(jnp.where(lo_a, -gs, 0.0), 8, 1) + pltpu.roll(jnp.where(lo_a, 0.0, gs), 504, 1)

        def wide(ref):
            return jnp.concatenate([ref[j] for j in range(4)], axis=1)

        dp_ref[:, 0:256] = unrot_r(dqr_ref[...]).astype(BF16)
        dp_ref[:, 256:512] = unrot_r(dkr_ref[...] * RET_SCALE).astype(BF16)
        dp_ref[:, 512:1024] = drv_ref[...].astype(BF16)
        dp_ref[:, 1024:1536] = drg_ref[...].astype(BF16)
        dp_ref[:, 1536:2048] = unrot_a(wide(dqa_ref)).astype(BF16)
        dp_ref[:, 2048:2560] = unrot_a(wide(dka_ref)).astype(BF16)
        dp_ref[:, 2560:3072] = wide(dva_ref).astype(BF16)

    row = lambda w: pl.BlockSpec((tm, w), lambda i: (i, 0))
    slab = pl.BlockSpec((4, tm, 128), lambda i: (0, i, 0))
    return pl.pallas_call(
        body, grid=(S // tm,), name="rot_bwd",
        in_specs=[row(768), row(768), row(256), row(256), row(512), row(512), slab, slab, slab],
        out_specs=row(PW), out_shape=jax.ShapeDtypeStruct((S, PW), BF16),
        compiler_params=_params("parallel"),
    )(cos, sin, dqr, dkr, drv, drg, dq_att, dk_att, dv_att)


def _win_bwd_w(h1, dproj, exchange, exchange_args):
    tm = 512

    def body(h_ref, dp_ref, dw_ref, acc, xc):
        k, i = pl.program_id(0), pl.program_id(1)

        @pl.when((k == 0) & (i == 0))
        def _():
            xc.start()

        @pl.when(i == 0)
        def _():
            acc[...] = jnp.zeros_like(acc)

        acc[...] += _tn(h_ref[...], dp_ref[...])

        @pl.when(i == S // tm - 1)
        def _():
            dw_ref[...] = acc[...].astype(BF16)

        @pl.when((k == N_CHIP - 1) & (i == S // tm - 1))
        def _():
            xc.middle()
            xc.finish()

    (dw,), out = _carry(
        "win_bwd_w", body, exchange, exchange_args, (h1, dproj),
        [pl.BlockSpec((tm, D), lambda k, i: (i, 0)), pl.BlockSpec((tm, WIN_C), lambda k, i: (i, k))],
        [pl.BlockSpec((None, D, WIN_C), lambda k, i: (k, 0, 0))],
        [jax.ShapeDtypeStruct((N_CHIP, D, WIN_C), BF16)],
        scratch_shapes=[pltpu.VMEM((D, WIN_C), F32)], grid=(N_CHIP, S // tm), semantics=("arbitrary", "arbitrary"))
    return dw, out


def _in_bwd(dproj, win_g, x, dx2, g1, after):
    tm = 512

    def body(dp_ref, w_ref, x_ref, dx2_ref, g_ref, dx_ref, dg_ref, _):
        @pl.when(pl.program_id(0) == 0)
        def _():
            dg_ref[...] = jnp.zeros_like(dg_ref)

        dh = _nt(dp_ref[:, 0:WIN_C], w_ref[0])
        for k in range(1, N_CHIP):
            dh = dh + _nt(dp_ref[:, k * WIN_C:(k + 1) * WIN_C], w_ref[k])
        xv = x_ref[...]
        r = _rstd(xv)
        xn = xv * r
        dg_ref[...] = dg_ref[...] + jnp.sum(dh * xn, axis=0, keepdims=True)
        t = dh * g_ref[...]
        dx_ref[...] = dx2_ref[...] + r * (t - xn * jnp.mean(t * xn, axis=-1, keepdims=True))

    row = lambda w: pl.BlockSpec((tm, w), lambda i: (i, 0))
    vec = pl.BlockSpec((1, D), lambda i: (0, 0))
    return _carry("in_bwd", body, _NoExchange(), (), (dproj, win_g, x, dx2, g1),
                  [row(PW), pl.BlockSpec((N_CHIP, D, WIN_C), lambda i: (0, 0, 0)), row(D), row(D), vec],
                  [row(D), vec], [jax.ShapeDtypeStruct((S, D), F32), jax.ShapeDtypeStruct((1, D), F32)],
                  grid=(S // tm,), semantics=("arbitrary",), after=after)[0]


ANY = pl.BlockSpec(memory_space=pl.ANY)
VMEM = pl.BlockSpec(memory_space=pltpu.VMEM)
FLIPS = ((1, 0), (0, 1), (1, 1))


def _place():
    x, y, c = lax.axis_index("x"), lax.axis_index("y"), lax.axis_index("c")
    chips = [((1 - x) if fx else x, (1 - y) if fy else y) for fx, fy in FLIPS]
    return x, y, c, 2 * x + y, chips


def _remote(src, dst, send_sem, recv_sem, device):
    return pltpu.make_async_remote_copy(src_ref=src, dst_ref=dst, send_sem=send_sem, recv_sem=recv_sem,
                                        device_id=device, device_id_type=MESH)


class _Exchange:
    aliases = {}

    def middle(self, ins, outs, sems):
        pass


class _GatherShards(_Exchange):
    def __init__(self, shards):
        n = self.n = len(shards)
        self.n_in = self.n_out = n
        self.out_shape = [jax.ShapeDtypeStruct((N_CHIP,) + s.shape, s.dtype) for s in shards]
        dma = pltpu.SemaphoreType.DMA
        self.scratch = [dma((3 * n,)), dma((3 * n,)), dma((3 * n,)), dma((3 * n,)), dma((n,)), dma((n,))]

    def _ici(self, ins, outs, sems, a, j, chip):
        x, y, c, me, chips = _place()
        half = ins[a].shape[0] // 2
        return _remote(ins[a].at[pl.ds(c * half, half), :], outs[a].at[me, pl.ds(c * half, half), :],
                       sems[0].at[3 * a + j], sems[1].at[3 * a + j], (*chip, c))

    def _fwd(self, outs, sems, a, j, chip, half_of):
        x, y, c, me, chips = _place()
        half = outs[a].shape[1] // 2
        blk = outs[a].at[2 * chip[0] + chip[1], pl.ds(half_of * half, half), :]
        return _remote(blk, blk, sems[2].at[3 * a + j], sems[3].at[3 * a + j], (x, y, 1 - c))

    def _own(self, ins, outs, sems, a):
        return _own_shard_to_sibling(ins[a], outs[a], sems[4].at[a], sems[5].at[a])

    def start(self, ins, outs, sems):
        chips = _place()[4]
        for a in range(self.n):
            for j, chip in enumerate(chips):
                self._ici(ins, outs, sems, a, j, chip).start()
        for a in range(self.n):
            self._own(ins, outs, sems, a).start()

    def middle(self, ins, outs, sems):
        x, y, c, me, chips = _place()
        for a in range(self.n):
            for j, chip in enumerate(chips):
                half = outs[a].shape[1] // 2
                blk = outs[a].at[2 * chip[0] + chip[1], pl.ds(c * half, half), :]
                _remote(blk, blk, sems[0].at[3 * a + j], sems[1].at[3 * a + j], (x, y, c)).wait_recv()
                self._fwd(outs, sems, a, j, chip, c).start()

    def finish(self, ins, outs, sems):
        x, y, c, me, chips = _place()
        for a in range(self.n):
            for j, chip in enumerate(chips):
                self._fwd(outs, sems, a, j, chip, 1 - c).wait_recv()
        for a in range(self.n):
            for j, chip in enumerate(chips):
                self._ici(ins, outs, sems, a, j, chip).wait_send()
                self._fwd(outs, sems, a, j, chip, c).wait_send()
            self._own(ins, outs, sems, a).wait()


def _own_shard_to_sibling(shard_ref, gathered_ref, send_sem, recv_sem):
    x, y, c, me, chips = _place()
    return _remote(shard_ref, gathered_ref.at[me], send_sem, recv_sem, (x, y, 1 - c))


class _NoExchange(_Exchange):
    n_in = n_out = 0
    out_shape = ()
    scratch = ()

    def start(self, ins, outs, sems):
        pass

    def finish(self, ins, outs, sems):
        pass


class _ForwardGathered(_Exchange):
    def __init__(self, shards, own=True, forward=True):
        self.own, self.forward = own, forward
        n = self.n = len(shards)
        self.n_in, self.n_out = 2 * n, n
        self.out_shape = [jax.ShapeDtypeStruct((N_CHIP,) + s.shape, s.dtype) for s in shards]
        dma = pltpu.SemaphoreType.DMA
        self.scratch = [dma((3 * n,)), dma((3 * n,)), dma((n,)), dma((n,))]
        self.aliases = {n + a: a for a in range(n)}

    def _fwd(self, outs, sems, a, j, chip, half_of):
        x, y, c, me, chips = _place()
        half = outs[a].shape[1] // 2
        blk = outs[a].at[2 * chip[0] + chip[1], pl.ds(half_of * half, half), :]
        return _remote(blk, blk, sems[0].at[3 * a + j], sems[1].at[3 * a + j], (x, y, 1 - c))

    def _own(self, ins, outs, sems, a):
        return _own_shard_to_sibling(ins[a], outs[a], sems[2].at[a], sems[3].at[a])

    def start(self, ins, outs, sems):
        x, y, c, me, chips = _place()
        for a in range(self.n):
            for j, chip in enumerate(chips if self.forward else ()):
                self._fwd(outs, sems, a, j, chip, c).start()
        for a in range(self.n if self.own else 0):
            self._own(ins, outs, sems, a).start()

    def finish(self, ins, outs, sems):
        x, y, c, me, chips = _place()
        for a in range(self.n):
            for j, chip in enumerate(chips if self.forward else ()):
                self._fwd(outs, sems, a, j, chip, 1 - c).wait_recv()
        for a in range(self.n):
            for j, chip in enumerate(chips if self.forward else ()):
                self._fwd(outs, sems, a, j, chip, c).wait_send()
            if self.own:
                self._own(ins, outs, sems, a).wait()


HBM = pl.BlockSpec(memory_space=pltpu.HBM)
SEMS = pl.BlockSpec(memory_space=pltpu.SEMAPHORE)
DATAFLOW = pltpu.SideEffectType.DATAFLOW_SIDE_EFFECTING


class _OverIci:
    def __init__(self, name, sources, lands):
        self.name, self.n = name, len(sources)
        hbm = lambda t: pltpu.with_memory_space_constraint(t, pltpu.HBM)
        self.arrays = [hbm(t) for t in sources] + [hbm(t) for t in lands]

    def sent(self, src, land, a, chip):
        raise NotImplementedError

    def landed(self, land, a, chip):
        raise NotImplementedError

    def _copy(self, arr, sems, a, j, receiving):
        x, y, c, me, chips = _place()
        src, dst = self.sent(arr[a], arr[self.n + a], a, chips[j])
        if receiving:
            dst = self.landed(arr[self.n + a], a, chips[j])
        return _remote(src, dst, sems[0].at[3 * a + j], sems[1].at[3 * a + j], (*chips[j], c))

    def start(self, after):
        m = len(self.arrays)

        def body(*refs):
            arr, sems, token = refs[:m], refs[m + 1:m + 3], refs[-1]
            for a in range(self.n):
                for j in range(3):
                    self._copy(arr, sems, a, j, False).start()
            token[...] = jnp.zeros_like(token)

        dma = pltpu.SemaphoreType.DMA
        outs = pl.pallas_call(
            body, name=self.name + "_start",
            out_shape=[dma((3 * self.n,)), dma((3 * self.n,))] + [pltpu.HBM(t.shape, t.dtype) for t in self.arrays]
                      + [jax.ShapeDtypeStruct((8, 128), F32)],
            in_specs=[HBM] * m + [ANY], out_specs=[SEMS, SEMS] + [HBM] * m + [VMEM],
            input_output_aliases={i: 2 + i for i in range(m)},
            compiler_params=pltpu.CompilerParams(has_side_effects=DATAFLOW),
        )(*self.arrays, after)
        self.sems, self.arrays = outs[0:2], list(outs[2:2 + m])
        return outs[-1]

    def wait(self, after):
        m = len(self.arrays)

        def body(*refs):
            arr, sems = refs[:m], refs[m:m + 2]
            for a in range(self.n):
                for j in range(3):
                    self._copy(arr, sems, a, j, False).wait_send()
                    self._copy(arr, sems, a, j, True).wait_recv()

        outs = pl.pallas_call(
            body, name=self.name + "_wait",
            out_shape=[pltpu.HBM(t.shape, t.dtype) for t in self.arrays],
            in_specs=[HBM] * m + [SEMS, SEMS, ANY], out_specs=[HBM] * m,
            input_output_aliases={i: i for i in range(m)},
            compiler_params=pltpu.CompilerParams(has_side_effects=DATAFLOW),
        )(*self.arrays, *self.sems, after)
        return list(outs[:self.n]), list(outs[self.n:])


class _GatherOverIci(_OverIci):
    def __init__(self, name, shards):
        super().__init__(name, shards, [lax.empty((N_CHIP,) + s.shape, s.dtype) for s in shards])

    @staticmethod
    def _half(ref):
        c = lax.axis_index("c")
        half = ref.shape[-2] // 2
        return pl.ds(c * half, half)

    def sent(self, src, land, a, chip):
        return src.at[self._half(src), :], land.at[_place()[3], self._half(src), :]

    def landed(self, land, a, chip):
        return land.at[2 * chip[0] + chip[1], self._half(land), :]


class _SumOverIci(_OverIci):
    def __init__(self, name, pre):
        super().__init__(name, pre, [lax.empty(p.shape, p.dtype) for p in pre])

    def sent(self, src, land, a, chip):
        return src.at[2 * chip[0] + chip[1]], land.at[_place()[3]]

    def landed(self, land, a, chip):
        return land.at[2 * chip[0] + chip[1]]


class _HalvesToSibling(_Exchange):
    def __init__(self, grads):
        n = self.n = len(grads)
        self.n_in = self.n_out = n
        self.out_shape = [jax.ShapeDtypeStruct((N_CHIP, g.shape[1] // 2, g.shape[2]), g.dtype) for g in grads]
        self.scratch = [pltpu.SemaphoreType.DMA((n,)), pltpu.SemaphoreType.DMA((n,))]

    def _copy(self, ins, outs, sems, a):
        x, y, c, me, chips = _place()
        half = ins[a].shape[1] // 2
        return _remote(ins[a].at[:, pl.ds((1 - c) * half, half), :], outs[a], sems[0].at[a], sems[1].at[a], (x, y, 1 - c))

    def start(self, ins, outs, sems):
        for a in range(self.n):
            self._copy(ins, outs, sems, a).start()

    def finish(self, ins, outs, sems):
        for a in range(self.n):
            self._copy(ins, outs, sems, a).wait_recv()
        for a in range(self.n):
            self._copy(ins, outs, sems, a).wait_send()


class _ShareHalves(_Exchange):
    def __init__(self, fulls):
        n = self.n = len(fulls)
        self.n_in = self.n_out = n
        self.out_shape = [jax.ShapeDtypeStruct(f.shape, f.dtype) for f in fulls]
        self.scratch = [pltpu.SemaphoreType.DMA((n,)), pltpu.SemaphoreType.DMA((n,))]
        self.aliases = {a: a for a in range(n)}

    def _copy(self, outs, sems, a, half_of):
        x, y, c, me, chips = _place()
        half = outs[a].shape[0] // 2
        rows = outs[a].at[pl.ds(half_of * half, half), :]
        return _remote(rows, rows, sems[0].at[a], sems[1].at[a], (x, y, 1 - c))

    def start(self, ins, outs, sems):
        c = _place()[2]
        for a in range(self.n):
            self._copy(outs, sems, a, c).start()

    def finish(self, ins, outs, sems):
        c = _place()[2]
        for a in range(self.n):
            self._copy(outs, sems, a, 1 - c).wait_recv()
        for a in range(self.n):
            self._copy(outs, sems, a, c).wait_send()


class _GatherBlocks(_Exchange):
    def __init__(self, block):
        self.n_in = self.n_out = 1
        self.out_shape = [jax.ShapeDtypeStruct((8,) + block.shape, block.dtype)]
        dma = pltpu.SemaphoreType.DMA
        self.scratch = [dma((7,)), dma((7,)), dma]

    @staticmethod
    def _peer(f):
        x, y, c, me, chips = _place()
        return ((1 - x) if f & 4 else x, (1 - y) if f & 2 else y, (1 - c) if f & 1 else c)

    def start(self, ins, outs, sems):
        x, y, c, me, chips = _place()
        for f in range(1, 8):
            _remote(ins[0], outs[0].at[2 * me + c], sems[0].at[f - 1], sems[1].at[f - 1], self._peer(f)).start()
        pltpu.make_async_copy(ins[0], outs[0].at[2 * me + c], sems[2]).start()

    def finish(self, ins, outs, sems):
        x, y, c, me, chips = _place()
        for f in range(1, 8):
            px, py, pc = self._peer(f)
            blk = outs[0].at[4 * px + 2 * py + pc]
            _remote(blk, blk, sems[0].at[f - 1], sems[1].at[f - 1], (x, y, c)).wait_recv()
        for f in range(1, 8):
            _remote(ins[0], outs[0].at[2 * me + c], sems[0].at[f - 1], sems[1].at[f - 1], self._peer(f)).wait_send()
        pltpu.make_async_copy(ins[0], outs[0].at[2 * me + c], sems[2]).wait()


class _Both(_Exchange):
    def __init__(self, first, second):
        self.parts = (first, second)
        self.n_in, self.n_out = first.n_in + second.n_in, first.n_out + second.n_out
        self.out_shape = first.out_shape + second.out_shape
        self.scratch = first.scratch + second.scratch
        self.aliases = dict(first.aliases)
        self.aliases.update({first.n_in + i: first.n_out + o for i, o in second.aliases.items()})

    def _split(self, ins, outs, sems):
        a, b = self.parts
        return ((a, ins[:a.n_in], outs[:a.n_out], sems[:len(a.scratch)]),
                (b, ins[a.n_in:], outs[a.n_out:], sems[len(a.scratch):]))

    def start(self, ins, outs, sems):
        for ex, i, o, s in self._split(ins, outs, sems):
            ex.start(i, o, s)

    def middle(self, ins, outs, sems):
        for ex, i, o, s in self._split(ins, outs, sems):
            ex.middle(i, o, s)

    def finish(self, ins, outs, sems):
        for ex, i, o, s in self._split(ins, outs, sems):
            ex.finish(i, o, s)


class _Bound:
    def __init__(self, ex, ins, outs, sems):
        self.start = lambda: ex.start(ins, outs, sems)
        self.middle = lambda: ex.middle(ins, outs, sems)
        self.finish = lambda: ex.finish(ins, outs, sems)


def _carry(name, body, ex, ex_args, args, in_specs, out_specs, out_shape, scratch_shapes=(), grid=None, semantics=(),
           after=None):
    n_a, n_o, n_s = len(args), len(out_shape), len(scratch_shapes)
    behind = [] if after is None else [after]

    def full_body(*refs):
        p = 0
        groups = []
        for size in (n_a, ex.n_in, len(behind), n_o, ex.n_out, n_s, len(ex.scratch)):
            groups.append(refs[p:p + size])
            p += size
        a, ei, _, o, eo, s, es = groups
        body(*a, *o, *s, _Bound(ex, ei, eo, es))

    kwargs = {} if grid is None else {"grid": grid}
    outs = pl.pallas_call(
        full_body, name=name,
        in_specs=list(in_specs) + [ANY] * (ex.n_in + len(behind)), out_specs=list(out_specs) + [ANY] * ex.n_out,
        out_shape=list(out_shape) + list(ex.out_shape), scratch_shapes=list(scratch_shapes) + list(ex.scratch),
        input_output_aliases={n_a + i: n_o + o for i, o in ex.aliases.items()},
        compiler_params=_params(*semantics) if semantics else pltpu.CompilerParams(vmem_limit_bytes=VMEM_LIMIT),
        **kwargs,
    )(*args, *ex_args, *behind)
    return outs[:n_o], outs[n_o:]


def _prepare_carrying(name, x, g1, pos, ifc, spread, arrays, ex, ex_args):
    n = len(arrays)
    r, cc = arrays[0].shape
    steps = 4
    tr, tm = r // steps, S // steps

    def body(x_ref, g_ref, pos_ref, ifc_ref, e_ref, *refs):
        src, h_ref, cos_ref, sin_ref, dst, xc = refs[:n], refs[n], refs[n + 1], refs[n + 2], refs[n + 3:2 * n + 3], refs[-1]

        @pl.when(pl.program_id(0) == 0)
        def _():
            xc.start()

        xv = x_ref[...]
        h_ref[...] = (xv * _rstd(xv) * g_ref[...]).astype(BF16)
        ang = pos_ref[...].astype(F32) * ifc_ref[...]
        cos_ref[...] = _spread_exact(jnp.cos(ang), e_ref[...])
        sin_ref[...] = _spread_exact(jnp.sin(ang), e_ref[...])
        for a in range(n):
            dst[a][...] = src[a][...].astype(BF16)

        @pl.when(pl.program_id(0) == steps - 1)
        def _():
            xc.middle()
            xc.finish()

    row = lambda w: pl.BlockSpec((tm, w), lambda i: (i, 0))
    const = lambda w: pl.BlockSpec((1, w), lambda i: (0, 0))
    blk = pl.BlockSpec((tr, cc), lambda i: (i, 0))
    return _carry(name, body, ex, ex_args, (x, g1, pos, ifc, spread, *arrays),
                  [row(D), const(D), row(1), const(128), pl.BlockSpec((128, 768), lambda i: (0, 0))] + [blk] * n,
                  [row(D), row(768), row(768)] + [blk] * n,
                  [jax.ShapeDtypeStruct((S, D), BF16)] + [jax.ShapeDtypeStruct((S, 768), F32)] * 2
                  + [jax.ShapeDtypeStruct((r, cc), BF16)] * n,
                  grid=(steps,), semantics=("arbitrary",))


def _exchange_alone(name, ex, ex_args):
    def body(xc):
        xc.start()
        xc.middle()
        xc.finish()

    return _carry(name, body, ex, ex_args, (), (), (), ())[1]


def _core_index():
    return lax.axis_index("c").astype(jnp.int32).reshape(1)


def _pair_sum(gs, gots):
    n = len(gs)
    _, r, cc = gs[0].shape
    half = r // 2

    def body(c_ref, *refs):
        for a in range(n):
            refs[2 * n + a][...] = (refs[a][...].astype(F32) + refs[n + a][...].astype(F32)).astype(BF16)

    mine = pl.BlockSpec((None, half, cc), lambda k, c_ref: (k, c_ref[0], 0))
    blk = pl.BlockSpec((None, half, cc), lambda k, c_ref: (k, 0, 0))
    return pl.pallas_call(
        body, name=f"pair_sum_{r}x{cc}",
        grid_spec=pltpu.PrefetchScalarGridSpec(
            num_scalar_prefetch=1, grid=(N_CHIP,), in_specs=[mine] * n + [blk] * n, out_specs=[blk] * n),
        out_shape=[jax.ShapeDtypeStruct((N_CHIP, half, cc), BF16)] * n,
        compiler_params=_params("parallel"),
    )(_core_index(), *gs, *gots)


def _chip_sum(pre, parts):
    n = len(parts)
    _, half, cc = parts[0].shape
    tr = half // 2
    me = 2 * lax.axis_index("x") + lax.axis_index("y")
    others = [k + (k >= me).astype(jnp.int32) for k in range(3)]
    where = jnp.stack([lax.axis_index("c"), me, *others]).astype(jnp.int32)

    def body(w_ref, *refs):
        for a in range(n):
            own, p1, p2, p3 = refs[4 * a:4 * a + 4]
            refs[4 * n + a][...] = ((own[...].astype(F32) + p1[...].astype(F32)) + p2[...].astype(F32)) + p3[...].astype(F32)

    slot = lambda s: pl.BlockSpec((None, tr, cc), lambda i, w_ref: (w_ref[s], i, 0))
    operands = []
    for a in range(n):
        operands += [pre[a], parts[a], parts[a], parts[a]]
    return pl.pallas_call(
        body, name=f"chip_sum_{half}x{cc}",
        grid_spec=pltpu.PrefetchScalarGridSpec(
            num_scalar_prefetch=1, grid=(2,),
            in_specs=[slot(1), slot(2), slot(3), slot(4)] * n,
            out_specs=[pl.BlockSpec((tr, cc), lambda i, w_ref: (2 * w_ref[0] + i, 0))] * n),
        out_shape=[jax.ShapeDtypeStruct((2 * half, cc), F32)] * n,
        compiler_params=_params("parallel"),
    )(where, *operands)


def _adamw_math(w, g, m, v):
    m = ADAM_B1 * m + (1.0 - ADAM_B1) * g
    v = ADAM_B2 * v + (1.0 - ADAM_B2) * (g * g)
    m_hat = m / (1.0 - ADAM_B1 ** ADAM_STEP)
    v_hat = v / (1.0 - ADAM_B2 ** ADAM_STEP)
    delta = -ADAM_LR * (m_hat / (jnp.sqrt(v_hat) + ADAM_EPS) + ADAM_WD * w)
    return delta, m, v


def _adamw(w, g, m, v, after=None):
    r, cc = w.shape
    tr = r // 4

    def body(w_ref, g_ref, m_ref, v_ref, go_ref, d_ref, nm_ref, nv_ref, _):
        g = g_ref[...]
        go_ref[...] = g
        d_ref[...], nm_ref[...], nv_ref[...] = _adamw_math(w_ref[...], g, m_ref[...], v_ref[...])

    blk = pl.BlockSpec((tr, cc), lambda i: (i, 0))
    return _carry(f"adamw_{r}x{cc}", body, _NoExchange(), (), (w, g, m, v), [blk] * 4, [blk] * 4,
                  [jax.ShapeDtypeStruct((r, cc), F32)] * 4, grid=(4,), semantics=("parallel",), after=after)[0]


def _pack8(rows):
    def body(*refs):
        out_ref = refs[-1]
        out_ref[...] = jnp.zeros_like(out_ref)
        for i, r in enumerate(refs[:-1]):
            out_ref[i:i + 1, :] = r[...]

    return pl.pallas_call(body, name="pack8", out_shape=jax.ShapeDtypeStruct((8, D), F32))(*rows)


def _adamw_gains(gall, ws, ms, vs):
    def body(ga_ref, *refs):
        w, m, v = refs[0:4], refs[4:8], refs[8:12]
        outs, loss_ref, total = refs[12:28], refs[28], refs[29]
        g = ga_ref[0]
        for dev in range(1, 8):
            g = g + ga_ref[dev]
        total[...] = g
        for i in range(4):
            gi = total[i:i + 1, :]
            outs[i][...] = gi
            outs[4 + i][...], outs[8 + i][...], outs[12 + i][...] = _adamw_math(w[i][...], gi, m[i][...], v[i][...])
        loss_ref[...] = total[4:5, 0:128] * (0.5 / D)

    outs = pl.pallas_call(
        body, name="adamw_gains",
        out_shape=[jax.ShapeDtypeStruct((1, D), F32)] * 16 + [jax.ShapeDtypeStruct((1, 128), F32)],
        scratch_shapes=[pltpu.VMEM((8, D), F32)],
    )(gall, *ws, *ms, *vs)
    return outs[0:4], outs[4:8], outs[8:12], outs[12:16], outs[16]


def kernel(x, positions, w_in, w_out, g_pre_mix, g_post_mix, g_pre_ffn, g_post_ffn, w_gate, w_up, w_down, loss_target, m_w_in, m_w_out, m_g_pre_mix, m_g_post_mix, m_g_pre_ffn, m_g_post_ffn, m_w_gate, m_w_up, m_w_down, v_w_in, v_w_out, v_g_pre_mix, v_g_post_mix, v_g_pre_ffn, v_g_post_ffn, v_w_gate, v_w_up, v_w_down):
    tr = lambda t: jnp.swapaxes(t, 1, 2)[0]
    shards = [w_in[0], w_out[0], tr(w_gate), tr(w_up), w_down[0]]
    moms = [m_w_in[0], m_w_out[0], tr(m_w_gate), tr(m_w_up), m_w_down[0]]
    vels = [v_w_in[0], v_w_out[0], tr(v_w_gate), tr(v_w_up), v_w_down[0]]
    xs, pos, tgt = x[0], positions.reshape(S, 1), loss_target[0]
    g1, g2, g3, g4 = g_pre_mix, g_post_mix, g_pre_ffn, g_post_ffn
    tabs = tuple(jnp.asarray(t) for t in _retention_tables())
    ifc, spread = _rotary_tables()
    ifc, spread = jnp.asarray(ifc), jnp.asarray(spread, dtype=BF16)
    bf = [s.astype(BF16) for s in shards[:2]]

    (h1, cos, sin, *ffn_bf), (win_g, wout_g) = _prepare_carrying(
        "gather_in", xs, g1, pos, ifc, spread, shards[2:], _GatherShards(bf), bf)
    bf += list(ffn_bf)
    wout_g = wout_g.reshape(D, D)
    ffn_gather = _GatherOverIci("ffn_gather", bf[2:])
    token = ffn_gather.start(win_g)
    qr, kr, rv, rg, aq, ak, av = _proj_fwd(h1, win_g, cos, sin, token)
    (o_raw, cat_r, states), _ = _ret_fwd(qr, kr, rv, rg, tabs, _NoExchange(), ())
    n_ffn = len(bf[2:])
    (att_out, lse, cat_a), ffn_gather.arrays[n_ffn:] = _att_fwd(
        aq, ak, av, _ForwardGathered(bf[2:], forward=False), ffn_gather.arrays)
    ffn_sh, ffn_lands = ffn_gather.wait(cat_a)
    (mix, x2, h3), (wg_g, wu_g, wd_g) = _mix_fwd(cat_r, cat_a, wout_g, xs, g2, g3,
                                                _ForwardGathered(bf[2:], own=False), [*ffn_sh, *ffn_lands])
    gt, up, a, sq, dy, df, dg4 = _ffn_fwd(h3, wg_g, wu_g, wd_g, x2, tgt, g4)

    dgt, dup, dx2, dmix, dg3, dg2 = _ffn_bwd_act(df, gt, up, wg_g, wu_g, wd_g, dy, x2, mix, g2, g3)
    ffn_grads = list(_ffn_bwd_w(a, df, h3, dgt, dup))
    (dret, datt, dwout), got = _mix_bwd(dmix, cat_r, cat_a, wout_g, _HalvesToSibling(ffn_grads), ffn_grads)
    ffn_sum = _SumOverIci("ffn_sum", _pair_sum(ffn_grads, got))
    token = ffn_sum.start(datt)
    (dq_att, dk_att, dv_att), _ = _att_bwd(aq, ak, av, datt, att_out, lse, _NoExchange(), (), token)
    (dqr, dkr, drv, drg), _ = _ret_bwd(qr, kr, rv, rg, o_raw, states, dret, tabs, _NoExchange(), (), token)
    dproj = _rot_bwd(cos, sin, dqr, dkr, drv, drg, dq_att, dk_att, dv_att)
    sums = _chip_sum(*ffn_sum.wait(dproj))
    dwin, ffn_full = _win_bwd_w(h1, dproj, _ShareHalves(sums), sums)
    in_grads = [dwin, dwout.reshape(N_CHIP, WOUT_R, D)]

    got = _exchange_alone("halves_to_sibling", _HalvesToSibling(in_grads), in_grads)
    in_sum = _SumOverIci("in_sum", [*_pair_sum(in_grads[:1], got[:1]), *_pair_sum(in_grads[1:], got[1:])])
    token = in_sum.start(dproj)
    dx, dg1 = _in_bwd(dproj, win_g, xs, dx2, g1, token)
    ffn_upd = [_adamw(shards[2 + i], ffn_full[o], moms[2 + i], vels[2 + i], token)
               for i, o in enumerate((1, 2, 0))]
    pre, parts = in_sum.wait(ffn_upd[2][0])
    sums = [*_chip_sum(pre[:1], parts[:1]), *_chip_sum(pre[1:], parts[1:])]
    gblock = _pack8([dg1, dg2, dg3, dg4, sq])
    *in_full, gall = _exchange_alone("share_rest", _Both(_ShareHalves(sums), _GatherBlocks(gblock)), [*sums, gblock])
    upd = [_adamw(w, g, m, v) for w, g, m, v in zip(shards[:2], in_full, moms[:2], vels[:2])] + ffn_upd
    gg, gd, gm, gv, loss_row = _adamw_gains(gall, [g1, g2, g3, g4],
                                            [m_g_pre_mix, m_g_post_mix, m_g_pre_ffn, m_g_post_ffn],
                                            [v_g_pre_mix, v_g_post_mix, v_g_pre_ffn, v_g_post_ffn])

    def order(mats, vecs):
        back = lambda t: jnp.swapaxes(t[None], 1, 2)
        return [mats[0][None], mats[1][None], *vecs, back(mats[2]), back(mats[3]), mats[4][None]]

    return (loss_row[0, 0], dx[None],
            *order([u[0] for u in upd], gg),
            *order([u[1] for u in upd], gd),
            *order([u[2] for u in upd], gm),
            *order([u[3] for u in upd], gv))
```

```python
import numpy as np
import jax
import jax.numpy as jnp
from jax import lax
from jax.experimental import pallas as pl
from jax.experimental.pallas import tpu as pltpu

F32, BF16 = jnp.float32, jnp.bfloat16
MESH = pl.DeviceIdType.MESH

S = 2048
D = 1024
PW = 3072
N_CHIP = 4
WIN_C = PW // N_CHIP
DFF = 2816
FF_C = DFF // N_CHIP
WOUT_R = D // N_CHIP
RMS_EPS = 1e-6
GN_EPS = 1e-5
RET_C = 128
RET_SCALE = 32 ** -0.5
ATT_BLK = 128
ATT_SCALE = 64 ** -0.5
PATTERN_DILATIONS = (1, 4, 16)
NEG = -1e30
VMEM_LIMIT = 56 * 1024 * 1024

ADAM_LR, ADAM_B1, ADAM_B2, ADAM_EPS, ADAM_WD, ADAM_STEP = 0.001, 0.9, 0.999, 1e-08, 0.01, 10


def _params(*sem):
    return pltpu.CompilerParams(dimension_semantics=sem, vmem_limit_bytes=VMEM_LIMIT)


def _nt(a, b):
    return lax.dot_general(a, b, (((1,), (1,)), ((), ())), preferred_element_type=F32)


def _tn(a, b):
    return lax.dot_general(a, b, (((0,), (0,)), ((), ())), preferred_element_type=F32)


def _nn(a, b):
    return jnp.dot(a, b, preferred_element_type=F32)


def _rstd(v):
    return lax.rsqrt(jnp.mean(v * v, axis=-1, keepdims=True) + RMS_EPS)


def _sigmoid(v):
    return 1.0 / (1.0 + jnp.exp(-v))


def _rows(i, t):
    return pl.ds(pl.multiple_of(i * t, t), t)


def _retention_tables():
    h = np.arange(8, dtype=np.float32)
    log_g = np.log1p(-np.exp2(-5.0 - h)).astype(np.float32)
    idx = np.arange(RET_C, dtype=np.float32)
    diff = idx[:, None] - idx[None, :]
    dtab = np.where(diff >= 0, np.exp(log_g[:, None, None] * np.maximum(diff, 0.0)), 0.0).astype(np.float32)
    dtab = dtab.reshape(8 * RET_C, RET_C)
    lane_head = np.arange(256) // 32
    a_tab = np.exp(log_g[lane_head][None, :] * (idx + 1.0)[:, None]).astype(np.float32)
    b_tab = np.exp(log_g[lane_head][None, :] * (RET_C - 1.0 - idx)[:, None]).astype(np.float32)
    lam = np.exp(log_g[lane_head] * RET_C).astype(np.float32)[:, None]
    bd = (lane_head[:, None] == (np.arange(512) // 64)[None, :]).astype(np.float32)
    return dtab, a_tab, b_tab, lam, bd


def _rotary_tables():
    inv_r = (1.0 / (np.float32(10000.0) ** np.linspace(0.0, 1.0, 16, dtype=np.float32))).astype(np.float32)
    inv_a = (np.float32(500000.0) ** (-np.arange(0, 16, 2, dtype=np.float32) / np.float32(16))).astype(np.float32)
    ifc = np.zeros((1, 128), np.float32)
    ifc[0, 0:16], ifc[0, 16:24] = inv_r, inv_a
    spread = np.zeros((128, 768), np.float32)
    for lane in range(256):
        spread[(lane % 32) % 16, lane] = 1.0
    for lane in range(512):
        d = lane % 64
        spread[16 + d % 8 if d < 16 else 24, 256 + lane] = 1.0
    return ifc, spread


def _rot_halves(tm):
    lo_r = (lax.broadcasted_iota(jnp.int32, (tm, 256), 1) % 32) < 16
    lo_a = (lax.broadcasted_iota(jnp.int32, (tm, 512), 1) % 64) < 8
    return lo_r, lo_a


def _spread_exact(t, e):
    hi = t.astype(BF16)
    r1 = t - hi.astype(F32)
    mid = r1.astype(BF16)
    lo = (r1 - mid.astype(F32)).astype(BF16)
    return _nn(hi, e) + _nn(mid, e) + _nn(lo, e)


def _proj_fwd(h1, win_g, cos, sin, after):
    tm = 256

    def body(h_ref, w_ref, cos_ref, sin_ref, qr_ref, kr_ref, rv_ref, rg_ref, aq_ref, ak_ref, av_ref, p_ref, _):
        h = h_ref[...]
        for k in range(N_CHIP):
            p_ref[:, k * WIN_C:(k + 1) * WIN_C] = _nn(h, w_ref[k])
        cr, ca, sr, sa = cos_ref[:, 0:256], cos_ref[:, 256:768], sin_ref[:, 0:256], sin_ref[:, 256:768]
        lo_r, lo_a = _rot_halves(tm)

        def rot_r(v):
            return v * cr + sr * jnp.where(lo_r, -pltpu.roll(v, 240, 1), pltpu.roll(v, 16, 1))

        def rot_a(v):
            return v * ca + sa * jnp.where(lo_a, -pltpu.roll(v, 504, 1), pltpu.roll(v, 8, 1))

        qr_ref[...] = rot_r(p_ref[:, 0:256]).astype(BF16)
        kr_ref[...] = (rot_r(p_ref[:, 256:512]) * RET_SCALE).astype(BF16)
        rv_ref[...] = p_ref[:, 512:1024].astype(BF16)
        rg_ref[...] = p_ref[:, 1024:1536]
        aq, ak = rot_a(p_ref[:, 1536:2048]), rot_a(p_ref[:, 2048:2560])
        for j in range(4):
            aq_ref[j] = aq[:, 128 * j:128 * j + 128]
            ak_ref[j] = ak[:, 128 * j:128 * j + 128]
            av_ref[j] = p_ref[:, 2560 + 128 * j:2560 + 128 * j + 128]

    row = lambda w: pl.BlockSpec((tm, w), lambda i: (i, 0))
    slab = pl.BlockSpec((4, tm, 128), lambda i: (0, i, 0))
    return _carry(
        "proj_fwd", body, _NoExchange(), (), (h1, win_g, cos, sin),
        [row(D), pl.BlockSpec((N_CHIP, D, WIN_C), lambda i: (0, 0, 0)), row(768), row(768)],
        [row(256), row(256), row(512), row(512), slab, slab, slab],
        [jax.ShapeDtypeStruct((S, w), BF16) for w in (256, 256, 512)]
        + [jax.ShapeDtypeStruct((S, 512), F32)] + [jax.ShapeDtypeStruct((4, S, 128), F32)] * 3,
        scratch_shapes=[pltpu.VMEM((tm, PW), F32)], grid=(S // tm,), semantics=("parallel",), after=after)[0]


def _seg_mean(v):
    lo = lax.broadcasted_iota(jnp.int32, v.shape, 1) < 64
    s_lo = jnp.sum(jnp.where(lo, v, 0.0), axis=-1, keepdims=True)
    s_hi = jnp.sum(jnp.where(lo, 0.0, v), axis=-1, keepdims=True)
    return jnp.where(lo, s_lo, s_hi) * (1.0 / 64.0)


def _ret_fwd(qr, kr, rv, proj, tabs, exchange, exchange_args):
    C = RET_C
    dtab, a_tab, b_tab, lam, bd = tabs

    def body(q_ref, k_ref, v_ref, g_ref, dt_ref, a_ref, b_ref, lam_ref, bd_ref, o_ref, cat_ref, st_ref, R, exch):
        @pl.when(pl.program_id(0) == 0)
        def _():
            exch.start()
            R[...] = jnp.zeros_like(R)

        @pl.when(pl.program_id(0) == S // C // 2)
        def _():
            exch.middle()

        q, k, v = q_ref[...], k_ref[...], v_ref[...]
        lane_head = lax.broadcasted_iota(jnp.int32, (C, 256), 1) // 32
        col_head = lax.broadcasted_iota(jnp.int32, (C, 256), 1) // 64
        rb = R[...].astype(BF16)
        st_ref[...] = rb
        qa = (q.astype(F32) * a_ref[...]).astype(BF16)
        cross = _nn(qa, rb)
        p = (_nt(_stack_heads(q, lane_head, n=8), k) * dt_ref[...]).astype(BF16)
        og = [cross[:, 256 * g:256 * g + 256]
              + _unstack_heads(_nn(p[4 * C * g:4 * C * (g + 1)], v[:, 256 * g:256 * g + 256]), col_head)
              for g in range(2)]
        kb = (k.astype(F32) * b_ref[...]).astype(BF16)
        R[...] = R[...] * lam_ref[...] + _tn(kb, v) * bd_ref[...]
        o_ref[:, 0:256] = og[0]
        o_ref[:, 256:512] = og[1]
        for j in range(4):
            oj = og[j // 2][:, 128 * (j % 2):128 * (j % 2) + 128]
            xc = oj - _seg_mean(oj)
            rn = xc * lax.rsqrt(_seg_mean(xc * xc) + GN_EPS)
            gj = g_ref[:, 128 * j:128 * j + 128]
            cat_ref[:, 128 * j:128 * j + 128] = (rn * (gj * _sigmoid(gj))).astype(BF16)

        @pl.when(pl.program_id(0) == S // C - 1)
        def _():
            exch.finish()

    row = lambda w: pl.BlockSpec((C, w), lambda n: (n, 0))
    full = lambda a: pl.BlockSpec(a.shape, lambda n: (0,) * a.ndim)
    return _carry(
        "ret_fwd", body, exchange, exchange_args, (qr, kr, rv, proj, dtab, a_tab, b_tab, lam, bd),
        [row(256), row(256), row(512), row(512),
         full(dtab), full(a_tab), full(b_tab), full(lam), full(bd)],
        [row(512), row(512), pl.BlockSpec((None, 256, 512), lambda n: (n, 0, 0))],
        [jax.ShapeDtypeStruct((S, 512), F32), jax.ShapeDtypeStruct((S, 512), BF16),
         jax.ShapeDtypeStruct((S // C, 256, 512), BF16)],
        scratch_shapes=[pltpu.VMEM((256, 512), F32)], grid=(S // C,), semantics=("arbitrary",))


def _stack_heads(v, lane_head, fill=0.0, n=4):
    return jnp.concatenate([jnp.where(lane_head == h, v, jnp.full_like(v, fill)) for h in range(n)], axis=0)


def _unstack_heads(v, lane_head, n=4):
    out = v[0:ATT_BLK]
    for h in range(1, n):
        out = jnp.where(lane_head == h, v[h * ATT_BLK:(h + 1) * ATT_BLK], out)
    return out


def _att_bias(has_prev):
    nk = 2 * ATT_BLK if has_prev else ATT_BLK
    a = lax.broadcasted_iota(jnp.int32, (4 * ATT_BLK, nk), 0) % ATT_BLK
    kk = lax.broadcasted_iota(jnp.int32, (4 * ATT_BLK, nk), 1)
    if not has_prev:
        return None, jnp.where((a - kk) >= 0, 0.0, NEG)
    dist = ATT_BLK + a - kk
    inside = (dist >= 0) & (dist <= ATT_BLK)
    return jnp.where(inside, 0.0, NEG), jnp.where(inside & (kk >= ATT_BLK), 0.0, NEG)


def _class_rows(ib, r, d):
    if d == 1:
        return pl.ds(pl.multiple_of(ib * ATT_BLK, ATT_BLK), ATT_BLK)
    return pl.ds(ib * ATT_BLK * d + r, ATT_BLK, stride=d)


def _slab_pair(ref, g, rows):
    return jnp.concatenate([ref[2 * g, rows, :], ref[2 * g + 1, rows, :]], axis=1)


def _att_blocks(d):
    nb = S // d // ATT_BLK
    return nb, nb > 1


def _att_fwd(aq, ak, av, exchange, exchange_args):
    def body(q_ref, k_ref, v_ref, o_ref, l_ref, cat_ref, xc):
        xc.start()
        lane_head = lax.broadcasted_iota(jnp.int32, (ATT_BLK, 256), 1) // 64
        for pi, d in enumerate(PATTERN_DILATIONS):
            if pi == len(PATTERN_DILATIONS) - 1:
                xc.middle()
            nb, has_prev = _att_blocks(d)
            bias_rest, bias_first = _att_bias(has_prev)

            def block(b, carry, pi=pi, d=d, nb=nb, has_prev=has_prev, bias_rest=bias_rest, bias_first=bias_first):
                r, ib = b // nb, b % nb
                rows = _class_rows(ib, r, d)
                prow = _class_rows(jnp.maximum(ib - 1, 0), r, d)
                bias = jnp.where(ib == 0, bias_first, bias_rest) if has_prev else bias_first
                for g in range(2):
                    qg = _slab_pair(q_ref, g, rows).astype(BF16)
                    kg = _slab_pair(k_ref, g, rows)
                    vg = _slab_pair(v_ref, g, rows)
                    if has_prev:
                        kg = jnp.concatenate([_slab_pair(k_ref, g, prow), kg], axis=0)
                        vg = jnp.concatenate([_slab_pair(v_ref, g, prow), vg], axis=0)
                    kg, vg = kg.astype(BF16), vg.astype(BF16)
                    s = _nt(_stack_heads(qg, lane_head), kg) * ATT_SCALE + bias
                    m = jnp.max(s, axis=-1, keepdims=True)
                    p = jnp.exp(s - m)
                    den = jnp.sum(p, axis=-1, keepdims=True)
                    og = _unstack_heads(_nn(p.astype(BF16), vg) / den, lane_head)
                    lg = _unstack_heads(jnp.broadcast_to(m + jnp.log(den), (4 * ATT_BLK, 256)), lane_head)
                    for jj in range(2):
                        j = 2 * g + jj
                        o_new, l_new = og[:, 128 * jj:128 * jj + 128], lg[:, 128 * jj:128 * jj + 128]
                        if pi > 0:
                            o_old, l_old = o_ref[j, rows, :], l_ref[j, rows, :]
                            mx = jnp.maximum(l_old, l_new)
                            ea, eb = jnp.exp(l_old - mx), jnp.exp(l_new - mx)
                            den = ea + eb
                            o_new = (ea * o_old + eb * o_new) / den
                            l_new = mx + jnp.log(den)
                        o_ref[j, rows, :] = o_new
                        l_ref[j, rows, :] = l_new
                return carry

            lax.fori_loop(0, S // ATT_BLK, block, 0)

        def to_cat(i, carry):
            rows = _rows(i, 256)
            for j in range(4):
                cat_ref[rows, 128 * j:128 * j + 128] = o_ref[j, rows, :].astype(BF16)
            return carry

        lax.fori_loop(0, S // 256, to_cat, 0)
        xc.finish()

    slab = jax.ShapeDtypeStruct((4, S, 128), F32)
    return _carry("att_fwd", body, exchange, exchange_args, (aq, ak, av), [VMEM] * 3, [VMEM] * 3,
                  [slab, slab, jax.ShapeDtypeStruct((S, 512), BF16)])


def _mix_fwd(cat_r, cat_a, wout, x, g2, g3, exchange, exchange_args):
    tm = 512

    def body(cr_ref, ca_ref, w_ref, x_ref, g2_ref, g3_ref, mix_ref, x2_ref, h3_ref, xc):
        @pl.when(pl.program_id(0) == 0)
        def _():
            xc.start()

        mix = _nn(cr_ref[...], w_ref[0:512, :]) + _nn(ca_ref[...], w_ref[512:1024, :])
        mix_ref[...] = mix
        x2 = x_ref[...] + mix * _rstd(mix) * g2_ref[...]
        x2_ref[...] = x2
        h3_ref[...] = (x2 * _rstd(x2) * g3_ref[...]).astype(BF16)

        @pl.when(pl.program_id(0) == S // tm - 1)
        def _():
            xc.middle()
            xc.finish()

    row = lambda w: pl.BlockSpec((tm, w), lambda i: (i, 0))
    vec = pl.BlockSpec((1, D), lambda i: (0, 0))
    return _carry("mix_fwd", body, exchange, exchange_args, (cat_r, cat_a, wout, x, g2, g3),
                  [row(512), row(512), pl.BlockSpec((D, D), lambda i: (0, 0)), row(D), vec, vec],
                  [row(D), row(D), row(D)],
                  [jax.ShapeDtypeStruct((S, D), F32), jax.ShapeDtypeStruct((S, D), F32),
                   jax.ShapeDtypeStruct((S, D), BF16)],
                  grid=(S // tm,), semantics=("arbitrary",))


def _ffn_fwd(h3, wg, wu, wd, x2, tgt, g4):
    tm = 512
    last = N_CHIP - 1

    def body(h_ref, wg_ref, wu_ref, wd_ref, x2_ref, t_ref, g_ref,
             gt_ref, up_ref, a_ref, loss_ref, dy_ref, df_ref, dg_ref, f_ref):
        k, i = pl.program_id(0), pl.program_id(1)
        h = h_ref[...]
        gt = _nt(h, wg_ref[...])
        up = _nt(h, wu_ref[...])
        gt_ref[...] = gt.astype(BF16)
        up_ref[...] = up.astype(BF16)
        a = (gt * _sigmoid(gt) * up).astype(BF16)
        a_ref[...] = a
        part = _nn(a, wd_ref[...])
        rows = _rows(i, tm)

        @pl.when(k == 0)
        def _():
            f_ref[rows, :] = part

        @pl.when((k > 0) & (k < last))
        def _():
            f_ref[rows, :] = f_ref[rows, :] + part

        @pl.when((k == last) & (i == 0))
        def _():
            loss_ref[...] = jnp.zeros_like(loss_ref)
            dg_ref[...] = jnp.zeros_like(dg_ref)

        @pl.when(k == last)
        def _():
            fv = f_ref[rows, :] + part
            r = _rstd(fv)
            fn = fv * r
            e = x2_ref[...] + fn * g_ref[...] - t_ref[...]
            loss_ref[...] = loss_ref[...] + jnp.sum(jnp.sum(e * e, axis=-1, keepdims=True), axis=0, keepdims=True)
            dy = e * (1.0 / D)
            dy_ref[...] = dy
            dg_ref[...] = dg_ref[...] + jnp.sum(dy * fn, axis=0, keepdims=True)
            t = dy * g_ref[...]
            df_ref[...] = (r * (t - fn * jnp.mean(t * fn, axis=-1, keepdims=True))).astype(BF16)

    wrow = pl.BlockSpec((None, FF_C, D), lambda k, i: (k, 0, 0))
    act = pl.BlockSpec((None, tm, FF_C), lambda k, i: (k, i, 0))
    late = pl.BlockSpec((tm, D), lambda k, i: (jnp.where(k == last, i, 0), 0))
    vec = pl.BlockSpec((1, D), lambda k, i: (0, 0))
    return pl.pallas_call(
        body, grid=(N_CHIP, S // tm), name="ffn_fwd",
        in_specs=[pl.BlockSpec((tm, D), lambda k, i: (i, 0)), wrow, wrow, wrow, late, late, vec],
        out_specs=[act, act, act, vec, late, late, vec],
        out_shape=[jax.ShapeDtypeStruct((N_CHIP, S, FF_C), BF16)] * 3
                  + [jax.ShapeDtypeStruct((1, D), F32), jax.ShapeDtypeStruct((S, D), F32),
                     jax.ShapeDtypeStruct((S, D), BF16), jax.ShapeDtypeStruct((1, D), F32)],
        scratch_shapes=[pltpu.VMEM((S, D), F32)],
        compiler_params=_params("arbitrary", "arbitrary"),
    )(h3, wg, wu, wd, x2, tgt, g4)


def _ffn_bwd_act(df, gt, up, wg, wu, wd, dy, x2, mix, g2, g3):
    tm, sub = 512, 256
    last = N_CHIP - 1

    def body(df_ref, gt_ref, up_ref, wg_ref, wu_ref, wd_ref, dy_ref, x2_ref, mix_ref, g2_ref, g3_ref,
             dgt_ref, dup_ref, dx2_ref, dmix_ref, dg3_ref, dg2_ref, dh_ref):
        k, i = pl.program_id(0), pl.program_id(1)
        parts = []
        for s in range(tm // sub):
            rows = slice(s * sub, (s + 1) * sub)
            da = _nt(df_ref[rows, :], wd_ref[...])
            gt, up = gt_ref[rows, :].astype(F32), up_ref[rows, :].astype(F32)
            sg = _sigmoid(gt)
            dup = (da * gt * sg).astype(BF16)
            dgt = (da * up * (sg * (1.0 + gt * (1.0 - sg)))).astype(BF16)
            dup_ref[rows, :] = dup
            dgt_ref[rows, :] = dgt
            parts.append(_nn(dgt, wg_ref[...]) + _nn(dup, wu_ref[...]))
        part = jnp.concatenate(parts, axis=0)
        rows = _rows(i, tm)

        @pl.when(k == 0)
        def _():
            dh_ref[rows, :] = part

        @pl.when((k > 0) & (k < last))
        def _():
            dh_ref[rows, :] = dh_ref[rows, :] + part

        @pl.when((k == last) & (i == 0))
        def _():
            dg3_ref[...] = jnp.zeros_like(dg3_ref)
            dg2_ref[...] = jnp.zeros_like(dg2_ref)

        @pl.when(k == last)
        def _():
            dh = dh_ref[rows, :] + part
            x2 = x2_ref[...]
            r3 = _rstd(x2)
            xn = x2 * r3
            dg3_ref[...] = dg3_ref[...] + jnp.sum(dh * xn, axis=0, keepdims=True)
            t = dh * g3_ref[...]
            dx2 = dy_ref[...] + r3 * (t - xn * jnp.mean(t * xn, axis=-1, keepdims=True))
            dx2_ref[...] = dx2
            mix = mix_ref[...]
            r2 = _rstd(mix)
            mn = mix * r2
            dg2_ref[...] = dg2_ref[...] + jnp.sum(dx2 * mn, axis=0, keepdims=True)
            u = dx2 * g2_ref[...]
            dmix_ref[...] = (r2 * (u - mn * jnp.mean(u * mn, axis=-1, keepdims=True))).astype(BF16)

    wrow = pl.BlockSpec((None, FF_C, D), lambda k, i: (k, 0, 0))
    act = pl.BlockSpec((None, tm, FF_C), lambda k, i: (k, i, 0))
    row = pl.BlockSpec((tm, D), lambda k, i: (i, 0))
    late = pl.BlockSpec((tm, D), lambda k, i: (jnp.where(k == last, i, 0), 0))
    vec = pl.BlockSpec((1, D), lambda k, i: (0, 0))
    return pl.pallas_call(
        body, grid=(N_CHIP, S // tm), name="ffn_bwd_act",
        in_specs=[row, act, act, wrow, wrow, wrow, late, late, late, vec, vec],
        out_specs=[act, act, late, late, vec, vec],
        out_shape=[jax.ShapeDtypeStruct((N_CHIP, S, FF_C), BF16), jax.ShapeDtypeStruct((N_CHIP, S, FF_C), BF16),
                   jax.ShapeDtypeStruct((S, D), F32), jax.ShapeDtypeStruct((S, D), BF16),
                   jax.ShapeDtypeStruct((1, D), F32), jax.ShapeDtypeStruct((1, D), F32)],
        scratch_shapes=[pltpu.VMEM((S, D), F32)],
        compiler_params=_params("arbitrary", "arbitrary"),
    )(df, gt, up, wg, wu, wd, dy, x2, mix, g2, g3)


def _ffn_bwd_w(a, df, h3, dgt, dup):
    tm = 1024
    assert S // tm == 2

    def body(a_ref, df_ref, h_ref, dgt_ref, dup_ref, dwd_ref, dwg_ref, dwu_ref, acc_d, acc_g, acc_u):
        i = pl.program_id(1)
        h = h_ref[...]
        parts = (_tn(a_ref[...], df_ref[...]), _tn(dgt_ref[...], h), _tn(dup_ref[...], h))

        @pl.when(i == 0)
        def _():
            for acc, part in zip((acc_d, acc_g, acc_u), parts):
                acc[...] = part

        @pl.when(i == S // tm - 1)
        def _():
            for out, acc, part in zip((dwd_ref, dwg_ref, dwu_ref), (acc_d, acc_g, acc_u), parts):
                out[...] = (acc[...] + part).astype(BF16)

    act = pl.BlockSpec((None, tm, FF_C), lambda k, i: (k, i, 0))
    row = pl.BlockSpec((tm, D), lambda k, i: (i, 0))
    wrow = pl.BlockSpec((None, FF_C, D), lambda k, i: (k, 0, 0))
    return pl.pallas_call(
        body, grid=(N_CHIP, S // tm), name="ffn_bwd_w",
        in_specs=[act, row, row, act, act],
        out_specs=[wrow, wrow, wrow],
        out_shape=[jax.ShapeDtypeStruct((N_CHIP, FF_C, D), BF16)] * 3,
        scratch_shapes=[pltpu.VMEM((FF_C, D), F32)] * 3,
        compiler_params=_params("parallel", "arbitrary"),
    )(a, df, h3, dgt, dup)


def _mix_bwd(dmix, cat_r, cat_a, wout, exchange, exchange_args):
    tm = 1024

    def body(dm_ref, cr_ref, ca_ref, w_ref, dret_ref, datt_ref, dw_ref, acc, xc):
        i = pl.program_id(0)

        @pl.when(i == 0)
        def _():
            xc.start()
            acc[...] = jnp.zeros_like(acc)

        dm = dm_ref[...]
        dret_ref[...] = _nt(dm, w_ref[0:512, :])
        datt = _nt(dm, w_ref[512:1024, :])
        for j in range(4):
            datt_ref[j] = datt[:, 128 * j:128 * j + 128]
        acc[0:512, :] += _tn(cr_ref[...], dm)
        acc[512:1024, :] += _tn(ca_ref[...], dm)

        @pl.when(i == S // tm - 1)
        def _():
            dw_ref[...] = acc[...].astype(BF16)
            xc.middle()
            xc.finish()

    row = lambda w: pl.BlockSpec((tm, w), lambda i: (i, 0))
    full = pl.BlockSpec((D, D), lambda i: (0, 0))
    return _carry("mix_bwd", body, exchange, exchange_args, (dmix, cat_r, cat_a, wout),
                  [row(D), row(512), row(512), full],
                  [row(512), pl.BlockSpec((4, tm, 128), lambda i: (0, i, 0)), full],
                  [jax.ShapeDtypeStruct((S, 512), F32), jax.ShapeDtypeStruct((4, S, 128), F32),
                   jax.ShapeDtypeStruct((D, D), BF16)],
                  scratch_shapes=[pltpu.VMEM((D, D), F32)], grid=(S // tm,), semantics=("arbitrary",))


def _att_bwd(aq, ak, av, datt, att_out, lse, exchange, exchange_args, after=None):
    def body(q_ref, k_ref, v_ref, do_ref, out_ref, l_ref, dq_ref, dk_ref, dv_ref, xc):
        xc.start()

        def clear(i, carry):
            rows = _rows(i, 256)
            for ref in (dq_ref, dk_ref, dv_ref):
                for j in range(4):
                    ref[j, rows, :] = jnp.zeros((256, 128), F32)
            return carry

        lax.fori_loop(0, S // 256, clear, 0)
        lane_head = lax.broadcasted_iota(jnp.int32, (ATT_BLK, 256), 1) // 64
        for d in PATTERN_DILATIONS:
            nb, has_prev = _att_blocks(d)
            bias_rest, bias_first = _att_bias(has_prev)

            def block(b, carry, d=d, nb=nb, has_prev=has_prev, bias_rest=bias_rest, bias_first=bias_first):
                r, ib = b // nb, b % nb
                rows = _class_rows(ib, r, d)
                prow = _class_rows(jnp.maximum(ib - 1, 0), r, d)
                bias = jnp.where(ib == 0, bias_first, bias_rest) if has_prev else bias_first
                for g in range(2):
                    qg = _slab_pair(q_ref, g, rows).astype(BF16)
                    kg = _slab_pair(k_ref, g, rows)
                    vg = _slab_pair(v_ref, g, rows)
                    if has_prev:
                        kg = jnp.concatenate([_slab_pair(k_ref, g, prow), kg], axis=0)
                        vg = jnp.concatenate([_slab_pair(v_ref, g, prow), vg], axis=0)
                    kg, vg = kg.astype(BF16), vg.astype(BF16)
                    dog = _slab_pair(do_ref, g, rows)
                    outg = _slab_pair(out_ref, g, rows)
                    lg = _slab_pair(l_ref, g, rows)
                    qs = _stack_heads(qg, lane_head)
                    dos = _stack_heads(dog, lane_head)
                    delta = jnp.sum(dos * jnp.concatenate([outg] * 4, axis=0), axis=-1, keepdims=True)
                    lh = jnp.max(_stack_heads(lg, lane_head, NEG), axis=-1, keepdims=True)
                    s = _nt(qs, kg) * ATT_SCALE + bias
                    p = jnp.exp(s - lh)
                    dosb = dos.astype(BF16)
                    ds = (p * (_nt(dosb, vg) - delta) * ATT_SCALE).astype(BF16)
                    dq = _unstack_heads(_nn(ds, kg), lane_head)
                    dk = _tn(ds, qs)
                    dv = _tn(p.astype(BF16), dosb)
                    for jj in range(2):
                        j, sl = 2 * g + jj, slice(128 * jj, 128 * jj + 128)
                        dq_ref[j, rows, :] += dq[:, sl]
                        if has_prev:
                            dk_ref[j, prow, :] += dk[0:ATT_BLK, sl]
                            dv_ref[j, prow, :] += dv[0:ATT_BLK, sl]
                            dk_ref[j, rows, :] += dk[ATT_BLK:2 * ATT_BLK, sl]
                            dv_ref[j, rows, :] += dv[ATT_BLK:2 * ATT_BLK, sl]
                        else:
                            dk_ref[j, rows, :] += dk[:, sl]
                            dv_ref[j, rows, :] += dv[:, sl]
                return carry

            lax.fori_loop(0, S // ATT_BLK, block, 0)
        xc.middle()
        xc.finish()

    slab = jax.ShapeDtypeStruct((4, S, 128), F32)
    return _carry("att_bwd", body, exchange, exchange_args, (aq, ak, av, datt, att_out, lse), [VMEM] * 6, [VMEM] * 3,
                  [slab, slab, slab], after=after)


def _ret_bwd(qr, kr, rv, proj, o_raw, states, dret, tabs, exchange, exchange_args, after=None):
    C = RET_C
    nc = S // C
    dtab, a_tab, b_tab, lam, bd = tabs

    def body(q_ref, k_ref, v_ref, g_ref, o_ref, st_ref, dr_ref, dt_ref, a_ref, b_ref, lam_ref, bd_ref,
             dq_ref, dk_ref, dv_ref, dg_ref, dR, exch):
        @pl.when(pl.program_id(0) == 0)
        def _():
            exch.start()
            dR[...] = jnp.zeros_like(dR)

        q, k, v = q_ref[...], k_ref[...], v_ref[...]
        lane_head = lax.broadcasted_iota(jnp.int32, (C, 256), 1) // 32
        col_head = lax.broadcasted_iota(jnp.int32, (C, 256), 1) // 64
        dos = []
        for j in range(4):
            sl = slice(128 * j, 128 * j + 128)
            oj = o_ref[:, sl]
            xc = oj - _seg_mean(oj)
            rs = lax.rsqrt(_seg_mean(xc * xc) + GN_EPS)
            rn = xc * rs
            gj = g_ref[:, sl]
            sg = _sigmoid(gj)
            dret = dr_ref[:, sl]
            dg_ref[:, sl] = dret * rn * (sg * (1.0 + gj * (1.0 - sg)))
            drn = dret * (gj * sg)
            dos.append(rs * (drn - _seg_mean(drn) - rn * _seg_mean(drn * rn)))
        do = [jnp.concatenate(dos[0:2], axis=1), jnp.concatenate(dos[2:4], axis=1)]
        do8 = jnp.concatenate(do, axis=1).astype(BF16)
        drb = dR[...].astype(BF16)
        rb = st_ref[...]
        dq = _nt(do8, rb) * a_ref[...]
        dk = _nt(v, drb) * b_ref[...]
        kb = (k.astype(F32) * b_ref[...]).astype(BF16)
        dvall = _nn(kb, drb)
        qs = _stack_heads(q, lane_head, n=8)
        dec = dt_ref[...]
        p = (_nt(qs, k) * dec).astype(BF16)
        dos = [_stack_heads(do[g], col_head).astype(BF16) for g in range(2)]
        dp = jnp.concatenate([_nt(dos[g], v[:, 256 * g:256 * g + 256]) for g in range(2)], axis=0)
        ds = (dp * dec).astype(BF16)
        dq = dq + _unstack_heads(_nn(ds, k), lane_head, n=8)
        dk = dk + _tn(ds, qs)
        dv = [dvall[:, 256 * g:256 * g + 256] + _tn(p[4 * C * g:4 * C * (g + 1)], dos[g]) for g in range(2)]
        qa = (q.astype(F32) * a_ref[...]).astype(BF16)
        dR[...] = dR[...] * lam_ref[...] + _tn(qa, do8) * bd_ref[...]
        dq_ref[...] = dq
        dk_ref[...] = dk
        dv_ref[:, 0:256] = dv[0]
        dv_ref[:, 256:512] = dv[1]

        @pl.when(pl.program_id(0) == nc - 1)
        def _():
            exch.middle()
            exch.finish()

    rev = lambda w: pl.BlockSpec((C, w), lambda n: (nc - 1 - n, 0))
    full = lambda a: pl.BlockSpec(a.shape, lambda n: (0,) * a.ndim)
    return _carry(
        "ret_bwd", body, exchange, exchange_args, (qr, kr, rv, proj, o_raw, states, dret, dtab, a_tab, b_tab, lam, bd),
        [rev(256), rev(256), rev(512), rev(512), rev(512),
         pl.BlockSpec((None, 256, 512), lambda n: (nc - 1 - n, 0, 0)), rev(512),
         full(dtab), full(a_tab), full(b_tab), full(lam), full(bd)],
        [rev(256), rev(256), rev(512), rev(512)],
        [jax.ShapeDtypeStruct((S, 256), F32), jax.ShapeDtypeStruct((S, 256), F32),
         jax.ShapeDtypeStruct((S, 512), F32), jax.ShapeDtypeStruct((S, 512), F32)],
        scratch_shapes=[pltpu.VMEM((256, 512), F32)], grid=(nc,), semantics=("arbitrary",), after=after)


def _rot_bwd(cos, sin, dqr, dkr, drv, drg, dq_att, dk_att, dv_att):
    tm = 256

    def body(cos_ref, sin_ref, dqr_ref, dkr_ref, drv_ref, drg_ref, dqa_ref, dka_ref, dva_ref, dp_ref):
        cr, ca, sr, sa = cos_ref[:, 0:256], cos_ref[:, 256:768], sin_ref[:, 0:256], sin_ref[:, 256:768]
        lo_r, lo_a = _rot_halves(tm)

        def unrot_r(g):
            gs = g * sr
            return g * cr + pltpu.roll(jnp.where(lo_r, -gs, 0.0), 16, 1) + pltpu.roll(jnp.where(lo_r, 0.0, gs), 240, 1)

        def unrot_a(g):
            gs = g * sa
            return g * ca + pltpu.roll(jnp.where(lo_a, -gs, 0.0), 8, 1) + pltpu.roll(jnp.where(lo_a, 0.0, gs), 504, 1)

        def wide(ref):
            return jnp.concatenate([ref[j] for j in range(4)], axis=1)

        dp_ref[:, 0:256] = unrot_r(dqr_ref[...]).astype(BF16)
        dp_ref[:, 256:512] = unrot_r(dkr_ref[...] * RET_SCALE).astype(BF16)
        dp_ref[:, 512:1024] = drv_ref[...].astype(BF16)
        dp_ref[:, 1024:1536] = drg_ref[...].astype(BF16)
        dp_ref[:, 1536:2048] = unrot_a(wide(dqa_ref)).astype(BF16)
        dp_ref[:, 2048:2560] = unrot_a(wide(dka_ref)).astype(BF16)
        dp_ref[:, 2560:3072] = wide(dva_ref).astype(BF16)

    row = lambda w: pl.BlockSpec((tm, w), lambda i: (i, 0))
    slab = pl.BlockSpec((4, tm, 128), lambda i: (0, i, 0))
    return pl.pallas_call(
        body, grid=(S // tm,), name="rot_bwd",
        in_specs=[row(768), row(768), row(256), row(256), row(512), row(512), slab, slab, slab],
        out_specs=row(PW), out_shape=jax.ShapeDtypeStruct((S, PW), BF16),
        compiler_params=_params("parallel"),
    )(cos, sin, dqr, dkr, drv, drg, dq_att, dk_att, dv_att)


def _win_bwd_w(h1, dproj, exchange, exchange_args):
    def body(h_ref, dp_ref, dw_ref, xc):
        k = pl.program_id(0)

        @pl.when(k == 0)
        def _():
            xc.start()

        dw_ref[...] = _tn(h_ref[...], dp_ref[...]).astype(BF16)

        @pl.when(k == N_CHIP - 1)
        def _():
            xc.middle()
            xc.finish()

    (dw,), out = _carry(
        "win_bwd_w", body, exchange, exchange_args, (h1, dproj),
        [pl.BlockSpec((S, D), lambda k: (0, 0)), pl.BlockSpec((S, WIN_C), lambda k: (0, k))],
        [pl.BlockSpec((None, D, WIN_C), lambda k: (k, 0, 0))],
        [jax.ShapeDtypeStruct((N_CHIP, D, WIN_C), BF16)], grid=(N_CHIP,), semantics=("arbitrary",))
    return dw, out


def _in_bwd(dproj, win_g, x, dx2, g1, after):
    tm = 512

    def body(dp_ref, w_ref, x_ref, dx2_ref, g_ref, dx_ref, dg_ref, _):
        @pl.when(pl.program_id(0) == 0)
        def _():
            dg_ref[...] = jnp.zeros_like(dg_ref)

        dh = _nt(dp_ref[:, 0:WIN_C], w_ref[0])
        for k in range(1, N_CHIP):
            dh = dh + _nt(dp_ref[:, k * WIN_C:(k + 1) * WIN_C], w_ref[k])
        xv = x_ref[...]
        r = _rstd(xv)
        xn = xv * r
        dg_ref[...] = dg_ref[...] + jnp.sum(dh * xn, axis=0, keepdims=True)
        t = dh * g_ref[...]
        dx_ref[...] = dx2_ref[...] + r * (t - xn * jnp.mean(t * xn, axis=-1, keepdims=True))

    row = lambda w: pl.BlockSpec((tm, w), lambda i: (i, 0))
    vec = pl.BlockSpec((1, D), lambda i: (0, 0))
    return _carry("in_bwd", body, _NoExchange(), (), (dproj, win_g, x, dx2, g1),
                  [row(PW), pl.BlockSpec((N_CHIP, D, WIN_C), lambda i: (0, 0, 0)), row(D), row(D), vec],
                  [row(D), vec], [jax.ShapeDtypeStruct((S, D), F32), jax.ShapeDtypeStruct((1, D), F32)],
                  grid=(S // tm,), semantics=("arbitrary",), after=after)[0]


ANY = pl.BlockSpec(memory_space=pl.ANY)
VMEM = pl.BlockSpec(memory_space=pltpu.VMEM)
FLIPS = ((1, 0), (0, 1), (1, 1))


def _place():
    x, y, c = lax.axis_index("x"), lax.axis_index("y"), lax.axis_index("c")
    chips = [((1 - x) if fx else x, (1 - y) if fy else y) for fx, fy in FLIPS]
    return x, y, c, 2 * x + y, chips


def _remote(src, dst, send_sem, recv_sem, device):
    return pltpu.make_async_remote_copy(src_ref=src, dst_ref=dst, send_sem=send_sem, recv_sem=recv_sem,
                                        device_id=device, device_id_type=MESH)


class _Exchange:
    aliases = {}

    def middle(self, ins, outs, sems):
        pass


class _GatherShards(_Exchange):
    def __init__(self, shards):
        n = self.n = len(shards)
        self.n_in = self.n_out = n
        self.out_shape = [jax.ShapeDtypeStruct((N_CHIP,) + s.shape, s.dtype) for s in shards]
        dma = pltpu.SemaphoreType.DMA
        self.scratch = [dma((3 * n,)), dma((3 * n,)), dma((3 * n,)), dma((3 * n,)), dma((n,)), dma((n,))]

    def _ici(self, ins, outs, sems, a, j, chip):
        x, y, c, me, chips = _place()
        half = ins[a].shape[0] // 2
        return _remote(ins[a].at[pl.ds(c * half, half), :], outs[a].at[me, pl.ds(c * half, half), :],
                       sems[0].at[3 * a + j], sems[1].at[3 * a + j], (*chip, c))

    def _fwd(self, outs, sems, a, j, chip, half_of):
        x, y, c, me, chips = _place()
        half = outs[a].shape[1] // 2
        blk = outs[a].at[2 * chip[0] + chip[1], pl.ds(half_of * half, half), :]
        return _remote(blk, blk, sems[2].at[3 * a + j], sems[3].at[3 * a + j], (x, y, 1 - c))

    def _own(self, ins, outs, sems, a):
        return _own_shard_to_sibling(ins[a], outs[a], sems[4].at[a], sems[5].at[a])

    def start(self, ins, outs, sems):
        chips = _place()[4]
        for a in range(self.n):
            for j, chip in enumerate(chips):
                self._ici(ins, outs, sems, a, j, chip).start()
        for a in range(self.n):
            self._own(ins, outs, sems, a).start()

    def middle(self, ins, outs, sems):
        x, y, c, me, chips = _place()
        for a in range(self.n):
            for j, chip in enumerate(chips):
                half = outs[a].shape[1] // 2
                blk = outs[a].at[2 * chip[0] + chip[1], pl.ds(c * half, half), :]
                _remote(blk, blk, sems[0].at[3 * a + j], sems[1].at[3 * a + j], (x, y, c)).wait_recv()
                self._fwd(outs, sems, a, j, chip, c).start()

    def finish(self, ins, outs, sems):
        x, y, c, me, chips = _place()
        for a in range(self.n):
            for j, chip in enumerate(chips):
                self._fwd(outs, sems, a, j, chip, 1 - c).wait_recv()
        for a in range(self.n):
            for j, chip in enumerate(chips):
                self._ici(ins, outs, sems, a, j, chip).wait_send()
                self._fwd(outs, sems, a, j, chip, c).wait_send()
            self._own(ins, outs, sems, a).wait()


def _own_shard_to_sibling(shard_ref, gathered_ref, send_sem, recv_sem):
    x, y, c, me, chips = _place()
    return _remote(shard_ref, gathered_ref.at[me], send_sem, recv_sem, (x, y, 1 - c))


class _NoExchange(_Exchange):
    n_in = n_out = 0
    out_shape = ()
    scratch = ()

    def start(self, ins, outs, sems):
        pass

    def finish(self, ins, outs, sems):
        pass


class _ForwardGathered(_Exchange):
    def __init__(self, shards, own=True, forward=True):
        self.own, self.forward = own, forward
        n = self.n = len(shards)
        self.n_in, self.n_out = 2 * n, n
        self.out_shape = [jax.ShapeDtypeStruct((N_CHIP,) + s.shape, s.dtype) for s in shards]
        dma = pltpu.SemaphoreType.DMA
        self.scratch = [dma((3 * n,)), dma((3 * n,)), dma((n,)), dma((n,))]
        self.aliases = {n + a: a for a in range(n)}

    def _fwd(self, outs, sems, a, j, chip, half_of):
        x, y, c, me, chips = _place()
        half = outs[a].shape[1] // 2
        blk = outs[a].at[2 * chip[0] + chip[1], pl.ds(half_of * half, half), :]
        return _remote(blk, blk, sems[0].at[3 * a + j], sems[1].at[3 * a + j], (x, y, 1 - c))

    def _own(self, ins, outs, sems, a):
        return _own_shard_to_sibling(ins[a], outs[a], sems[2].at[a], sems[3].at[a])

    def start(self, ins, outs, sems):
        x, y, c, me, chips = _place()
        for a in range(self.n):
            for j, chip in enumerate(chips if self.forward else ()):
                self._fwd(outs, sems, a, j, chip, c).start()
        for a in range(self.n if self.own else 0):
            self._own(ins, outs, sems, a).start()

    def finish(self, ins, outs, sems):
        x, y, c, me, chips = _place()
        for a in range(self.n):
            for j, chip in enumerate(chips if self.forward else ()):
                self._fwd(outs, sems, a, j, chip, 1 - c).wait_recv()
        for a in range(self.n):
            for j, chip in enumerate(chips if self.forward else ()):
                self._fwd(outs, sems, a, j, chip, c).wait_send()
            if self.own:
                self._own(ins, outs, sems, a).wait()


HBM = pl.BlockSpec(memory_space=pltpu.HBM)
SEMS = pl.BlockSpec(memory_space=pltpu.SEMAPHORE)
DATAFLOW = pltpu.SideEffectType.DATAFLOW_SIDE_EFFECTING


class _OverIci:
    def __init__(self, name, sources, lands):
        self.name, self.n = name, len(sources)
        hbm = lambda t: pltpu.with_memory_space_constraint(t, pltpu.HBM)
        self.arrays = [hbm(t) for t in sources] + [hbm(t) for t in lands]

    def sent(self, src, land, a, chip):
        raise NotImplementedError

    def landed(self, land, a, chip):
        raise NotImplementedError

    def _copy(self, arr, sems, a, j, receiving):
        x, y, c, me, chips = _place()
        src, dst = self.sent(arr[a], arr[self.n + a], a, chips[j])
        if receiving:
            dst = self.landed(arr[self.n + a], a, chips[j])
        return _remote(src, dst, sems[0].at[3 * a + j], sems[1].at[3 * a + j], (*chips[j], c))

    def start(self, after):
        m = len(self.arrays)

        def body(*refs):
            arr, sems, token = refs[:m], refs[m + 1:m + 3], refs[-1]
            for a in range(self.n):
                for j in range(3):
                    self._copy(arr, sems, a, j, False).start()
            token[...] = jnp.zeros_like(token)

        dma = pltpu.SemaphoreType.DMA
        outs = pl.pallas_call(
            body, name=self.name + "_start",
            out_shape=[dma((3 * self.n,)), dma((3 * self.n,))] + [pltpu.HBM(t.shape, t.dtype) for t in self.arrays]
                      + [jax.ShapeDtypeStruct((8, 128), F32)],
            in_specs=[HBM] * m + [ANY], out_specs=[SEMS, SEMS] + [HBM] * m + [VMEM],
            input_output_aliases={i: 2 + i for i in range(m)},
            compiler_params=pltpu.CompilerParams(has_side_effects=DATAFLOW),
        )(*self.arrays, after)
        self.sems, self.arrays = outs[0:2], list(outs[2:2 + m])
        return outs[-1]

    def wait(self, after):
        m = len(self.arrays)

        def body(*refs):
            arr, sems = refs[:m], refs[m:m + 2]
            for a in range(self.n):
                for j in range(3):
                    self._copy(arr, sems, a, j, False).wait_send()
                    self._copy(arr, sems, a, j, True).wait_recv()

        outs = pl.pallas_call(
            body, name=self.name + "_wait",
            out_shape=[pltpu.HBM(t.shape, t.dtype) for t in self.arrays],
            in_specs=[HBM] * m + [SEMS, SEMS, ANY], out_specs=[HBM] * m,
            input_output_aliases={i: i for i in range(m)},
            compiler_params=pltpu.CompilerParams(has_side_effects=DATAFLOW),
        )(*self.arrays, *self.sems, after)
        return list(outs[:self.n]), list(outs[self.n:])


class _GatherOverIci(_OverIci):
    def __init__(self, name, shards):
        super().__init__(name, shards, [lax.empty((N_CHIP,) + s.shape, s.dtype) for s in shards])

    @staticmethod
    def _half(ref):
        c = lax.axis_index("c")
        half = ref.shape[-2] // 2
        return pl.ds(c * half, half)

    def sent(self, src, land, a, chip):
        return src.at[self._half(src), :], land.at[_place()[3], self._half(src), :]

    def landed(self, land, a, chip):
        return land.at[2 * chip[0] + chip[1], self._half(land), :]


class _SumOverIci(_OverIci):
    def __init__(self, name, pre):
        super().__init__(name, pre, [lax.empty(p.shape, p.dtype) for p in pre])

    def sent(self, src, land, a, chip):
        return src.at[2 * chip[0] + chip[1]], land.at[_place()[3]]

    def landed(self, land, a, chip):
        return land.at[2 * chip[0] + chip[1]]


class _HalvesToSibling(_Exchange):
    def __init__(self, grads):
        n = self.n = len(grads)
        self.n_in = self.n_out = n
        self.out_shape = [jax.ShapeDtypeStruct((N_CHIP, g.shape[1] // 2, g.shape[2]), g.dtype) for g in grads]
        self.scratch = [pltpu.SemaphoreType.DMA((n,)), pltpu.SemaphoreType.DMA((n,))]

    def _copy(self, ins, outs, sems, a):
        x, y, c, me, chips = _place()
        half = ins[a].shape[1] // 2
        return _remote(ins[a].at[:, pl.ds((1 - c) * half, half), :], outs[a], sems[0].at[a], sems[1].at[a], (x, y, 1 - c))

    def start(self, ins, outs, sems):
        for a in range(self.n):
            self._copy(ins, outs, sems, a).start()

    def finish(self, ins, outs, sems):
        for a in range(self.n):
            self._copy(ins, outs, sems, a).wait_recv()
        for a in range(self.n):
            self._copy(ins, outs, sems, a).wait_send()


class _ShareHalves(_Exchange):
    def __init__(self, fulls):
        n = self.n = len(fulls)
        self.n_in = self.n_out = n
        self.out_shape = [jax.ShapeDtypeStruct(f.shape, f.dtype) for f in fulls]
        self.scratch = [pltpu.SemaphoreType.DMA((n,)), pltpu.SemaphoreType.DMA((n,))]
        self.aliases = {a: a for a in range(n)}

    def _copy(self, outs, sems, a, half_of):
        x, y, c, me, chips = _place()
        half = outs[a].shape[0] // 2
        rows = outs[a].at[pl.ds(half_of * half, half), :]
        return _remote(rows, rows, sems[0].at[a], sems[1].at[a], (x, y, 1 - c))

    def start(self, ins, outs, sems):
        c = _place()[2]
        for a in range(self.n):
            self._copy(outs, sems, a, c).start()

    def finish(self, ins, outs, sems):
        c = _place()[2]
        for a in range(self.n):
            self._copy(outs, sems, a, 1 - c).wait_recv()
        for a in range(self.n):
            self._copy(outs, sems, a, c).wait_send()


class _GatherBlocks(_Exchange):
    def __init__(self, block):
        self.n_in = self.n_out = 1
        self.out_shape = [jax.ShapeDtypeStruct((8,) + block.shape, block.dtype)]
        dma = pltpu.SemaphoreType.DMA
        self.scratch = [dma((7,)), dma((7,)), dma]

    @staticmethod
    def _peer(f):
        x, y, c, me, chips = _place()
        return ((1 - x) if f & 4 else x, (1 - y) if f & 2 else y, (1 - c) if f & 1 else c)

    def start(self, ins, outs, sems):
        x, y, c, me, chips = _place()
        for f in range(1, 8):
            _remote(ins[0], outs[0].at[2 * me + c], sems[0].at[f - 1], sems[1].at[f - 1], self._peer(f)).start()
        pltpu.make_async_copy(ins[0], outs[0].at[2 * me + c], sems[2]).start()

    def finish(self, ins, outs, sems):
        x, y, c, me, chips = _place()
        for f in range(1, 8):
            px, py, pc = self._peer(f)
            blk = outs[0].at[4 * px + 2 * py + pc]
            _remote(blk, blk, sems[0].at[f - 1], sems[1].at[f - 1], (x, y, c)).wait_recv()
        for f in range(1, 8):
            _remote(ins[0], outs[0].at[2 * me + c], sems[0].at[f - 1], sems[1].at[f - 1], self._peer(f)).wait_send()
        pltpu.make_async_copy(ins[0], outs[0].at[2 * me + c], sems[2]).wait()


class _Both(_Exchange):
    def __init__(self, first, second):
        self.parts = (first, second)
        self.n_in, self.n_out = first.n_in + second.n_in, first.n_out + second.n_out
        self.out_shape = first.out_shape + second.out_shape
        self.scratch = first.scratch + second.scratch
        self.aliases = dict(first.aliases)
        self.aliases.update({first.n_in + i: first.n_out + o for i, o in second.aliases.items()})

    def _split(self, ins, outs, sems):
        a, b = self.parts
        return ((a, ins[:a.n_in], outs[:a.n_out], sems[:len(a.scratch)]),
                (b, ins[a.n_in:], outs[a.n_out:], sems[len(a.scratch):]))

    def start(self, ins, outs, sems):
        for ex, i, o, s in self._split(ins, outs, sems):
            ex.start(i, o, s)

    def middle(self, ins, outs, sems):
        for ex, i, o, s in self._split(ins, outs, sems):
            ex.middle(i, o, s)

    def finish(self, ins, outs, sems):
        for ex, i, o, s in self._split(ins, outs, sems):
            ex.finish(i, o, s)


class _Bound:
    def __init__(self, ex, ins, outs, sems):
        self.start = lambda: ex.start(ins, outs, sems)
        self.middle = lambda: ex.middle(ins, outs, sems)
        self.finish = lambda: ex.finish(ins, outs, sems)


def _carry(name, body, ex, ex_args, args, in_specs, out_specs, out_shape, scratch_shapes=(), grid=None, semantics=(),
           after=None):
    n_a, n_o, n_s = len(args), len(out_shape), len(scratch_shapes)
    behind = [] if after is None else [after]

    def full_body(*refs):
        p = 0
        groups = []
        for size in (n_a, ex.n_in, len(behind), n_o, ex.n_out, n_s, len(ex.scratch)):
            groups.append(refs[p:p + size])
            p += size
        a, ei, _, o, eo, s, es = groups
        body(*a, *o, *s, _Bound(ex, ei, eo, es))

    kwargs = {} if grid is None else {"grid": grid}
    outs = pl.pallas_call(
        full_body, name=name,
        in_specs=list(in_specs) + [ANY] * (ex.n_in + len(behind)), out_specs=list(out_specs) + [ANY] * ex.n_out,
        out_shape=list(out_shape) + list(ex.out_shape), scratch_shapes=list(scratch_shapes) + list(ex.scratch),
        input_output_aliases={n_a + i: n_o + o for i, o in ex.aliases.items()},
        compiler_params=_params(*semantics) if semantics else pltpu.CompilerParams(vmem_limit_bytes=VMEM_LIMIT),
        **kwargs,
    )(*args, *ex_args, *behind)
    return outs[:n_o], outs[n_o:]


def _prepare_carrying(name, x, g1, pos, ifc, spread, arrays, ex, ex_args):
    n = len(arrays)
    r, cc = arrays[0].shape
    steps = 4
    tr, tm = r // steps, S // steps

    def body(x_ref, g_ref, pos_ref, ifc_ref, e_ref, *refs):
        src, h_ref, cos_ref, sin_ref, dst, xc = refs[:n], refs[n], refs[n + 1], refs[n + 2], refs[n + 3:2 * n + 3], refs[-1]

        @pl.when(pl.program_id(0) == 0)
        def _():
            xc.start()

        xv = x_ref[...]
        h_ref[...] = (xv * _rstd(xv) * g_ref[...]).astype(BF16)
        ang = pos_ref[...].astype(F32) * ifc_ref[...]
        cos_ref[...] = _spread_exact(jnp.cos(ang), e_ref[...])
        sin_ref[...] = _spread_exact(jnp.sin(ang), e_ref[...])
        for a in range(n):
            dst[a][...] = src[a][...].astype(BF16)

        @pl.when(pl.program_id(0) == steps - 1)
        def _():
            xc.middle()
            xc.finish()

    row = lambda w: pl.BlockSpec((tm, w), lambda i: (i, 0))
    const = lambda w: pl.BlockSpec((1, w), lambda i: (0, 0))
    blk = pl.BlockSpec((tr, cc), lambda i: (i, 0))
    return _carry(name, body, ex, ex_args, (x, g1, pos, ifc, spread, *arrays),
                  [row(D), const(D), row(1), const(128), pl.BlockSpec((128, 768), lambda i: (0, 0))] + [blk] * n,
                  [row(D), row(768), row(768)] + [blk] * n,
                  [jax.ShapeDtypeStruct((S, D), BF16)] + [jax.ShapeDtypeStruct((S, 768), F32)] * 2
                  + [jax.ShapeDtypeStruct((r, cc), BF16)] * n,
                  grid=(steps,), semantics=("arbitrary",))


def _exchange_alone(name, ex, ex_args):
    def body(xc):
        xc.start()
        xc.middle()
        xc.finish()

    return _carry(name, body, ex, ex_args, (), (), (), ())[1]


def _core_index():
    return lax.axis_index("c").astype(jnp.int32).reshape(1)


def _pair_sum(gs, gots):
    n = len(gs)
    _, r, cc = gs[0].shape
    half = r // 2

    def body(c_ref, *refs):
        for a in range(n):
            refs[2 * n + a][...] = (refs[a][...].astype(F32) + refs[n + a][...].astype(F32)).astype(BF16)

    mine = pl.BlockSpec((None, half, cc), lambda k, c_ref: (k, c_ref[0], 0))
    blk = pl.BlockSpec((None, half, cc), lambda k, c_ref: (k, 0, 0))
    return pl.pallas_call(
        body, name=f"pair_sum_{r}x{cc}",
        grid_spec=pltpu.PrefetchScalarGridSpec(
            num_scalar_prefetch=1, grid=(N_CHIP,), in_specs=[mine] * n + [blk] * n, out_specs=[blk] * n),
        out_shape=[jax.ShapeDtypeStruct((N_CHIP, half, cc), BF16)] * n,
        compiler_params=_params("parallel"),
    )(_core_index(), *gs, *gots)


def _chip_sum(pre, parts):
    n = len(parts)
    _, half, cc = parts[0].shape
    tr = half // 2
    me = 2 * lax.axis_index("x") + lax.axis_index("y")
    others = [k + (k >= me).astype(jnp.int32) for k in range(3)]
    where = jnp.stack([lax.axis_index("c"), me, *others]).astype(jnp.int32)

    def body(w_ref, *refs):
        for a in range(n):
            own, p1, p2, p3 = refs[4 * a:4 * a + 4]
            refs[4 * n + a][...] = ((own[...].astype(F32) + p1[...].astype(F32)) + p2[...].astype(F32)) + p3[...].astype(F32)

    slot = lambda s: pl.BlockSpec((None, tr, cc), lambda i, w_ref: (w_ref[s], i, 0))
    operands = []
    for a in range(n):
        operands += [pre[a], parts[a], parts[a], parts[a]]
    return pl.pallas_call(
        body, name=f"chip_sum_{half}x{cc}",
        grid_spec=pltpu.PrefetchScalarGridSpec(
            num_scalar_prefetch=1, grid=(2,),
            in_specs=[slot(1), slot(2), slot(3), slot(4)] * n,
            out_specs=[pl.BlockSpec((tr, cc), lambda i, w_ref: (2 * w_ref[0] + i, 0))] * n),
        out_shape=[jax.ShapeDtypeStruct((2 * half, cc), F32)] * n,
        compiler_params=_params("parallel"),
    )(where, *operands)


def _adamw_math(w, g, m, v):
    m = ADAM_B1 * m + (1.0 - ADAM_B1) * g
    v = ADAM_B2 * v + (1.0 - ADAM_B2) * (g * g)
    m_hat = m / (1.0 - ADAM_B1 ** ADAM_STEP)
    v_hat = v / (1.0 - ADAM_B2 ** ADAM_STEP)
    delta = -ADAM_LR * (m_hat / (jnp.sqrt(v_hat) + ADAM_EPS) + ADAM_WD * w)
    return delta, m, v


def _adamw(w, g, m, v, after=None):
    r, cc = w.shape
    tr = r // 4

    def body(w_ref, g_ref, m_ref, v_ref, go_ref, d_ref, nm_ref, nv_ref, _):
        g = g_ref[...]
        go_ref[...] = g
        d_ref[...], nm_ref[...], nv_ref[...] = _adamw_math(w_ref[...], g, m_ref[...], v_ref[...])

    blk = pl.BlockSpec((tr, cc), lambda i: (i, 0))
    return _carry(f"adamw_{r}x{cc}", body, _NoExchange(), (), (w, g, m, v), [blk] * 4, [blk] * 4,
                  [jax.ShapeDtypeStruct((r, cc), F32)] * 4, grid=(4,), semantics=("parallel",), after=after)[0]


def _pack8(rows):
    def body(*refs):
        out_ref = refs[-1]
        out_ref[...] = jnp.zeros_like(out_ref)
        for i, r in enumerate(refs[:-1]):
            out_ref[i:i + 1, :] = r[...]

    return pl.pallas_call(body, name="pack8", out_shape=jax.ShapeDtypeStruct((8, D), F32))(*rows)


def _adamw_gains(gall, ws, ms, vs):
    def body(ga_ref, *refs):
        w, m, v = refs[0:4], refs[4:8], refs[8:12]
        outs, loss_ref, total = refs[12:28], refs[28], refs[29]
        g = ga_ref[0]
        for dev in range(1, 8):
            g = g + ga_ref[dev]
        total[...] = g
        for i in range(4):
            gi = total[i:i + 1, :]
            outs[i][...] = gi
            outs[4 + i][...], outs[8 + i][...], outs[12 + i][...] = _adamw_math(w[i][...], gi, m[i][...], v[i][...])
        loss_ref[...] = total[4:5, 0:128] * (0.5 / D)

    outs = pl.pallas_call(
        body, name="adamw_gains",
        out_shape=[jax.ShapeDtypeStruct((1, D), F32)] * 16 + [jax.ShapeDtypeStruct((1, 128), F32)],
        scratch_shapes=[pltpu.VMEM((8, D), F32)],
    )(gall, *ws, *ms, *vs)
    return outs[0:4], outs[4:8], outs[8:12], outs[12:16], outs[16]


def kernel(x, positions, w_in, w_out, g_pre_mix, g_post_mix, g_pre_ffn, g_post_ffn, w_gate, w_up, w_down, loss_target, m_w_in, m_w_out, m_g_pre_mix, m_g_post_mix, m_g_pre_ffn, m_g_post_ffn, m_w_gate, m_w_up, m_w_down, v_w_in, v_w_out, v_g_pre_mix, v_g_post_mix, v_g_pre_ffn, v_g_post_ffn, v_w_gate, v_w_up, v_w_down):
    tr = lambda t: jnp.swapaxes(t, 1, 2)[0]
    shards = [w_in[0], w_out[0], tr(w_gate), tr(w_up), w_down[0]]
    moms = [m_w_in[0], m_w_out[0], tr(m_w_gate), tr(m_w_up), m_w_down[0]]
    vels = [v_w_in[0], v_w_out[0], tr(v_w_gate), tr(v_w_up), v_w_down[0]]
    xs, pos, tgt = x[0], positions.reshape(S, 1), loss_target[0]
    g1, g2, g3, g4 = g_pre_mix, g_post_mix, g_pre_ffn, g_post_ffn
    tabs = tuple(jnp.asarray(t) for t in _retention_tables())
    ifc, spread = _rotary_tables()
    ifc, spread = jnp.asarray(ifc), jnp.asarray(spread, dtype=BF16)
    bf = [s.astype(BF16) for s in shards[:2]]

    (h1, cos, sin, *ffn_bf), (win_g,) = _prepare_carrying(
        "gather_in", xs, g1, pos, ifc, spread, shards[2:], _GatherShards(bf[:1]), bf[:1])
    bf += list(ffn_bf)
    wout_gather = _GatherOverIci("wout_gather", bf[1:2])
    token = wout_gather.start(win_g)
    ffn_gather = _GatherOverIci("ffn_gather", bf[2:])
    token = ffn_gather.start(token)
    qr, kr, rv, rg, aq, ak, av = _proj_fwd(h1, win_g, cos, sin, token)
    (o_raw, cat_r, states), _ = _ret_fwd(qr, kr, rv, rg, tabs, _NoExchange(), ())
    wout_sh, wout_land = wout_gather.wait(qr)
    n_ffn = len(bf[2:])
    (att_out, lse, cat_a), (wout_g, *ffn_gather.arrays[n_ffn:]) = _att_fwd(
        aq, ak, av, _Both(_ForwardGathered(bf[1:2]), _ForwardGathered(bf[2:], forward=False)),
        [*wout_sh, *wout_land, *ffn_gather.arrays])
    wout_g = wout_g.reshape(D, D)
    ffn_sh, ffn_lands = ffn_gather.wait(cat_a)
    (mix, x2, h3), (wg_g, wu_g, wd_g) = _mix_fwd(cat_r, cat_a, wout_g, xs, g2, g3,
                                                _ForwardGathered(bf[2:], own=False), [*ffn_sh, *ffn_lands])
    gt, up, a, sq, dy, df, dg4 = _ffn_fwd(h3, wg_g, wu_g, wd_g, x2, tgt, g4)

    dgt, dup, dx2, dmix, dg3, dg2 = _ffn_bwd_act(df, gt, up, wg_g, wu_g, wd_g, dy, x2, mix, g2, g3)
    ffn_grads = list(_ffn_bwd_w(a, df, h3, dgt, dup))
    (dret, datt, dwout), got = _mix_bwd(dmix, cat_r, cat_a, wout_g, _HalvesToSibling(ffn_grads), ffn_grads)
    ffn_sum = _SumOverIci("ffn_sum", _pair_sum(ffn_grads, got))
    token = ffn_sum.start(datt)
    (dq_att, dk_att, dv_att), _ = _att_bwd(aq, ak, av, datt, att_out, lse, _NoExchange(), (), token)
    (dqr, dkr, drv, drg), _ = _ret_bwd(qr, kr, rv, rg, o_raw, states, dret, tabs, _NoExchange(), (), token)
    dproj = _rot_bwd(cos, sin, dqr, dkr, drv, drg, dq_att, dk_att, dv_att)
    sums = _chip_sum(*ffn_sum.wait(dproj))
    dwin, ffn_full = _win_bwd_w(h1, dproj, _ShareHalves(sums), sums)
    in_grads = [dwin, dwout.reshape(N_CHIP, WOUT_R, D)]

    got = _exchange_alone("halves_to_sibling", _HalvesToSibling(in_grads), in_grads)
    in_sum = _SumOverIci("in_sum", [*_pair_sum(in_grads[:1], got[:1]), *_pair_sum(in_grads[1:], got[1:])])
    token = in_sum.start(dproj)
    dx, dg1 = _in_bwd(dproj, win_g, xs, dx2, g1, token)
    ffn_upd = [_adamw(shards[2 + i], ffn_full[o], moms[2 + i], vels[2 + i], token)
               for i, o in enumerate((1, 2, 0))]
    pre, parts = in_sum.wait(ffn_upd[2][0])
    sums = [*_chip_sum(pre[:1], parts[:1]), *_chip_sum(pre[1:], parts[1:])]
    gblock = _pack8([dg1, dg2, dg3, dg4, sq])
    *in_full, gall = _exchange_alone("share_rest", _Both(_ShareHalves(sums), _GatherBlocks(gblock)), [*sums, gblock])
    upd = [_adamw(w, g, m, v) for w, g, m, v in zip(shards[:2], in_full, moms[:2], vels[:2])] + ffn_upd
    gg, gd, gm, gv, loss_row = _adamw_gains(gall, [g1, g2, g3, g4],
                                            [m_g_pre_mix, m_g_post_mix, m_g_pre_ffn, m_g_post_ffn],
                                            [v_g_pre_mix, v_g_post_mix, v_g_pre_ffn, v_g_post_ffn])

    def order(mats, vecs):
        back = lambda t: jnp.swapaxes(t[None], 1, 2)
        return [mats[0][None], mats[1][None], *vecs, back(mats[2]), back(mats[3]), mats[4][None]]

    return (loss_row[0, 0], dx[None],
            *order([u[0] for u in upd], gg),
            *order([u[1] for u in upd], gd),
            *order([u[2] for u in upd], gm),
            *order([u[3] for u in upd], gv))
```

```python
import numpy as np
import jax
import jax.numpy as jnp
from jax import lax
from jax.experimental import pallas as pl
from jax.experimental.pallas import tpu as pltpu

F32, BF16 = jnp.float32, jnp.bfloat16
MESH = pl.DeviceIdType.MESH

S = 2048
D = 1024
PW = 3072
N_CHIP = 4
WIN_C = PW // N_CHIP
DFF = 2816
FF_C = DFF // N_CHIP
WOUT_R = D // N_CHIP
RMS_EPS = 1e-6
GN_EPS = 1e-5
RET_C = 128
RET_SCALE = 32 ** -0.5
ATT_BLK = 128
ATT_SCALE = 64 ** -0.5
PATTERN_DILATIONS = (1, 4, 16)
NEG = -1e30
VMEM_LIMIT = 56 * 1024 * 1024

ADAM_LR, ADAM_B1, ADAM_B2, ADAM_EPS, ADAM_WD, ADAM_STEP = 0.001, 0.9, 0.999, 1e-08, 0.01, 10


def _params(*sem):
    return pltpu.CompilerParams(dimension_semantics=sem, vmem_limit_bytes=VMEM_LIMIT)


def _nt(a, b):
    return lax.dot_general(a, b, (((1,), (1,)), ((), ())), preferred_element_type=F32)


def _tn(a, b):
    return lax.dot_general(a, b, (((0,), (0,)), ((), ())), preferred_element_type=F32)


def _nn(a, b):
    return jnp.dot(a, b, preferred_element_type=F32)


def _rstd(v):
    return lax.rsqrt(jnp.mean(v * v, axis=-1, keepdims=True) + RMS_EPS)


def _sigmoid(v):
    return 1.0 / (1.0 + jnp.exp(-v))


def _rows(i, t):
    return pl.ds(pl.multiple_of(i * t, t), t)


def _retention_tables():
    h = np.arange(8, dtype=np.float32)
    log_g = np.log1p(-np.exp2(-5.0 - h)).astype(np.float32)
    idx = np.arange(RET_C, dtype=np.float32)
    diff = idx[:, None] - idx[None, :]
    dtab = np.where(diff >= 0, np.exp(log_g[:, None, None] * np.maximum(diff, 0.0)), 0.0).astype(np.float32)
    dtab = dtab.reshape(8 * RET_C, RET_C)
    lane_head = np.arange(256) // 32
    a_tab = np.exp(log_g[lane_head][None, :] * (idx + 1.0)[:, None]).astype(np.float32)
    b_tab = np.exp(log_g[lane_head][None, :] * (RET_C - 1.0 - idx)[:, None]).astype(np.float32)
    lam = np.exp(log_g[lane_head] * RET_C).astype(np.float32)[:, None]
    bd = (lane_head[:, None] == (np.arange(512) // 64)[None, :]).astype(np.float32)
    return dtab, a_tab, b_tab, lam, bd


def _rotary_tables():
    inv_r = (1.0 / (np.float32(10000.0) ** np.linspace(0.0, 1.0, 16, dtype=np.float32))).astype(np.float32)
    inv_a = (np.float32(500000.0) ** (-np.arange(0, 16, 2, dtype=np.float32) / np.float32(16))).astype(np.float32)
    ifc = np.zeros((1, 128), np.float32)
    ifc[0, 0:16], ifc[0, 16:24] = inv_r, inv_a
    spread = np.zeros((128, 768), np.float32)
    for lane in range(256):
        spread[(lane % 32) % 16, lane] = 1.0
    for lane in range(512):
        d = lane % 64
        spread[16 + d % 8 if d < 16 else 24, 256 + lane] = 1.0
    return ifc, spread


def _rot_halves(tm):
    lo_r = (lax.broadcasted_iota(jnp.int32, (tm, 256), 1) % 32) < 16
    lo_a = (lax.broadcasted_iota(jnp.int32, (tm, 512), 1) % 64) < 8
    return lo_r, lo_a


def _spread_exact(t, e):
    hi = t.astype(BF16)
    r1 = t - hi.astype(F32)
    mid = r1.astype(BF16)
    lo = (r1 - mid.astype(F32)).astype(BF16)
    return _nn(hi, e) + _nn(mid, e) + _nn(lo, e)


def _proj_fwd(h1, win_g, cos, sin, after):
    tm = 256

    def body(h_ref, w_ref, cos_ref, sin_ref, qr_ref, kr_ref, rv_ref, rg_ref, aq_ref, ak_ref, av_ref, p_ref, _):
        h = h_ref[...]
        for k in range(N_CHIP):
            p_ref[:, k * WIN_C:(k + 1) * WIN_C] = _nn(h, w_ref[k])
        cr, ca, sr, sa = cos_ref[:, 0:256], cos_ref[:, 256:768], sin_ref[:, 0:256], sin_ref[:, 256:768]
        lo_r, lo_a = _rot_halves(tm)

        def rot_r(v):
            return v * cr + sr * jnp.where(lo_r, -pltpu.roll(v, 240, 1), pltpu.roll(v, 16, 1))

        def rot_a(v):
            return v * ca + sa * jnp.where(lo_a, -pltpu.roll(v, 504, 1), pltpu.roll(v, 8, 1))

        qr_ref[...] = rot_r(p_ref[:, 0:256]).astype(BF16)
        kr_ref[...] = (rot_r(p_ref[:, 256:512]) * RET_SCALE).astype(BF16)
        rv_ref[...] = p_ref[:, 512:1024].astype(BF16)
        rg_ref[...] = p_ref[:, 1024:1536]
        aq, ak = rot_a(p_ref[:, 1536:2048]), rot_a(p_ref[:, 2048:2560])
        for j in range(4):
            aq_ref[j] = aq[:, 128 * j:128 * j + 128]
            ak_ref[j] = ak[:, 128 * j:128 * j + 128]
            av_ref[j] = p_ref[:, 2560 + 128 * j:2560 + 128 * j + 128]

    row = lambda w: pl.BlockSpec((tm, w), lambda i: (i, 0))
    slab = pl.BlockSpec((4, tm, 128), lambda i: (0, i, 0))
    return _carry(
        "proj_fwd", body, _NoExchange(), (), (h1, win_g, cos, sin),
        [row(D), pl.BlockSpec((N_CHIP, D, WIN_C), lambda i: (0, 0, 0)), row(768), row(768)],
        [row(256), row(256), row(512), row(512), slab, slab, slab],
        [jax.ShapeDtypeStruct((S, w), BF16) for w in (256, 256, 512)]
        + [jax.ShapeDtypeStruct((S, 512), F32)] + [jax.ShapeDtypeStruct((4, S, 128), F32)] * 3,
        scratch_shapes=[pltpu.VMEM((tm, PW), F32)], grid=(S // tm,), semantics=("parallel",), after=after)[0]


def _seg_mean(v):
    lo = lax.broadcasted_iota(jnp.int32, v.shape, 1) < 64
    s_lo = jnp.sum(jnp.where(lo, v, 0.0), axis=-1, keepdims=True)
    s_hi = jnp.sum(jnp.where(lo, 0.0, v), axis=-1, keepdims=True)
    return jnp.where(lo, s_lo, s_hi) * (1.0 / 64.0)


def _ret_fwd(qr, kr, rv, proj, tabs, exchange, exchange_args):
    C = RET_C
    dtab, a_tab, b_tab, lam, bd = tabs

    def body(q_ref, k_ref, v_ref, g_ref, dt_ref, a_ref, b_ref, lam_ref, bd_ref, o_ref, cat_ref, st_ref, R, exch):
        @pl.when(pl.program_id(0) == 0)
        def _():
            exch.start()
            R[...] = jnp.zeros_like(R)

        @pl.when(pl.program_id(0) == S // C // 2)
        def _():
            exch.middle()

        q, k, v = q_ref[...], k_ref[...], v_ref[...]
        lane_head = lax.broadcasted_iota(jnp.int32, (C, 256), 1) // 32
        col_head = lax.broadcasted_iota(jnp.int32, (C, 256), 1) // 64
        rb = R[...].astype(BF16)
        st_ref[...] = rb
        qa = (q.astype(F32) * a_ref[...]).astype(BF16)
        cross = _nn(qa, rb)
        p = (_nt(_stack_heads(q, lane_head, n=8), k) * dt_ref[...]).astype(BF16)
        og = [cross[:, 256 * g:256 * g + 256]
              + _unstack_heads(_nn(p[4 * C * g:4 * C * (g + 1)], v[:, 256 * g:256 * g + 256]), col_head)
              for g in range(2)]
        kb = (k.astype(F32) * b_ref[...]).astype(BF16)
        R[...] = R[...] * lam_ref[...] + _tn(kb, v) * bd_ref[...]
        o_ref[:, 0:256] = og[0]
        o_ref[:, 256:512] = og[1]
        for j in range(4):
            oj = og[j // 2][:, 128 * (j % 2):128 * (j % 2) + 128]
            xc = oj - _seg_mean(oj)
            rn = xc * lax.rsqrt(_seg_mean(xc * xc) + GN_EPS)
            gj = g_ref[:, 128 * j:128 * j + 128]
            cat_ref[:, 128 * j:128 * j + 128] = (rn * (gj * _sigmoid(gj))).astype(BF16)

        @pl.when(pl.program_id(0) == S // C - 1)
        def _():
            exch.finish()

    row = lambda w: pl.BlockSpec((C, w), lambda n: (n, 0))
    full = lambda a: pl.BlockSpec(a.shape, lambda n: (0,) * a.ndim)
    return _carry(
        "ret_fwd", body, exchange, exchange_args, (qr, kr, rv, proj, dtab, a_tab, b_tab, lam, bd),
        [row(256), row(256), row(512), row(512),
         full(dtab), full(a_tab), full(b_tab), full(lam), full(bd)],
        [row(512), row(512), pl.BlockSpec((None, 256, 512), lambda n: (n, 0, 0))],
        [jax.ShapeDtypeStruct((S, 512), F32), jax.ShapeDtypeStruct((S, 512), BF16),
         jax.ShapeDtypeStruct((S // C, 256, 512), BF16)],
        scratch_shapes=[pltpu.VMEM((256, 512), F32)], grid=(S // C,), semantics=("arbitrary",))


def _stack_heads(v, lane_head, fill=0.0, n=4):
    return jnp.concatenate([jnp.where(lane_head == h, v, jnp.full_like(v, fill)) for h in range(n)], axis=0)


def _unstack_heads(v, lane_head, n=4):
    out = v[0:ATT_BLK]
    for h in range(1, n):
        out = jnp.where(lane_head == h, v[h * ATT_BLK:(h + 1) * ATT_BLK], out)
    return out


def _att_bias(has_prev):
    nk = 2 * ATT_BLK if has_prev else ATT_BLK
    a = lax.broadcasted_iota(jnp.int32, (4 * ATT_BLK, nk), 0) % ATT_BLK
    kk = lax.broadcasted_iota(jnp.int32, (4 * ATT_BLK, nk), 1)
    if not has_prev:
        return None, jnp.where((a - kk) >= 0, 0.0, NEG)
    dist = ATT_BLK + a - kk
    inside = (dist >= 0) & (dist <= ATT_BLK)
    return jnp.where(inside, 0.0, NEG), jnp.where(inside & (kk >= ATT_BLK), 0.0, NEG)


def _class_rows(ib, r, d):
    if d == 1:
        return pl.ds(pl.multiple_of(ib * ATT_BLK, ATT_BLK), ATT_BLK)
    return pl.ds(ib * ATT_BLK * d + r, ATT_BLK, stride=d)


def _slab_pair(ref, g, rows):
    return jnp.concatenate([ref[2 * g, rows, :], ref[2 * g + 1, rows, :]], axis=1)


def _att_blocks(d):
    nb = S // d // ATT_BLK
    return nb, nb > 1


def _att_fwd(aq, ak, av, exchange, exchange_args):
    def body(q_ref, k_ref, v_ref, o_ref, l_ref, cat_ref, xc):
        xc.start()
        lane_head = lax.broadcasted_iota(jnp.int32, (ATT_BLK, 256), 1) // 64
        for pi, d in enumerate(PATTERN_DILATIONS):
            if pi == len(PATTERN_DILATIONS) - 1:
                xc.middle()
            nb, has_prev = _att_blocks(d)
            bias_rest, bias_first = _att_bias(has_prev)

            def block(b, carry, pi=pi, d=d, nb=nb, has_prev=has_prev, bias_rest=bias_rest, bias_first=bias_first):
                r, ib = b // nb, b % nb
                rows = _class_rows(ib, r, d)
                prow = _class_rows(jnp.maximum(ib - 1, 0), r, d)
                bias = jnp.where(ib == 0, bias_first, bias_rest) if has_prev else bias_first
                for g in range(2):
                    qg = _slab_pair(q_ref, g, rows).astype(BF16)
                    kg = _slab_pair(k_ref, g, rows)
                    vg = _slab_pair(v_ref, g, rows)
                    if has_prev:
                        kg = jnp.concatenate([_slab_pair(k_ref, g, prow), kg], axis=0)
                        vg = jnp.concatenate([_slab_pair(v_ref, g, prow), vg], axis=0)
                    kg, vg = kg.astype(BF16), vg.astype(BF16)
                    s = _nt(_stack_heads(qg, lane_head), kg) * ATT_SCALE + bias
                    m = jnp.max(s, axis=-1, keepdims=True)
                    p = jnp.exp(s - m)
                    den = jnp.sum(p, axis=-1, keepdims=True)
                    og = _unstack_heads(_nn(p.astype(BF16), vg) / den, lane_head)
                    lg = _unstack_heads(jnp.broadcast_to(m + jnp.log(den), (4 * ATT_BLK, 256)), lane_head)
                    for jj in range(2):
                        j = 2 * g + jj
                        o_new, l_new = og[:, 128 * jj:128 * jj + 128], lg[:, 128 * jj:128 * jj + 128]
                        if pi > 0:
                            o_old, l_old = o_ref[j, rows, :], l_ref[j, rows, :]
                            mx = jnp.maximum(l_old, l_new)
                            ea, eb = jnp.exp(l_old - mx), jnp.exp(l_new - mx)
                            den = ea + eb
                            o_new = (ea * o_old + eb * o_new) / den
                            l_new = mx + jnp.log(den)
                        o_ref[j, rows, :] = o_new
                        l_ref[j, rows, :] = l_new
                return carry

            lax.fori_loop(0, S // ATT_BLK, block, 0, unroll=4)

        def to_cat(i, carry):
            rows = _rows(i, 256)
            for j in range(4):
                cat_ref[rows, 128 * j:128 * j + 128] = o_ref[j, rows, :].astype(BF16)
            return carry

        lax.fori_loop(0, S // 256, to_cat, 0)
        xc.finish()

    slab = jax.ShapeDtypeStruct((4, S, 128), F32)
    return _carry("att_fwd", body, exchange, exchange_args, (aq, ak, av), [VMEM] * 3, [VMEM] * 3,
                  [slab, slab, jax.ShapeDtypeStruct((S, 512), BF16)])


def _mix_fwd(cat_r, cat_a, wout, x, g2, g3, exchange, exchange_args):
    tm = 512

    def body(cr_ref, ca_ref, w_ref, x_ref, g2_ref, g3_ref, mix_ref, x2_ref, h3_ref, xc):
        @pl.when(pl.program_id(0) == 0)
        def _():
            xc.start()

        mix = _nn(cr_ref[...], w_ref[0:512, :]) + _nn(ca_ref[...], w_ref[512:1024, :])
        mix_ref[...] = mix
        x2 = x_ref[...] + mix * _rstd(mix) * g2_ref[...]
        x2_ref[...] = x2
        h3_ref[...] = (x2 * _rstd(x2) * g3_ref[...]).astype(BF16)

        @pl.when(pl.program_id(0) == S // tm - 1)
        def _():
            xc.middle()
            xc.finish()

    row = lambda w: pl.BlockSpec((tm, w), lambda i: (i, 0))
    vec = pl.BlockSpec((1, D), lambda i: (0, 0))
    return _carry("mix_fwd", body, exchange, exchange_args, (cat_r, cat_a, wout, x, g2, g3),
                  [row(512), row(512), pl.BlockSpec((D, D), lambda i: (0, 0)), row(D), vec, vec],
                  [row(D), row(D), row(D)],
                  [jax.ShapeDtypeStruct((S, D), F32), jax.ShapeDtypeStruct((S, D), F32),
                   jax.ShapeDtypeStruct((S, D), BF16)],
                  grid=(S // tm,), semantics=("arbitrary",))


def _ffn_fwd(h3, wg, wu, wd, x2, tgt, g4):
    tm = 512
    last = N_CHIP - 1

    def body(h_ref, wg_ref, wu_ref, wd_ref, x2_ref, t_ref, g_ref,
             gt_ref, up_ref, a_ref, loss_ref, dy_ref, df_ref, dg_ref, f_ref):
        k, i = pl.program_id(0), pl.program_id(1)
        h = h_ref[...]
        gt = _nt(h, wg_ref[...])
        up = _nt(h, wu_ref[...])
        gt_ref[...] = gt.astype(BF16)
        up_ref[...] = up.astype(BF16)
        a = (gt * _sigmoid(gt) * up).astype(BF16)
        a_ref[...] = a
        part = _nn(a, wd_ref[...])
        rows = _rows(i, tm)

        @pl.when(k == 0)
        def _():
            f_ref[rows, :] = part

        @pl.when((k > 0) & (k < last))
        def _():
            f_ref[rows, :] = f_ref[rows, :] + part

        @pl.when((k == last) & (i == 0))
        def _():
            loss_ref[...] = jnp.zeros_like(loss_ref)
            dg_ref[...] = jnp.zeros_like(dg_ref)

        @pl.when(k == last)
        def _():
            fv = f_ref[rows, :] + part
            r = _rstd(fv)
            fn = fv * r
            e = x2_ref[...] + fn * g_ref[...] - t_ref[...]
            loss_ref[...] = loss_ref[...] + jnp.sum(jnp.sum(e * e, axis=-1, keepdims=True), axis=0, keepdims=True)
            dy = e * (1.0 / D)
            dy_ref[...] = dy
            dg_ref[...] = dg_ref[...] + jnp.sum(dy * fn, axis=0, keepdims=True)
            t = dy * g_ref[...]
            df_ref[...] = (r * (t - fn * jnp.mean(t * fn, axis=-1, keepdims=True))).astype(BF16)

    wrow = pl.BlockSpec((None, FF_C, D), lambda k, i: (k, 0, 0))
    act = pl.BlockSpec((None, tm, FF_C), lambda k, i: (k, i, 0))
    late = pl.BlockSpec((tm, D), lambda k, i: (jnp.where(k == last, i, 0), 0))
    vec = pl.BlockSpec((1, D), lambda k, i: (0, 0))
    return pl.pallas_call(
        body, grid=(N_CHIP, S // tm), name="ffn_fwd",
        in_specs=[pl.BlockSpec((tm, D), lambda k, i: (i, 0)), wrow, wrow, wrow, late, late, vec],
        out_specs=[act, act, act, vec, late, late, vec],
        out_shape=[jax.ShapeDtypeStruct((N_CHIP, S, FF_C), BF16)] * 3
                  + [jax.ShapeDtypeStruct((1, D), F32), jax.ShapeDtypeStruct((S, D), F32),
                     jax.ShapeDtypeStruct((S, D), BF16), jax.ShapeDtypeStruct((1, D), F32)],
        scratch_shapes=[pltpu.VMEM((S, D), F32)],
        compiler_params=_params("arbitrary", "arbitrary"),
    )(h3, wg, wu, wd, x2, tgt, g4)


def _ffn_bwd_act(df, gt, up, wg, wu, wd, dy, x2, mix, g2, g3):
    tm, sub = 512, 256
    last = N_CHIP - 1

    def body(df_ref, gt_ref, up_ref, wg_ref, wu_ref, wd_ref, dy_ref, x2_ref, mix_ref, g2_ref, g3_ref,
             dgt_ref, dup_ref, dx2_ref, dmix_ref, dg3_ref, dg2_ref, dh_ref):
        k, i = pl.program_id(0), pl.program_id(1)
        parts = []
        for s in range(tm // sub):
            rows = slice(s * sub, (s + 1) * sub)
            da = _nt(df_ref[rows, :], wd_ref[...])
            gt, up = gt_ref[rows, :].astype(F32), up_ref[rows, :].astype(F32)
            sg = _sigmoid(gt)
            dup = (da * gt * sg).astype(BF16)
            dgt = (da * up * (sg * (1.0 + gt * (1.0 - sg)))).astype(BF16)
            dup_ref[rows, :] = dup
            dgt_ref[rows, :] = dgt
            parts.append(_nn(dgt, wg_ref[...]) + _nn(dup, wu_ref[...]))
        part = jnp.concatenate(parts, axis=0)
        rows = _rows(i, tm)

        @pl.when(k == 0)
        def _():
            dh_ref[rows, :] = part

        @pl.when((k > 0) & (k < last))
        def _():
            dh_ref[rows, :] = dh_ref[rows, :] + part

        @pl.when((k == last) & (i == 0))
        def _():
            dg3_ref[...] = jnp.zeros_like(dg3_ref)
            dg2_ref[...] = jnp.zeros_like(dg2_ref)

        @pl.when(k == last)
        def _():
            dh = dh_ref[rows, :] + part
            x2 = x2_ref[...]
            r3 = _rstd(x2)
            xn = x2 * r3
            dg3_ref[...] = dg3_ref[...] + jnp.sum(dh * xn, axis=0, keepdims=True)
            t = dh * g3_ref[...]
            dx2 = dy_ref[...] + r3 * (t - xn * jnp.mean(t * xn, axis=-1, keepdims=True))
            dx2_ref[...] = dx2
            mix = mix_ref[...]
            r2 = _rstd(mix)
            mn = mix * r2
            dg2_ref[...] = dg2_ref[...] + jnp.sum(dx2 * mn, axis=0, keepdims=True)
            u = dx2 * g2_ref[...]
            dmix_ref[...] = (r2 * (u - mn * jnp.mean(u * mn, axis=-1, keepdims=True))).astype(BF16)

    wrow = pl.BlockSpec((None, FF_C, D), lambda k, i: (k, 0, 0))
    act = pl.BlockSpec((None, tm, FF_C), lambda k, i: (k, i, 0))
    row = pl.BlockSpec((tm, D), lambda k, i: (i, 0))
    late = pl.BlockSpec((tm, D), lambda k, i: (jnp.where(k == last, i, 0), 0))
    vec = pl.BlockSpec((1, D), lambda k, i: (0, 0))
    return pl.pallas_call(
        body, grid=(N_CHIP, S // tm), name="ffn_bwd_act",
        in_specs=[row, act, act, wrow, wrow, wrow, late, late, late, vec, vec],
        out_specs=[act, act, late, late, vec, vec],
        out_shape=[jax.ShapeDtypeStruct((N_CHIP, S, FF_C), BF16), jax.ShapeDtypeStruct((N_CHIP, S, FF_C), BF16),
                   jax.ShapeDtypeStruct((S, D), F32), jax.ShapeDtypeStruct((S, D), BF16),
                   jax.ShapeDtypeStruct((1, D), F32), jax.ShapeDtypeStruct((1, D), F32)],
        scratch_shapes=[pltpu.VMEM((S, D), F32)],
        compiler_params=_params("arbitrary", "arbitrary"),
    )(df, gt, up, wg, wu, wd, dy, x2, mix, g2, g3)


def _ffn_bwd_w(a, df, h3, dgt, dup):
    tm = 1024
    assert S // tm == 2

    def body(a_ref, df_ref, h_ref, dgt_ref, dup_ref, dwd_ref, dwg_ref, dwu_ref, acc_d, acc_g, acc_u):
        i = pl.program_id(1)
        h = h_ref[...]
        parts = (_tn(a_ref[...], df_ref[...]), _tn(dgt_ref[...], h), _tn(dup_ref[...], h))

        @pl.when(i == 0)
        def _():
            for acc, part in zip((acc_d, acc_g, acc_u), parts):
                acc[...] = part

        @pl.when(i == S // tm - 1)
        def _():
            for out, acc, part in zip((dwd_ref, dwg_ref, dwu_ref), (acc_d, acc_g, acc_u), parts):
                out[...] = (acc[...] + part).astype(BF16)

    act = pl.BlockSpec((None, tm, FF_C), lambda k, i: (k, i, 0))
    row = pl.BlockSpec((tm, D), lambda k, i: (i, 0))
    wrow = pl.BlockSpec((None, FF_C, D), lambda k, i: (k, 0, 0))
    return pl.pallas_call(
        body, grid=(N_CHIP, S // tm), name="ffn_bwd_w",
        in_specs=[act, row, row, act, act],
        out_specs=[wrow, wrow, wrow],
        out_shape=[jax.ShapeDtypeStruct((N_CHIP, FF_C, D), BF16)] * 3,
        scratch_shapes=[pltpu.VMEM((FF_C, D), F32)] * 3,
        compiler_params=_params("parallel", "arbitrary"),
    )(a, df, h3, dgt, dup)


def _mix_bwd(dmix, cat_r, cat_a, wout, exchange, exchange_args):
    tm = 1024

    def body(dm_ref, cr_ref, ca_ref, w_ref, dret_ref, datt_ref, dw_ref, acc, xc):
        i = pl.program_id(0)

        @pl.when(i == 0)
        def _():
            xc.start()
            acc[...] = jnp.zeros_like(acc)

        dm = dm_ref[...]
        dret_ref[...] = _nt(dm, w_ref[0:512, :])
        datt = _nt(dm, w_ref[512:1024, :])
        for j in range(4):
            datt_ref[j] = datt[:, 128 * j:128 * j + 128]
        acc[0:512, :] += _tn(cr_ref[...], dm)
        acc[512:1024, :] += _tn(ca_ref[...], dm)

        @pl.when(i == S // tm - 1)
        def _():
            dw_ref[...] = acc[...].astype(BF16)
            xc.middle()
            xc.finish()

    row = lambda w: pl.BlockSpec((tm, w), lambda i: (i, 0))
    full = pl.BlockSpec((D, D), lambda i: (0, 0))
    return _carry("mix_bwd", body, exchange, exchange_args, (dmix, cat_r, cat_a, wout),
                  [row(D), row(512), row(512), full],
                  [row(512), pl.BlockSpec((4, tm, 128), lambda i: (0, i, 0)), full],
                  [jax.ShapeDtypeStruct((S, 512), F32), jax.ShapeDtypeStruct((4, S, 128), F32),
                   jax.ShapeDtypeStruct((D, D), BF16)],
                  scratch_shapes=[pltpu.VMEM((D, D), F32)], grid=(S // tm,), semantics=("arbitrary",))


def _att_bwd(aq, ak, av, datt, att_out, lse, exchange, exchange_args, after=None):
    def body(q_ref, k_ref, v_ref, do_ref, out_ref, l_ref, dq_ref, dk_ref, dv_ref, xc):
        xc.start()

        def clear(i, carry):
            rows = _rows(i, 256)
            for ref in (dq_ref, dk_ref, dv_ref):
                for j in range(4):
                    ref[j, rows, :] = jnp.zeros((256, 128), F32)
            return carry

        lax.fori_loop(0, S // 256, clear, 0)
        lane_head = lax.broadcasted_iota(jnp.int32, (ATT_BLK, 256), 1) // 64
        for d in PATTERN_DILATIONS:
            nb, has_prev = _att_blocks(d)
            bias_rest, bias_first = _att_bias(has_prev)

            def block(b, carry, d=d, nb=nb, has_prev=has_prev, bias_rest=bias_rest, bias_first=bias_first):
                r, ib = b // nb, b % nb
                rows = _class_rows(ib, r, d)
                prow = _class_rows(jnp.maximum(ib - 1, 0), r, d)
                bias = jnp.where(ib == 0, bias_first, bias_rest) if has_prev else bias_first
                for g in range(2):
                    qg = _slab_pair(q_ref, g, rows).astype(BF16)
                    kg = _slab_pair(k_ref, g, rows)
                    vg = _slab_pair(v_ref, g, rows)
                    if has_prev:
                        kg = jnp.concatenate([_slab_pair(k_ref, g, prow), kg], axis=0)
                        vg = jnp.concatenate([_slab_pair(v_ref, g, prow), vg], axis=0)
                    kg, vg = kg.astype(BF16), vg.astype(BF16)
                    dog = _slab_pair(do_ref, g, rows)
                    outg = _slab_pair(out_ref, g, rows)
                    lg = _slab_pair(l_ref, g, rows)
                    qs = _stack_heads(qg, lane_head)
                    dos = _stack_heads(dog, lane_head)
                    delta = jnp.sum(dos * jnp.concatenate([outg] * 4, axis=0), axis=-1, keepdims=True)
                    lh = jnp.max(_stack_heads(lg, lane_head, NEG), axis=-1, keepdims=True)
                    s = _nt(qs, kg) * ATT_SCALE + bias
                    p = jnp.exp(s - lh)
                    dosb = dos.astype(BF16)
                    ds = (p * (_nt(dosb, vg) - delta) * ATT_SCALE).astype(BF16)
                    dq = _unstack_heads(_nn(ds, kg), lane_head)
                    dk = _tn(ds, qs)
                    dv = _tn(p.astype(BF16), dosb)
                    for jj in range(2):
                        j, sl = 2 * g + jj, slice(128 * jj, 128 * jj + 128)
                        dq_ref[j, rows, :] += dq[:, sl]
                        if has_prev:
                            dk_ref[j, prow, :] += dk[0:ATT_BLK, sl]
                            dv_ref[j, prow, :] += dv[0:ATT_BLK, sl]
                            dk_ref[j, rows, :] += dk[ATT_BLK:2 * ATT_BLK, sl]
                            dv_ref[j, rows, :] += dv[ATT_BLK:2 * ATT_BLK, sl]
                        else:
                            dk_ref[j, rows, :] += dk[:, sl]
                            dv_ref[j, rows, :] += dv[:, sl]
                return carry

            lax.fori_loop(0, S // ATT_BLK, block, 0, unroll=4)
        xc.middle()
        xc.finish()

    slab = jax.ShapeDtypeStruct((4, S, 128), F32)
    return _carry("att_bwd", body, exchange, exchange_args, (aq, ak, av, datt, att_out, lse), [VMEM] * 6, [VMEM] * 3,
                  [slab, slab, slab], after=after)


def _ret_bwd(qr, kr, rv, proj, o_raw, states, dret, tabs, exchange, exchange_args, after=None):
    C = RET_C
    nc = S // C
    dtab, a_tab, b_tab, lam, bd = tabs

    def body(q_ref, k_ref, v_ref, g_ref, o_ref, st_ref, dr_ref, dt_ref, a_ref, b_ref, lam_ref, bd_ref,
             dq_ref, dk_ref, dv_ref, dg_ref, dR, exch):
        @pl.when(pl.program_id(0) == 0)
        def _():
            exch.start()
            dR[...] = jnp.zeros_like(dR)

        q, k, v = q_ref[...], k_ref[...], v_ref[...]
        lane_head = lax.broadcasted_iota(jnp.int32, (C, 256), 1) // 32
        col_head = lax.broadcasted_iota(jnp.int32, (C, 256), 1) // 64
        dos = []
        for j in range(4):
            sl = slice(128 * j, 128 * j + 128)
            oj = o_ref[:, sl]
            xc = oj - _seg_mean(oj)
            rs = lax.rsqrt(_seg_mean(xc * xc) + GN_EPS)
            rn = xc * rs
            gj = g_ref[:, sl]
            sg = _sigmoid(gj)
            dret = dr_ref[:, sl]
            dg_ref[:, sl] = dret * rn * (sg * (1.0 + gj * (1.0 - sg)))
            drn = dret * (gj * sg)
            dos.append(rs * (drn - _seg_mean(drn) - rn * _seg_mean(drn * rn)))
        do = [jnp.concatenate(dos[0:2], axis=1), jnp.concatenate(dos[2:4], axis=1)]
        do8 = jnp.concatenate(do, axis=1).astype(BF16)
        drb = dR[...].astype(BF16)
        rb = st_ref[...]
        dq = _nt(do8, rb) * a_ref[...]
        dk = _nt(v, drb) * b_ref[...]
        kb = (k.astype(F32) * b_ref[...]).astype(BF16)
        dvall = _nn(kb, drb)
        qs = _stack_heads(q, lane_head, n=8)
        dec = dt_ref[...]
        p = (_nt(qs, k) * dec).astype(BF16)
        dos = [_stack_heads(do[g], col_head).astype(BF16) for g in range(2)]
        dp = jnp.concatenate([_nt(dos[g], v[:, 256 * g:256 * g + 256]) for g in range(2)], axis=0)
        ds = (dp * dec).astype(BF16)
        dq = dq + _unstack_heads(_nn(ds, k), lane_head, n=8)
        dk = dk + _tn(ds, qs)
        dv = [dvall[:, 256 * g:256 * g + 256] + _tn(p[4 * C * g:4 * C * (g + 1)], dos[g]) for g in range(2)]
        qa = (q.astype(F32) * a_ref[...]).astype(BF16)
        dR[...] = dR[...] * lam_ref[...] + _tn(qa, do8) * bd_ref[...]
        dq_ref[...] = dq
        dk_ref[...] = dk
        dv_ref[:, 0:256] = dv[0]
        dv_ref[:, 256:512] = dv[1]

        @pl.when(pl.program_id(0) == nc - 1)
        def _():
            exch.middle()
            exch.finish()

    rev = lambda w: pl.BlockSpec((C, w), lambda n: (nc - 1 - n, 0))
    full = lambda a: pl.BlockSpec(a.shape, lambda n: (0,) * a.ndim)
    return _carry(
        "ret_bwd", body, exchange, exchange_args, (qr, kr, rv, proj, o_raw, states, dret, dtab, a_tab, b_tab, lam, bd),
        [rev(256), rev(256), rev(512), rev(512), rev(512),
         pl.BlockSpec((None, 256, 512), lambda n: (nc - 1 - n, 0, 0)), rev(512),
         full(dtab), full(a_tab), full(b_tab), full(lam), full(bd)],
        [rev(256), rev(256), rev(512), rev(512)],
        [jax.ShapeDtypeStruct((S, 256), F32), jax.ShapeDtypeStruct((S, 256), F32),
         jax.ShapeDtypeStruct((S, 512), F32), jax.ShapeDtypeStruct((S, 512), F32)],
        scratch_shapes=[pltpu.VMEM((256, 512), F32)], grid=(nc,), semantics=("arbitrary",), after=after)


def _rot_bwd(cos, sin, dqr, dkr, drv, drg, dq_att, dk_att, dv_att):
    tm = 256

    def body(cos_ref, sin_ref, dqr_ref, dkr_ref, drv_ref, drg_ref, dqa_ref, dka_ref, dva_ref, dp_ref):
        cr, ca, sr, sa = cos_ref[:, 0:256], cos_ref[:, 256:768], sin_ref[:, 0:256], sin_ref[:, 256:768]
        lo_r, lo_a = _rot_halves(tm)

        def unrot_r(g):
            gs = g * sr
            return g * cr + pltpu.roll(jnp.where(lo_r, -gs, 0.0), 16, 1) + pltpu.roll(jnp.where(lo_r, 0.0, gs), 240, 1)

        def unrot_a(g):
            gs = g * sa
            return g * ca + pltpu.roll(jnp.where(lo_a, -gs, 0.0), 8, 1) + pltpu.roll(jnp.where(lo_a, 0.0, gs), 504, 1)

        def wide(ref):
            return jnp.concatenate([ref[j] for j in range(4)], axis=1)

        dp_ref[:, 0:256] = unrot_r(dqr_ref[...]).astype(BF16)
        dp_ref[:, 256:512] = unrot_r(dkr_ref[...] * RET_SCALE).astype(BF16)
        dp_ref[:, 512:1024] = drv_ref[...].astype(BF16)
        dp_ref[:, 1024:1536] = drg_ref[...].astype(BF16)
        dp_ref[:, 1536:2048] = unrot_a(wide(dqa_ref)).astype(BF16)
        dp_ref[:, 2048:2560] = unrot_a(wide(dka_ref)).astype(BF16)
        dp_ref[:, 2560:3072] = wide(dva_ref).astype(BF16)

    row = lambda w: pl.BlockSpec((tm, w), lambda i: (i, 0))
    slab = pl.BlockSpec((4, tm, 128), lambda i: (0, i, 0))
    return pl.pallas_call(
        body, grid=(S // tm,), name="rot_bwd",
        in_specs=[row(768), row(768), row(256), row(256), row(512), row(512), slab, slab, slab],
        out_specs=row(PW), out_shape=jax.ShapeDtypeStruct((S, PW), BF16),
        compiler_params=_params("parallel"),
    )(cos, sin, dqr, dkr, drv, drg, dq_att, dk_att, dv_att)


def _win_bwd_w(h1, dproj, exchange, exchange_args):
    def body(h_ref, dp_ref, dw_ref, xc):
        k = pl.program_id(0)

        @pl.when(k == 0)
        def _():
            xc.start()

        dw_ref[...] = _tn(h_ref[...], dp_ref[...]).astype(BF16)

        @pl.when(k == N_CHIP - 1)
        def _():
            xc.middle()
            xc.finish()

    (dw,), out = _carry(
        "win_bwd_w", body, exchange, exchange_args, (h1, dproj),
        [pl.BlockSpec((S, D), lambda k: (0, 0)), pl.BlockSpec((S, WIN_C), lambda k: (0, k))],
        [pl.BlockSpec((None, D, WIN_C), lambda k: (k, 0, 0))],
        [jax.ShapeDtypeStruct((N_CHIP, D, WIN_C), BF16)], grid=(N_CHIP,), semantics=("arbitrary",))
    return dw, out


def _in_bwd(dproj, win_g, x, dx2, g1, after):
    tm = 512

    def body(dp_ref, w_ref, x_ref, dx2_ref, g_ref, dx_ref, dg_ref, _):
        @pl.when(pl.program_id(0) == 0)
        def _():
            dg_ref[...] = jnp.zeros_like(dg_ref)

        dh = _nt(dp_ref[:, 0:WIN_C], w_ref[0])
        for k in range(1, N_CHIP):
            dh = dh + _nt(dp_ref[:, k * WIN_C:(k + 1) * WIN_C], w_ref[k])
        xv = x_ref[...]
        r = _rstd(xv)
        xn = xv * r
        dg_ref[...] = dg_ref[...] + jnp.sum(dh * xn, axis=0, keepdims=True)
        t = dh * g_ref[...]
        dx_ref[...] = dx2_ref[...] + r * (t - xn * jnp.mean(t * xn, axis=-1, keepdims=True))

    row = lambda w: pl.BlockSpec((tm, w), lambda i: (i, 0))
    vec = pl.BlockSpec((1, D), lambda i: (0, 0))
    return _carry("in_bwd", body, _NoExchange(), (), (dproj, win_g, x, dx2, g1),
                  [row(PW), pl.BlockSpec((N_CHIP, D, WIN_C), lambda i: (0, 0, 0)), row(D), row(D), vec],
                  [row(D), vec], [jax.ShapeDtypeStruct((S, D), F32), jax.ShapeDtypeStruct((1, D), F32)],
                  grid=(S // tm,), semantics=("arbitrary",), after=after)[0]


ANY = pl.BlockSpec(memory_space=pl.ANY)
VMEM = pl.BlockSpec(memory_space=pltpu.VMEM)
FLIPS = ((1, 0), (0, 1), (1, 1))


def _place():
    x, y, c = lax.axis_index("x"), lax.axis_index("y"), lax.axis_index("c")
    chips = [((1 - x) if fx else x, (1 - y) if fy else y) for fx, fy in FLIPS]
    return x, y, c, 2 * x + y, chips


def _remote(src, dst, send_sem, recv_sem, device):
    return pltpu.make_async_remote_copy(src_ref=src, dst_ref=dst, send_sem=send_sem, recv_sem=recv_sem,
                                        device_id=device, device_id_type=MESH)


class _Exchange:
    aliases = {}

    def middle(self, ins, outs, sems):
        pass


class _GatherShards(_Exchange):
    def __init__(self, shards):
        n = self.n = len(shards)
        self.n_in = self.n_out = n
        self.out_shape = [jax.ShapeDtypeStruct((N_CHIP,) + s.shape, s.dtype) for s in shards]
        dma = pltpu.SemaphoreType.DMA
        self.scratch = [dma((3 * n,)), dma((3 * n,)), dma((3 * n,)), dma((3 * n,)), dma((n,)), dma((n,))]

    def _ici(self, ins, outs, sems, a, j, chip):
        x, y, c, me, chips = _place()
        half = ins[a].shape[0] // 2
        return _remote(ins[a].at[pl.ds(c * half, half), :], outs[a].at[me, pl.ds(c * half, half), :],
                       sems[0].at[3 * a + j], sems[1].at[3 * a + j], (*chip, c))

    def _fwd(self, outs, sems, a, j, chip, half_of):
        x, y, c, me, chips = _place()
        half = outs[a].shape[1] // 2
        blk = outs[a].at[2 * chip[0] + chip[1], pl.ds(half_of * half, half), :]
        return _remote(blk, blk, sems[2].at[3 * a + j], sems[3].at[3 * a + j], (x, y, 1 - c))

    def _own(self, ins, outs, sems, a):
        return _own_shard_to_sibling(ins[a], outs[a], sems[4].at[a], sems[5].at[a])

    def start(self, ins, outs, sems):
        chips = _place()[4]
        for a in range(self.n):
            for j, chip in enumerate(chips):
                self._ici(ins, outs, sems, a, j, chip).start()
        for a in range(self.n):
            self._own(ins, outs, sems, a).start()

    def middle(self, ins, outs, sems):
        x, y, c, me, chips = _place()
        for a in range(self.n):
            for j, chip in enumerate(chips):
                half = outs[a].shape[1] // 2
                blk = outs[a].at[2 * chip[0] + chip[1], pl.ds(c * half, half), :]
                _remote(blk, blk, sems[0].at[3 * a + j], sems[1].at[3 * a + j], (x, y, c)).wait_recv()
                self._fwd(outs, sems, a, j, chip, c).start()

    def finish(self, ins, outs, sems):
        x, y, c, me, chips = _place()
        for a in range(self.n):
            for j, chip in enumerate(chips):
                self._fwd(outs, sems, a, j, chip, 1 - c).wait_recv()
        for a in range(self.n):
            for j, chip in enumerate(chips):
                self._ici(ins, outs, sems, a, j, chip).wait_send()
                self._fwd(outs, sems, a, j, chip, c).wait_send()
            self._own(ins, outs, sems, a).wait()


def _own_shard_to_sibling(shard_ref, gathered_ref, send_sem, recv_sem):
    x, y, c, me, chips = _place()
    return _remote(shard_ref, gathered_ref.at[me], send_sem, recv_sem, (x, y, 1 - c))


class _NoExchange(_Exchange):
    n_in = n_out = 0
    out_shape = ()
    scratch = ()

    def start(self, ins, outs, sems):
        pass

    def finish(self, ins, outs, sems):
        pass


class _ForwardGathered(_Exchange):
    def __init__(self, shards, own=True, forward=True):
        self.own, self.forward = own, forward
        n = self.n = len(shards)
        self.n_in, self.n_out = 2 * n, n
        self.out_shape = [jax.ShapeDtypeStruct((N_CHIP,) + s.shape, s.dtype) for s in shards]
        dma = pltpu.SemaphoreType.DMA
        self.scratch = [dma((3 * n,)), dma((3 * n,)), dma((n,)), dma((n,))]
        self.aliases = {n + a: a for a in range(n)}

    def _fwd(self, outs, sems, a, j, chip, half_of):
        x, y, c, me, chips = _place()
        half = outs[a].shape[1] // 2
        blk = outs[a].at[2 * chip[0] + chip[1], pl.ds(half_of * half, half), :]
        return _remote(blk, blk, sems[0].at[3 * a + j], sems[1].at[3 * a + j], (x, y, 1 - c))

    def _own(self, ins, outs, sems, a):
        return _own_shard_to_sibling(ins[a], outs[a], sems[2].at[a], sems[3].at[a])

    def start(self, ins, outs, sems):
        x, y, c, me, chips = _place()
        for a in range(self.n):
            for j, chip in enumerate(chips if self.forward else ()):
                self._fwd(outs, sems, a, j, chip, c).start()
        for a in range(self.n if self.own else 0):
            self._own(ins, outs, sems, a).start()

    def finish(self, ins, outs, sems):
        x, y, c, me, chips = _place()
        for a in range(self.n):
            for j, chip in enumerate(chips if self.forward else ()):
                self._fwd(outs, sems, a, j, chip, 1 - c).wait_recv()
        for a in range(self.n):
            for j, chip in enumerate(chips if self.forward else ()):
                self._fwd(outs, sems, a, j, chip, c).wait_send()
            if self.own:
                self._own(ins, outs, sems, a).wait()


HBM = pl.BlockSpec(memory_space=pltpu.HBM)
SEMS = pl.BlockSpec(memory_space=pltpu.SEMAPHORE)
DATAFLOW = pltpu.SideEffectType.DATAFLOW_SIDE_EFFECTING


class _OverIci:
    def __init__(self, name, sources, lands):
        self.name, self.n = name, len(sources)
        hbm = lambda t: pltpu.with_memory_space_constraint(t, pltpu.HBM)
        self.arrays = [hbm(t) for t in sources] + [hbm(t) for t in lands]

    def sent(self, src, land, a, chip):
        raise NotImplementedError

    def landed(self, land, a, chip):
        raise NotImplementedError

    def _copy(self, arr, sems, a, j, receiving):
        x, y, c, me, chips = _place()
        src, dst = self.sent(arr[a], arr[self.n + a], a, chips[j])
        if receiving:
            dst = self.landed(arr[self.n + a], a, chips[j])
        return _remote(src, dst, sems[0].at[3 * a + j], sems[1].at[3 * a + j], (*chips[j], c))

    def start(self, after):
        m = len(self.arrays)

        def body(*refs):
            arr, sems, token = refs[:m], refs[m + 1:m + 3], refs[-1]
            for a in range(self.n):
                for j in range(3):
                    self._copy(arr, sems, a, j, False).start()
            token[...] = jnp.zeros_like(token)

        dma = pltpu.SemaphoreType.DMA
        outs = pl.pallas_call(
            body, name=self.name + "_start",
            out_shape=[dma((3 * self.n,)), dma((3 * self.n,))] + [pltpu.HBM(t.shape, t.dtype) for t in self.arrays]
                      + [jax.ShapeDtypeStruct((8, 128), F32)],
            in_specs=[HBM] * m + [ANY], out_specs=[SEMS, SEMS] + [HBM] * m + [VMEM],
            input_output_aliases={i: 2 + i for i in range(m)},
            compiler_params=pltpu.CompilerParams(has_side_effects=DATAFLOW),
        )(*self.arrays, after)
        self.sems, self.arrays = outs[0:2], list(outs[2:2 + m])
        return outs[-1]

    def wait(self, after):
        m = len(self.arrays)

        def body(*refs):
            arr, sems = refs[:m], refs[m:m + 2]
            for a in range(self.n):
                for j in range(3):
                    self._copy(arr, sems, a, j, False).wait_send()
                    self._copy(arr, sems, a, j, True).wait_recv()

        outs = pl.pallas_call(
            body, name=self.name + "_wait",
            out_shape=[pltpu.HBM(t.shape, t.dtype) for t in self.arrays],
            in_specs=[HBM] * m + [SEMS, SEMS, ANY], out_specs=[HBM] * m,
            input_output_aliases={i: i for i in range(m)},
            compiler_params=pltpu.CompilerParams(has_side_effects=DATAFLOW),
        )(*self.arrays, *self.sems, after)
        return list(outs[:self.n]), list(outs[self.n:])


class _GatherOverIci(_OverIci):
    def __init__(self, name, shards):
        super().__init__(name, shards, [lax.empty((N_CHIP,) + s.shape, s.dtype) for s in shards])

    @staticmethod
    def _half(ref):
        c = lax.axis_index("c")
        half = ref.shape[-2] // 2
        return pl.ds(c * half, half)

    def sent(self, src, land, a, chip):
        return src.at[self._half(src), :], land.at[_place()[3], self._half(src), :]

    def landed(self, land, a, chip):
        return land.at[2 * chip[0] + chip[1], self._half(land), :]


class _SumOverIci(_OverIci):
    def __init__(self, name, pre):
        super().__init__(name, pre, [lax.empty(p.shape, p.dtype) for p in pre])

    def sent(self, src, land, a, chip):
        return src.at[2 * chip[0] + chip[1]], land.at[_place()[3]]

    def landed(self, land, a, chip):
        return land.at[2 * chip[0] + chip[1]]


class _HalvesToSibling(_Exchange):
    def __init__(self, grads):
        n = self.n = len(grads)
        self.n_in = self.n_out = n
        self.out_shape = [jax.ShapeDtypeStruct((N_CHIP, g.shape[1] // 2, g.shape[2]), g.dtype) for g in grads]
        self.scratch = [pltpu.SemaphoreType.DMA((n,)), pltpu.SemaphoreType.DMA((n,))]

    def _copy(self, ins, outs, sems, a):
        x, y, c, me, chips = _place()
        half = ins[a].shape[1] // 2
        return _remote(ins[a].at[:, pl.ds((1 - c) * half, half), :], outs[a], sems[0].at[a], sems[1].at[a], (x, y, 1 - c))

    def start(self, ins, outs, sems):
        for a in range(self.n):
            self._copy(ins, outs, sems, a).start()

    def finish(self, ins, outs, sems):
        for a in range(self.n):
            self._copy(ins, outs, sems, a).wait_recv()
        for a in range(self.n):
            self._copy(ins, outs, sems, a).wait_send()


class _ShareHalves(_Exchange):
    def __init__(self, fulls):
        n = self.n = len(fulls)
        self.n_in = self.n_out = n
        self.out_shape = [jax.ShapeDtypeStruct(f.shape, f.dtype) for f in fulls]
        self.scratch = [pltpu.SemaphoreType.DMA((n,)), pltpu.SemaphoreType.DMA((n,))]
        self.aliases = {a: a for a in range(n)}

    def _copy(self, outs, sems, a, half_of):
        x, y, c, me, chips = _place()
        half = outs[a].shape[0] // 2
        rows = outs[a].at[pl.ds(half_of * half, half), :]
        return _remote(rows, rows, sems[0].at[a], sems[1].at[a], (x, y, 1 - c))

    def start(self, ins, outs, sems):
        c = _place()[2]
        for a in range(self.n):
            self._copy(outs, sems, a, c).start()

    def finish(self, ins, outs, sems):
        c = _place()[2]
        for a in range(self.n):
            self._copy(outs, sems, a, 1 - c).wait_recv()
        for a in range(self.n):
            self._copy(outs, sems, a, c).wait_send()


class _GatherBlocks(_Exchange):
    def __init__(self, block):
        self.n_in = self.n_out = 1
        self.out_shape = [jax.ShapeDtypeStruct((8,) + block.shape, block.dtype)]
        dma = pltpu.SemaphoreType.DMA
        self.scratch = [dma((7,)), dma((7,)), dma]

    @staticmethod
    def _peer(f):
        x, y, c, me, chips = _place()
        return ((1 - x) if f & 4 else x, (1 - y) if f & 2 else y, (1 - c) if f & 1 else c)

    def start(self, ins, outs, sems):
        x, y, c, me, chips = _place()
        for f in range(1, 8):
            _remote(ins[0], outs[0].at[2 * me + c], sems[0].at[f - 1], sems[1].at[f - 1], self._peer(f)).start()
        pltpu.make_async_copy(ins[0], outs[0].at[2 * me + c], sems[2]).start()

    def finish(self, ins, outs, sems):
        x, y, c, me, chips = _place()
        for f in range(1, 8):
            px, py, pc = self._peer(f)
            blk = outs[0].at[4 * px + 2 * py + pc]
            _remote(blk, blk, sems[0].at[f - 1], sems[1].at[f - 1], (x, y, c)).wait_recv()
        for f in range(1, 8):
            _remote(ins[0], outs[0].at[2 * me + c], sems[0].at[f - 1], sems[1].at[f - 1], self._peer(f)).wait_send()
        pltpu.make_async_copy(ins[0], outs[0].at[2 * me + c], sems[2]).wait()


class _Both(_Exchange):
    def __init__(self, first, second):
        self.parts = (first, second)
        self.n_in, self.n_out = first.n_in + second.n_in, first.n_out + second.n_out
        self.out_shape = first.out_shape + second.out_shape
        self.scratch = first.scratch + second.scratch
        self.aliases = dict(first.aliases)
        self.aliases.update({first.n_in + i: first.n_out + o for i, o in second.aliases.items()})

    def _split(self, ins, outs, sems):
        a, b = self.parts
        return ((a, ins[:a.n_in], outs[:a.n_out], sems[:len(a.scratch)]),
                (b, ins[a.n_in:], outs[a.n_out:], sems[len(a.scratch):]))

    def start(self, ins, outs, sems):
        for ex, i, o, s in self._split(ins, outs, sems):
            ex.start(i, o, s)

    def middle(self, ins, outs, sems):
        for ex, i, o, s in self._split(ins, outs, sems):
            ex.middle(i, o, s)

    def finish(self, ins, outs, sems):
        for ex, i, o, s in self._split(ins, outs, sems):
            ex.finish(i, o, s)


class _Bound:
    def __init__(self, ex, ins, outs, sems):
        self.start = lambda: ex.start(ins, outs, sems)
        self.middle = lambda: ex.middle(ins, outs, sems)
        self.finish = lambda: ex.finish(ins, outs, sems)


def _carry(name, body, ex, ex_args, args, in_specs, out_specs, out_shape, scratch_shapes=(), grid=None, semantics=(),
           after=None):
    n_a, n_o, n_s = len(args), len(out_shape), len(scratch_shapes)
    behind = [] if after is None else [after]

    def full_body(*refs):
        p = 0
        groups = []
        for size in (n_a, ex.n_in, len(behind), n_o, ex.n_out, n_s, len(ex.scratch)):
            groups.append(refs[p:p + size])
            p += size
        a, ei, _, o, eo, s, es = groups
        body(*a, *o, *s, _Bound(ex, ei, eo, es))

    kwargs = {} if grid is None else {"grid": grid}
    outs = pl.pallas_call(
        full_body, name=name,
        in_specs=list(in_specs) + [ANY] * (ex.n_in + len(behind)), out_specs=list(out_specs) + [ANY] * ex.n_out,
        out_shape=list(out_shape) + list(ex.out_shape), scratch_shapes=list(scratch_shapes) + list(ex.scratch),
        input_output_aliases={n_a + i: n_o + o for i, o in ex.aliases.items()},
        compiler_params=_params(*semantics) if semantics else pltpu.CompilerParams(vmem_limit_bytes=VMEM_LIMIT),
        **kwargs,
    )(*args, *ex_args, *behind)
    return outs[:n_o], outs[n_o:]


def _prepare_carrying(name, x, g1, pos, ifc, spread, arrays, ex, ex_args):
    n = len(arrays)
    r, cc = arrays[0].shape
    steps = 4
    tr, tm = r // steps, S // steps

    def body(x_ref, g_ref, pos_ref, ifc_ref, e_ref, *refs):
        src, h_ref, cos_ref, sin_ref, dst, xc = refs[:n], refs[n], refs[n + 1], refs[n + 2], refs[n + 3:2 * n + 3], refs[-1]

        @pl.when(pl.program_id(0) == 0)
        def _():
            xc.start()

        xv = x_ref[...]
        h_ref[...] = (xv * _rstd(xv) * g_ref[...]).astype(BF16)
        ang = pos_ref[...].astype(F32) * ifc_ref[...]
        cos_ref[...] = _spread_exact(jnp.cos(ang), e_ref[...])
        sin_ref[...] = _spread_exact(jnp.sin(ang), e_ref[...])
        for a in range(n):
            dst[a][...] = src[a][...].astype(BF16)

        @pl.when(pl.program_id(0) == steps - 1)
        def _():
            xc.middle()
            xc.finish()

    row = lambda w: pl.BlockSpec((tm, w), lambda i: (i, 0))
    const = lambda w: pl.BlockSpec((1, w), lambda i: (0, 0))
    blk = pl.BlockSpec((tr, cc), lambda i: (i, 0))
    return _carry(name, body, ex, ex_args, (x, g1, pos, ifc, spread, *arrays),
                  [row(D), const(D), row(1), const(128), pl.BlockSpec((128, 768), lambda i: (0, 0))] + [blk] * n,
                  [row(D), row(768), row(768)] + [blk] * n,
                  [jax.ShapeDtypeStruct((S, D), BF16)] + [jax.ShapeDtypeStruct((S, 768), F32)] * 2
                  + [jax.ShapeDtypeStruct((r, cc), BF16)] * n,
                  grid=(steps,), semantics=("arbitrary",))


def _exchange_alone(name, ex, ex_args):
    def body(xc):
        xc.start()
        xc.middle()
        xc.finish()

    return _carry(name, body, ex, ex_args, (), (), (), ())[1]


def _core_index():
    return lax.axis_index("c").astype(jnp.int32).reshape(1)


def _pair_sum(gs, gots):
    n = len(gs)
    _, r, cc = gs[0].shape
    half = r // 2

    def body(c_ref, *refs):
        for a in range(n):
            refs[2 * n + a][...] = (refs[a][...].astype(F32) + refs[n + a][...].astype(F32)).astype(BF16)

    mine = pl.BlockSpec((None, half, cc), lambda k, c_ref: (k, c_ref[0], 0))
    blk = pl.BlockSpec((None, half, cc), lambda k, c_ref: (k, 0, 0))
    return pl.pallas_call(
        body, name=f"pair_sum_{r}x{cc}",
        grid_spec=pltpu.PrefetchScalarGridSpec(
            num_scalar_prefetch=1, grid=(N_CHIP,), in_specs=[mine] * n + [blk] * n, out_specs=[blk] * n),
        out_shape=[jax.ShapeDtypeStruct((N_CHIP, half, cc), BF16)] * n,
        compiler_params=_params("parallel"),
    )(_core_index(), *gs, *gots)


def _chip_sum(pre, parts):
    n = len(parts)
    _, half, cc = parts[0].shape
    tr = half // 2
    me = 2 * lax.axis_index("x") + lax.axis_index("y")
    others = [k + (k >= me).astype(jnp.int32) for k in range(3)]
    where = jnp.stack([lax.axis_index("c"), me, *others]).astype(jnp.int32)

    def body(w_ref, *refs):
        for a in range(n):
            own, p1, p2, p3 = refs[4 * a:4 * a + 4]
            refs[4 * n + a][...] = ((own[...].astype(F32) + p1[...].astype(F32)) + p2[...].astype(F32)) + p3[...].astype(F32)

    slot = lambda s: pl.BlockSpec((None, tr, cc), lambda i, w_ref: (w_ref[s], i, 0))
    operands = []
    for a in range(n):
        operands += [pre[a], parts[a], parts[a], parts[a]]
    return pl.pallas_call(
        body, name=f"chip_sum_{half}x{cc}",
        grid_spec=pltpu.PrefetchScalarGridSpec(
            num_scalar_prefetch=1, grid=(2,),
            in_specs=[slot(1), slot(2), slot(3), slot(4)] * n,
            out_specs=[pl.BlockSpec((tr, cc), lambda i, w_ref: (2 * w_ref[0] + i, 0))] * n),
        out_shape=[jax.ShapeDtypeStruct((2 * half, cc), F32)] * n,
        compiler_params=_params("parallel"),
    )(where, *operands)


def _adamw_math(w, g, m, v):
    m = ADAM_B1 * m + (1.0 - ADAM_B1) * g
    v = ADAM_B2 * v + (1.0 - ADAM_B2) * (g * g)
    m_hat = m / (1.0 - ADAM_B1 ** ADAM_STEP)
    v_hat = v / (1.0 - ADAM_B2 ** ADAM_STEP)
    delta = -ADAM_LR * (m_hat / (jnp.sqrt(v_hat) + ADAM_EPS) + ADAM_WD * w)
    return delta, m, v


def _adamw(w, g, m, v, after=None):
    r, cc = w.shape
    tr = r // 4

    def body(w_ref, g_ref, m_ref, v_ref, go_ref, d_ref, nm_ref, nv_ref, _):
        g = g_ref[...]
        go_ref[...] = g
        d_ref[...], nm_ref[...], nv_ref[...] = _adamw_math(w_ref[...], g, m_ref[...], v_ref[...])

    blk = pl.BlockSpec((tr, cc), lambda i: (i, 0))
    return _carry(f"adamw_{r}x{cc}", body, _NoExchange(), (), (w, g, m, v), [blk] * 4, [blk] * 4,
                  [jax.ShapeDtypeStruct((r, cc), F32)] * 4, grid=(4,), semantics=("parallel",), after=after)[0]


def _pack8(rows):
    def body(*refs):
        out_ref = refs[-1]
        out_ref[...] = jnp.zeros_like(out_ref)
        for i, r in enumerate(refs[:-1]):
            out_ref[i:i + 1, :] = r[...]

    return pl.pallas_call(body, name="pack8", out_shape=jax.ShapeDtypeStruct((8, D), F32))(*rows)


def _adamw_gains(gall, ws, ms, vs):
    def body(ga_ref, *refs):
        w, m, v = refs[0:4], refs[4:8], refs[8:12]
        outs, loss_ref, total = refs[12:28], refs[28], refs[29]
        g = ga_ref[0]
        for dev in range(1, 8):
            g = g + ga_ref[dev]
        total[...] = g
        for i in range(4):
            gi = total[i:i + 1, :]
            outs[i][...] = gi
            outs[4 + i][...], outs[8 + i][...], outs[12 + i][...] = _adamw_math(w[i][...], gi, m[i][...], v[i][...])
        loss_ref[...] = total[4:5, 0:128] * (0.5 / D)

    outs = pl.pallas_call(
        body, name="adamw_gains",
        out_shape=[jax.ShapeDtypeStruct((1, D), F32)] * 16 + [jax.ShapeDtypeStruct((1, 128), F32)],
        scratch_shapes=[pltpu.VMEM((8, D), F32)],
    )(gall, *ws, *ms, *vs)
    return outs[0:4], outs[4:8], outs[8:12], outs[12:16], outs[16]


def kernel(x, positions, w_in, w_out, g_pre_mix, g_post_mix, g_pre_ffn, g_post_ffn, w_gate, w_up, w_down, loss_target, m_w_in, m_w_out, m_g_pre_mix, m_g_post_mix, m_g_pre_ffn, m_g_post_ffn, m_w_gate, m_w_up, m_w_down, v_w_in, v_w_out, v_g_pre_mix, v_g_post_mix, v_g_pre_ffn, v_g_post_ffn, v_w_gate, v_w_up, v_w_down):
    tr = lambda t: jnp.swapaxes(t, 1, 2)[0]
    shards = [w_in[0], w_out[0], tr(w_gate), tr(w_up), w_down[0]]
    moms = [m_w_in[0], m_w_out[0], tr(m_w_gate), tr(m_w_up), m_w_down[0]]
    vels = [v_w_in[0], v_w_out[0], tr(v_w_gate), tr(v_w_up), v_w_down[0]]
    xs, pos, tgt = x[0], positions.reshape(S, 1), loss_target[0]
    g1, g2, g3, g4 = g_pre_mix, g_post_mix, g_pre_ffn, g_post_ffn
    tabs = tuple(jnp.asarray(t) for t in _retention_tables())
    ifc, spread = _rotary_tables()
    ifc, spread = jnp.asarray(ifc), jnp.asarray(spread, dtype=BF16)
    bf = [s.astype(BF16) for s in shards[:2]]

    (h1, cos, sin, *ffn_bf), (win_g,) = _prepare_carrying(
        "gather_in", xs, g1, pos, ifc, spread, shards[2:], _GatherShards(bf[:1]), bf[:1])
    bf += list(ffn_bf)
    wout_gather = _GatherOverIci("wout_gather", bf[1:2])
    token = wout_gather.start(win_g)
    ffn_gather = _GatherOverIci("ffn_gather", bf[2:])
    token = ffn_gather.start(token)
    qr, kr, rv, rg, aq, ak, av = _proj_fwd(h1, win_g, cos, sin, token)
    (o_raw, cat_r, states), _ = _ret_fwd(qr, kr, rv, rg, tabs, _NoExchange(), ())
    wout_sh, wout_land = wout_gather.wait(qr)
    n_ffn = len(bf[2:])
    (att_out, lse, cat_a), (wout_g, *ffn_gather.arrays[n_ffn:]) = _att_fwd(
        aq, ak, av, _Both(_ForwardGathered(bf[1:2]), _ForwardGathered(bf[2:], forward=False)),
        [*wout_sh, *wout_land, *ffn_gather.arrays])
    wout_g = wout_g.reshape(D, D)
    ffn_sh, ffn_lands = ffn_gather.wait(cat_a)
    (mix, x2, h3), (wg_g, wu_g, wd_g) = _mix_fwd(cat_r, cat_a, wout_g, xs, g2, g3,
                                                _ForwardGathered(bf[2:], own=False), [*ffn_sh, *ffn_lands])
    gt, up, a, sq, dy, df, dg4 = _ffn_fwd(h3, wg_g, wu_g, wd_g, x2, tgt, g4)

    dgt, dup, dx2, dmix, dg3, dg2 = _ffn_bwd_act(df, gt, up, wg_g, wu_g, wd_g, dy, x2, mix, g2, g3)
    ffn_grads = list(_ffn_bwd_w(a, df, h3, dgt, dup))
    (dret, datt, dwout), got = _mix_bwd(dmix, cat_r, cat_a, wout_g, _HalvesToSibling(ffn_grads), ffn_grads)
    ffn_sum = _SumOverIci("ffn_sum", _pair_sum(ffn_grads, got))
    token = ffn_sum.start(datt)
    (dq_att, dk_att, dv_att), _ = _att_bwd(aq, ak, av, datt, att_out, lse, _NoExchange(), (), token)
    (dqr, dkr, drv, drg), _ = _ret_bwd(qr, kr, rv, rg, o_raw, states, dret, tabs, _NoExchange(), (), token)
    dproj = _rot_bwd(cos, sin, dqr, dkr, drv, drg, dq_att, dk_att, dv_att)
    sums = _chip_sum(*ffn_sum.wait(dproj))
    dwin, ffn_full = _win_bwd_w(h1, dproj, _ShareHalves(sums), sums)
    in_grads = [dwin, dwout.reshape(N_CHIP, WOUT_R, D)]

    got = _exchange_alone("halves_to_sibling", _HalvesToSibling(in_grads), in_grads)
    in_sum = _SumOverIci("in_sum", [*_pair_sum(in_grads[:1], got[:1]), *_pair_sum(in_grads[1:], got[1:])])
    token = in_sum.start(dproj)
    dx, dg1 = _in_bwd(dproj, win_g, xs, dx2, g1, token)
    ffn_upd = [_adamw(shards[2 + i], ffn_full[o], moms[2 + i], vels[2 + i], token)
               for i, o in enumerate((1, 2, 0))]
    pre, parts = in_sum.wait(ffn_upd[2][0])
    sums = [*_chip_sum(pre[:1], parts[:1]), *_chip_sum(pre[1:], parts[1:])]
    gblock = _pack8([dg1, dg2, dg3, dg4, sq])
    *in_full, gall = _exchange_alone("share_rest", _Both(_ShareHalves(sums), _GatherBlocks(gblock)), [*sums, gblock])
    upd = [_adamw(w, g, m, v) for w, g, m, v in zip(shards[:2], in_full, moms[:2], vels[:2])] + ffn_upd
    gg, gd, gm, gv, loss_row = _adamw_gains(gall, [g1, g2, g3, g4],
                                            [m_g_pre_mix, m_g_post_mix, m_g_pre_ffn, m_g_post_ffn],
                                            [v_g_pre_mix, v_g_post_mix, v_g_pre_ffn, v_g_post_ffn])

    def order(mats, vecs):
        back = lambda t: jnp.swapaxes(t[None], 1, 2)
        return [mats[0][None], mats[1][None], *vecs, back(mats[2]), back(mats[3]), mats[4][None]]

    return (loss_row[0, 0], dx[None],
            *order([u[0] for u in upd], gg),
            *order([u[1] for u in upd], gd),
            *order([u[2] for u in upd], gm),
            *order([u[3] for u in upd], gv))
```

```python
import numpy as np
import jax
import jax.numpy as jnp
from jax import lax
from jax.experimental import pallas as pl
from jax.experimental.pallas import tpu as pltpu

F32, BF16 = jnp.float32, jnp.bfloat16
MESH = pl.DeviceIdType.MESH

S = 2048
D = 1024
PW = 3072
N_CHIP = 4
WIN_C = PW // N_CHIP
DFF = 2816
FF_C = DFF // N_CHIP
WOUT_R = D // N_CHIP
RMS_EPS = 1e-6
GN_EPS = 1e-5
RET_C = 128
RET_SCALE = 32 ** -0.5
ATT_BLK = 128
ATT_SCALE = 64 ** -0.5
PATTERN_DILATIONS = (1, 4, 16)
NEG = -1e30
VMEM_LIMIT = 56 * 1024 * 1024

ADAM_LR, ADAM_B1, ADAM_B2, ADAM_EPS, ADAM_WD, ADAM_STEP = 0.001, 0.9, 0.999, 1e-08, 0.01, 10


def _params(*sem):
    return pltpu.CompilerParams(dimension_semantics=sem, vmem_limit_bytes=VMEM_LIMIT)


def _nt(a, b):
    return lax.dot_general(a, b, (((1,), (1,)), ((), ())), preferred_element_type=F32)


def _tn(a, b):
    return lax.dot_general(a, b, (((0,), (0,)), ((), ())), preferred_element_type=F32)


def _nn(a, b):
    return jnp.dot(a, b, preferred_element_type=F32)


def _rstd(v):
    return lax.rsqrt(jnp.mean(v * v, axis=-1, keepdims=True) + RMS_EPS)


def _sigmoid(v):
    return 1.0 / (1.0 + jnp.exp(-v))


def _rows(i, t):
    return pl.ds(pl.multiple_of(i * t, t), t)


def _retention_tables():
    h = np.arange(8, dtype=np.float32)
    log_g = np.log1p(-np.exp2(-5.0 - h)).astype(np.float32)
    idx = np.arange(RET_C, dtype=np.float32)
    diff = idx[:, None] - idx[None, :]
    dtab = np.where(diff >= 0, np.exp(log_g[:, None, None] * np.maximum(diff, 0.0)), 0.0).astype(np.float32)
    dtab = dtab.reshape(8 * RET_C, RET_C)
    lane_head = np.arange(256) // 32
    a_tab = np.exp(log_g[lane_head][None, :] * (idx + 1.0)[:, None]).astype(np.float32)
    b_tab = np.exp(log_g[lane_head][None, :] * (RET_C - 1.0 - idx)[:, None]).astype(np.float32)
    lam = np.exp(log_g[lane_head] * RET_C).astype(np.float32)[:, None]
    bd = (lane_head[:, None] == (np.arange(512) // 64)[None, :]).astype(np.float32)
    return dtab, a_tab, b_tab, lam, bd


def _rotary_tables():
    inv_r = (1.0 / (np.float32(10000.0) ** np.linspace(0.0, 1.0, 16, dtype=np.float32))).astype(np.float32)
    inv_a = (np.float32(500000.0) ** (-np.arange(0, 16, 2, dtype=np.float32) / np.float32(16))).astype(np.float32)
    ifc = np.zeros((1, 128), np.float32)
    ifc[0, 0:16], ifc[0, 16:24] = inv_r, inv_a
    spread = np.zeros((128, 768), np.float32)
    for lane in range(256):
        spread[(lane % 32) % 16, lane] = 1.0
    for lane in range(512):
        d = lane % 64
        spread[16 + d % 8 if d < 16 else 24, 256 + lane] = 1.0
    return ifc, spread


def _rot_halves(tm):
    lo_r = (lax.broadcasted_iota(jnp.int32, (tm, 256), 1) % 32) < 16
    lo_a = (lax.broadcasted_iota(jnp.int32, (tm, 512), 1) % 64) < 8
    return lo_r, lo_a


def _spread_exact(t, e):
    hi = t.astype(BF16)
    r1 = t - hi.astype(F32)
    mid = r1.astype(BF16)
    lo = (r1 - mid.astype(F32)).astype(BF16)
    return _nn(hi, e) + _nn(mid, e) + _nn(lo, e)


def _proj_fwd(h1, win_g, cos, sin, after):
    tm = 256

    def body(h_ref, w_ref, cos_ref, sin_ref, qr_ref, kr_ref, rv_ref, rg_ref, aq_ref, ak_ref, av_ref, p_ref, _):
        h = h_ref[...]
        for k in range(N_CHIP):
            p_ref[:, k * WIN_C:(k + 1) * WIN_C] = _nn(h, w_ref[k])
        cr, ca, sr, sa = cos_ref[:, 0:256], cos_ref[:, 256:768], sin_ref[:, 0:256], sin_ref[:, 256:768]
        lo_r, lo_a = _rot_halves(tm)

        def rot_r(v):
            return v * cr + sr * jnp.where(lo_r, -pltpu.roll(v, 240, 1), pltpu.roll(v, 16, 1))

        def rot_a(v):
            return v * ca + sa * jnp.where(lo_a, -pltpu.roll(v, 504, 1), pltpu.roll(v, 8, 1))

        qr_ref[...] = rot_r(p_ref[:, 0:256]).astype(BF16)
        kr_ref[...] = (rot_r(p_ref[:, 256:512]) * RET_SCALE).astype(BF16)
        rv_ref[...] = p_ref[:, 512:1024].astype(BF16)
        rg_ref[...] = p_ref[:, 1024:1536]
        aq, ak = rot_a(p_ref[:, 1536:2048]), rot_a(p_ref[:, 2048:2560])
        for j in range(4):
            aq_ref[j] = aq[:, 128 * j:128 * j + 128]
            ak_ref[j] = ak[:, 128 * j:128 * j + 128]
            av_ref[j] = p_ref[:, 2560 + 128 * j:2560 + 128 * j + 128]

    row = lambda w: pl.BlockSpec((tm, w), lambda i: (i, 0))
    slab = pl.BlockSpec((4, tm, 128), lambda i: (0, i, 0))
    return _carry(
        "proj_fwd", body, _NoExchange(), (), (h1, win_g, cos, sin),
        [row(D), pl.BlockSpec((N_CHIP, D, WIN_C), lambda i: (0, 0, 0)), row(768), row(768)],
        [row(256), row(256), row(512), row(512), slab, slab, slab],
        [jax.ShapeDtypeStruct((S, w), BF16) for w in (256, 256, 512)]
        + [jax.ShapeDtypeStruct((S, 512), F32)] + [jax.ShapeDtypeStruct((4, S, 128), F32)] * 3,
        scratch_shapes=[pltpu.VMEM((tm, PW), F32)], grid=(S // tm,), semantics=("parallel",), after=after)[0]


def _seg_mean(v):
    lo = lax.broadcasted_iota(jnp.int32, v.shape, 1) < 64
    s_lo = jnp.sum(jnp.where(lo, v, 0.0), axis=-1, keepdims=True)
    s_hi = jnp.sum(jnp.where(lo, 0.0, v), axis=-1, keepdims=True)
    return jnp.where(lo, s_lo, s_hi) * (1.0 / 64.0)


def _ret_fwd(qr, kr, rv, proj, tabs, exchange, exchange_args, after=None):
    C = RET_C
    dtab, a_tab, b_tab, lam, bd = tabs

    def body(q_ref, k_ref, v_ref, g_ref, dt_ref, a_ref, b_ref, lam_ref, bd_ref, o_ref, cat_ref, st_ref, R, exch):
        @pl.when(pl.program_id(0) == 0)
        def _():
            exch.start()
            R[...] = jnp.zeros_like(R)

        @pl.when(pl.program_id(0) == S // C // 2)
        def _():
            exch.middle()

        q, k, v = q_ref[...], k_ref[...], v_ref[...]
        lane_head = lax.broadcasted_iota(jnp.int32, (C, 256), 1) // 32
        col_head = lax.broadcasted_iota(jnp.int32, (C, 256), 1) // 64
        rb = R[...].astype(BF16)
        st_ref[...] = rb
        qa = (q.astype(F32) * a_ref[...]).astype(BF16)
        cross = _nn(qa, rb)
        p = (_nt(_stack_heads(q, lane_head, n=8), k) * dt_ref[...]).astype(BF16)
        og = [cross[:, 256 * g:256 * g + 256]
              + _unstack_heads(_nn(p[4 * C * g:4 * C * (g + 1)], v[:, 256 * g:256 * g + 256]), col_head)
              for g in range(2)]
        kb = (k.astype(F32) * b_ref[...]).astype(BF16)
        R[...] = R[...] * lam_ref[...] + _tn(kb, v) * bd_ref[...]
        o_ref[:, 0:256] = og[0]
        o_ref[:, 256:512] = og[1]
        for j in range(4):
            oj = og[j // 2][:, 128 * (j % 2):128 * (j % 2) + 128]
            xc = oj - _seg_mean(oj)
            rn = xc * lax.rsqrt(_seg_mean(xc * xc) + GN_EPS)
            gj = g_ref[:, 128 * j:128 * j + 128]
            cat_ref[:, 128 * j:128 * j + 128] = (rn * (gj * _sigmoid(gj))).astype(BF16)

        @pl.when(pl.program_id(0) == S // C - 1)
        def _():
            exch.finish()

    row = lambda w: pl.BlockSpec((C, w), lambda n: (n, 0))
    full = lambda a: pl.BlockSpec(a.shape, lambda n: (0,) * a.ndim)
    return _carry(
        "ret_fwd", body, exchange, exchange_args, (qr, kr, rv, proj, dtab, a_tab, b_tab, lam, bd),
        [row(256), row(256), row(512), row(512),
         full(dtab), full(a_tab), full(b_tab), full(lam), full(bd)],
        [row(512), row(512), pl.BlockSpec((None, 256, 512), lambda n: (n, 0, 0))],
        [jax.ShapeDtypeStruct((S, 512), F32), jax.ShapeDtypeStruct((S, 512), BF16),
         jax.ShapeDtypeStruct((S // C, 256, 512), BF16)],
        scratch_shapes=[pltpu.VMEM((256, 512), F32)], grid=(S // C,), semantics=("arbitrary",), after=after)


def _stack_heads(v, lane_head, fill=0.0, n=4):
    return jnp.concatenate([jnp.where(lane_head == h, v, jnp.full_like(v, fill)) for h in range(n)], axis=0)


def _unstack_heads(v, lane_head, n=4):
    out = v[0:ATT_BLK]
    for h in range(1, n):
        out = jnp.where(lane_head == h, v[h * ATT_BLK:(h + 1) * ATT_BLK], out)
    return out


def _att_bias(has_prev):
    nk = 2 * ATT_BLK if has_prev else ATT_BLK
    a = lax.broadcasted_iota(jnp.int32, (4 * ATT_BLK, nk), 0) % ATT_BLK
    kk = lax.broadcasted_iota(jnp.int32, (4 * ATT_BLK, nk), 1)
    if not has_prev:
        return None, jnp.where((a - kk) >= 0, 0.0, NEG)
    dist = ATT_BLK + a - kk
    inside = (dist >= 0) & (dist <= ATT_BLK)
    return jnp.where(inside, 0.0, NEG), jnp.where(inside & (kk >= ATT_BLK), 0.0, NEG)


def _class_rows(ib, r, d):
    if d == 1:
        return pl.ds(pl.multiple_of(ib * ATT_BLK, ATT_BLK), ATT_BLK)
    return pl.ds(ib * ATT_BLK * d + r, ATT_BLK, stride=d)


def _slab_pair(ref, g, rows):
    return jnp.concatenate([ref[2 * g, rows, :], ref[2 * g + 1, rows, :]], axis=1)


def _att_blocks(d):
    nb = S // d // ATT_BLK
    return nb, nb > 1


def _att_fwd(aq, ak, av, exchange, exchange_args):
    def body(q_ref, k_ref, v_ref, o_ref, l_ref, cat_ref, xc):
        xc.start()
        lane_head = lax.broadcasted_iota(jnp.int32, (ATT_BLK, 256), 1) // 64
        for pi, d in enumerate(PATTERN_DILATIONS):
            if pi == len(PATTERN_DILATIONS) - 1:
                xc.middle()
            nb, has_prev = _att_blocks(d)
            bias_rest, bias_first = _att_bias(has_prev)

            def block(b, carry, pi=pi, d=d, nb=nb, has_prev=has_prev, bias_rest=bias_rest, bias_first=bias_first):
                r, ib = b // nb, b % nb
                rows = _class_rows(ib, r, d)
                prow = _class_rows(jnp.maximum(ib - 1, 0), r, d)
                bias = jnp.where(ib == 0, bias_first, bias_rest) if has_prev else bias_first
                for g in range(2):
                    qg = _slab_pair(q_ref, g, rows).astype(BF16)
                    kg = _slab_pair(k_ref, g, rows)
                    vg = _slab_pair(v_ref, g, rows)
                    if has_prev:
                        kg = jnp.concatenate([_slab_pair(k_ref, g, prow), kg], axis=0)
                        vg = jnp.concatenate([_slab_pair(v_ref, g, prow), vg], axis=0)
                    kg, vg = kg.astype(BF16), vg.astype(BF16)
                    s = _nt(_stack_heads(qg, lane_head), kg) * ATT_SCALE + bias
                    m = jnp.max(s, axis=-1, keepdims=True)
                    p = jnp.exp(s - m)
                    den = jnp.sum(p, axis=-1, keepdims=True)
                    og = _unstack_heads(_nn(p.astype(BF16), vg) / den, lane_head)
                    lg = _unstack_heads(jnp.broadcast_to(m + jnp.log(den), (4 * ATT_BLK, 256)), lane_head)
                    for jj in range(2):
                        j = 2 * g + jj
                        o_new, l_new = og[:, 128 * jj:128 * jj + 128], lg[:, 128 * jj:128 * jj + 128]
                        if pi > 0:
                            o_old, l_old = o_ref[j, rows, :], l_ref[j, rows, :]
                            mx = jnp.maximum(l_old, l_new)
                            ea, eb = jnp.exp(l_old - mx), jnp.exp(l_new - mx)
                            den = ea + eb
                            o_new = (ea * o_old + eb * o_new) / den
                            l_new = mx + jnp.log(den)
                        o_ref[j, rows, :] = o_new
                        l_ref[j, rows, :] = l_new
                return carry

            lax.fori_loop(0, S // ATT_BLK, block, 0, unroll=4)

        def to_cat(i, carry):
            rows = _rows(i, 256)
            for j in range(4):
                cat_ref[rows, 128 * j:128 * j + 128] = o_ref[j, rows, :].astype(BF16)
            return carry

        lax.fori_loop(0, S // 256, to_cat, 0)
        xc.finish()

    slab = jax.ShapeDtypeStruct((4, S, 128), F32)
    return _carry("att_fwd", body, exchange, exchange_args, (aq, ak, av), [VMEM] * 3, [VMEM] * 3,
                  [slab, slab, jax.ShapeDtypeStruct((S, 512), BF16)])


def _mix_fwd(cat_r, cat_a, wout, x, g2, g3, exchange, exchange_args):
    tm = 512

    def body(cr_ref, ca_ref, w_ref, x_ref, g2_ref, g3_ref, mix_ref, x2_ref, h3_ref, xc):
        @pl.when(pl.program_id(0) == 0)
        def _():
            xc.start()

        mix = _nn(cr_ref[...], w_ref[0:512, :]) + _nn(ca_ref[...], w_ref[512:1024, :])
        mix_ref[...] = mix
        x2 = x_ref[...] + mix * _rstd(mix) * g2_ref[...]
        x2_ref[...] = x2
        h3_ref[...] = (x2 * _rstd(x2) * g3_ref[...]).astype(BF16)

        @pl.when(pl.program_id(0) == S // tm - 1)
        def _():
            xc.middle()
            xc.finish()

    row = lambda w: pl.BlockSpec((tm, w), lambda i: (i, 0))
    vec = pl.BlockSpec((1, D), lambda i: (0, 0))
    return _carry("mix_fwd", body, exchange, exchange_args, (cat_r, cat_a, wout, x, g2, g3),
                  [row(512), row(512), pl.BlockSpec((D, D), lambda i: (0, 0)), row(D), vec, vec],
                  [row(D), row(D), row(D)],
                  [jax.ShapeDtypeStruct((S, D), F32), jax.ShapeDtypeStruct((S, D), F32),
                   jax.ShapeDtypeStruct((S, D), BF16)],
                  grid=(S // tm,), semantics=("arbitrary",))


def _ffn_fwd(h3, wg, wu, wd, x2, tgt, g4):
    tm = 512
    last = N_CHIP - 1

    def body(h_ref, wg_ref, wu_ref, wd_ref, x2_ref, t_ref, g_ref,
             gt_ref, up_ref, a_ref, loss_ref, dy_ref, df_ref, dg_ref, f_ref):
        k, i = pl.program_id(0), pl.program_id(1)
        h = h_ref[...]
        gt = _nt(h, wg_ref[...])
        up = _nt(h, wu_ref[...])
        gt_ref[...] = gt.astype(BF16)
        up_ref[...] = up.astype(BF16)
        a = (gt * _sigmoid(gt) * up).astype(BF16)
        a_ref[...] = a
        part = _nn(a, wd_ref[...])
        rows = _rows(i, tm)

        @pl.when(k == 0)
        def _():
            f_ref[rows, :] = part

        @pl.when((k > 0) & (k < last))
        def _():
            f_ref[rows, :] = f_ref[rows, :] + part

        @pl.when((k == last) & (i == 0))
        def _():
            loss_ref[...] = jnp.zeros_like(loss_ref)
            dg_ref[...] = jnp.zeros_like(dg_ref)

        @pl.when(k == last)
        def _():
            fv = f_ref[rows, :] + part
            r = _rstd(fv)
            fn = fv * r
            e = x2_ref[...] + fn * g_ref[...] - t_ref[...]
            loss_ref[...] = loss_ref[...] + jnp.sum(jnp.sum(e * e, axis=-1, keepdims=True), axis=0, keepdims=True)
            dy = e * (1.0 / D)
            dy_ref[...] = dy
            dg_ref[...] = dg_ref[...] + jnp.sum(dy * fn, axis=0, keepdims=True)
            t = dy * g_ref[...]
            df_ref[...] = (r * (t - fn * jnp.mean(t * fn, axis=-1, keepdims=True))).astype(BF16)

    wrow = pl.BlockSpec((None, FF_C, D), lambda k, i: (k, 0, 0))
    act = pl.BlockSpec((None, tm, FF_C), lambda k, i: (k, i, 0))
    late = pl.BlockSpec((tm, D), lambda k, i: (jnp.where(k == last, i, 0), 0))
    vec = pl.BlockSpec((1, D), lambda k, i: (0, 0))
    return pl.pallas_call(
        body, grid=(N_CHIP, S // tm), name="ffn_fwd",
        in_specs=[pl.BlockSpec((tm, D), lambda k, i: (i, 0)), wrow, wrow, wrow, late, late, vec],
        out_specs=[act, act, act, vec, late, late, vec],
        out_shape=[jax.ShapeDtypeStruct((N_CHIP, S, FF_C), BF16)] * 3
                  + [jax.ShapeDtypeStruct((1, D), F32), jax.ShapeDtypeStruct((S, D), F32),
                     jax.ShapeDtypeStruct((S, D), BF16), jax.ShapeDtypeStruct((1, D), F32)],
        scratch_shapes=[pltpu.VMEM((S, D), F32)],
        compiler_params=_params("arbitrary", "arbitrary"),
    )(h3, wg, wu, wd, x2, tgt, g4)


def _ffn_bwd_act(df, gt, up, wg, wu, wd, dy, x2, mix, g2, g3):
    tm, sub = 512, 256
    last = N_CHIP - 1

    def body(df_ref, gt_ref, up_ref, wg_ref, wu_ref, wd_ref, dy_ref, x2_ref, mix_ref, g2_ref, g3_ref,
             dgt_ref, dup_ref, dx2_ref, dmix_ref, dg3_ref, dg2_ref, dh_ref):
        k, i = pl.program_id(0), pl.program_id(1)
        parts = []
        for s in range(tm // sub):
            rows = slice(s * sub, (s + 1) * sub)
            da = _nt(df_ref[rows, :], wd_ref[...])
            gt, up = gt_ref[rows, :].astype(F32), up_ref[rows, :].astype(F32)
            sg = _sigmoid(gt)
            dup = (da * gt * sg).astype(BF16)
            dgt = (da * up * (sg * (1.0 + gt * (1.0 - sg)))).astype(BF16)
            dup_ref[rows, :] = dup
            dgt_ref[rows, :] = dgt
            parts.append(_nn(dgt, wg_ref[...]) + _nn(dup, wu_ref[...]))
        part = jnp.concatenate(parts, axis=0)
        rows = _rows(i, tm)

        @pl.when(k == 0)
        def _():
            dh_ref[rows, :] = part

        @pl.when((k > 0) & (k < last))
        def _():
            dh_ref[rows, :] = dh_ref[rows, :] + part

        @pl.when((k == last) & (i == 0))
        def _():
            dg3_ref[...] = jnp.zeros_like(dg3_ref)
            dg2_ref[...] = jnp.zeros_like(dg2_ref)

        @pl.when(k == last)
        def _():
            dh = dh_ref[rows, :] + part
            x2 = x2_ref[...]
            r3 = _rstd(x2)
            xn = x2 * r3
            dg3_ref[...] = dg3_ref[...] + jnp.sum(dh * xn, axis=0, keepdims=True)
            t = dh * g3_ref[...]
            dx2 = dy_ref[...] + r3 * (t - xn * jnp.mean(t * xn, axis=-1, keepdims=True))
            dx2_ref[...] = dx2
            mix = mix_ref[...]
            r2 = _rstd(mix)
            mn = mix * r2
            dg2_ref[...] = dg2_ref[...] + jnp.sum(dx2 * mn, axis=0, keepdims=True)
            u = dx2 * g2_ref[...]
            dmix_ref[...] = (r2 * (u - mn * jnp.mean(u * mn, axis=-1, keepdims=True))).astype(BF16)

    wrow = pl.BlockSpec((None, FF_C, D), lambda k, i: (k, 0, 0))
    act = pl.BlockSpec((None, tm, FF_C), lambda k, i: (k, i, 0))
    row = pl.BlockSpec((tm, D), lambda k, i: (i, 0))
    late = pl.BlockSpec((tm, D), lambda k, i: (jnp.where(k == last, i, 0), 0))
    vec = pl.BlockSpec((1, D), lambda k, i: (0, 0))
    return pl.pallas_call(
        body, grid=(N_CHIP, S // tm), name="ffn_bwd_act",
        in_specs=[row, act, act, wrow, wrow, wrow, late, late, late, vec, vec],
        out_specs=[act, act, late, late, vec, vec],
        out_shape=[jax.ShapeDtypeStruct((N_CHIP, S, FF_C), BF16), jax.ShapeDtypeStruct((N_CHIP, S, FF_C), BF16),
                   jax.ShapeDtypeStruct((S, D), F32), jax.ShapeDtypeStruct((S, D), BF16),
                   jax.ShapeDtypeStruct((1, D), F32), jax.ShapeDtypeStruct((1, D), F32)],
        scratch_shapes=[pltpu.VMEM((S, D), F32)],
        compiler_params=_params("arbitrary", "arbitrary"),
    )(df, gt, up, wg, wu, wd, dy, x2, mix, g2, g3)


def _ffn_bwd_w(a, df, h3, dgt, dup):
    tm = 1024
    assert S // tm == 2

    def body(a_ref, df_ref, h_ref, dgt_ref, dup_ref, dwd_ref, dwg_ref, dwu_ref, acc_d, acc_g, acc_u):
        i = pl.program_id(1)
        h = h_ref[...]
        parts = (_tn(a_ref[...], df_ref[...]), _tn(dgt_ref[...], h), _tn(dup_ref[...], h))

        @pl.when(i == 0)
        def _():
            for acc, part in zip((acc_d, acc_g, acc_u), parts):
                acc[...] = part

        @pl.when(i == S // tm - 1)
        def _():
            for out, acc, part in zip((dwd_ref, dwg_ref, dwu_ref), (acc_d, acc_g, acc_u), parts):
                out[...] = (acc[...] + part).astype(BF16)

    act = pl.BlockSpec((None, tm, FF_C), lambda k, i: (k, i, 0))
    row = pl.BlockSpec((tm, D), lambda k, i: (i, 0))
    wrow = pl.BlockSpec((None, FF_C, D), lambda k, i: (k, 0, 0))
    return pl.pallas_call(
        body, grid=(N_CHIP, S // tm), name="ffn_bwd_w",
        in_specs=[act, row, row, act, act],
        out_specs=[wrow, wrow, wrow],
        out_shape=[jax.ShapeDtypeStruct((N_CHIP, FF_C, D), BF16)] * 3,
        scratch_shapes=[pltpu.VMEM((FF_C, D), F32)] * 3,
        compiler_params=_params("parallel", "arbitrary"),
    )(a, df, h3, dgt, dup)


def _mix_bwd(dmix, cat_r, cat_a, wout, exchange, exchange_args):
    tm = 1024

    def body(dm_ref, cr_ref, ca_ref, w_ref, dret_ref, datt_ref, dw_ref, acc, xc):
        i = pl.program_id(0)

        @pl.when(i == 0)
        def _():
            xc.start()
            acc[...] = jnp.zeros_like(acc)

        dm = dm_ref[...]
        dret_ref[...] = _nt(dm, w_ref[0:512, :])
        datt = _nt(dm, w_ref[512:1024, :])
        for j in range(4):
            datt_ref[j] = datt[:, 128 * j:128 * j + 128]
        acc[0:512, :] += _tn(cr_ref[...], dm)
        acc[512:1024, :] += _tn(ca_ref[...], dm)

        @pl.when(i == S // tm - 1)
        def _():
            dw_ref[...] = acc[...].astype(BF16)
            xc.middle()
            xc.finish()

    row = lambda w: pl.BlockSpec((tm, w), lambda i: (i, 0))
    full = pl.BlockSpec((D, D), lambda i: (0, 0))
    return _carry("mix_bwd", body, exchange, exchange_args, (dmix, cat_r, cat_a, wout),
                  [row(D), row(512), row(512), full],
                  [row(512), pl.BlockSpec((4, tm, 128), lambda i: (0, i, 0)), full],
                  [jax.ShapeDtypeStruct((S, 512), F32), jax.ShapeDtypeStruct((4, S, 128), F32),
                   jax.ShapeDtypeStruct((D, D), BF16)],
                  scratch_shapes=[pltpu.VMEM((D, D), F32)], grid=(S // tm,), semantics=("arbitrary",))


def _att_bwd(aq, ak, av, datt, att_out, lse, exchange, exchange_args, after=None):
    def body(q_ref, k_ref, v_ref, do_ref, out_ref, l_ref, dq_ref, dk_ref, dv_ref, xc):
        xc.start()

        def clear(i, carry):
            rows = _rows(i, 256)
            for ref in (dq_ref, dk_ref, dv_ref):
                for j in range(4):
                    ref[j, rows, :] = jnp.zeros((256, 128), F32)
            return carry

        lax.fori_loop(0, S // 256, clear, 0)
        lane_head = lax.broadcasted_iota(jnp.int32, (ATT_BLK, 256), 1) // 64
        for d in PATTERN_DILATIONS:
            nb, has_prev = _att_blocks(d)
            bias_rest, bias_first = _att_bias(has_prev)

            def block(b, carry, d=d, nb=nb, has_prev=has_prev, bias_rest=bias_rest, bias_first=bias_first):
                r, ib = b // nb, b % nb
                rows = _class_rows(ib, r, d)
                prow = _class_rows(jnp.maximum(ib - 1, 0), r, d)
                bias = jnp.where(ib == 0, bias_first, bias_rest) if has_prev else bias_first
                for g in range(2):
                    qg = _slab_pair(q_ref, g, rows).astype(BF16)
                    kg = _slab_pair(k_ref, g, rows)
                    vg = _slab_pair(v_ref, g, rows)
                    if has_prev:
                        kg = jnp.concatenate([_slab_pair(k_ref, g, prow), kg], axis=0)
                        vg = jnp.concatenate([_slab_pair(v_ref, g, prow), vg], axis=0)
                    kg, vg = kg.astype(BF16), vg.astype(BF16)
                    dog = _slab_pair(do_ref, g, rows)
                    outg = _slab_pair(out_ref, g, rows)
                    lg = _slab_pair(l_ref, g, rows)
                    qs = _stack_heads(qg, lane_head)
                    dos = _stack_heads(dog, lane_head)
                    delta = jnp.sum(dos * jnp.concatenate([outg] * 4, axis=0), axis=-1, keepdims=True)
                    lh = jnp.max(_stack_heads(lg, lane_head, NEG), axis=-1, keepdims=True)
                    s = _nt(qs, kg) * ATT_SCALE + bias
                    p = jnp.exp(s - lh)
                    dosb = dos.astype(BF16)
                    ds = (p * (_nt(dosb, vg) - delta) * ATT_SCALE).astype(BF16)
                    dq = _unstack_heads(_nn(ds, kg), lane_head)
                    dk = _tn(ds, qs)
                    dv = _tn(p.astype(BF16), dosb)
                    for jj in range(2):
                        j, sl = 2 * g + jj, slice(128 * jj, 128 * jj + 128)
                        dq_ref[j, rows, :] += dq[:, sl]
                        if has_prev:
                            dk_ref[j, prow, :] += dk[0:ATT_BLK, sl]
                            dv_ref[j, prow, :] += dv[0:ATT_BLK, sl]
                            dk_ref[j, rows, :] += dk[ATT_BLK:2 * ATT_BLK, sl]
                            dv_ref[j, rows, :] += dv[ATT_BLK:2 * ATT_BLK, sl]
                        else:
                            dk_ref[j, rows, :] += dk[:, sl]
                            dv_ref[j, rows, :] += dv[:, sl]
                return carry

            lax.fori_loop(0, S // ATT_BLK, block, 0, unroll=4)
        xc.middle()
        xc.finish()

    slab = jax.ShapeDtypeStruct((4, S, 128), F32)
    return _carry("att_bwd", body, exchange, exchange_args, (aq, ak, av, datt, att_out, lse), [VMEM] * 6, [VMEM] * 3,
                  [slab, slab, slab], after=after)


def _ret_bwd(qr, kr, rv, proj, o_raw, states, dret, tabs, exchange, exchange_args, after=None):
    C = RET_C
    nc = S // C
    dtab, a_tab, b_tab, lam, bd = tabs

    def body(q_ref, k_ref, v_ref, g_ref, o_ref, st_ref, dr_ref, dt_ref, a_ref, b_ref, lam_ref, bd_ref,
             dq_ref, dk_ref, dv_ref, dg_ref, dR, exch):
        @pl.when(pl.program_id(0) == 0)
        def _():
            exch.start()
            dR[...] = jnp.zeros_like(dR)

        q, k, v = q_ref[...], k_ref[...], v_ref[...]
        lane_head = lax.broadcasted_iota(jnp.int32, (C, 256), 1) // 32
        col_head = lax.broadcasted_iota(jnp.int32, (C, 256), 1) // 64
        dos = []
        for j in range(4):
            sl = slice(128 * j, 128 * j + 128)
            oj = o_ref[:, sl]
            xc = oj - _seg_mean(oj)
            rs = lax.rsqrt(_seg_mean(xc * xc) + GN_EPS)
            rn = xc * rs
            gj = g_ref[:, sl]
            sg = _sigmoid(gj)
            dret = dr_ref[:, sl]
            dg_ref[:, sl] = dret * rn * (sg * (1.0 + gj * (1.0 - sg)))
            drn = dret * (gj * sg)
            dos.append(rs * (drn - _seg_mean(drn) - rn * _seg_mean(drn * rn)))
        do = [jnp.concatenate(dos[0:2], axis=1), jnp.concatenate(dos[2:4], axis=1)]
        do8 = jnp.concatenate(do, axis=1).astype(BF16)
        drb = dR[...].astype(BF16)
        rb = st_ref[...]
        dq = _nt(do8, rb) * a_ref[...]
        dk = _nt(v, drb) * b_ref[...]
        kb = (k.astype(F32) * b_ref[...]).astype(BF16)
        dvall = _nn(kb, drb)
        qs = _stack_heads(q, lane_head, n=8)
        dec = dt_ref[...]
        p = (_nt(qs, k) * dec).astype(BF16)
        dos = [_stack_heads(do[g], col_head).astype(BF16) for g in range(2)]
        dp = jnp.concatenate([_nt(dos[g], v[:, 256 * g:256 * g + 256]) for g in range(2)], axis=0)
        ds = (dp * dec).astype(BF16)
        dq = dq + _unstack_heads(_nn(ds, k), lane_head, n=8)
        dk = dk + _tn(ds, qs)
        dv = [dvall[:, 256 * g:256 * g + 256] + _tn(p[4 * C * g:4 * C * (g + 1)], dos[g]) for g in range(2)]
        qa = (q.astype(F32) * a_ref[...]).astype(BF16)
        dR[...] = dR[...] * lam_ref[...] + _tn(qa, do8) * bd_ref[...]
        dq_ref[...] = dq
        dk_ref[...] = dk
        dv_ref[:, 0:256] = dv[0]
        dv_ref[:, 256:512] = dv[1]

        @pl.when(pl.program_id(0) == nc - 1)
        def _():
            exch.middle()
            exch.finish()

    rev = lambda w: pl.BlockSpec((C, w), lambda n: (nc - 1 - n, 0))
    full = lambda a: pl.BlockSpec(a.shape, lambda n: (0,) * a.ndim)
    return _carry(
        "ret_bwd", body, exchange, exchange_args, (qr, kr, rv, proj, o_raw, states, dret, dtab, a_tab, b_tab, lam, bd),
        [rev(256), rev(256), rev(512), rev(512), rev(512),
         pl.BlockSpec((None, 256, 512), lambda n: (nc - 1 - n, 0, 0)), rev(512),
         full(dtab), full(a_tab), full(b_tab), full(lam), full(bd)],
        [rev(256), rev(256), rev(512), rev(512)],
        [jax.ShapeDtypeStruct((S, 256), F32), jax.ShapeDtypeStruct((S, 256), F32),
         jax.ShapeDtypeStruct((S, 512), F32), jax.ShapeDtypeStruct((S, 512), F32)],
        scratch_shapes=[pltpu.VMEM((256, 512), F32)], grid=(nc,), semantics=("arbitrary",), after=after)


def _rot_bwd(cos, sin, dqr, dkr, drv, drg, dq_att, dk_att, dv_att):
    tm = 256

    def body(cos_ref, sin_ref, dqr_ref, dkr_ref, drv_ref, drg_ref, dqa_ref, dka_ref, dva_ref, dp_ref):
        cr, ca, sr, sa = cos_ref[:, 0:256], cos_ref[:, 256:768], sin_ref[:, 0:256], sin_ref[:, 256:768]
        lo_r, lo_a = _rot_halves(tm)

        def unrot_r(g):
            gs = g * sr
            return g * cr + pltpu.roll(jnp.where(lo_r, -gs, 0.0), 16, 1) + pltpu.roll(jnp.where(lo_r, 0.0, gs), 240, 1)

        def unrot_a(g):
            gs = g * sa
            return g * ca + pltpu.roll(jnp.where(lo_a, -gs, 0.0), 8, 1) + pltpu.roll(jnp.where(lo_a, 0.0, gs), 504, 1)

        def wide(ref):
            return jnp.concatenate([ref[j] for j in range(4)], axis=1)

        dp_ref[:, 0:256] = unrot_r(dqr_ref[...]).astype(BF16)
        dp_ref[:, 256:512] = unrot_r(dkr_ref[...] * RET_SCALE).astype(BF16)
        dp_ref[:, 512:1024] = drv_ref[...].astype(BF16)
        dp_ref[:, 1024:1536] = drg_ref[...].astype(BF16)
        dp_ref[:, 1536:2048] = unrot_a(wide(dqa_ref)).astype(BF16)
        dp_ref[:, 2048:2560] = unrot_a(wide(dka_ref)).astype(BF16)
        dp_ref[:, 2560:3072] = wide(dva_ref).astype(BF16)

    row = lambda w: pl.BlockSpec((tm, w), lambda i: (i, 0))
    slab = pl.BlockSpec((4, tm, 128), lambda i: (0, i, 0))
    return pl.pallas_call(
        body, grid=(S // tm,), name="rot_bwd",
        in_specs=[row(768), row(768), row(256), row(256), row(512), row(512), slab, slab, slab],
        out_specs=row(PW), out_shape=jax.ShapeDtypeStruct((S, PW), BF16),
        compiler_params=_params("parallel"),
    )(cos, sin, dqr, dkr, drv, drg, dq_att, dk_att, dv_att)


def _win_bwd_w(h1, dproj, exchange, exchange_args):
    def body(h_ref, dp_ref, dw_ref, xc):
        k = pl.program_id(0)

        @pl.when(k == 0)
        def _():
            xc.start()

        dw_ref[...] = _tn(h_ref[...], dp_ref[...]).astype(BF16)

        @pl.when(k == N_CHIP - 1)
        def _():
            xc.middle()
            xc.finish()

    (dw,), out = _carry(
        "win_bwd_w", body, exchange, exchange_args, (h1, dproj),
        [pl.BlockSpec((S, D), lambda k: (0, 0)), pl.BlockSpec((S, WIN_C), lambda k: (0, k))],
        [pl.BlockSpec((None, D, WIN_C), lambda k: (k, 0, 0))],
        [jax.ShapeDtypeStruct((N_CHIP, D, WIN_C), BF16)], grid=(N_CHIP,), semantics=("arbitrary",))
    return dw, out


def _in_bwd(dproj, win_g, x, dx2, g1, after):
    tm = 512

    def body(dp_ref, w_ref, x_ref, dx2_ref, g_ref, dx_ref, dg_ref, _):
        @pl.when(pl.program_id(0) == 0)
        def _():
            dg_ref[...] = jnp.zeros_like(dg_ref)

        dh = _nt(dp_ref[:, 0:WIN_C], w_ref[0])
        for k in range(1, N_CHIP):
            dh = dh + _nt(dp_ref[:, k * WIN_C:(k + 1) * WIN_C], w_ref[k])
        xv = x_ref[...]
        r = _rstd(xv)
        xn = xv * r
        dg_ref[...] = dg_ref[...] + jnp.sum(dh * xn, axis=0, keepdims=True)
        t = dh * g_ref[...]
        dx_ref[...] = dx2_ref[...] + r * (t - xn * jnp.mean(t * xn, axis=-1, keepdims=True))

    row = lambda w: pl.BlockSpec((tm, w), lambda i: (i, 0))
    vec = pl.BlockSpec((1, D), lambda i: (0, 0))
    return _carry("in_bwd", body, _NoExchange(), (), (dproj, win_g, x, dx2, g1),
                  [row(PW), pl.BlockSpec((N_CHIP, D, WIN_C), lambda i: (0, 0, 0)), row(D), row(D), vec],
                  [row(D), vec], [jax.ShapeDtypeStruct((S, D), F32), jax.ShapeDtypeStruct((1, D), F32)],
                  grid=(S // tm,), semantics=("arbitrary",), after=after)[0]


ANY = pl.BlockSpec(memory_space=pl.ANY)
VMEM = pl.BlockSpec(memory_space=pltpu.VMEM)
FLIPS = ((1, 0), (0, 1), (1, 1))


def _place():
    x, y, c = lax.axis_index("x"), lax.axis_index("y"), lax.axis_index("c")
    chips = [((1 - x) if fx else x, (1 - y) if fy else y) for fx, fy in FLIPS]
    return x, y, c, 2 * x + y, chips


def _remote(src, dst, send_sem, recv_sem, device):
    return pltpu.make_async_remote_copy(src_ref=src, dst_ref=dst, send_sem=send_sem, recv_sem=recv_sem,
                                        device_id=device, device_id_type=MESH)


class _Exchange:
    aliases = {}

    def middle(self, ins, outs, sems):
        pass


class _GatherShards(_Exchange):
    def __init__(self, shards):
        n = self.n = len(shards)
        self.n_in = self.n_out = n
        self.out_shape = [jax.ShapeDtypeStruct((N_CHIP,) + s.shape, s.dtype) for s in shards]
        dma = pltpu.SemaphoreType.DMA
        self.scratch = [dma((3 * n,)), dma((3 * n,)), dma((3 * n,)), dma((3 * n,)), dma((n,)), dma((n,))]

    def _ici(self, ins, outs, sems, a, j, chip):
        x, y, c, me, chips = _place()
        half = ins[a].shape[0] // 2
        return _remote(ins[a].at[pl.ds(c * half, half), :], outs[a].at[me, pl.ds(c * half, half), :],
                       sems[0].at[3 * a + j], sems[1].at[3 * a + j], (*chip, c))

    def _fwd(self, outs, sems, a, j, chip, half_of):
        x, y, c, me, chips = _place()
        half = outs[a].shape[1] // 2
        blk = outs[a].at[2 * chip[0] + chip[1], pl.ds(half_of * half, half), :]
        return _remote(blk, blk, sems[2].at[3 * a + j], sems[3].at[3 * a + j], (x, y, 1 - c))

    def _own(self, ins, outs, sems, a):
        return _own_shard_to_sibling(ins[a], outs[a], sems[4].at[a], sems[5].at[a])

    def start(self, ins, outs, sems):
        chips = _place()[4]
        for a in range(self.n):
            for j, chip in enumerate(chips):
                self._ici(ins, outs, sems, a, j, chip).start()
        for a in range(self.n):
            self._own(ins, outs, sems, a).start()

    def middle(self, ins, outs, sems):
        x, y, c, me, chips = _place()
        for a in range(self.n):
            for j, chip in enumerate(chips):
                half = outs[a].shape[1] // 2
                blk = outs[a].at[2 * chip[0] + chip[1], pl.ds(c * half, half), :]
                _remote(blk, blk, sems[0].at[3 * a + j], sems[1].at[3 * a + j], (x, y, c)).wait_recv()
                self._fwd(outs, sems, a, j, chip, c).start()

    def finish(self, ins, outs, sems):
        x, y, c, me, chips = _place()
        for a in range(self.n):
            for j, chip in enumerate(chips):
                self._fwd(outs, sems, a, j, chip, 1 - c).wait_recv()
        for a in range(self.n):
            for j, chip in enumerate(chips):
                self._ici(ins, outs, sems, a, j, chip).wait_send()
                self._fwd(outs, sems, a, j, chip, c).wait_send()
            self._own(ins, outs, sems, a).wait()


def _own_shard_to_sibling(shard_ref, gathered_ref, send_sem, recv_sem):
    x, y, c, me, chips = _place()
    return _remote(shard_ref, gathered_ref.at[me], send_sem, recv_sem, (x, y, 1 - c))


class _NoExchange(_Exchange):
    n_in = n_out = 0
    out_shape = ()
    scratch = ()

    def start(self, ins, outs, sems):
        pass

    def finish(self, ins, outs, sems):
        pass


class _ForwardGathered(_Exchange):
    def __init__(self, shards, own=True, forward=True):
        self.own, self.forward = own, forward
        n = self.n = len(shards)
        self.n_in, self.n_out = 2 * n, n
        self.out_shape = [jax.ShapeDtypeStruct((N_CHIP,) + s.shape, s.dtype) for s in shards]
        dma = pltpu.SemaphoreType.DMA
        self.scratch = [dma((3 * n,)), dma((3 * n,)), dma((n,)), dma((n,))]
        self.aliases = {n + a: a for a in range(n)}

    def _fwd(self, outs, sems, a, j, chip, half_of):
        x, y, c, me, chips = _place()
        half = outs[a].shape[1] // 2
        blk = outs[a].at[2 * chip[0] + chip[1], pl.ds(half_of * half, half), :]
        return _remote(blk, blk, sems[0].at[3 * a + j], sems[1].at[3 * a + j], (x, y, 1 - c))

    def _own(self, ins, outs, sems, a):
        return _own_shard_to_sibling(ins[a], outs[a], sems[2].at[a], sems[3].at[a])

    def start(self, ins, outs, sems):
        x, y, c, me, chips = _place()
        for a in range(self.n):
            for j, chip in enumerate(chips if self.forward else ()):
                self._fwd(outs, sems, a, j, chip, c).start()
        for a in range(self.n if self.own else 0):
            self._own(ins, outs, sems, a).start()

    def finish(self, ins, outs, sems):
        x, y, c, me, chips = _place()
        for a in range(self.n):
            for j, chip in enumerate(chips if self.forward else ()):
                self._fwd(outs, sems, a, j, chip, 1 - c).wait_recv()
        for a in range(self.n):
            for j, chip in enumerate(chips if self.forward else ()):
                self._fwd(outs, sems, a, j, chip, c).wait_send()
            if self.own:
                self._own(ins, outs, sems, a).wait()


HBM = pl.BlockSpec(memory_space=pltpu.HBM)
SEMS = pl.BlockSpec(memory_space=pltpu.SEMAPHORE)
DATAFLOW = pltpu.SideEffectType.DATAFLOW_SIDE_EFFECTING


class _OverIci:
    def __init__(self, name, sources, lands):
        self.name, self.n = name, len(sources)
        hbm = lambda t: pltpu.with_memory_space_constraint(t, pltpu.HBM)
        self.arrays = [hbm(t) for t in sources] + [hbm(t) for t in lands]

    def sent(self, src, land, a, chip):
        raise NotImplementedError

    def landed(self, land, a, chip):
        raise NotImplementedError

    def _copy(self, arr, sems, a, j, receiving):
        x, y, c, me, chips = _place()
        src, dst = self.sent(arr[a], arr[self.n + a], a, chips[j])
        if receiving:
            dst = self.landed(arr[self.n + a], a, chips[j])
        return _remote(src, dst, sems[0].at[3 * a + j], sems[1].at[3 * a + j], (*chips[j], c))

    def start(self, after):
        m = len(self.arrays)

        def body(*refs):
            arr, sems, token = refs[:m], refs[m + 1:m + 3], refs[-1]
            for a in range(self.n):
                for j in range(3):
                    self._copy(arr, sems, a, j, False).start()
            token[...] = jnp.zeros_like(token)

        dma = pltpu.SemaphoreType.DMA
        outs = pl.pallas_call(
            body, name=self.name + "_start",
            out_shape=[dma((3 * self.n,)), dma((3 * self.n,))] + [pltpu.HBM(t.shape, t.dtype) for t in self.arrays]
                      + [jax.ShapeDtypeStruct((8, 128), F32)],
            in_specs=[HBM] * m + [ANY], out_specs=[SEMS, SEMS] + [HBM] * m + [VMEM],
            input_output_aliases={i: 2 + i for i in range(m)},
            compiler_params=pltpu.CompilerParams(has_side_effects=DATAFLOW),
        )(*self.arrays, after)
        self.sems, self.arrays = outs[0:2], list(outs[2:2 + m])
        return outs[-1]

    def wait(self, after):
        m = len(self.arrays)

        def body(*refs):
            arr, sems = refs[:m], refs[m:m + 2]
            for a in range(self.n):
                for j in range(3):
                    self._copy(arr, sems, a, j, False).wait_send()
                    self._copy(arr, sems, a, j, True).wait_recv()

        outs = pl.pallas_call(
            body, name=self.name + "_wait",
            out_shape=[pltpu.HBM(t.shape, t.dtype) for t in self.arrays],
            in_specs=[HBM] * m + [SEMS, SEMS, ANY], out_specs=[HBM] * m,
            input_output_aliases={i: i for i in range(m)},
            compiler_params=pltpu.CompilerParams(has_side_effects=DATAFLOW),
        )(*self.arrays, *self.sems, after)
        return list(outs[:self.n]), list(outs[self.n:])


class _GatherOverIci(_OverIci):
    def __init__(self, name, shards):
        super().__init__(name, shards, [lax.empty((N_CHIP,) + s.shape, s.dtype) for s in shards])

    @staticmethod
    def _half(ref):
        c = lax.axis_index("c")
        half = ref.shape[-2] // 2
        return pl.ds(c * half, half)

    def sent(self, src, land, a, chip):
        return src.at[self._half(src), :], land.at[_place()[3], self._half(src), :]

    def landed(self, land, a, chip):
        return land.at[2 * chip[0] + chip[1], self._half(land), :]


class _SumOverIci(_OverIci):
    def __init__(self, name, pre):
        super().__init__(name, pre, [lax.empty(p.shape, p.dtype) for p in pre])

    def sent(self, src, land, a, chip):
        return src.at[2 * chip[0] + chip[1]], land.at[_place()[3]]

    def landed(self, land, a, chip):
        return land.at[2 * chip[0] + chip[1]]


class _HalvesToSibling(_Exchange):
    def __init__(self, grads):
        n = self.n = len(grads)
        self.n_in = self.n_out = n
        self.out_shape = [jax.ShapeDtypeStruct((N_CHIP, g.shape[1] // 2, g.shape[2]), g.dtype) for g in grads]
        self.scratch = [pltpu.SemaphoreType.DMA((n,)), pltpu.SemaphoreType.DMA((n,))]

    def _copy(self, ins, outs, sems, a):
        x, y, c, me, chips = _place()
        half = ins[a].shape[1] // 2
        return _remote(ins[a].at[:, pl.ds((1 - c) * half, half), :], outs[a], sems[0].at[a], sems[1].at[a], (x, y, 1 - c))

    def start(self, ins, outs, sems):
        for a in range(self.n):
            self._copy(ins, outs, sems, a).start()

    def finish(self, ins, outs, sems):
        for a in range(self.n):
            self._copy(ins, outs, sems, a).wait_recv()
        for a in range(self.n):
            self._copy(ins, outs, sems, a).wait_send()


class _ShareHalves(_Exchange):
    def __init__(self, fulls):
        n = self.n = len(fulls)
        self.n_in = self.n_out = n
        self.out_shape = [jax.ShapeDtypeStruct(f.shape, f.dtype) for f in fulls]
        self.scratch = [pltpu.SemaphoreType.DMA((n,)), pltpu.SemaphoreType.DMA((n,))]
        self.aliases = {a: a for a in range(n)}

    def _copy(self, outs, sems, a, half_of):
        x, y, c, me, chips = _place()
        half = outs[a].shape[0] // 2
        rows = outs[a].at[pl.ds(half_of * half, half), :]
        return _remote(rows, rows, sems[0].at[a], sems[1].at[a], (x, y, 1 - c))

    def start(self, ins, outs, sems):
        c = _place()[2]
        for a in range(self.n):
            self._copy(outs, sems, a, c).start()

    def finish(self, ins, outs, sems):
        c = _place()[2]
        for a in range(self.n):
            self._copy(outs, sems, a, 1 - c).wait_recv()
        for a in range(self.n):
            self._copy(outs, sems, a, c).wait_send()


class _GatherBlocks(_Exchange):
    def __init__(self, block):
        self.n_in = self.n_out = 1
        self.out_shape = [jax.ShapeDtypeStruct((8,) + block.shape, block.dtype)]
        dma = pltpu.SemaphoreType.DMA
        self.scratch = [dma((7,)), dma((7,)), dma]

    @staticmethod
    def _peer(f):
        x, y, c, me, chips = _place()
        return ((1 - x) if f & 4 else x, (1 - y) if f & 2 else y, (1 - c) if f & 1 else c)

    def start(self, ins, outs, sems):
        x, y, c, me, chips = _place()
        for f in range(1, 8):
            _remote(ins[0], outs[0].at[2 * me + c], sems[0].at[f - 1], sems[1].at[f - 1], self._peer(f)).start()
        pltpu.make_async_copy(ins[0], outs[0].at[2 * me + c], sems[2]).start()

    def finish(self, ins, outs, sems):
        x, y, c, me, chips = _place()
        for f in range(1, 8):
            px, py, pc = self._peer(f)
            blk = outs[0].at[4 * px + 2 * py + pc]
            _remote(blk, blk, sems[0].at[f - 1], sems[1].at[f - 1], (x, y, c)).wait_recv()
        for f in range(1, 8):
            _remote(ins[0], outs[0].at[2 * me + c], sems[0].at[f - 1], sems[1].at[f - 1], self._peer(f)).wait_send()
        pltpu.make_async_copy(ins[0], outs[0].at[2 * me + c], sems[2]).wait()


class _Both(_Exchange):
    def __init__(self, first, second):
        self.parts = (first, second)
        self.n_in, self.n_out = first.n_in + second.n_in, first.n_out + second.n_out
        self.out_shape = first.out_shape + second.out_shape
        self.scratch = first.scratch + second.scratch
        self.aliases = dict(first.aliases)
        self.aliases.update({first.n_in + i: first.n_out + o for i, o in second.aliases.items()})

    def _split(self, ins, outs, sems):
        a, b = self.parts
        return ((a, ins[:a.n_in], outs[:a.n_out], sems[:len(a.scratch)]),
                (b, ins[a.n_in:], outs[a.n_out:], sems[len(a.scratch):]))

    def start(self, ins, outs, sems):
        for ex, i, o, s in self._split(ins, outs, sems):
            ex.start(i, o, s)

    def middle(self, ins, outs, sems):
        for ex, i, o, s in self._split(ins, outs, sems):
            ex.middle(i, o, s)

    def finish(self, ins, outs, sems):
        for ex, i, o, s in self._split(ins, outs, sems):
            ex.finish(i, o, s)


class _Bound:
    def __init__(self, ex, ins, outs, sems):
        self.start = lambda: ex.start(ins, outs, sems)
        self.middle = lambda: ex.middle(ins, outs, sems)
        self.finish = lambda: ex.finish(ins, outs, sems)


def _carry(name, body, ex, ex_args, args, in_specs, out_specs, out_shape, scratch_shapes=(), grid=None, semantics=(),
           after=None):
    n_a, n_o, n_s = len(args), len(out_shape), len(scratch_shapes)
    behind = [] if after is None else [after]

    def full_body(*refs):
        p = 0
        groups = []
        for size in (n_a, ex.n_in, len(behind), n_o, ex.n_out, n_s, len(ex.scratch)):
            groups.append(refs[p:p + size])
            p += size
        a, ei, _, o, eo, s, es = groups
        body(*a, *o, *s, _Bound(ex, ei, eo, es))

    kwargs = {} if grid is None else {"grid": grid}
    outs = pl.pallas_call(
        full_body, name=name,
        in_specs=list(in_specs) + [ANY] * (ex.n_in + len(behind)), out_specs=list(out_specs) + [ANY] * ex.n_out,
        out_shape=list(out_shape) + list(ex.out_shape), scratch_shapes=list(scratch_shapes) + list(ex.scratch),
        input_output_aliases={n_a + i: n_o + o for i, o in ex.aliases.items()},
        compiler_params=_params(*semantics) if semantics else pltpu.CompilerParams(vmem_limit_bytes=VMEM_LIMIT),
        **kwargs,
    )(*args, *ex_args, *behind)
    return outs[:n_o], outs[n_o:]


def _prepare_carrying(name, x, g1, pos, ifc, spread, arrays, ex, ex_args):
    n = len(arrays)
    r, cc = arrays[0].shape
    steps = 4
    tr, tm = r // steps, S // steps

    def body(x_ref, g_ref, pos_ref, ifc_ref, e_ref, *refs):
        src, h_ref, cos_ref, sin_ref, dst, xc = refs[:n], refs[n], refs[n + 1], refs[n + 2], refs[n + 3:2 * n + 3], refs[-1]

        @pl.when(pl.program_id(0) == 0)
        def _():
            xc.start()

        xv = x_ref[...]
        h_ref[...] = (xv * _rstd(xv) * g_ref[...]).astype(BF16)
        ang = pos_ref[...].astype(F32) * ifc_ref[...]
        cos_ref[...] = _spread_exact(jnp.cos(ang), e_ref[...])
        sin_ref[...] = _spread_exact(jnp.sin(ang), e_ref[...])
        for a in range(n):
            dst[a][...] = src[a][...].astype(BF16)

        @pl.when(pl.program_id(0) == steps - 1)
        def _():
            xc.middle()
            xc.finish()

    row = lambda w: pl.BlockSpec((tm, w), lambda i: (i, 0))
    const = lambda w: pl.BlockSpec((1, w), lambda i: (0, 0))
    blk = pl.BlockSpec((tr, cc), lambda i: (i, 0))
    return _carry(name, body, ex, ex_args, (x, g1, pos, ifc, spread, *arrays),
                  [row(D), const(D), row(1), const(128), pl.BlockSpec((128, 768), lambda i: (0, 0))] + [blk] * n,
                  [row(D), row(768), row(768)] + [blk] * n,
                  [jax.ShapeDtypeStruct((S, D), BF16)] + [jax.ShapeDtypeStruct((S, 768), F32)] * 2
                  + [jax.ShapeDtypeStruct((r, cc), BF16)] * n,
                  grid=(steps,), semantics=("arbitrary",))


def _exchange_alone(name, ex, ex_args):
    def body(xc):
        xc.start()
        xc.middle()
        xc.finish()

    return _carry(name, body, ex, ex_args, (), (), (), ())[1]


def _core_index():
    return lax.axis_index("c").astype(jnp.int32).reshape(1)


def _pair_sum(gs, gots):
    n = len(gs)
    _, r, cc = gs[0].shape
    half = r // 2

    def body(c_ref, *refs):
        for a in range(n):
            refs[2 * n + a][...] = (refs[a][...].astype(F32) + refs[n + a][...].astype(F32)).astype(BF16)

    mine = pl.BlockSpec((None, half, cc), lambda k, c_ref: (k, c_ref[0], 0))
    blk = pl.BlockSpec((None, half, cc), lambda k, c_ref: (k, 0, 0))
    return pl.pallas_call(
        body, name=f"pair_sum_{r}x{cc}",
        grid_spec=pltpu.PrefetchScalarGridSpec(
            num_scalar_prefetch=1, grid=(N_CHIP,), in_specs=[mine] * n + [blk] * n, out_specs=[blk] * n),
        out_shape=[jax.ShapeDtypeStruct((N_CHIP, half, cc), BF16)] * n,
        compiler_params=_params("parallel"),
    )(_core_index(), *gs, *gots)


def _chip_sum(pre, parts):
    n = len(parts)
    _, half, cc = parts[0].shape
    tr = half // 2
    me = 2 * lax.axis_index("x") + lax.axis_index("y")
    others = [k + (k >= me).astype(jnp.int32) for k in range(3)]
    where = jnp.stack([lax.axis_index("c"), me, *others]).astype(jnp.int32)

    def body(w_ref, *refs):
        for a in range(n):
            own, p1, p2, p3 = refs[4 * a:4 * a + 4]
            refs[4 * n + a][...] = ((own[...].astype(F32) + p1[...].astype(F32)) + p2[...].astype(F32)) + p3[...].astype(F32)

    slot = lambda s: pl.BlockSpec((None, tr, cc), lambda i, w_ref: (w_ref[s], i, 0))
    operands = []
    for a in range(n):
        operands += [pre[a], parts[a], parts[a], parts[a]]
    return pl.pallas_call(
        body, name=f"chip_sum_{half}x{cc}",
        grid_spec=pltpu.PrefetchScalarGridSpec(
            num_scalar_prefetch=1, grid=(2,),
            in_specs=[slot(1), slot(2), slot(3), slot(4)] * n,
            out_specs=[pl.BlockSpec((tr, cc), lambda i, w_ref: (2 * w_ref[0] + i, 0))] * n),
        out_shape=[jax.ShapeDtypeStruct((2 * half, cc), F32)] * n,
        compiler_params=_params("parallel"),
    )(where, *operands)


def _adamw_math(w, g, m, v):
    m = ADAM_B1 * m + (1.0 - ADAM_B1) * g
    v = ADAM_B2 * v + (1.0 - ADAM_B2) * (g * g)
    m_hat = m / (1.0 - ADAM_B1 ** ADAM_STEP)
    v_hat = v / (1.0 - ADAM_B2 ** ADAM_STEP)
    delta = -ADAM_LR * (m_hat / (jnp.sqrt(v_hat) + ADAM_EPS) + ADAM_WD * w)
    return delta, m, v


def _adamw(w, g, m, v, after=None):
    r, cc = w.shape
    tr = r // 4

    def body(w_ref, g_ref, m_ref, v_ref, go_ref, d_ref, nm_ref, nv_ref, _):
        g = g_ref[...]
        go_ref[...] = g
        d_ref[...], nm_ref[...], nv_ref[...] = _adamw_math(w_ref[...], g, m_ref[...], v_ref[...])

    blk = pl.BlockSpec((tr, cc), lambda i: (i, 0))
    return _carry(f"adamw_{r}x{cc}", body, _NoExchange(), (), (w, g, m, v), [blk] * 4, [blk] * 4,
                  [jax.ShapeDtypeStruct((r, cc), F32)] * 4, grid=(4,), semantics=("parallel",), after=after)[0]


def _pack8(rows):
    def body(*refs):
        out_ref = refs[-1]
        out_ref[...] = jnp.zeros_like(out_ref)
        for i, r in enumerate(refs[:-1]):
            out_ref[i:i + 1, :] = r[...]

    return pl.pallas_call(body, name="pack8", out_shape=jax.ShapeDtypeStruct((8, D), F32))(*rows)


def _adamw_gains(gall, ws, ms, vs):
    def body(ga_ref, *refs):
        w, m, v = refs[0:4], refs[4:8], refs[8:12]
        outs, loss_ref, total = refs[12:28], refs[28], refs[29]
        g = ga_ref[0]
        for dev in range(1, 8):
            g = g + ga_ref[dev]
        total[...] = g
        for i in range(4):
            gi = total[i:i + 1, :]
            outs[i][...] = gi
            outs[4 + i][...], outs[8 + i][...], outs[12 + i][...] = _adamw_math(w[i][...], gi, m[i][...], v[i][...])
        loss_ref[...] = total[4:5, 0:128] * (0.5 / D)

    outs = pl.pallas_call(
        body, name="adamw_gains",
        out_shape=[jax.ShapeDtypeStruct((1, D), F32)] * 16 + [jax.ShapeDtypeStruct((1, 128), F32)],
        scratch_shapes=[pltpu.VMEM((8, D), F32)],
    )(gall, *ws, *ms, *vs)
    return outs[0:4], outs[4:8], outs[8:12], outs[12:16], outs[16]


def kernel(x, positions, w_in, w_out, g_pre_mix, g_post_mix, g_pre_ffn, g_post_ffn, w_gate, w_up, w_down, loss_target, m_w_in, m_w_out, m_g_pre_mix, m_g_post_mix, m_g_pre_ffn, m_g_post_ffn, m_w_gate, m_w_up, m_w_down, v_w_in, v_w_out, v_g_pre_mix, v_g_post_mix, v_g_pre_ffn, v_g_post_ffn, v_w_gate, v_w_up, v_w_down):
    tr = lambda t: jnp.swapaxes(t, 1, 2)[0]
    shards = [w_in[0], w_out[0], tr(w_gate), tr(w_up), w_down[0]]
    moms = [m_w_in[0], m_w_out[0], tr(m_w_gate), tr(m_w_up), m_w_down[0]]
    vels = [v_w_in[0], v_w_out[0], tr(v_w_gate), tr(v_w_up), v_w_down[0]]
    xs, pos, tgt = x[0], positions.reshape(S, 1), loss_target[0]
    g1, g2, g3, g4 = g_pre_mix, g_post_mix, g_pre_ffn, g_post_ffn
    tabs = tuple(jnp.asarray(t) for t in _retention_tables())
    ifc, spread = _rotary_tables()
    ifc, spread = jnp.asarray(ifc), jnp.asarray(spread, dtype=BF16)
    bf = [s.astype(BF16) for s in shards[:2]]

    (h1, cos, sin, *ffn_bf), (win_g,) = _prepare_carrying(
        "gather_in", xs, g1, pos, ifc, spread, shards[2:], _GatherShards(bf[:1]), bf[:1])
    bf += list(ffn_bf)
    wout_gather = _GatherOverIci("wout_gather", bf[1:2])
    token = wout_gather.start(win_g)
    ffn_gather = _GatherOverIci("ffn_gather", bf[2:])
    token = ffn_gather.start(token)
    qr, kr, rv, rg, aq, ak, av = _proj_fwd(h1, win_g, cos, sin, token)
    wout_sh, wout_land = wout_gather.wait(qr)
    n_ffn = len(bf[2:])
    (att_out, lse, cat_a), (wout_g, *ffn_gather.arrays[n_ffn:]) = _att_fwd(
        aq, ak, av, _Both(_ForwardGathered(bf[1:2]), _ForwardGathered(bf[2:], forward=False)),
        [*wout_sh, *wout_land, *ffn_gather.arrays])
    wout_g = wout_g.reshape(D, D)
    (o_raw, cat_r, states), _ = _ret_fwd(qr, kr, rv, rg, tabs, _NoExchange(), (), cat_a)
    ffn_sh, ffn_lands = ffn_gather.wait(cat_r)
    (mix, x2, h3), (wg_g, wu_g, wd_g) = _mix_fwd(cat_r, cat_a, wout_g, xs, g2, g3,
                                                _ForwardGathered(bf[2:], own=False), [*ffn_sh, *ffn_lands])
    gt, up, a, sq, dy, df, dg4 = _ffn_fwd(h3, wg_g, wu_g, wd_g, x2, tgt, g4)

    dgt, dup, dx2, dmix, dg3, dg2 = _ffn_bwd_act(df, gt, up, wg_g, wu_g, wd_g, dy, x2, mix, g2, g3)
    ffn_grads = list(_ffn_bwd_w(a, df, h3, dgt, dup))
    (dret, datt, dwout), got = _mix_bwd(dmix, cat_r, cat_a, wout_g, _HalvesToSibling(ffn_grads), ffn_grads)
    ffn_sum = _SumOverIci("ffn_sum", _pair_sum(ffn_grads, got))
    token = ffn_sum.start(datt)
    (dq_att, dk_att, dv_att), _ = _att_bwd(aq, ak, av, datt, att_out, lse, _NoExchange(), (), token)
    (dqr, dkr, drv, drg), _ = _ret_bwd(qr, kr, rv, rg, o_raw, states, dret, tabs, _NoExchange(), (), token)
    dproj = _rot_bwd(cos, sin, dqr, dkr, drv, drg, dq_att, dk_att, dv_att)
    sums = _chip_sum(*ffn_sum.wait(dproj))
    dwin, ffn_full = _win_bwd_w(h1, dproj, _ShareHalves(sums), sums)
    in_grads = [dwin, dwout.reshape(N_CHIP, WOUT_R, D)]

    got = _exchange_alone("halves_to_sibling", _HalvesToSibling(in_grads), in_grads)
    in_sum = _SumOverIci("in_sum", [*_pair_sum(in_grads[:1], got[:1]), *_pair_sum(in_grads[1:], got[1:])])
    token = in_sum.start(dproj)
    dx, dg1 = _in_bwd(dproj, win_g, xs, dx2, g1, token)
    ffn_upd = [_adamw(shards[2 + i], ffn_full[o], moms[2 + i], vels[2 + i], token)
               for i, o in enumerate((1, 2, 0))]
    pre, parts = in_sum.wait(ffn_upd[2][0])
    sums = [*_chip_sum(pre[:1], parts[:1]), *_chip_sum(pre[1:], parts[1:])]
    gblock = _pack8([dg1, dg2, dg3, dg4, sq])
    *in_full, gall = _exchange_alone("share_rest", _Both(_ShareHalves(sums), _GatherBlocks(gblock)), [*sums, gblock])
    upd = [_adamw(w, g, m, v) for w, g, m, v in zip(shards[:2], in_full, moms[:2], vels[:2])] + ffn_upd
    gg, gd, gm, gv, loss_row = _adamw_gains(gall, [g1, g2, g3, g4],
                                            [m_g_pre_mix, m_g_post_mix, m_g_pre_ffn, m_g_post_ffn],
                                            [v_g_pre_mix, v_g_post_mix, v_g_pre_ffn, v_g_post_ffn])

    def order(mats, vecs):
        back = lambda t: jnp.swapaxes(t[None], 1, 2)
        return [mats[0][None], mats[1][None], *vecs, back(mats[2]), back(mats[3]), mats[4][None]]

    return (loss_row[0, 0], dx[None],
            *order([u[0] for u in upd], gg),
            *order([u[1] for u in upd], gd),
            *order([u[2] for u in upd], gm),
            *order([u[3] for u in upd], gv))
```

```python
import numpy as np
import jax
import jax.numpy as jnp
from jax import lax
from jax.experimental import pallas as pl
from jax.experimental.pallas import tpu as pltpu

F32, BF16 = jnp.float32, jnp.bfloat16
MESH = pl.DeviceIdType.MESH

S = 2048
D = 1024
PW = 3072
N_CHIP = 4
WIN_C = PW // N_CHIP
DFF = 2816
FF_C = DFF // N_CHIP
WOUT_R = D // N_CHIP
RMS_EPS = 1e-6
GN_EPS = 1e-5
RET_C = 128
RET_PER_STEP = 4
RET_SCALE = 32 ** -0.5
ATT_BLK = 128
ATT_SCALE = 64 ** -0.5
PATTERN_DILATIONS = (1, 4, 16)
NEG = -1e30
VMEM_LIMIT = 56 * 1024 * 1024

ADAM_LR, ADAM_B1, ADAM_B2, ADAM_EPS, ADAM_WD, ADAM_STEP = 0.001, 0.9, 0.999, 1e-08, 0.01, 10


def _params(*sem):
    return pltpu.CompilerParams(dimension_semantics=sem, vmem_limit_bytes=VMEM_LIMIT)


def _nt(a, b):
    return lax.dot_general(a, b, (((1,), (1,)), ((), ())), preferred_element_type=F32)


def _tn(a, b):
    return lax.dot_general(a, b, (((0,), (0,)), ((), ())), preferred_element_type=F32)


def _nn(a, b):
    return jnp.dot(a, b, preferred_element_type=F32)


def _rstd(v):
    return lax.rsqrt(jnp.mean(v * v, axis=-1, keepdims=True) + RMS_EPS)


def _sigmoid(v):
    return 1.0 / (1.0 + jnp.exp(-v))


def _rows(i, t):
    return pl.ds(pl.multiple_of(i * t, t), t)


def _retention_tables():
    h = np.arange(8, dtype=np.float32)
    log_g = np.log1p(-np.exp2(-5.0 - h)).astype(np.float32)
    idx = np.arange(RET_C, dtype=np.float32)
    diff = idx[:, None] - idx[None, :]
    dtab = np.where(diff >= 0, np.exp(log_g[:, None, None] * np.maximum(diff, 0.0)), 0.0).astype(np.float32)
    dtab = dtab.reshape(8 * RET_C, RET_C)
    lane_head = np.arange(256) // 32
    a_tab = np.exp(log_g[lane_head][None, :] * (idx + 1.0)[:, None]).astype(np.float32)
    b_tab = np.exp(log_g[lane_head][None, :] * (RET_C - 1.0 - idx)[:, None]).astype(np.float32)
    lam = np.exp(log_g[lane_head] * RET_C).astype(np.float32)[:, None]
    bd = (lane_head[:, None] == (np.arange(512) // 64)[None, :]).astype(np.float32)
    return dtab, a_tab, b_tab, lam, bd


def _rotary_tables():
    inv_r = (1.0 / (np.float32(10000.0) ** np.linspace(0.0, 1.0, 16, dtype=np.float32))).astype(np.float32)
    inv_a = (np.float32(500000.0) ** (-np.arange(0, 16, 2, dtype=np.float32) / np.float32(16))).astype(np.float32)
    ifc = np.zeros((1, 128), np.float32)
    ifc[0, 0:16], ifc[0, 16:24] = inv_r, inv_a
    spread = np.zeros((128, 768), np.float32)
    for lane in range(256):
        spread[(lane % 32) % 16, lane] = 1.0
    for lane in range(512):
        d = lane % 64
        spread[16 + d % 8 if d < 16 else 24, 256 + lane] = 1.0
    return ifc, spread


def _rot_halves(tm):
    lo_r = (lax.broadcasted_iota(jnp.int32, (tm, 256), 1) % 32) < 16
    lo_a = (lax.broadcasted_iota(jnp.int32, (tm, 512), 1) % 64) < 8
    return lo_r, lo_a


def _spread_exact(t, e):
    hi = t.astype(BF16)
    r1 = t - hi.astype(F32)
    mid = r1.astype(BF16)
    lo = (r1 - mid.astype(F32)).astype(BF16)
    return _nn(hi, e) + _nn(mid, e) + _nn(lo, e)


def _proj_fwd(h1, win_g, cos, sin, after):
    tm = 256

    def body(h_ref, w_ref, cos_ref, sin_ref, qr_ref, kr_ref, rv_ref, rg_ref, aq_ref, ak_ref, av_ref, p_ref, _):
        h = h_ref[...]
        for k in range(N_CHIP):
            p_ref[:, k * WIN_C:(k + 1) * WIN_C] = _nn(h, w_ref[k])
        cr, ca, sr, sa = cos_ref[:, 0:256], cos_ref[:, 256:768], sin_ref[:, 0:256], sin_ref[:, 256:768]
        lo_r, lo_a = _rot_halves(tm)

        def rot_r(v):
            return v * cr + sr * jnp.where(lo_r, -pltpu.roll(v, 240, 1), pltpu.roll(v, 16, 1))

        def rot_a(v):
            return v * ca + sa * jnp.where(lo_a, -pltpu.roll(v, 504, 1), pltpu.roll(v, 8, 1))

        qr_ref[...] = rot_r(p_ref[:, 0:256]).astype(BF16)
        kr_ref[...] = (rot_r(p_ref[:, 256:512]) * RET_SCALE).astype(BF16)
        rv_ref[...] = p_ref[:, 512:1024].astype(BF16)
        rg_ref[...] = p_ref[:, 1024:1536]
        aq, ak = rot_a(p_ref[:, 1536:2048]), rot_a(p_ref[:, 2048:2560])
        for j in range(4):
            aq_ref[j] = aq[:, 128 * j:128 * j + 128]
            ak_ref[j] = ak[:, 128 * j:128 * j + 128]
            av_ref[j] = p_ref[:, 2560 + 128 * j:2560 + 128 * j + 128]

    row = lambda w: pl.BlockSpec((tm, w), lambda i: (i, 0))
    slab = pl.BlockSpec((4, tm, 128), lambda i: (0, i, 0))
    return _carry(
        "proj_fwd", body, _NoExchange(), (), (h1, win_g, cos, sin),
        [row(D), pl.BlockSpec((N_CHIP, D, WIN_C), lambda i: (0, 0, 0)), row(768), row(768)],
        [row(256), row(256), row(512), row(512), slab, slab, slab],
        [jax.ShapeDtypeStruct((S, w), BF16) for w in (256, 256, 512)]
        + [jax.ShapeDtypeStruct((S, 512), F32)] + [jax.ShapeDtypeStruct((4, S, 128), F32)] * 3,
        scratch_shapes=[pltpu.VMEM((tm, PW), F32)], grid=(S // tm,), semantics=("parallel",), after=after)[0]


def _seg_mean(v):
    lo = lax.broadcasted_iota(jnp.int32, v.shape, 1) < 64
    s_lo = jnp.sum(jnp.where(lo, v, 0.0), axis=-1, keepdims=True)
    s_hi = jnp.sum(jnp.where(lo, 0.0, v), axis=-1, keepdims=True)
    return jnp.where(lo, s_lo, s_hi) * (1.0 / 64.0)


def _ret_fwd(qr, kr, rv, proj, tabs, exchange, exchange_args, after=None):
    C, G = RET_C, RET_PER_STEP
    steps = S // (C * G)
    dtab, a_tab, b_tab, lam, bd = tabs

    def body(q_ref, k_ref, v_ref, g_ref, dt_ref, a_ref, b_ref, lam_ref, bd_ref, o_ref, cat_ref, st_ref, R, exch):
        @pl.when(pl.program_id(0) == 0)
        def _():
            exch.start()
            R[...] = jnp.zeros_like(R)

        lane_head = lax.broadcasted_iota(jnp.int32, (C, 256), 1) // 32
        col_head = lax.broadcasted_iota(jnp.int32, (C, 256), 1) // 64
        for s in range(G):
            rows = slice(s * C, (s + 1) * C)
            q, k, v = q_ref[rows, :], k_ref[rows, :], v_ref[rows, :]
            rb = R[...].astype(BF16)
            st_ref[s] = rb
            qa = (q.astype(F32) * a_ref[...]).astype(BF16)
            cross = _nn(qa, rb)
            p = (_nt(_stack_heads(q, lane_head, n=8), k) * dt_ref[...]).astype(BF16)
            og = [cross[:, 256 * g:256 * g + 256]
                  + _unstack_heads(_nn(p[4 * C * g:4 * C * (g + 1)], v[:, 256 * g:256 * g + 256]), col_head)
                  for g in range(2)]
            kb = (k.astype(F32) * b_ref[...]).astype(BF16)
            R[...] = R[...] * lam_ref[...] + _tn(kb, v) * bd_ref[...]
            o_ref[rows, 0:256] = og[0]
            o_ref[rows, 256:512] = og[1]
            for j in range(4):
                oj = og[j // 2][:, 128 * (j % 2):128 * (j % 2) + 128]
                xc = oj - _seg_mean(oj)
                rn = xc * lax.rsqrt(_seg_mean(xc * xc) + GN_EPS)
                gj = g_ref[rows, 128 * j:128 * j + 128]
                cat_ref[rows, 128 * j:128 * j + 128] = (rn * (gj * _sigmoid(gj))).astype(BF16)

        @pl.when(pl.program_id(0) == steps - 1)
        def _():
            exch.middle()
            exch.finish()

    row = lambda w: pl.BlockSpec((C * G, w), lambda n: (n, 0))
    full = lambda a: pl.BlockSpec(a.shape, lambda n: (0,) * a.ndim)
    return _carry(
        "ret_fwd", body, exchange, exchange_args, (qr, kr, rv, proj, dtab, a_tab, b_tab, lam, bd),
        [row(256), row(256), row(512), row(512),
         full(dtab), full(a_tab), full(b_tab), full(lam), full(bd)],
        [row(512), row(512), pl.BlockSpec((G, 256, 512), lambda n: (n, 0, 0))],
        [jax.ShapeDtypeStruct((S, 512), F32), jax.ShapeDtypeStruct((S, 512), BF16),
         jax.ShapeDtypeStruct((S // C, 256, 512), BF16)],
        scratch_shapes=[pltpu.VMEM((256, 512), F32)], grid=(steps,), semantics=("arbitrary",), after=after)


def _stack_heads(v, lane_head, fill=0.0, n=4):
    return jnp.concatenate([jnp.where(lane_head == h, v, jnp.full_like(v, fill)) for h in range(n)], axis=0)


def _unstack_heads(v, lane_head, n=4):
    out = v[0:ATT_BLK]
    for h in range(1, n):
        out = jnp.where(lane_head == h, v[h * ATT_BLK:(h + 1) * ATT_BLK], out)
    return out


def _att_bias(has_prev):
    nk = 2 * ATT_BLK if has_prev else ATT_BLK
    a = lax.broadcasted_iota(jnp.int32, (4 * ATT_BLK, nk), 0) % ATT_BLK
    kk = lax.broadcasted_iota(jnp.int32, (4 * ATT_BLK, nk), 1)
    if not has_prev:
        return None, jnp.where((a - kk) >= 0, 0.0, NEG)
    dist = ATT_BLK + a - kk
    inside = (dist >= 0) & (dist <= ATT_BLK)
    return jnp.where(inside, 0.0, NEG), jnp.where(inside & (kk >= ATT_BLK), 0.0, NEG)


def _class_rows(ib, r, d):
    if d == 1:
        return pl.ds(pl.multiple_of(ib * ATT_BLK, ATT_BLK), ATT_BLK)
    return pl.ds(ib * ATT_BLK * d + r, ATT_BLK, stride=d)


def _slab_pair(ref, g, rows):
    return jnp.concatenate([ref[2 * g, rows, :], ref[2 * g + 1, rows, :]], axis=1)


def _att_blocks(d):
    nb = S // d // ATT_BLK
    return nb, nb > 1


def _att_fwd(aq, ak, av, exchange, exchange_args):
    def body(q_ref, k_ref, v_ref, o_ref, l_ref, cat_ref, xc):
        xc.start()
        lane_head = lax.broadcasted_iota(jnp.int32, (ATT_BLK, 256), 1) // 64
        for pi, d in enumerate(PATTERN_DILATIONS):
            if pi == len(PATTERN_DILATIONS) - 1:
                xc.middle()
            nb, has_prev = _att_blocks(d)
            bias_rest, bias_first = _att_bias(has_prev)

            def block(b, carry, pi=pi, d=d, nb=nb, has_prev=has_prev, bias_rest=bias_rest, bias_first=bias_first):
                r, ib = b // nb, b % nb
                rows = _class_rows(ib, r, d)
                prow = _class_rows(jnp.maximum(ib - 1, 0), r, d)
                bias = jnp.where(ib == 0, bias_first, bias_rest) if has_prev else bias_first
                for g in range(2):
                    qg = _slab_pair(q_ref, g, rows).astype(BF16)
                    kg = _slab_pair(k_ref, g, rows)
                    vg = _slab_pair(v_ref, g, rows)
                    if has_prev:
                        kg = jnp.concatenate([_slab_pair(k_ref, g, prow), kg], axis=0)
                        vg = jnp.concatenate([_slab_pair(v_ref, g, prow), vg], axis=0)
                    kg, vg = kg.astype(BF16), vg.astype(BF16)
                    s = _nt(_stack_heads(qg, lane_head), kg) * ATT_SCALE + bias
                    m = jnp.max(s, axis=-1, keepdims=True)
                    p = jnp.exp(s - m)
                    den = jnp.sum(p, axis=-1, keepdims=True)
                    og = _unstack_heads(_nn(p.astype(BF16), vg) / den, lane_head)
                    lg = _unstack_heads(jnp.broadcast_to(m + jnp.log(den), (4 * ATT_BLK, 256)), lane_head)
                    for jj in range(2):
                        j = 2 * g + jj
                        o_new, l_new = og[:, 128 * jj:128 * jj + 128], lg[:, 128 * jj:128 * jj + 128]
                        if pi > 0:
                            o_old, l_old = o_ref[j, rows, :], l_ref[j, rows, :]
                            mx = jnp.maximum(l_old, l_new)
                            ea, eb = jnp.exp(l_old - mx), jnp.exp(l_new - mx)
                            den = ea + eb
                            o_new = (ea * o_old + eb * o_new) / den
                            l_new = mx + jnp.log(den)
                        o_ref[j, rows, :] = o_new
                        l_ref[j, rows, :] = l_new
                return carry

            lax.fori_loop(0, S // ATT_BLK, block, 0, unroll=4)

        def to_cat(i, carry):
            rows = _rows(i, 256)
            for j in range(4):
                cat_ref[rows, 128 * j:128 * j + 128] = o_ref[j, rows, :].astype(BF16)
            return carry

        lax.fori_loop(0, S // 256, to_cat, 0)
        xc.finish()

    slab = jax.ShapeDtypeStruct((4, S, 128), F32)
    return _carry("att_fwd", body, exchange, exchange_args, (aq, ak, av), [VMEM] * 3, [VMEM] * 3,
                  [slab, slab, jax.ShapeDtypeStruct((S, 512), BF16)])


def _mix_fwd(cat_r, cat_a, wout, x, g2, g3, exchange, exchange_args):
    tm = 512

    def body(cr_ref, ca_ref, w_ref, x_ref, g2_ref, g3_ref, mix_ref, x2_ref, h3_ref, xc):
        @pl.when(pl.program_id(0) == 0)
        def _():
            xc.start()

        mix = _nn(cr_ref[...], w_ref[0:512, :]) + _nn(ca_ref[...], w_ref[512:1024, :])
        mix_ref[...] = mix
        x2 = x_ref[...] + mix * _rstd(mix) * g2_ref[...]
        x2_ref[...] = x2
        h3_ref[...] = (x2 * _rstd(x2) * g3_ref[...]).astype(BF16)

        @pl.when(pl.program_id(0) == S // tm - 1)
        def _():
            xc.middle()
            xc.finish()

    row = lambda w: pl.BlockSpec((tm, w), lambda i: (i, 0))
    vec = pl.BlockSpec((1, D), lambda i: (0, 0))
    return _carry("mix_fwd", body, exchange, exchange_args, (cat_r, cat_a, wout, x, g2, g3),
                  [row(512), row(512), pl.BlockSpec((D, D), lambda i: (0, 0)), row(D), vec, vec],
                  [row(D), row(D), row(D)],
                  [jax.ShapeDtypeStruct((S, D), F32), jax.ShapeDtypeStruct((S, D), F32),
                   jax.ShapeDtypeStruct((S, D), BF16)],
                  grid=(S // tm,), semantics=("arbitrary",))


def _ffn_fwd(h3, wg, wu, wd, x2, tgt, g4):
    tm = 512
    last = N_CHIP - 1

    def body(h_ref, wg_ref, wu_ref, wd_ref, x2_ref, t_ref, g_ref,
             gt_ref, up_ref, a_ref, loss_ref, dy_ref, df_ref, dg_ref, f_ref):
        k, i = pl.program_id(0), pl.program_id(1)
        h = h_ref[...]
        gt = _nt(h, wg_ref[...])
        up = _nt(h, wu_ref[...])
        gt_ref[...] = gt.astype(BF16)
        up_ref[...] = up.astype(BF16)
        a = (gt * _sigmoid(gt) * up).astype(BF16)
        a_ref[...] = a
        part = _nn(a, wd_ref[...])
        rows = _rows(i, tm)

        @pl.when(k == 0)
        def _():
            f_ref[rows, :] = part

        @pl.when((k > 0) & (k < last))
        def _():
            f_ref[rows, :] = f_ref[rows, :] + part

        @pl.when((k == last) & (i == 0))
        def _():
            loss_ref[...] = jnp.zeros_like(loss_ref)
            dg_ref[...] = jnp.zeros_like(dg_ref)

        @pl.when(k == last)
        def _():
            fv = f_ref[rows, :] + part
            r = _rstd(fv)
            fn = fv * r
            e = x2_ref[...] + fn * g_ref[...] - t_ref[...]
            loss_ref[...] = loss_ref[...] + jnp.sum(jnp.sum(e * e, axis=-1, keepdims=True), axis=0, keepdims=True)
            dy = e * (1.0 / D)
            dy_ref[...] = dy
            dg_ref[...] = dg_ref[...] + jnp.sum(dy * fn, axis=0, keepdims=True)
            t = dy * g_ref[...]
            df_ref[...] = (r * (t - fn * jnp.mean(t * fn, axis=-1, keepdims=True))).astype(BF16)

    wrow = pl.BlockSpec((None, FF_C, D), lambda k, i: (k, 0, 0))
    act = pl.BlockSpec((None, tm, FF_C), lambda k, i: (k, i, 0))
    late = pl.BlockSpec((tm, D), lambda k, i: (jnp.where(k == last, i, 0), 0))
    vec = pl.BlockSpec((1, D), lambda k, i: (0, 0))
    return pl.pallas_call(
        body, grid=(N_CHIP, S // tm), name="ffn_fwd",
        in_specs=[pl.BlockSpec((tm, D), lambda k, i: (i, 0)), wrow, wrow, wrow, late, late, vec],
        out_specs=[act, act, act, vec, late, late, vec],
        out_shape=[jax.ShapeDtypeStruct((N_CHIP, S, FF_C), BF16)] * 3
                  + [jax.ShapeDtypeStruct((1, D), F32), jax.ShapeDtypeStruct((S, D), F32),
                     jax.ShapeDtypeStruct((S, D), BF16), jax.ShapeDtypeStruct((1, D), F32)],
        scratch_shapes=[pltpu.VMEM((S, D), F32)],
        compiler_params=_params("arbitrary", "arbitrary"),
    )(h3, wg, wu, wd, x2, tgt, g4)


def _ffn_bwd_act(df, gt, up, wg, wu, wd, dy, x2, mix, g2, g3):
    tm, sub = 512, 256
    last = N_CHIP - 1

    def body(df_ref, gt_ref, up_ref, wg_ref, wu_ref, wd_ref, dy_ref, x2_ref, mix_ref, g2_ref, g3_ref,
             dgt_ref, dup_ref, dx2_ref, dmix_ref, dg3_ref, dg2_ref, dh_ref):
        k, i = pl.program_id(0), pl.program_id(1)
        parts = []
        for s in range(tm // sub):
            rows = slice(s * sub, (s + 1) * sub)
            da = _nt(df_ref[rows, :], wd_ref[...])
            gt, up = gt_ref[rows, :].astype(F32), up_ref[rows, :].astype(F32)
            sg = _sigmoid(gt)
            dup = (da * gt * sg).astype(BF16)
            dgt = (da * up * (sg * (1.0 + gt * (1.0 - sg)))).astype(BF16)
            dup_ref[rows, :] = dup
            dgt_ref[rows, :] = dgt
            parts.append(_nn(dgt, wg_ref[...]) + _nn(dup, wu_ref[...]))
        part = jnp.concatenate(parts, axis=0)
        rows = _rows(i, tm)

        @pl.when(k == 0)
        def _():
            dh_ref[rows, :] = part

        @pl.when((k > 0) & (k < last))
        def _():
            dh_ref[rows, :] = dh_ref[rows, :] + part

        @pl.when((k == last) & (i == 0))
        def _():
            dg3_ref[...] = jnp.zeros_like(dg3_ref)
            dg2_ref[...] = jnp.zeros_like(dg2_ref)

        @pl.when(k == last)
        def _():
            dh = dh_ref[rows, :] + part
            x2 = x2_ref[...]
            r3 = _rstd(x2)
            xn = x2 * r3
            dg3_ref[...] = dg3_ref[...] + jnp.sum(dh * xn, axis=0, keepdims=True)
            t = dh * g3_ref[...]
            dx2 = dy_ref[...] + r3 * (t - xn * jnp.mean(t * xn, axis=-1, keepdims=True))
            dx2_ref[...] = dx2
            mix = mix_ref[...]
            r2 = _rstd(mix)
            mn = mix * r2
            dg2_ref[...] = dg2_ref[...] + jnp.sum(dx2 * mn, axis=0, keepdims=True)
            u = dx2 * g2_ref[...]
            dmix_ref[...] = (r2 * (u - mn * jnp.mean(u * mn, axis=-1, keepdims=True))).astype(BF16)

    wrow = pl.BlockSpec((None, FF_C, D), lambda k, i: (k, 0, 0))
    act = pl.BlockSpec((None, tm, FF_C), lambda k, i: (k, i, 0))
    row = pl.BlockSpec((tm, D), lambda k, i: (i, 0))
    late = pl.BlockSpec((tm, D), lambda k, i: (jnp.where(k == last, i, 0), 0))
    vec = pl.BlockSpec((1, D), lambda k, i: (0, 0))
    return pl.pallas_call(
        body, grid=(N_CHIP, S // tm), name="ffn_bwd_act",
        in_specs=[row, act, act, wrow, wrow, wrow, late, late, late, vec, vec],
        out_specs=[act, act, late, late, vec, vec],
        out_shape=[jax.ShapeDtypeStruct((N_CHIP, S, FF_C), BF16), jax.ShapeDtypeStruct((N_CHIP, S, FF_C), BF16),
                   jax.ShapeDtypeStruct((S, D), F32), jax.ShapeDtypeStruct((S, D), BF16),
                   jax.ShapeDtypeStruct((1, D), F32), jax.ShapeDtypeStruct((1, D), F32)],
        scratch_shapes=[pltpu.VMEM((S, D), F32)],
        compiler_params=_params("arbitrary", "arbitrary"),
    )(df, gt, up, wg, wu, wd, dy, x2, mix, g2, g3)


def _ffn_bwd_w(a, df, h3, dgt, dup):
    tm = 1024
    assert S // tm == 2

    def body(a_ref, df_ref, h_ref, dgt_ref, dup_ref, dwd_ref, dwg_ref, dwu_ref, acc_d, acc_g, acc_u):
        i = pl.program_id(1)
        h = h_ref[...]
        parts = (_tn(a_ref[...], df_ref[...]), _tn(dgt_ref[...], h), _tn(dup_ref[...], h))

        @pl.when(i == 0)
        def _():
            for acc, part in zip((acc_d, acc_g, acc_u), parts):
                acc[...] = part

        @pl.when(i == S // tm - 1)
        def _():
            for out, acc, part in zip((dwd_ref, dwg_ref, dwu_ref), (acc_d, acc_g, acc_u), parts):
                out[...] = (acc[...] + part).astype(BF16)

    act = pl.BlockSpec((None, tm, FF_C), lambda k, i: (k, i, 0))
    row = pl.BlockSpec((tm, D), lambda k, i: (i, 0))
    wrow = pl.BlockSpec((None, FF_C, D), lambda k, i: (k, 0, 0))
    return pl.pallas_call(
        body, grid=(N_CHIP, S // tm), name="ffn_bwd_w",
        in_specs=[act, row, row, act, act],
        out_specs=[wrow, wrow, wrow],
        out_shape=[jax.ShapeDtypeStruct((N_CHIP, FF_C, D), BF16)] * 3,
        scratch_shapes=[pltpu.VMEM((FF_C, D), F32)] * 3,
        compiler_params=_params("parallel", "arbitrary"),
    )(a, df, h3, dgt, dup)


def _mix_bwd(dmix, cat_r, cat_a, wout, exchange, exchange_args):
    tm = 1024

    def body(dm_ref, cr_ref, ca_ref, w_ref, dret_ref, datt_ref, dw_ref, acc, xc):
        i = pl.program_id(0)

        @pl.when(i == 0)
        def _():
            xc.start()
            acc[...] = jnp.zeros_like(acc)

        dm = dm_ref[...]
        dret_ref[...] = _nt(dm, w_ref[0:512, :])
        datt = _nt(dm, w_ref[512:1024, :])
        for j in range(4):
            datt_ref[j] = datt[:, 128 * j:128 * j + 128]
        acc[0:512, :] += _tn(cr_ref[...], dm)
        acc[512:1024, :] += _tn(ca_ref[...], dm)

        @pl.when(i == S // tm - 1)
        def _():
            dw_ref[...] = acc[...].astype(BF16)
            xc.middle()
            xc.finish()

    row = lambda w: pl.BlockSpec((tm, w), lambda i: (i, 0))
    full = pl.BlockSpec((D, D), lambda i: (0, 0))
    return _carry("mix_bwd", body, exchange, exchange_args, (dmix, cat_r, cat_a, wout),
                  [row(D), row(512), row(512), full],
                  [row(512), pl.BlockSpec((4, tm, 128), lambda i: (0, i, 0)), full],
                  [jax.ShapeDtypeStruct((S, 512), F32), jax.ShapeDtypeStruct((4, S, 128), F32),
                   jax.ShapeDtypeStruct((D, D), BF16)],
                  scratch_shapes=[pltpu.VMEM((D, D), F32)], grid=(S // tm,), semantics=("arbitrary",))


def _att_bwd(aq, ak, av, datt, att_out, lse, exchange, exchange_args, after=None):
    def body(q_ref, k_ref, v_ref, do_ref, out_ref, l_ref, dq_ref, dk_ref, dv_ref, xc):
        xc.start()

        def clear(i, carry):
            rows = _rows(i, 256)
            for ref in (dq_ref, dk_ref, dv_ref):
                for j in range(4):
                    ref[j, rows, :] = jnp.zeros((256, 128), F32)
            return carry

        lax.fori_loop(0, S // 256, clear, 0)
        lane_head = lax.broadcasted_iota(jnp.int32, (ATT_BLK, 256), 1) // 64
        for d in PATTERN_DILATIONS:
            nb, has_prev = _att_blocks(d)
            bias_rest, bias_first = _att_bias(has_prev)

            def block(b, carry, d=d, nb=nb, has_prev=has_prev, bias_rest=bias_rest, bias_first=bias_first):
                r, ib = b // nb, b % nb
                rows = _class_rows(ib, r, d)
                prow = _class_rows(jnp.maximum(ib - 1, 0), r, d)
                bias = jnp.where(ib == 0, bias_first, bias_rest) if has_prev else bias_first
                for g in range(2):
                    qg = _slab_pair(q_ref, g, rows).astype(BF16)
                    kg = _slab_pair(k_ref, g, rows)
                    vg = _slab_pair(v_ref, g, rows)
                    if has_prev:
                        kg = jnp.concatenate([_slab_pair(k_ref, g, prow), kg], axis=0)
                        vg = jnp.concatenate([_slab_pair(v_ref, g, prow), vg], axis=0)
                    kg, vg = kg.astype(BF16), vg.astype(BF16)
                    dog = _slab_pair(do_ref, g, rows)
                    outg = _slab_pair(out_ref, g, rows)
                    lg = _slab_pair(l_ref, g, rows)
                    qs = _stack_heads(qg, lane_head)
                    dos = _stack_heads(dog, lane_head)
                    delta = jnp.sum(dos * jnp.concatenate([outg] * 4, axis=0), axis=-1, keepdims=True)
                    lh = jnp.max(_stack_heads(lg, lane_head, NEG), axis=-1, keepdims=True)
                    s = _nt(qs, kg) * ATT_SCALE + bias
                    p = jnp.exp(s - lh)
                    dosb = dos.astype(BF16)
                    ds = (p * (_nt(dosb, vg) - delta) * ATT_SCALE).astype(BF16)
                    dq = _unstack_heads(_nn(ds, kg), lane_head)
                    dk = _tn(ds, qs)
                    dv = _tn(p.astype(BF16), dosb)
                    for jj in range(2):
                        j, sl = 2 * g + jj, slice(128 * jj, 128 * jj + 128)
                        dq_ref[j, rows, :] += dq[:, sl]
                        if has_prev:
                            dk_ref[j, prow, :] += dk[0:ATT_BLK, sl]
                            dv_ref[j, prow, :] += dv[0:ATT_BLK, sl]
                            dk_ref[j, rows, :] += dk[ATT_BLK:2 * ATT_BLK, sl]
                            dv_ref[j, rows, :] += dv[ATT_BLK:2 * ATT_BLK, sl]
                        else:
                            dk_ref[j, rows, :] += dk[:, sl]
                            dv_ref[j, rows, :] += dv[:, sl]
                return carry

            lax.fori_loop(0, S // ATT_BLK, block, 0, unroll=4)
        xc.middle()
        xc.finish()

    slab = jax.ShapeDtypeStruct((4, S, 128), F32)
    return _carry("att_bwd", body, exchange, exchange_args, (aq, ak, av, datt, att_out, lse), [VMEM] * 6, [VMEM] * 3,
                  [slab, slab, slab], after=after)


def _ret_bwd(qr, kr, rv, proj, o_raw, states, dret, tabs, exchange, exchange_args, after=None):
    C, G = RET_C, RET_PER_STEP
    steps = S // (C * G)
    dtab, a_tab, b_tab, lam, bd = tabs

    def body(q_ref, k_ref, v_ref, g_ref, o_ref, st_ref, dr_ref, dt_ref, a_ref, b_ref, lam_ref, bd_ref,
             dq_ref, dk_ref, dv_ref, dg_ref, dR, exch):
        @pl.when(pl.program_id(0) == 0)
        def _():
            exch.start()
            dR[...] = jnp.zeros_like(dR)

        lane_head = lax.broadcasted_iota(jnp.int32, (C, 256), 1) // 32
        col_head = lax.broadcasted_iota(jnp.int32, (C, 256), 1) // 64
        for s in reversed(range(G)):
            rows = slice(s * C, (s + 1) * C)
            q, k, v = q_ref[rows, :], k_ref[rows, :], v_ref[rows, :]
            dos = []
            for j in range(4):
                sl = slice(128 * j, 128 * j + 128)
                oj = o_ref[rows, sl]
                xc = oj - _seg_mean(oj)
                rs = lax.rsqrt(_seg_mean(xc * xc) + GN_EPS)
                rn = xc * rs
                gj = g_ref[rows, sl]
                sg = _sigmoid(gj)
                dret = dr_ref[rows, sl]
                dg_ref[rows, sl] = dret * rn * (sg * (1.0 + gj * (1.0 - sg)))
                drn = dret * (gj * sg)
                dos.append(rs * (drn - _seg_mean(drn) - rn * _seg_mean(drn * rn)))
            do = [jnp.concatenate(dos[0:2], axis=1), jnp.concatenate(dos[2:4], axis=1)]
            do8 = jnp.concatenate(do, axis=1).astype(BF16)
            drb = dR[...].astype(BF16)
            rb = st_ref[s]
            dq = _nt(do8, rb) * a_ref[...]
            dk = _nt(v, drb) * b_ref[...]
            kb = (k.astype(F32) * b_ref[...]).astype(BF16)
            dvall = _nn(kb, drb)
            qs = _stack_heads(q, lane_head, n=8)
            dec = dt_ref[...]
            p = (_nt(qs, k) * dec).astype(BF16)
            dos = [_stack_heads(do[g], col_head).astype(BF16) for g in range(2)]
            dp = jnp.concatenate([_nt(dos[g], v[:, 256 * g:256 * g + 256]) for g in range(2)], axis=0)
            ds = (dp * dec).astype(BF16)
            dq = dq + _unstack_heads(_nn(ds, k), lane_head, n=8)
            dk = dk + _tn(ds, qs)
            dv = [dvall[:, 256 * g:256 * g + 256] + _tn(p[4 * C * g:4 * C * (g + 1)], dos[g]) for g in range(2)]
            qa = (q.astype(F32) * a_ref[...]).astype(BF16)
            dR[...] = dR[...] * lam_ref[...] + _tn(qa, do8) * bd_ref[...]
            dq_ref[rows, :] = dq
            dk_ref[rows, :] = dk
            dv_ref[rows, 0:256] = dv[0]
            dv_ref[rows, 256:512] = dv[1]

        @pl.when(pl.program_id(0) == steps - 1)
        def _():
            exch.middle()
            exch.finish()

    rev = lambda w: pl.BlockSpec((C * G, w), lambda n: (steps - 1 - n, 0))
    full = lambda a: pl.BlockSpec(a.shape, lambda n: (0,) * a.ndim)
    return _carry(
        "ret_bwd", body, exchange, exchange_args, (qr, kr, rv, proj, o_raw, states, dret, dtab, a_tab, b_tab, lam, bd),
        [rev(256), rev(256), rev(512), rev(512), rev(512),
         pl.BlockSpec((G, 256, 512), lambda n: (steps - 1 - n, 0, 0)), rev(512),
         full(dtab), full(a_tab), full(b_tab), full(lam), full(bd)],
        [rev(256), rev(256), rev(512), rev(512)],
        [jax.ShapeDtypeStruct((S, 256), F32), jax.ShapeDtypeStruct((S, 256), F32),
         jax.ShapeDtypeStruct((S, 512), F32), jax.ShapeDtypeStruct((S, 512), F32)],
        scratch_shapes=[pltpu.VMEM((256, 512), F32)], grid=(steps,), semantics=("arbitrary",), after=after)


def _rot_bwd(cos, sin, dqr, dkr, drv, drg, dq_att, dk_att, dv_att):
    tm = 256

    def body(cos_ref, sin_ref, dqr_ref, dkr_ref, drv_ref, drg_ref, dqa_ref, dka_ref, dva_ref, dp_ref):
        cr, ca, sr, sa = cos_ref[:, 0:256], cos_ref[:, 256:768], sin_ref[:, 0:256], sin_ref[:, 256:768]
        lo_r, lo_a = _rot_halves(tm)

        def unrot_r(g):
            gs = g * sr
            return g * cr + pltpu.roll(jnp.where(lo_r, -gs, 0.0), 16, 1) + pltpu.roll(jnp.where(lo_r, 0.0, gs), 240, 1)

        def unrot_a(g):
            gs = g * sa
            return g * ca + pltpu.roll(jnp.where(lo_a, -gs, 0.0), 8, 1) + pltpu.roll(jnp.where(lo_a, 0.0, gs), 504, 1)

        def wide(ref):
            return jnp.concatenate([ref[j] for j in range(4)], axis=1)

        dp_ref[:, 0:256] = unrot_r(dqr_ref[...]).astype(BF16)
        dp_ref[:, 256:512] = unrot_r(dkr_ref[...] * RET_SCALE).astype(BF16)
        dp_ref[:, 512:1024] = drv_ref[...].astype(BF16)
        dp_ref[:, 1024:1536] = drg_ref[...].astype(BF16)
        dp_ref[:, 1536:2048] = unrot_a(wide(dqa_ref)).astype(BF16)
        dp_ref[:, 2048:2560] = unrot_a(wide(dka_ref)).astype(BF16)
        dp_ref[:, 2560:3072] = wide(dva_ref).astype(BF16)

    row = lambda w: pl.BlockSpec((tm, w), lambda i: (i, 0))
    slab = pl.BlockSpec((4, tm, 128), lambda i: (0, i, 0))
    return pl.pallas_call(
        body, grid=(S // tm,), name="rot_bwd",
        in_specs=[row(768), row(768), row(256), row(256), row(512), row(512), slab, slab, slab],
        out_specs=row(PW), out_shape=jax.ShapeDtypeStruct((S, PW), BF16),
        compiler_params=_params("parallel"),
    )(cos, sin, dqr, dkr, drv, drg, dq_att, dk_att, dv_att)


def _win_bwd_w(h1, dproj, exchange, exchange_args):
    def body(h_ref, dp_ref, dw_ref, xc):
        k = pl.program_id(0)

        @pl.when(k == 0)
        def _():
            xc.start()

        dw_ref[...] = _tn(h_ref[...], dp_ref[...]).astype(BF16)

        @pl.when(k == N_CHIP - 1)
        def _():
            xc.middle()
            xc.finish()

    (dw,), out = _carry(
        "win_bwd_w", body, exchange, exchange_args, (h1, dproj),
        [pl.BlockSpec((S, D), lambda k: (0, 0)), pl.BlockSpec((S, WIN_C), lambda k: (0, k))],
        [pl.BlockSpec((None, D, WIN_C), lambda k: (k, 0, 0))],
        [jax.ShapeDtypeStruct((N_CHIP, D, WIN_C), BF16)], grid=(N_CHIP,), semantics=("arbitrary",))
    return dw, out


def _in_bwd(dproj, win_g, x, dx2, g1, after):
    tm = 512

    def body(dp_ref, w_ref, x_ref, dx2_ref, g_ref, dx_ref, dg_ref, _):
        @pl.when(pl.program_id(0) == 0)
        def _():
            dg_ref[...] = jnp.zeros_like(dg_ref)

        dh = _nt(dp_ref[:, 0:WIN_C], w_ref[0])
        for k in range(1, N_CHIP):
            dh = dh + _nt(dp_ref[:, k * WIN_C:(k + 1) * WIN_C], w_ref[k])
        xv = x_ref[...]
        r = _rstd(xv)
        xn = xv * r
        dg_ref[...] = dg_ref[...] + jnp.sum(dh * xn, axis=0, keepdims=True)
        t = dh * g_ref[...]
        dx_ref[...] = dx2_ref[...] + r * (t - xn * jnp.mean(t * xn, axis=-1, keepdims=True))

    row = lambda w: pl.BlockSpec((tm, w), lambda i: (i, 0))
    vec = pl.BlockSpec((1, D), lambda i: (0, 0))
    return _carry("in_bwd", body, _NoExchange(), (), (dproj, win_g, x, dx2, g1),
                  [row(PW), pl.BlockSpec((N_CHIP, D, WIN_C), lambda i: (0, 0, 0)), row(D), row(D), vec],
                  [row(D), vec], [jax.ShapeDtypeStruct((S, D), F32), jax.ShapeDtypeStruct((1, D), F32)],
                  grid=(S // tm,), semantics=("arbitrary",), after=after)[0]


ANY = pl.BlockSpec(memory_space=pl.ANY)
VMEM = pl.BlockSpec(memory_space=pltpu.VMEM)
FLIPS = ((1, 0), (0, 1), (1, 1))


def _place():
    x, y, c = lax.axis_index("x"), lax.axis_index("y"), lax.axis_index("c")
    chips = [((1 - x) if fx else x, (1 - y) if fy else y) for fx, fy in FLIPS]
    return x, y, c, 2 * x + y, chips


def _remote(src, dst, send_sem, recv_sem, device):
    return pltpu.make_async_remote_copy(src_ref=src, dst_ref=dst, send_sem=send_sem, recv_sem=recv_sem,
                                        device_id=device, device_id_type=MESH)


class _Exchange:
    aliases = {}

    def middle(self, ins, outs, sems):
        pass


class _GatherShards(_Exchange):
    def __init__(self, shards):
        n = self.n = len(shards)
        self.n_in = self.n_out = n
        self.out_shape = [jax.ShapeDtypeStruct((N_CHIP,) + s.shape, s.dtype) for s in shards]
        dma = pltpu.SemaphoreType.DMA
        self.scratch = [dma((3 * n,)), dma((3 * n,)), dma((3 * n,)), dma((3 * n,)), dma((n,)), dma((n,))]

    def _ici(self, ins, outs, sems, a, j, chip):
        x, y, c, me, chips = _place()
        half = ins[a].shape[0] // 2
        return _remote(ins[a].at[pl.ds(c * half, half), :], outs[a].at[me, pl.ds(c * half, half), :],
                       sems[0].at[3 * a + j], sems[1].at[3 * a + j], (*chip, c))

    def _fwd(self, outs, sems, a, j, chip, half_of):
        x, y, c, me, chips = _place()
        half = outs[a].shape[1] // 2
        blk = outs[a].at[2 * chip[0] + chip[1], pl.ds(half_of * half, half), :]
        return _remote(blk, blk, sems[2].at[3 * a + j], sems[3].at[3 * a + j], (x, y, 1 - c))

    def _own(self, ins, outs, sems, a):
        return _own_shard_to_sibling(ins[a], outs[a], sems[4].at[a], sems[5].at[a])

    def start(self, ins, outs, sems):
        chips = _place()[4]
        for a in range(self.n):
            for j, chip in enumerate(chips):
                self._ici(ins, outs, sems, a, j, chip).start()
        for a in range(self.n):
            self._own(ins, outs, sems, a).start()

    def middle(self, ins, outs, sems):
        x, y, c, me, chips = _place()
        for a in range(self.n):
            for j, chip in enumerate(chips):
                half = outs[a].shape[1] // 2
                blk = outs[a].at[2 * chip[0] + chip[1], pl.ds(c * half, half), :]
                _remote(blk, blk, sems[0].at[3 * a + j], sems[1].at[3 * a + j], (x, y, c)).wait_recv()
                self._fwd(outs, sems, a, j, chip, c).start()

    def finish(self, ins, outs, sems):
        x, y, c, me, chips = _place()
        for a in range(self.n):
            for j, chip in enumerate(chips):
                self._fwd(outs, sems, a, j, chip, 1 - c).wait_recv()
        for a in range(self.n):
            for j, chip in enumerate(chips):
                self._ici(ins, outs, sems, a, j, chip).wait_send()
                self._fwd(outs, sems, a, j, chip, c).wait_send()
            self._own(ins, outs, sems, a).wait()


def _own_shard_to_sibling(shard_ref, gathered_ref, send_sem, recv_sem):
    x, y, c, me, chips = _place()
    return _remote(shard_ref, gathered_ref.at[me], send_sem, recv_sem, (x, y, 1 - c))


class _NoExchange(_Exchange):
    n_in = n_out = 0
    out_shape = ()
    scratch = ()

    def start(self, ins, outs, sems):
        pass

    def finish(self, ins, outs, sems):
        pass


class _ForwardGathered(_Exchange):
    def __init__(self, shards, own=True, forward=True):
        self.own, self.forward = own, forward
        n = self.n = len(shards)
        self.n_in, self.n_out = 2 * n, n
        self.out_shape = [jax.ShapeDtypeStruct((N_CHIP,) + s.shape, s.dtype) for s in shards]
        dma = pltpu.SemaphoreType.DMA
        self.scratch = [dma((3 * n,)), dma((3 * n,)), dma((n,)), dma((n,))]
        self.aliases = {n + a: a for a in range(n)}

    def _fwd(self, outs, sems, a, j, chip, half_of):
        x, y, c, me, chips = _place()
        half = outs[a].shape[1] // 2
        blk = outs[a].at[2 * chip[0] + chip[1], pl.ds(half_of * half, half), :]
        return _remote(blk, blk, sems[0].at[3 * a + j], sems[1].at[3 * a + j], (x, y, 1 - c))

    def _own(self, ins, outs, sems, a):
        return _own_shard_to_sibling(ins[a], outs[a], sems[2].at[a], sems[3].at[a])

    def start(self, ins, outs, sems):
        x, y, c, me, chips = _place()
        for a in range(self.n):
            for j, chip in enumerate(chips if self.forward else ()):
                self._fwd(outs, sems, a, j, chip, c).start()
        for a in range(self.n if self.own else 0):
            self._own(ins, outs, sems, a).start()

    def finish(self, ins, outs, sems):
        x, y, c, me, chips = _place()
        for a in range(self.n):
            for j, chip in enumerate(chips if self.forward else ()):
                self._fwd(outs, sems, a, j, chip, 1 - c).wait_recv()
        for a in range(self.n):
            for j, chip in enumerate(chips if self.forward else ()):
                self._fwd(outs, sems, a, j, chip, c).wait_send()
            if self.own:
                self._own(ins, outs, sems, a).wait()


HBM = pl.BlockSpec(memory_space=pltpu.HBM)
SEMS = pl.BlockSpec(memory_space=pltpu.SEMAPHORE)
DATAFLOW = pltpu.SideEffectType.DATAFLOW_SIDE_EFFECTING


class _OverIci:
    def __init__(self, name, sources, lands):
        self.name, self.n = name, len(sources)
        hbm = lambda t: pltpu.with_memory_space_constraint(t, pltpu.HBM)
        self.arrays = [hbm(t) for t in sources] + [hbm(t) for t in lands]

    def sent(self, src, land, a, chip):
        raise NotImplementedError

    def landed(self, land, a, chip):
        raise NotImplementedError

    def _copy(self, arr, sems, a, j, receiving):
        x, y, c, me, chips = _place()
        src, dst = self.sent(arr[a], arr[self.n + a], a, chips[j])
        if receiving:
            dst = self.landed(arr[self.n + a], a, chips[j])
        return _remote(src, dst, sems[0].at[3 * a + j], sems[1].at[3 * a + j], (*chips[j], c))

    def start(self, after):
        m = len(self.arrays)

        def body(*refs):
            arr, sems, token = refs[:m], refs[m + 1:m + 3], refs[-1]
            for a in range(self.n):
                for j in range(3):
                    self._copy(arr, sems, a, j, False).start()
            token[...] = jnp.zeros_like(token)

        dma = pltpu.SemaphoreType.DMA
        outs = pl.pallas_call(
            body, name=self.name + "_start",
            out_shape=[dma((3 * self.n,)), dma((3 * self.n,))] + [pltpu.HBM(t.shape, t.dtype) for t in self.arrays]
                      + [jax.ShapeDtypeStruct((8, 128), F32)],
            in_specs=[HBM] * m + [ANY], out_specs=[SEMS, SEMS] + [HBM] * m + [VMEM],
            input_output_aliases={i: 2 + i for i in range(m)},
            compiler_params=pltpu.CompilerParams(has_side_effects=DATAFLOW),
        )(*self.arrays, after)
        self.sems, self.arrays = outs[0:2], list(outs[2:2 + m])
        return outs[-1]

    def wait(self, after):
        m = len(self.arrays)

        def body(*refs):
            arr, sems = refs[:m], refs[m:m + 2]
            for a in range(self.n):
                for j in range(3):
                    self._copy(arr, sems, a, j, False).wait_send()
                    self._copy(arr, sems, a, j, True).wait_recv()

        outs = pl.pallas_call(
            body, name=self.name + "_wait",
            out_shape=[pltpu.HBM(t.shape, t.dtype) for t in self.arrays],
            in_specs=[HBM] * m + [SEMS, SEMS, ANY], out_specs=[HBM] * m,
            input_output_aliases={i: i for i in range(m)},
            compiler_params=pltpu.CompilerParams(has_side_effects=DATAFLOW),
        )(*self.arrays, *self.sems, after)
        return list(outs[:self.n]), list(outs[self.n:])


class _GatherOverIci(_OverIci):
    def __init__(self, name, shards):
        super().__init__(name, shards, [lax.empty((N_CHIP,) + s.shape, s.dtype) for s in shards])

    @staticmethod
    def _half(ref):
        c = lax.axis_index("c")
        half = ref.shape[-2] // 2
        return pl.ds(c * half, half)

    def sent(self, src, land, a, chip):
        return src.at[self._half(src), :], land.at[_place()[3], self._half(src), :]

    def landed(self, land, a, chip):
        return land.at[2 * chip[0] + chip[1], self._half(land), :]


class _SumOverIci(_OverIci):
    def __init__(self, name, pre):
        super().__init__(name, pre, [lax.empty(p.shape, p.dtype) for p in pre])

    def sent(self, src, land, a, chip):
        return src.at[2 * chip[0] + chip[1]], land.at[_place()[3]]

    def landed(self, land, a, chip):
        return land.at[2 * chip[0] + chip[1]]


class _HalvesToSibling(_Exchange):
    def __init__(self, grads):
        n = self.n = len(grads)
        self.n_in = self.n_out = n
        self.out_shape = [jax.ShapeDtypeStruct((N_CHIP, g.shape[1] // 2, g.shape[2]), g.dtype) for g in grads]
        self.scratch = [pltpu.SemaphoreType.DMA((n,)), pltpu.SemaphoreType.DMA((n,))]

    def _copy(self, ins, outs, sems, a):
        x, y, c, me, chips = _place()
        half = ins[a].shape[1] // 2
        return _remote(ins[a].at[:, pl.ds((1 - c) * half, half), :], outs[a], sems[0].at[a], sems[1].at[a], (x, y, 1 - c))

    def start(self, ins, outs, sems):
        for a in range(self.n):
            self._copy(ins, outs, sems, a).start()

    def finish(self, ins, outs, sems):
        for a in range(self.n):
            self._copy(ins, outs, sems, a).wait_recv()
        for a in range(self.n):
            self._copy(ins, outs, sems, a).wait_send()


class _ShareHalves(_Exchange):
    def __init__(self, fulls):
        n = self.n = len(fulls)
        self.n_in = self.n_out = n
        self.out_shape = [jax.ShapeDtypeStruct(f.shape, f.dtype) for f in fulls]
        self.scratch = [pltpu.SemaphoreType.DMA((n,)), pltpu.SemaphoreType.DMA((n,))]
        self.aliases = {a: a for a in range(n)}

    def _copy(self, outs, sems, a, half_of):
        x, y, c, me, chips = _place()
        half = outs[a].shape[0] // 2
        rows = outs[a].at[pl.ds(half_of * half, half), :]
        return _remote(rows, rows, sems[0].at[a], sems[1].at[a], (x, y, 1 - c))

    def start(self, ins, outs, sems):
        c = _place()[2]
        for a in range(self.n):
            self._copy(outs, sems, a, c).start()

    def finish(self, ins, outs, sems):
        c = _place()[2]
        for a in range(self.n):
            self._copy(outs, sems, a, 1 - c).wait_recv()
        for a in range(self.n):
            self._copy(outs, sems, a, c).wait_send()


class _GatherBlocks(_Exchange):
    def __init__(self, block):
        self.n_in = self.n_out = 1
        self.out_shape = [jax.ShapeDtypeStruct((8,) + block.shape, block.dtype)]
        dma = pltpu.SemaphoreType.DMA
        self.scratch = [dma((7,)), dma((7,)), dma]

    @staticmethod
    def _peer(f):
        x, y, c, me, chips = _place()
        return ((1 - x) if f & 4 else x, (1 - y) if f & 2 else y, (1 - c) if f & 1 else c)

    def start(self, ins, outs, sems):
        x, y, c, me, chips = _place()
        for f in range(1, 8):
            _remote(ins[0], outs[0].at[2 * me + c], sems[0].at[f - 1], sems[1].at[f - 1], self._peer(f)).start()
        pltpu.make_async_copy(ins[0], outs[0].at[2 * me + c], sems[2]).start()

    def finish(self, ins, outs, sems):
        x, y, c, me, chips = _place()
        for f in range(1, 8):
            px, py, pc = self._peer(f)
            blk = outs[0].at[4 * px + 2 * py + pc]
            _remote(blk, blk, sems[0].at[f - 1], sems[1].at[f - 1], (x, y, c)).wait_recv()
        for f in range(1, 8):
            _remote(ins[0], outs[0].at[2 * me + c], sems[0].at[f - 1], sems[1].at[f - 1], self._peer(f)).wait_send()
        pltpu.make_async_copy(ins[0], outs[0].at[2 * me + c], sems[2]).wait()


class _Both(_Exchange):
    def __init__(self, first, second):
        self.parts = (first, second)
        self.n_in, self.n_out = first.n_in + second.n_in, first.n_out + second.n_out
        self.out_shape = first.out_shape + second.out_shape
        self.scratch = first.scratch + second.scratch
        self.aliases = dict(first.aliases)
        self.aliases.update({first.n_in + i: first.n_out + o for i, o in second.aliases.items()})

    def _split(self, ins, outs, sems):
        a, b = self.parts
        return ((a, ins[:a.n_in], outs[:a.n_out], sems[:len(a.scratch)]),
                (b, ins[a.n_in:], outs[a.n_out:], sems[len(a.scratch):]))

    def start(self, ins, outs, sems):
        for ex, i, o, s in self._split(ins, outs, sems):
            ex.start(i, o, s)

    def middle(self, ins, outs, sems):
        for ex, i, o, s in self._split(ins, outs, sems):
            ex.middle(i, o, s)

    def finish(self, ins, outs, sems):
        for ex, i, o, s in self._split(ins, outs, sems):
            ex.finish(i, o, s)


class _Bound:
    def __init__(self, ex, ins, outs, sems):
        self.start = lambda: ex.start(ins, outs, sems)
        self.middle = lambda: ex.middle(ins, outs, sems)
        self.finish = lambda: ex.finish(ins, outs, sems)


def _carry(name, body, ex, ex_args, args, in_specs, out_specs, out_shape, scratch_shapes=(), grid=None, semantics=(),
           after=None):
    n_a, n_o, n_s = len(args), len(out_shape), len(scratch_shapes)
    behind = [] if after is None else [after]

    def full_body(*refs):
        p = 0
        groups = []
        for size in (n_a, ex.n_in, len(behind), n_o, ex.n_out, n_s, len(ex.scratch)):
            groups.append(refs[p:p + size])
            p += size
        a, ei, _, o, eo, s, es = groups
        body(*a, *o, *s, _Bound(ex, ei, eo, es))

    kwargs = {} if grid is None else {"grid": grid}
    outs = pl.pallas_call(
        full_body, name=name,
        in_specs=list(in_specs) + [ANY] * (ex.n_in + len(behind)), out_specs=list(out_specs) + [ANY] * ex.n_out,
        out_shape=list(out_shape) + list(ex.out_shape), scratch_shapes=list(scratch_shapes) + list(ex.scratch),
        input_output_aliases={n_a + i: n_o + o for i, o in ex.aliases.items()},
        compiler_params=_params(*semantics) if semantics else pltpu.CompilerParams(vmem_limit_bytes=VMEM_LIMIT),
        **kwargs,
    )(*args, *ex_args, *behind)
    return outs[:n_o], outs[n_o:]


def _prepare_carrying(name, x, g1, pos, ifc, spread, arrays, ex, ex_args):
    n = len(arrays)
    r, cc = arrays[0].shape
    steps = 4
    tr, tm = r // steps, S // steps

    def body(x_ref, g_ref, pos_ref, ifc_ref, e_ref, *refs):
        src, h_ref, cos_ref, sin_ref, dst, xc = refs[:n], refs[n], refs[n + 1], refs[n + 2], refs[n + 3:2 * n + 3], refs[-1]

        @pl.when(pl.program_id(0) == 0)
        def _():
            xc.start()

        xv = x_ref[...]
        h_ref[...] = (xv * _rstd(xv) * g_ref[...]).astype(BF16)
        ang = pos_ref[...].astype(F32) * ifc_ref[...]
        cos_ref[...] = _spread_exact(jnp.cos(ang), e_ref[...])
        sin_ref[...] = _spread_exact(jnp.sin(ang), e_ref[...])
        for a in range(n):
            dst[a][...] = src[a][...].astype(BF16)

        @pl.when(pl.program_id(0) == steps - 1)
        def _():
            xc.middle()
            xc.finish()

    row = lambda w: pl.BlockSpec((tm, w), lambda i: (i, 0))
    const = lambda w: pl.BlockSpec((1, w), lambda i: (0, 0))
    blk = pl.BlockSpec((tr, cc), lambda i: (i, 0))
    return _carry(name, body, ex, ex_args, (x, g1, pos, ifc, spread, *arrays),
                  [row(D), const(D), row(1), const(128), pl.BlockSpec((128, 768), lambda i: (0, 0))] + [blk] * n,
                  [row(D), row(768), row(768)] + [blk] * n,
                  [jax.ShapeDtypeStruct((S, D), BF16)] + [jax.ShapeDtypeStruct((S, 768), F32)] * 2
                  + [jax.ShapeDtypeStruct((r, cc), BF16)] * n,
                  grid=(steps,), semantics=("arbitrary",))


def _exchange_alone(name, ex, ex_args):
    def body(xc):
        xc.start()
        xc.middle()
        xc.finish()

    return _carry(name, body, ex, ex_args, (), (), (), ())[1]


def _core_index():
    return lax.axis_index("c").astype(jnp.int32).reshape(1)


def _pair_sum(gs, gots):
    n = len(gs)
    _, r, cc = gs[0].shape
    half = r // 2

    def body(c_ref, *refs):
        for a in range(n):
            refs[2 * n + a][...] = (refs[a][...].astype(F32) + refs[n + a][...].astype(F32)).astype(BF16)

    mine = pl.BlockSpec((None, half, cc), lambda k, c_ref: (k, c_ref[0], 0))
    blk = pl.BlockSpec((None, half, cc), lambda k, c_ref: (k, 0, 0))
    return pl.pallas_call(
        body, name=f"pair_sum_{r}x{cc}",
        grid_spec=pltpu.PrefetchScalarGridSpec(
            num_scalar_prefetch=1, grid=(N_CHIP,), in_specs=[mine] * n + [blk] * n, out_specs=[blk] * n),
        out_shape=[jax.ShapeDtypeStruct((N_CHIP, half, cc), BF16)] * n,
        compiler_params=_params("parallel"),
    )(_core_index(), *gs, *gots)


def _chip_sum(pre, parts):
    n = len(parts)
    _, half, cc = parts[0].shape
    tr = half // 2
    me = 2 * lax.axis_index("x") + lax.axis_index("y")
    others = [k + (k >= me).astype(jnp.int32) for k in range(3)]
    where = jnp.stack([lax.axis_index("c"), me, *others]).astype(jnp.int32)

    def body(w_ref, *refs):
        for a in range(n):
            own, p1, p2, p3 = refs[4 * a:4 * a + 4]
            refs[4 * n + a][...] = ((own[...].astype(F32) + p1[...].astype(F32)) + p2[...].astype(F32)) + p3[...].astype(F32)

    slot = lambda s: pl.BlockSpec((None, tr, cc), lambda i, w_ref: (w_ref[s], i, 0))
    operands = []
    for a in range(n):
        operands += [pre[a], parts[a], parts[a], parts[a]]
    return pl.pallas_call(
        body, name=f"chip_sum_{half}x{cc}",
        grid_spec=pltpu.PrefetchScalarGridSpec(
            num_scalar_prefetch=1, grid=(2,),
            in_specs=[slot(1), slot(2), slot(3), slot(4)] * n,
            out_specs=[pl.BlockSpec((tr, cc), lambda i, w_ref: (2 * w_ref[0] + i, 0))] * n),
        out_shape=[jax.ShapeDtypeStruct((2 * half, cc), F32)] * n,
        compiler_params=_params("parallel"),
    )(where, *operands)


def _adamw_math(w, g, m, v):
    m = ADAM_B1 * m + (1.0 - ADAM_B1) * g
    v = ADAM_B2 * v + (1.0 - ADAM_B2) * (g * g)
    m_hat = m / (1.0 - ADAM_B1 ** ADAM_STEP)
    v_hat = v / (1.0 - ADAM_B2 ** ADAM_STEP)
    delta = -ADAM_LR * (m_hat / (jnp.sqrt(v_hat) + ADAM_EPS) + ADAM_WD * w)
    return delta, m, v


def _adamw(w, g, m, v, after=None):
    r, cc = w.shape
    tr = r // 4

    def body(w_ref, g_ref, m_ref, v_ref, go_ref, d_ref, nm_ref, nv_ref, _):
        g = g_ref[...]
        go_ref[...] = g
        d_ref[...], nm_ref[...], nv_ref[...] = _adamw_math(w_ref[...], g, m_ref[...], v_ref[...])

    blk = pl.BlockSpec((tr, cc), lambda i: (i, 0))
    return _carry(f"adamw_{r}x{cc}", body, _NoExchange(), (), (w, g, m, v), [blk] * 4, [blk] * 4,
                  [jax.ShapeDtypeStruct((r, cc), F32)] * 4, grid=(4,), semantics=("parallel",), after=after)[0]


def _pack8(rows):
    def body(*refs):
        out_ref = refs[-1]
        out_ref[...] = jnp.zeros_like(out_ref)
        for i, r in enumerate(refs[:-1]):
            out_ref[i:i + 1, :] = r[...]

    return pl.pallas_call(body, name="pack8", out_shape=jax.ShapeDtypeStruct((8, D), F32))(*rows)


def _adamw_gains(gall, ws, ms, vs):
    def body(ga_ref, *refs):
        w, m, v = refs[0:4], refs[4:8], refs[8:12]
        outs, loss_ref, total = refs[12:28], refs[28], refs[29]
        g = ga_ref[0]
        for dev in range(1, 8):
            g = g + ga_ref[dev]
        total[...] = g
        for i in range(4):
            gi = total[i:i + 1, :]
            outs[i][...] = gi
            outs[4 + i][...], outs[8 + i][...], outs[12 + i][...] = _adamw_math(w[i][...], gi, m[i][...], v[i][...])
        loss_ref[...] = total[4:5, 0:128] * (0.5 / D)

    outs = pl.pallas_call(
        body, name="adamw_gains",
        out_shape=[jax.ShapeDtypeStruct((1, D), F32)] * 16 + [jax.ShapeDtypeStruct((1, 128), F32)],
        scratch_shapes=[pltpu.VMEM((8, D), F32)],
    )(gall, *ws, *ms, *vs)
    return outs[0:4], outs[4:8], outs[8:12], outs[12:16], outs[16]


def kernel(x, positions, w_in, w_out, g_pre_mix, g_post_mix, g_pre_ffn, g_post_ffn, w_gate, w_up, w_down, loss_target, m_w_in, m_w_out, m_g_pre_mix, m_g_post_mix, m_g_pre_ffn, m_g_post_ffn, m_w_gate, m_w_up, m_w_down, v_w_in, v_w_out, v_g_pre_mix, v_g_post_mix, v_g_pre_ffn, v_g_post_ffn, v_w_gate, v_w_up, v_w_down):
    tr = lambda t: jnp.swapaxes(t, 1, 2)[0]
    shards = [w_in[0], w_out[0], tr(w_gate), tr(w_up), w_down[0]]
    moms = [m_w_in[0], m_w_out[0], tr(m_w_gate), tr(m_w_up), m_w_down[0]]
    vels = [v_w_in[0], v_w_out[0], tr(v_w_gate), tr(v_w_up), v_w_down[0]]
    xs, pos, tgt = x[0], positions.reshape(S, 1), loss_target[0]
    g1, g2, g3, g4 = g_pre_mix, g_post_mix, g_pre_ffn, g_post_ffn
    tabs = tuple(jnp.asarray(t) for t in _retention_tables())
    ifc, spread = _rotary_tables()
    ifc, spread = jnp.asarray(ifc), jnp.asarray(spread, dtype=BF16)
    bf = [s.astype(BF16) for s in shards[:2]]

    (h1, cos, sin, *ffn_bf), (win_g,) = _prepare_carrying(
        "gather_in", xs, g1, pos, ifc, spread, shards[2:], _GatherShards(bf[:1]), bf[:1])
    bf += list(ffn_bf)
    wout_gather = _GatherOverIci("wout_gather", bf[1:2])
    token = wout_gather.start(win_g)
    ffn_gather = _GatherOverIci("ffn_gather", bf[2:])
    token = ffn_gather.start(token)
    qr, kr, rv, rg, aq, ak, av = _proj_fwd(h1, win_g, cos, sin, token)
    wout_sh, wout_land = wout_gather.wait(qr)
    n_ffn = len(bf[2:])
    (att_out, lse, cat_a), (wout_g, *ffn_gather.arrays[n_ffn:]) = _att_fwd(
        aq, ak, av, _Both(_ForwardGathered(bf[1:2]), _ForwardGathered(bf[2:], forward=False)),
        [*wout_sh, *wout_land, *ffn_gather.arrays])
    wout_g = wout_g.reshape(D, D)
    (o_raw, cat_r, states), _ = _ret_fwd(qr, kr, rv, rg, tabs, _NoExchange(), (), cat_a)
    ffn_sh, ffn_lands = ffn_gather.wait(cat_r)
    (mix, x2, h3), (wg_g, wu_g, wd_g) = _mix_fwd(cat_r, cat_a, wout_g, xs, g2, g3,
                                                _ForwardGathered(bf[2:], own=False), [*ffn_sh, *ffn_lands])
    gt, up, a, sq, dy, df, dg4 = _ffn_fwd(h3, wg_g, wu_g, wd_g, x2, tgt, g4)

    dgt, dup, dx2, dmix, dg3, dg2 = _ffn_bwd_act(df, gt, up, wg_g, wu_g, wd_g, dy, x2, mix, g2, g3)
    ffn_grads = list(_ffn_bwd_w(a, df, h3, dgt, dup))
    (dret, datt, dwout), got = _mix_bwd(dmix, cat_r, cat_a, wout_g, _HalvesToSibling(ffn_grads), ffn_grads)
    ffn_sum = _SumOverIci("ffn_sum", _pair_sum(ffn_grads, got))
    token = ffn_sum.start(datt)
    (dq_att, dk_att, dv_att), _ = _att_bwd(aq, ak, av, datt, att_out, lse, _NoExchange(), (), token)
    (dqr, dkr, drv, drg), _ = _ret_bwd(qr, kr, rv, rg, o_raw, states, dret, tabs, _NoExchange(), (), token)
    dproj = _rot_bwd(cos, sin, dqr, dkr, drv, drg, dq_att, dk_att, dv_att)
    sums = _chip_sum(*ffn_sum.wait(dproj))
    dwin, ffn_full = _win_bwd_w(h1, dproj, _ShareHalves(sums), sums)
    in_grads = [dwin, dwout.reshape(N_CHIP, WOUT_R, D)]

    got = _exchange_alone("halves_to_sibling", _HalvesToSibling(in_grads), in_grads)
    in_sum = _SumOverIci("in_sum", [*_pair_sum(in_grads[:1], got[:1]), *_pair_sum(in_grads[1:], got[1:])])
    token = in_sum.start(dproj)
    dx, dg1 = _in_bwd(dproj, win_g, xs, dx2, g1, token)
    ffn_upd = [_adamw(shards[2 + i], ffn_full[o], moms[2 + i], vels[2 + i], token)
               for i, o in enumerate((1, 2, 0))]
    pre, parts = in_sum.wait(ffn_upd[2][0])
    sums = [*_chip_sum(pre[:1], parts[:1]), *_chip_sum(pre[1:], parts[1:])]
    gblock = _pack8([dg1, dg2, dg3, dg4, sq])
    *in_full, gall = _exchange_alone("share_rest", _Both(_ShareHalves(sums), _GatherBlocks(gblock)), [*sums, gblock])
    upd = [_adamw(w, g, m, v) for w, g, m, v in zip(shards[:2], in_full, moms[:2], vels[:2])] + ffn_upd
    gg, gd, gm, gv, loss_row = _adamw_gains(gall, [g1, g2, g3, g4],
                                            [m_g_pre_mix, m_g_post_mix, m_g_pre_ffn, m_g_post_ffn],
                                            [v_g_pre_mix, v_g_post_mix, v_g_pre_ffn, v_g_post_ffn])

    def order(mats, vecs):
        back = lambda t: jnp.swapaxes(t[None], 1, 2)
        return [mats[0][None], mats[1][None], *vecs, back(mats[2]), back(mats[3]), mats[4][None]]

    return (loss_row[0, 0], dx[None],
            *order([u[0] for u in upd], gg),
            *order([u[1] for u in upd], gd),
            *order([u[2] for u in upd], gm),
            *order([u[3] for u in upd], gv))
```

```python
import numpy as np
import jax
import jax.numpy as jnp
from jax import lax
from jax.experimental import pallas as pl
from jax.experimental.pallas import tpu as pltpu

F32, BF16 = jnp.float32, jnp.bfloat16
MESH = pl.DeviceIdType.MESH

S = 2048
D = 1024
PW = 3072
N_CHIP = 4
WIN_C = PW // N_CHIP
DFF = 2816
FF_C = DFF // N_CHIP
WOUT_R = D // N_CHIP
RMS_EPS = 1e-6
GN_EPS = 1e-5
RET_C = 128
RET_PER_STEP = 4
RET_SCALE = 32 ** -0.5
ATT_BLK = 128
ATT_SCALE = 64 ** -0.5
PATTERN_DILATIONS = (1, 4, 16)
NEG = -1e30
VMEM_LIMIT = 56 * 1024 * 1024

ADAM_LR, ADAM_B1, ADAM_B2, ADAM_EPS, ADAM_WD, ADAM_STEP = 0.001, 0.9, 0.999, 1e-08, 0.01, 10


def _params(*sem):
    return pltpu.CompilerParams(dimension_semantics=sem, vmem_limit_bytes=VMEM_LIMIT)


def _nt(a, b):
    return lax.dot_general(a, b, (((1,), (1,)), ((), ())), preferred_element_type=F32)


def _tn(a, b):
    return lax.dot_general(a, b, (((0,), (0,)), ((), ())), preferred_element_type=F32)


def _nn(a, b):
    return jnp.dot(a, b, preferred_element_type=F32)


def _rstd(v):
    return lax.rsqrt(jnp.mean(v * v, axis=-1, keepdims=True) + RMS_EPS)


def _sigmoid(v):
    return 1.0 / (1.0 + jnp.exp(-v))


def _rows(i, t):
    return pl.ds(pl.multiple_of(i * t, t), t)


def _retention_tables():
    h = np.arange(8, dtype=np.float32)
    log_g = np.log1p(-np.exp2(-5.0 - h)).astype(np.float32)
    idx = np.arange(RET_C, dtype=np.float32)
    diff = idx[:, None] - idx[None, :]
    dtab = np.where(diff >= 0, np.exp(log_g[:, None, None] * np.maximum(diff, 0.0)), 0.0).astype(np.float32)
    dtab = dtab.reshape(8 * RET_C, RET_C)
    lane_head = np.arange(256) // 32
    a_tab = np.exp(log_g[lane_head][None, :] * (idx + 1.0)[:, None]).astype(np.float32)
    b_tab = np.exp(log_g[lane_head][None, :] * (RET_C - 1.0 - idx)[:, None]).astype(np.float32)
    lam = np.exp(log_g[lane_head] * RET_C).astype(np.float32)[:, None]
    bd = (lane_head[:, None] == (np.arange(512) // 64)[None, :]).astype(np.float32)
    return dtab, a_tab, b_tab, lam, bd


def _rotary_tables():
    inv_r = (1.0 / (np.float32(10000.0) ** np.linspace(0.0, 1.0, 16, dtype=np.float32))).astype(np.float32)
    inv_a = (np.float32(500000.0) ** (-np.arange(0, 16, 2, dtype=np.float32) / np.float32(16))).astype(np.float32)
    ifc = np.zeros((1, 128), np.float32)
    ifc[0, 0:16], ifc[0, 16:24] = inv_r, inv_a
    spread = np.zeros((128, 768), np.float32)
    for lane in range(256):
        spread[(lane % 32) % 16, lane] = 1.0
    for lane in range(512):
        d = lane % 64
        spread[16 + d % 8 if d < 16 else 24, 256 + lane] = 1.0
    return ifc, spread


def _rot_halves(tm):
    lo_r = (lax.broadcasted_iota(jnp.int32, (tm, 256), 1) % 32) < 16
    lo_a = (lax.broadcasted_iota(jnp.int32, (tm, 512), 1) % 64) < 8
    return lo_r, lo_a


def _spread_exact(t, e):
    hi = t.astype(BF16)
    r1 = t - hi.astype(F32)
    mid = r1.astype(BF16)
    lo = (r1 - mid.astype(F32)).astype(BF16)
    return _nn(hi, e) + _nn(mid, e) + _nn(lo, e)


def _rot_tables(cos_ref, sin_ref, e_ref):
    cs = _spread_exact(cos_ref[...], e_ref[...])
    sn = _spread_exact(sin_ref[...], e_ref[...])
    return cs[:, 0:256], cs[:, 256:768], sn[:, 0:256], sn[:, 256:768]


def _proj_fwd(h1, win_g, cos, sin, spread, after):
    tm = 256

    def body(h_ref, w_ref, cos_ref, sin_ref, e_ref, qr_ref, kr_ref, rv_ref, rg_ref, aq_ref, ak_ref, av_ref, p_ref, _):
        h = h_ref[...]
        for k in range(N_CHIP):
            p_ref[:, k * WIN_C:(k + 1) * WIN_C] = _nn(h, w_ref[k])
        cr, ca, sr, sa = _rot_tables(cos_ref, sin_ref, e_ref)
        lo_r, lo_a = _rot_halves(tm)

        def rot_r(v):
            return v * cr + sr * jnp.where(lo_r, -pltpu.roll(v, 240, 1), pltpu.roll(v, 16, 1))

        def rot_a(v):
            return v * ca + sa * jnp.where(lo_a, -pltpu.roll(v, 504, 1), pltpu.roll(v, 8, 1))

        qr_ref[...] = rot_r(p_ref[:, 0:256]).astype(BF16)
        kr_ref[...] = (rot_r(p_ref[:, 256:512]) * RET_SCALE).astype(BF16)
        rv_ref[...] = p_ref[:, 512:1024].astype(BF16)
        rg_ref[...] = p_ref[:, 1024:1536]
        aq, ak = rot_a(p_ref[:, 1536:2048]), rot_a(p_ref[:, 2048:2560])
        for j in range(4):
            aq_ref[j] = aq[:, 128 * j:128 * j + 128]
            ak_ref[j] = ak[:, 128 * j:128 * j + 128]
            av_ref[j] = p_ref[:, 2560 + 128 * j:2560 + 128 * j + 128]

    row = lambda w: pl.BlockSpec((tm, w), lambda i: (i, 0))
    slab = pl.BlockSpec((4, tm, 128), lambda i: (0, i, 0))
    return _carry(
        "proj_fwd", body, _NoExchange(), (), (h1, win_g, cos, sin, spread),
        [row(D), pl.BlockSpec((N_CHIP, D, WIN_C), lambda i: (0, 0, 0)), row(128), row(128),
         pl.BlockSpec((128, 768), lambda i: (0, 0))],
        [row(256), row(256), row(512), row(512), slab, slab, slab],
        [jax.ShapeDtypeStruct((S, w), BF16) for w in (256, 256, 512)]
        + [jax.ShapeDtypeStruct((S, 512), F32)] + [jax.ShapeDtypeStruct((4, S, 128), F32)] * 3,
        scratch_shapes=[pltpu.VMEM((tm, PW), F32)], grid=(S // tm,), semantics=("parallel",), after=after)[0]


def _seg_mean(v):
    lo = lax.broadcasted_iota(jnp.int32, v.shape, 1) < 64
    s_lo = jnp.sum(jnp.where(lo, v, 0.0), axis=-1, keepdims=True)
    s_hi = jnp.sum(jnp.where(lo, 0.0, v), axis=-1, keepdims=True)
    return jnp.where(lo, s_lo, s_hi) * (1.0 / 64.0)


def _ret_fwd(qr, kr, rv, proj, tabs, exchange, exchange_args, after=None):
    C, G = RET_C, RET_PER_STEP
    steps = S // (C * G)
    dtab, a_tab, b_tab, lam, bd = tabs

    def body(q_ref, k_ref, v_ref, g_ref, dt_ref, a_ref, b_ref, lam_ref, bd_ref, o_ref, cat_ref, st_ref, R, exch):
        @pl.when(pl.program_id(0) == 0)
        def _():
            exch.start()
            R[...] = jnp.zeros_like(R)

        lane_head = lax.broadcasted_iota(jnp.int32, (C, 256), 1) // 32
        col_head = lax.broadcasted_iota(jnp.int32, (C, 256), 1) // 64
        for s in range(G):
            rows = slice(s * C, (s + 1) * C)
            q, k, v = q_ref[rows, :], k_ref[rows, :], v_ref[rows, :]
            rb = R[...].astype(BF16)
            st_ref[s] = rb
            qa = (q.astype(F32) * a_ref[...]).astype(BF16)
            cross = _nn(qa, rb)
            p = (_nt(_stack_heads(q, lane_head, n=8), k) * dt_ref[...]).astype(BF16)
            og = [cross[:, 256 * g:256 * g + 256]
                  + _unstack_heads(_nn(p[4 * C * g:4 * C * (g + 1)], v[:, 256 * g:256 * g + 256]), col_head)
                  for g in range(2)]
            kb = (k.astype(F32) * b_ref[...]).astype(BF16)
            R[...] = R[...] * lam_ref[...] + _tn(kb, v) * bd_ref[...]
            o_ref[rows, 0:256] = og[0]
            o_ref[rows, 256:512] = og[1]
            for j in range(4):
                oj = og[j // 2][:, 128 * (j % 2):128 * (j % 2) + 128]
                xc = oj - _seg_mean(oj)
                rn = xc * lax.rsqrt(_seg_mean(xc * xc) + GN_EPS)
                gj = g_ref[rows, 128 * j:128 * j + 128]
                cat_ref[rows, 128 * j:128 * j + 128] = (rn * (gj * _sigmoid(gj))).astype(BF16)

        @pl.when(pl.program_id(0) == steps - 1)
        def _():
            exch.middle()
            exch.finish()

    row = lambda w: pl.BlockSpec((C * G, w), lambda n: (n, 0))
    full = lambda a: pl.BlockSpec(a.shape, lambda n: (0,) * a.ndim)
    return _carry(
        "ret_fwd", body, exchange, exchange_args, (qr, kr, rv, proj, dtab, a_tab, b_tab, lam, bd),
        [row(256), row(256), row(512), row(512),
         full(dtab), full(a_tab), full(b_tab), full(lam), full(bd)],
        [row(512), row(512), pl.BlockSpec((G, 256, 512), lambda n: (n, 0, 0))],
        [jax.ShapeDtypeStruct((S, 512), F32), jax.ShapeDtypeStruct((S, 512), BF16),
         jax.ShapeDtypeStruct((S // C, 256, 512), BF16)],
        scratch_shapes=[pltpu.VMEM((256, 512), F32)], grid=(steps,), semantics=("arbitrary",), after=after)


def _stack_heads(v, lane_head, fill=0.0, n=4):
    return jnp.concatenate([jnp.where(lane_head == h, v, jnp.full_like(v, fill)) for h in range(n)], axis=0)


def _unstack_heads(v, lane_head, n=4):
    out = v[0:ATT_BLK]
    for h in range(1, n):
        out = jnp.where(lane_head == h, v[h * ATT_BLK:(h + 1) * ATT_BLK], out)
    return out


def _att_bias(has_prev):
    nk = 2 * ATT_BLK if has_prev else ATT_BLK
    a = lax.broadcasted_iota(jnp.int32, (4 * ATT_BLK, nk), 0) % ATT_BLK
    kk = lax.broadcasted_iota(jnp.int32, (4 * ATT_BLK, nk), 1)
    if not has_prev:
        return None, jnp.where((a - kk) >= 0, 0.0, NEG)
    dist = ATT_BLK + a - kk
    inside = (dist >= 0) & (dist <= ATT_BLK)
    return jnp.where(inside, 0.0, NEG), jnp.where(inside & (kk >= ATT_BLK), 0.0, NEG)


def _class_rows(ib, r, d):
    if d == 1:
        return pl.ds(pl.multiple_of(ib * ATT_BLK, ATT_BLK), ATT_BLK)
    return pl.ds(ib * ATT_BLK * d + r, ATT_BLK, stride=d)


def _slab_pair(ref, g, rows):
    return jnp.concatenate([ref[2 * g, rows, :], ref[2 * g + 1, rows, :]], axis=1)


def _att_blocks(d):
    nb = S // d // ATT_BLK
    return nb, nb > 1


def _att_fwd(aq, ak, av, exchange, exchange_args):
    def body(q_ref, k_ref, v_ref, o_ref, l_ref, cat_ref, xc):
        xc.start()
        lane_head = lax.broadcasted_iota(jnp.int32, (ATT_BLK, 256), 1) // 64
        for pi, d in enumerate(PATTERN_DILATIONS):
            if pi == len(PATTERN_DILATIONS) - 1:
                xc.middle()
            nb, has_prev = _att_blocks(d)
            bias_rest, bias_first = _att_bias(has_prev)

            def block(b, carry, pi=pi, d=d, nb=nb, has_prev=has_prev, bias_rest=bias_rest, bias_first=bias_first):
                r, ib = b // nb, b % nb
                rows = _class_rows(ib, r, d)
                prow = _class_rows(jnp.maximum(ib - 1, 0), r, d)
                bias = jnp.where(ib == 0, bias_first, bias_rest) if has_prev else bias_first
                for g in range(2):
                    qg = _slab_pair(q_ref, g, rows).astype(BF16)
                    kg = _slab_pair(k_ref, g, rows)
                    vg = _slab_pair(v_ref, g, rows)
                    if has_prev:
                        kg = jnp.concatenate([_slab_pair(k_ref, g, prow), kg], axis=0)
                        vg = jnp.concatenate([_slab_pair(v_ref, g, prow), vg], axis=0)
                    kg, vg = kg.astype(BF16), vg.astype(BF16)
                    s = _nt(_stack_heads(qg, lane_head), kg) * ATT_SCALE + bias
                    m = jnp.max(s, axis=-1, keepdims=True)
                    p = jnp.exp(s - m)
                    den = jnp.sum(p, axis=-1, keepdims=True)
                    og = _unstack_heads(_nn(p.astype(BF16), vg) / den, lane_head)
                    lg = _unstack_heads(jnp.broadcast_to(m + jnp.log(den), (4 * ATT_BLK, 256)), lane_head)
                    for jj in range(2):
                        j = 2 * g + jj
                        o_new, l_new = og[:, 128 * jj:128 * jj + 128], lg[:, 128 * jj:128 * jj + 128]
                        if pi > 0:
                            o_old, l_old = o_ref[j, rows, :], l_ref[j, rows, :]
                            mx = jnp.maximum(l_old, l_new)
                            ea, eb = jnp.exp(l_old - mx), jnp.exp(l_new - mx)
                            den = ea + eb
                            o_new = (ea * o_old + eb * o_new) / den
                            l_new = mx + jnp.log(den)
                        o_ref[j, rows, :] = o_new
                        l_ref[j, rows, :] = l_new
                return carry

            lax.fori_loop(0, S // ATT_BLK, block, 0, unroll=4)

        def to_cat(i, carry):
            rows = _rows(i, 256)
            for j in range(4):
                cat_ref[rows, 128 * j:128 * j + 128] = o_ref[j, rows, :].astype(BF16)
            return carry

        lax.fori_loop(0, S // 256, to_cat, 0)
        xc.finish()

    slab = jax.ShapeDtypeStruct((4, S, 128), F32)
    return _carry("att_fwd", body, exchange, exchange_args, (aq, ak, av), [VMEM] * 3, [VMEM] * 3,
                  [slab, slab, jax.ShapeDtypeStruct((S, 512), BF16)])


def _mix_fwd(cat_r, cat_a, wout, x, g2, g3, exchange, exchange_args):
    tm = 512

    def body(cr_ref, ca_ref, w_ref, x_ref, g2_ref, g3_ref, mix_ref, x2_ref, h3_ref, xc):
        @pl.when(pl.program_id(0) == 0)
        def _():
            xc.start()

        mix = _nn(cr_ref[...], w_ref[0:512, :]) + _nn(ca_ref[...], w_ref[512:1024, :])
        mix_ref[...] = mix
        x2 = x_ref[...] + mix * _rstd(mix) * g2_ref[...]
        x2_ref[...] = x2
        h3_ref[...] = (x2 * _rstd(x2) * g3_ref[...]).astype(BF16)

        @pl.when(pl.program_id(0) == S // tm - 1)
        def _():
            xc.middle()
            xc.finish()

    row = lambda w: pl.BlockSpec((tm, w), lambda i: (i, 0))
    vec = pl.BlockSpec((1, D), lambda i: (0, 0))
    return _carry("mix_fwd", body, exchange, exchange_args, (cat_r, cat_a, wout, x, g2, g3),
                  [row(512), row(512), pl.BlockSpec((D, D), lambda i: (0, 0)), row(D), vec, vec],
                  [row(D), row(D), row(D)],
                  [jax.ShapeDtypeStruct((S, D), F32), jax.ShapeDtypeStruct((S, D), F32),
                   jax.ShapeDtypeStruct((S, D), BF16)],
                  grid=(S // tm,), semantics=("arbitrary",))


def _ffn_fwd(h3, wg, wu, wd, x2, tgt, g4):
    tm = 512
    last = N_CHIP - 1

    def body(h_ref, wg_ref, wu_ref, wd_ref, x2_ref, t_ref, g_ref,
             gt_ref, up_ref, a_ref, loss_ref, dy_ref, df_ref, dg_ref, f_ref):
        k, i = pl.program_id(0), pl.program_id(1)
        h = h_ref[...]
        gt = _nt(h, wg_ref[...])
        up = _nt(h, wu_ref[...])
        gt_ref[...] = gt.astype(BF16)
        up_ref[...] = up.astype(BF16)
        a = (gt * _sigmoid(gt) * up).astype(BF16)
        a_ref[...] = a
        part = _nn(a, wd_ref[...])
        rows = _rows(i, tm)

        @pl.when(k == 0)
        def _():
            f_ref[rows, :] = part

        @pl.when((k > 0) & (k < last))
        def _():
            f_ref[rows, :] = f_ref[rows, :] + part

        @pl.when((k == last) & (i == 0))
        def _():
            loss_ref[...] = jnp.zeros_like(loss_ref)
            dg_ref[...] = jnp.zeros_like(dg_ref)

        @pl.when(k == last)
        def _():
            fv = f_ref[rows, :] + part
            r = _rstd(fv)
            fn = fv * r
            e = x2_ref[...] + fn * g_ref[...] - t_ref[...]
            loss_ref[...] = loss_ref[...] + jnp.sum(jnp.sum(e * e, axis=-1, keepdims=True), axis=0, keepdims=True)
            dy = e * (1.0 / D)
            dy_ref[...] = dy
            dg_ref[...] = dg_ref[...] + jnp.sum(dy * fn, axis=0, keepdims=True)
            t = dy * g_ref[...]
            df_ref[...] = (r * (t - fn * jnp.mean(t * fn, axis=-1, keepdims=True))).astype(BF16)

    wrow = pl.BlockSpec((None, FF_C, D), lambda k, i: (k, 0, 0))
    act = pl.BlockSpec((None, tm, FF_C), lambda k, i: (k, i, 0))
    late = pl.BlockSpec((tm, D), lambda k, i: (jnp.where(k == last, i, 0), 0))
    vec = pl.BlockSpec((1, D), lambda k, i: (0, 0))
    return pl.pallas_call(
        body, grid=(N_CHIP, S // tm), name="ffn_fwd",
        in_specs=[pl.BlockSpec((tm, D), lambda k, i: (i, 0)), wrow, wrow, wrow, late, late, vec],
        out_specs=[act, act, act, vec, late, late, vec],
        out_shape=[jax.ShapeDtypeStruct((N_CHIP, S, FF_C), BF16)] * 3
                  + [jax.ShapeDtypeStruct((1, D), F32), jax.ShapeDtypeStruct((S, D), F32),
                     jax.ShapeDtypeStruct((S, D), BF16), jax.ShapeDtypeStruct((1, D), F32)],
        scratch_shapes=[pltpu.VMEM((S, D), F32)],
        compiler_params=_params("arbitrary", "arbitrary"),
    )(h3, wg, wu, wd, x2, tgt, g4)


def _ffn_bwd_act(df, gt, up, wg, wu, wd, dy, x2, mix, g2, g3):
    tm, sub = 512, 256
    last = N_CHIP - 1

    def body(df_ref, gt_ref, up_ref, wg_ref, wu_ref, wd_ref, dy_ref, x2_ref, mix_ref, g2_ref, g3_ref,
             dgt_ref, dup_ref, dx2_ref, dmix_ref, dg3_ref, dg2_ref, dh_ref):
        k, i = pl.program_id(0), pl.program_id(1)
        parts = []
        for s in range(tm // sub):
            rows = slice(s * sub, (s + 1) * sub)
            da = _nt(df_ref[rows, :], wd_ref[...])
            gt, up = gt_ref[rows, :].astype(F32), up_ref[rows, :].astype(F32)
            sg = _sigmoid(gt)
            dup = (da * gt * sg).astype(BF16)
            dgt = (da * up * (sg * (1.0 + gt * (1.0 - sg)))).astype(BF16)
            dup_ref[rows, :] = dup
            dgt_ref[rows, :] = dgt
            parts.append(_nn(dgt, wg_ref[...]) + _nn(dup, wu_ref[...]))
        part = jnp.concatenate(parts, axis=0)
        rows = _rows(i, tm)

        @pl.when(k == 0)
        def _():
            dh_ref[rows, :] = part

        @pl.when((k > 0) & (k < last))
        def _():
            dh_ref[rows, :] = dh_ref[rows, :] + part

        @pl.when((k == last) & (i == 0))
        def _():
            dg3_ref[...] = jnp.zeros_like(dg3_ref)
            dg2_ref[...] = jnp.zeros_like(dg2_ref)

        @pl.when(k == last)
        def _():
            dh = dh_ref[rows, :] + part
            x2 = x2_ref[...]
            r3 = _rstd(x2)
            xn = x2 * r3
            dg3_ref[...] = dg3_ref[...] + jnp.sum(dh * xn, axis=0, keepdims=True)
            t = dh * g3_ref[...]
            dx2 = dy_ref[...] + r3 * (t - xn * jnp.mean(t * xn, axis=-1, keepdims=True))
            dx2_ref[...] = dx2
            mix = mix_ref[...]
            r2 = _rstd(mix)
            mn = mix * r2
            dg2_ref[...] = dg2_ref[...] + jnp.sum(dx2 * mn, axis=0, keepdims=True)
            u = dx2 * g2_ref[...]
            dmix_ref[...] = (r2 * (u - mn * jnp.mean(u * mn, axis=-1, keepdims=True))).astype(BF16)

    wrow = pl.BlockSpec((None, FF_C, D), lambda k, i: (k, 0, 0))
    act = pl.BlockSpec((None, tm, FF_C), lambda k, i: (k, i, 0))
    row = pl.BlockSpec((tm, D), lambda k, i: (i, 0))
    late = pl.BlockSpec((tm, D), lambda k, i: (jnp.where(k == last, i, 0), 0))
    vec = pl.BlockSpec((1, D), lambda k, i: (0, 0))
    return pl.pallas_call(
        body, grid=(N_CHIP, S // tm), name="ffn_bwd_act",
        in_specs=[row, act, act, wrow, wrow, wrow, late, late, late, vec, vec],
        out_specs=[act, act, late, late, vec, vec],
        out_shape=[jax.ShapeDtypeStruct((N_CHIP, S, FF_C), BF16), jax.ShapeDtypeStruct((N_CHIP, S, FF_C), BF16),
                   jax.ShapeDtypeStruct((S, D), F32), jax.ShapeDtypeStruct((S, D), BF16),
                   jax.ShapeDtypeStruct((1, D), F32), jax.ShapeDtypeStruct((1, D), F32)],
        scratch_shapes=[pltpu.VMEM((S, D), F32)],
        compiler_params=_params("arbitrary", "arbitrary"),
    )(df, gt, up, wg, wu, wd, dy, x2, mix, g2, g3)


def _ffn_bwd_w(a, df, h3, dgt, dup):
    tm = 1024
    assert S // tm == 2

    def body(a_ref, df_ref, h_ref, dgt_ref, dup_ref, dwd_ref, dwg_ref, dwu_ref, acc_d, acc_g, acc_u):
        i = pl.program_id(1)
        h = h_ref[...]
        parts = (_tn(a_ref[...], df_ref[...]), _tn(dgt_ref[...], h), _tn(dup_ref[...], h))

        @pl.when(i == 0)
        def _():
            for acc, part in zip((acc_d, acc_g, acc_u), parts):
                acc[...] = part

        @pl.when(i == S // tm - 1)
        def _():
            for out, acc, part in zip((dwd_ref, dwg_ref, dwu_ref), (acc_d, acc_g, acc_u), parts):
                out[...] = (acc[...] + part).astype(BF16)

    act = pl.BlockSpec((None, tm, FF_C), lambda k, i: (k, i, 0))
    row = pl.BlockSpec((tm, D), lambda k, i: (i, 0))
    wrow = pl.BlockSpec((None, FF_C, D), lambda k, i: (k, 0, 0))
    return pl.pallas_call(
        body, grid=(N_CHIP, S // tm), name="ffn_bwd_w",
        in_specs=[act, row, row, act, act],
        out_specs=[wrow, wrow, wrow],
        out_shape=[jax.ShapeDtypeStruct((N_CHIP, FF_C, D), BF16)] * 3,
        scratch_shapes=[pltpu.VMEM((FF_C, D), F32)] * 3,
        compiler_params=_params("parallel", "arbitrary"),
    )(a, df, h3, dgt, dup)


def _mix_bwd(dmix, cat_r, cat_a, wout, exchange, exchange_args):
    tm = 1024

    def body(dm_ref, cr_ref, ca_ref, w_ref, dret_ref, datt_ref, dw_ref, acc, xc):
        i = pl.program_id(0)

        @pl.when(i == 0)
        def _():
            xc.start()
            acc[...] = jnp.zeros_like(acc)

        dm = dm_ref[...]
        dret_ref[...] = _nt(dm, w_ref[0:512, :])
        datt = _nt(dm, w_ref[512:1024, :])
        for j in range(4):
            datt_ref[j] = datt[:, 128 * j:128 * j + 128]
        acc[0:512, :] += _tn(cr_ref[...], dm)
        acc[512:1024, :] += _tn(ca_ref[...], dm)

        @pl.when(i == S // tm - 1)
        def _():
            dw_ref[...] = acc[...].astype(BF16)
            xc.middle()
            xc.finish()

    row = lambda w: pl.BlockSpec((tm, w), lambda i: (i, 0))
    full = pl.BlockSpec((D, D), lambda i: (0, 0))
    return _carry("mix_bwd", body, exchange, exchange_args, (dmix, cat_r, cat_a, wout),
                  [row(D), row(512), row(512), full],
                  [row(512), pl.BlockSpec((4, tm, 128), lambda i: (0, i, 0)), full],
                  [jax.ShapeDtypeStruct((S, 512), F32), jax.ShapeDtypeStruct((4, S, 128), F32),
                   jax.ShapeDtypeStruct((D, D), BF16)],
                  scratch_shapes=[pltpu.VMEM((D, D), F32)], grid=(S // tm,), semantics=("arbitrary",))


def _att_bwd(aq, ak, av, datt, att_out, lse, exchange, exchange_args, after=None):
    def body(q_ref, k_ref, v_ref, do_ref, out_ref, l_ref, dq_ref, dk_ref, dv_ref, xc):
        xc.start()

        def clear(i, carry):
            rows = _rows(i, 256)
            for ref in (dq_ref, dk_ref, dv_ref):
                for j in range(4):
                    ref[j, rows, :] = jnp.zeros((256, 128), F32)
            return carry

        lax.fori_loop(0, S // 256, clear, 0)
        lane_head = lax.broadcasted_iota(jnp.int32, (ATT_BLK, 256), 1) // 64
        for d in PATTERN_DILATIONS:
            nb, has_prev = _att_blocks(d)
            bias_rest, bias_first = _att_bias(has_prev)

            def block(b, carry, d=d, nb=nb, has_prev=has_prev, bias_rest=bias_rest, bias_first=bias_first):
                r, ib = b // nb, b % nb
                rows = _class_rows(ib, r, d)
                prow = _class_rows(jnp.maximum(ib - 1, 0), r, d)
                bias = jnp.where(ib == 0, bias_first, bias_rest) if has_prev else bias_first
                for g in range(2):
                    qg = _slab_pair(q_ref, g, rows).astype(BF16)
                    kg = _slab_pair(k_ref, g, rows)
                    vg = _slab_pair(v_ref, g, rows)
                    if has_prev:
                        kg = jnp.concatenate([_slab_pair(k_ref, g, prow), kg], axis=0)
                        vg = jnp.concatenate([_slab_pair(v_ref, g, prow), vg], axis=0)
                    kg, vg = kg.astype(BF16), vg.astype(BF16)
                    dog = _slab_pair(do_ref, g, rows)
                    outg = _slab_pair(out_ref, g, rows)
                    lg = _slab_pair(l_ref, g, rows)
                    qs = _stack_heads(qg, lane_head)
                    dos = _stack_heads(dog, lane_head)
                    delta = jnp.sum(dos * jnp.concatenate([outg] * 4, axis=0), axis=-1, keepdims=True)
                    lh = jnp.max(_stack_heads(lg, lane_head, NEG), axis=-1, keepdims=True)
                    s = _nt(qs, kg) * ATT_SCALE + bias
                    p = jnp.exp(s - lh)
                    dosb = dos.astype(BF16)
                    ds = (p * (_nt(dosb, vg) - delta) * ATT_SCALE).astype(BF16)
                    dq = _unstack_heads(_nn(ds, kg), lane_head)
                    dk = _tn(ds, qs)
                    dv = _tn(p.astype(BF16), dosb)
                    for jj in range(2):
                        j, sl = 2 * g + jj, slice(128 * jj, 128 * jj + 128)
                        dq_ref[j, rows, :] += dq[:, sl]
                        if has_prev:
                            dk_ref[j, prow, :] += dk[0:ATT_BLK, sl]
                            dv_ref[j, prow, :] += dv[0:ATT_BLK, sl]
                            dk_ref[j, rows, :] += dk[ATT_BLK:2 * ATT_BLK, sl]
                            dv_ref[j, rows, :] += dv[ATT_BLK:2 * ATT_BLK, sl]
                        else:
                            dk_ref[j, rows, :] += dk[:, sl]
                            dv_ref[j, rows, :] += dv[:, sl]
                return carry

            lax.fori_loop(0, S // ATT_BLK, block, 0, unroll=4)
        xc.middle()
        xc.finish()

    slab = jax.ShapeDtypeStruct((4, S, 128), F32)
    return _carry("att_bwd", body, exchange, exchange_args, (aq, ak, av, datt, att_out, lse), [VMEM] * 6, [VMEM] * 3,
                  [slab, slab, slab], after=after)


def _ret_bwd(qr, kr, rv, proj, o_raw, states, dret, tabs, exchange, exchange_args, after=None):
    C, G = RET_C, RET_PER_STEP
    steps = S // (C * G)
    dtab, a_tab, b_tab, lam, bd = tabs

    def body(q_ref, k_ref, v_ref, g_ref, o_ref, st_ref, dr_ref, dt_ref, a_ref, b_ref, lam_ref, bd_ref,
             dq_ref, dk_ref, dv_ref, dg_ref, dR, exch):
        @pl.when(pl.program_id(0) == 0)
        def _():
            exch.start()
            dR[...] = jnp.zeros_like(dR)

        lane_head = lax.broadcasted_iota(jnp.int32, (C, 256), 1) // 32
        col_head = lax.broadcasted_iota(jnp.int32, (C, 256), 1) // 64
        for s in reversed(range(G)):
            rows = slice(s * C, (s + 1) * C)
            q, k, v = q_ref[rows, :], k_ref[rows, :], v_ref[rows, :]
            dos = []
            for j in range(4):
                sl = slice(128 * j, 128 * j + 128)
                oj = o_ref[rows, sl]
                xc = oj - _seg_mean(oj)
                rs = lax.rsqrt(_seg_mean(xc * xc) + GN_EPS)
                rn = xc * rs
                gj = g_ref[rows, sl]
                sg = _sigmoid(gj)
                dret = dr_ref[rows, sl]
                dg_ref[rows, sl] = dret * rn * (sg * (1.0 + gj * (1.0 - sg)))
                drn = dret * (gj * sg)
                dos.append(rs * (drn - _seg_mean(drn) - rn * _seg_mean(drn * rn)))
            do = [jnp.concatenate(dos[0:2], axis=1), jnp.concatenate(dos[2:4], axis=1)]
            do8 = jnp.concatenate(do, axis=1).astype(BF16)
            drb = dR[...].astype(BF16)
            rb = st_ref[s]
            dq = _nt(do8, rb) * a_ref[...]
            dk = _nt(v, drb) * b_ref[...]
            kb = (k.astype(F32) * b_ref[...]).astype(BF16)
            dvall = _nn(kb, drb)
            qs = _stack_heads(q, lane_head, n=8)
            dec = dt_ref[...]
            p = (_nt(qs, k) * dec).astype(BF16)
            dos = [_stack_heads(do[g], col_head).astype(BF16) for g in range(2)]
            dp = jnp.concatenate([_nt(dos[g], v[:, 256 * g:256 * g + 256]) for g in range(2)], axis=0)
            ds = (dp * dec).astype(BF16)
            dq = dq + _unstack_heads(_nn(ds, k), lane_head, n=8)
            dk = dk + _tn(ds, qs)
            dv = [dvall[:, 256 * g:256 * g + 256] + _tn(p[4 * C * g:4 * C * (g + 1)], dos[g]) for g in range(2)]
            qa = (q.astype(F32) * a_ref[...]).astype(BF16)
            dR[...] = dR[...] * lam_ref[...] + _tn(qa, do8) * bd_ref[...]
            dq_ref[rows, :] = dq
            dk_ref[rows, :] = dk
            dv_ref[rows, 0:256] = dv[0]
            dv_ref[rows, 256:512] = dv[1]

        @pl.when(pl.program_id(0) == steps - 1)
        def _():
            exch.middle()
            exch.finish()

    rev = lambda w: pl.BlockSpec((C * G, w), lambda n: (steps - 1 - n, 0))
    full = lambda a: pl.BlockSpec(a.shape, lambda n: (0,) * a.ndim)
    return _carry(
        "ret_bwd", body, exchange, exchange_args, (qr, kr, rv, proj, o_raw, states, dret, dtab, a_tab, b_tab, lam, bd),
        [rev(256), rev(256), rev(512), rev(512), rev(512),
         pl.BlockSpec((G, 256, 512), lambda n: (steps - 1 - n, 0, 0)), rev(512),
         full(dtab), full(a_tab), full(b_tab), full(lam), full(bd)],
        [rev(256), rev(256), rev(512), rev(512)],
        [jax.ShapeDtypeStruct((S, 256), F32), jax.ShapeDtypeStruct((S, 256), F32),
         jax.ShapeDtypeStruct((S, 512), F32), jax.ShapeDtypeStruct((S, 512), F32)],
        scratch_shapes=[pltpu.VMEM((256, 512), F32)], grid=(steps,), semantics=("arbitrary",), after=after)


def _rot_bwd(cos, sin, spread, dqr, dkr, drv, drg, dq_att, dk_att, dv_att):
    tm = 256

    def body(cos_ref, sin_ref, e_ref, dqr_ref, dkr_ref, drv_ref, drg_ref, dqa_ref, dka_ref, dva_ref, dp_ref):
        cr, ca, sr, sa = _rot_tables(cos_ref, sin_ref, e_ref)
        lo_r, lo_a = _rot_halves(tm)

        def unrot_r(g):
            gs = g * sr
            return g * cr + pltpu.roll(jnp.where(lo_r, -gs, 0.0), 16, 1) + pltpu.roll(jnp.where(lo_r, 0.0, gs), 240, 1)

        def unrot_a(g):
            gs = g * sa
            return g * ca + pltpu.roll(jnp.where(lo_a, -gs, 0.0), 8, 1) + pltpu.roll(jnp.where(lo_a, 0.0, gs), 504, 1)

        def wide(ref):
            return jnp.concatenate([ref[j] for j in range(4)], axis=1)

        dp_ref[:, 0:256] = unrot_r(dqr_ref[...]).astype(BF16)
        dp_ref[:, 256:512] = unrot_r(dkr_ref[...] * RET_SCALE).astype(BF16)
        dp_ref[:, 512:1024] = drv_ref[...].astype(BF16)
        dp_ref[:, 1024:1536] = drg_ref[...].astype(BF16)
        dp_ref[:, 1536:2048] = unrot_a(wide(dqa_ref)).astype(BF16)
        dp_ref[:, 2048:2560] = unrot_a(wide(dka_ref)).astype(BF16)
        dp_ref[:, 2560:3072] = wide(dva_ref).astype(BF16)

    row = lambda w: pl.BlockSpec((tm, w), lambda i: (i, 0))
    slab = pl.BlockSpec((4, tm, 128), lambda i: (0, i, 0))
    return pl.pallas_call(
        body, grid=(S // tm,), name="rot_bwd",
        in_specs=[row(128), row(128), pl.BlockSpec((128, 768), lambda i: (0, 0)),
                  row(256), row(256), row(512), row(512), slab, slab, slab],
        out_specs=row(PW), out_shape=jax.ShapeDtypeStruct((S, PW), BF16),
        compiler_params=_params("parallel"),
    )(cos, sin, spread, dqr, dkr, drv, drg, dq_att, dk_att, dv_att)


def _win_bwd_w(h1, dproj, exchange, exchange_args):
    def body(h_ref, dp_ref, dw_ref, xc):
        k = pl.program_id(0)

        @pl.when(k == 0)
        def _():
            xc.start()

        dw_ref[...] = _tn(h_ref[...], dp_ref[...]).astype(BF16)

        @pl.when(k == N_CHIP - 1)
        def _():
            xc.middle()
            xc.finish()

    (dw,), out = _carry(
        "win_bwd_w", body, exchange, exchange_args, (h1, dproj),
        [pl.BlockSpec((S, D), lambda k: (0, 0)), pl.BlockSpec((S, WIN_C), lambda k: (0, k))],
        [pl.BlockSpec((None, D, WIN_C), lambda k: (k, 0, 0))],
        [jax.ShapeDtypeStruct((N_CHIP, D, WIN_C), BF16)], grid=(N_CHIP,), semantics=("arbitrary",))
    return dw, out


def _in_bwd(dproj, win_g, x, dx2, g1, after):
    tm = 512

    def body(dp_ref, w_ref, x_ref, dx2_ref, g_ref, dx_ref, dg_ref, _):
        @pl.when(pl.program_id(0) == 0)
        def _():
            dg_ref[...] = jnp.zeros_like(dg_ref)

        dh = _nt(dp_ref[:, 0:WIN_C], w_ref[0])
        for k in range(1, N_CHIP):
            dh = dh + _nt(dp_ref[:, k * WIN_C:(k + 1) * WIN_C], w_ref[k])
        xv = x_ref[...]
        r = _rstd(xv)
        xn = xv * r
        dg_ref[...] = dg_ref[...] + jnp.sum(dh * xn, axis=0, keepdims=True)
        t = dh * g_ref[...]
        dx_ref[...] = dx2_ref[...] + r * (t - xn * jnp.mean(t * xn, axis=-1, keepdims=True))

    row = lambda w: pl.BlockSpec((tm, w), lambda i: (i, 0))
    vec = pl.BlockSpec((1, D), lambda i: (0, 0))
    return _carry("in_bwd", body, _NoExchange(), (), (dproj, win_g, x, dx2, g1),
                  [row(PW), pl.BlockSpec((N_CHIP, D, WIN_C), lambda i: (0, 0, 0)), row(D), row(D), vec],
                  [row(D), vec], [jax.ShapeDtypeStruct((S, D), F32), jax.ShapeDtypeStruct((1, D), F32)],
                  grid=(S // tm,), semantics=("arbitrary",), after=after)[0]


ANY = pl.BlockSpec(memory_space=pl.ANY)
VMEM = pl.BlockSpec(memory_space=pltpu.VMEM)
FLIPS = ((1, 0), (0, 1), (1, 1))


def _place():
    x, y, c = lax.axis_index("x"), lax.axis_index("y"), lax.axis_index("c")
    chips = [((1 - x) if fx else x, (1 - y) if fy else y) for fx, fy in FLIPS]
    return x, y, c, 2 * x + y, chips


def _remote(src, dst, send_sem, recv_sem, device):
    return pltpu.make_async_remote_copy(src_ref=src, dst_ref=dst, send_sem=send_sem, recv_sem=recv_sem,
                                        device_id=device, device_id_type=MESH)


class _Exchange:
    aliases = {}

    def middle(self, ins, outs, sems):
        pass


class _GatherShards(_Exchange):
    def __init__(self, shards):
        n = self.n = len(shards)
        self.n_in = self.n_out = n
        self.out_shape = [jax.ShapeDtypeStruct((N_CHIP,) + s.shape, s.dtype) for s in shards]
        dma = pltpu.SemaphoreType.DMA
        self.scratch = [dma((3 * n,)), dma((3 * n,)), dma((3 * n,)), dma((3 * n,)), dma((n,)), dma((n,))]

    def _ici(self, ins, outs, sems, a, j, chip):
        x, y, c, me, chips = _place()
        half = ins[a].shape[0] // 2
        return _remote(ins[a].at[pl.ds(c * half, half), :], outs[a].at[me, pl.ds(c * half, half), :],
                       sems[0].at[3 * a + j], sems[1].at[3 * a + j], (*chip, c))

    def _fwd(self, outs, sems, a, j, chip, half_of):
        x, y, c, me, chips = _place()
        half = outs[a].shape[1] // 2
        blk = outs[a].at[2 * chip[0] + chip[1], pl.ds(half_of * half, half), :]
        return _remote(blk, blk, sems[2].at[3 * a + j], sems[3].at[3 * a + j], (x, y, 1 - c))

    def _own(self, ins, outs, sems, a):
        return _own_shard_to_sibling(ins[a], outs[a], sems[4].at[a], sems[5].at[a])

    def start(self, ins, outs, sems):
        chips = _place()[4]
        for a in range(self.n):
            for j, chip in enumerate(chips):
                self._ici(ins, outs, sems, a, j, chip).start()
        for a in range(self.n):
            self._own(ins, outs, sems, a).start()

    def middle(self, ins, outs, sems):
        x, y, c, me, chips = _place()
        for a in range(self.n):
            for j, chip in enumerate(chips):
                half = outs[a].shape[1] // 2
                blk = outs[a].at[2 * chip[0] + chip[1], pl.ds(c * half, half), :]
                _remote(blk, blk, sems[0].at[3 * a + j], sems[1].at[3 * a + j], (x, y, c)).wait_recv()
                self._fwd(outs, sems, a, j, chip, c).start()

    def finish(self, ins, outs, sems):
        x, y, c, me, chips = _place()
        for a in range(self.n):
            for j, chip in enumerate(chips):
                self._fwd(outs, sems, a, j, chip, 1 - c).wait_recv()
        for a in range(self.n):
            for j, chip in enumerate(chips):
                self._ici(ins, outs, sems, a, j, chip).wait_send()
                self._fwd(outs, sems, a, j, chip, c).wait_send()
            self._own(ins, outs, sems, a).wait()


def _own_shard_to_sibling(shard_ref, gathered_ref, send_sem, recv_sem):
    x, y, c, me, chips = _place()
    return _remote(shard_ref, gathered_ref.at[me], send_sem, recv_sem, (x, y, 1 - c))


class _NoExchange(_Exchange):
    n_in = n_out = 0
    out_shape = ()
    scratch = ()

    def start(self, ins, outs, sems):
        pass

    def finish(self, ins, outs, sems):
        pass


class _ForwardGathered(_Exchange):
    def __init__(self, shards, own=True, forward=True):
        self.own, self.forward = own, forward
        n = self.n = len(shards)
        self.n_in, self.n_out = 2 * n, n
        self.out_shape = [jax.ShapeDtypeStruct((N_CHIP,) + s.shape, s.dtype) for s in shards]
        dma = pltpu.SemaphoreType.DMA
        self.scratch = [dma((3 * n,)), dma((3 * n,)), dma((n,)), dma((n,))]
        self.aliases = {n + a: a for a in range(n)}

    def _fwd(self, outs, sems, a, j, chip, half_of):
        x, y, c, me, chips = _place()
        half = outs[a].shape[1] // 2
        blk = outs[a].at[2 * chip[0] + chip[1], pl.ds(half_of * half, half), :]
        return _remote(blk, blk, sems[0].at[3 * a + j], sems[1].at[3 * a + j], (x, y, 1 - c))

    def _own(self, ins, outs, sems, a):
        return _own_shard_to_sibling(ins[a], outs[a], sems[2].at[a], sems[3].at[a])

    def start(self, ins, outs, sems):
        x, y, c, me, chips = _place()
        for a in range(self.n):
            for j, chip in enumerate(chips if self.forward else ()):
                self._fwd(outs, sems, a, j, chip, c).start()
        for a in range(self.n if self.own else 0):
            self._own(ins, outs, sems, a).start()

    def finish(self, ins, outs, sems):
        x, y, c, me, chips = _place()
        for a in range(self.n):
            for j, chip in enumerate(chips if self.forward else ()):
                self._fwd(outs, sems, a, j, chip, 1 - c).wait_recv()
        for a in range(self.n):
            for j, chip in enumerate(chips if self.forward else ()):
                self._fwd(outs, sems, a, j, chip, c).wait_send()
            if self.own:
                self._own(ins, outs, sems, a).wait()


HBM = pl.BlockSpec(memory_space=pltpu.HBM)
SEMS = pl.BlockSpec(memory_space=pltpu.SEMAPHORE)
DATAFLOW = pltpu.SideEffectType.DATAFLOW_SIDE_EFFECTING


class _OverIci:
    def __init__(self, name, sources, lands):
        self.name, self.n = name, len(sources)
        hbm = lambda t: pltpu.with_memory_space_constraint(t, pltpu.HBM)
        self.arrays = [hbm(t) for t in sources] + [hbm(t) for t in lands]

    def sent(self, src, land, a, chip):
        raise NotImplementedError

    def landed(self, land, a, chip):
        raise NotImplementedError

    def _copy(self, arr, sems, a, j, receiving):
        x, y, c, me, chips = _place()
        src, dst = self.sent(arr[a], arr[self.n + a], a, chips[j])
        if receiving:
            dst = self.landed(arr[self.n + a], a, chips[j])
        return _remote(src, dst, sems[0].at[3 * a + j], sems[1].at[3 * a + j], (*chips[j], c))

    def start(self, after):
        m = len(self.arrays)

        def body(*refs):
            arr, sems, token = refs[:m], refs[m + 1:m + 3], refs[-1]
            for a in range(self.n):
                for j in range(3):
                    self._copy(arr, sems, a, j, False).start()
            token[...] = jnp.zeros_like(token)

        dma = pltpu.SemaphoreType.DMA
        outs = pl.pallas_call(
            body, name=self.name + "_start",
            out_shape=[dma((3 * self.n,)), dma((3 * self.n,))] + [pltpu.HBM(t.shape, t.dtype) for t in self.arrays]
                      + [jax.ShapeDtypeStruct((8, 128), F32)],
            in_specs=[HBM] * m + [ANY], out_specs=[SEMS, SEMS] + [HBM] * m + [VMEM],
            input_output_aliases={i: 2 + i for i in range(m)},
            compiler_params=pltpu.CompilerParams(has_side_effects=DATAFLOW),
        )(*self.arrays, after)
        self.sems, self.arrays = outs[0:2], list(outs[2:2 + m])
        return outs[-1]

    def wait(self, after):
        m = len(self.arrays)

        def body(*refs):
            arr, sems = refs[:m], refs[m:m + 2]
            for a in range(self.n):
                for j in range(3):
                    self._copy(arr, sems, a, j, False).wait_send()
                    self._copy(arr, sems, a, j, True).wait_recv()

        outs = pl.pallas_call(
            body, name=self.name + "_wait",
            out_shape=[pltpu.HBM(t.shape, t.dtype) for t in self.arrays],
            in_specs=[HBM] * m + [SEMS, SEMS, ANY], out_specs=[HBM] * m,
            input_output_aliases={i: i for i in range(m)},
            compiler_params=pltpu.CompilerParams(has_side_effects=DATAFLOW),
        )(*self.arrays, *self.sems, after)
        return list(outs[:self.n]), list(outs[self.n:])


class _GatherOverIci(_OverIci):
    def __init__(self, name, shards):
        super().__init__(name, shards, [lax.empty((N_CHIP,) + s.shape, s.dtype) for s in shards])

    @staticmethod
    def _half(ref):
        c = lax.axis_index("c")
        half = ref.shape[-2] // 2
        return pl.ds(c * half, half)

    def sent(self, src, land, a, chip):
        return src.at[self._half(src), :], land.at[_place()[3], self._half(src), :]

    def landed(self, land, a, chip):
        return land.at[2 * chip[0] + chip[1], self._half(land), :]


class _SumOverIci(_OverIci):
    def __init__(self, name, pre):
        super().__init__(name, pre, [lax.empty(p.shape, p.dtype) for p in pre])

    def sent(self, src, land, a, chip):
        return src.at[2 * chip[0] + chip[1]], land.at[_place()[3]]

    def landed(self, land, a, chip):
        return land.at[2 * chip[0] + chip[1]]


class _HalvesToSibling(_Exchange):
    def __init__(self, grads):
        n = self.n = len(grads)
        self.n_in = self.n_out = n
        self.out_shape = [jax.ShapeDtypeStruct((N_CHIP, g.shape[1] // 2, g.shape[2]), g.dtype) for g in grads]
        self.scratch = [pltpu.SemaphoreType.DMA((n,)), pltpu.SemaphoreType.DMA((n,))]

    def _copy(self, ins, outs, sems, a):
        x, y, c, me, chips = _place()
        half = ins[a].shape[1] // 2
        return _remote(ins[a].at[:, pl.ds((1 - c) * half, half), :], outs[a], sems[0].at[a], sems[1].at[a], (x, y, 1 - c))

    def start(self, ins, outs, sems):
        for a in range(self.n):
            self._copy(ins, outs, sems, a).start()

    def finish(self, ins, outs, sems):
        for a in range(self.n):
            self._copy(ins, outs, sems, a).wait_recv()
        for a in range(self.n):
            self._copy(ins, outs, sems, a).wait_send()


class _ShareHalves(_Exchange):
    def __init__(self, fulls):
        n = self.n = len(fulls)
        self.n_in = self.n_out = n
        self.out_shape = [jax.ShapeDtypeStruct(f.shape, f.dtype) for f in fulls]
        self.scratch = [pltpu.SemaphoreType.DMA((n,)), pltpu.SemaphoreType.DMA((n,))]
        self.aliases = {a: a for a in range(n)}

    def _copy(self, outs, sems, a, half_of):
        x, y, c, me, chips = _place()
        half = outs[a].shape[0] // 2
        rows = outs[a].at[pl.ds(half_of * half, half), :]
        return _remote(rows, rows, sems[0].at[a], sems[1].at[a], (x, y, 1 - c))

    def start(self, ins, outs, sems):
        c = _place()[2]
        for a in range(self.n):
            self._copy(outs, sems, a, c).start()

    def finish(self, ins, outs, sems):
        c = _place()[2]
        for a in range(self.n):
            self._copy(outs, sems, a, 1 - c).wait_recv()
        for a in range(self.n):
            self._copy(outs, sems, a, c).wait_send()


class _GatherBlocks(_Exchange):
    def __init__(self, block):
        self.n_in = self.n_out = 1
        self.out_shape = [jax.ShapeDtypeStruct((8,) + block.shape, block.dtype)]
        dma = pltpu.SemaphoreType.DMA
        self.scratch = [dma((7,)), dma((7,)), dma]

    @staticmethod
    def _peer(f):
        x, y, c, me, chips = _place()
        return ((1 - x) if f & 4 else x, (1 - y) if f & 2 else y, (1 - c) if f & 1 else c)

    def start(self, ins, outs, sems):
        x, y, c, me, chips = _place()
        for f in range(1, 8):
            _remote(ins[0], outs[0].at[2 * me + c], sems[0].at[f - 1], sems[1].at[f - 1], self._peer(f)).start()
        pltpu.make_async_copy(ins[0], outs[0].at[2 * me + c], sems[2]).start()

    def finish(self, ins, outs, sems):
        x, y, c, me, chips = _place()
        for f in range(1, 8):
            px, py, pc = self._peer(f)
            blk = outs[0].at[4 * px + 2 * py + pc]
            _remote(blk, blk, sems[0].at[f - 1], sems[1].at[f - 1], (x, y, c)).wait_recv()
        for f in range(1, 8):
            _remote(ins[0], outs[0].at[2 * me + c], sems[0].at[f - 1], sems[1].at[f - 1], self._peer(f)).wait_send()
        pltpu.make_async_copy(ins[0], outs[0].at[2 * me + c], sems[2]).wait()


class _Both(_Exchange):
    def __init__(self, first, second):
        self.parts = (first, second)
        self.n_in, self.n_out = first.n_in + second.n_in, first.n_out + second.n_out
        self.out_shape = first.out_shape + second.out_shape
        self.scratch = first.scratch + second.scratch
        self.aliases = dict(first.aliases)
        self.aliases.update({first.n_in + i: first.n_out + o for i, o in second.aliases.items()})

    def _split(self, ins, outs, sems):
        a, b = self.parts
        return ((a, ins[:a.n_in], outs[:a.n_out], sems[:len(a.scratch)]),
                (b, ins[a.n_in:], outs[a.n_out:], sems[len(a.scratch):]))

    def start(self, ins, outs, sems):
        for ex, i, o, s in self._split(ins, outs, sems):
            ex.start(i, o, s)

    def middle(self, ins, outs, sems):
        for ex, i, o, s in self._split(ins, outs, sems):
            ex.middle(i, o, s)

    def finish(self, ins, outs, sems):
        for ex, i, o, s in self._split(ins, outs, sems):
            ex.finish(i, o, s)


class _Bound:
    def __init__(self, ex, ins, outs, sems):
        self.start = lambda: ex.start(ins, outs, sems)
        self.middle = lambda: ex.middle(ins, outs, sems)
        self.finish = lambda: ex.finish(ins, outs, sems)


def _carry(name, body, ex, ex_args, args, in_specs, out_specs, out_shape, scratch_shapes=(), grid=None, semantics=(),
           after=None):
    n_a, n_o, n_s = len(args), len(out_shape), len(scratch_shapes)
    behind = [] if after is None else [after]

    def full_body(*refs):
        p = 0
        groups = []
        for size in (n_a, ex.n_in, len(behind), n_o, ex.n_out, n_s, len(ex.scratch)):
            groups.append(refs[p:p + size])
            p += size
        a, ei, _, o, eo, s, es = groups
        body(*a, *o, *s, _Bound(ex, ei, eo, es))

    kwargs = {} if grid is None else {"grid": grid}
    outs = pl.pallas_call(
        full_body, name=name,
        in_specs=list(in_specs) + [ANY] * (ex.n_in + len(behind)), out_specs=list(out_specs) + [ANY] * ex.n_out,
        out_shape=list(out_shape) + list(ex.out_shape), scratch_shapes=list(scratch_shapes) + list(ex.scratch),
        input_output_aliases={n_a + i: n_o + o for i, o in ex.aliases.items()},
        compiler_params=_params(*semantics) if semantics else pltpu.CompilerParams(vmem_limit_bytes=VMEM_LIMIT),
        **kwargs,
    )(*args, *ex_args, *behind)
    return outs[:n_o], outs[n_o:]


def _prepare_carrying(name, x, g1, pos, ifc, arrays, ex, ex_args):
    n = len(arrays)
    r, cc = arrays[0].shape
    steps = 4
    tr, tm = r // steps, S // steps

    def body(x_ref, g_ref, pos_ref, ifc_ref, *refs):
        src, h_ref, cos_ref, sin_ref, dst, xc = refs[:n], refs[n], refs[n + 1], refs[n + 2], refs[n + 3:2 * n + 3], refs[-1]

        @pl.when(pl.program_id(0) == 0)
        def _():
            xc.start()

        xv = x_ref[...]
        h_ref[...] = (xv * _rstd(xv) * g_ref[...]).astype(BF16)
        ang = pos_ref[...].astype(F32) * ifc_ref[...]
        cos_ref[...] = jnp.cos(ang)
        sin_ref[...] = jnp.sin(ang)
        for a in range(n):
            dst[a][...] = src[a][...].astype(BF16)

        @pl.when(pl.program_id(0) == steps - 1)
        def _():
            xc.middle()
            xc.finish()

    row = lambda w: pl.BlockSpec((tm, w), lambda i: (i, 0))
    const = lambda w: pl.BlockSpec((1, w), lambda i: (0, 0))
    blk = pl.BlockSpec((tr, cc), lambda i: (i, 0))
    return _carry(name, body, ex, ex_args, (x, g1, pos, ifc, *arrays),
                  [row(D), const(D), row(1), const(128)] + [blk] * n,
                  [row(D), row(128), row(128)] + [blk] * n,
                  [jax.ShapeDtypeStruct((S, D), BF16)] + [jax.ShapeDtypeStruct((S, 128), F32)] * 2
                  + [jax.ShapeDtypeStruct((r, cc), BF16)] * n,
                  grid=(steps,), semantics=("arbitrary",))


def _exchange_alone(name, ex, ex_args):
    def body(xc):
        xc.start()
        xc.middle()
        xc.finish()

    return _carry(name, body, ex, ex_args, (), (), (), ())[1]


def _core_index():
    return lax.axis_index("c").astype(jnp.int32).reshape(1)


def _pair_sum(gs, gots):
    n = len(gs)
    _, r, cc = gs[0].shape
    half = r // 2

    def body(c_ref, *refs):
        for a in range(n):
            refs[2 * n + a][...] = (refs[a][...].astype(F32) + refs[n + a][...].astype(F32)).astype(BF16)

    mine = pl.BlockSpec((None, half, cc), lambda k, c_ref: (k, c_ref[0], 0))
    blk = pl.BlockSpec((None, half, cc), lambda k, c_ref: (k, 0, 0))
    return pl.pallas_call(
        body, name=f"pair_sum_{r}x{cc}",
        grid_spec=pltpu.PrefetchScalarGridSpec(
            num_scalar_prefetch=1, grid=(N_CHIP,), in_specs=[mine] * n + [blk] * n, out_specs=[blk] * n),
        out_shape=[jax.ShapeDtypeStruct((N_CHIP, half, cc), BF16)] * n,
        compiler_params=_params("parallel"),
    )(_core_index(), *gs, *gots)


def _chip_sum(pre, parts):
    n = len(parts)
    _, half, cc = parts[0].shape
    tr = half // 2
    me = 2 * lax.axis_index("x") + lax.axis_index("y")
    others = [k + (k >= me).astype(jnp.int32) for k in range(3)]
    where = jnp.stack([lax.axis_index("c"), me, *others]).astype(jnp.int32)

    def body(w_ref, *refs):
        for a in range(n):
            own, p1, p2, p3 = refs[4 * a:4 * a + 4]
            refs[4 * n + a][...] = ((own[...].astype(F32) + p1[...].astype(F32)) + p2[...].astype(F32)) + p3[...].astype(F32)

    slot = lambda s: pl.BlockSpec((None, tr, cc), lambda i, w_ref: (w_ref[s], i, 0))
    operands = []
    for a in range(n):
        operands += [pre[a], parts[a], parts[a], parts[a]]
    return pl.pallas_call(
        body, name=f"chip_sum_{half}x{cc}",
        grid_spec=pltpu.PrefetchScalarGridSpec(
            num_scalar_prefetch=1, grid=(2,),
            in_specs=[slot(1), slot(2), slot(3), slot(4)] * n,
            out_specs=[pl.BlockSpec((tr, cc), lambda i, w_ref: (2 * w_ref[0] + i, 0))] * n),
        out_shape=[jax.ShapeDtypeStruct((2 * half, cc), F32)] * n,
        compiler_params=_params("parallel"),
    )(where, *operands)


def _adamw_math(w, g, m, v):
    m = ADAM_B1 * m + (1.0 - ADAM_B1) * g
    v = ADAM_B2 * v + (1.0 - ADAM_B2) * (g * g)
    m_hat = m / (1.0 - ADAM_B1 ** ADAM_STEP)
    v_hat = v / (1.0 - ADAM_B2 ** ADAM_STEP)
    delta = -ADAM_LR * (m_hat / (jnp.sqrt(v_hat) + ADAM_EPS) + ADAM_WD * w)
    return delta, m, v


def _adamw(w, g, m, v, after=None):
    r, cc = w.shape
    tr = r // 4

    def body(w_ref, g_ref, m_ref, v_ref, go_ref, d_ref, nm_ref, nv_ref, _):
        g = g_ref[...]
        go_ref[...] = g
        d_ref[...], nm_ref[...], nv_ref[...] = _adamw_math(w_ref[...], g, m_ref[...], v_ref[...])

    blk = pl.BlockSpec((tr, cc), lambda i: (i, 0))
    return _carry(f"adamw_{r}x{cc}", body, _NoExchange(), (), (w, g, m, v), [blk] * 4, [blk] * 4,
                  [jax.ShapeDtypeStruct((r, cc), F32)] * 4, grid=(4,), semantics=("parallel",), after=after)[0]


def _pack8(rows):
    def body(*refs):
        out_ref = refs[-1]
        out_ref[...] = jnp.zeros_like(out_ref)
        for i, r in enumerate(refs[:-1]):
            out_ref[i:i + 1, :] = r[...]

    return pl.pallas_call(body, name="pack8", out_shape=jax.ShapeDtypeStruct((8, D), F32))(*rows)


def _adamw_gains(gall, ws, ms, vs):
    def body(ga_ref, *refs):
        w, m, v = refs[0:4], refs[4:8], refs[8:12]
        outs, loss_ref, total = refs[12:28], refs[28], refs[29]
        g = ga_ref[0]
        for dev in range(1, 8):
            g = g + ga_ref[dev]
        total[...] = g
        for i in range(4):
            gi = total[i:i + 1, :]
            outs[i][...] = gi
            outs[4 + i][...], outs[8 + i][...], outs[12 + i][...] = _adamw_math(w[i][...], gi, m[i][...], v[i][...])
        loss_ref[...] = total[4:5, 0:128] * (0.5 / D)

    outs = pl.pallas_call(
        body, name="adamw_gains",
        out_shape=[jax.ShapeDtypeStruct((1, D), F32)] * 16 + [jax.ShapeDtypeStruct((1, 128), F32)],
        scratch_shapes=[pltpu.VMEM((8, D), F32)],
    )(gall, *ws, *ms, *vs)
    return outs[0:4], outs[4:8], outs[8:12], outs[12:16], outs[16]


def kernel(x, positions, w_in, w_out, g_pre_mix, g_post_mix, g_pre_ffn, g_post_ffn, w_gate, w_up, w_down, loss_target, m_w_in, m_w_out, m_g_pre_mix, m_g_post_mix, m_g_pre_ffn, m_g_post_ffn, m_w_gate, m_w_up, m_w_down, v_w_in, v_w_out, v_g_pre_mix, v_g_post_mix, v_g_pre_ffn, v_g_post_ffn, v_w_gate, v_w_up, v_w_down):
    tr = lambda t: jnp.swapaxes(t, 1, 2)[0]
    shards = [w_in[0], w_out[0], tr(w_gate), tr(w_up), w_down[0]]
    moms = [m_w_in[0], m_w_out[0], tr(m_w_gate), tr(m_w_up), m_w_down[0]]
    vels = [v_w_in[0], v_w_out[0], tr(v_w_gate), tr(v_w_up), v_w_down[0]]
    xs, pos, tgt = x[0], positions.reshape(S, 1), loss_target[0]
    g1, g2, g3, g4 = g_pre_mix, g_post_mix, g_pre_ffn, g_post_ffn
    tabs = tuple(jnp.asarray(t) for t in _retention_tables())
    ifc, spread = _rotary_tables()
    ifc, spread = jnp.asarray(ifc), jnp.asarray(spread, dtype=BF16)
    bf = [s.astype(BF16) for s in shards[:2]]

    (h1, cos, sin, *ffn_bf), (win_g,) = _prepare_carrying(
        "gather_in", xs, g1, pos, ifc, shards[2:], _GatherShards(bf[:1]), bf[:1])
    bf += list(ffn_bf)
    wout_gather = _GatherOverIci("wout_gather", bf[1:2])
    token = wout_gather.start(win_g)
    ffn_gather = _GatherOverIci("ffn_gather", bf[2:])
    token = ffn_gather.start(token)
    qr, kr, rv, rg, aq, ak, av = _proj_fwd(h1, win_g, cos, sin, spread, token)
    wout_sh, wout_land = wout_gather.wait(qr)
    n_ffn = len(bf[2:])
    (att_out, lse, cat_a), (wout_g, *ffn_gather.arrays[n_ffn:]) = _att_fwd(
        aq, ak, av, _Both(_ForwardGathered(bf[1:2]), _ForwardGathered(bf[2:], forward=False)),
        [*wout_sh, *wout_land, *ffn_gather.arrays])
    wout_g = wout_g.reshape(D, D)
    (o_raw, cat_r, states), _ = _ret_fwd(qr, kr, rv, rg, tabs, _NoExchange(), (), cat_a)
    ffn_sh, ffn_lands = ffn_gather.wait(cat_r)
    (mix, x2, h3), (wg_g, wu_g, wd_g) = _mix_fwd(cat_r, cat_a, wout_g, xs, g2, g3,
                                                _ForwardGathered(bf[2:], own=False), [*ffn_sh, *ffn_lands])
    gt, up, a, sq, dy, df, dg4 = _ffn_fwd(h3, wg_g, wu_g, wd_g, x2, tgt, g4)

    dgt, dup, dx2, dmix, dg3, dg2 = _ffn_bwd_act(df, gt, up, wg_g, wu_g, wd_g, dy, x2, mix, g2, g3)
    ffn_grads = list(_ffn_bwd_w(a, df, h3, dgt, dup))
    (dret, datt, dwout), got = _mix_bwd(dmix, cat_r, cat_a, wout_g, _HalvesToSibling(ffn_grads), ffn_grads)
    ffn_sum = _SumOverIci("ffn_sum", _pair_sum(ffn_grads, got))
    token = ffn_sum.start(datt)
    (dq_att, dk_att, dv_att), _ = _att_bwd(aq, ak, av, datt, att_out, lse, _NoExchange(), (), token)
    (dqr, dkr, drv, drg), _ = _ret_bwd(qr, kr, rv, rg, o_raw, states, dret, tabs, _NoExchange(), (), token)
    dproj = _rot_bwd(cos, sin, spread, dqr, dkr, drv, drg, dq_att, dk_att, dv_att)
    sums = _chip_sum(*ffn_sum.wait(dproj))
    dwin, ffn_full = _win_bwd_w(h1, dproj, _ShareHalves(sums), sums)
    in_grads = [dwin, dwout.reshape(N_CHIP, WOUT_R, D)]

    got = _exchange_alone("halves_to_sibling", _HalvesToSibling(in_grads), in_grads)
    in_sum = _SumOverIci("in_sum", [*_pair_sum(in_grads[:1], got[:1]), *_pair_sum(in_grads[1:], got[1:])])
    token = in_sum.start(dproj)
    dx, dg1 = _in_bwd(dproj, win_g, xs, dx2, g1, token)
    ffn_upd = [_adamw(shards[2 + i], ffn_full[o], moms[2 + i], vels[2 + i], token)
               for i, o in enumerate((1, 2, 0))]
    pre, parts = in_sum.wait(ffn_upd[2][0])
    sums = [*_chip_sum(pre[:1], parts[:1]), *_chip_sum(pre[1:], parts[1:])]
    gblock = _pack8([dg1, dg2, dg3, dg4, sq])
    *in_full, gall = _exchange_alone("share_rest", _Both(_ShareHalves(sums), _GatherBlocks(gblock)), [*sums, gblock])
    upd = [_adamw(w, g, m, v) for w, g, m, v in zip(shards[:2], in_full, moms[:2], vels[:2])] + ffn_upd
    gg, gd, gm, gv, loss_row = _adamw_gains(gall, [g1, g2, g3, g4],
                                            [m_g_pre_mix, m_g_post_mix, m_g_pre_ffn, m_g_post_ffn],
                                            [v_g_pre_mix, v_g_post_mix, v_g_pre_ffn, v_g_post_ffn])

    def order(mats, vecs):
        back = lambda t: jnp.swapaxes(t[None], 1, 2)
        return [mats[0][None], mats[1][None], *vecs, back(mats[2]), back(mats[3]), mats[4][None]]

    return (loss_row[0, 0], dx[None],
            *order([u[0] for u in upd], gg),
            *order([u[1] for u in upd], gd),
            *order([u[2] for u in upd], gm),
            *order([u[3] for u in upd], gv))
```

```python
import numpy as np
import jax
import jax.numpy as jnp
from jax import lax
from jax.experimental import pallas as pl
from jax.experimental.pallas import tpu as pltpu

F32, BF16 = jnp.float32, jnp.bfloat16
MESH = pl.DeviceIdType.MESH

S = 2048
D = 1024
PW = 3072
N_CHIP = 4
WIN_C = PW // N_CHIP
DFF = 2816
FF_C = DFF // N_CHIP
WOUT_R = D // N_CHIP
RMS_EPS = 1e-6
GN_EPS = 1e-5
RET_C = 128
RET_PER_STEP = 4
RET_SCALE = 32 ** -0.5
ATT_BLK = 128
ATT_SCALE = 64 ** -0.5
PATTERN_DILATIONS = (16, 1, 4)
NEG = -1e30
VMEM_LIMIT = 56 * 1024 * 1024

ADAM_LR, ADAM_B1, ADAM_B2, ADAM_EPS, ADAM_WD, ADAM_STEP = 0.001, 0.9, 0.999, 1e-08, 0.01, 10


def _params(*sem):
    return pltpu.CompilerParams(dimension_semantics=sem, vmem_limit_bytes=VMEM_LIMIT)


def _nt(a, b):
    return lax.dot_general(a, b, (((1,), (1,)), ((), ())), preferred_element_type=F32)


def _tn(a, b):
    return lax.dot_general(a, b, (((0,), (0,)), ((), ())), preferred_element_type=F32)


def _nn(a, b):
    return jnp.dot(a, b, preferred_element_type=F32)


def _rstd(v):
    return lax.rsqrt(jnp.mean(v * v, axis=-1, keepdims=True) + RMS_EPS)


def _sigmoid(v):
    return 1.0 / (1.0 + jnp.exp(-v))


def _rows(i, t):
    return pl.ds(pl.multiple_of(i * t, t), t)


def _retention_tables():
    h = np.arange(8, dtype=np.float32)
    log_g = np.log1p(-np.exp2(-5.0 - h)).astype(np.float32)
    idx = np.arange(RET_C, dtype=np.float32)
    diff = idx[:, None] - idx[None, :]
    dtab = np.where(diff >= 0, np.exp(log_g[:, None, None] * np.maximum(diff, 0.0)), 0.0).astype(np.float32)
    dtab = dtab.reshape(8 * RET_C, RET_C)
    lane_head = np.arange(256) // 32
    a_tab = np.exp(log_g[lane_head][None, :] * (idx + 1.0)[:, None]).astype(np.float32)
    b_tab = np.exp(log_g[lane_head][None, :] * (RET_C - 1.0 - idx)[:, None]).astype(np.float32)
    lam = np.exp(log_g[lane_head] * RET_C).astype(np.float32)[:, None]
    bd = (lane_head[:, None] == (np.arange(512) // 64)[None, :]).astype(np.float32)
    return dtab, a_tab, b_tab, lam, bd


def _rotary_tables():
    inv_r = (1.0 / (np.float32(10000.0) ** np.linspace(0.0, 1.0, 16, dtype=np.float32))).astype(np.float32)
    inv_a = (np.float32(500000.0) ** (-np.arange(0, 16, 2, dtype=np.float32) / np.float32(16))).astype(np.float32)
    ifc = np.zeros((1, 128), np.float32)
    ifc[0, 0:16], ifc[0, 16:24] = inv_r, inv_a
    spread = np.zeros((128, 768), np.float32)
    for lane in range(256):
        spread[(lane % 32) % 16, lane] = 1.0
    for lane in range(512):
        d = lane % 64
        spread[16 + d % 8 if d < 16 else 24, 256 + lane] = 1.0
    return ifc, spread


def _rot_halves(tm):
    lo_r = (lax.broadcasted_iota(jnp.int32, (tm, 256), 1) % 32) < 16
    lo_a = (lax.broadcasted_iota(jnp.int32, (tm, 512), 1) % 64) < 8
    return lo_r, lo_a


def _spread_exact(t, e):
    hi = t.astype(BF16)
    r1 = t - hi.astype(F32)
    mid = r1.astype(BF16)
    lo = (r1 - mid.astype(F32)).astype(BF16)
    return _nn(hi, e) + _nn(mid, e) + _nn(lo, e)


def _rot_tables(cos_ref, sin_ref, e_ref):
    cs = _spread_exact(cos_ref[...], e_ref[...])
    sn = _spread_exact(sin_ref[...], e_ref[...])
    return cs[:, 0:256], cs[:, 256:768], sn[:, 0:256], sn[:, 256:768]


def _proj_fwd(h1, win_g, cos, sin, spread, after):
    tm = 256

    def body(h_ref, w_ref, cos_ref, sin_ref, e_ref, qr_ref, kr_ref, rv_ref, rg_ref, aq_ref, ak_ref, av_ref, p_ref, _):
        h = h_ref[...]
        for k in range(N_CHIP):
            p_ref[:, k * WIN_C:(k + 1) * WIN_C] = _nn(h, w_ref[k])
        cr, ca, sr, sa = _rot_tables(cos_ref, sin_ref, e_ref)
        lo_r, lo_a = _rot_halves(tm)

        def rot_r(v):
            return v * cr + sr * jnp.where(lo_r, -pltpu.roll(v, 240, 1), pltpu.roll(v, 16, 1))

        def rot_a(v):
            return v * ca + sa * jnp.where(lo_a, -pltpu.roll(v, 504, 1), pltpu.roll(v, 8, 1))

        qr_ref[...] = rot_r(p_ref[:, 0:256]).astype(BF16)
        kr_ref[...] = (rot_r(p_ref[:, 256:512]) * RET_SCALE).astype(BF16)
        rv_ref[...] = p_ref[:, 512:1024].astype(BF16)
        rg_ref[...] = p_ref[:, 1024:1536]
        aq, ak = rot_a(p_ref[:, 1536:2048]), rot_a(p_ref[:, 2048:2560])
        for j in range(4):
            aq_ref[j] = aq[:, 128 * j:128 * j + 128]
            ak_ref[j] = ak[:, 128 * j:128 * j + 128]
            av_ref[j] = p_ref[:, 2560 + 128 * j:2560 + 128 * j + 128]

    row = lambda w: pl.BlockSpec((tm, w), lambda i: (i, 0))
    slab = pl.BlockSpec((4, tm, 128), lambda i: (0, i, 0))
    return _carry(
        "proj_fwd", body, _NoExchange(), (), (h1, win_g, cos, sin, spread),
        [row(D), pl.BlockSpec((N_CHIP, D, WIN_C), lambda i: (0, 0, 0)), row(128), row(128),
         pl.BlockSpec((128, 768), lambda i: (0, 0))],
        [row(256), row(256), row(512), row(512), slab, slab, slab],
        [jax.ShapeDtypeStruct((S, w), BF16) for w in (256, 256, 512)]
        + [jax.ShapeDtypeStruct((S, 512), F32)] + [jax.ShapeDtypeStruct((4, S, 128), F32)] * 3,
        scratch_shapes=[pltpu.VMEM((tm, PW), F32)], grid=(S // tm,), semantics=("parallel",), after=after)[0]


def _seg_mean(v):
    lo = lax.broadcasted_iota(jnp.int32, v.shape, 1) < 64
    s_lo = jnp.sum(jnp.where(lo, v, 0.0), axis=-1, keepdims=True)
    s_hi = jnp.sum(jnp.where(lo, 0.0, v), axis=-1, keepdims=True)
    return jnp.where(lo, s_lo, s_hi) * (1.0 / 64.0)


def _ret_fwd(qr, kr, rv, proj, tabs, exchange, exchange_args, after=None):
    C, G = RET_C, RET_PER_STEP
    steps = S // (C * G)
    dtab, a_tab, b_tab, lam, bd = tabs

    def body(q_ref, k_ref, v_ref, g_ref, dt_ref, a_ref, b_ref, lam_ref, bd_ref, o_ref, cat_ref, st_ref, R, exch):
        @pl.when(pl.program_id(0) == 0)
        def _():
            exch.start()
            R[...] = jnp.zeros_like(R)

        lane_head = lax.broadcasted_iota(jnp.int32, (C, 256), 1) // 32
        col_head = lax.broadcasted_iota(jnp.int32, (C, 256), 1) // 64
        for s in range(G):
            rows = slice(s * C, (s + 1) * C)
            q, k, v = q_ref[rows, :], k_ref[rows, :], v_ref[rows, :]
            rb = R[...].astype(BF16)
            st_ref[s] = rb
            qa = (q.astype(F32) * a_ref[...]).astype(BF16)
            cross = _nn(qa, rb)
            p = (_nt(_stack_heads(q, lane_head, n=8), k) * dt_ref[...]).astype(BF16)
            og = [cross[:, 256 * g:256 * g + 256]
                  + _unstack_heads(_nn(p[4 * C * g:4 * C * (g + 1)], v[:, 256 * g:256 * g + 256]), col_head)
                  for g in range(2)]
            kb = (k.astype(F32) * b_ref[...]).astype(BF16)
            R[...] = R[...] * lam_ref[...] + _tn(kb, v) * bd_ref[...]
            o_ref[rows, 0:256] = og[0]
            o_ref[rows, 256:512] = og[1]
            for j in range(4):
                oj = og[j // 2][:, 128 * (j % 2):128 * (j % 2) + 128]
                xc = oj - _seg_mean(oj)
                rn = xc * lax.rsqrt(_seg_mean(xc * xc) + GN_EPS)
                gj = g_ref[rows, 128 * j:128 * j + 128]
                cat_ref[rows, 128 * j:128 * j + 128] = (rn * (gj * _sigmoid(gj))).astype(BF16)

        @pl.when(pl.program_id(0) == steps - 1)
        def _():
            exch.middle()
            exch.finish()

    row = lambda w: pl.BlockSpec((C * G, w), lambda n: (n, 0))
    full = lambda a: pl.BlockSpec(a.shape, lambda n: (0,) * a.ndim)
    return _carry(
        "ret_fwd", body, exchange, exchange_args, (qr, kr, rv, proj, dtab, a_tab, b_tab, lam, bd),
        [row(256), row(256), row(512), row(512),
         full(dtab), full(a_tab), full(b_tab), full(lam), full(bd)],
        [row(512), row(512), pl.BlockSpec((G, 256, 512), lambda n: (n, 0, 0))],
        [jax.ShapeDtypeStruct((S, 512), F32), jax.ShapeDtypeStruct((S, 512), BF16),
         jax.ShapeDtypeStruct((S // C, 256, 512), BF16)],
        scratch_shapes=[pltpu.VMEM((256, 512), F32)], grid=(steps,), semantics=("arbitrary",), after=after)


def _stack_heads(v, lane_head, fill=0.0, n=4):
    return jnp.concatenate([jnp.where(lane_head == h, v, jnp.full_like(v, fill)) for h in range(n)], axis=0)


def _unstack_heads(v, lane_head, n=4):
    out = v[0:ATT_BLK]
    for h in range(1, n):
        out = jnp.where(lane_head == h, v[h * ATT_BLK:(h + 1) * ATT_BLK], out)
    return out


def _att_bias(has_prev):
    nk = 2 * ATT_BLK if has_prev else ATT_BLK
    a = lax.broadcasted_iota(jnp.int32, (4 * ATT_BLK, nk), 0) % ATT_BLK
    kk = lax.broadcasted_iota(jnp.int32, (4 * ATT_BLK, nk), 1)
    if not has_prev:
        return None, jnp.where((a - kk) >= 0, 0.0, NEG)
    dist = ATT_BLK + a - kk
    inside = (dist >= 0) & (dist <= ATT_BLK)
    return jnp.where(inside, 0.0, NEG), jnp.where(inside & (kk >= ATT_BLK), 0.0, NEG)


def _class_rows(ib, r, d):
    if d == 1:
        return pl.ds(pl.multiple_of(ib * ATT_BLK, ATT_BLK), ATT_BLK)
    return pl.ds(ib * ATT_BLK * d + r, ATT_BLK, stride=d)


def _slab_pair(ref, g, rows):
    return jnp.concatenate([ref[2 * g, rows, :], ref[2 * g + 1, rows, :]], axis=1)


def _att_blocks(d):
    nb = S // d // ATT_BLK
    return nb, nb > 1


def _att_fwd(aq, ak, av, exchange, exchange_args):
    def body(q_ref, k_ref, v_ref, o_ref, l_ref, cat_ref, xc):
        xc.start()
        lane_head = lax.broadcasted_iota(jnp.int32, (ATT_BLK, 256), 1) // 64
        for pi, d in enumerate(PATTERN_DILATIONS):
            if pi == len(PATTERN_DILATIONS) - 1:
                xc.middle()
            nb, has_prev = _att_blocks(d)
            bias_rest, bias_first = _att_bias(has_prev)

            def block(b, carry, pi=pi, d=d, nb=nb, has_prev=has_prev, bias_rest=bias_rest, bias_first=bias_first):
                r, ib = b // nb, b % nb
                rows = _class_rows(ib, r, d)
                prow = _class_rows(jnp.maximum(ib - 1, 0), r, d)
                bias = jnp.where(ib == 0, bias_first, bias_rest) if has_prev else bias_first
                for g in range(2):
                    qg = _slab_pair(q_ref, g, rows).astype(BF16)
                    kg = _slab_pair(k_ref, g, rows)
                    vg = _slab_pair(v_ref, g, rows)
                    if has_prev:
                        kg = jnp.concatenate([_slab_pair(k_ref, g, prow), kg], axis=0)
                        vg = jnp.concatenate([_slab_pair(v_ref, g, prow), vg], axis=0)
                    kg, vg = kg.astype(BF16), vg.astype(BF16)
                    s = _nt(_stack_heads(qg, lane_head), kg) * ATT_SCALE + bias
                    m = jnp.max(s, axis=-1, keepdims=True)
                    p = jnp.exp(s - m)
                    den = jnp.sum(p, axis=-1, keepdims=True)
                    og = _unstack_heads(_nn(p.astype(BF16), vg) / den, lane_head)
                    lg = _unstack_heads(jnp.broadcast_to(m + jnp.log(den), (4 * ATT_BLK, 256)), lane_head)
                    for jj in range(2):
                        j = 2 * g + jj
                        o_new, l_new = og[:, 128 * jj:128 * jj + 128], lg[:, 128 * jj:128 * jj + 128]
                        if pi > 0:
                            o_old, l_old = o_ref[j, rows, :], l_ref[j, rows, :]
                            mx = jnp.maximum(l_old, l_new)
                            ea, eb = jnp.exp(l_old - mx), jnp.exp(l_new - mx)
                            den = ea + eb
                            o_new = (ea * o_old + eb * o_new) / den
                            l_new = mx + jnp.log(den)
                        o_ref[j, rows, :] = o_new
                        l_ref[j, rows, :] = l_new
                return carry

            lax.fori_loop(0, S // ATT_BLK, block, 0, unroll=4)

        def to_cat(i, carry):
            rows = _rows(i, 256)
            for j in range(4):
                cat_ref[rows, 128 * j:128 * j + 128] = o_ref[j, rows, :].astype(BF16)
            return carry

        lax.fori_loop(0, S // 256, to_cat, 0)
        xc.finish()

    slab = jax.ShapeDtypeStruct((4, S, 128), F32)
    return _carry("att_fwd", body, exchange, exchange_args, (aq, ak, av), [VMEM] * 3, [VMEM] * 3,
                  [slab, slab, jax.ShapeDtypeStruct((S, 512), BF16)])


def _mix_fwd(cat_r, cat_a, wout, x, g2, g3, exchange, exchange_args):
    tm = 512

    def body(cr_ref, ca_ref, w_ref, x_ref, g2_ref, g3_ref, mix_ref, x2_ref, h3_ref, xc):
        @pl.when(pl.program_id(0) == 0)
        def _():
            xc.start()

        mix = _nn(cr_ref[...], w_ref[0:512, :]) + _nn(ca_ref[...], w_ref[512:1024, :])
        mix_ref[...] = mix
        x2 = x_ref[...] + mix * _rstd(mix) * g2_ref[...]
        x2_ref[...] = x2
        h3_ref[...] = (x2 * _rstd(x2) * g3_ref[...]).astype(BF16)

        @pl.when(pl.program_id(0) == S // tm - 1)
        def _():
            xc.middle()
            xc.finish()

    row = lambda w: pl.BlockSpec((tm, w), lambda i: (i, 0))
    vec = pl.BlockSpec((1, D), lambda i: (0, 0))
    return _carry("mix_fwd", body, exchange, exchange_args, (cat_r, cat_a, wout, x, g2, g3),
                  [row(512), row(512), pl.BlockSpec((D, D), lambda i: (0, 0)), row(D), vec, vec],
                  [row(D), row(D), row(D)],
                  [jax.ShapeDtypeStruct((S, D), F32), jax.ShapeDtypeStruct((S, D), F32),
                   jax.ShapeDtypeStruct((S, D), BF16)],
                  grid=(S // tm,), semantics=("arbitrary",))


def _ffn_fwd(h3, wg, wu, wd, x2, tgt, g4):
    tm = 512
    last = N_CHIP - 1

    def body(h_ref, wg_ref, wu_ref, wd_ref, x2_ref, t_ref, g_ref,
             gt_ref, up_ref, a_ref, loss_ref, dy_ref, df_ref, dg_ref, f_ref):
        k, i = pl.program_id(0), pl.program_id(1)
        h = h_ref[...]
        gt = _nt(h, wg_ref[...])
        up = _nt(h, wu_ref[...])
        gt_ref[...] = gt.astype(BF16)
        up_ref[...] = up.astype(BF16)
        a = (gt * _sigmoid(gt) * up).astype(BF16)
        a_ref[...] = a
        part = _nn(a, wd_ref[...])
        rows = _rows(i, tm)

        @pl.when(k == 0)
        def _():
            f_ref[rows, :] = part

        @pl.when((k > 0) & (k < last))
        def _():
            f_ref[rows, :] = f_ref[rows, :] + part

        @pl.when((k == last) & (i == 0))
        def _():
            loss_ref[...] = jnp.zeros_like(loss_ref)
            dg_ref[...] = jnp.zeros_like(dg_ref)

        @pl.when(k == last)
        def _():
            fv = f_ref[rows, :] + part
            r = _rstd(fv)
            fn = fv * r
            e = x2_ref[...] + fn * g_ref[...] - t_ref[...]
            loss_ref[...] = loss_ref[...] + jnp.sum(jnp.sum(e * e, axis=-1, keepdims=True), axis=0, keepdims=True)
            dy = e * (1.0 / D)
            dy_ref[...] = dy
            dg_ref[...] = dg_ref[...] + jnp.sum(dy * fn, axis=0, keepdims=True)
            t = dy * g_ref[...]
            df_ref[...] = (r * (t - fn * jnp.mean(t * fn, axis=-1, keepdims=True))).astype(BF16)

    wrow = pl.BlockSpec((None, FF_C, D), lambda k, i: (k, 0, 0))
    act = pl.BlockSpec((None, tm, FF_C), lambda k, i: (k, i, 0))
    late = pl.BlockSpec((tm, D), lambda k, i: (jnp.where(k == last, i, 0), 0))
    vec = pl.BlockSpec((1, D), lambda k, i: (0, 0))
    return pl.pallas_call(
        body, grid=(N_CHIP, S // tm), name="ffn_fwd",
        in_specs=[pl.BlockSpec((tm, D), lambda k, i: (i, 0)), wrow, wrow, wrow, late, late, vec],
        out_specs=[act, act, act, vec, late, late, vec],
        out_shape=[jax.ShapeDtypeStruct((N_CHIP, S, FF_C), BF16)] * 3
                  + [jax.ShapeDtypeStruct((1, D), F32), jax.ShapeDtypeStruct((S, D), F32),
                     jax.ShapeDtypeStruct((S, D), BF16), jax.ShapeDtypeStruct((1, D), F32)],
        scratch_shapes=[pltpu.VMEM((S, D), F32)],
        compiler_params=_params("arbitrary", "arbitrary"),
    )(h3, wg, wu, wd, x2, tgt, g4)


def _ffn_bwd_act(df, gt, up, wg, wu, wd, dy, x2, mix, g2, g3):
    tm, sub = 512, 256
    last = N_CHIP - 1

    def body(df_ref, gt_ref, up_ref, wg_ref, wu_ref, wd_ref, dy_ref, x2_ref, mix_ref, g2_ref, g3_ref,
             dgt_ref, dup_ref, dx2_ref, dmix_ref, dg3_ref, dg2_ref, dh_ref):
        k, i = pl.program_id(0), pl.program_id(1)
        parts = []
        for s in range(tm // sub):
            rows = slice(s * sub, (s + 1) * sub)
            da = _nt(df_ref[rows, :], wd_ref[...])
            gt, up = gt_ref[rows, :].astype(F32), up_ref[rows, :].astype(F32)
            sg = _sigmoid(gt)
            dup = (da * gt * sg).astype(BF16)
            dgt = (da * up * (sg * (1.0 + gt * (1.0 - sg)))).astype(BF16)
            dup_ref[rows, :] = dup
            dgt_ref[rows, :] = dgt
            parts.append(_nn(dgt, wg_ref[...]) + _nn(dup, wu_ref[...]))
        part = jnp.concatenate(parts, axis=0)
        rows = _rows(i, tm)

        @pl.when(k == 0)
        def _():
            dh_ref[rows, :] = part

        @pl.when((k > 0) & (k < last))
        def _():
            dh_ref[rows, :] = dh_ref[rows, :] + part

        @pl.when((k == last) & (i == 0))
        def _():
            dg3_ref[...] = jnp.zeros_like(dg3_ref)
            dg2_ref[...] = jnp.zeros_like(dg2_ref)

        @pl.when(k == last)
        def _():
            dh = dh_ref[rows, :] + part
            x2 = x2_ref[...]
            r3 = _rstd(x2)
            xn = x2 * r3
            dg3_ref[...] = dg3_ref[...] + jnp.sum(dh * xn, axis=0, keepdims=True)
            t = dh * g3_ref[...]
            dx2 = dy_ref[...] + r3 * (t - xn * jnp.mean(t * xn, axis=-1, keepdims=True))
            dx2_ref[...] = dx2
            mix = mix_ref[...]
            r2 = _rstd(mix)
            mn = mix * r2
            dg2_ref[...] = dg2_ref[...] + jnp.sum(dx2 * mn, axis=0, keepdims=True)
            u = dx2 * g2_ref[...]
            dmix_ref[...] = (r2 * (u - mn * jnp.mean(u * mn, axis=-1, keepdims=True))).astype(BF16)

    wrow = pl.BlockSpec((None, FF_C, D), lambda k, i: (k, 0, 0))
    act = pl.BlockSpec((None, tm, FF_C), lambda k, i: (k, i, 0))
    row = pl.BlockSpec((tm, D), lambda k, i: (i, 0))
    late = pl.BlockSpec((tm, D), lambda k, i: (jnp.where(k == last, i, 0), 0))
    vec = pl.BlockSpec((1, D), lambda k, i: (0, 0))
    return pl.pallas_call(
        body, grid=(N_CHIP, S // tm), name="ffn_bwd_act",
        in_specs=[row, act, act, wrow, wrow, wrow, late, late, late, vec, vec],
        out_specs=[act, act, late, late, vec, vec],
        out_shape=[jax.ShapeDtypeStruct((N_CHIP, S, FF_C), BF16), jax.ShapeDtypeStruct((N_CHIP, S, FF_C), BF16),
                   jax.ShapeDtypeStruct((S, D), F32), jax.ShapeDtypeStruct((S, D), BF16),
                   jax.ShapeDtypeStruct((1, D), F32), jax.ShapeDtypeStruct((1, D), F32)],
        scratch_shapes=[pltpu.VMEM((S, D), F32)],
        compiler_params=_params("arbitrary", "arbitrary"),
    )(df, gt, up, wg, wu, wd, dy, x2, mix, g2, g3)


def _ffn_bwd_w(a, df, h3, dgt, dup):
    tm = 1024
    assert S // tm == 2

    def body(a_ref, df_ref, h_ref, dgt_ref, dup_ref, dwd_ref, dwg_ref, dwu_ref, acc_d, acc_g, acc_u):
        i = pl.program_id(1)
        h = h_ref[...]
        parts = (_tn(a_ref[...], df_ref[...]), _tn(dgt_ref[...], h), _tn(dup_ref[...], h))

        @pl.when(i == 0)
        def _():
            for acc, part in zip((acc_d, acc_g, acc_u), parts):
                acc[...] = part

        @pl.when(i == S // tm - 1)
        def _():
            for out, acc, part in zip((dwd_ref, dwg_ref, dwu_ref), (acc_d, acc_g, acc_u), parts):
                out[...] = (acc[...] + part).astype(BF16)

    act = pl.BlockSpec((None, tm, FF_C), lambda k, i: (k, i, 0))
    row = pl.BlockSpec((tm, D), lambda k, i: (i, 0))
    wrow = pl.BlockSpec((None, FF_C, D), lambda k, i: (k, 0, 0))
    return pl.pallas_call(
        body, grid=(N_CHIP, S // tm), name="ffn_bwd_w",
        in_specs=[act, row, row, act, act],
        out_specs=[wrow, wrow, wrow],
        out_shape=[jax.ShapeDtypeStruct((N_CHIP, FF_C, D), BF16)] * 3,
        scratch_shapes=[pltpu.VMEM((FF_C, D), F32)] * 3,
        compiler_params=_params("parallel", "arbitrary"),
    )(a, df, h3, dgt, dup)


def _mix_bwd(dmix, cat_r, cat_a, wout, exchange, exchange_args):
    tm = 1024

    def body(dm_ref, cr_ref, ca_ref, w_ref, dret_ref, datt_ref, dw_ref, acc, xc):
        i = pl.program_id(0)

        @pl.when(i == 0)
        def _():
            xc.start()
            acc[...] = jnp.zeros_like(acc)

        dm = dm_ref[...]
        dret_ref[...] = _nt(dm, w_ref[0:512, :])
        datt = _nt(dm, w_ref[512:1024, :])
        for j in range(4):
            datt_ref[j] = datt[:, 128 * j:128 * j + 128]
        acc[0:512, :] += _tn(cr_ref[...], dm)
        acc[512:1024, :] += _tn(ca_ref[...], dm)

        @pl.when(i == S // tm - 1)
        def _():
            dw_ref[...] = acc[...].astype(BF16)
            xc.middle()
            xc.finish()

    row = lambda w: pl.BlockSpec((tm, w), lambda i: (i, 0))
    full = pl.BlockSpec((D, D), lambda i: (0, 0))
    return _carry("mix_bwd", body, exchange, exchange_args, (dmix, cat_r, cat_a, wout),
                  [row(D), row(512), row(512), full],
                  [row(512), pl.BlockSpec((4, tm, 128), lambda i: (0, i, 0)), full],
                  [jax.ShapeDtypeStruct((S, 512), F32), jax.ShapeDtypeStruct((4, S, 128), F32),
                   jax.ShapeDtypeStruct((D, D), BF16)],
                  scratch_shapes=[pltpu.VMEM((D, D), F32)], grid=(S // tm,), semantics=("arbitrary",))


def _att_bwd(aq, ak, av, datt, att_out, lse, exchange, exchange_args, after=None):
    def body(q_ref, k_ref, v_ref, do_ref, out_ref, l_ref, dq_ref, dk_ref, dv_ref, xc):
        xc.start()

        lane_head = lax.broadcasted_iota(jnp.int32, (ATT_BLK, 256), 1) // 64
        for pi, d in enumerate(PATTERN_DILATIONS):
            nb, has_prev = _att_blocks(d)
            assert pi > 0 or not has_prev
            bias_rest, bias_first = _att_bias(has_prev)

            def block(b, carry, pi=pi, d=d, nb=nb, has_prev=has_prev, bias_rest=bias_rest, bias_first=bias_first):
                r, ib = b // nb, b % nb
                rows = _class_rows(ib, r, d)
                prow = _class_rows(jnp.maximum(ib - 1, 0), r, d)
                bias = jnp.where(ib == 0, bias_first, bias_rest) if has_prev else bias_first
                for g in range(2):
                    qg = _slab_pair(q_ref, g, rows).astype(BF16)
                    kg = _slab_pair(k_ref, g, rows)
                    vg = _slab_pair(v_ref, g, rows)
                    if has_prev:
                        kg = jnp.concatenate([_slab_pair(k_ref, g, prow), kg], axis=0)
                        vg = jnp.concatenate([_slab_pair(v_ref, g, prow), vg], axis=0)
                    kg, vg = kg.astype(BF16), vg.astype(BF16)
                    dog = _slab_pair(do_ref, g, rows)
                    outg = _slab_pair(out_ref, g, rows)
                    lg = _slab_pair(l_ref, g, rows)
                    qs = _stack_heads(qg, lane_head)
                    dos = _stack_heads(dog, lane_head)
                    delta = jnp.sum(dos * jnp.concatenate([outg] * 4, axis=0), axis=-1, keepdims=True)
                    lh = jnp.max(_stack_heads(lg, lane_head, NEG), axis=-1, keepdims=True)
                    s = _nt(qs, kg) * ATT_SCALE + bias
                    p = jnp.exp(s - lh)
                    dosb = dos.astype(BF16)
                    ds = (p * (_nt(dosb, vg) - delta) * ATT_SCALE).astype(BF16)
                    dq = _unstack_heads(_nn(ds, kg), lane_head)
                    dk = _tn(ds, qs)
                    dv = _tn(p.astype(BF16), dosb)
                    for jj in range(2):
                        j, sl = 2 * g + jj, slice(128 * jj, 128 * jj + 128)
                        if pi == 0:
                            dq_ref[j, rows, :] = dq[:, sl]
                            dk_ref[j, rows, :] = dk[:, sl]
                            dv_ref[j, rows, :] = dv[:, sl]
                            continue
                        dq_ref[j, rows, :] += dq[:, sl]
                        if has_prev:
                            dk_ref[j, prow, :] += dk[0:ATT_BLK, sl]
                            dv_ref[j, prow, :] += dv[0:ATT_BLK, sl]
                            dk_ref[j, rows, :] += dk[ATT_BLK:2 * ATT_BLK, sl]
                            dv_ref[j, rows, :] += dv[ATT_BLK:2 * ATT_BLK, sl]
                        else:
                            dk_ref[j, rows, :] += dk[:, sl]
                            dv_ref[j, rows, :] += dv[:, sl]
                return carry

            lax.fori_loop(0, S // ATT_BLK, block, 0, unroll=4)
        xc.middle()
        xc.finish()

    slab = jax.ShapeDtypeStruct((4, S, 128), F32)
    return _carry("att_bwd", body, exchange, exchange_args, (aq, ak, av, datt, att_out, lse), [VMEM] * 6, [VMEM] * 3,
                  [slab, slab, slab], after=after)


def _ret_bwd(qr, kr, rv, proj, o_raw, states, dret, tabs, exchange, exchange_args, after=None):
    C, G = RET_C, RET_PER_STEP
    steps = S // (C * G)
    dtab, a_tab, b_tab, lam, bd = tabs

    def body(q_ref, k_ref, v_ref, g_ref, o_ref, st_ref, dr_ref, dt_ref, a_ref, b_ref, lam_ref, bd_ref,
             dq_ref, dk_ref, dv_ref, dg_ref, dR, exch):
        @pl.when(pl.program_id(0) == 0)
        def _():
            exch.start()
            dR[...] = jnp.zeros_like(dR)

        lane_head = lax.broadcasted_iota(jnp.int32, (C, 256), 1) // 32
        col_head = lax.broadcasted_iota(jnp.int32, (C, 256), 1) // 64
        for s in reversed(range(G)):
            rows = slice(s * C, (s + 1) * C)
            q, k, v = q_ref[rows, :], k_ref[rows, :], v_ref[rows, :]
            dos = []
            for j in range(4):
                sl = slice(128 * j, 128 * j + 128)
                oj = o_ref[rows, sl]
                xc = oj - _seg_mean(oj)
                rs = lax.rsqrt(_seg_mean(xc * xc) + GN_EPS)
                rn = xc * rs
                gj = g_ref[rows, sl]
                sg = _sigmoid(gj)
                dret = dr_ref[rows, sl]
                dg_ref[rows, sl] = dret * rn * (sg * (1.0 + gj * (1.0 - sg)))
                drn = dret * (gj * sg)
                dos.append(rs * (drn - _seg_mean(drn) - rn * _seg_mean(drn * rn)))
            do = [jnp.concatenate(dos[0:2], axis=1), jnp.concatenate(dos[2:4], axis=1)]
            do8 = jnp.concatenate(do, axis=1).astype(BF16)
            drb = dR[...].astype(BF16)
            rb = st_ref[s]
            dq = _nt(do8, rb) * a_ref[...]
            dk = _nt(v, drb) * b_ref[...]
            kb = (k.astype(F32) * b_ref[...]).astype(BF16)
            dvall = _nn(kb, drb)
            qs = _stack_heads(q, lane_head, n=8)
            dec = dt_ref[...]
            p = (_nt(qs, k) * dec).astype(BF16)
            dos = [_stack_heads(do[g], col_head).astype(BF16) for g in range(2)]
            dp = jnp.concatenate([_nt(dos[g], v[:, 256 * g:256 * g + 256]) for g in range(2)], axis=0)
            ds = (dp * dec).astype(BF16)
            dq = dq + _unstack_heads(_nn(ds, k), lane_head, n=8)
            dk = dk + _tn(ds, qs)
            dv = [dvall[:, 256 * g:256 * g + 256] + _tn(p[4 * C * g:4 * C * (g + 1)], dos[g]) for g in range(2)]
            qa = (q.astype(F32) * a_ref[...]).astype(BF16)
            dR[...] = dR[...] * lam_ref[...] + _tn(qa, do8) * bd_ref[...]
            dq_ref[rows, :] = dq
            dk_ref[rows, :] = dk
            dv_ref[rows, 0:256] = dv[0]
            dv_ref[rows, 256:512] = dv[1]

        @pl.when(pl.program_id(0) == steps - 1)
        def _():
            exch.middle()
            exch.finish()

    rev = lambda w: pl.BlockSpec((C * G, w), lambda n: (steps - 1 - n, 0))
    full = lambda a: pl.BlockSpec(a.shape, lambda n: (0,) * a.ndim)
    return _carry(
        "ret_bwd", body, exchange, exchange_args, (qr, kr, rv, proj, o_raw, states, dret, dtab, a_tab, b_tab, lam, bd),
        [rev(256), rev(256), rev(512), rev(512), rev(512),
         pl.BlockSpec((G, 256, 512), lambda n: (steps - 1 - n, 0, 0)), rev(512),
         full(dtab), full(a_tab), full(b_tab), full(lam), full(bd)],
        [rev(256), rev(256), rev(512), rev(512)],
        [jax.ShapeDtypeStruct((S, 256), F32), jax.ShapeDtypeStruct((S, 256), F32),
         jax.ShapeDtypeStruct((S, 512), F32), jax.ShapeDtypeStruct((S, 512), F32)],
        scratch_shapes=[pltpu.VMEM((256, 512), F32)], grid=(steps,), semantics=("arbitrary",), after=after)


def _rot_bwd(cos, sin, spread, dqr, dkr, drv, drg, dq_att, dk_att, dv_att):
    tm = 256

    def body(cos_ref, sin_ref, e_ref, dqr_ref, dkr_ref, drv_ref, drg_ref, dqa_ref, dka_ref, dva_ref, dp_ref):
        cr, ca, sr, sa = _rot_tables(cos_ref, sin_ref, e_ref)
        lo_r, lo_a = _rot_halves(tm)

        def unrot_r(g):
            gs = g * sr
            return g * cr + pltpu.roll(jnp.where(lo_r, -gs, 0.0), 16, 1) + pltpu.roll(jnp.where(lo_r, 0.0, gs), 240, 1)

        def unrot_a(g):
            gs = g * sa
            return g * ca + pltpu.roll(jnp.where(lo_a, -gs, 0.0), 8, 1) + pltpu.roll(jnp.where(lo_a, 0.0, gs), 504, 1)

        def wide(ref):
            return jnp.concatenate([ref[j] for j in range(4)], axis=1)

        dp_ref[:, 0:256] = unrot_r(dqr_ref[...]).astype(BF16)
        dp_ref[:, 256:512] = unrot_r(dkr_ref[...] * RET_SCALE).astype(BF16)
        dp_ref[:, 512:1024] = drv_ref[...].astype(BF16)
        dp_ref[:, 1024:1536] = drg_ref[...].astype(BF16)
        dp_ref[:, 1536:2048] = unrot_a(wide(dqa_ref)).astype(BF16)
        dp_ref[:, 2048:2560] = unrot_a(wide(dka_ref)).astype(BF16)
        dp_ref[:, 2560:3072] = wide(dva_ref).astype(BF16)

    row = lambda w: pl.BlockSpec((tm, w), lambda i: (i, 0))
    slab = pl.BlockSpec((4, tm, 128), lambda i: (0, i, 0))
    return pl.pallas_call(
        body, grid=(S // tm,), name="rot_bwd",
        in_specs=[row(128), row(128), pl.BlockSpec((128, 768), lambda i: (0, 0)),
                  row(256), row(256), row(512), row(512), slab, slab, slab],
        out_specs=row(PW), out_shape=jax.ShapeDtypeStruct((S, PW), BF16),
        compiler_params=_params("parallel"),
    )(cos, sin, spread, dqr, dkr, drv, drg, dq_att, dk_att, dv_att)


def _win_bwd_w(h1, dproj, exchange, exchange_args):
    def body(h_ref, dp_ref, dw_ref, xc):
        k = pl.program_id(0)

        @pl.when(k == 0)
        def _():
            xc.start()

        dw_ref[...] = _tn(h_ref[...], dp_ref[...]).astype(BF16)

        @pl.when(k == N_CHIP - 1)
        def _():
            xc.middle()
            xc.finish()

    (dw,), out = _carry(
        "win_bwd_w", body, exchange, exchange_args, (h1, dproj),
        [pl.BlockSpec((S, D), lambda k: (0, 0)), pl.BlockSpec((S, WIN_C), lambda k: (0, k))],
        [pl.BlockSpec((None, D, WIN_C), lambda k: (k, 0, 0))],
        [jax.ShapeDtypeStruct((N_CHIP, D, WIN_C), BF16)], grid=(N_CHIP,), semantics=("arbitrary",))
    return dw, out


def _in_bwd(dproj, win_g, x, dx2, g1, after):
    tm = 512

    def body(dp_ref, w_ref, x_ref, dx2_ref, g_ref, dx_ref, dg_ref, _):
        @pl.when(pl.program_id(0) == 0)
        def _():
            dg_ref[...] = jnp.zeros_like(dg_ref)

        dh = _nt(dp_ref[:, 0:WIN_C], w_ref[0])
        for k in range(1, N_CHIP):
            dh = dh + _nt(dp_ref[:, k * WIN_C:(k + 1) * WIN_C], w_ref[k])
        xv = x_ref[...]
        r = _rstd(xv)
        xn = xv * r
        dg_ref[...] = dg_ref[...] + jnp.sum(dh * xn, axis=0, keepdims=True)
        t = dh * g_ref[...]
        dx_ref[...] = dx2_ref[...] + r * (t - xn * jnp.mean(t * xn, axis=-1, keepdims=True))

    row = lambda w: pl.BlockSpec((tm, w), lambda i: (i, 0))
    vec = pl.BlockSpec((1, D), lambda i: (0, 0))
    return _carry("in_bwd", body, _NoExchange(), (), (dproj, win_g, x, dx2, g1),
                  [row(PW), pl.BlockSpec((N_CHIP, D, WIN_C), lambda i: (0, 0, 0)), row(D), row(D), vec],
                  [row(D), vec], [jax.ShapeDtypeStruct((S, D), F32), jax.ShapeDtypeStruct((1, D), F32)],
                  grid=(S // tm,), semantics=("arbitrary",), after=after)[0]


ANY = pl.BlockSpec(memory_space=pl.ANY)
VMEM = pl.BlockSpec(memory_space=pltpu.VMEM)
FLIPS = ((1, 0), (0, 1), (1, 1))


def _place():
    x, y, c = lax.axis_index("x"), lax.axis_index("y"), lax.axis_index("c")
    chips = [((1 - x) if fx else x, (1 - y) if fy else y) for fx, fy in FLIPS]
    return x, y, c, 2 * x + y, chips


def _remote(src, dst, send_sem, recv_sem, device):
    return pltpu.make_async_remote_copy(src_ref=src, dst_ref=dst, send_sem=send_sem, recv_sem=recv_sem,
                                        device_id=device, device_id_type=MESH)


class _Exchange:
    aliases = {}

    def middle(self, ins, outs, sems):
        pass


class _GatherShards(_Exchange):
    def __init__(self, shards):
        n = self.n = len(shards)
        self.n_in = self.n_out = n
        self.out_shape = [jax.ShapeDtypeStruct((N_CHIP,) + s.shape, s.dtype) for s in shards]
        dma = pltpu.SemaphoreType.DMA
        self.scratch = [dma((3 * n,)), dma((3 * n,)), dma((3 * n,)), dma((3 * n,)), dma((n,)), dma((n,))]

    def _ici(self, ins, outs, sems, a, j, chip):
        x, y, c, me, chips = _place()
        half = ins[a].shape[0] // 2
        return _remote(ins[a].at[pl.ds(c * half, half), :], outs[a].at[me, pl.ds(c * half, half), :],
                       sems[0].at[3 * a + j], sems[1].at[3 * a + j], (*chip, c))

    def _fwd(self, outs, sems, a, j, chip, half_of):
        x, y, c, me, chips = _place()
        half = outs[a].shape[1] // 2
        blk = outs[a].at[2 * chip[0] + chip[1], pl.ds(half_of * half, half), :]
        return _remote(blk, blk, sems[2].at[3 * a + j], sems[3].at[3 * a + j], (x, y, 1 - c))

    def _own(self, ins, outs, sems, a):
        return _own_shard_to_sibling(ins[a], outs[a], sems[4].at[a], sems[5].at[a])

    def start(self, ins, outs, sems):
        chips = _place()[4]
        for a in range(self.n):
            for j, chip in enumerate(chips):
                self._ici(ins, outs, sems, a, j, chip).start()
        for a in range(self.n):
            self._own(ins, outs, sems, a).start()

    def middle(self, ins, outs, sems):
        x, y, c, me, chips = _place()
        for a in range(self.n):
            for j, chip in enumerate(chips):
                half = outs[a].shape[1] // 2
                blk = outs[a].at[2 * chip[0] + chip[1], pl.ds(c * half, half), :]
                _remote(blk, blk, sems[0].at[3 * a + j], sems[1].at[3 * a + j], (x, y, c)).wait_recv()
                self._fwd(outs, sems, a, j, chip, c).start()

    def finish(self, ins, outs, sems):
        x, y, c, me, chips = _place()
        for a in range(self.n):
            for j, chip in enumerate(chips):
                self._fwd(outs, sems, a, j, chip, 1 - c).wait_recv()
        for a in range(self.n):
            for j, chip in enumerate(chips):
                self._ici(ins, outs, sems, a, j, chip).wait_send()
                self._fwd(outs, sems, a, j, chip, c).wait_send()
            self._own(ins, outs, sems, a).wait()


def _own_shard_to_sibling(shard_ref, gathered_ref, send_sem, recv_sem):
    x, y, c, me, chips = _place()
    return _remote(shard_ref, gathered_ref.at[me], send_sem, recv_sem, (x, y, 1 - c))


class _NoExchange(_Exchange):
    n_in = n_out = 0
    out_shape = ()
    scratch = ()

    def start(self, ins, outs, sems):
        pass

    def finish(self, ins, outs, sems):
        pass


class _ForwardGathered(_Exchange):
    def __init__(self, shards, own=True, forward=True):
        self.own, self.forward = own, forward
        n = self.n = len(shards)
        self.n_in, self.n_out = 2 * n, n
        self.out_shape = [jax.ShapeDtypeStruct((N_CHIP,) + s.shape, s.dtype) for s in shards]
        dma = pltpu.SemaphoreType.DMA
        self.scratch = [dma((3 * n,)), dma((3 * n,)), dma((n,)), dma((n,))]
        self.aliases = {n + a: a for a in range(n)}

    def _fwd(self, outs, sems, a, j, chip, half_of):
        x, y, c, me, chips = _place()
        half = outs[a].shape[1] // 2
        blk = outs[a].at[2 * chip[0] + chip[1], pl.ds(half_of * half, half), :]
        return _remote(blk, blk, sems[0].at[3 * a + j], sems[1].at[3 * a + j], (x, y, 1 - c))

    def _own(self, ins, outs, sems, a):
        return _own_shard_to_sibling(ins[a], outs[a], sems[2].at[a], sems[3].at[a])

    def start(self, ins, outs, sems):
        x, y, c, me, chips = _place()
        for a in range(self.n):
            for j, chip in enumerate(chips if self.forward else ()):
                self._fwd(outs, sems, a, j, chip, c).start()
        for a in range(self.n if self.own else 0):
            self._own(ins, outs, sems, a).start()

    def finish(self, ins, outs, sems):
        x, y, c, me, chips = _place()
        for a in range(self.n):
            for j, chip in enumerate(chips if self.forward else ()):
                self._fwd(outs, sems, a, j, chip, 1 - c).wait_recv()
        for a in range(self.n):
            for j, chip in enumerate(chips if self.forward else ()):
                self._fwd(outs, sems, a, j, chip, c).wait_send()
            if self.own:
                self._own(ins, outs, sems, a).wait()


HBM = pl.BlockSpec(memory_space=pltpu.HBM)
SEMS = pl.BlockSpec(memory_space=pltpu.SEMAPHORE)
DATAFLOW = pltpu.SideEffectType.DATAFLOW_SIDE_EFFECTING


class _OverIci:
    def __init__(self, name, sources, lands):
        self.name, self.n = name, len(sources)
        hbm = lambda t: pltpu.with_memory_space_constraint(t, pltpu.HBM)
        self.arrays = [hbm(t) for t in sources] + [hbm(t) for t in lands]

    def sent(self, src, land, a, chip):
        raise NotImplementedError

    def landed(self, land, a, chip):
        raise NotImplementedError

    def _copy(self, arr, sems, a, j, receiving):
        x, y, c, me, chips = _place()
        src, dst = self.sent(arr[a], arr[self.n + a], a, chips[j])
        if receiving:
            dst = self.landed(arr[self.n + a], a, chips[j])
        return _remote(src, dst, sems[0].at[3 * a + j], sems[1].at[3 * a + j], (*chips[j], c))

    def start(self, after):
        m = len(self.arrays)

        def body(*refs):
            arr, sems, token = refs[:m], refs[m + 1:m + 3], refs[-1]
            for a in range(self.n):
                for j in range(3):
                    self._copy(arr, sems, a, j, False).start()
            token[...] = jnp.zeros_like(token)

        dma = pltpu.SemaphoreType.DMA
        outs = pl.pallas_call(
            body, name=self.name + "_start",
            out_shape=[dma((3 * self.n,)), dma((3 * self.n,))] + [pltpu.HBM(t.shape, t.dtype) for t in self.arrays]
                      + [jax.ShapeDtypeStruct((8, 128), F32)],
            in_specs=[HBM] * m + [ANY], out_specs=[SEMS, SEMS] + [HBM] * m + [VMEM],
            input_output_aliases={i: 2 + i for i in range(m)},
            compiler_params=pltpu.CompilerParams(has_side_effects=DATAFLOW),
        )(*self.arrays, after)
        self.sems, self.arrays = outs[0:2], list(outs[2:2 + m])
        return outs[-1]

    def wait(self, after):
        m = len(self.arrays)

        def body(*refs):
            arr, sems = refs[:m], refs[m:m + 2]
            for a in range(self.n):
                for j in range(3):
                    self._copy(arr, sems, a, j, False).wait_send()
                    self._copy(arr, sems, a, j, True).wait_recv()

        outs = pl.pallas_call(
            body, name=self.name + "_wait",
            out_shape=[pltpu.HBM(t.shape, t.dtype) for t in self.arrays],
            in_specs=[HBM] * m + [SEMS, SEMS, ANY], out_specs=[HBM] * m,
            input_output_aliases={i: i for i in range(m)},
            compiler_params=pltpu.CompilerParams(has_side_effects=DATAFLOW),
        )(*self.arrays, *self.sems, after)
        return list(outs[:self.n]), list(outs[self.n:])


class _GatherOverIci(_OverIci):
    def __init__(self, name, shards):
        super().__init__(name, shards, [lax.empty((N_CHIP,) + s.shape, s.dtype) for s in shards])

    @staticmethod
    def _half(ref):
        c = lax.axis_index("c")
        half = ref.shape[-2] // 2
        return pl.ds(c * half, half)

    def sent(self, src, land, a, chip):
        return src.at[self._half(src), :], land.at[_place()[3], self._half(src), :]

    def landed(self, land, a, chip):
        return land.at[2 * chip[0] + chip[1], self._half(land), :]


class _SumOverIci(_OverIci):
    def __init__(self, name, pre):
        super().__init__(name, pre, [lax.empty(p.shape, p.dtype) for p in pre])

    def sent(self, src, land, a, chip):
        return src.at[2 * chip[0] + chip[1]], land.at[_place()[3]]

    def landed(self, land, a, chip):
        return land.at[2 * chip[0] + chip[1]]


class _HalvesToSibling(_Exchange):
    def __init__(self, grads):
        n = self.n = len(grads)
        self.n_in = self.n_out = n
        self.out_shape = [jax.ShapeDtypeStruct((N_CHIP, g.shape[1] // 2, g.shape[2]), g.dtype) for g in grads]
        self.scratch = [pltpu.SemaphoreType.DMA((n,)), pltpu.SemaphoreType.DMA((n,))]

    def _copy(self, ins, outs, sems, a):
        x, y, c, me, chips = _place()
        half = ins[a].shape[1] // 2
        return _remote(ins[a].at[:, pl.ds((1 - c) * half, half), :], outs[a], sems[0].at[a], sems[1].at[a], (x, y, 1 - c))

    def start(self, ins, outs, sems):
        for a in range(self.n):
            self._copy(ins, outs, sems, a).start()

    def finish(self, ins, outs, sems):
        for a in range(self.n):
            self._copy(ins, outs, sems, a).wait_recv()
        for a in range(self.n):
            self._copy(ins, outs, sems, a).wait_send()


class _ShareHalves(_Exchange):
    def __init__(self, fulls):
        n = self.n = len(fulls)
        self.n_in = self.n_out = n
        self.out_shape = [jax.ShapeDtypeStruct(f.shape, f.dtype) for f in fulls]
        self.scratch = [pltpu.SemaphoreType.DMA((n,)), pltpu.SemaphoreType.DMA((n,))]
        self.aliases = {a: a for a in range(n)}

    def _copy(self, outs, sems, a, half_of):
        x, y, c, me, chips = _place()
        half = outs[a].shape[0] // 2
        rows = outs[a].at[pl.ds(half_of * half, half), :]
        return _remote(rows, rows, sems[0].at[a], sems[1].at[a], (x, y, 1 - c))

    def start(self, ins, outs, sems):
        c = _place()[2]
        for a in range(self.n):
            self._copy(outs, sems, a, c).start()

    def finish(self, ins, outs, sems):
        c = _place()[2]
        for a in range(self.n):
            self._copy(outs, sems, a, 1 - c).wait_recv()
        for a in range(self.n):
            self._copy(outs, sems, a, c).wait_send()


class _GatherBlocks(_Exchange):
    def __init__(self, block):
        self.n_in = self.n_out = 1
        self.out_shape = [jax.ShapeDtypeStruct((8,) + block.shape, block.dtype)]
        dma = pltpu.SemaphoreType.DMA
        self.scratch = [dma((7,)), dma((7,)), dma]

    @staticmethod
    def _peer(f):
        x, y, c, me, chips = _place()
        return ((1 - x) if f & 4 else x, (1 - y) if f & 2 else y, (1 - c) if f & 1 else c)

    def start(self, ins, outs, sems):
        x, y, c, me, chips = _place()
        for f in range(1, 8):
            _remote(ins[0], outs[0].at[2 * me + c], sems[0].at[f - 1], sems[1].at[f - 1], self._peer(f)).start()
        pltpu.make_async_copy(ins[0], outs[0].at[2 * me + c], sems[2]).start()

    def finish(self, ins, outs, sems):
        x, y, c, me, chips = _place()
        for f in range(1, 8):
            px, py, pc = self._peer(f)
            blk = outs[0].at[4 * px + 2 * py + pc]
            _remote(blk, blk, sems[0].at[f - 1], sems[1].at[f - 1], (x, y, c)).wait_recv()
        for f in range(1, 8):
            _remote(ins[0], outs[0].at[2 * me + c], sems[0].at[f - 1], sems[1].at[f - 1], self._peer(f)).wait_send()
        pltpu.make_async_copy(ins[0], outs[0].at[2 * me + c], sems[2]).wait()


class _Both(_Exchange):
    def __init__(self, first, second):
        self.parts = (first, second)
        self.n_in, self.n_out = first.n_in + second.n_in, first.n_out + second.n_out
        self.out_shape = first.out_shape + second.out_shape
        self.scratch = first.scratch + second.scratch
        self.aliases = dict(first.aliases)
        self.aliases.update({first.n_in + i: first.n_out + o for i, o in second.aliases.items()})

    def _split(self, ins, outs, sems):
        a, b = self.parts
        return ((a, ins[:a.n_in], outs[:a.n_out], sems[:len(a.scratch)]),
                (b, ins[a.n_in:], outs[a.n_out:], sems[len(a.scratch):]))

    def start(self, ins, outs, sems):
        for ex, i, o, s in self._split(ins, outs, sems):
            ex.start(i, o, s)

    def middle(self, ins, outs, sems):
        for ex, i, o, s in self._split(ins, outs, sems):
            ex.middle(i, o, s)

    def finish(self, ins, outs, sems):
        for ex, i, o, s in self._split(ins, outs, sems):
            ex.finish(i, o, s)


class _Bound:
    def __init__(self, ex, ins, outs, sems):
        self.start = lambda: ex.start(ins, outs, sems)
        self.middle = lambda: ex.middle(ins, outs, sems)
        self.finish = lambda: ex.finish(ins, outs, sems)


def _carry(name, body, ex, ex_args, args, in_specs, out_specs, out_shape, scratch_shapes=(), grid=None, semantics=(),
           after=None):
    n_a, n_o, n_s = len(args), len(out_shape), len(scratch_shapes)
    behind = [] if after is None else [after]

    def full_body(*refs):
        p = 0
        groups = []
        for size in (n_a, ex.n_in, len(behind), n_o, ex.n_out, n_s, len(ex.scratch)):
            groups.append(refs[p:p + size])
            p += size
        a, ei, _, o, eo, s, es = groups
        body(*a, *o, *s, _Bound(ex, ei, eo, es))

    kwargs = {} if grid is None else {"grid": grid}
    outs = pl.pallas_call(
        full_body, name=name,
        in_specs=list(in_specs) + [ANY] * (ex.n_in + len(behind)), out_specs=list(out_specs) + [ANY] * ex.n_out,
        out_shape=list(out_shape) + list(ex.out_shape), scratch_shapes=list(scratch_shapes) + list(ex.scratch),
        input_output_aliases={n_a + i: n_o + o for i, o in ex.aliases.items()},
        compiler_params=_params(*semantics) if semantics else pltpu.CompilerParams(vmem_limit_bytes=VMEM_LIMIT),
        **kwargs,
    )(*args, *ex_args, *behind)
    return outs[:n_o], outs[n_o:]


def _prepare_carrying(name, x, g1, pos, ifc, arrays, ex, ex_args):
    n = len(arrays)
    r, cc = arrays[0].shape
    steps = 4
    tr, tm = r // steps, S // steps

    def body(x_ref, g_ref, pos_ref, ifc_ref, *refs):
        src, h_ref, cos_ref, sin_ref, dst, xc = refs[:n], refs[n], refs[n + 1], refs[n + 2], refs[n + 3:2 * n + 3], refs[-1]

        @pl.when(pl.program_id(0) == 0)
        def _():
            xc.start()

        xv = x_ref[...]
        h_ref[...] = (xv * _rstd(xv) * g_ref[...]).astype(BF16)
        ang = pos_ref[...].astype(F32) * ifc_ref[...]
        cos_ref[...] = jnp.cos(ang)
        sin_ref[...] = jnp.sin(ang)
        for a in range(n):
            dst[a][...] = src[a][...].astype(BF16)

        @pl.when(pl.program_id(0) == steps - 1)
        def _():
            xc.middle()
            xc.finish()

    row = lambda w: pl.BlockSpec((tm, w), lambda i: (i, 0))
    const = lambda w: pl.BlockSpec((1, w), lambda i: (0, 0))
    blk = pl.BlockSpec((tr, cc), lambda i: (i, 0))
    return _carry(name, body, ex, ex_args, (x, g1, pos, ifc, *arrays),
                  [row(D), const(D), row(1), const(128)] + [blk] * n,
                  [row(D), row(128), row(128)] + [blk] * n,
                  [jax.ShapeDtypeStruct((S, D), BF16)] + [jax.ShapeDtypeStruct((S, 128), F32)] * 2
                  + [jax.ShapeDtypeStruct((r, cc), BF16)] * n,
                  grid=(steps,), semantics=("arbitrary",))


def _exchange_alone(name, ex, ex_args):
    def body(xc):
        xc.start()
        xc.middle()
        xc.finish()

    return _carry(name, body, ex, ex_args, (), (), (), ())[1]


def _core_index():
    return lax.axis_index("c").astype(jnp.int32).reshape(1)


def _pair_sum(gs, gots):
    n = len(gs)

    def body(c_ref, *refs):
        for a in range(n):
            refs[2 * n + a][...] = (refs[a][...].astype(F32) + refs[n + a][...].astype(F32)).astype(BF16)

    mine = [pl.BlockSpec((None, g.shape[1] // 2, g.shape[2]), lambda k, c_ref: (k, c_ref[0], 0)) for g in gs]
    blk = [pl.BlockSpec((None, g.shape[1] // 2, g.shape[2]), lambda k, c_ref: (k, 0, 0)) for g in gs]
    return pl.pallas_call(
        body, name=f"pair_sum_{gs[0].shape[1]}x{gs[0].shape[2]}",
        grid_spec=pltpu.PrefetchScalarGridSpec(
            num_scalar_prefetch=1, grid=(N_CHIP,), in_specs=mine + blk, out_specs=blk),
        out_shape=[jax.ShapeDtypeStruct((N_CHIP, g.shape[1] // 2, g.shape[2]), BF16) for g in gs],
        compiler_params=_params("parallel"),
    )(_core_index(), *gs, *gots)


def _chip_sum(pre, parts):
    n = len(parts)
    me = 2 * lax.axis_index("x") + lax.axis_index("y")
    others = [k + (k >= me).astype(jnp.int32) for k in range(3)]
    where = jnp.stack([lax.axis_index("c"), me, *others]).astype(jnp.int32)

    def body(w_ref, *refs):
        for a in range(n):
            own, p1, p2, p3 = refs[4 * a:4 * a + 4]
            refs[4 * n + a][...] = ((own[...].astype(F32) + p1[...].astype(F32)) + p2[...].astype(F32)) + p3[...].astype(F32)

    in_specs, out_specs, operands = [], [], []
    for a in range(n):
        _, half, cc = parts[a].shape
        tr = half // 2
        in_specs += [pl.BlockSpec((None, tr, cc), lambda i, w_ref, s=s: (w_ref[s], i, 0)) for s in (1, 2, 3, 4)]
        out_specs.append(pl.BlockSpec((tr, cc), lambda i, w_ref: (2 * w_ref[0] + i, 0)))
        operands += [pre[a], parts[a], parts[a], parts[a]]
    return pl.pallas_call(
        body, name=f"chip_sum_{parts[0].shape[1]}x{parts[0].shape[2]}",
        grid_spec=pltpu.PrefetchScalarGridSpec(num_scalar_prefetch=1, grid=(2,), in_specs=in_specs, out_specs=out_specs),
        out_shape=[jax.ShapeDtypeStruct((2 * p.shape[1], p.shape[2]), F32) for p in parts],
        compiler_params=_params("parallel"),
    )(where, *operands)


def _adamw_math(w, g, m, v):
    m = ADAM_B1 * m + (1.0 - ADAM_B1) * g
    v = ADAM_B2 * v + (1.0 - ADAM_B2) * (g * g)
    m_hat = m / (1.0 - ADAM_B1 ** ADAM_STEP)
    v_hat = v / (1.0 - ADAM_B2 ** ADAM_STEP)
    delta = -ADAM_LR * (m_hat / (jnp.sqrt(v_hat) + ADAM_EPS) + ADAM_WD * w)
    return delta, m, v


def _adamw(w, g, m, v, after=None):
    r, cc = w.shape
    tr = r // 4

    def body(w_ref, g_ref, m_ref, v_ref, go_ref, d_ref, nm_ref, nv_ref, _):
        g = g_ref[...]
        go_ref[...] = g
        d_ref[...], nm_ref[...], nv_ref[...] = _adamw_math(w_ref[...], g, m_ref[...], v_ref[...])

    blk = pl.BlockSpec((tr, cc), lambda i: (i, 0))
    return _carry(f"adamw_{r}x{cc}", body, _NoExchange(), (), (w, g, m, v), [blk] * 4, [blk] * 4,
                  [jax.ShapeDtypeStruct((r, cc), F32)] * 4, grid=(4,), semantics=("parallel",), after=after)[0]


def _pack8(rows):
    def body(*refs):
        out_ref = refs[-1]
        out_ref[...] = jnp.zeros_like(out_ref)
        for i, r in enumerate(refs[:-1]):
            out_ref[i:i + 1, :] = r[...]

    return pl.pallas_call(body, name="pack8", out_shape=jax.ShapeDtypeStruct((8, D), F32))(*rows)


def _adamw_gains(gall, ws, ms, vs):
    def body(ga_ref, *refs):
        w, m, v = refs[0:4], refs[4:8], refs[8:12]
        outs, loss_ref, total = refs[12:28], refs[28], refs[29]
        g = ga_ref[0]
        for dev in range(1, 8):
            g = g + ga_ref[dev]
        total[...] = g
        for i in range(4):
            gi = total[i:i + 1, :]
            outs[i][...] = gi
            outs[4 + i][...], outs[8 + i][...], outs[12 + i][...] = _adamw_math(w[i][...], gi, m[i][...], v[i][...])
        loss_ref[...] = total[4:5, 0:128] * (0.5 / D)

    outs = pl.pallas_call(
        body, name="adamw_gains",
        out_shape=[jax.ShapeDtypeStruct((1, D), F32)] * 16 + [jax.ShapeDtypeStruct((1, 128), F32)],
        scratch_shapes=[pltpu.VMEM((8, D), F32)],
    )(gall, *ws, *ms, *vs)
    return outs[0:4], outs[4:8], outs[8:12], outs[12:16], outs[16]


def kernel(x, positions, w_in, w_out, g_pre_mix, g_post_mix, g_pre_ffn, g_post_ffn, w_gate, w_up, w_down, loss_target, m_w_in, m_w_out, m_g_pre_mix, m_g_post_mix, m_g_pre_ffn, m_g_post_ffn, m_w_gate, m_w_up, m_w_down, v_w_in, v_w_out, v_g_pre_mix, v_g_post_mix, v_g_pre_ffn, v_g_post_ffn, v_w_gate, v_w_up, v_w_down):
    tr = lambda t: jnp.swapaxes(t, 1, 2)[0]
    shards = [w_in[0], w_out[0], tr(w_gate), tr(w_up), w_down[0]]
    moms = [m_w_in[0], m_w_out[0], tr(m_w_gate), tr(m_w_up), m_w_down[0]]
    vels = [v_w_in[0], v_w_out[0], tr(v_w_gate), tr(v_w_up), v_w_down[0]]
    xs, pos, tgt = x[0], positions.reshape(S, 1), loss_target[0]
    g1, g2, g3, g4 = g_pre_mix, g_post_mix, g_pre_ffn, g_post_ffn
    tabs = tuple(jnp.asarray(t) for t in _retention_tables())
    ifc, spread = _rotary_tables()
    ifc, spread = jnp.asarray(ifc), jnp.asarray(spread, dtype=BF16)
    bf = [s.astype(BF16) for s in shards[:2]]

    (h1, cos, sin, *ffn_bf), (win_g,) = _prepare_carrying(
        "gather_in", xs, g1, pos, ifc, shards[2:], _GatherShards(bf[:1]), bf[:1])
    bf += list(ffn_bf)
    wout_gather = _GatherOverIci("wout_gather", bf[1:2])
    token = wout_gather.start(win_g)
    ffn_gather = _GatherOverIci("ffn_gather", bf[2:])
    token = ffn_gather.start(token)
    qr, kr, rv, rg, aq, ak, av = _proj_fwd(h1, win_g, cos, sin, spread, token)
    wout_sh, wout_land = wout_gather.wait(qr)
    n_ffn = len(bf[2:])
    (att_out, lse, cat_a), (wout_g, *ffn_gather.arrays[n_ffn:]) = _att_fwd(
        aq, ak, av, _Both(_ForwardGathered(bf[1:2]), _ForwardGathered(bf[2:], forward=False)),
        [*wout_sh, *wout_land, *ffn_gather.arrays])
    wout_g = wout_g.reshape(D, D)
    (o_raw, cat_r, states), _ = _ret_fwd(qr, kr, rv, rg, tabs, _NoExchange(), (), cat_a)
    ffn_sh, ffn_lands = ffn_gather.wait(cat_r)
    (mix, x2, h3), (wg_g, wu_g, wd_g) = _mix_fwd(cat_r, cat_a, wout_g, xs, g2, g3,
                                                _ForwardGathered(bf[2:], own=False), [*ffn_sh, *ffn_lands])
    gt, up, a, sq, dy, df, dg4 = _ffn_fwd(h3, wg_g, wu_g, wd_g, x2, tgt, g4)

    dgt, dup, dx2, dmix, dg3, dg2 = _ffn_bwd_act(df, gt, up, wg_g, wu_g, wd_g, dy, x2, mix, g2, g3)
    ffn_grads = list(_ffn_bwd_w(a, df, h3, dgt, dup))
    (dret, datt, dwout), got = _mix_bwd(dmix, cat_r, cat_a, wout_g, _HalvesToSibling(ffn_grads), ffn_grads)
    ffn_sum = _SumOverIci("ffn_sum", _pair_sum(ffn_grads, got))
    token = ffn_sum.start(datt)
    (dq_att, dk_att, dv_att), _ = _att_bwd(aq, ak, av, datt, att_out, lse, _NoExchange(), (), token)
    (dqr, dkr, drv, drg), _ = _ret_bwd(qr, kr, rv, rg, o_raw, states, dret, tabs, _NoExchange(), (), token)
    dproj = _rot_bwd(cos, sin, spread, dqr, dkr, drv, drg, dq_att, dk_att, dv_att)
    sums = _chip_sum(*ffn_sum.wait(dproj))
    dwin, ffn_full = _win_bwd_w(h1, dproj, _ShareHalves(sums), sums)
    in_grads = [dwin, dwout.reshape(N_CHIP, WOUT_R, D)]

    got = _exchange_alone("halves_to_sibling", _HalvesToSibling(in_grads), in_grads)
    in_sum = _SumOverIci("in_sum", _pair_sum(in_grads, got))
    token = in_sum.start(dproj)
    dx, dg1 = _in_bwd(dproj, win_g, xs, dx2, g1, token)
    ffn_upd = [_adamw(shards[2 + i], ffn_full[o], moms[2 + i], vels[2 + i], token)
               for i, o in enumerate((1, 2, 0))]
    pre, parts = in_sum.wait(ffn_upd[2][0])
    sums = _chip_sum(pre, parts)
    gblock = _pack8([dg1, dg2, dg3, dg4, sq])
    *in_full, gall = _exchange_alone("share_rest", _Both(_ShareHalves(sums), _GatherBlocks(gblock)), [*sums, gblock])
    upd = [_adamw(w, g, m, v) for w, g, m, v in zip(shards[:2], in_full, moms[:2], vels[:2])] + ffn_upd
    gg, gd, gm, gv, loss_row = _adamw_gains(gall, [g1, g2, g3, g4],
                                            [m_g_pre_mix, m_g_post_mix, m_g_pre_ffn, m_g_post_ffn],
                                            [v_g_pre_mix, v_g_post_mix, v_g_pre_ffn, v_g_post_ffn])

    def order(mats, vecs):
        back = lambda t: jnp.swapaxes(t[None], 1, 2)
        return [mats[0][None], mats[1][None], *vecs, back(mats[2]), back(mats[3]), mats[4][None]]

    return (loss_row[0, 0], dx[None],
            *order([u[0] for u in upd], gg),
            *order([u[1] for u in upd], gd),
            *order([u[2] for u in upd], gm),
            *order([u[3] for u in upd], gv))
```

```python
import numpy as np
import jax
import jax.numpy as jnp
from jax import lax
from jax.experimental import pallas as pl
from jax.experimental.pallas import tpu as pltpu

F32, BF16 = jnp.float32, jnp.bfloat16
MESH = pl.DeviceIdType.MESH

S = 2048
D = 1024
PW = 3072
N_CHIP = 4
WIN_C = PW // N_CHIP
DFF = 2816
FF_C = DFF // N_CHIP
WOUT_R = D // N_CHIP
RMS_EPS = 1e-6
GN_EPS = 1e-5
RET_C = 128
RET_PER_STEP = 4
RET_SCALE = 32 ** -0.5
ATT_BLK = 128
ATT_SCALE = 64 ** -0.5
PATTERN_DILATIONS = (16, 1, 4)
NEG = -1e30
VMEM_LIMIT = 56 * 1024 * 1024

ADAM_LR, ADAM_B1, ADAM_B2, ADAM_EPS, ADAM_WD, ADAM_STEP = 0.001, 0.9, 0.999, 1e-08, 0.01, 10


def _params(*sem):
    return pltpu.CompilerParams(dimension_semantics=sem, vmem_limit_bytes=VMEM_LIMIT)


def _nt(a, b):
    return lax.dot_general(a, b, (((1,), (1,)), ((), ())), preferred_element_type=F32)


def _tn(a, b):
    return lax.dot_general(a, b, (((0,), (0,)), ((), ())), preferred_element_type=F32)


def _nn(a, b):
    return jnp.dot(a, b, preferred_element_type=F32)


def _rstd(v):
    return lax.rsqrt(jnp.mean(v * v, axis=-1, keepdims=True) + RMS_EPS)


def _sigmoid(v):
    return 1.0 / (1.0 + jnp.exp(-v))


def _rows(i, t):
    return pl.ds(pl.multiple_of(i * t, t), t)


def _retention_tables():
    h = np.arange(8, dtype=np.float32)
    log_g = np.log1p(-np.exp2(-5.0 - h)).astype(np.float32)
    idx = np.arange(RET_C, dtype=np.float32)
    diff = idx[:, None] - idx[None, :]
    dtab = np.where(diff >= 0, np.exp(log_g[:, None, None] * np.maximum(diff, 0.0)), 0.0).astype(np.float32)
    dtab = dtab.reshape(8 * RET_C, RET_C)
    lane_head = np.arange(256) // 32
    a_tab = np.exp(log_g[lane_head][None, :] * (idx + 1.0)[:, None]).astype(np.float32)
    b_tab = np.exp(log_g[lane_head][None, :] * (RET_C - 1.0 - idx)[:, None]).astype(np.float32)
    lam = np.exp(log_g[lane_head] * RET_C).astype(np.float32)[:, None]
    bd = (lane_head[:, None] == (np.arange(512) // 64)[None, :]).astype(np.float32)
    return dtab, a_tab, b_tab, lam, bd


def _rotary_tables():
    inv_r = (1.0 / (np.float32(10000.0) ** np.linspace(0.0, 1.0, 16, dtype=np.float32))).astype(np.float32)
    inv_a = (np.float32(500000.0) ** (-np.arange(0, 16, 2, dtype=np.float32) / np.float32(16))).astype(np.float32)
    ifc = np.zeros((1, 128), np.float32)
    ifc[0, 0:16], ifc[0, 16:24] = inv_r, inv_a
    spread = np.zeros((128, 768), np.float32)
    for lane in range(256):
        spread[(lane % 32) % 16, lane] = 1.0
    for lane in range(512):
        d = lane % 64
        spread[16 + d % 8 if d < 16 else 24, 256 + lane] = 1.0
    return ifc, spread


def _rot_halves(tm):
    lo_r = (lax.broadcasted_iota(jnp.int32, (tm, 256), 1) % 32) < 16
    lo_a = (lax.broadcasted_iota(jnp.int32, (tm, 512), 1) % 64) < 8
    return lo_r, lo_a


def _spread_exact(t, e):
    hi = t.astype(BF16)
    r1 = t - hi.astype(F32)
    mid = r1.astype(BF16)
    lo = (r1 - mid.astype(F32)).astype(BF16)
    return _nn(hi, e) + _nn(mid, e) + _nn(lo, e)


def _rot_tables(cos_ref, sin_ref, e_ref):
    cs = _spread_exact(cos_ref[...], e_ref[...])
    sn = _spread_exact(sin_ref[...], e_ref[...])
    return cs[:, 0:256], cs[:, 256:768], sn[:, 0:256], sn[:, 256:768]


def _proj_fwd(h1, win_g, cos, sin, spread, after):
    tm = 256

    def body(h_ref, w_ref, cos_ref, sin_ref, e_ref, qr_ref, kr_ref, rv_ref, rg_ref, aq_ref, ak_ref, av_ref, p_ref, _):
        h = h_ref[...]
        for k in range(N_CHIP):
            p_ref[:, k * WIN_C:(k + 1) * WIN_C] = _nn(h, w_ref[k])
        cr, ca, sr, sa = _rot_tables(cos_ref, sin_ref, e_ref)
        lo_r, lo_a = _rot_halves(tm)

        def rot_r(v):
            return v * cr + sr * jnp.where(lo_r, -pltpu.roll(v, 240, 1), pltpu.roll(v, 16, 1))

        def rot_a(v):
            return v * ca + sa * jnp.where(lo_a, -pltpu.roll(v, 504, 1), pltpu.roll(v, 8, 1))

        qr_ref[...] = rot_r(p_ref[:, 0:256]).astype(BF16)
        kr_ref[...] = (rot_r(p_ref[:, 256:512]) * RET_SCALE).astype(BF16)
        rv_ref[...] = p_ref[:, 512:1024].astype(BF16)
        rg_ref[...] = p_ref[:, 1024:1536]
        aq, ak = rot_a(p_ref[:, 1536:2048]), rot_a(p_ref[:, 2048:2560])
        for j in range(4):
            aq_ref[j] = aq[:, 128 * j:128 * j + 128]
            ak_ref[j] = ak[:, 128 * j:128 * j + 128]
            av_ref[j] = p_ref[:, 2560 + 128 * j:2560 + 128 * j + 128]

    row = lambda w: pl.BlockSpec((tm, w), lambda i: (i, 0))
    slab = pl.BlockSpec((4, tm, 128), lambda i: (0, i, 0))
    return _carry(
        "proj_fwd", body, _NoExchange(), (), (h1, win_g, cos, sin, spread),
        [row(D), pl.BlockSpec((N_CHIP, D, WIN_C), lambda i: (0, 0, 0)), row(128), row(128),
         pl.BlockSpec((128, 768), lambda i: (0, 0))],
        [row(256), row(256), row(512), row(512), slab, slab, slab],
        [jax.ShapeDtypeStruct((S, w), BF16) for w in (256, 256, 512)]
        + [jax.ShapeDtypeStruct((S, 512), F32)] + [jax.ShapeDtypeStruct((4, S, 128), F32)] * 3,
        scratch_shapes=[pltpu.VMEM((tm, PW), F32)], grid=(S // tm,), semantics=("parallel",), after=after)[0]


def _seg_mean(v):
    lo = lax.broadcasted_iota(jnp.int32, v.shape, 1) < 64
    s_lo = jnp.sum(jnp.where(lo, v, 0.0), axis=-1, keepdims=True)
    s_hi = jnp.sum(jnp.where(lo, 0.0, v), axis=-1, keepdims=True)
    return jnp.where(lo, s_lo, s_hi) * (1.0 / 64.0)


def _ret_fwd(qr, kr, rv, proj, tabs, exchange, exchange_args, after=None):
    C, G = RET_C, RET_PER_STEP
    steps = S // (C * G)
    dtab, a_tab, b_tab, lam, bd = tabs

    def body(q_ref, k_ref, v_ref, g_ref, dt_ref, a_ref, b_ref, lam_ref, bd_ref, o_ref, cat_ref, st_ref, R, exch):
        @pl.when(pl.program_id(0) == 0)
        def _():
            exch.start()
            R[...] = jnp.zeros_like(R)

        lane_head = lax.broadcasted_iota(jnp.int32, (C, 256), 1) // 32
        col_head = lax.broadcasted_iota(jnp.int32, (C, 256), 1) // 64
        for s in range(G):
            rows = slice(s * C, (s + 1) * C)
            q, k, v = q_ref[rows, :], k_ref[rows, :], v_ref[rows, :]
            rb = R[...].astype(BF16)
            st_ref[s] = rb
            qa = (q.astype(F32) * a_ref[...]).astype(BF16)
            cross = _nn(qa, rb)
            p = (_nt(_stack_heads(q, lane_head, n=8), k) * dt_ref[...]).astype(BF16)
            og = [cross[:, 256 * g:256 * g + 256]
                  + _unstack_heads(_nn(p[4 * C * g:4 * C * (g + 1)], v[:, 256 * g:256 * g + 256]), col_head)
                  for g in range(2)]
            kb = (k.astype(F32) * b_ref[...]).astype(BF16)
            R[...] = R[...] * lam_ref[...] + _tn(kb, v) * bd_ref[...]
            o_ref[rows, 0:256] = og[0]
            o_ref[rows, 256:512] = og[1]
            for j in range(4):
                oj = og[j // 2][:, 128 * (j % 2):128 * (j % 2) + 128]
                xc = oj - _seg_mean(oj)
                rn = xc * lax.rsqrt(_seg_mean(xc * xc) + GN_EPS)
                gj = g_ref[rows, 128 * j:128 * j + 128]
                cat_ref[rows, 128 * j:128 * j + 128] = (rn * (gj * _sigmoid(gj))).astype(BF16)

        @pl.when(pl.program_id(0) == steps - 1)
        def _():
            exch.middle()
            exch.finish()

    row = lambda w: pl.BlockSpec((C * G, w), lambda n: (n, 0))
    full = lambda a: pl.BlockSpec(a.shape, lambda n: (0,) * a.ndim)
    return _carry(
        "ret_fwd", body, exchange, exchange_args, (qr, kr, rv, proj, dtab, a_tab, b_tab, lam, bd),
        [row(256), row(256), row(512), row(512),
         full(dtab), full(a_tab), full(b_tab), full(lam), full(bd)],
        [row(512), row(512), pl.BlockSpec((G, 256, 512), lambda n: (n, 0, 0))],
        [jax.ShapeDtypeStruct((S, 512), F32), jax.ShapeDtypeStruct((S, 512), BF16),
         jax.ShapeDtypeStruct((S // C, 256, 512), BF16)],
        scratch_shapes=[pltpu.VMEM((256, 512), F32)], grid=(steps,), semantics=("arbitrary",), after=after)


def _stack_heads(v, lane_head, fill=0.0, n=4):
    return jnp.concatenate([jnp.where(lane_head == h, v, jnp.full_like(v, fill)) for h in range(n)], axis=0)


def _unstack_heads(v, lane_head, n=4):
    out = v[0:ATT_BLK]
    for h in range(1, n):
        out = jnp.where(lane_head == h, v[h * ATT_BLK:(h + 1) * ATT_BLK], out)
    return out


def _att_bias(has_prev):
    nk = 2 * ATT_BLK if has_prev else ATT_BLK
    a = lax.broadcasted_iota(jnp.int32, (4 * ATT_BLK, nk), 0) % ATT_BLK
    kk = lax.broadcasted_iota(jnp.int32, (4 * ATT_BLK, nk), 1)
    if not has_prev:
        return None, jnp.where((a - kk) >= 0, 0.0, NEG)
    dist = ATT_BLK + a - kk
    inside = (dist >= 0) & (dist <= ATT_BLK)
    return jnp.where(inside, 0.0, NEG), jnp.where(inside & (kk >= ATT_BLK), 0.0, NEG)


def _class_rows(ib, r, d):
    if d == 1:
        return pl.ds(pl.multiple_of(ib * ATT_BLK, ATT_BLK), ATT_BLK)
    return pl.ds(ib * ATT_BLK * d + r, ATT_BLK, stride=d)


def _slab_pair(ref, g, rows):
    return jnp.concatenate([ref[2 * g, rows, :], ref[2 * g + 1, rows, :]], axis=1)


def _att_blocks(d):
    nb = S // d // ATT_BLK
    return nb, nb > 1


def _att_fwd(aq, ak, av, exchange, exchange_args):
    def body(q_ref, k_ref, v_ref, o_ref, l_ref, cat_ref, xc):
        xc.start()
        lane_head = lax.broadcasted_iota(jnp.int32, (ATT_BLK, 256), 1) // 64
        for pi, d in enumerate(PATTERN_DILATIONS):
            if pi == len(PATTERN_DILATIONS) - 1:
                xc.middle()
            nb, has_prev = _att_blocks(d)
            bias_rest, bias_first = _att_bias(has_prev)

            def block(b, carry, pi=pi, d=d, nb=nb, has_prev=has_prev, bias_rest=bias_rest, bias_first=bias_first):
                r, ib = b // nb, b % nb
                rows = _class_rows(ib, r, d)
                prow = _class_rows(jnp.maximum(ib - 1, 0), r, d)
                bias = jnp.where(ib == 0, bias_first, bias_rest) if has_prev else bias_first
                for g in range(2):
                    qg = _slab_pair(q_ref, g, rows).astype(BF16)
                    kg = _slab_pair(k_ref, g, rows)
                    vg = _slab_pair(v_ref, g, rows)
                    if has_prev:
                        kg = jnp.concatenate([_slab_pair(k_ref, g, prow), kg], axis=0)
                        vg = jnp.concatenate([_slab_pair(v_ref, g, prow), vg], axis=0)
                    kg, vg = kg.astype(BF16), vg.astype(BF16)
                    s = _nt(_stack_heads(qg, lane_head), kg) * ATT_SCALE + bias
                    m = jnp.max(s, axis=-1, keepdims=True)
                    p = jnp.exp(s - m)
                    den = jnp.sum(p, axis=-1, keepdims=True)
                    og = _unstack_heads(_nn(p.astype(BF16), vg) / den, lane_head)
                    lg = _unstack_heads(jnp.broadcast_to(m + jnp.log(den), (4 * ATT_BLK, 256)), lane_head)
                    for jj in range(2):
                        j = 2 * g + jj
                        o_new, l_new = og[:, 128 * jj:128 * jj + 128], lg[:, 128 * jj:128 * jj + 128]
                        if pi > 0:
                            o_old, l_old = o_ref[j, rows, :], l_ref[j, rows, :]
                            mx = jnp.maximum(l_old, l_new)
                            ea, eb = jnp.exp(l_old - mx), jnp.exp(l_new - mx)
                            den = ea + eb
                            o_new = (ea * o_old + eb * o_new) / den
                            l_new = mx + jnp.log(den)
                        o_ref[j, rows, :] = o_new
                        l_ref[j, rows, :] = l_new
                return carry

            lax.fori_loop(0, S // ATT_BLK, block, 0, unroll=4)

        def to_cat(i, carry):
            rows = _rows(i, 256)
            for j in range(4):
                cat_ref[rows, 128 * j:128 * j + 128] = o_ref[j, rows, :].astype(BF16)
            return carry

        lax.fori_loop(0, S // 256, to_cat, 0)
        xc.finish()

    slab = jax.ShapeDtypeStruct((4, S, 128), F32)
    return _carry("att_fwd", body, exchange, exchange_args, (aq, ak, av), [VMEM] * 3, [VMEM] * 3,
                  [slab, slab, jax.ShapeDtypeStruct((S, 512), BF16)])


def _mix_fwd(cat_r, cat_a, wout, x, g2, g3, exchange, exchange_args):
    tm = 512

    def body(cr_ref, ca_ref, w_ref, x_ref, g2_ref, g3_ref, mix_ref, x2_ref, h3_ref, xc):
        @pl.when(pl.program_id(0) == 0)
        def _():
            xc.start()

        mix = _nn(cr_ref[...], w_ref[0:512, :]) + _nn(ca_ref[...], w_ref[512:1024, :])
        mix_ref[...] = mix
        x2 = x_ref[...] + mix * _rstd(mix) * g2_ref[...]
        x2_ref[...] = x2
        h3_ref[...] = (x2 * _rstd(x2) * g3_ref[...]).astype(BF16)

        @pl.when(pl.program_id(0) == S // tm - 1)
        def _():
            xc.middle()
            xc.finish()

    row = lambda w: pl.BlockSpec((tm, w), lambda i: (i, 0))
    vec = pl.BlockSpec((1, D), lambda i: (0, 0))
    return _carry("mix_fwd", body, exchange, exchange_args, (cat_r, cat_a, wout, x, g2, g3),
                  [row(512), row(512), pl.BlockSpec((D, D), lambda i: (0, 0)), row(D), vec, vec],
                  [row(D), row(D), row(D)],
                  [jax.ShapeDtypeStruct((S, D), F32), jax.ShapeDtypeStruct((S, D), F32),
                   jax.ShapeDtypeStruct((S, D), BF16)],
                  grid=(S // tm,), semantics=("arbitrary",))


def _ffn_fwd(h3, wg, wu, wd, x2, tgt, g4):
    tm = 512
    last = N_CHIP - 1

    def body(h_ref, wg_ref, wu_ref, wd_ref, x2_ref, t_ref, g_ref,
             gt_ref, up_ref, a_ref, loss_ref, dy_ref, df_ref, dg_ref, f_ref):
        k, i = pl.program_id(0), pl.program_id(1)
        h = h_ref[...]
        gt = _nt(h, wg_ref[...])
        up = _nt(h, wu_ref[...])
        gt_ref[...] = gt.astype(BF16)
        up_ref[...] = up.astype(BF16)
        a = (gt * _sigmoid(gt) * up).astype(BF16)
        a_ref[...] = a
        part = _nn(a, wd_ref[...])
        rows = _rows(i, tm)

        @pl.when(k == 0)
        def _():
            f_ref[rows, :] = part

        @pl.when((k > 0) & (k < last))
        def _():
            f_ref[rows, :] = f_ref[rows, :] + part

        @pl.when((k == last) & (i == 0))
        def _():
            loss_ref[...] = jnp.zeros_like(loss_ref)
            dg_ref[...] = jnp.zeros_like(dg_ref)

        @pl.when(k == last)
        def _():
            fv = f_ref[rows, :] + part
            r = _rstd(fv)
            fn = fv * r
            e = x2_ref[...] + fn * g_ref[...] - t_ref[...]
            loss_ref[...] = loss_ref[...] + jnp.sum(jnp.sum(e * e, axis=-1, keepdims=True), axis=0, keepdims=True)
            dy = e * (1.0 / D)
            dy_ref[...] = dy
            dg_ref[...] = dg_ref[...] + jnp.sum(dy * fn, axis=0, keepdims=True)
            t = dy * g_ref[...]
            df_ref[...] = (r * (t - fn * jnp.mean(t * fn, axis=-1, keepdims=True))).astype(BF16)

    wrow = pl.BlockSpec((None, FF_C, D), lambda k, i: (k, 0, 0))
    act = pl.BlockSpec((None, tm, FF_C), lambda k, i: (k, i, 0))
    late = pl.BlockSpec((tm, D), lambda k, i: (jnp.where(k == last, i, 0), 0))
    vec = pl.BlockSpec((1, D), lambda k, i: (0, 0))
    return pl.pallas_call(
        body, grid=(N_CHIP, S // tm), name="ffn_fwd",
        in_specs=[pl.BlockSpec((tm, D), lambda k, i: (i, 0)), wrow, wrow, wrow, late, late, vec],
        out_specs=[act, act, act, vec, late, late, vec],
        out_shape=[jax.ShapeDtypeStruct((N_CHIP, S, FF_C), BF16)] * 3
                  + [jax.ShapeDtypeStruct((1, D), F32), jax.ShapeDtypeStruct((S, D), F32),
                     jax.ShapeDtypeStruct((S, D), BF16), jax.ShapeDtypeStruct((1, D), F32)],
        scratch_shapes=[pltpu.VMEM((S, D), F32)],
        compiler_params=_params("arbitrary", "arbitrary"),
    )(h3, wg, wu, wd, x2, tgt, g4)


def _ffn_bwd_act(df, gt, up, wg, wu, wd, dy, x2, mix, g2, g3):
    tm, sub = 512, 256
    last = N_CHIP - 1

    def body(df_ref, gt_ref, up_ref, wg_ref, wu_ref, wd_ref, dy_ref, x2_ref, mix_ref, g2_ref, g3_ref,
             dgt_ref, dup_ref, dx2_ref, dmix_ref, dg3_ref, dg2_ref, dh_ref):
        k, i = pl.program_id(0), pl.program_id(1)
        parts = []
        for s in range(tm // sub):
            rows = slice(s * sub, (s + 1) * sub)
            da = _nt(df_ref[rows, :], wd_ref[...])
            gt, up = gt_ref[rows, :].astype(F32), up_ref[rows, :].astype(F32)
            sg = _sigmoid(gt)
            dup = (da * gt * sg).astype(BF16)
            dgt = (da * up * (sg * (1.0 + gt * (1.0 - sg)))).astype(BF16)
            dup_ref[rows, :] = dup
            dgt_ref[rows, :] = dgt
            parts.append(_nn(dgt, wg_ref[...]) + _nn(dup, wu_ref[...]))
        part = jnp.concatenate(parts, axis=0)
        rows = _rows(i, tm)

        @pl.when(k == 0)
        def _():
            dh_ref[rows, :] = part

        @pl.when((k > 0) & (k < last))
        def _():
            dh_ref[rows, :] = dh_ref[rows, :] + part

        @pl.when((k == last) & (i == 0))
        def _():
            dg3_ref[...] = jnp.zeros_like(dg3_ref)
            dg2_ref[...] = jnp.zeros_like(dg2_ref)

        @pl.when(k == last)
        def _():
            dh = dh_ref[rows, :] + part
            x2 = x2_ref[...]
            r3 = _rstd(x2)
            xn = x2 * r3
            dg3_ref[...] = dg3_ref[...] + jnp.sum(dh * xn, axis=0, keepdims=True)
            t = dh * g3_ref[...]
            dx2 = dy_ref[...] + r3 * (t - xn * jnp.mean(t * xn, axis=-1, keepdims=True))
            dx2_ref[...] = dx2
            mix = mix_ref[...]
            r2 = _rstd(mix)
            mn = mix * r2
            dg2_ref[...] = dg2_ref[...] + jnp.sum(dx2 * mn, axis=0, keepdims=True)
            u = dx2 * g2_ref[...]
            dmix_ref[...] = (r2 * (u - mn * jnp.mean(u * mn, axis=-1, keepdims=True))).astype(BF16)

    wrow = pl.BlockSpec((None, FF_C, D), lambda k, i: (k, 0, 0))
    act = pl.BlockSpec((None, tm, FF_C), lambda k, i: (k, i, 0))
    row = pl.BlockSpec((tm, D), lambda k, i: (i, 0))
    late = pl.BlockSpec((tm, D), lambda k, i: (jnp.where(k == last, i, 0), 0))
    vec = pl.BlockSpec((1, D), lambda k, i: (0, 0))
    return pl.pallas_call(
        body, grid=(N_CHIP, S // tm), name="ffn_bwd_act",
        in_specs=[row, act, act, wrow, wrow, wrow, late, late, late, vec, vec],
        out_specs=[act, act, late, late, vec, vec],
        out_shape=[jax.ShapeDtypeStruct((N_CHIP, S, FF_C), BF16), jax.ShapeDtypeStruct((N_CHIP, S, FF_C), BF16),
                   jax.ShapeDtypeStruct((S, D), F32), jax.ShapeDtypeStruct((S, D), BF16),
                   jax.ShapeDtypeStruct((1, D), F32), jax.ShapeDtypeStruct((1, D), F32)],
        scratch_shapes=[pltpu.VMEM((S, D), F32)],
        compiler_params=_params("arbitrary", "arbitrary"),
    )(df, gt, up, wg, wu, wd, dy, x2, mix, g2, g3)


def _ffn_bwd_w(a, df, h3, dgt, dup):
    tm = 1024
    assert S // tm == 2

    def body(a_ref, df_ref, h_ref, dgt_ref, dup_ref, dwd_ref, dwg_ref, dwu_ref, acc_d, acc_g, acc_u):
        i = pl.program_id(1)
        h = h_ref[...]
        parts = (_tn(a_ref[...], df_ref[...]), _tn(dgt_ref[...], h), _tn(dup_ref[...], h))

        @pl.when(i == 0)
        def _():
            for acc, part in zip((acc_d, acc_g, acc_u), parts):
                acc[...] = part

        @pl.when(i == S // tm - 1)
        def _():
            for out, acc, part in zip((dwd_ref, dwg_ref, dwu_ref), (acc_d, acc_g, acc_u), parts):
                out[...] = (acc[...] + part).astype(BF16)

    act = pl.BlockSpec((None, tm, FF_C), lambda k, i: (k, i, 0))
    row = pl.BlockSpec((tm, D), lambda k, i: (i, 0))
    wrow = pl.BlockSpec((None, FF_C, D), lambda k, i: (k, 0, 0))
    return pl.pallas_call(
        body, grid=(N_CHIP, S // tm), name="ffn_bwd_w",
        in_specs=[act, row, row, act, act],
        out_specs=[wrow, wrow, wrow],
        out_shape=[jax.ShapeDtypeStruct((N_CHIP, FF_C, D), BF16)] * 3,
        scratch_shapes=[pltpu.VMEM((FF_C, D), F32)] * 3,
        compiler_params=_params("parallel", "arbitrary"),
    )(a, df, h3, dgt, dup)


def _mix_bwd(dmix, cat_r, cat_a, wout, exchange, exchange_args):
    tm = 1024

    def body(dm_ref, cr_ref, ca_ref, w_ref, dret_ref, datt_ref, dw_ref, acc, xc):
        i = pl.program_id(0)

        @pl.when(i == 0)
        def _():
            xc.start()
            acc[...] = jnp.zeros_like(acc)

        dm = dm_ref[...]
        dret_ref[...] = _nt(dm, w_ref[0:512, :])
        datt = _nt(dm, w_ref[512:1024, :])
        for j in range(4):
            datt_ref[j] = datt[:, 128 * j:128 * j + 128]
        acc[0:512, :] += _tn(cr_ref[...], dm)
        acc[512:1024, :] += _tn(ca_ref[...], dm)

        @pl.when(i == S // tm - 1)
        def _():
            dw_ref[...] = acc[...].astype(BF16)
            xc.middle()
            xc.finish()

    row = lambda w: pl.BlockSpec((tm, w), lambda i: (i, 0))
    full = pl.BlockSpec((D, D), lambda i: (0, 0))
    return _carry("mix_bwd", body, exchange, exchange_args, (dmix, cat_r, cat_a, wout),
                  [row(D), row(512), row(512), full],
                  [row(512), pl.BlockSpec((4, tm, 128), lambda i: (0, i, 0)), full],
                  [jax.ShapeDtypeStruct((S, 512), F32), jax.ShapeDtypeStruct((4, S, 128), F32),
                   jax.ShapeDtypeStruct((D, D), BF16)],
                  scratch_shapes=[pltpu.VMEM((D, D), F32)], grid=(S // tm,), semantics=("arbitrary",))


def _att_bwd(aq, ak, av, datt, att_out, lse, exchange, exchange_args, after=None):
    def body(q_ref, k_ref, v_ref, do_ref, out_ref, l_ref, dq_ref, dk_ref, dv_ref, xc):
        xc.start()

        lane_head = lax.broadcasted_iota(jnp.int32, (ATT_BLK, 256), 1) // 64
        for pi, d in enumerate(PATTERN_DILATIONS):
            nb, has_prev = _att_blocks(d)
            assert pi > 0 or not has_prev
            bias_rest, bias_first = _att_bias(has_prev)

            def block(b, carry, pi=pi, d=d, nb=nb, has_prev=has_prev, bias_rest=bias_rest, bias_first=bias_first):
                r, ib = b // nb, b % nb
                rows = _class_rows(ib, r, d)
                prow = _class_rows(jnp.maximum(ib - 1, 0), r, d)
                bias = jnp.where(ib == 0, bias_first, bias_rest) if has_prev else bias_first
                for g in range(2):
                    qg = _slab_pair(q_ref, g, rows).astype(BF16)
                    kg = _slab_pair(k_ref, g, rows)
                    vg = _slab_pair(v_ref, g, rows)
                    if has_prev:
                        kg = jnp.concatenate([_slab_pair(k_ref, g, prow), kg], axis=0)
                        vg = jnp.concatenate([_slab_pair(v_ref, g, prow), vg], axis=0)
                    kg, vg = kg.astype(BF16), vg.astype(BF16)
                    dog = _slab_pair(do_ref, g, rows)
                    outg = _slab_pair(out_ref, g, rows)
                    lg = _slab_pair(l_ref, g, rows)
                    qs = _stack_heads(qg, lane_head)
                    dos = _stack_heads(dog, lane_head)
                    delta = jnp.sum(dos * jnp.concatenate([outg] * 4, axis=0), axis=-1, keepdims=True)
                    lh = jnp.max(_stack_heads(lg, lane_head, NEG), axis=-1, keepdims=True)
                    s = _nt(qs, kg) * ATT_SCALE + bias
                    p = jnp.exp(s - lh)
                    dosb = dos.astype(BF16)
                    ds = (p * (_nt(dosb, vg) - delta) * ATT_SCALE).astype(BF16)
                    dq = _unstack_heads(_nn(ds, kg), lane_head)
                    dk = _tn(ds, qs)
                    dv = _tn(p.astype(BF16), dosb)
                    for jj in range(2):
                        j, sl = 2 * g + jj, slice(128 * jj, 128 * jj + 128)
                        if pi == 0:
                            dq_ref[j, rows, :] = dq[:, sl]
                            dk_ref[j, rows, :] = dk[:, sl]
                            dv_ref[j, rows, :] = dv[:, sl]
                            continue
                        dq_ref[j, rows, :] += dq[:, sl]
                        if has_prev:
                            dk_ref[j, prow, :] += dk[0:ATT_BLK, sl]
                            dv_ref[j, prow, :] += dv[0:ATT_BLK, sl]
                            dk_ref[j, rows, :] += dk[ATT_BLK:2 * ATT_BLK, sl]
                            dv_ref[j, rows, :] += dv[ATT_BLK:2 * ATT_BLK, sl]
                        else:
                            dk_ref[j, rows, :] += dk[:, sl]
                            dv_ref[j, rows, :] += dv[:, sl]
                return carry

            lax.fori_loop(0, S // ATT_BLK, block, 0, unroll=4)
        xc.middle()
        xc.finish()

    slab = jax.ShapeDtypeStruct((4, S, 128), F32)
    return _carry("att_bwd", body, exchange, exchange_args, (aq, ak, av, datt, att_out, lse), [VMEM] * 6, [VMEM] * 3,
                  [slab, slab, slab], after=after)


def _ret_bwd(qr, kr, rv, proj, o_raw, states, dret, tabs, exchange, exchange_args, after=None):
    C, G = RET_C, RET_PER_STEP
    steps = S // (C * G)
    dtab, a_tab, b_tab, lam, bd = tabs

    def body(q_ref, k_ref, v_ref, g_ref, o_ref, st_ref, dr_ref, dt_ref, a_ref, b_ref, lam_ref, bd_ref,
             dq_ref, dk_ref, dv_ref, dg_ref, dR, exch):
        @pl.when(pl.program_id(0) == 0)
        def _():
            exch.start()
            dR[...] = jnp.zeros_like(dR)

        lane_head = lax.broadcasted_iota(jnp.int32, (C, 256), 1) // 32
        col_head = lax.broadcasted_iota(jnp.int32, (C, 256), 1) // 64
        for s in reversed(range(G)):
            rows = slice(s * C, (s + 1) * C)
            q, k, v = q_ref[rows, :], k_ref[rows, :], v_ref[rows, :]
            dos = []
            for j in range(4):
                sl = slice(128 * j, 128 * j + 128)
                oj = o_ref[rows, sl]
                xc = oj - _seg_mean(oj)
                rs = lax.rsqrt(_seg_mean(xc * xc) + GN_EPS)
                rn = xc * rs
                gj = g_ref[rows, sl]
                sg = _sigmoid(gj)
                dret = dr_ref[rows, sl]
                dg_ref[rows, sl] = dret * rn * (sg * (1.0 + gj * (1.0 - sg)))
                drn = dret * (gj * sg)
                dos.append(rs * (drn - _seg_mean(drn) - rn * _seg_mean(drn * rn)))
            do = [jnp.concatenate(dos[0:2], axis=1), jnp.concatenate(dos[2:4], axis=1)]
            do8 = jnp.concatenate(do, axis=1).astype(BF16)
            drb = dR[...].astype(BF16)
            rb = st_ref[s]
            dq = _nt(do8, rb) * a_ref[...]
            dk = _nt(v, drb) * b_ref[...]
            kb = (k.astype(F32) * b_ref[...]).astype(BF16)
            dvall = _nn(kb, drb)
            qs = _stack_heads(q, lane_head, n=8)
            dec = dt_ref[...]
            p = (_nt(qs, k) * dec).astype(BF16)
            dos = [_stack_heads(do[g], col_head).astype(BF16) for g in range(2)]
            dp = jnp.concatenate([_nt(dos[g], v[:, 256 * g:256 * g + 256]) for g in range(2)], axis=0)
            ds = (dp * dec).astype(BF16)
            dq = dq + _unstack_heads(_nn(ds, k), lane_head, n=8)
            dk = dk + _tn(ds, qs)
            dv = [dvall[:, 256 * g:256 * g + 256] + _tn(p[4 * C * g:4 * C * (g + 1)], dos[g]) for g in range(2)]
            qa = (q.astype(F32) * a_ref[...]).astype(BF16)
            dR[...] = dR[...] * lam_ref[...] + _tn(qa, do8) * bd_ref[...]
            dq_ref[rows, :] = dq
            dk_ref[rows, :] = dk
            dv_ref[rows, 0:256] = dv[0]
            dv_ref[rows, 256:512] = dv[1]

        @pl.when(pl.program_id(0) == steps - 1)
        def _():
            exch.middle()
            exch.finish()

    rev = lambda w: pl.BlockSpec((C * G, w), lambda n: (steps - 1 - n, 0))
    full = lambda a: pl.BlockSpec(a.shape, lambda n: (0,) * a.ndim)
    return _carry(
        "ret_bwd", body, exchange, exchange_args, (qr, kr, rv, proj, o_raw, states, dret, dtab, a_tab, b_tab, lam, bd),
        [rev(256), rev(256), rev(512), rev(512), rev(512),
         pl.BlockSpec((G, 256, 512), lambda n: (steps - 1 - n, 0, 0)), rev(512),
         full(dtab), full(a_tab), full(b_tab), full(lam), full(bd)],
        [rev(256), rev(256), rev(512), rev(512)],
        [jax.ShapeDtypeStruct((S, 256), F32), jax.ShapeDtypeStruct((S, 256), F32),
         jax.ShapeDtypeStruct((S, 512), F32), jax.ShapeDtypeStruct((S, 512), F32)],
        scratch_shapes=[pltpu.VMEM((256, 512), F32)], grid=(steps,), semantics=("arbitrary",), after=after)


def _rot_bwd(cos, sin, spread, dqr, dkr, drv, drg, dq_att, dk_att, dv_att):
    tm = 256

    def body(cos_ref, sin_ref, e_ref, dqr_ref, dkr_ref, drv_ref, drg_ref, dqa_ref, dka_ref, dva_ref, dp_ref):
        cr, ca, sr, sa = _rot_tables(cos_ref, sin_ref, e_ref)
        lo_r, lo_a = _rot_halves(tm)

        def unrot_r(g):
            gs = g * sr
            return g * cr + pltpu.roll(jnp.where(lo_r, -gs, 0.0), 16, 1) + pltpu.roll(jnp.where(lo_r, 0.0, gs), 240, 1)

        def unrot_a(g):
            gs = g * sa
            return g * ca + pltpu.roll(jnp.where(lo_a, -gs, 0.0), 8, 1) + pltpu.roll(jnp.where(lo_a, 0.0, gs), 504, 1)

        def wide(ref):
            return jnp.concatenate([ref[j] for j in range(4)], axis=1)

        dp_ref[:, 0:256] = unrot_r(dqr_ref[...]).astype(BF16)
        dp_ref[:, 256:512] = unrot_r(dkr_ref[...] * RET_SCALE).astype(BF16)
        dp_ref[:, 512:1024] = drv_ref[...].astype(BF16)
        dp_ref[:, 1024:1536] = drg_ref[...].astype(BF16)
        dp_ref[:, 1536:2048] = unrot_a(wide(dqa_ref)).astype(BF16)
        dp_ref[:, 2048:2560] = unrot_a(wide(dka_ref)).astype(BF16)
        dp_ref[:, 2560:3072] = wide(dva_ref).astype(BF16)

    row = lambda w: pl.BlockSpec((tm, w), lambda i: (i, 0))
    slab = pl.BlockSpec((4, tm, 128), lambda i: (0, i, 0))
    return pl.pallas_call(
        body, grid=(S // tm,), name="rot_bwd",
        in_specs=[row(128), row(128), pl.BlockSpec((128, 768), lambda i: (0, 0)),
                  row(256), row(256), row(512), row(512), slab, slab, slab],
        out_specs=row(PW), out_shape=jax.ShapeDtypeStruct((S, PW), BF16),
        compiler_params=_params("parallel"),
    )(cos, sin, spread, dqr, dkr, drv, drg, dq_att, dk_att, dv_att)


def _win_bwd_w(h1, dproj, exchange, exchange_args):
    def body(h_ref, dp_ref, dw_ref, xc):
        k = pl.program_id(0)

        @pl.when(k == 0)
        def _():
            xc.start()

        dw_ref[...] = _tn(h_ref[...], dp_ref[...]).astype(BF16)

        @pl.when(k == N_CHIP - 1)
        def _():
            xc.middle()
            xc.finish()

    (dw,), out = _carry(
        "win_bwd_w", body, exchange, exchange_args, (h1, dproj),
        [pl.BlockSpec((S, D), lambda k: (0, 0)), pl.BlockSpec((S, WIN_C), lambda k: (0, k))],
        [pl.BlockSpec((None, D, WIN_C), lambda k: (k, 0, 0))],
        [jax.ShapeDtypeStruct((N_CHIP, D, WIN_C), BF16)], grid=(N_CHIP,), semantics=("arbitrary",))
    return dw, out


def _in_bwd(dproj, win_g, x, dx2, g1, after):
    tm = 512

    def body(dp_ref, w_ref, x_ref, dx2_ref, g_ref, dx_ref, dg_ref, _):
        @pl.when(pl.program_id(0) == 0)
        def _():
            dg_ref[...] = jnp.zeros_like(dg_ref)

        dh = _nt(dp_ref[:, 0:WIN_C], w_ref[0])
        for k in range(1, N_CHIP):
            dh = dh + _nt(dp_ref[:, k * WIN_C:(k + 1) * WIN_C], w_ref[k])
        xv = x_ref[...]
        r = _rstd(xv)
        xn = xv * r
        dg_ref[...] = dg_ref[...] + jnp.sum(dh * xn, axis=0, keepdims=True)
        t = dh * g_ref[...]
        dx_ref[...] = dx2_ref[...] + r * (t - xn * jnp.mean(t * xn, axis=-1, keepdims=True))

    row = lambda w: pl.BlockSpec((tm, w), lambda i: (i, 0))
    vec = pl.BlockSpec((1, D), lambda i: (0, 0))
    return _carry("in_bwd", body, _NoExchange(), (), (dproj, win_g, x, dx2, g1),
                  [row(PW), pl.BlockSpec((N_CHIP, D, WIN_C), lambda i: (0, 0, 0)), row(D), row(D), vec],
                  [row(D), vec], [jax.ShapeDtypeStruct((S, D), F32), jax.ShapeDtypeStruct((1, D), F32)],
                  grid=(S // tm,), semantics=("arbitrary",), after=after)[0]


ANY = pl.BlockSpec(memory_space=pl.ANY)
VMEM = pl.BlockSpec(memory_space=pltpu.VMEM)
FLIPS = ((1, 0), (0, 1), (1, 1))


def _place():
    x, y, c = lax.axis_index("x"), lax.axis_index("y"), lax.axis_index("c")
    chips = [((1 - x) if fx else x, (1 - y) if fy else y) for fx, fy in FLIPS]
    return x, y, c, 2 * x + y, chips


def _remote(src, dst, send_sem, recv_sem, device):
    return pltpu.make_async_remote_copy(src_ref=src, dst_ref=dst, send_sem=send_sem, recv_sem=recv_sem,
                                        device_id=device, device_id_type=MESH)


class _Exchange:
    aliases = {}

    def middle(self, ins, outs, sems):
        pass


class _GatherShards(_Exchange):
    def __init__(self, shards):
        n = self.n = len(shards)
        self.n_in = self.n_out = n
        self.out_shape = [jax.ShapeDtypeStruct((N_CHIP,) + s.shape, s.dtype) for s in shards]
        dma = pltpu.SemaphoreType.DMA
        self.scratch = [dma((3 * n,)), dma((3 * n,)), dma((3 * n,)), dma((3 * n,)), dma((n,)), dma((n,))]

    def _ici(self, ins, outs, sems, a, j, chip):
        x, y, c, me, chips = _place()
        half = ins[a].shape[0] // 2
        return _remote(ins[a].at[pl.ds(c * half, half), :], outs[a].at[me, pl.ds(c * half, half), :],
                       sems[0].at[3 * a + j], sems[1].at[3 * a + j], (*chip, c))

    def _fwd(self, outs, sems, a, j, chip, half_of):
        x, y, c, me, chips = _place()
        half = outs[a].shape[1] // 2
        blk = outs[a].at[2 * chip[0] + chip[1], pl.ds(half_of * half, half), :]
        return _remote(blk, blk, sems[2].at[3 * a + j], sems[3].at[3 * a + j], (x, y, 1 - c))

    def _own(self, ins, outs, sems, a):
        return _own_shard_to_sibling(ins[a], outs[a], sems[4].at[a], sems[5].at[a])

    def start(self, ins, outs, sems):
        chips = _place()[4]
        for a in range(self.n):
            for j, chip in enumerate(chips):
                self._ici(ins, outs, sems, a, j, chip).start()
        for a in range(self.n):
            self._own(ins, outs, sems, a).start()

    def middle(self, ins, outs, sems):
        x, y, c, me, chips = _place()
        for a in range(self.n):
            for j, chip in enumerate(chips):
                half = outs[a].shape[1] // 2
                blk = outs[a].at[2 * chip[0] + chip[1], pl.ds(c * half, half), :]
                _remote(blk, blk, sems[0].at[3 * a + j], sems[1].at[3 * a + j], (x, y, c)).wait_recv()
                self._fwd(outs, sems, a, j, chip, c).start()

    def finish(self, ins, outs, sems):
        x, y, c, me, chips = _place()
        for a in range(self.n):
            for j, chip in enumerate(chips):
                self._fwd(outs, sems, a, j, chip, 1 - c).wait_recv()
        for a in range(self.n):
            for j, chip in enumerate(chips):
                self._ici(ins, outs, sems, a, j, chip).wait_send()
                self._fwd(outs, sems, a, j, chip, c).wait_send()
            self._own(ins, outs, sems, a).wait()


def _own_shard_to_sibling(shard_ref, gathered_ref, send_sem, recv_sem):
    x, y, c, me, chips = _place()
    return _remote(shard_ref, gathered_ref.at[me], send_sem, recv_sem, (x, y, 1 - c))


class _NoExchange(_Exchange):
    n_in = n_out = 0
    out_shape = ()
    scratch = ()

    def start(self, ins, outs, sems):
        pass

    def finish(self, ins, outs, sems):
        pass


class _ForwardGathered(_Exchange):
    def __init__(self, shards, own=True, forward=True):
        self.own, self.forward = own, forward
        n = self.n = len(shards)
        self.n_in, self.n_out = 2 * n, n
        self.out_shape = [jax.ShapeDtypeStruct((N_CHIP,) + s.shape, s.dtype) for s in shards]
        dma = pltpu.SemaphoreType.DMA
        self.scratch = [dma((3 * n,)), dma((3 * n,)), dma((n,)), dma((n,))]
        self.aliases = {n + a: a for a in range(n)}

    def _fwd(self, outs, sems, a, j, chip, half_of):
        x, y, c, me, chips = _place()
        half = outs[a].shape[1] // 2
        blk = outs[a].at[2 * chip[0] + chip[1], pl.ds(half_of * half, half), :]
        return _remote(blk, blk, sems[0].at[3 * a + j], sems[1].at[3 * a + j], (x, y, 1 - c))

    def _own(self, ins, outs, sems, a):
        return _own_shard_to_sibling(ins[a], outs[a], sems[2].at[a], sems[3].at[a])

    def start(self, ins, outs, sems):
        x, y, c, me, chips = _place()
        for a in range(self.n):
            for j, chip in enumerate(chips if self.forward else ()):
                self._fwd(outs, sems, a, j, chip, c).start()
        for a in range(self.n if self.own else 0):
            self._own(ins, outs, sems, a).start()

    def finish(self, ins, outs, sems):
        x, y, c, me, chips = _place()
        for a in range(self.n):
            for j, chip in enumerate(chips if self.forward else ()):
                self._fwd(outs, sems, a, j, chip, 1 - c).wait_recv()
        for a in range(self.n):
            for j, chip in enumerate(chips if self.forward else ()):
                self._fwd(outs, sems, a, j, chip, c).wait_send()
            if self.own:
                self._own(ins, outs, sems, a).wait()


HBM = pl.BlockSpec(memory_space=pltpu.HBM)
SEMS = pl.BlockSpec(memory_space=pltpu.SEMAPHORE)
DATAFLOW = pltpu.SideEffectType.DATAFLOW_SIDE_EFFECTING


class _OverIci:
    def __init__(self, name, sources, lands):
        self.name, self.n = name, len(sources)
        hbm = lambda t: pltpu.with_memory_space_constraint(t, pltpu.HBM)
        self.arrays = [hbm(t) for t in sources] + [hbm(t) for t in lands]

    def sent(self, src, land, a, chip):
        raise NotImplementedError

    def landed(self, land, a, chip):
        raise NotImplementedError

    def _copy(self, arr, sems, a, j, receiving):
        x, y, c, me, chips = _place()
        src, dst = self.sent(arr[a], arr[self.n + a], a, chips[j])
        if receiving:
            dst = self.landed(arr[self.n + a], a, chips[j])
        return _remote(src, dst, sems[0].at[3 * a + j], sems[1].at[3 * a + j], (*chips[j], c))

    def start(self, after):
        m = len(self.arrays)

        def body(*refs):
            arr, sems, token = refs[:m], refs[m + 1:m + 3], refs[-1]
            for a in range(self.n):
                for j in range(3):
                    self._copy(arr, sems, a, j, False).start()
            token[...] = jnp.zeros_like(token)

        dma = pltpu.SemaphoreType.DMA
        outs = pl.pallas_call(
            body, name=self.name + "_start",
            out_shape=[dma((3 * self.n,)), dma((3 * self.n,))] + [pltpu.HBM(t.shape, t.dtype) for t in self.arrays]
                      + [jax.ShapeDtypeStruct((8, 128), F32)],
            in_specs=[HBM] * m + [ANY], out_specs=[SEMS, SEMS] + [HBM] * m + [VMEM],
            input_output_aliases={i: 2 + i for i in range(m)},
            compiler_params=pltpu.CompilerParams(has_side_effects=DATAFLOW),
        )(*self.arrays, after)
        self.sems, self.arrays = outs[0:2], list(outs[2:2 + m])
        return outs[-1]

    def wait(self, after):
        m = len(self.arrays)

        def body(*refs):
            arr, sems = refs[:m], refs[m:m + 2]
            for a in range(self.n):
                for j in range(3):
                    self._copy(arr, sems, a, j, False).wait_send()
                    self._copy(arr, sems, a, j, True).wait_recv()

        outs = pl.pallas_call(
            body, name=self.name + "_wait",
            out_shape=[pltpu.HBM(t.shape, t.dtype) for t in self.arrays],
            in_specs=[HBM] * m + [SEMS, SEMS, ANY], out_specs=[HBM] * m,
            input_output_aliases={i: i for i in range(m)},
            compiler_params=pltpu.CompilerParams(has_side_effects=DATAFLOW),
        )(*self.arrays, *self.sems, after)
        return list(outs[:self.n]), list(outs[self.n:])


class _GatherOverIci(_OverIci):
    def __init__(self, name, shards):
        super().__init__(name, shards, [lax.empty((N_CHIP,) + s.shape, s.dtype) for s in shards])

    @staticmethod
    def _half(ref):
        c = lax.axis_index("c")
        half = ref.shape[-2] // 2
        return pl.ds(c * half, half)

    def sent(self, src, land, a, chip):
        return src.at[self._half(src), :], land.at[_place()[3], self._half(src), :]

    def landed(self, land, a, chip):
        return land.at[2 * chip[0] + chip[1], self._half(land), :]


class _SumOverIci(_OverIci):
    def __init__(self, name, pre):
        super().__init__(name, pre, [lax.empty(p.shape, p.dtype) for p in pre])

    def sent(self, src, land, a, chip):
        return src.at[2 * chip[0] + chip[1]], land.at[_place()[3]]

    def landed(self, land, a, chip):
        return land.at[2 * chip[0] + chip[1]]


class _HalvesToSibling(_Exchange):
    def __init__(self, grads):
        n = self.n = len(grads)
        self.n_in = self.n_out = n
        self.out_shape = [jax.ShapeDtypeStruct((N_CHIP, g.shape[1] // 2, g.shape[2]), g.dtype) for g in grads]
        self.scratch = [pltpu.SemaphoreType.DMA((n,)), pltpu.SemaphoreType.DMA((n,))]

    def _copy(self, ins, outs, sems, a):
        x, y, c, me, chips = _place()
        half = ins[a].shape[1] // 2
        return _remote(ins[a].at[:, pl.ds((1 - c) * half, half), :], outs[a], sems[0].at[a], sems[1].at[a], (x, y, 1 - c))

    def start(self, ins, outs, sems):
        for a in range(self.n):
            self._copy(ins, outs, sems, a).start()

    def finish(self, ins, outs, sems):
        for a in range(self.n):
            self._copy(ins, outs, sems, a).wait_recv()
        for a in range(self.n):
            self._copy(ins, outs, sems, a).wait_send()


class _ShareHalves(_Exchange):
    def __init__(self, fulls):
        n = self.n = len(fulls)
        self.n_in = self.n_out = n
        self.out_shape = [jax.ShapeDtypeStruct(f.shape, f.dtype) for f in fulls]
        self.scratch = [pltpu.SemaphoreType.DMA((n,)), pltpu.SemaphoreType.DMA((n,))]
        self.aliases = {a: a for a in range(n)}

    def _copy(self, outs, sems, a, half_of):
        x, y, c, me, chips = _place()
        half = outs[a].shape[0] // 2
        rows = outs[a].at[pl.ds(half_of * half, half), :]
        return _remote(rows, rows, sems[0].at[a], sems[1].at[a], (x, y, 1 - c))

    def start(self, ins, outs, sems):
        c = _place()[2]
        for a in range(self.n):
            self._copy(outs, sems, a, c).start()

    def finish(self, ins, outs, sems):
        c = _place()[2]
        for a in range(self.n):
            self._copy(outs, sems, a, 1 - c).wait_recv()
        for a in range(self.n):
            self._copy(outs, sems, a, c).wait_send()


class _GatherBlocks(_Exchange):
    def __init__(self, block):
        self.n_in = self.n_out = 1
        self.out_shape = [jax.ShapeDtypeStruct((8,) + block.shape, block.dtype)]
        dma = pltpu.SemaphoreType.DMA
        self.scratch = [dma((7,)), dma((7,)), dma]

    @staticmethod
    def _peer(f):
        x, y, c, me, chips = _place()
        return ((1 - x) if f & 4 else x, (1 - y) if f & 2 else y, (1 - c) if f & 1 else c)

    def start(self, ins, outs, sems):
        x, y, c, me, chips = _place()
        for f in range(1, 8):
            _remote(ins[0], outs[0].at[2 * me + c], sems[0].at[f - 1], sems[1].at[f - 1], self._peer(f)).start()
        pltpu.make_async_copy(ins[0], outs[0].at[2 * me + c], sems[2]).start()

    def finish(self, ins, outs, sems):
        x, y, c, me, chips = _place()
        for f in range(1, 8):
            px, py, pc = self._peer(f)
            blk = outs[0].at[4 * px + 2 * py + pc]
            _remote(blk, blk, sems[0].at[f - 1], sems[1].at[f - 1], (x, y, c)).wait_recv()
        for f in range(1, 8):
            _remote(ins[0], outs[0].at[2 * me + c], sems[0].at[f - 1], sems[1].at[f - 1], self._peer(f)).wait_send()
        pltpu.make_async_copy(ins[0], outs[0].at[2 * me + c], sems[2]).wait()


class _Both(_Exchange):
    def __init__(self, first, second):
        self.parts = (first, second)
        self.n_in, self.n_out = first.n_in + second.n_in, first.n_out + second.n_out
        self.out_shape = first.out_shape + second.out_shape
        self.scratch = first.scratch + second.scratch
        self.aliases = dict(first.aliases)
        self.aliases.update({first.n_in + i: first.n_out + o for i, o in second.aliases.items()})

    def _split(self, ins, outs, sems):
        a, b = self.parts
        return ((a, ins[:a.n_in], outs[:a.n_out], sems[:len(a.scratch)]),
                (b, ins[a.n_in:], outs[a.n_out:], sems[len(a.scratch):]))

    def start(self, ins, outs, sems):
        for ex, i, o, s in self._split(ins, outs, sems):
            ex.start(i, o, s)

    def middle(self, ins, outs, sems):
        for ex, i, o, s in self._split(ins, outs, sems):
            ex.middle(i, o, s)

    def finish(self, ins, outs, sems):
        for ex, i, o, s in self._split(ins, outs, sems):
            ex.finish(i, o, s)


class _Bound:
    def __init__(self, ex, ins, outs, sems):
        self.start = lambda: ex.start(ins, outs, sems)
        self.middle = lambda: ex.middle(ins, outs, sems)
        self.finish = lambda: ex.finish(ins, outs, sems)


def _carry(name, body, ex, ex_args, args, in_specs, out_specs, out_shape, scratch_shapes=(), grid=None, semantics=(),
           after=None):
    n_a, n_o, n_s = len(args), len(out_shape), len(scratch_shapes)
    behind = [] if after is None else [after]

    def full_body(*refs):
        p = 0
        groups = []
        for size in (n_a, ex.n_in, len(behind), n_o, ex.n_out, n_s, len(ex.scratch)):
            groups.append(refs[p:p + size])
            p += size
        a, ei, _, o, eo, s, es = groups
        body(*a, *o, *s, _Bound(ex, ei, eo, es))

    kwargs = {} if grid is None else {"grid": grid}
    outs = pl.pallas_call(
        full_body, name=name,
        in_specs=list(in_specs) + [ANY] * (ex.n_in + len(behind)), out_specs=list(out_specs) + [ANY] * ex.n_out,
        out_shape=list(out_shape) + list(ex.out_shape), scratch_shapes=list(scratch_shapes) + list(ex.scratch),
        input_output_aliases={n_a + i: n_o + o for i, o in ex.aliases.items()},
        compiler_params=_params(*semantics) if semantics else pltpu.CompilerParams(vmem_limit_bytes=VMEM_LIMIT),
        **kwargs,
    )(*args, *ex_args, *behind)
    return outs[:n_o], outs[n_o:]


def _prepare_carrying(name, x, g1, pos, ifc, arrays, ex, ex_args):
    n = len(arrays)
    r, cc = arrays[0].shape
    steps = 4
    tr, tm = r // steps, S // steps

    def body(x_ref, g_ref, pos_ref, ifc_ref, *refs):
        src, h_ref, cos_ref, sin_ref, dst, xc = refs[:n], refs[n], refs[n + 1], refs[n + 2], refs[n + 3:2 * n + 3], refs[-1]

        @pl.when(pl.program_id(0) == 0)
        def _():
            xc.start()

        xv = x_ref[...]
        h_ref[...] = (xv * _rstd(xv) * g_ref[...]).astype(BF16)
        ang = pos_ref[...].astype(F32) * ifc_ref[...]
        cos_ref[...] = jnp.cos(ang)
        sin_ref[...] = jnp.sin(ang)
        for a in range(n):
            dst[a][...] = src[a][...].astype(BF16)

        @pl.when(pl.program_id(0) == steps - 1)
        def _():
            xc.middle()
            xc.finish()

    row = lambda w: pl.BlockSpec((tm, w), lambda i: (i, 0))
    const = lambda w: pl.BlockSpec((1, w), lambda i: (0, 0))
    blk = pl.BlockSpec((tr, cc), lambda i: (i, 0))
    return _carry(name, body, ex, ex_args, (x, g1, pos, ifc, *arrays),
                  [row(D), const(D), row(1), const(128)] + [blk] * n,
                  [row(D), row(128), row(128)] + [blk] * n,
                  [jax.ShapeDtypeStruct((S, D), BF16)] + [jax.ShapeDtypeStruct((S, 128), F32)] * 2
                  + [jax.ShapeDtypeStruct((r, cc), BF16)] * n,
                  grid=(steps,), semantics=("arbitrary",))


def _exchange_alone(name, ex, ex_args):
    def body(xc):
        xc.start()
        xc.middle()
        xc.finish()

    return _carry(name, body, ex, ex_args, (), (), (), ())[1]


def _core_index():
    return lax.axis_index("c").astype(jnp.int32).reshape(1)


def _pair_sum(gs, gots):
    n = len(gs)

    def body(c_ref, *refs):
        for a in range(n):
            refs[2 * n + a][...] = (refs[a][...].astype(F32) + refs[n + a][...].astype(F32)).astype(BF16)

    mine = [pl.BlockSpec((None, g.shape[1] // 2, g.shape[2]), lambda k, c_ref: (k, c_ref[0], 0)) for g in gs]
    blk = [pl.BlockSpec((None, g.shape[1] // 2, g.shape[2]), lambda k, c_ref: (k, 0, 0)) for g in gs]
    return pl.pallas_call(
        body, name=f"pair_sum_{gs[0].shape[1]}x{gs[0].shape[2]}",
        grid_spec=pltpu.PrefetchScalarGridSpec(
            num_scalar_prefetch=1, grid=(N_CHIP,), in_specs=mine + blk, out_specs=blk),
        out_shape=[jax.ShapeDtypeStruct((N_CHIP, g.shape[1] // 2, g.shape[2]), BF16) for g in gs],
        compiler_params=_params("parallel"),
    )(_core_index(), *gs, *gots)


def _chip_sum(pre, parts):
    n = len(parts)
    me = 2 * lax.axis_index("x") + lax.axis_index("y")
    others = [k + (k >= me).astype(jnp.int32) for k in range(3)]
    where = jnp.stack([lax.axis_index("c"), me, *others]).astype(jnp.int32)

    def body(w_ref, *refs):
        for a in range(n):
            own, p1, p2, p3 = refs[4 * a:4 * a + 4]
            refs[4 * n + a][...] = ((own[...].astype(F32) + p1[...].astype(F32)) + p2[...].astype(F32)) + p3[...].astype(F32)

    in_specs, out_specs, operands = [], [], []
    for a in range(n):
        _, half, cc = parts[a].shape
        tr = half // 2
        in_specs += [pl.BlockSpec((None, tr, cc), lambda i, w_ref, s=s: (w_ref[s], i, 0)) for s in (1, 2, 3, 4)]
        out_specs.append(pl.BlockSpec((tr, cc), lambda i, w_ref: (2 * w_ref[0] + i, 0)))
        operands += [pre[a], parts[a], parts[a], parts[a]]
    return pl.pallas_call(
        body, name=f"chip_sum_{parts[0].shape[1]}x{parts[0].shape[2]}",
        grid_spec=pltpu.PrefetchScalarGridSpec(num_scalar_prefetch=1, grid=(2,), in_specs=in_specs, out_specs=out_specs),
        out_shape=[jax.ShapeDtypeStruct((2 * p.shape[1], p.shape[2]), F32) for p in parts],
        compiler_params=_params("parallel"),
    )(where, *operands)


def _adamw_math(w, g, m, v):
    m = ADAM_B1 * m + (1.0 - ADAM_B1) * g
    v = ADAM_B2 * v + (1.0 - ADAM_B2) * (g * g)
    m_hat = m / (1.0 - ADAM_B1 ** ADAM_STEP)
    v_hat = v / (1.0 - ADAM_B2 ** ADAM_STEP)
    delta = -ADAM_LR * (m_hat / (jnp.sqrt(v_hat) + ADAM_EPS) + ADAM_WD * w)
    return delta, m, v


def _adamw(ws, gs, ms, vs, after=None):
    n = len(ws)

    def body(*refs):
        for a in range(n):
            w_ref, g_ref, m_ref, v_ref = (refs[t * n + a] for t in range(4))
            go_ref, d_ref, nm_ref, nv_ref = refs[4 * n + 4 * a:4 * n + 4 * a + 4]
            g = g_ref[...]
            go_ref[...] = g
            d_ref[...], nm_ref[...], nv_ref[...] = _adamw_math(w_ref[...], g, m_ref[...], v_ref[...])

    blks = [pl.BlockSpec((w.shape[0] // 4, w.shape[1]), lambda i: (i, 0)) for w in ws]
    outs = _carry(f"adamw_{ws[0].shape[0]}x{ws[0].shape[1]}", body, _NoExchange(), (), (*ws, *gs, *ms, *vs),
                  blks * 4, [b for b in blks for _ in range(4)],
                  [jax.ShapeDtypeStruct(w.shape, F32) for w in ws for _ in range(4)],
                  grid=(4,), semantics=("parallel",), after=after)[0]
    return [outs[4 * a:4 * a + 4] for a in range(n)]


def _pack8(rows):
    def body(*refs):
        out_ref = refs[-1]
        out_ref[...] = jnp.zeros_like(out_ref)
        for i, r in enumerate(refs[:-1]):
            out_ref[i:i + 1, :] = r[...]

    return pl.pallas_call(body, name="pack8", out_shape=jax.ShapeDtypeStruct((8, D), F32))(*rows)


def _adamw_gains(gall, ws, ms, vs):
    def body(ga_ref, *refs):
        w, m, v = refs[0:4], refs[4:8], refs[8:12]
        outs, loss_ref, total = refs[12:28], refs[28], refs[29]
        g = ga_ref[0]
        for dev in range(1, 8):
            g = g + ga_ref[dev]
        total[...] = g
        for i in range(4):
            gi = total[i:i + 1, :]
            outs[i][...] = gi
            outs[4 + i][...], outs[8 + i][...], outs[12 + i][...] = _adamw_math(w[i][...], gi, m[i][...], v[i][...])
        loss_ref[...] = total[4:5, 0:128] * (0.5 / D)

    outs = pl.pallas_call(
        body, name="adamw_gains",
        out_shape=[jax.ShapeDtypeStruct((1, D), F32)] * 16 + [jax.ShapeDtypeStruct((1, 128), F32)],
        scratch_shapes=[pltpu.VMEM((8, D), F32)],
    )(gall, *ws, *ms, *vs)
    return outs[0:4], outs[4:8], outs[8:12], outs[12:16], outs[16]


def kernel(x, positions, w_in, w_out, g_pre_mix, g_post_mix, g_pre_ffn, g_post_ffn, w_gate, w_up, w_down, loss_target, m_w_in, m_w_out, m_g_pre_mix, m_g_post_mix, m_g_pre_ffn, m_g_post_ffn, m_w_gate, m_w_up, m_w_down, v_w_in, v_w_out, v_g_pre_mix, v_g_post_mix, v_g_pre_ffn, v_g_post_ffn, v_w_gate, v_w_up, v_w_down):
    tr = lambda t: jnp.swapaxes(t, 1, 2)[0]
    shards = [w_in[0], w_out[0], tr(w_gate), tr(w_up), w_down[0]]
    moms = [m_w_in[0], m_w_out[0], tr(m_w_gate), tr(m_w_up), m_w_down[0]]
    vels = [v_w_in[0], v_w_out[0], tr(v_w_gate), tr(v_w_up), v_w_down[0]]
    xs, pos, tgt = x[0], positions.reshape(S, 1), loss_target[0]
    g1, g2, g3, g4 = g_pre_mix, g_post_mix, g_pre_ffn, g_post_ffn
    tabs = tuple(jnp.asarray(t) for t in _retention_tables())
    ifc, spread = _rotary_tables()
    ifc, spread = jnp.asarray(ifc), jnp.asarray(spread, dtype=BF16)
    bf = [s.astype(BF16) for s in shards[:2]]

    (h1, cos, sin, *ffn_bf), (win_g,) = _prepare_carrying(
        "gather_in", xs, g1, pos, ifc, shards[2:], _GatherShards(bf[:1]), bf[:1])
    bf += list(ffn_bf)
    wout_gather = _GatherOverIci("wout_gather", bf[1:2])
    token = wout_gather.start(win_g)
    ffn_gather = _GatherOverIci("ffn_gather", bf[2:])
    token = ffn_gather.start(token)
    qr, kr, rv, rg, aq, ak, av = _proj_fwd(h1, win_g, cos, sin, spread, token)
    wout_sh, wout_land = wout_gather.wait(qr)
    n_ffn = len(bf[2:])
    (att_out, lse, cat_a), (wout_g, *ffn_gather.arrays[n_ffn:]) = _att_fwd(
        aq, ak, av, _Both(_ForwardGathered(bf[1:2]), _ForwardGathered(bf[2:], forward=False)),
        [*wout_sh, *wout_land, *ffn_gather.arrays])
    wout_g = wout_g.reshape(D, D)
    (o_raw, cat_r, states), _ = _ret_fwd(qr, kr, rv, rg, tabs, _NoExchange(), (), cat_a)
    ffn_sh, ffn_lands = ffn_gather.wait(cat_r)
    (mix, x2, h3), (wg_g, wu_g, wd_g) = _mix_fwd(cat_r, cat_a, wout_g, xs, g2, g3,
                                                _ForwardGathered(bf[2:], own=False), [*ffn_sh, *ffn_lands])
    gt, up, a, sq, dy, df, dg4 = _ffn_fwd(h3, wg_g, wu_g, wd_g, x2, tgt, g4)

    dgt, dup, dx2, dmix, dg3, dg2 = _ffn_bwd_act(df, gt, up, wg_g, wu_g, wd_g, dy, x2, mix, g2, g3)
    ffn_grads = list(_ffn_bwd_w(a, df, h3, dgt, dup))
    (dret, datt, dwout), got = _mix_bwd(dmix, cat_r, cat_a, wout_g, _HalvesToSibling(ffn_grads), ffn_grads)
    ffn_sum = _SumOverIci("ffn_sum", _pair_sum(ffn_grads, got))
    token = ffn_sum.start(datt)
    (dq_att, dk_att, dv_att), _ = _att_bwd(aq, ak, av, datt, att_out, lse, _NoExchange(), (), token)
    (dqr, dkr, drv, drg), _ = _ret_bwd(qr, kr, rv, rg, o_raw, states, dret, tabs, _NoExchange(), (), token)
    dproj = _rot_bwd(cos, sin, spread, dqr, dkr, drv, drg, dq_att, dk_att, dv_att)
    sums = _chip_sum(*ffn_sum.wait(dproj))
    dwin, ffn_full = _win_bwd_w(h1, dproj, _ShareHalves(sums), sums)
    in_grads = [dwin, dwout.reshape(N_CHIP, WOUT_R, D)]

    got = _exchange_alone("halves_to_sibling", _HalvesToSibling(in_grads), in_grads)
    in_sum = _SumOverIci("in_sum", _pair_sum(in_grads, got))
    token = in_sum.start(dproj)
    dx, dg1 = _in_bwd(dproj, win_g, xs, dx2, g1, token)
    ffn_upd = _adamw(shards[2:], [ffn_full[o] for o in (1, 2, 0)],
                     moms[2:], vels[2:], token)
    pre, parts = in_sum.wait(ffn_upd[2][0])
    sums = _chip_sum(pre, parts)
    gblock = _pack8([dg1, dg2, dg3, dg4, sq])
    *in_full, gall = _exchange_alone("share_rest", _Both(_ShareHalves(sums), _GatherBlocks(gblock)), [*sums, gblock])
    upd = _adamw(shards[:2], in_full, moms[:2], vels[:2]) + ffn_upd
    gg, gd, gm, gv, loss_row = _adamw_gains(gall, [g1, g2, g3, g4],
                                            [m_g_pre_mix, m_g_post_mix, m_g_pre_ffn, m_g_post_ffn],
                                            [v_g_pre_mix, v_g_post_mix, v_g_pre_ffn, v_g_post_ffn])

    def order(mats, vecs):
        back = lambda t: jnp.swapaxes(t[None], 1, 2)
        return [mats[0][None], mats[1][None], *vecs, back(mats[2]), back(mats[3]), mats[4][None]]

    return (loss_row[0, 0], dx[None],
            *order([u[0] for u in upd], gg),
            *order([u[1] for u in upd], gd),
            *order([u[2] for u in upd], gm),
            *order([u[3] for u in upd], gv))
```

```python
import numpy as np
import jax
import jax.numpy as jnp
from jax import lax
from jax.experimental import pallas as pl
from jax.experimental.pallas import tpu as pltpu

F32, BF16 = jnp.float32, jnp.bfloat16
MESH = pl.DeviceIdType.MESH

S = 2048
D = 1024
PW = 3072
N_CHIP = 4
WIN_C = PW // N_CHIP
DFF = 2816
FF_C = DFF // N_CHIP
WOUT_R = D // N_CHIP
RMS_EPS = 1e-6
GN_EPS = 1e-5
RET_C = 128
RET_PER_STEP = 4
RET_SCALE = 32 ** -0.5
ATT_BLK = 128
ATT_SCALE = 64 ** -0.5
PATTERN_DILATIONS = (16, 1, 4)
NEG = -1e30
VMEM_LIMIT = 56 * 1024 * 1024

ADAM_LR, ADAM_B1, ADAM_B2, ADAM_EPS, ADAM_WD, ADAM_STEP = 0.001, 0.9, 0.999, 1e-08, 0.01, 10


def _params(*sem):
    return pltpu.CompilerParams(dimension_semantics=sem, vmem_limit_bytes=VMEM_LIMIT)


def _nt(a, b):
    return lax.dot_general(a, b, (((1,), (1,)), ((), ())), preferred_element_type=F32)


def _tn(a, b):
    return lax.dot_general(a, b, (((0,), (0,)), ((), ())), preferred_element_type=F32)


def _nn(a, b):
    return jnp.dot(a, b, preferred_element_type=F32)


def _rstd(v):
    return lax.rsqrt(jnp.mean(v * v, axis=-1, keepdims=True) + RMS_EPS)


def _sigmoid(v):
    return 1.0 / (1.0 + jnp.exp(-v))


def _rows(i, t):
    return pl.ds(pl.multiple_of(i * t, t), t)


def _retention_tables():
    h = np.arange(8, dtype=np.float32)
    log_g = np.log1p(-np.exp2(-5.0 - h)).astype(np.float32)
    idx = np.arange(RET_C, dtype=np.float32)
    diff = idx[:, None] - idx[None, :]
    dtab = np.where(diff >= 0, np.exp(log_g[:, None, None] * np.maximum(diff, 0.0)), 0.0).astype(np.float32)
    dtab = dtab.reshape(8 * RET_C, RET_C)
    lane_head = np.arange(256) // 32
    a_tab = np.exp(log_g[lane_head][None, :] * (idx + 1.0)[:, None]).astype(np.float32)
    b_tab = np.exp(log_g[lane_head][None, :] * (RET_C - 1.0 - idx)[:, None]).astype(np.float32)
    lam = np.exp(log_g[lane_head] * RET_C).astype(np.float32)[:, None]
    bd = (lane_head[:, None] == (np.arange(512) // 64)[None, :]).astype(np.float32)
    return dtab, a_tab, b_tab, lam, bd


def _rotary_tables():
    inv_r = (1.0 / (np.float32(10000.0) ** np.linspace(0.0, 1.0, 16, dtype=np.float32))).astype(np.float32)
    inv_a = (np.float32(500000.0) ** (-np.arange(0, 16, 2, dtype=np.float32) / np.float32(16))).astype(np.float32)
    ifc = np.zeros((1, 128), np.float32)
    ifc[0, 0:16], ifc[0, 16:24] = inv_r, inv_a
    spread = np.zeros((128, 768), np.float32)
    for lane in range(256):
        spread[(lane % 32) % 16, lane] = 1.0
    for lane in range(512):
        d = lane % 64
        spread[16 + d % 8 if d < 16 else 24, 256 + lane] = 1.0
    return ifc, spread


def _rot_halves(tm):
    lo_r = (lax.broadcasted_iota(jnp.int32, (tm, 256), 1) % 32) < 16
    lo_a = (lax.broadcasted_iota(jnp.int32, (tm, 512), 1) % 64) < 8
    return lo_r, lo_a


def _spread_exact(t, e):
    hi = t.astype(BF16)
    r1 = t - hi.astype(F32)
    mid = r1.astype(BF16)
    lo = (r1 - mid.astype(F32)).astype(BF16)
    return _nn(hi, e) + _nn(mid, e) + _nn(lo, e)


def _rot_tables(cos_ref, sin_ref, e_ref):
    cs = _spread_exact(cos_ref[...], e_ref[...])
    sn = _spread_exact(sin_ref[...], e_ref[...])
    return cs[:, 0:256], cs[:, 256:768], sn[:, 0:256], sn[:, 256:768]


def _proj_fwd(h1, win_g, cos, sin, spread, after):
    tm = 256

    def body(h_ref, w_ref, cos_ref, sin_ref, e_ref, qr_ref, kr_ref, rv_ref, rg_ref, aq_ref, ak_ref, av_ref, p_ref, _):
        h = h_ref[...]
        for k in range(N_CHIP):
            p_ref[:, k * WIN_C:(k + 1) * WIN_C] = _nn(h, w_ref[k])
        cr, ca, sr, sa = _rot_tables(cos_ref, sin_ref, e_ref)
        lo_r, lo_a = _rot_halves(tm)

        def rot_r(v):
            return v * cr + sr * jnp.where(lo_r, -pltpu.roll(v, 240, 1), pltpu.roll(v, 16, 1))

        def rot_a(v):
            return v * ca + sa * jnp.where(lo_a, -pltpu.roll(v, 504, 1), pltpu.roll(v, 8, 1))

        qr_ref[...] = rot_r(p_ref[:, 0:256]).astype(BF16)
        kr_ref[...] = (rot_r(p_ref[:, 256:512]) * RET_SCALE).astype(BF16)
        rv_ref[...] = p_ref[:, 512:1024].astype(BF16)
        rg_ref[...] = p_ref[:, 1024:1536]
        aq, ak = rot_a(p_ref[:, 1536:2048]), rot_a(p_ref[:, 2048:2560])
        for j in range(4):
            aq_ref[j] = aq[:, 128 * j:128 * j + 128]
            ak_ref[j] = ak[:, 128 * j:128 * j + 128]
            av_ref[j] = p_ref[:, 2560 + 128 * j:2560 + 128 * j + 128]

    row = lambda w: pl.BlockSpec((tm, w), lambda i: (i, 0))
    slab = pl.BlockSpec((4, tm, 128), lambda i: (0, i, 0))
    return _carry(
        "proj_fwd", body, _NoExchange(), (), (h1, win_g, cos, sin, spread),
        [row(D), pl.BlockSpec((N_CHIP, D, WIN_C), lambda i: (0, 0, 0)), row(128), row(128),
         pl.BlockSpec((128, 768), lambda i: (0, 0))],
        [row(256), row(256), row(512), row(512), slab, slab, slab],
        [jax.ShapeDtypeStruct((S, w), BF16) for w in (256, 256, 512)]
        + [jax.ShapeDtypeStruct((S, 512), F32)] + [jax.ShapeDtypeStruct((4, S, 128), F32)] * 3,
        scratch_shapes=[pltpu.VMEM((tm, PW), F32)], grid=(S // tm,), semantics=("parallel",), after=after)[0]


def _seg_mean(v):
    lo = lax.broadcasted_iota(jnp.int32, v.shape, 1) < 64
    s_lo = jnp.sum(jnp.where(lo, v, 0.0), axis=-1, keepdims=True)
    s_hi = jnp.sum(jnp.where(lo, 0.0, v), axis=-1, keepdims=True)
    return jnp.where(lo, s_lo, s_hi) * (1.0 / 64.0)


def _ret_fwd(qr, kr, rv, proj, tabs, exchange, exchange_args, after=None):
    C, G = RET_C, RET_PER_STEP
    steps = S // (C * G)
    dtab, a_tab, b_tab, lam, bd = tabs

    def body(q_ref, k_ref, v_ref, g_ref, dt_ref, a_ref, b_ref, lam_ref, bd_ref, o_ref, cat_ref, st_ref, R, exch):
        @pl.when(pl.program_id(0) == 0)
        def _():
            exch.start()
            R[...] = jnp.zeros_like(R)

        lane_head = lax.broadcasted_iota(jnp.int32, (C, 256), 1) // 32
        col_head = lax.broadcasted_iota(jnp.int32, (C, 256), 1) // 64
        for s in range(G):
            rows = slice(s * C, (s + 1) * C)
            q, k, v = q_ref[rows, :], k_ref[rows, :], v_ref[rows, :]
            rb = R[...].astype(BF16)
            st_ref[s] = rb
            qa = (q.astype(F32) * a_ref[...]).astype(BF16)
            cross = _nn(qa, rb)
            p = (_nt(_stack_heads(q, lane_head, n=8), k) * dt_ref[...]).astype(BF16)
            og = [cross[:, 256 * g:256 * g + 256]
                  + _unstack_heads(_nn(p[4 * C * g:4 * C * (g + 1)], v[:, 256 * g:256 * g + 256]), col_head)
                  for g in range(2)]
            kb = (k.astype(F32) * b_ref[...]).astype(BF16)
            R[...] = R[...] * lam_ref[...] + _tn(kb, v) * bd_ref[...]
            o_ref[rows, 0:256] = og[0]
            o_ref[rows, 256:512] = og[1]
            for j in range(4):
                oj = og[j // 2][:, 128 * (j % 2):128 * (j % 2) + 128]
                xc = oj - _seg_mean(oj)
                rn = xc * lax.rsqrt(_seg_mean(xc * xc) + GN_EPS)
                gj = g_ref[rows, 128 * j:128 * j + 128]
                cat_ref[rows, 128 * j:128 * j + 128] = (rn * (gj * _sigmoid(gj))).astype(BF16)

        @pl.when(pl.program_id(0) == steps - 1)
        def _():
            exch.middle()
            exch.finish()

    row = lambda w: pl.BlockSpec((C * G, w), lambda n: (n, 0))
    full = lambda a: pl.BlockSpec(a.shape, lambda n: (0,) * a.ndim)
    return _carry(
        "ret_fwd", body, exchange, exchange_args, (qr, kr, rv, proj, dtab, a_tab, b_tab, lam, bd),
        [row(256), row(256), row(512), row(512),
         full(dtab), full(a_tab), full(b_tab), full(lam), full(bd)],
        [row(512), row(512), pl.BlockSpec((G, 256, 512), lambda n: (n, 0, 0))],
        [jax.ShapeDtypeStruct((S, 512), F32), jax.ShapeDtypeStruct((S, 512), BF16),
         jax.ShapeDtypeStruct((S // C, 256, 512), BF16)],
        scratch_shapes=[pltpu.VMEM((256, 512), F32)], grid=(steps,), semantics=("arbitrary",), after=after)


def _stack_heads(v, lane_head, fill=0.0, n=4):
    return jnp.concatenate([jnp.where(lane_head == h, v, jnp.full_like(v, fill)) for h in range(n)], axis=0)


def _unstack_heads(v, lane_head, n=4):
    out = v[0:ATT_BLK]
    for h in range(1, n):
        out = jnp.where(lane_head == h, v[h * ATT_BLK:(h + 1) * ATT_BLK], out)
    return out


def _att_bias(has_prev):
    nk = 2 * ATT_BLK if has_prev else ATT_BLK
    a = lax.broadcasted_iota(jnp.int32, (4 * ATT_BLK, nk), 0) % ATT_BLK
    kk = lax.broadcasted_iota(jnp.int32, (4 * ATT_BLK, nk), 1)
    if not has_prev:
        return None, jnp.where((a - kk) >= 0, 0.0, NEG)
    dist = ATT_BLK + a - kk
    inside = (dist >= 0) & (dist <= ATT_BLK)
    return jnp.where(inside, 0.0, NEG), jnp.where(inside & (kk >= ATT_BLK), 0.0, NEG)


def _class_rows(ib, r, d):
    if d == 1:
        return pl.ds(pl.multiple_of(ib * ATT_BLK, ATT_BLK), ATT_BLK)
    return pl.ds(ib * ATT_BLK * d + r, ATT_BLK, stride=d)


def _slab_pair(ref, g, rows):
    return jnp.concatenate([ref[2 * g, rows, :], ref[2 * g + 1, rows, :]], axis=1)


def _att_blocks(d):
    nb = S // d // ATT_BLK
    return nb, nb > 1


def _att_fwd(aq, ak, av, exchange, exchange_args):
    def body(q_ref, k_ref, v_ref, o_ref, l_ref, cat_ref, xc):
        xc.start()
        lane_head = lax.broadcasted_iota(jnp.int32, (ATT_BLK, 256), 1) // 64
        for pi, d in enumerate(PATTERN_DILATIONS):
            if pi == len(PATTERN_DILATIONS) - 1:
                xc.middle()
            nb, has_prev = _att_blocks(d)
            bias_rest, bias_first = _att_bias(has_prev)

            def block(b, carry, pi=pi, d=d, nb=nb, has_prev=has_prev, bias_rest=bias_rest, bias_first=bias_first):
                r, ib = b // nb, b % nb
                rows = _class_rows(ib, r, d)
                prow = _class_rows(jnp.maximum(ib - 1, 0), r, d)
                bias = jnp.where(ib == 0, bias_first, bias_rest) if has_prev else bias_first
                for g in range(2):
                    qg = _slab_pair(q_ref, g, rows).astype(BF16)
                    kg = _slab_pair(k_ref, g, rows)
                    vg = _slab_pair(v_ref, g, rows)
                    if has_prev:
                        kg = jnp.concatenate([_slab_pair(k_ref, g, prow), kg], axis=0)
                        vg = jnp.concatenate([_slab_pair(v_ref, g, prow), vg], axis=0)
                    kg, vg = kg.astype(BF16), vg.astype(BF16)
                    s = _nt(_stack_heads(qg, lane_head), kg) * ATT_SCALE + bias
                    m = jnp.max(s, axis=-1, keepdims=True)
                    p = jnp.exp(s - m)
                    den = jnp.sum(p, axis=-1, keepdims=True)
                    og = _unstack_heads(_nn(p.astype(BF16), vg) / den, lane_head)
                    lg = _unstack_heads(jnp.broadcast_to(m + jnp.log(den), (4 * ATT_BLK, 256)), lane_head)
                    for jj in range(2):
                        j = 2 * g + jj
                        o_new, l_new = og[:, 128 * jj:128 * jj + 128], lg[:, 128 * jj:128 * jj + 128]
                        if pi > 0:
                            o_old, l_old = o_ref[j, rows, :], l_ref[j, rows, :]
                            mx = jnp.maximum(l_old, l_new)
                            ea, eb = jnp.exp(l_old - mx), jnp.exp(l_new - mx)
                            den = ea + eb
                            o_new = (ea * o_old + eb * o_new) / den
                            l_new = mx + jnp.log(den)
                        o_ref[j, rows, :] = o_new
                        l_ref[j, rows, :] = l_new
                return carry

            lax.fori_loop(0, S // ATT_BLK, block, 0, unroll=4)

        def to_cat(i, carry):
            rows = _rows(i, 256)
            for j in range(4):
                cat_ref[rows, 128 * j:128 * j + 128] = o_ref[j, rows, :].astype(BF16)
            return carry

        lax.fori_loop(0, S // 256, to_cat, 0)
        xc.finish()

    slab = jax.ShapeDtypeStruct((4, S, 128), F32)
    return _carry("att_fwd", body, exchange, exchange_args, (aq, ak, av), [VMEM] * 3, [VMEM] * 3,
                  [slab, slab, jax.ShapeDtypeStruct((S, 512), BF16)])


def _mix_fwd(cat_r, cat_a, wout, x, g2, g3, exchange, exchange_args):
    tm = 512

    def body(cr_ref, ca_ref, w_ref, x_ref, g2_ref, g3_ref, mix_ref, x2_ref, h3_ref, xc):
        @pl.when(pl.program_id(0) == 0)
        def _():
            xc.start()

        mix = _nn(cr_ref[...], w_ref[0:512, :]) + _nn(ca_ref[...], w_ref[512:1024, :])
        mix_ref[...] = mix
        x2 = x_ref[...] + mix * _rstd(mix) * g2_ref[...]
        x2_ref[...] = x2
        h3_ref[...] = (x2 * _rstd(x2) * g3_ref[...]).astype(BF16)

        @pl.when(pl.program_id(0) == S // tm - 1)
        def _():
            xc.middle()
            xc.finish()

    row = lambda w: pl.BlockSpec((tm, w), lambda i: (i, 0))
    vec = pl.BlockSpec((1, D), lambda i: (0, 0))
    return _carry("mix_fwd", body, exchange, exchange_args, (cat_r, cat_a, wout, x, g2, g3),
                  [row(512), row(512), pl.BlockSpec((D, D), lambda i: (0, 0)), row(D), vec, vec],
                  [row(D), row(D), row(D)],
                  [jax.ShapeDtypeStruct((S, D), F32), jax.ShapeDtypeStruct((S, D), F32),
                   jax.ShapeDtypeStruct((S, D), BF16)],
                  grid=(S // tm,), semantics=("arbitrary",))


def _ffn_fwd(h3, wg, wu, wd, x2, tgt, g4):
    tm = 512
    last = N_CHIP - 1

    def body(h_ref, wg_ref, wu_ref, wd_ref, x2_ref, t_ref, g_ref,
             gt_ref, up_ref, a_ref, loss_ref, dy_ref, df_ref, dg_ref, f_ref):
        k, i = pl.program_id(0), pl.program_id(1)
        h = h_ref[...]
        gt = _nt(h, wg_ref[...])
        up = _nt(h, wu_ref[...])
        gt_ref[...] = gt.astype(BF16)
        up_ref[...] = up.astype(BF16)
        a = (gt * _sigmoid(gt) * up).astype(BF16)
        a_ref[...] = a
        part = _nn(a, wd_ref[...])
        rows = _rows(i, tm)

        @pl.when(k == 0)
        def _():
            f_ref[rows, :] = part

        @pl.when((k > 0) & (k < last))
        def _():
            f_ref[rows, :] = f_ref[rows, :] + part

        @pl.when((k == last) & (i == 0))
        def _():
            loss_ref[...] = jnp.zeros_like(loss_ref)
            dg_ref[...] = jnp.zeros_like(dg_ref)

        @pl.when(k == last)
        def _():
            fv = f_ref[rows, :] + part
            r = _rstd(fv)
            fn = fv * r
            e = x2_ref[...] + fn * g_ref[...] - t_ref[...]
            loss_ref[...] = loss_ref[...] + jnp.sum(jnp.sum(e * e, axis=-1, keepdims=True), axis=0, keepdims=True)
            dy = e * (1.0 / D)
            dy_ref[...] = dy
            dg_ref[...] = dg_ref[...] + jnp.sum(dy * fn, axis=0, keepdims=True)
            t = dy * g_ref[...]
            df_ref[...] = (r * (t - fn * jnp.mean(t * fn, axis=-1, keepdims=True))).astype(BF16)

    wrow = pl.BlockSpec((None, FF_C, D), lambda k, i: (k, 0, 0))
    act = pl.BlockSpec((None, tm, FF_C), lambda k, i: (k, i, 0))
    late = pl.BlockSpec((tm, D), lambda k, i: (jnp.where(k == last, i, 0), 0))
    vec = pl.BlockSpec((1, D), lambda k, i: (0, 0))
    return pl.pallas_call(
        body, grid=(N_CHIP, S // tm), name="ffn_fwd",
        in_specs=[pl.BlockSpec((tm, D), lambda k, i: (i, 0)), wrow, wrow, wrow, late, late, vec],
        out_specs=[act, act, act, vec, late, late, vec],
        out_shape=[jax.ShapeDtypeStruct((N_CHIP, S, FF_C), BF16)] * 3
                  + [jax.ShapeDtypeStruct((1, D), F32), jax.ShapeDtypeStruct((S, D), F32),
                     jax.ShapeDtypeStruct((S, D), BF16), jax.ShapeDtypeStruct((1, D), F32)],
        scratch_shapes=[pltpu.VMEM((S, D), F32)],
        compiler_params=_params("arbitrary", "arbitrary"),
    )(h3, wg, wu, wd, x2, tgt, g4)


def _ffn_bwd_act(df, gt, up, wg, wu, wd, dy, x2, mix, g2, g3):
    tm, sub = 512, 256
    last = N_CHIP - 1

    def body(df_ref, gt_ref, up_ref, wg_ref, wu_ref, wd_ref, dy_ref, x2_ref, mix_ref, g2_ref, g3_ref,
             dgt_ref, dup_ref, dx2_ref, dmix_ref, dg3_ref, dg2_ref, dh_ref):
        k, i = pl.program_id(0), pl.program_id(1)
        parts = []
        for s in range(tm // sub):
            rows = slice(s * sub, (s + 1) * sub)
            da = _nt(df_ref[rows, :], wd_ref[...])
            gt, up = gt_ref[rows, :].astype(F32), up_ref[rows, :].astype(F32)
            sg = _sigmoid(gt)
            dup = (da * gt * sg).astype(BF16)
            dgt = (da * up * (sg * (1.0 + gt * (1.0 - sg)))).astype(BF16)
            dup_ref[rows, :] = dup
            dgt_ref[rows, :] = dgt
            parts.append(_nn(dgt, wg_ref[...]) + _nn(dup, wu_ref[...]))
        part = jnp.concatenate(parts, axis=0)
        rows = _rows(i, tm)

        @pl.when(k == 0)
        def _():
            dh_ref[rows, :] = part

        @pl.when((k > 0) & (k < last))
        def _():
            dh_ref[rows, :] = dh_ref[rows, :] + part

        @pl.when((k == last) & (i == 0))
        def _():
            dg3_ref[...] = jnp.zeros_like(dg3_ref)
            dg2_ref[...] = jnp.zeros_like(dg2_ref)

        @pl.when(k == last)
        def _():
            dh = dh_ref[rows, :] + part
            x2 = x2_ref[...]
            r3 = _rstd(x2)
            xn = x2 * r3
            dg3_ref[...] = dg3_ref[...] + jnp.sum(dh * xn, axis=0, keepdims=True)
            t = dh * g3_ref[...]
            dx2 = dy_ref[...] + r3 * (t - xn * jnp.mean(t * xn, axis=-1, keepdims=True))
            dx2_ref[...] = dx2
            mix = mix_ref[...]
            r2 = _rstd(mix)
            mn = mix * r2
            dg2_ref[...] = dg2_ref[...] + jnp.sum(dx2 * mn, axis=0, keepdims=True)
            u = dx2 * g2_ref[...]
            dmix_ref[...] = (r2 * (u - mn * jnp.mean(u * mn, axis=-1, keepdims=True))).astype(BF16)

    wrow = pl.BlockSpec((None, FF_C, D), lambda k, i: (k, 0, 0))
    act = pl.BlockSpec((None, tm, FF_C), lambda k, i: (k, i, 0))
    row = pl.BlockSpec((tm, D), lambda k, i: (i, 0))
    late = pl.BlockSpec((tm, D), lambda k, i: (jnp.where(k == last, i, 0), 0))
    vec = pl.BlockSpec((1, D), lambda k, i: (0, 0))
    return pl.pallas_call(
        body, grid=(N_CHIP, S // tm), name="ffn_bwd_act",
        in_specs=[row, act, act, wrow, wrow, wrow, late, late, late, vec, vec],
        out_specs=[act, act, late, late, vec, vec],
        out_shape=[jax.ShapeDtypeStruct((N_CHIP, S, FF_C), BF16), jax.ShapeDtypeStruct((N_CHIP, S, FF_C), BF16),
                   jax.ShapeDtypeStruct((S, D), F32), jax.ShapeDtypeStruct((S, D), BF16),
                   jax.ShapeDtypeStruct((1, D), F32), jax.ShapeDtypeStruct((1, D), F32)],
        scratch_shapes=[pltpu.VMEM((S, D), F32)],
        compiler_params=_params("arbitrary", "arbitrary"),
    )(df, gt, up, wg, wu, wd, dy, x2, mix, g2, g3)


def _ffn_bwd_w(a, df, h3, dgt, dup):
    tm = 1024
    assert S // tm == 2

    def body(a_ref, df_ref, h_ref, dgt_ref, dup_ref, dwd_ref, dwg_ref, dwu_ref, acc_d, acc_g, acc_u):
        i = pl.program_id(1)
        h = h_ref[...]
        parts = (_tn(a_ref[...], df_ref[...]), _tn(dgt_ref[...], h), _tn(dup_ref[...], h))

        @pl.when(i == 0)
        def _():
            for acc, part in zip((acc_d, acc_g, acc_u), parts):
                acc[...] = part

        @pl.when(i == S // tm - 1)
        def _():
            for out, acc, part in zip((dwd_ref, dwg_ref, dwu_ref), (acc_d, acc_g, acc_u), parts):
                out[...] = (acc[...] + part).astype(BF16)

    act = pl.BlockSpec((None, tm, FF_C), lambda k, i: (k, i, 0))
    row = pl.BlockSpec((tm, D), lambda k, i: (i, 0))
    wrow = pl.BlockSpec((None, FF_C, D), lambda k, i: (k, 0, 0))
    return pl.pallas_call(
        body, grid=(N_CHIP, S // tm), name="ffn_bwd_w",
        in_specs=[act, row, row, act, act],
        out_specs=[wrow, wrow, wrow],
        out_shape=[jax.ShapeDtypeStruct((N_CHIP, FF_C, D), BF16)] * 3,
        scratch_shapes=[pltpu.VMEM((FF_C, D), F32)] * 3,
        compiler_params=_params("parallel", "arbitrary"),
    )(a, df, h3, dgt, dup)


def _mix_bwd(dmix, cat_r, cat_a, wout, exchange, exchange_args):
    tm = 1024

    def body(dm_ref, cr_ref, ca_ref, w_ref, dret_ref, datt_ref, dw_ref, acc, xc):
        i = pl.program_id(0)

        @pl.when(i == 0)
        def _():
            xc.start()
            acc[...] = jnp.zeros_like(acc)

        dm = dm_ref[...]
        dret_ref[...] = _nt(dm, w_ref[0:512, :])
        datt = _nt(dm, w_ref[512:1024, :])
        for j in range(4):
            datt_ref[j] = datt[:, 128 * j:128 * j + 128]
        acc[0:512, :] += _tn(cr_ref[...], dm)
        acc[512:1024, :] += _tn(ca_ref[...], dm)

        @pl.when(i == S // tm - 1)
        def _():
            dw_ref[...] = acc[...].astype(BF16)
            xc.middle()
            xc.finish()

    row = lambda w: pl.BlockSpec((tm, w), lambda i: (i, 0))
    full = pl.BlockSpec((D, D), lambda i: (0, 0))
    return _carry("mix_bwd", body, exchange, exchange_args, (dmix, cat_r, cat_a, wout),
                  [row(D), row(512), row(512), full],
                  [row(512), pl.BlockSpec((4, tm, 128), lambda i: (0, i, 0)), full],
                  [jax.ShapeDtypeStruct((S, 512), F32), jax.ShapeDtypeStruct((4, S, 128), F32),
                   jax.ShapeDtypeStruct((D, D), BF16)],
                  scratch_shapes=[pltpu.VMEM((D, D), F32)], grid=(S // tm,), semantics=("arbitrary",))


def _att_bwd(aq, ak, av, datt, att_out, lse, exchange, exchange_args, after=None):
    def body(q_ref, k_ref, v_ref, do_ref, out_ref, l_ref, dq_ref, dk_ref, dv_ref, xc):
        xc.start()

        lane_head = lax.broadcasted_iota(jnp.int32, (ATT_BLK, 256), 1) // 64
        for pi, d in enumerate(PATTERN_DILATIONS):
            nb, has_prev = _att_blocks(d)
            assert pi > 0 or not has_prev
            bias_rest, bias_first = _att_bias(has_prev)

            def block(b, carry, pi=pi, d=d, nb=nb, has_prev=has_prev, bias_rest=bias_rest, bias_first=bias_first):
                r, ib = b // nb, b % nb
                rows = _class_rows(ib, r, d)
                prow = _class_rows(jnp.maximum(ib - 1, 0), r, d)
                bias = jnp.where(ib == 0, bias_first, bias_rest) if has_prev else bias_first
                for g in range(2):
                    qg = _slab_pair(q_ref, g, rows).astype(BF16)
                    kg = _slab_pair(k_ref, g, rows)
                    vg = _slab_pair(v_ref, g, rows)
                    if has_prev:
                        kg = jnp.concatenate([_slab_pair(k_ref, g, prow), kg], axis=0)
                        vg = jnp.concatenate([_slab_pair(v_ref, g, prow), vg], axis=0)
                    kg, vg = kg.astype(BF16), vg.astype(BF16)
                    dog = _slab_pair(do_ref, g, rows)
                    outg = _slab_pair(out_ref, g, rows)
                    lg = _slab_pair(l_ref, g, rows)
                    qs = _stack_heads(qg, lane_head)
                    dos = _stack_heads(dog, lane_head)
                    delta = jnp.sum(dos * jnp.concatenate([outg] * 4, axis=0), axis=-1, keepdims=True)
                    lh = jnp.max(_stack_heads(lg, lane_head, NEG), axis=-1, keepdims=True)
                    s = _nt(qs, kg) * ATT_SCALE + bias
                    p = jnp.exp(s - lh)
                    dosb = dos.astype(BF16)
                    ds = (p * (_nt(dosb, vg) - delta) * ATT_SCALE).astype(BF16)
                    dq = _unstack_heads(_nn(ds, kg), lane_head)
                    dk = _tn(ds, qs)
                    dv = _tn(p.astype(BF16), dosb)
                    for jj in range(2):
                        j, sl = 2 * g + jj, slice(128 * jj, 128 * jj + 128)
                        if pi == 0:
                            dq_ref[j, rows, :] = dq[:, sl]
                            dk_ref[j, rows, :] = dk[:, sl]
                            dv_ref[j, rows, :] = dv[:, sl]
                            continue
                        dq_ref[j, rows, :] += dq[:, sl]
                        if has_prev:
                            dk_ref[j, prow, :] += dk[0:ATT_BLK, sl]
                            dv_ref[j, prow, :] += dv[0:ATT_BLK, sl]
                            dk_ref[j, rows, :] += dk[ATT_BLK:2 * ATT_BLK, sl]
                            dv_ref[j, rows, :] += dv[ATT_BLK:2 * ATT_BLK, sl]
                        else:
                            dk_ref[j, rows, :] += dk[:, sl]
                            dv_ref[j, rows, :] += dv[:, sl]
                return carry

            lax.fori_loop(0, S // ATT_BLK, block, 0, unroll=4)
        xc.middle()
        xc.finish()

    slab = jax.ShapeDtypeStruct((4, S, 128), F32)
    return _carry("att_bwd", body, exchange, exchange_args, (aq, ak, av, datt, att_out, lse), [VMEM] * 6, [VMEM] * 3,
                  [slab, slab, slab], after=after)


def _ret_bwd(qr, kr, rv, proj, o_raw, states, dret, tabs, exchange, exchange_args, after=None):
    C, G = RET_C, RET_PER_STEP
    steps = S // (C * G)
    dtab, a_tab, b_tab, lam, bd = tabs

    def body(q_ref, k_ref, v_ref, g_ref, o_ref, st_ref, dr_ref, dt_ref, a_ref, b_ref, lam_ref, bd_ref,
             dq_ref, dk_ref, dv_ref, dg_ref, dR, exch):
        @pl.when(pl.program_id(0) == 0)
        def _():
            exch.start()
            dR[...] = jnp.zeros_like(dR)

        lane_head = lax.broadcasted_iota(jnp.int32, (C, 256), 1) // 32
        col_head = lax.broadcasted_iota(jnp.int32, (C, 256), 1) // 64
        for s in reversed(range(G)):
            rows = slice(s * C, (s + 1) * C)
            q, k, v = q_ref[rows, :], k_ref[rows, :], v_ref[rows, :]
            dos = []
            for j in range(4):
                sl = slice(128 * j, 128 * j + 128)
                oj = o_ref[rows, sl]
                xc = oj - _seg_mean(oj)
                rs = lax.rsqrt(_seg_mean(xc * xc) + GN_EPS)
                rn = xc * rs
                gj = g_ref[rows, sl]
                sg = _sigmoid(gj)
                dret = dr_ref[rows, sl]
                dg_ref[rows, sl] = dret * rn * (sg * (1.0 + gj * (1.0 - sg)))
                drn = dret * (gj * sg)
                dos.append(rs * (drn - _seg_mean(drn) - rn * _seg_mean(drn * rn)))
            do = [jnp.concatenate(dos[0:2], axis=1), jnp.concatenate(dos[2:4], axis=1)]
            do8 = jnp.concatenate(do, axis=1).astype(BF16)
            drb = dR[...].astype(BF16)
            rb = st_ref[s]
            dq = _nt(do8, rb) * a_ref[...]
            dk = _nt(v, drb) * b_ref[...]
            kb = (k.astype(F32) * b_ref[...]).astype(BF16)
            dvall = _nn(kb, drb)
            qs = _stack_heads(q, lane_head, n=8)
            dec = dt_ref[...]
            p = (_nt(qs, k) * dec).astype(BF16)
            dos = [_stack_heads(do[g], col_head).astype(BF16) for g in range(2)]
            dp = jnp.concatenate([_nt(dos[g], v[:, 256 * g:256 * g + 256]) for g in range(2)], axis=0)
            ds = (dp * dec).astype(BF16)
            dq = dq + _unstack_heads(_nn(ds, k), lane_head, n=8)
            dk = dk + _tn(ds, qs)
            dv = [dvall[:, 256 * g:256 * g + 256] + _tn(p[4 * C * g:4 * C * (g + 1)], dos[g]) for g in range(2)]
            qa = (q.astype(F32) * a_ref[...]).astype(BF16)
            dR[...] = dR[...] * lam_ref[...] + _tn(qa, do8) * bd_ref[...]
            dq_ref[rows, :] = dq
            dk_ref[rows, :] = dk
            dv_ref[rows, 0:256] = dv[0]
            dv_ref[rows, 256:512] = dv[1]

        @pl.when(pl.program_id(0) == steps - 1)
        def _():
            exch.middle()
            exch.finish()

    rev = lambda w: pl.BlockSpec((C * G, w), lambda n: (steps - 1 - n, 0))
    full = lambda a: pl.BlockSpec(a.shape, lambda n: (0,) * a.ndim)
    return _carry(
        "ret_bwd", body, exchange, exchange_args, (qr, kr, rv, proj, o_raw, states, dret, dtab, a_tab, b_tab, lam, bd),
        [rev(256), rev(256), rev(512), rev(512), rev(512),
         pl.BlockSpec((G, 256, 512), lambda n: (steps - 1 - n, 0, 0)), rev(512),
         full(dtab), full(a_tab), full(b_tab), full(lam), full(bd)],
        [rev(256), rev(256), rev(512), rev(512)],
        [jax.ShapeDtypeStruct((S, 256), F32), jax.ShapeDtypeStruct((S, 256), F32),
         jax.ShapeDtypeStruct((S, 512), F32), jax.ShapeDtypeStruct((S, 512), F32)],
        scratch_shapes=[pltpu.VMEM((256, 512), F32)], grid=(steps,), semantics=("arbitrary",), after=after)


def _rot_bwd(cos, sin, spread, dqr, dkr, drv, drg, dq_att, dk_att, dv_att):
    tm = 256

    def body(cos_ref, sin_ref, e_ref, dqr_ref, dkr_ref, drv_ref, drg_ref, dqa_ref, dka_ref, dva_ref, dp_ref):
        cr, ca, sr, sa = _rot_tables(cos_ref, sin_ref, e_ref)
        lo_r, lo_a = _rot_halves(tm)

        def unrot_r(g):
            gs = g * sr
            return g * cr + pltpu.roll(jnp.where(lo_r, -gs, 0.0), 16, 1) + pltpu.roll(jnp.where(lo_r, 0.0, gs), 240, 1)

        def unrot_a(g):
            gs = g * sa
            return g * ca + pltpu.roll(jnp.where(lo_a, -gs, 0.0), 8, 1) + pltpu.roll(jnp.where(lo_a, 0.0, gs), 504, 1)

        def wide(ref):
            return jnp.concatenate([ref[j] for j in range(4)], axis=1)

        dp_ref[:, 0:256] = unrot_r(dqr_ref[...]).astype(BF16)
        dp_ref[:, 256:512] = unrot_r(dkr_ref[...] * RET_SCALE).astype(BF16)
        dp_ref[:, 512:1024] = drv_ref[...].astype(BF16)
        dp_ref[:, 1024:1536] = drg_ref[...].astype(BF16)
        dp_ref[:, 1536:2048] = unrot_a(wide(dqa_ref)).astype(BF16)
        dp_ref[:, 2048:2560] = unrot_a(wide(dka_ref)).astype(BF16)
        dp_ref[:, 2560:3072] = wide(dva_ref).astype(BF16)

    row = lambda w: pl.BlockSpec((tm, w), lambda i: (i, 0))
    slab = pl.BlockSpec((4, tm, 128), lambda i: (0, i, 0))
    return pl.pallas_call(
        body, grid=(S // tm,), name="rot_bwd",
        in_specs=[row(128), row(128), pl.BlockSpec((128, 768), lambda i: (0, 0)),
                  row(256), row(256), row(512), row(512), slab, slab, slab],
        out_specs=row(PW), out_shape=jax.ShapeDtypeStruct((S, PW), BF16),
        compiler_params=_params("parallel"),
    )(cos, sin, spread, dqr, dkr, drv, drg, dq_att, dk_att, dv_att)


def _win_bwd_w(h1, dproj, exchange, exchange_args):
    def body(h_ref, dp_ref, dw_ref, xc):
        k = pl.program_id(0)

        @pl.when(k == 0)
        def _():
            xc.start()

        dw_ref[...] = _tn(h_ref[...], dp_ref[...]).astype(BF16)

        @pl.when(k == N_CHIP - 1)
        def _():
            xc.middle()
            xc.finish()

    (dw,), out = _carry(
        "win_bwd_w", body, exchange, exchange_args, (h1, dproj),
        [pl.BlockSpec((S, D), lambda k: (0, 0)), pl.BlockSpec((S, WIN_C), lambda k: (0, k))],
        [pl.BlockSpec((None, D, WIN_C), lambda k: (k, 0, 0))],
        [jax.ShapeDtypeStruct((N_CHIP, D, WIN_C), BF16)], grid=(N_CHIP,), semantics=("arbitrary",))
    return dw, out


def _in_bwd(dproj, win_g, x, dx2, g1, other_rows, after):
    tm = 512
    n = len(other_rows)

    def body(dp_ref, w_ref, x_ref, dx2_ref, g_ref, *refs):
        rows, dx_ref, blk_ref = refs[:n], refs[n], refs[n + 1]

        @pl.when(pl.program_id(0) == 0)
        def _():
            blk_ref[...] = jnp.zeros_like(blk_ref)
            for i, r_ref in enumerate(rows):
                blk_ref[i + 1:i + 2, :] = r_ref[...]

        dh = _nt(dp_ref[:, 0:WIN_C], w_ref[0])
        for k in range(1, N_CHIP):
            dh = dh + _nt(dp_ref[:, k * WIN_C:(k + 1) * WIN_C], w_ref[k])
        xv = x_ref[...]
        r = _rstd(xv)
        xn = xv * r
        blk_ref[0:1, :] = blk_ref[0:1, :] + jnp.sum(dh * xn, axis=0, keepdims=True)
        t = dh * g_ref[...]
        dx_ref[...] = dx2_ref[...] + r * (t - xn * jnp.mean(t * xn, axis=-1, keepdims=True))

    row = lambda w: pl.BlockSpec((tm, w), lambda i: (i, 0))
    vec = pl.BlockSpec((1, D), lambda i: (0, 0))
    return _carry("in_bwd", body, _NoExchange(), (), (dproj, win_g, x, dx2, g1, *other_rows),
                  [row(PW), pl.BlockSpec((N_CHIP, D, WIN_C), lambda i: (0, 0, 0)), row(D), row(D), vec] + [vec] * n,
                  [row(D), pl.BlockSpec((8, D), lambda i: (0, 0))],
                  [jax.ShapeDtypeStruct((S, D), F32), jax.ShapeDtypeStruct((8, D), F32)],
                  grid=(S // tm,), semantics=("arbitrary",), after=after)[0]


ANY = pl.BlockSpec(memory_space=pl.ANY)
VMEM = pl.BlockSpec(memory_space=pltpu.VMEM)
FLIPS = ((1, 0), (0, 1), (1, 1))


def _place():
    x, y, c = lax.axis_index("x"), lax.axis_index("y"), lax.axis_index("c")
    chips = [((1 - x) if fx else x, (1 - y) if fy else y) for fx, fy in FLIPS]
    return x, y, c, 2 * x + y, chips


def _remote(src, dst, send_sem, recv_sem, device):
    return pltpu.make_async_remote_copy(src_ref=src, dst_ref=dst, send_sem=send_sem, recv_sem=recv_sem,
                                        device_id=device, device_id_type=MESH)


class _Exchange:
    aliases = {}

    def middle(self, ins, outs, sems):
        pass


class _GatherShards(_Exchange):
    def __init__(self, shards):
        n = self.n = len(shards)
        self.n_in = self.n_out = n
        self.out_shape = [jax.ShapeDtypeStruct((N_CHIP,) + s.shape, s.dtype) for s in shards]
        dma = pltpu.SemaphoreType.DMA
        self.scratch = [dma((3 * n,)), dma((3 * n,)), dma((3 * n,)), dma((3 * n,)), dma((n,)), dma((n,))]

    def _ici(self, ins, outs, sems, a, j, chip):
        x, y, c, me, chips = _place()
        half = ins[a].shape[0] // 2
        return _remote(ins[a].at[pl.ds(c * half, half), :], outs[a].at[me, pl.ds(c * half, half), :],
                       sems[0].at[3 * a + j], sems[1].at[3 * a + j], (*chip, c))

    def _fwd(self, outs, sems, a, j, chip, half_of):
        x, y, c, me, chips = _place()
        half = outs[a].shape[1] // 2
        blk = outs[a].at[2 * chip[0] + chip[1], pl.ds(half_of * half, half), :]
        return _remote(blk, blk, sems[2].at[3 * a + j], sems[3].at[3 * a + j], (x, y, 1 - c))

    def _own(self, ins, outs, sems, a):
        return _own_shard_to_sibling(ins[a], outs[a], sems[4].at[a], sems[5].at[a])

    def start(self, ins, outs, sems):
        chips = _place()[4]
        for a in range(self.n):
            for j, chip in enumerate(chips):
                self._ici(ins, outs, sems, a, j, chip).start()
        for a in range(self.n):
            self._own(ins, outs, sems, a).start()

    def middle(self, ins, outs, sems):
        x, y, c, me, chips = _place()
        for a in range(self.n):
            for j, chip in enumerate(chips):
                half = outs[a].shape[1] // 2
                blk = outs[a].at[2 * chip[0] + chip[1], pl.ds(c * half, half), :]
                _remote(blk, blk, sems[0].at[3 * a + j], sems[1].at[3 * a + j], (x, y, c)).wait_recv()
                self._fwd(outs, sems, a, j, chip, c).start()

    def finish(self, ins, outs, sems):
        x, y, c, me, chips = _place()
        for a in range(self.n):
            for j, chip in enumerate(chips):
                self._fwd(outs, sems, a, j, chip, 1 - c).wait_recv()
        for a in range(self.n):
            for j, chip in enumerate(chips):
                self._ici(ins, outs, sems, a, j, chip).wait_send()
                self._fwd(outs, sems, a, j, chip, c).wait_send()
            self._own(ins, outs, sems, a).wait()


def _own_shard_to_sibling(shard_ref, gathered_ref, send_sem, recv_sem):
    x, y, c, me, chips = _place()
    return _remote(shard_ref, gathered_ref.at[me], send_sem, recv_sem, (x, y, 1 - c))


class _NoExchange(_Exchange):
    n_in = n_out = 0
    out_shape = ()
    scratch = ()

    def start(self, ins, outs, sems):
        pass

    def finish(self, ins, outs, sems):
        pass


class _ForwardGathered(_Exchange):
    def __init__(self, shards, own=True, forward=True):
        self.own, self.forward = own, forward
        n = self.n = len(shards)
        self.n_in, self.n_out = 2 * n, n
        self.out_shape = [jax.ShapeDtypeStruct((N_CHIP,) + s.shape, s.dtype) for s in shards]
        dma = pltpu.SemaphoreType.DMA
        self.scratch = [dma((3 * n,)), dma((3 * n,)), dma((n,)), dma((n,))]
        self.aliases = {n + a: a for a in range(n)}

    def _fwd(self, outs, sems, a, j, chip, half_of):
        x, y, c, me, chips = _place()
        half = outs[a].shape[1] // 2
        blk = outs[a].at[2 * chip[0] + chip[1], pl.ds(half_of * half, half), :]
        return _remote(blk, blk, sems[0].at[3 * a + j], sems[1].at[3 * a + j], (x, y, 1 - c))

    def _own(self, ins, outs, sems, a):
        return _own_shard_to_sibling(ins[a], outs[a], sems[2].at[a], sems[3].at[a])

    def start(self, ins, outs, sems):
        x, y, c, me, chips = _place()
        for a in range(self.n):
            for j, chip in enumerate(chips if self.forward else ()):
                self._fwd(outs, sems, a, j, chip, c).start()
        for a in range(self.n if self.own else 0):
            self._own(ins, outs, sems, a).start()

    def finish(self, ins, outs, sems):
        x, y, c, me, chips = _place()
        for a in range(self.n):
            for j, chip in enumerate(chips if self.forward else ()):
                self._fwd(outs, sems, a, j, chip, 1 - c).wait_recv()
        for a in range(self.n):
            for j, chip in enumerate(chips if self.forward else ()):
                self._fwd(outs, sems, a, j, chip, c).wait_send()
            if self.own:
                self._own(ins, outs, sems, a).wait()


HBM = pl.BlockSpec(memory_space=pltpu.HBM)
SEMS = pl.BlockSpec(memory_space=pltpu.SEMAPHORE)
DATAFLOW = pltpu.SideEffectType.DATAFLOW_SIDE_EFFECTING


class _OverIci:
    def __init__(self, name, sources, lands):
        self.name, self.n = name, len(sources)
        hbm = lambda t: pltpu.with_memory_space_constraint(t, pltpu.HBM)
        self.arrays = [hbm(t) for t in sources] + [hbm(t) for t in lands]

    def sent(self, src, land, a, chip):
        raise NotImplementedError

    def landed(self, land, a, chip):
        raise NotImplementedError

    def _copy(self, arr, sems, a, j, receiving):
        x, y, c, me, chips = _place()
        src, dst = self.sent(arr[a], arr[self.n + a], a, chips[j])
        if receiving:
            dst = self.landed(arr[self.n + a], a, chips[j])
        return _remote(src, dst, sems[0].at[3 * a + j], sems[1].at[3 * a + j], (*chips[j], c))

    def start(self, after):
        m = len(self.arrays)

        def body(*refs):
            arr, sems, token = refs[:m], refs[m + 1:m + 3], refs[-1]
            for a in range(self.n):
                for j in range(3):
                    self._copy(arr, sems, a, j, False).start()
            token[...] = jnp.zeros_like(token)

        dma = pltpu.SemaphoreType.DMA
        outs = pl.pallas_call(
            body, name=self.name + "_start",
            out_shape=[dma((3 * self.n,)), dma((3 * self.n,))] + [pltpu.HBM(t.shape, t.dtype) for t in self.arrays]
                      + [jax.ShapeDtypeStruct((8, 128), F32)],
            in_specs=[HBM] * m + [ANY], out_specs=[SEMS, SEMS] + [HBM] * m + [VMEM],
            input_output_aliases={i: 2 + i for i in range(m)},
            compiler_params=pltpu.CompilerParams(has_side_effects=DATAFLOW),
        )(*self.arrays, after)
        self.sems, self.arrays = outs[0:2], list(outs[2:2 + m])
        return outs[-1]

    def wait(self, after):
        m = len(self.arrays)

        def body(*refs):
            arr, sems = refs[:m], refs[m:m + 2]
            for a in range(self.n):
                for j in range(3):
                    self._copy(arr, sems, a, j, False).wait_send()
                    self._copy(arr, sems, a, j, True).wait_recv()

        outs = pl.pallas_call(
            body, name=self.name + "_wait",
            out_shape=[pltpu.HBM(t.shape, t.dtype) for t in self.arrays],
            in_specs=[HBM] * m + [SEMS, SEMS, ANY], out_specs=[HBM] * m,
            input_output_aliases={i: i for i in range(m)},
            compiler_params=pltpu.CompilerParams(has_side_effects=DATAFLOW),
        )(*self.arrays, *self.sems, after)
        return list(outs[:self.n]), list(outs[self.n:])


class _GatherOverIci(_OverIci):
    def __init__(self, name, shards):
        super().__init__(name, shards, [lax.empty((N_CHIP,) + s.shape, s.dtype) for s in shards])

    @staticmethod
    def _half(ref):
        c = lax.axis_index("c")
        half = ref.shape[-2] // 2
        return pl.ds(c * half, half)

    def sent(self, src, land, a, chip):
        return src.at[self._half(src), :], land.at[_place()[3], self._half(src), :]

    def landed(self, land, a, chip):
        return land.at[2 * chip[0] + chip[1], self._half(land), :]


class _SumOverIci(_OverIci):
    def __init__(self, name, pre):
        super().__init__(name, pre, [lax.empty(p.shape, p.dtype) for p in pre])

    def sent(self, src, land, a, chip):
        return src.at[2 * chip[0] + chip[1]], land.at[_place()[3]]

    def landed(self, land, a, chip):
        return land.at[2 * chip[0] + chip[1]]


class _HalvesToSibling(_Exchange):
    def __init__(self, grads):
        n = self.n = len(grads)
        self.n_in = self.n_out = n
        self.out_shape = [jax.ShapeDtypeStruct((N_CHIP, g.shape[1] // 2, g.shape[2]), g.dtype) for g in grads]
        self.scratch = [pltpu.SemaphoreType.DMA((n,)), pltpu.SemaphoreType.DMA((n,))]

    def _copy(self, ins, outs, sems, a):
        x, y, c, me, chips = _place()
        half = ins[a].shape[1] // 2
        return _remote(ins[a].at[:, pl.ds((1 - c) * half, half), :], outs[a], sems[0].at[a], sems[1].at[a], (x, y, 1 - c))

    def start(self, ins, outs, sems):
        for a in range(self.n):
            self._copy(ins, outs, sems, a).start()

    def finish(self, ins, outs, sems):
        for a in range(self.n):
            self._copy(ins, outs, sems, a).wait_recv()
        for a in range(self.n):
            self._copy(ins, outs, sems, a).wait_send()


class _ShareHalves(_Exchange):
    def __init__(self, fulls):
        n = self.n = len(fulls)
        self.n_in = self.n_out = n
        self.out_shape = [jax.ShapeDtypeStruct(f.shape, f.dtype) for f in fulls]
        self.scratch = [pltpu.SemaphoreType.DMA((n,)), pltpu.SemaphoreType.DMA((n,))]
        self.aliases = {a: a for a in range(n)}

    def _copy(self, outs, sems, a, half_of):
        x, y, c, me, chips = _place()
        half = outs[a].shape[0] // 2
        rows = outs[a].at[pl.ds(half_of * half, half), :]
        return _remote(rows, rows, sems[0].at[a], sems[1].at[a], (x, y, 1 - c))

    def start(self, ins, outs, sems):
        c = _place()[2]
        for a in range(self.n):
            self._copy(outs, sems, a, c).start()

    def finish(self, ins, outs, sems):
        c = _place()[2]
        for a in range(self.n):
            self._copy(outs, sems, a, 1 - c).wait_recv()
        for a in range(self.n):
            self._copy(outs, sems, a, c).wait_send()


class _GatherBlocks(_Exchange):
    def __init__(self, block):
        self.n_in = self.n_out = 1
        self.out_shape = [jax.ShapeDtypeStruct((8,) + block.shape, block.dtype)]
        dma = pltpu.SemaphoreType.DMA
        self.scratch = [dma((7,)), dma((7,)), dma]

    @staticmethod
    def _peer(f):
        x, y, c, me, chips = _place()
        return ((1 - x) if f & 4 else x, (1 - y) if f & 2 else y, (1 - c) if f & 1 else c)

    def start(self, ins, outs, sems):
        x, y, c, me, chips = _place()
        for f in range(1, 8):
            _remote(ins[0], outs[0].at[2 * me + c], sems[0].at[f - 1], sems[1].at[f - 1], self._peer(f)).start()
        pltpu.make_async_copy(ins[0], outs[0].at[2 * me + c], sems[2]).start()

    def finish(self, ins, outs, sems):
        x, y, c, me, chips = _place()
        for f in range(1, 8):
            px, py, pc = self._peer(f)
            blk = outs[0].at[4 * px + 2 * py + pc]
            _remote(blk, blk, sems[0].at[f - 1], sems[1].at[f - 1], (x, y, c)).wait_recv()
        for f in range(1, 8):
            _remote(ins[0], outs[0].at[2 * me + c], sems[0].at[f - 1], sems[1].at[f - 1], self._peer(f)).wait_send()
        pltpu.make_async_copy(ins[0], outs[0].at[2 * me + c], sems[2]).wait()


class _Both(_Exchange):
    def __init__(self, first, second):
        self.parts = (first, second)
        self.n_in, self.n_out = first.n_in + second.n_in, first.n_out + second.n_out
        self.out_shape = first.out_shape + second.out_shape
        self.scratch = first.scratch + second.scratch
        self.aliases = dict(first.aliases)
        self.aliases.update({first.n_in + i: first.n_out + o for i, o in second.aliases.items()})

    def _split(self, ins, outs, sems):
        a, b = self.parts
        return ((a, ins[:a.n_in], outs[:a.n_out], sems[:len(a.scratch)]),
                (b, ins[a.n_in:], outs[a.n_out:], sems[len(a.scratch):]))

    def start(self, ins, outs, sems):
        for ex, i, o, s in self._split(ins, outs, sems):
            ex.start(i, o, s)

    def middle(self, ins, outs, sems):
        for ex, i, o, s in self._split(ins, outs, sems):
            ex.middle(i, o, s)

    def finish(self, ins, outs, sems):
        for ex, i, o, s in self._split(ins, outs, sems):
            ex.finish(i, o, s)


class _Bound:
    def __init__(self, ex, ins, outs, sems):
        self.start = lambda: ex.start(ins, outs, sems)
        self.middle = lambda: ex.middle(ins, outs, sems)
        self.finish = lambda: ex.finish(ins, outs, sems)


def _carry(name, body, ex, ex_args, args, in_specs, out_specs, out_shape, scratch_shapes=(), grid=None, semantics=(),
           after=None):
    n_a, n_o, n_s = len(args), len(out_shape), len(scratch_shapes)
    behind = [] if after is None else [after]

    def full_body(*refs):
        p = 0
        groups = []
        for size in (n_a, ex.n_in, len(behind), n_o, ex.n_out, n_s, len(ex.scratch)):
            groups.append(refs[p:p + size])
            p += size
        a, ei, _, o, eo, s, es = groups
        body(*a, *o, *s, _Bound(ex, ei, eo, es))

    kwargs = {} if grid is None else {"grid": grid}
    outs = pl.pallas_call(
        full_body, name=name,
        in_specs=list(in_specs) + [ANY] * (ex.n_in + len(behind)), out_specs=list(out_specs) + [ANY] * ex.n_out,
        out_shape=list(out_shape) + list(ex.out_shape), scratch_shapes=list(scratch_shapes) + list(ex.scratch),
        input_output_aliases={n_a + i: n_o + o for i, o in ex.aliases.items()},
        compiler_params=_params(*semantics) if semantics else pltpu.CompilerParams(vmem_limit_bytes=VMEM_LIMIT),
        **kwargs,
    )(*args, *ex_args, *behind)
    return outs[:n_o], outs[n_o:]


def _prepare_carrying(name, x, g1, pos, ifc, arrays, ex, ex_args):
    n = len(arrays)
    r, cc = arrays[0].shape
    steps = 4
    tr, tm = r // steps, S // steps

    def body(x_ref, g_ref, pos_ref, ifc_ref, *refs):
        src, h_ref, cos_ref, sin_ref, dst, xc = refs[:n], refs[n], refs[n + 1], refs[n + 2], refs[n + 3:2 * n + 3], refs[-1]

        @pl.when(pl.program_id(0) == 0)
        def _():
            xc.start()

        xv = x_ref[...]
        h_ref[...] = (xv * _rstd(xv) * g_ref[...]).astype(BF16)
        ang = pos_ref[...].astype(F32) * ifc_ref[...]
        cos_ref[...] = jnp.cos(ang)
        sin_ref[...] = jnp.sin(ang)
        for a in range(n):
            dst[a][...] = src[a][...].astype(BF16)

        @pl.when(pl.program_id(0) == steps - 1)
        def _():
            xc.middle()
            xc.finish()

    row = lambda w: pl.BlockSpec((tm, w), lambda i: (i, 0))
    const = lambda w: pl.BlockSpec((1, w), lambda i: (0, 0))
    blk = pl.BlockSpec((tr, cc), lambda i: (i, 0))
    return _carry(name, body, ex, ex_args, (x, g1, pos, ifc, *arrays),
                  [row(D), const(D), row(1), const(128)] + [blk] * n,
                  [row(D), row(128), row(128)] + [blk] * n,
                  [jax.ShapeDtypeStruct((S, D), BF16)] + [jax.ShapeDtypeStruct((S, 128), F32)] * 2
                  + [jax.ShapeDtypeStruct((r, cc), BF16)] * n,
                  grid=(steps,), semantics=("arbitrary",))


def _exchange_alone(name, ex, ex_args):
    def body(xc):
        xc.start()
        xc.middle()
        xc.finish()

    return _carry(name, body, ex, ex_args, (), (), (), ())[1]


def _core_index():
    return lax.axis_index("c").astype(jnp.int32).reshape(1)


def _pair_sum(gs, gots):
    n = len(gs)

    def body(c_ref, *refs):
        for a in range(n):
            refs[2 * n + a][...] = (refs[a][...].astype(F32) + refs[n + a][...].astype(F32)).astype(BF16)

    mine = [pl.BlockSpec((None, g.shape[1] // 2, g.shape[2]), lambda k, c_ref: (k, c_ref[0], 0)) for g in gs]
    blk = [pl.BlockSpec((None, g.shape[1] // 2, g.shape[2]), lambda k, c_ref: (k, 0, 0)) for g in gs]
    return pl.pallas_call(
        body, name=f"pair_sum_{gs[0].shape[1]}x{gs[0].shape[2]}",
        grid_spec=pltpu.PrefetchScalarGridSpec(
            num_scalar_prefetch=1, grid=(N_CHIP,), in_specs=mine + blk, out_specs=blk),
        out_shape=[jax.ShapeDtypeStruct((N_CHIP, g.shape[1] // 2, g.shape[2]), BF16) for g in gs],
        compiler_params=_params("parallel"),
    )(_core_index(), *gs, *gots)


def _chip_sum(pre, parts):
    n = len(parts)
    me = 2 * lax.axis_index("x") + lax.axis_index("y")
    others = [k + (k >= me).astype(jnp.int32) for k in range(3)]
    where = jnp.stack([lax.axis_index("c"), me, *others]).astype(jnp.int32)

    def body(w_ref, *refs):
        for a in range(n):
            own, p1, p2, p3 = refs[4 * a:4 * a + 4]
            refs[4 * n + a][...] = ((own[...].astype(F32) + p1[...].astype(F32)) + p2[...].astype(F32)) + p3[...].astype(F32)

    in_specs, out_specs, operands = [], [], []
    for a in range(n):
        _, half, cc = parts[a].shape
        tr = half // 2
        in_specs += [pl.BlockSpec((None, tr, cc), lambda i, w_ref, s=s: (w_ref[s], i, 0)) for s in (1, 2, 3, 4)]
        out_specs.append(pl.BlockSpec((tr, cc), lambda i, w_ref: (2 * w_ref[0] + i, 0)))
        operands += [pre[a], parts[a], parts[a], parts[a]]
    return pl.pallas_call(
        body, name=f"chip_sum_{parts[0].shape[1]}x{parts[0].shape[2]}",
        grid_spec=pltpu.PrefetchScalarGridSpec(num_scalar_prefetch=1, grid=(2,), in_specs=in_specs, out_specs=out_specs),
        out_shape=[jax.ShapeDtypeStruct((2 * p.shape[1], p.shape[2]), F32) for p in parts],
        compiler_params=_params("parallel"),
    )(where, *operands)


def _adamw_math(w, g, m, v):
    m = ADAM_B1 * m + (1.0 - ADAM_B1) * g
    v = ADAM_B2 * v + (1.0 - ADAM_B2) * (g * g)
    m_hat = m / (1.0 - ADAM_B1 ** ADAM_STEP)
    v_hat = v / (1.0 - ADAM_B2 ** ADAM_STEP)
    delta = -ADAM_LR * (m_hat / (jnp.sqrt(v_hat) + ADAM_EPS) + ADAM_WD * w)
    return delta, m, v


def _adamw(ws, gs, ms, vs, after=None):
    n = len(ws)

    def body(*refs):
        for a in range(n):
            w_ref, g_ref, m_ref, v_ref = (refs[t * n + a] for t in range(4))
            go_ref, d_ref, nm_ref, nv_ref = refs[4 * n + 4 * a:4 * n + 4 * a + 4]
            g = g_ref[...]
            go_ref[...] = g
            d_ref[...], nm_ref[...], nv_ref[...] = _adamw_math(w_ref[...], g, m_ref[...], v_ref[...])

    blks = [pl.BlockSpec((w.shape[0] // 4, w.shape[1]), lambda i: (i, 0)) for w in ws]
    outs = _carry(f"adamw_{ws[0].shape[0]}x{ws[0].shape[1]}", body, _NoExchange(), (), (*ws, *gs, *ms, *vs),
                  blks * 4, [b for b in blks for _ in range(4)],
                  [jax.ShapeDtypeStruct(w.shape, F32) for w in ws for _ in range(4)],
                  grid=(4,), semantics=("parallel",), after=after)[0]
    return [outs[4 * a:4 * a + 4] for a in range(n)]


def _adamw_gains(gall, ws, ms, vs):
    def body(ga_ref, *refs):
        w, m, v = refs[0:4], refs[4:8], refs[8:12]
        outs, loss_ref, total = refs[12:28], refs[28], refs[29]
        g = ga_ref[0]
        for dev in range(1, 8):
            g = g + ga_ref[dev]
        total[...] = g
        for i in range(4):
            gi = total[i:i + 1, :]
            outs[i][...] = gi
            outs[4 + i][...], outs[8 + i][...], outs[12 + i][...] = _adamw_math(w[i][...], gi, m[i][...], v[i][...])
        loss_ref[...] = total[4:5, 0:128] * (0.5 / D)

    outs = pl.pallas_call(
        body, name="adamw_gains",
        out_shape=[jax.ShapeDtypeStruct((1, D), F32)] * 16 + [jax.ShapeDtypeStruct((1, 128), F32)],
        scratch_shapes=[pltpu.VMEM((8, D), F32)],
    )(gall, *ws, *ms, *vs)
    return outs[0:4], outs[4:8], outs[8:12], outs[12:16], outs[16]


def kernel(x, positions, w_in, w_out, g_pre_mix, g_post_mix, g_pre_ffn, g_post_ffn, w_gate, w_up, w_down, loss_target, m_w_in, m_w_out, m_g_pre_mix, m_g_post_mix, m_g_pre_ffn, m_g_post_ffn, m_w_gate, m_w_up, m_w_down, v_w_in, v_w_out, v_g_pre_mix, v_g_post_mix, v_g_pre_ffn, v_g_post_ffn, v_w_gate, v_w_up, v_w_down):
    tr = lambda t: jnp.swapaxes(t, 1, 2)[0]
    shards = [w_in[0], w_out[0], tr(w_gate), tr(w_up), w_down[0]]
    moms = [m_w_in[0], m_w_out[0], tr(m_w_gate), tr(m_w_up), m_w_down[0]]
    vels = [v_w_in[0], v_w_out[0], tr(v_w_gate), tr(v_w_up), v_w_down[0]]
    xs, pos, tgt = x[0], positions.reshape(S, 1), loss_target[0]
    g1, g2, g3, g4 = g_pre_mix, g_post_mix, g_pre_ffn, g_post_ffn
    tabs = tuple(jnp.asarray(t) for t in _retention_tables())
    ifc, spread = _rotary_tables()
    ifc, spread = jnp.asarray(ifc), jnp.asarray(spread, dtype=BF16)
    bf = [s.astype(BF16) for s in shards[:2]]

    (h1, cos, sin, *ffn_bf), (win_g,) = _prepare_carrying(
        "gather_in", xs, g1, pos, ifc, shards[2:], _GatherShards(bf[:1]), bf[:1])
    bf += list(ffn_bf)
    wout_gather = _GatherOverIci("wout_gather", bf[1:2])
    token = wout_gather.start(win_g)
    ffn_gather = _GatherOverIci("ffn_gather", bf[2:])
    token = ffn_gather.start(token)
    qr, kr, rv, rg, aq, ak, av = _proj_fwd(h1, win_g, cos, sin, spread, token)
    wout_sh, wout_land = wout_gather.wait(qr)
    n_ffn = len(bf[2:])
    (att_out, lse, cat_a), (wout_g, *ffn_gather.arrays[n_ffn:]) = _att_fwd(
        aq, ak, av, _Both(_ForwardGathered(bf[1:2]), _ForwardGathered(bf[2:], forward=False)),
        [*wout_sh, *wout_land, *ffn_gather.arrays])
    wout_g = wout_g.reshape(D, D)
    (o_raw, cat_r, states), _ = _ret_fwd(qr, kr, rv, rg, tabs, _NoExchange(), (), cat_a)
    ffn_sh, ffn_lands = ffn_gather.wait(cat_r)
    (mix, x2, h3), (wg_g, wu_g, wd_g) = _mix_fwd(cat_r, cat_a, wout_g, xs, g2, g3,
                                                _ForwardGathered(bf[2:], own=False), [*ffn_sh, *ffn_lands])
    gt, up, a, sq, dy, df, dg4 = _ffn_fwd(h3, wg_g, wu_g, wd_g, x2, tgt, g4)

    dgt, dup, dx2, dmix, dg3, dg2 = _ffn_bwd_act(df, gt, up, wg_g, wu_g, wd_g, dy, x2, mix, g2, g3)
    ffn_grads = list(_ffn_bwd_w(a, df, h3, dgt, dup))
    (dret, datt, dwout), got = _mix_bwd(dmix, cat_r, cat_a, wout_g, _HalvesToSibling(ffn_grads), ffn_grads)
    ffn_sum = _SumOverIci("ffn_sum", _pair_sum(ffn_grads, got))
    token = ffn_sum.start(datt)
    (dq_att, dk_att, dv_att), _ = _att_bwd(aq, ak, av, datt, att_out, lse, _NoExchange(), (), token)
    (dqr, dkr, drv, drg), _ = _ret_bwd(qr, kr, rv, rg, o_raw, states, dret, tabs, _NoExchange(), (), token)
    dproj = _rot_bwd(cos, sin, spread, dqr, dkr, drv, drg, dq_att, dk_att, dv_att)
    sums = _chip_sum(*ffn_sum.wait(dproj))
    dwin, ffn_full = _win_bwd_w(h1, dproj, _ShareHalves(sums), sums)
    in_grads = [dwin, dwout.reshape(N_CHIP, WOUT_R, D)]

    got = _exchange_alone("halves_to_sibling", _HalvesToSibling(in_grads), in_grads)
    in_sum = _SumOverIci("in_sum", _pair_sum(in_grads, got))
    token = in_sum.start(dproj)
    dx, gblock = _in_bwd(dproj, win_g, xs, dx2, g1, [dg2, dg3, dg4, sq], token)
    ffn_upd = _adamw(shards[2:], [ffn_full[o] for o in (1, 2, 0)],
                     moms[2:], vels[2:], token)
    pre, parts = in_sum.wait(ffn_upd[2][0])
    sums = _chip_sum(pre, parts)
    *in_full, gall = _exchange_alone("share_rest", _Both(_ShareHalves(sums), _GatherBlocks(gblock)), [*sums, gblock])
    upd = _adamw(shards[:2], in_full, moms[:2], vels[:2]) + ffn_upd
    gg, gd, gm, gv, loss_row = _adamw_gains(gall, [g1, g2, g3, g4],
                                            [m_g_pre_mix, m_g_post_mix, m_g_pre_ffn, m_g_post_ffn],
                                            [v_g_pre_mix, v_g_post_mix, v_g_pre_ffn, v_g_post_ffn])

    def order(mats, vecs):
        back = lambda t: jnp.swapaxes(t[None], 1, 2)
        return [mats[0][None], mats[1][None], *vecs, back(mats[2]), back(mats[3]), mats[4][None]]

    return (loss_row[0, 0], dx[None],
            *order([u[0] for u in upd], gg),
            *order([u[1] for u in upd], gd),
            *order([u[2] for u in upd], gm),
            *order([u[3] for u in upd], gv))
```

```python
import numpy as np
import jax
import jax.numpy as jnp
from jax import lax
from jax.experimental import pallas as pl
from jax.experimental.pallas import tpu as pltpu

F32, BF16 = jnp.float32, jnp.bfloat16
MESH = pl.DeviceIdType.MESH

S = 2048
D = 1024
PW = 3072
N_CHIP = 4
WIN_C = PW // N_CHIP
DFF = 2816
FF_C = DFF // N_CHIP
WOUT_R = D // N_CHIP
RMS_EPS = 1e-6
GN_EPS = 1e-5
RET_C = 128
RET_PER_STEP = 4
RET_SCALE = 32 ** -0.5
ATT_BLK = 128
ATT_SCALE = 64 ** -0.5
PATTERN_DILATIONS = (16, 1, 4)
NEG = -1e30
VMEM_LIMIT = 56 * 1024 * 1024

ADAM_LR, ADAM_B1, ADAM_B2, ADAM_EPS, ADAM_WD, ADAM_STEP = 0.001, 0.9, 0.999, 1e-08, 0.01, 10


def _params(*sem):
    return pltpu.CompilerParams(dimension_semantics=sem, vmem_limit_bytes=VMEM_LIMIT)


def _nt(a, b):
    return lax.dot_general(a, b, (((1,), (1,)), ((), ())), preferred_element_type=F32)


def _tn(a, b):
    return lax.dot_general(a, b, (((0,), (0,)), ((), ())), preferred_element_type=F32)


def _nn(a, b):
    return jnp.dot(a, b, preferred_element_type=F32)


def _rstd(v):
    return lax.rsqrt(jnp.mean(v * v, axis=-1, keepdims=True) + RMS_EPS)


def _sigmoid(v):
    return 1.0 / (1.0 + jnp.exp(-v))


def _rows(i, t):
    return pl.ds(pl.multiple_of(i * t, t), t)


def _retention_tables():
    h = np.arange(8, dtype=np.float32)
    log_g = np.log1p(-np.exp2(-5.0 - h)).astype(np.float32)
    idx = np.arange(RET_C, dtype=np.float32)
    diff = idx[:, None] - idx[None, :]
    dtab = np.where(diff >= 0, np.exp(log_g[:, None, None] * np.maximum(diff, 0.0)), 0.0).astype(np.float32)
    dtab = dtab.reshape(8 * RET_C, RET_C)
    lane_head = np.arange(256) // 32
    a_tab = np.exp(log_g[lane_head][None, :] * (idx + 1.0)[:, None]).astype(np.float32)
    b_tab = np.exp(log_g[lane_head][None, :] * (RET_C - 1.0 - idx)[:, None]).astype(np.float32)
    lam = np.exp(log_g[lane_head] * RET_C).astype(np.float32)[:, None]
    bd = (lane_head[:, None] == (np.arange(512) // 64)[None, :]).astype(np.float32)
    return dtab, a_tab, b_tab, lam, bd


def _rotary_tables():
    inv_r = (1.0 / (np.float32(10000.0) ** np.linspace(0.0, 1.0, 16, dtype=np.float32))).astype(np.float32)
    inv_a = (np.float32(500000.0) ** (-np.arange(0, 16, 2, dtype=np.float32) / np.float32(16))).astype(np.float32)
    ifc = np.zeros((1, 128), np.float32)
    ifc[0, 0:16], ifc[0, 16:24] = inv_r, inv_a
    spread = np.zeros((128, 768), np.float32)
    for lane in range(256):
        spread[(lane % 32) % 16, lane] = 1.0
    for lane in range(512):
        d = lane % 64
        spread[16 + d % 8 if d < 16 else 24, 256 + lane] = 1.0
    return ifc, spread


def _rot_halves(tm):
    lo_r = (lax.broadcasted_iota(jnp.int32, (tm, 256), 1) % 32) < 16
    lo_a = (lax.broadcasted_iota(jnp.int32, (tm, 512), 1) % 64) < 8
    return lo_r, lo_a


def _spread_exact(t, e):
    hi = t.astype(BF16)
    r1 = t - hi.astype(F32)
    mid = r1.astype(BF16)
    lo = (r1 - mid.astype(F32)).astype(BF16)
    return _nn(hi, e) + _nn(mid, e) + _nn(lo, e)


def _rot_tables(cos_ref, sin_ref, e_ref):
    cs = _spread_exact(cos_ref[...], e_ref[...])
    sn = _spread_exact(sin_ref[...], e_ref[...])
    return cs[:, 0:256], cs[:, 256:768], sn[:, 0:256], sn[:, 256:768]


def _proj_fwd(h1, win_g, cos, sin, spread, after):
    tm = 256

    def body(h_ref, w_ref, cos_ref, sin_ref, e_ref, qr_ref, kr_ref, rv_ref, rg_ref, aq_ref, ak_ref, av_ref, p_ref, _):
        h = h_ref[...]
        for k in range(N_CHIP):
            p_ref[:, k * WIN_C:(k + 1) * WIN_C] = _nn(h, w_ref[k])
        cr, ca, sr, sa = _rot_tables(cos_ref, sin_ref, e_ref)
        lo_r, lo_a = _rot_halves(tm)

        def rot_r(v):
            return v * cr + sr * jnp.where(lo_r, -pltpu.roll(v, 240, 1), pltpu.roll(v, 16, 1))

        def rot_a(v):
            return v * ca + sa * jnp.where(lo_a, -pltpu.roll(v, 504, 1), pltpu.roll(v, 8, 1))

        qr_ref[...] = rot_r(p_ref[:, 0:256]).astype(BF16)
        kr_ref[...] = (rot_r(p_ref[:, 256:512]) * RET_SCALE).astype(BF16)
        rv_ref[...] = p_ref[:, 512:1024].astype(BF16)
        rg_ref[...] = p_ref[:, 1024:1536]
        aq, ak = rot_a(p_ref[:, 1536:2048]), rot_a(p_ref[:, 2048:2560])
        for j in range(4):
            aq_ref[j] = aq[:, 128 * j:128 * j + 128]
            ak_ref[j] = ak[:, 128 * j:128 * j + 128]
            av_ref[j] = p_ref[:, 2560 + 128 * j:2560 + 128 * j + 128]

    row = lambda w: pl.BlockSpec((tm, w), lambda i: (i, 0))
    slab = pl.BlockSpec((4, tm, 128), lambda i: (0, i, 0))
    return _carry(
        "proj_fwd", body, _NoExchange(), (), (h1, win_g, cos, sin, spread),
        [row(D), pl.BlockSpec((N_CHIP, D, WIN_C), lambda i: (0, 0, 0)), row(128), row(128),
         pl.BlockSpec((128, 768), lambda i: (0, 0))],
        [row(256), row(256), row(512), row(512), slab, slab, slab],
        [jax.ShapeDtypeStruct((S, w), BF16) for w in (256, 256, 512)]
        + [jax.ShapeDtypeStruct((S, 512), F32)] + [jax.ShapeDtypeStruct((4, S, 128), F32)] * 3,
        scratch_shapes=[pltpu.VMEM((tm, PW), F32)], grid=(S // tm,), semantics=("parallel",), after=after)[0]


def _seg_mean(v):
    lo = lax.broadcasted_iota(jnp.int32, v.shape, 1) < 64
    s_lo = jnp.sum(jnp.where(lo, v, 0.0), axis=-1, keepdims=True)
    s_hi = jnp.sum(jnp.where(lo, 0.0, v), axis=-1, keepdims=True)
    return jnp.where(lo, s_lo, s_hi) * (1.0 / 64.0)


def _ret_fwd(qr, kr, rv, proj, tabs, exchange, exchange_args, after=None):
    C, G = RET_C, RET_PER_STEP
    steps = S // (C * G)
    dtab, a_tab, b_tab, lam, bd = tabs

    def body(q_ref, k_ref, v_ref, g_ref, dt_ref, a_ref, b_ref, lam_ref, bd_ref, o_ref, cat_ref, st_ref, R, exch):
        @pl.when(pl.program_id(0) == 0)
        def _():
            exch.start()
            R[...] = jnp.zeros_like(R)

        lane_head = lax.broadcasted_iota(jnp.int32, (C, 256), 1) // 32
        col_head = lax.broadcasted_iota(jnp.int32, (C, 256), 1) // 64
        for s in range(G):
            rows = slice(s * C, (s + 1) * C)
            q, k, v = q_ref[rows, :], k_ref[rows, :], v_ref[rows, :]
            rb = R[...].astype(BF16)
            st_ref[s] = rb
            qa = (q.astype(F32) * a_ref[...]).astype(BF16)
            cross = _nn(qa, rb)
            p = (_nt(_stack_heads(q, lane_head, n=8), k) * dt_ref[...]).astype(BF16)
            og = [cross[:, 256 * g:256 * g + 256]
                  + _unstack_heads(_nn(p[4 * C * g:4 * C * (g + 1)], v[:, 256 * g:256 * g + 256]), col_head)
                  for g in range(2)]
            kb = (k.astype(F32) * b_ref[...]).astype(BF16)
            R[...] = R[...] * lam_ref[...] + _tn(kb, v) * bd_ref[...]
            o_ref[rows, 0:256] = og[0]
            o_ref[rows, 256:512] = og[1]
            for j in range(4):
                oj = og[j // 2][:, 128 * (j % 2):128 * (j % 2) + 128]
                xc = oj - _seg_mean(oj)
                rn = xc * lax.rsqrt(_seg_mean(xc * xc) + GN_EPS)
                gj = g_ref[rows, 128 * j:128 * j + 128]
                cat_ref[rows, 128 * j:128 * j + 128] = (rn * (gj * _sigmoid(gj))).astype(BF16)

        @pl.when(pl.program_id(0) == steps - 1)
        def _():
            exch.middle()
            exch.finish()

    row = lambda w: pl.BlockSpec((C * G, w), lambda n: (n, 0))
    full = lambda a: pl.BlockSpec(a.shape, lambda n: (0,) * a.ndim)
    return _carry(
        "ret_fwd", body, exchange, exchange_args, (qr, kr, rv, proj, dtab, a_tab, b_tab, lam, bd),
        [row(256), row(256), row(512), row(512),
         full(dtab), full(a_tab), full(b_tab), full(lam), full(bd)],
        [row(512), row(512), pl.BlockSpec((G, 256, 512), lambda n: (n, 0, 0))],
        [jax.ShapeDtypeStruct((S, 512), F32), jax.ShapeDtypeStruct((S, 512), BF16),
         jax.ShapeDtypeStruct((S // C, 256, 512), BF16)],
        scratch_shapes=[pltpu.VMEM((256, 512), F32)], grid=(steps,), semantics=("arbitrary",), after=after)


def _stack_heads(v, lane_head, fill=0.0, n=4):
    return jnp.concatenate([jnp.where(lane_head == h, v, jnp.full_like(v, fill)) for h in range(n)], axis=0)


def _unstack_heads(v, lane_head, n=4):
    out = v[0:ATT_BLK]
    for h in range(1, n):
        out = jnp.where(lane_head == h, v[h * ATT_BLK:(h + 1) * ATT_BLK], out)
    return out


def _att_bias(has_prev):
    nk = 2 * ATT_BLK if has_prev else ATT_BLK
    a = lax.broadcasted_iota(jnp.int32, (4 * ATT_BLK, nk), 0) % ATT_BLK
    kk = lax.broadcasted_iota(jnp.int32, (4 * ATT_BLK, nk), 1)
    if not has_prev:
        return None, jnp.where((a - kk) >= 0, 0.0, NEG)
    dist = ATT_BLK + a - kk
    inside = (dist >= 0) & (dist <= ATT_BLK)
    return jnp.where(inside, 0.0, NEG), jnp.where(inside & (kk >= ATT_BLK), 0.0, NEG)


def _class_rows(ib, r, d):
    if d == 1:
        return pl.ds(pl.multiple_of(ib * ATT_BLK, ATT_BLK), ATT_BLK)
    return pl.ds(ib * ATT_BLK * d + r, ATT_BLK, stride=d)


def _slab_pair(ref, g, rows):
    return jnp.concatenate([ref[2 * g, rows, :], ref[2 * g + 1, rows, :]], axis=1)


def _att_blocks(d):
    nb = S // d // ATT_BLK
    return nb, nb > 1


def _att_fwd(aq, ak, av, exchange, exchange_args):
    def body(q_ref, k_ref, v_ref, o_ref, l_ref, cat_ref, xc):
        xc.start()
        lane_head = lax.broadcasted_iota(jnp.int32, (ATT_BLK, 256), 1) // 64
        for pi, d in enumerate(PATTERN_DILATIONS):
            if pi == len(PATTERN_DILATIONS) - 1:
                xc.middle()
            nb, has_prev = _att_blocks(d)
            bias_rest, bias_first = _att_bias(has_prev)

            def block(b, carry, pi=pi, d=d, nb=nb, has_prev=has_prev, bias_rest=bias_rest, bias_first=bias_first):
                r, ib = b // nb, b % nb
                rows = _class_rows(ib, r, d)
                prow = _class_rows(jnp.maximum(ib - 1, 0), r, d)
                bias = jnp.where(ib == 0, bias_first, bias_rest) if has_prev else bias_first
                for g in range(2):
                    qg = _slab_pair(q_ref, g, rows).astype(BF16)
                    kg = _slab_pair(k_ref, g, rows)
                    vg = _slab_pair(v_ref, g, rows)
                    if has_prev:
                        kg = jnp.concatenate([_slab_pair(k_ref, g, prow), kg], axis=0)
                        vg = jnp.concatenate([_slab_pair(v_ref, g, prow), vg], axis=0)
                    kg, vg = kg.astype(BF16), vg.astype(BF16)
                    s = _nt(_stack_heads(qg, lane_head), kg) * ATT_SCALE + bias
                    m = jnp.max(s, axis=-1, keepdims=True)
                    p = jnp.exp(s - m)
                    den = jnp.sum(p, axis=-1, keepdims=True)
                    og = _unstack_heads(_nn(p.astype(BF16), vg) / den, lane_head)
                    lg = _unstack_heads(jnp.broadcast_to(m + jnp.log(den), (4 * ATT_BLK, 256)), lane_head)
                    for jj in range(2):
                        j = 2 * g + jj
                        o_new, l_new = og[:, 128 * jj:128 * jj + 128], lg[:, 128 * jj:128 * jj + 128]
                        if pi > 0:
                            o_old, l_old = o_ref[j, rows, :], l_ref[j, rows, :]
                            mx = jnp.maximum(l_old, l_new)
                            ea, eb = jnp.exp(l_old - mx), jnp.exp(l_new - mx)
                            den = ea + eb
                            o_new = (ea * o_old + eb * o_new) / den
                            l_new = mx + jnp.log(den)
                        o_ref[j, rows, :] = o_new
                        l_ref[j, rows, :] = l_new
                return carry

            lax.fori_loop(0, S // ATT_BLK, block, 0, unroll=4)

        def to_cat(i, carry):
            rows = _rows(i, 256)
            for j in range(4):
                cat_ref[rows, 128 * j:128 * j + 128] = o_ref[j, rows, :].astype(BF16)
            return carry

        lax.fori_loop(0, S // 256, to_cat, 0)
        xc.finish()

    slab = jax.ShapeDtypeStruct((4, S, 128), F32)
    return _carry("att_fwd", body, exchange, exchange_args, (aq, ak, av), [VMEM] * 3, [VMEM] * 3,
                  [slab, slab, jax.ShapeDtypeStruct((S, 512), BF16)])


def _mix_fwd(cat_r, cat_a, wout, x, g2, g3, exchange, exchange_args):
    tm = 512

    def body(cr_ref, ca_ref, w_ref, x_ref, g2_ref, g3_ref, mix_ref, x2_ref, h3_ref, xc):
        @pl.when(pl.program_id(0) == 0)
        def _():
            xc.start()

        mix = _nn(cr_ref[...], w_ref[0:512, :]) + _nn(ca_ref[...], w_ref[512:1024, :])
        mix_ref[...] = mix
        x2 = x_ref[...] + mix * _rstd(mix) * g2_ref[...]
        x2_ref[...] = x2
        h3_ref[...] = (x2 * _rstd(x2) * g3_ref[...]).astype(BF16)

        @pl.when(pl.program_id(0) == S // tm - 1)
        def _():
            xc.middle()
            xc.finish()

    row = lambda w: pl.BlockSpec((tm, w), lambda i: (i, 0))
    vec = pl.BlockSpec((1, D), lambda i: (0, 0))
    return _carry("mix_fwd", body, exchange, exchange_args, (cat_r, cat_a, wout, x, g2, g3),
                  [row(512), row(512), pl.BlockSpec((D, D), lambda i: (0, 0)), row(D), vec, vec],
                  [row(D), row(D), row(D)],
                  [jax.ShapeDtypeStruct((S, D), F32), jax.ShapeDtypeStruct((S, D), F32),
                   jax.ShapeDtypeStruct((S, D), BF16)],
                  grid=(S // tm,), semantics=("arbitrary",))


def _ffn_fwd(h3, wg, wu, wd, x2, tgt, g4):
    tm = 512
    last = N_CHIP - 1

    def body(h_ref, wg_ref, wu_ref, wd_ref, x2_ref, t_ref, g_ref,
             gt_ref, up_ref, a_ref, loss_ref, dy_ref, df_ref, dg_ref, f_ref):
        k, i = pl.program_id(0), pl.program_id(1)
        h = h_ref[...]
        gt = _nt(h, wg_ref[...])
        up = _nt(h, wu_ref[...])
        gt_ref[...] = gt.astype(BF16)
        up_ref[...] = up.astype(BF16)
        a = (gt * _sigmoid(gt) * up).astype(BF16)
        a_ref[...] = a
        part = _nn(a, wd_ref[...])
        rows = _rows(i, tm)

        @pl.when(k == 0)
        def _():
            f_ref[rows, :] = part

        @pl.when((k > 0) & (k < last))
        def _():
            f_ref[rows, :] = f_ref[rows, :] + part

        @pl.when((k == last) & (i == 0))
        def _():
            loss_ref[...] = jnp.zeros_like(loss_ref)
            dg_ref[...] = jnp.zeros_like(dg_ref)

        @pl.when(k == last)
        def _():
            fv = f_ref[rows, :] + part
            r = _rstd(fv)
            fn = fv * r
            e = x2_ref[...] + fn * g_ref[...] - t_ref[...]
            loss_ref[...] = loss_ref[...] + jnp.sum(jnp.sum(e * e, axis=-1, keepdims=True), axis=0, keepdims=True)
            dy = e * (1.0 / D)
            dy_ref[...] = dy
            dg_ref[...] = dg_ref[...] + jnp.sum(dy * fn, axis=0, keepdims=True)
            t = dy * g_ref[...]
            df_ref[...] = (r * (t - fn * jnp.mean(t * fn, axis=-1, keepdims=True))).astype(BF16)

    wrow = pl.BlockSpec((None, FF_C, D), lambda k, i: (k, 0, 0))
    act = pl.BlockSpec((None, tm, FF_C), lambda k, i: (k, i, 0))
    late = pl.BlockSpec((tm, D), lambda k, i: (jnp.where(k == last, i, 0), 0))
    vec = pl.BlockSpec((1, D), lambda k, i: (0, 0))
    return pl.pallas_call(
        body, grid=(N_CHIP, S // tm), name="ffn_fwd",
        in_specs=[pl.BlockSpec((tm, D), lambda k, i: (i, 0)), wrow, wrow, wrow, late, late, vec],
        out_specs=[act, act, act, vec, late, late, vec],
        out_shape=[jax.ShapeDtypeStruct((N_CHIP, S, FF_C), BF16)] * 3
                  + [jax.ShapeDtypeStruct((1, D), F32), jax.ShapeDtypeStruct((S, D), F32),
                     jax.ShapeDtypeStruct((S, D), BF16), jax.ShapeDtypeStruct((1, D), F32)],
        scratch_shapes=[pltpu.VMEM((S, D), F32)],
        compiler_params=_params("arbitrary", "arbitrary"),
    )(h3, wg, wu, wd, x2, tgt, g4)


def _ffn_bwd_act(df, gt, up, wg, wu, wd, dy, x2, mix, g2, g3):
    tm, sub = 512, 256
    last = N_CHIP - 1

    def body(df_ref, gt_ref, up_ref, wg_ref, wu_ref, wd_ref, dy_ref, x2_ref, mix_ref, g2_ref, g3_ref,
             dgt_ref, dup_ref, dx2_ref, dmix_ref, dg3_ref, dg2_ref, dh_ref):
        k, i = pl.program_id(0), pl.program_id(1)
        parts = []
        for s in range(tm // sub):
            rows = slice(s * sub, (s + 1) * sub)
            da = _nt(df_ref[rows, :], wd_ref[...])
            gt, up = gt_ref[rows, :].astype(F32), up_ref[rows, :].astype(F32)
            sg = _sigmoid(gt)
            dup = (da * gt * sg).astype(BF16)
            dgt = (da * up * (sg * (1.0 + gt * (1.0 - sg)))).astype(BF16)
            dup_ref[rows, :] = dup
            dgt_ref[rows, :] = dgt
            parts.append(_nn(dgt, wg_ref[...]) + _nn(dup, wu_ref[...]))
        part = jnp.concatenate(parts, axis=0)
        rows = _rows(i, tm)

        @pl.when(k == 0)
        def _():
            dh_ref[rows, :] = part

        @pl.when((k > 0) & (k < last))
        def _():
            dh_ref[rows, :] = dh_ref[rows, :] + part

        @pl.when((k == last) & (i == 0))
        def _():
            dg3_ref[...] = jnp.zeros_like(dg3_ref)
            dg2_ref[...] = jnp.zeros_like(dg2_ref)

        @pl.when(k == last)
        def _():
            dh = dh_ref[rows, :] + part
            x2 = x2_ref[...]
            r3 = _rstd(x2)
            xn = x2 * r3
            dg3_ref[...] = dg3_ref[...] + jnp.sum(dh * xn, axis=0, keepdims=True)
            t = dh * g3_ref[...]
            dx2 = dy_ref[...] + r3 * (t - xn * jnp.mean(t * xn, axis=-1, keepdims=True))
            dx2_ref[...] = dx2
            mix = mix_ref[...]
            r2 = _rstd(mix)
            mn = mix * r2
            dg2_ref[...] = dg2_ref[...] + jnp.sum(dx2 * mn, axis=0, keepdims=True)
            u = dx2 * g2_ref[...]
            dmix_ref[...] = (r2 * (u - mn * jnp.mean(u * mn, axis=-1, keepdims=True))).astype(BF16)

    wrow = pl.BlockSpec((None, FF_C, D), lambda k, i: (k, 0, 0))
    act = pl.BlockSpec((None, tm, FF_C), lambda k, i: (k, i, 0))
    row = pl.BlockSpec((tm, D), lambda k, i: (i, 0))
    late = pl.BlockSpec((tm, D), lambda k, i: (jnp.where(k == last, i, 0), 0))
    vec = pl.BlockSpec((1, D), lambda k, i: (0, 0))
    return pl.pallas_call(
        body, grid=(N_CHIP, S // tm), name="ffn_bwd_act",
        in_specs=[row, act, act, wrow, wrow, wrow, late, late, late, vec, vec],
        out_specs=[act, act, late, late, vec, vec],
        out_shape=[jax.ShapeDtypeStruct((N_CHIP, S, FF_C), BF16), jax.ShapeDtypeStruct((N_CHIP, S, FF_C), BF16),
                   jax.ShapeDtypeStruct((S, D), F32), jax.ShapeDtypeStruct((S, D), BF16),
                   jax.ShapeDtypeStruct((1, D), F32), jax.ShapeDtypeStruct((1, D), F32)],
        scratch_shapes=[pltpu.VMEM((S, D), F32)],
        compiler_params=_params("arbitrary", "arbitrary"),
    )(df, gt, up, wg, wu, wd, dy, x2, mix, g2, g3)


def _ffn_bwd_w(a, df, h3, dgt, dup):
    tm = 1024
    assert S // tm == 2

    def body(a_ref, df_ref, h_ref, dgt_ref, dup_ref, dwd_ref, dwg_ref, dwu_ref, acc_d, acc_g, acc_u):
        i = pl.program_id(1)
        h = h_ref[...]
        parts = (_tn(a_ref[...], df_ref[...]), _tn(dgt_ref[...], h), _tn(dup_ref[...], h))

        @pl.when(i == 0)
        def _():
            for acc, part in zip((acc_d, acc_g, acc_u), parts):
                acc[...] = part

        @pl.when(i == S // tm - 1)
        def _():
            for out, acc, part in zip((dwd_ref, dwg_ref, dwu_ref), (acc_d, acc_g, acc_u), parts):
                out[...] = (acc[...] + part).astype(BF16)

    act = pl.BlockSpec((None, tm, FF_C), lambda k, i: (k, i, 0))
    row = pl.BlockSpec((tm, D), lambda k, i: (i, 0))
    wrow = pl.BlockSpec((None, FF_C, D), lambda k, i: (k, 0, 0))
    return pl.pallas_call(
        body, grid=(N_CHIP, S // tm), name="ffn_bwd_w",
        in_specs=[act, row, row, act, act],
        out_specs=[wrow, wrow, wrow],
        out_shape=[jax.ShapeDtypeStruct((N_CHIP, FF_C, D), BF16)] * 3,
        scratch_shapes=[pltpu.VMEM((FF_C, D), F32)] * 3,
        compiler_params=_params("parallel", "arbitrary"),
    )(a, df, h3, dgt, dup)


def _mix_bwd(dmix, cat_r, cat_a, wout, exchange, exchange_args):
    tm = 1024

    def body(dm_ref, cr_ref, ca_ref, w_ref, dret_ref, datt_ref, dw_ref, acc, xc):
        i = pl.program_id(0)

        @pl.when(i == 0)
        def _():
            xc.start()
            acc[...] = jnp.zeros_like(acc)

        dm = dm_ref[...]
        dret_ref[...] = _nt(dm, w_ref[0:512, :])
        datt = _nt(dm, w_ref[512:1024, :])
        for j in range(4):
            datt_ref[j] = datt[:, 128 * j:128 * j + 128]
        acc[0:512, :] += _tn(cr_ref[...], dm)
        acc[512:1024, :] += _tn(ca_ref[...], dm)

        @pl.when(i == S // tm - 1)
        def _():
            dw_ref[...] = acc[...].astype(BF16)
            xc.middle()
            xc.finish()

    row = lambda w: pl.BlockSpec((tm, w), lambda i: (i, 0))
    full = pl.BlockSpec((D, D), lambda i: (0, 0))
    return _carry("mix_bwd", body, exchange, exchange_args, (dmix, cat_r, cat_a, wout),
                  [row(D), row(512), row(512), full],
                  [row(512), pl.BlockSpec((4, tm, 128), lambda i: (0, i, 0)), full],
                  [jax.ShapeDtypeStruct((S, 512), F32), jax.ShapeDtypeStruct((4, S, 128), F32),
                   jax.ShapeDtypeStruct((D, D), BF16)],
                  scratch_shapes=[pltpu.VMEM((D, D), F32)], grid=(S // tm,), semantics=("arbitrary",))


def _att_bwd(aq, ak, av, datt, att_out, lse, exchange, exchange_args, after=None):
    def body(q_ref, k_ref, v_ref, do_ref, out_ref, l_ref, dq_ref, dk_ref, dv_ref, xc):
        xc.start()

        lane_head = lax.broadcasted_iota(jnp.int32, (ATT_BLK, 256), 1) // 64
        for pi, d in enumerate(PATTERN_DILATIONS):
            nb, has_prev = _att_blocks(d)
            assert pi > 0 or not has_prev
            bias_rest, bias_first = _att_bias(has_prev)

            def block(b, carry, pi=pi, d=d, nb=nb, has_prev=has_prev, bias_rest=bias_rest, bias_first=bias_first):
                r, ib = b // nb, b % nb
                rows = _class_rows(ib, r, d)
                prow = _class_rows(jnp.maximum(ib - 1, 0), r, d)
                bias = jnp.where(ib == 0, bias_first, bias_rest) if has_prev else bias_first
                for g in range(2):
                    qg = _slab_pair(q_ref, g, rows).astype(BF16)
                    kg = _slab_pair(k_ref, g, rows)
                    vg = _slab_pair(v_ref, g, rows)
                    if has_prev:
                        kg = jnp.concatenate([_slab_pair(k_ref, g, prow), kg], axis=0)
                        vg = jnp.concatenate([_slab_pair(v_ref, g, prow), vg], axis=0)
                    kg, vg = kg.astype(BF16), vg.astype(BF16)
                    dog = _slab_pair(do_ref, g, rows)
                    outg = _slab_pair(out_ref, g, rows)
                    lg = _slab_pair(l_ref, g, rows)
                    qs = _stack_heads(qg, lane_head)
                    dos = _stack_heads(dog, lane_head)
                    delta = jnp.sum(dos * jnp.concatenate([outg] * 4, axis=0), axis=-1, keepdims=True)
                    lh = jnp.max(_stack_heads(lg, lane_head, NEG), axis=-1, keepdims=True)
                    s = _nt(qs, kg) * ATT_SCALE + bias
                    p = jnp.exp(s - lh)
                    dosb = dos.astype(BF16)
                    ds = (p * (_nt(dosb, vg) - delta) * ATT_SCALE).astype(BF16)
                    dq = _unstack_heads(_nn(ds, kg), lane_head)
                    dk = _tn(ds, qs)
                    dv = _tn(p.astype(BF16), dosb)
                    for jj in range(2):
                        j, sl = 2 * g + jj, slice(128 * jj, 128 * jj + 128)
                        if pi == 0:
                            dq_ref[j, rows, :] = dq[:, sl]
                            dk_ref[j, rows, :] = dk[:, sl]
                            dv_ref[j, rows, :] = dv[:, sl]
                            continue
                        dq_ref[j, rows, :] += dq[:, sl]
                        if has_prev:
                            dk_ref[j, prow, :] += dk[0:ATT_BLK, sl]
                            dv_ref[j, prow, :] += dv[0:ATT_BLK, sl]
                            dk_ref[j, rows, :] += dk[ATT_BLK:2 * ATT_BLK, sl]
                            dv_ref[j, rows, :] += dv[ATT_BLK:2 * ATT_BLK, sl]
                        else:
                            dk_ref[j, rows, :] += dk[:, sl]
                            dv_ref[j, rows, :] += dv[:, sl]
                return carry

            lax.fori_loop(0, S // ATT_BLK, block, 0, unroll=4)
        xc.middle()
        xc.finish()

    slab = jax.ShapeDtypeStruct((4, S, 128), F32)
    return _carry("att_bwd", body, exchange, exchange_args, (aq, ak, av, datt, att_out, lse), [VMEM] * 6, [VMEM] * 3,
                  [slab, slab, slab], after=after)


def _ret_bwd(qr, kr, rv, proj, o_raw, states, dret, tabs, exchange, exchange_args, after=None):
    C, G = RET_C, RET_PER_STEP
    steps = S // (C * G)
    dtab, a_tab, b_tab, lam, bd = tabs

    def body(q_ref, k_ref, v_ref, g_ref, o_ref, st_ref, dr_ref, dt_ref, a_ref, b_ref, lam_ref, bd_ref,
             dq_ref, dk_ref, dv_ref, dg_ref, dR, exch):
        @pl.when(pl.program_id(0) == 0)
        def _():
            exch.start()
            dR[...] = jnp.zeros_like(dR)

        lane_head = lax.broadcasted_iota(jnp.int32, (C, 256), 1) // 32
        col_head = lax.broadcasted_iota(jnp.int32, (C, 256), 1) // 64
        for s in reversed(range(G)):
            rows = slice(s * C, (s + 1) * C)
            q, k, v = q_ref[rows, :], k_ref[rows, :], v_ref[rows, :]
            dos = []
            for j in range(4):
                sl = slice(128 * j, 128 * j + 128)
                oj = o_ref[rows, sl]
                xc = oj - _seg_mean(oj)
                rs = lax.rsqrt(_seg_mean(xc * xc) + GN_EPS)
                rn = xc * rs
                gj = g_ref[rows, sl]
                sg = _sigmoid(gj)
                dret = dr_ref[rows, sl]
                dg_ref[rows, sl] = dret * rn * (sg * (1.0 + gj * (1.0 - sg)))
                drn = dret * (gj * sg)
                dos.append(rs * (drn - _seg_mean(drn) - rn * _seg_mean(drn * rn)))
            do = [jnp.concatenate(dos[0:2], axis=1), jnp.concatenate(dos[2:4], axis=1)]
            do8 = jnp.concatenate(do, axis=1).astype(BF16)
            drb = dR[...].astype(BF16)
            rb = st_ref[s]
            dq = _nt(do8, rb) * a_ref[...]
            dk = _nt(v, drb) * b_ref[...]
            kb = (k.astype(F32) * b_ref[...]).astype(BF16)
            dvall = _nn(kb, drb)
            qs = _stack_heads(q, lane_head, n=8)
            dec = dt_ref[...]
            p = (_nt(qs, k) * dec).astype(BF16)
            dos = [_stack_heads(do[g], col_head).astype(BF16) for g in range(2)]
            dp = jnp.concatenate([_nt(dos[g], v[:, 256 * g:256 * g + 256]) for g in range(2)], axis=0)
            ds = (dp * dec).astype(BF16)
            dq = dq + _unstack_heads(_nn(ds, k), lane_head, n=8)
            dk = dk + _tn(ds, qs)
            dv = [dvall[:, 256 * g:256 * g + 256] + _tn(p[4 * C * g:4 * C * (g + 1)], dos[g]) for g in range(2)]
            qa = (q.astype(F32) * a_ref[...]).astype(BF16)
            dR[...] = dR[...] * lam_ref[...] + _tn(qa, do8) * bd_ref[...]
            dq_ref[rows, :] = dq
            dk_ref[rows, :] = dk
            dv_ref[rows, 0:256] = dv[0]
            dv_ref[rows, 256:512] = dv[1]

        @pl.when(pl.program_id(0) == steps - 1)
        def _():
            exch.middle()
            exch.finish()

    rev = lambda w: pl.BlockSpec((C * G, w), lambda n: (steps - 1 - n, 0))
    full = lambda a: pl.BlockSpec(a.shape, lambda n: (0,) * a.ndim)
    return _carry(
        "ret_bwd", body, exchange, exchange_args, (qr, kr, rv, proj, o_raw, states, dret, dtab, a_tab, b_tab, lam, bd),
        [rev(256), rev(256), rev(512), rev(512), rev(512),
         pl.BlockSpec((G, 256, 512), lambda n: (steps - 1 - n, 0, 0)), rev(512),
         full(dtab), full(a_tab), full(b_tab), full(lam), full(bd)],
        [rev(256), rev(256), rev(512), rev(512)],
        [jax.ShapeDtypeStruct((S, 256), F32), jax.ShapeDtypeStruct((S, 256), F32),
         jax.ShapeDtypeStruct((S, 512), F32), jax.ShapeDtypeStruct((S, 512), F32)],
        scratch_shapes=[pltpu.VMEM((256, 512), F32)], grid=(steps,), semantics=("arbitrary",), after=after)


def _rot_bwd(cos, sin, spread, dqr, dkr, drv, drg, dq_att, dk_att, dv_att):
    tm = 256

    def body(cos_ref, sin_ref, e_ref, dqr_ref, dkr_ref, drv_ref, drg_ref, dqa_ref, dka_ref, dva_ref, dp_ref):
        cr, ca, sr, sa = _rot_tables(cos_ref, sin_ref, e_ref)
        lo_r, lo_a = _rot_halves(tm)

        def unrot_r(g):
            gs = g * sr
            return g * cr + pltpu.roll(jnp.where(lo_r, -gs, 0.0), 16, 1) + pltpu.roll(jnp.where(lo_r, 0.0, gs), 240, 1)

        def unrot_a(g):
            gs = g * sa
            return g * ca + pltpu.roll(jnp.where(lo_a, -gs, 0.0), 8, 1) + pltpu.roll(jnp.where(lo_a, 0.0, gs), 504, 1)

        def wide(ref):
            return jnp.concatenate([ref[j] for j in range(4)], axis=1)

        dp_ref[:, 0:256] = unrot_r(dqr_ref[...]).astype(BF16)
        dp_ref[:, 256:512] = unrot_r(dkr_ref[...] * RET_SCALE).astype(BF16)
        dp_ref[:, 512:1024] = drv_ref[...].astype(BF16)
        dp_ref[:, 1024:1536] = drg_ref[...].astype(BF16)
        dp_ref[:, 1536:2048] = unrot_a(wide(dqa_ref)).astype(BF16)
        dp_ref[:, 2048:2560] = unrot_a(wide(dka_ref)).astype(BF16)
        dp_ref[:, 2560:3072] = wide(dva_ref).astype(BF16)

    row = lambda w: pl.BlockSpec((tm, w), lambda i: (i, 0))
    slab = pl.BlockSpec((4, tm, 128), lambda i: (0, i, 0))
    return pl.pallas_call(
        body, grid=(S // tm,), name="rot_bwd",
        in_specs=[row(128), row(128), pl.BlockSpec((128, 768), lambda i: (0, 0)),
                  row(256), row(256), row(512), row(512), slab, slab, slab],
        out_specs=row(PW), out_shape=jax.ShapeDtypeStruct((S, PW), BF16),
        compiler_params=_params("parallel"),
    )(cos, sin, spread, dqr, dkr, drv, drg, dq_att, dk_att, dv_att)


def _win_bwd_w(h1, dproj, exchange, exchange_args):
    def body(h_ref, dp_ref, dw_ref, xc):
        k = pl.program_id(0)

        @pl.when(k == 0)
        def _():
            xc.start()

        dw_ref[...] = _tn(h_ref[...], dp_ref[...]).astype(BF16)

        @pl.when(k == N_CHIP - 1)
        def _():
            xc.middle()
            xc.finish()

    (dw,), out = _carry(
        "win_bwd_w", body, exchange, exchange_args, (h1, dproj),
        [pl.BlockSpec((S, D), lambda k: (0, 0)), pl.BlockSpec((S, WIN_C), lambda k: (0, k))],
        [pl.BlockSpec((None, D, WIN_C), lambda k: (k, 0, 0))],
        [jax.ShapeDtypeStruct((N_CHIP, D, WIN_C), BF16)], grid=(N_CHIP,), semantics=("arbitrary",))
    return dw, out


def _in_bwd(dproj, win_g, x, dx2, g1, other_rows, after):
    tm = 512
    n = len(other_rows)

    def body(dp_ref, w_ref, x_ref, dx2_ref, g_ref, *refs):
        rows, dx_ref, blk_ref = refs[:n], refs[n], refs[n + 1]

        @pl.when(pl.program_id(0) == 0)
        def _():
            blk_ref[...] = jnp.zeros_like(blk_ref)
            for i, r_ref in enumerate(rows):
                blk_ref[i + 1:i + 2, :] = r_ref[...]

        dh = _nt(dp_ref[:, 0:WIN_C], w_ref[0])
        for k in range(1, N_CHIP):
            dh = dh + _nt(dp_ref[:, k * WIN_C:(k + 1) * WIN_C], w_ref[k])
        xv = x_ref[...]
        r = _rstd(xv)
        xn = xv * r
        blk_ref[0:1, :] = blk_ref[0:1, :] + jnp.sum(dh * xn, axis=0, keepdims=True)
        t = dh * g_ref[...]
        dx_ref[...] = dx2_ref[...] + r * (t - xn * jnp.mean(t * xn, axis=-1, keepdims=True))

    row = lambda w: pl.BlockSpec((tm, w), lambda i: (i, 0))
    vec = pl.BlockSpec((1, D), lambda i: (0, 0))
    return _carry("in_bwd", body, _NoExchange(), (), (dproj, win_g, x, dx2, g1, *other_rows),
                  [row(PW), pl.BlockSpec((N_CHIP, D, WIN_C), lambda i: (0, 0, 0)), row(D), row(D), vec] + [vec] * n,
                  [row(D), pl.BlockSpec((8, D), lambda i: (0, 0))],
                  [jax.ShapeDtypeStruct((S, D), F32), jax.ShapeDtypeStruct((8, D), F32)],
                  grid=(S // tm,), semantics=("arbitrary",), after=after)[0]


ANY = pl.BlockSpec(memory_space=pl.ANY)
VMEM = pl.BlockSpec(memory_space=pltpu.VMEM)
FLIPS = ((1, 0), (0, 1), (1, 1))


def _place():
    x, y, c = lax.axis_index("x"), lax.axis_index("y"), lax.axis_index("c")
    chips = [((1 - x) if fx else x, (1 - y) if fy else y) for fx, fy in FLIPS]
    return x, y, c, 2 * x + y, chips


def _remote(src, dst, send_sem, recv_sem, device):
    return pltpu.make_async_remote_copy(src_ref=src, dst_ref=dst, send_sem=send_sem, recv_sem=recv_sem,
                                        device_id=device, device_id_type=MESH)


class _Exchange:
    aliases = {}

    def middle(self, ins, outs, sems):
        pass


class _GatherShards(_Exchange):
    def __init__(self, shards):
        n = self.n = len(shards)
        self.n_in = self.n_out = n
        self.out_shape = [jax.ShapeDtypeStruct((N_CHIP,) + s.shape, s.dtype) for s in shards]
        dma = pltpu.SemaphoreType.DMA
        self.scratch = [dma((3 * n,)), dma((3 * n,)), dma((3 * n,)), dma((3 * n,)), dma((n,)), dma((n,))]

    def _ici(self, ins, outs, sems, a, j, chip):
        x, y, c, me, chips = _place()
        half = ins[a].shape[0] // 2
        return _remote(ins[a].at[pl.ds(c * half, half), :], outs[a].at[me, pl.ds(c * half, half), :],
                       sems[0].at[3 * a + j], sems[1].at[3 * a + j], (*chip, c))

    def _fwd(self, outs, sems, a, j, chip, half_of):
        x, y, c, me, chips = _place()
        half = outs[a].shape[1] // 2
        blk = outs[a].at[2 * chip[0] + chip[1], pl.ds(half_of * half, half), :]
        return _remote(blk, blk, sems[2].at[3 * a + j], sems[3].at[3 * a + j], (x, y, 1 - c))

    def _own(self, ins, outs, sems, a):
        return _own_shard_to_sibling(ins[a], outs[a], sems[4].at[a], sems[5].at[a])

    def start(self, ins, outs, sems):
        chips = _place()[4]
        for a in range(self.n):
            for j, chip in enumerate(chips):
                self._ici(ins, outs, sems, a, j, chip).start()
        for a in range(self.n):
            self._own(ins, outs, sems, a).start()

    def middle(self, ins, outs, sems):
        x, y, c, me, chips = _place()
        for a in range(self.n):
            for j, chip in enumerate(chips):
                half = outs[a].shape[1] // 2
                blk = outs[a].at[2 * chip[0] + chip[1], pl.ds(c * half, half), :]
                _remote(blk, blk, sems[0].at[3 * a + j], sems[1].at[3 * a + j], (x, y, c)).wait_recv()
                self._fwd(outs, sems, a, j, chip, c).start()

    def finish(self, ins, outs, sems):
        x, y, c, me, chips = _place()
        for a in range(self.n):
            for j, chip in enumerate(chips):
                self._fwd(outs, sems, a, j, chip, 1 - c).wait_recv()
        for a in range(self.n):
            for j, chip in enumerate(chips):
                self._ici(ins, outs, sems, a, j, chip).wait_send()
                self._fwd(outs, sems, a, j, chip, c).wait_send()
            self._own(ins, outs, sems, a).wait()


def _own_shard_to_sibling(shard_ref, gathered_ref, send_sem, recv_sem):
    x, y, c, me, chips = _place()
    return _remote(shard_ref, gathered_ref.at[me], send_sem, recv_sem, (x, y, 1 - c))


class _NoExchange(_Exchange):
    n_in = n_out = 0
    out_shape = ()
    scratch = ()

    def start(self, ins, outs, sems):
        pass

    def finish(self, ins, outs, sems):
        pass


class _ForwardGathered(_Exchange):
    def __init__(self, shards, own=True, forward=True):
        self.own, self.forward = own, forward
        n = self.n = len(shards)
        self.n_in, self.n_out = 2 * n, n
        self.out_shape = [jax.ShapeDtypeStruct((N_CHIP,) + s.shape, s.dtype) for s in shards]
        dma = pltpu.SemaphoreType.DMA
        self.scratch = [dma((3 * n,)), dma((3 * n,)), dma((n,)), dma((n,))]
        self.aliases = {n + a: a for a in range(n)}

    def _fwd(self, outs, sems, a, j, chip, half_of):
        x, y, c, me, chips = _place()
        half = outs[a].shape[1] // 2
        blk = outs[a].at[2 * chip[0] + chip[1], pl.ds(half_of * half, half), :]
        return _remote(blk, blk, sems[0].at[3 * a + j], sems[1].at[3 * a + j], (x, y, 1 - c))

    def _own(self, ins, outs, sems, a):
        return _own_shard_to_sibling(ins[a], outs[a], sems[2].at[a], sems[3].at[a])

    def start(self, ins, outs, sems):
        x, y, c, me, chips = _place()
        for a in range(self.n):
            for j, chip in enumerate(chips if self.forward else ()):
                self._fwd(outs, sems, a, j, chip, c).start()
        for a in range(self.n if self.own else 0):
            self._own(ins, outs, sems, a).start()

    def finish(self, ins, outs, sems):
        x, y, c, me, chips = _place()
        for a in range(self.n):
            for j, chip in enumerate(chips if self.forward else ()):
                self._fwd(outs, sems, a, j, chip, 1 - c).wait_recv()
        for a in range(self.n):
            for j, chip in enumerate(chips if self.forward else ()):
                self._fwd(outs, sems, a, j, chip, c).wait_send()
            if self.own:
                self._own(ins, outs, sems, a).wait()


HBM = pl.BlockSpec(memory_space=pltpu.HBM)
SEMS = pl.BlockSpec(memory_space=pltpu.SEMAPHORE)
DATAFLOW = pltpu.SideEffectType.DATAFLOW_SIDE_EFFECTING


class _OverIci:
    def __init__(self, name, sources, lands):
        self.name, self.n = name, len(sources)
        hbm = lambda t: pltpu.with_memory_space_constraint(t, pltpu.HBM)
        self.arrays = [hbm(t) for t in sources] + [hbm(t) for t in lands]

    def sent(self, src, land, a, chip):
        raise NotImplementedError

    def landed(self, land, a, chip):
        raise NotImplementedError

    def _copy(self, arr, sems, a, j, receiving):
        x, y, c, me, chips = _place()
        src, dst = self.sent(arr[a], arr[self.n + a], a, chips[j])
        if receiving:
            dst = self.landed(arr[self.n + a], a, chips[j])
        return _remote(src, dst, sems[0].at[3 * a + j], sems[1].at[3 * a + j], (*chips[j], c))

    def start(self, after):
        m = len(self.arrays)

        def body(*refs):
            arr, sems, token = refs[:m], refs[m + 1:m + 3], refs[-1]
            for a in range(self.n):
                for j in range(3):
                    self._copy(arr, sems, a, j, False).start()
            token[...] = jnp.zeros_like(token)

        dma = pltpu.SemaphoreType.DMA
        outs = pl.pallas_call(
            body, name=self.name + "_start",
            out_shape=[dma((3 * self.n,)), dma((3 * self.n,))] + [pltpu.HBM(t.shape, t.dtype) for t in self.arrays]
                      + [jax.ShapeDtypeStruct((8, 128), F32)],
            in_specs=[HBM] * m + [ANY], out_specs=[SEMS, SEMS] + [HBM] * m + [VMEM],
            input_output_aliases={i: 2 + i for i in range(m)},
            compiler_params=pltpu.CompilerParams(has_side_effects=DATAFLOW),
        )(*self.arrays, after)
        self.sems, self.arrays = outs[0:2], list(outs[2:2 + m])
        return outs[-1]

    def wait(self, after):
        m = len(self.arrays)

        def body(*refs):
            arr, sems = refs[:m], refs[m:m + 2]
            for a in range(self.n):
                for j in range(3):
                    self._copy(arr, sems, a, j, False).wait_send()
                    self._copy(arr, sems, a, j, True).wait_recv()

        outs = pl.pallas_call(
            body, name=self.name + "_wait",
            out_shape=[pltpu.HBM(t.shape, t.dtype) for t in self.arrays],
            in_specs=[HBM] * m + [SEMS, SEMS, ANY], out_specs=[HBM] * m,
            input_output_aliases={i: i for i in range(m)},
            compiler_params=pltpu.CompilerParams(has_side_effects=DATAFLOW),
        )(*self.arrays, *self.sems, after)
        return list(outs[:self.n]), list(outs[self.n:])


class _GatherOverIci(_OverIci):
    def __init__(self, name, shards):
        super().__init__(name, shards, [lax.empty((N_CHIP,) + s.shape, s.dtype) for s in shards])

    @staticmethod
    def _half(ref):
        c = lax.axis_index("c")
        half = ref.shape[-2] // 2
        return pl.ds(c * half, half)

    def sent(self, src, land, a, chip):
        return src.at[self._half(src), :], land.at[_place()[3], self._half(src), :]

    def landed(self, land, a, chip):
        return land.at[2 * chip[0] + chip[1], self._half(land), :]


class _SumOverIci(_OverIci):
    def __init__(self, name, pre):
        super().__init__(name, pre, [lax.empty(p.shape, p.dtype) for p in pre])

    def sent(self, src, land, a, chip):
        return src.at[2 * chip[0] + chip[1]], land.at[_place()[3]]

    def landed(self, land, a, chip):
        return land.at[2 * chip[0] + chip[1]]


class _HalvesToSibling(_Exchange):
    def __init__(self, grads):
        n = self.n = len(grads)
        self.n_in = self.n_out = n
        self.out_shape = [jax.ShapeDtypeStruct((N_CHIP, g.shape[1] // 2, g.shape[2]), g.dtype) for g in grads]
        self.scratch = [pltpu.SemaphoreType.DMA((n,)), pltpu.SemaphoreType.DMA((n,))]

    def _copy(self, ins, outs, sems, a):
        x, y, c, me, chips = _place()
        half = ins[a].shape[1] // 2
        return _remote(ins[a].at[:, pl.ds((1 - c) * half, half), :], outs[a], sems[0].at[a], sems[1].at[a], (x, y, 1 - c))

    def start(self, ins, outs, sems):
        for a in range(self.n):
            self._copy(ins, outs, sems, a).start()

    def finish(self, ins, outs, sems):
        for a in range(self.n):
            self._copy(ins, outs, sems, a).wait_recv()
        for a in range(self.n):
            self._copy(ins, outs, sems, a).wait_send()


class _ShareHalves(_Exchange):
    def __init__(self, fulls):
        n = self.n = len(fulls)
        self.n_in = self.n_out = n
        self.out_shape = [jax.ShapeDtypeStruct(f.shape, f.dtype) for f in fulls]
        self.scratch = [pltpu.SemaphoreType.DMA((n,)), pltpu.SemaphoreType.DMA((n,))]
        self.aliases = {a: a for a in range(n)}

    def _copy(self, outs, sems, a, half_of):
        x, y, c, me, chips = _place()
        half = outs[a].shape[0] // 2
        rows = outs[a].at[pl.ds(half_of * half, half), :]
        return _remote(rows, rows, sems[0].at[a], sems[1].at[a], (x, y, 1 - c))

    def start(self, ins, outs, sems):
        c = _place()[2]
        for a in range(self.n):
            self._copy(outs, sems, a, c).start()

    def finish(self, ins, outs, sems):
        c = _place()[2]
        for a in range(self.n):
            self._copy(outs, sems, a, 1 - c).wait_recv()
        for a in range(self.n):
            self._copy(outs, sems, a, c).wait_send()


class _GatherBlocks(_Exchange):
    def __init__(self, block):
        self.n_in = self.n_out = 1
        self.out_shape = [jax.ShapeDtypeStruct((8,) + block.shape, block.dtype)]
        dma = pltpu.SemaphoreType.DMA
        self.scratch = [dma((7,)), dma((7,)), dma]

    @staticmethod
    def _peer(f):
        x, y, c, me, chips = _place()
        return ((1 - x) if f & 4 else x, (1 - y) if f & 2 else y, (1 - c) if f & 1 else c)

    def start(self, ins, outs, sems):
        x, y, c, me, chips = _place()
        for f in range(1, 8):
            _remote(ins[0], outs[0].at[2 * me + c], sems[0].at[f - 1], sems[1].at[f - 1], self._peer(f)).start()
        pltpu.make_async_copy(ins[0], outs[0].at[2 * me + c], sems[2]).start()

    def finish(self, ins, outs, sems):
        x, y, c, me, chips = _place()
        for f in range(1, 8):
            px, py, pc = self._peer(f)
            blk = outs[0].at[4 * px + 2 * py + pc]
            _remote(blk, blk, sems[0].at[f - 1], sems[1].at[f - 1], (x, y, c)).wait_recv()
        for f in range(1, 8):
            _remote(ins[0], outs[0].at[2 * me + c], sems[0].at[f - 1], sems[1].at[f - 1], self._peer(f)).wait_send()
        pltpu.make_async_copy(ins[0], outs[0].at[2 * me + c], sems[2]).wait()


class _Both(_Exchange):
    def __init__(self, first, second):
        self.parts = (first, second)
        self.n_in, self.n_out = first.n_in + second.n_in, first.n_out + second.n_out
        self.out_shape = first.out_shape + second.out_shape
        self.scratch = first.scratch + second.scratch
        self.aliases = dict(first.aliases)
        self.aliases.update({first.n_in + i: first.n_out + o for i, o in second.aliases.items()})

    def _split(self, ins, outs, sems):
        a, b = self.parts
        return ((a, ins[:a.n_in], outs[:a.n_out], sems[:len(a.scratch)]),
                (b, ins[a.n_in:], outs[a.n_out:], sems[len(a.scratch):]))

    def start(self, ins, outs, sems):
        for ex, i, o, s in self._split(ins, outs, sems):
            ex.start(i, o, s)

    def middle(self, ins, outs, sems):
        for ex, i, o, s in self._split(ins, outs, sems):
            ex.middle(i, o, s)

    def finish(self, ins, outs, sems):
        for ex, i, o, s in self._split(ins, outs, sems):
            ex.finish(i, o, s)


class _Bound:
    def __init__(self, ex, ins, outs, sems):
        self.start = lambda: ex.start(ins, outs, sems)
        self.middle = lambda: ex.middle(ins, outs, sems)
        self.finish = lambda: ex.finish(ins, outs, sems)


def _carry(name, body, ex, ex_args, args, in_specs, out_specs, out_shape, scratch_shapes=(), grid=None, semantics=(),
           after=None):
    n_a, n_o, n_s = len(args), len(out_shape), len(scratch_shapes)
    behind = [] if after is None else [after]

    def full_body(*refs):
        p = 0
        groups = []
        for size in (n_a, ex.n_in, len(behind), n_o, ex.n_out, n_s, len(ex.scratch)):
            groups.append(refs[p:p + size])
            p += size
        a, ei, _, o, eo, s, es = groups
        body(*a, *o, *s, _Bound(ex, ei, eo, es))

    kwargs = {} if grid is None else {"grid": grid}
    outs = pl.pallas_call(
        full_body, name=name,
        in_specs=list(in_specs) + [ANY] * (ex.n_in + len(behind)), out_specs=list(out_specs) + [ANY] * ex.n_out,
        out_shape=list(out_shape) + list(ex.out_shape), scratch_shapes=list(scratch_shapes) + list(ex.scratch),
        input_output_aliases={n_a + i: n_o + o for i, o in ex.aliases.items()},
        compiler_params=_params(*semantics) if semantics else pltpu.CompilerParams(vmem_limit_bytes=VMEM_LIMIT),
        **kwargs,
    )(*args, *ex_args, *behind)
    return outs[:n_o], outs[n_o:]


def _cast_bf16(arrays):
    n = len(arrays)

    def body(*refs):
        for a in range(n):
            refs[n + a][...] = refs[a][...].astype(BF16)

    blks = [pl.BlockSpec((t.shape[0] // 4, t.shape[1]), lambda i: (i, 0)) for t in arrays]
    return pl.pallas_call(
        body, grid=(4,), name="cast_bf16", in_specs=blks, out_specs=blks,
        out_shape=[jax.ShapeDtypeStruct(t.shape, BF16) for t in arrays], compiler_params=_params("parallel"),
    )(*arrays)


def _prepare(x, g1, pos, ifc, after):
    tm = 512

    def body(x_ref, g_ref, pos_ref, ifc_ref, h_ref, cos_ref, sin_ref, _):
        xv = x_ref[...]
        h_ref[...] = (xv * _rstd(xv) * g_ref[...]).astype(BF16)
        ang = pos_ref[...].astype(F32) * ifc_ref[...]
        cos_ref[...] = jnp.cos(ang)
        sin_ref[...] = jnp.sin(ang)

    row = lambda w: pl.BlockSpec((tm, w), lambda i: (i, 0))
    const = lambda w: pl.BlockSpec((1, w), lambda i: (0, 0))
    return _carry("prepare", body, _NoExchange(), (), (x, g1, pos, ifc),
                  [row(D), const(D), row(1), const(128)], [row(D), row(128), row(128)],
                  [jax.ShapeDtypeStruct((S, D), BF16)] + [jax.ShapeDtypeStruct((S, 128), F32)] * 2,
                  grid=(S // tm,), semantics=("parallel",), after=after)[0]


def _exchange_alone(name, ex, ex_args):
    def body(xc):
        xc.start()
        xc.middle()
        xc.finish()

    return _carry(name, body, ex, ex_args, (), (), (), ())[1]


def _core_index():
    return lax.axis_index("c").astype(jnp.int32).reshape(1)


def _pair_sum(gs, gots):
    n = len(gs)

    def body(c_ref, *refs):
        for a in range(n):
            refs[2 * n + a][...] = (refs[a][...].astype(F32) + refs[n + a][...].astype(F32)).astype(BF16)

    mine = [pl.BlockSpec((None, g.shape[1] // 2, g.shape[2]), lambda k, c_ref: (k, c_ref[0], 0)) for g in gs]
    blk = [pl.BlockSpec((None, g.shape[1] // 2, g.shape[2]), lambda k, c_ref: (k, 0, 0)) for g in gs]
    return pl.pallas_call(
        body, name=f"pair_sum_{gs[0].shape[1]}x{gs[0].shape[2]}",
        grid_spec=pltpu.PrefetchScalarGridSpec(
            num_scalar_prefetch=1, grid=(N_CHIP,), in_specs=mine + blk, out_specs=blk),
        out_shape=[jax.ShapeDtypeStruct((N_CHIP, g.shape[1] // 2, g.shape[2]), BF16) for g in gs],
        compiler_params=_params("parallel"),
    )(_core_index(), *gs, *gots)


def _chip_sum(pre, parts):
    n = len(parts)
    me = 2 * lax.axis_index("x") + lax.axis_index("y")
    others = [k + (k >= me).astype(jnp.int32) for k in range(3)]
    where = jnp.stack([lax.axis_index("c"), me, *others]).astype(jnp.int32)

    def body(w_ref, *refs):
        for a in range(n):
            own, p1, p2, p3 = refs[4 * a:4 * a + 4]
            refs[4 * n + a][...] = ((own[...].astype(F32) + p1[...].astype(F32)) + p2[...].astype(F32)) + p3[...].astype(F32)

    in_specs, out_specs, operands = [], [], []
    for a in range(n):
        _, half, cc = parts[a].shape
        tr = half // 2
        in_specs += [pl.BlockSpec((None, tr, cc), lambda i, w_ref, s=s: (w_ref[s], i, 0)) for s in (1, 2, 3, 4)]
        out_specs.append(pl.BlockSpec((tr, cc), lambda i, w_ref: (2 * w_ref[0] + i, 0)))
        operands += [pre[a], parts[a], parts[a], parts[a]]
    return pl.pallas_call(
        body, name=f"chip_sum_{parts[0].shape[1]}x{parts[0].shape[2]}",
        grid_spec=pltpu.PrefetchScalarGridSpec(num_scalar_prefetch=1, grid=(2,), in_specs=in_specs, out_specs=out_specs),
        out_shape=[jax.ShapeDtypeStruct((2 * p.shape[1], p.shape[2]), F32) for p in parts],
        compiler_params=_params("parallel"),
    )(where, *operands)


def _adamw_math(w, g, m, v):
    m = ADAM_B1 * m + (1.0 - ADAM_B1) * g
    v = ADAM_B2 * v + (1.0 - ADAM_B2) * (g * g)
    m_hat = m / (1.0 - ADAM_B1 ** ADAM_STEP)
    v_hat = v / (1.0 - ADAM_B2 ** ADAM_STEP)
    delta = -ADAM_LR * (m_hat / (jnp.sqrt(v_hat) + ADAM_EPS) + ADAM_WD * w)
    return delta, m, v


def _adamw(ws, gs, ms, vs, after=None):
    n = len(ws)

    def body(*refs):
        for a in range(n):
            w_ref, g_ref, m_ref, v_ref = (refs[t * n + a] for t in range(4))
            go_ref, d_ref, nm_ref, nv_ref = refs[4 * n + 4 * a:4 * n + 4 * a + 4]
            g = g_ref[...]
            go_ref[...] = g
            d_ref[...], nm_ref[...], nv_ref[...] = _adamw_math(w_ref[...], g, m_ref[...], v_ref[...])

    blks = [pl.BlockSpec((w.shape[0] // 4, w.shape[1]), lambda i: (i, 0)) for w in ws]
    outs = _carry(f"adamw_{ws[0].shape[0]}x{ws[0].shape[1]}", body, _NoExchange(), (), (*ws, *gs, *ms, *vs),
                  blks * 4, [b for b in blks for _ in range(4)],
                  [jax.ShapeDtypeStruct(w.shape, F32) for w in ws for _ in range(4)],
                  grid=(4,), semantics=("parallel",), after=after)[0]
    return [outs[4 * a:4 * a + 4] for a in range(n)]


def _adamw_gains(gall, ws, ms, vs):
    def body(ga_ref, *refs):
        w, m, v = refs[0:4], refs[4:8], refs[8:12]
        outs, loss_ref, total = refs[12:28], refs[28], refs[29]
        g = ga_ref[0]
        for dev in range(1, 8):
            g = g + ga_ref[dev]
        total[...] = g
        for i in range(4):
            gi = total[i:i + 1, :]
            outs[i][...] = gi
            outs[4 + i][...], outs[8 + i][...], outs[12 + i][...] = _adamw_math(w[i][...], gi, m[i][...], v[i][...])
        loss_ref[...] = total[4:5, 0:128] * (0.5 / D)

    outs = pl.pallas_call(
        body, name="adamw_gains",
        out_shape=[jax.ShapeDtypeStruct((1, D), F32)] * 16 + [jax.ShapeDtypeStruct((1, 128), F32)],
        scratch_shapes=[pltpu.VMEM((8, D), F32)],
    )(gall, *ws, *ms, *vs)
    return outs[0:4], outs[4:8], outs[8:12], outs[12:16], outs[16]


def kernel(x, positions, w_in, w_out, g_pre_mix, g_post_mix, g_pre_ffn, g_post_ffn, w_gate, w_up, w_down, loss_target, m_w_in, m_w_out, m_g_pre_mix, m_g_post_mix, m_g_pre_ffn, m_g_post_ffn, m_w_gate, m_w_up, m_w_down, v_w_in, v_w_out, v_g_pre_mix, v_g_post_mix, v_g_pre_ffn, v_g_post_ffn, v_w_gate, v_w_up, v_w_down):
    tr = lambda t: jnp.swapaxes(t, 1, 2)[0]
    shards = [w_in[0], w_out[0], tr(w_gate), tr(w_up), w_down[0]]
    moms = [m_w_in[0], m_w_out[0], tr(m_w_gate), tr(m_w_up), m_w_down[0]]
    vels = [v_w_in[0], v_w_out[0], tr(v_w_gate), tr(v_w_up), v_w_down[0]]
    xs, pos, tgt = x[0], positions.reshape(S, 1), loss_target[0]
    g1, g2, g3, g4 = g_pre_mix, g_post_mix, g_pre_ffn, g_post_ffn
    tabs = tuple(jnp.asarray(t) for t in _retention_tables())
    ifc, spread = _rotary_tables()
    ifc, spread = jnp.asarray(ifc), jnp.asarray(spread, dtype=BF16)
    bf = list(_cast_bf16(shards))

    win_gather = _GatherOverIci("win_gather", bf[:1])
    token = win_gather.start(bf[0])
    wout_gather = _GatherOverIci("wout_gather", bf[1:2])
    token = wout_gather.start(token)
    ffn_gather = _GatherOverIci("ffn_gather", bf[2:])
    token = ffn_gather.start(token)
    h1, cos, sin = _prepare(xs, g1, pos, ifc, token)
    win_sh, win_land = win_gather.wait(h1)
    (win_g,) = _exchange_alone("forward_win", _ForwardGathered(bf[:1]), [*win_sh, *win_land])
    qr, kr, rv, rg, aq, ak, av = _proj_fwd(h1, win_g, cos, sin, spread, None)
    wout_sh, wout_land = wout_gather.wait(qr)
    n_ffn = len(bf[2:])
    (att_out, lse, cat_a), (wout_g, *ffn_gather.arrays[n_ffn:]) = _att_fwd(
        aq, ak, av, _Both(_ForwardGathered(bf[1:2]), _ForwardGathered(bf[2:], forward=False)),
        [*wout_sh, *wout_land, *ffn_gather.arrays])
    wout_g = wout_g.reshape(D, D)
    (o_raw, cat_r, states), _ = _ret_fwd(qr, kr, rv, rg, tabs, _NoExchange(), (), cat_a)
    ffn_sh, ffn_lands = ffn_gather.wait(cat_r)
    (mix, x2, h3), (wg_g, wu_g, wd_g) = _mix_fwd(cat_r, cat_a, wout_g, xs, g2, g3,
                                                _ForwardGathered(bf[2:], own=False), [*ffn_sh, *ffn_lands])
    gt, up, a, sq, dy, df, dg4 = _ffn_fwd(h3, wg_g, wu_g, wd_g, x2, tgt, g4)

    dgt, dup, dx2, dmix, dg3, dg2 = _ffn_bwd_act(df, gt, up, wg_g, wu_g, wd_g, dy, x2, mix, g2, g3)
    ffn_grads = list(_ffn_bwd_w(a, df, h3, dgt, dup))
    (dret, datt, dwout), got = _mix_bwd(dmix, cat_r, cat_a, wout_g, _HalvesToSibling(ffn_grads), ffn_grads)
    ffn_sum = _SumOverIci("ffn_sum", _pair_sum(ffn_grads, got))
    token = ffn_sum.start(datt)
    (dq_att, dk_att, dv_att), _ = _att_bwd(aq, ak, av, datt, att_out, lse, _NoExchange(), (), token)
    (dqr, dkr, drv, drg), _ = _ret_bwd(qr, kr, rv, rg, o_raw, states, dret, tabs, _NoExchange(), (), token)
    dproj = _rot_bwd(cos, sin, spread, dqr, dkr, drv, drg, dq_att, dk_att, dv_att)
    sums = _chip_sum(*ffn_sum.wait(dproj))
    dwin, ffn_full = _win_bwd_w(h1, dproj, _ShareHalves(sums), sums)
    in_grads = [dwin, dwout.reshape(N_CHIP, WOUT_R, D)]

    got = _exchange_alone("halves_to_sibling", _HalvesToSibling(in_grads), in_grads)
    in_sum = _SumOverIci("in_sum", _pair_sum(in_grads, got))
    token = in_sum.start(dproj)
    dx, gblock = _in_bwd(dproj, win_g, xs, dx2, g1, [dg2, dg3, dg4, sq], token)
    ffn_upd = _adamw(shards[2:], [ffn_full[o] for o in (1, 2, 0)],
                     moms[2:], vels[2:], token)
    pre, parts = in_sum.wait(ffn_upd[2][0])
    sums = _chip_sum(pre, parts)
    *in_full, gall = _exchange_alone("share_rest", _Both(_ShareHalves(sums), _GatherBlocks(gblock)), [*sums, gblock])
    upd = _adamw(shards[:2], in_full, moms[:2], vels[:2]) + ffn_upd
    gg, gd, gm, gv, loss_row = _adamw_gains(gall, [g1, g2, g3, g4],
                                            [m_g_pre_mix, m_g_post_mix, m_g_pre_ffn, m_g_post_ffn],
                                            [v_g_pre_mix, v_g_post_mix, v_g_pre_ffn, v_g_post_ffn])

    def order(mats, vecs):
        back = lambda t: jnp.swapaxes(t[None], 1, 2)
        return [mats[0][None], mats[1][None], *vecs, back(mats[2]), back(mats[3]), mats[4][None]]

    return (loss_row[0, 0], dx[None],
            *order([u[0] for u in upd], gg),
            *order([u[1] for u in upd], gd),
            *order([u[2] for u in upd], gm),
            *order([u[3] for u in upd], gv))
```

```python
import numpy as np
import jax
import jax.numpy as jnp
from jax import lax
from jax.experimental import pallas as pl
from jax.experimental.pallas import tpu as pltpu

F32, BF16 = jnp.float32, jnp.bfloat16
MESH = pl.DeviceIdType.MESH

S = 2048
D = 1024
PW = 3072
N_CHIP = 4
WIN_C = PW // N_CHIP
DFF = 2816
FF_C = DFF // N_CHIP
WOUT_R = D // N_CHIP
RMS_EPS = 1e-6
GN_EPS = 1e-5
RET_C = 128
RET_PER_STEP = 4
RET_SCALE = 32 ** -0.5
ATT_BLK = 128
ATT_SCALE = 64 ** -0.5
PATTERN_DILATIONS = (16, 1, 4)
NEG = -1e30
VMEM_LIMIT = 56 * 1024 * 1024

ADAM_LR, ADAM_B1, ADAM_B2, ADAM_EPS, ADAM_WD, ADAM_STEP = 0.001, 0.9, 0.999, 1e-08, 0.01, 10


def _params(*sem):
    return pltpu.CompilerParams(dimension_semantics=sem, vmem_limit_bytes=VMEM_LIMIT)


def _nt(a, b):
    return lax.dot_general(a, b, (((1,), (1,)), ((), ())), preferred_element_type=F32)


def _tn(a, b):
    return lax.dot_general(a, b, (((0,), (0,)), ((), ())), preferred_element_type=F32)


def _nn(a, b):
    return jnp.dot(a, b, preferred_element_type=F32)


def _rstd(v):
    return lax.rsqrt(jnp.mean(v * v, axis=-1, keepdims=True) + RMS_EPS)


def _sigmoid(v):
    return 1.0 / (1.0 + jnp.exp(-v))


def _rows(i, t):
    return pl.ds(pl.multiple_of(i * t, t), t)


def _retention_tables():
    h = np.arange(8, dtype=np.float32)
    log_g = np.log1p(-np.exp2(-5.0 - h)).astype(np.float32)
    idx = np.arange(RET_C, dtype=np.float32)
    diff = idx[:, None] - idx[None, :]
    dtab = np.where(diff >= 0, np.exp(log_g[:, None, None] * np.maximum(diff, 0.0)), 0.0).astype(np.float32)
    dtab = dtab.reshape(8 * RET_C, RET_C)
    lane_head = np.arange(256) // 32
    a_tab = np.exp(log_g[lane_head][None, :] * (idx + 1.0)[:, None]).astype(np.float32)
    b_tab = np.exp(log_g[lane_head][None, :] * (RET_C - 1.0 - idx)[:, None]).astype(np.float32)
    lam = np.exp(log_g[lane_head] * RET_C).astype(np.float32)[:, None]
    bd = (lane_head[:, None] == (np.arange(512) // 64)[None, :]).astype(np.float32)
    return dtab, a_tab, b_tab, lam, bd


def _rotary_tables():
    inv_r = (1.0 / (np.float32(10000.0) ** np.linspace(0.0, 1.0, 16, dtype=np.float32))).astype(np.float32)
    inv_a = (np.float32(500000.0) ** (-np.arange(0, 16, 2, dtype=np.float32) / np.float32(16))).astype(np.float32)
    ifc = np.zeros((1, 128), np.float32)
    ifc[0, 0:16], ifc[0, 16:24] = inv_r, inv_a
    spread = np.zeros((128, 768), np.float32)
    for lane in range(256):
        spread[(lane % 32) % 16, lane] = 1.0
    for lane in range(512):
        d = lane % 64
        spread[16 + d % 8 if d < 16 else 24, 256 + lane] = 1.0
    return ifc, spread


def _rot_halves(tm):
    lo_r = (lax.broadcasted_iota(jnp.int32, (tm, 256), 1) % 32) < 16
    lo_a = (lax.broadcasted_iota(jnp.int32, (tm, 512), 1) % 64) < 8
    return lo_r, lo_a


def _spread_exact(t, e):
    hi = t.astype(BF16)
    r1 = t - hi.astype(F32)
    mid = r1.astype(BF16)
    lo = (r1 - mid.astype(F32)).astype(BF16)
    return _nn(hi, e) + _nn(mid, e) + _nn(lo, e)


def _rot_tables(cos_ref, sin_ref, e_ref):
    cs = _spread_exact(cos_ref[...], e_ref[...])
    sn = _spread_exact(sin_ref[...], e_ref[...])
    return cs[:, 0:256], cs[:, 256:768], sn[:, 0:256], sn[:, 256:768]


def _proj_fwd(h1, win_g, cos, sin, spread, after):
    tm = 256

    def body(h_ref, w_ref, cos_ref, sin_ref, e_ref, qr_ref, kr_ref, rv_ref, rg_ref, aq_ref, ak_ref, av_ref, p_ref, _):
        h = h_ref[...]
        for k in range(N_CHIP):
            p_ref[:, k * WIN_C:(k + 1) * WIN_C] = _nn(h, w_ref[k])
        cr, ca, sr, sa = _rot_tables(cos_ref, sin_ref, e_ref)
        lo_r, lo_a = _rot_halves(tm)

        def rot_r(v):
            return v * cr + sr * jnp.where(lo_r, -pltpu.roll(v, 240, 1), pltpu.roll(v, 16, 1))

        def rot_a(v):
            return v * ca + sa * jnp.where(lo_a, -pltpu.roll(v, 504, 1), pltpu.roll(v, 8, 1))

        qr_ref[...] = rot_r(p_ref[:, 0:256]).astype(BF16)
        kr_ref[...] = (rot_r(p_ref[:, 256:512]) * RET_SCALE).astype(BF16)
        rv_ref[...] = p_ref[:, 512:1024].astype(BF16)
        rg_ref[...] = p_ref[:, 1024:1536]
        aq, ak = rot_a(p_ref[:, 1536:2048]), rot_a(p_ref[:, 2048:2560])
        for j in range(4):
            aq_ref[j] = aq[:, 128 * j:128 * j + 128]
            ak_ref[j] = ak[:, 128 * j:128 * j + 128]
            av_ref[j] = p_ref[:, 2560 + 128 * j:2560 + 128 * j + 128]

    row = lambda w: pl.BlockSpec((tm, w), lambda i: (i, 0))
    slab = pl.BlockSpec((4, tm, 128), lambda i: (0, i, 0))
    return _carry(
        "proj_fwd", body, _NoExchange(), (), (h1, win_g, cos, sin, spread),
        [row(D), pl.BlockSpec((N_CHIP, D, WIN_C), lambda i: (0, 0, 0)), row(128), row(128),
         pl.BlockSpec((128, 768), lambda i: (0, 0))],
        [row(256), row(256), row(512), row(512), slab, slab, slab],
        [jax.ShapeDtypeStruct((S, w), BF16) for w in (256, 256, 512)]
        + [jax.ShapeDtypeStruct((S, 512), F32)] + [jax.ShapeDtypeStruct((4, S, 128), F32)] * 3,
        scratch_shapes=[pltpu.VMEM((tm, PW), F32)], grid=(S // tm,), semantics=("parallel",), after=after)[0]


def _seg_mean(v):
    lo = lax.broadcasted_iota(jnp.int32, v.shape, 1) < 64
    s_lo = jnp.sum(jnp.where(lo, v, 0.0), axis=-1, keepdims=True)
    s_hi = jnp.sum(jnp.where(lo, 0.0, v), axis=-1, keepdims=True)
    return jnp.where(lo, s_lo, s_hi) * (1.0 / 64.0)


def _ret_fwd(qr, kr, rv, proj, tabs, exchange, exchange_args, after=None):
    C, G = RET_C, RET_PER_STEP
    steps = S // (C * G)
    dtab, a_tab, b_tab, lam, bd = tabs

    def body(q_ref, k_ref, v_ref, g_ref, dt_ref, a_ref, b_ref, lam_ref, bd_ref, o_ref, cat_ref, st_ref, R, exch):
        @pl.when(pl.program_id(0) == 0)
        def _():
            exch.start()
            R[...] = jnp.zeros_like(R)

        lane_head = lax.broadcasted_iota(jnp.int32, (C, 256), 1) // 32
        col_head = lax.broadcasted_iota(jnp.int32, (C, 256), 1) // 64
        for s in range(G):
            rows = slice(s * C, (s + 1) * C)
            q, k, v = q_ref[rows, :], k_ref[rows, :], v_ref[rows, :]
            rb = R[...].astype(BF16)
            st_ref[s] = rb
            qa = (q.astype(F32) * a_ref[...]).astype(BF16)
            cross = _nn(qa, rb)
            p = (_nt(_stack_heads(q, lane_head, n=8), k) * dt_ref[...]).astype(BF16)
            og = [cross[:, 256 * g:256 * g + 256]
                  + _unstack_heads(_nn(p[4 * C * g:4 * C * (g + 1)], v[:, 256 * g:256 * g + 256]), col_head)
                  for g in range(2)]
            kb = (k.astype(F32) * b_ref[...]).astype(BF16)
            R[...] = R[...] * lam_ref[...] + _tn(kb, v) * bd_ref[...]
            o_ref[rows, 0:256] = og[0]
            o_ref[rows, 256:512] = og[1]
            for j in range(4):
                oj = og[j // 2][:, 128 * (j % 2):128 * (j % 2) + 128]
                xc = oj - _seg_mean(oj)
                rn = xc * lax.rsqrt(_seg_mean(xc * xc) + GN_EPS)
                gj = g_ref[rows, 128 * j:128 * j + 128]
                cat_ref[rows, 128 * j:128 * j + 128] = (rn * (gj * _sigmoid(gj))).astype(BF16)

        @pl.when(pl.program_id(0) == steps - 1)
        def _():
            exch.middle()
            exch.finish()

    row = lambda w: pl.BlockSpec((C * G, w), lambda n: (n, 0))
    full = lambda a: pl.BlockSpec(a.shape, lambda n: (0,) * a.ndim)
    return _carry(
        "ret_fwd", body, exchange, exchange_args, (qr, kr, rv, proj, dtab, a_tab, b_tab, lam, bd),
        [row(256), row(256), row(512), row(512),
         full(dtab), full(a_tab), full(b_tab), full(lam), full(bd)],
        [row(512), row(512), pl.BlockSpec((G, 256, 512), lambda n: (n, 0, 0))],
        [jax.ShapeDtypeStruct((S, 512), F32), jax.ShapeDtypeStruct((S, 512), BF16),
         jax.ShapeDtypeStruct((S // C, 256, 512), BF16)],
        scratch_shapes=[pltpu.VMEM((256, 512), F32)], grid=(steps,), semantics=("arbitrary",), after=after)


def _stack_heads(v, lane_head, fill=0.0, n=4):
    return jnp.concatenate([jnp.where(lane_head == h, v, jnp.full_like(v, fill)) for h in range(n)], axis=0)


def _unstack_heads(v, lane_head, n=4):
    out = v[0:ATT_BLK]
    for h in range(1, n):
        out = jnp.where(lane_head == h, v[h * ATT_BLK:(h + 1) * ATT_BLK], out)
    return out


def _att_bias(has_prev):
    nk = 2 * ATT_BLK if has_prev else ATT_BLK
    a = lax.broadcasted_iota(jnp.int32, (4 * ATT_BLK, nk), 0) % ATT_BLK
    kk = lax.broadcasted_iota(jnp.int32, (4 * ATT_BLK, nk), 1)
    if not has_prev:
        return None, jnp.where((a - kk) >= 0, 0.0, NEG)
    dist = ATT_BLK + a - kk
    inside = (dist >= 0) & (dist <= ATT_BLK)
    return jnp.where(inside, 0.0, NEG), jnp.where(inside & (kk >= ATT_BLK), 0.0, NEG)


def _class_rows(ib, r, d):
    if d == 1:
        return pl.ds(pl.multiple_of(ib * ATT_BLK, ATT_BLK), ATT_BLK)
    return pl.ds(ib * ATT_BLK * d + r, ATT_BLK, stride=d)


def _slab_pair(ref, g, rows):
    return jnp.concatenate([ref[2 * g, rows, :], ref[2 * g + 1, rows, :]], axis=1)


def _att_blocks(d):
    nb = S // d // ATT_BLK
    return nb, nb > 1


def _att_fwd(aq, ak, av, exchange, exchange_args):
    def body(q_ref, k_ref, v_ref, o_ref, l_ref, cat_ref, xc):
        xc.start()
        lane_head = lax.broadcasted_iota(jnp.int32, (ATT_BLK, 256), 1) // 64
        for pi, d in enumerate(PATTERN_DILATIONS):
            if pi == len(PATTERN_DILATIONS) - 1:
                xc.middle()
            nb, has_prev = _att_blocks(d)
            bias_rest, bias_first = _att_bias(has_prev)

            def block(b, carry, pi=pi, d=d, nb=nb, has_prev=has_prev, bias_rest=bias_rest, bias_first=bias_first):
                r, ib = b // nb, b % nb
                rows = _class_rows(ib, r, d)
                prow = _class_rows(jnp.maximum(ib - 1, 0), r, d)
                bias = jnp.where(ib == 0, bias_first, bias_rest) if has_prev else bias_first
                for g in range(2):
                    qg = _slab_pair(q_ref, g, rows).astype(BF16)
                    kg = _slab_pair(k_ref, g, rows)
                    vg = _slab_pair(v_ref, g, rows)
                    if has_prev:
                        kg = jnp.concatenate([_slab_pair(k_ref, g, prow), kg], axis=0)
                        vg = jnp.concatenate([_slab_pair(v_ref, g, prow), vg], axis=0)
                    kg, vg = kg.astype(BF16), vg.astype(BF16)
                    s = _nt(_stack_heads(qg, lane_head), kg) * ATT_SCALE + bias
                    m = jnp.max(s, axis=-1, keepdims=True)
                    p = jnp.exp(s - m)
                    den = jnp.sum(p, axis=-1, keepdims=True)
                    og = _unstack_heads(_nn(p.astype(BF16), vg) / den, lane_head)
                    lg = _unstack_heads(jnp.broadcast_to(m + jnp.log(den), (4 * ATT_BLK, 256)), lane_head)
                    for jj in range(2):
                        j = 2 * g + jj
                        o_new, l_new = og[:, 128 * jj:128 * jj + 128], lg[:, 128 * jj:128 * jj + 128]
                        if pi > 0:
                            o_old, l_old = o_ref[j, rows, :], l_ref[j, rows, :]
                            mx = jnp.maximum(l_old, l_new)
                            ea, eb = jnp.exp(l_old - mx), jnp.exp(l_new - mx)
                            den = ea + eb
                            o_new = (ea * o_old + eb * o_new) / den
                            l_new = mx + jnp.log(den)
                        o_ref[j, rows, :] = o_new
                        l_ref[j, rows, :] = l_new
                return carry

            lax.fori_loop(0, S // ATT_BLK, block, 0, unroll=4)

        def to_cat(i, carry):
            rows = _rows(i, 256)
            for j in range(4):
                cat_ref[rows, 128 * j:128 * j + 128] = o_ref[j, rows, :].astype(BF16)
            return carry

        lax.fori_loop(0, S // 256, to_cat, 0)
        xc.finish()

    slab = jax.ShapeDtypeStruct((4, S, 128), F32)
    return _carry("att_fwd", body, exchange, exchange_args, (aq, ak, av), [VMEM] * 3, [VMEM] * 3,
                  [slab, slab, jax.ShapeDtypeStruct((S, 512), BF16)])


def _mix_fwd(cat_r, cat_a, wout, x, g2, g3, exchange, exchange_args):
    tm = 512

    def body(cr_ref, ca_ref, w_ref, x_ref, g2_ref, g3_ref, mix_ref, x2_ref, h3_ref, xc):
        @pl.when(pl.program_id(0) == 0)
        def _():
            xc.start()

        mix = _nn(cr_ref[...], w_ref[0:512, :]) + _nn(ca_ref[...], w_ref[512:1024, :])
        mix_ref[...] = mix
        x2 = x_ref[...] + mix * _rstd(mix) * g2_ref[...]
        x2_ref[...] = x2
        h3_ref[...] = (x2 * _rstd(x2) * g3_ref[...]).astype(BF16)

        @pl.when(pl.program_id(0) == S // tm - 1)
        def _():
            xc.middle()
            xc.finish()

    row = lambda w: pl.BlockSpec((tm, w), lambda i: (i, 0))
    vec = pl.BlockSpec((1, D), lambda i: (0, 0))
    return _carry("mix_fwd", body, exchange, exchange_args, (cat_r, cat_a, wout, x, g2, g3),
                  [row(512), row(512), pl.BlockSpec((D, D), lambda i: (0, 0)), row(D), vec, vec],
                  [row(D), row(D), row(D)],
                  [jax.ShapeDtypeStruct((S, D), F32), jax.ShapeDtypeStruct((S, D), F32),
                   jax.ShapeDtypeStruct((S, D), BF16)],
                  grid=(S // tm,), semantics=("arbitrary",))


def _ffn_fwd(h3, wg, wu, wd, x2, tgt, g4):
    tm = 512
    last = N_CHIP - 1

    def body(h_ref, wg_ref, wu_ref, wd_ref, x2_ref, t_ref, g_ref,
             gt_ref, up_ref, a_ref, loss_ref, dy_ref, df_ref, dg_ref, f_ref):
        k, i = pl.program_id(0), pl.program_id(1)
        h = h_ref[...]
        gt = _nt(h, wg_ref[...])
        up = _nt(h, wu_ref[...])
        gt_ref[...] = gt.astype(BF16)
        up_ref[...] = up.astype(BF16)
        a = (gt * _sigmoid(gt) * up).astype(BF16)
        a_ref[...] = a
        part = _nn(a, wd_ref[...])
        rows = _rows(i, tm)

        @pl.when(k == 0)
        def _():
            f_ref[rows, :] = part

        @pl.when((k > 0) & (k < last))
        def _():
            f_ref[rows, :] = f_ref[rows, :] + part

        @pl.when((k == last) & (i == 0))
        def _():
            loss_ref[...] = jnp.zeros_like(loss_ref)
            dg_ref[...] = jnp.zeros_like(dg_ref)

        @pl.when(k == last)
        def _():
            fv = f_ref[rows, :] + part
            r = _rstd(fv)
            fn = fv * r
            e = x2_ref[...] + fn * g_ref[...] - t_ref[...]
            loss_ref[...] = loss_ref[...] + jnp.sum(jnp.sum(e * e, axis=-1, keepdims=True), axis=0, keepdims=True)
            dy = e * (1.0 / D)
            dy_ref[...] = dy
            dg_ref[...] = dg_ref[...] + jnp.sum(dy * fn, axis=0, keepdims=True)
            t = dy * g_ref[...]
            df_ref[...] = (r * (t - fn * jnp.mean(t * fn, axis=-1, keepdims=True))).astype(BF16)

    wrow = pl.BlockSpec((None, FF_C, D), lambda k, i: (k, 0, 0))
    act = pl.BlockSpec((None, tm, FF_C), lambda k, i: (k, i, 0))
    late = pl.BlockSpec((tm, D), lambda k, i: (jnp.where(k == last, i, 0), 0))
    vec = pl.BlockSpec((1, D), lambda k, i: (0, 0))
    return pl.pallas_call(
        body, grid=(N_CHIP, S // tm), name="ffn_fwd",
        in_specs=[pl.BlockSpec((tm, D), lambda k, i: (i, 0)), wrow, wrow, wrow, late, late, vec],
        out_specs=[act, act, act, vec, late, late, vec],
        out_shape=[jax.ShapeDtypeStruct((N_CHIP, S, FF_C), BF16)] * 3
                  + [jax.ShapeDtypeStruct((1, D), F32), jax.ShapeDtypeStruct((S, D), F32),
                     jax.ShapeDtypeStruct((S, D), BF16), jax.ShapeDtypeStruct((1, D), F32)],
        scratch_shapes=[pltpu.VMEM((S, D), F32)],
        compiler_params=_params("arbitrary", "arbitrary"),
    )(h3, wg, wu, wd, x2, tgt, g4)


def _ffn_bwd_act(df, gt, up, wg, wu, wd, dy, x2, mix, g2, g3):
    tm, sub = 512, 256
    last = N_CHIP - 1

    def body(df_ref, gt_ref, up_ref, wg_ref, wu_ref, wd_ref, dy_ref, x2_ref, mix_ref, g2_ref, g3_ref,
             dgt_ref, dup_ref, dx2_ref, dmix_ref, dg3_ref, dg2_ref, dh_ref):
        k, i = pl.program_id(0), pl.program_id(1)
        parts = []
        for s in range(tm // sub):
            rows = slice(s * sub, (s + 1) * sub)
            da = _nt(df_ref[rows, :], wd_ref[...])
            gt, up = gt_ref[rows, :].astype(F32), up_ref[rows, :].astype(F32)
            sg = _sigmoid(gt)
            dup = (da * gt * sg).astype(BF16)
            dgt = (da * up * (sg * (1.0 + gt * (1.0 - sg)))).astype(BF16)
            dup_ref[rows, :] = dup
            dgt_ref[rows, :] = dgt
            parts.append(_nn(dgt, wg_ref[...]) + _nn(dup, wu_ref[...]))
        part = jnp.concatenate(parts, axis=0)
        rows = _rows(i, tm)

        @pl.when(k == 0)
        def _():
            dh_ref[rows, :] = part

        @pl.when((k > 0) & (k < last))
        def _():
            dh_ref[rows, :] = dh_ref[rows, :] + part

        @pl.when((k == last) & (i == 0))
        def _():
            dg3_ref[...] = jnp.zeros_like(dg3_ref)
            dg2_ref[...] = jnp.zeros_like(dg2_ref)

        @pl.when(k == last)
        def _():
            dh = dh_ref[rows, :] + part
            x2 = x2_ref[...]
            r3 = _rstd(x2)
            xn = x2 * r3
            dg3_ref[...] = dg3_ref[...] + jnp.sum(dh * xn, axis=0, keepdims=True)
            t = dh * g3_ref[...]
            dx2 = dy_ref[...] + r3 * (t - xn * jnp.mean(t * xn, axis=-1, keepdims=True))
            dx2_ref[...] = dx2
            mix = mix_ref[...]
            r2 = _rstd(mix)
            mn = mix * r2
            dg2_ref[...] = dg2_ref[...] + jnp.sum(dx2 * mn, axis=0, keepdims=True)
            u = dx2 * g2_ref[...]
            dmix_ref[...] = (r2 * (u - mn * jnp.mean(u * mn, axis=-1, keepdims=True))).astype(BF16)

    wrow = pl.BlockSpec((None, FF_C, D), lambda k, i: (k, 0, 0))
    act = pl.BlockSpec((None, tm, FF_C), lambda k, i: (k, i, 0))
    row = pl.BlockSpec((tm, D), lambda k, i: (i, 0))
    late = pl.BlockSpec((tm, D), lambda k, i: (jnp.where(k == last, i, 0), 0))
    vec = pl.BlockSpec((1, D), lambda k, i: (0, 0))
    return pl.pallas_call(
        body, grid=(N_CHIP, S // tm), name="ffn_bwd_act",
        in_specs=[row, act, act, wrow, wrow, wrow, late, late, late, vec, vec],
        out_specs=[act, act, late, late, vec, vec],
        out_shape=[jax.ShapeDtypeStruct((N_CHIP, S, FF_C), BF16), jax.ShapeDtypeStruct((N_CHIP, S, FF_C), BF16),
                   jax.ShapeDtypeStruct((S, D), F32), jax.ShapeDtypeStruct((S, D), BF16),
                   jax.ShapeDtypeStruct((1, D), F32), jax.ShapeDtypeStruct((1, D), F32)],
        scratch_shapes=[pltpu.VMEM((S, D), F32)],
        compiler_params=_params("arbitrary", "arbitrary"),
    )(df, gt, up, wg, wu, wd, dy, x2, mix, g2, g3)


def _ffn_bwd_w(a, df, h3, dgt, dup):
    tm = 1024
    assert S // tm == 2

    def body(a_ref, df_ref, h_ref, dgt_ref, dup_ref, dwd_ref, dwg_ref, dwu_ref, acc_d, acc_g, acc_u):
        i = pl.program_id(1)
        h = h_ref[...]
        parts = (_tn(a_ref[...], df_ref[...]), _tn(dgt_ref[...], h), _tn(dup_ref[...], h))

        @pl.when(i == 0)
        def _():
            for acc, part in zip((acc_d, acc_g, acc_u), parts):
                acc[...] = part

        @pl.when(i == S // tm - 1)
        def _():
            for out, acc, part in zip((dwd_ref, dwg_ref, dwu_ref), (acc_d, acc_g, acc_u), parts):
                out[...] = (acc[...] + part).astype(BF16)

    act = pl.BlockSpec((None, tm, FF_C), lambda k, i: (k, i, 0))
    row = pl.BlockSpec((tm, D), lambda k, i: (i, 0))
    wrow = pl.BlockSpec((None, FF_C, D), lambda k, i: (k, 0, 0))
    return pl.pallas_call(
        body, grid=(N_CHIP, S // tm), name="ffn_bwd_w",
        in_specs=[act, row, row, act, act],
        out_specs=[wrow, wrow, wrow],
        out_shape=[jax.ShapeDtypeStruct((N_CHIP, FF_C, D), BF16)] * 3,
        scratch_shapes=[pltpu.VMEM((FF_C, D), F32)] * 3,
        compiler_params=_params("parallel", "arbitrary"),
    )(a, df, h3, dgt, dup)


def _mix_bwd(dmix, cat_r, cat_a, wout, exchange, exchange_args):
    tm = 1024

    def body(dm_ref, cr_ref, ca_ref, w_ref, dret_ref, datt_ref, dw_ref, acc, xc):
        i = pl.program_id(0)

        @pl.when(i == 0)
        def _():
            xc.start()
            acc[...] = jnp.zeros_like(acc)

        dm = dm_ref[...]
        dret_ref[...] = _nt(dm, w_ref[0:512, :])
        datt = _nt(dm, w_ref[512:1024, :])
        for j in range(4):
            datt_ref[j] = datt[:, 128 * j:128 * j + 128]
        acc[0:512, :] += _tn(cr_ref[...], dm)
        acc[512:1024, :] += _tn(ca_ref[...], dm)

        @pl.when(i == S // tm - 1)
        def _():
            dw_ref[...] = acc[...].astype(BF16)
            xc.middle()
            xc.finish()

    row = lambda w: pl.BlockSpec((tm, w), lambda i: (i, 0))
    full = pl.BlockSpec((D, D), lambda i: (0, 0))
    return _carry("mix_bwd", body, exchange, exchange_args, (dmix, cat_r, cat_a, wout),
                  [row(D), row(512), row(512), full],
                  [row(512), pl.BlockSpec((4, tm, 128), lambda i: (0, i, 0)), full],
                  [jax.ShapeDtypeStruct((S, 512), F32), jax.ShapeDtypeStruct((4, S, 128), F32),
                   jax.ShapeDtypeStruct((D, D), BF16)],
                  scratch_shapes=[pltpu.VMEM((D, D), F32)], grid=(S // tm,), semantics=("arbitrary",))


def _att_bwd(aq, ak, av, datt, att_out, lse, exchange, exchange_args, after=None):
    def body(q_ref, k_ref, v_ref, do_ref, out_ref, l_ref, dq_ref, dk_ref, dv_ref, xc):
        xc.start()

        lane_head = lax.broadcasted_iota(jnp.int32, (ATT_BLK, 256), 1) // 64
        for pi, d in enumerate(PATTERN_DILATIONS):
            nb, has_prev = _att_blocks(d)
            assert pi > 0 or not has_prev
            bias_rest, bias_first = _att_bias(has_prev)

            def block(b, carry, pi=pi, d=d, nb=nb, has_prev=has_prev, bias_rest=bias_rest, bias_first=bias_first):
                r, ib = b // nb, b % nb
                rows = _class_rows(ib, r, d)
                prow = _class_rows(jnp.maximum(ib - 1, 0), r, d)
                bias = jnp.where(ib == 0, bias_first, bias_rest) if has_prev else bias_first
                for g in range(2):
                    qg = _slab_pair(q_ref, g, rows).astype(BF16)
                    kg = _slab_pair(k_ref, g, rows)
                    vg = _slab_pair(v_ref, g, rows)
                    if has_prev:
                        kg = jnp.concatenate([_slab_pair(k_ref, g, prow), kg], axis=0)
                        vg = jnp.concatenate([_slab_pair(v_ref, g, prow), vg], axis=0)
                    kg, vg = kg.astype(BF16), vg.astype(BF16)
                    dog = _slab_pair(do_ref, g, rows)
                    outg = _slab_pair(out_ref, g, rows)
                    lg = _slab_pair(l_ref, g, rows)
                    qs = _stack_heads(qg, lane_head)
                    dos = _stack_heads(dog, lane_head)
                    delta = jnp.sum(dos * jnp.concatenate([outg] * 4, axis=0), axis=-1, keepdims=True)
                    lh = jnp.max(_stack_heads(lg, lane_head, NEG), axis=-1, keepdims=True)
                    s = _nt(qs, kg) * ATT_SCALE + bias
                    p = jnp.exp(s - lh)
                    dosb = dos.astype(BF16)
                    ds = (p * (_nt(dosb, vg) - delta) * ATT_SCALE).astype(BF16)
                    dq = _unstack_heads(_nn(ds, kg), lane_head)
                    dk = _tn(ds, qs)
                    dv = _tn(p.astype(BF16), dosb)
                    for jj in range(2):
                        j, sl = 2 * g + jj, slice(128 * jj, 128 * jj + 128)
                        if pi == 0:
                            dq_ref[j, rows, :] = dq[:, sl]
                            dk_ref[j, rows, :] = dk[:, sl]
                            dv_ref[j, rows, :] = dv[:, sl]
                            continue
                        dq_ref[j, rows, :] += dq[:, sl]
                        if has_prev:
                            dk_ref[j, prow, :] += dk[0:ATT_BLK, sl]
                            dv_ref[j, prow, :] += dv[0:ATT_BLK, sl]
                            dk_ref[j, rows, :] += dk[ATT_BLK:2 * ATT_BLK, sl]
                            dv_ref[j, rows, :] += dv[ATT_BLK:2 * ATT_BLK, sl]
                        else:
                            dk_ref[j, rows, :] += dk[:, sl]
                            dv_ref[j, rows, :] += dv[:, sl]
                return carry

            lax.fori_loop(0, S // ATT_BLK, block, 0, unroll=4)
        xc.middle()
        xc.finish()

    slab = jax.ShapeDtypeStruct((4, S, 128), F32)
    return _carry("att_bwd", body, exchange, exchange_args, (aq, ak, av, datt, att_out, lse), [VMEM] * 6, [VMEM] * 3,
                  [slab, slab, slab], after=after)


def _ret_bwd(qr, kr, rv, proj, o_raw, states, dret, tabs, exchange, exchange_args, after=None):
    C, G = RET_C, RET_PER_STEP
    steps = S // (C * G)
    dtab, a_tab, b_tab, lam, bd = tabs

    def body(q_ref, k_ref, v_ref, g_ref, o_ref, st_ref, dr_ref, dt_ref, a_ref, b_ref, lam_ref, bd_ref,
             dq_ref, dk_ref, dv_ref, dg_ref, dR, exch):
        @pl.when(pl.program_id(0) == 0)
        def _():
            exch.start()
            dR[...] = jnp.zeros_like(dR)

        lane_head = lax.broadcasted_iota(jnp.int32, (C, 256), 1) // 32
        col_head = lax.broadcasted_iota(jnp.int32, (C, 256), 1) // 64
        for s in reversed(range(G)):
            rows = slice(s * C, (s + 1) * C)
            q, k, v = q_ref[rows, :], k_ref[rows, :], v_ref[rows, :]
            dos = []
            for j in range(4):
                sl = slice(128 * j, 128 * j + 128)
                oj = o_ref[rows, sl]
                xc = oj - _seg_mean(oj)
                rs = lax.rsqrt(_seg_mean(xc * xc) + GN_EPS)
                rn = xc * rs
                gj = g_ref[rows, sl]
                sg = _sigmoid(gj)
                dret = dr_ref[rows, sl]
                dg_ref[rows, sl] = dret * rn * (sg * (1.0 + gj * (1.0 - sg)))
                drn = dret * (gj * sg)
                dos.append(rs * (drn - _seg_mean(drn) - rn * _seg_mean(drn * rn)))
            do = [jnp.concatenate(dos[0:2], axis=1), jnp.concatenate(dos[2:4], axis=1)]
            do8 = jnp.concatenate(do, axis=1).astype(BF16)
            drb = dR[...].astype(BF16)
            rb = st_ref[s]
            dq = _nt(do8, rb) * a_ref[...]
            dk = _nt(v, drb) * b_ref[...]
            kb = (k.astype(F32) * b_ref[...]).astype(BF16)
            dvall = _nn(kb, drb)
            qs = _stack_heads(q, lane_head, n=8)
            dec = dt_ref[...]
            p = (_nt(qs, k) * dec).astype(BF16)
            dos = [_stack_heads(do[g], col_head).astype(BF16) for g in range(2)]
            dp = jnp.concatenate([_nt(dos[g], v[:, 256 * g:256 * g + 256]) for g in range(2)], axis=0)
            ds = (dp * dec).astype(BF16)
            dq = dq + _unstack_heads(_nn(ds, k), lane_head, n=8)
            dk = dk + _tn(ds, qs)
            dv = [dvall[:, 256 * g:256 * g + 256] + _tn(p[4 * C * g:4 * C * (g + 1)], dos[g]) for g in range(2)]
            qa = (q.astype(F32) * a_ref[...]).astype(BF16)
            dR[...] = dR[...] * lam_ref[...] + _tn(qa, do8) * bd_ref[...]
            dq_ref[rows, :] = dq
            dk_ref[rows, :] = dk
            dv_ref[rows, 0:256] = dv[0]
            dv_ref[rows, 256:512] = dv[1]

        @pl.when(pl.program_id(0) == steps - 1)
        def _():
            exch.middle()
            exch.finish()

    rev = lambda w: pl.BlockSpec((C * G, w), lambda n: (steps - 1 - n, 0))
    full = lambda a: pl.BlockSpec(a.shape, lambda n: (0,) * a.ndim)
    return _carry(
        "ret_bwd", body, exchange, exchange_args, (qr, kr, rv, proj, o_raw, states, dret, dtab, a_tab, b_tab, lam, bd),
        [rev(256), rev(256), rev(512), rev(512), rev(512),
         pl.BlockSpec((G, 256, 512), lambda n: (steps - 1 - n, 0, 0)), rev(512),
         full(dtab), full(a_tab), full(b_tab), full(lam), full(bd)],
        [rev(256), rev(256), rev(512), rev(512)],
        [jax.ShapeDtypeStruct((S, 256), F32), jax.ShapeDtypeStruct((S, 256), F32),
         jax.ShapeDtypeStruct((S, 512), F32), jax.ShapeDtypeStruct((S, 512), F32)],
        scratch_shapes=[pltpu.VMEM((256, 512), F32)], grid=(steps,), semantics=("arbitrary",), after=after)


def _rot_bwd(cos, sin, spread, dqr, dkr, drv, drg, dq_att, dk_att, dv_att):
    tm = 256

    def body(cos_ref, sin_ref, e_ref, dqr_ref, dkr_ref, drv_ref, drg_ref, dqa_ref, dka_ref, dva_ref, dp_ref):
        cr, ca, sr, sa = _rot_tables(cos_ref, sin_ref, e_ref)
        lo_r, lo_a = _rot_halves(tm)

        def unrot_r(g):
            gs = g * sr
            return g * cr + pltpu.roll(jnp.where(lo_r, -gs, 0.0), 16, 1) + pltpu.roll(jnp.where(lo_r, 0.0, gs), 240, 1)

        def unrot_a(g):
            gs = g * sa
            return g * ca + pltpu.roll(jnp.where(lo_a, -gs, 0.0), 8, 1) + pltpu.roll(jnp.where(lo_a, 0.0, gs), 504, 1)

        def wide(ref):
            return jnp.concatenate([ref[j] for j in range(4)], axis=1)

        dp_ref[:, 0:256] = unrot_r(dqr_ref[...]).astype(BF16)
        dp_ref[:, 256:512] = unrot_r(dkr_ref[...] * RET_SCALE).astype(BF16)
        dp_ref[:, 512:1024] = drv_ref[...].astype(BF16)
        dp_ref[:, 1024:1536] = drg_ref[...].astype(BF16)
        dp_ref[:, 1536:2048] = unrot_a(wide(dqa_ref)).astype(BF16)
        dp_ref[:, 2048:2560] = unrot_a(wide(dka_ref)).astype(BF16)
        dp_ref[:, 2560:3072] = wide(dva_ref).astype(BF16)

    row = lambda w: pl.BlockSpec((tm, w), lambda i: (i, 0))
    slab = pl.BlockSpec((4, tm, 128), lambda i: (0, i, 0))
    return pl.pallas_call(
        body, grid=(S // tm,), name="rot_bwd",
        in_specs=[row(128), row(128), pl.BlockSpec((128, 768), lambda i: (0, 0)),
                  row(256), row(256), row(512), row(512), slab, slab, slab],
        out_specs=row(PW), out_shape=jax.ShapeDtypeStruct((S, PW), BF16),
        compiler_params=_params("parallel"),
    )(cos, sin, spread, dqr, dkr, drv, drg, dq_att, dk_att, dv_att)


def _win_bwd_w(h1, dproj, exchange, exchange_args):
    def body(h_ref, dp_ref, dw_ref, xc):
        k = pl.program_id(0)

        @pl.when(k == 0)
        def _():
            xc.start()

        dw_ref[...] = _tn(h_ref[...], dp_ref[...]).astype(BF16)

        @pl.when(k == N_CHIP - 1)
        def _():
            xc.middle()
            xc.finish()

    (dw,), out = _carry(
        "win_bwd_w", body, exchange, exchange_args, (h1, dproj),
        [pl.BlockSpec((S, D), lambda k: (0, 0)), pl.BlockSpec((S, WIN_C), lambda k: (0, k))],
        [pl.BlockSpec((None, D, WIN_C), lambda k: (k, 0, 0))],
        [jax.ShapeDtypeStruct((N_CHIP, D, WIN_C), BF16)], grid=(N_CHIP,), semantics=("arbitrary",))
    return dw, out


def _in_bwd(dproj, win_g, x, dx2, g1, other_rows, after):
    tm = 512
    n = len(other_rows)

    def body(dp_ref, w_ref, x_ref, dx2_ref, g_ref, *refs):
        rows, dx_ref, blk_ref = refs[:n], refs[n], refs[n + 1]

        @pl.when(pl.program_id(0) == 0)
        def _():
            blk_ref[...] = jnp.zeros_like(blk_ref)
            for i, r_ref in enumerate(rows):
                blk_ref[i + 1:i + 2, :] = r_ref[...]

        dh = _nt(dp_ref[:, 0:WIN_C], w_ref[0])
        for k in range(1, N_CHIP):
            dh = dh + _nt(dp_ref[:, k * WIN_C:(k + 1) * WIN_C], w_ref[k])
        xv = x_ref[...]
        r = _rstd(xv)
        xn = xv * r
        blk_ref[0:1, :] = blk_ref[0:1, :] + jnp.sum(dh * xn, axis=0, keepdims=True)
        t = dh * g_ref[...]
        dx_ref[...] = dx2_ref[...] + r * (t - xn * jnp.mean(t * xn, axis=-1, keepdims=True))

    row = lambda w: pl.BlockSpec((tm, w), lambda i: (i, 0))
    vec = pl.BlockSpec((1, D), lambda i: (0, 0))
    return _carry("in_bwd", body, _NoExchange(), (), (dproj, win_g, x, dx2, g1, *other_rows),
                  [row(PW), pl.BlockSpec((N_CHIP, D, WIN_C), lambda i: (0, 0, 0)), row(D), row(D), vec] + [vec] * n,
                  [row(D), pl.BlockSpec((8, D), lambda i: (0, 0))],
                  [jax.ShapeDtypeStruct((S, D), F32), jax.ShapeDtypeStruct((8, D), F32)],
                  grid=(S // tm,), semantics=("arbitrary",), after=after)[0]


ANY = pl.BlockSpec(memory_space=pl.ANY)
VMEM = pl.BlockSpec(memory_space=pltpu.VMEM)
FLIPS = ((1, 0), (0, 1), (1, 1))


def _place():
    x, y, c = lax.axis_index("x"), lax.axis_index("y"), lax.axis_index("c")
    chips = [((1 - x) if fx else x, (1 - y) if fy else y) for fx, fy in FLIPS]
    return x, y, c, 2 * x + y, chips


def _remote(src, dst, send_sem, recv_sem, device):
    return pltpu.make_async_remote_copy(src_ref=src, dst_ref=dst, send_sem=send_sem, recv_sem=recv_sem,
                                        device_id=device, device_id_type=MESH)


class _Exchange:
    aliases = {}

    def middle(self, ins, outs, sems):
        pass


class _GatherShards(_Exchange):
    def __init__(self, shards):
        n = self.n = len(shards)
        self.n_in = self.n_out = n
        self.out_shape = [jax.ShapeDtypeStruct((N_CHIP,) + s.shape, s.dtype) for s in shards]
        dma = pltpu.SemaphoreType.DMA
        self.scratch = [dma((3 * n,)), dma((3 * n,)), dma((3 * n,)), dma((3 * n,)), dma((n,)), dma((n,))]

    def _ici(self, ins, outs, sems, a, j, chip):
        x, y, c, me, chips = _place()
        half = ins[a].shape[0] // 2
        return _remote(ins[a].at[pl.ds(c * half, half), :], outs[a].at[me, pl.ds(c * half, half), :],
                       sems[0].at[3 * a + j], sems[1].at[3 * a + j], (*chip, c))

    def _fwd(self, outs, sems, a, j, chip, half_of):
        x, y, c, me, chips = _place()
        half = outs[a].shape[1] // 2
        blk = outs[a].at[2 * chip[0] + chip[1], pl.ds(half_of * half, half), :]
        return _remote(blk, blk, sems[2].at[3 * a + j], sems[3].at[3 * a + j], (x, y, 1 - c))

    def _own(self, ins, outs, sems, a):
        return _own_shard_to_sibling(ins[a], outs[a], sems[4].at[a], sems[5].at[a])

    def start(self, ins, outs, sems):
        chips = _place()[4]
        for a in range(self.n):
            for j, chip in enumerate(chips):
                self._ici(ins, outs, sems, a, j, chip).start()
        for a in range(self.n):
            self._own(ins, outs, sems, a).start()

    def middle(self, ins, outs, sems):
        x, y, c, me, chips = _place()
        for a in range(self.n):
            for j, chip in enumerate(chips):
                half = outs[a].shape[1] // 2
                blk = outs[a].at[2 * chip[0] + chip[1], pl.ds(c * half, half), :]
                _remote(blk, blk, sems[0].at[3 * a + j], sems[1].at[3 * a + j], (x, y, c)).wait_recv()
                self._fwd(outs, sems, a, j, chip, c).start()

    def finish(self, ins, outs, sems):
        x, y, c, me, chips = _place()
        for a in range(self.n):
            for j, chip in enumerate(chips):
                self._fwd(outs, sems, a, j, chip, 1 - c).wait_recv()
        for a in range(self.n):
            for j, chip in enumerate(chips):
                self._ici(ins, outs, sems, a, j, chip).wait_send()
                self._fwd(outs, sems, a, j, chip, c).wait_send()
            self._own(ins, outs, sems, a).wait()


def _own_shard_to_sibling(shard_ref, gathered_ref, send_sem, recv_sem):
    x, y, c, me, chips = _place()
    return _remote(shard_ref, gathered_ref.at[me], send_sem, recv_sem, (x, y, 1 - c))


class _NoExchange(_Exchange):
    n_in = n_out = 0
    out_shape = ()
    scratch = ()

    def start(self, ins, outs, sems):
        pass

    def finish(self, ins, outs, sems):
        pass


class _ForwardGathered(_Exchange):
    def __init__(self, shards, own=True, forward=True):
        self.own, self.forward = own, forward
        n = self.n = len(shards)
        self.n_in, self.n_out = 2 * n, n
        self.out_shape = [jax.ShapeDtypeStruct((N_CHIP,) + s.shape, s.dtype) for s in shards]
        dma = pltpu.SemaphoreType.DMA
        self.scratch = [dma((3 * n,)), dma((3 * n,)), dma((n,)), dma((n,))]
        self.aliases = {n + a: a for a in range(n)}

    def _fwd(self, outs, sems, a, j, chip, half_of):
        x, y, c, me, chips = _place()
        half = outs[a].shape[1] // 2
        blk = outs[a].at[2 * chip[0] + chip[1], pl.ds(half_of * half, half), :]
        return _remote(blk, blk, sems[0].at[3 * a + j], sems[1].at[3 * a + j], (x, y, 1 - c))

    def _own(self, ins, outs, sems, a):
        return _own_shard_to_sibling(ins[a], outs[a], sems[2].at[a], sems[3].at[a])

    def start(self, ins, outs, sems):
        x, y, c, me, chips = _place()
        for a in range(self.n):
            for j, chip in enumerate(chips if self.forward else ()):
                self._fwd(outs, sems, a, j, chip, c).start()
        for a in range(self.n if self.own else 0):
            self._own(ins, outs, sems, a).start()

    def finish(self, ins, outs, sems):
        x, y, c, me, chips = _place()
        for a in range(self.n):
            for j, chip in enumerate(chips if self.forward else ()):
                self._fwd(outs, sems, a, j, chip, 1 - c).wait_recv()
        for a in range(self.n):
            for j, chip in enumerate(chips if self.forward else ()):
                self._fwd(outs, sems, a, j, chip, c).wait_send()
            if self.own:
                self._own(ins, outs, sems, a).wait()


HBM = pl.BlockSpec(memory_space=pltpu.HBM)
SEMS = pl.BlockSpec(memory_space=pltpu.SEMAPHORE)
DATAFLOW = pltpu.SideEffectType.DATAFLOW_SIDE_EFFECTING


class _OverIci:
    def __init__(self, name, sources, lands, pieces=1):
        self.name, self.n, self.pieces = name, len(sources), pieces
        hbm = lambda t: pltpu.with_memory_space_constraint(t, pltpu.HBM)
        self.arrays = [hbm(t) for t in sources] + [hbm(t) for t in lands]

    def sent(self, src, land, a, chip):
        raise NotImplementedError

    def landed(self, land, a, chip):
        raise NotImplementedError

    def _copies(self, arr, sems, a, j, receiving):
        x, y, c, me, chips = _place()
        src, dst = self.sent(arr[a], arr[self.n + a], a, chips[j])
        if receiving:
            dst = self.landed(arr[self.n + a], a, chips[j])
        rows = src.shape[0] // self.pieces
        assert rows * self.pieces == src.shape[0]
        for p in range(self.pieces):
            i = (3 * a + j) * self.pieces + p
            cut = pl.ds(p * rows, rows)
            yield _remote(src.at[cut, :], dst.at[cut, :], sems[0].at[i], sems[1].at[i], (*chips[j], c))

    def start(self, after):
        m = len(self.arrays)

        def body(*refs):
            arr, sems, token = refs[:m], refs[m + 1:m + 3], refs[-1]
            for a in range(self.n):
                for j in range(3):
                    for copy in self._copies(arr, sems, a, j, False):
                        copy.start()
            token[...] = jnp.zeros_like(token)

        dma = pltpu.SemaphoreType.DMA
        outs = pl.pallas_call(
            body, name=self.name + "_start",
            out_shape=[dma((3 * self.n * self.pieces,))] * 2 + [pltpu.HBM(t.shape, t.dtype) for t in self.arrays]
                      + [jax.ShapeDtypeStruct((8, 128), F32)],
            in_specs=[HBM] * m + [ANY], out_specs=[SEMS, SEMS] + [HBM] * m + [VMEM],
            input_output_aliases={i: 2 + i for i in range(m)},
            compiler_params=pltpu.CompilerParams(has_side_effects=DATAFLOW),
        )(*self.arrays, after)
        self.sems, self.arrays = outs[0:2], list(outs[2:2 + m])
        return outs[-1]

    def wait(self, after):
        m = len(self.arrays)

        def body(*refs):
            arr, sems = refs[:m], refs[m:m + 2]
            for a in range(self.n):
                for j in range(3):
                    for copy in self._copies(arr, sems, a, j, False):
                        copy.wait_send()
                    for copy in self._copies(arr, sems, a, j, True):
                        copy.wait_recv()

        outs = pl.pallas_call(
            body, name=self.name + "_wait",
            out_shape=[pltpu.HBM(t.shape, t.dtype) for t in self.arrays],
            in_specs=[HBM] * m + [SEMS, SEMS, ANY], out_specs=[HBM] * m,
            input_output_aliases={i: i for i in range(m)},
            compiler_params=pltpu.CompilerParams(has_side_effects=DATAFLOW),
        )(*self.arrays, *self.sems, after)
        return list(outs[:self.n]), list(outs[self.n:])


class _GatherOverIci(_OverIci):
    def __init__(self, name, shards, pieces=1):
        super().__init__(name, shards, [lax.empty((N_CHIP,) + s.shape, s.dtype) for s in shards], pieces)

    @staticmethod
    def _half(ref):
        c = lax.axis_index("c")
        half = ref.shape[-2] // 2
        return pl.ds(c * half, half)

    def sent(self, src, land, a, chip):
        return src.at[self._half(src), :], land.at[_place()[3], self._half(src), :]

    def landed(self, land, a, chip):
        return land.at[2 * chip[0] + chip[1], self._half(land), :]


class _SumOverIci(_OverIci):
    def __init__(self, name, pre):
        super().__init__(name, pre, [lax.empty(p.shape, p.dtype) for p in pre])

    def sent(self, src, land, a, chip):
        return src.at[2 * chip[0] + chip[1]], land.at[_place()[3]]

    def landed(self, land, a, chip):
        return land.at[2 * chip[0] + chip[1]]


class _HalvesToSibling(_Exchange):
    def __init__(self, grads):
        n = self.n = len(grads)
        self.n_in = self.n_out = n
        self.out_shape = [jax.ShapeDtypeStruct((N_CHIP, g.shape[1] // 2, g.shape[2]), g.dtype) for g in grads]
        self.scratch = [pltpu.SemaphoreType.DMA((n,)), pltpu.SemaphoreType.DMA((n,))]

    def _copy(self, ins, outs, sems, a):
        x, y, c, me, chips = _place()
        half = ins[a].shape[1] // 2
        return _remote(ins[a].at[:, pl.ds((1 - c) * half, half), :], outs[a], sems[0].at[a], sems[1].at[a], (x, y, 1 - c))

    def start(self, ins, outs, sems):
        for a in range(self.n):
            self._copy(ins, outs, sems, a).start()

    def finish(self, ins, outs, sems):
        for a in range(self.n):
            self._copy(ins, outs, sems, a).wait_recv()
        for a in range(self.n):
            self._copy(ins, outs, sems, a).wait_send()


class _ShareHalves(_Exchange):
    def __init__(self, fulls):
        n = self.n = len(fulls)
        self.n_in = self.n_out = n
        self.out_shape = [jax.ShapeDtypeStruct(f.shape, f.dtype) for f in fulls]
        self.scratch = [pltpu.SemaphoreType.DMA((n,)), pltpu.SemaphoreType.DMA((n,))]
        self.aliases = {a: a for a in range(n)}

    def _copy(self, outs, sems, a, half_of):
        x, y, c, me, chips = _place()
        half = outs[a].shape[0] // 2
        rows = outs[a].at[pl.ds(half_of * half, half), :]
        return _remote(rows, rows, sems[0].at[a], sems[1].at[a], (x, y, 1 - c))

    def start(self, ins, outs, sems):
        c = _place()[2]
        for a in range(self.n):
            self._copy(outs, sems, a, c).start()

    def finish(self, ins, outs, sems):
        c = _place()[2]
        for a in range(self.n):
            self._copy(outs, sems, a, 1 - c).wait_recv()
        for a in range(self.n):
            self._copy(outs, sems, a, c).wait_send()


class _GatherBlocks(_Exchange):
    def __init__(self, block):
        self.n_in = self.n_out = 1
        self.out_shape = [jax.ShapeDtypeStruct((8,) + block.shape, block.dtype)]
        dma = pltpu.SemaphoreType.DMA
        self.scratch = [dma((7,)), dma((7,)), dma]

    @staticmethod
    def _peer(f):
        x, y, c, me, chips = _place()
        return ((1 - x) if f & 4 else x, (1 - y) if f & 2 else y, (1 - c) if f & 1 else c)

    def start(self, ins, outs, sems):
        x, y, c, me, chips = _place()
        for f in range(1, 8):
            _remote(ins[0], outs[0].at[2 * me + c], sems[0].at[f - 1], sems[1].at[f - 1], self._peer(f)).start()
        pltpu.make_async_copy(ins[0], outs[0].at[2 * me + c], sems[2]).start()

    def finish(self, ins, outs, sems):
        x, y, c, me, chips = _place()
        for f in range(1, 8):
            px, py, pc = self._peer(f)
            blk = outs[0].at[4 * px + 2 * py + pc]
            _remote(blk, blk, sems[0].at[f - 1], sems[1].at[f - 1], (x, y, c)).wait_recv()
        for f in range(1, 8):
            _remote(ins[0], outs[0].at[2 * me + c], sems[0].at[f - 1], sems[1].at[f - 1], self._peer(f)).wait_send()
        pltpu.make_async_copy(ins[0], outs[0].at[2 * me + c], sems[2]).wait()


class _Both(_Exchange):
    def __init__(self, first, second):
        self.parts = (first, second)
        self.n_in, self.n_out = first.n_in + second.n_in, first.n_out + second.n_out
        self.out_shape = first.out_shape + second.out_shape
        self.scratch = first.scratch + second.scratch
        self.aliases = dict(first.aliases)
        self.aliases.update({first.n_in + i: first.n_out + o for i, o in second.aliases.items()})

    def _split(self, ins, outs, sems):
        a, b = self.parts
        return ((a, ins[:a.n_in], outs[:a.n_out], sems[:len(a.scratch)]),
                (b, ins[a.n_in:], outs[a.n_out:], sems[len(a.scratch):]))

    def start(self, ins, outs, sems):
        for ex, i, o, s in self._split(ins, outs, sems):
            ex.start(i, o, s)

    def middle(self, ins, outs, sems):
        for ex, i, o, s in self._split(ins, outs, sems):
            ex.middle(i, o, s)

    def finish(self, ins, outs, sems):
        for ex, i, o, s in self._split(ins, outs, sems):
            ex.finish(i, o, s)


class _Bound:
    def __init__(self, ex, ins, outs, sems):
        self.start = lambda: ex.start(ins, outs, sems)
        self.middle = lambda: ex.middle(ins, outs, sems)
        self.finish = lambda: ex.finish(ins, outs, sems)


def _carry(name, body, ex, ex_args, args, in_specs, out_specs, out_shape, scratch_shapes=(), grid=None, semantics=(),
           after=None):
    n_a, n_o, n_s = len(args), len(out_shape), len(scratch_shapes)
    behind = [] if after is None else [after]

    def full_body(*refs):
        p = 0
        groups = []
        for size in (n_a, ex.n_in, len(behind), n_o, ex.n_out, n_s, len(ex.scratch)):
            groups.append(refs[p:p + size])
            p += size
        a, ei, _, o, eo, s, es = groups
        body(*a, *o, *s, _Bound(ex, ei, eo, es))

    kwargs = {} if grid is None else {"grid": grid}
    outs = pl.pallas_call(
        full_body, name=name,
        in_specs=list(in_specs) + [ANY] * (ex.n_in + len(behind)), out_specs=list(out_specs) + [ANY] * ex.n_out,
        out_shape=list(out_shape) + list(ex.out_shape), scratch_shapes=list(scratch_shapes) + list(ex.scratch),
        input_output_aliases={n_a + i: n_o + o for i, o in ex.aliases.items()},
        compiler_params=_params(*semantics) if semantics else pltpu.CompilerParams(vmem_limit_bytes=VMEM_LIMIT),
        **kwargs,
    )(*args, *ex_args, *behind)
    return outs[:n_o], outs[n_o:]


def _cast_bf16(arrays):
    n = len(arrays)

    def body(*refs):
        for a in range(n):
            refs[n + a][...] = refs[a][...].astype(BF16)

    blks = [pl.BlockSpec((t.shape[0] // 4, t.shape[1]), lambda i: (i, 0)) for t in arrays]
    return pl.pallas_call(
        body, grid=(4,), name="cast_bf16", in_specs=blks, out_specs=blks,
        out_shape=[jax.ShapeDtypeStruct(t.shape, BF16) for t in arrays], compiler_params=_params("parallel"),
    )(*arrays)


def _prepare(x, g1, pos, ifc, after):
    tm = 512

    def body(x_ref, g_ref, pos_ref, ifc_ref, h_ref, cos_ref, sin_ref, _):
        xv = x_ref[...]
        h_ref[...] = (xv * _rstd(xv) * g_ref[...]).astype(BF16)
        ang = pos_ref[...].astype(F32) * ifc_ref[...]
        cos_ref[...] = jnp.cos(ang)
        sin_ref[...] = jnp.sin(ang)

    row = lambda w: pl.BlockSpec((tm, w), lambda i: (i, 0))
    const = lambda w: pl.BlockSpec((1, w), lambda i: (0, 0))
    return _carry("prepare", body, _NoExchange(), (), (x, g1, pos, ifc),
                  [row(D), const(D), row(1), const(128)], [row(D), row(128), row(128)],
                  [jax.ShapeDtypeStruct((S, D), BF16)] + [jax.ShapeDtypeStruct((S, 128), F32)] * 2,
                  grid=(S // tm,), semantics=("parallel",), after=after)[0]


def _exchange_alone(name, ex, ex_args):
    def body(xc):
        xc.start()
        xc.middle()
        xc.finish()

    return _carry(name, body, ex, ex_args, (), (), (), ())[1]


def _core_index():
    return lax.axis_index("c").astype(jnp.int32).reshape(1)


def _pair_sum(gs, gots):
    n = len(gs)

    def body(c_ref, *refs):
        for a in range(n):
            refs[2 * n + a][...] = (refs[a][...].astype(F32) + refs[n + a][...].astype(F32)).astype(BF16)

    mine = [pl.BlockSpec((None, g.shape[1] // 2, g.shape[2]), lambda k, c_ref: (k, c_ref[0], 0)) for g in gs]
    blk = [pl.BlockSpec((None, g.shape[1] // 2, g.shape[2]), lambda k, c_ref: (k, 0, 0)) for g in gs]
    return pl.pallas_call(
        body, name=f"pair_sum_{gs[0].shape[1]}x{gs[0].shape[2]}",
        grid_spec=pltpu.PrefetchScalarGridSpec(
            num_scalar_prefetch=1, grid=(N_CHIP,), in_specs=mine + blk, out_specs=blk),
        out_shape=[jax.ShapeDtypeStruct((N_CHIP, g.shape[1] // 2, g.shape[2]), BF16) for g in gs],
        compiler_params=_params("parallel"),
    )(_core_index(), *gs, *gots)


def _chip_sum(pre, parts):
    n = len(parts)
    me = 2 * lax.axis_index("x") + lax.axis_index("y")
    others = [k + (k >= me).astype(jnp.int32) for k in range(3)]
    where = jnp.stack([lax.axis_index("c"), me, *others]).astype(jnp.int32)

    def body(w_ref, *refs):
        for a in range(n):
            own, p1, p2, p3 = refs[4 * a:4 * a + 4]
            refs[4 * n + a][...] = ((own[...].astype(F32) + p1[...].astype(F32)) + p2[...].astype(F32)) + p3[...].astype(F32)

    in_specs, out_specs, operands = [], [], []
    for a in range(n):
        _, half, cc = parts[a].shape
        tr = half // 2
        in_specs += [pl.BlockSpec((None, tr, cc), lambda i, w_ref, s=s: (w_ref[s], i, 0)) for s in (1, 2, 3, 4)]
        out_specs.append(pl.BlockSpec((tr, cc), lambda i, w_ref: (2 * w_ref[0] + i, 0)))
        operands += [pre[a], parts[a], parts[a], parts[a]]
    return pl.pallas_call(
        body, name=f"chip_sum_{parts[0].shape[1]}x{parts[0].shape[2]}",
        grid_spec=pltpu.PrefetchScalarGridSpec(num_scalar_prefetch=1, grid=(2,), in_specs=in_specs, out_specs=out_specs),
        out_shape=[jax.ShapeDtypeStruct((2 * p.shape[1], p.shape[2]), F32) for p in parts],
        compiler_params=_params("parallel"),
    )(where, *operands)


def _adamw_math(w, g, m, v):
    m = ADAM_B1 * m + (1.0 - ADAM_B1) * g
    v = ADAM_B2 * v + (1.0 - ADAM_B2) * (g * g)
    m_hat = m / (1.0 - ADAM_B1 ** ADAM_STEP)
    v_hat = v / (1.0 - ADAM_B2 ** ADAM_STEP)
    delta = -ADAM_LR * (m_hat / (jnp.sqrt(v_hat) + ADAM_EPS) + ADAM_WD * w)
    return delta, m, v


def _adamw(ws, gs, ms, vs, after=None):
    n = len(ws)

    def body(*refs):
        for a in range(n):
            w_ref, g_ref, m_ref, v_ref = (refs[t * n + a] for t in range(4))
            go_ref, d_ref, nm_ref, nv_ref = refs[4 * n + 4 * a:4 * n + 4 * a + 4]
            g = g_ref[...]
            go_ref[...] = g
            d_ref[...], nm_ref[...], nv_ref[...] = _adamw_math(w_ref[...], g, m_ref[...], v_ref[...])

    blks = [pl.BlockSpec((w.shape[0] // 4, w.shape[1]), lambda i: (i, 0)) for w in ws]
    outs = _carry(f"adamw_{ws[0].shape[0]}x{ws[0].shape[1]}", body, _NoExchange(), (), (*ws, *gs, *ms, *vs),
                  blks * 4, [b for b in blks for _ in range(4)],
                  [jax.ShapeDtypeStruct(w.shape, F32) for w in ws for _ in range(4)],
                  grid=(4,), semantics=("parallel",), after=after)[0]
    return [outs[4 * a:4 * a + 4] for a in range(n)]


def _adamw_gains(gall, ws, ms, vs):
    def body(ga_ref, *refs):
        w, m, v = refs[0:4], refs[4:8], refs[8:12]
        outs, loss_ref, total = refs[12:28], refs[28], refs[29]
        g = ga_ref[0]
        for dev in range(1, 8):
            g = g + ga_ref[dev]
        total[...] = g
        for i in range(4):
            gi = total[i:i + 1, :]
            outs[i][...] = gi
            outs[4 + i][...], outs[8 + i][...], outs[12 + i][...] = _adamw_math(w[i][...], gi, m[i][...], v[i][...])
        loss_ref[...] = total[4:5, 0:128] * (0.5 / D)

    outs = pl.pallas_call(
        body, name="adamw_gains",
        out_shape=[jax.ShapeDtypeStruct((1, D), F32)] * 16 + [jax.ShapeDtypeStruct((1, 128), F32)],
        scratch_shapes=[pltpu.VMEM((8, D), F32)],
    )(gall, *ws, *ms, *vs)
    return outs[0:4], outs[4:8], outs[8:12], outs[12:16], outs[16]


def kernel(x, positions, w_in, w_out, g_pre_mix, g_post_mix, g_pre_ffn, g_post_ffn, w_gate, w_up, w_down, loss_target, m_w_in, m_w_out, m_g_pre_mix, m_g_post_mix, m_g_pre_ffn, m_g_post_ffn, m_w_gate, m_w_up, m_w_down, v_w_in, v_w_out, v_g_pre_mix, v_g_post_mix, v_g_pre_ffn, v_g_post_ffn, v_w_gate, v_w_up, v_w_down):
    tr = lambda t: jnp.swapaxes(t, 1, 2)[0]
    shards = [w_in[0], w_out[0], tr(w_gate), tr(w_up), w_down[0]]
    moms = [m_w_in[0], m_w_out[0], tr(m_w_gate), tr(m_w_up), m_w_down[0]]
    vels = [v_w_in[0], v_w_out[0], tr(v_w_gate), tr(v_w_up), v_w_down[0]]
    xs, pos, tgt = x[0], positions.reshape(S, 1), loss_target[0]
    g1, g2, g3, g4 = g_pre_mix, g_post_mix, g_pre_ffn, g_post_ffn
    tabs = tuple(jnp.asarray(t) for t in _retention_tables())
    ifc, spread = _rotary_tables()
    ifc, spread = jnp.asarray(ifc), jnp.asarray(spread, dtype=BF16)
    bf = list(_cast_bf16(shards))

    win_gather = _GatherOverIci("win_gather", bf[:1], pieces=4)
    token = win_gather.start(bf[0])
    wout_gather = _GatherOverIci("wout_gather", bf[1:2])
    token = wout_gather.start(token)
    ffn_gather = _GatherOverIci("ffn_gather", bf[2:])
    token = ffn_gather.start(token)
    h1, cos, sin = _prepare(xs, g1, pos, ifc, token)
    win_sh, win_land = win_gather.wait(h1)
    (win_g,) = _exchange_alone("forward_win", _ForwardGathered(bf[:1]), [*win_sh, *win_land])
    qr, kr, rv, rg, aq, ak, av = _proj_fwd(h1, win_g, cos, sin, spread, None)
    wout_sh, wout_land = wout_gather.wait(qr)
    n_ffn = len(bf[2:])
    (att_out, lse, cat_a), (wout_g, *ffn_gather.arrays[n_ffn:]) = _att_fwd(
        aq, ak, av, _Both(_ForwardGathered(bf[1:2]), _ForwardGathered(bf[2:], forward=False)),
        [*wout_sh, *wout_land, *ffn_gather.arrays])
    wout_g = wout_g.reshape(D, D)
    (o_raw, cat_r, states), _ = _ret_fwd(qr, kr, rv, rg, tabs, _NoExchange(), (), cat_a)
    ffn_sh, ffn_lands = ffn_gather.wait(cat_r)
    (mix, x2, h3), (wg_g, wu_g, wd_g) = _mix_fwd(cat_r, cat_a, wout_g, xs, g2, g3,
                                                _ForwardGathered(bf[2:], own=False), [*ffn_sh, *ffn_lands])
    gt, up, a, sq, dy, df, dg4 = _ffn_fwd(h3, wg_g, wu_g, wd_g, x2, tgt, g4)

    dgt, dup, dx2, dmix, dg3, dg2 = _ffn_bwd_act(df, gt, up, wg_g, wu_g, wd_g, dy, x2, mix, g2, g3)
    ffn_grads = list(_ffn_bwd_w(a, df, h3, dgt, dup))
    (dret, datt, dwout), got = _mix_bwd(dmix, cat_r, cat_a, wout_g, _HalvesToSibling(ffn_grads), ffn_grads)
    ffn_sum = _SumOverIci("ffn_sum", _pair_sum(ffn_grads, got))
    token = ffn_sum.start(datt)
    (dq_att, dk_att, dv_att), _ = _att_bwd(aq, ak, av, datt, att_out, lse, _NoExchange(), (), token)
    (dqr, dkr, drv, drg), _ = _ret_bwd(qr, kr, rv, rg, o_raw, states, dret, tabs, _NoExchange(), (), token)
    dproj = _rot_bwd(cos, sin, spread, dqr, dkr, drv, drg, dq_att, dk_att, dv_att)
    sums = _chip_sum(*ffn_sum.wait(dproj))
    dwin, ffn_full = _win_bwd_w(h1, dproj, _ShareHalves(sums), sums)
    in_grads = [dwin, dwout.reshape(N_CHIP, WOUT_R, D)]

    got = _exchange_alone("halves_to_sibling", _HalvesToSibling(in_grads), in_grads)
    in_sum = _SumOverIci("in_sum", _pair_sum(in_grads, got))
    token = in_sum.start(dproj)
    dx, gblock = _in_bwd(dproj, win_g, xs, dx2, g1, [dg2, dg3, dg4, sq], token)
    ffn_upd = _adamw(shards[2:], [ffn_full[o] for o in (1, 2, 0)],
                     moms[2:], vels[2:], token)
    pre, parts = in_sum.wait(ffn_upd[2][0])
    sums = _chip_sum(pre, parts)
    *in_full, gall = _exchange_alone("share_rest", _Both(_ShareHalves(sums), _GatherBlocks(gblock)), [*sums, gblock])
    upd = _adamw(shards[:2], in_full, moms[:2], vels[:2]) + ffn_upd
    gg, gd, gm, gv, loss_row = _adamw_gains(gall, [g1, g2, g3, g4],
                                            [m_g_pre_mix, m_g_post_mix, m_g_pre_ffn, m_g_post_ffn],
                                            [v_g_pre_mix, v_g_post_mix, v_g_pre_ffn, v_g_post_ffn])

    def order(mats, vecs):
        back = lambda t: jnp.swapaxes(t[None], 1, 2)
        return [mats[0][None], mats[1][None], *vecs, back(mats[2]), back(mats[3]), mats[4][None]]

    return (loss_row[0, 0], dx[None],
            *order([u[0] for u in upd], gg),
            *order([u[1] for u in upd], gd),
            *order([u[2] for u in upd], gm),
            *order([u[3] for u in upd], gv))
```

```python
import numpy as np
import jax
import jax.numpy as jnp
from jax import lax
from jax.experimental import pallas as pl
from jax.experimental.pallas import tpu as pltpu

F32, BF16 = jnp.float32, jnp.bfloat16
MESH = pl.DeviceIdType.MESH

S = 2048
D = 1024
PW = 3072
N_CHIP = 4
WIN_C = PW // N_CHIP
DFF = 2816
FF_C = DFF // N_CHIP
WOUT_R = D // N_CHIP
RMS_EPS = 1e-6
GN_EPS = 1e-5
RET_C = 128
RET_PER_STEP = 4
RET_SCALE = 32 ** -0.5
ATT_BLK = 128
ATT_SCALE = 64 ** -0.5
PATTERN_DILATIONS = (16, 1, 4)
NEG = -1e30
VMEM_LIMIT = 56 * 1024 * 1024

ADAM_LR, ADAM_B1, ADAM_B2, ADAM_EPS, ADAM_WD, ADAM_STEP = 0.001, 0.9, 0.999, 1e-08, 0.01, 10


def _params(*sem):
    return pltpu.CompilerParams(dimension_semantics=sem, vmem_limit_bytes=VMEM_LIMIT)


def _nt(a, b):
    return lax.dot_general(a, b, (((1,), (1,)), ((), ())), preferred_element_type=F32)


def _tn(a, b):
    return lax.dot_general(a, b, (((0,), (0,)), ((), ())), preferred_element_type=F32)


def _nn(a, b):
    return jnp.dot(a, b, preferred_element_type=F32)


def _rstd(v):
    return lax.rsqrt(jnp.mean(v * v, axis=-1, keepdims=True) + RMS_EPS)


def _sigmoid(v):
    return 1.0 / (1.0 + jnp.exp(-v))


def _rows(i, t):
    return pl.ds(pl.multiple_of(i * t, t), t)


def _retention_tables():
    h = np.arange(8, dtype=np.float32)
    log_g = np.log1p(-np.exp2(-5.0 - h)).astype(np.float32)
    idx = np.arange(RET_C, dtype=np.float32)
    diff = idx[:, None] - idx[None, :]
    dtab = np.where(diff >= 0, np.exp(log_g[:, None, None] * np.maximum(diff, 0.0)), 0.0).astype(np.float32)
    dtab = dtab.reshape(8 * RET_C, RET_C)
    lane_head = np.arange(256) // 32
    a_tab = np.exp(log_g[lane_head][None, :] * (idx + 1.0)[:, None]).astype(np.float32)
    b_tab = np.exp(log_g[lane_head][None, :] * (RET_C - 1.0 - idx)[:, None]).astype(np.float32)
    lam = np.exp(log_g[lane_head] * RET_C).astype(np.float32)[:, None]
    bd = (lane_head[:, None] == (np.arange(512) // 64)[None, :]).astype(np.float32)
    return dtab, a_tab, b_tab, lam, bd


def _rotary_tables():
    inv_r = (1.0 / (np.float32(10000.0) ** np.linspace(0.0, 1.0, 16, dtype=np.float32))).astype(np.float32)
    inv_a = (np.float32(500000.0) ** (-np.arange(0, 16, 2, dtype=np.float32) / np.float32(16))).astype(np.float32)
    ifc = np.zeros((1, 128), np.float32)
    ifc[0, 0:16], ifc[0, 16:24] = inv_r, inv_a
    spread = np.zeros((128, 768), np.float32)
    for lane in range(256):
        spread[(lane % 32) % 16, lane] = 1.0
    for lane in range(512):
        d = lane % 64
        spread[16 + d % 8 if d < 16 else 24, 256 + lane] = 1.0
    return ifc, spread


def _rot_halves(tm):
    lo_r = (lax.broadcasted_iota(jnp.int32, (tm, 256), 1) % 32) < 16
    lo_a = (lax.broadcasted_iota(jnp.int32, (tm, 512), 1) % 64) < 8
    return lo_r, lo_a


def _spread_exact(t, e):
    hi = t.astype(BF16)
    r1 = t - hi.astype(F32)
    mid = r1.astype(BF16)
    lo = (r1 - mid.astype(F32)).astype(BF16)
    return _nn(hi, e) + _nn(mid, e) + _nn(lo, e)


def _rot_tables(cos_ref, sin_ref, e_ref):
    cs = _spread_exact(cos_ref[...], e_ref[...])
    sn = _spread_exact(sin_ref[...], e_ref[...])
    return cs[:, 0:256], cs[:, 256:768], sn[:, 0:256], sn[:, 256:768]


def _proj_fwd(h1, win_g, cos, sin, spread, after):
    tm = 256

    def body(h_ref, w_ref, cos_ref, sin_ref, e_ref, qr_ref, kr_ref, rv_ref, rg_ref, aq_ref, ak_ref, av_ref, p_ref, _):
        h = h_ref[...]
        for k in range(N_CHIP):
            p_ref[:, k * WIN_C:(k + 1) * WIN_C] = _nn(h, w_ref[k])
        cr, ca, sr, sa = _rot_tables(cos_ref, sin_ref, e_ref)
        lo_r, lo_a = _rot_halves(tm)

        def rot_r(v):
            return v * cr + sr * jnp.where(lo_r, -pltpu.roll(v, 240, 1), pltpu.roll(v, 16, 1))

        def rot_a(v):
            return v * ca + sa * jnp.where(lo_a, -pltpu.roll(v, 504, 1), pltpu.roll(v, 8, 1))

        qr_ref[...] = rot_r(p_ref[:, 0:256]).astype(BF16)
        kr_ref[...] = (rot_r(p_ref[:, 256:512]) * RET_SCALE).astype(BF16)
        rv_ref[...] = p_ref[:, 512:1024].astype(BF16)
        rg_ref[...] = p_ref[:, 1024:1536]
        aq, ak = rot_a(p_ref[:, 1536:2048]), rot_a(p_ref[:, 2048:2560])
        for j in range(4):
            aq_ref[j] = aq[:, 128 * j:128 * j + 128]
            ak_ref[j] = ak[:, 128 * j:128 * j + 128]
            av_ref[j] = p_ref[:, 2560 + 128 * j:2560 + 128 * j + 128]

    row = lambda w: pl.BlockSpec((tm, w), lambda i: (i, 0))
    slab = pl.BlockSpec((4, tm, 128), lambda i: (0, i, 0))
    return _carry(
        "proj_fwd", body, _NoExchange(), (), (h1, win_g, cos, sin, spread),
        [row(D), pl.BlockSpec((N_CHIP, D, WIN_C), lambda i: (0, 0, 0)), row(128), row(128),
         pl.BlockSpec((128, 768), lambda i: (0, 0))],
        [row(256), row(256), row(512), row(512), slab, slab, slab],
        [jax.ShapeDtypeStruct((S, w), BF16) for w in (256, 256, 512)]
        + [jax.ShapeDtypeStruct((S, 512), F32)] + [jax.ShapeDtypeStruct((4, S, 128), F32)] * 3,
        scratch_shapes=[pltpu.VMEM((tm, PW), F32)], grid=(S // tm,), semantics=("parallel",), after=after)[0]


def _seg_mean(v):
    lo = lax.broadcasted_iota(jnp.int32, v.shape, 1) < 64
    s_lo = jnp.sum(jnp.where(lo, v, 0.0), axis=-1, keepdims=True)
    s_hi = jnp.sum(jnp.where(lo, 0.0, v), axis=-1, keepdims=True)
    return jnp.where(lo, s_lo, s_hi) * (1.0 / 64.0)


def _ret_fwd(qr, kr, rv, proj, tabs, exchange, exchange_args, after=None):
    C, G = RET_C, RET_PER_STEP
    steps = S // (C * G)
    dtab, a_tab, b_tab, lam, bd = tabs

    def body(q_ref, k_ref, v_ref, g_ref, dt_ref, a_ref, b_ref, lam_ref, bd_ref, o_ref, cat_ref, st_ref, R, exch):
        @pl.when(pl.program_id(0) == 0)
        def _():
            exch.start()
            R[...] = jnp.zeros_like(R)

        lane_head = lax.broadcasted_iota(jnp.int32, (C, 256), 1) // 32
        col_head = lax.broadcasted_iota(jnp.int32, (C, 256), 1) // 64
        for s in range(G):
            rows = slice(s * C, (s + 1) * C)
            q, k, v = q_ref[rows, :], k_ref[rows, :], v_ref[rows, :]
            rb = R[...].astype(BF16)
            st_ref[s] = rb
            qa = (q.astype(F32) * a_ref[...]).astype(BF16)
            cross = _nn(qa, rb)
            p = (_nt(_stack_heads(q, lane_head, n=8), k) * dt_ref[...]).astype(BF16)
            og = [cross[:, 256 * g:256 * g + 256]
                  + _unstack_heads(_nn(p[4 * C * g:4 * C * (g + 1)], v[:, 256 * g:256 * g + 256]), col_head)
                  for g in range(2)]
            kb = (k.astype(F32) * b_ref[...]).astype(BF16)
            R[...] = R[...] * lam_ref[...] + _tn(kb, v) * bd_ref[...]
            o_ref[rows, 0:256] = og[0]
            o_ref[rows, 256:512] = og[1]
            for j in range(4):
                oj = og[j // 2][:, 128 * (j % 2):128 * (j % 2) + 128]
                xc = oj - _seg_mean(oj)
                rn = xc * lax.rsqrt(_seg_mean(xc * xc) + GN_EPS)
                gj = g_ref[rows, 128 * j:128 * j + 128]
                cat_ref[rows, 128 * j:128 * j + 128] = (rn * (gj * _sigmoid(gj))).astype(BF16)

        @pl.when(pl.program_id(0) == steps - 1)
        def _():
            exch.middle()
            exch.finish()

    row = lambda w: pl.BlockSpec((C * G, w), lambda n: (n, 0))
    full = lambda a: pl.BlockSpec(a.shape, lambda n: (0,) * a.ndim)
    return _carry(
        "ret_fwd", body, exchange, exchange_args, (qr, kr, rv, proj, dtab, a_tab, b_tab, lam, bd),
        [row(256), row(256), row(512), row(512),
         full(dtab), full(a_tab), full(b_tab), full(lam), full(bd)],
        [row(512), row(512), pl.BlockSpec((G, 256, 512), lambda n: (n, 0, 0))],
        [jax.ShapeDtypeStruct((S, 512), F32), jax.ShapeDtypeStruct((S, 512), BF16),
         jax.ShapeDtypeStruct((S // C, 256, 512), BF16)],
        scratch_shapes=[pltpu.VMEM((256, 512), F32)], grid=(steps,), semantics=("arbitrary",), after=after)


def _stack_heads(v, lane_head, fill=0.0, n=4):
    return jnp.concatenate([jnp.where(lane_head == h, v, jnp.full_like(v, fill)) for h in range(n)], axis=0)


def _unstack_heads(v, lane_head, n=4):
    out = v[0:ATT_BLK]
    for h in range(1, n):
        out = jnp.where(lane_head == h, v[h * ATT_BLK:(h + 1) * ATT_BLK], out)
    return out


def _att_bias(has_prev):
    nk = 2 * ATT_BLK if has_prev else ATT_BLK
    a = lax.broadcasted_iota(jnp.int32, (4 * ATT_BLK, nk), 0) % ATT_BLK
    kk = lax.broadcasted_iota(jnp.int32, (4 * ATT_BLK, nk), 1)
    if not has_prev:
        return None, jnp.where((a - kk) >= 0, 0.0, NEG)
    dist = ATT_BLK + a - kk
    inside = (dist >= 0) & (dist <= ATT_BLK)
    return jnp.where(inside, 0.0, NEG), jnp.where(inside & (kk >= ATT_BLK), 0.0, NEG)


def _class_rows(ib, r, d):
    if d == 1:
        return pl.ds(pl.multiple_of(ib * ATT_BLK, ATT_BLK), ATT_BLK)
    return pl.ds(ib * ATT_BLK * d + r, ATT_BLK, stride=d)


def _slab_pair(ref, g, rows):
    return jnp.concatenate([ref[2 * g, rows, :], ref[2 * g + 1, rows, :]], axis=1)


def _att_blocks(d):
    nb = S // d // ATT_BLK
    return nb, nb > 1


def _att_fwd(aq, ak, av, exchange, exchange_args):
    def body(q_ref, k_ref, v_ref, o_ref, l_ref, cat_ref, xc):
        xc.start()
        lane_head = lax.broadcasted_iota(jnp.int32, (ATT_BLK, 256), 1) // 64
        for pi, d in enumerate(PATTERN_DILATIONS):
            if pi == len(PATTERN_DILATIONS) - 1:
                xc.middle()
            nb, has_prev = _att_blocks(d)
            bias_rest, bias_first = _att_bias(has_prev)

            def block(b, carry, pi=pi, d=d, nb=nb, has_prev=has_prev, bias_rest=bias_rest, bias_first=bias_first):
                r, ib = b // nb, b % nb
                rows = _class_rows(ib, r, d)
                prow = _class_rows(jnp.maximum(ib - 1, 0), r, d)
                bias = jnp.where(ib == 0, bias_first, bias_rest) if has_prev else bias_first
                for g in range(2):
                    qg = _slab_pair(q_ref, g, rows).astype(BF16)
                    kg = _slab_pair(k_ref, g, rows)
                    vg = _slab_pair(v_ref, g, rows)
                    if has_prev:
                        kg = jnp.concatenate([_slab_pair(k_ref, g, prow), kg], axis=0)
                        vg = jnp.concatenate([_slab_pair(v_ref, g, prow), vg], axis=0)
                    kg, vg = kg.astype(BF16), vg.astype(BF16)
                    s = _nt(_stack_heads(qg, lane_head), kg) * ATT_SCALE + bias
                    m = jnp.max(s, axis=-1, keepdims=True)
                    p = jnp.exp(s - m)
                    den = jnp.sum(p, axis=-1, keepdims=True)
                    og = _unstack_heads(_nn(p.astype(BF16), vg) / den, lane_head)
                    lg = _unstack_heads(jnp.broadcast_to(m + jnp.log(den), (4 * ATT_BLK, 256)), lane_head)
                    for jj in range(2):
                        j = 2 * g + jj
                        o_new, l_new = og[:, 128 * jj:128 * jj + 128], lg[:, 128 * jj:128 * jj + 128]
                        if pi > 0:
                            o_old, l_old = o_ref[j, rows, :], l_ref[j, rows, :]
                            mx = jnp.maximum(l_old, l_new)
                            ea, eb = jnp.exp(l_old - mx), jnp.exp(l_new - mx)
                            den = ea + eb
                            o_new = (ea * o_old + eb * o_new) / den
                            l_new = mx + jnp.log(den)
                        o_ref[j, rows, :] = o_new
                        l_ref[j, rows, :] = l_new
                return carry

            lax.fori_loop(0, S // ATT_BLK, block, 0, unroll=4)

        def to_cat(i, carry):
            rows = _rows(i, 256)
            for j in range(4):
                cat_ref[rows, 128 * j:128 * j + 128] = o_ref[j, rows, :].astype(BF16)
            return carry

        lax.fori_loop(0, S // 256, to_cat, 0)
        xc.finish()

    slab = jax.ShapeDtypeStruct((4, S, 128), F32)
    return _carry("att_fwd", body, exchange, exchange_args, (aq, ak, av), [VMEM] * 3, [VMEM] * 3,
                  [slab, slab, jax.ShapeDtypeStruct((S, 512), BF16)])


def _mix_fwd(cat_r, cat_a, wout, x, g2, g3, exchange, exchange_args):
    tm = 512

    def body(cr_ref, ca_ref, w_ref, x_ref, g2_ref, g3_ref, mix_ref, x2_ref, h3_ref, xc):
        @pl.when(pl.program_id(0) == 0)
        def _():
            xc.start()

        mix = _nn(cr_ref[...], w_ref[0:512, :]) + _nn(ca_ref[...], w_ref[512:1024, :])
        mix_ref[...] = mix
        x2 = x_ref[...] + mix * _rstd(mix) * g2_ref[...]
        x2_ref[...] = x2
        h3_ref[...] = (x2 * _rstd(x2) * g3_ref[...]).astype(BF16)

        @pl.when(pl.program_id(0) == S // tm - 1)
        def _():
            xc.middle()
            xc.finish()

    row = lambda w: pl.BlockSpec((tm, w), lambda i: (i, 0))
    vec = pl.BlockSpec((1, D), lambda i: (0, 0))
    return _carry("mix_fwd", body, exchange, exchange_args, (cat_r, cat_a, wout, x, g2, g3),
                  [row(512), row(512), pl.BlockSpec((D, D), lambda i: (0, 0)), row(D), vec, vec],
                  [row(D), row(D), row(D)],
                  [jax.ShapeDtypeStruct((S, D), F32), jax.ShapeDtypeStruct((S, D), F32),
                   jax.ShapeDtypeStruct((S, D), BF16)],
                  grid=(S // tm,), semantics=("arbitrary",))


def _ffn_fwd(h3, wg, wu, wd, x2, tgt, g4):
    tm = 512
    last = N_CHIP - 1

    def body(h_ref, wg_ref, wu_ref, wd_ref, x2_ref, t_ref, g_ref,
             gt_ref, up_ref, a_ref, loss_ref, dy_ref, df_ref, dg_ref, f_ref):
        k, i = pl.program_id(0), pl.program_id(1)
        h = h_ref[...]
        gt = _nt(h, wg_ref[...])
        up = _nt(h, wu_ref[...])
        gt_ref[...] = gt.astype(BF16)
        up_ref[...] = up.astype(BF16)
        a = (gt * _sigmoid(gt) * up).astype(BF16)
        a_ref[...] = a
        part = _nn(a, wd_ref[...])
        rows = _rows(i, tm)

        @pl.when(k == 0)
        def _():
            f_ref[rows, :] = part

        @pl.when((k > 0) & (k < last))
        def _():
            f_ref[rows, :] = f_ref[rows, :] + part

        @pl.when((k == last) & (i == 0))
        def _():
            loss_ref[...] = jnp.zeros_like(loss_ref)
            dg_ref[...] = jnp.zeros_like(dg_ref)

        @pl.when(k == last)
        def _():
            fv = f_ref[rows, :] + part
            r = _rstd(fv)
            fn = fv * r
            e = x2_ref[...] + fn * g_ref[...] - t_ref[...]
            loss_ref[...] = loss_ref[...] + jnp.sum(jnp.sum(e * e, axis=-1, keepdims=True), axis=0, keepdims=True)
            dy = e * (1.0 / D)
            dy_ref[...] = dy
            dg_ref[...] = dg_ref[...] + jnp.sum(dy * fn, axis=0, keepdims=True)
            t = dy * g_ref[...]
            df_ref[...] = (r * (t - fn * jnp.mean(t * fn, axis=-1, keepdims=True))).astype(BF16)

    wrow = pl.BlockSpec((None, FF_C, D), lambda k, i: (k, 0, 0))
    act = pl.BlockSpec((None, tm, FF_C), lambda k, i: (k, i, 0))
    late = pl.BlockSpec((tm, D), lambda k, i: (jnp.where(k == last, i, 0), 0))
    vec = pl.BlockSpec((1, D), lambda k, i: (0, 0))
    return pl.pallas_call(
        body, grid=(N_CHIP, S // tm), name="ffn_fwd",
        in_specs=[pl.BlockSpec((tm, D), lambda k, i: (i, 0)), wrow, wrow, wrow, late, late, vec],
        out_specs=[act, act, act, vec, late, late, vec],
        out_shape=[jax.ShapeDtypeStruct((N_CHIP, S, FF_C), BF16)] * 3
                  + [jax.ShapeDtypeStruct((1, D), F32), jax.ShapeDtypeStruct((S, D), F32),
                     jax.ShapeDtypeStruct((S, D), BF16), jax.ShapeDtypeStruct((1, D), F32)],
        scratch_shapes=[pltpu.VMEM((S, D), F32)],
        compiler_params=_params("arbitrary", "arbitrary"),
    )(h3, wg, wu, wd, x2, tgt, g4)


def _ffn_bwd_act(df, gt, up, wg, wu, wd, dy, x2, mix, g2, g3):
    tm, sub = 512, 256
    last = N_CHIP - 1

    def body(df_ref, gt_ref, up_ref, wg_ref, wu_ref, wd_ref, dy_ref, x2_ref, mix_ref, g2_ref, g3_ref,
             dgt_ref, dup_ref, dx2_ref, dmix_ref, dg3_ref, dg2_ref, dh_ref):
        k, i = pl.program_id(0), pl.program_id(1)
        parts = []
        for s in range(tm // sub):
            rows = slice(s * sub, (s + 1) * sub)
            da = _nt(df_ref[rows, :], wd_ref[...])
            gt, up = gt_ref[rows, :].astype(F32), up_ref[rows, :].astype(F32)
            sg = _sigmoid(gt)
            dup = (da * gt * sg).astype(BF16)
            dgt = (da * up * (sg * (1.0 + gt * (1.0 - sg)))).astype(BF16)
            dup_ref[rows, :] = dup
            dgt_ref[rows, :] = dgt
            parts.append(_nn(dgt, wg_ref[...]) + _nn(dup, wu_ref[...]))
        part = jnp.concatenate(parts, axis=0)
        rows = _rows(i, tm)

        @pl.when(k == 0)
        def _():
            dh_ref[rows, :] = part

        @pl.when((k > 0) & (k < last))
        def _():
            dh_ref[rows, :] = dh_ref[rows, :] + part

        @pl.when((k == last) & (i == 0))
        def _():
            dg3_ref[...] = jnp.zeros_like(dg3_ref)
            dg2_ref[...] = jnp.zeros_like(dg2_ref)

        @pl.when(k == last)
        def _():
            dh = dh_ref[rows, :] + part
            x2 = x2_ref[...]
            r3 = _rstd(x2)
            xn = x2 * r3
            dg3_ref[...] = dg3_ref[...] + jnp.sum(dh * xn, axis=0, keepdims=True)
            t = dh * g3_ref[...]
            dx2 = dy_ref[...] + r3 * (t - xn * jnp.mean(t * xn, axis=-1, keepdims=True))
            dx2_ref[...] = dx2
            mix = mix_ref[...]
            r2 = _rstd(mix)
            mn = mix * r2
            dg2_ref[...] = dg2_ref[...] + jnp.sum(dx2 * mn, axis=0, keepdims=True)
            u = dx2 * g2_ref[...]
            dmix_ref[...] = (r2 * (u - mn * jnp.mean(u * mn, axis=-1, keepdims=True))).astype(BF16)

    wrow = pl.BlockSpec((None, FF_C, D), lambda k, i: (k, 0, 0))
    act = pl.BlockSpec((None, tm, FF_C), lambda k, i: (k, i, 0))
    row = pl.BlockSpec((tm, D), lambda k, i: (i, 0))
    late = pl.BlockSpec((tm, D), lambda k, i: (jnp.where(k == last, i, 0), 0))
    vec = pl.BlockSpec((1, D), lambda k, i: (0, 0))
    return pl.pallas_call(
        body, grid=(N_CHIP, S // tm), name="ffn_bwd_act",
        in_specs=[row, act, act, wrow, wrow, wrow, late, late, late, vec, vec],
        out_specs=[act, act, late, late, vec, vec],
        out_shape=[jax.ShapeDtypeStruct((N_CHIP, S, FF_C), BF16), jax.ShapeDtypeStruct((N_CHIP, S, FF_C), BF16),
                   jax.ShapeDtypeStruct((S, D), F32), jax.ShapeDtypeStruct((S, D), BF16),
                   jax.ShapeDtypeStruct((1, D), F32), jax.ShapeDtypeStruct((1, D), F32)],
        scratch_shapes=[pltpu.VMEM((S, D), F32)],
        compiler_params=_params("arbitrary", "arbitrary"),
    )(df, gt, up, wg, wu, wd, dy, x2, mix, g2, g3)


def _ffn_bwd_w(a, df, h3, dgt, dup):
    tm = 1024
    assert S // tm == 2

    def body(a_ref, df_ref, h_ref, dgt_ref, dup_ref, dwd_ref, dwg_ref, dwu_ref, acc_d, acc_g, acc_u):
        i = pl.program_id(1)
        h = h_ref[...]
        parts = (_tn(a_ref[...], df_ref[...]), _tn(dgt_ref[...], h), _tn(dup_ref[...], h))

        @pl.when(i == 0)
        def _():
            for acc, part in zip((acc_d, acc_g, acc_u), parts):
                acc[...] = part

        @pl.when(i == S // tm - 1)
        def _():
            for out, acc, part in zip((dwd_ref, dwg_ref, dwu_ref), (acc_d, acc_g, acc_u), parts):
                out[...] = (acc[...] + part).astype(BF16)

    act = pl.BlockSpec((None, tm, FF_C), lambda k, i: (k, i, 0))
    row = pl.BlockSpec((tm, D), lambda k, i: (i, 0))
    wrow = pl.BlockSpec((None, FF_C, D), lambda k, i: (k, 0, 0))
    return pl.pallas_call(
        body, grid=(N_CHIP, S // tm), name="ffn_bwd_w",
        in_specs=[act, row, row, act, act],
        out_specs=[wrow, wrow, wrow],
        out_shape=[jax.ShapeDtypeStruct((N_CHIP, FF_C, D), BF16)] * 3,
        scratch_shapes=[pltpu.VMEM((FF_C, D), F32)] * 3,
        compiler_params=_params("parallel", "arbitrary"),
    )(a, df, h3, dgt, dup)


def _mix_bwd(dmix, cat_r, cat_a, wout, exchange, exchange_args):
    tm = 1024

    def body(dm_ref, cr_ref, ca_ref, w_ref, dret_ref, datt_ref, dw_ref, acc, xc):
        i = pl.program_id(0)

        @pl.when(i == 0)
        def _():
            xc.start()
            acc[...] = jnp.zeros_like(acc)

        dm = dm_ref[...]
        dret_ref[...] = _nt(dm, w_ref[0:512, :])
        datt = _nt(dm, w_ref[512:1024, :])
        for j in range(4):
            datt_ref[j] = datt[:, 128 * j:128 * j + 128]
        acc[0:512, :] += _tn(cr_ref[...], dm)
        acc[512:1024, :] += _tn(ca_ref[...], dm)

        @pl.when(i == S // tm - 1)
        def _():
            dw_ref[...] = acc[...].astype(BF16)
            xc.middle()
            xc.finish()

    row = lambda w: pl.BlockSpec((tm, w), lambda i: (i, 0))
    full = pl.BlockSpec((D, D), lambda i: (0, 0))
    return _carry("mix_bwd", body, exchange, exchange_args, (dmix, cat_r, cat_a, wout),
                  [row(D), row(512), row(512), full],
                  [row(512), pl.BlockSpec((4, tm, 128), lambda i: (0, i, 0)), full],
                  [jax.ShapeDtypeStruct((S, 512), F32), jax.ShapeDtypeStruct((4, S, 128), F32),
                   jax.ShapeDtypeStruct((D, D), BF16)],
                  scratch_shapes=[pltpu.VMEM((D, D), F32)], grid=(S // tm,), semantics=("arbitrary",))


def _att_bwd(aq, ak, av, datt, att_out, lse, exchange, exchange_args, after=None):
    def body(q_ref, k_ref, v_ref, do_ref, out_ref, l_ref, dq_ref, dk_ref, dv_ref, xc):
        xc.start()

        lane_head = lax.broadcasted_iota(jnp.int32, (ATT_BLK, 256), 1) // 64
        for pi, d in enumerate(PATTERN_DILATIONS):
            nb, has_prev = _att_blocks(d)
            assert pi > 0 or not has_prev
            bias_rest, bias_first = _att_bias(has_prev)

            def block(b, carry, pi=pi, d=d, nb=nb, has_prev=has_prev, bias_rest=bias_rest, bias_first=bias_first):
                r, ib = b // nb, b % nb
                rows = _class_rows(ib, r, d)
                prow = _class_rows(jnp.maximum(ib - 1, 0), r, d)
                bias = jnp.where(ib == 0, bias_first, bias_rest) if has_prev else bias_first
                for g in range(2):
                    qg = _slab_pair(q_ref, g, rows).astype(BF16)
                    kg = _slab_pair(k_ref, g, rows)
                    vg = _slab_pair(v_ref, g, rows)
                    if has_prev:
                        kg = jnp.concatenate([_slab_pair(k_ref, g, prow), kg], axis=0)
                        vg = jnp.concatenate([_slab_pair(v_ref, g, prow), vg], axis=0)
                    kg, vg = kg.astype(BF16), vg.astype(BF16)
                    dog = _slab_pair(do_ref, g, rows)
                    outg = _slab_pair(out_ref, g, rows)
                    lg = _slab_pair(l_ref, g, rows)
                    qs = _stack_heads(qg, lane_head)
                    dos = _stack_heads(dog, lane_head)
                    delta = jnp.sum(dos * jnp.concatenate([outg] * 4, axis=0), axis=-1, keepdims=True)
                    lh = jnp.max(_stack_heads(lg, lane_head, NEG), axis=-1, keepdims=True)
                    s = _nt(qs, kg) * ATT_SCALE + bias
                    p = jnp.exp(s - lh)
                    dosb = dos.astype(BF16)
                    ds = (p * (_nt(dosb, vg) - delta) * ATT_SCALE).astype(BF16)
                    dq = _unstack_heads(_nn(ds, kg), lane_head)
                    dk = _tn(ds, qs)
                    dv = _tn(p.astype(BF16), dosb)
                    for jj in range(2):
                        j, sl = 2 * g + jj, slice(128 * jj, 128 * jj + 128)
                        if pi == 0:
                            dq_ref[j, rows, :] = dq[:, sl]
                            dk_ref[j, rows, :] = dk[:, sl]
                            dv_ref[j, rows, :] = dv[:, sl]
                            continue
                        dq_ref[j, rows, :] += dq[:, sl]
                        if has_prev:
                            dk_ref[j, prow, :] += dk[0:ATT_BLK, sl]
                            dv_ref[j, prow, :] += dv[0:ATT_BLK, sl]
                            dk_ref[j, rows, :] += dk[ATT_BLK:2 * ATT_BLK, sl]
                            dv_ref[j, rows, :] += dv[ATT_BLK:2 * ATT_BLK, sl]
                        else:
                            dk_ref[j, rows, :] += dk[:, sl]
                            dv_ref[j, rows, :] += dv[:, sl]
                return carry

            lax.fori_loop(0, S // ATT_BLK, block, 0, unroll=4)
        xc.middle()
        xc.finish()

    slab = jax.ShapeDtypeStruct((4, S, 128), F32)
    return _carry("att_bwd", body, exchange, exchange_args, (aq, ak, av, datt, att_out, lse), [VMEM] * 6, [VMEM] * 3,
                  [slab, slab, slab], after=after)


def _ret_bwd(qr, kr, rv, proj, o_raw, states, dret, tabs, exchange, exchange_args, after=None):
    C, G = RET_C, RET_PER_STEP
    steps = S // (C * G)
    dtab, a_tab, b_tab, lam, bd = tabs

    def body(q_ref, k_ref, v_ref, g_ref, o_ref, st_ref, dr_ref, dt_ref, a_ref, b_ref, lam_ref, bd_ref,
             dq_ref, dk_ref, dv_ref, dg_ref, dR, exch):
        @pl.when(pl.program_id(0) == 0)
        def _():
            exch.start()
            dR[...] = jnp.zeros_like(dR)

        lane_head = lax.broadcasted_iota(jnp.int32, (C, 256), 1) // 32
        col_head = lax.broadcasted_iota(jnp.int32, (C, 256), 1) // 64
        for s in reversed(range(G)):
            rows = slice(s * C, (s + 1) * C)
            q, k, v = q_ref[rows, :], k_ref[rows, :], v_ref[rows, :]
            dos = []
            for j in range(4):
                sl = slice(128 * j, 128 * j + 128)
                oj = o_ref[rows, sl]
                xc = oj - _seg_mean(oj)
                rs = lax.rsqrt(_seg_mean(xc * xc) + GN_EPS)
                rn = xc * rs
                gj = g_ref[rows, sl]
                sg = _sigmoid(gj)
                dret = dr_ref[rows, sl]
                dg_ref[rows, sl] = dret * rn * (sg * (1.0 + gj * (1.0 - sg)))
                drn = dret * (gj * sg)
                dos.append(rs * (drn - _seg_mean(drn) - rn * _seg_mean(drn * rn)))
            do = [jnp.concatenate(dos[0:2], axis=1), jnp.concatenate(dos[2:4], axis=1)]
            do8 = jnp.concatenate(do, axis=1).astype(BF16)
            drb = dR[...].astype(BF16)
            rb = st_ref[s]
            dq = _nt(do8, rb) * a_ref[...]
            dk = _nt(v, drb) * b_ref[...]
            kb = (k.astype(F32) * b_ref[...]).astype(BF16)
            dvall = _nn(kb, drb)
            qs = _stack_heads(q, lane_head, n=8)
            dec = dt_ref[...]
            p = (_nt(qs, k) * dec).astype(BF16)
            dos = [_stack_heads(do[g], col_head).astype(BF16) for g in range(2)]
            dp = jnp.concatenate([_nt(dos[g], v[:, 256 * g:256 * g + 256]) for g in range(2)], axis=0)
            ds = (dp * dec).astype(BF16)
            dq = dq + _unstack_heads(_nn(ds, k), lane_head, n=8)
            dk = dk + _tn(ds, qs)
            dv = [dvall[:, 256 * g:256 * g + 256] + _tn(p[4 * C * g:4 * C * (g + 1)], dos[g]) for g in range(2)]
            qa = (q.astype(F32) * a_ref[...]).astype(BF16)
            dR[...] = dR[...] * lam_ref[...] + _tn(qa, do8) * bd_ref[...]
            dq_ref[rows, :] = dq
            dk_ref[rows, :] = dk
            dv_ref[rows, 0:256] = dv[0]
            dv_ref[rows, 256:512] = dv[1]

        @pl.when(pl.program_id(0) == steps - 1)
        def _():
            exch.middle()
            exch.finish()

    rev = lambda w: pl.BlockSpec((C * G, w), lambda n: (steps - 1 - n, 0))
    full = lambda a: pl.BlockSpec(a.shape, lambda n: (0,) * a.ndim)
    return _carry(
        "ret_bwd", body, exchange, exchange_args, (qr, kr, rv, proj, o_raw, states, dret, dtab, a_tab, b_tab, lam, bd),
        [rev(256), rev(256), rev(512), rev(512), rev(512),
         pl.BlockSpec((G, 256, 512), lambda n: (steps - 1 - n, 0, 0)), rev(512),
         full(dtab), full(a_tab), full(b_tab), full(lam), full(bd)],
        [rev(256), rev(256), rev(512), rev(512)],
        [jax.ShapeDtypeStruct((S, 256), F32), jax.ShapeDtypeStruct((S, 256), F32),
         jax.ShapeDtypeStruct((S, 512), F32), jax.ShapeDtypeStruct((S, 512), F32)],
        scratch_shapes=[pltpu.VMEM((256, 512), F32)], grid=(steps,), semantics=("arbitrary",), after=after)


def _rot_bwd(cos, sin, spread, dqr, dkr, drv, drg, dq_att, dk_att, dv_att):
    tm = 256

    def body(cos_ref, sin_ref, e_ref, dqr_ref, dkr_ref, drv_ref, drg_ref, dqa_ref, dka_ref, dva_ref, dp_ref):
        cr, ca, sr, sa = _rot_tables(cos_ref, sin_ref, e_ref)
        lo_r, lo_a = _rot_halves(tm)

        def unrot_r(g):
            gs = g * sr
            return g * cr + pltpu.roll(jnp.where(lo_r, -gs, 0.0), 16, 1) + pltpu.roll(jnp.where(lo_r, 0.0, gs), 240, 1)

        def unrot_a(g):
            gs = g * sa
            return g * ca + pltpu.roll(jnp.where(lo_a, -gs, 0.0), 8, 1) + pltpu.roll(jnp.where(lo_a, 0.0, gs), 504, 1)

        def wide(ref):
            return jnp.concatenate([ref[j] for j in range(4)], axis=1)

        dp_ref[:, 0:256] = unrot_r(dqr_ref[...]).astype(BF16)
        dp_ref[:, 256:512] = unrot_r(dkr_ref[...] * RET_SCALE).astype(BF16)
        dp_ref[:, 512:1024] = drv_ref[...].astype(BF16)
        dp_ref[:, 1024:1536] = drg_ref[...].astype(BF16)
        dp_ref[:, 1536:2048] = unrot_a(wide(dqa_ref)).astype(BF16)
        dp_ref[:, 2048:2560] = unrot_a(wide(dka_ref)).astype(BF16)
        dp_ref[:, 2560:3072] = wide(dva_ref).astype(BF16)

    row = lambda w: pl.BlockSpec((tm, w), lambda i: (i, 0))
    slab = pl.BlockSpec((4, tm, 128), lambda i: (0, i, 0))
    return pl.pallas_call(
        body, grid=(S // tm,), name="rot_bwd",
        in_specs=[row(128), row(128), pl.BlockSpec((128, 768), lambda i: (0, 0)),
                  row(256), row(256), row(512), row(512), slab, slab, slab],
        out_specs=row(PW), out_shape=jax.ShapeDtypeStruct((S, PW), BF16),
        compiler_params=_params("parallel"),
    )(cos, sin, spread, dqr, dkr, drv, drg, dq_att, dk_att, dv_att)


def _win_bwd_w(h1, dproj, exchange, exchange_args):
    def body(h_ref, dp_ref, dw_ref, xc):
        k = pl.program_id(0)

        @pl.when(k == 0)
        def _():
            xc.start()

        dw_ref[...] = _tn(h_ref[...], dp_ref[...]).astype(BF16)

        @pl.when(k == N_CHIP - 1)
        def _():
            xc.middle()
            xc.finish()

    (dw,), out = _carry(
        "win_bwd_w", body, exchange, exchange_args, (h1, dproj),
        [pl.BlockSpec((S, D), lambda k: (0, 0)), pl.BlockSpec((S, WIN_C), lambda k: (0, k))],
        [pl.BlockSpec((None, D, WIN_C), lambda k: (k, 0, 0))],
        [jax.ShapeDtypeStruct((N_CHIP, D, WIN_C), BF16)], grid=(N_CHIP,), semantics=("arbitrary",))
    return dw, out


def _in_bwd(dproj, win_g, x, dx2, g1, other_rows, after):
    tm = 512
    n = len(other_rows)

    def body(dp_ref, w_ref, x_ref, dx2_ref, g_ref, *refs):
        rows, dx_ref, blk_ref = refs[:n], refs[n], refs[n + 1]

        @pl.when(pl.program_id(0) == 0)
        def _():
            blk_ref[...] = jnp.zeros_like(blk_ref)
            for i, r_ref in enumerate(rows):
                blk_ref[i + 1:i + 2, :] = r_ref[...]

        dh = _nt(dp_ref[:, 0:WIN_C], w_ref[0])
        for k in range(1, N_CHIP):
            dh = dh + _nt(dp_ref[:, k * WIN_C:(k + 1) * WIN_C], w_ref[k])
        xv = x_ref[...]
        r = _rstd(xv)
        xn = xv * r
        blk_ref[0:1, :] = blk_ref[0:1, :] + jnp.sum(dh * xn, axis=0, keepdims=True)
        t = dh * g_ref[...]
        dx_ref[...] = dx2_ref[...] + r * (t - xn * jnp.mean(t * xn, axis=-1, keepdims=True))

    row = lambda w: pl.BlockSpec((tm, w), lambda i: (i, 0))
    vec = pl.BlockSpec((1, D), lambda i: (0, 0))
    return _carry("in_bwd", body, _NoExchange(), (), (dproj, win_g, x, dx2, g1, *other_rows),
                  [row(PW), pl.BlockSpec((N_CHIP, D, WIN_C), lambda i: (0, 0, 0)), row(D), row(D), vec] + [vec] * n,
                  [row(D), pl.BlockSpec((8, D), lambda i: (0, 0))],
                  [jax.ShapeDtypeStruct((S, D), F32), jax.ShapeDtypeStruct((8, D), F32)],
                  grid=(S // tm,), semantics=("arbitrary",), after=after)[0]


ANY = pl.BlockSpec(memory_space=pl.ANY)
VMEM = pl.BlockSpec(memory_space=pltpu.VMEM)
FLIPS = ((1, 0), (0, 1), (1, 1))


def _place():
    x, y, c = lax.axis_index("x"), lax.axis_index("y"), lax.axis_index("c")
    chips = [((1 - x) if fx else x, (1 - y) if fy else y) for fx, fy in FLIPS]
    return x, y, c, 2 * x + y, chips


def _remote(src, dst, send_sem, recv_sem, device):
    return pltpu.make_async_remote_copy(src_ref=src, dst_ref=dst, send_sem=send_sem, recv_sem=recv_sem,
                                        device_id=device, device_id_type=MESH)


class _Exchange:
    aliases = {}

    def middle(self, ins, outs, sems):
        pass


class _GatherShards(_Exchange):
    def __init__(self, shards):
        n = self.n = len(shards)
        self.n_in = self.n_out = n
        self.out_shape = [jax.ShapeDtypeStruct((N_CHIP,) + s.shape, s.dtype) for s in shards]
        dma = pltpu.SemaphoreType.DMA
        self.scratch = [dma((3 * n,)), dma((3 * n,)), dma((3 * n,)), dma((3 * n,)), dma((n,)), dma((n,))]

    def _ici(self, ins, outs, sems, a, j, chip):
        x, y, c, me, chips = _place()
        half = ins[a].shape[0] // 2
        return _remote(ins[a].at[pl.ds(c * half, half), :], outs[a].at[me, pl.ds(c * half, half), :],
                       sems[0].at[3 * a + j], sems[1].at[3 * a + j], (*chip, c))

    def _fwd(self, outs, sems, a, j, chip, half_of):
        x, y, c, me, chips = _place()
        half = outs[a].shape[1] // 2
        blk = outs[a].at[2 * chip[0] + chip[1], pl.ds(half_of * half, half), :]
        return _remote(blk, blk, sems[2].at[3 * a + j], sems[3].at[3 * a + j], (x, y, 1 - c))

    def _own(self, ins, outs, sems, a):
        return _own_shard_to_sibling(ins[a], outs[a], sems[4].at[a], sems[5].at[a])

    def start(self, ins, outs, sems):
        chips = _place()[4]
        for a in range(self.n):
            for j, chip in enumerate(chips):
                self._ici(ins, outs, sems, a, j, chip).start()
        for a in range(self.n):
            self._own(ins, outs, sems, a).start()

    def middle(self, ins, outs, sems):
        x, y, c, me, chips = _place()
        for a in range(self.n):
            for j, chip in enumerate(chips):
                half = outs[a].shape[1] // 2
                blk = outs[a].at[2 * chip[0] + chip[1], pl.ds(c * half, half), :]
                _remote(blk, blk, sems[0].at[3 * a + j], sems[1].at[3 * a + j], (x, y, c)).wait_recv()
                self._fwd(outs, sems, a, j, chip, c).start()

    def finish(self, ins, outs, sems):
        x, y, c, me, chips = _place()
        for a in range(self.n):
            for j, chip in enumerate(chips):
                self._fwd(outs, sems, a, j, chip, 1 - c).wait_recv()
        for a in range(self.n):
            for j, chip in enumerate(chips):
                self._ici(ins, outs, sems, a, j, chip).wait_send()
                self._fwd(outs, sems, a, j, chip, c).wait_send()
            self._own(ins, outs, sems, a).wait()


def _own_shard_to_sibling(shard_ref, gathered_ref, send_sem, recv_sem):
    x, y, c, me, chips = _place()
    return _remote(shard_ref, gathered_ref.at[me], send_sem, recv_sem, (x, y, 1 - c))


class _NoExchange(_Exchange):
    n_in = n_out = 0
    out_shape = ()
    scratch = ()

    def start(self, ins, outs, sems):
        pass

    def finish(self, ins, outs, sems):
        pass


class _ForwardGathered(_Exchange):
    def __init__(self, shards, own=True, forward=True):
        self.own, self.forward = own, forward
        n = self.n = len(shards)
        self.n_in, self.n_out = 2 * n, n
        self.out_shape = [jax.ShapeDtypeStruct((N_CHIP,) + s.shape, s.dtype) for s in shards]
        dma = pltpu.SemaphoreType.DMA
        self.scratch = [dma((3 * n,)), dma((3 * n,)), dma((n,)), dma((n,))]
        self.aliases = {n + a: a for a in range(n)}

    def _fwd(self, outs, sems, a, j, chip, half_of):
        x, y, c, me, chips = _place()
        half = outs[a].shape[1] // 2
        blk = outs[a].at[2 * chip[0] + chip[1], pl.ds(half_of * half, half), :]
        return _remote(blk, blk, sems[0].at[3 * a + j], sems[1].at[3 * a + j], (x, y, 1 - c))

    def _own(self, ins, outs, sems, a):
        return _own_shard_to_sibling(ins[a], outs[a], sems[2].at[a], sems[3].at[a])

    def start(self, ins, outs, sems):
        x, y, c, me, chips = _place()
        for a in range(self.n):
            for j, chip in enumerate(chips if self.forward else ()):
                self._fwd(outs, sems, a, j, chip, c).start()
        for a in range(self.n if self.own else 0):
            self._own(ins, outs, sems, a).start()

    def finish(self, ins, outs, sems):
        x, y, c, me, chips = _place()
        for a in range(self.n):
            for j, chip in enumerate(chips if self.forward else ()):
                self._fwd(outs, sems, a, j, chip, 1 - c).wait_recv()
        for a in range(self.n):
            for j, chip in enumerate(chips if self.forward else ()):
                self._fwd(outs, sems, a, j, chip, c).wait_send()
            if self.own:
                self._own(ins, outs, sems, a).wait()


HBM = pl.BlockSpec(memory_space=pltpu.HBM)
SEMS = pl.BlockSpec(memory_space=pltpu.SEMAPHORE)
DATAFLOW = pltpu.SideEffectType.DATAFLOW_SIDE_EFFECTING


class _OverIci:
    def __init__(self, name, sources, lands):
        self.name, self.n = name, len(sources)
        hbm = lambda t: pltpu.with_memory_space_constraint(t, pltpu.HBM)
        self.arrays = [hbm(t) for t in sources] + [hbm(t) for t in lands]

    def sent(self, src, land, a, chip):
        raise NotImplementedError

    def landed(self, land, a, chip):
        raise NotImplementedError

    def _copy(self, arr, sems, a, j, receiving):
        x, y, c, me, chips = _place()
        src, dst = self.sent(arr[a], arr[self.n + a], a, chips[j])
        if receiving:
            dst = self.landed(arr[self.n + a], a, chips[j])
        return _remote(src, dst, sems[0].at[3 * a + j], sems[1].at[3 * a + j], (*chips[j], c))

    def start(self, after):
        m = len(self.arrays)

        def body(*refs):
            arr, sems, token = refs[:m], refs[m + 1:m + 3], refs[-1]
            for a in range(self.n):
                for j in range(3):
                    self._copy(arr, sems, a, j, False).start()
            token[...] = jnp.zeros_like(token)

        dma = pltpu.SemaphoreType.DMA
        outs = pl.pallas_call(
            body, name=self.name + "_start",
            out_shape=[dma((3 * self.n,)), dma((3 * self.n,))] + [pltpu.HBM(t.shape, t.dtype) for t in self.arrays]
                      + [jax.ShapeDtypeStruct((8, 128), F32)],
            in_specs=[HBM] * m + [ANY], out_specs=[SEMS, SEMS] + [HBM] * m + [VMEM],
            input_output_aliases={i: 2 + i for i in range(m)},
            compiler_params=pltpu.CompilerParams(has_side_effects=DATAFLOW),
        )(*self.arrays, after)
        self.sems, self.arrays = outs[0:2], list(outs[2:2 + m])
        return outs[-1]

    def wait(self, after):
        m = len(self.arrays)

        def body(*refs):
            arr, sems = refs[:m], refs[m:m + 2]
            for a in range(self.n):
                for j in range(3):
                    self._copy(arr, sems, a, j, False).wait_send()
                    self._copy(arr, sems, a, j, True).wait_recv()

        outs = pl.pallas_call(
            body, name=self.name + "_wait",
            out_shape=[pltpu.HBM(t.shape, t.dtype) for t in self.arrays],
            in_specs=[HBM] * m + [SEMS, SEMS, ANY], out_specs=[HBM] * m,
            input_output_aliases={i: i for i in range(m)},
            compiler_params=pltpu.CompilerParams(has_side_effects=DATAFLOW),
        )(*self.arrays, *self.sems, after)
        return list(outs[:self.n]), list(outs[self.n:])


class _GatherOverIci(_OverIci):
    def __init__(self, name, shards):
        super().__init__(name, shards, [lax.empty((N_CHIP,) + s.shape, s.dtype) for s in shards])

    @staticmethod
    def _half(ref):
        c = lax.axis_index("c")
        half = ref.shape[-2] // 2
        return pl.ds(c * half, half)

    def sent(self, src, land, a, chip):
        return src.at[self._half(src), :], land.at[_place()[3], self._half(src), :]

    def landed(self, land, a, chip):
        return land.at[2 * chip[0] + chip[1], self._half(land), :]


class _SumOverIci(_OverIci):
    def __init__(self, name, pre):
        super().__init__(name, pre, [lax.empty(p.shape, p.dtype) for p in pre])

    def sent(self, src, land, a, chip):
        return src.at[2 * chip[0] + chip[1]], land.at[_place()[3]]

    def landed(self, land, a, chip):
        return land.at[2 * chip[0] + chip[1]]


class _HalvesToSibling(_Exchange):
    def __init__(self, grads):
        n = self.n = len(grads)
        self.n_in = self.n_out = n
        self.out_shape = [jax.ShapeDtypeStruct((N_CHIP, g.shape[1] // 2, g.shape[2]), g.dtype) for g in grads]
        self.scratch = [pltpu.SemaphoreType.DMA((n,)), pltpu.SemaphoreType.DMA((n,))]

    def _copy(self, ins, outs, sems, a):
        x, y, c, me, chips = _place()
        half = ins[a].shape[1] // 2
        return _remote(ins[a].at[:, pl.ds((1 - c) * half, half), :], outs[a], sems[0].at[a], sems[1].at[a], (x, y, 1 - c))

    def start(self, ins, outs, sems):
        for a in range(self.n):
            self._copy(ins, outs, sems, a).start()

    def finish(self, ins, outs, sems):
        for a in range(self.n):
            self._copy(ins, outs, sems, a).wait_recv()
        for a in range(self.n):
            self._copy(ins, outs, sems, a).wait_send()


class _ShareHalves(_Exchange):
    def __init__(self, fulls):
        n = self.n = len(fulls)
        self.n_in = self.n_out = n
        self.out_shape = [jax.ShapeDtypeStruct(f.shape, f.dtype) for f in fulls]
        self.scratch = [pltpu.SemaphoreType.DMA((n,)), pltpu.SemaphoreType.DMA((n,))]
        self.aliases = {a: a for a in range(n)}

    def _copy(self, outs, sems, a, half_of):
        x, y, c, me, chips = _place()
        half = outs[a].shape[0] // 2
        rows = outs[a].at[pl.ds(half_of * half, half), :]
        return _remote(rows, rows, sems[0].at[a], sems[1].at[a], (x, y, 1 - c))

    def start(self, ins, outs, sems):
        c = _place()[2]
        for a in range(self.n):
            self._copy(outs, sems, a, c).start()

    def finish(self, ins, outs, sems):
        c = _place()[2]
        for a in range(self.n):
            self._copy(outs, sems, a, 1 - c).wait_recv()
        for a in range(self.n):
            self._copy(outs, sems, a, c).wait_send()


class _GatherBlocks(_Exchange):
    def __init__(self, block):
        self.n_in = self.n_out = 1
        self.out_shape = [jax.ShapeDtypeStruct((8,) + block.shape, block.dtype)]
        dma = pltpu.SemaphoreType.DMA
        self.scratch = [dma((7,)), dma((7,)), dma]

    @staticmethod
    def _peer(f):
        x, y, c, me, chips = _place()
        return ((1 - x) if f & 4 else x, (1 - y) if f & 2 else y, (1 - c) if f & 1 else c)

    def start(self, ins, outs, sems):
        x, y, c, me, chips = _place()
        for f in range(1, 8):
            _remote(ins[0], outs[0].at[2 * me + c], sems[0].at[f - 1], sems[1].at[f - 1], self._peer(f)).start()
        pltpu.make_async_copy(ins[0], outs[0].at[2 * me + c], sems[2]).start()

    def finish(self, ins, outs, sems):
        x, y, c, me, chips = _place()
        for f in range(1, 8):
            px, py, pc = self._peer(f)
            blk = outs[0].at[4 * px + 2 * py + pc]
            _remote(blk, blk, sems[0].at[f - 1], sems[1].at[f - 1], (x, y, c)).wait_recv()
        for f in range(1, 8):
            _remote(ins[0], outs[0].at[2 * me + c], sems[0].at[f - 1], sems[1].at[f - 1], self._peer(f)).wait_send()
        pltpu.make_async_copy(ins[0], outs[0].at[2 * me + c], sems[2]).wait()


class _Both(_Exchange):
    def __init__(self, first, second):
        self.parts = (first, second)
        self.n_in, self.n_out = first.n_in + second.n_in, first.n_out + second.n_out
        self.out_shape = first.out_shape + second.out_shape
        self.scratch = first.scratch + second.scratch
        self.aliases = dict(first.aliases)
        self.aliases.update({first.n_in + i: first.n_out + o for i, o in second.aliases.items()})

    def _split(self, ins, outs, sems):
        a, b = self.parts
        return ((a, ins[:a.n_in], outs[:a.n_out], sems[:len(a.scratch)]),
                (b, ins[a.n_in:], outs[a.n_out:], sems[len(a.scratch):]))

    def start(self, ins, outs, sems):
        for ex, i, o, s in self._split(ins, outs, sems):
            ex.start(i, o, s)

    def middle(self, ins, outs, sems):
        for ex, i, o, s in self._split(ins, outs, sems):
            ex.middle(i, o, s)

    def finish(self, ins, outs, sems):
        for ex, i, o, s in self._split(ins, outs, sems):
            ex.finish(i, o, s)


class _Bound:
    def __init__(self, ex, ins, outs, sems):
        self.start = lambda: ex.start(ins, outs, sems)
        self.middle = lambda: ex.middle(ins, outs, sems)
        self.finish = lambda: ex.finish(ins, outs, sems)


def _carry(name, body, ex, ex_args, args, in_specs, out_specs, out_shape, scratch_shapes=(), grid=None, semantics=(),
           after=None):
    n_a, n_o, n_s = len(args), len(out_shape), len(scratch_shapes)
    behind = [] if after is None else [after]

    def full_body(*refs):
        p = 0
        groups = []
        for size in (n_a, ex.n_in, len(behind), n_o, ex.n_out, n_s, len(ex.scratch)):
            groups.append(refs[p:p + size])
            p += size
        a, ei, _, o, eo, s, es = groups
        body(*a, *o, *s, _Bound(ex, ei, eo, es))

    kwargs = {} if grid is None else {"grid": grid}
    outs = pl.pallas_call(
        full_body, name=name,
        in_specs=list(in_specs) + [ANY] * (ex.n_in + len(behind)), out_specs=list(out_specs) + [ANY] * ex.n_out,
        out_shape=list(out_shape) + list(ex.out_shape), scratch_shapes=list(scratch_shapes) + list(ex.scratch),
        input_output_aliases={n_a + i: n_o + o for i, o in ex.aliases.items()},
        compiler_params=_params(*semantics) if semantics else pltpu.CompilerParams(vmem_limit_bytes=VMEM_LIMIT),
        **kwargs,
    )(*args, *ex_args, *behind)
    return outs[:n_o], outs[n_o:]


def _cast_bf16(arrays, after=None):
    n = len(arrays)
    behind = [] if after is None else [after]

    def body(*refs):
        for a in range(n):
            refs[len(refs) - n + a][...] = refs[a][...].astype(BF16)

    blks = [pl.BlockSpec((t.shape[0] // 4, t.shape[1]), lambda i: (i, 0)) for t in arrays]
    return pl.pallas_call(
        body, grid=(4,), name="cast_bf16", in_specs=blks + [ANY] * len(behind), out_specs=blks,
        out_shape=[jax.ShapeDtypeStruct(t.shape, BF16) for t in arrays], compiler_params=_params("parallel"),
    )(*arrays, *behind)


def _prepare(x, g1, pos, ifc, after):
    tm = 512

    def body(x_ref, g_ref, pos_ref, ifc_ref, h_ref, cos_ref, sin_ref, _):
        xv = x_ref[...]
        h_ref[...] = (xv * _rstd(xv) * g_ref[...]).astype(BF16)
        ang = pos_ref[...].astype(F32) * ifc_ref[...]
        cos_ref[...] = jnp.cos(ang)
        sin_ref[...] = jnp.sin(ang)

    row = lambda w: pl.BlockSpec((tm, w), lambda i: (i, 0))
    const = lambda w: pl.BlockSpec((1, w), lambda i: (0, 0))
    return _carry("prepare", body, _NoExchange(), (), (x, g1, pos, ifc),
                  [row(D), const(D), row(1), const(128)], [row(D), row(128), row(128)],
                  [jax.ShapeDtypeStruct((S, D), BF16)] + [jax.ShapeDtypeStruct((S, 128), F32)] * 2,
                  grid=(S // tm,), semantics=("parallel",), after=after)[0]


def _exchange_alone(name, ex, ex_args):
    def body(xc):
        xc.start()
        xc.middle()
        xc.finish()

    return _carry(name, body, ex, ex_args, (), (), (), ())[1]


def _core_index():
    return lax.axis_index("c").astype(jnp.int32).reshape(1)


def _pair_sum(gs, gots):
    n = len(gs)

    def body(c_ref, *refs):
        for a in range(n):
            refs[2 * n + a][...] = (refs[a][...].astype(F32) + refs[n + a][...].astype(F32)).astype(BF16)

    mine = [pl.BlockSpec((None, g.shape[1] // 2, g.shape[2]), lambda k, c_ref: (k, c_ref[0], 0)) for g in gs]
    blk = [pl.BlockSpec((None, g.shape[1] // 2, g.shape[2]), lambda k, c_ref: (k, 0, 0)) for g in gs]
    return pl.pallas_call(
        body, name=f"pair_sum_{gs[0].shape[1]}x{gs[0].shape[2]}",
        grid_spec=pltpu.PrefetchScalarGridSpec(
            num_scalar_prefetch=1, grid=(N_CHIP,), in_specs=mine + blk, out_specs=blk),
        out_shape=[jax.ShapeDtypeStruct((N_CHIP, g.shape[1] // 2, g.shape[2]), BF16) for g in gs],
        compiler_params=_params("parallel"),
    )(_core_index(), *gs, *gots)


def _chip_sum(pre, parts):
    n = len(parts)
    me = 2 * lax.axis_index("x") + lax.axis_index("y")
    others = [k + (k >= me).astype(jnp.int32) for k in range(3)]
    where = jnp.stack([lax.axis_index("c"), me, *others]).astype(jnp.int32)

    def body(w_ref, *refs):
        for a in range(n):
            own, p1, p2, p3 = refs[4 * a:4 * a + 4]
            refs[4 * n + a][...] = ((own[...].astype(F32) + p1[...].astype(F32)) + p2[...].astype(F32)) + p3[...].astype(F32)

    in_specs, out_specs, operands = [], [], []
    for a in range(n):
        _, half, cc = parts[a].shape
        tr = half // 2
        in_specs += [pl.BlockSpec((None, tr, cc), lambda i, w_ref, s=s: (w_ref[s], i, 0)) for s in (1, 2, 3, 4)]
        out_specs.append(pl.BlockSpec((tr, cc), lambda i, w_ref: (2 * w_ref[0] + i, 0)))
        operands += [pre[a], parts[a], parts[a], parts[a]]
    return pl.pallas_call(
        body, name=f"chip_sum_{parts[0].shape[1]}x{parts[0].shape[2]}",
        grid_spec=pltpu.PrefetchScalarGridSpec(num_scalar_prefetch=1, grid=(2,), in_specs=in_specs, out_specs=out_specs),
        out_shape=[jax.ShapeDtypeStruct((2 * p.shape[1], p.shape[2]), F32) for p in parts],
        compiler_params=_params("parallel"),
    )(where, *operands)


def _adamw_math(w, g, m, v):
    m = ADAM_B1 * m + (1.0 - ADAM_B1) * g
    v = ADAM_B2 * v + (1.0 - ADAM_B2) * (g * g)
    m_hat = m / (1.0 - ADAM_B1 ** ADAM_STEP)
    v_hat = v / (1.0 - ADAM_B2 ** ADAM_STEP)
    delta = -ADAM_LR * (m_hat / (jnp.sqrt(v_hat) + ADAM_EPS) + ADAM_WD * w)
    return delta, m, v


def _adamw(ws, gs, ms, vs, after=None):
    n = len(ws)

    def body(*refs):
        for a in range(n):
            w_ref, g_ref, m_ref, v_ref = (refs[t * n + a] for t in range(4))
            go_ref, d_ref, nm_ref, nv_ref = refs[4 * n + 4 * a:4 * n + 4 * a + 4]
            g = g_ref[...]
            go_ref[...] = g
            d_ref[...], nm_ref[...], nv_ref[...] = _adamw_math(w_ref[...], g, m_ref[...], v_ref[...])

    blks = [pl.BlockSpec((w.shape[0] // 4, w.shape[1]), lambda i: (i, 0)) for w in ws]
    outs = _carry(f"adamw_{ws[0].shape[0]}x{ws[0].shape[1]}", body, _NoExchange(), (), (*ws, *gs, *ms, *vs),
                  blks * 4, [b for b in blks for _ in range(4)],
                  [jax.ShapeDtypeStruct(w.shape, F32) for w in ws for _ in range(4)],
                  grid=(4,), semantics=("parallel",), after=after)[0]
    return [outs[4 * a:4 * a + 4] for a in range(n)]


def _adamw_gains(gall, ws, ms, vs):
    def body(ga_ref, *refs):
        w, m, v = refs[0:4], refs[4:8], refs[8:12]
        outs, loss_ref, total = refs[12:28], refs[28], refs[29]
        g = ga_ref[0]
        for dev in range(1, 8):
            g = g + ga_ref[dev]
        total[...] = g
        for i in range(4):
            gi = total[i:i + 1, :]
            outs[i][...] = gi
            outs[4 + i][...], outs[8 + i][...], outs[12 + i][...] = _adamw_math(w[i][...], gi, m[i][...], v[i][...])
        loss_ref[...] = total[4:5, 0:128] * (0.5 / D)

    outs = pl.pallas_call(
        body, name="adamw_gains",
        out_shape=[jax.ShapeDtypeStruct((1, D), F32)] * 16 + [jax.ShapeDtypeStruct((1, 128), F32)],
        scratch_shapes=[pltpu.VMEM((8, D), F32)],
    )(gall, *ws, *ms, *vs)
    return outs[0:4], outs[4:8], outs[8:12], outs[12:16], outs[16]


def kernel(x, positions, w_in, w_out, g_pre_mix, g_post_mix, g_pre_ffn, g_post_ffn, w_gate, w_up, w_down, loss_target, m_w_in, m_w_out, m_g_pre_mix, m_g_post_mix, m_g_pre_ffn, m_g_post_ffn, m_w_gate, m_w_up, m_w_down, v_w_in, v_w_out, v_g_pre_mix, v_g_post_mix, v_g_pre_ffn, v_g_post_ffn, v_w_gate, v_w_up, v_w_down):
    tr = lambda t: jnp.swapaxes(t, 1, 2)[0]
    shards = [w_in[0], w_out[0], tr(w_gate), tr(w_up), w_down[0]]
    moms = [m_w_in[0], m_w_out[0], tr(m_w_gate), tr(m_w_up), m_w_down[0]]
    vels = [v_w_in[0], v_w_out[0], tr(v_w_gate), tr(v_w_up), v_w_down[0]]
    xs, pos, tgt = x[0], positions.reshape(S, 1), loss_target[0]
    g1, g2, g3, g4 = g_pre_mix, g_post_mix, g_pre_ffn, g_post_ffn
    tabs = tuple(jnp.asarray(t) for t in _retention_tables())
    ifc, spread = _rotary_tables()
    ifc, spread = jnp.asarray(ifc), jnp.asarray(spread, dtype=BF16)
    bf = list(_cast_bf16(shards[:1]))
    win_gather = _GatherOverIci("win_gather", bf[:1])
    token = win_gather.start(shards[0])
    bf += _cast_bf16(shards[1:], token)
    wout_gather = _GatherOverIci("wout_gather", bf[1:2])
    token = wout_gather.start(token)
    ffn_gather = _GatherOverIci("ffn_gather", bf[2:])
    token = ffn_gather.start(token)
    h1, cos, sin = _prepare(xs, g1, pos, ifc, token)
    win_sh, win_land = win_gather.wait(h1)
    (win_g,) = _exchange_alone("forward_win", _ForwardGathered(bf[:1]), [*win_sh, *win_land])
    qr, kr, rv, rg, aq, ak, av = _proj_fwd(h1, win_g, cos, sin, spread, None)
    wout_sh, wout_land = wout_gather.wait(qr)
    n_ffn = len(bf[2:])
    (att_out, lse, cat_a), (wout_g, *ffn_gather.arrays[n_ffn:]) = _att_fwd(
        aq, ak, av, _Both(_ForwardGathered(bf[1:2]), _ForwardGathered(bf[2:], forward=False)),
        [*wout_sh, *wout_land, *ffn_gather.arrays])
    wout_g = wout_g.reshape(D, D)
    (o_raw, cat_r, states), _ = _ret_fwd(qr, kr, rv, rg, tabs, _NoExchange(), (), cat_a)
    ffn_sh, ffn_lands = ffn_gather.wait(cat_r)
    (mix, x2, h3), (wg_g, wu_g, wd_g) = _mix_fwd(cat_r, cat_a, wout_g, xs, g2, g3,
                                                _ForwardGathered(bf[2:], own=False), [*ffn_sh, *ffn_lands])
    gt, up, a, sq, dy, df, dg4 = _ffn_fwd(h3, wg_g, wu_g, wd_g, x2, tgt, g4)

    dgt, dup, dx2, dmix, dg3, dg2 = _ffn_bwd_act(df, gt, up, wg_g, wu_g, wd_g, dy, x2, mix, g2, g3)
    ffn_grads = list(_ffn_bwd_w(a, df, h3, dgt, dup))
    (dret, datt, dwout), got = _mix_bwd(dmix, cat_r, cat_a, wout_g, _HalvesToSibling(ffn_grads), ffn_grads)
    ffn_sum = _SumOverIci("ffn_sum", _pair_sum(ffn_grads, got))
    token = ffn_sum.start(datt)
    (dq_att, dk_att, dv_att), _ = _att_bwd(aq, ak, av, datt, att_out, lse, _NoExchange(), (), token)
    (dqr, dkr, drv, drg), _ = _ret_bwd(qr, kr, rv, rg, o_raw, states, dret, tabs, _NoExchange(), (), token)
    dproj = _rot_bwd(cos, sin, spread, dqr, dkr, drv, drg, dq_att, dk_att, dv_att)
    sums = _chip_sum(*ffn_sum.wait(dproj))
    dwin, ffn_full = _win_bwd_w(h1, dproj, _ShareHalves(sums), sums)
    in_grads = [dwin, dwout.reshape(N_CHIP, WOUT_R, D)]

    got = _exchange_alone("halves_to_sibling", _HalvesToSibling(in_grads), in_grads)
    in_sum = _SumOverIci("in_sum", _pair_sum(in_grads, got))
    token = in_sum.start(dproj)
    dx, gblock = _in_bwd(dproj, win_g, xs, dx2, g1, [dg2, dg3, dg4, sq], token)
    ffn_upd = _adamw(shards[2:], [ffn_full[o] for o in (1, 2, 0)],
                     moms[2:], vels[2:], token)
    pre, parts = in_sum.wait(ffn_upd[2][0])
    sums = _chip_sum(pre, parts)
    *in_full, gall = _exchange_alone("share_rest", _Both(_ShareHalves(sums), _GatherBlocks(gblock)), [*sums, gblock])
    upd = _adamw(shards[:2], in_full, moms[:2], vels[:2]) + ffn_upd
    gg, gd, gm, gv, loss_row = _adamw_gains(gall, [g1, g2, g3, g4],
                                            [m_g_pre_mix, m_g_post_mix, m_g_pre_ffn, m_g_post_ffn],
                                            [v_g_pre_mix, v_g_post_mix, v_g_pre_ffn, v_g_post_ffn])

    def order(mats, vecs):
        back = lambda t: jnp.swapaxes(t[None], 1, 2)
        return [mats[0][None], mats[1][None], *vecs, back(mats[2]), back(mats[3]), mats[4][None]]

    return (loss_row[0, 0], dx[None],
            *order([u[0] for u in upd], gg),
            *order([u[1] for u in upd], gd),
            *order([u[2] for u in upd], gm),
            *order([u[3] for u in upd], gv))
```

```python
import numpy as np
import jax
import jax.numpy as jnp
from jax import lax
from jax.experimental import pallas as pl
from jax.experimental.pallas import tpu as pltpu

F32, BF16 = jnp.float32, jnp.bfloat16
MESH = pl.DeviceIdType.MESH

S = 2048
D = 1024
PW = 3072
N_CHIP = 4
WIN_C = PW // N_CHIP
DFF = 2816
FF_C = DFF // N_CHIP
WOUT_R = D // N_CHIP
RMS_EPS = 1e-6
GN_EPS = 1e-5
RET_C = 128
RET_PER_STEP = 4
RET_SCALE = 32 ** -0.5
ATT_BLK = 128
ATT_SCALE = 64 ** -0.5
PATTERN_DILATIONS = (16, 1, 4)
NEG = -1e30
VMEM_LIMIT = 56 * 1024 * 1024

ADAM_LR, ADAM_B1, ADAM_B2, ADAM_EPS, ADAM_WD, ADAM_STEP = 0.001, 0.9, 0.999, 1e-08, 0.01, 10


def _params(*sem):
    return pltpu.CompilerParams(dimension_semantics=sem, vmem_limit_bytes=VMEM_LIMIT)


def _nt(a, b):
    return lax.dot_general(a, b, (((1,), (1,)), ((), ())), preferred_element_type=F32)


def _tn(a, b):
    return lax.dot_general(a, b, (((0,), (0,)), ((), ())), preferred_element_type=F32)


def _nn(a, b):
    return jnp.dot(a, b, preferred_element_type=F32)


def _rstd(v):
    return lax.rsqrt(jnp.mean(v * v, axis=-1, keepdims=True) + RMS_EPS)


def _sigmoid(v):
    return 1.0 / (1.0 + jnp.exp(-v))


def _rows(i, t):
    return pl.ds(pl.multiple_of(i * t, t), t)


def _retention_tables():
    h = np.arange(8, dtype=np.float32)
    log_g = np.log1p(-np.exp2(-5.0 - h)).astype(np.float32)
    idx = np.arange(RET_C, dtype=np.float32)
    diff = idx[:, None] - idx[None, :]
    dtab = np.where(diff >= 0, np.exp(log_g[:, None, None] * np.maximum(diff, 0.0)), 0.0).astype(np.float32)
    dtab = dtab.reshape(8 * RET_C, RET_C)
    lane_head = np.arange(256) // 32
    a_tab = np.exp(log_g[lane_head][None, :] * (idx + 1.0)[:, None]).astype(np.float32)
    b_tab = np.exp(log_g[lane_head][None, :] * (RET_C - 1.0 - idx)[:, None]).astype(np.float32)
    lam = np.exp(log_g[lane_head] * RET_C).astype(np.float32)[:, None]
    bd = (lane_head[:, None] == (np.arange(512) // 64)[None, :]).astype(np.float32)
    return dtab, a_tab, b_tab, lam, bd


def _rotary_tables():
    inv_r = (1.0 / (np.float32(10000.0) ** np.linspace(0.0, 1.0, 16, dtype=np.float32))).astype(np.float32)
    inv_a = (np.float32(500000.0) ** (-np.arange(0, 16, 2, dtype=np.float32) / np.float32(16))).astype(np.float32)
    ifc = np.zeros((1, 128), np.float32)
    ifc[0, 0:16], ifc[0, 16:24] = inv_r, inv_a
    spread = np.zeros((128, 768), np.float32)
    for lane in range(256):
        spread[(lane % 32) % 16, lane] = 1.0
    for lane in range(512):
        d = lane % 64
        spread[16 + d % 8 if d < 16 else 24, 256 + lane] = 1.0
    return ifc, spread


def _rot_halves(tm):
    lo_r = (lax.broadcasted_iota(jnp.int32, (tm, 256), 1) % 32) < 16
    lo_a = (lax.broadcasted_iota(jnp.int32, (tm, 512), 1) % 64) < 8
    return lo_r, lo_a


def _spread_exact(t, e):
    hi = t.astype(BF16)
    r1 = t - hi.astype(F32)
    mid = r1.astype(BF16)
    lo = (r1 - mid.astype(F32)).astype(BF16)
    return _nn(hi, e) + _nn(mid, e) + _nn(lo, e)


def _rot_tables(cos_ref, sin_ref, e_ref):
    cs = _spread_exact(cos_ref[...], e_ref[...])
    sn = _spread_exact(sin_ref[...], e_ref[...])
    return cs[:, 0:256], cs[:, 256:768], sn[:, 0:256], sn[:, 256:768]


def _proj_fwd(h1, win_g, cos, sin, spread, after):
    tm = 256

    def body(h_ref, w_ref, cos_ref, sin_ref, e_ref, qr_ref, kr_ref, rv_ref, rg_ref, aq_ref, ak_ref, av_ref, p_ref, _):
        h = h_ref[...]
        for k in range(N_CHIP):
            p_ref[:, k * WIN_C:(k + 1) * WIN_C] = _nn(h, w_ref[k])
        cr, ca, sr, sa = _rot_tables(cos_ref, sin_ref, e_ref)
        lo_r, lo_a = _rot_halves(tm)

        def rot_r(v):
            return v * cr + sr * jnp.where(lo_r, -pltpu.roll(v, 240, 1), pltpu.roll(v, 16, 1))

        def rot_a(v):
            return v * ca + sa * jnp.where(lo_a, -pltpu.roll(v, 504, 1), pltpu.roll(v, 8, 1))

        qr_ref[...] = rot_r(p_ref[:, 0:256]).astype(BF16)
        kr_ref[...] = (rot_r(p_ref[:, 256:512]) * RET_SCALE).astype(BF16)
        rv_ref[...] = p_ref[:, 512:1024].astype(BF16)
        rg_ref[...] = p_ref[:, 1024:1536]
        aq, ak = rot_a(p_ref[:, 1536:2048]), rot_a(p_ref[:, 2048:2560])
        for j in range(4):
            aq_ref[j] = aq[:, 128 * j:128 * j + 128]
            ak_ref[j] = ak[:, 128 * j:128 * j + 128]
            av_ref[j] = p_ref[:, 2560 + 128 * j:2560 + 128 * j + 128]

    row = lambda w: pl.BlockSpec((tm, w), lambda i: (i, 0))
    slab = pl.BlockSpec((4, tm, 128), lambda i: (0, i, 0))
    return _carry(
        "proj_fwd", body, _NoExchange(), (), (h1, win_g, cos, sin, spread),
        [row(D), pl.BlockSpec((N_CHIP, D, WIN_C), lambda i: (0, 0, 0)), row(128), row(128),
         pl.BlockSpec((128, 768), lambda i: (0, 0))],
        [row(256), row(256), row(512), row(512), slab, slab, slab],
        [jax.ShapeDtypeStruct((S, w), BF16) for w in (256, 256, 512)]
        + [jax.ShapeDtypeStruct((S, 512), F32)] + [jax.ShapeDtypeStruct((4, S, 128), F32)] * 3,
        scratch_shapes=[pltpu.VMEM((tm, PW), F32)], grid=(S // tm,), semantics=("parallel",), after=after)[0]


def _seg_mean(v):
    lo = lax.broadcasted_iota(jnp.int32, v.shape, 1) < 64
    s_lo = jnp.sum(jnp.where(lo, v, 0.0), axis=-1, keepdims=True)
    s_hi = jnp.sum(jnp.where(lo, 0.0, v), axis=-1, keepdims=True)
    return jnp.where(lo, s_lo, s_hi) * (1.0 / 64.0)


def _ret_fwd(qr, kr, rv, proj, tabs, exchange, exchange_args, after=None):
    C, G = RET_C, RET_PER_STEP
    steps = S // (C * G)
    dtab, a_tab, b_tab, lam, bd = tabs

    def body(q_ref, k_ref, v_ref, g_ref, dt_ref, a_ref, b_ref, lam_ref, bd_ref, o_ref, cat_ref, st_ref, R, exch):
        @pl.when(pl.program_id(0) == 0)
        def _():
            exch.start()
            R[...] = jnp.zeros_like(R)

        lane_head = lax.broadcasted_iota(jnp.int32, (C, 256), 1) // 32
        col_head = lax.broadcasted_iota(jnp.int32, (C, 256), 1) // 64
        for s in range(G):
            rows = slice(s * C, (s + 1) * C)
            q, k, v = q_ref[rows, :], k_ref[rows, :], v_ref[rows, :]
            rb = R[...].astype(BF16)
            st_ref[s] = rb
            qa = (q.astype(F32) * a_ref[...]).astype(BF16)
            cross = _nn(qa, rb)
            p = (_nt(_stack_heads(q, lane_head, n=8), k) * dt_ref[...]).astype(BF16)
            og = [cross[:, 256 * g:256 * g + 256]
                  + _unstack_heads(_nn(p[4 * C * g:4 * C * (g + 1)], v[:, 256 * g:256 * g + 256]), col_head)
                  for g in range(2)]
            kb = (k.astype(F32) * b_ref[...]).astype(BF16)
            R[...] = R[...] * lam_ref[...] + _tn(kb, v) * bd_ref[...]
            o_ref[rows, 0:256] = og[0]
            o_ref[rows, 256:512] = og[1]
            for j in range(4):
                oj = og[j // 2][:, 128 * (j % 2):128 * (j % 2) + 128]
                xc = oj - _seg_mean(oj)
                rn = xc * lax.rsqrt(_seg_mean(xc * xc) + GN_EPS)
                gj = g_ref[rows, 128 * j:128 * j + 128]
                cat_ref[rows, 128 * j:128 * j + 128] = (rn * (gj * _sigmoid(gj))).astype(BF16)

        @pl.when(pl.program_id(0) == steps - 1)
        def _():
            exch.middle()
            exch.finish()

    row = lambda w: pl.BlockSpec((C * G, w), lambda n: (n, 0))
    full = lambda a: pl.BlockSpec(a.shape, lambda n: (0,) * a.ndim)
    return _carry(
        "ret_fwd", body, exchange, exchange_args, (qr, kr, rv, proj, dtab, a_tab, b_tab, lam, bd),
        [row(256), row(256), row(512), row(512),
         full(dtab), full(a_tab), full(b_tab), full(lam), full(bd)],
        [row(512), row(512), pl.BlockSpec((G, 256, 512), lambda n: (n, 0, 0))],
        [jax.ShapeDtypeStruct((S, 512), F32), jax.ShapeDtypeStruct((S, 512), BF16),
         jax.ShapeDtypeStruct((S // C, 256, 512), BF16)],
        scratch_shapes=[pltpu.VMEM((256, 512), F32)], grid=(steps,), semantics=("arbitrary",), after=after)


def _stack_heads(v, lane_head, fill=0.0, n=4):
    return jnp.concatenate([jnp.where(lane_head == h, v, jnp.full_like(v, fill)) for h in range(n)], axis=0)


def _unstack_heads(v, lane_head, n=4):
    out = v[0:ATT_BLK]
    for h in range(1, n):
        out = jnp.where(lane_head == h, v[h * ATT_BLK:(h + 1) * ATT_BLK], out)
    return out


def _att_bias(has_prev):
    nk = 2 * ATT_BLK if has_prev else ATT_BLK
    a = lax.broadcasted_iota(jnp.int32, (4 * ATT_BLK, nk), 0) % ATT_BLK
    kk = lax.broadcasted_iota(jnp.int32, (4 * ATT_BLK, nk), 1)
    if not has_prev:
        return None, jnp.where((a - kk) >= 0, 0.0, NEG)
    dist = ATT_BLK + a - kk
    inside = (dist >= 0) & (dist <= ATT_BLK)
    return jnp.where(inside, 0.0, NEG), jnp.where(inside & (kk >= ATT_BLK), 0.0, NEG)


def _class_rows(ib, r, d):
    if d == 1:
        return pl.ds(pl.multiple_of(ib * ATT_BLK, ATT_BLK), ATT_BLK)
    return pl.ds(ib * ATT_BLK * d + r, ATT_BLK, stride=d)


def _slab_pair(ref, g, rows):
    return jnp.concatenate([ref[2 * g, rows, :], ref[2 * g + 1, rows, :]], axis=1)


def _att_blocks(d):
    nb = S // d // ATT_BLK
    return nb, nb > 1


def _att_fwd(aq, ak, av, exchange, exchange_args):
    def body(q_ref, k_ref, v_ref, o_ref, l_ref, cat_ref, xc):
        xc.start()
        lane_head = lax.broadcasted_iota(jnp.int32, (ATT_BLK, 256), 1) // 64
        for pi, d in enumerate(PATTERN_DILATIONS):
            if pi == len(PATTERN_DILATIONS) - 1:
                xc.middle()
            nb, has_prev = _att_blocks(d)
            bias_rest, bias_first = _att_bias(has_prev)

            def block(b, carry, pi=pi, d=d, nb=nb, has_prev=has_prev, bias_rest=bias_rest, bias_first=bias_first):
                r, ib = b // nb, b % nb
                rows = _class_rows(ib, r, d)
                prow = _class_rows(jnp.maximum(ib - 1, 0), r, d)
                bias = jnp.where(ib == 0, bias_first, bias_rest) if has_prev else bias_first
                for g in range(2):
                    qg = _slab_pair(q_ref, g, rows).astype(BF16)
                    kg = _slab_pair(k_ref, g, rows)
                    vg = _slab_pair(v_ref, g, rows)
                    if has_prev:
                        kg = jnp.concatenate([_slab_pair(k_ref, g, prow), kg], axis=0)
                        vg = jnp.concatenate([_slab_pair(v_ref, g, prow), vg], axis=0)
                    kg, vg = kg.astype(BF16), vg.astype(BF16)
                    s = _nt(_stack_heads(qg, lane_head), kg) * ATT_SCALE + bias
                    m = jnp.max(s, axis=-1, keepdims=True)
                    p = jnp.exp(s - m)
                    den = jnp.sum(p, axis=-1, keepdims=True)
                    og = _unstack_heads(_nn(p.astype(BF16), vg) / den, lane_head)
                    lg = _unstack_heads(jnp.broadcast_to(m + jnp.log(den), (4 * ATT_BLK, 256)), lane_head)
                    for jj in range(2):
                        j = 2 * g + jj
                        o_new, l_new = og[:, 128 * jj:128 * jj + 128], lg[:, 128 * jj:128 * jj + 128]
                        if pi > 0:
                            o_old, l_old = o_ref[j, rows, :], l_ref[j, rows, :]
                            mx = jnp.maximum(l_old, l_new)
                            ea, eb = jnp.exp(l_old - mx), jnp.exp(l_new - mx)
                            den = ea + eb
                            o_new = (ea * o_old + eb * o_new) / den
                            l_new = mx + jnp.log(den)
                        o_ref[j, rows, :] = o_new
                        l_ref[j, rows, :] = l_new
                return carry

            lax.fori_loop(0, S // ATT_BLK, block, 0, unroll=4)

        def to_cat(i, carry):
            rows = _rows(i, 256)
            for j in range(4):
                cat_ref[rows, 128 * j:128 * j + 128] = o_ref[j, rows, :].astype(BF16)
            return carry

        lax.fori_loop(0, S // 256, to_cat, 0)
        xc.finish()

    slab = jax.ShapeDtypeStruct((4, S, 128), F32)
    return _carry("att_fwd", body, exchange, exchange_args, (aq, ak, av), [VMEM] * 3, [VMEM] * 3,
                  [slab, slab, jax.ShapeDtypeStruct((S, 512), BF16)])


def _mix_fwd(cat_r, cat_a, wout, x, g2, g3, exchange, exchange_args):
    tm = 512

    def body(cr_ref, ca_ref, w_ref, x_ref, g2_ref, g3_ref, mix_ref, x2_ref, h3_ref, xc):
        @pl.when(pl.program_id(0) == 0)
        def _():
            xc.start()

        mix = _nn(cr_ref[...], w_ref[0:512, :]) + _nn(ca_ref[...], w_ref[512:1024, :])
        mix_ref[...] = mix
        x2 = x_ref[...] + mix * _rstd(mix) * g2_ref[...]
        x2_ref[...] = x2
        h3_ref[...] = (x2 * _rstd(x2) * g3_ref[...]).astype(BF16)

        @pl.when(pl.program_id(0) == S // tm - 1)
        def _():
            xc.middle()
            xc.finish()

    row = lambda w: pl.BlockSpec((tm, w), lambda i: (i, 0))
    vec = pl.BlockSpec((1, D), lambda i: (0, 0))
    return _carry("mix_fwd", body, exchange, exchange_args, (cat_r, cat_a, wout, x, g2, g3),
                  [row(512), row(512), pl.BlockSpec((D, D), lambda i: (0, 0)), row(D), vec, vec],
                  [row(D), row(D), row(D)],
                  [jax.ShapeDtypeStruct((S, D), F32), jax.ShapeDtypeStruct((S, D), F32),
                   jax.ShapeDtypeStruct((S, D), BF16)],
                  grid=(S // tm,), semantics=("arbitrary",))


def _ffn_fwd(h3, wg, wu, wd, x2, tgt, g4):
    tm = 512
    last = N_CHIP - 1

    def body(h_ref, wg_ref, wu_ref, wd_ref, x2_ref, t_ref, g_ref,
             gt_ref, up_ref, a_ref, loss_ref, dy_ref, df_ref, dg_ref, f_ref):
        k, i = pl.program_id(0), pl.program_id(1)
        h = h_ref[...]
        gt = _nt(h, wg_ref[...])
        up = _nt(h, wu_ref[...])
        gt_ref[...] = gt.astype(BF16)
        up_ref[...] = up.astype(BF16)
        a = (gt * _sigmoid(gt) * up).astype(BF16)
        a_ref[...] = a
        part = _nn(a, wd_ref[...])
        rows = _rows(i, tm)

        @pl.when(k == 0)
        def _():
            f_ref[rows, :] = part

        @pl.when((k > 0) & (k < last))
        def _():
            f_ref[rows, :] = f_ref[rows, :] + part

        @pl.when((k == last) & (i == 0))
        def _():
            loss_ref[...] = jnp.zeros_like(loss_ref)
            dg_ref[...] = jnp.zeros_like(dg_ref)

        @pl.when(k == last)
        def _():
            fv = f_ref[rows, :] + part
            r = _rstd(fv)
            fn = fv * r
            e = x2_ref[...] + fn * g_ref[...] - t_ref[...]
            loss_ref[...] = loss_ref[...] + jnp.sum(jnp.sum(e * e, axis=-1, keepdims=True), axis=0, keepdims=True)
            dy = e * (1.0 / D)
            dy_ref[...] = dy
            dg_ref[...] = dg_ref[...] + jnp.sum(dy * fn, axis=0, keepdims=True)
            t = dy * g_ref[...]
            df_ref[...] = (r * (t - fn * jnp.mean(t * fn, axis=-1, keepdims=True))).astype(BF16)

    wrow = pl.BlockSpec((None, FF_C, D), lambda k, i: (k, 0, 0))
    act = pl.BlockSpec((None, tm, FF_C), lambda k, i: (k, i, 0))
    late = pl.BlockSpec((tm, D), lambda k, i: (jnp.where(k == last, i, 0), 0))
    vec = pl.BlockSpec((1, D), lambda k, i: (0, 0))
    return pl.pallas_call(
        body, grid=(N_CHIP, S // tm), name="ffn_fwd",
        in_specs=[pl.BlockSpec((tm, D), lambda k, i: (i, 0)), wrow, wrow, wrow, late, late, vec],
        out_specs=[act, act, act, vec, late, late, vec],
        out_shape=[jax.ShapeDtypeStruct((N_CHIP, S, FF_C), BF16)] * 3
                  + [jax.ShapeDtypeStruct((1, D), F32), jax.ShapeDtypeStruct((S, D), F32),
                     jax.ShapeDtypeStruct((S, D), BF16), jax.ShapeDtypeStruct((1, D), F32)],
        scratch_shapes=[pltpu.VMEM((S, D), F32)],
        compiler_params=_params("arbitrary", "arbitrary"),
    )(h3, wg, wu, wd, x2, tgt, g4)


def _ffn_bwd_act(df, gt, up, wg, wu, wd, dy, x2, mix, g2, g3):
    tm, sub = 512, 256
    last = N_CHIP - 1

    def body(df_ref, gt_ref, up_ref, wg_ref, wu_ref, wd_ref, dy_ref, x2_ref, mix_ref, g2_ref, g3_ref,
             dgt_ref, dup_ref, dx2_ref, dmix_ref, dg3_ref, dg2_ref, dh_ref):
        k, i = pl.program_id(0), pl.program_id(1)
        parts = []
        for s in range(tm // sub):
            rows = slice(s * sub, (s + 1) * sub)
            da = _nt(df_ref[rows, :], wd_ref[...])
            gt, up = gt_ref[rows, :].astype(F32), up_ref[rows, :].astype(F32)
            sg = _sigmoid(gt)
            dup = (da * gt * sg).astype(BF16)
            dgt = (da * up * (sg * (1.0 + gt * (1.0 - sg)))).astype(BF16)
            dup_ref[rows, :] = dup
            dgt_ref[rows, :] = dgt
            parts.append(_nn(dgt, wg_ref[...]) + _nn(dup, wu_ref[...]))
        part = jnp.concatenate(parts, axis=0)
        rows = _rows(i, tm)

        @pl.when(k == 0)
        def _():
            dh_ref[rows, :] = part

        @pl.when((k > 0) & (k < last))
        def _():
            dh_ref[rows, :] = dh_ref[rows, :] + part

        @pl.when((k == last) & (i == 0))
        def _():
            dg3_ref[...] = jnp.zeros_like(dg3_ref)
            dg2_ref[...] = jnp.zeros_like(dg2_ref)

        @pl.when(k == last)
        def _():
            dh = dh_ref[rows, :] + part
            x2 = x2_ref[...]
            r3 = _rstd(x2)
            xn = x2 * r3
            dg3_ref[...] = dg3_ref[...] + jnp.sum(dh * xn, axis=0, keepdims=True)
            t = dh * g3_ref[...]
            dx2 = dy_ref[...] + r3 * (t - xn * jnp.mean(t * xn, axis=-1, keepdims=True))
            dx2_ref[...] = dx2
            mix = mix_ref[...]
            r2 = _rstd(mix)
            mn = mix * r2
            dg2_ref[...] = dg2_ref[...] + jnp.sum(dx2 * mn, axis=0, keepdims=True)
            u = dx2 * g2_ref[...]
            dmix_ref[...] = (r2 * (u - mn * jnp.mean(u * mn, axis=-1, keepdims=True))).astype(BF16)

    wrow = pl.BlockSpec((None, FF_C, D), lambda k, i: (k, 0, 0))
    act = pl.BlockSpec((None, tm, FF_C), lambda k, i: (k, i, 0))
    row = pl.BlockSpec((tm, D), lambda k, i: (i, 0))
    late = pl.BlockSpec((tm, D), lambda k, i: (jnp.where(k == last, i, 0), 0))
    vec = pl.BlockSpec((1, D), lambda k, i: (0, 0))
    return pl.pallas_call(
        body, grid=(N_CHIP, S // tm), name="ffn_bwd_act",
        in_specs=[row, act, act, wrow, wrow, wrow, late, late, late, vec, vec],
        out_specs=[act, act, late, late, vec, vec],
        out_shape=[jax.ShapeDtypeStruct((N_CHIP, S, FF_C), BF16), jax.ShapeDtypeStruct((N_CHIP, S, FF_C), BF16),
                   jax.ShapeDtypeStruct((S, D), F32), jax.ShapeDtypeStruct((S, D), BF16),
                   jax.ShapeDtypeStruct((1, D), F32), jax.ShapeDtypeStruct((1, D), F32)],
        scratch_shapes=[pltpu.VMEM((S, D), F32)],
        compiler_params=_params("arbitrary", "arbitrary"),
    )(df, gt, up, wg, wu, wd, dy, x2, mix, g2, g3)


def _ffn_bwd_w(a, df, h3, dgt, dup):
    tm = 1024
    assert S // tm == 2

    def body(a_ref, df_ref, h_ref, dgt_ref, dup_ref, dwd_ref, dwg_ref, dwu_ref, acc_d, acc_g, acc_u):
        i = pl.program_id(1)
        h = h_ref[...]
        parts = (_tn(a_ref[...], df_ref[...]), _tn(dgt_ref[...], h), _tn(dup_ref[...], h))

        @pl.when(i == 0)
        def _():
            for acc, part in zip((acc_d, acc_g, acc_u), parts):
                acc[...] = part

        @pl.when(i == S // tm - 1)
        def _():
            for out, acc, part in zip((dwd_ref, dwg_ref, dwu_ref), (acc_d, acc_g, acc_u), parts):
                out[...] = (acc[...] + part).astype(BF16)

    act = pl.BlockSpec((None, tm, FF_C), lambda k, i: (k, i, 0))
    row = pl.BlockSpec((tm, D), lambda k, i: (i, 0))
    wrow = pl.BlockSpec((None, FF_C, D), lambda k, i: (k, 0, 0))
    return pl.pallas_call(
        body, grid=(N_CHIP, S // tm), name="ffn_bwd_w",
        in_specs=[act, row, row, act, act],
        out_specs=[wrow, wrow, wrow],
        out_shape=[jax.ShapeDtypeStruct((N_CHIP, FF_C, D), BF16)] * 3,
        scratch_shapes=[pltpu.VMEM((FF_C, D), F32)] * 3,
        compiler_params=_params("parallel", "arbitrary"),
    )(a, df, h3, dgt, dup)


def _mix_bwd(dmix, cat_r, cat_a, wout, exchange, exchange_args):
    tm = 1024

    def body(dm_ref, cr_ref, ca_ref, w_ref, dret_ref, datt_ref, dw_ref, acc, xc):
        i = pl.program_id(0)

        @pl.when(i == 0)
        def _():
            xc.start()
            acc[...] = jnp.zeros_like(acc)

        dm = dm_ref[...]
        dret_ref[...] = _nt(dm, w_ref[0:512, :])
        datt = _nt(dm, w_ref[512:1024, :])
        for j in range(4):
            datt_ref[j] = datt[:, 128 * j:128 * j + 128]
        acc[0:512, :] += _tn(cr_ref[...], dm)
        acc[512:1024, :] += _tn(ca_ref[...], dm)

        @pl.when(i == S // tm - 1)
        def _():
            dw_ref[...] = acc[...].astype(BF16)
            xc.middle()
            xc.finish()

    row = lambda w: pl.BlockSpec((tm, w), lambda i: (i, 0))
    full = pl.BlockSpec((D, D), lambda i: (0, 0))
    return _carry("mix_bwd", body, exchange, exchange_args, (dmix, cat_r, cat_a, wout),
                  [row(D), row(512), row(512), full],
                  [row(512), pl.BlockSpec((4, tm, 128), lambda i: (0, i, 0)), full],
                  [jax.ShapeDtypeStruct((S, 512), F32), jax.ShapeDtypeStruct((4, S, 128), F32),
                   jax.ShapeDtypeStruct((D, D), BF16)],
                  scratch_shapes=[pltpu.VMEM((D, D), F32)], grid=(S // tm,), semantics=("arbitrary",))


def _att_bwd(aq, ak, av, datt, att_out, lse, exchange, exchange_args, after=None):
    def body(q_ref, k_ref, v_ref, do_ref, out_ref, l_ref, dq_ref, dk_ref, dv_ref, xc):
        xc.start()

        lane_head = lax.broadcasted_iota(jnp.int32, (ATT_BLK, 256), 1) // 64
        for pi, d in enumerate(PATTERN_DILATIONS):
            nb, has_prev = _att_blocks(d)
            assert pi > 0 or not has_prev
            bias_rest, bias_first = _att_bias(has_prev)

            def block(b, carry, pi=pi, d=d, nb=nb, has_prev=has_prev, bias_rest=bias_rest, bias_first=bias_first):
                r, ib = b // nb, b % nb
                rows = _class_rows(ib, r, d)
                prow = _class_rows(jnp.maximum(ib - 1, 0), r, d)
                bias = jnp.where(ib == 0, bias_first, bias_rest) if has_prev else bias_first
                for g in range(2):
                    qg = _slab_pair(q_ref, g, rows).astype(BF16)
                    kg = _slab_pair(k_ref, g, rows)
                    vg = _slab_pair(v_ref, g, rows)
                    if has_prev:
                        kg = jnp.concatenate([_slab_pair(k_ref, g, prow), kg], axis=0)
                        vg = jnp.concatenate([_slab_pair(v_ref, g, prow), vg], axis=0)
                    kg, vg = kg.astype(BF16), vg.astype(BF16)
                    dog = _slab_pair(do_ref, g, rows)
                    outg = _slab_pair(out_ref, g, rows)
                    lg = _slab_pair(l_ref, g, rows)
                    qs = _stack_heads(qg, lane_head)
                    dos = _stack_heads(dog, lane_head)
                    delta = jnp.sum(dos * jnp.concatenate([outg] * 4, axis=0), axis=-1, keepdims=True)
                    lh = jnp.max(_stack_heads(lg, lane_head, NEG), axis=-1, keepdims=True)
                    s = _nt(qs, kg) * ATT_SCALE + bias
                    p = jnp.exp(s - lh)
                    dosb = dos.astype(BF16)
                    ds = (p * (_nt(dosb, vg) - delta) * ATT_SCALE).astype(BF16)
                    dq = _unstack_heads(_nn(ds, kg), lane_head)
                    dk = _tn(ds, qs)
                    dv = _tn(p.astype(BF16), dosb)
                    for jj in range(2):
                        j, sl = 2 * g + jj, slice(128 * jj, 128 * jj + 128)
                        if pi == 0:
                            dq_ref[j, rows, :] = dq[:, sl]
                            dk_ref[j, rows, :] = dk[:, sl]
                            dv_ref[j, rows, :] = dv[:, sl]
                            continue
                        dq_ref[j, rows, :] += dq[:, sl]
                        if has_prev:
                            dk_ref[j, prow, :] += dk[0:ATT_BLK, sl]
                            dv_ref[j, prow, :] += dv[0:ATT_BLK, sl]
                            dk_ref[j, rows, :] += dk[ATT_BLK:2 * ATT_BLK, sl]
                            dv_ref[j, rows, :] += dv[ATT_BLK:2 * ATT_BLK, sl]
                        else:
                            dk_ref[j, rows, :] += dk[:, sl]
                            dv_ref[j, rows, :] += dv[:, sl]
                return carry

            lax.fori_loop(0, S // ATT_BLK, block, 0, unroll=4)
        xc.middle()
        xc.finish()

    slab = jax.ShapeDtypeStruct((4, S, 128), F32)
    return _carry("att_bwd", body, exchange, exchange_args, (aq, ak, av, datt, att_out, lse), [VMEM] * 6, [VMEM] * 3,
                  [slab, slab, slab], after=after)


def _ret_bwd(qr, kr, rv, proj, o_raw, states, dret, tabs, exchange, exchange_args, after=None):
    C, G = RET_C, RET_PER_STEP
    steps = S // (C * G)
    dtab, a_tab, b_tab, lam, bd = tabs

    def body(q_ref, k_ref, v_ref, g_ref, o_ref, st_ref, dr_ref, dt_ref, a_ref, b_ref, lam_ref, bd_ref,
             dq_ref, dk_ref, dv_ref, dg_ref, dR, exch):
        @pl.when(pl.program_id(0) == 0)
        def _():
            exch.start()
            dR[...] = jnp.zeros_like(dR)

        lane_head = lax.broadcasted_iota(jnp.int32, (C, 256), 1) // 32
        col_head = lax.broadcasted_iota(jnp.int32, (C, 256), 1) // 64
        for s in reversed(range(G)):
            rows = slice(s * C, (s + 1) * C)
            q, k, v = q_ref[rows, :], k_ref[rows, :], v_ref[rows, :]
            dos = []
            for j in range(4):
                sl = slice(128 * j, 128 * j + 128)
                oj = o_ref[rows, sl]
                xc = oj - _seg_mean(oj)
                rs = lax.rsqrt(_seg_mean(xc * xc) + GN_EPS)
                rn = xc * rs
                gj = g_ref[rows, sl]
                sg = _sigmoid(gj)
                dret = dr_ref[rows, sl]
                dg_ref[rows, sl] = dret * rn * (sg * (1.0 + gj * (1.0 - sg)))
                drn = dret * (gj * sg)
                dos.append(rs * (drn - _seg_mean(drn) - rn * _seg_mean(drn * rn)))
            do = [jnp.concatenate(dos[0:2], axis=1), jnp.concatenate(dos[2:4], axis=1)]
            do8 = jnp.concatenate(do, axis=1).astype(BF16)
            drb = dR[...].astype(BF16)
            rb = st_ref[s]
            dq = _nt(do8, rb) * a_ref[...]
            dk = _nt(v, drb) * b_ref[...]
            kb = (k.astype(F32) * b_ref[...]).astype(BF16)
            dvall = _nn(kb, drb)
            qs = _stack_heads(q, lane_head, n=8)
            dec = dt_ref[...]
            p = (_nt(qs, k) * dec).astype(BF16)
            dos = [_stack_heads(do[g], col_head).astype(BF16) for g in range(2)]
            dp = jnp.concatenate([_nt(dos[g], v[:, 256 * g:256 * g + 256]) for g in range(2)], axis=0)
            ds = (dp * dec).astype(BF16)
            dq = dq + _unstack_heads(_nn(ds, k), lane_head, n=8)
            dk = dk + _tn(ds, qs)
            dv = [dvall[:, 256 * g:256 * g + 256] + _tn(p[4 * C * g:4 * C * (g + 1)], dos[g]) for g in range(2)]
            qa = (q.astype(F32) * a_ref[...]).astype(BF16)
            dR[...] = dR[...] * lam_ref[...] + _tn(qa, do8) * bd_ref[...]
            dq_ref[rows, :] = dq
            dk_ref[rows, :] = dk
            dv_ref[rows, 0:256] = dv[0]
            dv_ref[rows, 256:512] = dv[1]

        @pl.when(pl.program_id(0) == steps - 1)
        def _():
            exch.middle()
            exch.finish()

    rev = lambda w: pl.BlockSpec((C * G, w), lambda n: (steps - 1 - n, 0))
    full = lambda a: pl.BlockSpec(a.shape, lambda n: (0,) * a.ndim)
    return _carry(
        "ret_bwd", body, exchange, exchange_args, (qr, kr, rv, proj, o_raw, states, dret, dtab, a_tab, b_tab, lam, bd),
        [rev(256), rev(256), rev(512), rev(512), rev(512),
         pl.BlockSpec((G, 256, 512), lambda n: (steps - 1 - n, 0, 0)), rev(512),
         full(dtab), full(a_tab), full(b_tab), full(lam), full(bd)],
        [rev(256), rev(256), rev(512), rev(512)],
        [jax.ShapeDtypeStruct((S, 256), F32), jax.ShapeDtypeStruct((S, 256), F32),
         jax.ShapeDtypeStruct((S, 512), F32), jax.ShapeDtypeStruct((S, 512), F32)],
        scratch_shapes=[pltpu.VMEM((256, 512), F32)], grid=(steps,), semantics=("arbitrary",), after=after)


def _rot_bwd(cos, sin, spread, dqr, dkr, drv, drg, dq_att, dk_att, dv_att):
    tm = 256

    def body(cos_ref, sin_ref, e_ref, dqr_ref, dkr_ref, drv_ref, drg_ref, dqa_ref, dka_ref, dva_ref, dp_ref):
        cr, ca, sr, sa = _rot_tables(cos_ref, sin_ref, e_ref)
        lo_r, lo_a = _rot_halves(tm)

        def unrot_r(g):
            gs = g * sr
            return g * cr + pltpu.roll(jnp.where(lo_r, -gs, 0.0), 16, 1) + pltpu.roll(jnp.where(lo_r, 0.0, gs), 240, 1)

        def unrot_a(g):
            gs = g * sa
            return g * ca + pltpu.roll(jnp.where(lo_a, -gs, 0.0), 8, 1) + pltpu.roll(jnp.where(lo_a, 0.0, gs), 504, 1)

        def wide(ref):
            return jnp.concatenate([ref[j] for j in range(4)], axis=1)

        dp_ref[:, 0:256] = unrot_r(dqr_ref[...]).astype(BF16)
        dp_ref[:, 256:512] = unrot_r(dkr_ref[...] * RET_SCALE).astype(BF16)
        dp_ref[:, 512:1024] = drv_ref[...].astype(BF16)
        dp_ref[:, 1024:1536] = drg_ref[...].astype(BF16)
        dp_ref[:, 1536:2048] = unrot_a(wide(dqa_ref)).astype(BF16)
        dp_ref[:, 2048:2560] = unrot_a(wide(dka_ref)).astype(BF16)
        dp_ref[:, 2560:3072] = wide(dva_ref).astype(BF16)

    row = lambda w: pl.BlockSpec((tm, w), lambda i: (i, 0))
    slab = pl.BlockSpec((4, tm, 128), lambda i: (0, i, 0))
    return pl.pallas_call(
        body, grid=(S // tm,), name="rot_bwd",
        in_specs=[row(128), row(128), pl.BlockSpec((128, 768), lambda i: (0, 0)),
                  row(256), row(256), row(512), row(512), slab, slab, slab],
        out_specs=row(PW), out_shape=jax.ShapeDtypeStruct((S, PW), BF16),
        compiler_params=_params("parallel"),
    )(cos, sin, spread, dqr, dkr, drv, drg, dq_att, dk_att, dv_att)


def _win_bwd_w(h1, dproj, exchange, exchange_args):
    half = D // 2

    def sibling_copy(got_ref, buf, sems, k):
        x, y, c, me, chips = _place()
        return _remote(buf.at[k, pl.ds((1 - c) * half, half), :], got_ref.at[k], sems[0].at[k], sems[1].at[k],
                       (x, y, 1 - c))

    def body(h_ref, dp_ref, dw_ref, got_ref, buf, send, recv, xc):
        k = pl.program_id(0)

        @pl.when(k == 0)
        def _():
            xc.start()

        buf[k] = _tn(h_ref[...], dp_ref[...]).astype(BF16)
        sibling_copy(got_ref, buf, (send, recv), k).start()
        dw_ref[...] = buf[k, pl.ds(lax.axis_index("c") * half, half), :]

        @pl.when(k == N_CHIP - 1)
        def _():
            for j in range(N_CHIP):
                sibling_copy(got_ref, buf, (send, recv), j).wait_recv()
                sibling_copy(got_ref, buf, (send, recv), j).wait_send()
            xc.middle()
            xc.finish()

    halves = jax.ShapeDtypeStruct((N_CHIP, half, WIN_C), BF16)
    dma = pltpu.SemaphoreType.DMA((N_CHIP,))
    return _carry(
        "win_bwd_w", body, exchange, exchange_args, (h1, dproj),
        [pl.BlockSpec((S, D), lambda k: (0, 0)), pl.BlockSpec((S, WIN_C), lambda k: (0, k))],
        [pl.BlockSpec((None, half, WIN_C), lambda k: (k, 0, 0)), ANY], [halves, halves],
        scratch_shapes=[pltpu.VMEM((N_CHIP, D, WIN_C), BF16), dma, dma], grid=(N_CHIP,), semantics=("arbitrary",))


def _in_bwd(dproj, win_g, x, dx2, g1, other_rows, after):
    tm = 512
    n = len(other_rows)

    def body(dp_ref, w_ref, x_ref, dx2_ref, g_ref, *refs):
        rows, dx_ref, blk_ref = refs[:n], refs[n], refs[n + 1]

        @pl.when(pl.program_id(0) == 0)
        def _():
            blk_ref[...] = jnp.zeros_like(blk_ref)
            for i, r_ref in enumerate(rows):
                blk_ref[i + 1:i + 2, :] = r_ref[...]

        dh = _nt(dp_ref[:, 0:WIN_C], w_ref[0])
        for k in range(1, N_CHIP):
            dh = dh + _nt(dp_ref[:, k * WIN_C:(k + 1) * WIN_C], w_ref[k])
        xv = x_ref[...]
        r = _rstd(xv)
        xn = xv * r
        blk_ref[0:1, :] = blk_ref[0:1, :] + jnp.sum(dh * xn, axis=0, keepdims=True)
        t = dh * g_ref[...]
        dx_ref[...] = dx2_ref[...] + r * (t - xn * jnp.mean(t * xn, axis=-1, keepdims=True))

    row = lambda w: pl.BlockSpec((tm, w), lambda i: (i, 0))
    vec = pl.BlockSpec((1, D), lambda i: (0, 0))
    return _carry("in_bwd", body, _NoExchange(), (), (dproj, win_g, x, dx2, g1, *other_rows),
                  [row(PW), pl.BlockSpec((N_CHIP, D, WIN_C), lambda i: (0, 0, 0)), row(D), row(D), vec] + [vec] * n,
                  [row(D), pl.BlockSpec((8, D), lambda i: (0, 0))],
                  [jax.ShapeDtypeStruct((S, D), F32), jax.ShapeDtypeStruct((8, D), F32)],
                  grid=(S // tm,), semantics=("arbitrary",), after=after)[0]


ANY = pl.BlockSpec(memory_space=pl.ANY)
VMEM = pl.BlockSpec(memory_space=pltpu.VMEM)
FLIPS = ((1, 0), (0, 1), (1, 1))


def _place():
    x, y, c = lax.axis_index("x"), lax.axis_index("y"), lax.axis_index("c")
    chips = [((1 - x) if fx else x, (1 - y) if fy else y) for fx, fy in FLIPS]
    return x, y, c, 2 * x + y, chips


def _remote(src, dst, send_sem, recv_sem, device):
    return pltpu.make_async_remote_copy(src_ref=src, dst_ref=dst, send_sem=send_sem, recv_sem=recv_sem,
                                        device_id=device, device_id_type=MESH)


class _Exchange:
    aliases = {}

    def middle(self, ins, outs, sems):
        pass


class _GatherShards(_Exchange):
    def __init__(self, shards):
        n = self.n = len(shards)
        self.n_in = self.n_out = n
        self.out_shape = [jax.ShapeDtypeStruct((N_CHIP,) + s.shape, s.dtype) for s in shards]
        dma = pltpu.SemaphoreType.DMA
        self.scratch = [dma((3 * n,)), dma((3 * n,)), dma((3 * n,)), dma((3 * n,)), dma((n,)), dma((n,))]

    def _ici(self, ins, outs, sems, a, j, chip):
        x, y, c, me, chips = _place()
        half = ins[a].shape[0] // 2
        return _remote(ins[a].at[pl.ds(c * half, half), :], outs[a].at[me, pl.ds(c * half, half), :],
                       sems[0].at[3 * a + j], sems[1].at[3 * a + j], (*chip, c))

    def _fwd(self, outs, sems, a, j, chip, half_of):
        x, y, c, me, chips = _place()
        half = outs[a].shape[1] // 2
        blk = outs[a].at[2 * chip[0] + chip[1], pl.ds(half_of * half, half), :]
        return _remote(blk, blk, sems[2].at[3 * a + j], sems[3].at[3 * a + j], (x, y, 1 - c))

    def _own(self, ins, outs, sems, a):
        return _own_shard_to_sibling(ins[a], outs[a], sems[4].at[a], sems[5].at[a])

    def start(self, ins, outs, sems):
        chips = _place()[4]
        for a in range(self.n):
            for j, chip in enumerate(chips):
                self._ici(ins, outs, sems, a, j, chip).start()
        for a in range(self.n):
            self._own(ins, outs, sems, a).start()

    def middle(self, ins, outs, sems):
        x, y, c, me, chips = _place()
        for a in range(self.n):
            for j, chip in enumerate(chips):
                half = outs[a].shape[1] // 2
                blk = outs[a].at[2 * chip[0] + chip[1], pl.ds(c * half, half), :]
                _remote(blk, blk, sems[0].at[3 * a + j], sems[1].at[3 * a + j], (x, y, c)).wait_recv()
                self._fwd(outs, sems, a, j, chip, c).start()

    def finish(self, ins, outs, sems):
        x, y, c, me, chips = _place()
        for a in range(self.n):
            for j, chip in enumerate(chips):
                self._fwd(outs, sems, a, j, chip, 1 - c).wait_recv()
        for a in range(self.n):
            for j, chip in enumerate(chips):
                self._ici(ins, outs, sems, a, j, chip).wait_send()
                self._fwd(outs, sems, a, j, chip, c).wait_send()
            self._own(ins, outs, sems, a).wait()


def _own_shard_to_sibling(shard_ref, gathered_ref, send_sem, recv_sem):
    x, y, c, me, chips = _place()
    return _remote(shard_ref, gathered_ref.at[me], send_sem, recv_sem, (x, y, 1 - c))


class _NoExchange(_Exchange):
    n_in = n_out = 0
    out_shape = ()
    scratch = ()

    def start(self, ins, outs, sems):
        pass

    def finish(self, ins, outs, sems):
        pass


class _ForwardGathered(_Exchange):
    def __init__(self, shards, own=True, forward=True):
        self.own, self.forward = own, forward
        n = self.n = len(shards)
        self.n_in, self.n_out = 2 * n, n
        self.out_shape = [jax.ShapeDtypeStruct((N_CHIP,) + s.shape, s.dtype) for s in shards]
        dma = pltpu.SemaphoreType.DMA
        self.scratch = [dma((3 * n,)), dma((3 * n,)), dma((n,)), dma((n,))]
        self.aliases = {n + a: a for a in range(n)}

    def _fwd(self, outs, sems, a, j, chip, half_of):
        x, y, c, me, chips = _place()
        half = outs[a].shape[1] // 2
        blk = outs[a].at[2 * chip[0] + chip[1], pl.ds(half_of * half, half), :]
        return _remote(blk, blk, sems[0].at[3 * a + j], sems[1].at[3 * a + j], (x, y, 1 - c))

    def _own(self, ins, outs, sems, a):
        return _own_shard_to_sibling(ins[a], outs[a], sems[2].at[a], sems[3].at[a])

    def start(self, ins, outs, sems):
        x, y, c, me, chips = _place()
        for a in range(self.n):
            for j, chip in enumerate(chips if self.forward else ()):
                self._fwd(outs, sems, a, j, chip, c).start()
        for a in range(self.n if self.own else 0):
            self._own(ins, outs, sems, a).start()

    def finish(self, ins, outs, sems):
        x, y, c, me, chips = _place()
        for a in range(self.n):
            for j, chip in enumerate(chips if self.forward else ()):
                self._fwd(outs, sems, a, j, chip, 1 - c).wait_recv()
        for a in range(self.n):
            for j, chip in enumerate(chips if self.forward else ()):
                self._fwd(outs, sems, a, j, chip, c).wait_send()
            if self.own:
                self._own(ins, outs, sems, a).wait()


HBM = pl.BlockSpec(memory_space=pltpu.HBM)
SEMS = pl.BlockSpec(memory_space=pltpu.SEMAPHORE)
DATAFLOW = pltpu.SideEffectType.DATAFLOW_SIDE_EFFECTING


class _OverIci:
    def __init__(self, name, sources, lands):
        self.name, self.n = name, len(sources)
        hbm = lambda t: pltpu.with_memory_space_constraint(t, pltpu.HBM)
        self.arrays = [hbm(t) for t in sources] + [hbm(t) for t in lands]

    def sent(self, src, land, a, chip):
        raise NotImplementedError

    def landed(self, land, a, chip):
        raise NotImplementedError

    def _copy(self, arr, sems, a, j, receiving):
        x, y, c, me, chips = _place()
        src, dst = self.sent(arr[a], arr[self.n + a], a, chips[j])
        if receiving:
            dst = self.landed(arr[self.n + a], a, chips[j])
        return _remote(src, dst, sems[0].at[3 * a + j], sems[1].at[3 * a + j], (*chips[j], c))

    def start(self, after):
        m = len(self.arrays)

        def body(*refs):
            arr, sems, token = refs[:m], refs[m + 1:m + 3], refs[-1]
            for a in range(self.n):
                for j in range(3):
                    self._copy(arr, sems, a, j, False).start()
            token[...] = jnp.zeros_like(token)

        dma = pltpu.SemaphoreType.DMA
        outs = pl.pallas_call(
            body, name=self.name + "_start",
            out_shape=[dma((3 * self.n,)), dma((3 * self.n,))] + [pltpu.HBM(t.shape, t.dtype) for t in self.arrays]
                      + [jax.ShapeDtypeStruct((8, 128), F32)],
            in_specs=[HBM] * m + [ANY], out_specs=[SEMS, SEMS] + [HBM] * m + [VMEM],
            input_output_aliases={i: 2 + i for i in range(m)},
            compiler_params=pltpu.CompilerParams(has_side_effects=DATAFLOW),
        )(*self.arrays, after)
        self.sems, self.arrays = outs[0:2], list(outs[2:2 + m])
        return outs[-1]

    def wait(self, after):
        m = len(self.arrays)

        def body(*refs):
            arr, sems = refs[:m], refs[m:m + 2]
            for a in range(self.n):
                for j in range(3):
                    self._copy(arr, sems, a, j, False).wait_send()
                    self._copy(arr, sems, a, j, True).wait_recv()

        outs = pl.pallas_call(
            body, name=self.name + "_wait",
            out_shape=[pltpu.HBM(t.shape, t.dtype) for t in self.arrays],
            in_specs=[HBM] * m + [SEMS, SEMS, ANY], out_specs=[HBM] * m,
            input_output_aliases={i: i for i in range(m)},
            compiler_params=pltpu.CompilerParams(has_side_effects=DATAFLOW),
        )(*self.arrays, *self.sems, after)
        return list(outs[:self.n]), list(outs[self.n:])


class _GatherOverIci(_OverIci):
    def __init__(self, name, shards):
        super().__init__(name, shards, [lax.empty((N_CHIP,) + s.shape, s.dtype) for s in shards])

    @staticmethod
    def _half(ref):
        c = lax.axis_index("c")
        half = ref.shape[-2] // 2
        return pl.ds(c * half, half)

    def sent(self, src, land, a, chip):
        return src.at[self._half(src), :], land.at[_place()[3], self._half(src), :]

    def landed(self, land, a, chip):
        return land.at[2 * chip[0] + chip[1], self._half(land), :]


class _SumOverIci(_OverIci):
    def __init__(self, name, pre):
        super().__init__(name, pre, [lax.empty(p.shape, p.dtype) for p in pre])

    def sent(self, src, land, a, chip):
        return src.at[2 * chip[0] + chip[1]], land.at[_place()[3]]

    def landed(self, land, a, chip):
        return land.at[2 * chip[0] + chip[1]]


class _HalvesToSibling(_Exchange):
    def __init__(self, grads):
        n = self.n = len(grads)
        self.n_in = self.n_out = n
        self.out_shape = [jax.ShapeDtypeStruct((N_CHIP, g.shape[1] // 2, g.shape[2]), g.dtype) for g in grads]
        self.scratch = [pltpu.SemaphoreType.DMA((n,)), pltpu.SemaphoreType.DMA((n,))]

    def _copy(self, ins, outs, sems, a):
        x, y, c, me, chips = _place()
        half = ins[a].shape[1] // 2
        return _remote(ins[a].at[:, pl.ds((1 - c) * half, half), :], outs[a], sems[0].at[a], sems[1].at[a], (x, y, 1 - c))

    def start(self, ins, outs, sems):
        for a in range(self.n):
            self._copy(ins, outs, sems, a).start()

    def finish(self, ins, outs, sems):
        for a in range(self.n):
            self._copy(ins, outs, sems, a).wait_recv()
        for a in range(self.n):
            self._copy(ins, outs, sems, a).wait_send()


class _ShareHalves(_Exchange):
    def __init__(self, fulls):
        n = self.n = len(fulls)
        self.n_in = self.n_out = n
        self.out_shape = [jax.ShapeDtypeStruct(f.shape, f.dtype) for f in fulls]
        self.scratch = [pltpu.SemaphoreType.DMA((n,)), pltpu.SemaphoreType.DMA((n,))]
        self.aliases = {a: a for a in range(n)}

    def _copy(self, outs, sems, a, half_of):
        x, y, c, me, chips = _place()
        half = outs[a].shape[0] // 2
        rows = outs[a].at[pl.ds(half_of * half, half), :]
        return _remote(rows, rows, sems[0].at[a], sems[1].at[a], (x, y, 1 - c))

    def start(self, ins, outs, sems):
        c = _place()[2]
        for a in range(self.n):
            self._copy(outs, sems, a, c).start()

    def finish(self, ins, outs, sems):
        c = _place()[2]
        for a in range(self.n):
            self._copy(outs, sems, a, 1 - c).wait_recv()
        for a in range(self.n):
            self._copy(outs, sems, a, c).wait_send()


class _GatherBlocks(_Exchange):
    def __init__(self, block):
        self.n_in = self.n_out = 1
        self.out_shape = [jax.ShapeDtypeStruct((8,) + block.shape, block.dtype)]
        dma = pltpu.SemaphoreType.DMA
        self.scratch = [dma((7,)), dma((7,)), dma]

    @staticmethod
    def _peer(f):
        x, y, c, me, chips = _place()
        return ((1 - x) if f & 4 else x, (1 - y) if f & 2 else y, (1 - c) if f & 1 else c)

    def start(self, ins, outs, sems):
        x, y, c, me, chips = _place()
        for f in range(1, 8):
            _remote(ins[0], outs[0].at[2 * me + c], sems[0].at[f - 1], sems[1].at[f - 1], self._peer(f)).start()
        pltpu.make_async_copy(ins[0], outs[0].at[2 * me + c], sems[2]).start()

    def finish(self, ins, outs, sems):
        x, y, c, me, chips = _place()
        for f in range(1, 8):
            px, py, pc = self._peer(f)
            blk = outs[0].at[4 * px + 2 * py + pc]
            _remote(blk, blk, sems[0].at[f - 1], sems[1].at[f - 1], (x, y, c)).wait_recv()
        for f in range(1, 8):
            _remote(ins[0], outs[0].at[2 * me + c], sems[0].at[f - 1], sems[1].at[f - 1], self._peer(f)).wait_send()
        pltpu.make_async_copy(ins[0], outs[0].at[2 * me + c], sems[2]).wait()


class _Both(_Exchange):
    def __init__(self, first, second):
        self.parts = (first, second)
        self.n_in, self.n_out = first.n_in + second.n_in, first.n_out + second.n_out
        self.out_shape = first.out_shape + second.out_shape
        self.scratch = first.scratch + second.scratch
        self.aliases = dict(first.aliases)
        self.aliases.update({first.n_in + i: first.n_out + o for i, o in second.aliases.items()})

    def _split(self, ins, outs, sems):
        a, b = self.parts
        return ((a, ins[:a.n_in], outs[:a.n_out], sems[:len(a.scratch)]),
                (b, ins[a.n_in:], outs[a.n_out:], sems[len(a.scratch):]))

    def start(self, ins, outs, sems):
        for ex, i, o, s in self._split(ins, outs, sems):
            ex.start(i, o, s)

    def middle(self, ins, outs, sems):
        for ex, i, o, s in self._split(ins, outs, sems):
            ex.middle(i, o, s)

    def finish(self, ins, outs, sems):
        for ex, i, o, s in self._split(ins, outs, sems):
            ex.finish(i, o, s)


class _Bound:
    def __init__(self, ex, ins, outs, sems):
        self.start = lambda: ex.start(ins, outs, sems)
        self.middle = lambda: ex.middle(ins, outs, sems)
        self.finish = lambda: ex.finish(ins, outs, sems)


def _carry(name, body, ex, ex_args, args, in_specs, out_specs, out_shape, scratch_shapes=(), grid=None, semantics=(),
           after=None):
    n_a, n_o, n_s = len(args), len(out_shape), len(scratch_shapes)
    behind = [] if after is None else [after]

    def full_body(*refs):
        p = 0
        groups = []
        for size in (n_a, ex.n_in, len(behind), n_o, ex.n_out, n_s, len(ex.scratch)):
            groups.append(refs[p:p + size])
            p += size
        a, ei, _, o, eo, s, es = groups
        body(*a, *o, *s, _Bound(ex, ei, eo, es))

    kwargs = {} if grid is None else {"grid": grid}
    outs = pl.pallas_call(
        full_body, name=name,
        in_specs=list(in_specs) + [ANY] * (ex.n_in + len(behind)), out_specs=list(out_specs) + [ANY] * ex.n_out,
        out_shape=list(out_shape) + list(ex.out_shape), scratch_shapes=list(scratch_shapes) + list(ex.scratch),
        input_output_aliases={n_a + i: n_o + o for i, o in ex.aliases.items()},
        compiler_params=_params(*semantics) if semantics else pltpu.CompilerParams(vmem_limit_bytes=VMEM_LIMIT),
        **kwargs,
    )(*args, *ex_args, *behind)
    return outs[:n_o], outs[n_o:]


def _cast_bf16(arrays, after=None):
    n = len(arrays)
    behind = [] if after is None else [after]

    def body(*refs):
        for a in range(n):
            refs[len(refs) - n + a][...] = refs[a][...].astype(BF16)

    blks = [pl.BlockSpec((t.shape[0] // 4, t.shape[1]), lambda i: (i, 0)) for t in arrays]
    return pl.pallas_call(
        body, grid=(4,), name="cast_bf16", in_specs=blks + [ANY] * len(behind), out_specs=blks,
        out_shape=[jax.ShapeDtypeStruct(t.shape, BF16) for t in arrays], compiler_params=_params("parallel"),
    )(*arrays, *behind)


def _prepare(x, g1, pos, ifc, after):
    tm = 512

    def body(x_ref, g_ref, pos_ref, ifc_ref, h_ref, cos_ref, sin_ref, _):
        xv = x_ref[...]
        h_ref[...] = (xv * _rstd(xv) * g_ref[...]).astype(BF16)
        ang = pos_ref[...].astype(F32) * ifc_ref[...]
        cos_ref[...] = jnp.cos(ang)
        sin_ref[...] = jnp.sin(ang)

    row = lambda w: pl.BlockSpec((tm, w), lambda i: (i, 0))
    const = lambda w: pl.BlockSpec((1, w), lambda i: (0, 0))
    return _carry("prepare", body, _NoExchange(), (), (x, g1, pos, ifc),
                  [row(D), const(D), row(1), const(128)], [row(D), row(128), row(128)],
                  [jax.ShapeDtypeStruct((S, D), BF16)] + [jax.ShapeDtypeStruct((S, 128), F32)] * 2,
                  grid=(S // tm,), semantics=("parallel",), after=after)[0]


def _exchange_alone(name, ex, ex_args):
    def body(xc):
        xc.start()
        xc.middle()
        xc.finish()

    return _carry(name, body, ex, ex_args, (), (), (), ())[1]


def _core_index():
    return lax.axis_index("c").astype(jnp.int32).reshape(1)


def _pair_sum(gs, gots):
    n = len(gs)

    def body(c_ref, *refs):
        for a in range(n):
            refs[2 * n + a][...] = (refs[a][...].astype(F32) + refs[n + a][...].astype(F32)).astype(BF16)

    blk = [pl.BlockSpec((None,) + g.shape[1:], lambda k, c_ref: (k, 0, 0)) for g in gots]
    mine = [b if g.shape == got.shape else pl.BlockSpec((None,) + got.shape[1:], lambda k, c_ref: (k, c_ref[0], 0))
            for g, got, b in zip(gs, gots, blk)]
    return pl.pallas_call(
        body, name=f"pair_sum_{gots[0].shape[1]}x{gots[0].shape[2]}",
        grid_spec=pltpu.PrefetchScalarGridSpec(
            num_scalar_prefetch=1, grid=(N_CHIP,), in_specs=mine + blk, out_specs=blk),
        out_shape=[jax.ShapeDtypeStruct(g.shape, BF16) for g in gots],
        compiler_params=_params("parallel"),
    )(_core_index(), *gs, *gots)


def _chip_sum(pre, parts, share=False):
    n = len(parts)
    me = 2 * lax.axis_index("x") + lax.axis_index("y")
    others = [k + (k >= me).astype(jnp.int32) for k in range(3)]
    where = jnp.stack([lax.axis_index("c"), me, *others]).astype(jnp.int32)
    steps = 2

    def total(own, p1, p2, p3):
        return ((own[...].astype(F32) + p1[...].astype(F32)) + p2[...].astype(F32)) + p3[...].astype(F32)

    def body(w_ref, *refs):
        for a in range(n):
            refs[4 * n + a][...] = total(*refs[4 * a:4 * a + 4])

    def copies(outs, bufs, sems, a, i, sending):
        x, y, c, _, _ = _place()
        tr = bufs[a].shape[1]
        rows = lambda core: outs[a].at[pl.ds((steps * core + i) * tr, tr), :]
        s = steps * a + i
        return (pltpu.make_async_copy(bufs[a].at[i], rows(c), sems[2].at[s]),
                _remote(bufs[a].at[i], rows(c if sending else 1 - c), sems[0].at[s], sems[1].at[s], (x, y, 1 - c)))

    def sharing_body(w_ref, *refs):
        outs, bufs, sems = refs[4 * n:5 * n], refs[5 * n:6 * n], refs[6 * n:]
        i = pl.program_id(0)
        for a in range(n):
            bufs[a][i] = total(*refs[4 * a:4 * a + 4])
            for copy in copies(outs, bufs, sems, a, i, True):
                copy.start()

        @pl.when(i == steps - 1)
        def _():
            for a in range(n):
                for j in range(steps):
                    local, sent = copies(outs, bufs, sems, a, j, True)
                    local.wait()
                    sent.wait_send()
                    copies(outs, bufs, sems, a, j, False)[1].wait_recv()

    in_specs, out_specs, operands, scratch = [], [], [], []
    for a in range(n):
        _, half, cc = parts[a].shape
        tr = half // steps
        in_specs += [pl.BlockSpec((None, tr, cc), lambda i, w_ref, s=s: (w_ref[s], i, 0)) for s in (1, 2, 3, 4)]
        out_specs.append(ANY if share else pl.BlockSpec((tr, cc), lambda i, w_ref: (steps * w_ref[0] + i, 0)))
        operands += [pre[a], parts[a], parts[a], parts[a]]
        scratch.append(pltpu.VMEM((steps, tr, cc), F32))
    if share:
        scratch += [pltpu.SemaphoreType.DMA((steps * n,))] * 3
    return pl.pallas_call(
        sharing_body if share else body, name=f"chip_sum_{parts[0].shape[1]}x{parts[0].shape[2]}",
        grid_spec=pltpu.PrefetchScalarGridSpec(num_scalar_prefetch=1, grid=(steps,), in_specs=in_specs,
                                               out_specs=out_specs, scratch_shapes=scratch if share else []),
        out_shape=[jax.ShapeDtypeStruct((2 * p.shape[1], p.shape[2]), F32) for p in parts],
        compiler_params=_params("arbitrary" if share else "parallel"),
    )(where, *operands)


def _adamw_math(w, g, m, v):
    m = ADAM_B1 * m + (1.0 - ADAM_B1) * g
    v = ADAM_B2 * v + (1.0 - ADAM_B2) * (g * g)
    m_hat = m / (1.0 - ADAM_B1 ** ADAM_STEP)
    v_hat = v / (1.0 - ADAM_B2 ** ADAM_STEP)
    delta = -ADAM_LR * (m_hat / (jnp.sqrt(v_hat) + ADAM_EPS) + ADAM_WD * w)
    return delta, m, v


def _adamw(ws, gs, ms, vs, after=None, exchange=None, exchange_args=()):
    n, steps = len(ws), 4
    exchange = exchange or _NoExchange()

    def body(*refs):
        xc = refs[-1]

        @pl.when(pl.program_id(0) == 0)
        def _():
            xc.start()

        for a in range(n):
            w_ref, g_ref, m_ref, v_ref = (refs[t * n + a] for t in range(4))
            go_ref, d_ref, nm_ref, nv_ref = refs[4 * n + 4 * a:4 * n + 4 * a + 4]
            g = g_ref[...]
            go_ref[...] = g
            d_ref[...], nm_ref[...], nv_ref[...] = _adamw_math(w_ref[...], g, m_ref[...], v_ref[...])

        @pl.when(pl.program_id(0) == steps - 1)
        def _():
            xc.middle()
            xc.finish()

    blks = [pl.BlockSpec((w.shape[0] // steps, w.shape[1]), lambda i: (i, 0)) for w in ws]
    outs, exchanged = _carry(
        f"adamw_{ws[0].shape[0]}x{ws[0].shape[1]}", body, exchange, exchange_args, (*ws, *gs, *ms, *vs),
        blks * 4, [b for b in blks for _ in range(4)],
        [jax.ShapeDtypeStruct(w.shape, F32) for w in ws for _ in range(4)],
        grid=(steps,), semantics=("arbitrary" if exchange.n_in else "parallel",), after=after)
    return [outs[4 * a:4 * a + 4] for a in range(n)], exchanged


def _adamw_gains(gall, ws, ms, vs):
    def body(ga_ref, *refs):
        w, m, v = refs[0:4], refs[4:8], refs[8:12]
        outs, loss_ref, total = refs[12:28], refs[28], refs[29]
        g = ga_ref[0]
        for dev in range(1, 8):
            g = g + ga_ref[dev]
        total[...] = g
        for i in range(4):
            gi = total[i:i + 1, :]
            outs[i][...] = gi
            outs[4 + i][...], outs[8 + i][...], outs[12 + i][...] = _adamw_math(w[i][...], gi, m[i][...], v[i][...])
        loss_ref[...] = total[4:5, 0:128] * (0.5 / D)

    outs = pl.pallas_call(
        body, name="adamw_gains",
        out_shape=[jax.ShapeDtypeStruct((1, D), F32)] * 16 + [jax.ShapeDtypeStruct((1, 128), F32)],
        scratch_shapes=[pltpu.VMEM((8, D), F32)],
    )(gall, *ws, *ms, *vs)
    return outs[0:4], outs[4:8], outs[8:12], outs[12:16], outs[16]


def kernel(x, positions, w_in, w_out, g_pre_mix, g_post_mix, g_pre_ffn, g_post_ffn, w_gate, w_up, w_down, loss_target, m_w_in, m_w_out, m_g_pre_mix, m_g_post_mix, m_g_pre_ffn, m_g_post_ffn, m_w_gate, m_w_up, m_w_down, v_w_in, v_w_out, v_g_pre_mix, v_g_post_mix, v_g_pre_ffn, v_g_post_ffn, v_w_gate, v_w_up, v_w_down):
    tr = lambda t: jnp.swapaxes(t, 1, 2)[0]
    shards = [w_in[0], w_out[0], tr(w_gate), tr(w_up), w_down[0]]
    moms = [m_w_in[0], m_w_out[0], tr(m_w_gate), tr(m_w_up), m_w_down[0]]
    vels = [v_w_in[0], v_w_out[0], tr(v_w_gate), tr(v_w_up), v_w_down[0]]
    xs, pos, tgt = x[0], positions.reshape(S, 1), loss_target[0]
    g1, g2, g3, g4 = g_pre_mix, g_post_mix, g_pre_ffn, g_post_ffn
    tabs = tuple(jnp.asarray(t) for t in _retention_tables())
    ifc, spread = _rotary_tables()
    ifc, spread = jnp.asarray(ifc), jnp.asarray(spread, dtype=BF16)
    bf = list(_cast_bf16(shards[:1]))
    win_gather = _GatherOverIci("win_gather", bf[:1])
    token = win_gather.start(shards[0])
    bf += _cast_bf16(shards[1:], token)
    wout_gather = _GatherOverIci("wout_gather", bf[1:2])
    token = wout_gather.start(token)
    ffn_gather = _GatherOverIci("ffn_gather", bf[2:])
    token = ffn_gather.start(token)
    h1, cos, sin = _prepare(xs, g1, pos, ifc, token)
    win_sh, win_land = win_gather.wait(h1)
    (win_g,) = _exchange_alone("forward_win", _ForwardGathered(bf[:1]), [*win_sh, *win_land])
    qr, kr, rv, rg, aq, ak, av = _proj_fwd(h1, win_g, cos, sin, spread, None)
    wout_sh, wout_land = wout_gather.wait(qr)
    n_ffn = len(bf[2:])
    (att_out, lse, cat_a), (wout_g, *ffn_gather.arrays[n_ffn:]) = _att_fwd(
        aq, ak, av, _Both(_ForwardGathered(bf[1:2]), _ForwardGathered(bf[2:], forward=False)),
        [*wout_sh, *wout_land, *ffn_gather.arrays])
    wout_g = wout_g.reshape(D, D)
    (o_raw, cat_r, states), _ = _ret_fwd(qr, kr, rv, rg, tabs, _NoExchange(), (), cat_a)
    ffn_sh, ffn_lands = ffn_gather.wait(cat_r)
    (mix, x2, h3), (wg_g, wu_g, wd_g) = _mix_fwd(cat_r, cat_a, wout_g, xs, g2, g3,
                                                _ForwardGathered(bf[2:], own=False), [*ffn_sh, *ffn_lands])
    gt, up, a, sq, dy, df, dg4 = _ffn_fwd(h3, wg_g, wu_g, wd_g, x2, tgt, g4)

    dgt, dup, dx2, dmix, dg3, dg2 = _ffn_bwd_act(df, gt, up, wg_g, wu_g, wd_g, dy, x2, mix, g2, g3)
    ffn_grads = list(_ffn_bwd_w(a, df, h3, dgt, dup))
    (dret, datt, dwout), got = _mix_bwd(dmix, cat_r, cat_a, wout_g, _HalvesToSibling(ffn_grads), ffn_grads)
    ffn_sum = _SumOverIci("ffn_sum", _pair_sum(ffn_grads, got))
    token = ffn_sum.start(datt)
    (dq_att, dk_att, dv_att), _ = _att_bwd(aq, ak, av, datt, att_out, lse, _NoExchange(), (), token)
    (dqr, dkr, drv, drg), _ = _ret_bwd(qr, kr, rv, rg, o_raw, states, dret, tabs, _NoExchange(), (), token)
    dproj = _rot_bwd(cos, sin, spread, dqr, dkr, drv, drg, dq_att, dk_att, dv_att)
    sums = _chip_sum(*ffn_sum.wait(dproj))
    dwout = dwout.reshape(N_CHIP, WOUT_R, D)
    (dwin, got_win), (*ffn_full, got_wout) = _win_bwd_w(
        h1, dproj, _Both(_ShareHalves(sums), _HalvesToSibling([dwout])), [*sums, dwout])

    in_sum = _SumOverIci("in_sum", _pair_sum([dwin, dwout], [got_win, got_wout]))
    token = in_sum.start(dproj)
    dx, gblock = _in_bwd(dproj, win_g, xs, dx2, g1, [dg2, dg3, dg4, sq], token)
    ffn_upd, (gall,) = _adamw(shards[2:], [ffn_full[o] for o in (1, 2, 0)],
                              moms[2:], vels[2:], token, _GatherBlocks(gblock), [gblock])
    pre, parts = in_sum.wait(ffn_upd[2][0])
    in_full = _chip_sum(pre, parts, share=True)
    upd = _adamw(shards[:2], in_full, moms[:2], vels[:2])[0] + ffn_upd
    gg, gd, gm, gv, loss_row = _adamw_gains(gall, [g1, g2, g3, g4],
                                            [m_g_pre_mix, m_g_post_mix, m_g_pre_ffn, m_g_post_ffn],
                                            [v_g_pre_mix, v_g_post_mix, v_g_pre_ffn, v_g_post_ffn])

    def order(mats, vecs):
        back = lambda t: jnp.swapaxes(t[None], 1, 2)
        return [mats[0][None], mats[1][None], *vecs, back(mats[2]), back(mats[3]), mats[4][None]]

    return (loss_row[0, 0], dx[None],
            *order([u[0] for u in upd], gg),
            *order([u[1] for u in upd], gd),
            *order([u[2] for u in upd], gm),
            *order([u[3] for u in upd], gv))
```

```python
import numpy as np
import jax
import jax.numpy as jnp
from jax import lax
from jax.experimental import pallas as pl
from jax.experimental.pallas import tpu as pltpu

F32, BF16 = jnp.float32, jnp.bfloat16
MESH = pl.DeviceIdType.MESH

S = 2048
D = 1024
PW = 3072
N_CHIP = 4
WIN_C = PW // N_CHIP
DFF = 2816
FF_C = DFF // N_CHIP
WOUT_R = D // N_CHIP
RMS_EPS = 1e-6
GN_EPS = 1e-5
RET_C = 128
RET_PER_STEP = 4
RET_SCALE = 32 ** -0.5
ATT_BLK = 128
ATT_SCALE = 64 ** -0.5
PATTERN_DILATIONS = (16, 1, 4)
NEG = -1e30
VMEM_LIMIT = 56 * 1024 * 1024

ADAM_LR, ADAM_B1, ADAM_B2, ADAM_EPS, ADAM_WD, ADAM_STEP = 0.001, 0.9, 0.999, 1e-08, 0.01, 10


def _params(*sem):
    return pltpu.CompilerParams(dimension_semantics=sem, vmem_limit_bytes=VMEM_LIMIT)


def _nt(a, b):
    return lax.dot_general(a, b, (((1,), (1,)), ((), ())), preferred_element_type=F32)


def _tn(a, b):
    return lax.dot_general(a, b, (((0,), (0,)), ((), ())), preferred_element_type=F32)


def _nn(a, b):
    return jnp.dot(a, b, preferred_element_type=F32)


def _rstd(v):
    return lax.rsqrt(jnp.mean(v * v, axis=-1, keepdims=True) + RMS_EPS)


def _sigmoid(v):
    return 1.0 / (1.0 + jnp.exp(-v))


def _rows(i, t):
    return pl.ds(pl.multiple_of(i * t, t), t)


def _retention_tables():
    h = np.arange(8, dtype=np.float32)
    log_g = np.log1p(-np.exp2(-5.0 - h)).astype(np.float32)
    idx = np.arange(RET_C, dtype=np.float32)
    diff = idx[:, None] - idx[None, :]
    dtab = np.where(diff >= 0, np.exp(log_g[:, None, None] * np.maximum(diff, 0.0)), 0.0).astype(np.float32)
    dtab = dtab.reshape(8 * RET_C, RET_C)
    lane_head = np.arange(256) // 32
    a_tab = np.exp(log_g[lane_head][None, :] * (idx + 1.0)[:, None]).astype(np.float32)
    b_tab = np.exp(log_g[lane_head][None, :] * (RET_C - 1.0 - idx)[:, None]).astype(np.float32)
    lam = np.exp(log_g[lane_head] * RET_C).astype(np.float32)[:, None]
    bd = (lane_head[:, None] == (np.arange(512) // 64)[None, :]).astype(np.float32)
    return dtab, a_tab, b_tab, lam, bd


def _rotary_tables():
    inv_r = (1.0 / (np.float32(10000.0) ** np.linspace(0.0, 1.0, 16, dtype=np.float32))).astype(np.float32)
    inv_a = (np.float32(500000.0) ** (-np.arange(0, 16, 2, dtype=np.float32) / np.float32(16))).astype(np.float32)
    ifc = np.zeros((1, 128), np.float32)
    ifc[0, 0:16], ifc[0, 16:24] = inv_r, inv_a
    spread = np.zeros((128, 768), np.float32)
    for lane in range(256):
        spread[(lane % 32) % 16, lane] = 1.0
    for lane in range(512):
        d = lane % 64
        spread[16 + d % 8 if d < 16 else 24, 256 + lane] = 1.0
    return ifc, spread


def _rot_halves(tm):
    lo_r = (lax.broadcasted_iota(jnp.int32, (tm, 256), 1) % 32) < 16
    lo_a = (lax.broadcasted_iota(jnp.int32, (tm, 512), 1) % 64) < 8
    return lo_r, lo_a


def _spread_exact(t, e):
    hi = t.astype(BF16)
    r1 = t - hi.astype(F32)
    mid = r1.astype(BF16)
    lo = (r1 - mid.astype(F32)).astype(BF16)
    return _nn(hi, e) + _nn(mid, e) + _nn(lo, e)


def _rot_tables(cos_ref, sin_ref, e_ref):
    cs = _spread_exact(cos_ref[...], e_ref[...])
    sn = _spread_exact(sin_ref[...], e_ref[...])
    return cs[:, 0:256], cs[:, 256:768], sn[:, 0:256], sn[:, 256:768]


def _proj_fwd(h1, win_g, cos, sin, spread, after):
    tm = 256

    def body(h_ref, w_ref, cos_ref, sin_ref, e_ref, qr_ref, kr_ref, rv_ref, rg_ref, aq_ref, ak_ref, av_ref, p_ref, _):
        h = h_ref[...]
        for k in range(N_CHIP):
            p_ref[:, k * WIN_C:(k + 1) * WIN_C] = _nn(h, w_ref[k])
        cr, ca, sr, sa = _rot_tables(cos_ref, sin_ref, e_ref)
        lo_r, lo_a = _rot_halves(tm)

        def rot_r(v):
            return v * cr + sr * jnp.where(lo_r, -pltpu.roll(v, 240, 1), pltpu.roll(v, 16, 1))

        def rot_a(v):
            return v * ca + sa * jnp.where(lo_a, -pltpu.roll(v, 504, 1), pltpu.roll(v, 8, 1))

        qr_ref[...] = rot_r(p_ref[:, 0:256]).astype(BF16)
        kr_ref[...] = (rot_r(p_ref[:, 256:512]) * RET_SCALE).astype(BF16)
        rv_ref[...] = p_ref[:, 512:1024].astype(BF16)
        rg_ref[...] = p_ref[:, 1024:1536]
        aq, ak = rot_a(p_ref[:, 1536:2048]), rot_a(p_ref[:, 2048:2560])
        for j in range(4):
            aq_ref[j] = aq[:, 128 * j:128 * j + 128]
            ak_ref[j] = ak[:, 128 * j:128 * j + 128]
            av_ref[j] = p_ref[:, 2560 + 128 * j:2560 + 128 * j + 128]

    row = lambda w: pl.BlockSpec((tm, w), lambda i: (i, 0))
    slab = pl.BlockSpec((4, tm, 128), lambda i: (0, i, 0))
    return _carry(
        "proj_fwd", body, _NoExchange(), (), (h1, win_g, cos, sin, spread),
        [row(D), pl.BlockSpec((N_CHIP, D, WIN_C), lambda i: (0, 0, 0)), row(128), row(128),
         pl.BlockSpec((128, 768), lambda i: (0, 0))],
        [row(256), row(256), row(512), row(512), slab, slab, slab],
        [jax.ShapeDtypeStruct((S, w), BF16) for w in (256, 256, 512)]
        + [jax.ShapeDtypeStruct((S, 512), F32)] + [jax.ShapeDtypeStruct((4, S, 128), F32)] * 3,
        scratch_shapes=[pltpu.VMEM((tm, PW), F32)], grid=(S // tm,), semantics=("parallel",), after=after)[0]


def _seg_mean(v):
    lo = lax.broadcasted_iota(jnp.int32, v.shape, 1) < 64
    s_lo = jnp.sum(jnp.where(lo, v, 0.0), axis=-1, keepdims=True)
    s_hi = jnp.sum(jnp.where(lo, 0.0, v), axis=-1, keepdims=True)
    return jnp.where(lo, s_lo, s_hi) * (1.0 / 64.0)


def _ret_fwd(qr, kr, rv, proj, tabs, exchange, exchange_args, after=None):
    C, G = RET_C, RET_PER_STEP
    steps = S // (C * G)
    dtab, a_tab, b_tab, lam, bd = tabs

    def body(q_ref, k_ref, v_ref, g_ref, dt_ref, a_ref, b_ref, lam_ref, bd_ref, o_ref, cat_ref, st_ref, R, exch):
        @pl.when(pl.program_id(0) == 0)
        def _():
            exch.start()
            R[...] = jnp.zeros_like(R)

        lane_head = lax.broadcasted_iota(jnp.int32, (C, 256), 1) // 32
        col_head = lax.broadcasted_iota(jnp.int32, (C, 256), 1) // 64
        for s in range(G):
            rows = slice(s * C, (s + 1) * C)
            q, k, v = q_ref[rows, :], k_ref[rows, :], v_ref[rows, :]
            rb = R[...].astype(BF16)
            st_ref[s] = rb
            qa = (q.astype(F32) * a_ref[...]).astype(BF16)
            cross = _nn(qa, rb)
            p = (_nt(_stack_heads(q, lane_head, n=8), k) * dt_ref[...]).astype(BF16)
            og = [cross[:, 256 * g:256 * g + 256]
                  + _unstack_heads(_nn(p[4 * C * g:4 * C * (g + 1)], v[:, 256 * g:256 * g + 256]), col_head)
                  for g in range(2)]
            kb = (k.astype(F32) * b_ref[...]).astype(BF16)
            R[...] = R[...] * lam_ref[...] + _tn(kb, v) * bd_ref[...]
            o_ref[rows, 0:256] = og[0]
            o_ref[rows, 256:512] = og[1]
            for j in range(4):
                oj = og[j // 2][:, 128 * (j % 2):128 * (j % 2) + 128]
                xc = oj - _seg_mean(oj)
                rn = xc * lax.rsqrt(_seg_mean(xc * xc) + GN_EPS)
                gj = g_ref[rows, 128 * j:128 * j + 128]
                cat_ref[rows, 128 * j:128 * j + 128] = (rn * (gj * _sigmoid(gj))).astype(BF16)

        @pl.when(pl.program_id(0) == steps - 1)
        def _():
            exch.middle()
            exch.finish()

    row = lambda w: pl.BlockSpec((C * G, w), lambda n: (n, 0))
    full = lambda a: pl.BlockSpec(a.shape, lambda n: (0,) * a.ndim)
    return _carry(
        "ret_fwd", body, exchange, exchange_args, (qr, kr, rv, proj, dtab, a_tab, b_tab, lam, bd),
        [row(256), row(256), row(512), row(512),
         full(dtab), full(a_tab), full(b_tab), full(lam), full(bd)],
        [row(512), row(512), pl.BlockSpec((G, 256, 512), lambda n: (n, 0, 0))],
        [jax.ShapeDtypeStruct((S, 512), F32), jax.ShapeDtypeStruct((S, 512), BF16),
         jax.ShapeDtypeStruct((S // C, 256, 512), BF16)],
        scratch_shapes=[pltpu.VMEM((256, 512), F32)], grid=(steps,), semantics=("arbitrary",), after=after)


def _stack_heads(v, lane_head, fill=0.0, n=4):
    return jnp.concatenate([jnp.where(lane_head == h, v, jnp.full_like(v, fill)) for h in range(n)], axis=0)


def _unstack_heads(v, lane_head, n=4):
    out = v[0:ATT_BLK]
    for h in range(1, n):
        out = jnp.where(lane_head == h, v[h * ATT_BLK:(h + 1) * ATT_BLK], out)
    return out


def _att_bias(has_prev):
    nk = 2 * ATT_BLK if has_prev else ATT_BLK
    a = lax.broadcasted_iota(jnp.int32, (4 * ATT_BLK, nk), 0) % ATT_BLK
    kk = lax.broadcasted_iota(jnp.int32, (4 * ATT_BLK, nk), 1)
    if not has_prev:
        return None, jnp.where((a - kk) >= 0, 0.0, NEG)
    dist = ATT_BLK + a - kk
    inside = (dist >= 0) & (dist <= ATT_BLK)
    return jnp.where(inside, 0.0, NEG), jnp.where(inside & (kk >= ATT_BLK), 0.0, NEG)


def _class_rows(ib, r, d):
    if d == 1:
        return pl.ds(pl.multiple_of(ib * ATT_BLK, ATT_BLK), ATT_BLK)
    return pl.ds(ib * ATT_BLK * d + r, ATT_BLK, stride=d)


def _slab_pair(ref, g, rows):
    return jnp.concatenate([ref[2 * g, rows, :], ref[2 * g + 1, rows, :]], axis=1)


def _att_blocks(d):
    nb = S // d // ATT_BLK
    return nb, nb > 1


def _att_fwd(aq, ak, av, exchange, exchange_args):
    def body(q_ref, k_ref, v_ref, o_ref, l_ref, cat_ref, xc):
        xc.start()
        lane_head = lax.broadcasted_iota(jnp.int32, (ATT_BLK, 256), 1) // 64
        for pi, d in enumerate(PATTERN_DILATIONS):
            if pi == len(PATTERN_DILATIONS) - 1:
                xc.middle()
            nb, has_prev = _att_blocks(d)
            bias_rest, bias_first = _att_bias(has_prev)

            def block(b, carry, pi=pi, d=d, nb=nb, has_prev=has_prev, bias_rest=bias_rest, bias_first=bias_first):
                r, ib = b // nb, b % nb
                rows = _class_rows(ib, r, d)
                prow = _class_rows(jnp.maximum(ib - 1, 0), r, d)
                bias = jnp.where(ib == 0, bias_first, bias_rest) if has_prev else bias_first
                for g in range(2):
                    qg = _slab_pair(q_ref, g, rows).astype(BF16)
                    kg = _slab_pair(k_ref, g, rows)
                    vg = _slab_pair(v_ref, g, rows)
                    if has_prev:
                        kg = jnp.concatenate([_slab_pair(k_ref, g, prow), kg], axis=0)
                        vg = jnp.concatenate([_slab_pair(v_ref, g, prow), vg], axis=0)
                    kg, vg = kg.astype(BF16), vg.astype(BF16)
                    s = _nt(_stack_heads(qg, lane_head), kg) * ATT_SCALE + bias
                    m = jnp.max(s, axis=-1, keepdims=True)
                    p = jnp.exp(s - m)
                    den = jnp.sum(p, axis=-1, keepdims=True)
                    og = _unstack_heads(_nn(p.astype(BF16), vg) / den, lane_head)
                    lg = _unstack_heads(jnp.broadcast_to(m + jnp.log(den), (4 * ATT_BLK, 256)), lane_head)
                    for jj in range(2):
                        j = 2 * g + jj
                        o_new, l_new = og[:, 128 * jj:128 * jj + 128], lg[:, 128 * jj:128 * jj + 128]
                        if pi > 0:
                            o_old, l_old = o_ref[j, rows, :], l_ref[j, rows, :]
                            mx = jnp.maximum(l_old, l_new)
                            ea, eb = jnp.exp(l_old - mx), jnp.exp(l_new - mx)
                            den = ea + eb
                            o_new = (ea * o_old + eb * o_new) / den
                            l_new = mx + jnp.log(den)
                        o_ref[j, rows, :] = o_new
                        l_ref[j, rows, :] = l_new
                return carry

            lax.fori_loop(0, S // ATT_BLK, block, 0, unroll=4)

        def to_cat(i, carry):
            rows = _rows(i, 256)
            for j in range(4):
                cat_ref[rows, 128 * j:128 * j + 128] = o_ref[j, rows, :].astype(BF16)
            return carry

        lax.fori_loop(0, S // 256, to_cat, 0)
        xc.finish()

    slab = jax.ShapeDtypeStruct((4, S, 128), F32)
    return _carry("att_fwd", body, exchange, exchange_args, (aq, ak, av), [VMEM] * 3, [VMEM] * 3,
                  [slab, slab, jax.ShapeDtypeStruct((S, 512), BF16)])


def _mix_fwd(cat_r, cat_a, wout, x, g2, g3, exchange, exchange_args):
    tm = 512

    def body(cr_ref, ca_ref, w_ref, x_ref, g2_ref, g3_ref, mix_ref, x2_ref, h3_ref, xc):
        @pl.when(pl.program_id(0) == 0)
        def _():
            xc.start()

        mix = _nn(cr_ref[...], w_ref[0:512, :]) + _nn(ca_ref[...], w_ref[512:1024, :])
        mix_ref[...] = mix
        x2 = x_ref[...] + mix * _rstd(mix) * g2_ref[...]
        x2_ref[...] = x2
        h3_ref[...] = (x2 * _rstd(x2) * g3_ref[...]).astype(BF16)

        @pl.when(pl.program_id(0) == S // tm - 1)
        def _():
            xc.middle()
            xc.finish()

    row = lambda w: pl.BlockSpec((tm, w), lambda i: (i, 0))
    vec = pl.BlockSpec((1, D), lambda i: (0, 0))
    return _carry("mix_fwd", body, exchange, exchange_args, (cat_r, cat_a, wout, x, g2, g3),
                  [row(512), row(512), pl.BlockSpec((D, D), lambda i: (0, 0)), row(D), vec, vec],
                  [row(D), row(D), row(D)],
                  [jax.ShapeDtypeStruct((S, D), F32), jax.ShapeDtypeStruct((S, D), F32),
                   jax.ShapeDtypeStruct((S, D), BF16)],
                  grid=(S // tm,), semantics=("arbitrary",))


def _ffn_fwd(h3, wg, wu, wd, x2, tgt, g4):
    tm = 512
    last = N_CHIP - 1

    def body(h_ref, wg_ref, wu_ref, wd_ref, x2_ref, t_ref, g_ref,
             gt_ref, up_ref, a_ref, loss_ref, dy_ref, df_ref, dg_ref, f_ref):
        k, i = pl.program_id(0), pl.program_id(1)
        h = h_ref[...]
        gt = _nt(h, wg_ref[...])
        up = _nt(h, wu_ref[...])
        gt_ref[...] = gt.astype(BF16)
        up_ref[...] = up.astype(BF16)
        a = (gt * _sigmoid(gt) * up).astype(BF16)
        a_ref[...] = a
        part = _nn(a, wd_ref[...])
        rows = _rows(i, tm)

        @pl.when(k == 0)
        def _():
            f_ref[rows, :] = part

        @pl.when((k > 0) & (k < last))
        def _():
            f_ref[rows, :] = f_ref[rows, :] + part

        @pl.when((k == last) & (i == 0))
        def _():
            loss_ref[...] = jnp.zeros_like(loss_ref)
            dg_ref[...] = jnp.zeros_like(dg_ref)

        @pl.when(k == last)
        def _():
            fv = f_ref[rows, :] + part
            r = _rstd(fv)
            fn = fv * r
            e = x2_ref[...] + fn * g_ref[...] - t_ref[...]
            loss_ref[...] = loss_ref[...] + jnp.sum(jnp.sum(e * e, axis=-1, keepdims=True), axis=0, keepdims=True)
            dy = e * (1.0 / D)
            dy_ref[...] = dy
            dg_ref[...] = dg_ref[...] + jnp.sum(dy * fn, axis=0, keepdims=True)
            t = dy * g_ref[...]
            df_ref[...] = (r * (t - fn * jnp.mean(t * fn, axis=-1, keepdims=True))).astype(BF16)

    wrow = pl.BlockSpec((None, FF_C, D), lambda k, i: (k, 0, 0))
    act = pl.BlockSpec((None, tm, FF_C), lambda k, i: (k, i, 0))
    late = pl.BlockSpec((tm, D), lambda k, i: (jnp.where(k == last, i, 0), 0))
    vec = pl.BlockSpec((1, D), lambda k, i: (0, 0))
    return pl.pallas_call(
        body, grid=(N_CHIP, S // tm), name="ffn_fwd",
        in_specs=[pl.BlockSpec((tm, D), lambda k, i: (i, 0)), wrow, wrow, wrow, late, late, vec],
        out_specs=[act, act, act, vec, late, late, vec],
        out_shape=[jax.ShapeDtypeStruct((N_CHIP, S, FF_C), BF16)] * 3
                  + [jax.ShapeDtypeStruct((1, D), F32), jax.ShapeDtypeStruct((S, D), F32),
                     jax.ShapeDtypeStruct((S, D), BF16), jax.ShapeDtypeStruct((1, D), F32)],
        scratch_shapes=[pltpu.VMEM((S, D), F32)],
        compiler_params=_params("arbitrary", "arbitrary"),
    )(h3, wg, wu, wd, x2, tgt, g4)


def _ffn_bwd_act(df, gt, up, wg, wu, wd, dy, x2, mix, g2, g3):
    tm, sub = 512, 256
    last = N_CHIP - 1

    def body(df_ref, gt_ref, up_ref, wg_ref, wu_ref, wd_ref, dy_ref, x2_ref, mix_ref, g2_ref, g3_ref,
             dgt_ref, dup_ref, dx2_ref, dmix_ref, dg3_ref, dg2_ref, dh_ref):
        k, i = pl.program_id(0), pl.program_id(1)
        parts = []
        for s in range(tm // sub):
            rows = slice(s * sub, (s + 1) * sub)
            da = _nt(df_ref[rows, :], wd_ref[...])
            gt, up = gt_ref[rows, :].astype(F32), up_ref[rows, :].astype(F32)
            sg = _sigmoid(gt)
            dup = (da * gt * sg).astype(BF16)
            dgt = (da * up * (sg * (1.0 + gt * (1.0 - sg)))).astype(BF16)
            dup_ref[rows, :] = dup
            dgt_ref[rows, :] = dgt
            parts.append(_nn(dgt, wg_ref[...]) + _nn(dup, wu_ref[...]))
        part = jnp.concatenate(parts, axis=0)
        rows = _rows(i, tm)

        @pl.when(k == 0)
        def _():
            dh_ref[rows, :] = part

        @pl.when((k > 0) & (k < last))
        def _():
            dh_ref[rows, :] = dh_ref[rows, :] + part

        @pl.when((k == last) & (i == 0))
        def _():
            dg3_ref[...] = jnp.zeros_like(dg3_ref)
            dg2_ref[...] = jnp.zeros_like(dg2_ref)

        @pl.when(k == last)
        def _():
            dh = dh_ref[rows, :] + part
            x2 = x2_ref[...]
            r3 = _rstd(x2)
            xn = x2 * r3
            dg3_ref[...] = dg3_ref[...] + jnp.sum(dh * xn, axis=0, keepdims=True)
            t = dh * g3_ref[...]
            dx2 = dy_ref[...] + r3 * (t - xn * jnp.mean(t * xn, axis=-1, keepdims=True))
            dx2_ref[...] = dx2
            mix = mix_ref[...]
            r2 = _rstd(mix)
            mn = mix * r2
            dg2_ref[...] = dg2_ref[...] + jnp.sum(dx2 * mn, axis=0, keepdims=True)
            u = dx2 * g2_ref[...]
            dmix_ref[...] = (r2 * (u - mn * jnp.mean(u * mn, axis=-1, keepdims=True))).astype(BF16)

    wrow = pl.BlockSpec((None, FF_C, D), lambda k, i: (k, 0, 0))
    act = pl.BlockSpec((None, tm, FF_C), lambda k, i: (k, i, 0))
    row = pl.BlockSpec((tm, D), lambda k, i: (i, 0))
    late = pl.BlockSpec((tm, D), lambda k, i: (jnp.where(k == last, i, 0), 0))
    vec = pl.BlockSpec((1, D), lambda k, i: (0, 0))
    return pl.pallas_call(
        body, grid=(N_CHIP, S // tm), name="ffn_bwd_act",
        in_specs=[row, act, act, wrow, wrow, wrow, late, late, late, vec, vec],
        out_specs=[act, act, late, late, vec, vec],
        out_shape=[jax.ShapeDtypeStruct((N_CHIP, S, FF_C), BF16), jax.ShapeDtypeStruct((N_CHIP, S, FF_C), BF16),
                   jax.ShapeDtypeStruct((S, D), F32), jax.ShapeDtypeStruct((S, D), BF16),
                   jax.ShapeDtypeStruct((1, D), F32), jax.ShapeDtypeStruct((1, D), F32)],
        scratch_shapes=[pltpu.VMEM((S, D), F32)],
        compiler_params=_params("arbitrary", "arbitrary"),
    )(df, gt, up, wg, wu, wd, dy, x2, mix, g2, g3)


def _ffn_bwd_w(a, df, h3, dgt, dup):
    tm = 1024
    assert S // tm == 2

    def body(a_ref, df_ref, h_ref, dgt_ref, dup_ref, dwd_ref, dwg_ref, dwu_ref, acc_d, acc_g, acc_u):
        i = pl.program_id(1)
        h = h_ref[...]
        parts = (_tn(a_ref[...], df_ref[...]), _tn(dgt_ref[...], h), _tn(dup_ref[...], h))

        @pl.when(i == 0)
        def _():
            for acc, part in zip((acc_d, acc_g, acc_u), parts):
                acc[...] = part

        @pl.when(i == S // tm - 1)
        def _():
            for out, acc, part in zip((dwd_ref, dwg_ref, dwu_ref), (acc_d, acc_g, acc_u), parts):
                out[...] = (acc[...] + part).astype(BF16)

    act = pl.BlockSpec((None, tm, FF_C), lambda k, i: (k, i, 0))
    row = pl.BlockSpec((tm, D), lambda k, i: (i, 0))
    wrow = pl.BlockSpec((None, FF_C, D), lambda k, i: (k, 0, 0))
    return pl.pallas_call(
        body, grid=(N_CHIP, S // tm), name="ffn_bwd_w",
        in_specs=[act, row, row, act, act],
        out_specs=[wrow, wrow, wrow],
        out_shape=[jax.ShapeDtypeStruct((N_CHIP, FF_C, D), BF16)] * 3,
        scratch_shapes=[pltpu.VMEM((FF_C, D), F32)] * 3,
        compiler_params=_params("parallel", "arbitrary"),
    )(a, df, h3, dgt, dup)


def _mix_bwd(dmix, cat_r, cat_a, wout, exchange, exchange_args):
    tm = 1024

    def body(dm_ref, cr_ref, ca_ref, w_ref, dret_ref, datt_ref, dw_ref, acc, xc):
        i = pl.program_id(0)

        @pl.when(i == 0)
        def _():
            xc.start()
            acc[...] = jnp.zeros_like(acc)

        dm = dm_ref[...]
        dret_ref[...] = _nt(dm, w_ref[0:512, :])
        datt = _nt(dm, w_ref[512:1024, :])
        for j in range(4):
            datt_ref[j] = datt[:, 128 * j:128 * j + 128]
        acc[0:512, :] += _tn(cr_ref[...], dm)
        acc[512:1024, :] += _tn(ca_ref[...], dm)

        @pl.when(i == S // tm - 1)
        def _():
            dw_ref[...] = acc[...].astype(BF16)
            xc.middle()
            xc.finish()

    row = lambda w: pl.BlockSpec((tm, w), lambda i: (i, 0))
    full = pl.BlockSpec((D, D), lambda i: (0, 0))
    return _carry("mix_bwd", body, exchange, exchange_args, (dmix, cat_r, cat_a, wout),
                  [row(D), row(512), row(512), full],
                  [row(512), pl.BlockSpec((4, tm, 128), lambda i: (0, i, 0)), full],
                  [jax.ShapeDtypeStruct((S, 512), F32), jax.ShapeDtypeStruct((4, S, 128), F32),
                   jax.ShapeDtypeStruct((D, D), BF16)],
                  scratch_shapes=[pltpu.VMEM((D, D), F32)], grid=(S // tm,), semantics=("arbitrary",))


def _att_bwd(aq, ak, av, datt, att_out, lse, exchange, exchange_args, after=None):
    def body(q_ref, k_ref, v_ref, do_ref, out_ref, l_ref, dq_ref, dk_ref, dv_ref, xc):
        xc.start()

        lane_head = lax.broadcasted_iota(jnp.int32, (ATT_BLK, 256), 1) // 64
        for pi, d in enumerate(PATTERN_DILATIONS):
            nb, has_prev = _att_blocks(d)
            assert pi > 0 or not has_prev
            bias_rest, bias_first = _att_bias(has_prev)

            def block(b, carry, pi=pi, d=d, nb=nb, has_prev=has_prev, bias_rest=bias_rest, bias_first=bias_first):
                r, ib = b // nb, b % nb
                rows = _class_rows(ib, r, d)
                prow = _class_rows(jnp.maximum(ib - 1, 0), r, d)
                bias = jnp.where(ib == 0, bias_first, bias_rest) if has_prev else bias_first
                for g in range(2):
                    qg = _slab_pair(q_ref, g, rows).astype(BF16)
                    kg = _slab_pair(k_ref, g, rows)
                    vg = _slab_pair(v_ref, g, rows)
                    if has_prev:
                        kg = jnp.concatenate([_slab_pair(k_ref, g, prow), kg], axis=0)
                        vg = jnp.concatenate([_slab_pair(v_ref, g, prow), vg], axis=0)
                    kg, vg = kg.astype(BF16), vg.astype(BF16)
                    dog = _slab_pair(do_ref, g, rows)
                    outg = _slab_pair(out_ref, g, rows)
                    lg = _slab_pair(l_ref, g, rows)
                    qs = _stack_heads(qg, lane_head)
                    dos = _stack_heads(dog, lane_head)
                    delta = jnp.sum(dos * jnp.concatenate([outg] * 4, axis=0), axis=-1, keepdims=True)
                    lh = jnp.max(_stack_heads(lg, lane_head, NEG), axis=-1, keepdims=True)
                    s = _nt(qs, kg) * ATT_SCALE + bias
                    p = jnp.exp(s - lh)
                    dosb = dos.astype(BF16)
                    ds = (p * (_nt(dosb, vg) - delta) * ATT_SCALE).astype(BF16)
                    dq = _unstack_heads(_nn(ds, kg), lane_head)
                    dk = _tn(ds, qs)
                    dv = _tn(p.astype(BF16), dosb)
                    for jj in range(2):
                        j, sl = 2 * g + jj, slice(128 * jj, 128 * jj + 128)
                        if pi == 0:
                            dq_ref[j, rows, :] = dq[:, sl]
                            dk_ref[j, rows, :] = dk[:, sl]
                            dv_ref[j, rows, :] = dv[:, sl]
                            continue
                        dq_ref[j, rows, :] += dq[:, sl]
                        if has_prev:
                            dk_ref[j, prow, :] += dk[0:ATT_BLK, sl]
                            dv_ref[j, prow, :] += dv[0:ATT_BLK, sl]
                            dk_ref[j, rows, :] += dk[ATT_BLK:2 * ATT_BLK, sl]
                            dv_ref[j, rows, :] += dv[ATT_BLK:2 * ATT_BLK, sl]
                        else:
                            dk_ref[j, rows, :] += dk[:, sl]
                            dv_ref[j, rows, :] += dv[:, sl]
                return carry

            lax.fori_loop(0, S // ATT_BLK, block, 0, unroll=4)
        xc.middle()
        xc.finish()

    slab = jax.ShapeDtypeStruct((4, S, 128), F32)
    return _carry("att_bwd", body, exchange, exchange_args, (aq, ak, av, datt, att_out, lse), [VMEM] * 6, [VMEM] * 3,
                  [slab, slab, slab], after=after)


def _ret_bwd(qr, kr, rv, proj, o_raw, states, dret, tabs, exchange, exchange_args, after=None):
    C, G = RET_C, RET_PER_STEP
    steps = S // (C * G)
    dtab, a_tab, b_tab, lam, bd = tabs

    def body(q_ref, k_ref, v_ref, g_ref, o_ref, st_ref, dr_ref, dt_ref, a_ref, b_ref, lam_ref, bd_ref,
             dq_ref, dk_ref, dv_ref, dg_ref, dR, exch):
        @pl.when(pl.program_id(0) == 0)
        def _():
            exch.start()
            dR[...] = jnp.zeros_like(dR)

        lane_head = lax.broadcasted_iota(jnp.int32, (C, 256), 1) // 32
        col_head = lax.broadcasted_iota(jnp.int32, (C, 256), 1) // 64
        for s in reversed(range(G)):
            rows = slice(s * C, (s + 1) * C)
            q, k, v = q_ref[rows, :], k_ref[rows, :], v_ref[rows, :]
            dos = []
            for j in range(4):
                sl = slice(128 * j, 128 * j + 128)
                oj = o_ref[rows, sl]
                xc = oj - _seg_mean(oj)
                rs = lax.rsqrt(_seg_mean(xc * xc) + GN_EPS)
                rn = xc * rs
                gj = g_ref[rows, sl]
                sg = _sigmoid(gj)
                dret = dr_ref[rows, sl]
                dg_ref[rows, sl] = dret * rn * (sg * (1.0 + gj * (1.0 - sg)))
                drn = dret * (gj * sg)
                dos.append(rs * (drn - _seg_mean(drn) - rn * _seg_mean(drn * rn)))
            do = [jnp.concatenate(dos[0:2], axis=1), jnp.concatenate(dos[2:4], axis=1)]
            do8 = jnp.concatenate(do, axis=1).astype(BF16)
            drb = dR[...].astype(BF16)
            rb = st_ref[s]
            dq = _nt(do8, rb) * a_ref[...]
            dk = _nt(v, drb) * b_ref[...]
            kb = (k.astype(F32) * b_ref[...]).astype(BF16)
            dvall = _nn(kb, drb)
            qs = _stack_heads(q, lane_head, n=8)
            dec = dt_ref[...]
            p = (_nt(qs, k) * dec).astype(BF16)
            dos = [_stack_heads(do[g], col_head).astype(BF16) for g in range(2)]
            dp = jnp.concatenate([_nt(dos[g], v[:, 256 * g:256 * g + 256]) for g in range(2)], axis=0)
            ds = (dp * dec).astype(BF16)
            dq = dq + _unstack_heads(_nn(ds, k), lane_head, n=8)
            dk = dk + _tn(ds, qs)
            dv = [dvall[:, 256 * g:256 * g + 256] + _tn(p[4 * C * g:4 * C * (g + 1)], dos[g]) for g in range(2)]
            qa = (q.astype(F32) * a_ref[...]).astype(BF16)
            dR[...] = dR[...] * lam_ref[...] + _tn(qa, do8) * bd_ref[...]
            dq_ref[rows, :] = dq
            dk_ref[rows, :] = dk
            dv_ref[rows, 0:256] = dv[0]
            dv_ref[rows, 256:512] = dv[1]

        @pl.when(pl.program_id(0) == steps - 1)
        def _():
            exch.middle()
            exch.finish()

    rev = lambda w: pl.BlockSpec((C * G, w), lambda n: (steps - 1 - n, 0))
    full = lambda a: pl.BlockSpec(a.shape, lambda n: (0,) * a.ndim)
    return _carry(
        "ret_bwd", body, exchange, exchange_args, (qr, kr, rv, proj, o_raw, states, dret, dtab, a_tab, b_tab, lam, bd),
        [rev(256), rev(256), rev(512), rev(512), rev(512),
         pl.BlockSpec((G, 256, 512), lambda n: (steps - 1 - n, 0, 0)), rev(512),
         full(dtab), full(a_tab), full(b_tab), full(lam), full(bd)],
        [rev(256), rev(256), rev(512), rev(512)],
        [jax.ShapeDtypeStruct((S, 256), F32), jax.ShapeDtypeStruct((S, 256), F32),
         jax.ShapeDtypeStruct((S, 512), F32), jax.ShapeDtypeStruct((S, 512), F32)],
        scratch_shapes=[pltpu.VMEM((256, 512), F32)], grid=(steps,), semantics=("arbitrary",), after=after)


def _rot_bwd(cos, sin, spread, dqr, dkr, drv, drg, dq_att, dk_att, dv_att):
    tm = 256

    def body(cos_ref, sin_ref, e_ref, dqr_ref, dkr_ref, drv_ref, drg_ref, dqa_ref, dka_ref, dva_ref, dp_ref):
        cr, ca, sr, sa = _rot_tables(cos_ref, sin_ref, e_ref)
        lo_r, lo_a = _rot_halves(tm)

        def unrot_r(g):
            gs = g * sr
            return g * cr + pltpu.roll(jnp.where(lo_r, -gs, 0.0), 16, 1) + pltpu.roll(jnp.where(lo_r, 0.0, gs), 240, 1)

        def unrot_a(g):
            gs = g * sa
            return g * ca + pltpu.roll(jnp.where(lo_a, -gs, 0.0), 8, 1) + pltpu.roll(jnp.where(lo_a, 0.0, gs), 504, 1)

        def wide(ref):
            return jnp.concatenate([ref[j] for j in range(4)], axis=1)

        dp_ref[:, 0:256] = unrot_r(dqr_ref[...]).astype(BF16)
        dp_ref[:, 256:512] = unrot_r(dkr_ref[...] * RET_SCALE).astype(BF16)
        dp_ref[:, 512:1024] = drv_ref[...].astype(BF16)
        dp_ref[:, 1024:1536] = drg_ref[...].astype(BF16)
        dp_ref[:, 1536:2048] = unrot_a(wide(dqa_ref)).astype(BF16)
        dp_ref[:, 2048:2560] = unrot_a(wide(dka_ref)).astype(BF16)
        dp_ref[:, 2560:3072] = wide(dva_ref).astype(BF16)

    row = lambda w: pl.BlockSpec((tm, w), lambda i: (i, 0))
    slab = pl.BlockSpec((4, tm, 128), lambda i: (0, i, 0))
    return pl.pallas_call(
        body, grid=(S // tm,), name="rot_bwd",
        in_specs=[row(128), row(128), pl.BlockSpec((128, 768), lambda i: (0, 0)),
                  row(256), row(256), row(512), row(512), slab, slab, slab],
        out_specs=row(PW), out_shape=jax.ShapeDtypeStruct((S, PW), BF16),
        compiler_params=_params("parallel"),
    )(cos, sin, spread, dqr, dkr, drv, drg, dq_att, dk_att, dv_att)


def _win_bwd_w(h1, dproj, exchange, exchange_args):
    half = D // 2

    def sibling_copy(got_ref, buf, sems, k):
        x, y, c, me, chips = _place()
        return _remote(buf.at[k, pl.ds((1 - c) * half, half), :], got_ref.at[k], sems[0].at[k], sems[1].at[k],
                       (x, y, 1 - c))

    def body(h_ref, dp_ref, dw_ref, got_ref, buf, send, recv, xc):
        k = pl.program_id(0)

        @pl.when(k == 0)
        def _():
            xc.start()

        buf[k] = _tn(h_ref[...], dp_ref[...]).astype(BF16)
        sibling_copy(got_ref, buf, (send, recv), k).start()
        dw_ref[...] = buf[k, pl.ds(lax.axis_index("c") * half, half), :]

        @pl.when(k == N_CHIP - 1)
        def _():
            for j in range(N_CHIP):
                sibling_copy(got_ref, buf, (send, recv), j).wait_recv()
                sibling_copy(got_ref, buf, (send, recv), j).wait_send()
            xc.middle()
            xc.finish()

    halves = jax.ShapeDtypeStruct((N_CHIP, half, WIN_C), BF16)
    dma = pltpu.SemaphoreType.DMA((N_CHIP,))
    return _carry(
        "win_bwd_w", body, exchange, exchange_args, (h1, dproj),
        [pl.BlockSpec((S, D), lambda k: (0, 0)), pl.BlockSpec((S, WIN_C), lambda k: (0, k))],
        [pl.BlockSpec((None, half, WIN_C), lambda k: (k, 0, 0)), ANY], [halves, halves],
        scratch_shapes=[pltpu.VMEM((N_CHIP, D, WIN_C), BF16), dma, dma], grid=(N_CHIP,), semantics=("arbitrary",))


def _in_bwd(dproj, win_g, x, dx2, g1, other_rows, after):
    tm = 512
    n = len(other_rows)

    def body(dp_ref, w_ref, x_ref, dx2_ref, g_ref, *refs):
        rows, dx_ref, blk_ref = refs[:n], refs[n], refs[n + 1]

        @pl.when(pl.program_id(0) == 0)
        def _():
            blk_ref[...] = jnp.zeros_like(blk_ref)
            for i, r_ref in enumerate(rows):
                blk_ref[i + 1:i + 2, :] = r_ref[...]

        dh = _nt(dp_ref[:, 0:WIN_C], w_ref[0])
        for k in range(1, N_CHIP):
            dh = dh + _nt(dp_ref[:, k * WIN_C:(k + 1) * WIN_C], w_ref[k])
        xv = x_ref[...]
        r = _rstd(xv)
        xn = xv * r
        blk_ref[0:1, :] = blk_ref[0:1, :] + jnp.sum(dh * xn, axis=0, keepdims=True)
        t = dh * g_ref[...]
        dx_ref[...] = dx2_ref[...] + r * (t - xn * jnp.mean(t * xn, axis=-1, keepdims=True))

    row = lambda w: pl.BlockSpec((tm, w), lambda i: (i, 0))
    vec = pl.BlockSpec((1, D), lambda i: (0, 0))
    return _carry("in_bwd", body, _NoExchange(), (), (dproj, win_g, x, dx2, g1, *other_rows),
                  [row(PW), pl.BlockSpec((N_CHIP, D, WIN_C), lambda i: (0, 0, 0)), row(D), row(D), vec] + [vec] * n,
                  [row(D), pl.BlockSpec((8, D), lambda i: (0, 0))],
                  [jax.ShapeDtypeStruct((S, D), F32), jax.ShapeDtypeStruct((8, D), F32)],
                  grid=(S // tm,), semantics=("arbitrary",), after=after)[0]


ANY = pl.BlockSpec(memory_space=pl.ANY)
VMEM = pl.BlockSpec(memory_space=pltpu.VMEM)
FLIPS = ((1, 0), (0, 1), (1, 1))


def _place():
    x, y, c = lax.axis_index("x"), lax.axis_index("y"), lax.axis_index("c")
    chips = [((1 - x) if fx else x, (1 - y) if fy else y) for fx, fy in FLIPS]
    return x, y, c, 2 * x + y, chips


def _remote(src, dst, send_sem, recv_sem, device):
    return pltpu.make_async_remote_copy(src_ref=src, dst_ref=dst, send_sem=send_sem, recv_sem=recv_sem,
                                        device_id=device, device_id_type=MESH)


class _Exchange:
    aliases = {}

    def middle(self, ins, outs, sems):
        pass


class _GatherShards(_Exchange):
    def __init__(self, shards):
        n = self.n = len(shards)
        self.n_in = self.n_out = n
        self.out_shape = [jax.ShapeDtypeStruct((N_CHIP,) + s.shape, s.dtype) for s in shards]
        dma = pltpu.SemaphoreType.DMA
        self.scratch = [dma((3 * n,)), dma((3 * n,)), dma((3 * n,)), dma((3 * n,)), dma((n,)), dma((n,))]

    def _ici(self, ins, outs, sems, a, j, chip):
        x, y, c, me, chips = _place()
        half = ins[a].shape[0] // 2
        return _remote(ins[a].at[pl.ds(c * half, half), :], outs[a].at[me, pl.ds(c * half, half), :],
                       sems[0].at[3 * a + j], sems[1].at[3 * a + j], (*chip, c))

    def _fwd(self, outs, sems, a, j, chip, half_of):
        x, y, c, me, chips = _place()
        half = outs[a].shape[1] // 2
        blk = outs[a].at[2 * chip[0] + chip[1], pl.ds(half_of * half, half), :]
        return _remote(blk, blk, sems[2].at[3 * a + j], sems[3].at[3 * a + j], (x, y, 1 - c))

    def _own(self, ins, outs, sems, a):
        return _own_shard_to_sibling(ins[a], outs[a], sems[4].at[a], sems[5].at[a])

    def start(self, ins, outs, sems):
        chips = _place()[4]
        for a in range(self.n):
            for j, chip in enumerate(chips):
                self._ici(ins, outs, sems, a, j, chip).start()
        for a in range(self.n):
            self._own(ins, outs, sems, a).start()

    def middle(self, ins, outs, sems):
        x, y, c, me, chips = _place()
        for a in range(self.n):
            for j, chip in enumerate(chips):
                half = outs[a].shape[1] // 2
                blk = outs[a].at[2 * chip[0] + chip[1], pl.ds(c * half, half), :]
                _remote(blk, blk, sems[0].at[3 * a + j], sems[1].at[3 * a + j], (x, y, c)).wait_recv()
                self._fwd(outs, sems, a, j, chip, c).start()

    def finish(self, ins, outs, sems):
        x, y, c, me, chips = _place()
        for a in range(self.n):
            for j, chip in enumerate(chips):
                self._fwd(outs, sems, a, j, chip, 1 - c).wait_recv()
        for a in range(self.n):
            for j, chip in enumerate(chips):
                self._ici(ins, outs, sems, a, j, chip).wait_send()
                self._fwd(outs, sems, a, j, chip, c).wait_send()
            self._own(ins, outs, sems, a).wait()


def _own_shard_to_sibling(shard_ref, gathered_ref, send_sem, recv_sem):
    x, y, c, me, chips = _place()
    return _remote(shard_ref, gathered_ref.at[me], send_sem, recv_sem, (x, y, 1 - c))


class _NoExchange(_Exchange):
    n_in = n_out = 0
    out_shape = ()
    scratch = ()

    def start(self, ins, outs, sems):
        pass

    def finish(self, ins, outs, sems):
        pass


class _ForwardGathered(_Exchange):
    def __init__(self, shards, own=True, forward=True):
        self.own, self.forward = own, forward
        n = self.n = len(shards)
        self.n_in, self.n_out = 2 * n, n
        self.out_shape = [jax.ShapeDtypeStruct((N_CHIP,) + s.shape, s.dtype) for s in shards]
        dma = pltpu.SemaphoreType.DMA
        self.scratch = [dma((3 * n,)), dma((3 * n,)), dma((n,)), dma((n,))]
        self.aliases = {n + a: a for a in range(n)}

    def _fwd(self, outs, sems, a, j, chip, half_of):
        x, y, c, me, chips = _place()
        half = outs[a].shape[1] // 2
        blk = outs[a].at[2 * chip[0] + chip[1], pl.ds(half_of * half, half), :]
        return _remote(blk, blk, sems[0].at[3 * a + j], sems[1].at[3 * a + j], (x, y, 1 - c))

    def _own(self, ins, outs, sems, a):
        return _own_shard_to_sibling(ins[a], outs[a], sems[2].at[a], sems[3].at[a])

    def start(self, ins, outs, sems):
        x, y, c, me, chips = _place()
        for a in range(self.n):
            for j, chip in enumerate(chips if self.forward else ()):
                self._fwd(outs, sems, a, j, chip, c).start()
        for a in range(self.n if self.own else 0):
            self._own(ins, outs, sems, a).start()

    def finish(self, ins, outs, sems):
        x, y, c, me, chips = _place()
        for a in range(self.n):
            for j, chip in enumerate(chips if self.forward else ()):
                self._fwd(outs, sems, a, j, chip, 1 - c).wait_recv()
        for a in range(self.n):
            for j, chip in enumerate(chips if self.forward else ()):
                self._fwd(outs, sems, a, j, chip, c).wait_send()
            if self.own:
                self._own(ins, outs, sems, a).wait()


HBM = pl.BlockSpec(memory_space=pltpu.HBM)
SEMS = pl.BlockSpec(memory_space=pltpu.SEMAPHORE)
DATAFLOW = pltpu.SideEffectType.DATAFLOW_SIDE_EFFECTING


class _OverIci:
    def __init__(self, name, sources, lands):
        self.name, self.n = name, len(sources)
        hbm = lambda t: pltpu.with_memory_space_constraint(t, pltpu.HBM)
        self.arrays = [hbm(t) for t in sources] + [hbm(t) for t in lands]

    def sent(self, src, land, a, chip):
        raise NotImplementedError

    def landed(self, land, a, chip):
        raise NotImplementedError

    def _copy(self, arr, sems, a, j, receiving):
        x, y, c, me, chips = _place()
        src, dst = self.sent(arr[a], arr[self.n + a], a, chips[j])
        if receiving:
            dst = self.landed(arr[self.n + a], a, chips[j])
        return _remote(src, dst, sems[0].at[3 * a + j], sems[1].at[3 * a + j], (*chips[j], c))

    def start(self, after):
        m = len(self.arrays)

        def body(*refs):
            arr, sems, token = refs[:m], refs[m + 1:m + 3], refs[-1]
            for a in range(self.n):
                for j in range(3):
                    self._copy(arr, sems, a, j, False).start()
            token[...] = jnp.zeros_like(token)

        dma = pltpu.SemaphoreType.DMA
        outs = pl.pallas_call(
            body, name=self.name + "_start",
            out_shape=[dma((3 * self.n,)), dma((3 * self.n,))] + [pltpu.HBM(t.shape, t.dtype) for t in self.arrays]
                      + [jax.ShapeDtypeStruct((8, 128), F32)],
            in_specs=[HBM] * m + [ANY], out_specs=[SEMS, SEMS] + [HBM] * m + [VMEM],
            input_output_aliases={i: 2 + i for i in range(m)},
            compiler_params=pltpu.CompilerParams(has_side_effects=DATAFLOW),
        )(*self.arrays, after)
        self.sems, self.arrays = outs[0:2], list(outs[2:2 + m])
        return outs[-1]

    def wait(self, after):
        m = len(self.arrays)

        def body(*refs):
            arr, sems = refs[:m], refs[m:m + 2]
            for a in range(self.n):
                for j in range(3):
                    self._copy(arr, sems, a, j, False).wait_send()
                    self._copy(arr, sems, a, j, True).wait_recv()

        outs = pl.pallas_call(
            body, name=self.name + "_wait",
            out_shape=[pltpu.HBM(t.shape, t.dtype) for t in self.arrays],
            in_specs=[HBM] * m + [SEMS, SEMS, ANY], out_specs=[HBM] * m,
            input_output_aliases={i: i for i in range(m)},
            compiler_params=pltpu.CompilerParams(has_side_effects=DATAFLOW),
        )(*self.arrays, *self.sems, after)
        return list(outs[:self.n]), list(outs[self.n:])


class _GatherOverIci(_OverIci):
    def __init__(self, name, shards):
        super().__init__(name, shards, [lax.empty((N_CHIP,) + s.shape, s.dtype) for s in shards])

    @staticmethod
    def _half(ref):
        c = lax.axis_index("c")
        half = ref.shape[-2] // 2
        return pl.ds(c * half, half)

    def sent(self, src, land, a, chip):
        return src.at[self._half(src), :], land.at[_place()[3], self._half(src), :]

    def landed(self, land, a, chip):
        return land.at[2 * chip[0] + chip[1], self._half(land), :]


class _SumOverIci(_OverIci):
    def __init__(self, name, pre):
        super().__init__(name, pre, [lax.empty(p.shape, p.dtype) for p in pre])

    def sent(self, src, land, a, chip):
        return src.at[2 * chip[0] + chip[1]], land.at[_place()[3]]

    def landed(self, land, a, chip):
        return land.at[2 * chip[0] + chip[1]]


class _HalvesToSibling(_Exchange):
    def __init__(self, grads):
        n = self.n = len(grads)
        self.n_in = self.n_out = n
        self.out_shape = [jax.ShapeDtypeStruct((N_CHIP, g.shape[1] // 2, g.shape[2]), g.dtype) for g in grads]
        self.scratch = [pltpu.SemaphoreType.DMA((n,)), pltpu.SemaphoreType.DMA((n,))]

    def _copy(self, ins, outs, sems, a):
        x, y, c, me, chips = _place()
        half = ins[a].shape[1] // 2
        return _remote(ins[a].at[:, pl.ds((1 - c) * half, half), :], outs[a], sems[0].at[a], sems[1].at[a], (x, y, 1 - c))

    def start(self, ins, outs, sems):
        for a in range(self.n):
            self._copy(ins, outs, sems, a).start()

    def finish(self, ins, outs, sems):
        for a in range(self.n):
            self._copy(ins, outs, sems, a).wait_recv()
        for a in range(self.n):
            self._copy(ins, outs, sems, a).wait_send()


class _ShareHalves(_Exchange):
    def __init__(self, fulls):
        n = self.n = len(fulls)
        self.n_in = self.n_out = n
        self.out_shape = [jax.ShapeDtypeStruct(f.shape, f.dtype) for f in fulls]
        self.scratch = [pltpu.SemaphoreType.DMA((n,)), pltpu.SemaphoreType.DMA((n,))]
        self.aliases = {a: a for a in range(n)}

    def _copy(self, outs, sems, a, half_of):
        x, y, c, me, chips = _place()
        half = outs[a].shape[0] // 2
        rows = outs[a].at[pl.ds(half_of * half, half), :]
        return _remote(rows, rows, sems[0].at[a], sems[1].at[a], (x, y, 1 - c))

    def start(self, ins, outs, sems):
        c = _place()[2]
        for a in range(self.n):
            self._copy(outs, sems, a, c).start()

    def finish(self, ins, outs, sems):
        c = _place()[2]
        for a in range(self.n):
            self._copy(outs, sems, a, 1 - c).wait_recv()
        for a in range(self.n):
            self._copy(outs, sems, a, c).wait_send()


class _GatherBlocks(_Exchange):
    def __init__(self, block):
        self.n_in = self.n_out = 1
        self.out_shape = [jax.ShapeDtypeStruct((8,) + block.shape, block.dtype)]
        dma = pltpu.SemaphoreType.DMA
        self.scratch = [dma((7,)), dma((7,)), dma]

    @staticmethod
    def _peer(f):
        x, y, c, me, chips = _place()
        return ((1 - x) if f & 4 else x, (1 - y) if f & 2 else y, (1 - c) if f & 1 else c)

    def start(self, ins, outs, sems):
        x, y, c, me, chips = _place()
        for f in range(1, 8):
            _remote(ins[0], outs[0].at[2 * me + c], sems[0].at[f - 1], sems[1].at[f - 1], self._peer(f)).start()
        pltpu.make_async_copy(ins[0], outs[0].at[2 * me + c], sems[2]).start()

    def finish(self, ins, outs, sems):
        x, y, c, me, chips = _place()
        for f in range(1, 8):
            px, py, pc = self._peer(f)
            blk = outs[0].at[4 * px + 2 * py + pc]
            _remote(blk, blk, sems[0].at[f - 1], sems[1].at[f - 1], (x, y, c)).wait_recv()
        for f in range(1, 8):
            _remote(ins[0], outs[0].at[2 * me + c], sems[0].at[f - 1], sems[1].at[f - 1], self._peer(f)).wait_send()
        pltpu.make_async_copy(ins[0], outs[0].at[2 * me + c], sems[2]).wait()


class _Both(_Exchange):
    def __init__(self, first, second):
        self.parts = (first, second)
        self.n_in, self.n_out = first.n_in + second.n_in, first.n_out + second.n_out
        self.out_shape = first.out_shape + second.out_shape
        self.scratch = first.scratch + second.scratch
        self.aliases = dict(first.aliases)
        self.aliases.update({first.n_in + i: first.n_out + o for i, o in second.aliases.items()})

    def _split(self, ins, outs, sems):
        a, b = self.parts
        return ((a, ins[:a.n_in], outs[:a.n_out], sems[:len(a.scratch)]),
                (b, ins[a.n_in:], outs[a.n_out:], sems[len(a.scratch):]))

    def start(self, ins, outs, sems):
        for ex, i, o, s in self._split(ins, outs, sems):
            ex.start(i, o, s)

    def middle(self, ins, outs, sems):
        for ex, i, o, s in self._split(ins, outs, sems):
            ex.middle(i, o, s)

    def finish(self, ins, outs, sems):
        for ex, i, o, s in self._split(ins, outs, sems):
            ex.finish(i, o, s)


class _Bound:
    def __init__(self, ex, ins, outs, sems):
        self.start = lambda: ex.start(ins, outs, sems)
        self.middle = lambda: ex.middle(ins, outs, sems)
        self.finish = lambda: ex.finish(ins, outs, sems)


def _carry(name, body, ex, ex_args, args, in_specs, out_specs, out_shape, scratch_shapes=(), grid=None, semantics=(),
           after=None):
    n_a, n_o, n_s = len(args), len(out_shape), len(scratch_shapes)
    behind = [] if after is None else [after]

    def full_body(*refs):
        p = 0
        groups = []
        for size in (n_a, ex.n_in, len(behind), n_o, ex.n_out, n_s, len(ex.scratch)):
            groups.append(refs[p:p + size])
            p += size
        a, ei, _, o, eo, s, es = groups
        body(*a, *o, *s, _Bound(ex, ei, eo, es))

    kwargs = {} if grid is None else {"grid": grid}
    outs = pl.pallas_call(
        full_body, name=name,
        in_specs=list(in_specs) + [ANY] * (ex.n_in + len(behind)), out_specs=list(out_specs) + [ANY] * ex.n_out,
        out_shape=list(out_shape) + list(ex.out_shape), scratch_shapes=list(scratch_shapes) + list(ex.scratch),
        input_output_aliases={n_a + i: n_o + o for i, o in ex.aliases.items()},
        compiler_params=_params(*semantics) if semantics else pltpu.CompilerParams(vmem_limit_bytes=VMEM_LIMIT),
        **kwargs,
    )(*args, *ex_args, *behind)
    return outs[:n_o], outs[n_o:]


def _cast_bf16(arrays, after=None):
    n = len(arrays)
    behind = [] if after is None else [after]

    def body(*refs):
        for a in range(n):
            refs[len(refs) - n + a][...] = refs[a][...].astype(BF16)

    blks = [pl.BlockSpec((t.shape[0] // 4, t.shape[1]), lambda i: (i, 0)) for t in arrays]
    return pl.pallas_call(
        body, grid=(4,), name="cast_bf16", in_specs=blks + [ANY] * len(behind), out_specs=blks,
        out_shape=[jax.ShapeDtypeStruct(t.shape, BF16) for t in arrays], compiler_params=_params("parallel"),
    )(*arrays, *behind)


def _prepare(x, g1, pos, ifc, exchange, exchange_args, after):
    tm = 512

    def body(x_ref, g_ref, pos_ref, ifc_ref, h_ref, cos_ref, sin_ref, xc):
        @pl.when(pl.program_id(0) == 0)
        def _():
            xc.start()

        xv = x_ref[...]
        h_ref[...] = (xv * _rstd(xv) * g_ref[...]).astype(BF16)
        ang = pos_ref[...].astype(F32) * ifc_ref[...]
        cos_ref[...] = jnp.cos(ang)
        sin_ref[...] = jnp.sin(ang)

        @pl.when(pl.program_id(0) == S // tm - 1)
        def _():
            xc.middle()
            xc.finish()

    row = lambda w: pl.BlockSpec((tm, w), lambda i: (i, 0))
    const = lambda w: pl.BlockSpec((1, w), lambda i: (0, 0))
    return _carry("prepare", body, exchange, exchange_args, (x, g1, pos, ifc),
                  [row(D), const(D), row(1), const(128)], [row(D), row(128), row(128)],
                  [jax.ShapeDtypeStruct((S, D), BF16)] + [jax.ShapeDtypeStruct((S, 128), F32)] * 2,
                  grid=(S // tm,), semantics=("arbitrary",), after=after)


def _exchange_alone(name, ex, ex_args):
    def body(xc):
        xc.start()
        xc.middle()
        xc.finish()

    return _carry(name, body, ex, ex_args, (), (), (), ())[1]


def _core_index():
    return lax.axis_index("c").astype(jnp.int32).reshape(1)


def _pair_sum(gs, gots):
    n = len(gs)

    def body(c_ref, *refs):
        for a in range(n):
            refs[2 * n + a][...] = (refs[a][...].astype(F32) + refs[n + a][...].astype(F32)).astype(BF16)

    blk = [pl.BlockSpec((None,) + g.shape[1:], lambda k, c_ref: (k, 0, 0)) for g in gots]
    mine = [b if g.shape == got.shape else pl.BlockSpec((None,) + got.shape[1:], lambda k, c_ref: (k, c_ref[0], 0))
            for g, got, b in zip(gs, gots, blk)]
    return pl.pallas_call(
        body, name=f"pair_sum_{gots[0].shape[1]}x{gots[0].shape[2]}",
        grid_spec=pltpu.PrefetchScalarGridSpec(
            num_scalar_prefetch=1, grid=(N_CHIP,), in_specs=mine + blk, out_specs=blk),
        out_shape=[jax.ShapeDtypeStruct(g.shape, BF16) for g in gots],
        compiler_params=_params("parallel"),
    )(_core_index(), *gs, *gots)


def _chip_sum(pre, parts):
    n = len(parts)
    me = 2 * lax.axis_index("x") + lax.axis_index("y")
    others = [k + (k >= me).astype(jnp.int32) for k in range(3)]
    where = jnp.stack([lax.axis_index("c"), me, *others]).astype(jnp.int32)

    def body(w_ref, *refs):
        for a in range(n):
            own, p1, p2, p3 = refs[4 * a:4 * a + 4]
            refs[4 * n + a][...] = ((own[...].astype(F32) + p1[...].astype(F32)) + p2[...].astype(F32)) + p3[...].astype(F32)

    in_specs, out_specs, operands = [], [], []
    for a in range(n):
        _, half, cc = parts[a].shape
        tr = half // 2
        in_specs += [pl.BlockSpec((None, tr, cc), lambda i, w_ref, s=s: (w_ref[s], i, 0)) for s in (1, 2, 3, 4)]
        out_specs.append(pl.BlockSpec((tr, cc), lambda i, w_ref: (2 * w_ref[0] + i, 0)))
        operands += [pre[a], parts[a], parts[a], parts[a]]
    return pl.pallas_call(
        body, name=f"chip_sum_{parts[0].shape[1]}x{parts[0].shape[2]}",
        grid_spec=pltpu.PrefetchScalarGridSpec(num_scalar_prefetch=1, grid=(2,), in_specs=in_specs, out_specs=out_specs),
        out_shape=[jax.ShapeDtypeStruct((2 * p.shape[1], p.shape[2]), F32) for p in parts],
        compiler_params=_params("parallel"),
    )(where, *operands)


def _adamw_math(w, g, m, v):
    m = ADAM_B1 * m + (1.0 - ADAM_B1) * g
    v = ADAM_B2 * v + (1.0 - ADAM_B2) * (g * g)
    m_hat = m / (1.0 - ADAM_B1 ** ADAM_STEP)
    v_hat = v / (1.0 - ADAM_B2 ** ADAM_STEP)
    delta = -ADAM_LR * (m_hat / (jnp.sqrt(v_hat) + ADAM_EPS) + ADAM_WD * w)
    return delta, m, v


def _adamw(ws, gs, ms, vs, after=None):
    n = len(ws)

    def body(*refs):
        for a in range(n):
            w_ref, g_ref, m_ref, v_ref = (refs[t * n + a] for t in range(4))
            go_ref, d_ref, nm_ref, nv_ref = refs[4 * n + 4 * a:4 * n + 4 * a + 4]
            g = g_ref[...]
            go_ref[...] = g
            d_ref[...], nm_ref[...], nv_ref[...] = _adamw_math(w_ref[...], g, m_ref[...], v_ref[...])

    blks = [pl.BlockSpec((w.shape[0] // 4, w.shape[1]), lambda i: (i, 0)) for w in ws]
    outs = _carry(f"adamw_{ws[0].shape[0]}x{ws[0].shape[1]}", body, _NoExchange(), (), (*ws, *gs, *ms, *vs),
                  blks * 4, [b for b in blks for _ in range(4)],
                  [jax.ShapeDtypeStruct(w.shape, F32) for w in ws for _ in range(4)],
                  grid=(4,), semantics=("parallel",), after=after)[0]
    return [outs[4 * a:4 * a + 4] for a in range(n)]


def _adamw_gains(gall, ws, ms, vs):
    def body(ga_ref, *refs):
        w, m, v = refs[0:4], refs[4:8], refs[8:12]
        outs, loss_ref, total = refs[12:28], refs[28], refs[29]
        g = ga_ref[0]
        for dev in range(1, 8):
            g = g + ga_ref[dev]
        total[...] = g
        for i in range(4):
            gi = total[i:i + 1, :]
            outs[i][...] = gi
            outs[4 + i][...], outs[8 + i][...], outs[12 + i][...] = _adamw_math(w[i][...], gi, m[i][...], v[i][...])
        loss_ref[...] = total[4:5, 0:128] * (0.5 / D)

    outs = pl.pallas_call(
        body, name="adamw_gains",
        out_shape=[jax.ShapeDtypeStruct((1, D), F32)] * 16 + [jax.ShapeDtypeStruct((1, 128), F32)],
        scratch_shapes=[pltpu.VMEM((8, D), F32)],
    )(gall, *ws, *ms, *vs)
    return outs[0:4], outs[4:8], outs[8:12], outs[12:16], outs[16]


def kernel(x, positions, w_in, w_out, g_pre_mix, g_post_mix, g_pre_ffn, g_post_ffn, w_gate, w_up, w_down, loss_target, m_w_in, m_w_out, m_g_pre_mix, m_g_post_mix, m_g_pre_ffn, m_g_post_ffn, m_w_gate, m_w_up, m_w_down, v_w_in, v_w_out, v_g_pre_mix, v_g_post_mix, v_g_pre_ffn, v_g_post_ffn, v_w_gate, v_w_up, v_w_down):
    tr = lambda t: jnp.swapaxes(t, 1, 2)[0]
    shards = [w_in[0], w_out[0], tr(w_gate), tr(w_up), w_down[0]]
    moms = [m_w_in[0], m_w_out[0], tr(m_w_gate), tr(m_w_up), m_w_down[0]]
    vels = [v_w_in[0], v_w_out[0], tr(v_w_gate), tr(v_w_up), v_w_down[0]]
    xs, pos, tgt = x[0], positions.reshape(S, 1), loss_target[0]
    g1, g2, g3, g4 = g_pre_mix, g_post_mix, g_pre_ffn, g_post_ffn
    tabs = tuple(jnp.asarray(t) for t in _retention_tables())
    ifc, spread = _rotary_tables()
    ifc, spread = jnp.asarray(ifc), jnp.asarray(spread, dtype=BF16)
    bf = list(_cast_bf16(shards[:1]))
    win_gather = _GatherOverIci("win_gather", bf[:1])
    token = win_gather.start(shards[0])
    bf += _cast_bf16(shards[1:], token)
    wout_gather = _GatherOverIci("wout_gather", bf[1:2])
    token = wout_gather.start(token)
    ffn_gather = _GatherOverIci("ffn_gather", bf[2:])
    token = ffn_gather.start(token)
    (h1, cos, sin), win_gather.arrays[1:] = _prepare(
        xs, g1, pos, ifc, _ForwardGathered(bf[:1], forward=False), win_gather.arrays, token)
    win_sh, win_land = win_gather.wait(h1)
    (win_g,) = _exchange_alone("forward_win", _ForwardGathered(bf[:1], own=False), [*win_sh, *win_land])
    qr, kr, rv, rg, aq, ak, av = _proj_fwd(h1, win_g, cos, sin, spread, None)
    wout_sh, wout_land = wout_gather.wait(qr)
    n_ffn = len(bf[2:])
    (att_out, lse, cat_a), (wout_g, *ffn_gather.arrays[n_ffn:]) = _att_fwd(
        aq, ak, av, _Both(_ForwardGathered(bf[1:2]), _ForwardGathered(bf[2:], forward=False)),
        [*wout_sh, *wout_land, *ffn_gather.arrays])
    wout_g = wout_g.reshape(D, D)
    (o_raw, cat_r, states), _ = _ret_fwd(qr, kr, rv, rg, tabs, _NoExchange(), (), cat_a)
    ffn_sh, ffn_lands = ffn_gather.wait(cat_r)
    (mix, x2, h3), (wg_g, wu_g, wd_g) = _mix_fwd(cat_r, cat_a, wout_g, xs, g2, g3,
                                                _ForwardGathered(bf[2:], own=False), [*ffn_sh, *ffn_lands])
    gt, up, a, sq, dy, df, dg4 = _ffn_fwd(h3, wg_g, wu_g, wd_g, x2, tgt, g4)

    dgt, dup, dx2, dmix, dg3, dg2 = _ffn_bwd_act(df, gt, up, wg_g, wu_g, wd_g, dy, x2, mix, g2, g3)
    ffn_grads = list(_ffn_bwd_w(a, df, h3, dgt, dup))
    (dret, datt, dwout), got = _mix_bwd(dmix, cat_r, cat_a, wout_g, _HalvesToSibling(ffn_grads), ffn_grads)
    ffn_sum = _SumOverIci("ffn_sum", _pair_sum(ffn_grads, got))
    token = ffn_sum.start(datt)
    (dq_att, dk_att, dv_att), _ = _att_bwd(aq, ak, av, datt, att_out, lse, _NoExchange(), (), token)
    (dqr, dkr, drv, drg), _ = _ret_bwd(qr, kr, rv, rg, o_raw, states, dret, tabs, _NoExchange(), (), token)
    dproj = _rot_bwd(cos, sin, spread, dqr, dkr, drv, drg, dq_att, dk_att, dv_att)
    sums = _chip_sum(*ffn_sum.wait(dproj))
    dwout = dwout.reshape(N_CHIP, WOUT_R, D)
    (dwin, got_win), (*ffn_full, got_wout) = _win_bwd_w(
        h1, dproj, _Both(_ShareHalves(sums), _HalvesToSibling([dwout])), [*sums, dwout])

    in_sum = _SumOverIci("in_sum", _pair_sum([dwin, dwout], [got_win, got_wout]))
    token = in_sum.start(dproj)
    dx, gblock = _in_bwd(dproj, win_g, xs, dx2, g1, [dg2, dg3, dg4, sq], token)
    ffn_upd = _adamw(shards[2:], [ffn_full[o] for o in (1, 2, 0)],
                     moms[2:], vels[2:], token)
    pre, parts = in_sum.wait(ffn_upd[2][0])
    sums = _chip_sum(pre, parts)
    *in_full, gall = _exchange_alone("share_rest", _Both(_ShareHalves(sums), _GatherBlocks(gblock)), [*sums, gblock])
    upd = _adamw(shards[:2], in_full, moms[:2], vels[:2]) + ffn_upd
    gg, gd, gm, gv, loss_row = _adamw_gains(gall, [g1, g2, g3, g4],
                                            [m_g_pre_mix, m_g_post_mix, m_g_pre_ffn, m_g_post_ffn],
                                            [v_g_pre_mix, v_g_post_mix, v_g_pre_ffn, v_g_post_ffn])

    def order(mats, vecs):
        back = lambda t: jnp.swapaxes(t[None], 1, 2)
        return [mats[0][None], mats[1][None], *vecs, back(mats[2]), back(mats[3]), mats[4][None]]

    return (loss_row[0, 0], dx[None],
            *order([u[0] for u in upd], gg),
            *order([u[1] for u in upd], gd),
            *order([u[2] for u in upd], gm),
            *order([u[3] for u in upd], gv))
```

```python
import numpy as np
import jax
import jax.numpy as jnp
from jax import lax
from jax.experimental import pallas as pl
from jax.experimental.pallas import tpu as pltpu

F32, BF16 = jnp.float32, jnp.bfloat16
MESH = pl.DeviceIdType.MESH

S = 2048
D = 1024
PW = 3072
N_CHIP = 4
WIN_C = PW // N_CHIP
DFF = 2816
FF_C = DFF // N_CHIP
WOUT_R = D // N_CHIP
RMS_EPS = 1e-6
GN_EPS = 1e-5
RET_C = 128
RET_PER_STEP = 4
RET_SCALE = 32 ** -0.5
ATT_BLK = 128
ATT_SCALE = 64 ** -0.5
PATTERN_DILATIONS = (16, 1, 4)
NEG = -1e30
VMEM_LIMIT = 56 * 1024 * 1024

ADAM_LR, ADAM_B1, ADAM_B2, ADAM_EPS, ADAM_WD, ADAM_STEP = 0.001, 0.9, 0.999, 1e-08, 0.01, 10


def _params(*sem):
    return pltpu.CompilerParams(dimension_semantics=sem, vmem_limit_bytes=VMEM_LIMIT)


def _nt(a, b):
    return lax.dot_general(a, b, (((1,), (1,)), ((), ())), preferred_element_type=F32)


def _tn(a, b):
    return lax.dot_general(a, b, (((0,), (0,)), ((), ())), preferred_element_type=F32)


def _nn(a, b):
    return jnp.dot(a, b, preferred_element_type=F32)


def _rstd(v):
    return lax.rsqrt(jnp.mean(v * v, axis=-1, keepdims=True) + RMS_EPS)


def _sigmoid(v):
    return 1.0 / (1.0 + jnp.exp(-v))


def _rows(i, t):
    return pl.ds(pl.multiple_of(i * t, t), t)


def _retention_tables():
    h = np.arange(8, dtype=np.float32)
    log_g = np.log1p(-np.exp2(-5.0 - h)).astype(np.float32)
    idx = np.arange(RET_C, dtype=np.float32)
    diff = idx[:, None] - idx[None, :]
    dtab = np.where(diff >= 0, np.exp(log_g[:, None, None] * np.maximum(diff, 0.0)), 0.0).astype(np.float32)
    dtab = dtab.reshape(8 * RET_C, RET_C)
    lane_head = np.arange(256) // 32
    a_tab = np.exp(log_g[lane_head][None, :] * (idx + 1.0)[:, None]).astype(np.float32)
    b_tab = np.exp(log_g[lane_head][None, :] * (RET_C - 1.0 - idx)[:, None]).astype(np.float32)
    lam = np.exp(log_g[lane_head] * RET_C).astype(np.float32)[:, None]
    bd = (lane_head[:, None] == (np.arange(512) // 64)[None, :]).astype(np.float32)
    return dtab, a_tab, b_tab, lam, bd


def _rotary_tables():
    inv_r = (1.0 / (np.float32(10000.0) ** np.linspace(0.0, 1.0, 16, dtype=np.float32))).astype(np.float32)
    inv_a = (np.float32(500000.0) ** (-np.arange(0, 16, 2, dtype=np.float32) / np.float32(16))).astype(np.float32)
    ifc = np.zeros((1, 128), np.float32)
    ifc[0, 0:16], ifc[0, 16:24] = inv_r, inv_a
    spread = np.zeros((128, 768), np.float32)
    for lane in range(256):
        spread[(lane % 32) % 16, lane] = 1.0
    for lane in range(512):
        d = lane % 64
        spread[16 + d % 8 if d < 16 else 24, 256 + lane] = 1.0
    return ifc, spread


def _rot_halves(tm):
    lo_r = (lax.broadcasted_iota(jnp.int32, (tm, 256), 1) % 32) < 16
    lo_a = (lax.broadcasted_iota(jnp.int32, (tm, 512), 1) % 64) < 8
    return lo_r, lo_a


def _spread_exact(t, e):
    hi = t.astype(BF16)
    r1 = t - hi.astype(F32)
    mid = r1.astype(BF16)
    lo = (r1 - mid.astype(F32)).astype(BF16)
    return _nn(hi, e) + _nn(mid, e) + _nn(lo, e)


def _rot_tables(cos_ref, sin_ref, e_ref):
    cs = _spread_exact(cos_ref[...], e_ref[...])
    sn = _spread_exact(sin_ref[...], e_ref[...])
    return cs[:, 0:256], cs[:, 256:768], sn[:, 0:256], sn[:, 256:768]


def _proj_fwd(h1, win_g, cos, sin, spread, after):
    tm = 256

    def body(h_ref, w_ref, cos_ref, sin_ref, e_ref, qr_ref, kr_ref, rv_ref, rg_ref, aq_ref, ak_ref, av_ref, p_ref, _):
        h = h_ref[...]
        for k in range(N_CHIP):
            p_ref[:, k * WIN_C:(k + 1) * WIN_C] = _nn(h, w_ref[k])
        cr, ca, sr, sa = _rot_tables(cos_ref, sin_ref, e_ref)
        lo_r, lo_a = _rot_halves(tm)

        def rot_r(v):
            return v * cr + sr * jnp.where(lo_r, -pltpu.roll(v, 240, 1), pltpu.roll(v, 16, 1))

        def rot_a(v):
            return v * ca + sa * jnp.where(lo_a, -pltpu.roll(v, 504, 1), pltpu.roll(v, 8, 1))

        qr_ref[...] = rot_r(p_ref[:, 0:256]).astype(BF16)
        kr_ref[...] = (rot_r(p_ref[:, 256:512]) * RET_SCALE).astype(BF16)
        rv_ref[...] = p_ref[:, 512:1024].astype(BF16)
        rg_ref[...] = p_ref[:, 1024:1536]
        aq, ak = rot_a(p_ref[:, 1536:2048]), rot_a(p_ref[:, 2048:2560])
        for j in range(4):
            aq_ref[j] = aq[:, 128 * j:128 * j + 128]
            ak_ref[j] = ak[:, 128 * j:128 * j + 128]
            av_ref[j] = p_ref[:, 2560 + 128 * j:2560 + 128 * j + 128]

    row = lambda w: pl.BlockSpec((tm, w), lambda i: (i, 0))
    slab = pl.BlockSpec((4, tm, 128), lambda i: (0, i, 0))
    return _carry(
        "proj_fwd", body, _NoExchange(), (), (h1, win_g, cos, sin, spread),
        [row(D), pl.BlockSpec((N_CHIP, D, WIN_C), lambda i: (0, 0, 0)), row(128), row(128),
         pl.BlockSpec((128, 768), lambda i: (0, 0))],
        [row(256), row(256), row(512), row(512), slab, slab, slab],
        [jax.ShapeDtypeStruct((S, w), BF16) for w in (256, 256, 512)]
        + [jax.ShapeDtypeStruct((S, 512), F32)] + [jax.ShapeDtypeStruct((4, S, 128), F32)] * 3,
        scratch_shapes=[pltpu.VMEM((tm, PW), F32)], grid=(S // tm,), semantics=("parallel",), after=after)[0]


def _seg_mean(v):
    lo = lax.broadcasted_iota(jnp.int32, v.shape, 1) < 64
    s_lo = jnp.sum(jnp.where(lo, v, 0.0), axis=-1, keepdims=True)
    s_hi = jnp.sum(jnp.where(lo, 0.0, v), axis=-1, keepdims=True)
    return jnp.where(lo, s_lo, s_hi) * (1.0 / 64.0)


def _ret_fwd(qr, kr, rv, proj, tabs, exchange, exchange_args, after=None):
    C, G = RET_C, RET_PER_STEP
    steps = S // (C * G)
    dtab, a_tab, b_tab, lam, bd = tabs

    def body(q_ref, k_ref, v_ref, g_ref, dt_ref, a_ref, b_ref, lam_ref, bd_ref, o_ref, cat_ref, st_ref, R, exch):
        @pl.when(pl.program_id(0) == 0)
        def _():
            exch.start()
            R[...] = jnp.zeros_like(R)

        lane_head = lax.broadcasted_iota(jnp.int32, (C, 256), 1) // 32
        col_head = lax.broadcasted_iota(jnp.int32, (C, 256), 1) // 64
        for s in range(G):
            rows = slice(s * C, (s + 1) * C)
            q, k, v = q_ref[rows, :], k_ref[rows, :], v_ref[rows, :]
            rb = R[...].astype(BF16)
            st_ref[s] = rb
            qa = (q.astype(F32) * a_ref[...]).astype(BF16)
            cross = _nn(qa, rb)
            p = (_nt(_stack_heads(q, lane_head, n=8), k) * dt_ref[...]).astype(BF16)
            og = [cross[:, 256 * g:256 * g + 256]
                  + _unstack_heads(_nn(p[4 * C * g:4 * C * (g + 1)], v[:, 256 * g:256 * g + 256]), col_head)
                  for g in range(2)]
            kb = (k.astype(F32) * b_ref[...]).astype(BF16)
            R[...] = R[...] * lam_ref[...] + _tn(kb, v) * bd_ref[...]
            o_ref[rows, 0:256] = og[0]
            o_ref[rows, 256:512] = og[1]
            for j in range(4):
                oj = og[j // 2][:, 128 * (j % 2):128 * (j % 2) + 128]
                xc = oj - _seg_mean(oj)
                rn = xc * lax.rsqrt(_seg_mean(xc * xc) + GN_EPS)
                gj = g_ref[rows, 128 * j:128 * j + 128]
                cat_ref[rows, 128 * j:128 * j + 128] = (rn * (gj * _sigmoid(gj))).astype(BF16)

        @pl.when(pl.program_id(0) == steps - 1)
        def _():
            exch.middle()
            exch.finish()

    row = lambda w: pl.BlockSpec((C * G, w), lambda n: (n, 0))
    full = lambda a: pl.BlockSpec(a.shape, lambda n: (0,) * a.ndim)
    return _carry(
        "ret_fwd", body, exchange, exchange_args, (qr, kr, rv, proj, dtab, a_tab, b_tab, lam, bd),
        [row(256), row(256), row(512), row(512),
         full(dtab), full(a_tab), full(b_tab), full(lam), full(bd)],
        [row(512), row(512), pl.BlockSpec((G, 256, 512), lambda n: (n, 0, 0))],
        [jax.ShapeDtypeStruct((S, 512), F32), jax.ShapeDtypeStruct((S, 512), BF16),
         jax.ShapeDtypeStruct((S // C, 256, 512), BF16)],
        scratch_shapes=[pltpu.VMEM((256, 512), F32)], grid=(steps,), semantics=("arbitrary",), after=after)


def _stack_heads(v, lane_head, fill=0.0, n=4):
    return jnp.concatenate([jnp.where(lane_head == h, v, jnp.full_like(v, fill)) for h in range(n)], axis=0)


def _unstack_heads(v, lane_head, n=4):
    out = v[0:ATT_BLK]
    for h in range(1, n):
        out = jnp.where(lane_head == h, v[h * ATT_BLK:(h + 1) * ATT_BLK], out)
    return out


def _att_bias(has_prev):
    nk = 2 * ATT_BLK if has_prev else ATT_BLK
    a = lax.broadcasted_iota(jnp.int32, (4 * ATT_BLK, nk), 0) % ATT_BLK
    kk = lax.broadcasted_iota(jnp.int32, (4 * ATT_BLK, nk), 1)
    if not has_prev:
        return None, jnp.where((a - kk) >= 0, 0.0, NEG)
    dist = ATT_BLK + a - kk
    inside = (dist >= 0) & (dist <= ATT_BLK)
    return jnp.where(inside, 0.0, NEG), jnp.where(inside & (kk >= ATT_BLK), 0.0, NEG)


def _class_rows(ib, r, d):
    if d == 1:
        return pl.ds(pl.multiple_of(ib * ATT_BLK, ATT_BLK), ATT_BLK)
    return pl.ds(ib * ATT_BLK * d + r, ATT_BLK, stride=d)


def _slab_pair(ref, g, rows):
    return jnp.concatenate([ref[2 * g, rows, :], ref[2 * g + 1, rows, :]], axis=1)


def _att_blocks(d):
    nb = S // d // ATT_BLK
    return nb, nb > 1


def _att_fwd(aq, ak, av, exchange, exchange_args):
    def body(q_ref, k_ref, v_ref, o_ref, l_ref, cat_ref, xc):
        xc.start()
        lane_head = lax.broadcasted_iota(jnp.int32, (ATT_BLK, 256), 1) // 64
        for pi, d in enumerate(PATTERN_DILATIONS):
            if pi == len(PATTERN_DILATIONS) - 1:
                xc.middle()
            nb, has_prev = _att_blocks(d)
            bias_rest, bias_first = _att_bias(has_prev)

            def block(b, carry, pi=pi, d=d, nb=nb, has_prev=has_prev, bias_rest=bias_rest, bias_first=bias_first):
                r, ib = b // nb, b % nb
                rows = _class_rows(ib, r, d)
                prow = _class_rows(jnp.maximum(ib - 1, 0), r, d)
                bias = jnp.where(ib == 0, bias_first, bias_rest) if has_prev else bias_first
                for g in range(2):
                    qg = _slab_pair(q_ref, g, rows).astype(BF16)
                    kg = _slab_pair(k_ref, g, rows)
                    vg = _slab_pair(v_ref, g, rows)
                    if has_prev:
                        kg = jnp.concatenate([_slab_pair(k_ref, g, prow), kg], axis=0)
                        vg = jnp.concatenate([_slab_pair(v_ref, g, prow), vg], axis=0)
                    kg, vg = kg.astype(BF16), vg.astype(BF16)
                    s = _nt(_stack_heads(qg, lane_head), kg) * ATT_SCALE + bias
                    m = jnp.max(s, axis=-1, keepdims=True)
                    p = jnp.exp(s - m)
                    den = jnp.sum(p, axis=-1, keepdims=True)
                    og = _unstack_heads(_nn(p.astype(BF16), vg) / den, lane_head)
                    lg = _unstack_heads(jnp.broadcast_to(m + jnp.log(den), (4 * ATT_BLK, 256)), lane_head)
                    for jj in range(2):
                        j = 2 * g + jj
                        o_new, l_new = og[:, 128 * jj:128 * jj + 128], lg[:, 128 * jj:128 * jj + 128]
                        if pi > 0:
                            o_old, l_old = o_ref[j, rows, :], l_ref[j, rows, :]
                            mx = jnp.maximum(l_old, l_new)
                            ea, eb = jnp.exp(l_old - mx), jnp.exp(l_new - mx)
                            den = ea + eb
                            o_new = (ea * o_old + eb * o_new) / den
                            l_new = mx + jnp.log(den)
                        o_ref[j, rows, :] = o_new
                        l_ref[j, rows, :] = l_new
                return carry

            lax.fori_loop(0, S // ATT_BLK, block, 0, unroll=4)

        def to_cat(i, carry):
            rows = _rows(i, 256)
            for j in range(4):
                cat_ref[rows, 128 * j:128 * j + 128] = o_ref[j, rows, :].astype(BF16)
            return carry

        lax.fori_loop(0, S // 256, to_cat, 0)
        xc.finish()

    slab = jax.ShapeDtypeStruct((4, S, 128), F32)
    return _carry("att_fwd", body, exchange, exchange_args, (aq, ak, av), [VMEM] * 3, [VMEM] * 3,
                  [slab, slab, jax.ShapeDtypeStruct((S, 512), BF16)])


def _mix_fwd(cat_r, cat_a, wout, x, g2, g3, exchange, exchange_args):
    tm = 512

    def body(cr_ref, ca_ref, w_ref, x_ref, g2_ref, g3_ref, mix_ref, x2_ref, h3_ref, xc):
        @pl.when(pl.program_id(0) == 0)
        def _():
            xc.start()

        mix = _nn(cr_ref[...], w_ref[0:512, :]) + _nn(ca_ref[...], w_ref[512:1024, :])
        mix_ref[...] = mix
        x2 = x_ref[...] + mix * _rstd(mix) * g2_ref[...]
        x2_ref[...] = x2
        h3_ref[...] = (x2 * _rstd(x2) * g3_ref[...]).astype(BF16)

        @pl.when(pl.program_id(0) == S // tm - 1)
        def _():
            xc.middle()
            xc.finish()

    row = lambda w: pl.BlockSpec((tm, w), lambda i: (i, 0))
    vec = pl.BlockSpec((1, D), lambda i: (0, 0))
    return _carry("mix_fwd", body, exchange, exchange_args, (cat_r, cat_a, wout, x, g2, g3),
                  [row(512), row(512), pl.BlockSpec((D, D), lambda i: (0, 0)), row(D), vec, vec],
                  [row(D), row(D), row(D)],
                  [jax.ShapeDtypeStruct((S, D), F32), jax.ShapeDtypeStruct((S, D), F32),
                   jax.ShapeDtypeStruct((S, D), BF16)],
                  grid=(S // tm,), semantics=("arbitrary",))


def _ffn_fwd(h3, wg, wu, wd, x2, tgt, g4):
    tm = 512
    last = N_CHIP - 1

    def body(h_ref, wg_ref, wu_ref, wd_ref, x2_ref, t_ref, g_ref,
             gt_ref, up_ref, a_ref, loss_ref, dy_ref, df_ref, dg_ref, f_ref):
        k, i = pl.program_id(0), pl.program_id(1)
        h = h_ref[...]
        gt = _nt(h, wg_ref[...])
        up = _nt(h, wu_ref[...])
        gt_ref[...] = gt.astype(BF16)
        up_ref[...] = up.astype(BF16)
        a = (gt * _sigmoid(gt) * up).astype(BF16)
        a_ref[...] = a
        part = _nn(a, wd_ref[...])
        rows = _rows(i, tm)

        @pl.when(k == 0)
        def _():
            f_ref[rows, :] = part

        @pl.when((k > 0) & (k < last))
        def _():
            f_ref[rows, :] = f_ref[rows, :] + part

        @pl.when((k == last) & (i == 0))
        def _():
            loss_ref[...] = jnp.zeros_like(loss_ref)
            dg_ref[...] = jnp.zeros_like(dg_ref)

        @pl.when(k == last)
        def _():
            fv = f_ref[rows, :] + part
            r = _rstd(fv)
            fn = fv * r
            e = x2_ref[...] + fn * g_ref[...] - t_ref[...]
            loss_ref[...] = loss_ref[...] + jnp.sum(jnp.sum(e * e, axis=-1, keepdims=True), axis=0, keepdims=True)
            dy = e * (1.0 / D)
            dy_ref[...] = dy
            dg_ref[...] = dg_ref[...] + jnp.sum(dy * fn, axis=0, keepdims=True)
            t = dy * g_ref[...]
            df_ref[...] = (r * (t - fn * jnp.mean(t * fn, axis=-1, keepdims=True))).astype(BF16)

    wrow = pl.BlockSpec((None, FF_C, D), lambda k, i: (k, 0, 0))
    act = pl.BlockSpec((None, tm, FF_C), lambda k, i: (k, i, 0))
    late = pl.BlockSpec((tm, D), lambda k, i: (jnp.where(k == last, i, 0), 0))
    vec = pl.BlockSpec((1, D), lambda k, i: (0, 0))
    return pl.pallas_call(
        body, grid=(N_CHIP, S // tm), name="ffn_fwd",
        in_specs=[pl.BlockSpec((tm, D), lambda k, i: (i, 0)), wrow, wrow, wrow, late, late, vec],
        out_specs=[act, act, act, vec, late, late, vec],
        out_shape=[jax.ShapeDtypeStruct((N_CHIP, S, FF_C), BF16)] * 3
                  + [jax.ShapeDtypeStruct((1, D), F32), jax.ShapeDtypeStruct((S, D), F32),
                     jax.ShapeDtypeStruct((S, D), BF16), jax.ShapeDtypeStruct((1, D), F32)],
        scratch_shapes=[pltpu.VMEM((S, D), F32)],
        compiler_params=_params("arbitrary", "arbitrary"),
    )(h3, wg, wu, wd, x2, tgt, g4)


def _ffn_bwd_act(df, gt, up, wg, wu, wd, dy, x2, mix, g2, g3):
    tm, sub = 512, 256
    last = N_CHIP - 1

    def body(df_ref, gt_ref, up_ref, wg_ref, wu_ref, wd_ref, dy_ref, x2_ref, mix_ref, g2_ref, g3_ref,
             dgt_ref, dup_ref, dx2_ref, dmix_ref, dg3_ref, dg2_ref, dh_ref):
        k, i = pl.program_id(0), pl.program_id(1)
        parts = []
        for s in range(tm // sub):
            rows = slice(s * sub, (s + 1) * sub)
            da = _nt(df_ref[rows, :], wd_ref[...])
            gt, up = gt_ref[rows, :].astype(F32), up_ref[rows, :].astype(F32)
            sg = _sigmoid(gt)
            dup = (da * gt * sg).astype(BF16)
            dgt = (da * up * (sg * (1.0 + gt * (1.0 - sg)))).astype(BF16)
            dup_ref[rows, :] = dup
            dgt_ref[rows, :] = dgt
            parts.append(_nn(dgt, wg_ref[...]) + _nn(dup, wu_ref[...]))
        part = jnp.concatenate(parts, axis=0)
        rows = _rows(i, tm)

        @pl.when(k == 0)
        def _():
            dh_ref[rows, :] = part

        @pl.when((k > 0) & (k < last))
        def _():
            dh_ref[rows, :] = dh_ref[rows, :] + part

        @pl.when((k == last) & (i == 0))
        def _():
            dg3_ref[...] = jnp.zeros_like(dg3_ref)
            dg2_ref[...] = jnp.zeros_like(dg2_ref)

        @pl.when(k == last)
        def _():
            dh = dh_ref[rows, :] + part
            x2 = x2_ref[...]
            r3 = _rstd(x2)
            xn = x2 * r3
            dg3_ref[...] = dg3_ref[...] + jnp.sum(dh * xn, axis=0, keepdims=True)
            t = dh * g3_ref[...]
            dx2 = dy_ref[...] + r3 * (t - xn * jnp.mean(t * xn, axis=-1, keepdims=True))
            dx2_ref[...] = dx2
            mix = mix_ref[...]
            r2 = _rstd(mix)
            mn = mix * r2
            dg2_ref[...] = dg2_ref[...] + jnp.sum(dx2 * mn, axis=0, keepdims=True)
            u = dx2 * g2_ref[...]
            dmix_ref[...] = (r2 * (u - mn * jnp.mean(u * mn, axis=-1, keepdims=True))).astype(BF16)

    wrow = pl.BlockSpec((None, FF_C, D), lambda k, i: (k, 0, 0))
    act = pl.BlockSpec((None, tm, FF_C), lambda k, i: (k, i, 0))
    row = pl.BlockSpec((tm, D), lambda k, i: (i, 0))
    late = pl.BlockSpec((tm, D), lambda k, i: (jnp.where(k == last, i, 0), 0))
    vec = pl.BlockSpec((1, D), lambda k, i: (0, 0))
    return pl.pallas_call(
        body, grid=(N_CHIP, S // tm), name="ffn_bwd_act",
        in_specs=[row, act, act, wrow, wrow, wrow, late, late, late, vec, vec],
        out_specs=[act, act, late, late, vec, vec],
        out_shape=[jax.ShapeDtypeStruct((N_CHIP, S, FF_C), BF16), jax.ShapeDtypeStruct((N_CHIP, S, FF_C), BF16),
                   jax.ShapeDtypeStruct((S, D), F32), jax.ShapeDtypeStruct((S, D), BF16),
                   jax.ShapeDtypeStruct((1, D), F32), jax.ShapeDtypeStruct((1, D), F32)],
        scratch_shapes=[pltpu.VMEM((S, D), F32)],
        compiler_params=_params("arbitrary", "arbitrary"),
    )(df, gt, up, wg, wu, wd, dy, x2, mix, g2, g3)


def _ffn_bwd_w(a, df, h3, dgt, dup):
    tm = 1024
    assert S // tm == 2

    def body(a_ref, df_ref, h_ref, dgt_ref, dup_ref, dwd_ref, dwg_ref, dwu_ref, acc_d, acc_g, acc_u):
        i = pl.program_id(1)
        h = h_ref[...]
        parts = (_tn(a_ref[...], df_ref[...]), _tn(dgt_ref[...], h), _tn(dup_ref[...], h))

        @pl.when(i == 0)
        def _():
            for acc, part in zip((acc_d, acc_g, acc_u), parts):
                acc[...] = part

        @pl.when(i == S // tm - 1)
        def _():
            for out, acc, part in zip((dwd_ref, dwg_ref, dwu_ref), (acc_d, acc_g, acc_u), parts):
                out[...] = (acc[...] + part).astype(BF16)

    act = pl.BlockSpec((None, tm, FF_C), lambda k, i: (k, i, 0))
    row = pl.BlockSpec((tm, D), lambda k, i: (i, 0))
    wrow = pl.BlockSpec((None, FF_C, D), lambda k, i: (k, 0, 0))
    return pl.pallas_call(
        body, grid=(N_CHIP, S // tm), name="ffn_bwd_w",
        in_specs=[act, row, row, act, act],
        out_specs=[wrow, wrow, wrow],
        out_shape=[jax.ShapeDtypeStruct((N_CHIP, FF_C, D), BF16)] * 3,
        scratch_shapes=[pltpu.VMEM((FF_C, D), F32)] * 3,
        compiler_params=_params("parallel", "arbitrary"),
    )(a, df, h3, dgt, dup)


def _mix_bwd(dmix, cat_r, cat_a, wout, exchange, exchange_args):
    tm = 1024

    def body(dm_ref, cr_ref, ca_ref, w_ref, dret_ref, datt_ref, dw_ref, acc, xc):
        i = pl.program_id(0)

        @pl.when(i == 0)
        def _():
            xc.start()
            acc[...] = jnp.zeros_like(acc)

        dm = dm_ref[...]
        dret_ref[...] = _nt(dm, w_ref[0:512, :])
        datt = _nt(dm, w_ref[512:1024, :])
        for j in range(4):
            datt_ref[j] = datt[:, 128 * j:128 * j + 128]
        acc[0:512, :] += _tn(cr_ref[...], dm)
        acc[512:1024, :] += _tn(ca_ref[...], dm)

        @pl.when(i == S // tm - 1)
        def _():
            dw_ref[...] = acc[...].astype(BF16)
            xc.middle()
            xc.finish()

    row = lambda w: pl.BlockSpec((tm, w), lambda i: (i, 0))
    full = pl.BlockSpec((D, D), lambda i: (0, 0))
    return _carry("mix_bwd", body, exchange, exchange_args, (dmix, cat_r, cat_a, wout),
                  [row(D), row(512), row(512), full],
                  [row(512), pl.BlockSpec((4, tm, 128), lambda i: (0, i, 0)), full],
                  [jax.ShapeDtypeStruct((S, 512), F32), jax.ShapeDtypeStruct((4, S, 128), F32),
                   jax.ShapeDtypeStruct((D, D), BF16)],
                  scratch_shapes=[pltpu.VMEM((D, D), F32)], grid=(S // tm,), semantics=("arbitrary",))


def _att_bwd(aq, ak, av, datt, att_out, lse, exchange, exchange_args, after=None):
    def body(q_ref, k_ref, v_ref, do_ref, out_ref, l_ref, dq_ref, dk_ref, dv_ref, xc):
        xc.start()

        lane_head = lax.broadcasted_iota(jnp.int32, (ATT_BLK, 256), 1) // 64
        for pi, d in enumerate(PATTERN_DILATIONS):
            nb, has_prev = _att_blocks(d)
            assert pi > 0 or not has_prev
            bias_rest, bias_first = _att_bias(has_prev)

            def block(b, carry, pi=pi, d=d, nb=nb, has_prev=has_prev, bias_rest=bias_rest, bias_first=bias_first):
                r, ib = b // nb, b % nb
                rows = _class_rows(ib, r, d)
                prow = _class_rows(jnp.maximum(ib - 1, 0), r, d)
                bias = jnp.where(ib == 0, bias_first, bias_rest) if has_prev else bias_first
                for g in range(2):
                    qg = _slab_pair(q_ref, g, rows).astype(BF16)
                    kg = _slab_pair(k_ref, g, rows)
                    vg = _slab_pair(v_ref, g, rows)
                    if has_prev:
                        kg = jnp.concatenate([_slab_pair(k_ref, g, prow), kg], axis=0)
                        vg = jnp.concatenate([_slab_pair(v_ref, g, prow), vg], axis=0)
                    kg, vg = kg.astype(BF16), vg.astype(BF16)
                    dog = _slab_pair(do_ref, g, rows)
                    outg = _slab_pair(out_ref, g, rows)
                    lg = _slab_pair(l_ref, g, rows)
                    qs = _stack_heads(qg, lane_head)
                    dos = _stack_heads(dog, lane_head)
                    delta = jnp.sum(dos * jnp.concatenate([outg] * 4, axis=0), axis=-1, keepdims=True)
                    lh = jnp.max(_stack_heads(lg, lane_head, NEG), axis=-1, keepdims=True)
                    s = _nt(qs, kg) * ATT_SCALE + bias
                    p = jnp.exp(s - lh)
                    dosb = dos.astype(BF16)
                    ds = (p * (_nt(dosb, vg) - delta) * ATT_SCALE).astype(BF16)
                    dq = _unstack_heads(_nn(ds, kg), lane_head)
                    dk = _tn(ds, qs)
                    dv = _tn(p.astype(BF16), dosb)
                    for jj in range(2):
                        j, sl = 2 * g + jj, slice(128 * jj, 128 * jj + 128)
                        if pi == 0:
                            dq_ref[j, rows, :] = dq[:, sl]
                            dk_ref[j, rows, :] = dk[:, sl]
                            dv_ref[j, rows, :] = dv[:, sl]
                            continue
                        dq_ref[j, rows, :] += dq[:, sl]
                        if has_prev:
                            dk_ref[j, prow, :] += dk[0:ATT_BLK, sl]
                            dv_ref[j, prow, :] += dv[0:ATT_BLK, sl]
                            dk_ref[j, rows, :] += dk[ATT_BLK:2 * ATT_BLK, sl]
                            dv_ref[j, rows, :] += dv[ATT_BLK:2 * ATT_BLK, sl]
                        else:
                            dk_ref[j, rows, :] += dk[:, sl]
                            dv_ref[j, rows, :] += dv[:, sl]
                return carry

            lax.fori_loop(0, S // ATT_BLK, block, 0, unroll=4)
        xc.middle()
        xc.finish()

    slab = jax.ShapeDtypeStruct((4, S, 128), F32)
    return _carry("att_bwd", body, exchange, exchange_args, (aq, ak, av, datt, att_out, lse), [VMEM] * 6, [VMEM] * 3,
                  [slab, slab, slab], after=after)


def _ret_bwd(qr, kr, rv, proj, o_raw, states, dret, tabs, exchange, exchange_args, after=None):
    C, G = RET_C, RET_PER_STEP
    steps = S // (C * G)
    dtab, a_tab, b_tab, lam, bd = tabs

    def body(q_ref, k_ref, v_ref, g_ref, o_ref, st_ref, dr_ref, dt_ref, a_ref, b_ref, lam_ref, bd_ref,
             dq_ref, dk_ref, dv_ref, dg_ref, dR, exch):
        @pl.when(pl.program_id(0) == 0)
        def _():
            exch.start()
            dR[...] = jnp.zeros_like(dR)

        lane_head = lax.broadcasted_iota(jnp.int32, (C, 256), 1) // 32
        col_head = lax.broadcasted_iota(jnp.int32, (C, 256), 1) // 64
        for s in reversed(range(G)):
            rows = slice(s * C, (s + 1) * C)
            q, k, v = q_ref[rows, :], k_ref[rows, :], v_ref[rows, :]
            dos = []
            for j in range(4):
                sl = slice(128 * j, 128 * j + 128)
                oj = o_ref[rows, sl]
                xc = oj - _seg_mean(oj)
                rs = lax.rsqrt(_seg_mean(xc * xc) + GN_EPS)
                rn = xc * rs
                gj = g_ref[rows, sl]
                sg = _sigmoid(gj)
                dret = dr_ref[rows, sl]
                dg_ref[rows, sl] = dret * rn * (sg * (1.0 + gj * (1.0 - sg)))
                drn = dret * (gj * sg)
                dos.append(rs * (drn - _seg_mean(drn) - rn * _seg_mean(drn * rn)))
            do = [jnp.concatenate(dos[0:2], axis=1), jnp.concatenate(dos[2:4], axis=1)]
            do8 = jnp.concatenate(do, axis=1).astype(BF16)
            drb = dR[...].astype(BF16)
            rb = st_ref[s]
            dq = _nt(do8, rb) * a_ref[...]
            dk = _nt(v, drb) * b_ref[...]
            kb = (k.astype(F32) * b_ref[...]).astype(BF16)
            dvall = _nn(kb, drb)
            qs = _stack_heads(q, lane_head, n=8)
            dec = dt_ref[...]
            p = (_nt(qs, k) * dec).astype(BF16)
            dos = [_stack_heads(do[g], col_head).astype(BF16) for g in range(2)]
            dp = jnp.concatenate([_nt(dos[g], v[:, 256 * g:256 * g + 256]) for g in range(2)], axis=0)
            ds = (dp * dec).astype(BF16)
            dq = dq + _unstack_heads(_nn(ds, k), lane_head, n=8)
            dk = dk + _tn(ds, qs)
            dv = [dvall[:, 256 * g:256 * g + 256] + _tn(p[4 * C * g:4 * C * (g + 1)], dos[g]) for g in range(2)]
            qa = (q.astype(F32) * a_ref[...]).astype(BF16)
            dR[...] = dR[...] * lam_ref[...] + _tn(qa, do8) * bd_ref[...]
            dq_ref[rows, :] = dq
            dk_ref[rows, :] = dk
            dv_ref[rows, 0:256] = dv[0]
            dv_ref[rows, 256:512] = dv[1]

        @pl.when(pl.program_id(0) == steps - 1)
        def _():
            exch.middle()
            exch.finish()

    rev = lambda w: pl.BlockSpec((C * G, w), lambda n: (steps - 1 - n, 0))
    full = lambda a: pl.BlockSpec(a.shape, lambda n: (0,) * a.ndim)
    return _carry(
        "ret_bwd", body, exchange, exchange_args, (qr, kr, rv, proj, o_raw, states, dret, dtab, a_tab, b_tab, lam, bd),
        [rev(256), rev(256), rev(512), rev(512), rev(512),
         pl.BlockSpec((G, 256, 512), lambda n: (steps - 1 - n, 0, 0)), rev(512),
         full(dtab), full(a_tab), full(b_tab), full(lam), full(bd)],
        [rev(256), rev(256), rev(512), rev(512)],
        [jax.ShapeDtypeStruct((S, 256), F32), jax.ShapeDtypeStruct((S, 256), F32),
         jax.ShapeDtypeStruct((S, 512), F32), jax.ShapeDtypeStruct((S, 512), F32)],
        scratch_shapes=[pltpu.VMEM((256, 512), F32)], grid=(steps,), semantics=("arbitrary",), after=after)


def _rot_bwd(cos, sin, spread, dqr, dkr, drv, drg, dq_att, dk_att, dv_att):
    tm = 256

    def body(cos_ref, sin_ref, e_ref, dqr_ref, dkr_ref, drv_ref, drg_ref, dqa_ref, dka_ref, dva_ref, dp_ref):
        cr, ca, sr, sa = _rot_tables(cos_ref, sin_ref, e_ref)
        lo_r, lo_a = _rot_halves(tm)

        def unrot_r(g):
            gs = g * sr
            return g * cr + pltpu.roll(jnp.where(lo_r, -gs, 0.0), 16, 1) + pltpu.roll(jnp.where(lo_r, 0.0, gs), 240, 1)

        def unrot_a(g):
            gs = g * sa
            return g * ca + pltpu.roll(jnp.where(lo_a, -gs, 0.0), 8, 1) + pltpu.roll(jnp.where(lo_a, 0.0, gs), 504, 1)

        def wide(ref):
            return jnp.concatenate([ref[j] for j in range(4)], axis=1)

        dp_ref[:, 0:256] = unrot_r(dqr_ref[...]).astype(BF16)
        dp_ref[:, 256:512] = unrot_r(dkr_ref[...] * RET_SCALE).astype(BF16)
        dp_ref[:, 512:1024] = drv_ref[...].astype(BF16)
        dp_ref[:, 1024:1536] = drg_ref[...].astype(BF16)
        dp_ref[:, 1536:2048] = unrot_a(wide(dqa_ref)).astype(BF16)
        dp_ref[:, 2048:2560] = unrot_a(wide(dka_ref)).astype(BF16)
        dp_ref[:, 2560:3072] = wide(dva_ref).astype(BF16)

    row = lambda w: pl.BlockSpec((tm, w), lambda i: (i, 0))
    slab = pl.BlockSpec((4, tm, 128), lambda i: (0, i, 0))
    return pl.pallas_call(
        body, grid=(S // tm,), name="rot_bwd",
        in_specs=[row(128), row(128), pl.BlockSpec((128, 768), lambda i: (0, 0)),
                  row(256), row(256), row(512), row(512), slab, slab, slab],
        out_specs=row(PW), out_shape=jax.ShapeDtypeStruct((S, PW), BF16),
        compiler_params=_params("parallel"),
    )(cos, sin, spread, dqr, dkr, drv, drg, dq_att, dk_att, dv_att)


def _win_bwd_w(h1, dproj, exchange, exchange_args):
    half = D // 2

    def sibling_copy(got_ref, buf, sems, k):
        x, y, c, me, chips = _place()
        return _remote(buf.at[k, pl.ds((1 - c) * half, half), :], got_ref.at[k], sems[0].at[k], sems[1].at[k],
                       (x, y, 1 - c))

    def body(h_ref, dp_ref, dw_ref, got_ref, buf, send, recv, xc):
        k = pl.program_id(0)

        @pl.when(k == 0)
        def _():
            xc.start()

        buf[k] = _tn(h_ref[...], dp_ref[...]).astype(BF16)
        sibling_copy(got_ref, buf, (send, recv), k).start()
        dw_ref[...] = buf[k, pl.ds(lax.axis_index("c") * half, half), :]

        @pl.when(k == N_CHIP - 1)
        def _():
            for j in range(N_CHIP):
                sibling_copy(got_ref, buf, (send, recv), j).wait_recv()
                sibling_copy(got_ref, buf, (send, recv), j).wait_send()
            xc.middle()
            xc.finish()

    halves = jax.ShapeDtypeStruct((N_CHIP, half, WIN_C), BF16)
    dma = pltpu.SemaphoreType.DMA((N_CHIP,))
    return _carry(
        "win_bwd_w", body, exchange, exchange_args, (h1, dproj),
        [pl.BlockSpec((S, D), lambda k: (0, 0)), pl.BlockSpec((S, WIN_C), lambda k: (0, k))],
        [pl.BlockSpec((None, half, WIN_C), lambda k: (k, 0, 0)), ANY], [halves, halves],
        scratch_shapes=[pltpu.VMEM((N_CHIP, D, WIN_C), BF16), dma, dma], grid=(N_CHIP,), semantics=("arbitrary",))


def _in_bwd(dproj, win_g, x, dx2, g1, other_rows, after):
    tm = 512
    n = len(other_rows)

    def body(dp_ref, w_ref, x_ref, dx2_ref, g_ref, *refs):
        rows, dx_ref, blk_ref = refs[:n], refs[n], refs[n + 1]

        @pl.when(pl.program_id(0) == 0)
        def _():
            blk_ref[...] = jnp.zeros_like(blk_ref)
            for i, r_ref in enumerate(rows):
                blk_ref[i + 1:i + 2, :] = r_ref[...]

        dh = _nt(dp_ref[:, 0:WIN_C], w_ref[0])
        for k in range(1, N_CHIP):
            dh = dh + _nt(dp_ref[:, k * WIN_C:(k + 1) * WIN_C], w_ref[k])
        xv = x_ref[...]
        r = _rstd(xv)
        xn = xv * r
        blk_ref[0:1, :] = blk_ref[0:1, :] + jnp.sum(dh * xn, axis=0, keepdims=True)
        t = dh * g_ref[...]
        dx_ref[...] = dx2_ref[...] + r * (t - xn * jnp.mean(t * xn, axis=-1, keepdims=True))

    row = lambda w: pl.BlockSpec((tm, w), lambda i: (i, 0))
    vec = pl.BlockSpec((1, D), lambda i: (0, 0))
    return _carry("in_bwd", body, _NoExchange(), (), (dproj, win_g, x, dx2, g1, *other_rows),
                  [row(PW), pl.BlockSpec((N_CHIP, D, WIN_C), lambda i: (0, 0, 0)), row(D), row(D), vec] + [vec] * n,
                  [row(D), pl.BlockSpec((8, D), lambda i: (0, 0))],
                  [jax.ShapeDtypeStruct((S, D), F32), jax.ShapeDtypeStruct((8, D), F32)],
                  grid=(S // tm,), semantics=("arbitrary",), after=after)[0]


ANY = pl.BlockSpec(memory_space=pl.ANY)
VMEM = pl.BlockSpec(memory_space=pltpu.VMEM)
FLIPS = ((1, 0), (0, 1), (1, 1))


def _place():
    x, y, c = lax.axis_index("x"), lax.axis_index("y"), lax.axis_index("c")
    chips = [((1 - x) if fx else x, (1 - y) if fy else y) for fx, fy in FLIPS]
    return x, y, c, 2 * x + y, chips


def _remote(src, dst, send_sem, recv_sem, device):
    return pltpu.make_async_remote_copy(src_ref=src, dst_ref=dst, send_sem=send_sem, recv_sem=recv_sem,
                                        device_id=device, device_id_type=MESH)


class _Exchange:
    aliases = {}

    def middle(self, ins, outs, sems):
        pass


def _own_shard_to_sibling(shard_ref, gathered_ref, send_sem, recv_sem):
    x, y, c, me, chips = _place()
    return _remote(shard_ref, gathered_ref.at[me], send_sem, recv_sem, (x, y, 1 - c))


class _NoExchange(_Exchange):
    n_in = n_out = 0
    out_shape = ()
    scratch = ()

    def start(self, ins, outs, sems):
        pass

    def finish(self, ins, outs, sems):
        pass


class _ForwardGathered(_Exchange):
    def __init__(self, shards, own=True, forward=True):
        self.own, self.forward = own, forward
        n = self.n = len(shards)
        self.n_in, self.n_out = 2 * n, n
        self.out_shape = [jax.ShapeDtypeStruct((N_CHIP,) + s.shape, s.dtype) for s in shards]
        dma = pltpu.SemaphoreType.DMA
        self.scratch = [dma((3 * n,)), dma((3 * n,)), dma((n,)), dma((n,))]
        self.aliases = {n + a: a for a in range(n)}

    def _fwd(self, outs, sems, a, j, chip, half_of):
        x, y, c, me, chips = _place()
        half = outs[a].shape[1] // 2
        blk = outs[a].at[2 * chip[0] + chip[1], pl.ds(half_of * half, half), :]
        return _remote(blk, blk, sems[0].at[3 * a + j], sems[1].at[3 * a + j], (x, y, 1 - c))

    def _own(self, ins, outs, sems, a):
        return _own_shard_to_sibling(ins[a], outs[a], sems[2].at[a], sems[3].at[a])

    def start(self, ins, outs, sems):
        x, y, c, me, chips = _place()
        for a in range(self.n):
            for j, chip in enumerate(chips if self.forward else ()):
                self._fwd(outs, sems, a, j, chip, c).start()
        for a in range(self.n if self.own else 0):
            self._own(ins, outs, sems, a).start()

    def finish(self, ins, outs, sems):
        x, y, c, me, chips = _place()
        for a in range(self.n):
            for j, chip in enumerate(chips if self.forward else ()):
                self._fwd(outs, sems, a, j, chip, 1 - c).wait_recv()
        for a in range(self.n):
            for j, chip in enumerate(chips if self.forward else ()):
                self._fwd(outs, sems, a, j, chip, c).wait_send()
            if self.own:
                self._own(ins, outs, sems, a).wait()


HBM = pl.BlockSpec(memory_space=pltpu.HBM)
SEMS = pl.BlockSpec(memory_space=pltpu.SEMAPHORE)
DATAFLOW = pltpu.SideEffectType.DATAFLOW_SIDE_EFFECTING


class _OverIci:
    def __init__(self, name, sources, lands):
        self.name, self.n = name, len(sources)
        hbm = lambda t: pltpu.with_memory_space_constraint(t, pltpu.HBM)
        self.arrays = [hbm(t) for t in sources] + [hbm(t) for t in lands]

    def sent(self, src, land, a, chip):
        raise NotImplementedError

    def landed(self, land, a, chip):
        raise NotImplementedError

    def _copy(self, arr, sems, a, j, receiving):
        x, y, c, me, chips = _place()
        src, dst = self.sent(arr[a], arr[self.n + a], a, chips[j])
        if receiving:
            dst = self.landed(arr[self.n + a], a, chips[j])
        return _remote(src, dst, sems[0].at[3 * a + j], sems[1].at[3 * a + j], (*chips[j], c))

    def start(self, after):
        m = len(self.arrays)

        def body(*refs):
            arr, sems, token = refs[:m], refs[m + 1:m + 3], refs[-1]
            for a in range(self.n):
                for j in range(3):
                    self._copy(arr, sems, a, j, False).start()
            token[...] = jnp.zeros_like(token)

        dma = pltpu.SemaphoreType.DMA
        outs = pl.pallas_call(
            body, name=self.name + "_start",
            out_shape=[dma((3 * self.n,)), dma((3 * self.n,))] + [pltpu.HBM(t.shape, t.dtype) for t in self.arrays]
                      + [jax.ShapeDtypeStruct((8, 128), F32)],
            in_specs=[HBM] * m + [ANY], out_specs=[SEMS, SEMS] + [HBM] * m + [VMEM],
            input_output_aliases={i: 2 + i for i in range(m)},
            compiler_params=pltpu.CompilerParams(has_side_effects=DATAFLOW),
        )(*self.arrays, after)
        self.sems, self.arrays = outs[0:2], list(outs[2:2 + m])
        return outs[-1]

    def wait(self, after, which=None):
        which = list(range(self.n) if which is None else which)
        places = which + [self.n + a for a in which]
        m = len(places)

        def body(*refs):
            arr, sems = [None] * (2 * self.n), refs[m:m + 2]
            for place, ref in zip(places, refs[:m]):
                arr[place] = ref
            for a in which:
                for j in range(3):
                    self._copy(arr, sems, a, j, False).wait_send()
                    self._copy(arr, sems, a, j, True).wait_recv()

        outs = pl.pallas_call(
            body, name=f"{self.name}_wait_{which[0]}",
            out_shape=[pltpu.HBM(self.arrays[p].shape, self.arrays[p].dtype) for p in places],
            in_specs=[HBM] * m + [SEMS, SEMS, ANY], out_specs=[HBM] * m,
            input_output_aliases={i: i for i in range(m)},
            compiler_params=pltpu.CompilerParams(has_side_effects=DATAFLOW),
        )(*[self.arrays[p] for p in places], *self.sems, after)
        for place, out in zip(places, outs):
            self.arrays[place] = out
        return list(outs[:len(which)]), list(outs[len(which):])


class _GatherOverIci(_OverIci):
    def __init__(self, name, shards):
        super().__init__(name, shards, [lax.empty((N_CHIP,) + s.shape, s.dtype) for s in shards])

    @staticmethod
    def _half(ref):
        c = lax.axis_index("c")
        half = ref.shape[-2] // 2
        return pl.ds(c * half, half)

    def sent(self, src, land, a, chip):
        return src.at[self._half(src), :], land.at[_place()[3], self._half(src), :]

    def landed(self, land, a, chip):
        return land.at[2 * chip[0] + chip[1], self._half(land), :]


class _SumOverIci(_OverIci):
    def __init__(self, name, pre):
        super().__init__(name, pre, [lax.empty(p.shape, p.dtype) for p in pre])

    def sent(self, src, land, a, chip):
        return src.at[2 * chip[0] + chip[1]], land.at[_place()[3]]

    def landed(self, land, a, chip):
        return land.at[2 * chip[0] + chip[1]]


class _HalvesToSibling(_Exchange):
    def __init__(self, grads):
        n = self.n = len(grads)
        self.n_in = self.n_out = n
        self.out_shape = [jax.ShapeDtypeStruct((N_CHIP, g.shape[1] // 2, g.shape[2]), g.dtype) for g in grads]
        self.scratch = [pltpu.SemaphoreType.DMA((n,)), pltpu.SemaphoreType.DMA((n,))]

    def _copy(self, ins, outs, sems, a):
        x, y, c, me, chips = _place()
        half = ins[a].shape[1] // 2
        return _remote(ins[a].at[:, pl.ds((1 - c) * half, half), :], outs[a], sems[0].at[a], sems[1].at[a], (x, y, 1 - c))

    def start(self, ins, outs, sems):
        for a in range(self.n):
            self._copy(ins, outs, sems, a).start()

    def finish(self, ins, outs, sems):
        for a in range(self.n):
            self._copy(ins, outs, sems, a).wait_recv()
        for a in range(self.n):
            self._copy(ins, outs, sems, a).wait_send()


class _ShareHalves(_Exchange):
    def __init__(self, fulls):
        n = self.n = len(fulls)
        self.n_in = self.n_out = n
        self.out_shape = [jax.ShapeDtypeStruct(f.shape, f.dtype) for f in fulls]
        self.scratch = [pltpu.SemaphoreType.DMA((n,)), pltpu.SemaphoreType.DMA((n,))]
        self.aliases = {a: a for a in range(n)}

    def _copy(self, outs, sems, a, half_of):
        x, y, c, me, chips = _place()
        half = outs[a].shape[0] // 2
        rows = outs[a].at[pl.ds(half_of * half, half), :]
        return _remote(rows, rows, sems[0].at[a], sems[1].at[a], (x, y, 1 - c))

    def start(self, ins, outs, sems):
        c = _place()[2]
        for a in range(self.n):
            self._copy(outs, sems, a, c).start()

    def finish(self, ins, outs, sems):
        c = _place()[2]
        for a in range(self.n):
            self._copy(outs, sems, a, 1 - c).wait_recv()
        for a in range(self.n):
            self._copy(outs, sems, a, c).wait_send()


class _GatherBlocks(_Exchange):
    def __init__(self, block):
        self.n_in = self.n_out = 1
        self.out_shape = [jax.ShapeDtypeStruct((8,) + block.shape, block.dtype)]
        dma = pltpu.SemaphoreType.DMA
        self.scratch = [dma((7,)), dma((7,)), dma]

    @staticmethod
    def _peer(f):
        x, y, c, me, chips = _place()
        return ((1 - x) if f & 4 else x, (1 - y) if f & 2 else y, (1 - c) if f & 1 else c)

    def start(self, ins, outs, sems):
        x, y, c, me, chips = _place()
        for f in range(1, 8):
            _remote(ins[0], outs[0].at[2 * me + c], sems[0].at[f - 1], sems[1].at[f - 1], self._peer(f)).start()
        pltpu.make_async_copy(ins[0], outs[0].at[2 * me + c], sems[2]).start()

    def finish(self, ins, outs, sems):
        x, y, c, me, chips = _place()
        for f in range(1, 8):
            px, py, pc = self._peer(f)
            blk = outs[0].at[4 * px + 2 * py + pc]
            _remote(blk, blk, sems[0].at[f - 1], sems[1].at[f - 1], (x, y, c)).wait_recv()
        for f in range(1, 8):
            _remote(ins[0], outs[0].at[2 * me + c], sems[0].at[f - 1], sems[1].at[f - 1], self._peer(f)).wait_send()
        pltpu.make_async_copy(ins[0], outs[0].at[2 * me + c], sems[2]).wait()


class _Both(_Exchange):
    def __init__(self, first, second):
        self.parts = (first, second)
        self.n_in, self.n_out = first.n_in + second.n_in, first.n_out + second.n_out
        self.out_shape = first.out_shape + second.out_shape
        self.scratch = first.scratch + second.scratch
        self.aliases = dict(first.aliases)
        self.aliases.update({first.n_in + i: first.n_out + o for i, o in second.aliases.items()})

    def _split(self, ins, outs, sems):
        a, b = self.parts
        return ((a, ins[:a.n_in], outs[:a.n_out], sems[:len(a.scratch)]),
                (b, ins[a.n_in:], outs[a.n_out:], sems[len(a.scratch):]))

    def start(self, ins, outs, sems):
        for ex, i, o, s in self._split(ins, outs, sems):
            ex.start(i, o, s)

    def middle(self, ins, outs, sems):
        for ex, i, o, s in self._split(ins, outs, sems):
            ex.middle(i, o, s)

    def finish(self, ins, outs, sems):
        for ex, i, o, s in self._split(ins, outs, sems):
            ex.finish(i, o, s)


class _Bound:
    def __init__(self, ex, ins, outs, sems):
        self.start = lambda: ex.start(ins, outs, sems)
        self.middle = lambda: ex.middle(ins, outs, sems)
        self.finish = lambda: ex.finish(ins, outs, sems)


def _carry(name, body, ex, ex_args, args, in_specs, out_specs, out_shape, scratch_shapes=(), grid=None, semantics=(),
           after=None):
    n_a, n_o, n_s = len(args), len(out_shape), len(scratch_shapes)
    behind = [] if after is None else [after]

    def full_body(*refs):
        p = 0
        groups = []
        for size in (n_a, ex.n_in, len(behind), n_o, ex.n_out, n_s, len(ex.scratch)):
            groups.append(refs[p:p + size])
            p += size
        a, ei, _, o, eo, s, es = groups
        body(*a, *o, *s, _Bound(ex, ei, eo, es))

    kwargs = {} if grid is None else {"grid": grid}
    outs = pl.pallas_call(
        full_body, name=name,
        in_specs=list(in_specs) + [ANY] * (ex.n_in + len(behind)), out_specs=list(out_specs) + [ANY] * ex.n_out,
        out_shape=list(out_shape) + list(ex.out_shape), scratch_shapes=list(scratch_shapes) + list(ex.scratch),
        input_output_aliases={n_a + i: n_o + o for i, o in ex.aliases.items()},
        compiler_params=_params(*semantics) if semantics else pltpu.CompilerParams(vmem_limit_bytes=VMEM_LIMIT),
        **kwargs,
    )(*args, *ex_args, *behind)
    return outs[:n_o], outs[n_o:]


def _cast_bf16(arrays, after=None):
    n = len(arrays)
    behind = [] if after is None else [after]

    def body(*refs):
        for a in range(n):
            refs[len(refs) - n + a][...] = refs[a][...].astype(BF16)

    blks = [pl.BlockSpec((t.shape[0] // 4, t.shape[1]), lambda i: (i, 0)) for t in arrays]
    return pl.pallas_call(
        body, grid=(4,), name="cast_bf16", in_specs=blks + [ANY] * len(behind), out_specs=blks,
        out_shape=[jax.ShapeDtypeStruct(t.shape, BF16) for t in arrays], compiler_params=_params("parallel"),
    )(*arrays, *behind)


def _prepare(x, g1, pos, ifc, after):
    tm = 512

    def body(x_ref, g_ref, pos_ref, ifc_ref, h_ref, cos_ref, sin_ref, _):
        xv = x_ref[...]
        h_ref[...] = (xv * _rstd(xv) * g_ref[...]).astype(BF16)
        ang = pos_ref[...].astype(F32) * ifc_ref[...]
        cos_ref[...] = jnp.cos(ang)
        sin_ref[...] = jnp.sin(ang)

    row = lambda w: pl.BlockSpec((tm, w), lambda i: (i, 0))
    const = lambda w: pl.BlockSpec((1, w), lambda i: (0, 0))
    return _carry("prepare", body, _NoExchange(), (), (x, g1, pos, ifc),
                  [row(D), const(D), row(1), const(128)], [row(D), row(128), row(128)],
                  [jax.ShapeDtypeStruct((S, D), BF16)] + [jax.ShapeDtypeStruct((S, 128), F32)] * 2,
                  grid=(S // tm,), semantics=("parallel",), after=after)[0]


def _exchange_alone(name, ex, ex_args):
    def body(xc):
        xc.start()
        xc.middle()
        xc.finish()

    return _carry(name, body, ex, ex_args, (), (), (), ())[1]


def _core_index():
    return lax.axis_index("c").astype(jnp.int32).reshape(1)


def _pair_sum(gs, gots):
    n = len(gs)

    def body(c_ref, *refs):
        for a in range(n):
            refs[2 * n + a][...] = (refs[a][...].astype(F32) + refs[n + a][...].astype(F32)).astype(BF16)

    blk = [pl.BlockSpec((None,) + g.shape[1:], lambda k, c_ref: (k, 0, 0)) for g in gots]
    mine = [b if g.shape == got.shape else pl.BlockSpec((None,) + got.shape[1:], lambda k, c_ref: (k, c_ref[0], 0))
            for g, got, b in zip(gs, gots, blk)]
    return pl.pallas_call(
        body, name=f"pair_sum_{gots[0].shape[1]}x{gots[0].shape[2]}",
        grid_spec=pltpu.PrefetchScalarGridSpec(
            num_scalar_prefetch=1, grid=(N_CHIP,), in_specs=mine + blk, out_specs=blk),
        out_shape=[jax.ShapeDtypeStruct(g.shape, BF16) for g in gots],
        compiler_params=_params("parallel"),
    )(_core_index(), *gs, *gots)


def _chip_sum(pre, parts):
    n = len(parts)
    me = 2 * lax.axis_index("x") + lax.axis_index("y")
    others = [k + (k >= me).astype(jnp.int32) for k in range(3)]
    where = jnp.stack([lax.axis_index("c"), me, *others]).astype(jnp.int32)

    def body(w_ref, *refs):
        for a in range(n):
            own, p1, p2, p3 = refs[4 * a:4 * a + 4]
            refs[4 * n + a][...] = ((own[...].astype(F32) + p1[...].astype(F32)) + p2[...].astype(F32)) + p3[...].astype(F32)

    in_specs, out_specs, operands = [], [], []
    for a in range(n):
        _, half, cc = parts[a].shape
        tr = half // 2
        in_specs += [pl.BlockSpec((None, tr, cc), lambda i, w_ref, s=s: (w_ref[s], i, 0)) for s in (1, 2, 3, 4)]
        out_specs.append(pl.BlockSpec((tr, cc), lambda i, w_ref: (2 * w_ref[0] + i, 0)))
        operands += [pre[a], parts[a], parts[a], parts[a]]
    return pl.pallas_call(
        body, name=f"chip_sum_{parts[0].shape[1]}x{parts[0].shape[2]}",
        grid_spec=pltpu.PrefetchScalarGridSpec(num_scalar_prefetch=1, grid=(2,), in_specs=in_specs, out_specs=out_specs),
        out_shape=[jax.ShapeDtypeStruct((2 * p.shape[1], p.shape[2]), F32) for p in parts],
        compiler_params=_params("parallel"),
    )(where, *operands)


def _adamw_math(w, g, m, v):
    m = ADAM_B1 * m + (1.0 - ADAM_B1) * g
    v = ADAM_B2 * v + (1.0 - ADAM_B2) * (g * g)
    m_hat = m / (1.0 - ADAM_B1 ** ADAM_STEP)
    v_hat = v / (1.0 - ADAM_B2 ** ADAM_STEP)
    delta = -ADAM_LR * (m_hat / (jnp.sqrt(v_hat) + ADAM_EPS) + ADAM_WD * w)
    return delta, m, v


def _adamw(ws, gs, ms, vs, after=None):
    n = len(ws)

    def body(*refs):
        for a in range(n):
            w_ref, g_ref, m_ref, v_ref = (refs[t * n + a] for t in range(4))
            go_ref, d_ref, nm_ref, nv_ref = refs[4 * n + 4 * a:4 * n + 4 * a + 4]
            g = g_ref[...]
            go_ref[...] = g
            d_ref[...], nm_ref[...], nv_ref[...] = _adamw_math(w_ref[...], g, m_ref[...], v_ref[...])

    blks = [pl.BlockSpec((w.shape[0] // 4, w.shape[1]), lambda i: (i, 0)) for w in ws]
    outs = _carry(f"adamw_{ws[0].shape[0]}x{ws[0].shape[1]}", body, _NoExchange(), (), (*ws, *gs, *ms, *vs),
                  blks * 4, [b for b in blks for _ in range(4)],
                  [jax.ShapeDtypeStruct(w.shape, F32) for w in ws for _ in range(4)],
                  grid=(4,), semantics=("parallel",), after=after)[0]
    return [outs[4 * a:4 * a + 4] for a in range(n)]


def _adamw_gains(gall, ws, ms, vs):
    def body(ga_ref, *refs):
        w, m, v = refs[0:4], refs[4:8], refs[8:12]
        outs, loss_ref, total = refs[12:28], refs[28], refs[29]
        g = ga_ref[0]
        for dev in range(1, 8):
            g = g + ga_ref[dev]
        total[...] = g
        for i in range(4):
            gi = total[i:i + 1, :]
            outs[i][...] = gi
            outs[4 + i][...], outs[8 + i][...], outs[12 + i][...] = _adamw_math(w[i][...], gi, m[i][...], v[i][...])
        loss_ref[...] = total[4:5, 0:128] * (0.5 / D)

    outs = pl.pallas_call(
        body, name="adamw_gains",
        out_shape=[jax.ShapeDtypeStruct((1, D), F32)] * 16 + [jax.ShapeDtypeStruct((1, 128), F32)],
        scratch_shapes=[pltpu.VMEM((8, D), F32)],
    )(gall, *ws, *ms, *vs)
    return outs[0:4], outs[4:8], outs[8:12], outs[12:16], outs[16]


def kernel(x, positions, w_in, w_out, g_pre_mix, g_post_mix, g_pre_ffn, g_post_ffn, w_gate, w_up, w_down, loss_target, m_w_in, m_w_out, m_g_pre_mix, m_g_post_mix, m_g_pre_ffn, m_g_post_ffn, m_w_gate, m_w_up, m_w_down, v_w_in, v_w_out, v_g_pre_mix, v_g_post_mix, v_g_pre_ffn, v_g_post_ffn, v_w_gate, v_w_up, v_w_down):
    tr = lambda t: jnp.swapaxes(t, 1, 2)[0]
    shards = [w_in[0], w_out[0], tr(w_gate), tr(w_up), w_down[0]]
    moms = [m_w_in[0], m_w_out[0], tr(m_w_gate), tr(m_w_up), m_w_down[0]]
    vels = [v_w_in[0], v_w_out[0], tr(v_w_gate), tr(v_w_up), v_w_down[0]]
    xs, pos, tgt = x[0], positions.reshape(S, 1), loss_target[0]
    g1, g2, g3, g4 = g_pre_mix, g_post_mix, g_pre_ffn, g_post_ffn
    tabs = tuple(jnp.asarray(t) for t in _retention_tables())
    ifc, spread = _rotary_tables()
    ifc, spread = jnp.asarray(ifc), jnp.asarray(spread, dtype=BF16)
    bf = list(_cast_bf16(shards[:1]))
    win_gather = _GatherOverIci("win_gather", bf[:1])
    token = win_gather.start(shards[0])
    bf += _cast_bf16(shards[1:], token)
    rest_gather = _GatherOverIci("rest_gather", bf[1:])
    token = rest_gather.start(token)
    h1, cos, sin = _prepare(xs, g1, pos, ifc, token)
    win_sh, win_land = win_gather.wait(h1)
    (win_g,) = _exchange_alone("forward_win", _ForwardGathered(bf[:1]), [*win_sh, *win_land])
    qr, kr, rv, rg, aq, ak, av = _proj_fwd(h1, win_g, cos, sin, spread, None)
    wout_sh, wout_land = rest_gather.wait(qr, [0])
    n, ffn = rest_gather.n, [1, 2, 3]
    (att_out, lse, cat_a), (wout_g, *rest_gather.arrays[n + 1:]) = _att_fwd(
        aq, ak, av, _Both(_ForwardGathered(bf[1:2]), _ForwardGathered(bf[2:], forward=False)),
        [*wout_sh, *wout_land, *rest_gather.arrays[1:n], *rest_gather.arrays[n + 1:]])
    wout_g = wout_g.reshape(D, D)
    (o_raw, cat_r, states), _ = _ret_fwd(qr, kr, rv, rg, tabs, _NoExchange(), (), cat_a)
    ffn_sh, ffn_lands = rest_gather.wait(cat_r, ffn)
    (mix, x2, h3), (wg_g, wu_g, wd_g) = _mix_fwd(cat_r, cat_a, wout_g, xs, g2, g3,
                                                _ForwardGathered(bf[2:], own=False), [*ffn_sh, *ffn_lands])
    gt, up, a, sq, dy, df, dg4 = _ffn_fwd(h3, wg_g, wu_g, wd_g, x2, tgt, g4)

    dgt, dup, dx2, dmix, dg3, dg2 = _ffn_bwd_act(df, gt, up, wg_g, wu_g, wd_g, dy, x2, mix, g2, g3)
    ffn_grads = list(_ffn_bwd_w(a, df, h3, dgt, dup))
    (dret, datt, dwout), got = _mix_bwd(dmix, cat_r, cat_a, wout_g, _HalvesToSibling(ffn_grads), ffn_grads)
    ffn_sum = _SumOverIci("ffn_sum", _pair_sum(ffn_grads, got))
    token = ffn_sum.start(datt)
    (dq_att, dk_att, dv_att), _ = _att_bwd(aq, ak, av, datt, att_out, lse, _NoExchange(), (), token)
    (dqr, dkr, drv, drg), _ = _ret_bwd(qr, kr, rv, rg, o_raw, states, dret, tabs, _NoExchange(), (), token)
    dproj = _rot_bwd(cos, sin, spread, dqr, dkr, drv, drg, dq_att, dk_att, dv_att)
    sums = _chip_sum(*ffn_sum.wait(dproj))
    dwout = dwout.reshape(N_CHIP, WOUT_R, D)
    (dwin, got_win), (*ffn_full, got_wout) = _win_bwd_w(
        h1, dproj, _Both(_ShareHalves(sums), _HalvesToSibling([dwout])), [*sums, dwout])

    in_sum = _SumOverIci("in_sum", _pair_sum([dwin, dwout], [got_win, got_wout]))
    token = in_sum.start(dproj)
    dx, gblock = _in_bwd(dproj, win_g, xs, dx2, g1, [dg2, dg3, dg4, sq], token)
    ffn_upd = _adamw(shards[2:], [ffn_full[o] for o in (1, 2, 0)],
                     moms[2:], vels[2:], token)
    pre, parts = in_sum.wait(ffn_upd[2][0])
    sums = _chip_sum(pre, parts)
    *in_full, gall = _exchange_alone("share_rest", _Both(_ShareHalves(sums), _GatherBlocks(gblock)), [*sums, gblock])
    upd = _adamw(shards[:2], in_full, moms[:2], vels[:2]) + ffn_upd
    gg, gd, gm, gv, loss_row = _adamw_gains(gall, [g1, g2, g3, g4],
                                            [m_g_pre_mix, m_g_post_mix, m_g_pre_ffn, m_g_post_ffn],
                                            [v_g_pre_mix, v_g_post_mix, v_g_pre_ffn, v_g_post_ffn])

    def order(mats, vecs):
        back = lambda t: jnp.swapaxes(t[None], 1, 2)
        return [mats[0][None], mats[1][None], *vecs, back(mats[2]), back(mats[3]), mats[4][None]]

    return (loss_row[0, 0], dx[None],
            *order([u[0] for u in upd], gg),
            *order([u[1] for u in upd], gd),
            *order([u[2] for u in upd], gm),
            *order([u[3] for u in upd], gv))
```

```python
import numpy as np
import jax
import jax.numpy as jnp
from jax import lax
from jax.experimental import pallas as pl
from jax.experimental.pallas import tpu as pltpu

F32, BF16 = jnp.float32, jnp.bfloat16
MESH = pl.DeviceIdType.MESH

S = 2048
D = 1024
PW = 3072
N_CHIP = 4
WIN_C = PW // N_CHIP
DFF = 2816
FF_C = DFF // N_CHIP
WOUT_R = D // N_CHIP
RMS_EPS = 1e-6
GN_EPS = 1e-5
RET_C = 128
RET_PER_STEP = 4
RET_SCALE = 32 ** -0.5
ATT_BLK = 128
ATT_SCALE = 64 ** -0.5
PATTERN_DILATIONS = (16, 1, 4)
NEG = -1e30
VMEM_LIMIT = 56 * 1024 * 1024

ADAM_LR, ADAM_B1, ADAM_B2, ADAM_EPS, ADAM_WD, ADAM_STEP = 0.001, 0.9, 0.999, 1e-08, 0.01, 10


def _params(*sem):
    return pltpu.CompilerParams(dimension_semantics=sem, vmem_limit_bytes=VMEM_LIMIT)


def _nt(a, b):
    return lax.dot_general(a, b, (((1,), (1,)), ((), ())), preferred_element_type=F32)


def _tn(a, b):
    return lax.dot_general(a, b, (((0,), (0,)), ((), ())), preferred_element_type=F32)


def _nn(a, b):
    return jnp.dot(a, b, preferred_element_type=F32)


def _rstd(v):
    return lax.rsqrt(jnp.mean(v * v, axis=-1, keepdims=True) + RMS_EPS)


def _sigmoid(v):
    return 1.0 / (1.0 + jnp.exp(-v))


def _rows(i, t):
    return pl.ds(pl.multiple_of(i * t, t), t)


def _retention_tables():
    h = np.arange(8, dtype=np.float32)
    log_g = np.log1p(-np.exp2(-5.0 - h)).astype(np.float32)
    idx = np.arange(RET_C, dtype=np.float32)
    diff = idx[:, None] - idx[None, :]
    dtab = np.where(diff >= 0, np.exp(log_g[:, None, None] * np.maximum(diff, 0.0)), 0.0).astype(np.float32)
    dtab = dtab.reshape(8 * RET_C, RET_C)
    lane_head = np.arange(256) // 32
    a_tab = np.exp(log_g[lane_head][None, :] * (idx + 1.0)[:, None]).astype(np.float32)
    b_tab = np.exp(log_g[lane_head][None, :] * (RET_C - 1.0 - idx)[:, None]).astype(np.float32)
    lam = np.exp(log_g[lane_head] * RET_C).astype(np.float32)[:, None]
    bd = (lane_head[:, None] == (np.arange(512) // 64)[None, :]).astype(np.float32)
    return dtab, a_tab, b_tab, lam, bd


def _rotary_tables():
    inv_r = (1.0 / (np.float32(10000.0) ** np.linspace(0.0, 1.0, 16, dtype=np.float32))).astype(np.float32)
    inv_a = (np.float32(500000.0) ** (-np.arange(0, 16, 2, dtype=np.float32) / np.float32(16))).astype(np.float32)
    ifc = np.zeros((1, 128), np.float32)
    ifc[0, 0:16], ifc[0, 16:24] = inv_r, inv_a
    spread = np.zeros((128, 768), np.float32)
    for lane in range(256):
        spread[(lane % 32) % 16, lane] = 1.0
    for lane in range(512):
        d = lane % 64
        spread[16 + d % 8 if d < 16 else 24, 256 + lane] = 1.0
    return ifc, spread


def _rot_halves(tm):
    lo_r = (lax.broadcasted_iota(jnp.int32, (tm, 256), 1) % 32) < 16
    lo_a = (lax.broadcasted_iota(jnp.int32, (tm, 512), 1) % 64) < 8
    return lo_r, lo_a


def _spread_exact(t, e):
    hi = t.astype(BF16)
    r1 = t - hi.astype(F32)
    mid = r1.astype(BF16)
    lo = (r1 - mid.astype(F32)).astype(BF16)
    return _nn(hi, e) + _nn(mid, e) + _nn(lo, e)


def _rot_tables(cos_ref, sin_ref, e_ref):
    cs = _spread_exact(cos_ref[...], e_ref[...])
    sn = _spread_exact(sin_ref[...], e_ref[...])
    return cs[:, 0:256], cs[:, 256:768], sn[:, 0:256], sn[:, 256:768]


def _proj_own(h1, win_own, after):
    tm = 512

    def body(h_ref, w_ref, p_ref, _):
        p_ref[...] = _nn(h_ref[...], w_ref[...])

    return _carry("proj_own", body, _NoExchange(), (), (h1, win_own),
                  [pl.BlockSpec((tm, D), lambda i: (i, 0)), pl.BlockSpec((D, WIN_C), lambda i: (0, 0))],
                  [pl.BlockSpec((tm, WIN_C), lambda i: (i, 0))], [jax.ShapeDtypeStruct((S, WIN_C), F32)],
                  grid=(S // tm,), semantics=("parallel",), after=after)[0][0]


def _proj_fwd(h1, win_g, p_own, cos, sin, spread, after):
    tm = 256

    def body(h_ref, w_ref, po_ref, cos_ref, sin_ref, e_ref, qr_ref, kr_ref, rv_ref, rg_ref, aq_ref, ak_ref, av_ref,
             p_ref, _):
        h = h_ref[...]
        me = _place()[3]
        for k in range(N_CHIP):
            @pl.when(k != me)
            def _(k=k):
                p_ref[:, k * WIN_C:(k + 1) * WIN_C] = _nn(h, w_ref[k])

            @pl.when(k == me)
            def _(k=k):
                p_ref[:, k * WIN_C:(k + 1) * WIN_C] = po_ref[...]
        cr, ca, sr, sa = _rot_tables(cos_ref, sin_ref, e_ref)
        lo_r, lo_a = _rot_halves(tm)

        def rot_r(v):
            return v * cr + sr * jnp.where(lo_r, -pltpu.roll(v, 240, 1), pltpu.roll(v, 16, 1))

        def rot_a(v):
            return v * ca + sa * jnp.where(lo_a, -pltpu.roll(v, 504, 1), pltpu.roll(v, 8, 1))

        qr_ref[...] = rot_r(p_ref[:, 0:256]).astype(BF16)
        kr_ref[...] = (rot_r(p_ref[:, 256:512]) * RET_SCALE).astype(BF16)
        rv_ref[...] = p_ref[:, 512:1024].astype(BF16)
        rg_ref[...] = p_ref[:, 1024:1536]
        aq, ak = rot_a(p_ref[:, 1536:2048]), rot_a(p_ref[:, 2048:2560])
        for j in range(4):
            aq_ref[j] = aq[:, 128 * j:128 * j + 128]
            ak_ref[j] = ak[:, 128 * j:128 * j + 128]
            av_ref[j] = p_ref[:, 2560 + 128 * j:2560 + 128 * j + 128]

    row = lambda w: pl.BlockSpec((tm, w), lambda i: (i, 0))
    slab = pl.BlockSpec((4, tm, 128), lambda i: (0, i, 0))
    return _carry(
        "proj_fwd", body, _NoExchange(), (), (h1, win_g, p_own, cos, sin, spread),
        [row(D), pl.BlockSpec((N_CHIP, D, WIN_C), lambda i: (0, 0, 0)), row(WIN_C), row(128), row(128),
         pl.BlockSpec((128, 768), lambda i: (0, 0))],
        [row(256), row(256), row(512), row(512), slab, slab, slab],
        [jax.ShapeDtypeStruct((S, w), BF16) for w in (256, 256, 512)]
        + [jax.ShapeDtypeStruct((S, 512), F32)] + [jax.ShapeDtypeStruct((4, S, 128), F32)] * 3,
        scratch_shapes=[pltpu.VMEM((tm, PW), F32)], grid=(S // tm,), semantics=("parallel",), after=after)[0]


def _seg_mean(v):
    lo = lax.broadcasted_iota(jnp.int32, v.shape, 1) < 64
    s_lo = jnp.sum(jnp.where(lo, v, 0.0), axis=-1, keepdims=True)
    s_hi = jnp.sum(jnp.where(lo, 0.0, v), axis=-1, keepdims=True)
    return jnp.where(lo, s_lo, s_hi) * (1.0 / 64.0)


def _ret_fwd(qr, kr, rv, proj, tabs, exchange, exchange_args, after=None):
    C, G = RET_C, RET_PER_STEP
    steps = S // (C * G)
    dtab, a_tab, b_tab, lam, bd = tabs

    def body(q_ref, k_ref, v_ref, g_ref, dt_ref, a_ref, b_ref, lam_ref, bd_ref, o_ref, cat_ref, st_ref, R, exch):
        @pl.when(pl.program_id(0) == 0)
        def _():
            exch.start()
            R[...] = jnp.zeros_like(R)

        lane_head = lax.broadcasted_iota(jnp.int32, (C, 256), 1) // 32
        col_head = lax.broadcasted_iota(jnp.int32, (C, 256), 1) // 64
        for s in range(G):
            rows = slice(s * C, (s + 1) * C)
            q, k, v = q_ref[rows, :], k_ref[rows, :], v_ref[rows, :]
            rb = R[...].astype(BF16)
            st_ref[s] = rb
            qa = (q.astype(F32) * a_ref[...]).astype(BF16)
            cross = _nn(qa, rb)
            p = (_nt(_stack_heads(q, lane_head, n=8), k) * dt_ref[...]).astype(BF16)
            og = [cross[:, 256 * g:256 * g + 256]
                  + _unstack_heads(_nn(p[4 * C * g:4 * C * (g + 1)], v[:, 256 * g:256 * g + 256]), col_head)
                  for g in range(2)]
            kb = (k.astype(F32) * b_ref[...]).astype(BF16)
            R[...] = R[...] * lam_ref[...] + _tn(kb, v) * bd_ref[...]
            o_ref[rows, 0:256] = og[0]
            o_ref[rows, 256:512] = og[1]
            for j in range(4):
                oj = og[j // 2][:, 128 * (j % 2):128 * (j % 2) + 128]
                xc = oj - _seg_mean(oj)
                rn = xc * lax.rsqrt(_seg_mean(xc * xc) + GN_EPS)
                gj = g_ref[rows, 128 * j:128 * j + 128]
                cat_ref[rows, 128 * j:128 * j + 128] = (rn * (gj * _sigmoid(gj))).astype(BF16)

        @pl.when(pl.program_id(0) == steps - 1)
        def _():
            exch.middle()
            exch.finish()

    row = lambda w: pl.BlockSpec((C * G, w), lambda n: (n, 0))
    full = lambda a: pl.BlockSpec(a.shape, lambda n: (0,) * a.ndim)
    return _carry(
        "ret_fwd", body, exchange, exchange_args, (qr, kr, rv, proj, dtab, a_tab, b_tab, lam, bd),
        [row(256), row(256), row(512), row(512),
         full(dtab), full(a_tab), full(b_tab), full(lam), full(bd)],
        [row(512), row(512), pl.BlockSpec((G, 256, 512), lambda n: (n, 0, 0))],
        [jax.ShapeDtypeStruct((S, 512), F32), jax.ShapeDtypeStruct((S, 512), BF16),
         jax.ShapeDtypeStruct((S // C, 256, 512), BF16)],
        scratch_shapes=[pltpu.VMEM((256, 512), F32)], grid=(steps,), semantics=("arbitrary",), after=after)


def _stack_heads(v, lane_head, fill=0.0, n=4):
    return jnp.concatenate([jnp.where(lane_head == h, v, jnp.full_like(v, fill)) for h in range(n)], axis=0)


def _unstack_heads(v, lane_head, n=4):
    out = v[0:ATT_BLK]
    for h in range(1, n):
        out = jnp.where(lane_head == h, v[h * ATT_BLK:(h + 1) * ATT_BLK], out)
    return out


def _att_bias(has_prev):
    nk = 2 * ATT_BLK if has_prev else ATT_BLK
    a = lax.broadcasted_iota(jnp.int32, (4 * ATT_BLK, nk), 0) % ATT_BLK
    kk = lax.broadcasted_iota(jnp.int32, (4 * ATT_BLK, nk), 1)
    if not has_prev:
        return None, jnp.where((a - kk) >= 0, 0.0, NEG)
    dist = ATT_BLK + a - kk
    inside = (dist >= 0) & (dist <= ATT_BLK)
    return jnp.where(inside, 0.0, NEG), jnp.where(inside & (kk >= ATT_BLK), 0.0, NEG)


def _class_rows(ib, r, d):
    if d == 1:
        return pl.ds(pl.multiple_of(ib * ATT_BLK, ATT_BLK), ATT_BLK)
    return pl.ds(ib * ATT_BLK * d + r, ATT_BLK, stride=d)


def _slab_pair(ref, g, rows):
    return jnp.concatenate([ref[2 * g, rows, :], ref[2 * g + 1, rows, :]], axis=1)


def _att_blocks(d):
    nb = S // d // ATT_BLK
    return nb, nb > 1


def _att_fwd(aq, ak, av, exchange, exchange_args):
    def body(q_ref, k_ref, v_ref, o_ref, l_ref, cat_ref, xc):
        xc.start()
        lane_head = lax.broadcasted_iota(jnp.int32, (ATT_BLK, 256), 1) // 64
        for pi, d in enumerate(PATTERN_DILATIONS):
            if pi == len(PATTERN_DILATIONS) - 1:
                xc.middle()
            nb, has_prev = _att_blocks(d)
            bias_rest, bias_first = _att_bias(has_prev)

            def block(b, carry, pi=pi, d=d, nb=nb, has_prev=has_prev, bias_rest=bias_rest, bias_first=bias_first):
                r, ib = b // nb, b % nb
                rows = _class_rows(ib, r, d)
                prow = _class_rows(jnp.maximum(ib - 1, 0), r, d)
                bias = jnp.where(ib == 0, bias_first, bias_rest) if has_prev else bias_first
                for g in range(2):
                    qg = _slab_pair(q_ref, g, rows).astype(BF16)
                    kg = _slab_pair(k_ref, g, rows)
                    vg = _slab_pair(v_ref, g, rows)
                    if has_prev:
                        kg = jnp.concatenate([_slab_pair(k_ref, g, prow), kg], axis=0)
                        vg = jnp.concatenate([_slab_pair(v_ref, g, prow), vg], axis=0)
                    kg, vg = kg.astype(BF16), vg.astype(BF16)
                    s = _nt(_stack_heads(qg, lane_head), kg) * ATT_SCALE + bias
                    m = jnp.max(s, axis=-1, keepdims=True)
                    p = jnp.exp(s - m)
                    den = jnp.sum(p, axis=-1, keepdims=True)
                    og = _unstack_heads(_nn(p.astype(BF16), vg) / den, lane_head)
                    lg = _unstack_heads(jnp.broadcast_to(m + jnp.log(den), (4 * ATT_BLK, 256)), lane_head)
                    for jj in range(2):
                        j = 2 * g + jj
                        o_new, l_new = og[:, 128 * jj:128 * jj + 128], lg[:, 128 * jj:128 * jj + 128]
                        if pi > 0:
                            o_old, l_old = o_ref[j, rows, :], l_ref[j, rows, :]
                            mx = jnp.maximum(l_old, l_new)
                            ea, eb = jnp.exp(l_old - mx), jnp.exp(l_new - mx)
                            den = ea + eb
                            o_new = (ea * o_old + eb * o_new) / den
                            l_new = mx + jnp.log(den)
                        o_ref[j, rows, :] = o_new
                        l_ref[j, rows, :] = l_new
                return carry

            lax.fori_loop(0, S // ATT_BLK, block, 0, unroll=4)

        def to_cat(i, carry):
            rows = _rows(i, 256)
            for j in range(4):
                cat_ref[rows, 128 * j:128 * j + 128] = o_ref[j, rows, :].astype(BF16)
            return carry

        lax.fori_loop(0, S // 256, to_cat, 0)
        xc.finish()

    slab = jax.ShapeDtypeStruct((4, S, 128), F32)
    return _carry("att_fwd", body, exchange, exchange_args, (aq, ak, av), [VMEM] * 3, [VMEM] * 3,
                  [slab, slab, jax.ShapeDtypeStruct((S, 512), BF16)])


def _mix_fwd(cat_r, cat_a, wout, x, g2, g3, exchange, exchange_args):
    tm = 512

    def body(cr_ref, ca_ref, w_ref, x_ref, g2_ref, g3_ref, mix_ref, x2_ref, h3_ref, xc):
        @pl.when(pl.program_id(0) == 0)
        def _():
            xc.start()

        mix = _nn(cr_ref[...], w_ref[0:512, :]) + _nn(ca_ref[...], w_ref[512:1024, :])
        mix_ref[...] = mix
        x2 = x_ref[...] + mix * _rstd(mix) * g2_ref[...]
        x2_ref[...] = x2
        h3_ref[...] = (x2 * _rstd(x2) * g3_ref[...]).astype(BF16)

        @pl.when(pl.program_id(0) == S // tm - 1)
        def _():
            xc.middle()
            xc.finish()

    row = lambda w: pl.BlockSpec((tm, w), lambda i: (i, 0))
    vec = pl.BlockSpec((1, D), lambda i: (0, 0))
    return _carry("mix_fwd", body, exchange, exchange_args, (cat_r, cat_a, wout, x, g2, g3),
                  [row(512), row(512), pl.BlockSpec((D, D), lambda i: (0, 0)), row(D), vec, vec],
                  [row(D), row(D), row(D)],
                  [jax.ShapeDtypeStruct((S, D), F32), jax.ShapeDtypeStruct((S, D), F32),
                   jax.ShapeDtypeStruct((S, D), BF16)],
                  grid=(S // tm,), semantics=("arbitrary",))


def _ffn_fwd(h3, wg, wu, wd, x2, tgt, g4):
    tm = 512
    last = N_CHIP - 1

    def body(h_ref, wg_ref, wu_ref, wd_ref, x2_ref, t_ref, g_ref,
             gt_ref, up_ref, a_ref, loss_ref, dy_ref, df_ref, dg_ref, f_ref):
        k, i = pl.program_id(0), pl.program_id(1)
        h = h_ref[...]
        gt = _nt(h, wg_ref[...])
        up = _nt(h, wu_ref[...])
        gt_ref[...] = gt.astype(BF16)
        up_ref[...] = up.astype(BF16)
        a = (gt * _sigmoid(gt) * up).astype(BF16)
        a_ref[...] = a
        part = _nn(a, wd_ref[...])
        rows = _rows(i, tm)

        @pl.when(k == 0)
        def _():
            f_ref[rows, :] = part

        @pl.when((k > 0) & (k < last))
        def _():
            f_ref[rows, :] = f_ref[rows, :] + part

        @pl.when((k == last) & (i == 0))
        def _():
            loss_ref[...] = jnp.zeros_like(loss_ref)
            dg_ref[...] = jnp.zeros_like(dg_ref)

        @pl.when(k == last)
        def _():
            fv = f_ref[rows, :] + part
            r = _rstd(fv)
            fn = fv * r
            e = x2_ref[...] + fn * g_ref[...] - t_ref[...]
            loss_ref[...] = loss_ref[...] + jnp.sum(jnp.sum(e * e, axis=-1, keepdims=True), axis=0, keepdims=True)
            dy = e * (1.0 / D)
            dy_ref[...] = dy
            dg_ref[...] = dg_ref[...] + jnp.sum(dy * fn, axis=0, keepdims=True)
            t = dy * g_ref[...]
            df_ref[...] = (r * (t - fn * jnp.mean(t * fn, axis=-1, keepdims=True))).astype(BF16)

    wrow = pl.BlockSpec((None, FF_C, D), lambda k, i: (k, 0, 0))
    act = pl.BlockSpec((None, tm, FF_C), lambda k, i: (k, i, 0))
    late = pl.BlockSpec((tm, D), lambda k, i: (jnp.where(k == last, i, 0), 0))
    vec = pl.BlockSpec((1, D), lambda k, i: (0, 0))
    return pl.pallas_call(
        body, grid=(N_CHIP, S // tm), name="ffn_fwd",
        in_specs=[pl.BlockSpec((tm, D), lambda k, i: (i, 0)), wrow, wrow, wrow, late, late, vec],
        out_specs=[act, act, act, vec, late, late, vec],
        out_shape=[jax.ShapeDtypeStruct((N_CHIP, S, FF_C), BF16)] * 3
                  + [jax.ShapeDtypeStruct((1, D), F32), jax.ShapeDtypeStruct((S, D), F32),
                     jax.ShapeDtypeStruct((S, D), BF16), jax.ShapeDtypeStruct((1, D), F32)],
        scratch_shapes=[pltpu.VMEM((S, D), F32)],
        compiler_params=_params("arbitrary", "arbitrary"),
    )(h3, wg, wu, wd, x2, tgt, g4)


def _ffn_bwd_act(df, gt, up, wg, wu, wd, dy, x2, mix, g2, g3):
    tm, sub = 512, 256
    last = N_CHIP - 1

    def body(df_ref, gt_ref, up_ref, wg_ref, wu_ref, wd_ref, dy_ref, x2_ref, mix_ref, g2_ref, g3_ref,
             dgt_ref, dup_ref, dx2_ref, dmix_ref, dg3_ref, dg2_ref, dh_ref):
        k, i = pl.program_id(0), pl.program_id(1)
        parts = []
        for s in range(tm // sub):
            rows = slice(s * sub, (s + 1) * sub)
            da = _nt(df_ref[rows, :], wd_ref[...])
            gt, up = gt_ref[rows, :].astype(F32), up_ref[rows, :].astype(F32)
            sg = _sigmoid(gt)
            dup = (da * gt * sg).astype(BF16)
            dgt = (da * up * (sg * (1.0 + gt * (1.0 - sg)))).astype(BF16)
            dup_ref[rows, :] = dup
            dgt_ref[rows, :] = dgt
            parts.append(_nn(dgt, wg_ref[...]) + _nn(dup, wu_ref[...]))
        part = jnp.concatenate(parts, axis=0)
        rows = _rows(i, tm)

        @pl.when(k == 0)
        def _():
            dh_ref[rows, :] = part

        @pl.when((k > 0) & (k < last))
        def _():
            dh_ref[rows, :] = dh_ref[rows, :] + part

        @pl.when((k == last) & (i == 0))
        def _():
            dg3_ref[...] = jnp.zeros_like(dg3_ref)
            dg2_ref[...] = jnp.zeros_like(dg2_ref)

        @pl.when(k == last)
        def _():
            dh = dh_ref[rows, :] + part
            x2 = x2_ref[...]
            r3 = _rstd(x2)
            xn = x2 * r3
            dg3_ref[...] = dg3_ref[...] + jnp.sum(dh * xn, axis=0, keepdims=True)
            t = dh * g3_ref[...]
            dx2 = dy_ref[...] + r3 * (t - xn * jnp.mean(t * xn, axis=-1, keepdims=True))
            dx2_ref[...] = dx2
            mix = mix_ref[...]
            r2 = _rstd(mix)
            mn = mix * r2
            dg2_ref[...] = dg2_ref[...] + jnp.sum(dx2 * mn, axis=0, keepdims=True)
            u = dx2 * g2_ref[...]
            dmix_ref[...] = (r2 * (u - mn * jnp.mean(u * mn, axis=-1, keepdims=True))).astype(BF16)

    wrow = pl.BlockSpec((None, FF_C, D), lambda k, i: (k, 0, 0))
    act = pl.BlockSpec((None, tm, FF_C), lambda k, i: (k, i, 0))
    row = pl.BlockSpec((tm, D), lambda k, i: (i, 0))
    late = pl.BlockSpec((tm, D), lambda k, i: (jnp.where(k == last, i, 0), 0))
    vec = pl.BlockSpec((1, D), lambda k, i: (0, 0))
    return pl.pallas_call(
        body, grid=(N_CHIP, S // tm), name="ffn_bwd_act",
        in_specs=[row, act, act, wrow, wrow, wrow, late, late, late, vec, vec],
        out_specs=[act, act, late, late, vec, vec],
        out_shape=[jax.ShapeDtypeStruct((N_CHIP, S, FF_C), BF16), jax.ShapeDtypeStruct((N_CHIP, S, FF_C), BF16),
                   jax.ShapeDtypeStruct((S, D), F32), jax.ShapeDtypeStruct((S, D), BF16),
                   jax.ShapeDtypeStruct((1, D), F32), jax.ShapeDtypeStruct((1, D), F32)],
        scratch_shapes=[pltpu.VMEM((S, D), F32)],
        compiler_params=_params("arbitrary", "arbitrary"),
    )(df, gt, up, wg, wu, wd, dy, x2, mix, g2, g3)


def _ffn_bwd_w(a, df, h3, dgt, dup):
    tm = 1024
    assert S // tm == 2

    def body(a_ref, df_ref, h_ref, dgt_ref, dup_ref, dwd_ref, dwg_ref, dwu_ref, acc_d, acc_g, acc_u):
        i = pl.program_id(1)
        h = h_ref[...]
        parts = (_tn(a_ref[...], df_ref[...]), _tn(dgt_ref[...], h), _tn(dup_ref[...], h))

        @pl.when(i == 0)
        def _():
            for acc, part in zip((acc_d, acc_g, acc_u), parts):
                acc[...] = part

        @pl.when(i == S // tm - 1)
        def _():
            for out, acc, part in zip((dwd_ref, dwg_ref, dwu_ref), (acc_d, acc_g, acc_u), parts):
                out[...] = (acc[...] + part).astype(BF16)

    act = pl.BlockSpec((None, tm, FF_C), lambda k, i: (k, i, 0))
    row = pl.BlockSpec((tm, D), lambda k, i: (i, 0))
    wrow = pl.BlockSpec((None, FF_C, D), lambda k, i: (k, 0, 0))
    return pl.pallas_call(
        body, grid=(N_CHIP, S // tm), name="ffn_bwd_w",
        in_specs=[act, row, row, act, act],
        out_specs=[wrow, wrow, wrow],
        out_shape=[jax.ShapeDtypeStruct((N_CHIP, FF_C, D), BF16)] * 3,
        scratch_shapes=[pltpu.VMEM((FF_C, D), F32)] * 3,
        compiler_params=_params("parallel", "arbitrary"),
    )(a, df, h3, dgt, dup)


def _mix_bwd(dmix, cat_r, cat_a, wout, exchange, exchange_args):
    tm = 1024

    def body(dm_ref, cr_ref, ca_ref, w_ref, dret_ref, datt_ref, dw_ref, acc, xc):
        i = pl.program_id(0)

        @pl.when(i == 0)
        def _():
            xc.start()
            acc[...] = jnp.zeros_like(acc)

        dm = dm_ref[...]
        dret_ref[...] = _nt(dm, w_ref[0:512, :])
        datt = _nt(dm, w_ref[512:1024, :])
        for j in range(4):
            datt_ref[j] = datt[:, 128 * j:128 * j + 128]
        acc[0:512, :] += _tn(cr_ref[...], dm)
        acc[512:1024, :] += _tn(ca_ref[...], dm)

        @pl.when(i == S // tm - 1)
        def _():
            dw_ref[...] = acc[...].astype(BF16)
            xc.middle()
            xc.finish()

    row = lambda w: pl.BlockSpec((tm, w), lambda i: (i, 0))
    full = pl.BlockSpec((D, D), lambda i: (0, 0))
    return _carry("mix_bwd", body, exchange, exchange_args, (dmix, cat_r, cat_a, wout),
                  [row(D), row(512), row(512), full],
                  [row(512), pl.BlockSpec((4, tm, 128), lambda i: (0, i, 0)), full],
                  [jax.ShapeDtypeStruct((S, 512), F32), jax.ShapeDtypeStruct((4, S, 128), F32),
                   jax.ShapeDtypeStruct((D, D), BF16)],
                  scratch_shapes=[pltpu.VMEM((D, D), F32)], grid=(S // tm,), semantics=("arbitrary",))


def _att_bwd(aq, ak, av, datt, att_out, lse, exchange, exchange_args, after=None):
    def body(q_ref, k_ref, v_ref, do_ref, out_ref, l_ref, dq_ref, dk_ref, dv_ref, xc):
        xc.start()

        lane_head = lax.broadcasted_iota(jnp.int32, (ATT_BLK, 256), 1) // 64
        for pi, d in enumerate(PATTERN_DILATIONS):
            nb, has_prev = _att_blocks(d)
            assert pi > 0 or not has_prev
            bias_rest, bias_first = _att_bias(has_prev)

            def block(b, carry, pi=pi, d=d, nb=nb, has_prev=has_prev, bias_rest=bias_rest, bias_first=bias_first):
                r, ib = b // nb, b % nb
                rows = _class_rows(ib, r, d)
                prow = _class_rows(jnp.maximum(ib - 1, 0), r, d)
                bias = jnp.where(ib == 0, bias_first, bias_rest) if has_prev else bias_first
                for g in range(2):
                    qg = _slab_pair(q_ref, g, rows).astype(BF16)
                    kg = _slab_pair(k_ref, g, rows)
                    vg = _slab_pair(v_ref, g, rows)
                    if has_prev:
                        kg = jnp.concatenate([_slab_pair(k_ref, g, prow), kg], axis=0)
                        vg = jnp.concatenate([_slab_pair(v_ref, g, prow), vg], axis=0)
                    kg, vg = kg.astype(BF16), vg.astype(BF16)
                    dog = _slab_pair(do_ref, g, rows)
                    outg = _slab_pair(out_ref, g, rows)
                    lg = _slab_pair(l_ref, g, rows)
                    qs = _stack_heads(qg, lane_head)
                    dos = _stack_heads(dog, lane_head)
                    delta = jnp.sum(dos * jnp.concatenate([outg] * 4, axis=0), axis=-1, keepdims=True)
                    lh = jnp.max(_stack_heads(lg, lane_head, NEG), axis=-1, keepdims=True)
                    s = _nt(qs, kg) * ATT_SCALE + bias
                    p = jnp.exp(s - lh)
                    dosb = dos.astype(BF16)
                    ds = (p * (_nt(dosb, vg) - delta) * ATT_SCALE).astype(BF16)
                    dq = _unstack_heads(_nn(ds, kg), lane_head)
                    dk = _tn(ds, qs)
                    dv = _tn(p.astype(BF16), dosb)
                    for jj in range(2):
                        j, sl = 2 * g + jj, slice(128 * jj, 128 * jj + 128)
                        if pi == 0:
                            dq_ref[j, rows, :] = dq[:, sl]
                            dk_ref[j, rows, :] = dk[:, sl]
                            dv_ref[j, rows, :] = dv[:, sl]
                            continue
                        dq_ref[j, rows, :] += dq[:, sl]
                        if has_prev:
                            dk_ref[j, prow, :] += dk[0:ATT_BLK, sl]
                            dv_ref[j, prow, :] += dv[0:ATT_BLK, sl]
                            dk_ref[j, rows, :] += dk[ATT_BLK:2 * ATT_BLK, sl]
                            dv_ref[j, rows, :] += dv[ATT_BLK:2 * ATT_BLK, sl]
                        else:
                            dk_ref[j, rows, :] += dk[:, sl]
                            dv_ref[j, rows, :] += dv[:, sl]
                return carry

            lax.fori_loop(0, S // ATT_BLK, block, 0, unroll=4)
        xc.middle()
        xc.finish()

    slab = jax.ShapeDtypeStruct((4, S, 128), F32)
    return _carry("att_bwd", body, exchange, exchange_args, (aq, ak, av, datt, att_out, lse), [VMEM] * 6, [VMEM] * 3,
                  [slab, slab, slab], after=after)


def _ret_bwd(qr, kr, rv, proj, o_raw, states, dret, tabs, exchange, exchange_args, after=None):
    C, G = RET_C, RET_PER_STEP
    steps = S // (C * G)
    dtab, a_tab, b_tab, lam, bd = tabs

    def body(q_ref, k_ref, v_ref, g_ref, o_ref, st_ref, dr_ref, dt_ref, a_ref, b_ref, lam_ref, bd_ref,
             dq_ref, dk_ref, dv_ref, dg_ref, dR, exch):
        @pl.when(pl.program_id(0) == 0)
        def _():
            exch.start()
            dR[...] = jnp.zeros_like(dR)

        lane_head = lax.broadcasted_iota(jnp.int32, (C, 256), 1) // 32
        col_head = lax.broadcasted_iota(jnp.int32, (C, 256), 1) // 64
        for s in reversed(range(G)):
            rows = slice(s * C, (s + 1) * C)
            q, k, v = q_ref[rows, :], k_ref[rows, :], v_ref[rows, :]
            dos = []
            for j in range(4):
                sl = slice(128 * j, 128 * j + 128)
                oj = o_ref[rows, sl]
                xc = oj - _seg_mean(oj)
                rs = lax.rsqrt(_seg_mean(xc * xc) + GN_EPS)
                rn = xc * rs
                gj = g_ref[rows, sl]
                sg = _sigmoid(gj)
                dret = dr_ref[rows, sl]
                dg_ref[rows, sl] = dret * rn * (sg * (1.0 + gj * (1.0 - sg)))
                drn = dret * (gj * sg)
                dos.append(rs * (drn - _seg_mean(drn) - rn * _seg_mean(drn * rn)))
            do = [jnp.concatenate(dos[0:2], axis=1), jnp.concatenate(dos[2:4], axis=1)]
            do8 = jnp.concatenate(do, axis=1).astype(BF16)
            drb = dR[...].astype(BF16)
            rb = st_ref[s]
            dq = _nt(do8, rb) * a_ref[...]
            dk = _nt(v, drb) * b_ref[...]
            kb = (k.astype(F32) * b_ref[...]).astype(BF16)
            dvall = _nn(kb, drb)
            qs = _stack_heads(q, lane_head, n=8)
            dec = dt_ref[...]
            p = (_nt(qs, k) * dec).astype(BF16)
            dos = [_stack_heads(do[g], col_head).astype(BF16) for g in range(2)]
            dp = jnp.concatenate([_nt(dos[g], v[:, 256 * g:256 * g + 256]) for g in range(2)], axis=0)
            ds = (dp * dec).astype(BF16)
            dq = dq + _unstack_heads(_nn(ds, k), lane_head, n=8)
            dk = dk + _tn(ds, qs)
            dv = [dvall[:, 256 * g:256 * g + 256] + _tn(p[4 * C * g:4 * C * (g + 1)], dos[g]) for g in range(2)]
            qa = (q.astype(F32) * a_ref[...]).astype(BF16)
            dR[...] = dR[...] * lam_ref[...] + _tn(qa, do8) * bd_ref[...]
            dq_ref[rows, :] = dq
            dk_ref[rows, :] = dk
            dv_ref[rows, 0:256] = dv[0]
            dv_ref[rows, 256:512] = dv[1]

        @pl.when(pl.program_id(0) == steps - 1)
        def _():
            exch.middle()
            exch.finish()

    rev = lambda w: pl.BlockSpec((C * G, w), lambda n: (steps - 1 - n, 0))
    full = lambda a: pl.BlockSpec(a.shape, lambda n: (0,) * a.ndim)
    return _carry(
        "ret_bwd", body, exchange, exchange_args, (qr, kr, rv, proj, o_raw, states, dret, dtab, a_tab, b_tab, lam, bd),
        [rev(256), rev(256), rev(512), rev(512), rev(512),
         pl.BlockSpec((G, 256, 512), lambda n: (steps - 1 - n, 0, 0)), rev(512),
         full(dtab), full(a_tab), full(b_tab), full(lam), full(bd)],
        [rev(256), rev(256), rev(512), rev(512)],
        [jax.ShapeDtypeStruct((S, 256), F32), jax.ShapeDtypeStruct((S, 256), F32),
         jax.ShapeDtypeStruct((S, 512), F32), jax.ShapeDtypeStruct((S, 512), F32)],
        scratch_shapes=[pltpu.VMEM((256, 512), F32)], grid=(steps,), semantics=("arbitrary",), after=after)


def _rot_bwd(cos, sin, spread, dqr, dkr, drv, drg, dq_att, dk_att, dv_att):
    tm = 256

    def body(cos_ref, sin_ref, e_ref, dqr_ref, dkr_ref, drv_ref, drg_ref, dqa_ref, dka_ref, dva_ref, dp_ref):
        cr, ca, sr, sa = _rot_tables(cos_ref, sin_ref, e_ref)
        lo_r, lo_a = _rot_halves(tm)

        def unrot_r(g):
            gs = g * sr
            return g * cr + pltpu.roll(jnp.where(lo_r, -gs, 0.0), 16, 1) + pltpu.roll(jnp.where(lo_r, 0.0, gs), 240, 1)

        def unrot_a(g):
            gs = g * sa
            return g * ca + pltpu.roll(jnp.where(lo_a, -gs, 0.0), 8, 1) + pltpu.roll(jnp.where(lo_a, 0.0, gs), 504, 1)

        def wide(ref):
            return jnp.concatenate([ref[j] for j in range(4)], axis=1)

        dp_ref[:, 0:256] = unrot_r(dqr_ref[...]).astype(BF16)
        dp_ref[:, 256:512] = unrot_r(dkr_ref[...] * RET_SCALE).astype(BF16)
        dp_ref[:, 512:1024] = drv_ref[...].astype(BF16)
        dp_ref[:, 1024:1536] = drg_ref[...].astype(BF16)
        dp_ref[:, 1536:2048] = unrot_a(wide(dqa_ref)).astype(BF16)
        dp_ref[:, 2048:2560] = unrot_a(wide(dka_ref)).astype(BF16)
        dp_ref[:, 2560:3072] = wide(dva_ref).astype(BF16)

    row = lambda w: pl.BlockSpec((tm, w), lambda i: (i, 0))
    slab = pl.BlockSpec((4, tm, 128), lambda i: (0, i, 0))
    return pl.pallas_call(
        body, grid=(S // tm,), name="rot_bwd",
        in_specs=[row(128), row(128), pl.BlockSpec((128, 768), lambda i: (0, 0)),
                  row(256), row(256), row(512), row(512), slab, slab, slab],
        out_specs=row(PW), out_shape=jax.ShapeDtypeStruct((S, PW), BF16),
        compiler_params=_params("parallel"),
    )(cos, sin, spread, dqr, dkr, drv, drg, dq_att, dk_att, dv_att)


def _win_bwd_w(h1, dproj, exchange, exchange_args):
    half = D // 2

    def sibling_copy(got_ref, buf, sems, k):
        x, y, c, me, chips = _place()
        return _remote(buf.at[k, pl.ds((1 - c) * half, half), :], got_ref.at[k], sems[0].at[k], sems[1].at[k],
                       (x, y, 1 - c))

    def body(h_ref, dp_ref, dw_ref, got_ref, buf, send, recv, xc):
        k = pl.program_id(0)

        @pl.when(k == 0)
        def _():
            xc.start()

        buf[k] = _tn(h_ref[...], dp_ref[...]).astype(BF16)
        sibling_copy(got_ref, buf, (send, recv), k).start()
        dw_ref[...] = buf[k, pl.ds(lax.axis_index("c") * half, half), :]

        @pl.when(k == N_CHIP - 1)
        def _():
            for j in range(N_CHIP):
                sibling_copy(got_ref, buf, (send, recv), j).wait_recv()
                sibling_copy(got_ref, buf, (send, recv), j).wait_send()
            xc.middle()
            xc.finish()

    halves = jax.ShapeDtypeStruct((N_CHIP, half, WIN_C), BF16)
    dma = pltpu.SemaphoreType.DMA((N_CHIP,))
    return _carry(
        "win_bwd_w", body, exchange, exchange_args, (h1, dproj),
        [pl.BlockSpec((S, D), lambda k: (0, 0)), pl.BlockSpec((S, WIN_C), lambda k: (0, k))],
        [pl.BlockSpec((None, half, WIN_C), lambda k: (k, 0, 0)), ANY], [halves, halves],
        scratch_shapes=[pltpu.VMEM((N_CHIP, D, WIN_C), BF16), dma, dma], grid=(N_CHIP,), semantics=("arbitrary",))


def _in_bwd(dproj, win_g, x, dx2, g1, other_rows, after):
    tm = 512
    n = len(other_rows)

    def body(dp_ref, w_ref, x_ref, dx2_ref, g_ref, *refs):
        rows, dx_ref, blk_ref = refs[:n], refs[n], refs[n + 1]

        @pl.when(pl.program_id(0) == 0)
        def _():
            blk_ref[...] = jnp.zeros_like(blk_ref)
            for i, r_ref in enumerate(rows):
                blk_ref[i + 1:i + 2, :] = r_ref[...]

        dh = _nt(dp_ref[:, 0:WIN_C], w_ref[0])
        for k in range(1, N_CHIP):
            dh = dh + _nt(dp_ref[:, k * WIN_C:(k + 1) * WIN_C], w_ref[k])
        xv = x_ref[...]
        r = _rstd(xv)
        xn = xv * r
        blk_ref[0:1, :] = blk_ref[0:1, :] + jnp.sum(dh * xn, axis=0, keepdims=True)
        t = dh * g_ref[...]
        dx_ref[...] = dx2_ref[...] + r * (t - xn * jnp.mean(t * xn, axis=-1, keepdims=True))

    row = lambda w: pl.BlockSpec((tm, w), lambda i: (i, 0))
    vec = pl.BlockSpec((1, D), lambda i: (0, 0))
    return _carry("in_bwd", body, _NoExchange(), (), (dproj, win_g, x, dx2, g1, *other_rows),
                  [row(PW), pl.BlockSpec((N_CHIP, D, WIN_C), lambda i: (0, 0, 0)), row(D), row(D), vec] + [vec] * n,
                  [row(D), pl.BlockSpec((8, D), lambda i: (0, 0))],
                  [jax.ShapeDtypeStruct((S, D), F32), jax.ShapeDtypeStruct((8, D), F32)],
                  grid=(S // tm,), semantics=("arbitrary",), after=after)[0]


ANY = pl.BlockSpec(memory_space=pl.ANY)
VMEM = pl.BlockSpec(memory_space=pltpu.VMEM)
FLIPS = ((1, 0), (0, 1), (1, 1))


def _place():
    x, y, c = lax.axis_index("x"), lax.axis_index("y"), lax.axis_index("c")
    chips = [((1 - x) if fx else x, (1 - y) if fy else y) for fx, fy in FLIPS]
    return x, y, c, 2 * x + y, chips


def _remote(src, dst, send_sem, recv_sem, device):
    return pltpu.make_async_remote_copy(src_ref=src, dst_ref=dst, send_sem=send_sem, recv_sem=recv_sem,
                                        device_id=device, device_id_type=MESH)


class _Exchange:
    aliases = {}

    def middle(self, ins, outs, sems):
        pass


def _own_shard_to_sibling(shard_ref, gathered_ref, send_sem, recv_sem):
    x, y, c, me, chips = _place()
    return _remote(shard_ref, gathered_ref.at[me], send_sem, recv_sem, (x, y, 1 - c))


class _NoExchange(_Exchange):
    n_in = n_out = 0
    out_shape = ()
    scratch = ()

    def start(self, ins, outs, sems):
        pass

    def finish(self, ins, outs, sems):
        pass


class _ForwardGathered(_Exchange):
    def __init__(self, shards, own=True, forward=True):
        self.own, self.forward = own, forward
        n = self.n = len(shards)
        self.n_in, self.n_out = 2 * n, n
        self.out_shape = [jax.ShapeDtypeStruct((N_CHIP,) + s.shape, s.dtype) for s in shards]
        dma = pltpu.SemaphoreType.DMA
        self.scratch = [dma((3 * n,)), dma((3 * n,)), dma((n,)), dma((n,))]
        self.aliases = {n + a: a for a in range(n)}

    def _fwd(self, outs, sems, a, j, chip, half_of):
        x, y, c, me, chips = _place()
        half = outs[a].shape[1] // 2
        blk = outs[a].at[2 * chip[0] + chip[1], pl.ds(half_of * half, half), :]
        return _remote(blk, blk, sems[0].at[3 * a + j], sems[1].at[3 * a + j], (x, y, 1 - c))

    def _own(self, ins, outs, sems, a):
        return _own_shard_to_sibling(ins[a], outs[a], sems[2].at[a], sems[3].at[a])

    def start(self, ins, outs, sems):
        x, y, c, me, chips = _place()
        for a in range(self.n):
            for j, chip in enumerate(chips if self.forward else ()):
                self._fwd(outs, sems, a, j, chip, c).start()
        for a in range(self.n if self.own else 0):
            self._own(ins, outs, sems, a).start()

    def finish(self, ins, outs, sems):
        x, y, c, me, chips = _place()
        for a in range(self.n):
            for j, chip in enumerate(chips if self.forward else ()):
                self._fwd(outs, sems, a, j, chip, 1 - c).wait_recv()
        for a in range(self.n):
            for j, chip in enumerate(chips if self.forward else ()):
                self._fwd(outs, sems, a, j, chip, c).wait_send()
            if self.own:
                self._own(ins, outs, sems, a).wait()


HBM = pl.BlockSpec(memory_space=pltpu.HBM)
SEMS = pl.BlockSpec(memory_space=pltpu.SEMAPHORE)
DATAFLOW = pltpu.SideEffectType.DATAFLOW_SIDE_EFFECTING


class _OverIci:
    def __init__(self, name, sources, lands):
        self.name, self.n = name, len(sources)
        hbm = lambda t: pltpu.with_memory_space_constraint(t, pltpu.HBM)
        self.arrays = [hbm(t) for t in sources] + [hbm(t) for t in lands]

    def sent(self, src, land, a, chip):
        raise NotImplementedError

    def landed(self, land, a, chip):
        raise NotImplementedError

    def _copy(self, arr, sems, a, j, receiving):
        x, y, c, me, chips = _place()
        src, dst = self.sent(arr[a], arr[self.n + a], a, chips[j])
        if receiving:
            dst = self.landed(arr[self.n + a], a, chips[j])
        return _remote(src, dst, sems[0].at[3 * a + j], sems[1].at[3 * a + j], (*chips[j], c))

    def start(self, after):
        m = len(self.arrays)

        def body(*refs):
            arr, sems, token = refs[:m], refs[m + 1:m + 3], refs[-1]
            for a in range(self.n):
                for j in range(3):
                    self._copy(arr, sems, a, j, False).start()
            token[...] = jnp.zeros_like(token)

        dma = pltpu.SemaphoreType.DMA
        outs = pl.pallas_call(
            body, name=self.name + "_start",
            out_shape=[dma((3 * self.n,)), dma((3 * self.n,))] + [pltpu.HBM(t.shape, t.dtype) for t in self.arrays]
                      + [jax.ShapeDtypeStruct((8, 128), F32)],
            in_specs=[HBM] * m + [ANY], out_specs=[SEMS, SEMS] + [HBM] * m + [VMEM],
            input_output_aliases={i: 2 + i for i in range(m)},
            compiler_params=pltpu.CompilerParams(has_side_effects=DATAFLOW),
        )(*self.arrays, after)
        self.sems, self.arrays = outs[0:2], list(outs[2:2 + m])
        return outs[-1]

    def wait(self, after):
        m = len(self.arrays)

        def body(*refs):
            arr, sems = refs[:m], refs[m:m + 2]
            for a in range(self.n):
                for j in range(3):
                    self._copy(arr, sems, a, j, False).wait_send()
                    self._copy(arr, sems, a, j, True).wait_recv()

        outs = pl.pallas_call(
            body, name=self.name + "_wait",
            out_shape=[pltpu.HBM(t.shape, t.dtype) for t in self.arrays],
            in_specs=[HBM] * m + [SEMS, SEMS, ANY], out_specs=[HBM] * m,
            input_output_aliases={i: i for i in range(m)},
            compiler_params=pltpu.CompilerParams(has_side_effects=DATAFLOW),
        )(*self.arrays, *self.sems, after)
        return list(outs[:self.n]), list(outs[self.n:])


class _GatherOverIci(_OverIci):
    def __init__(self, name, shards):
        super().__init__(name, shards, [lax.empty((N_CHIP,) + s.shape, s.dtype) for s in shards])

    @staticmethod
    def _half(ref):
        c = lax.axis_index("c")
        half = ref.shape[-2] // 2
        return pl.ds(c * half, half)

    def sent(self, src, land, a, chip):
        return src.at[self._half(src), :], land.at[_place()[3], self._half(src), :]

    def landed(self, land, a, chip):
        return land.at[2 * chip[0] + chip[1], self._half(land), :]


class _SumOverIci(_OverIci):
    def __init__(self, name, pre):
        super().__init__(name, pre, [lax.empty(p.shape, p.dtype) for p in pre])

    def sent(self, src, land, a, chip):
        return src.at[2 * chip[0] + chip[1]], land.at[_place()[3]]

    def landed(self, land, a, chip):
        return land.at[2 * chip[0] + chip[1]]


class _HalvesToSibling(_Exchange):
    def __init__(self, grads):
        n = self.n = len(grads)
        self.n_in = self.n_out = n
        self.out_shape = [jax.ShapeDtypeStruct((N_CHIP, g.shape[1] // 2, g.shape[2]), g.dtype) for g in grads]
        self.scratch = [pltpu.SemaphoreType.DMA((n,)), pltpu.SemaphoreType.DMA((n,))]

    def _copy(self, ins, outs, sems, a):
        x, y, c, me, chips = _place()
        half = ins[a].shape[1] // 2
        return _remote(ins[a].at[:, pl.ds((1 - c) * half, half), :], outs[a], sems[0].at[a], sems[1].at[a], (x, y, 1 - c))

    def start(self, ins, outs, sems):
        for a in range(self.n):
            self._copy(ins, outs, sems, a).start()

    def finish(self, ins, outs, sems):
        for a in range(self.n):
            self._copy(ins, outs, sems, a).wait_recv()
        for a in range(self.n):
            self._copy(ins, outs, sems, a).wait_send()


class _ShareHalves(_Exchange):
    def __init__(self, fulls):
        n = self.n = len(fulls)
        self.n_in = self.n_out = n
        self.out_shape = [jax.ShapeDtypeStruct(f.shape, f.dtype) for f in fulls]
        self.scratch = [pltpu.SemaphoreType.DMA((n,)), pltpu.SemaphoreType.DMA((n,))]
        self.aliases = {a: a for a in range(n)}

    def _copy(self, outs, sems, a, half_of):
        x, y, c, me, chips = _place()
        half = outs[a].shape[0] // 2
        rows = outs[a].at[pl.ds(half_of * half, half), :]
        return _remote(rows, rows, sems[0].at[a], sems[1].at[a], (x, y, 1 - c))

    def start(self, ins, outs, sems):
        c = _place()[2]
        for a in range(self.n):
            self._copy(outs, sems, a, c).start()

    def finish(self, ins, outs, sems):
        c = _place()[2]
        for a in range(self.n):
            self._copy(outs, sems, a, 1 - c).wait_recv()
        for a in range(self.n):
            self._copy(outs, sems, a, c).wait_send()


class _GatherBlocks(_Exchange):
    def __init__(self, block):
        self.n_in = self.n_out = 1
        self.out_shape = [jax.ShapeDtypeStruct((8,) + block.shape, block.dtype)]
        dma = pltpu.SemaphoreType.DMA
        self.scratch = [dma((7,)), dma((7,)), dma]

    @staticmethod
    def _peer(f):
        x, y, c, me, chips = _place()
        return ((1 - x) if f & 4 else x, (1 - y) if f & 2 else y, (1 - c) if f & 1 else c)

    def start(self, ins, outs, sems):
        x, y, c, me, chips = _place()
        for f in range(1, 8):
            _remote(ins[0], outs[0].at[2 * me + c], sems[0].at[f - 1], sems[1].at[f - 1], self._peer(f)).start()
        pltpu.make_async_copy(ins[0], outs[0].at[2 * me + c], sems[2]).start()

    def finish(self, ins, outs, sems):
        x, y, c, me, chips = _place()
        for f in range(1, 8):
            px, py, pc = self._peer(f)
            blk = outs[0].at[4 * px + 2 * py + pc]
            _remote(blk, blk, sems[0].at[f - 1], sems[1].at[f - 1], (x, y, c)).wait_recv()
        for f in range(1, 8):
            _remote(ins[0], outs[0].at[2 * me + c], sems[0].at[f - 1], sems[1].at[f - 1], self._peer(f)).wait_send()
        pltpu.make_async_copy(ins[0], outs[0].at[2 * me + c], sems[2]).wait()


class _Both(_Exchange):
    def __init__(self, first, second):
        self.parts = (first, second)
        self.n_in, self.n_out = first.n_in + second.n_in, first.n_out + second.n_out
        self.out_shape = first.out_shape + second.out_shape
        self.scratch = first.scratch + second.scratch
        self.aliases = dict(first.aliases)
        self.aliases.update({first.n_in + i: first.n_out + o for i, o in second.aliases.items()})

    def _split(self, ins, outs, sems):
        a, b = self.parts
        return ((a, ins[:a.n_in], outs[:a.n_out], sems[:len(a.scratch)]),
                (b, ins[a.n_in:], outs[a.n_out:], sems[len(a.scratch):]))

    def start(self, ins, outs, sems):
        for ex, i, o, s in self._split(ins, outs, sems):
            ex.start(i, o, s)

    def middle(self, ins, outs, sems):
        for ex, i, o, s in self._split(ins, outs, sems):
            ex.middle(i, o, s)

    def finish(self, ins, outs, sems):
        for ex, i, o, s in self._split(ins, outs, sems):
            ex.finish(i, o, s)


class _Bound:
    def __init__(self, ex, ins, outs, sems):
        self.start = lambda: ex.start(ins, outs, sems)
        self.middle = lambda: ex.middle(ins, outs, sems)
        self.finish = lambda: ex.finish(ins, outs, sems)


def _carry(name, body, ex, ex_args, args, in_specs, out_specs, out_shape, scratch_shapes=(), grid=None, semantics=(),
           after=None):
    n_a, n_o, n_s = len(args), len(out_shape), len(scratch_shapes)
    behind = [] if after is None else [after]

    def full_body(*refs):
        p = 0
        groups = []
        for size in (n_a, ex.n_in, len(behind), n_o, ex.n_out, n_s, len(ex.scratch)):
            groups.append(refs[p:p + size])
            p += size
        a, ei, _, o, eo, s, es = groups
        body(*a, *o, *s, _Bound(ex, ei, eo, es))

    kwargs = {} if grid is None else {"grid": grid}
    outs = pl.pallas_call(
        full_body, name=name,
        in_specs=list(in_specs) + [ANY] * (ex.n_in + len(behind)), out_specs=list(out_specs) + [ANY] * ex.n_out,
        out_shape=list(out_shape) + list(ex.out_shape), scratch_shapes=list(scratch_shapes) + list(ex.scratch),
        input_output_aliases={n_a + i: n_o + o for i, o in ex.aliases.items()},
        compiler_params=_params(*semantics) if semantics else pltpu.CompilerParams(vmem_limit_bytes=VMEM_LIMIT),
        **kwargs,
    )(*args, *ex_args, *behind)
    return outs[:n_o], outs[n_o:]


def _cast_bf16(arrays, after=None):
    n = len(arrays)
    behind = [] if after is None else [after]

    def body(*refs):
        for a in range(n):
            refs[len(refs) - n + a][...] = refs[a][...].astype(BF16)

    blks = [pl.BlockSpec((t.shape[0] // 4, t.shape[1]), lambda i: (i, 0)) for t in arrays]
    return pl.pallas_call(
        body, grid=(4,), name="cast_bf16", in_specs=blks + [ANY] * len(behind), out_specs=blks,
        out_shape=[jax.ShapeDtypeStruct(t.shape, BF16) for t in arrays], compiler_params=_params("parallel"),
    )(*arrays, *behind)


def _prepare(x, g1, pos, ifc, after):
    tm = 512

    def body(x_ref, g_ref, pos_ref, ifc_ref, h_ref, cos_ref, sin_ref, _):
        xv = x_ref[...]
        h_ref[...] = (xv * _rstd(xv) * g_ref[...]).astype(BF16)
        ang = pos_ref[...].astype(F32) * ifc_ref[...]
        cos_ref[...] = jnp.cos(ang)
        sin_ref[...] = jnp.sin(ang)

    row = lambda w: pl.BlockSpec((tm, w), lambda i: (i, 0))
    const = lambda w: pl.BlockSpec((1, w), lambda i: (0, 0))
    return _carry("prepare", body, _NoExchange(), (), (x, g1, pos, ifc),
                  [row(D), const(D), row(1), const(128)], [row(D), row(128), row(128)],
                  [jax.ShapeDtypeStruct((S, D), BF16)] + [jax.ShapeDtypeStruct((S, 128), F32)] * 2,
                  grid=(S // tm,), semantics=("parallel",), after=after)[0]


def _exchange_alone(name, ex, ex_args):
    def body(xc):
        xc.start()
        xc.middle()
        xc.finish()

    return _carry(name, body, ex, ex_args, (), (), (), ())[1]


def _core_index():
    return lax.axis_index("c").astype(jnp.int32).reshape(1)


def _pair_sum(gs, gots):
    n = len(gs)

    def body(c_ref, *refs):
        for a in range(n):
            refs[2 * n + a][...] = (refs[a][...].astype(F32) + refs[n + a][...].astype(F32)).astype(BF16)

    blk = [pl.BlockSpec((None,) + g.shape[1:], lambda k, c_ref: (k, 0, 0)) for g in gots]
    mine = [b if g.shape == got.shape else pl.BlockSpec((None,) + got.shape[1:], lambda k, c_ref: (k, c_ref[0], 0))
            for g, got, b in zip(gs, gots, blk)]
    return pl.pallas_call(
        body, name=f"pair_sum_{gots[0].shape[1]}x{gots[0].shape[2]}",
        grid_spec=pltpu.PrefetchScalarGridSpec(
            num_scalar_prefetch=1, grid=(N_CHIP,), in_specs=mine + blk, out_specs=blk),
        out_shape=[jax.ShapeDtypeStruct(g.shape, BF16) for g in gots],
        compiler_params=_params("parallel"),
    )(_core_index(), *gs, *gots)


def _chip_sum(pre, parts):
    n = len(parts)
    me = 2 * lax.axis_index("x") + lax.axis_index("y")
    others = [k + (k >= me).astype(jnp.int32) for k in range(3)]
    where = jnp.stack([lax.axis_index("c"), me, *others]).astype(jnp.int32)

    def body(w_ref, *refs):
        for a in range(n):
            own, p1, p2, p3 = refs[4 * a:4 * a + 4]
            refs[4 * n + a][...] = ((own[...].astype(F32) + p1[...].astype(F32)) + p2[...].astype(F32)) + p3[...].astype(F32)

    in_specs, out_specs, operands = [], [], []
    for a in range(n):
        _, half, cc = parts[a].shape
        tr = half // 2
        in_specs += [pl.BlockSpec((None, tr, cc), lambda i, w_ref, s=s: (w_ref[s], i, 0)) for s in (1, 2, 3, 4)]
        out_specs.append(pl.BlockSpec((tr, cc), lambda i, w_ref: (2 * w_ref[0] + i, 0)))
        operands += [pre[a], parts[a], parts[a], parts[a]]
    return pl.pallas_call(
        body, name=f"chip_sum_{parts[0].shape[1]}x{parts[0].shape[2]}",
        grid_spec=pltpu.PrefetchScalarGridSpec(num_scalar_prefetch=1, grid=(2,), in_specs=in_specs, out_specs=out_specs),
        out_shape=[jax.ShapeDtypeStruct((2 * p.shape[1], p.shape[2]), F32) for p in parts],
        compiler_params=_params("parallel"),
    )(where, *operands)


def _adamw_math(w, g, m, v):
    m = ADAM_B1 * m + (1.0 - ADAM_B1) * g
    v = ADAM_B2 * v + (1.0 - ADAM_B2) * (g * g)
    m_hat = m / (1.0 - ADAM_B1 ** ADAM_STEP)
    v_hat = v / (1.0 - ADAM_B2 ** ADAM_STEP)
    delta = -ADAM_LR * (m_hat / (jnp.sqrt(v_hat) + ADAM_EPS) + ADAM_WD * w)
    return delta, m, v


def _adamw(ws, gs, ms, vs, after=None):
    n = len(ws)

    def body(*refs):
        for a in range(n):
            w_ref, g_ref, m_ref, v_ref = (refs[t * n + a] for t in range(4))
            go_ref, d_ref, nm_ref, nv_ref = refs[4 * n + 4 * a:4 * n + 4 * a + 4]
            g = g_ref[...]
            go_ref[...] = g
            d_ref[...], nm_ref[...], nv_ref[...] = _adamw_math(w_ref[...], g, m_ref[...], v_ref[...])

    blks = [pl.BlockSpec((w.shape[0] // 4, w.shape[1]), lambda i: (i, 0)) for w in ws]
    outs = _carry(f"adamw_{ws[0].shape[0]}x{ws[0].shape[1]}", body, _NoExchange(), (), (*ws, *gs, *ms, *vs),
                  blks * 4, [b for b in blks for _ in range(4)],
                  [jax.ShapeDtypeStruct(w.shape, F32) for w in ws for _ in range(4)],
                  grid=(4,), semantics=("parallel",), after=after)[0]
    return [outs[4 * a:4 * a + 4] for a in range(n)]


def _adamw_gains(gall, ws, ms, vs):
    def body(ga_ref, *refs):
        w, m, v = refs[0:4], refs[4:8], refs[8:12]
        outs, loss_ref, total = refs[12:28], refs[28], refs[29]
        g = ga_ref[0]
        for dev in range(1, 8):
            g = g + ga_ref[dev]
        total[...] = g
        for i in range(4):
            gi = total[i:i + 1, :]
            outs[i][...] = gi
            outs[4 + i][...], outs[8 + i][...], outs[12 + i][...] = _adamw_math(w[i][...], gi, m[i][...], v[i][...])
        loss_ref[...] = total[4:5, 0:128] * (0.5 / D)

    outs = pl.pallas_call(
        body, name="adamw_gains",
        out_shape=[jax.ShapeDtypeStruct((1, D), F32)] * 16 + [jax.ShapeDtypeStruct((1, 128), F32)],
        scratch_shapes=[pltpu.VMEM((8, D), F32)],
    )(gall, *ws, *ms, *vs)
    return outs[0:4], outs[4:8], outs[8:12], outs[12:16], outs[16]


def kernel(x, positions, w_in, w_out, g_pre_mix, g_post_mix, g_pre_ffn, g_post_ffn, w_gate, w_up, w_down, loss_target, m_w_in, m_w_out, m_g_pre_mix, m_g_post_mix, m_g_pre_ffn, m_g_post_ffn, m_w_gate, m_w_up, m_w_down, v_w_in, v_w_out, v_g_pre_mix, v_g_post_mix, v_g_pre_ffn, v_g_post_ffn, v_w_gate, v_w_up, v_w_down):
    tr = lambda t: jnp.swapaxes(t, 1, 2)[0]
    shards = [w_in[0], w_out[0], tr(w_gate), tr(w_up), w_down[0]]
    moms = [m_w_in[0], m_w_out[0], tr(m_w_gate), tr(m_w_up), m_w_down[0]]
    vels = [v_w_in[0], v_w_out[0], tr(v_w_gate), tr(v_w_up), v_w_down[0]]
    xs, pos, tgt = x[0], positions.reshape(S, 1), loss_target[0]
    g1, g2, g3, g4 = g_pre_mix, g_post_mix, g_pre_ffn, g_post_ffn
    tabs = tuple(jnp.asarray(t) for t in _retention_tables())
    ifc, spread = _rotary_tables()
    ifc, spread = jnp.asarray(ifc), jnp.asarray(spread, dtype=BF16)
    bf = list(_cast_bf16(shards[:1]))
    win_gather = _GatherOverIci("win_gather", bf[:1])
    token = win_gather.start(shards[0])
    bf += _cast_bf16(shards[1:], token)
    wout_gather = _GatherOverIci("wout_gather", bf[1:2])
    token = wout_gather.start(token)
    ffn_gather = _GatherOverIci("ffn_gather", bf[2:])
    token = ffn_gather.start(token)
    h1, cos, sin = _prepare(xs, g1, pos, ifc, token)
    p_own = _proj_own(h1, win_gather.arrays[0], token)
    win_sh, win_land = win_gather.wait(p_own)
    (win_g,) = _exchange_alone("forward_win", _ForwardGathered(bf[:1]), [*win_sh, *win_land])
    qr, kr, rv, rg, aq, ak, av = _proj_fwd(h1, win_g, p_own, cos, sin, spread, None)
    wout_sh, wout_land = wout_gather.wait(qr)
    n_ffn = len(bf[2:])
    (att_out, lse, cat_a), (wout_g, *ffn_gather.arrays[n_ffn:]) = _att_fwd(
        aq, ak, av, _Both(_ForwardGathered(bf[1:2]), _ForwardGathered(bf[2:], forward=False)),
        [*wout_sh, *wout_land, *ffn_gather.arrays])
    wout_g = wout_g.reshape(D, D)
    (o_raw, cat_r, states), _ = _ret_fwd(qr, kr, rv, rg, tabs, _NoExchange(), (), cat_a)
    ffn_sh, ffn_lands = ffn_gather.wait(cat_r)
    (mix, x2, h3), (wg_g, wu_g, wd_g) = _mix_fwd(cat_r, cat_a, wout_g, xs, g2, g3,
                                                _ForwardGathered(bf[2:], own=False), [*ffn_sh, *ffn_lands])
    gt, up, a, sq, dy, df, dg4 = _ffn_fwd(h3, wg_g, wu_g, wd_g, x2, tgt, g4)

    dgt, dup, dx2, dmix, dg3, dg2 = _ffn_bwd_act(df, gt, up, wg_g, wu_g, wd_g, dy, x2, mix, g2, g3)
    ffn_grads = list(_ffn_bwd_w(a, df, h3, dgt, dup))
    (dret, datt, dwout), got = _mix_bwd(dmix, cat_r, cat_a, wout_g, _HalvesToSibling(ffn_grads), ffn_grads)
    ffn_sum = _SumOverIci("ffn_sum", _pair_sum(ffn_grads, got))
    token = ffn_sum.start(datt)
    (dq_att, dk_att, dv_att), _ = _att_bwd(aq, ak, av, datt, att_out, lse, _NoExchange(), (), token)
    (dqr, dkr, drv, drg), _ = _ret_bwd(qr, kr, rv, rg, o_raw, states, dret, tabs, _NoExchange(), (), token)
    dproj = _rot_bwd(cos, sin, spread, dqr, dkr, drv, drg, dq_att, dk_att, dv_att)
    sums = _chip_sum(*ffn_sum.wait(dproj))
    dwout = dwout.reshape(N_CHIP, WOUT_R, D)
    (dwin, got_win), (*ffn_full, got_wout) = _win_bwd_w(
        h1, dproj, _Both(_ShareHalves(sums), _HalvesToSibling([dwout])), [*sums, dwout])

    in_sum = _SumOverIci("in_sum", _pair_sum([dwin, dwout], [got_win, got_wout]))
    token = in_sum.start(dproj)
    dx, gblock = _in_bwd(dproj, win_g, xs, dx2, g1, [dg2, dg3, dg4, sq], token)
    ffn_upd = _adamw(shards[2:], [ffn_full[o] for o in (1, 2, 0)],
                     moms[2:], vels[2:], token)
    pre, parts = in_sum.wait(ffn_upd[2][0])
    sums = _chip_sum(pre, parts)
    *in_full, gall = _exchange_alone("share_rest", _Both(_ShareHalves(sums), _GatherBlocks(gblock)), [*sums, gblock])
    upd = _adamw(shards[:2], in_full, moms[:2], vels[:2]) + ffn_upd
    gg, gd, gm, gv, loss_row = _adamw_gains(gall, [g1, g2, g3, g4],
                                            [m_g_pre_mix, m_g_post_mix, m_g_pre_ffn, m_g_post_ffn],
                                            [v_g_pre_mix, v_g_post_mix, v_g_pre_ffn, v_g_post_ffn])

    def order(mats, vecs):
        back = lambda t: jnp.swapaxes(t[None], 1, 2)
        return [mats[0][None], mats[1][None], *vecs, back(mats[2]), back(mats[3]), mats[4][None]]

    return (loss_row[0, 0], dx[None],
            *order([u[0] for u in upd], gg),
            *order([u[1] for u in upd], gd),
            *order([u[2] for u in upd], gm),
            *order([u[3] for u in upd], gv))
```

```python
import numpy as np
import jax
import jax.numpy as jnp
from jax import lax
from jax.experimental import pallas as pl
from jax.experimental.pallas import tpu as pltpu

F32, BF16 = jnp.float32, jnp.bfloat16
MESH = pl.DeviceIdType.MESH

S = 2048
D = 1024
PW = 3072
N_CHIP = 4
WIN_C = PW // N_CHIP
DFF = 2816
FF_C = DFF // N_CHIP
WOUT_R = D // N_CHIP
RMS_EPS = 1e-6
GN_EPS = 1e-5
RET_C = 128
RET_PER_STEP = 4
RET_SCALE = 32 ** -0.5
ATT_BLK = 128
ATT_SCALE = 64 ** -0.5
PATTERN_DILATIONS = (16, 1, 4)
NEG = -1e30
VMEM_LIMIT = 56 * 1024 * 1024

ADAM_LR, ADAM_B1, ADAM_B2, ADAM_EPS, ADAM_WD, ADAM_STEP = 0.001, 0.9, 0.999, 1e-08, 0.01, 10


def _params(*sem):
    return pltpu.CompilerParams(dimension_semantics=sem, vmem_limit_bytes=VMEM_LIMIT)


def _nt(a, b):
    return lax.dot_general(a, b, (((1,), (1,)), ((), ())), preferred_element_type=F32)


def _tn(a, b):
    return lax.dot_general(a, b, (((0,), (0,)), ((), ())), preferred_element_type=F32)


def _nn(a, b):
    return jnp.dot(a, b, preferred_element_type=F32)


def _rstd(v):
    return lax.rsqrt(jnp.mean(v * v, axis=-1, keepdims=True) + RMS_EPS)


def _sigmoid(v):
    return 1.0 / (1.0 + jnp.exp(-v))


def _rows(i, t):
    return pl.ds(pl.multiple_of(i * t, t), t)


def _retention_tables():
    h = np.arange(8, dtype=np.float32)
    log_g = np.log1p(-np.exp2(-5.0 - h)).astype(np.float32)
    idx = np.arange(RET_C, dtype=np.float32)
    diff = idx[:, None] - idx[None, :]
    dtab = np.where(diff >= 0, np.exp(log_g[:, None, None] * np.maximum(diff, 0.0)), 0.0).astype(np.float32)
    dtab = dtab.reshape(8 * RET_C, RET_C)
    lane_head = np.arange(256) // 32
    a_tab = np.exp(log_g[lane_head][None, :] * (idx + 1.0)[:, None]).astype(np.float32)
    b_tab = np.exp(log_g[lane_head][None, :] * (RET_C - 1.0 - idx)[:, None]).astype(np.float32)
    lam = np.exp(log_g[lane_head] * RET_C).astype(np.float32)[:, None]
    bd = (lane_head[:, None] == (np.arange(512) // 64)[None, :]).astype(np.float32)
    return dtab, a_tab, b_tab, lam, bd


def _rotary_tables():
    inv_r = (1.0 / (np.float32(10000.0) ** np.linspace(0.0, 1.0, 16, dtype=np.float32))).astype(np.float32)
    inv_a = (np.float32(500000.0) ** (-np.arange(0, 16, 2, dtype=np.float32) / np.float32(16))).astype(np.float32)
    ifc = np.zeros((1, 128), np.float32)
    ifc[0, 0:16], ifc[0, 16:24] = inv_r, inv_a
    spread = np.zeros((128, 768), np.float32)
    for lane in range(256):
        spread[(lane % 32) % 16, lane] = 1.0
    for lane in range(512):
        d = lane % 64
        spread[16 + d % 8 if d < 16 else 24, 256 + lane] = 1.0
    return ifc, spread


def _rot_halves(tm):
    lo_r = (lax.broadcasted_iota(jnp.int32, (tm, 256), 1) % 32) < 16
    lo_a = (lax.broadcasted_iota(jnp.int32, (tm, 512), 1) % 64) < 8
    return lo_r, lo_a


def _spread_exact(t, e):
    hi = t.astype(BF16)
    r1 = t - hi.astype(F32)
    mid = r1.astype(BF16)
    lo = (r1 - mid.astype(F32)).astype(BF16)
    return _nn(hi, e) + _nn(mid, e) + _nn(lo, e)


def _rot_tables(cos_ref, sin_ref, e_ref):
    cs = _spread_exact(cos_ref[...], e_ref[...])
    sn = _spread_exact(sin_ref[...], e_ref[...])
    return cs[:, 0:256], cs[:, 256:768], sn[:, 0:256], sn[:, 256:768]


def _proj_own(h1, win_own, after):
    tm = 512

    def body(h_ref, w_ref, p_ref, _):
        p_ref[...] = _nn(h_ref[...], w_ref[...])

    return _carry("proj_own", body, _NoExchange(), (), (h1, win_own),
                  [pl.BlockSpec((tm, D), lambda i: (i, 0)), pl.BlockSpec((D, WIN_C), lambda i: (0, 0))],
                  [pl.BlockSpec((tm, WIN_C), lambda i: (i, 0))], [jax.ShapeDtypeStruct((S, WIN_C), F32)],
                  grid=(S // tm,), semantics=("parallel",), after=after)[0][0]


def _proj_fwd(h1, win_g, p_own, cos, sin, spread, after):
    tm = 256

    def body(h_ref, w_ref, po_ref, cos_ref, sin_ref, e_ref, qr_ref, kr_ref, rv_ref, rg_ref, aq_ref, ak_ref, av_ref,
             p_ref, _):
        h = h_ref[...]
        x, y, c, me, chips = _place()
        for cx, cy in chips:
            k = 2 * cx + cy
            p_ref[k] = _nn(h, w_ref[k])
        p_ref[me] = po_ref[...]

        def cols(lo, hi):
            pieces = []
            while lo < hi:
                k, at = divmod(lo, WIN_C)
                stop = min(hi - k * WIN_C, WIN_C)
                pieces.append(p_ref[k, :, at:stop])
                lo = k * WIN_C + stop
            return pieces[0] if len(pieces) == 1 else jnp.concatenate(pieces, axis=1)

        cr, ca, sr, sa = _rot_tables(cos_ref, sin_ref, e_ref)
        lo_r, lo_a = _rot_halves(tm)

        def rot_r(v):
            return v * cr + sr * jnp.where(lo_r, -pltpu.roll(v, 240, 1), pltpu.roll(v, 16, 1))

        def rot_a(v):
            return v * ca + sa * jnp.where(lo_a, -pltpu.roll(v, 504, 1), pltpu.roll(v, 8, 1))

        qr_ref[...] = rot_r(cols(0, 256)).astype(BF16)
        kr_ref[...] = (rot_r(cols(256, 512)) * RET_SCALE).astype(BF16)
        rv_ref[...] = cols(512, 1024).astype(BF16)
        rg_ref[...] = cols(1024, 1536)
        aq, ak = rot_a(cols(1536, 2048)), rot_a(cols(2048, 2560))
        for j in range(4):
            aq_ref[j] = aq[:, 128 * j:128 * j + 128]
            ak_ref[j] = ak[:, 128 * j:128 * j + 128]
            av_ref[j] = cols(2560 + 128 * j, 2560 + 128 * j + 128)

    row = lambda w: pl.BlockSpec((tm, w), lambda i: (i, 0))
    slab = pl.BlockSpec((4, tm, 128), lambda i: (0, i, 0))
    return _carry(
        "proj_fwd", body, _NoExchange(), (), (h1, win_g, p_own, cos, sin, spread),
        [row(D), pl.BlockSpec((N_CHIP, D, WIN_C), lambda i: (0, 0, 0)), row(WIN_C), row(128), row(128),
         pl.BlockSpec((128, 768), lambda i: (0, 0))],
        [row(256), row(256), row(512), row(512), slab, slab, slab],
        [jax.ShapeDtypeStruct((S, w), BF16) for w in (256, 256, 512)]
        + [jax.ShapeDtypeStruct((S, 512), F32)] + [jax.ShapeDtypeStruct((4, S, 128), F32)] * 3,
        scratch_shapes=[pltpu.VMEM((N_CHIP, tm, WIN_C), F32)], grid=(S // tm,), semantics=("parallel",),
        after=after)[0]


def _seg_mean(v):
    lo = lax.broadcasted_iota(jnp.int32, v.shape, 1) < 64
    s_lo = jnp.sum(jnp.where(lo, v, 0.0), axis=-1, keepdims=True)
    s_hi = jnp.sum(jnp.where(lo, 0.0, v), axis=-1, keepdims=True)
    return jnp.where(lo, s_lo, s_hi) * (1.0 / 64.0)


def _ret_fwd(qr, kr, rv, proj, tabs, exchange, exchange_args, after=None):
    C, G = RET_C, RET_PER_STEP
    steps = S // (C * G)
    dtab, a_tab, b_tab, lam, bd = tabs

    def body(q_ref, k_ref, v_ref, g_ref, dt_ref, a_ref, b_ref, lam_ref, bd_ref, o_ref, cat_ref, st_ref, R, exch):
        @pl.when(pl.program_id(0) == 0)
        def _():
            exch.start()
            R[...] = jnp.zeros_like(R)

        lane_head = lax.broadcasted_iota(jnp.int32, (C, 256), 1) // 32
        col_head = lax.broadcasted_iota(jnp.int32, (C, 256), 1) // 64
        for s in range(G):
            rows = slice(s * C, (s + 1) * C)
            q, k, v = q_ref[rows, :], k_ref[rows, :], v_ref[rows, :]
            rb = R[...].astype(BF16)
            st_ref[s] = rb
            qa = (q.astype(F32) * a_ref[...]).astype(BF16)
            cross = _nn(qa, rb)
            p = (_nt(_stack_heads(q, lane_head, n=8), k) * dt_ref[...]).astype(BF16)
            og = [cross[:, 256 * g:256 * g + 256]
                  + _unstack_heads(_nn(p[4 * C * g:4 * C * (g + 1)], v[:, 256 * g:256 * g + 256]), col_head)
                  for g in range(2)]
            kb = (k.astype(F32) * b_ref[...]).astype(BF16)
            R[...] = R[...] * lam_ref[...] + _tn(kb, v) * bd_ref[...]
            o_ref[rows, 0:256] = og[0]
            o_ref[rows, 256:512] = og[1]
            for j in range(4):
                oj = og[j // 2][:, 128 * (j % 2):128 * (j % 2) + 128]
                xc = oj - _seg_mean(oj)
                rn = xc * lax.rsqrt(_seg_mean(xc * xc) + GN_EPS)
                gj = g_ref[rows, 128 * j:128 * j + 128]
                cat_ref[rows, 128 * j:128 * j + 128] = (rn * (gj * _sigmoid(gj))).astype(BF16)

        @pl.when(pl.program_id(0) == steps - 1)
        def _():
            exch.middle()
            exch.finish()

    row = lambda w: pl.BlockSpec((C * G, w), lambda n: (n, 0))
    full = lambda a: pl.BlockSpec(a.shape, lambda n: (0,) * a.ndim)
    return _carry(
        "ret_fwd", body, exchange, exchange_args, (qr, kr, rv, proj, dtab, a_tab, b_tab, lam, bd),
        [row(256), row(256), row(512), row(512),
         full(dtab), full(a_tab), full(b_tab), full(lam), full(bd)],
        [row(512), row(512), pl.BlockSpec((G, 256, 512), lambda n: (n, 0, 0))],
        [jax.ShapeDtypeStruct((S, 512), F32), jax.ShapeDtypeStruct((S, 512), BF16),
         jax.ShapeDtypeStruct((S // C, 256, 512), BF16)],
        scratch_shapes=[pltpu.VMEM((256, 512), F32)], grid=(steps,), semantics=("arbitrary",), after=after)


def _stack_heads(v, lane_head, fill=0.0, n=4):
    return jnp.concatenate([jnp.where(lane_head == h, v, jnp.full_like(v, fill)) for h in range(n)], axis=0)


def _unstack_heads(v, lane_head, n=4):
    out = v[0:ATT_BLK]
    for h in range(1, n):
        out = jnp.where(lane_head == h, v[h * ATT_BLK:(h + 1) * ATT_BLK], out)
    return out


def _att_bias(has_prev):
    nk = 2 * ATT_BLK if has_prev else ATT_BLK
    a = lax.broadcasted_iota(jnp.int32, (4 * ATT_BLK, nk), 0) % ATT_BLK
    kk = lax.broadcasted_iota(jnp.int32, (4 * ATT_BLK, nk), 1)
    if not has_prev:
        return None, jnp.where((a - kk) >= 0, 0.0, NEG)
    dist = ATT_BLK + a - kk
    inside = (dist >= 0) & (dist <= ATT_BLK)
    return jnp.where(inside, 0.0, NEG), jnp.where(inside & (kk >= ATT_BLK), 0.0, NEG)


def _class_rows(ib, r, d):
    if d == 1:
        return pl.ds(pl.multiple_of(ib * ATT_BLK, ATT_BLK), ATT_BLK)
    return pl.ds(ib * ATT_BLK * d + r, ATT_BLK, stride=d)


def _slab_pair(ref, g, rows):
    return jnp.concatenate([ref[2 * g, rows, :], ref[2 * g + 1, rows, :]], axis=1)


def _att_blocks(d):
    nb = S // d // ATT_BLK
    return nb, nb > 1


def _att_fwd(aq, ak, av, exchange, exchange_args):
    def body(q_ref, k_ref, v_ref, o_ref, l_ref, cat_ref, xc):
        xc.start()
        lane_head = lax.broadcasted_iota(jnp.int32, (ATT_BLK, 256), 1) // 64
        for pi, d in enumerate(PATTERN_DILATIONS):
            if pi == len(PATTERN_DILATIONS) - 1:
                xc.middle()
            nb, has_prev = _att_blocks(d)
            bias_rest, bias_first = _att_bias(has_prev)

            def block(b, carry, pi=pi, d=d, nb=nb, has_prev=has_prev, bias_rest=bias_rest, bias_first=bias_first):
                r, ib = b // nb, b % nb
                rows = _class_rows(ib, r, d)
                prow = _class_rows(jnp.maximum(ib - 1, 0), r, d)
                bias = jnp.where(ib == 0, bias_first, bias_rest) if has_prev else bias_first
                for g in range(2):
                    qg = _slab_pair(q_ref, g, rows).astype(BF16)
                    kg = _slab_pair(k_ref, g, rows)
                    vg = _slab_pair(v_ref, g, rows)
                    if has_prev:
                        kg = jnp.concatenate([_slab_pair(k_ref, g, prow), kg], axis=0)
                        vg = jnp.concatenate([_slab_pair(v_ref, g, prow), vg], axis=0)
                    kg, vg = kg.astype(BF16), vg.astype(BF16)
                    s = _nt(_stack_heads(qg, lane_head), kg) * ATT_SCALE + bias
                    m = jnp.max(s, axis=-1, keepdims=True)
                    p = jnp.exp(s - m)
                    den = jnp.sum(p, axis=-1, keepdims=True)
                    og = _unstack_heads(_nn(p.astype(BF16), vg) / den, lane_head)
                    lg = _unstack_heads(jnp.broadcast_to(m + jnp.log(den), (4 * ATT_BLK, 256)), lane_head)
                    for jj in range(2):
                        j = 2 * g + jj
                        o_new, l_new = og[:, 128 * jj:128 * jj + 128], lg[:, 128 * jj:128 * jj + 128]
                        if pi > 0:
                            o_old, l_old = o_ref[j, rows, :], l_ref[j, rows, :]
                            mx = jnp.maximum(l_old, l_new)
                            ea, eb = jnp.exp(l_old - mx), jnp.exp(l_new - mx)
                            den = ea + eb
                            o_new = (ea * o_old + eb * o_new) / den
                            l_new = mx + jnp.log(den)
                        o_ref[j, rows, :] = o_new
                        l_ref[j, rows, :] = l_new
                return carry

            lax.fori_loop(0, S // ATT_BLK, block, 0, unroll=4)

        def to_cat(i, carry):
            rows = _rows(i, 256)
            for j in range(4):
                cat_ref[rows, 128 * j:128 * j + 128] = o_ref[j, rows, :].astype(BF16)
            return carry

        lax.fori_loop(0, S // 256, to_cat, 0)
        xc.finish()

    slab = jax.ShapeDtypeStruct((4, S, 128), F32)
    return _carry("att_fwd", body, exchange, exchange_args, (aq, ak, av), [VMEM] * 3, [VMEM] * 3,
                  [slab, slab, jax.ShapeDtypeStruct((S, 512), BF16)])


def _mix_fwd(cat_r, cat_a, wout, x, g2, g3, exchange, exchange_args):
    tm = 512

    def body(cr_ref, ca_ref, w_ref, x_ref, g2_ref, g3_ref, mix_ref, x2_ref, h3_ref, xc):
        @pl.when(pl.program_id(0) == 0)
        def _():
            xc.start()

        mix = _nn(cr_ref[...], w_ref[0:512, :]) + _nn(ca_ref[...], w_ref[512:1024, :])
        mix_ref[...] = mix
        x2 = x_ref[...] + mix * _rstd(mix) * g2_ref[...]
        x2_ref[...] = x2
        h3_ref[...] = (x2 * _rstd(x2) * g3_ref[...]).astype(BF16)

        @pl.when(pl.program_id(0) == S // tm - 1)
        def _():
            xc.middle()
            xc.finish()

    row = lambda w: pl.BlockSpec((tm, w), lambda i: (i, 0))
    vec = pl.BlockSpec((1, D), lambda i: (0, 0))
    return _carry("mix_fwd", body, exchange, exchange_args, (cat_r, cat_a, wout, x, g2, g3),
                  [row(512), row(512), pl.BlockSpec((D, D), lambda i: (0, 0)), row(D), vec, vec],
                  [row(D), row(D), row(D)],
                  [jax.ShapeDtypeStruct((S, D), F32), jax.ShapeDtypeStruct((S, D), F32),
                   jax.ShapeDtypeStruct((S, D), BF16)],
                  grid=(S // tm,), semantics=("arbitrary",))


def _ffn_fwd(h3, wg, wu, wd, x2, tgt, g4):
    tm = 512
    last = N_CHIP - 1

    def body(h_ref, wg_ref, wu_ref, wd_ref, x2_ref, t_ref, g_ref,
             gt_ref, up_ref, a_ref, loss_ref, dy_ref, df_ref, dg_ref, f_ref):
        k, i = pl.program_id(0), pl.program_id(1)
        h = h_ref[...]
        gt = _nt(h, wg_ref[...])
        up = _nt(h, wu_ref[...])
        gt_ref[...] = gt.astype(BF16)
        up_ref[...] = up.astype(BF16)
        a = (gt * _sigmoid(gt) * up).astype(BF16)
        a_ref[...] = a
        part = _nn(a, wd_ref[...])
        rows = _rows(i, tm)

        @pl.when(k == 0)
        def _():
            f_ref[rows, :] = part

        @pl.when((k > 0) & (k < last))
        def _():
            f_ref[rows, :] = f_ref[rows, :] + part

        @pl.when((k == last) & (i == 0))
        def _():
            loss_ref[...] = jnp.zeros_like(loss_ref)
            dg_ref[...] = jnp.zeros_like(dg_ref)

        @pl.when(k == last)
        def _():
            fv = f_ref[rows, :] + part
            r = _rstd(fv)
            fn = fv * r
            e = x2_ref[...] + fn * g_ref[...] - t_ref[...]
            loss_ref[...] = loss_ref[...] + jnp.sum(jnp.sum(e * e, axis=-1, keepdims=True), axis=0, keepdims=True)
            dy = e * (1.0 / D)
            dy_ref[...] = dy
            dg_ref[...] = dg_ref[...] + jnp.sum(dy * fn, axis=0, keepdims=True)
            t = dy * g_ref[...]
            df_ref[...] = (r * (t - fn * jnp.mean(t * fn, axis=-1, keepdims=True))).astype(BF16)

    wrow = pl.BlockSpec((None, FF_C, D), lambda k, i: (k, 0, 0))
    act = pl.BlockSpec((None, tm, FF_C), lambda k, i: (k, i, 0))
    late = pl.BlockSpec((tm, D), lambda k, i: (jnp.where(k == last, i, 0), 0))
    vec = pl.BlockSpec((1, D), lambda k, i: (0, 0))
    return pl.pallas_call(
        body, grid=(N_CHIP, S // tm), name="ffn_fwd",
        in_specs=[pl.BlockSpec((tm, D), lambda k, i: (i, 0)), wrow, wrow, wrow, late, late, vec],
        out_specs=[act, act, act, vec, late, late, vec],
        out_shape=[jax.ShapeDtypeStruct((N_CHIP, S, FF_C), BF16)] * 3
                  + [jax.ShapeDtypeStruct((1, D), F32), jax.ShapeDtypeStruct((S, D), F32),
                     jax.ShapeDtypeStruct((S, D), BF16), jax.ShapeDtypeStruct((1, D), F32)],
        scratch_shapes=[pltpu.VMEM((S, D), F32)],
        compiler_params=_params("arbitrary", "arbitrary"),
    )(h3, wg, wu, wd, x2, tgt, g4)


def _ffn_bwd_act(df, gt, up, wg, wu, wd, dy, x2, mix, g2, g3):
    tm, sub = 512, 256
    last = N_CHIP - 1

    def body(df_ref, gt_ref, up_ref, wg_ref, wu_ref, wd_ref, dy_ref, x2_ref, mix_ref, g2_ref, g3_ref,
             dgt_ref, dup_ref, dx2_ref, dmix_ref, dg3_ref, dg2_ref, dh_ref):
        k, i = pl.program_id(0), pl.program_id(1)
        parts = []
        for s in range(tm // sub):
            rows = slice(s * sub, (s + 1) * sub)
            da = _nt(df_ref[rows, :], wd_ref[...])
            gt, up = gt_ref[rows, :].astype(F32), up_ref[rows, :].astype(F32)
            sg = _sigmoid(gt)
            dup = (da * gt * sg).astype(BF16)
            dgt = (da * up * (sg * (1.0 + gt * (1.0 - sg)))).astype(BF16)
            dup_ref[rows, :] = dup
            dgt_ref[rows, :] = dgt
            parts.append(_nn(dgt, wg_ref[...]) + _nn(dup, wu_ref[...]))
        part = jnp.concatenate(parts, axis=0)
        rows = _rows(i, tm)

        @pl.when(k == 0)
        def _():
            dh_ref[rows, :] = part

        @pl.when((k > 0) & (k < last))
        def _():
            dh_ref[rows, :] = dh_ref[rows, :] + part

        @pl.when((k == last) & (i == 0))
        def _():
            dg3_ref[...] = jnp.zeros_like(dg3_ref)
            dg2_ref[...] = jnp.zeros_like(dg2_ref)

        @pl.when(k == last)
        def _():
            dh = dh_ref[rows, :] + part
            x2 = x2_ref[...]
            r3 = _rstd(x2)
            xn = x2 * r3
            dg3_ref[...] = dg3_ref[...] + jnp.sum(dh * xn, axis=0, keepdims=True)
            t = dh * g3_ref[...]
            dx2 = dy_ref[...] + r3 * (t - xn * jnp.mean(t * xn, axis=-1, keepdims=True))
            dx2_ref[...] = dx2
            mix = mix_ref[...]
            r2 = _rstd(mix)
            mn = mix * r2
            dg2_ref[...] = dg2_ref[...] + jnp.sum(dx2 * mn, axis=0, keepdims=True)
            u = dx2 * g2_ref[...]
            dmix_ref[...] = (r2 * (u - mn * jnp.mean(u * mn, axis=-1, keepdims=True))).astype(BF16)

    wrow = pl.BlockSpec((None, FF_C, D), lambda k, i: (k, 0, 0))
    act = pl.BlockSpec((None, tm, FF_C), lambda k, i: (k, i, 0))
    row = pl.BlockSpec((tm, D), lambda k, i: (i, 0))
    late = pl.BlockSpec((tm, D), lambda k, i: (jnp.where(k == last, i, 0), 0))
    vec = pl.BlockSpec((1, D), lambda k, i: (0, 0))
    return pl.pallas_call(
        body, grid=(N_CHIP, S // tm), name="ffn_bwd_act",
        in_specs=[row, act, act, wrow, wrow, wrow, late, late, late, vec, vec],
        out_specs=[act, act, late, late, vec, vec],
        out_shape=[jax.ShapeDtypeStruct((N_CHIP, S, FF_C), BF16), jax.ShapeDtypeStruct((N_CHIP, S, FF_C), BF16),
                   jax.ShapeDtypeStruct((S, D), F32), jax.ShapeDtypeStruct((S, D), BF16),
                   jax.ShapeDtypeStruct((1, D), F32), jax.ShapeDtypeStruct((1, D), F32)],
        scratch_shapes=[pltpu.VMEM((S, D), F32)],
        compiler_params=_params("arbitrary", "arbitrary"),
    )(df, gt, up, wg, wu, wd, dy, x2, mix, g2, g3)


def _ffn_bwd_w(a, df, h3, dgt, dup):
    tm = 1024
    assert S // tm == 2

    def body(a_ref, df_ref, h_ref, dgt_ref, dup_ref, dwd_ref, dwg_ref, dwu_ref, acc_d, acc_g, acc_u):
        i = pl.program_id(1)
        h = h_ref[...]
        parts = (_tn(a_ref[...], df_ref[...]), _tn(dgt_ref[...], h), _tn(dup_ref[...], h))

        @pl.when(i == 0)
        def _():
            for acc, part in zip((acc_d, acc_g, acc_u), parts):
                acc[...] = part

        @pl.when(i == S // tm - 1)
        def _():
            for out, acc, part in zip((dwd_ref, dwg_ref, dwu_ref), (acc_d, acc_g, acc_u), parts):
                out[...] = (acc[...] + part).astype(BF16)

    act = pl.BlockSpec((None, tm, FF_C), lambda k, i: (k, i, 0))
    row = pl.BlockSpec((tm, D), lambda k, i: (i, 0))
    wrow = pl.BlockSpec((None, FF_C, D), lambda k, i: (k, 0, 0))
    return pl.pallas_call(
        body, grid=(N_CHIP, S // tm), name="ffn_bwd_w",
        in_specs=[act, row, row, act, act],
        out_specs=[wrow, wrow, wrow],
        out_shape=[jax.ShapeDtypeStruct((N_CHIP, FF_C, D), BF16)] * 3,
        scratch_shapes=[pltpu.VMEM((FF_C, D), F32)] * 3,
        compiler_params=_params("parallel", "arbitrary"),
    )(a, df, h3, dgt, dup)


def _mix_bwd(dmix, cat_r, cat_a, wout, exchange, exchange_args):
    tm = 1024

    def body(dm_ref, cr_ref, ca_ref, w_ref, dret_ref, datt_ref, dw_ref, acc, xc):
        i = pl.program_id(0)

        @pl.when(i == 0)
        def _():
            xc.start()
            acc[...] = jnp.zeros_like(acc)

        dm = dm_ref[...]
        dret_ref[...] = _nt(dm, w_ref[0:512, :])
        datt = _nt(dm, w_ref[512:1024, :])
        for j in range(4):
            datt_ref[j] = datt[:, 128 * j:128 * j + 128]
        acc[0:512, :] += _tn(cr_ref[...], dm)
        acc[512:1024, :] += _tn(ca_ref[...], dm)

        @pl.when(i == S // tm - 1)
        def _():
            dw_ref[...] = acc[...].astype(BF16)
            xc.middle()
            xc.finish()

    row = lambda w: pl.BlockSpec((tm, w), lambda i: (i, 0))
    full = pl.BlockSpec((D, D), lambda i: (0, 0))
    return _carry("mix_bwd", body, exchange, exchange_args, (dmix, cat_r, cat_a, wout),
                  [row(D), row(512), row(512), full],
                  [row(512), pl.BlockSpec((4, tm, 128), lambda i: (0, i, 0)), full],
                  [jax.ShapeDtypeStruct((S, 512), F32), jax.ShapeDtypeStruct((4, S, 128), F32),
                   jax.ShapeDtypeStruct((D, D), BF16)],
                  scratch_shapes=[pltpu.VMEM((D, D), F32)], grid=(S // tm,), semantics=("arbitrary",))


def _att_bwd(aq, ak, av, datt, att_out, lse, exchange, exchange_args, after=None):
    def body(q_ref, k_ref, v_ref, do_ref, out_ref, l_ref, dq_ref, dk_ref, dv_ref, xc):
        xc.start()

        lane_head = lax.broadcasted_iota(jnp.int32, (ATT_BLK, 256), 1) // 64
        for pi, d in enumerate(PATTERN_DILATIONS):
            nb, has_prev = _att_blocks(d)
            assert pi > 0 or not has_prev
            bias_rest, bias_first = _att_bias(has_prev)

            def block(b, carry, pi=pi, d=d, nb=nb, has_prev=has_prev, bias_rest=bias_rest, bias_first=bias_first):
                r, ib = b // nb, b % nb
                rows = _class_rows(ib, r, d)
                prow = _class_rows(jnp.maximum(ib - 1, 0), r, d)
                bias = jnp.where(ib == 0, bias_first, bias_rest) if has_prev else bias_first
                for g in range(2):
                    qg = _slab_pair(q_ref, g, rows).astype(BF16)
                    kg = _slab_pair(k_ref, g, rows)
                    vg = _slab_pair(v_ref, g, rows)
                    if has_prev:
                        kg = jnp.concatenate([_slab_pair(k_ref, g, prow), kg], axis=0)
                        vg = jnp.concatenate([_slab_pair(v_ref, g, prow), vg], axis=0)
                    kg, vg = kg.astype(BF16), vg.astype(BF16)
                    dog = _slab_pair(do_ref, g, rows)
                    outg = _slab_pair(out_ref, g, rows)
                    lg = _slab_pair(l_ref, g, rows)
                    qs = _stack_heads(qg, lane_head)
                    dos = _stack_heads(dog, lane_head)
                    delta = jnp.sum(dos * jnp.concatenate([outg] * 4, axis=0), axis=-1, keepdims=True)
                    lh = jnp.max(_stack_heads(lg, lane_head, NEG), axis=-1, keepdims=True)
                    s = _nt(qs, kg) * ATT_SCALE + bias
                    p = jnp.exp(s - lh)
                    dosb = dos.astype(BF16)
                    ds = (p * (_nt(dosb, vg) - delta) * ATT_SCALE).astype(BF16)
                    dq = _unstack_heads(_nn(ds, kg), lane_head)
                    dk = _tn(ds, qs)
                    dv = _tn(p.astype(BF16), dosb)
                    for jj in range(2):
                        j, sl = 2 * g + jj, slice(128 * jj, 128 * jj + 128)
                        if pi == 0:
                            dq_ref[j, rows, :] = dq[:, sl]
                            dk_ref[j, rows, :] = dk[:, sl]
                            dv_ref[j, rows, :] = dv[:, sl]
                            continue
                        dq_ref[j, rows, :] += dq[:, sl]
                        if has_prev:
                            dk_ref[j, prow, :] += dk[0:ATT_BLK, sl]
                            dv_ref[j, prow, :] += dv[0:ATT_BLK, sl]
                            dk_ref[j, rows, :] += dk[ATT_BLK:2 * ATT_BLK, sl]
                            dv_ref[j, rows, :] += dv[ATT_BLK:2 * ATT_BLK, sl]
                        else:
                            dk_ref[j, rows, :] += dk[:, sl]
                            dv_ref[j, rows, :] += dv[:, sl]
                return carry

            lax.fori_loop(0, S // ATT_BLK, block, 0, unroll=4)
        xc.middle()
        xc.finish()

    slab = jax.ShapeDtypeStruct((4, S, 128), F32)
    return _carry("att_bwd", body, exchange, exchange_args, (aq, ak, av, datt, att_out, lse), [VMEM] * 6, [VMEM] * 3,
                  [slab, slab, slab], after=after)


def _ret_bwd(qr, kr, rv, proj, o_raw, states, dret, tabs, exchange, exchange_args, after=None):
    C, G = RET_C, RET_PER_STEP
    steps = S // (C * G)
    dtab, a_tab, b_tab, lam, bd = tabs

    def body(q_ref, k_ref, v_ref, g_ref, o_ref, st_ref, dr_ref, dt_ref, a_ref, b_ref, lam_ref, bd_ref,
             dq_ref, dk_ref, dv_ref, dg_ref, dR, exch):
        @pl.when(pl.program_id(0) == 0)
        def _():
            exch.start()
            dR[...] = jnp.zeros_like(dR)

        lane_head = lax.broadcasted_iota(jnp.int32, (C, 256), 1) // 32
        col_head = lax.broadcasted_iota(jnp.int32, (C, 256), 1) // 64
        for s in reversed(range(G)):
            rows = slice(s * C, (s + 1) * C)
            q, k, v = q_ref[rows, :], k_ref[rows, :], v_ref[rows, :]
            dos = []
            for j in range(4):
                sl = slice(128 * j, 128 * j + 128)
                oj = o_ref[rows, sl]
                xc = oj - _seg_mean(oj)
                rs = lax.rsqrt(_seg_mean(xc * xc) + GN_EPS)
                rn = xc * rs
                gj = g_ref[rows, sl]
                sg = _sigmoid(gj)
                dret = dr_ref[rows, sl]
                dg_ref[rows, sl] = dret * rn * (sg * (1.0 + gj * (1.0 - sg)))
                drn = dret * (gj * sg)
                dos.append(rs * (drn - _seg_mean(drn) - rn * _seg_mean(drn * rn)))
            do = [jnp.concatenate(dos[0:2], axis=1), jnp.concatenate(dos[2:4], axis=1)]
            do8 = jnp.concatenate(do, axis=1).astype(BF16)
            drb = dR[...].astype(BF16)
            rb = st_ref[s]
            dq = _nt(do8, rb) * a_ref[...]
            dk = _nt(v, drb) * b_ref[...]
            kb = (k.astype(F32) * b_ref[...]).astype(BF16)
            dvall = _nn(kb, drb)
            qs = _stack_heads(q, lane_head, n=8)
            dec = dt_ref[...]
            p = (_nt(qs, k) * dec).astype(BF16)
            dos = [_stack_heads(do[g], col_head).astype(BF16) for g in range(2)]
            dp = jnp.concatenate([_nt(dos[g], v[:, 256 * g:256 * g + 256]) for g in range(2)], axis=0)
            ds = (dp * dec).astype(BF16)
            dq = dq + _unstack_heads(_nn(ds, k), lane_head, n=8)
            dk = dk + _tn(ds, qs)
            dv = [dvall[:, 256 * g:256 * g + 256] + _tn(p[4 * C * g:4 * C * (g + 1)], dos[g]) for g in range(2)]
            qa = (q.astype(F32) * a_ref[...]).astype(BF16)
            dR[...] = dR[...] * lam_ref[...] + _tn(qa, do8) * bd_ref[...]
            dq_ref[rows, :] = dq
            dk_ref[rows, :] = dk
            dv_ref[rows, 0:256] = dv[0]
            dv_ref[rows, 256:512] = dv[1]

        @pl.when(pl.program_id(0) == steps - 1)
        def _():
            exch.middle()
            exch.finish()

    rev = lambda w: pl.BlockSpec((C * G, w), lambda n: (steps - 1 - n, 0))
    full = lambda a: pl.BlockSpec(a.shape, lambda n: (0,) * a.ndim)
    return _carry(
        "ret_bwd", body, exchange, exchange_args, (qr, kr, rv, proj, o_raw, states, dret, dtab, a_tab, b_tab, lam, bd),
        [rev(256), rev(256), rev(512), rev(512), rev(512),
         pl.BlockSpec((G, 256, 512), lambda n: (steps - 1 - n, 0, 0)), rev(512),
         full(dtab), full(a_tab), full(b_tab), full(lam), full(bd)],
        [rev(256), rev(256), rev(512), rev(512)],
        [jax.ShapeDtypeStruct((S, 256), F32), jax.ShapeDtypeStruct((S, 256), F32),
         jax.ShapeDtypeStruct((S, 512), F32), jax.ShapeDtypeStruct((S, 512), F32)],
        scratch_shapes=[pltpu.VMEM((256, 512), F32)], grid=(steps,), semantics=("arbitrary",), after=after)


def _rot_bwd(cos, sin, spread, dqr, dkr, drv, drg, dq_att, dk_att, dv_att):
    tm = 256

    def body(cos_ref, sin_ref, e_ref, dqr_ref, dkr_ref, drv_ref, drg_ref, dqa_ref, dka_ref, dva_ref, dp_ref):
        cr, ca, sr, sa = _rot_tables(cos_ref, sin_ref, e_ref)
        lo_r, lo_a = _rot_halves(tm)

        def unrot_r(g):
            gs = g * sr
            return g * cr + pltpu.roll(jnp.where(lo_r, -gs, 0.0), 16, 1) + pltpu.roll(jnp.where(lo_r, 0.0, gs), 240, 1)

        def unrot_a(g):
            gs = g * sa
            return g * ca + pltpu.roll(jnp.where(lo_a, -gs, 0.0), 8, 1) + pltpu.roll(jnp.where(lo_a, 0.0, gs), 504, 1)

        def wide(ref):
            return jnp.concatenate([ref[j] for j in range(4)], axis=1)

        dp_ref[:, 0:256] = unrot_r(dqr_ref[...]).astype(BF16)
        dp_ref[:, 256:512] = unrot_r(dkr_ref[...] * RET_SCALE).astype(BF16)
        dp_ref[:, 512:1024] = drv_ref[...].astype(BF16)
        dp_ref[:, 1024:1536] = drg_ref[...].astype(BF16)
        dp_ref[:, 1536:2048] = unrot_a(wide(dqa_ref)).astype(BF16)
        dp_ref[:, 2048:2560] = unrot_a(wide(dka_ref)).astype(BF16)
        dp_ref[:, 2560:3072] = wide(dva_ref).astype(BF16)

    row = lambda w: pl.BlockSpec((tm, w), lambda i: (i, 0))
    slab = pl.BlockSpec((4, tm, 128), lambda i: (0, i, 0))
    return pl.pallas_call(
        body, grid=(S // tm,), name="rot_bwd",
        in_specs=[row(128), row(128), pl.BlockSpec((128, 768), lambda i: (0, 0)),
                  row(256), row(256), row(512), row(512), slab, slab, slab],
        out_specs=row(PW), out_shape=jax.ShapeDtypeStruct((S, PW), BF16),
        compiler_params=_params("parallel"),
    )(cos, sin, spread, dqr, dkr, drv, drg, dq_att, dk_att, dv_att)


def _win_bwd_w(h1, dproj, exchange, exchange_args):
    half = D // 2

    def sibling_copy(got_ref, buf, sems, k):
        x, y, c, me, chips = _place()
        return _remote(buf.at[k, pl.ds((1 - c) * half, half), :], got_ref.at[k], sems[0].at[k], sems[1].at[k],
                       (x, y, 1 - c))

    def body(h_ref, dp_ref, dw_ref, got_ref, buf, send, recv, xc):
        k = pl.program_id(0)

        @pl.when(k == 0)
        def _():
            xc.start()

        buf[k] = _tn(h_ref[...], dp_ref[...]).astype(BF16)
        sibling_copy(got_ref, buf, (send, recv), k).start()
        dw_ref[...] = buf[k, pl.ds(lax.axis_index("c") * half, half), :]

        @pl.when(k == N_CHIP - 1)
        def _():
            for j in range(N_CHIP):
                sibling_copy(got_ref, buf, (send, recv), j).wait_recv()
                sibling_copy(got_ref, buf, (send, recv), j).wait_send()
            xc.middle()
            xc.finish()

    halves = jax.ShapeDtypeStruct((N_CHIP, half, WIN_C), BF16)
    dma = pltpu.SemaphoreType.DMA((N_CHIP,))
    return _carry(
        "win_bwd_w", body, exchange, exchange_args, (h1, dproj),
        [pl.BlockSpec((S, D), lambda k: (0, 0)), pl.BlockSpec((S, WIN_C), lambda k: (0, k))],
        [pl.BlockSpec((None, half, WIN_C), lambda k: (k, 0, 0)), ANY], [halves, halves],
        scratch_shapes=[pltpu.VMEM((N_CHIP, D, WIN_C), BF16), dma, dma], grid=(N_CHIP,), semantics=("arbitrary",))


def _in_bwd(dproj, win_g, x, dx2, g1, other_rows, after):
    tm = 512
    n = len(other_rows)

    def body(dp_ref, w_ref, x_ref, dx2_ref, g_ref, *refs):
        rows, dx_ref, blk_ref = refs[:n], refs[n], refs[n + 1]

        @pl.when(pl.program_id(0) == 0)
        def _():
            blk_ref[...] = jnp.zeros_like(blk_ref)
            for i, r_ref in enumerate(rows):
                blk_ref[i + 1:i + 2, :] = r_ref[...]

        dh = _nt(dp_ref[:, 0:WIN_C], w_ref[0])
        for k in range(1, N_CHIP):
            dh = dh + _nt(dp_ref[:, k * WIN_C:(k + 1) * WIN_C], w_ref[k])
        xv = x_ref[...]
        r = _rstd(xv)
        xn = xv * r
        blk_ref[0:1, :] = blk_ref[0:1, :] + jnp.sum(dh * xn, axis=0, keepdims=True)
        t = dh * g_ref[...]
        dx_ref[...] = dx2_ref[...] + r * (t - xn * jnp.mean(t * xn, axis=-1, keepdims=True))

    row = lambda w: pl.BlockSpec((tm, w), lambda i: (i, 0))
    vec = pl.BlockSpec((1, D), lambda i: (0, 0))
    return _carry("in_bwd", body, _NoExchange(), (), (dproj, win_g, x, dx2, g1, *other_rows),
                  [row(PW), pl.BlockSpec((N_CHIP, D, WIN_C), lambda i: (0, 0, 0)), row(D), row(D), vec] + [vec] * n,
                  [row(D), pl.BlockSpec((8, D), lambda i: (0, 0))],
                  [jax.ShapeDtypeStruct((S, D), F32), jax.ShapeDtypeStruct((8, D), F32)],
                  grid=(S // tm,), semantics=("arbitrary",), after=after)[0]


ANY = pl.BlockSpec(memory_space=pl.ANY)
VMEM = pl.BlockSpec(memory_space=pltpu.VMEM)
FLIPS = ((1, 0), (0, 1), (1, 1))


def _place():
    x, y, c = lax.axis_index("x"), lax.axis_index("y"), lax.axis_index("c")
    chips = [((1 - x) if fx else x, (1 - y) if fy else y) for fx, fy in FLIPS]
    return x, y, c, 2 * x + y, chips


def _remote(src, dst, send_sem, recv_sem, device):
    return pltpu.make_async_remote_copy(src_ref=src, dst_ref=dst, send_sem=send_sem, recv_sem=recv_sem,
                                        device_id=device, device_id_type=MESH)


class _Exchange:
    aliases = {}

    def middle(self, ins, outs, sems):
        pass


def _own_shard_to_sibling(shard_ref, gathered_ref, send_sem, recv_sem):
    x, y, c, me, chips = _place()
    return _remote(shard_ref, gathered_ref.at[me], send_sem, recv_sem, (x, y, 1 - c))


class _NoExchange(_Exchange):
    n_in = n_out = 0
    out_shape = ()
    scratch = ()

    def start(self, ins, outs, sems):
        pass

    def finish(self, ins, outs, sems):
        pass


class _ForwardGathered(_Exchange):
    def __init__(self, shards, own=True, forward=True):
        self.own, self.forward = own, forward
        n = self.n = len(shards)
        self.n_in, self.n_out = 2 * n, n
        self.out_shape = [jax.ShapeDtypeStruct((N_CHIP,) + s.shape, s.dtype) for s in shards]
        dma = pltpu.SemaphoreType.DMA
        self.scratch = [dma((3 * n,)), dma((3 * n,)), dma((n,)), dma((n,))]
        self.aliases = {n + a: a for a in range(n)}

    def _fwd(self, outs, sems, a, j, chip, half_of):
        x, y, c, me, chips = _place()
        half = outs[a].shape[1] // 2
        blk = outs[a].at[2 * chip[0] + chip[1], pl.ds(half_of * half, half), :]
        return _remote(blk, blk, sems[0].at[3 * a + j], sems[1].at[3 * a + j], (x, y, 1 - c))

    def _own(self, ins, outs, sems, a):
        return _own_shard_to_sibling(ins[a], outs[a], sems[2].at[a], sems[3].at[a])

    def start(self, ins, outs, sems):
        x, y, c, me, chips = _place()
        for a in range(self.n):
            for j, chip in enumerate(chips if self.forward else ()):
                self._fwd(outs, sems, a, j, chip, c).start()
        for a in range(self.n if self.own else 0):
            self._own(ins, outs, sems, a).start()

    def finish(self, ins, outs, sems):
        x, y, c, me, chips = _place()
        for a in range(self.n):
            for j, chip in enumerate(chips if self.forward else ()):
                self._fwd(outs, sems, a, j, chip, 1 - c).wait_recv()
        for a in range(self.n):
            for j, chip in enumerate(chips if self.forward else ()):
                self._fwd(outs, sems, a, j, chip, c).wait_send()
            if self.own:
                self._own(ins, outs, sems, a).wait()


HBM = pl.BlockSpec(memory_space=pltpu.HBM)
SEMS = pl.BlockSpec(memory_space=pltpu.SEMAPHORE)
DATAFLOW = pltpu.SideEffectType.DATAFLOW_SIDE_EFFECTING


class _OverIci:
    def __init__(self, name, sources, lands):
        self.name, self.n = name, len(sources)
        hbm = lambda t: pltpu.with_memory_space_constraint(t, pltpu.HBM)
        self.arrays = [hbm(t) for t in sources] + [hbm(t) for t in lands]

    def sent(self, src, land, a, chip):
        raise NotImplementedError

    def landed(self, land, a, chip):
        raise NotImplementedError

    def _copy(self, arr, sems, a, j, receiving):
        x, y, c, me, chips = _place()
        src, dst = self.sent(arr[a], arr[self.n + a], a, chips[j])
        if receiving:
            dst = self.landed(arr[self.n + a], a, chips[j])
        return _remote(src, dst, sems[0].at[3 * a + j], sems[1].at[3 * a + j], (*chips[j], c))

    def start(self, after):
        m = len(self.arrays)

        def body(*refs):
            arr, sems, token = refs[:m], refs[m + 1:m + 3], refs[-1]
            for a in range(self.n):
                for j in range(3):
                    self._copy(arr, sems, a, j, False).start()
            token[...] = jnp.zeros_like(token)

        dma = pltpu.SemaphoreType.DMA
        outs = pl.pallas_call(
            body, name=self.name + "_start",
            out_shape=[dma((3 * self.n,)), dma((3 * self.n,))] + [pltpu.HBM(t.shape, t.dtype) for t in self.arrays]
                      + [jax.ShapeDtypeStruct((8, 128), F32)],
            in_specs=[HBM] * m + [ANY], out_specs=[SEMS, SEMS] + [HBM] * m + [VMEM],
            input_output_aliases={i: 2 + i for i in range(m)},
            compiler_params=pltpu.CompilerParams(has_side_effects=DATAFLOW),
        )(*self.arrays, after)
        self.sems, self.arrays = outs[0:2], list(outs[2:2 + m])
        return outs[-1]

    def wait(self, after):
        m = len(self.arrays)

        def body(*refs):
            arr, sems = refs[:m], refs[m:m + 2]
            for a in range(self.n):
                for j in range(3):
                    self._copy(arr, sems, a, j, False).wait_send()
                    self._copy(arr, sems, a, j, True).wait_recv()

        outs = pl.pallas_call(
            body, name=self.name + "_wait",
            out_shape=[pltpu.HBM(t.shape, t.dtype) for t in self.arrays],
            in_specs=[HBM] * m + [SEMS, SEMS, ANY], out_specs=[HBM] * m,
            input_output_aliases={i: i for i in range(m)},
            compiler_params=pltpu.CompilerParams(has_side_effects=DATAFLOW),
        )(*self.arrays, *self.sems, after)
        return list(outs[:self.n]), list(outs[self.n:])


class _GatherOverIci(_OverIci):
    def __init__(self, name, shards):
        super().__init__(name, shards, [lax.empty((N_CHIP,) + s.shape, s.dtype) for s in shards])

    @staticmethod
    def _half(ref):
        c = lax.axis_index("c")
        half = ref.shape[-2] // 2
        return pl.ds(c * half, half)

    def sent(self, src, land, a, chip):
        return src.at[self._half(src), :], land.at[_place()[3], self._half(src), :]

    def landed(self, land, a, chip):
        return land.at[2 * chip[0] + chip[1], self._half(land), :]


class _SumOverIci(_OverIci):
    def __init__(self, name, pre):
        super().__init__(name, pre, [lax.empty(p.shape, p.dtype) for p in pre])

    def sent(self, src, land, a, chip):
        return src.at[2 * chip[0] + chip[1]], land.at[_place()[3]]

    def landed(self, land, a, chip):
        return land.at[2 * chip[0] + chip[1]]


class _HalvesToSibling(_Exchange):
    def __init__(self, grads):
        n = self.n = len(grads)
        self.n_in = self.n_out = n
        self.out_shape = [jax.ShapeDtypeStruct((N_CHIP, g.shape[1] // 2, g.shape[2]), g.dtype) for g in grads]
        self.scratch = [pltpu.SemaphoreType.DMA((n,)), pltpu.SemaphoreType.DMA((n,))]

    def _copy(self, ins, outs, sems, a):
        x, y, c, me, chips = _place()
        half = ins[a].shape[1] // 2
        return _remote(ins[a].at[:, pl.ds((1 - c) * half, half), :], outs[a], sems[0].at[a], sems[1].at[a], (x, y, 1 - c))

    def start(self, ins, outs, sems):
        for a in range(self.n):
            self._copy(ins, outs, sems, a).start()

    def finish(self, ins, outs, sems):
        for a in range(self.n):
            self._copy(ins, outs, sems, a).wait_recv()
        for a in range(self.n):
            self._copy(ins, outs, sems, a).wait_send()


class _ShareHalves(_Exchange):
    def __init__(self, fulls):
        n = self.n = len(fulls)
        self.n_in = self.n_out = n
        self.out_shape = [jax.ShapeDtypeStruct(f.shape, f.dtype) for f in fulls]
        self.scratch = [pltpu.SemaphoreType.DMA((n,)), pltpu.SemaphoreType.DMA((n,))]
        self.aliases = {a: a for a in range(n)}

    def _copy(self, outs, sems, a, half_of):
        x, y, c, me, chips = _place()
        half = outs[a].shape[0] // 2
        rows = outs[a].at[pl.ds(half_of * half, half), :]
        return _remote(rows, rows, sems[0].at[a], sems[1].at[a], (x, y, 1 - c))

    def start(self, ins, outs, sems):
        c = _place()[2]
        for a in range(self.n):
            self._copy(outs, sems, a, c).start()

    def finish(self, ins, outs, sems):
        c = _place()[2]
        for a in range(self.n):
            self._copy(outs, sems, a, 1 - c).wait_recv()
        for a in range(self.n):
            self._copy(outs, sems, a, c).wait_send()


class _GatherBlocks(_Exchange):
    def __init__(self, block):
        self.n_in = self.n_out = 1
        self.out_shape = [jax.ShapeDtypeStruct((8,) + block.shape, block.dtype)]
        dma = pltpu.SemaphoreType.DMA
        self.scratch = [dma((7,)), dma((7,)), dma]

    @staticmethod
    def _peer(f):
        x, y, c, me, chips = _place()
        return ((1 - x) if f & 4 else x, (1 - y) if f & 2 else y, (1 - c) if f & 1 else c)

    def start(self, ins, outs, sems):
        x, y, c, me, chips = _place()
        for f in range(1, 8):
            _remote(ins[0], outs[0].at[2 * me + c], sems[0].at[f - 1], sems[1].at[f - 1], self._peer(f)).start()
        pltpu.make_async_copy(ins[0], outs[0].at[2 * me + c], sems[2]).start()

    def finish(self, ins, outs, sems):
        x, y, c, me, chips = _place()
        for f in range(1, 8):
            px, py, pc = self._peer(f)
            blk = outs[0].at[4 * px + 2 * py + pc]
            _remote(blk, blk, sems[0].at[f - 1], sems[1].at[f - 1], (x, y, c)).wait_recv()
        for f in range(1, 8):
            _remote(ins[0], outs[0].at[2 * me + c], sems[0].at[f - 1], sems[1].at[f - 1], self._peer(f)).wait_send()
        pltpu.make_async_copy(ins[0], outs[0].at[2 * me + c], sems[2]).wait()


class _Both(_Exchange):
    def __init__(self, first, second):
        self.parts = (first, second)
        self.n_in, self.n_out = first.n_in + second.n_in, first.n_out + second.n_out
        self.out_shape = first.out_shape + second.out_shape
        self.scratch = first.scratch + second.scratch
        self.aliases = dict(first.aliases)
        self.aliases.update({first.n_in + i: first.n_out + o for i, o in second.aliases.items()})

    def _split(self, ins, outs, sems):
        a, b = self.parts
        return ((a, ins[:a.n_in], outs[:a.n_out], sems[:len(a.scratch)]),
                (b, ins[a.n_in:], outs[a.n_out:], sems[len(a.scratch):]))

    def start(self, ins, outs, sems):
        for ex, i, o, s in self._split(ins, outs, sems):
            ex.start(i, o, s)

    def middle(self, ins, outs, sems):
        for ex, i, o, s in self._split(ins, outs, sems):
            ex.middle(i, o, s)

    def finish(self, ins, outs, sems):
        for ex, i, o, s in self._split(ins, outs, sems):
            ex.finish(i, o, s)


class _Bound:
    def __init__(self, ex, ins, outs, sems):
        self.start = lambda: ex.start(ins, outs, sems)
        self.middle = lambda: ex.middle(ins, outs, sems)
        self.finish = lambda: ex.finish(ins, outs, sems)


def _carry(name, body, ex, ex_args, args, in_specs, out_specs, out_shape, scratch_shapes=(), grid=None, semantics=(),
           after=None):
    n_a, n_o, n_s = len(args), len(out_shape), len(scratch_shapes)
    behind = [] if after is None else [after]

    def full_body(*refs):
        p = 0
        groups = []
        for size in (n_a, ex.n_in, len(behind), n_o, ex.n_out, n_s, len(ex.scratch)):
            groups.append(refs[p:p + size])
            p += size
        a, ei, _, o, eo, s, es = groups
        body(*a, *o, *s, _Bound(ex, ei, eo, es))

    kwargs = {} if grid is None else {"grid": grid}
    outs = pl.pallas_call(
        full_body, name=name,
        in_specs=list(in_specs) + [ANY] * (ex.n_in + len(behind)), out_specs=list(out_specs) + [ANY] * ex.n_out,
        out_shape=list(out_shape) + list(ex.out_shape), scratch_shapes=list(scratch_shapes) + list(ex.scratch),
        input_output_aliases={n_a + i: n_o + o for i, o in ex.aliases.items()},
        compiler_params=_params(*semantics) if semantics else pltpu.CompilerParams(vmem_limit_bytes=VMEM_LIMIT),
        **kwargs,
    )(*args, *ex_args, *behind)
    return outs[:n_o], outs[n_o:]


def _cast_bf16(arrays, after=None):
    n = len(arrays)
    behind = [] if after is None else [after]

    def body(*refs):
        for a in range(n):
            refs[len(refs) - n + a][...] = refs[a][...].astype(BF16)

    blks = [pl.BlockSpec((t.shape[0] // 4, t.shape[1]), lambda i: (i, 0)) for t in arrays]
    return pl.pallas_call(
        body, grid=(4,), name="cast_bf16", in_specs=blks + [ANY] * len(behind), out_specs=blks,
        out_shape=[jax.ShapeDtypeStruct(t.shape, BF16) for t in arrays], compiler_params=_params("parallel"),
    )(*arrays, *behind)


def _prepare(x, g1, pos, ifc, after):
    tm = 512

    def body(x_ref, g_ref, pos_ref, ifc_ref, h_ref, cos_ref, sin_ref, _):
        xv = x_ref[...]
        h_ref[...] = (xv * _rstd(xv) * g_ref[...]).astype(BF16)
        ang = pos_ref[...].astype(F32) * ifc_ref[...]
        cos_ref[...] = jnp.cos(ang)
        sin_ref[...] = jnp.sin(ang)

    row = lambda w: pl.BlockSpec((tm, w), lambda i: (i, 0))
    const = lambda w: pl.BlockSpec((1, w), lambda i: (0, 0))
    return _carry("prepare", body, _NoExchange(), (), (x, g1, pos, ifc),
                  [row(D), const(D), row(1), const(128)], [row(D), row(128), row(128)],
                  [jax.ShapeDtypeStruct((S, D), BF16)] + [jax.ShapeDtypeStruct((S, 128), F32)] * 2,
                  grid=(S // tm,), semantics=("parallel",), after=after)[0]


def _exchange_alone(name, ex, ex_args):
    def body(xc):
        xc.start()
        xc.middle()
        xc.finish()

    return _carry(name, body, ex, ex_args, (), (), (), ())[1]


def _core_index():
    return lax.axis_index("c").astype(jnp.int32).reshape(1)


def _pair_sum(gs, gots):
    n = len(gs)

    def body(c_ref, *refs):
        for a in range(n):
            refs[2 * n + a][...] = (refs[a][...].astype(F32) + refs[n + a][...].astype(F32)).astype(BF16)

    blk = [pl.BlockSpec((None,) + g.shape[1:], lambda k, c_ref: (k, 0, 0)) for g in gots]
    mine = [b if g.shape == got.shape else pl.BlockSpec((None,) + got.shape[1:], lambda k, c_ref: (k, c_ref[0], 0))
            for g, got, b in zip(gs, gots, blk)]
    return pl.pallas_call(
        body, name=f"pair_sum_{gots[0].shape[1]}x{gots[0].shape[2]}",
        grid_spec=pltpu.PrefetchScalarGridSpec(
            num_scalar_prefetch=1, grid=(N_CHIP,), in_specs=mine + blk, out_specs=blk),
        out_shape=[jax.ShapeDtypeStruct(g.shape, BF16) for g in gots],
        compiler_params=_params("parallel"),
    )(_core_index(), *gs, *gots)


def _chip_sum(pre, parts):
    n = len(parts)
    me = 2 * lax.axis_index("x") + lax.axis_index("y")
    others = [k + (k >= me).astype(jnp.int32) for k in range(3)]
    where = jnp.stack([lax.axis_index("c"), me, *others]).astype(jnp.int32)

    def body(w_ref, *refs):
        for a in range(n):
            own, p1, p2, p3 = refs[4 * a:4 * a + 4]
            refs[4 * n + a][...] = ((own[...].astype(F32) + p1[...].astype(F32)) + p2[...].astype(F32)) + p3[...].astype(F32)

    in_specs, out_specs, operands = [], [], []
    for a in range(n):
        _, half, cc = parts[a].shape
        tr = half // 2
        in_specs += [pl.BlockSpec((None, tr, cc), lambda i, w_ref, s=s: (w_ref[s], i, 0)) for s in (1, 2, 3, 4)]
        out_specs.append(pl.BlockSpec((tr, cc), lambda i, w_ref: (2 * w_ref[0] + i, 0)))
        operands += [pre[a], parts[a], parts[a], parts[a]]
    return pl.pallas_call(
        body, name=f"chip_sum_{parts[0].shape[1]}x{parts[0].shape[2]}",
        grid_spec=pltpu.PrefetchScalarGridSpec(num_scalar_prefetch=1, grid=(2,), in_specs=in_specs, out_specs=out_specs),
        out_shape=[jax.ShapeDtypeStruct((2 * p.shape[1], p.shape[2]), F32) for p in parts],
        compiler_params=_params("parallel"),
    )(where, *operands)


def _adamw_math(w, g, m, v):
    m = ADAM_B1 * m + (1.0 - ADAM_B1) * g
    v = ADAM_B2 * v + (1.0 - ADAM_B2) * (g * g)
    m_hat = m / (1.0 - ADAM_B1 ** ADAM_STEP)
    v_hat = v / (1.0 - ADAM_B2 ** ADAM_STEP)
    delta = -ADAM_LR * (m_hat / (jnp.sqrt(v_hat) + ADAM_EPS) + ADAM_WD * w)
    return delta, m, v


def _adamw(ws, gs, ms, vs, after=None):
    n = len(ws)

    def body(*refs):
        for a in range(n):
            w_ref, g_ref, m_ref, v_ref = (refs[t * n + a] for t in range(4))
            go_ref, d_ref, nm_ref, nv_ref = refs[4 * n + 4 * a:4 * n + 4 * a + 4]
            g = g_ref[...]
            go_ref[...] = g
            d_ref[...], nm_ref[...], nv_ref[...] = _adamw_math(w_ref[...], g, m_ref[...], v_ref[...])

    blks = [pl.BlockSpec((w.shape[0] // 4, w.shape[1]), lambda i: (i, 0)) for w in ws]
    outs = _carry(f"adamw_{ws[0].shape[0]}x{ws[0].shape[1]}", body, _NoExchange(), (), (*ws, *gs, *ms, *vs),
                  blks * 4, [b for b in blks for _ in range(4)],
                  [jax.ShapeDtypeStruct(w.shape, F32) for w in ws for _ in range(4)],
                  grid=(4,), semantics=("parallel",), after=after)[0]
    return [outs[4 * a:4 * a + 4] for a in range(n)]


def _adamw_gains(gall, ws, ms, vs):
    def body(ga_ref, *refs):
        w, m, v = refs[0:4], refs[4:8], refs[8:12]
        outs, loss_ref, total = refs[12:28], refs[28], refs[29]
        g = ga_ref[0]
        for dev in range(1, 8):
            g = g + ga_ref[dev]
        total[...] = g
        for i in range(4):
            gi = total[i:i + 1, :]
            outs[i][...] = gi
            outs[4 + i][...], outs[8 + i][...], outs[12 + i][...] = _adamw_math(w[i][...], gi, m[i][...], v[i][...])
        loss_ref[...] = total[4:5, 0:128] * (0.5 / D)

    outs = pl.pallas_call(
        body, name="adamw_gains",
        out_shape=[jax.ShapeDtypeStruct((1, D), F32)] * 16 + [jax.ShapeDtypeStruct((1, 128), F32)],
        scratch_shapes=[pltpu.VMEM((8, D), F32)],
    )(gall, *ws, *ms, *vs)
    return outs[0:4], outs[4:8], outs[8:12], outs[12:16], outs[16]


def kernel(x, positions, w_in, w_out, g_pre_mix, g_post_mix, g_pre_ffn, g_post_ffn, w_gate, w_up, w_down, loss_target, m_w_in, m_w_out, m_g_pre_mix, m_g_post_mix, m_g_pre_ffn, m_g_post_ffn, m_w_gate, m_w_up, m_w_down, v_w_in, v_w_out, v_g_pre_mix, v_g_post_mix, v_g_pre_ffn, v_g_post_ffn, v_w_gate, v_w_up, v_w_down):
    tr = lambda t: jnp.swapaxes(t, 1, 2)[0]
    shards = [w_in[0], w_out[0], tr(w_gate), tr(w_up), w_down[0]]
    moms = [m_w_in[0], m_w_out[0], tr(m_w_gate), tr(m_w_up), m_w_down[0]]
    vels = [v_w_in[0], v_w_out[0], tr(v_w_gate), tr(v_w_up), v_w_down[0]]
    xs, pos, tgt = x[0], positions.reshape(S, 1), loss_target[0]
    g1, g2, g3, g4 = g_pre_mix, g_post_mix, g_pre_ffn, g_post_ffn
    tabs = tuple(jnp.asarray(t) for t in _retention_tables())
    ifc, spread = _rotary_tables()
    ifc, spread = jnp.asarray(ifc), jnp.asarray(spread, dtype=BF16)
    bf = list(_cast_bf16(shards[:1]))
    win_gather = _GatherOverIci("win_gather", bf[:1])
    token = win_gather.start(shards[0])
    bf += _cast_bf16(shards[1:], token)
    wout_gather = _GatherOverIci("wout_gather", bf[1:2])
    token = wout_gather.start(token)
    ffn_gather = _GatherOverIci("ffn_gather", bf[2:])
    token = ffn_gather.start(token)
    h1, cos, sin = _prepare(xs, g1, pos, ifc, token)
    p_own = _proj_own(h1, win_gather.arrays[0], token)
    win_sh, win_land = win_gather.wait(p_own)
    (win_g,) = _exchange_alone("forward_win", _ForwardGathered(bf[:1]), [*win_sh, *win_land])
    qr, kr, rv, rg, aq, ak, av = _proj_fwd(h1, win_g, p_own, cos, sin, spread, None)
    wout_sh, wout_land = wout_gather.wait(qr)
    n_ffn = len(bf[2:])
    (att_out, lse, cat_a), (wout_g, *ffn_gather.arrays[n_ffn:]) = _att_fwd(
        aq, ak, av, _Both(_ForwardGathered(bf[1:2]), _ForwardGathered(bf[2:], forward=False)),
        [*wout_sh, *wout_land, *ffn_gather.arrays])
    wout_g = wout_g.reshape(D, D)
    (o_raw, cat_r, states), _ = _ret_fwd(qr, kr, rv, rg, tabs, _NoExchange(), (), cat_a)
    ffn_sh, ffn_lands = ffn_gather.wait(cat_r)
    (mix, x2, h3), (wg_g, wu_g, wd_g) = _mix_fwd(cat_r, cat_a, wout_g, xs, g2, g3,
                                                _ForwardGathered(bf[2:], own=False), [*ffn_sh, *ffn_lands])
    gt, up, a, sq, dy, df, dg4 = _ffn_fwd(h3, wg_g, wu_g, wd_g, x2, tgt, g4)

    dgt, dup, dx2, dmix, dg3, dg2 = _ffn_bwd_act(df, gt, up, wg_g, wu_g, wd_g, dy, x2, mix, g2, g3)
    ffn_grads = list(_ffn_bwd_w(a, df, h3, dgt, dup))
    (dret, datt, dwout), got = _mix_bwd(dmix, cat_r, cat_a, wout_g, _HalvesToSibling(ffn_grads), ffn_grads)
    ffn_sum = _SumOverIci("ffn_sum", _pair_sum(ffn_grads, got))
    token = ffn_sum.start(datt)
    (dq_att, dk_att, dv_att), _ = _att_bwd(aq, ak, av, datt, att_out, lse, _NoExchange(), (), token)
    (dqr, dkr, drv, drg), _ = _ret_bwd(qr, kr, rv, rg, o_raw, states, dret, tabs, _NoExchange(), (), token)
    dproj = _rot_bwd(cos, sin, spread, dqr, dkr, drv, drg, dq_att, dk_att, dv_att)
    sums = _chip_sum(*ffn_sum.wait(dproj))
    dwout = dwout.reshape(N_CHIP, WOUT_R, D)
    (dwin, got_win), (*ffn_full, got_wout) = _win_bwd_w(
        h1, dproj, _Both(_ShareHalves(sums), _HalvesToSibling([dwout])), [*sums, dwout])

    in_sum = _SumOverIci("in_sum", _pair_sum([dwin, dwout], [got_win, got_wout]))
    token = in_sum.start(dproj)
    dx, gblock = _in_bwd(dproj, win_g, xs, dx2, g1, [dg2, dg3, dg4, sq], token)
    ffn_upd = _adamw(shards[2:], [ffn_full[o] for o in (1, 2, 0)],
                     moms[2:], vels[2:], token)
    pre, parts = in_sum.wait(ffn_upd[2][0])
    sums = _chip_sum(pre, parts)
    *in_full, gall = _exchange_alone("share_rest", _Both(_ShareHalves(sums), _GatherBlocks(gblock)), [*sums, gblock])
    upd = _adamw(shards[:2], in_full, moms[:2], vels[:2]) + ffn_upd
    gg, gd, gm, gv, loss_row = _adamw_gains(gall, [g1, g2, g3, g4],
                                            [m_g_pre_mix, m_g_post_mix, m_g_pre_ffn, m_g_post_ffn],
                                            [v_g_pre_mix, v_g_post_mix, v_g_pre_ffn, v_g_post_ffn])

    def order(mats, vecs):
        back = lambda t: jnp.swapaxes(t[None], 1, 2)
        return [mats[0][None], mats[1][None], *vecs, back(mats[2]), back(mats[3]), mats[4][None]]

    return (loss_row[0, 0], dx[None],
            *order([u[0] for u in upd], gg),
            *order([u[1] for u in upd], gd),
            *order([u[2] for u in upd], gm),
            *order([u[3] for u in upd], gv))
```

```python
import numpy as np
import jax
import jax.numpy as jnp
from jax import lax
from jax.experimental import pallas as pl
from jax.experimental.pallas import tpu as pltpu

F32, BF16 = jnp.float32, jnp.bfloat16
MESH = pl.DeviceIdType.MESH

S = 2048
D = 1024
PW = 3072
N_CHIP = 4
WIN_C = PW // N_CHIP
DFF = 2816
FF_C = DFF // N_CHIP
WOUT_R = D // N_CHIP
RMS_EPS = 1e-6
GN_EPS = 1e-5
RET_C = 128
RET_PER_STEP = 4
RET_SCALE = 32 ** -0.5
ATT_BLK = 128
ATT_SCALE = 64 ** -0.5
PATTERN_DILATIONS = (16, 1, 4)
NEG = -1e30
VMEM_LIMIT = 56 * 1024 * 1024

ADAM_LR, ADAM_B1, ADAM_B2, ADAM_EPS, ADAM_WD, ADAM_STEP = 0.001, 0.9, 0.999, 1e-08, 0.01, 10


def _params(*sem):
    return pltpu.CompilerParams(dimension_semantics=sem, vmem_limit_bytes=VMEM_LIMIT)


def _nt(a, b):
    return lax.dot_general(a, b, (((1,), (1,)), ((), ())), preferred_element_type=F32)


def _tn(a, b):
    return lax.dot_general(a, b, (((0,), (0,)), ((), ())), preferred_element_type=F32)


def _nn(a, b):
    return jnp.dot(a, b, preferred_element_type=F32)


def _rstd(v):
    return lax.rsqrt(jnp.mean(v * v, axis=-1, keepdims=True) + RMS_EPS)


def _sigmoid(v):
    return 1.0 / (1.0 + jnp.exp(-v))


def _rows(i, t):
    return pl.ds(pl.multiple_of(i * t, t), t)


def _retention_tables():
    h = np.arange(8, dtype=np.float32)
    log_g = np.log1p(-np.exp2(-5.0 - h)).astype(np.float32)
    idx = np.arange(RET_C, dtype=np.float32)
    diff = idx[:, None] - idx[None, :]
    dtab = np.where(diff >= 0, np.exp(log_g[:, None, None] * np.maximum(diff, 0.0)), 0.0).astype(np.float32)
    dtab = dtab.reshape(8 * RET_C, RET_C)
    lane_head = np.arange(256) // 32
    a_tab = np.exp(log_g[lane_head][None, :] * (idx + 1.0)[:, None]).astype(np.float32)
    b_tab = np.exp(log_g[lane_head][None, :] * (RET_C - 1.0 - idx)[:, None]).astype(np.float32)
    lam = np.exp(log_g[lane_head] * RET_C).astype(np.float32)[:, None]
    bd = (lane_head[:, None] == (np.arange(512) // 64)[None, :]).astype(np.float32)
    return dtab, a_tab, b_tab, lam, bd


def _rotary_tables():
    inv_r = (1.0 / (np.float32(10000.0) ** np.linspace(0.0, 1.0, 16, dtype=np.float32))).astype(np.float32)
    inv_a = (np.float32(500000.0) ** (-np.arange(0, 16, 2, dtype=np.float32) / np.float32(16))).astype(np.float32)
    ifc = np.zeros((1, 128), np.float32)
    ifc[0, 0:16], ifc[0, 16:24] = inv_r, inv_a
    spread = np.zeros((128, 768), np.float32)
    for lane in range(256):
        spread[(lane % 32) % 16, lane] = 1.0
    for lane in range(512):
        d = lane % 64
        spread[16 + d % 8 if d < 16 else 24, 256 + lane] = 1.0
    return ifc, spread


def _rot_halves(tm):
    lo_r = (lax.broadcasted_iota(jnp.int32, (tm, 256), 1) % 32) < 16
    lo_a = (lax.broadcasted_iota(jnp.int32, (tm, 512), 1) % 64) < 8
    return lo_r, lo_a


def _spread_exact(t, e):
    hi = t.astype(BF16)
    r1 = t - hi.astype(F32)
    mid = r1.astype(BF16)
    lo = (r1 - mid.astype(F32)).astype(BF16)
    return _nn(hi, e) + _nn(mid, e) + _nn(lo, e)


def _rot_tables(cos_ref, sin_ref, e_ref):
    cs = _spread_exact(cos_ref[...], e_ref[...])
    sn = _spread_exact(sin_ref[...], e_ref[...])
    return cs[:, 0:256], cs[:, 256:768], sn[:, 0:256], sn[:, 256:768]


def _proj_fwd(h1, win_g, cos, sin, spread, after):
    tm = 512

    def body(h_ref, w_ref, cos_ref, sin_ref, e_ref, qr_ref, kr_ref, rv_ref, rg_ref, aq_ref, ak_ref, av_ref, p_ref, _):
        h = h_ref[...]
        for k in range(N_CHIP):
            p_ref[:, k * WIN_C:(k + 1) * WIN_C] = _nn(h, w_ref[k])
        cr, ca, sr, sa = _rot_tables(cos_ref, sin_ref, e_ref)
        lo_r, lo_a = _rot_halves(tm)

        def rot_r(v):
            return v * cr + sr * jnp.where(lo_r, -pltpu.roll(v, 240, 1), pltpu.roll(v, 16, 1))

        def rot_a(v):
            return v * ca + sa * jnp.where(lo_a, -pltpu.roll(v, 504, 1), pltpu.roll(v, 8, 1))

        qr_ref[...] = rot_r(p_ref[:, 0:256]).astype(BF16)
        kr_ref[...] = (rot_r(p_ref[:, 256:512]) * RET_SCALE).astype(BF16)
        rv_ref[...] = p_ref[:, 512:1024].astype(BF16)
        rg_ref[...] = p_ref[:, 1024:1536]
        aq, ak = rot_a(p_ref[:, 1536:2048]), rot_a(p_ref[:, 2048:2560])
        for j in range(4):
            aq_ref[j] = aq[:, 128 * j:128 * j + 128]
            ak_ref[j] = ak[:, 128 * j:128 * j + 128]
            av_ref[j] = p_ref[:, 2560 + 128 * j:2560 + 128 * j + 128]

    row = lambda w: pl.BlockSpec((tm, w), lambda i: (i, 0))
    slab = pl.BlockSpec((4, tm, 128), lambda i: (0, i, 0))
    return _carry(
        "proj_fwd", body, _NoExchange(), (), (h1, win_g, cos, sin, spread),
        [row(D), pl.BlockSpec((N_CHIP, D, WIN_C), lambda i: (0, 0, 0)), row(128), row(128),
         pl.BlockSpec((128, 768), lambda i: (0, 0))],
        [row(256), row(256), row(512), row(512), slab, slab, slab],
        [jax.ShapeDtypeStruct((S, w), BF16) for w in (256, 256, 512)]
        + [jax.ShapeDtypeStruct((S, 512), F32)] + [jax.ShapeDtypeStruct((4, S, 128), F32)] * 3,
        scratch_shapes=[pltpu.VMEM((tm, PW), F32)], grid=(S // tm,), semantics=("parallel",), after=after)[0]


def _seg_mean(v):
    lo = lax.broadcasted_iota(jnp.int32, v.shape, 1) < 64
    s_lo = jnp.sum(jnp.where(lo, v, 0.0), axis=-1, keepdims=True)
    s_hi = jnp.sum(jnp.where(lo, 0.0, v), axis=-1, keepdims=True)
    return jnp.where(lo, s_lo, s_hi) * (1.0 / 64.0)


def _ret_fwd(qr, kr, rv, proj, tabs, exchange, exchange_args, after=None):
    C, G = RET_C, RET_PER_STEP
    steps = S // (C * G)
    dtab, a_tab, b_tab, lam, bd = tabs

    def body(q_ref, k_ref, v_ref, g_ref, dt_ref, a_ref, b_ref, lam_ref, bd_ref, o_ref, cat_ref, st_ref, R, exch):
        @pl.when(pl.program_id(0) == 0)
        def _():
            exch.start()
            R[...] = jnp.zeros_like(R)

        lane_head = lax.broadcasted_iota(jnp.int32, (C, 256), 1) // 32
        col_head = lax.broadcasted_iota(jnp.int32, (C, 256), 1) // 64
        for s in range(G):
            rows = slice(s * C, (s + 1) * C)
            q, k, v = q_ref[rows, :], k_ref[rows, :], v_ref[rows, :]
            rb = R[...].astype(BF16)
            st_ref[s] = rb
            qa = (q.astype(F32) * a_ref[...]).astype(BF16)
            cross = _nn(qa, rb)
            p = (_nt(_stack_heads(q, lane_head, n=8), k) * dt_ref[...]).astype(BF16)
            og = [cross[:, 256 * g:256 * g + 256]
                  + _unstack_heads(_nn(p[4 * C * g:4 * C * (g + 1)], v[:, 256 * g:256 * g + 256]), col_head)
                  for g in range(2)]
            kb = (k.astype(F32) * b_ref[...]).astype(BF16)
            R[...] = R[...] * lam_ref[...] + _tn(kb, v) * bd_ref[...]
            o_ref[rows, 0:256] = og[0]
            o_ref[rows, 256:512] = og[1]
            for j in range(4):
                oj = og[j // 2][:, 128 * (j % 2):128 * (j % 2) + 128]
                xc = oj - _seg_mean(oj)
                rn = xc * lax.rsqrt(_seg_mean(xc * xc) + GN_EPS)
                gj = g_ref[rows, 128 * j:128 * j + 128]
                cat_ref[rows, 128 * j:128 * j + 128] = (rn * (gj * _sigmoid(gj))).astype(BF16)

        @pl.when(pl.program_id(0) == steps - 1)
        def _():
            exch.middle()
            exch.finish()

    row = lambda w: pl.BlockSpec((C * G, w), lambda n: (n, 0))
    full = lambda a: pl.BlockSpec(a.shape, lambda n: (0,) * a.ndim)
    return _carry(
        "ret_fwd", body, exchange, exchange_args, (qr, kr, rv, proj, dtab, a_tab, b_tab, lam, bd),
        [row(256), row(256), row(512), row(512),
         full(dtab), full(a_tab), full(b_tab), full(lam), full(bd)],
        [row(512), row(512), pl.BlockSpec((G, 256, 512), lambda n: (n, 0, 0))],
        [jax.ShapeDtypeStruct((S, 512), F32), jax.ShapeDtypeStruct((S, 512), BF16),
         jax.ShapeDtypeStruct((S // C, 256, 512), BF16)],
        scratch_shapes=[pltpu.VMEM((256, 512), F32)], grid=(steps,), semantics=("arbitrary",), after=after)


def _stack_heads(v, lane_head, fill=0.0, n=4):
    return jnp.concatenate([jnp.where(lane_head == h, v, jnp.full_like(v, fill)) for h in range(n)], axis=0)


def _unstack_heads(v, lane_head, n=4):
    out = v[0:ATT_BLK]
    for h in range(1, n):
        out = jnp.where(lane_head == h, v[h * ATT_BLK:(h + 1) * ATT_BLK], out)
    return out


def _att_bias(has_prev):
    nk = 2 * ATT_BLK if has_prev else ATT_BLK
    a = lax.broadcasted_iota(jnp.int32, (4 * ATT_BLK, nk), 0) % ATT_BLK
    kk = lax.broadcasted_iota(jnp.int32, (4 * ATT_BLK, nk), 1)
    if not has_prev:
        return None, jnp.where((a - kk) >= 0, 0.0, NEG)
    dist = ATT_BLK + a - kk
    inside = (dist >= 0) & (dist <= ATT_BLK)
    return jnp.where(inside, 0.0, NEG), jnp.where(inside & (kk >= ATT_BLK), 0.0, NEG)


def _class_rows(ib, r, d):
    if d == 1:
        return pl.ds(pl.multiple_of(ib * ATT_BLK, ATT_BLK), ATT_BLK)
    return pl.ds(ib * ATT_BLK * d + r, ATT_BLK, stride=d)


def _slab_pair(ref, g, rows):
    return jnp.concatenate([ref[2 * g, rows, :], ref[2 * g + 1, rows, :]], axis=1)


def _att_blocks(d):
    nb = S // d // ATT_BLK
    return nb, nb > 1


def _att_fwd(aq, ak, av, exchange, exchange_args):
    def body(q_ref, k_ref, v_ref, o_ref, l_ref, cat_ref, xc):
        xc.start()
        lane_head = lax.broadcasted_iota(jnp.int32, (ATT_BLK, 256), 1) // 64
        for pi, d in enumerate(PATTERN_DILATIONS):
            if pi == len(PATTERN_DILATIONS) - 1:
                xc.middle()
            nb, has_prev = _att_blocks(d)
            bias_rest, bias_first = _att_bias(has_prev)

            def block(b, carry, pi=pi, d=d, nb=nb, has_prev=has_prev, bias_rest=bias_rest, bias_first=bias_first):
                r, ib = b // nb, b % nb
                rows = _class_rows(ib, r, d)
                prow = _class_rows(jnp.maximum(ib - 1, 0), r, d)
                bias = jnp.where(ib == 0, bias_first, bias_rest) if has_prev else bias_first
                for g in range(2):
                    qg = _slab_pair(q_ref, g, rows).astype(BF16)
                    kg = _slab_pair(k_ref, g, rows)
                    vg = _slab_pair(v_ref, g, rows)
                    if has_prev:
                        kg = jnp.concatenate([_slab_pair(k_ref, g, prow), kg], axis=0)
                        vg = jnp.concatenate([_slab_pair(v_ref, g, prow), vg], axis=0)
                    kg, vg = kg.astype(BF16), vg.astype(BF16)
                    s = _nt(_stack_heads(qg, lane_head), kg) * ATT_SCALE + bias
                    m = jnp.max(s, axis=-1, keepdims=True)
                    p = jnp.exp(s - m)
                    den = jnp.sum(p, axis=-1, keepdims=True)
                    og = _unstack_heads(_nn(p.astype(BF16), vg) / den, lane_head)
                    lg = _unstack_heads(jnp.broadcast_to(m + jnp.log(den), (4 * ATT_BLK, 256)), lane_head)
                    for jj in range(2):
                        j = 2 * g + jj
                        o_new, l_new = og[:, 128 * jj:128 * jj + 128], lg[:, 128 * jj:128 * jj + 128]
                        if pi > 0:
                            o_old, l_old = o_ref[j, rows, :], l_ref[j, rows, :]
                            mx = jnp.maximum(l_old, l_new)
                            ea, eb = jnp.exp(l_old - mx), jnp.exp(l_new - mx)
                            den = ea + eb
                            o_new = (ea * o_old + eb * o_new) / den
                            l_new = mx + jnp.log(den)
                        o_ref[j, rows, :] = o_new
                        l_ref[j, rows, :] = l_new
                return carry

            lax.fori_loop(0, S // ATT_BLK, block, 0, unroll=4)

        def to_cat(i, carry):
            rows = _rows(i, 256)
            for j in range(4):
                cat_ref[rows, 128 * j:128 * j + 128] = o_ref[j, rows, :].astype(BF16)
            return carry

        lax.fori_loop(0, S // 256, to_cat, 0)
        xc.finish()

    slab = jax.ShapeDtypeStruct((4, S, 128), F32)
    return _carry("att_fwd", body, exchange, exchange_args, (aq, ak, av), [VMEM] * 3, [VMEM] * 3,
                  [slab, slab, jax.ShapeDtypeStruct((S, 512), BF16)])


def _mix_fwd(cat_r, cat_a, wout, x, g2, g3, exchange, exchange_args):
    tm = 512

    def body(cr_ref, ca_ref, w_ref, x_ref, g2_ref, g3_ref, mix_ref, x2_ref, h3_ref, xc):
        @pl.when(pl.program_id(0) == 0)
        def _():
            xc.start()

        mix = _nn(cr_ref[...], w_ref[0:512, :]) + _nn(ca_ref[...], w_ref[512:1024, :])
        mix_ref[...] = mix
        x2 = x_ref[...] + mix * _rstd(mix) * g2_ref[...]
        x2_ref[...] = x2
        h3_ref[...] = (x2 * _rstd(x2) * g3_ref[...]).astype(BF16)

        @pl.when(pl.program_id(0) == S // tm - 1)
        def _():
            xc.middle()
            xc.finish()

    row = lambda w: pl.BlockSpec((tm, w), lambda i: (i, 0))
    vec = pl.BlockSpec((1, D), lambda i: (0, 0))
    return _carry("mix_fwd", body, exchange, exchange_args, (cat_r, cat_a, wout, x, g2, g3),
                  [row(512), row(512), pl.BlockSpec((D, D), lambda i: (0, 0)), row(D), vec, vec],
                  [row(D), row(D), row(D)],
                  [jax.ShapeDtypeStruct((S, D), F32), jax.ShapeDtypeStruct((S, D), F32),
                   jax.ShapeDtypeStruct((S, D), BF16)],
                  grid=(S // tm,), semantics=("arbitrary",))


def _ffn_fwd(h3, wg, wu, wd, x2, tgt, g4):
    tm = 512
    last = N_CHIP - 1

    def body(h_ref, wg_ref, wu_ref, wd_ref, x2_ref, t_ref, g_ref,
             gt_ref, up_ref, a_ref, loss_ref, dy_ref, df_ref, dg_ref, f_ref):
        k, i = pl.program_id(0), pl.program_id(1)
        h = h_ref[...]
        gt = _nt(h, wg_ref[...])
        up = _nt(h, wu_ref[...])
        gt_ref[...] = gt.astype(BF16)
        up_ref[...] = up.astype(BF16)
        a = (gt * _sigmoid(gt) * up).astype(BF16)
        a_ref[...] = a
        part = _nn(a, wd_ref[...])
        rows = _rows(i, tm)

        @pl.when(k == 0)
        def _():
            f_ref[rows, :] = part

        @pl.when((k > 0) & (k < last))
        def _():
            f_ref[rows, :] = f_ref[rows, :] + part

        @pl.when((k == last) & (i == 0))
        def _():
            loss_ref[...] = jnp.zeros_like(loss_ref)
            dg_ref[...] = jnp.zeros_like(dg_ref)

        @pl.when(k == last)
        def _():
            fv = f_ref[rows, :] + part
            r = _rstd(fv)
            fn = fv * r
            e = x2_ref[...] + fn * g_ref[...] - t_ref[...]
            loss_ref[...] = loss_ref[...] + jnp.sum(jnp.sum(e * e, axis=-1, keepdims=True), axis=0, keepdims=True)
            dy = e * (1.0 / D)
            dy_ref[...] = dy
            dg_ref[...] = dg_ref[...] + jnp.sum(dy * fn, axis=0, keepdims=True)
            t = dy * g_ref[...]
            df_ref[...] = (r * (t - fn * jnp.mean(t * fn, axis=-1, keepdims=True))).astype(BF16)

    wrow = pl.BlockSpec((None, FF_C, D), lambda k, i: (k, 0, 0))
    act = pl.BlockSpec((None, tm, FF_C), lambda k, i: (k, i, 0))
    late = pl.BlockSpec((tm, D), lambda k, i: (jnp.where(k == last, i, 0), 0))
    vec = pl.BlockSpec((1, D), lambda k, i: (0, 0))
    return pl.pallas_call(
        body, grid=(N_CHIP, S // tm), name="ffn_fwd",
        in_specs=[pl.BlockSpec((tm, D), lambda k, i: (i, 0)), wrow, wrow, wrow, late, late, vec],
        out_specs=[act, act, act, vec, late, late, vec],
        out_shape=[jax.ShapeDtypeStruct((N_CHIP, S, FF_C), BF16)] * 3
                  + [jax.ShapeDtypeStruct((1, D), F32), jax.ShapeDtypeStruct((S, D), F32),
                     jax.ShapeDtypeStruct((S, D), BF16), jax.ShapeDtypeStruct((1, D), F32)],
        scratch_shapes=[pltpu.VMEM((S, D), F32)],
        compiler_params=_params("arbitrary", "arbitrary"),
    )(h3, wg, wu, wd, x2, tgt, g4)


def _ffn_bwd_act(df, gt, up, wg, wu, wd, dy, x2, mix, g2, g3):
    tm, sub = 512, 256
    last = N_CHIP - 1

    def body(df_ref, gt_ref, up_ref, wg_ref, wu_ref, wd_ref, dy_ref, x2_ref, mix_ref, g2_ref, g3_ref,
             dgt_ref, dup_ref, dx2_ref, dmix_ref, dg3_ref, dg2_ref, dh_ref):
        k, i = pl.program_id(0), pl.program_id(1)
        parts = []
        for s in range(tm // sub):
            rows = slice(s * sub, (s + 1) * sub)
            da = _nt(df_ref[rows, :], wd_ref[...])
            gt, up = gt_ref[rows, :].astype(F32), up_ref[rows, :].astype(F32)
            sg = _sigmoid(gt)
            dup = (da * gt * sg).astype(BF16)
            dgt = (da * up * (sg * (1.0 + gt * (1.0 - sg)))).astype(BF16)
            dup_ref[rows, :] = dup
            dgt_ref[rows, :] = dgt
            parts.append(_nn(dgt, wg_ref[...]) + _nn(dup, wu_ref[...]))
        part = jnp.concatenate(parts, axis=0)
        rows = _rows(i, tm)

        @pl.when(k == 0)
        def _():
            dh_ref[rows, :] = part

        @pl.when((k > 0) & (k < last))
        def _():
            dh_ref[rows, :] = dh_ref[rows, :] + part

        @pl.when((k == last) & (i == 0))
        def _():
            dg3_ref[...] = jnp.zeros_like(dg3_ref)
            dg2_ref[...] = jnp.zeros_like(dg2_ref)

        @pl.when(k == last)
        def _():
            dh = dh_ref[rows, :] + part
            x2 = x2_ref[...]
            r3 = _rstd(x2)
            xn = x2 * r3
            dg3_ref[...] = dg3_ref[...] + jnp.sum(dh * xn, axis=0, keepdims=True)
            t = dh * g3_ref[...]
            dx2 = dy_ref[...] + r3 * (t - xn * jnp.mean(t * xn, axis=-1, keepdims=True))
            dx2_ref[...] = dx2
            mix = mix_ref[...]
            r2 = _rstd(mix)
            mn = mix * r2
            dg2_ref[...] = dg2_ref[...] + jnp.sum(dx2 * mn, axis=0, keepdims=True)
            u = dx2 * g2_ref[...]
            dmix_ref[...] = (r2 * (u - mn * jnp.mean(u * mn, axis=-1, keepdims=True))).astype(BF16)

    wrow = pl.BlockSpec((None, FF_C, D), lambda k, i: (k, 0, 0))
    act = pl.BlockSpec((None, tm, FF_C), lambda k, i: (k, i, 0))
    row = pl.BlockSpec((tm, D), lambda k, i: (i, 0))
    late = pl.BlockSpec((tm, D), lambda k, i: (jnp.where(k == last, i, 0), 0))
    vec = pl.BlockSpec((1, D), lambda k, i: (0, 0))
    return pl.pallas_call(
        body, grid=(N_CHIP, S // tm), name="ffn_bwd_act",
        in_specs=[row, act, act, wrow, wrow, wrow, late, late, late, vec, vec],
        out_specs=[act, act, late, late, vec, vec],
        out_shape=[jax.ShapeDtypeStruct((N_CHIP, S, FF_C), BF16), jax.ShapeDtypeStruct((N_CHIP, S, FF_C), BF16),
                   jax.ShapeDtypeStruct((S, D), F32), jax.ShapeDtypeStruct((S, D), BF16),
                   jax.ShapeDtypeStruct((1, D), F32), jax.ShapeDtypeStruct((1, D), F32)],
        scratch_shapes=[pltpu.VMEM((S, D), F32)],
        compiler_params=_params("arbitrary", "arbitrary"),
    )(df, gt, up, wg, wu, wd, dy, x2, mix, g2, g3)


def _ffn_bwd_w(a, df, h3, dgt, dup):
    tm = 1024
    assert S // tm == 2

    def body(a_ref, df_ref, h_ref, dgt_ref, dup_ref, dwd_ref, dwg_ref, dwu_ref, acc_d, acc_g, acc_u):
        i = pl.program_id(1)
        h = h_ref[...]
        parts = (_tn(a_ref[...], df_ref[...]), _tn(dgt_ref[...], h), _tn(dup_ref[...], h))

        @pl.when(i == 0)
        def _():
            for acc, part in zip((acc_d, acc_g, acc_u), parts):
                acc[...] = part

        @pl.when(i == S // tm - 1)
        def _():
            for out, acc, part in zip((dwd_ref, dwg_ref, dwu_ref), (acc_d, acc_g, acc_u), parts):
                out[...] = (acc[...] + part).astype(BF16)

    act = pl.BlockSpec((None, tm, FF_C), lambda k, i: (k, i, 0))
    row = pl.BlockSpec((tm, D), lambda k, i: (i, 0))
    wrow = pl.BlockSpec((None, FF_C, D), lambda k, i: (k, 0, 0))
    return pl.pallas_call(
        body, grid=(N_CHIP, S // tm), name="ffn_bwd_w",
        in_specs=[act, row, row, act, act],
        out_specs=[wrow, wrow, wrow],
        out_shape=[jax.ShapeDtypeStruct((N_CHIP, FF_C, D), BF16)] * 3,
        scratch_shapes=[pltpu.VMEM((FF_C, D), F32)] * 3,
        compiler_params=_params("parallel", "arbitrary"),
    )(a, df, h3, dgt, dup)


def _mix_bwd(dmix, cat_r, cat_a, wout, exchange, exchange_args):
    tm = 1024

    def body(dm_ref, cr_ref, ca_ref, w_ref, dret_ref, datt_ref, dw_ref, acc, xc):
        i = pl.program_id(0)

        @pl.when(i == 0)
        def _():
            xc.start()
            acc[...] = jnp.zeros_like(acc)

        dm = dm_ref[...]
        dret_ref[...] = _nt(dm, w_ref[0:512, :])
        datt = _nt(dm, w_ref[512:1024, :])
        for j in range(4):
            datt_ref[j] = datt[:, 128 * j:128 * j + 128]
        acc[0:512, :] += _tn(cr_ref[...], dm)
        acc[512:1024, :] += _tn(ca_ref[...], dm)

        @pl.when(i == S // tm - 1)
        def _():
            dw_ref[...] = acc[...].astype(BF16)
            xc.middle()
            xc.finish()

    row = lambda w: pl.BlockSpec((tm, w), lambda i: (i, 0))
    full = pl.BlockSpec((D, D), lambda i: (0, 0))
    return _carry("mix_bwd", body, exchange, exchange_args, (dmix, cat_r, cat_a, wout),
                  [row(D), row(512), row(512), full],
                  [row(512), pl.BlockSpec((4, tm, 128), lambda i: (0, i, 0)), full],
                  [jax.ShapeDtypeStruct((S, 512), F32), jax.ShapeDtypeStruct((4, S, 128), F32),
                   jax.ShapeDtypeStruct((D, D), BF16)],
                  scratch_shapes=[pltpu.VMEM((D, D), F32)], grid=(S // tm,), semantics=("arbitrary",))


def _att_bwd(aq, ak, av, datt, att_out, lse, exchange, exchange_args, after=None):
    def body(q_ref, k_ref, v_ref, do_ref, out_ref, l_ref, dq_ref, dk_ref, dv_ref, xc):
        xc.start()

        lane_head = lax.broadcasted_iota(jnp.int32, (ATT_BLK, 256), 1) // 64
        for pi, d in enumerate(PATTERN_DILATIONS):
            nb, has_prev = _att_blocks(d)
            assert pi > 0 or not has_prev
            bias_rest, bias_first = _att_bias(has_prev)

            def block(b, carry, pi=pi, d=d, nb=nb, has_prev=has_prev, bias_rest=bias_rest, bias_first=bias_first):
                r, ib = b // nb, b % nb
                rows = _class_rows(ib, r, d)
                prow = _class_rows(jnp.maximum(ib - 1, 0), r, d)
                bias = jnp.where(ib == 0, bias_first, bias_rest) if has_prev else bias_first
                for g in range(2):
                    qg = _slab_pair(q_ref, g, rows).astype(BF16)
                    kg = _slab_pair(k_ref, g, rows)
                    vg = _slab_pair(v_ref, g, rows)
                    if has_prev:
                        kg = jnp.concatenate([_slab_pair(k_ref, g, prow), kg], axis=0)
                        vg = jnp.concatenate([_slab_pair(v_ref, g, prow), vg], axis=0)
                    kg, vg = kg.astype(BF16), vg.astype(BF16)
                    dog = _slab_pair(do_ref, g, rows)
                    outg = _slab_pair(out_ref, g, rows)
                    lg = _slab_pair(l_ref, g, rows)
                    qs = _stack_heads(qg, lane_head)
                    dos = _stack_heads(dog, lane_head)
                    delta = jnp.sum(dos * jnp.concatenate([outg] * 4, axis=0), axis=-1, keepdims=True)
                    lh = jnp.max(_stack_heads(lg, lane_head, NEG), axis=-1, keepdims=True)
                    s = _nt(qs, kg) * ATT_SCALE + bias
                    p = jnp.exp(s - lh)
                    dosb = dos.astype(BF16)
                    ds = (p * (_nt(dosb, vg) - delta) * ATT_SCALE).astype(BF16)
                    dq = _unstack_heads(_nn(ds, kg), lane_head)
                    dk = _tn(ds, qs)
                    dv = _tn(p.astype(BF16), dosb)
                    for jj in range(2):
                        j, sl = 2 * g + jj, slice(128 * jj, 128 * jj + 128)
                        if pi == 0:
                            dq_ref[j, rows, :] = dq[:, sl]
                            dk_ref[j, rows, :] = dk[:, sl]
                            dv_ref[j, rows, :] = dv[:, sl]
                            continue
                        dq_ref[j, rows, :] += dq[:, sl]
                        if has_prev:
                            dk_ref[j, prow, :] += dk[0:ATT_BLK, sl]
                            dv_ref[j, prow, :] += dv[0:ATT_BLK, sl]
                            dk_ref[j, rows, :] += dk[ATT_BLK:2 * ATT_BLK, sl]
                            dv_ref[j, rows, :] += dv[ATT_BLK:2 * ATT_BLK, sl]
                        else:
                            dk_ref[j, rows, :] += dk[:, sl]
                            dv_ref[j, rows, :] += dv[:, sl]
                return carry

            lax.fori_loop(0, S // ATT_BLK, block, 0, unroll=4)
        xc.middle()
        xc.finish()

    slab = jax.ShapeDtypeStruct((4, S, 128), F32)
    return _carry("att_bwd", body, exchange, exchange_args, (aq, ak, av, datt, att_out, lse), [VMEM] * 6, [VMEM] * 3,
                  [slab, slab, slab], after=after)


def _ret_bwd(qr, kr, rv, proj, o_raw, states, dret, tabs, exchange, exchange_args, after=None):
    C, G = RET_C, RET_PER_STEP
    steps = S // (C * G)
    dtab, a_tab, b_tab, lam, bd = tabs

    def body(q_ref, k_ref, v_ref, g_ref, o_ref, st_ref, dr_ref, dt_ref, a_ref, b_ref, lam_ref, bd_ref,
             dq_ref, dk_ref, dv_ref, dg_ref, dR, exch):
        @pl.when(pl.program_id(0) == 0)
        def _():
            exch.start()
            dR[...] = jnp.zeros_like(dR)

        lane_head = lax.broadcasted_iota(jnp.int32, (C, 256), 1) // 32
        col_head = lax.broadcasted_iota(jnp.int32, (C, 256), 1) // 64
        for s in reversed(range(G)):
            rows = slice(s * C, (s + 1) * C)
            q, k, v = q_ref[rows, :], k_ref[rows, :], v_ref[rows, :]
            dos = []
            for j in range(4):
                sl = slice(128 * j, 128 * j + 128)
                oj = o_ref[rows, sl]
                xc = oj - _seg_mean(oj)
                rs = lax.rsqrt(_seg_mean(xc * xc) + GN_EPS)
                rn = xc * rs
                gj = g_ref[rows, sl]
                sg = _sigmoid(gj)
                dret = dr_ref[rows, sl]
                dg_ref[rows, sl] = dret * rn * (sg * (1.0 + gj * (1.0 - sg)))
                drn = dret * (gj * sg)
                dos.append(rs * (drn - _seg_mean(drn) - rn * _seg_mean(drn * rn)))
            do = [jnp.concatenate(dos[0:2], axis=1), jnp.concatenate(dos[2:4], axis=1)]
            do8 = jnp.concatenate(do, axis=1).astype(BF16)
            drb = dR[...].astype(BF16)
            rb = st_ref[s]
            dq = _nt(do8, rb) * a_ref[...]
            dk = _nt(v, drb) * b_ref[...]
            kb = (k.astype(F32) * b_ref[...]).astype(BF16)
            dvall = _nn(kb, drb)
            qs = _stack_heads(q, lane_head, n=8)
            dec = dt_ref[...]
            p = (_nt(qs, k) * dec).astype(BF16)
            dos = [_stack_heads(do[g], col_head).astype(BF16) for g in range(2)]
            dp = jnp.concatenate([_nt(dos[g], v[:, 256 * g:256 * g + 256]) for g in range(2)], axis=0)
            ds = (dp * dec).astype(BF16)
            dq = dq + _unstack_heads(_nn(ds, k), lane_head, n=8)
            dk = dk + _tn(ds, qs)
            dv = [dvall[:, 256 * g:256 * g + 256] + _tn(p[4 * C * g:4 * C * (g + 1)], dos[g]) for g in range(2)]
            qa = (q.astype(F32) * a_ref[...]).astype(BF16)
            dR[...] = dR[...] * lam_ref[...] + _tn(qa, do8) * bd_ref[...]
            dq_ref[rows, :] = dq
            dk_ref[rows, :] = dk
            dv_ref[rows, 0:256] = dv[0]
            dv_ref[rows, 256:512] = dv[1]

        @pl.when(pl.program_id(0) == steps - 1)
        def _():
            exch.middle()
            exch.finish()

    rev = lambda w: pl.BlockSpec((C * G, w), lambda n: (steps - 1 - n, 0))
    full = lambda a: pl.BlockSpec(a.shape, lambda n: (0,) * a.ndim)
    return _carry(
        "ret_bwd", body, exchange, exchange_args, (qr, kr, rv, proj, o_raw, states, dret, dtab, a_tab, b_tab, lam, bd),
        [rev(256), rev(256), rev(512), rev(512), rev(512),
         pl.BlockSpec((G, 256, 512), lambda n: (steps - 1 - n, 0, 0)), rev(512),
         full(dtab), full(a_tab), full(b_tab), full(lam), full(bd)],
        [rev(256), rev(256), rev(512), rev(512)],
        [jax.ShapeDtypeStruct((S, 256), F32), jax.ShapeDtypeStruct((S, 256), F32),
         jax.ShapeDtypeStruct((S, 512), F32), jax.ShapeDtypeStruct((S, 512), F32)],
        scratch_shapes=[pltpu.VMEM((256, 512), F32)], grid=(steps,), semantics=("arbitrary",), after=after)


def _rot_bwd(cos, sin, spread, dqr, dkr, drv, drg, dq_att, dk_att, dv_att):
    tm = 256

    def body(cos_ref, sin_ref, e_ref, dqr_ref, dkr_ref, drv_ref, drg_ref, dqa_ref, dka_ref, dva_ref, dp_ref):
        cr, ca, sr, sa = _rot_tables(cos_ref, sin_ref, e_ref)
        lo_r, lo_a = _rot_halves(tm)

        def unrot_r(g):
            gs = g * sr
            return g * cr + pltpu.roll(jnp.where(lo_r, -gs, 0.0), 16, 1) + pltpu.roll(jnp.where(lo_r, 0.0, gs), 240, 1)

        def unrot_a(g):
            gs = g * sa
            return g * ca + pltpu.roll(jnp.where(lo_a, -gs, 0.0), 8, 1) + pltpu.roll(jnp.where(lo_a, 0.0, gs), 504, 1)

        def wide(ref):
            return jnp.concatenate([ref[j] for j in range(4)], axis=1)

        dp_ref[:, 0:256] = unrot_r(dqr_ref[...]).astype(BF16)
        dp_ref[:, 256:512] = unrot_r(dkr_ref[...] * RET_SCALE).astype(BF16)
        dp_ref[:, 512:1024] = drv_ref[...].astype(BF16)
        dp_ref[:, 1024:1536] = drg_ref[...].astype(BF16)
        dp_ref[:, 1536:2048] = unrot_a(wide(dqa_ref)).astype(BF16)
        dp_ref[:, 2048:2560] = unrot_a(wide(dka_ref)).astype(BF16)
        dp_ref[:, 2560:3072] = wide(dva_ref).astype(BF16)

    row = lambda w: pl.BlockSpec((tm, w), lambda i: (i, 0))
    slab = pl.BlockSpec((4, tm, 128), lambda i: (0, i, 0))
    return pl.pallas_call(
        body, grid=(S // tm,), name="rot_bwd",
        in_specs=[row(128), row(128), pl.BlockSpec((128, 768), lambda i: (0, 0)),
                  row(256), row(256), row(512), row(512), slab, slab, slab],
        out_specs=row(PW), out_shape=jax.ShapeDtypeStruct((S, PW), BF16),
        compiler_params=_params("parallel"),
    )(cos, sin, spread, dqr, dkr, drv, drg, dq_att, dk_att, dv_att)


def _win_bwd_w(h1, dproj, exchange, exchange_args):
    half = D // 2

    def sibling_copy(got_ref, buf, sems, k):
        x, y, c, me, chips = _place()
        return _remote(buf.at[k, pl.ds((1 - c) * half, half), :], got_ref.at[k], sems[0].at[k], sems[1].at[k],
                       (x, y, 1 - c))

    def body(h_ref, dp_ref, dw_ref, got_ref, buf, send, recv, xc):
        k = pl.program_id(0)

        @pl.when(k == 0)
        def _():
            xc.start()

        buf[k] = _tn(h_ref[...], dp_ref[...]).astype(BF16)
        sibling_copy(got_ref, buf, (send, recv), k).start()
        dw_ref[...] = buf[k, pl.ds(lax.axis_index("c") * half, half), :]

        @pl.when(k == N_CHIP - 1)
        def _():
            for j in range(N_CHIP):
                sibling_copy(got_ref, buf, (send, recv), j).wait_recv()
                sibling_copy(got_ref, buf, (send, recv), j).wait_send()
            xc.middle()
            xc.finish()

    halves = jax.ShapeDtypeStruct((N_CHIP, half, WIN_C), BF16)
    dma = pltpu.SemaphoreType.DMA((N_CHIP,))
    return _carry(
        "win_bwd_w", body, exchange, exchange_args, (h1, dproj),
        [pl.BlockSpec((S, D), lambda k: (0, 0)), pl.BlockSpec((S, WIN_C), lambda k: (0, k))],
        [pl.BlockSpec((None, half, WIN_C), lambda k: (k, 0, 0)), ANY], [halves, halves],
        scratch_shapes=[pltpu.VMEM((N_CHIP, D, WIN_C), BF16), dma, dma], grid=(N_CHIP,), semantics=("arbitrary",))


def _in_bwd(dproj, win_g, x, dx2, g1, other_rows, after):
    tm = 512
    n = len(other_rows)

    def body(dp_ref, w_ref, x_ref, dx2_ref, g_ref, *refs):
        rows, dx_ref, blk_ref = refs[:n], refs[n], refs[n + 1]

        @pl.when(pl.program_id(0) == 0)
        def _():
            blk_ref[...] = jnp.zeros_like(blk_ref)
            for i, r_ref in enumerate(rows):
                blk_ref[i + 1:i + 2, :] = r_ref[...]

        dh = _nt(dp_ref[:, 0:WIN_C], w_ref[0])
        for k in range(1, N_CHIP):
            dh = dh + _nt(dp_ref[:, k * WIN_C:(k + 1) * WIN_C], w_ref[k])
        xv = x_ref[...]
        r = _rstd(xv)
        xn = xv * r
        blk_ref[0:1, :] = blk_ref[0:1, :] + jnp.sum(dh * xn, axis=0, keepdims=True)
        t = dh * g_ref[...]
        dx_ref[...] = dx2_ref[...] + r * (t - xn * jnp.mean(t * xn, axis=-1, keepdims=True))

    row = lambda w: pl.BlockSpec((tm, w), lambda i: (i, 0))
    vec = pl.BlockSpec((1, D), lambda i: (0, 0))
    return _carry("in_bwd", body, _NoExchange(), (), (dproj, win_g, x, dx2, g1, *other_rows),
                  [row(PW), pl.BlockSpec((N_CHIP, D, WIN_C), lambda i: (0, 0, 0)), row(D), row(D), vec] + [vec] * n,
                  [row(D), pl.BlockSpec((8, D), lambda i: (0, 0))],
                  [jax.ShapeDtypeStruct((S, D), F32), jax.ShapeDtypeStruct((8, D), F32)],
                  grid=(S // tm,), semantics=("arbitrary",), after=after)[0]


ANY = pl.BlockSpec(memory_space=pl.ANY)
VMEM = pl.BlockSpec(memory_space=pltpu.VMEM)
FLIPS = ((1, 0), (0, 1), (1, 1))


def _place():
    x, y, c = lax.axis_index("x"), lax.axis_index("y"), lax.axis_index("c")
    chips = [((1 - x) if fx else x, (1 - y) if fy else y) for fx, fy in FLIPS]
    return x, y, c, 2 * x + y, chips


def _remote(src, dst, send_sem, recv_sem, device):
    return pltpu.make_async_remote_copy(src_ref=src, dst_ref=dst, send_sem=send_sem, recv_sem=recv_sem,
                                        device_id=device, device_id_type=MESH)


class _Exchange:
    aliases = {}

    def middle(self, ins, outs, sems):
        pass


def _own_shard_to_sibling(shard_ref, gathered_ref, send_sem, recv_sem):
    x, y, c, me, chips = _place()
    return _remote(shard_ref, gathered_ref.at[me], send_sem, recv_sem, (x, y, 1 - c))


class _NoExchange(_Exchange):
    n_in = n_out = 0
    out_shape = ()
    scratch = ()

    def start(self, ins, outs, sems):
        pass

    def finish(self, ins, outs, sems):
        pass


class _ForwardGathered(_Exchange):
    def __init__(self, shards, own=True, forward=True):
        self.own, self.forward = own, forward
        n = self.n = len(shards)
        self.n_in, self.n_out = 2 * n, n
        self.out_shape = [jax.ShapeDtypeStruct((N_CHIP,) + s.shape, s.dtype) for s in shards]
        dma = pltpu.SemaphoreType.DMA
        self.scratch = [dma((3 * n,)), dma((3 * n,)), dma((n,)), dma((n,))]
        self.aliases = {n + a: a for a in range(n)}

    def _fwd(self, outs, sems, a, j, chip, half_of):
        x, y, c, me, chips = _place()
        half = outs[a].shape[1] // 2
        blk = outs[a].at[2 * chip[0] + chip[1], pl.ds(half_of * half, half), :]
        return _remote(blk, blk, sems[0].at[3 * a + j], sems[1].at[3 * a + j], (x, y, 1 - c))

    def _own(self, ins, outs, sems, a):
        return _own_shard_to_sibling(ins[a], outs[a], sems[2].at[a], sems[3].at[a])

    def start(self, ins, outs, sems):
        x, y, c, me, chips = _place()
        for a in range(self.n):
            for j, chip in enumerate(chips if self.forward else ()):
                self._fwd(outs, sems, a, j, chip, c).start()
        for a in range(self.n if self.own else 0):
            self._own(ins, outs, sems, a).start()

    def finish(self, ins, outs, sems):
        x, y, c, me, chips = _place()
        for a in range(self.n):
            for j, chip in enumerate(chips if self.forward else ()):
                self._fwd(outs, sems, a, j, chip, 1 - c).wait_recv()
        for a in range(self.n):
            for j, chip in enumerate(chips if self.forward else ()):
                self._fwd(outs, sems, a, j, chip, c).wait_send()
            if self.own:
                self._own(ins, outs, sems, a).wait()


HBM = pl.BlockSpec(memory_space=pltpu.HBM)
SEMS = pl.BlockSpec(memory_space=pltpu.SEMAPHORE)
DATAFLOW = pltpu.SideEffectType.DATAFLOW_SIDE_EFFECTING


class _OverIci:
    def __init__(self, name, sources, lands):
        self.name, self.n = name, len(sources)
        hbm = lambda t: pltpu.with_memory_space_constraint(t, pltpu.HBM)
        self.arrays = [hbm(t) for t in sources] + [hbm(t) for t in lands]

    def sent(self, src, land, a, chip):
        raise NotImplementedError

    def landed(self, land, a, chip):
        raise NotImplementedError

    def _copy(self, arr, sems, a, j, receiving):
        x, y, c, me, chips = _place()
        src, dst = self.sent(arr[a], arr[self.n + a], a, chips[j])
        if receiving:
            dst = self.landed(arr[self.n + a], a, chips[j])
        return _remote(src, dst, sems[0].at[3 * a + j], sems[1].at[3 * a + j], (*chips[j], c))

    def start(self, after):
        m = len(self.arrays)

        def body(*refs):
            arr, sems, token = refs[:m], refs[m + 1:m + 3], refs[-1]
            for a in range(self.n):
                for j in range(3):
                    self._copy(arr, sems, a, j, False).start()
            token[...] = jnp.zeros_like(token)

        dma = pltpu.SemaphoreType.DMA
        outs = pl.pallas_call(
            body, name=self.name + "_start",
            out_shape=[dma((3 * self.n,)), dma((3 * self.n,))] + [pltpu.HBM(t.shape, t.dtype) for t in self.arrays]
                      + [jax.ShapeDtypeStruct((8, 128), F32)],
            in_specs=[HBM] * m + [ANY], out_specs=[SEMS, SEMS] + [HBM] * m + [VMEM],
            input_output_aliases={i: 2 + i for i in range(m)},
            compiler_params=pltpu.CompilerParams(has_side_effects=DATAFLOW),
        )(*self.arrays, after)
        self.sems, self.arrays = outs[0:2], list(outs[2:2 + m])
        return outs[-1]

    def wait(self, after):
        m = len(self.arrays)

        def body(*refs):
            arr, sems = refs[:m], refs[m:m + 2]
            for a in range(self.n):
                for j in range(3):
                    self._copy(arr, sems, a, j, False).wait_send()
                    self._copy(arr, sems, a, j, True).wait_recv()

        outs = pl.pallas_call(
            body, name=self.name + "_wait",
            out_shape=[pltpu.HBM(t.shape, t.dtype) for t in self.arrays],
            in_specs=[HBM] * m + [SEMS, SEMS, ANY], out_specs=[HBM] * m,
            input_output_aliases={i: i for i in range(m)},
            compiler_params=pltpu.CompilerParams(has_side_effects=DATAFLOW),
        )(*self.arrays, *self.sems, after)
        return list(outs[:self.n]), list(outs[self.n:])


class _GatherOverIci(_OverIci):
    def __init__(self, name, shards):
        super().__init__(name, shards, [lax.empty((N_CHIP,) + s.shape, s.dtype) for s in shards])

    @staticmethod
    def _half(ref):
        c = lax.axis_index("c")
        half = ref.shape[-2] // 2
        return pl.ds(c * half, half)

    def sent(self, src, land, a, chip):
        return src.at[self._half(src), :], land.at[_place()[3], self._half(src), :]

    def landed(self, land, a, chip):
        return land.at[2 * chip[0] + chip[1], self._half(land), :]


class _SumOverIci(_OverIci):
    def __init__(self, name, pre):
        super().__init__(name, pre, [lax.empty(p.shape, p.dtype) for p in pre])

    def sent(self, src, land, a, chip):
        return src.at[2 * chip[0] + chip[1]], land.at[_place()[3]]

    def landed(self, land, a, chip):
        return land.at[2 * chip[0] + chip[1]]


class _HalvesToSibling(_Exchange):
    def __init__(self, grads):
        n = self.n = len(grads)
        self.n_in = self.n_out = n
        self.out_shape = [jax.ShapeDtypeStruct((N_CHIP, g.shape[1] // 2, g.shape[2]), g.dtype) for g in grads]
        self.scratch = [pltpu.SemaphoreType.DMA((n,)), pltpu.SemaphoreType.DMA((n,))]

    def _copy(self, ins, outs, sems, a):
        x, y, c, me, chips = _place()
        half = ins[a].shape[1] // 2
        return _remote(ins[a].at[:, pl.ds((1 - c) * half, half), :], outs[a], sems[0].at[a], sems[1].at[a], (x, y, 1 - c))

    def start(self, ins, outs, sems):
        for a in range(self.n):
            self._copy(ins, outs, sems, a).start()

    def finish(self, ins, outs, sems):
        for a in range(self.n):
            self._copy(ins, outs, sems, a).wait_recv()
        for a in range(self.n):
            self._copy(ins, outs, sems, a).wait_send()


class _ShareHalves(_Exchange):
    def __init__(self, fulls):
        n = self.n = len(fulls)
        self.n_in = self.n_out = n
        self.out_shape = [jax.ShapeDtypeStruct(f.shape, f.dtype) for f in fulls]
        self.scratch = [pltpu.SemaphoreType.DMA((n,)), pltpu.SemaphoreType.DMA((n,))]
        self.aliases = {a: a for a in range(n)}

    def _copy(self, outs, sems, a, half_of):
        x, y, c, me, chips = _place()
        half = outs[a].shape[0] // 2
        rows = outs[a].at[pl.ds(half_of * half, half), :]
        return _remote(rows, rows, sems[0].at[a], sems[1].at[a], (x, y, 1 - c))

    def start(self, ins, outs, sems):
        c = _place()[2]
        for a in range(self.n):
            self._copy(outs, sems, a, c).start()

    def finish(self, ins, outs, sems):
        c = _place()[2]
        for a in range(self.n):
            self._copy(outs, sems, a, 1 - c).wait_recv()
        for a in range(self.n):
            self._copy(outs, sems, a, c).wait_send()


class _GatherBlocks(_Exchange):
    def __init__(self, block):
        self.n_in = self.n_out = 1
        self.out_shape = [jax.ShapeDtypeStruct((8,) + block.shape, block.dtype)]
        dma = pltpu.SemaphoreType.DMA
        self.scratch = [dma((7,)), dma((7,)), dma]

    @staticmethod
    def _peer(f):
        x, y, c, me, chips = _place()
        return ((1 - x) if f & 4 else x, (1 - y) if f & 2 else y, (1 - c) if f & 1 else c)

    def start(self, ins, outs, sems):
        x, y, c, me, chips = _place()
        for f in range(1, 8):
            _remote(ins[0], outs[0].at[2 * me + c], sems[0].at[f - 1], sems[1].at[f - 1], self._peer(f)).start()
        pltpu.make_async_copy(ins[0], outs[0].at[2 * me + c], sems[2]).start()

    def finish(self, ins, outs, sems):
        x, y, c, me, chips = _place()
        for f in range(1, 8):
            px, py, pc = self._peer(f)
            blk = outs[0].at[4 * px + 2 * py + pc]
            _remote(blk, blk, sems[0].at[f - 1], sems[1].at[f - 1], (x, y, c)).wait_recv()
        for f in range(1, 8):
            _remote(ins[0], outs[0].at[2 * me + c], sems[0].at[f - 1], sems[1].at[f - 1], self._peer(f)).wait_send()
        pltpu.make_async_copy(ins[0], outs[0].at[2 * me + c], sems[2]).wait()


class _Both(_Exchange):
    def __init__(self, first, second):
        self.parts = (first, second)
        self.n_in, self.n_out = first.n_in + second.n_in, first.n_out + second.n_out
        self.out_shape = first.out_shape + second.out_shape
        self.scratch = first.scratch + second.scratch
        self.aliases = dict(first.aliases)
        self.aliases.update({first.n_in + i: first.n_out + o for i, o in second.aliases.items()})

    def _split(self, ins, outs, sems):
        a, b = self.parts
        return ((a, ins[:a.n_in], outs[:a.n_out], sems[:len(a.scratch)]),
                (b, ins[a.n_in:], outs[a.n_out:], sems[len(a.scratch):]))

    def start(self, ins, outs, sems):
        for ex, i, o, s in self._split(ins, outs, sems):
            ex.start(i, o, s)

    def middle(self, ins, outs, sems):
        for ex, i, o, s in self._split(ins, outs, sems):
            ex.middle(i, o, s)

    def finish(self, ins, outs, sems):
        for ex, i, o, s in self._split(ins, outs, sems):
            ex.finish(i, o, s)


class _Bound:
    def __init__(self, ex, ins, outs, sems):
        self.start = lambda: ex.start(ins, outs, sems)
        self.middle = lambda: ex.middle(ins, outs, sems)
        self.finish = lambda: ex.finish(ins, outs, sems)


def _carry(name, body, ex, ex_args, args, in_specs, out_specs, out_shape, scratch_shapes=(), grid=None, semantics=(),
           after=None):
    n_a, n_o, n_s = len(args), len(out_shape), len(scratch_shapes)
    behind = [] if after is None else [after]

    def full_body(*refs):
        p = 0
        groups = []
        for size in (n_a, ex.n_in, len(behind), n_o, ex.n_out, n_s, len(ex.scratch)):
            groups.append(refs[p:p + size])
            p += size
        a, ei, _, o, eo, s, es = groups
        body(*a, *o, *s, _Bound(ex, ei, eo, es))

    kwargs = {} if grid is None else {"grid": grid}
    outs = pl.pallas_call(
        full_body, name=name,
        in_specs=list(in_specs) + [ANY] * (ex.n_in + len(behind)), out_specs=list(out_specs) + [ANY] * ex.n_out,
        out_shape=list(out_shape) + list(ex.out_shape), scratch_shapes=list(scratch_shapes) + list(ex.scratch),
        input_output_aliases={n_a + i: n_o + o for i, o in ex.aliases.items()},
        compiler_params=_params(*semantics) if semantics else pltpu.CompilerParams(vmem_limit_bytes=VMEM_LIMIT),
        **kwargs,
    )(*args, *ex_args, *behind)
    return outs[:n_o], outs[n_o:]


def _cast_bf16(arrays, after=None):
    n = len(arrays)
    behind = [] if after is None else [after]

    def body(*refs):
        for a in range(n):
            refs[len(refs) - n + a][...] = refs[a][...].astype(BF16)

    blks = [pl.BlockSpec((t.shape[0] // 4, t.shape[1]), lambda i: (i, 0)) for t in arrays]
    return pl.pallas_call(
        body, grid=(4,), name="cast_bf16", in_specs=blks + [ANY] * len(behind), out_specs=blks,
        out_shape=[jax.ShapeDtypeStruct(t.shape, BF16) for t in arrays], compiler_params=_params("parallel"),
    )(*arrays, *behind)


def _prepare(x, g1, pos, ifc, after):
    tm = 512

    def body(x_ref, g_ref, pos_ref, ifc_ref, h_ref, cos_ref, sin_ref, _):
        xv = x_ref[...]
        h_ref[...] = (xv * _rstd(xv) * g_ref[...]).astype(BF16)
        ang = pos_ref[...].astype(F32) * ifc_ref[...]
        cos_ref[...] = jnp.cos(ang)
        sin_ref[...] = jnp.sin(ang)

    row = lambda w: pl.BlockSpec((tm, w), lambda i: (i, 0))
    const = lambda w: pl.BlockSpec((1, w), lambda i: (0, 0))
    return _carry("prepare", body, _NoExchange(), (), (x, g1, pos, ifc),
                  [row(D), const(D), row(1), const(128)], [row(D), row(128), row(128)],
                  [jax.ShapeDtypeStruct((S, D), BF16)] + [jax.ShapeDtypeStruct((S, 128), F32)] * 2,
                  grid=(S // tm,), semantics=("parallel",), after=after)[0]


def _exchange_alone(name, ex, ex_args):
    def body(xc):
        xc.start()
        xc.middle()
        xc.finish()

    return _carry(name, body, ex, ex_args, (), (), (), ())[1]


def _core_index():
    return lax.axis_index("c").astype(jnp.int32).reshape(1)


def _pair_sum(gs, gots):
    n = len(gs)

    def body(c_ref, *refs):
        for a in range(n):
            refs[2 * n + a][...] = (refs[a][...].astype(F32) + refs[n + a][...].astype(F32)).astype(BF16)

    blk = [pl.BlockSpec((None,) + g.shape[1:], lambda k, c_ref: (k, 0, 0)) for g in gots]
    mine = [b if g.shape == got.shape else pl.BlockSpec((None,) + got.shape[1:], lambda k, c_ref: (k, c_ref[0], 0))
            for g, got, b in zip(gs, gots, blk)]
    return pl.pallas_call(
        body, name=f"pair_sum_{gots[0].shape[1]}x{gots[0].shape[2]}",
        grid_spec=pltpu.PrefetchScalarGridSpec(
            num_scalar_prefetch=1, grid=(N_CHIP,), in_specs=mine + blk, out_specs=blk),
        out_shape=[jax.ShapeDtypeStruct(g.shape, BF16) for g in gots],
        compiler_params=_params("parallel"),
    )(_core_index(), *gs, *gots)


def _chip_sum(pre, parts):
    n = len(parts)
    me = 2 * lax.axis_index("x") + lax.axis_index("y")
    others = [k + (k >= me).astype(jnp.int32) for k in range(3)]
    where = jnp.stack([lax.axis_index("c"), me, *others]).astype(jnp.int32)

    def body(w_ref, *refs):
        for a in range(n):
            own, p1, p2, p3 = refs[4 * a:4 * a + 4]
            refs[4 * n + a][...] = ((own[...].astype(F32) + p1[...].astype(F32)) + p2[...].astype(F32)) + p3[...].astype(F32)

    in_specs, out_specs, operands = [], [], []
    for a in range(n):
        _, half, cc = parts[a].shape
        tr = half // 2
        in_specs += [pl.BlockSpec((None, tr, cc), lambda i, w_ref, s=s: (w_ref[s], i, 0)) for s in (1, 2, 3, 4)]
        out_specs.append(pl.BlockSpec((tr, cc), lambda i, w_ref: (2 * w_ref[0] + i, 0)))
        operands += [pre[a], parts[a], parts[a], parts[a]]
    return pl.pallas_call(
        body, name=f"chip_sum_{parts[0].shape[1]}x{parts[0].shape[2]}",
        grid_spec=pltpu.PrefetchScalarGridSpec(num_scalar_prefetch=1, grid=(2,), in_specs=in_specs, out_specs=out_specs),
        out_shape=[jax.ShapeDtypeStruct((2 * p.shape[1], p.shape[2]), F32) for p in parts],
        compiler_params=_params("parallel"),
    )(where, *operands)


def _adamw_math(w, g, m, v):
    m = ADAM_B1 * m + (1.0 - ADAM_B1) * g
    v = ADAM_B2 * v + (1.0 - ADAM_B2) * (g * g)
    m_hat = m / (1.0 - ADAM_B1 ** ADAM_STEP)
    v_hat = v / (1.0 - ADAM_B2 ** ADAM_STEP)
    delta = -ADAM_LR * (m_hat / (jnp.sqrt(v_hat) + ADAM_EPS) + ADAM_WD * w)
    return delta, m, v


def _adamw(ws, gs, ms, vs, after=None):
    n = len(ws)

    def body(*refs):
        for a in range(n):
            w_ref, g_ref, m_ref, v_ref = (refs[t * n + a] for t in range(4))
            go_ref, d_ref, nm_ref, nv_ref = refs[4 * n + 4 * a:4 * n + 4 * a + 4]
            g = g_ref[...]
            go_ref[...] = g
            d_ref[...], nm_ref[...], nv_ref[...] = _adamw_math(w_ref[...], g, m_ref[...], v_ref[...])

    blks = [pl.BlockSpec((w.shape[0] // 4, w.shape[1]), lambda i: (i, 0)) for w in ws]
    outs = _carry(f"adamw_{ws[0].shape[0]}x{ws[0].shape[1]}", body, _NoExchange(), (), (*ws, *gs, *ms, *vs),
                  blks * 4, [b for b in blks for _ in range(4)],
                  [jax.ShapeDtypeStruct(w.shape, F32) for w in ws for _ in range(4)],
                  grid=(4,), semantics=("parallel",), after=after)[0]
    return [outs[4 * a:4 * a + 4] for a in range(n)]


def _adamw_gains(gall, ws, ms, vs):
    def body(ga_ref, *refs):
        w, m, v = refs[0:4], refs[4:8], refs[8:12]
        outs, loss_ref, total = refs[12:28], refs[28], refs[29]
        g = ga_ref[0]
        for dev in range(1, 8):
            g = g + ga_ref[dev]
        total[...] = g
        for i in range(4):
            gi = total[i:i + 1, :]
            outs[i][...] = gi
            outs[4 + i][...], outs[8 + i][...], outs[12 + i][...] = _adamw_math(w[i][...], gi, m[i][...], v[i][...])
        loss_ref[...] = total[4:5, 0:128] * (0.5 / D)

    outs = pl.pallas_call(
        body, name="adamw_gains",
        out_shape=[jax.ShapeDtypeStruct((1, D), F32)] * 16 + [jax.ShapeDtypeStruct((1, 128), F32)],
        scratch_shapes=[pltpu.VMEM((8, D), F32)],
    )(gall, *ws, *ms, *vs)
    return outs[0:4], outs[4:8], outs[8:12], outs[12:16], outs[16]


def kernel(x, positions, w_in, w_out, g_pre_mix, g_post_mix, g_pre_ffn, g_post_ffn, w_gate, w_up, w_down, loss_target, m_w_in, m_w_out, m_g_pre_mix, m_g_post_mix, m_g_pre_ffn, m_g_post_ffn, m_w_gate, m_w_up, m_w_down, v_w_in, v_w_out, v_g_pre_mix, v_g_post_mix, v_g_pre_ffn, v_g_post_ffn, v_w_gate, v_w_up, v_w_down):
    tr = lambda t: jnp.swapaxes(t, 1, 2)[0]
    shards = [w_in[0], w_out[0], tr(w_gate), tr(w_up), w_down[0]]
    moms = [m_w_in[0], m_w_out[0], tr(m_w_gate), tr(m_w_up), m_w_down[0]]
    vels = [v_w_in[0], v_w_out[0], tr(v_w_gate), tr(v_w_up), v_w_down[0]]
    xs, pos, tgt = x[0], positions.reshape(S, 1), loss_target[0]
    g1, g2, g3, g4 = g_pre_mix, g_post_mix, g_pre_ffn, g_post_ffn
    tabs = tuple(jnp.asarray(t) for t in _retention_tables())
    ifc, spread = _rotary_tables()
    ifc, spread = jnp.asarray(ifc), jnp.asarray(spread, dtype=BF16)
    bf = list(_cast_bf16(shards[:1]))
    win_gather = _GatherOverIci("win_gather", bf[:1])
    token = win_gather.start(shards[0])
    bf += _cast_bf16(shards[1:], token)
    wout_gather = _GatherOverIci("wout_gather", bf[1:2])
    token = wout_gather.start(token)
    ffn_gather = _GatherOverIci("ffn_gather", bf[2:])
    token = ffn_gather.start(token)
    h1, cos, sin = _prepare(xs, g1, pos, ifc, token)
    win_sh, win_land = win_gather.wait(h1)
    (win_g,) = _exchange_alone("forward_win", _ForwardGathered(bf[:1]), [*win_sh, *win_land])
    qr, kr, rv, rg, aq, ak, av = _proj_fwd(h1, win_g, cos, sin, spread, None)
    wout_sh, wout_land = wout_gather.wait(qr)
    n_ffn = len(bf[2:])
    (att_out, lse, cat_a), (wout_g, *ffn_gather.arrays[n_ffn:]) = _att_fwd(
        aq, ak, av, _Both(_ForwardGathered(bf[1:2]), _ForwardGathered(bf[2:], forward=False)),
        [*wout_sh, *wout_land, *ffn_gather.arrays])
    wout_g = wout_g.reshape(D, D)
    (o_raw, cat_r, states), _ = _ret_fwd(qr, kr, rv, rg, tabs, _NoExchange(), (), cat_a)
    ffn_sh, ffn_lands = ffn_gather.wait(cat_r)
    (mix, x2, h3), (wg_g, wu_g, wd_g) = _mix_fwd(cat_r, cat_a, wout_g, xs, g2, g3,
                                                _ForwardGathered(bf[2:], own=False), [*ffn_sh, *ffn_lands])
    gt, up, a, sq, dy, df, dg4 = _ffn_fwd(h3, wg_g, wu_g, wd_g, x2, tgt, g4)

    dgt, dup, dx2, dmix, dg3, dg2 = _ffn_bwd_act(df, gt, up, wg_g, wu_g, wd_g, dy, x2, mix, g2, g3)
    ffn_grads = list(_ffn_bwd_w(a, df, h3, dgt, dup))
    (dret, datt, dwout), got = _mix_bwd(dmix, cat_r, cat_a, wout_g, _HalvesToSibling(ffn_grads), ffn_grads)
    ffn_sum = _SumOverIci("ffn_sum", _pair_sum(ffn_grads, got))
    token = ffn_sum.start(datt)
    (dq_att, dk_att, dv_att), _ = _att_bwd(aq, ak, av, datt, att_out, lse, _NoExchange(), (), token)
    (dqr, dkr, drv, drg), _ = _ret_bwd(qr, kr, rv, rg, o_raw, states, dret, tabs, _NoExchange(), (), token)
    dproj = _rot_bwd(cos, sin, spread, dqr, dkr, drv, drg, dq_att, dk_att, dv_att)
    sums = _chip_sum(*ffn_sum.wait(dproj))
    dwout = dwout.reshape(N_CHIP, WOUT_R, D)
    (dwin, got_win), (*ffn_full, got_wout) = _win_bwd_w(
        h1, dproj, _Both(_ShareHalves(sums), _HalvesToSibling([dwout])), [*sums, dwout])

    in_sum = _SumOverIci("in_sum", _pair_sum([dwin, dwout], [got_win, got_wout]))
    token = in_sum.start(dproj)
    dx, gblock = _in_bwd(dproj, win_g, xs, dx2, g1, [dg2, dg3, dg4, sq], token)
    ffn_upd = _adamw(shards[2:], [ffn_full[o] for o in (1, 2, 0)],
                     moms[2:], vels[2:], token)
    pre, parts = in_sum.wait(ffn_upd[2][0])
    sums = _chip_sum(pre, parts)
    *in_full, gall = _exchange_alone("share_rest", _Both(_ShareHalves(sums), _GatherBlocks(gblock)), [*sums, gblock])
    upd = _adamw(shards[:2], in_full, moms[:2], vels[:2]) + ffn_upd
    gg, gd, gm, gv, loss_row = _adamw_gains(gall, [g1, g2, g3, g4],
                                            [m_g_pre_mix, m_g_post_mix, m_g_pre_ffn, m_g_post_ffn],
                                            [v_g_pre_mix, v_g_post_mix, v_g_pre_ffn, v_g_post_ffn])

    def order(mats, vecs):
        back = lambda t: jnp.swapaxes(t[None], 1, 2)
        return [mats[0][None], mats[1][None], *vecs, back(mats[2]), back(mats[3]), mats[4][None]]

    return (loss_row[0, 0], dx[None],
            *order([u[0] for u in upd], gg),
            *order([u[1] for u in upd], gd),
            *order([u[2] for u in upd], gm),
            *order([u[3] for u in upd], gv))
```

```python
import numpy as np
import jax
import jax.numpy as jnp
from jax import lax
from jax.experimental import pallas as pl
from jax.experimental.pallas import tpu as pltpu

F32, BF16 = jnp.float32, jnp.bfloat16
MESH = pl.DeviceIdType.MESH

S = 2048
D = 1024
PW = 3072
N_CHIP = 4
WIN_C = PW // N_CHIP
DFF = 2816
FF_C = DFF // N_CHIP
WOUT_R = D // N_CHIP
RMS_EPS = 1e-6
GN_EPS = 1e-5
RET_C = 128
RET_PER_STEP = 4
RET_SCALE = 32 ** -0.5
ATT_BLK = 128
ATT_SCALE = 64 ** -0.5
PATTERN_DILATIONS = (16, 1, 4)
NEG = -1e30
VMEM_LIMIT = 56 * 1024 * 1024

ADAM_LR, ADAM_B1, ADAM_B2, ADAM_EPS, ADAM_WD, ADAM_STEP = 0.001, 0.9, 0.999, 1e-08, 0.01, 10


def _params(*sem):
    return pltpu.CompilerParams(dimension_semantics=sem, vmem_limit_bytes=VMEM_LIMIT)


def _nt(a, b):
    return lax.dot_general(a, b, (((1,), (1,)), ((), ())), preferred_element_type=F32)


def _tn(a, b):
    return lax.dot_general(a, b, (((0,), (0,)), ((), ())), preferred_element_type=F32)


def _nn(a, b):
    return jnp.dot(a, b, preferred_element_type=F32)


def _rstd(v):
    return lax.rsqrt(jnp.mean(v * v, axis=-1, keepdims=True) + RMS_EPS)


def _sigmoid(v):
    return 1.0 / (1.0 + jnp.exp(-v))


def _rows(i, t):
    return pl.ds(pl.multiple_of(i * t, t), t)


def _retention_tables():
    h = np.arange(8, dtype=np.float32)
    log_g = np.log1p(-np.exp2(-5.0 - h)).astype(np.float32)
    idx = np.arange(RET_C, dtype=np.float32)
    diff = idx[:, None] - idx[None, :]
    dtab = np.where(diff >= 0, np.exp(log_g[:, None, None] * np.maximum(diff, 0.0)), 0.0).astype(np.float32)
    dtab = dtab.reshape(8 * RET_C, RET_C)
    lane_head = np.arange(256) // 32
    a_tab = np.exp(log_g[lane_head][None, :] * (idx + 1.0)[:, None]).astype(np.float32)
    b_tab = np.exp(log_g[lane_head][None, :] * (RET_C - 1.0 - idx)[:, None]).astype(np.float32)
    lam = np.exp(log_g[lane_head] * RET_C).astype(np.float32)[:, None]
    bd = (lane_head[:, None] == (np.arange(512) // 64)[None, :]).astype(np.float32)
    return dtab, a_tab, b_tab, lam, bd


def _rotary_tables():
    inv_r = (1.0 / (np.float32(10000.0) ** np.linspace(0.0, 1.0, 16, dtype=np.float32))).astype(np.float32)
    inv_a = (np.float32(500000.0) ** (-np.arange(0, 16, 2, dtype=np.float32) / np.float32(16))).astype(np.float32)
    ifc = np.zeros((1, 128), np.float32)
    ifc[0, 0:16], ifc[0, 16:24] = inv_r, inv_a
    spread = np.zeros((128, 768), np.float32)
    for lane in range(256):
        spread[(lane % 32) % 16, lane] = 1.0
    for lane in range(512):
        d = lane % 64
        spread[16 + d % 8 if d < 16 else 24, 256 + lane] = 1.0
    return ifc, spread


def _rot_halves(tm):
    lo_r = (lax.broadcasted_iota(jnp.int32, (tm, 256), 1) % 32) < 16
    lo_a = (lax.broadcasted_iota(jnp.int32, (tm, 512), 1) % 64) < 8
    return lo_r, lo_a


def _spread_exact(t, e):
    hi = t.astype(BF16)
    r1 = t - hi.astype(F32)
    mid = r1.astype(BF16)
    lo = (r1 - mid.astype(F32)).astype(BF16)
    return _nn(hi, e) + _nn(mid, e) + _nn(lo, e)


def _rot_tables(cos_ref, sin_ref, e_ref):
    cs = _spread_exact(cos_ref[...], e_ref[...])
    sn = _spread_exact(sin_ref[...], e_ref[...])
    return cs[:, 0:256], cs[:, 256:768], sn[:, 0:256], sn[:, 256:768]


def _proj_fwd(h1, win_g, cos, sin, spread, after):
    tm = 256

    def body(h_ref, w_ref, cos_ref, sin_ref, e_ref, qr_ref, kr_ref, rv_ref, rg_ref, aq_ref, ak_ref, av_ref, p_ref, _):
        h = h_ref[...]
        for k in range(N_CHIP):
            p_ref[:, k * WIN_C:(k + 1) * WIN_C] = _nn(h, w_ref[k])
        cr, ca, sr, sa = _rot_tables(cos_ref, sin_ref, e_ref)
        lo_r, lo_a = _rot_halves(tm)

        def rot_r(v):
            return v * cr + sr * jnp.where(lo_r, -pltpu.roll(v, 240, 1), pltpu.roll(v, 16, 1))

        def rot_a(v):
            return v * ca + sa * jnp.where(lo_a, -pltpu.roll(v, 504, 1), pltpu.roll(v, 8, 1))

        qr_ref[...] = rot_r(p_ref[:, 0:256]).astype(BF16)
        kr_ref[...] = (rot_r(p_ref[:, 256:512]) * RET_SCALE).astype(BF16)
        rv_ref[...] = p_ref[:, 512:1024].astype(BF16)
        rg_ref[...] = p_ref[:, 1024:1536]
        aq, ak = rot_a(p_ref[:, 1536:2048]), rot_a(p_ref[:, 2048:2560])
        for j in range(4):
            aq_ref[j] = aq[:, 128 * j:128 * j + 128]
            ak_ref[j] = ak[:, 128 * j:128 * j + 128]
            av_ref[j] = p_ref[:, 2560 + 128 * j:2560 + 128 * j + 128]

    row = lambda w: pl.BlockSpec((tm, w), lambda i: (i, 0))
    slab = pl.BlockSpec((4, tm, 128), lambda i: (0, i, 0))
    return _carry(
        "proj_fwd", body, _NoExchange(), (), (h1, win_g, cos, sin, spread),
        [row(D), pl.BlockSpec((N_CHIP, D, WIN_C), lambda i: (0, 0, 0)), row(128), row(128),
         pl.BlockSpec((128, 768), lambda i: (0, 0))],
        [row(256), row(256), row(512), row(512), slab, slab, slab],
        [jax.ShapeDtypeStruct((S, w), BF16) for w in (256, 256, 512)]
        + [jax.ShapeDtypeStruct((S, 512), F32)] + [jax.ShapeDtypeStruct((4, S, 128), F32)] * 3,
        scratch_shapes=[pltpu.VMEM((tm, PW), F32)], grid=(S // tm,), semantics=("parallel",), after=after)[0]


def _seg_mean(v):
    lo = lax.broadcasted_iota(jnp.int32, v.shape, 1) < 64
    s_lo = jnp.sum(jnp.where(lo, v, 0.0), axis=-1, keepdims=True)
    s_hi = jnp.sum(jnp.where(lo, 0.0, v), axis=-1, keepdims=True)
    return jnp.where(lo, s_lo, s_hi) * (1.0 / 64.0)


def _ret_fwd(qr, kr, rv, proj, tabs, exchange, exchange_args, after=None):
    C, G = RET_C, RET_PER_STEP
    steps = S // (C * G)
    dtab, a_tab, b_tab, lam, bd = tabs

    def body(q_ref, k_ref, v_ref, g_ref, dt_ref, a_ref, b_ref, lam_ref, bd_ref, o_ref, cat_ref, st_ref, R, exch):
        @pl.when(pl.program_id(0) == 0)
        def _():
            exch.start()
            R[...] = jnp.zeros_like(R)

        lane_head = lax.broadcasted_iota(jnp.int32, (C, 256), 1) // 32
        col_head = lax.broadcasted_iota(jnp.int32, (C, 256), 1) // 64
        for s in range(G):
            rows = slice(s * C, (s + 1) * C)
            q, k, v = q_ref[rows, :], k_ref[rows, :], v_ref[rows, :]
            rb = R[...].astype(BF16)
            st_ref[s] = rb
            qa = (q.astype(F32) * a_ref[...]).astype(BF16)
            cross = _nn(qa, rb)
            p = (_nt(_stack_heads(q, lane_head, n=8), k) * dt_ref[...]).astype(BF16)
            og = [cross[:, 256 * g:256 * g + 256]
                  + _unstack_heads(_nn(p[4 * C * g:4 * C * (g + 1)], v[:, 256 * g:256 * g + 256]), col_head)
                  for g in range(2)]
            kb = (k.astype(F32) * b_ref[...]).astype(BF16)
            R[...] = R[...] * lam_ref[...] + _tn(kb, v) * bd_ref[...]
            o_ref[rows, 0:256] = og[0]
            o_ref[rows, 256:512] = og[1]
            for j in range(4):
                oj = og[j // 2][:, 128 * (j % 2):128 * (j % 2) + 128]
                xc = oj - _seg_mean(oj)
                rn = xc * lax.rsqrt(_seg_mean(xc * xc) + GN_EPS)
                gj = g_ref[rows, 128 * j:128 * j + 128]
                cat_ref[rows, 128 * j:128 * j + 128] = (rn * (gj * _sigmoid(gj))).astype(BF16)

        @pl.when(pl.program_id(0) == steps - 1)
        def _():
            exch.middle()
            exch.finish()

    row = lambda w: pl.BlockSpec((C * G, w), lambda n: (n, 0))
    full = lambda a: pl.BlockSpec(a.shape, lambda n: (0,) * a.ndim)
    return _carry(
        "ret_fwd", body, exchange, exchange_args, (qr, kr, rv, proj, dtab, a_tab, b_tab, lam, bd),
        [row(256), row(256), row(512), row(512),
         full(dtab), full(a_tab), full(b_tab), full(lam), full(bd)],
        [row(512), row(512), pl.BlockSpec((G, 256, 512), lambda n: (n, 0, 0))],
        [jax.ShapeDtypeStruct((S, 512), F32), jax.ShapeDtypeStruct((S, 512), BF16),
         jax.ShapeDtypeStruct((S // C, 256, 512), BF16)],
        scratch_shapes=[pltpu.VMEM((256, 512), F32)], grid=(steps,), semantics=("arbitrary",), after=after)


def _stack_heads(v, lane_head, fill=0.0, n=4):
    return jnp.concatenate([jnp.where(lane_head == h, v, jnp.full_like(v, fill)) for h in range(n)], axis=0)


def _unstack_heads(v, lane_head, n=4):
    out = v[0:ATT_BLK]
    for h in range(1, n):
        out = jnp.where(lane_head == h, v[h * ATT_BLK:(h + 1) * ATT_BLK], out)
    return out


def _att_bias(has_prev):
    nk = 2 * ATT_BLK if has_prev else ATT_BLK
    a = lax.broadcasted_iota(jnp.int32, (4 * ATT_BLK, nk), 0) % ATT_BLK
    kk = lax.broadcasted_iota(jnp.int32, (4 * ATT_BLK, nk), 1)
    if not has_prev:
        return None, jnp.where((a - kk) >= 0, 0.0, NEG)
    dist = ATT_BLK + a - kk
    inside = (dist >= 0) & (dist <= ATT_BLK)
    return jnp.where(inside, 0.0, NEG), jnp.where(inside & (kk >= ATT_BLK), 0.0, NEG)


def _class_rows(ib, r, d):
    if d == 1:
        return pl.ds(pl.multiple_of(ib * ATT_BLK, ATT_BLK), ATT_BLK)
    return pl.ds(ib * ATT_BLK * d + r, ATT_BLK, stride=d)


def _slab_pair(ref, g, rows):
    return jnp.concatenate([ref[2 * g, rows, :], ref[2 * g + 1, rows, :]], axis=1)


def _att_blocks(d):
    nb = S // d // ATT_BLK
    return nb, nb > 1


def _att_fwd(aq, ak, av, exchange, exchange_args):
    def body(q_ref, k_ref, v_ref, o_ref, l_ref, cat_ref, xc):
        xc.start()
        lane_head = lax.broadcasted_iota(jnp.int32, (ATT_BLK, 256), 1) // 64
        for pi, d in enumerate(PATTERN_DILATIONS):
            if pi == len(PATTERN_DILATIONS) - 1:
                xc.middle()
            nb, has_prev = _att_blocks(d)
            bias_rest, bias_first = _att_bias(has_prev)

            def block(b, carry, pi=pi, d=d, nb=nb, has_prev=has_prev, bias_rest=bias_rest, bias_first=bias_first):
                r, ib = b // nb, b % nb
                rows = _class_rows(ib, r, d)
                prow = _class_rows(jnp.maximum(ib - 1, 0), r, d)
                bias = jnp.where(ib == 0, bias_first, bias_rest) if has_prev else bias_first
                for g in range(2):
                    qg = _slab_pair(q_ref, g, rows).astype(BF16)
                    kg = _slab_pair(k_ref, g, rows)
                    vg = _slab_pair(v_ref, g, rows)
                    if has_prev:
                        kg = jnp.concatenate([_slab_pair(k_ref, g, prow), kg], axis=0)
                        vg = jnp.concatenate([_slab_pair(v_ref, g, prow), vg], axis=0)
                    kg, vg = kg.astype(BF16), vg.astype(BF16)
                    s = _nt(_stack_heads(qg, lane_head), kg) * ATT_SCALE + bias
                    m = jnp.max(s, axis=-1, keepdims=True)
                    p = jnp.exp(s - m)
                    den = jnp.sum(p, axis=-1, keepdims=True)
                    og = _unstack_heads(_nn(p.astype(BF16), vg) / den, lane_head)
                    lg = _unstack_heads(jnp.broadcast_to(m + jnp.log(den), (4 * ATT_BLK, 256)), lane_head)
                    for jj in range(2):
                        j = 2 * g + jj
                        o_new, l_new = og[:, 128 * jj:128 * jj + 128], lg[:, 128 * jj:128 * jj + 128]
                        if pi > 0:
                            o_old, l_old = o_ref[j, rows, :], l_ref[j, rows, :]
                            mx = jnp.maximum(l_old, l_new)
                            ea, eb = jnp.exp(l_old - mx), jnp.exp(l_new - mx)
                            den = ea + eb
                            o_new = (ea * o_old + eb * o_new) / den
                            l_new = mx + jnp.log(den)
                        o_ref[j, rows, :] = o_new
                        l_ref[j, rows, :] = l_new
                return carry

            lax.fori_loop(0, S // ATT_BLK, block, 0, unroll=4)

        def to_cat(i, carry):
            rows = _rows(i, 256)
            for j in range(4):
                cat_ref[rows, 128 * j:128 * j + 128] = o_ref[j, rows, :].astype(BF16)
            return carry

        lax.fori_loop(0, S // 256, to_cat, 0)
        xc.finish()

    slab = jax.ShapeDtypeStruct((4, S, 128), F32)
    return _carry("att_fwd", body, exchange, exchange_args, (aq, ak, av), [VMEM] * 3, [VMEM] * 3,
                  [slab, slab, jax.ShapeDtypeStruct((S, 512), BF16)])


def _mix_fwd(cat_r, cat_a, wout, x, g2, g3, exchange, exchange_args):
    tm = 512

    def body(cr_ref, ca_ref, w_ref, x_ref, g2_ref, g3_ref, mix_ref, x2_ref, h3_ref, xc):
        @pl.when(pl.program_id(0) == 0)
        def _():
            xc.start()

        mix = _nn(cr_ref[...], w_ref[0:512, :]) + _nn(ca_ref[...], w_ref[512:1024, :])
        mix_ref[...] = mix
        x2 = x_ref[...] + mix * _rstd(mix) * g2_ref[...]
        x2_ref[...] = x2
        h3_ref[...] = (x2 * _rstd(x2) * g3_ref[...]).astype(BF16)

        @pl.when(pl.program_id(0) == S // tm - 1)
        def _():
            xc.middle()
            xc.finish()

    row = lambda w: pl.BlockSpec((tm, w), lambda i: (i, 0))
    vec = pl.BlockSpec((1, D), lambda i: (0, 0))
    return _carry("mix_fwd", body, exchange, exchange_args, (cat_r, cat_a, wout, x, g2, g3),
                  [row(512), row(512), pl.BlockSpec((D, D), lambda i: (0, 0)), row(D), vec, vec],
                  [row(D), row(D), row(D)],
                  [jax.ShapeDtypeStruct((S, D), F32), jax.ShapeDtypeStruct((S, D), F32),
                   jax.ShapeDtypeStruct((S, D), BF16)],
                  grid=(S // tm,), semantics=("arbitrary",))


def _ffn_fwd(h3, wg, wu, wd, x2, tgt, g4):
    tm = 512
    last = N_CHIP - 1

    def body(h_ref, wg_ref, wu_ref, wd_ref, x2_ref, t_ref, g_ref,
             gt_ref, up_ref, a_ref, loss_ref, dy_ref, df_ref, dg_ref, f_ref):
        k, i = pl.program_id(0), pl.program_id(1)
        h = h_ref[...]
        gt = _nt(h, wg_ref[...])
        up = _nt(h, wu_ref[...])
        gt_ref[...] = gt.astype(BF16)
        up_ref[...] = up.astype(BF16)
        a = (gt * _sigmoid(gt) * up).astype(BF16)
        a_ref[...] = a
        part = _nn(a, wd_ref[...])
        rows = _rows(i, tm)

        @pl.when(k == 0)
        def _():
            f_ref[rows, :] = part

        @pl.when((k > 0) & (k < last))
        def _():
            f_ref[rows, :] = f_ref[rows, :] + part

        @pl.when((k == last) & (i == 0))
        def _():
            loss_ref[...] = jnp.zeros_like(loss_ref)
            dg_ref[...] = jnp.zeros_like(dg_ref)

        @pl.when(k == last)
        def _():
            fv = f_ref[rows, :] + part
            r = _rstd(fv)
            fn = fv * r
            e = x2_ref[...] + fn * g_ref[...] - t_ref[...]
            loss_ref[...] = loss_ref[...] + jnp.sum(jnp.sum(e * e, axis=-1, keepdims=True), axis=0, keepdims=True)
            dy = e * (1.0 / D)
            dy_ref[...] = dy
            dg_ref[...] = dg_ref[...] + jnp.sum(dy * fn, axis=0, keepdims=True)
            t = dy * g_ref[...]
            df_ref[...] = (r * (t - fn * jnp.mean(t * fn, axis=-1, keepdims=True))).astype(BF16)

    wrow = pl.BlockSpec((None, FF_C, D), lambda k, i: (k, 0, 0))
    act = pl.BlockSpec((None, tm, FF_C), lambda k, i: (k, i, 0))
    late = pl.BlockSpec((tm, D), lambda k, i: (jnp.where(k == last, i, 0), 0))
    vec = pl.BlockSpec((1, D), lambda k, i: (0, 0))
    return pl.pallas_call(
        body, grid=(N_CHIP, S // tm), name="ffn_fwd",
        in_specs=[pl.BlockSpec((tm, D), lambda k, i: (i, 0)), wrow, wrow, wrow, late, late, vec],
        out_specs=[act, act, act, vec, late, late, vec],
        out_shape=[jax.ShapeDtypeStruct((N_CHIP, S, FF_C), BF16)] * 3
                  + [jax.ShapeDtypeStruct((1, D), F32), jax.ShapeDtypeStruct((S, D), F32),
                     jax.ShapeDtypeStruct((S, D), BF16), jax.ShapeDtypeStruct((1, D), F32)],
        scratch_shapes=[pltpu.VMEM((S, D), F32)],
        compiler_params=_params("arbitrary", "arbitrary"),
    )(h3, wg, wu, wd, x2, tgt, g4)


def _ffn_bwd_act(df, gt, up, wg, wu, wd, dy, x2, mix, g2, g3):
    tm, sub = 512, 256
    last = N_CHIP - 1

    def body(df_ref, gt_ref, up_ref, wg_ref, wu_ref, wd_ref, dy_ref, x2_ref, mix_ref, g2_ref, g3_ref,
             dgt_ref, dup_ref, dx2_ref, dmix_ref, dg3_ref, dg2_ref, dh_ref):
        k, i = pl.program_id(0), pl.program_id(1)
        parts = []
        for s in range(tm // sub):
            rows = slice(s * sub, (s + 1) * sub)
            da = _nt(df_ref[rows, :], wd_ref[...])
            gt, up = gt_ref[rows, :].astype(F32), up_ref[rows, :].astype(F32)
            sg = _sigmoid(gt)
            dup = (da * gt * sg).astype(BF16)
            dgt = (da * up * (sg * (1.0 + gt * (1.0 - sg)))).astype(BF16)
            dup_ref[rows, :] = dup
            dgt_ref[rows, :] = dgt
            parts.append(_nn(dgt, wg_ref[...]) + _nn(dup, wu_ref[...]))
        part = jnp.concatenate(parts, axis=0)
        rows = _rows(i, tm)

        @pl.when(k == 0)
        def _():
            dh_ref[rows, :] = part

        @pl.when((k > 0) & (k < last))
        def _():
            dh_ref[rows, :] = dh_ref[rows, :] + part

        @pl.when((k == last) & (i == 0))
        def _():
            dg3_ref[...] = jnp.zeros_like(dg3_ref)
            dg2_ref[...] = jnp.zeros_like(dg2_ref)

        @pl.when(k == last)
        def _():
            dh = dh_ref[rows, :] + part
            x2 = x2_ref[...]
            r3 = _rstd(x2)
            xn = x2 * r3
            dg3_ref[...] = dg3_ref[...] + jnp.sum(dh * xn, axis=0, keepdims=True)
            t = dh * g3_ref[...]
            dx2 = dy_ref[...] + r3 * (t - xn * jnp.mean(t * xn, axis=-1, keepdims=True))
            dx2_ref[...] = dx2
            mix = mix_ref[...]
            r2 = _rstd(mix)
            mn = mix * r2
            dg2_ref[...] = dg2_ref[...] + jnp.sum(dx2 * mn, axis=0, keepdims=True)
            u = dx2 * g2_ref[...]
            dmix_ref[...] = (r2 * (u - mn * jnp.mean(u * mn, axis=-1, keepdims=True))).astype(BF16)

    wrow = pl.BlockSpec((None, FF_C, D), lambda k, i: (k, 0, 0))
    act = pl.BlockSpec((None, tm, FF_C), lambda k, i: (k, i, 0))
    row = pl.BlockSpec((tm, D), lambda k, i: (i, 0))
    late = pl.BlockSpec((tm, D), lambda k, i: (jnp.where(k == last, i, 0), 0))
    vec = pl.BlockSpec((1, D), lambda k, i: (0, 0))
    return pl.pallas_call(
        body, grid=(N_CHIP, S // tm), name="ffn_bwd_act",
        in_specs=[row, act, act, wrow, wrow, wrow, late, late, late, vec, vec],
        out_specs=[act, act, late, late, vec, vec],
        out_shape=[jax.ShapeDtypeStruct((N_CHIP, S, FF_C), BF16), jax.ShapeDtypeStruct((N_CHIP, S, FF_C), BF16),
                   jax.ShapeDtypeStruct((S, D), F32), jax.ShapeDtypeStruct((S, D), BF16),
                   jax.ShapeDtypeStruct((1, D), F32), jax.ShapeDtypeStruct((1, D), F32)],
        scratch_shapes=[pltpu.VMEM((S, D), F32)],
        compiler_params=_params("arbitrary", "arbitrary"),
    )(df, gt, up, wg, wu, wd, dy, x2, mix, g2, g3)


def _ffn_bwd_w(a, df, h3, dgt, dup):
    tm = 1024
    assert S // tm == 2

    def body(a_ref, df_ref, h_ref, dgt_ref, dup_ref, dwd_ref, dwg_ref, dwu_ref, acc_d, acc_g, acc_u):
        i = pl.program_id(1)
        h = h_ref[...]
        parts = (_tn(a_ref[...], df_ref[...]), _tn(dgt_ref[...], h), _tn(dup_ref[...], h))

        @pl.when(i == 0)
        def _():
            for acc, part in zip((acc_d, acc_g, acc_u), parts):
                acc[...] = part

        @pl.when(i == S // tm - 1)
        def _():
            for out, acc, part in zip((dwd_ref, dwg_ref, dwu_ref), (acc_d, acc_g, acc_u), parts):
                out[...] = (acc[...] + part).astype(BF16)

    act = pl.BlockSpec((None, tm, FF_C), lambda k, i: (k, i, 0))
    row = pl.BlockSpec((tm, D), lambda k, i: (i, 0))
    wrow = pl.BlockSpec((None, FF_C, D), lambda k, i: (k, 0, 0))
    return pl.pallas_call(
        body, grid=(N_CHIP, S // tm), name="ffn_bwd_w",
        in_specs=[act, row, row, act, act],
        out_specs=[wrow, wrow, wrow],
        out_shape=[jax.ShapeDtypeStruct((N_CHIP, FF_C, D), BF16)] * 3,
        scratch_shapes=[pltpu.VMEM((FF_C, D), F32)] * 3,
        compiler_params=_params("parallel", "arbitrary"),
    )(a, df, h3, dgt, dup)


def _mix_bwd(dmix, cat_r, cat_a, wout, exchange, exchange_args):
    tm = 1024

    def body(dm_ref, cr_ref, ca_ref, w_ref, dret_ref, datt_ref, dw_ref, acc, xc):
        i = pl.program_id(0)

        @pl.when(i == 0)
        def _():
            xc.start()
            acc[...] = jnp.zeros_like(acc)

        dm = dm_ref[...]
        dret_ref[...] = _nt(dm, w_ref[0:512, :])
        datt = _nt(dm, w_ref[512:1024, :])
        for j in range(4):
            datt_ref[j] = datt[:, 128 * j:128 * j + 128]
        acc[0:512, :] += _tn(cr_ref[...], dm)
        acc[512:1024, :] += _tn(ca_ref[...], dm)

        @pl.when(i == S // tm - 1)
        def _():
            dw_ref[...] = acc[...].astype(BF16)
            xc.middle()
            xc.finish()

    row = lambda w: pl.BlockSpec((tm, w), lambda i: (i, 0))
    full = pl.BlockSpec((D, D), lambda i: (0, 0))
    return _carry("mix_bwd", body, exchange, exchange_args, (dmix, cat_r, cat_a, wout),
                  [row(D), row(512), row(512), full],
                  [row(512), pl.BlockSpec((4, tm, 128), lambda i: (0, i, 0)), full],
                  [jax.ShapeDtypeStruct((S, 512), F32), jax.ShapeDtypeStruct((4, S, 128), F32),
                   jax.ShapeDtypeStruct((D, D), BF16)],
                  scratch_shapes=[pltpu.VMEM((D, D), F32)], grid=(S // tm,), semantics=("arbitrary",))


def _att_bwd(aq, ak, av, datt, att_out, lse, exchange, exchange_args, after=None):
    def body(q_ref, k_ref, v_ref, do_ref, out_ref, l_ref, dq_ref, dk_ref, dv_ref, xc):
        xc.start()

        lane_head = lax.broadcasted_iota(jnp.int32, (ATT_BLK, 256), 1) // 64
        for pi, d in enumerate(PATTERN_DILATIONS):
            nb, has_prev = _att_blocks(d)
            assert pi > 0 or not has_prev
            bias_rest, bias_first = _att_bias(has_prev)

            def block(b, carry, pi=pi, d=d, nb=nb, has_prev=has_prev, bias_rest=bias_rest, bias_first=bias_first):
                r, ib = b // nb, b % nb
                rows = _class_rows(ib, r, d)
                prow = _class_rows(jnp.maximum(ib - 1, 0), r, d)
                bias = jnp.where(ib == 0, bias_first, bias_rest) if has_prev else bias_first
                for g in range(2):
                    qg = _slab_pair(q_ref, g, rows).astype(BF16)
                    kg = _slab_pair(k_ref, g, rows)
                    vg = _slab_pair(v_ref, g, rows)
                    if has_prev:
                        kg = jnp.concatenate([_slab_pair(k_ref, g, prow), kg], axis=0)
                        vg = jnp.concatenate([_slab_pair(v_ref, g, prow), vg], axis=0)
                    kg, vg = kg.astype(BF16), vg.astype(BF16)
                    dog = _slab_pair(do_ref, g, rows)
                    outg = _slab_pair(out_ref, g, rows)
                    lg = _slab_pair(l_ref, g, rows)
                    qs = _stack_heads(qg, lane_head)
                    dos = _stack_heads(dog, lane_head)
                    delta = jnp.sum(dos * jnp.concatenate([outg] * 4, axis=0), axis=-1, keepdims=True)
                    lh = jnp.max(_stack_heads(lg, lane_head, NEG), axis=-1, keepdims=True)
                    s = _nt(qs, kg) * ATT_SCALE + bias
                    p = jnp.exp(s - lh)
                    dosb = dos.astype(BF16)
                    ds = (p * (_nt(dosb, vg) - delta) * ATT_SCALE).astype(BF16)
                    dq = _unstack_heads(_nn(ds, kg), lane_head)
                    dk = _tn(ds, qs)
                    dv = _tn(p.astype(BF16), dosb)
                    for jj in range(2):
                        j, sl = 2 * g + jj, slice(128 * jj, 128 * jj + 128)
                        if pi == 0:
                            dq_ref[j, rows, :] = dq[:, sl]
                            dk_ref[j, rows, :] = dk[:, sl]
                            dv_ref[j, rows, :] = dv[:, sl]
                            continue
                        dq_ref[j, rows, :] += dq[:, sl]
                        if has_prev:
                            dk_ref[j, prow, :] += dk[0:ATT_BLK, sl]
                            dv_ref[j, prow, :] += dv[0:ATT_BLK, sl]
                            dk_ref[j, rows, :] += dk[ATT_BLK:2 * ATT_BLK, sl]
                            dv_ref[j, rows, :] += dv[ATT_BLK:2 * ATT_BLK, sl]
                        else:
                            dk_ref[j, rows, :] += dk[:, sl]
                            dv_ref[j, rows, :] += dv[:, sl]
                return carry

            lax.fori_loop(0, S // ATT_BLK, block, 0, unroll=4)
        xc.middle()
        xc.finish()

    slab = jax.ShapeDtypeStruct((4, S, 128), F32)
    return _carry("att_bwd", body, exchange, exchange_args, (aq, ak, av, datt, att_out, lse), [VMEM] * 6, [VMEM] * 3,
                  [slab, slab, slab], after=after)


def _ret_bwd(qr, kr, rv, proj, o_raw, states, dret, tabs, exchange, exchange_args, after=None):
    C, G = RET_C, RET_PER_STEP
    steps = S // (C * G)
    dtab, a_tab, b_tab, lam, bd = tabs

    def body(q_ref, k_ref, v_ref, g_ref, o_ref, st_ref, dr_ref, dt_ref, a_ref, b_ref, lam_ref, bd_ref,
             dq_ref, dk_ref, dv_ref, dg_ref, dR, exch):
        @pl.when(pl.program_id(0) == 0)
        def _():
            exch.start()
            dR[...] = jnp.zeros_like(dR)

        lane_head = lax.broadcasted_iota(jnp.int32, (C, 256), 1) // 32
        col_head = lax.broadcasted_iota(jnp.int32, (C, 256), 1) // 64
        for s in reversed(range(G)):
            rows = slice(s * C, (s + 1) * C)
            q, k, v = q_ref[rows, :], k_ref[rows, :], v_ref[rows, :]
            dos = []
            for j in range(4):
                sl = slice(128 * j, 128 * j + 128)
                oj = o_ref[rows, sl]
                xc = oj - _seg_mean(oj)
                rs = lax.rsqrt(_seg_mean(xc * xc) + GN_EPS)
                rn = xc * rs
                gj = g_ref[rows, sl]
                sg = _sigmoid(gj)
                dret = dr_ref[rows, sl]
                dg_ref[rows, sl] = dret * rn * (sg * (1.0 + gj * (1.0 - sg)))
                drn = dret * (gj * sg)
                dos.append(rs * (drn - _seg_mean(drn) - rn * _seg_mean(drn * rn)))
            do = [jnp.concatenate(dos[0:2], axis=1), jnp.concatenate(dos[2:4], axis=1)]
            do8 = jnp.concatenate(do, axis=1).astype(BF16)
            drb = dR[...].astype(BF16)
            rb = st_ref[s]
            dq = _nt(do8, rb) * a_ref[...]
            dk = _nt(v, drb) * b_ref[...]
            kb = (k.astype(F32) * b_ref[...]).astype(BF16)
            dvall = _nn(kb, drb)
            qs = _stack_heads(q, lane_head, n=8)
            dec = dt_ref[...]
            p = (_nt(qs, k) * dec).astype(BF16)
            dos = [_stack_heads(do[g], col_head).astype(BF16) for g in range(2)]
            dp = jnp.concatenate([_nt(dos[g], v[:, 256 * g:256 * g + 256]) for g in range(2)], axis=0)
            ds = (dp * dec).astype(BF16)
            dq = dq + _unstack_heads(_nn(ds, k), lane_head, n=8)
            dk = dk + _tn(ds, qs)
            dv = [dvall[:, 256 * g:256 * g + 256] + _tn(p[4 * C * g:4 * C * (g + 1)], dos[g]) for g in range(2)]
            qa = (q.astype(F32) * a_ref[...]).astype(BF16)
            dR[...] = dR[...] * lam_ref[...] + _tn(qa, do8) * bd_ref[...]
            dq_ref[rows, :] = dq
            dk_ref[rows, :] = dk
            dv_ref[rows, 0:256] = dv[0]
            dv_ref[rows, 256:512] = dv[1]

        @pl.when(pl.program_id(0) == steps - 1)
        def _():
            exch.middle()
            exch.finish()

    rev = lambda w: pl.BlockSpec((C * G, w), lambda n: (steps - 1 - n, 0))
    full = lambda a: pl.BlockSpec(a.shape, lambda n: (0,) * a.ndim)
    return _carry(
        "ret_bwd", body, exchange, exchange_args, (qr, kr, rv, proj, o_raw, states, dret, dtab, a_tab, b_tab, lam, bd),
        [rev(256), rev(256), rev(512), rev(512), rev(512),
         pl.BlockSpec((G, 256, 512), lambda n: (steps - 1 - n, 0, 0)), rev(512),
         full(dtab), full(a_tab), full(b_tab), full(lam), full(bd)],
        [rev(256), rev(256), rev(512), rev(512)],
        [jax.ShapeDtypeStruct((S, 256), F32), jax.ShapeDtypeStruct((S, 256), F32),
         jax.ShapeDtypeStruct((S, 512), F32), jax.ShapeDtypeStruct((S, 512), F32)],
        scratch_shapes=[pltpu.VMEM((256, 512), F32)], grid=(steps,), semantics=("arbitrary",), after=after)


def _rot_bwd(cos, sin, spread, dqr, dkr, drv, drg, dq_att, dk_att, dv_att):
    tm = 256

    def body(cos_ref, sin_ref, e_ref, dqr_ref, dkr_ref, drv_ref, drg_ref, dqa_ref, dka_ref, dva_ref, dp_ref):
        cr, ca, sr, sa = _rot_tables(cos_ref, sin_ref, e_ref)
        lo_r, lo_a = _rot_halves(tm)

        def unrot_r(g):
            gs = g * sr
            return g * cr + pltpu.roll(jnp.where(lo_r, -gs, 0.0), 16, 1) + pltpu.roll(jnp.where(lo_r, 0.0, gs), 240, 1)

        def unrot_a(g):
            gs = g * sa
            return g * ca + pltpu.roll(jnp.where(lo_a, -gs, 0.0), 8, 1) + pltpu.roll(jnp.where(lo_a, 0.0, gs), 504, 1)

        def wide(ref):
            return jnp.concatenate([ref[j] for j in range(4)], axis=1)

        dp_ref[:, 0:256] = unrot_r(dqr_ref[...]).astype(BF16)
        dp_ref[:, 256:512] = unrot_r(dkr_ref[...] * RET_SCALE).astype(BF16)
        dp_ref[:, 512:1024] = drv_ref[...].astype(BF16)
        dp_ref[:, 1024:1536] = drg_ref[...].astype(BF16)
        dp_ref[:, 1536:2048] = unrot_a(wide(dqa_ref)).astype(BF16)
        dp_ref[:, 2048:2560] = unrot_a(wide(dka_ref)).astype(BF16)
        dp_ref[:, 2560:3072] = wide(dva_ref).astype(BF16)

    row = lambda w: pl.BlockSpec((tm, w), lambda i: (i, 0))
    slab = pl.BlockSpec((4, tm, 128), lambda i: (0, i, 0))
    return pl.pallas_call(
        body, grid=(S // tm,), name="rot_bwd",
        in_specs=[row(128), row(128), pl.BlockSpec((128, 768), lambda i: (0, 0)),
                  row(256), row(256), row(512), row(512), slab, slab, slab],
        out_specs=row(PW), out_shape=jax.ShapeDtypeStruct((S, PW), BF16),
        compiler_params=_params("parallel"),
    )(cos, sin, spread, dqr, dkr, drv, drg, dq_att, dk_att, dv_att)


def _win_bwd_w(h1, dproj, exchange, exchange_args):
    half = D // 2

    def sibling_copy(got_ref, buf, sems, k):
        x, y, c, me, chips = _place()
        return _remote(buf.at[k, pl.ds((1 - c) * half, half), :], got_ref.at[k], sems[0].at[k], sems[1].at[k],
                       (x, y, 1 - c))

    def body(h_ref, dp_ref, dw_ref, got_ref, buf, send, recv, xc):
        k = pl.program_id(0)

        @pl.when(k == 0)
        def _():
            xc.start()

        buf[k] = _tn(h_ref[...], dp_ref[...]).astype(BF16)
        sibling_copy(got_ref, buf, (send, recv), k).start()
        dw_ref[...] = buf[k, pl.ds(lax.axis_index("c") * half, half), :]

        @pl.when(k == N_CHIP - 1)
        def _():
            for j in range(N_CHIP):
                sibling_copy(got_ref, buf, (send, recv), j).wait_recv()
                sibling_copy(got_ref, buf, (send, recv), j).wait_send()
            xc.middle()
            xc.finish()

    halves = jax.ShapeDtypeStruct((N_CHIP, half, WIN_C), BF16)
    dma = pltpu.SemaphoreType.DMA((N_CHIP,))
    return _carry(
        "win_bwd_w", body, exchange, exchange_args, (h1, dproj),
        [pl.BlockSpec((S, D), lambda k: (0, 0)), pl.BlockSpec((S, WIN_C), lambda k: (0, k))],
        [pl.BlockSpec((None, half, WIN_C), lambda k: (k, 0, 0)), ANY], [halves, halves],
        scratch_shapes=[pltpu.VMEM((N_CHIP, D, WIN_C), BF16), dma, dma], grid=(N_CHIP,), semantics=("arbitrary",))


def _in_bwd(dproj, win_g, x, dx2, g1, other_rows, after):
    tm = 512
    n = len(other_rows)

    def body(dp_ref, w_ref, x_ref, dx2_ref, g_ref, *refs):
        rows, dx_ref, blk_ref = refs[:n], refs[n], refs[n + 1]

        @pl.when(pl.program_id(0) == 0)
        def _():
            blk_ref[...] = jnp.zeros_like(blk_ref)
            for i, r_ref in enumerate(rows):
                blk_ref[i + 1:i + 2, :] = r_ref[...]

        dh = _nt(dp_ref[:, 0:WIN_C], w_ref[0])
        for k in range(1, N_CHIP):
            dh = dh + _nt(dp_ref[:, k * WIN_C:(k + 1) * WIN_C], w_ref[k])
        xv = x_ref[...]
        r = _rstd(xv)
        xn = xv * r
        blk_ref[0:1, :] = blk_ref[0:1, :] + jnp.sum(dh * xn, axis=0, keepdims=True)
        t = dh * g_ref[...]
        dx_ref[...] = dx2_ref[...] + r * (t - xn * jnp.mean(t * xn, axis=-1, keepdims=True))

    row = lambda w: pl.BlockSpec((tm, w), lambda i: (i, 0))
    vec = pl.BlockSpec((1, D), lambda i: (0, 0))
    return _carry("in_bwd", body, _NoExchange(), (), (dproj, win_g, x, dx2, g1, *other_rows),
                  [row(PW), pl.BlockSpec((N_CHIP, D, WIN_C), lambda i: (0, 0, 0)), row(D), row(D), vec] + [vec] * n,
                  [row(D), pl.BlockSpec((8, D), lambda i: (0, 0))],
                  [jax.ShapeDtypeStruct((S, D), F32), jax.ShapeDtypeStruct((8, D), F32)],
                  grid=(S // tm,), semantics=("arbitrary",), after=after)[0]


ANY = pl.BlockSpec(memory_space=pl.ANY)
VMEM = pl.BlockSpec(memory_space=pltpu.VMEM)
FLIPS = ((1, 0), (0, 1), (1, 1))


def _place():
    x, y, c = lax.axis_index("x"), lax.axis_index("y"), lax.axis_index("c")
    chips = [((1 - x) if fx else x, (1 - y) if fy else y) for fx, fy in FLIPS]
    return x, y, c, 2 * x + y, chips


def _remote(src, dst, send_sem, recv_sem, device):
    return pltpu.make_async_remote_copy(src_ref=src, dst_ref=dst, send_sem=send_sem, recv_sem=recv_sem,
                                        device_id=device, device_id_type=MESH)


class _Exchange:
    aliases = {}

    def middle(self, ins, outs, sems):
        pass


def _own_shard_to_sibling(shard_ref, gathered_ref, send_sem, recv_sem):
    x, y, c, me, chips = _place()
    return _remote(shard_ref, gathered_ref.at[me], send_sem, recv_sem, (x, y, 1 - c))


class _NoExchange(_Exchange):
    n_in = n_out = 0
    out_shape = ()
    scratch = ()

    def start(self, ins, outs, sems):
        pass

    def finish(self, ins, outs, sems):
        pass


class _ForwardGathered(_Exchange):
    def __init__(self, shards, own=True, forward=True):
        self.own, self.forward = own, forward
        n = self.n = len(shards)
        self.n_in, self.n_out = 2 * n, n
        self.out_shape = [jax.ShapeDtypeStruct((N_CHIP,) + s.shape, s.dtype) for s in shards]
        dma = pltpu.SemaphoreType.DMA
        self.scratch = [dma((3 * n,)), dma((3 * n,)), dma((n,)), dma((n,))]
        self.aliases = {n + a: a for a in range(n)}

    def _fwd(self, outs, sems, a, j, chip, half_of):
        x, y, c, me, chips = _place()
        half = outs[a].shape[1] // 2
        blk = outs[a].at[2 * chip[0] + chip[1], pl.ds(half_of * half, half), :]
        return _remote(blk, blk, sems[0].at[3 * a + j], sems[1].at[3 * a + j], (x, y, 1 - c))

    def _own(self, ins, outs, sems, a):
        return _own_shard_to_sibling(ins[a], outs[a], sems[2].at[a], sems[3].at[a])

    def start(self, ins, outs, sems):
        x, y, c, me, chips = _place()
        for a in range(self.n):
            for j, chip in enumerate(chips if self.forward else ()):
                self._fwd(outs, sems, a, j, chip, c).start()
        for a in range(self.n if self.own else 0):
            self._own(ins, outs, sems, a).start()

    def finish(self, ins, outs, sems):
        x, y, c, me, chips = _place()
        for a in range(self.n):
            for j, chip in enumerate(chips if self.forward else ()):
                self._fwd(outs, sems, a, j, chip, 1 - c).wait_recv()
        for a in range(self.n):
            for j, chip in enumerate(chips if self.forward else ()):
                self._fwd(outs, sems, a, j, chip, c).wait_send()
            if self.own:
                self._own(ins, outs, sems, a).wait()


HBM = pl.BlockSpec(memory_space=pltpu.HBM)
SEMS = pl.BlockSpec(memory_space=pltpu.SEMAPHORE)
DATAFLOW = pltpu.SideEffectType.DATAFLOW_SIDE_EFFECTING


class _OverIci:
    def __init__(self, name, sources, lands):
        self.name, self.n = name, len(sources)
        hbm = lambda t: pltpu.with_memory_space_constraint(t, pltpu.HBM)
        self.arrays = [hbm(t) for t in sources] + [hbm(t) for t in lands]

    def sent(self, src, land, a, chip):
        raise NotImplementedError

    def landed(self, land, a, chip):
        raise NotImplementedError

    def _copy(self, arr, sems, a, j, receiving):
        x, y, c, me, chips = _place()
        src, dst = self.sent(arr[a], arr[self.n + a], a, chips[j])
        if receiving:
            dst = self.landed(arr[self.n + a], a, chips[j])
        return _remote(src, dst, sems[0].at[3 * a + j], sems[1].at[3 * a + j], (*chips[j], c))

    def start(self, after):
        m = len(self.arrays)

        def body(*refs):
            arr, sems, token = refs[:m], refs[m + 1:m + 3], refs[-1]
            for a in range(self.n):
                for j in range(3):
                    self._copy(arr, sems, a, j, False).start()
            token[...] = jnp.zeros_like(token)

        dma = pltpu.SemaphoreType.DMA
        outs = pl.pallas_call(
            body, name=self.name + "_start",
            out_shape=[dma((3 * self.n,)), dma((3 * self.n,))] + [pltpu.HBM(t.shape, t.dtype) for t in self.arrays]
                      + [jax.ShapeDtypeStruct((8, 128), F32)],
            in_specs=[HBM] * m + [ANY], out_specs=[SEMS, SEMS] + [HBM] * m + [VMEM],
            input_output_aliases={i: 2 + i for i in range(m)},
            compiler_params=pltpu.CompilerParams(has_side_effects=DATAFLOW),
        )(*self.arrays, after)
        self.sems, self.arrays = outs[0:2], list(outs[2:2 + m])
        return outs[-1]

    def wait(self, after):
        m = len(self.arrays)

        def body(*refs):
            arr, sems = refs[:m], refs[m:m + 2]
            for a in range(self.n):
                for j in range(3):
                    self._copy(arr, sems, a, j, False).wait_send()
                    self._copy(arr, sems, a, j, True).wait_recv()

        outs = pl.pallas_call(
            body, name=self.name + "_wait",
            out_shape=[pltpu.HBM(t.shape, t.dtype) for t in self.arrays],
            in_specs=[HBM] * m + [SEMS, SEMS, ANY], out_specs=[HBM] * m,
            input_output_aliases={i: i for i in range(m)},
            compiler_params=pltpu.CompilerParams(has_side_effects=DATAFLOW),
        )(*self.arrays, *self.sems, after)
        return list(outs[:self.n]), list(outs[self.n:])


class _GatherOverIci(_OverIci):
    def __init__(self, name, shards):
        super().__init__(name, shards, [lax.empty((N_CHIP,) + s.shape, s.dtype) for s in shards])

    @staticmethod
    def _half(ref):
        c = lax.axis_index("c")
        half = ref.shape[-2] // 2
        return pl.ds(c * half, half)

    def sent(self, src, land, a, chip):
        return src.at[self._half(src), :], land.at[_place()[3], self._half(src), :]

    def landed(self, land, a, chip):
        return land.at[2 * chip[0] + chip[1], self._half(land), :]


class _SumOverIci(_OverIci):
    def __init__(self, name, pre):
        super().__init__(name, pre, [lax.empty(p.shape, p.dtype) for p in pre])

    def sent(self, src, land, a, chip):
        return src.at[2 * chip[0] + chip[1]], land.at[_place()[3]]

    def landed(self, land, a, chip):
        return land.at[2 * chip[0] + chip[1]]


class _HalvesToSibling(_Exchange):
    def __init__(self, grads):
        n = self.n = len(grads)
        self.n_in = self.n_out = n
        self.out_shape = [jax.ShapeDtypeStruct((N_CHIP, g.shape[1] // 2, g.shape[2]), g.dtype) for g in grads]
        self.scratch = [pltpu.SemaphoreType.DMA((n,)), pltpu.SemaphoreType.DMA((n,))]

    def _copy(self, ins, outs, sems, a):
        x, y, c, me, chips = _place()
        half = ins[a].shape[1] // 2
        return _remote(ins[a].at[:, pl.ds((1 - c) * half, half), :], outs[a], sems[0].at[a], sems[1].at[a], (x, y, 1 - c))

    def start(self, ins, outs, sems):
        for a in range(self.n):
            self._copy(ins, outs, sems, a).start()

    def finish(self, ins, outs, sems):
        for a in range(self.n):
            self._copy(ins, outs, sems, a).wait_recv()
        for a in range(self.n):
            self._copy(ins, outs, sems, a).wait_send()


class _ShareHalves(_Exchange):
    def __init__(self, fulls):
        n = self.n = len(fulls)
        self.n_in = self.n_out = n
        self.out_shape = [jax.ShapeDtypeStruct(f.shape, f.dtype) for f in fulls]
        self.scratch = [pltpu.SemaphoreType.DMA((n,)), pltpu.SemaphoreType.DMA((n,))]
        self.aliases = {a: a for a in range(n)}

    def _copy(self, outs, sems, a, half_of):
        x, y, c, me, chips = _place()
        half = outs[a].shape[0] // 2
        rows = outs[a].at[pl.ds(half_of * half, half), :]
        return _remote(rows, rows, sems[0].at[a], sems[1].at[a], (x, y, 1 - c))

    def start(self, ins, outs, sems):
        c = _place()[2]
        for a in range(self.n):
            self._copy(outs, sems, a, c).start()

    def finish(self, ins, outs, sems):
        c = _place()[2]
        for a in range(self.n):
            self._copy(outs, sems, a, 1 - c).wait_recv()
        for a in range(self.n):
            self._copy(outs, sems, a, c).wait_send()


class _GatherBlocks(_Exchange):
    def __init__(self, block):
        self.n_in = self.n_out = 1
        self.out_shape = [jax.ShapeDtypeStruct((8,) + block.shape, block.dtype)]
        dma = pltpu.SemaphoreType.DMA
        self.scratch = [dma((7,)), dma((7,)), dma]

    @staticmethod
    def _peer(f):
        x, y, c, me, chips = _place()
        return ((1 - x) if f & 4 else x, (1 - y) if f & 2 else y, (1 - c) if f & 1 else c)

    def start(self, ins, outs, sems):
        x, y, c, me, chips = _place()
        for f in range(1, 8):
            _remote(ins[0], outs[0].at[2 * me + c], sems[0].at[f - 1], sems[1].at[f - 1], self._peer(f)).start()
        pltpu.make_async_copy(ins[0], outs[0].at[2 * me + c], sems[2]).start()

    def finish(self, ins, outs, sems):
        x, y, c, me, chips = _place()
        for f in range(1, 8):
            px, py, pc = self._peer(f)
            blk = outs[0].at[4 * px + 2 * py + pc]
            _remote(blk, blk, sems[0].at[f - 1], sems[1].at[f - 1], (x, y, c)).wait_recv()
        for f in range(1, 8):
            _remote(ins[0], outs[0].at[2 * me + c], sems[0].at[f - 1], sems[1].at[f - 1], self._peer(f)).wait_send()
        pltpu.make_async_copy(ins[0], outs[0].at[2 * me + c], sems[2]).wait()


class _Both(_Exchange):
    def __init__(self, first, second):
        self.parts = (first, second)
        self.n_in, self.n_out = first.n_in + second.n_in, first.n_out + second.n_out
        self.out_shape = first.out_shape + second.out_shape
        self.scratch = first.scratch + second.scratch
        self.aliases = dict(first.aliases)
        self.aliases.update({first.n_in + i: first.n_out + o for i, o in second.aliases.items()})

    def _split(self, ins, outs, sems):
        a, b = self.parts
        return ((a, ins[:a.n_in], outs[:a.n_out], sems[:len(a.scratch)]),
                (b, ins[a.n_in:], outs[a.n_out:], sems[len(a.scratch):]))

    def start(self, ins, outs, sems):
        for ex, i, o, s in self._split(ins, outs, sems):
            ex.start(i, o, s)

    def middle(self, ins, outs, sems):
        for ex, i, o, s in self._split(ins, outs, sems):
            ex.middle(i, o, s)

    def finish(self, ins, outs, sems):
        for ex, i, o, s in self._split(ins, outs, sems):
            ex.finish(i, o, s)


class _Bound:
    def __init__(self, ex, ins, outs, sems):
        self.start = lambda: ex.start(ins, outs, sems)
        self.middle = lambda: ex.middle(ins, outs, sems)
        self.finish = lambda: ex.finish(ins, outs, sems)


def _carry(name, body, ex, ex_args, args, in_specs, out_specs, out_shape, scratch_shapes=(), grid=None, semantics=(),
           after=None):
    n_a, n_o, n_s = len(args), len(out_shape), len(scratch_shapes)
    behind = [] if after is None else [after]

    def full_body(*refs):
        p = 0
        groups = []
        for size in (n_a, ex.n_in, len(behind), n_o, ex.n_out, n_s, len(ex.scratch)):
            groups.append(refs[p:p + size])
            p += size
        a, ei, _, o, eo, s, es = groups
        body(*a, *o, *s, _Bound(ex, ei, eo, es))

    kwargs = {} if grid is None else {"grid": grid}
    outs = pl.pallas_call(
        full_body, name=name,
        in_specs=list(in_specs) + [ANY] * (ex.n_in + len(behind)), out_specs=list(out_specs) + [ANY] * ex.n_out,
        out_shape=list(out_shape) + list(ex.out_shape), scratch_shapes=list(scratch_shapes) + list(ex.scratch),
        input_output_aliases={n_a + i: n_o + o for i, o in ex.aliases.items()},
        compiler_params=_params(*semantics) if semantics else pltpu.CompilerParams(vmem_limit_bytes=VMEM_LIMIT),
        **kwargs,
    )(*args, *ex_args, *behind)
    return outs[:n_o], outs[n_o:]


def _cast_bf16(arrays, after=None):
    n = len(arrays)
    behind = [] if after is None else [after]

    def body(*refs):
        for a in range(n):
            refs[len(refs) - n + a][...] = refs[a][...].astype(BF16)

    blks = [pl.BlockSpec((t.shape[0] // 4, t.shape[1]), lambda i: (i, 0)) for t in arrays]
    return pl.pallas_call(
        body, grid=(4,), name="cast_bf16", in_specs=blks + [ANY] * len(behind), out_specs=blks,
        out_shape=[jax.ShapeDtypeStruct(t.shape, BF16) for t in arrays], compiler_params=_params("parallel"),
    )(*arrays, *behind)


def _prepare(x, g1, pos, ifc, after):
    tm = 512

    def body(x_ref, g_ref, pos_ref, ifc_ref, h_ref, cos_ref, sin_ref, _):
        xv = x_ref[...]
        h_ref[...] = (xv * _rstd(xv) * g_ref[...]).astype(BF16)
        ang = pos_ref[...].astype(F32) * ifc_ref[...]
        cos_ref[...] = jnp.cos(ang)
        sin_ref[...] = jnp.sin(ang)

    row = lambda w: pl.BlockSpec((tm, w), lambda i: (i, 0))
    const = lambda w: pl.BlockSpec((1, w), lambda i: (0, 0))
    return _carry("prepare", body, _NoExchange(), (), (x, g1, pos, ifc),
                  [row(D), const(D), row(1), const(128)], [row(D), row(128), row(128)],
                  [jax.ShapeDtypeStruct((S, D), BF16)] + [jax.ShapeDtypeStruct((S, 128), F32)] * 2,
                  grid=(S // tm,), semantics=("parallel",), after=after)[0]


def _exchange_alone(name, ex, ex_args):
    def body(xc):
        xc.start()
        xc.middle()
        xc.finish()

    return _carry(name, body, ex, ex_args, (), (), (), ())[1]


def _core_index():
    return lax.axis_index("c").astype(jnp.int32).reshape(1)


def _pair_sum(gs, gots):
    n = len(gs)

    def body(c_ref, *refs):
        for a in range(n):
            refs[2 * n + a][...] = (refs[a][...].astype(F32) + refs[n + a][...].astype(F32)).astype(BF16)

    blk = [pl.BlockSpec((None,) + g.shape[1:], lambda k, c_ref: (k, 0, 0)) for g in gots]
    mine = [b if g.shape == got.shape else pl.BlockSpec((None,) + got.shape[1:], lambda k, c_ref: (k, c_ref[0], 0))
            for g, got, b in zip(gs, gots, blk)]
    return pl.pallas_call(
        body, name=f"pair_sum_{gots[0].shape[1]}x{gots[0].shape[2]}",
        grid_spec=pltpu.PrefetchScalarGridSpec(
            num_scalar_prefetch=1, grid=(N_CHIP,), in_specs=mine + blk, out_specs=blk),
        out_shape=[jax.ShapeDtypeStruct(g.shape, BF16) for g in gots],
        compiler_params=_params("parallel"),
    )(_core_index(), *gs, *gots)


def _chip_sum(pre, parts):
    n = len(parts)
    me = 2 * lax.axis_index("x") + lax.axis_index("y")
    others = [k + (k >= me).astype(jnp.int32) for k in range(3)]
    where = jnp.stack([lax.axis_index("c"), me, *others]).astype(jnp.int32)

    def body(w_ref, *refs):
        for a in range(n):
            own, p1, p2, p3 = refs[4 * a:4 * a + 4]
            refs[4 * n + a][...] = ((own[...].astype(F32) + p1[...].astype(F32)) + p2[...].astype(F32)) + p3[...].astype(F32)

    in_specs, out_specs, operands = [], [], []
    for a in range(n):
        _, half, cc = parts[a].shape
        tr = half // 2
        in_specs += [pl.BlockSpec((None, tr, cc), lambda i, w_ref, s=s: (w_ref[s], i, 0)) for s in (1, 2, 3, 4)]
        out_specs.append(pl.BlockSpec((tr, cc), lambda i, w_ref: (2 * w_ref[0] + i, 0)))
        operands += [pre[a], parts[a], parts[a], parts[a]]
    return pl.pallas_call(
        body, name=f"chip_sum_{parts[0].shape[1]}x{parts[0].shape[2]}",
        grid_spec=pltpu.PrefetchScalarGridSpec(num_scalar_prefetch=1, grid=(2,), in_specs=in_specs, out_specs=out_specs),
        out_shape=[jax.ShapeDtypeStruct((2 * p.shape[1], p.shape[2]), F32) for p in parts],
        compiler_params=_params("parallel"),
    )(where, *operands)


def _adamw_math(w, g, m, v):
    m = ADAM_B1 * m + (1.0 - ADAM_B1) * g
    v = ADAM_B2 * v + (1.0 - ADAM_B2) * (g * g)
    m_hat = m / (1.0 - ADAM_B1 ** ADAM_STEP)
    v_hat = v / (1.0 - ADAM_B2 ** ADAM_STEP)
    delta = -ADAM_LR * (m_hat / (jnp.sqrt(v_hat) + ADAM_EPS) + ADAM_WD * w)
    return delta, m, v


def _adamw(ws, gs, ms, vs, after=None):
    n = len(ws)

    def body(*refs):
        for a in range(n):
            w_ref, g_ref, m_ref, v_ref = (refs[t * n + a] for t in range(4))
            go_ref, d_ref, nm_ref, nv_ref = refs[4 * n + 4 * a:4 * n + 4 * a + 4]
            g = g_ref[...]
            go_ref[...] = g
            d_ref[...], nm_ref[...], nv_ref[...] = _adamw_math(w_ref[...], g, m_ref[...], v_ref[...])

    steps = 8
    blks = [pl.BlockSpec((w.shape[0] // steps, w.shape[1]), lambda i: (i, 0)) for w in ws]
    outs = _carry(f"adamw_{ws[0].shape[0]}x{ws[0].shape[1]}", body, _NoExchange(), (), (*ws, *gs, *ms, *vs),
                  blks * 4, [b for b in blks for _ in range(4)],
                  [jax.ShapeDtypeStruct(w.shape, F32) for w in ws for _ in range(4)],
                  grid=(steps,), semantics=("parallel",), after=after)[0]
    return [outs[4 * a:4 * a + 4] for a in range(n)]


def _adamw_gains(gall, ws, ms, vs):
    def body(ga_ref, *refs):
        w, m, v = refs[0:4], refs[4:8], refs[8:12]
        outs, loss_ref, total = refs[12:28], refs[28], refs[29]
        g = ga_ref[0]
        for dev in range(1, 8):
            g = g + ga_ref[dev]
        total[...] = g
        for i in range(4):
            gi = total[i:i + 1, :]
            outs[i][...] = gi
            outs[4 + i][...], outs[8 + i][...], outs[12 + i][...] = _adamw_math(w[i][...], gi, m[i][...], v[i][...])
        loss_ref[...] = total[4:5, 0:128] * (0.5 / D)

    outs = pl.pallas_call(
        body, name="adamw_gains",
        out_shape=[jax.ShapeDtypeStruct((1, D), F32)] * 16 + [jax.ShapeDtypeStruct((1, 128), F32)],
        scratch_shapes=[pltpu.VMEM((8, D), F32)],
    )(gall, *ws, *ms, *vs)
    return outs[0:4], outs[4:8], outs[8:12], outs[12:16], outs[16]


def kernel(x, positions, w_in, w_out, g_pre_mix, g_post_mix, g_pre_ffn, g_post_ffn, w_gate, w_up, w_down, loss_target, m_w_in, m_w_out, m_g_pre_mix, m_g_post_mix, m_g_pre_ffn, m_g_post_ffn, m_w_gate, m_w_up, m_w_down, v_w_in, v_w_out, v_g_pre_mix, v_g_post_mix, v_g_pre_ffn, v_g_post_ffn, v_w_gate, v_w_up, v_w_down):
    tr = lambda t: jnp.swapaxes(t, 1, 2)[0]
    shards = [w_in[0], w_out[0], tr(w_gate), tr(w_up), w_down[0]]
    moms = [m_w_in[0], m_w_out[0], tr(m_w_gate), tr(m_w_up), m_w_down[0]]
    vels = [v_w_in[0], v_w_out[0], tr(v_w_gate), tr(v_w_up), v_w_down[0]]
    xs, pos, tgt = x[0], positions.reshape(S, 1), loss_target[0]
    g1, g2, g3, g4 = g_pre_mix, g_post_mix, g_pre_ffn, g_post_ffn
    tabs = tuple(jnp.asarray(t) for t in _retention_tables())
    ifc, spread = _rotary_tables()
    ifc, spread = jnp.asarray(ifc), jnp.asarray(spread, dtype=BF16)
    bf = list(_cast_bf16(shards[:1]))
    win_gather = _GatherOverIci("win_gather", bf[:1])
    token = win_gather.start(shards[0])
    bf += _cast_bf16(shards[1:], token)
    wout_gather = _GatherOverIci("wout_gather", bf[1:2])
    token = wout_gather.start(token)
    ffn_gather = _GatherOverIci("ffn_gather", bf[2:])
    token = ffn_gather.start(token)
    h1, cos, sin = _prepare(xs, g1, pos, ifc, token)
    win_sh, win_land = win_gather.wait(h1)
    (win_g,) = _exchange_alone("forward_win", _ForwardGathered(bf[:1]), [*win_sh, *win_land])
    qr, kr, rv, rg, aq, ak, av = _proj_fwd(h1, win_g, cos, sin, spread, None)
    wout_sh, wout_land = wout_gather.wait(qr)
    n_ffn = len(bf[2:])
    (att_out, lse, cat_a), (wout_g, *ffn_gather.arrays[n_ffn:]) = _att_fwd(
        aq, ak, av, _Both(_ForwardGathered(bf[1:2]), _ForwardGathered(bf[2:], forward=False)),
        [*wout_sh, *wout_land, *ffn_gather.arrays])
    wout_g = wout_g.reshape(D, D)
    (o_raw, cat_r, states), _ = _ret_fwd(qr, kr, rv, rg, tabs, _NoExchange(), (), cat_a)
    ffn_sh, ffn_lands = ffn_gather.wait(cat_r)
    (mix, x2, h3), (wg_g, wu_g, wd_g) = _mix_fwd(cat_r, cat_a, wout_g, xs, g2, g3,
                                                _ForwardGathered(bf[2:], own=False), [*ffn_sh, *ffn_lands])
    gt, up, a, sq, dy, df, dg4 = _ffn_fwd(h3, wg_g, wu_g, wd_g, x2, tgt, g4)

    dgt, dup, dx2, dmix, dg3, dg2 = _ffn_bwd_act(df, gt, up, wg_g, wu_g, wd_g, dy, x2, mix, g2, g3)
    ffn_grads = list(_ffn_bwd_w(a, df, h3, dgt, dup))
    (dret, datt, dwout), got = _mix_bwd(dmix, cat_r, cat_a, wout_g, _HalvesToSibling(ffn_grads), ffn_grads)
    ffn_sum = _SumOverIci("ffn_sum", _pair_sum(ffn_grads, got))
    token = ffn_sum.start(datt)
    (dq_att, dk_att, dv_att), _ = _att_bwd(aq, ak, av, datt, att_out, lse, _NoExchange(), (), token)
    (dqr, dkr, drv, drg), _ = _ret_bwd(qr, kr, rv, rg, o_raw, states, dret, tabs, _NoExchange(), (), token)
    dproj = _rot_bwd(cos, sin, spread, dqr, dkr, drv, drg, dq_att, dk_att, dv_att)
    sums = _chip_sum(*ffn_sum.wait(dproj))
    dwout = dwout.reshape(N_CHIP, WOUT_R, D)
    (dwin, got_win), (*ffn_full, got_wout) = _win_bwd_w(
        h1, dproj, _Both(_ShareHalves(sums), _HalvesToSibling([dwout])), [*sums, dwout])

    in_sum = _SumOverIci("in_sum", _pair_sum([dwin, dwout], [got_win, got_wout]))
    token = in_sum.start(dproj)
    dx, gblock = _in_bwd(dproj, win_g, xs, dx2, g1, [dg2, dg3, dg4, sq], token)
    ffn_upd = _adamw(shards[2:], [ffn_full[o] for o in (1, 2, 0)],
                     moms[2:], vels[2:], token)
    pre, parts = in_sum.wait(ffn_upd[2][0])
    sums = _chip_sum(pre, parts)
    *in_full, gall = _exchange_alone("share_rest", _Both(_ShareHalves(sums), _GatherBlocks(gblock)), [*sums, gblock])
    upd = _adamw(shards[:2], in_full, moms[:2], vels[:2]) + ffn_upd
    gg, gd, gm, gv, loss_row = _adamw_gains(gall, [g1, g2, g3, g4],
                                            [m_g_pre_mix, m_g_post_mix, m_g_pre_ffn, m_g_post_ffn],
                                            [v_g_pre_mix, v_g_post_mix, v_g_pre_ffn, v_g_post_ffn])

    def order(mats, vecs):
        back = lambda t: jnp.swapaxes(t[None], 1, 2)
        return [mats[0][None], mats[1][None], *vecs, back(mats[2]), back(mats[3]), mats[4][None]]

    return (loss_row[0, 0], dx[None],
            *order([u[0] for u in upd], gg),
            *order([u[1] for u in upd], gd),
            *order([u[2] for u in upd], gm),
            *order([u[3] for u in upd], gv))
```
